```python
import jax, jax.numpy as jnp
from jax import lax
import numpy as np

D_MODEL = 1024
BATCH = 32
SEQ = 2048
DEPTH = 1

GRID_W = 64
CTX_LEN = 256
MLA_HEADS = 8
QK_NOPE = 64
QK_ROPE = 32
QK_HEAD = QK_NOPE + QK_ROPE
V_HEAD = 64
Q_LORA = 256
KV_LORA = 128
AXIS_DIM = QK_ROPE // 2
ROPE_BASE = 10000.0
Q_BLOCK = 128
MLA_WIDTH = MLA_HEADS * V_HEAD
GMLP_GROUPS = 8
GMLP_GROUP_DIM = 64
GMLP_WIDTH = GMLP_GROUPS * GMLP_GROUP_DIM
CHUNK = 128
D_MIX = MLA_WIDTH + GMLP_WIDTH
KV_COLS = KV_LORA + QK_ROPE
Q_START = KV_COLS
U_START = KV_COLS + Q_LORA
V_START = U_START + GMLP_WIDTH
IN_COLS = V_START + GMLP_WIDTH
D_FF = 2816
N_MOD = 9
EPS = 1e-6

kernel_name = "hymba_mla_gmlp_macaron_dit_layer"


def rms_norm(x, w):
    xf = x.astype(jnp.float32)
    y = xf * lax.rsqrt(jnp.mean(xf * xf, axis=-1, keepdims=True) + EPS)
    return (y * w.astype(jnp.float32)).astype(x.dtype)


def modulate(h, shift, scale):
    return h * (1 + scale) + shift


def swiglu(h, w1, w3, w2):
    return (jax.nn.silu(h @ w1) * (h @ w3)) @ w2


def ffn_sublayer(h_in, shift, scale, gate, norm_w, w1, w3, w2):
    h = modulate(rms_norm(h_in, norm_w), shift, scale)
    return h_in + 0.5 * gate * swiglu(h, w1, w3, w2)


def axial_rope(x, cos, sin):
    xr = x.reshape(x.shape[:-1] + (2, 2, AXIS_DIM // 2))
    rot = jnp.stack([-xr[..., 1, :], xr[..., 0, :]], axis=-2).reshape(x.shape)
    return x * cos[:, None, :] + rot * sin[:, None, :]


def rope_part(x, rope):
    if rope is None:
        return x
    return jnp.concatenate([x[..., :QK_NOPE], axial_rope(x[..., QK_NOPE:], *rope)], axis=-1)


def mla_keys_values(kv_proj, kv_a_norm_w, w_ukv, k_norm_w, rope):
    B, S, _ = kv_proj.shape
    c_kv = rms_norm(kv_proj[..., :KV_LORA], kv_a_norm_w)
    k_pe = kv_proj[..., KV_LORA:]
    kv = (c_kv @ w_ukv).reshape(B, S, MLA_HEADS, QK_NOPE + V_HEAD)
    k_nope, v = kv[..., :QK_NOPE], kv[..., QK_NOPE:]
    k_pe = jnp.broadcast_to(k_pe[:, :, None, :], (B, S, MLA_HEADS, QK_ROPE))
    k = rms_norm(jnp.concatenate([k_nope, k_pe], axis=-1), k_norm_w)
    return rope_part(k, rope), v


def mla_queries(q_proj, q_a_norm_w, w_uq, q_norm_w, rope):
    B, S, _ = q_proj.shape
    c_q = rms_norm(q_proj, q_a_norm_w)
    q = (c_q @ w_uq).reshape(B, S, MLA_HEADS, QK_HEAD)
    return rope_part(rms_norm(q, q_norm_w), rope)


def block_attention(q, k_all, v_all):
    B, S, H, Dk = q.shape
    nb = S // Q_BLOCK
    scale = Dk ** -0.5
    qb = jnp.moveaxis(q.reshape(B, nb, Q_BLOCK, H, Dk), 1, 0)

    def one_block(q_blk):
        s = jnp.einsum('bqhd,bkhd->bhqk', q_blk, k_all).astype(jnp.float32) * scale
        p = jax.nn.softmax(s, axis=-1).astype(v_all.dtype)
        return jnp.einsum('bhqk,bkhd->bqhd', p, v_all)

    out = lax.map(one_block, qb)
    return jnp.moveaxis(out, 0, 1).reshape(B, S, H * V_HEAD)


def chunk_gmlp(u, v, v_norm_w, w_s, b_s):
    B, S, _ = u.shape
    n = S // CHUNK
    u = jax.nn.gelu(u).reshape(B, n, CHUNK, GMLP_GROUPS, GMLP_GROUP_DIM)
    v = rms_norm(jax.nn.gelu(v).reshape(B, n, CHUNK, GMLP_GROUPS, GMLP_GROUP_DIM), v_norm_w)
    s = jnp.einsum('gpq,bnqgc->bnpgc', w_s, v) + b_s.T[:, :, None]
    return (u * s).reshape(B, S, GMLP_WIDTH)


def token_mix(proj, k_all, v_all, rope, q_a_norm_w, w_uq, q_norm_w, v_norm_w, w_s, b_s, w_out):
    q = mla_queries(proj[..., Q_START:U_START], q_a_norm_w, w_uq, q_norm_w, rope)
    attn = block_attention(q, k_all, v_all)
    sg = chunk_gmlp(proj[..., U_START:V_START], proj[..., V_START:], v_norm_w, w_s, b_s)
    return jnp.concatenate([attn, sg], axis=-1) @ w_out


def hybrid_layer(x, ctx, c, c_ctx, cos, sin,
                 w_ada, b_ada, norm1_w, ffn1_w1, ffn1_w3, ffn1_w2,
                 norm2_w, w_in, q_a_norm_w, w_uq, kv_a_norm_w, w_ukv, q_norm_w, k_norm_w,
                 v_norm_w, w_s, b_s, w_out,
                 norm3_w, ffn2_w1, ffn2_w3, ffn2_w2, update_ctx):
    mx = jnp.split((jax.nn.silu(c) @ w_ada + b_ada)[:, None, :], N_MOD, axis=-1)
    mc = jnp.split((jax.nn.silu(c_ctx) @ w_ada + b_ada)[None, None, :], N_MOD, axis=-1)
    rope = (cos, sin)

    x = ffn_sublayer(x, mx[0], mx[1], mx[2], norm1_w, ffn1_w1, ffn1_w3, ffn1_w2)
    ctx = ffn_sublayer(ctx, mc[0], mc[1], mc[2], norm1_w, ffn1_w1, ffn1_w3, ffn1_w2)

    proj = modulate(rms_norm(x, norm2_w), mx[3], mx[4]) @ w_in
    hc = modulate(rms_norm(ctx, norm2_w), mc[3], mc[4])
    proj_c = hc @ (w_in if update_ctx else w_in[:, :KV_COLS])
    k_lat, v_lat = mla_keys_values(proj[..., :KV_COLS], kv_a_norm_w, w_ukv, k_norm_w, rope)
    k_ctx, v_ctx = mla_keys_values(proj_c[..., :KV_COLS], kv_a_norm_w, w_ukv, k_norm_w, None)
    k_all = jnp.concatenate([k_lat, k_ctx], axis=1)
    v_all = jnp.concatenate([v_lat, v_ctx], axis=1)
    x = x + mx[5] * token_mix(proj, k_all, v_all, rope, q_a_norm_w, w_uq, q_norm_w,
                              v_norm_w, w_s, b_s, w_out)
    if update_ctx:
        ctx = ctx + mc[5] * token_mix(proj_c, k_ctx, v_ctx, None, q_a_norm_w, w_uq, q_norm_w,
                                      v_norm_w, w_s, b_s, w_out)
        ctx = ffn_sublayer(ctx, mc[6], mc[7], mc[8], norm3_w, ffn2_w1, ffn2_w3, ffn2_w2)

    x = ffn_sublayer(x, mx[6], mx[7], mx[8], norm3_w, ffn2_w1, ffn2_w3, ffn2_w2)
    return x, ctx


def _fwd_setup_inputs(seed: int = 0) -> dict:
    key = jax.random.key(seed)
    ks = jax.random.split(key, 26)
    f32 = jnp.float32

    def dense(k, shape, fan_in, gain=1.0):
        return jax.random.normal(k, shape, f32) * (gain * fan_in ** -0.5)

    def gain_vec(k, shape):
        return 1.0 + 0.02 * jax.random.normal(k, shape, f32)

    L = DEPTH
    return {
        "x": jax.random.normal(ks[0], (BATCH, SEQ, D_MODEL), f32),
        "c": jax.random.normal(ks[1], (BATCH, D_MODEL), f32),
        "ctx": jax.random.normal(ks[2], (BATCH, CTX_LEN, D_MODEL), f32),
        "c_ctx": jax.random.normal(ks[3], (D_MODEL,), f32),
        "w_ada": dense(ks[4], (L, D_MODEL, N_MOD * D_MODEL), D_MODEL, 0.5),
        "b_ada": 0.02 * jax.random.normal(ks[5], (L, N_MOD * D_MODEL), f32),
        "norm1_w": gain_vec(ks[6], (L, D_MODEL)),
        "ffn1_w1": dense(ks[7], (L, D_MODEL, D_FF), D_MODEL),
        "ffn1_w3": dense(ks[8], (L, D_MODEL, D_FF), D_MODEL),
        "ffn1_w2": dense(ks[9], (L, D_FF, D_MODEL), D_FF),
        "norm2_w": gain_vec(ks[10], (L, D_MODEL)),
        "w_in": dense(ks[11], (L, D_MODEL, IN_COLS), D_MODEL),
        "q_a_norm_w": gain_vec(ks[12], (L, Q_LORA)),
        "w_uq": dense(ks[13], (L, Q_LORA, MLA_HEADS * QK_HEAD), Q_LORA),
        "kv_a_norm_w": gain_vec(ks[14], (L, KV_LORA)),
        "w_ukv": dense(ks[15], (L, KV_LORA, MLA_HEADS * (QK_NOPE + V_HEAD)), KV_LORA),
        "q_norm_w": gain_vec(ks[16], (L, QK_HEAD)),
        "k_norm_w": gain_vec(ks[17], (L, QK_HEAD)),
        "v_norm_w": gain_vec(ks[18], (L, GMLP_GROUPS, GMLP_GROUP_DIM)),
        "w_s": dense(ks[19], (L, GMLP_GROUPS, CHUNK, CHUNK), CHUNK),
        "b_s": gain_vec(ks[20], (L, GMLP_GROUPS, CHUNK)),
        "w_out": dense(ks[21], (L, D_MIX, D_MODEL), D_MIX),
        "norm3_w": gain_vec(ks[22], (L, D_MODEL)),
        "ffn2_w1": dense(ks[23], (L, D_MODEL, D_FF), D_MODEL),
        "ffn2_w3": dense(ks[24], (L, D_MODEL, D_FF), D_MODEL),
        "ffn2_w2": dense(ks[25], (L, D_FF, D_MODEL), D_FF),
    }


def _fwd_reference(x, c, ctx, c_ctx, w_ada, b_ada, norm1_w, ffn1_w1, ffn1_w3, ffn1_w2,
              norm2_w, w_in, q_a_norm_w, w_uq, kv_a_norm_w, w_ukv, q_norm_w, k_norm_w,
              v_norm_w, w_s, b_s, w_out, norm3_w, ffn2_w1, ffn2_w3, ffn2_w2):
    S = x.shape[1]
    ROWS = S // GRID_W
    f32 = jnp.float32
    rows = jnp.repeat(jnp.arange(ROWS, dtype=f32), GRID_W)
    cols = jnp.tile(jnp.arange(GRID_W, dtype=f32), ROWS)
    inv = ROPE_BASE ** (-jnp.arange(0, AXIS_DIM, 2, dtype=f32) / AXIS_DIM)
    ang_r = rows[:, None] * inv
    ang_c = cols[:, None] * inv
    ang = jnp.concatenate([ang_r, ang_r, ang_c, ang_c], axis=-1)
    cos = jnp.cos(ang).astype(x.dtype)
    sin = jnp.sin(ang).astype(x.dtype)

    layer_weights = (w_ada, b_ada, norm1_w, ffn1_w1, ffn1_w3, ffn1_w2,
                     norm2_w, w_in, q_a_norm_w, w_uq, kv_a_norm_w, w_ukv, q_norm_w, k_norm_w,
                     v_norm_w, w_s, b_s, w_out, norm3_w, ffn2_w1, ffn2_w3, ffn2_w2)
    for i in range(DEPTH):
        x, ctx = hybrid_layer(x, ctx, c, c_ctx, cos, sin, *[w[i] for w in layer_weights],
                              update_ctx=i < DEPTH - 1)
    return x


import jax as _jax
import jax.numpy as _jnp

TWIN_FORMAT = 'train_step'
FWD_PARAMS = ['x', 'c', 'ctx', 'c_ctx', 'w_ada', 'b_ada', 'norm1_w', 'ffn1_w1', 'ffn1_w3', 'ffn1_w2', 'norm2_w', 'w_in', 'q_a_norm_w', 'w_uq', 'kv_a_norm_w', 'w_ukv', 'q_norm_w', 'k_norm_w', 'v_norm_w', 'w_s', 'b_s', 'w_out', 'norm3_w', 'ffn2_w1', 'ffn2_w3', 'ffn2_w2']
TWIN_WEIGHTS = ['c_ctx', 'w_ada', 'b_ada', 'norm1_w', 'ffn1_w1', 'ffn1_w3', 'ffn1_w2', 'norm2_w', 'w_in', 'q_a_norm_w', 'w_uq', 'kv_a_norm_w', 'w_ukv', 'q_norm_w', 'k_norm_w', 'v_norm_w', 'w_s', 'b_s', 'w_out', 'norm3_w', 'ffn2_w1', 'ffn2_w3', 'ffn2_w2']
TWIN_DIFF_INPUT = 'x'
TWIN_INPUTS = ['x', 'c', 'ctx', 'c_ctx', 'w_ada', 'b_ada', 'norm1_w', 'ffn1_w1', 'ffn1_w3', 'ffn1_w2', 'norm2_w', 'w_in', 'q_a_norm_w', 'w_uq', 'kv_a_norm_w', 'w_ukv', 'q_norm_w', 'k_norm_w', 'v_norm_w', 'w_s', 'b_s', 'w_out', 'norm3_w', 'ffn2_w1', 'ffn2_w3', 'ffn2_w2', 'loss_target', 'm_c_ctx', 'm_w_ada', 'm_b_ada', 'm_norm1_w', 'm_ffn1_w1', 'm_ffn1_w3', 'm_ffn1_w2', 'm_norm2_w', 'm_w_in', 'm_q_a_norm_w', 'm_w_uq', 'm_kv_a_norm_w', 'm_w_ukv', 'm_q_norm_w', 'm_k_norm_w', 'm_v_norm_w', 'm_w_s', 'm_b_s', 'm_w_out', 'm_norm3_w', 'm_ffn2_w1', 'm_ffn2_w3', 'm_ffn2_w2', 'v_c_ctx', 'v_w_ada', 'v_b_ada', 'v_norm1_w', 'v_ffn1_w1', 'v_ffn1_w3', 'v_ffn1_w2', 'v_norm2_w', 'v_w_in', 'v_q_a_norm_w', 'v_w_uq', 'v_kv_a_norm_w', 'v_w_ukv', 'v_q_norm_w', 'v_k_norm_w', 'v_v_norm_w', 'v_w_s', 'v_b_s', 'v_w_out', 'v_norm3_w', 'v_ffn2_w1', 'v_ffn2_w3', 'v_ffn2_w2']
TWIN_OUTPUTS = ['loss', 'grad_x', 'grad_c_ctx', 'grad_w_ada', 'grad_b_ada', 'grad_norm1_w', 'grad_ffn1_w1', 'grad_ffn1_w3', 'grad_ffn1_w2', 'grad_norm2_w', 'grad_w_in', 'grad_q_a_norm_w', 'grad_w_uq', 'grad_kv_a_norm_w', 'grad_w_ukv', 'grad_q_norm_w', 'grad_k_norm_w', 'grad_v_norm_w', 'grad_w_s', 'grad_b_s', 'grad_w_out', 'grad_norm3_w', 'grad_ffn2_w1', 'grad_ffn2_w3', 'grad_ffn2_w2', 'delta_c_ctx', 'delta_w_ada', 'delta_b_ada', 'delta_norm1_w', 'delta_ffn1_w1', 'delta_ffn1_w3', 'delta_ffn1_w2', 'delta_norm2_w', 'delta_w_in', 'delta_q_a_norm_w', 'delta_w_uq', 'delta_kv_a_norm_w', 'delta_w_ukv', 'delta_q_norm_w', 'delta_k_norm_w', 'delta_v_norm_w', 'delta_w_s', 'delta_b_s', 'delta_w_out', 'delta_norm3_w', 'delta_ffn2_w1', 'delta_ffn2_w3', 'delta_ffn2_w2', 'new_m_c_ctx', 'new_m_w_ada', 'new_m_b_ada', 'new_m_norm1_w', 'new_m_ffn1_w1', 'new_m_ffn1_w3', 'new_m_ffn1_w2', 'new_m_norm2_w', 'new_m_w_in', 'new_m_q_a_norm_w', 'new_m_w_uq', 'new_m_kv_a_norm_w', 'new_m_w_ukv', 'new_m_q_norm_w', 'new_m_k_norm_w', 'new_m_v_norm_w', 'new_m_w_s', 'new_m_b_s', 'new_m_w_out', 'new_m_norm3_w', 'new_m_ffn2_w1', 'new_m_ffn2_w3', 'new_m_ffn2_w2', 'new_v_c_ctx', 'new_v_w_ada', 'new_v_b_ada', 'new_v_norm1_w', 'new_v_ffn1_w1', 'new_v_ffn1_w3', 'new_v_ffn1_w2', 'new_v_norm2_w', 'new_v_w_in', 'new_v_q_a_norm_w', 'new_v_w_uq', 'new_v_kv_a_norm_w', 'new_v_w_ukv', 'new_v_q_norm_w', 'new_v_k_norm_w', 'new_v_v_norm_w', 'new_v_w_s', 'new_v_b_s', 'new_v_w_out', 'new_v_norm3_w', 'new_v_ffn2_w1', 'new_v_ffn2_w3', 'new_v_ffn2_w2']
TWIN_LEAF_KINDS = {'loss': 'loss', 'grad_x': 'grad_x', 'grad_c_ctx': 'grad_w', 'grad_w_ada': 'grad_w', 'grad_b_ada': 'grad_w', 'grad_norm1_w': 'grad_w', 'grad_ffn1_w1': 'grad_w', 'grad_ffn1_w3': 'grad_w', 'grad_ffn1_w2': 'grad_w', 'grad_norm2_w': 'grad_w', 'grad_w_in': 'grad_w', 'grad_q_a_norm_w': 'grad_w', 'grad_w_uq': 'grad_w', 'grad_kv_a_norm_w': 'grad_w', 'grad_w_ukv': 'grad_w', 'grad_q_norm_w': 'grad_w', 'grad_k_norm_w': 'grad_w', 'grad_v_norm_w': 'grad_w', 'grad_w_s': 'grad_w', 'grad_b_s': 'grad_w', 'grad_w_out': 'grad_w', 'grad_norm3_w': 'grad_w', 'grad_ffn2_w1': 'grad_w', 'grad_ffn2_w3': 'grad_w', 'grad_ffn2_w2': 'grad_w', 'delta_c_ctx': 'delta_w', 'delta_w_ada': 'delta_w', 'delta_b_ada': 'delta_w', 'delta_norm1_w': 'delta_w', 'delta_ffn1_w1': 'delta_w', 'delta_ffn1_w3': 'delta_w', 'delta_ffn1_w2': 'delta_w', 'delta_norm2_w': 'delta_w', 'delta_w_in': 'delta_w', 'delta_q_a_norm_w': 'delta_w', 'delta_w_uq': 'delta_w', 'delta_kv_a_norm_w': 'delta_w', 'delta_w_ukv': 'delta_w', 'delta_q_norm_w': 'delta_w', 'delta_k_norm_w': 'delta_w', 'delta_v_norm_w': 'delta_w', 'delta_w_s': 'delta_w', 'delta_b_s': 'delta_w', 'delta_w_out': 'delta_w', 'delta_norm3_w': 'delta_w', 'delta_ffn2_w1': 'delta_w', 'delta_ffn2_w3': 'delta_w', 'delta_ffn2_w2': 'delta_w', 'new_m_c_ctx': 'new_m', 'new_m_w_ada': 'new_m', 'new_m_b_ada': 'new_m', 'new_m_norm1_w': 'new_m', 'new_m_ffn1_w1': 'new_m', 'new_m_ffn1_w3': 'new_m', 'new_m_ffn1_w2': 'new_m', 'new_m_norm2_w': 'new_m', 'new_m_w_in': 'new_m', 'new_m_q_a_norm_w': 'new_m', 'new_m_w_uq': 'new_m', 'new_m_kv_a_norm_w': 'new_m', 'new_m_w_ukv': 'new_m', 'new_m_q_norm_w': 'new_m', 'new_m_k_norm_w': 'new_m', 'new_m_v_norm_w': 'new_m', 'new_m_w_s': 'new_m', 'new_m_b_s': 'new_m', 'new_m_w_out': 'new_m', 'new_m_norm3_w': 'new_m', 'new_m_ffn2_w1': 'new_m', 'new_m_ffn2_w3': 'new_m', 'new_m_ffn2_w2': 'new_m', 'new_v_c_ctx': 'new_v', 'new_v_w_ada': 'new_v', 'new_v_b_ada': 'new_v', 'new_v_norm1_w': 'new_v', 'new_v_ffn1_w1': 'new_v', 'new_v_ffn1_w3': 'new_v', 'new_v_ffn1_w2': 'new_v', 'new_v_norm2_w': 'new_v', 'new_v_w_in': 'new_v', 'new_v_q_a_norm_w': 'new_v', 'new_v_w_uq': 'new_v', 'new_v_kv_a_norm_w': 'new_v', 'new_v_w_ukv': 'new_v', 'new_v_q_norm_w': 'new_v', 'new_v_k_norm_w': 'new_v', 'new_v_v_norm_w': 'new_v', 'new_v_w_s': 'new_v', 'new_v_b_s': 'new_v', 'new_v_w_out': 'new_v', 'new_v_norm3_w': 'new_v', 'new_v_ffn2_w1': 'new_v', 'new_v_ffn2_w3': 'new_v', 'new_v_ffn2_w2': 'new_v'}


def _forward(args):
    return _fwd_reference(*[args[k] for k in FWD_PARAMS])


def _output_shape():
    out = _jax.eval_shape(lambda: _forward(_fwd_setup_inputs(0)))
    return out.shape, out.dtype

N_MICROBATCH = 1
ADAM_LR = 0.001
ADAM_B1 = 0.9
ADAM_B2 = 0.999
ADAM_EPS = 1e-08
ADAM_WD = 0.01
ADAM_STEP = 10
PER_EXAMPLE_BATCH_AXIS = {'x': 0, 'c': 0, 'ctx': 0, 'loss_target': 0}
SHARED_INPUTS = []
_WEIGHT_DTYPES = {'c_ctx': _jnp.float32, 'w_ada': _jnp.float32, 'b_ada': _jnp.float32, 'norm1_w': _jnp.float32, 'ffn1_w1': _jnp.float32, 'ffn1_w3': _jnp.float32, 'ffn1_w2': _jnp.float32, 'norm2_w': _jnp.float32, 'w_in': _jnp.float32, 'q_a_norm_w': _jnp.float32, 'w_uq': _jnp.float32, 'kv_a_norm_w': _jnp.float32, 'w_ukv': _jnp.float32, 'q_norm_w': _jnp.float32, 'k_norm_w': _jnp.float32, 'v_norm_w': _jnp.float32, 'w_s': _jnp.float32, 'b_s': _jnp.float32, 'w_out': _jnp.float32, 'norm3_w': _jnp.float32, 'ffn2_w1': _jnp.float32, 'ffn2_w3': _jnp.float32, 'ffn2_w2': _jnp.float32}
MOMENT_SCALE = {'c_ctx': 3.906546e-02, 'w_ada': 1.325355e+00, 'b_ada': 2.731763e+00, 'norm1_w': 1.522412e+00, 'ffn1_w1': 3.446185e-02, 'ffn1_w3': 3.436980e-02, 'ffn1_w2': 5.605575e-02, 'norm2_w': 3.045196e+00, 'w_in': 4.787838e-01, 'q_a_norm_w': 2.546640e-02, 'w_uq': 2.030007e-02, 'kv_a_norm_w': 1.873267e+00, 'w_ukv': 3.250427e-01, 'q_norm_w': 1.269566e-01, 'k_norm_w': 1.278490e-01, 'v_norm_w': 3.368897e+00, 'w_s': 6.399797e-01, 'b_s': 1.602178e+00, 'w_out': 4.278462e-01, 'norm3_w': 1.613852e+00, 'ffn2_w1': 3.434199e-02, 'ffn2_w3': 3.306735e-02, 'ffn2_w2': 5.359900e-02}


def _to_microbatches(a, axis):
    t = _jnp.moveaxis(a, axis, 0)
    t = t.reshape((N_MICROBATCH, t.shape[0] // N_MICROBATCH) + t.shape[1:])
    return _jnp.moveaxis(t, 1, axis + 1)


def setup_inputs(seed: int = 0) -> dict:
    inp = _fwd_setup_inputs(seed)
    key = _jax.random.fold_in(_jax.random.key(seed), 7919)
    shape, _ = _output_shape()
    out = dict(inp)
    out["loss_target"] = _jax.random.normal(_jax.random.fold_in(key, 0), shape, _jnp.float32)
    for i, name in enumerate(TWIN_WEIGHTS):
        w = inp[name].astype(_jnp.float32)
        if MOMENT_SCALE is None:
            s = _jnp.sqrt(_jnp.mean(_jnp.square(w)) + 1e-30)
        else:
            s = MOMENT_SCALE[name]
        km, kv = _jax.random.split(_jax.random.fold_in(key, i + 1))
        out[name] = w
        out["m_" + name] = s * _jax.random.normal(km, w.shape, _jnp.float32)
        out["v_" + name] = (s * s) * _jax.random.uniform(kv, w.shape, _jnp.float32, 0.5, 1.5)
    if N_MICROBATCH > 1:
        for name, axis in PER_EXAMPLE_BATCH_AXIS.items():
            out[name] = _to_microbatches(out[name], axis)
    return {'x': out['x'], 'c': out['c'], 'ctx': out['ctx'], 'c_ctx': out['c_ctx'], 'w_ada': out['w_ada'], 'b_ada': out['b_ada'], 'norm1_w': out['norm1_w'], 'ffn1_w1': out['ffn1_w1'], 'ffn1_w3': out['ffn1_w3'], 'ffn1_w2': out['ffn1_w2'], 'norm2_w': out['norm2_w'], 'w_in': out['w_in'], 'q_a_norm_w': out['q_a_norm_w'], 'w_uq': out['w_uq'], 'kv_a_norm_w': out['kv_a_norm_w'], 'w_ukv': out['w_ukv'], 'q_norm_w': out['q_norm_w'], 'k_norm_w': out['k_norm_w'], 'v_norm_w': out['v_norm_w'], 'w_s': out['w_s'], 'b_s': out['b_s'], 'w_out': out['w_out'], 'norm3_w': out['norm3_w'], 'ffn2_w1': out['ffn2_w1'], 'ffn2_w3': out['ffn2_w3'], 'ffn2_w2': out['ffn2_w2'], 'loss_target': out['loss_target'], 'm_c_ctx': out['m_c_ctx'], 'm_w_ada': out['m_w_ada'], 'm_b_ada': out['m_b_ada'], 'm_norm1_w': out['m_norm1_w'], 'm_ffn1_w1': out['m_ffn1_w1'], 'm_ffn1_w3': out['m_ffn1_w3'], 'm_ffn1_w2': out['m_ffn1_w2'], 'm_norm2_w': out['m_norm2_w'], 'm_w_in': out['m_w_in'], 'm_q_a_norm_w': out['m_q_a_norm_w'], 'm_w_uq': out['m_w_uq'], 'm_kv_a_norm_w': out['m_kv_a_norm_w'], 'm_w_ukv': out['m_w_ukv'], 'm_q_norm_w': out['m_q_norm_w'], 'm_k_norm_w': out['m_k_norm_w'], 'm_v_norm_w': out['m_v_norm_w'], 'm_w_s': out['m_w_s'], 'm_b_s': out['m_b_s'], 'm_w_out': out['m_w_out'], 'm_norm3_w': out['m_norm3_w'], 'm_ffn2_w1': out['m_ffn2_w1'], 'm_ffn2_w3': out['m_ffn2_w3'], 'm_ffn2_w2': out['m_ffn2_w2'], 'v_c_ctx': out['v_c_ctx'], 'v_w_ada': out['v_w_ada'], 'v_b_ada': out['v_b_ada'], 'v_norm1_w': out['v_norm1_w'], 'v_ffn1_w1': out['v_ffn1_w1'], 'v_ffn1_w3': out['v_ffn1_w3'], 'v_ffn1_w2': out['v_ffn1_w2'], 'v_norm2_w': out['v_norm2_w'], 'v_w_in': out['v_w_in'], 'v_q_a_norm_w': out['v_q_a_norm_w'], 'v_w_uq': out['v_w_uq'], 'v_kv_a_norm_w': out['v_kv_a_norm_w'], 'v_w_ukv': out['v_w_ukv'], 'v_q_norm_w': out['v_q_norm_w'], 'v_k_norm_w': out['v_k_norm_w'], 'v_v_norm_w': out['v_v_norm_w'], 'v_w_s': out['v_w_s'], 'v_b_s': out['v_b_s'], 'v_w_out': out['v_w_out'], 'v_norm3_w': out['v_norm3_w'], 'v_ffn2_w1': out['v_ffn2_w1'], 'v_ffn2_w3': out['v_ffn2_w3'], 'v_ffn2_w2': out['v_ffn2_w2']}


def _loss(weights, diff, rest, loss_target):
    with _jax.named_scope("forward"):
        args = {**rest, TWIN_DIFF_INPUT: diff, **{k: w.astype(_WEIGHT_DTYPES[k]) for k, w in weights.items()}}
        y = _forward(args)
    with _jax.named_scope("loss_head"):
        err = _jnp.square(y.astype(_jnp.float32) - loss_target)
        return 0.5 * _jnp.sum(_jnp.mean(err, axis=-1)) if err.ndim else 0.5 * err


def _adamw(w, g, m, v):
    m = ADAM_B1 * m + (1.0 - ADAM_B1) * g
    v = ADAM_B2 * v + (1.0 - ADAM_B2) * _jnp.square(g)
    m_hat = m / (1.0 - ADAM_B1 ** ADAM_STEP)
    v_hat = v / (1.0 - ADAM_B2 ** ADAM_STEP)
    delta = -ADAM_LR * (m_hat / (_jnp.sqrt(v_hat) + ADAM_EPS) + ADAM_WD * w)
    return delta, m, v


def reference(x, c, ctx, c_ctx, w_ada, b_ada, norm1_w, ffn1_w1, ffn1_w3, ffn1_w2, norm2_w, w_in, q_a_norm_w, w_uq, kv_a_norm_w, w_ukv, q_norm_w, k_norm_w, v_norm_w, w_s, b_s, w_out, norm3_w, ffn2_w1, ffn2_w3, ffn2_w2, loss_target, m_c_ctx, m_w_ada, m_b_ada, m_norm1_w, m_ffn1_w1, m_ffn1_w3, m_ffn1_w2, m_norm2_w, m_w_in, m_q_a_norm_w, m_w_uq, m_kv_a_norm_w, m_w_ukv, m_q_norm_w, m_k_norm_w, m_v_norm_w, m_w_s, m_b_s, m_w_out, m_norm3_w, m_ffn2_w1, m_ffn2_w3, m_ffn2_w2, v_c_ctx, v_w_ada, v_b_ada, v_norm1_w, v_ffn1_w1, v_ffn1_w3, v_ffn1_w2, v_norm2_w, v_w_in, v_q_a_norm_w, v_w_uq, v_kv_a_norm_w, v_w_ukv, v_q_norm_w, v_k_norm_w, v_v_norm_w, v_w_s, v_b_s, v_w_out, v_norm3_w, v_ffn2_w1, v_ffn2_w3, v_ffn2_w2):
    given = dict(x=x, c=c, ctx=ctx, c_ctx=c_ctx, w_ada=w_ada, b_ada=b_ada, norm1_w=norm1_w, ffn1_w1=ffn1_w1, ffn1_w3=ffn1_w3, ffn1_w2=ffn1_w2, norm2_w=norm2_w, w_in=w_in, q_a_norm_w=q_a_norm_w, w_uq=w_uq, kv_a_norm_w=kv_a_norm_w, w_ukv=w_ukv, q_norm_w=q_norm_w, k_norm_w=k_norm_w, v_norm_w=v_norm_w, w_s=w_s, b_s=b_s, w_out=w_out, norm3_w=norm3_w, ffn2_w1=ffn2_w1, ffn2_w3=ffn2_w3, ffn2_w2=ffn2_w2, loss_target=loss_target, m_c_ctx=m_c_ctx, m_w_ada=m_w_ada, m_b_ada=m_b_ada, m_norm1_w=m_norm1_w, m_ffn1_w1=m_ffn1_w1, m_ffn1_w3=m_ffn1_w3, m_ffn1_w2=m_ffn1_w2, m_norm2_w=m_norm2_w, m_w_in=m_w_in, m_q_a_norm_w=m_q_a_norm_w, m_w_uq=m_w_uq, m_kv_a_norm_w=m_kv_a_norm_w, m_w_ukv=m_w_ukv, m_q_norm_w=m_q_norm_w, m_k_norm_w=m_k_norm_w, m_v_norm_w=m_v_norm_w, m_w_s=m_w_s, m_b_s=m_b_s, m_w_out=m_w_out, m_norm3_w=m_norm3_w, m_ffn2_w1=m_ffn2_w1, m_ffn2_w3=m_ffn2_w3, m_ffn2_w2=m_ffn2_w2, v_c_ctx=v_c_ctx, v_w_ada=v_w_ada, v_b_ada=v_b_ada, v_norm1_w=v_norm1_w, v_ffn1_w1=v_ffn1_w1, v_ffn1_w3=v_ffn1_w3, v_ffn1_w2=v_ffn1_w2, v_norm2_w=v_norm2_w, v_w_in=v_w_in, v_q_a_norm_w=v_q_a_norm_w, v_w_uq=v_w_uq, v_kv_a_norm_w=v_kv_a_norm_w, v_w_ukv=v_w_ukv, v_q_norm_w=v_q_norm_w, v_k_norm_w=v_k_norm_w, v_v_norm_w=v_v_norm_w, v_w_s=v_w_s, v_b_s=v_b_s, v_w_out=v_w_out, v_norm3_w=v_norm3_w, v_ffn2_w1=v_ffn2_w1, v_ffn2_w3=v_ffn2_w3, v_ffn2_w2=v_ffn2_w2)
    weights = {n: given[n] for n in TWIN_WEIGHTS}
    shared = {n: given[n] for n in SHARED_INPUTS}
    per_example = {n: given[n] for n in ['x', 'c', 'ctx']}
    grad_fn = _jax.value_and_grad(_loss, argnums=(0, 1))

    def one_microbatch(ex, loss_target):
        ex = dict(ex)
        diff = ex.pop(TWIN_DIFF_INPUT)
        return grad_fn(weights, diff, {**shared, **ex}, loss_target)

    if N_MICROBATCH == 1:
        loss, (grad_w, grad_x) = one_microbatch(per_example, given["loss_target"])
    else:
        def body(carry, xs):
            loss_sum, grad_sum = carry
            l_k, (gw_k, gx_k) = one_microbatch(xs[0], xs[1])
            with _jax.named_scope("update"):
                return (loss_sum + l_k, _jax.tree.map(_jnp.add, grad_sum, gw_k)), gx_k

        init = (_jnp.zeros((), _jnp.float32), _jax.tree.map(_jnp.zeros_like, weights))
        (loss, grad_w), grad_x = _jax.lax.scan(body, init, (per_example, given["loss_target"]))
    with _jax.named_scope("update"):
        delta_w, new_m, new_v = {}, {}, {}
        for n in TWIN_WEIGHTS:
            delta_w[n], new_m[n], new_v[n] = _adamw(weights[n], grad_w[n], given["m_" + n], given["v_" + n])
    return (loss, grad_x, *[grad_w[n] for n in TWIN_WEIGHTS], *[delta_w[n] for n in TWIN_WEIGHTS],
            *[new_m[n] for n in TWIN_WEIGHTS], *[new_v[n] for n in TWIN_WEIGHTS])
```

```python
import functools

import numpy as np
import jax
import jax.numpy as jnp
from jax import lax
from jax.experimental import pallas as pl
from jax.experimental.pallas import tpu as pltpu

F32 = jnp.float32
BF16 = jnp.bfloat16

D = 1024
FF = 2816
FC = 256
H = 8
DN, DR, DV = 64, 32, 64
DH = DN + DR
QL, KVL = 256, 128
G, GD, CH = 8, 64, 128
NMOD = 9
EPS = 1e-6
GRID_W = 64
ROPE_BASE = 10000.0
NDEV = 8
LANE = 128
WIN_ROWS = 1536
IN_COLS = 1440
MIB = 1 << 20

ADAM_LR, ADAM_B1, ADAM_B2, ADAM_EPS, ADAM_WD, ADAM_STEP = 0.001, 0.9, 0.999, 1e-08, 0.01, 10

MESH = pl.DeviceIdType.MESH


def _mm(a, b):
    return jnp.dot(a, b, preferred_element_type=F32)


def _mm_nt(a, b):
    return lax.dot_general(a, b, (((1,), (1,)), ((), ())), preferred_element_type=F32)


def _mm_tn(a, b):
    return lax.dot_general(a, b, (((0,), (0,)), ((), ())), preferred_element_type=F32)


def _dot_hl(x, m):
    hi = x.astype(BF16)
    lo = (x - hi.astype(F32)).astype(BF16)
    return _mm(hi, m) + _mm(lo, m)


def _sigmoid(a):
    return 1.0 / (1.0 + jnp.exp(-a))


_G0 = 0.7978845608028654
_G1 = 0.044715


def _gelu(x):
    return 0.5 * x * (1.0 + jnp.tanh(_G0 * (x + _G1 * (x * x * x))))


def _gelu_grad(x):
    th = jnp.tanh(_G0 * (x + _G1 * (x * x * x)))
    return 0.5 * (1.0 + th) + 0.5 * x * (1.0 - th * th) * (_G0 * (1.0 + 3.0 * _G1 * x * x))


def _rowsum(y):
    return jnp.sum(y, axis=0, keepdims=True)


def _rms(x):
    return lax.rsqrt(jnp.mean(x * x, axis=-1, keepdims=True) + EPS)


def _pcall(body, *, name, out_shape, in_specs, out_specs, grid=None, scratch=(), vmem_mb=32):
    kw = {}
    if grid is not None:
        kw["grid"] = grid
        sem = ("arbitrary",) * len(grid)
    else:
        sem = None
    return pl.pallas_call(
        body, name=name, out_shape=out_shape, in_specs=in_specs, out_specs=out_specs,
        scratch_shapes=list(scratch),
        compiler_params=pltpu.CompilerParams(dimension_semantics=sem, vmem_limit_bytes=vmem_mb * MIB),
        **kw)


def _const(shape):
    nd = len(shape)
    return pl.BlockSpec(shape, lambda *_: (0,) * nd)


def _const1(shape):
    nd = len(shape)
    return pl.BlockSpec(shape, lambda *_: (0,) * nd, pipeline_mode=pl.Buffered(1))


def _sds(shape, dt):
    return jax.ShapeDtypeStruct(shape, dt)


def _seg(width, group):
    m = np.zeros((width, LANE), np.float32)
    m[np.arange(width), np.arange(width) // group] = 1.0
    return m


def _rot(n):
    m = np.zeros((n, n), np.float32)
    for base in range(0, n, 16):
        for j in range(8):
            m[base + j + 8, base + j] = -1.0
            m[base + j, base + j + 8] = 1.0
    return m


def _consts():
    seg64 = _seg(H * DN, DN)
    seg32 = _seg(H * DR, DR)
    p256 = _rot(H * DR)
    p128 = np.zeros((LANE, LANE), np.float32)
    p128[:DR, :DR] = _rot(DR)
    tile = np.zeros((LANE, H * DR), np.float32)
    for h in range(H):
        tile[np.arange(DR), h * DR + np.arange(DR)] = 1.0
    c = dict(seg64=seg64, seg64t=seg64.T, seg32=seg32, seg32t=seg32.T, p256=p256, p256t=p256.T,
             p128=p128, p128t=p128.T, tile=tile, tilet=tile.T)
    return {k: jnp.asarray(v, BF16) for k, v in c.items()}


def _all_gather(x, name):
    r, c = x.shape

    def body(x_ref, out_ref, send_sems, recv_sems, local_sem):
        mx, my, mc = lax.axis_index("x"), lax.axis_index("y"), lax.axis_index("c")
        me, sibling = (mx, my, mc), (mx, my, 1 - mc)
        chips = [(1 - mx, my), (mx, 1 - my), (1 - mx, 1 - my)]

        def blk(px, py, pc):
            return out_ref.at[4 * px + 2 * py + pc]

        def copy(k, block, to, src=None):
            return pltpu.make_async_remote_copy(
                src_ref=blk(*block) if src is None else src, dst_ref=blk(*block),
                send_sem=send_sems.at[k], recv_sem=recv_sems.at[k], device_id=to, device_id_type=MESH)

        mine = pltpu.make_async_copy(x_ref, blk(*me), local_sem)
        mine.start()
        first = [copy(0, me, sibling, src=x_ref)]
        first += [copy(1 + j, me, (*chip, mc), src=x_ref) for j, chip in enumerate(chips)]
        for cp in first:
            cp.start()
        passed = [copy(4 + j, (*chip, mc), sibling) for j, chip in enumerate(chips)]
        for j, chip in enumerate(chips):
            copy(1 + j, (*chip, mc), me).wait_recv()
            passed[j].start()
        copy(0, sibling, me).wait_recv()
        for j, chip in enumerate(chips):
            copy(4 + j, (*chip, 1 - mc), me).wait_recv()
        for cp in first + passed:
            cp.wait_send()
        mine.wait()

    return pl.pallas_call(
        body, name=name, out_shape=_sds((NDEV, r, c), x.dtype),
        in_specs=[pl.BlockSpec(memory_space=pl.ANY)], out_specs=pl.BlockSpec(memory_space=pl.ANY),
        scratch_shapes=[pltpu.SemaphoreType.DMA((7,)), pltpu.SemaphoreType.DMA((7,)), pltpu.SemaphoreType.DMA(())],
    )(x)


def _all_to_all(x, name):
    _, r, c = x.shape

    def body(x_ref, out_ref, send_sems, recv_sems, local_sem):
        mx, my, mc = lax.axis_index("x"), lax.axis_index("y"), lax.axis_index("c")
        me_idx = 4 * mx + 2 * my + mc
        mine = pltpu.make_async_copy(x_ref.at[me_idx], out_ref.at[me_idx], local_sem)
        mine.start()
        sends, recvs = [], []
        for k in range(1, NDEV):
            px = 1 - mx if (k & 4) else mx
            py = 1 - my if (k & 2) else my
            pc = 1 - mc if (k & 1) else mc
            pidx = 4 * px + 2 * py + pc
            sends.append(pltpu.make_async_remote_copy(
                src_ref=x_ref.at[pidx], dst_ref=out_ref.at[me_idx],
                send_sem=send_sems.at[k - 1], recv_sem=recv_sems.at[k - 1],
                device_id=(px, py, pc), device_id_type=MESH))
            recvs.append(pltpu.make_async_remote_copy(
                src_ref=x_ref.at[pidx], dst_ref=out_ref.at[pidx],
                send_sem=send_sems.at[k - 1], recv_sem=recv_sems.at[k - 1],
                device_id=(px, py, pc), device_id_type=MESH))
        for cp in sends:
            cp.start()
        for cp in recvs:
            cp.wait_recv()
        for cp in sends:
            cp.wait_send()
        mine.wait()

    return pl.pallas_call(
        body, name=name, out_shape=_sds(x.shape, x.dtype),
        in_specs=[pl.BlockSpec(memory_space=pl.ANY)], out_specs=pl.BlockSpec(memory_space=pl.ANY),
        scratch_shapes=[pltpu.SemaphoreType.DMA((7,)), pltpu.SemaphoreType.DMA((7,)), pltpu.SemaphoreType.DMA(())],
    )(x)


def _sum8(x, tr, name):
    _, r, c = x.shape

    def body(x_ref, o_ref):
        acc = x_ref[0].astype(F32)
        for s in range(1, NDEV):
            acc = acc + x_ref[s].astype(F32)
        o_ref[...] = acc

    return _pcall(body, name=name, grid=(r // tr,), out_shape=_sds((r, c), F32),
                  in_specs=[pl.BlockSpec((NDEV, tr, c), lambda t: (0, t, 0))],
                  out_specs=pl.BlockSpec((tr, c), lambda t: (t, 0)))(x)


def _ada_fwd(a_raw, w_loc, b_loc):
    ncol = w_loc.shape[1]

    def body(a_ref, w_ref, b_ref, o_ref):
        a = a_ref[...]
        act = (a * _sigmoid(a)).astype(BF16)
        o_ref[...] = _mm(act, w_ref[...].astype(BF16)) + b_ref[...]

    return _pcall(body, name="ada_fwd", out_shape=_sds((a_raw.shape[0], ncol), F32),
                  in_specs=[pl.BlockSpec(memory_space=pltpu.VMEM)] * 3,
                  out_specs=pl.BlockSpec(memory_space=pltpu.VMEM))(a_raw, w_loc, b_loc)


def _ada_bwd(a_raw, cctx_col, g_all, g_cols, w_loc, nb):
    nrow = a_raw.shape[0]
    ncol = w_loc.shape[1]

    def body(a_ref, cc_ref, gall_ref, g_ref, w_ref, dw_ref, pc_ref, gb_ref):
        a = a_ref[...]
        rowid = lax.broadcasted_iota(jnp.int32, (nrow, 1), 0) % 8
        act = jnp.where(rowid < nb, a * _sigmoid(a), 0.0).astype(BF16)
        g = g_ref[...]
        gc = _rowsum(jnp.where(rowid == nb, g, 0.0))
        cc = cc_ref[...]
        dw_ref[...] = _mm_tn(act, g.astype(BF16)) + (cc * _sigmoid(cc)) * gc
        pc_ref[...] = jnp.sum(w_ref[...] * gc, axis=1, keepdims=True)
        gb_ref[...] = _rowsum(gall_ref[...])

    return _pcall(body, name="ada_bwd",
                  out_shape=(_sds((D, ncol), F32), _sds((D, 1), F32), _sds((1, g_all.shape[1]), F32)),
                  in_specs=[pl.BlockSpec(memory_space=pltpu.VMEM)] * 5,
                  out_specs=(pl.BlockSpec(memory_space=pltpu.VMEM),) * 3,
                  vmem_mb=48)(a_raw, cctx_col, g_all, g_cols, w_loc)


def _mod_spec(k, tpe, nrows):
    return pl.BlockSpec((1, k, D), lambda t: (jnp.minimum(t // tpe, nrows - 1), 0, 0))


def _ffn_fwd(xin, mod3, norm_w, w1t, w3t, w2, *, tm, n_tiles, tpe, name):
    nrows = mod3.shape[0]
    r = n_tiles * tm

    def body(x_ref, mod_ref, nw_ref, w1_ref, w3_ref, w2_ref, xo_ref, a_ref, b_ref, o_ref, acc_ref):
        x = x_ref[...]
        n = x * _rms(x) * nw_ref[...]
        shift, scale, gate = mod_ref[0, 0:1, :], mod_ref[0, 1:2, :], mod_ref[0, 2:3, :]
        h = (n * (1.0 + scale) + shift).astype(BF16)
        for j in range(FF // FC):
            sl = slice(j * FC, (j + 1) * FC)
            a = _mm_nt(h, w1_ref[sl, :])
            b = _mm_nt(h, w3_ref[sl, :])
            a_ref[:, sl] = a.astype(BF16)
            b_ref[:, sl] = b.astype(BF16)
            g = (a * _sigmoid(a) * b).astype(BF16)
            part = _mm(g, w2_ref[sl, :])
            if j == 0:
                acc_ref[...] = part
            else:
                acc_ref[...] += part
        o = acc_ref[...]
        o_ref[...] = o.astype(BF16)
        xo_ref[...] = x + (0.5 * gate) * o

    row = lambda cols: pl.BlockSpec((tm, cols), lambda t: (t, 0))
    return _pcall(
        body, name=name, grid=(n_tiles,),
        out_shape=(_sds((r, D), F32), _sds((r, FF), BF16), _sds((r, FF), BF16), _sds((r, D), BF16)),
        in_specs=[row(D), _mod_spec(3, tpe, nrows), _const((1, D)), _const1((FF, D)), _const1((FF, D)), _const1((FF, D))],
        out_specs=(row(D), row(FF), row(FF), row(D)),
        scratch=[pltpu.VMEM((tm, D), F32)], vmem_mb=56)(xin, mod3, norm_w, w1t, w3t, w2)


def _ffn_bwd_dx(dout, xin, a, b, o, mod3, norm_w, w1t, w3t, w2, *, tm, n_tiles, tpe, n_lat, name):
    nrows = mod3.shape[0]
    r = n_tiles * tm

    def body(dout_ref, x_ref, a_ref, b_ref, o_ref, mod_ref, nw_ref, w1_ref, w3_ref, w2_ref,
             dx_ref, da_ref, db_ref, g_ref, do_ref, h_ref, dmod_ref, dnw_ref, acc_ref):
        t = pl.program_id(0)
        x = x_ref[...]
        dout = dout_ref[...]
        rr = _rms(x)
        xh = x * rr
        nw = nw_ref[...]
        n = xh * nw
        shift, scale, gate = mod_ref[0, 0:1, :], mod_ref[0, 1:2, :], mod_ref[0, 2:3, :]
        h = (n * (1.0 + scale) + shift).astype(BF16)
        h_ref[...] = h
        d_o = ((0.5 * gate) * dout).astype(BF16)
        do_ref[...] = d_o
        dgate = _rowsum(0.5 * o_ref[...].astype(F32) * dout)
        for j in range(FF // FC):
            sl = slice(j * FC, (j + 1) * FC)
            av = a_ref[:, sl].astype(F32)
            bv = b_ref[:, sl].astype(F32)
            dg = _mm_nt(d_o, w2_ref[sl, :])
            sig = _sigmoid(av)
            sa = av * sig
            g_ref[:, sl] = (sa * bv).astype(BF16)
            da = (dg * bv * (sig * (1.0 + av * (1.0 - sig)))).astype(BF16)
            db = (dg * sa).astype(BF16)
            da_ref[:, sl] = da
            db_ref[:, sl] = db
            part = _mm(da, w1_ref[sl, :]) + _mm(db, w3_ref[sl, :])
            if j == 0:
                acc_ref[...] = part
            else:
                acc_ref[...] += part
        dh = acc_ref[...]
        dn = dh * (1.0 + scale)
        dxh = dn * nw
        dx_ref[...] = dout + rr * (dxh - xh * jnp.mean(dxh * xh, axis=-1, keepdims=True))

        first = jnp.where(t < n_lat, t % tpe == 0, t == n_lat)

        @pl.when(first)
        def _():
            dmod_ref[...] = jnp.zeros_like(dmod_ref)

        @pl.when(t == 0)
        def _():
            dnw_ref[...] = jnp.zeros_like(dnw_ref)

        dmod_ref[0, 0:1, :] += _rowsum(dh)
        dmod_ref[0, 1:2, :] += _rowsum(dh * n)
        dmod_ref[0, 2:3, :] += dgate
        dnw_ref[...] += _rowsum(dn * xh)

    row = lambda cols: pl.BlockSpec((tm, cols), lambda t: (t, 0))
    return _pcall(
        body, name=name, grid=(n_tiles,),
        out_shape=(_sds((r, D), F32), _sds((r, FF), BF16), _sds((r, FF), BF16), _sds((r, FF), BF16),
                   _sds((r, D), BF16), _sds((r, D), BF16), _sds((nrows, 3, D), F32), _sds((1, D), F32)),
        in_specs=[row(D), row(D), row(FF), row(FF), row(D), _mod_spec(3, tpe, nrows), _const((1, D)),
                  _const1((FF, D)), _const1((FF, D)), _const1((FF, D))],
        out_specs=(row(D), row(FF), row(FF), row(FF), row(D), row(D), _mod_spec(3, tpe, nrows), _const((1, D))),
        scratch=[pltpu.VMEM((tm, D), F32)], vmem_mb=60)(dout, xin, a, b, o, mod3, norm_w, w1t, w3t, w2)


def _ffn_bwd_dw(h, d_o, da, db, g, *, tr, name):
    r = h.shape[0]
    fh = FF // 2
    nk = r // tr

    def body(h_ref, do_ref, da_ref, db_ref, g_ref, w1o, w3o, w2o, acc1, acc3, acc2):
        k = pl.program_id(1)

        @pl.when(k == 0)
        def _():
            acc1[...] = jnp.zeros_like(acc1)
            acc3[...] = jnp.zeros_like(acc3)
            acc2[...] = jnp.zeros_like(acc2)

        hv = h_ref[...]
        acc1[...] += _mm_tn(da_ref[...], hv)
        acc3[...] += _mm_tn(db_ref[...], hv)
        acc2[...] += _mm_tn(g_ref[...], do_ref[...])

        @pl.when(k == nk - 1)
        def _():
            w1o[...] = acc1[...].astype(BF16)
            w3o[...] = acc3[...].astype(BF16)
            w2o[...] = acc2[...].astype(BF16)

    rowd = pl.BlockSpec((tr, D), lambda f, k: (k, 0))
    rowf = pl.BlockSpec((tr, fh), lambda f, k: (k, f))
    outs = pl.BlockSpec((fh, D), lambda f, k: (f, 0))
    return _pcall(
        body, name=name, grid=(2, nk),
        out_shape=(_sds((FF, D), BF16),) * 3,
        in_specs=[rowd, rowd, rowf, rowf, rowf], out_specs=(outs,) * 3,
        scratch=[pltpu.VMEM((fh, D), F32)] * 3, vmem_mb=56)(h, d_o, da, db, g)


_PIECES = ((0, 128), (128, 384), (384, 896), (896, 1408), (1408, 1536))


def _proj_fwd(x1, mod2, norm_w, wint, *, tm, n_tiles, tpe, name="proj_fwd"):
    nrows = mod2.shape[0]
    r = n_tiles * tm

    def body(x_ref, mod_ref, nw_ref, w_ref, ckv_ref, q_ref, u_ref, v_ref, kpe_ref):
        x = x_ref[...]
        n = x * _rms(x) * nw_ref[...]
        h = (n * (1.0 + mod_ref[0, 1:2, :]) + mod_ref[0, 0:1, :]).astype(BF16)
        for (lo, hi), ref in zip(_PIECES, (ckv_ref, q_ref, u_ref, v_ref, kpe_ref)):
            ref[...] = _mm_nt(h, w_ref[lo:hi, :])

    row = lambda cols: pl.BlockSpec((tm, cols), lambda t: (t, 0))
    widths = [hi - lo for lo, hi in _PIECES]
    return _pcall(
        body, name=name, grid=(n_tiles,),
        out_shape=tuple(_sds((r, w), F32) for w in widths),
        in_specs=[row(D), _mod_spec(2, tpe, nrows), _const((1, D)), _const((WIN_ROWS, D))],
        out_specs=tuple(row(w) for w in widths), vmem_mb=40)(x1, mod2, norm_w, wint)


def _proj_bwd(dckv, dkpe, dq, du, dv, dx2, x1, mod2, norm_w, wint, *, tm, n_tiles, tpe, n_lat, name="proj_bwd"):
    nrows = mod2.shape[0]
    r = n_tiles * tm

    def body(dckv_ref, dkpe_ref, dq_ref, du_ref, dv_ref, dx2_ref, x_ref, mod_ref, nw_ref, w_ref,
             dx_ref, dw_ref, dmod_ref, dnw_ref, acc_ref):
        t = pl.program_id(0)
        is_lat = t < n_lat
        x = x_ref[...]
        rr = _rms(x)
        xh = x * rr
        nw = nw_ref[...]
        n = xh * nw
        scale = mod_ref[0, 1:2, :]
        h = (n * (1.0 + scale) + mod_ref[0, 0:1, :]).astype(BF16)

        @pl.when(t == 0)
        def _():
            dw_ref[...] = jnp.zeros_like(dw_ref)
            dnw_ref[...] = jnp.zeros_like(dnw_ref)

        dckv_v, dkpe_v = dckv_ref[...], dkpe_ref[...]
        acc_ref[...] = _mm(dckv_v, w_ref[0:128, :]) + _mm(dkpe_v, w_ref[1408:1536, :])
        dw_ref[0:128, :] += _mm_tn(dckv_v, h)
        dw_ref[1408:1536, :] += _mm_tn(dkpe_v, h)

        @pl.when(is_lat)
        def _():
            dq_v, du_v, dv_v = dq_ref[...], du_ref[...], dv_ref[...]
            acc_ref[...] += (_mm(dq_v, w_ref[128:384, :]) + _mm(du_v, w_ref[384:896, :])
                             + _mm(dv_v, w_ref[896:1408, :]))
            dw_ref[128:384, :] += _mm_tn(dq_v, h)
            dw_ref[384:896, :] += _mm_tn(du_v, h)
            dw_ref[896:1408, :] += _mm_tn(dv_v, h)

        dh = acc_ref[...]
        dn = dh * (1.0 + scale)
        dxh = dn * nw
        dx = rr * (dxh - xh * jnp.mean(dxh * xh, axis=-1, keepdims=True))
        dx_ref[...] = dx + jnp.where(is_lat, dx2_ref[...], 0.0)

        first = jnp.where(is_lat, t % tpe == 0, t == n_lat)

        @pl.when(first)
        def _():
            dmod_ref[...] = jnp.zeros_like(dmod_ref)

        dmod_ref[0, 0:1, :] += _rowsum(dh)
        dmod_ref[0, 1:2, :] += _rowsum(dh * n)
        dnw_ref[...] += _rowsum(dn * xh)

    row = lambda cols: pl.BlockSpec((tm, cols), lambda t: (t, 0))
    lat = lambda cols: pl.BlockSpec((tm, cols), lambda t: (jnp.minimum(t, n_lat - 1), 0))
    return _pcall(
        body, name=name, grid=(n_tiles,),
        out_shape=(_sds((r, D), F32), _sds((WIN_ROWS, D), F32), _sds((nrows, 2, D), F32), _sds((1, D), F32)),
        in_specs=[row(128), row(128), lat(256), lat(512), lat(512), lat(D), row(D), _mod_spec(2, tpe, nrows),
                  _const((1, D)), _const((WIN_ROWS, D))],
        out_specs=(row(D), _const((WIN_ROWS, D)), _mod_spec(2, tpe, nrows), _const((1, D))),
        scratch=[pltpu.VMEM((tm, D), F32)], vmem_mb=48)(dckv, dkpe, dq, du, dv, dx2, x1, mod2, norm_w, wint)


def _q_prep_fwd(qp, qa_w, wuq, wqn, wqr, cos, sin, cs, *, tm, n_lat, tpe):
    def body(qp_ref, qa_ref, wuq_ref, wqn_ref, wqr_ref, cos_ref, sin_ref, s64, s32, s64t, s32t, p256, qn_ref, qr_ref):
        x = qp_ref[...]
        cq = (x * _rms(x) * qa_ref[...]).astype(BF16)
        qraw = _mm_nt(cq, wuq_ref[...])
        qn, qr = qraw[:, :H * DN], qraw[:, H * DN:]
        ss = _dot_hl(qn * qn, s64[...]) + _dot_hl(qr * qr, s32[...])
        rh = lax.rsqrt(ss * (1.0 / DH) + EPS)
        qn_ref[...] = (qn * _dot_hl(rh, s64t[...]) * wqn_ref[...]).astype(BF16)
        tq = qr * _dot_hl(rh, s32t[...]) * wqr_ref[...]
        qr_ref[...] = (tq * cos_ref[...] + _dot_hl(tq, p256[...]) * sin_ref[...]).astype(BF16)

    row = lambda cols: pl.BlockSpec((tm, cols), lambda t: (t, 0))
    tab = pl.BlockSpec((tm, H * DR), lambda t: (t % tpe, 0))
    r = n_lat * tm
    return _pcall(
        body, name="q_prep_fwd", grid=(n_lat,),
        out_shape=(_sds((r, H * DN), BF16), _sds((r, H * DR), BF16)),
        in_specs=[row(QL), _const((1, QL)), _const((H * DH, QL)), _const((1, H * DN)), _const((1, H * DR)), tab, tab,
                  _const((H * DN, LANE)), _const((H * DR, LANE)), _const((LANE, H * DN)), _const((LANE, H * DR)),
                  _const((H * DR, H * DR))],
        out_specs=(row(H * DN), row(H * DR)))(
            qp, qa_w, wuq, wqn, wqr, cos, sin, cs["seg64"], cs["seg32"], cs["seg64t"], cs["seg32t"], cs["p256"])


def _q_prep_bwd(dqn, dqr, qp, qa_w, wuq, wqn, wqr, cos, sin, cs, *, tm, n_lat, tpe):
    def body(dqn_ref, dqr_ref, qp_ref, qa_ref, wuq_ref, wqn_ref, wqr_ref, cos_ref, sin_ref,
             s64, s32, s64t, s32t, p256t, dqp_ref, dwuq_ref, dqa_ref, dwqn_ref, dwqr_ref):
        t = pl.program_id(0)
        x = qp_ref[...]
        ra = _rms(x)
        xh = x * ra
        qa = qa_ref[...]
        cq = (xh * qa).astype(BF16)
        wuq_v = wuq_ref[...]
        qraw = _mm_nt(cq, wuq_v)
        qn, qr = qraw[:, :H * DN], qraw[:, H * DN:]
        ss = _dot_hl(qn * qn, s64[...]) + _dot_hl(qr * qr, s32[...])
        rh = lax.rsqrt(ss * (1.0 / DH) + EPS)
        r64, r32 = _dot_hl(rh, s64t[...]), _dot_hl(rh, s32t[...])
        yn, yr = qn * r64, qr * r32
        dtn = dqn_ref[...]
        dqr_v = dqr_ref[...]
        dtr = dqr_v * cos_ref[...] + _dot_hl(dqr_v * sin_ref[...], p256t[...])
        wqn_v, wqr_v = wqn_ref[...], wqr_ref[...]
        dyn, dyr = dtn * wqn_v, dtr * wqr_v
        mean_h = (_dot_hl(dyn * yn, s64[...]) + _dot_hl(dyr * yr, s32[...])) * (1.0 / DH)
        dqn_raw = r64 * (dyn - yn * _dot_hl(mean_h, s64t[...]))
        dqr_raw = r32 * (dyr - yr * _dot_hl(mean_h, s32t[...]))
        dqraw = jnp.concatenate([dqn_raw, dqr_raw], axis=-1).astype(BF16)
        dcq = _mm(dqraw, wuq_v)
        dxh = dcq * qa
        dqp_ref[...] = (ra * (dxh - xh * jnp.mean(dxh * xh, axis=-1, keepdims=True))).astype(BF16)

        @pl.when(t == 0)
        def _():
            dwuq_ref[...] = jnp.zeros_like(dwuq_ref)
            dqa_ref[...] = jnp.zeros_like(dqa_ref)
            dwqn_ref[...] = jnp.zeros_like(dwqn_ref)
            dwqr_ref[...] = jnp.zeros_like(dwqr_ref)

        dwuq_ref[...] += _mm_tn(dqraw, cq)
        dqa_ref[...] += _rowsum(dcq * xh)
        dwqn_ref[...] += _rowsum(dtn * yn)
        dwqr_ref[...] += _rowsum(dtr * yr)

    row = lambda cols: pl.BlockSpec((tm, cols), lambda t: (t, 0))
    tab = pl.BlockSpec((tm, H * DR), lambda t: (t % tpe, 0))
    r = n_lat * tm
    return _pcall(
        body, name="q_prep_bwd", grid=(n_lat,),
        out_shape=(_sds((r, QL), BF16), _sds((H * DH, QL), F32), _sds((1, QL), F32), _sds((1, H * DN), F32),
                   _sds((1, H * DR), F32)),
        in_specs=[row(H * DN), row(H * DR), row(QL), _const((1, QL)), _const((H * DH, QL)), _const((1, H * DN)),
                  _const((1, H * DR)), tab, tab,
                  _const((H * DN, LANE)), _const((H * DR, LANE)), _const((LANE, H * DN)), _const((LANE, H * DR)),
                  _const((H * DR, H * DR))],
        out_specs=(row(QL), _const((H * DH, QL)), _const((1, QL)), _const((1, H * DN)), _const((1, H * DR))))(
            dqn, dqr, qp, qa_w, wuq, wqn, wqr, cos, sin, cs["seg64"], cs["seg32"], cs["seg64t"], cs["seg32t"],
            cs["p256t"])


def _kv_tab_spec(tm, tpe, n_lat):
    return pl.BlockSpec((tm, LANE), lambda t: (jnp.where(t < n_lat, t % tpe, tpe), 0))


def _kv_prep_fwd(ckv, kpe, kva_w, wukv, wkn, wkr, cosk, sink, cs, *, tm, n_tiles, tpe, n_lat):
    def body(ckv_ref, kpe_ref, kva_ref, wukv_ref, wkn_ref, wkr_ref, cos_ref, sin_ref, s64, s64t, s32t, p128, tile,
             kn_ref, kr_ref, v_ref):
        x = ckv_ref[...]
        ckvn = (x * _rms(x) * kva_ref[...]).astype(BF16)
        kv = _mm_nt(ckvn, wukv_ref[...])
        kn = kv[:, :H * DN]
        kp = kpe_ref[...]
        ss = _dot_hl(kn * kn, s64[...]) + jnp.sum(kp * kp, axis=-1, keepdims=True)
        rh = lax.rsqrt(ss * (1.0 / DH) + EPS)
        kn_ref[...] = (kn * _dot_hl(rh, s64t[...]) * wkn_ref[...]).astype(BF16)
        tk = kp * wkr_ref[...]
        trp = tk * cos_ref[...] + _dot_hl(tk, p128[...]) * sin_ref[...]
        kr_ref[...] = (_dot_hl(trp, tile[...]) * _dot_hl(rh, s32t[...])).astype(BF16)
        v_ref[...] = kv[:, H * DN:].astype(BF16)

    row = lambda cols: pl.BlockSpec((tm, cols), lambda t: (t, 0))
    tab = _kv_tab_spec(tm, tpe, n_lat)
    r = n_tiles * tm
    return _pcall(
        body, name="kv_prep_fwd", grid=(n_tiles,),
        out_shape=(_sds((r, H * DN), BF16), _sds((r, H * DR), BF16), _sds((r, H * DV), BF16)),
        in_specs=[row(KVL), row(LANE), _const((1, KVL)), _const((H * (DN + DV), KVL)), _const((1, H * DN)),
                  _const((1, LANE)), tab, tab, _const((H * DN, LANE)), _const((LANE, H * DN)), _const((LANE, H * DR)),
                  _const((LANE, LANE)), _const((LANE, H * DR))],
        out_specs=(row(H * DN), row(H * DR), row(H * DV)))(
            ckv, kpe, kva_w, wukv, wkn, wkr, cosk, sink, cs["seg64"], cs["seg64t"], cs["seg32t"], cs["p128"],
            cs["tile"])


def _kv_prep_bwd(dkn, dkr, dvv, ckv, kpe, kva_w, wukv, wkn, wkr, cosk, sink, cs, *, tm, n_tiles, tpe, n_lat):
    def body(dkn_ref, dkr_ref, dv_ref, ckv_ref, kpe_ref, kva_ref, wukv_ref, wkn_ref, wkr_ref, cos_ref, sin_ref,
             s64, s32, s64t, s32t, p128, p128t, tile, tilet,
             dckv_ref, dkpe_ref, dwukv_ref, dkva_ref, dwkn_ref, dwkr_ref):
        t = pl.program_id(0)
        x = ckv_ref[...]
        ra = _rms(x)
        xh = x * ra
        kva = kva_ref[...]
        ckvn = (xh * kva).astype(BF16)
        wukv_v = wukv_ref[...]
        kv = _mm_nt(ckvn, wukv_v)
        kn = kv[:, :H * DN]
        kp = kpe_ref[...]
        ss = _dot_hl(kn * kn, s64[...]) + jnp.sum(kp * kp, axis=-1, keepdims=True)
        rh = lax.rsqrt(ss * (1.0 / DH) + EPS)
        r64, r32 = _dot_hl(rh, s64t[...]), _dot_hl(rh, s32t[...])
        wkr_v, wkn_v = wkr_ref[...], wkn_ref[...]
        cos_v, sin_v = cos_ref[...], sin_ref[...]
        tk = kp * wkr_v
        trp = tk * cos_v + _dot_hl(tk, p128[...]) * sin_v
        trt = _dot_hl(trp, tile[...])
        dkr_v = dkr_ref[...]
        dtr = _dot_hl(dkr_v * r32, tilet[...])
        dr_r = _dot_hl(dkr_v * trt, s32[...])
        dt = dtr * cos_v + _dot_hl(dtr * sin_v, p128t[...])
        yn = kn * r64
        dkn_v = dkn_ref[...]
        dyn = dkn_v * wkn_v
        mean_h = (_dot_hl(dyn * yn, s64[...]) + rh * dr_r) * (1.0 / DH)
        dkn_raw = r64 * (dyn - yn * _dot_hl(mean_h, s64t[...]))
        corr = jnp.sum(rh * rh * mean_h, axis=-1, keepdims=True)
        dkpe_ref[...] = (dt * wkr_v - kp * corr).astype(BF16)
        dkv = jnp.concatenate([dkn_raw, dv_ref[...]], axis=-1).astype(BF16)
        dckvn = _mm(dkv, wukv_v)
        dxh = dckvn * kva
        dckv_ref[...] = (ra * (dxh - xh * jnp.mean(dxh * xh, axis=-1, keepdims=True))).astype(BF16)

        @pl.when(t == 0)
        def _():
            dwukv_ref[...] = jnp.zeros_like(dwukv_ref)
            dkva_ref[...] = jnp.zeros_like(dkva_ref)
            dwkn_ref[...] = jnp.zeros_like(dwkn_ref)
            dwkr_ref[...] = jnp.zeros_like(dwkr_ref)

        dwukv_ref[...] += _mm_tn(dkv, ckvn)
        dkva_ref[...] += _rowsum(dckvn * xh)
        dwkn_ref[...] += _rowsum(dkn_v * yn)
        dwkr_ref[...] += _rowsum(dt * kp)

    row = lambda cols: pl.BlockSpec((tm, cols), lambda t: (t, 0))
    tab = _kv_tab_spec(tm, tpe, n_lat)
    r = n_tiles * tm
    return _pcall(
        body, name="kv_prep_bwd", grid=(n_tiles,),
        out_shape=(_sds((r, KVL), BF16), _sds((r, LANE), BF16), _sds((H * (DN + DV), KVL), F32), _sds((1, KVL), F32),
                   _sds((1, H * DN), F32), _sds((1, LANE), F32)),
        in_specs=[row(H * DN), row(H * DR), row(H * DV), row(KVL), row(LANE), _const((1, KVL)),
                  _const((H * (DN + DV), KVL)), _const((1, H * DN)), _const((1, LANE)), tab, tab,
                  _const((H * DN, LANE)), _const((H * DR, LANE)), _const((LANE, H * DN)), _const((LANE, H * DR)),
                  _const((LANE, LANE)), _const((LANE, LANE)), _const((LANE, H * DR)), _const((H * DR, LANE))],
        out_specs=(row(KVL), row(LANE), _const((H * (DN + DV), KVL)), _const((1, KVL)), _const((1, H * DN)),
                   _const((1, LANE))))(
            dkn, dkr, dvv, ckv, kpe, kva_w, wukv, wkn, wkr, cosk, sink, cs["seg64"], cs["seg32"], cs["seg64t"],
            cs["seg32t"], cs["p128"], cs["p128t"], cs["tile"], cs["tilet"])


_SCALE = DH ** -0.5


def _attn_fwd(q, k, v, *, tq):
    b, h, s, _ = q.shape
    sk = k.shape[2]

    def body(q_ref, k_ref, v_ref, o_ref):
        sc = _mm_nt(q_ref[0, 0], k_ref[0, 0]) * _SCALE
        e = jnp.exp(sc - jnp.max(sc, axis=-1, keepdims=True))
        l = jnp.sum(e, axis=-1, keepdims=True)
        o_ref[0, 0] = (_mm(e.astype(BF16), v_ref[0, 0]) / l).astype(BF16)

    return _pcall(
        body, name="attn_fwd", grid=(b, h, s // tq), out_shape=_sds((b, h, s, DV), BF16),
        in_specs=[pl.BlockSpec((1, 1, tq, LANE), lambda i, j, t: (i, j, t, 0)),
                  pl.BlockSpec((1, 1, sk, LANE), lambda i, j, t: (i, j, 0, 0)),
                  pl.BlockSpec((1, 1, sk, DV), lambda i, j, t: (i, j, 0, 0))],
        out_specs=pl.BlockSpec((1, 1, tq, DV), lambda i, j, t: (i, j, t, 0)), vmem_mb=48)(q, k, v)


def _attn_bwd(q, qt, k, v, o, do, dot, *, tq):
    b, h, s, _ = q.shape
    sk = k.shape[2]

    def body(q_ref, qt_ref, k_ref, v_ref, o_ref, do_ref, dot_ref, dq_ref, dkt_ref, dvt_ref):
        t = pl.program_id(2)
        kk = k_ref[0, 0]
        sc = _mm_nt(q_ref[0, 0], kk) * _SCALE
        e = jnp.exp(sc - jnp.max(sc, axis=-1, keepdims=True))
        p = e / jnp.sum(e, axis=-1, keepdims=True)
        dov = do_ref[0, 0]
        delta = jnp.sum(dov.astype(F32) * o_ref[0, 0].astype(F32), axis=-1, keepdims=True)
        dp = _mm_nt(dov, v_ref[0, 0])
        ds = (p * (dp - delta) * _SCALE).astype(BF16)
        dq_ref[0, 0] = _mm(ds, kk)

        @pl.when(t == 0)
        def _():
            dkt_ref[...] = jnp.zeros_like(dkt_ref)
            dvt_ref[...] = jnp.zeros_like(dvt_ref)

        dkt_ref[0, 0] += _mm(qt_ref[0, 0], ds)
        dvt_ref[0, 0] += _mm(dot_ref[0, 0], p.astype(BF16))

    qs = lambda w: pl.BlockSpec((1, 1, tq, w), lambda i, j, t: (i, j, t, 0))
    qts = lambda w: pl.BlockSpec((1, 1, w, tq), lambda i, j, t: (i, j, 0, t))
    ks = lambda w: pl.BlockSpec((1, 1, sk, w), lambda i, j, t: (i, j, 0, 0))
    kts = lambda w: pl.BlockSpec((1, 1, w, sk), lambda i, j, t: (i, j, 0, 0))
    return _pcall(
        body, name="attn_bwd", grid=(b, h, s // tq),
        out_shape=(_sds((b, h, s, LANE), F32), _sds((b, h, LANE, sk), F32), _sds((b, h, DV, sk), F32)),
        in_specs=[qs(LANE), qts(LANE), ks(LANE), ks(DV), qs(DV), qs(DV), qts(DV)],
        out_specs=(qs(LANE), kts(LANE), kts(DV)), vmem_mb=56)(q, qt, k, v, o, do, dot)


def _gating(vn, ws_ref, bias_ref, s_scr, tm):
    lane = lax.broadcasted_iota(jnp.int32, (CH, LANE), 1)
    for c in range(tm // CH):
        rs = slice(c * CH, (c + 1) * CH)
        for j in range(G // 2):
            ls = slice(j * LANE, (j + 1) * LANE)
            vp = vn[rs, ls]
            s_scr[rs, ls] = jnp.where(lane < GD, _mm(ws_ref[2 * j], vp), _mm(ws_ref[2 * j + 1], vp)) + bias_ref[:, ls]


def _mix_fwd(u, v, attn, x1, gate, wv, ws, bias, wout, cs, *, tm, n_lat, tpe):
    nrows = gate.shape[0]

    def body(u_ref, v_ref, attn_ref, x_ref, gate_ref, wv_ref, ws_ref, bias_ref, wout_ref, s64, s64t,
             x2_ref, mix_ref, s_scr):
        vg = _gelu(v_ref[...])
        rg = lax.rsqrt(_dot_hl(vg * vg, s64[...]) * (1.0 / GD) + EPS)
        vn = (vg * _dot_hl(rg, s64t[...]) * wv_ref[...]).astype(BF16)
        _gating(vn, ws_ref, bias_ref, s_scr, tm)
        sg = (_gelu(u_ref[...]) * s_scr[...]).astype(BF16)
        mix = _mm(attn_ref[...], wout_ref[0:H * DV, :]) + _mm(sg, wout_ref[H * DV:, :])
        mix_ref[...] = mix.astype(BF16)
        x2_ref[...] = x_ref[...] + gate_ref[0] * mix

    row = lambda cols: pl.BlockSpec((tm, cols), lambda t: (t, 0))
    r = n_lat * tm
    return _pcall(
        body, name="mix_fwd", grid=(n_lat,),
        out_shape=(_sds((r, D), F32), _sds((r, D), BF16)),
        in_specs=[row(G * GD), row(G * GD), row(H * DV), row(D), _mod_spec(1, tpe, nrows), _const((1, G * GD)),
                  _const((G, CH, CH)), _const((CH, G * GD)), _const((D, D)), _const((G * GD, LANE)),
                  _const((LANE, G * GD))],
        out_specs=(row(D), row(D)), scratch=[pltpu.VMEM((tm, G * GD), F32)], vmem_mb=40)(
            u, v, attn, x1, gate, wv, ws, bias, wout, cs["seg64"], cs["seg64t"])


def _mix_bwd(dx2, mix, u, v, attn, gate, wv, ws, wst, bias, wout, cs, *, tm, n_lat, tpe):
    nrows = gate.shape[0]

    def body(dx2_ref, mix_ref, u_ref, v_ref, attn_ref, gate_ref, wv_ref, ws_ref, wst_ref, bias_ref, wout_ref, s64, s64t,
             dattn_ref, du_ref, dv_ref, dgate_ref, dwout_ref, dws_ref, dbs_ref, dwv_ref, s_scr, dvn_scr, dbias_scr):
        t = pl.program_id(0)
        dx2 = dx2_ref[...]
        dmix = (dx2 * gate_ref[0]).astype(BF16)
        dcat = _mm_nt(dmix, wout_ref[...])
        dattn_ref[...] = dcat[:, :H * DV].astype(BF16)
        dsg = dcat[:, H * DV:]

        vraw = v_ref[...]
        vg = _gelu(vraw)
        rg = lax.rsqrt(_dot_hl(vg * vg, s64[...]) * (1.0 / GD) + EPS)
        r64 = _dot_hl(rg, s64t[...])
        y = vg * r64
        wv_v = wv_ref[...]
        vn = (y * wv_v).astype(BF16)
        _gating(vn, ws_ref, bias_ref, s_scr, tm)
        uraw = u_ref[...]
        ug = _gelu(uraw)
        s = s_scr[...]
        sg = (ug * s).astype(BF16)
        du_ref[...] = (dsg * s * _gelu_grad(uraw)).astype(BF16)
        ds = dsg * ug

        @pl.when(t == 0)
        def _():
            dwout_ref[...] = jnp.zeros_like(dwout_ref)
            dws_ref[...] = jnp.zeros_like(dws_ref)
            dwv_ref[...] = jnp.zeros_like(dwv_ref)
            dbias_scr[...] = jnp.zeros_like(dbias_scr)

        @pl.when(t % tpe == 0)
        def _():
            dgate_ref[...] = jnp.zeros_like(dgate_ref)

        dgate_ref[0] += _rowsum(dx2 * mix_ref[...].astype(F32))
        dwout_ref[...] += _mm_tn(jnp.concatenate([attn_ref[...], sg], axis=-1), dmix)

        lane = lax.broadcasted_iota(jnp.int32, (CH, LANE), 1)
        for c in range(tm // CH):
            rs = slice(c * CH, (c + 1) * CH)
            dbias_scr[...] += ds[rs, :]
            for j in range(G // 2):
                ls = slice(j * LANE, (j + 1) * LANE)
                dsp32 = ds[rs, ls]
                dsp = dsp32.astype(BF16)
                vp = vn[rs, ls]
                dvn_scr[rs, ls] = jnp.where(lane < GD, _mm(wst_ref[2 * j], dsp), _mm(wst_ref[2 * j + 1], dsp))
                dws_ref[2 * j] += _mm_nt(jnp.where(lane < GD, dsp32, 0.0).astype(BF16), vp)
                dws_ref[2 * j + 1] += _mm_nt(jnp.where(lane < GD, 0.0, dsp32).astype(BF16), vp)

        dvn = dvn_scr[...]
        dwv_ref[...] += _rowsum(dvn * y)
        dy = dvn * wv_v
        mean_g = _dot_hl(dy * y, s64[...]) * (1.0 / GD)
        dvg = r64 * (dy - y * _dot_hl(mean_g, s64t[...]))
        dv_ref[...] = (dvg * _gelu_grad(vraw)).astype(BF16)

        @pl.when(t == n_lat - 1)
        def _():
            dbs_ref[...] = _dot_hl(dbias_scr[...], s64[...])

    row = lambda cols: pl.BlockSpec((tm, cols), lambda t: (t, 0))
    r = n_lat * tm
    return _pcall(
        body, name="mix_bwd", grid=(n_lat,),
        out_shape=(_sds((r, H * DV), BF16), _sds((r, G * GD), BF16), _sds((r, G * GD), BF16), _sds((nrows, 1, D), F32),
                   _sds((D, D), F32), _sds((G, CH, CH), F32), _sds((CH, LANE), F32), _sds((1, G * GD), F32)),
        in_specs=[row(D), row(D), row(G * GD), row(G * GD), row(H * DV), _mod_spec(1, tpe, nrows), _const((1, G * GD)),
                  _const((G, CH, CH)), _const((G, CH, CH)), _const((CH, G * GD)), _const((D, D)),
                  _const((G * GD, LANE)), _const((LANE, G * GD))],
        out_specs=(row(H * DV), row(G * GD), row(G * GD), _mod_spec(1, tpe, nrows), _const((D, D)),
                   _const((G, CH, CH)), _const((CH, LANE)), _const((1, G * GD))),
        scratch=[pltpu.VMEM((tm, G * GD), F32), pltpu.VMEM((tm, G * GD), F32), pltpu.VMEM((CH, G * GD), F32)],
        vmem_mb=48)(dx2, mix, u, v, attn, gate, wv, ws, wst, bias, wout, cs["seg64"], cs["seg64t"])


def _loss_head(y, target, *, tm, n_lat):
    def body(y_ref, t_ref, dy_ref, ls_ref):
        @pl.when(pl.program_id(0) == 0)
        def _():
            ls_ref[...] = jnp.zeros_like(ls_ref)

        d = y_ref[...] - t_ref[...]
        dy_ref[...] = d * (1.0 / D)
        ls_ref[...] += jnp.sum(d * d)

    row = pl.BlockSpec((tm, D), lambda t: (t, 0))
    return _pcall(body, name="loss_head", grid=(n_lat,),
                  out_shape=(_sds((n_lat * tm, D), F32), _sds((8, LANE), F32)),
                  in_specs=[row, row], out_specs=(row, _const((8, LANE))))(y, target)


def _adamw_math(w, g, m, v):
    m2 = ADAM_B1 * m + (1.0 - ADAM_B1) * g
    v2 = ADAM_B2 * v + (1.0 - ADAM_B2) * (g * g)
    m_hat = m2 / (1.0 - ADAM_B1 ** ADAM_STEP)
    v_hat = v2 / (1.0 - ADAM_B2 ** ADAM_STEP)
    delta = -ADAM_LR * (m_hat / (jnp.sqrt(v_hat) + ADAM_EPS) + ADAM_WD * w)
    return delta, m2, v2


def _row_tile(r, c):
    best = r
    for tr in range(8, r, 8):
        if r % tr == 0 and tr * c * 4 <= MIB:
            best = tr
    return best


def _adamw(w, g, m, v, name):
    r, c = w.shape
    tr = _row_tile(r, c)

    def body(w_ref, g_ref, m_ref, v_ref, d_ref, mo_ref, vo_ref):
        d_ref[...], mo_ref[...], vo_ref[...] = _adamw_math(w_ref[...], g_ref[...], m_ref[...], v_ref[...])

    blk = pl.BlockSpec((tr, c), lambda t: (t, 0))
    return _pcall(body, name=name, grid=(r // tr,), out_shape=(_sds((r, c), F32),) * 3,
                  in_specs=[blk] * 4, out_specs=(blk,) * 3)(w, g, m, v)


def _adamw_small(params, cctx_row):
    n = len(params)

    def body(*refs):
        ins, outs = refs[:4 * n], refs[4 * n:]
        for i in range(n):
            w, g, m, v = (ins[4 * i + k][...] for k in range(4))
            if i == 0:
                sig = _sigmoid(w)
                g = g * (sig * (1.0 + w * (1.0 - sig)))
            d, m2, v2 = _adamw_math(w, g, m, v)
            outs[4 * i][...] = g
            outs[4 * i + 1][...] = d
            outs[4 * i + 2][...] = m2
            outs[4 * i + 3][...] = v2

    flat = [a for p in params for a in p]
    out_shape = tuple(_sds(p[0].shape, F32) for p in params for _ in range(4))
    res = _pcall(body, name="adamw_small", out_shape=out_shape,
                 in_specs=[pl.BlockSpec(memory_space=pltpu.VMEM)] * (4 * n),
                 out_specs=(pl.BlockSpec(memory_space=pltpu.VMEM),) * (4 * n))(*flat)
    return [res[4 * i:4 * i + 4] for i in range(n)]


def _rope_tables(s):
    rows = jnp.repeat(jnp.arange(s // GRID_W, dtype=F32), GRID_W)
    cols = jnp.tile(jnp.arange(GRID_W, dtype=F32), s // GRID_W)
    half = DR // 2
    inv = ROPE_BASE ** (-jnp.arange(0, half, 2, dtype=F32) / half)
    ang_r = rows[:, None] * inv
    ang_c = cols[:, None] * inv
    ang = jnp.concatenate([ang_r, ang_r, ang_c, ang_c], axis=-1)
    return jnp.cos(ang), jnp.sin(ang)


def _heads(a, b, n, w):
    return a.reshape(b, n, H, w)


def kernel(x, c, ctx, c_ctx, w_ada, b_ada, norm1_w, ffn1_w1, ffn1_w3, ffn1_w2, norm2_w, w_in, q_a_norm_w, w_uq, kv_a_norm_w, w_ukv, q_norm_w, k_norm_w, v_norm_w, w_s, b_s, w_out, norm3_w, ffn2_w1, ffn2_w3, ffn2_w2, loss_target, m_c_ctx, m_w_ada, m_b_ada, m_norm1_w, m_ffn1_w1, m_ffn1_w3, m_ffn1_w2, m_norm2_w, m_w_in, m_q_a_norm_w, m_w_uq, m_kv_a_norm_w, m_w_ukv, m_q_norm_w, m_k_norm_w, m_v_norm_w, m_w_s, m_b_s, m_w_out, m_norm3_w, m_ffn2_w1, m_ffn2_w3, m_ffn2_w2, v_c_ctx, v_w_ada, v_b_ada, v_norm1_w, v_ffn1_w1, v_ffn1_w3, v_ffn1_w2, v_norm2_w, v_w_in, v_q_a_norm_w, v_w_uq, v_kv_a_norm_w, v_w_ukv, v_q_norm_w, v_k_norm_w, v_v_norm_w, v_w_s, v_b_s, v_w_out, v_norm3_w, v_ffn2_w1, v_ffn2_w3, v_ffn2_w2):
    nb, s, _ = x.shape
    nc = ctx.shape[1]
    sk = s + nc
    tm = 256 if nc % 256 == 0 else 128
    tpe = s // tm
    n_lat = nb * tpe
    n_all = n_lat + nb * nc // tm
    r_lat = nb * s
    me = 4 * lax.axis_index("x") + 2 * lax.axis_index("y") + lax.axis_index("c")
    cs = _consts()
    ncol = w_ada.shape[2]
    fsh = ffn1_w1.shape[2]
    assert nb + 1 <= 8 and NDEV * fsh == FF and NDEV * ncol == NMOD * D

    a_loc = jnp.concatenate([c, c_ctx[None, :], jnp.zeros((7 - nb, D), F32)], axis=0)
    a_raw = _all_gather(a_loc, "gather_c").reshape(NDEV * 8, D)
    mod_cols = _ada_fwd(a_raw, w_ada[0], lax.dynamic_slice_in_dim(b_ada, me * ncol, ncol, axis=1))
    mod_all = _all_gather(mod_cols, "gather_mod")
    mod_mine = lax.dynamic_slice_in_dim(mod_all, 8 * me, 8, axis=1)
    modtab = mod_mine.transpose(1, 0, 2).reshape(8, NMOD, D)[:nb + 1]

    def t16(a):
        return a.T.astype(BF16)

    wpack = jnp.concatenate([
        t16(ffn1_w1[0]), t16(ffn1_w3[0]), ffn1_w2[0].astype(BF16),
        t16(ffn2_w1[0]), t16(ffn2_w3[0]), ffn2_w2[0].astype(BF16),
        t16(w_in[0]), jnp.zeros((12, D), BF16),
        w_out[0].astype(BF16),
        t16(w_uq[0]).reshape(24, D), jnp.zeros((8, D), BF16),
        t16(w_ukv[0]).reshape(16, D)], axis=0)
    wall = _all_gather(wpack, "gather_weights")

    def ffn_w(i):
        return wall[:, i * fsh:(i + 1) * fsh].reshape(FF, D)

    w1t_1, w3t_1, w2_1, w1t_2, w3t_2, w2_2 = (ffn_w(i) for i in range(6))
    o0 = 6 * fsh
    wint = wall[:, o0:o0 + 180].reshape(IN_COLS, D)
    wint = jnp.concatenate([wint[0:128], wint[160:416], wint[416:928], wint[928:1440], wint[128:160],
                            jnp.zeros((WIN_ROWS - IN_COLS, D), BF16)], axis=0)
    wout = wall[:, o0 + 192:o0 + 320].reshape(D, D)
    wuqt = wall[:, o0 + 320:o0 + 344].reshape(H, DH, QL)
    wuqt = jnp.concatenate([wuqt[:, :DN].reshape(H * DN, QL), wuqt[:, DN:].reshape(H * DR, QL)], axis=0)
    wukvt = wall[:, o0 + 352:o0 + 368].reshape(H, DN + DV, KVL)
    wukvt = jnp.concatenate([wukvt[:, :DN].reshape(H * DN, KVL), wukvt[:, DN:].reshape(H * DV, KVL)], axis=0)

    wqn = jnp.tile(q_norm_w[:, :DN], (1, H))
    wqr = jnp.tile(q_norm_w[:, DN:], (1, H))
    wkn = jnp.tile(k_norm_w[:, :DN], (1, H))
    wkr = jnp.concatenate([k_norm_w[:, DN:], jnp.zeros((1, LANE - DR), F32)], axis=1)
    wv = v_norm_w.reshape(1, G * GD)
    ws16 = w_s[0].astype(BF16)
    wst16 = w_s[0].transpose(0, 2, 1).astype(BF16)
    bias = jnp.repeat(b_s[0].T, GD, axis=1)
    cos, sin = _rope_tables(s)
    cos_q, sin_q = jnp.tile(cos, (1, H)), jnp.tile(sin, (1, H))
    pad = jnp.zeros((s, LANE - DR), F32)
    cos_k = jnp.concatenate([jnp.concatenate([cos, pad], axis=1), jnp.ones((tm, LANE), F32)], axis=0)
    sin_k = jnp.concatenate([jnp.concatenate([sin, pad], axis=1), jnp.zeros((tm, LANE), F32)], axis=0)

    xa = jnp.concatenate([x.reshape(r_lat, D), ctx.reshape(nb * nc, D)], axis=0)
    x1, a1, b1, o1 = _ffn_fwd(xa, modtab[:, 0:3], norm1_w, w1t_1, w3t_1, w2_1,
                              tm=tm, n_tiles=n_all, tpe=tpe, name="ffn1_fwd")
    ckv, qp, u_raw, v_raw, kpe = _proj_fwd(x1, modtab[:, 3:5], norm2_w, wint, tm=tm, n_tiles=n_all, tpe=tpe)
    q_n, q_r = _q_prep_fwd(qp, q_a_norm_w, wuqt, wqn, wqr, cos_q, sin_q, cs, tm=tm, n_lat=n_lat, tpe=tpe)
    k_n, k_r, vv = _kv_prep_fwd(ckv, kpe, kv_a_norm_w, wukvt, wkn, wkr, cos_k, sin_k, cs,
                                tm=tm, n_tiles=n_all, tpe=tpe, n_lat=n_lat)

    def seq(a, w):
        return jnp.concatenate([a[:r_lat].reshape(nb, s, H, w), a[r_lat:].reshape(nb, nc, H, w)], axis=1)

    qh = jnp.concatenate([q_n.reshape(nb, s, H, DN), q_r.reshape(nb, s, H, DR),
                          jnp.zeros((nb, s, H, LANE - DH), BF16)], axis=-1).transpose(0, 2, 1, 3)
    kh = jnp.concatenate([seq(k_n, DN), seq(k_r, DR), jnp.zeros((nb, sk, H, LANE - DH), BF16)],
                         axis=-1).transpose(0, 2, 1, 3)
    vh = seq(vv, DV).transpose(0, 2, 1, 3)
    oh = _attn_fwd(qh, kh, vh, tq=tm)
    attn = oh.transpose(0, 2, 1, 3).reshape(r_lat, H * DV)
    x2, mix = _mix_fwd(u_raw, v_raw, attn, x1, modtab[:nb, 5:6], wv, ws16, bias, wout, cs,
                       tm=tm, n_lat=n_lat, tpe=tpe)
    x3, a2, b2, o2 = _ffn_fwd(x2, modtab[:nb, 6:9], norm3_w, w1t_2, w3t_2, w2_2,
                              tm=tm, n_tiles=n_lat, tpe=tpe, name="ffn2_fwd")
    dy, lsum = _loss_head(x3, loss_target.reshape(r_lat, D), tm=tm, n_lat=n_lat)
    loss = lax.psum(lsum[0, 0] * (0.5 / D), ("x", "y", "c"))

    tr = 2 * tm if n_lat % 2 == 0 and n_all % 2 == 0 else tm
    dx2, da2, db2, g2, do2, h2, dmod678, dnorm3 = _ffn_bwd_dx(
        dy, x2, a2, b2, o2, modtab[:nb, 6:9], norm3_w, w1t_2, w3t_2, w2_2,
        tm=tm, n_tiles=n_lat, tpe=tpe, n_lat=n_lat, name="ffn2_bwd_dx")
    gw1t_2, gw3t_2, gw2_2 = _ffn_bwd_dw(h2, do2, da2, db2, g2, tr=tr, name="ffn2_bwd_dw")

    dattn, du, dv, dgate5, dwout, dws, dbs, dwv = _mix_bwd(
        dx2, mix, u_raw, v_raw, attn, modtab[:nb, 5:6], wv, ws16, wst16, bias, wout, cs, tm=tm, n_lat=n_lat, tpe=tpe)

    doh = dattn.reshape(nb, s, H, DV).transpose(0, 2, 1, 3)
    dqh, dkth, dvth = _attn_bwd(qh, qh.transpose(0, 1, 3, 2), kh, vh, oh, doh, doh.transpose(0, 1, 3, 2), tq=tm)
    dq_tok = dqh.transpose(0, 2, 1, 3)
    dqn = dq_tok[..., :DN].reshape(r_lat, H * DN)
    dqr = dq_tok[..., DN:DH].reshape(r_lat, H * DR)
    dk_tok = dkth.transpose(0, 3, 1, 2)
    dv_tok = dvth.transpose(0, 3, 1, 2)

    def unseq(a, w):
        return jnp.concatenate([a[:, :s].reshape(r_lat, H * w), a[:, s:].reshape(nb * nc, H * w)], axis=0)

    dkn = unseq(dk_tok[..., :DN], DN)
    dkr = unseq(dk_tok[..., DN:DH], DR)
    dvv = unseq(dv_tok, DV)

    dqp, dwuq, dqa, dwqn, dwqr = _q_prep_bwd(dqn, dqr, qp, q_a_norm_w, wuqt, wqn, wqr, cos_q, sin_q, cs,
                                             tm=tm, n_lat=n_lat, tpe=tpe)
    dckv, dkpe, dwukv, dkva, dwkn, dwkr = _kv_prep_bwd(dkn, dkr, dvv, ckv, kpe, kv_a_norm_w, wukvt, wkn, wkr,
                                                       cos_k, sin_k, cs, tm=tm, n_tiles=n_all, tpe=tpe, n_lat=n_lat)
    dx1, dwin, dmod34, dnorm2 = _proj_bwd(dckv, dkpe, dqp, du, dv, dx2, x1, modtab[:, 3:5], norm2_w, wint,
                                          tm=tm, n_tiles=n_all, tpe=tpe, n_lat=n_lat)
    dxa, da1, db1, g1, do1, h1, dmod012, dnorm1 = _ffn_bwd_dx(
        dx1, xa, a1, b1, o1, modtab[:, 0:3], norm1_w, w1t_1, w3t_1, w2_1,
        tm=tm, n_tiles=n_all, tpe=tpe, n_lat=n_lat, name="ffn1_bwd_dx")
    gw1t_1, gw3t_1, gw2_1 = _ffn_bwd_dw(h1, do1, da1, db1, g1, tr=tr, name="ffn1_bwd_dw")
    grad_x = dxa[:r_lat].reshape(nb, s, D)

    zrow = jnp.zeros((1, D), F32)
    g_lat = jnp.concatenate([dmod012[:nb, 0], dmod012[:nb, 1], dmod012[:nb, 2], dmod34[:nb, 0], dmod34[:nb, 1],
                             dgate5[:, 0], dmod678[:, 0], dmod678[:, 1], dmod678[:, 2]], axis=1)
    g_ctx = jnp.concatenate([dmod012[nb:, 0], dmod012[nb:, 1], dmod012[nb:, 2], dmod34[nb:, 0], dmod34[nb:, 1],
                             zrow, zrow, zrow, zrow], axis=1)
    g_loc = jnp.concatenate([g_lat, g_ctx, jnp.zeros((7 - nb, NMOD * D), F32)], axis=0)
    g_all = _all_gather(g_loc, "gather_gmod").reshape(NDEV * 8, NMOD * D)
    g_cols = lax.dynamic_slice_in_dim(g_all, me * ncol, ncol, axis=1)
    g_w_ada, pc_ctx, g_b_ada = _ada_bwd(a_raw, c_ctx.reshape(D, 1), g_all, g_cols, w_ada[0], nb)

    def blocks(a):
        return a.reshape(NDEV, a.shape[0] // NDEV, D)

    dwin_o = jnp.concatenate([dwin[0:128], dwin[1408:1440], dwin[128:384], dwin[384:896], dwin[896:1408]], axis=0)
    dwuq_o = jnp.concatenate([dwuq[:H * DN].reshape(H, DN, QL), dwuq[H * DN:].reshape(H, DR, QL)], axis=1)
    dwukv_o = jnp.concatenate([dwukv[:H * DN].reshape(H, DN, KVL), dwukv[H * DN:].reshape(H, DV, KVL)], axis=1)
    gpack = jnp.concatenate([
        blocks(gw1t_1), blocks(gw3t_1), blocks(gw2_1), blocks(gw1t_2), blocks(gw3t_2), blocks(gw2_2),
        blocks(dwin_o).astype(BF16), jnp.zeros((NDEV, 12, D), BF16),
        blocks(dwout).astype(BF16),
        dwuq_o.reshape(NDEV, 24, D).astype(BF16), jnp.zeros((NDEV, 8, D), BF16),
        dwukv_o.reshape(NDEV, 16, D).astype(BF16)], axis=1)
    gsum = _sum8(_all_to_all(gpack, "scatter_grads"), 80, "sum_grads")

    def gsl(lo, n):
        return gsum[lo:lo + n]

    g_big = {
        "ffn1_w1": gsl(0, fsh).T, "ffn1_w3": gsl(fsh, fsh).T, "ffn1_w2": gsl(2 * fsh, fsh),
        "ffn2_w1": gsl(3 * fsh, fsh).T, "ffn2_w3": gsl(4 * fsh, fsh).T, "ffn2_w2": gsl(5 * fsh, fsh),
        "w_in": gsl(o0, 180).T, "w_out": gsl(o0 + 192, 128),
        "w_uq": gsl(o0 + 320, 24).reshape(DH, QL).T, "w_ukv": gsl(o0 + 352, 16).reshape(DN + DV, KVL).T,
        "w_ada": g_w_ada,
    }

    def prow(a):
        a = a.reshape(1, -1)
        return jnp.concatenate([a, jnp.zeros((1, D - a.shape[1]), F32)], axis=1)

    g_qn = jnp.concatenate([dwqn.reshape(H, DN).sum(0), dwqr.reshape(H, DR).sum(0)])
    g_kn = jnp.concatenate([dwkn.reshape(H, DN).sum(0), dwkr[0, :DR]])
    spack = jnp.concatenate([
        dnorm1, dnorm2, dnorm3, prow(dqa), prow(dkva), prow(g_qn), prow(g_kn), prow(dwv),
        prow(dbs[:, :G].T), prow(pc_ctx), jnp.zeros((6, D), F32), dws.reshape(CH, D)], axis=0)
    ssum = _sum8(_all_gather(spack, "gather_small"), 144, "sum_small")

    big_in = {
        "w_ada": (w_ada, m_w_ada, v_w_ada), "ffn1_w1": (ffn1_w1, m_ffn1_w1, v_ffn1_w1),
        "ffn1_w3": (ffn1_w3, m_ffn1_w3, v_ffn1_w3), "ffn1_w2": (ffn1_w2, m_ffn1_w2, v_ffn1_w2),
        "w_in": (w_in, m_w_in, v_w_in), "w_uq": (w_uq, m_w_uq, v_w_uq), "w_ukv": (w_ukv, m_w_ukv, v_w_ukv),
        "w_out": (w_out, m_w_out, v_w_out), "ffn2_w1": (ffn2_w1, m_ffn2_w1, v_ffn2_w1),
        "ffn2_w3": (ffn2_w3, m_ffn2_w3, v_ffn2_w3), "ffn2_w2": (ffn2_w2, m_ffn2_w2, v_ffn2_w2),
    }
    res = {}
    for nm, (w, m, v) in big_in.items():
        g = g_big[nm]
        d_, m_, v_ = _adamw(w[0], g, m[0], v[0], "adamw_" + nm)
        res[nm] = tuple(a[None] for a in (g, d_, m_, v_))

    small_in = [
        ("c_ctx", c_ctx, m_c_ctx, v_c_ctx, ssum[9:10], (1, D)),
        ("b_ada", b_ada, m_b_ada, v_b_ada, g_b_ada, (1, NMOD * D)),
        ("norm1_w", norm1_w, m_norm1_w, v_norm1_w, ssum[0:1], (1, D)),
        ("norm2_w", norm2_w, m_norm2_w, v_norm2_w, ssum[1:2], (1, D)),
        ("norm3_w", norm3_w, m_norm3_w, v_norm3_w, ssum[2:3], (1, D)),
        ("q_a_norm_w", q_a_norm_w, m_q_a_norm_w, v_q_a_norm_w, ssum[3:4, :QL], (1, QL)),
        ("kv_a_norm_w", kv_a_norm_w, m_kv_a_norm_w, v_kv_a_norm_w, ssum[4:5, :KVL], (1, KVL)),
        ("q_norm_w", q_norm_w, m_q_norm_w, v_q_norm_w, ssum[5:6, :DH], (1, DH)),
        ("k_norm_w", k_norm_w, m_k_norm_w, v_k_norm_w, ssum[6:7, :DH], (1, DH)),
        ("v_norm_w", v_norm_w, m_v_norm_w, v_v_norm_w, ssum[7:8, :G * GD], (G, GD)),
        ("b_s", b_s, m_b_s, v_b_s, ssum[8:9], (G, CH)),
        ("w_s", w_s, m_w_s, v_w_s, ssum[16:144], (G * CH, CH)),
    ]
    small_out = _adamw_small(
        [(w.reshape(sh), g.reshape(sh), m.reshape(sh), v.reshape(sh)) for _, w, m, v, g, sh in small_in], None)
    for (nm, w, *_), outs in zip(small_in, small_out):
        res[nm] = tuple(a.reshape(w.shape) for a in outs)

    order = ["c_ctx", "w_ada", "b_ada", "norm1_w", "ffn1_w1", "ffn1_w3", "ffn1_w2", "norm2_w", "w_in", "q_a_norm_w",
             "w_uq", "kv_a_norm_w", "w_ukv", "q_norm_w", "k_norm_w", "v_norm_w", "w_s", "b_s", "w_out", "norm3_w",
             "ffn2_w1", "ffn2_w3", "ffn2_w2"]
    return (loss, grad_x, *[res[n][0] for n in order], *[res[n][1] for n in order],
            *[res[n][2] for n in order], *[res[n][3] for n in order])
```

```python
import numpy as np
import jax
import jax.numpy as jnp
from jax import lax
from jax.experimental import pallas as pl
from jax.experimental.pallas import tpu as pltpu

F32 = jnp.float32
BF16 = jnp.bfloat16

D = 1024
FF = 2816
FC = 256
H = 8
DN, DR, DV = 64, 32, 64
DH = DN + DR
QL, KVL = 256, 128
G, GD, CH = 8, 64, 128
NMOD = 9
EPS = 1e-6
GRID_W = 64
ROPE_BASE = 10000.0
NDEV = 8
LANE = 128
HP = H * LANE
IN_COLS = 1440
WIN_ROWS = 1536
KPE_LO = 1408 + DN
NFFN_W = 6
MIB = 1 << 20

ADAM_LR, ADAM_B1, ADAM_B2, ADAM_EPS, ADAM_WD, ADAM_STEP = 0.001, 0.9, 0.999, 1e-08, 0.01, 10

MESH = pl.DeviceIdType.MESH
ANY = pl.BlockSpec(memory_space=pl.ANY)
VMEM = pl.BlockSpec(memory_space=pltpu.VMEM)


def _mm(a, b):
    return jnp.dot(a, b, preferred_element_type=F32)


def _mm_nt(a, b):
    return lax.dot_general(a, b, (((1,), (1,)), ((), ())), preferred_element_type=F32)


def _mm_tn(a, b):
    return lax.dot_general(a, b, (((0,), (0,)), ((), ())), preferred_element_type=F32)


def _dot_hl(x, m):
    hi = x.astype(BF16)
    lo = (x - hi.astype(F32)).astype(BF16)
    return _mm(hi, m) + _mm(lo, m)


def _sigmoid(a):
    return 1.0 / (1.0 + jnp.exp(-a))


_G0 = 0.7978845608028654
_G1 = 0.044715


def _gelu(x):
    return 0.5 * x * (1.0 + jnp.tanh(_G0 * (x + _G1 * (x * x * x))))


def _gelu_grad(x):
    th = jnp.tanh(_G0 * (x + _G1 * (x * x * x)))
    return 0.5 * (1.0 + th) + 0.5 * x * (1.0 - th * th) * (_G0 * (1.0 + 3.0 * _G1 * x * x))


def _rowsum(y):
    return jnp.sum(y, axis=0, keepdims=True)


def _rms(x):
    return lax.rsqrt(jnp.mean(x * x, axis=-1, keepdims=True) + EPS)


def _pcall(body, *, name, out_shape, in_specs, out_specs, grid=None, scratch=(), vmem_mb=32, aliases=None):
    kw = {}
    if grid is not None:
        kw["grid"] = grid
        sem = ("arbitrary",) * len(grid)
    else:
        sem = None
    if aliases:
        kw["input_output_aliases"] = aliases
    return pl.pallas_call(
        body, name=name, out_shape=out_shape, in_specs=in_specs, out_specs=out_specs,
        scratch_shapes=list(scratch),
        compiler_params=pltpu.CompilerParams(dimension_semantics=sem, vmem_limit_bytes=vmem_mb * MIB),
        **kw)


def _const(shape):
    nd = len(shape)
    return pl.BlockSpec(shape, lambda *_: (0,) * nd)


def _sds(shape, dt):
    return jax.ShapeDtypeStruct(shape, dt)


def _consts():
    seg_h = np.zeros((HP, LANE), np.float32)
    seg_h[np.arange(HP), np.arange(HP) // LANE] = 1.0
    seg_g = np.zeros((G * GD, LANE), np.float32)
    seg_g[np.arange(G * GD), np.arange(G * GD) // GD] = 1.0
    rot = np.zeros((LANE, LANE), np.float32)
    for base in (DN, DN + 16):
        for j in range(8):
            rot[base + j + 8, base + j] = -1.0
            rot[base + j, base + j + 8] = 1.0
    c = dict(seg_h=seg_h, seg_ht=seg_h.T, seg_g=seg_g, seg_gt=seg_g.T, rot=rot, rot_t=rot.T)
    return {k: jnp.asarray(v, BF16) for k, v in c.items()}


def _all_gather(x, name):
    r, c = x.shape

    def body(x_ref, out_ref, send_sems, recv_sems, local_sem):
        mx, my, mc = lax.axis_index("x"), lax.axis_index("y"), lax.axis_index("c")
        me, sibling = (mx, my, mc), (mx, my, 1 - mc)
        chips = [(1 - mx, my), (mx, 1 - my), (1 - mx, 1 - my)]

        def blk(px, py, pc):
            return out_ref.at[4 * px + 2 * py + pc]

        def copy(k, block, to, src=None):
            return pltpu.make_async_remote_copy(
                src_ref=blk(*block) if src is None else src, dst_ref=blk(*block),
                send_sem=send_sems.at[k], recv_sem=recv_sems.at[k], device_id=to, device_id_type=MESH)

        mine = pltpu.make_async_copy(x_ref, blk(*me), local_sem)
        mine.start()
        first = [copy(0, me, sibling, src=x_ref)]
        first += [copy(1 + j, me, (*chip, mc), src=x_ref) for j, chip in enumerate(chips)]
        for cp in first:
            cp.start()
        passed = [copy(4 + j, (*chip, mc), sibling) for j, chip in enumerate(chips)]
        for j, chip in enumerate(chips):
            copy(1 + j, (*chip, mc), me).wait_recv()
            passed[j].start()
        copy(0, sibling, me).wait_recv()
        for j, chip in enumerate(chips):
            copy(4 + j, (*chip, 1 - mc), me).wait_recv()
        for cp in first + passed:
            cp.wait_send()
        mine.wait()

    return pl.pallas_call(
        body, name=name, out_shape=_sds((NDEV, r, c), x.dtype), in_specs=[ANY], out_specs=ANY,
        scratch_shapes=[pltpu.SemaphoreType.DMA((7,)), pltpu.SemaphoreType.DMA((7,)), pltpu.SemaphoreType.DMA(())],
    )(x)


def _scatter_sibling(xs, name):
    n = len(xs)

    def body(*refs):
        x_refs, keep_refs, got_refs = refs[:n], refs[n:2 * n], refs[2 * n:3 * n]
        send_sems, recv_sems, local_sems = refs[3 * n:]
        mx, my, mc = lax.axis_index("x"), lax.axis_index("y"), lax.axis_index("c")
        sibling = (mx, my, 1 - mc)
        local, remote = [], []
        for i in range(n):
            for j in range(4):
                k = 4 * i + j
                local.append(pltpu.make_async_copy(x_refs[i].at[2 * j + mc], keep_refs[i].at[j], local_sems.at[k]))
                remote.append(pltpu.make_async_remote_copy(
                    src_ref=x_refs[i].at[2 * j + 1 - mc], dst_ref=got_refs[i].at[j],
                    send_sem=send_sems.at[k], recv_sem=recv_sems.at[k], device_id=sibling, device_id_type=MESH))
        for cp in remote + local:
            cp.start()
        for cp in remote:
            cp.wait_recv()
        for cp in remote:
            cp.wait_send()
        for cp in local:
            cp.wait()

    shapes = tuple(_sds((4,) + x.shape[1:], x.dtype) for x in xs)
    res = pl.pallas_call(
        body, name=name, out_shape=shapes + shapes, in_specs=[ANY] * n, out_specs=(ANY,) * (2 * n),
        scratch_shapes=[pltpu.SemaphoreType.DMA((4 * n,)), pltpu.SemaphoreType.DMA((4 * n,)),
                        pltpu.SemaphoreType.DMA((4 * n,))],
    )(*xs)
    return res[:n], res[n:]


def _scatter_chips(ps, name):
    n = len(ps)

    def body(*refs):
        p_refs, out_refs = refs[:n], refs[n:2 * n]
        send_sems, recv_sems, local_sems = refs[2 * n:]
        mx, my, mc = lax.axis_index("x"), lax.axis_index("y"), lax.axis_index("c")
        peers = [(1 - mx, my), (mx, 1 - my), (1 - mx, 1 - my)]
        local, sends, recvs = [], [], []
        for i in range(n):
            local.append(pltpu.make_async_copy(p_refs[i].at[2 * mx + my], out_refs[i].at[3], local_sems.at[i]))
            for j, (px, py) in enumerate(peers):
                k = 3 * i + j
                sends.append(pltpu.make_async_remote_copy(
                    src_ref=p_refs[i].at[2 * px + py], dst_ref=out_refs[i].at[j],
                    send_sem=send_sems.at[k], recv_sem=recv_sems.at[k], device_id=(px, py, mc), device_id_type=MESH))
        for cp in sends + local:
            cp.start()
        for cp in sends:
            cp.wait_recv()
        for cp in sends:
            cp.wait_send()
        for cp in local:
            cp.wait()

    shapes = tuple(_sds(p.shape, p.dtype) for p in ps)
    return pl.pallas_call(
        body, name=name, out_shape=shapes, in_specs=[ANY] * n, out_specs=(ANY,) * n,
        scratch_shapes=[pltpu.SemaphoreType.DMA((3 * n,)), pltpu.SemaphoreType.DMA((3 * n,)),
                        pltpu.SemaphoreType.DMA((n,))],
    )(*ps)


def _add_pair(a, b, tr, name):
    _, r, c = a.shape

    def body(a_ref, b_ref, o_ref):
        o_ref[...] = (a_ref[...].astype(F32) + b_ref[...].astype(F32)).astype(o_ref.dtype)

    blk = pl.BlockSpec((4, tr, c), lambda t: (0, t, 0))
    return _pcall(body, name=name, grid=(r // tr,), out_shape=_sds(a.shape, a.dtype),
                  in_specs=[blk, blk], out_specs=blk)(a, b)


def _sum_slots(x, tr, name):
    n, r, c = x.shape

    def body(x_ref, o_ref):
        acc = x_ref[0].astype(F32)
        for s in range(1, n):
            acc = acc + x_ref[s].astype(F32)
        o_ref[...] = acc

    return _pcall(body, name=name, grid=(r // tr,), out_shape=_sds((r, c), F32),
                  in_specs=[pl.BlockSpec((n, tr, c), lambda t: (0, t, 0))],
                  out_specs=pl.BlockSpec((tr, c), lambda t: (t, 0)))(x)


def _ada_fwd(a_raw, w_loc, b_loc):
    ncol = w_loc.shape[1]

    def body(a_ref, w_ref, b_ref, o_ref):
        a = a_ref[...]
        act = (a * _sigmoid(a)).astype(BF16)
        o_ref[...] = _mm(act, w_ref[...].astype(BF16)) + b_ref[...]

    return _pcall(body, name="ada_fwd", out_shape=_sds((a_raw.shape[0], ncol), F32),
                  in_specs=[VMEM] * 3, out_specs=VMEM)(a_raw, w_loc, b_loc)


def _ada_bwd(a_raw, cctx_col, g_all, g_cols, w_loc, nb):
    nrow = a_raw.shape[0]
    ncol = w_loc.shape[1]

    def body(a_ref, cc_ref, gall_ref, g_ref, w_ref, dw_ref, pc_ref, gb_ref):
        a = a_ref[...]
        rowid = lax.broadcasted_iota(jnp.int32, (nrow, 1), 0) % 8
        act = jnp.where(rowid < nb, a * _sigmoid(a), 0.0).astype(BF16)
        g = g_ref[...]
        gc = _rowsum(jnp.where(rowid == nb, g, 0.0))
        cc = cc_ref[...]
        dw_ref[...] = _mm_tn(act, g.astype(BF16)) + (cc * _sigmoid(cc)) * gc
        pc_ref[...] = jnp.sum(w_ref[...] * gc, axis=1, keepdims=True)
        gb_ref[...] = _rowsum(gall_ref[...])

    return _pcall(body, name="ada_bwd",
                  out_shape=(_sds((D, ncol), F32), _sds((D, 1), F32), _sds((1, g_all.shape[1]), F32)),
                  in_specs=[VMEM] * 5, out_specs=(VMEM,) * 3, vmem_mb=48)(a_raw, cctx_col, g_all, g_cols, w_loc)


def _mod_spec(k, tpe, nrows):
    return pl.BlockSpec((1, k, D), lambda t: (jnp.minimum(t // tpe, nrows - 1), 0, 0))


def _load_ffn_weights(wall_ref, first, bufs, sems):
    fsh = FF // NDEV
    cps = []
    for j, buf in enumerate(bufs):
        for d in range(NDEV):
            cps.append(pltpu.make_async_copy(wall_ref.at[d, pl.ds((first + j) * fsh, fsh)],
                                             buf.at[pl.ds(d * fsh, fsh)], sems.at[j * NDEV + d]))
    for cp in cps:
        cp.start()
    for cp in cps:
        cp.wait()


def _token_specs(xs, tm, n_lat):
    specs = [pl.BlockSpec((tm, D), lambda t: (jnp.minimum(t, n_lat - 1), 0))]
    if len(xs) == 2:
        specs.append(pl.BlockSpec((tm, D), lambda t: (jnp.maximum(t - n_lat, 0), 0)))
    return specs


def _ffn_fwd(xs, mod3, norm_w, wall, first, *, tm, n_tiles, tpe, n_lat, name, target=None):
    nrows = mod3.shape[0]
    r = n_tiles * tm
    nx = len(xs)
    with_loss = target is not None

    def body(*refs):
        x_refs = refs[:nx]
        pos = nx
        if with_loss:
            tgt_ref = refs[pos]
            pos += 1
        mod_ref, nw_ref, wall_ref = refs[pos:pos + 3]
        pos += 3
        xo_ref, a_ref, b_ref, o_ref = refs[pos:pos + 4]
        pos += 4
        if with_loss:
            ls_ref = refs[pos]
            pos += 1
        w1_ref, w3_ref, w2_ref, wsem, acc_ref = refs[pos:]
        t = pl.program_id(0)

        @pl.when(t == 0)
        def _():
            _load_ffn_weights(wall_ref, first, (w1_ref, w3_ref, w2_ref), wsem)
            if with_loss:
                ls_ref[...] = jnp.zeros_like(ls_ref)

        x = x_refs[0][...]
        if nx == 2:
            x = jnp.where(t < n_lat, x, x_refs[1][...])
        n = x * _rms(x) * nw_ref[...]
        shift, scale, gate = mod_ref[0, 0:1, :], mod_ref[0, 1:2, :], mod_ref[0, 2:3, :]
        h = (n * (1.0 + scale) + shift).astype(BF16)
        for j in range(FF // FC):
            sl = slice(j * FC, (j + 1) * FC)
            a = _mm_nt(h, w1_ref[sl, :])
            b = _mm_nt(h, w3_ref[sl, :])
            a_ref[:, sl] = a.astype(BF16)
            b_ref[:, sl] = b.astype(BF16)
            g = (a * _sigmoid(a) * b).astype(BF16)
            part = _mm(g, w2_ref[sl, :])
            if j == 0:
                acc_ref[...] = part
            else:
                acc_ref[...] += part
        o = acc_ref[...]
        o_ref[...] = o.astype(BF16)
        out = x + (0.5 * gate) * o
        if with_loss:
            d = out - tgt_ref[...]
            xo_ref[...] = d * (1.0 / D)
            ls_ref[...] += jnp.sum(d * d)
        else:
            xo_ref[...] = out

    row = lambda cols: pl.BlockSpec((tm, cols), lambda t: (t, 0))
    in_specs = _token_specs(xs, tm, n_lat) + ([row(D)] if with_loss else []) + [
        _mod_spec(3, tpe, nrows), _const((1, D)), ANY]
    out_shape = [_sds((r, D), F32), _sds((r, FF), BF16), _sds((r, FF), BF16), _sds((r, D), BF16)]
    out_specs = [row(D), row(FF), row(FF), row(D)]
    if with_loss:
        out_shape.append(_sds((8, LANE), F32))
        out_specs.append(_const((8, LANE)))
    args = list(xs) + ([target] if with_loss else []) + [mod3, norm_w, wall]
    return _pcall(
        body, name=name, grid=(n_tiles,), out_shape=tuple(out_shape), in_specs=in_specs, out_specs=tuple(out_specs),
        scratch=[pltpu.VMEM((FF, D), BF16)] * 3 + [pltpu.SemaphoreType.DMA((3 * NDEV,)), pltpu.VMEM((tm, D), F32)],
        vmem_mb=56)(*args)


def _ffn_bwd_dx(dout, xs, a, b, o, mod3, norm_w, wall, first, *, tm, n_tiles, tpe, n_lat, name):
    nrows = mod3.shape[0]
    r = n_tiles * tm
    nx = len(xs)

    def body(*refs):
        dout_ref = refs[0]
        x_refs = refs[1:1 + nx]
        (a_ref, b_ref, o_ref, mod_ref, nw_ref, wall_ref,
         dx_ref, da_ref, db_ref, g_ref, do_ref, h_ref, dmod_ref, dnw_ref,
         w1_ref, w3_ref, w2_ref, wsem, acc_ref) = refs[1 + nx:]
        t = pl.program_id(0)

        @pl.when(t == 0)
        def _():
            _load_ffn_weights(wall_ref, first, (w1_ref, w3_ref, w2_ref), wsem)
            dnw_ref[...] = jnp.zeros_like(dnw_ref)

        x = x_refs[0][...]
        if nx == 2:
            x = jnp.where(t < n_lat, x, x_refs[1][...])
        dout = dout_ref[...]
        rr = _rms(x)
        xh = x * rr
        nw = nw_ref[...]
        n = xh * nw
        shift, scale, gate = mod_ref[0, 0:1, :], mod_ref[0, 1:2, :], mod_ref[0, 2:3, :]
        h = (n * (1.0 + scale) + shift).astype(BF16)
        h_ref[...] = h
        d_o = ((0.5 * gate) * dout).astype(BF16)
        do_ref[...] = d_o
        dgate = _rowsum(0.5 * o_ref[...].astype(F32) * dout)
        for j in range(FF // FC):
            sl = slice(j * FC, (j + 1) * FC)
            av = a_ref[:, sl].astype(F32)
            bv = b_ref[:, sl].astype(F32)
            dg = _mm_nt(d_o, w2_ref[sl, :])
            sig = _sigmoid(av)
            sa = av * sig
            g_ref[:, sl] = (sa * bv).astype(BF16)
            da = (dg * bv * (sig * (1.0 + av * (1.0 - sig)))).astype(BF16)
            db = (dg * sa).astype(BF16)
            da_ref[:, sl] = da
            db_ref[:, sl] = db
            part = _mm(da, w1_ref[sl, :]) + _mm(db, w3_ref[sl, :])
            if j == 0:
                acc_ref[...] = part
            else:
                acc_ref[...] += part
        dh = acc_ref[...]
        dn = dh * (1.0 + scale)
        dxh = dn * nw

        @pl.when(t < n_lat)
        def _():
            dx_ref[...] = dout + rr * (dxh - xh * jnp.mean(dxh * xh, axis=-1, keepdims=True))

        first_visit = jnp.where(t < n_lat, t % tpe == 0, t == n_lat)

        @pl.when(first_visit)
        def _():
            dmod_ref[...] = jnp.zeros_like(dmod_ref)

        dmod_ref[0, 0:1, :] += _rowsum(dh)
        dmod_ref[0, 1:2, :] += _rowsum(dh * n)
        dmod_ref[0, 2:3, :] += dgate
        dnw_ref[...] += _rowsum(dn * xh)

    row = lambda cols: pl.BlockSpec((tm, cols), lambda t: (t, 0))
    lat = pl.BlockSpec((tm, D), lambda t: (jnp.minimum(t, n_lat - 1), 0))
    return _pcall(
        body, name=name, grid=(n_tiles,),
        out_shape=(_sds((n_lat * tm, D), F32), _sds((r, FF), BF16), _sds((r, FF), BF16), _sds((r, FF), BF16),
                   _sds((r, D), BF16), _sds((r, D), BF16), _sds((nrows, 3, D), F32), _sds((1, D), F32)),
        in_specs=[row(D)] + _token_specs(xs, tm, n_lat) + [row(FF), row(FF), row(D), _mod_spec(3, tpe, nrows),
                                                            _const((1, D)), ANY],
        out_specs=(lat, row(FF), row(FF), row(FF), row(D), row(D), _mod_spec(3, tpe, nrows), _const((1, D))),
        scratch=[pltpu.VMEM((FF, D), BF16)] * 3 + [pltpu.SemaphoreType.DMA((3 * NDEV,)), pltpu.VMEM((tm, D), F32)],
        vmem_mb=60)(dout, *xs, a, b, o, mod3, norm_w, wall)


def _ffn_bwd_dw(h, d_o, da, db, g, gpack, slot, *, tr, name):
    r = h.shape[0]
    fh = FF // 2
    fsh = FF // NDEV
    nk = r // tr

    def body(h_ref, do_ref, da_ref, db_ref, g_ref, gin_ref, out_ref, acc1, acc3, acc2):
        k = pl.program_id(1)

        @pl.when(k == 0)
        def _():
            acc1[...] = jnp.zeros_like(acc1)
            acc3[...] = jnp.zeros_like(acc3)
            acc2[...] = jnp.zeros_like(acc2)

        hv = h_ref[...]
        acc1[...] += _mm_tn(da_ref[...], hv)
        acc3[...] += _mm_tn(db_ref[...], hv)
        acc2[...] += _mm_tn(g_ref[...], do_ref[...])

        @pl.when(k == nk - 1)
        def _():
            for i, acc in enumerate((acc1, acc3, acc2)):
                out_ref[:, i * fsh:(i + 1) * fsh, :] = acc[...].reshape(NDEV // 2, fsh, D).astype(BF16)

    rowd = pl.BlockSpec((tr, D), lambda f, k: (k, 0))
    rowf = pl.BlockSpec((tr, fh), lambda f, k: (k, f))
    return _pcall(
        body, name=name, grid=(2, nk), out_shape=_sds(gpack.shape, BF16),
        in_specs=[rowd, rowd, rowf, rowf, rowf, ANY],
        out_specs=pl.BlockSpec((NDEV // 2, 3 * fsh, D), lambda f, k: (f, slot, 0)),
        scratch=[pltpu.VMEM((fh, D), F32)] * 3, vmem_mb=56, aliases={5: 0})(h, d_o, da, db, g, gpack)


_PIECES = ((0, 128), (128, 384), (384, 896), (896, 1408), (1408, 1536))


def _proj_fwd(x1, mod2, norm_w, wint, *, tm, n_tiles, tpe, name="proj_fwd"):
    nrows = mod2.shape[0]
    r = n_tiles * tm

    def body(x_ref, mod_ref, nw_ref, w_ref, ckv_ref, q_ref, u_ref, v_ref, kpe_ref):
        x = x_ref[...]
        n = x * _rms(x) * nw_ref[...]
        h = (n * (1.0 + mod_ref[0, 1:2, :]) + mod_ref[0, 0:1, :]).astype(BF16)
        for (lo, hi), ref in zip(_PIECES, (ckv_ref, q_ref, u_ref, v_ref, kpe_ref)):
            ref[...] = _mm_nt(h, w_ref[lo:hi, :])

    row = lambda cols: pl.BlockSpec((tm, cols), lambda t: (t, 0))
    widths = [hi - lo for lo, hi in _PIECES]
    return _pcall(
        body, name=name, grid=(n_tiles,),
        out_shape=tuple(_sds((r, w), F32) for w in widths),
        in_specs=[row(D), _mod_spec(2, tpe, nrows), _const((1, D)), _const((WIN_ROWS, D))],
        out_specs=tuple(row(w) for w in widths), vmem_mb=40)(x1, mod2, norm_w, wint)


def _proj_bwd(dckv, dkpe, dq, du, dv, dx2, x1, mod2, norm_w, wint, *, tm, n_tiles, tpe, n_lat, name="proj_bwd"):
    nrows = mod2.shape[0]
    r = n_tiles * tm

    def body(dckv_ref, dkpe_ref, dq_ref, du_ref, dv_ref, dx2_ref, x_ref, mod_ref, nw_ref, w_ref,
             dx_ref, dw_ref, dmod_ref, dnw_ref, acc_ref):
        t = pl.program_id(0)
        is_lat = t < n_lat
        x = x_ref[...]
        rr = _rms(x)
        xh = x * rr
        nw = nw_ref[...]
        n = xh * nw
        scale = mod_ref[0, 1:2, :]
        h = (n * (1.0 + scale) + mod_ref[0, 0:1, :]).astype(BF16)

        @pl.when(t == 0)
        def _():
            dw_ref[...] = jnp.zeros_like(dw_ref)
            dnw_ref[...] = jnp.zeros_like(dnw_ref)

        dckv_v, dkpe_v = dckv_ref[...], dkpe_ref[...]
        acc_ref[...] = _mm(dckv_v, w_ref[0:128, :]) + _mm(dkpe_v, w_ref[1408:1536, :])
        dw_ref[0:128, :] += _mm_tn(dckv_v, h)
        dw_ref[1408:1536, :] += _mm_tn(dkpe_v, h)

        @pl.when(is_lat)
        def _():
            dq_v, du_v, dv_v = dq_ref[...], du_ref[...], dv_ref[...]
            acc_ref[...] += (_mm(dq_v, w_ref[128:384, :]) + _mm(du_v, w_ref[384:896, :])
                             + _mm(dv_v, w_ref[896:1408, :]))
            dw_ref[128:384, :] += _mm_tn(dq_v, h)
            dw_ref[384:896, :] += _mm_tn(du_v, h)
            dw_ref[896:1408, :] += _mm_tn(dv_v, h)

        dh = acc_ref[...]
        dn = dh * (1.0 + scale)
        dxh = dn * nw
        dx = rr * (dxh - xh * jnp.mean(dxh * xh, axis=-1, keepdims=True))
        dx_ref[...] = dx + jnp.where(is_lat, dx2_ref[...], 0.0)

        first = jnp.where(is_lat, t % tpe == 0, t == n_lat)

        @pl.when(first)
        def _():
            dmod_ref[...] = jnp.zeros_like(dmod_ref)

        dmod_ref[0, 0:1, :] += _rowsum(dh)
        dmod_ref[0, 1:2, :] += _rowsum(dh * n)
        dnw_ref[...] += _rowsum(dn * xh)

    row = lambda cols: pl.BlockSpec((tm, cols), lambda t: (t, 0))
    lat = lambda cols: pl.BlockSpec((tm, cols), lambda t: (jnp.minimum(t, n_lat - 1), 0))
    return _pcall(
        body, name=name, grid=(n_tiles,),
        out_shape=(_sds((r, D), F32), _sds((WIN_ROWS, D), F32), _sds((nrows, 2, D), F32), _sds((1, D), F32)),
        in_specs=[row(128), row(128), lat(256), lat(512), lat(512), lat(D), row(D), _mod_spec(2, tpe, nrows),
                  _const((1, D)), _const((WIN_ROWS, D))],
        out_specs=(row(D), _const((WIN_ROWS, D)), _mod_spec(2, tpe, nrows), _const((1, D))),
        scratch=[pltpu.VMEM((tm, D), F32)], vmem_mb=48)(dckv, dkpe, dq, du, dv, dx2, x1, mod2, norm_w, wint)


def _head_norm_rope(x, w_pad, cos, sin, seg, segt, rot):
    rh = lax.rsqrt(_dot_hl(x * x, seg) * (1.0 / DH) + EPS)
    rb = _dot_hl(rh, segt)
    y = x * rb
    t = y * w_pad
    out = []
    for h in range(H):
        th = t[:, h * LANE:(h + 1) * LANE]
        out.append(th * cos + _dot_hl(th, rot) * sin)
    return jnp.concatenate(out, axis=-1), y, rb


def _head_norm_rope_bwd(dout, y, rb, w_pad, cos, sin, seg, segt, rot_t):
    dt = []
    for h in range(H):
        dh = dout[:, h * LANE:(h + 1) * LANE]
        dt.append(dh * cos + _dot_hl(dh * sin, rot_t))
    dt = jnp.concatenate(dt, axis=-1)
    dw = _rowsum(dt * y)
    dy = dt * w_pad
    mean_h = _dot_hl(dy * y, seg) * (1.0 / DH)
    return rb * (dy - y * _dot_hl(mean_h, segt)), dw


def _q_prep_fwd(qp, qa_w, wuq, wq, cos, sin, cs, *, tm, n_lat, tpe):
    def body(qp_ref, qa_ref, wuq_ref, wq_ref, cos_ref, sin_ref, seg, segt, rot, q_ref):
        x = qp_ref[...]
        cq = (x * _rms(x) * qa_ref[...]).astype(BF16)
        q, _, _ = _head_norm_rope(_mm_nt(cq, wuq_ref[...]), wq_ref[...], cos_ref[...], sin_ref[...],
                                  seg[...], segt[...], rot[...])
        q_ref[...] = q.astype(BF16)

    row = lambda cols: pl.BlockSpec((tm, cols), lambda t: (t, 0))
    tab = pl.BlockSpec((tm, LANE), lambda t: (t % tpe, 0))
    return _pcall(
        body, name="q_prep_fwd", grid=(n_lat,), out_shape=_sds((n_lat * tm, HP), BF16),
        in_specs=[row(QL), _const((1, QL)), _const((HP, QL)), _const((1, HP)), tab, tab,
                  _const((HP, LANE)), _const((LANE, HP)), _const((LANE, LANE))],
        out_specs=row(HP))(qp, qa_w, wuq, wq, cos, sin, cs["seg_h"], cs["seg_ht"], cs["rot"])


def _q_prep_bwd(dq, qp, qa_w, wuq, wq, cos, sin, cs, *, tm, n_lat, tpe):
    def body(dq_ref, qp_ref, qa_ref, wuq_ref, wq_ref, cos_ref, sin_ref, seg, segt, rot, rot_t,
             dqp_ref, dwuq_ref, dqa_ref, dwq_ref):
        t = pl.program_id(0)
        x = qp_ref[...]
        ra = _rms(x)
        xh = x * ra
        qa = qa_ref[...]
        cq = (xh * qa).astype(BF16)
        wuq_v = wuq_ref[...]
        wq_v, cos_v, sin_v = wq_ref[...], cos_ref[...], sin_ref[...]
        _, y, rb = _head_norm_rope(_mm_nt(cq, wuq_v), wq_v, cos_v, sin_v, seg[...], segt[...], rot[...])
        dqraw, dwq = _head_norm_rope_bwd(dq_ref[...], y, rb, wq_v, cos_v, sin_v, seg[...], segt[...], rot_t[...])
        dqraw = dqraw.astype(BF16)
        dcq = _mm(dqraw, wuq_v)
        dxh = dcq * qa
        dqp_ref[...] = (ra * (dxh - xh * jnp.mean(dxh * xh, axis=-1, keepdims=True))).astype(BF16)

        @pl.when(t == 0)
        def _():
            dwuq_ref[...] = jnp.zeros_like(dwuq_ref)
            dqa_ref[...] = jnp.zeros_like(dqa_ref)
            dwq_ref[...] = jnp.zeros_like(dwq_ref)

        dwuq_ref[...] += _mm_tn(dqraw, cq)
        dqa_ref[...] += _rowsum(dcq * xh)
        dwq_ref[...] += dwq

    row = lambda cols: pl.BlockSpec((tm, cols), lambda t: (t, 0))
    tab = pl.BlockSpec((tm, LANE), lambda t: (t % tpe, 0))
    return _pcall(
        body, name="q_prep_bwd", grid=(n_lat,),
        out_shape=(_sds((n_lat * tm, QL), BF16), _sds((HP, QL), F32), _sds((1, QL), F32), _sds((1, HP), F32)),
        in_specs=[row(HP), row(QL), _const((1, QL)), _const((HP, QL)), _const((1, HP)), tab, tab,
                  _const((HP, LANE)), _const((LANE, HP)), _const((LANE, LANE)), _const((LANE, LANE))],
        out_specs=(row(QL), _const((HP, QL)), _const((1, QL)), _const((1, HP))), vmem_mb=40)(
            dq, qp, qa_w, wuq, wq, cos, sin, cs["seg_h"], cs["seg_ht"], cs["rot"], cs["rot_t"])


def _kv_tab_spec(tm, tpe, n_lat):
    return pl.BlockSpec((tm, LANE), lambda t: (jnp.where(t < n_lat, t % tpe, tpe), 0))


def _kv_prep_fwd(ckv, kpe, kva_w, wukv, wk, cosk, sink, cs, *, tm, n_tiles, tpe, n_lat):
    def body(ckv_ref, kpe_ref, kva_ref, wukv_ref, wk_ref, cos_ref, sin_ref, seg, segt, rot, k_ref, v_ref):
        x = ckv_ref[...]
        ckvn = (x * _rms(x) * kva_ref[...]).astype(BF16)
        kv = _mm_nt(ckvn, wukv_ref[...])
        kx = kv[:, :HP] + jnp.concatenate([kpe_ref[...]] * H, axis=-1)
        k, _, _ = _head_norm_rope(kx, wk_ref[...], cos_ref[...], sin_ref[...], seg[...], segt[...], rot[...])
        k_ref[...] = k.astype(BF16)
        v_ref[...] = kv[:, HP:].astype(BF16)

    row = lambda cols: pl.BlockSpec((tm, cols), lambda t: (t, 0))
    tab = _kv_tab_spec(tm, tpe, n_lat)
    r = n_tiles * tm
    return _pcall(
        body, name="kv_prep_fwd", grid=(n_tiles,), out_shape=(_sds((r, HP), BF16), _sds((r, HP), BF16)),
        in_specs=[row(KVL), row(LANE), _const((1, KVL)), _const((2 * HP, KVL)), _const((1, HP)), tab, tab,
                  _const((HP, LANE)), _const((LANE, HP)), _const((LANE, LANE))],
        out_specs=(row(HP), row(HP)), vmem_mb=40)(
            ckv, kpe, kva_w, wukv, wk, cosk, sink, cs["seg_h"], cs["seg_ht"], cs["rot"])


def _kv_prep_bwd(dks, dvs, ckv, kpe, kva_w, wukv, wk, cosk, sink, cs, *, tm, n_tiles, tpe, n_lat):
    def body(dkl_ref, dkc_ref, dvl_ref, dvc_ref, ckv_ref, kpe_ref, kva_ref, wukv_ref, wk_ref, cos_ref, sin_ref,
             seg, segt, rot, rot_t, dckv_ref, dkpe_ref, dwukv_ref, dkva_ref, dwk_ref):
        t = pl.program_id(0)
        is_lat = t < n_lat
        dk = jnp.where(is_lat, dkl_ref[...], dkc_ref[...])
        dv = jnp.where(is_lat, dvl_ref[...], dvc_ref[...])
        x = ckv_ref[...]
        ra = _rms(x)
        xh = x * ra
        kva = kva_ref[...]
        ckvn = (xh * kva).astype(BF16)
        wukv_v = wukv_ref[...]
        wk_v, cos_v, sin_v = wk_ref[...], cos_ref[...], sin_ref[...]
        kv = _mm_nt(ckvn, wukv_v)
        kx = kv[:, :HP] + jnp.concatenate([kpe_ref[...]] * H, axis=-1)
        _, y, rb = _head_norm_rope(kx, wk_v, cos_v, sin_v, seg[...], segt[...], rot[...])
        dkx, dwk = _head_norm_rope_bwd(dk, y, rb, wk_v, cos_v, sin_v, seg[...], segt[...], rot_t[...])
        dkpe = dkx[:, 0:LANE]
        for h in range(1, H):
            dkpe = dkpe + dkx[:, h * LANE:(h + 1) * LANE]
        lane = lax.broadcasted_iota(jnp.int32, (tm, LANE), 1)
        dkpe_ref[...] = jnp.where((lane >= DN) & (lane < DH), dkpe, 0.0).astype(BF16)
        dkv = jnp.concatenate([dkx, dv], axis=-1).astype(BF16)
        dckvn = _mm(dkv, wukv_v)
        dxh = dckvn * kva
        dckv_ref[...] = (ra * (dxh - xh * jnp.mean(dxh * xh, axis=-1, keepdims=True))).astype(BF16)

        @pl.when(t == 0)
        def _():
            dwukv_ref[...] = jnp.zeros_like(dwukv_ref)
            dkva_ref[...] = jnp.zeros_like(dkva_ref)
            dwk_ref[...] = jnp.zeros_like(dwk_ref)

        dwukv_ref[...] += _mm_tn(dkv, ckvn)
        dkva_ref[...] += _rowsum(dckvn * xh)
        dwk_ref[...] += dwk

    row = lambda cols: pl.BlockSpec((tm, cols), lambda t: (t, 0))
    lat = pl.BlockSpec((tm, HP), lambda t: (jnp.minimum(t, n_lat - 1), 0))
    ctx = pl.BlockSpec((tm, HP), lambda t: (jnp.maximum(t - n_lat, 0), 0))
    tab = _kv_tab_spec(tm, tpe, n_lat)
    r = n_tiles * tm
    return _pcall(
        body, name="kv_prep_bwd", grid=(n_tiles,),
        out_shape=(_sds((r, KVL), BF16), _sds((r, LANE), BF16), _sds((2 * HP, KVL), F32), _sds((1, KVL), F32),
                   _sds((1, HP), F32)),
        in_specs=[lat, ctx, lat, ctx, row(KVL), row(LANE), _const((1, KVL)), _const((2 * HP, KVL)), _const((1, HP)),
                  tab, tab, _const((HP, LANE)), _const((LANE, HP)), _const((LANE, LANE)), _const((LANE, LANE))],
        out_specs=(row(KVL), row(LANE), _const((2 * HP, KVL)), _const((1, KVL)), _const((1, HP))), vmem_mb=48)(
            dks[0], dks[1], dvs[0], dvs[1], ckv, kpe, kva_w, wukv, wk, cosk, sink,
            cs["seg_h"], cs["seg_ht"], cs["rot"], cs["rot_t"])


_SCALE = DH ** -0.5


def _attn_specs(tq, s, nc, tpe, n_lat_rows):
    qs = pl.BlockSpec((tq, LANE), lambda i, j, t: (i * tpe + t, j))
    kl = pl.BlockSpec((s, LANE), lambda i, j, t: (i, j))
    kc = pl.BlockSpec((nc, LANE), lambda i, j, t: (n_lat_rows // nc + i, j))
    return qs, kl, kc


def _softmax_parts(q, kl, kc):
    s1 = _mm_nt(q, kl) * _SCALE
    s2 = _mm_nt(q, kc) * _SCALE
    m = jnp.maximum(jnp.max(s1, axis=-1, keepdims=True), jnp.max(s2, axis=-1, keepdims=True))
    e1 = jnp.exp(s1 - m)
    e2 = jnp.exp(s2 - m)
    l = jnp.sum(e1, axis=-1, keepdims=True) + jnp.sum(e2, axis=-1, keepdims=True)
    return e1, e2, l


def _attn_fwd(q, k, v, *, nb, s, nc, tq):
    tpe = s // tq
    r_lat = nb * s

    def body(q_ref, kl_ref, kc_ref, vl_ref, vc_ref, o_ref):
        e1, e2, l = _softmax_parts(q_ref[...], kl_ref[...], kc_ref[...])
        o = _mm(e1.astype(BF16), vl_ref[...]) + _mm(e2.astype(BF16), vc_ref[...])
        o_ref[...] = (o / l).astype(BF16)

    qs, kl, kc = _attn_specs(tq, s, nc, tpe, r_lat)
    return _pcall(body, name="attn_fwd", grid=(nb, H, tpe), out_shape=_sds((r_lat, HP), BF16),
                  in_specs=[qs, kl, kc, kl, kc], out_specs=qs, vmem_mb=48)(q, k, k, v, v)


def _attn_bwd(q, k, v, o, do, *, nb, s, nc, tq):
    tpe = s // tq
    r_lat = nb * s

    def body(q_ref, kl_ref, kc_ref, vl_ref, vc_ref, o_ref, do_ref,
             dq_ref, dkl_ref, dkc_ref, dvl_ref, dvc_ref, akl, akc, avl, avc):
        t = pl.program_id(2)
        qv, klv, kcv = q_ref[...], kl_ref[...], kc_ref[...]
        e1, e2, l = _softmax_parts(qv, klv, kcv)
        inv = 1.0 / l
        p1, p2 = e1 * inv, e2 * inv
        dov = do_ref[...]
        delta = jnp.sum(dov.astype(F32) * o_ref[...].astype(F32), axis=-1, keepdims=True)
        ds1 = (p1 * (_mm_nt(dov, vl_ref[...]) - delta) * _SCALE).astype(BF16)
        ds2 = (p2 * (_mm_nt(dov, vc_ref[...]) - delta) * _SCALE).astype(BF16)
        dq_ref[...] = _mm(ds1, klv) + _mm(ds2, kcv)

        @pl.when(t == 0)
        def _():
            akl[...] = jnp.zeros_like(akl)
            akc[...] = jnp.zeros_like(akc)
            avl[...] = jnp.zeros_like(avl)
            avc[...] = jnp.zeros_like(avc)

        akl[...] += _mm_tn(qv, ds1)
        akc[...] += _mm_tn(qv, ds2)
        avl[...] += _mm_tn(dov, p1.astype(BF16))
        avc[...] += _mm_tn(dov, p2.astype(BF16))

        @pl.when(t == tpe - 1)
        def _():
            dkl_ref[...] = akl[...].T
            dkc_ref[...] = akc[...].T
            dvl_ref[...] = avl[...].T
            dvc_ref[...] = avc[...].T

    qs, kl, kc = _attn_specs(tq, s, nc, tpe, r_lat)
    kc_out = pl.BlockSpec((nc, LANE), lambda i, j, t: (i, j))
    return _pcall(
        body, name="attn_bwd", grid=(nb, H, tpe),
        out_shape=(_sds((r_lat, HP), F32), _sds((r_lat, HP), F32), _sds((nb * nc, HP), F32),
                   _sds((r_lat, HP), F32), _sds((nb * nc, HP), F32)),
        in_specs=[qs, kl, kc, kl, kc, qs, qs], out_specs=(qs, kl, kc_out, kl, kc_out),
        scratch=[pltpu.VMEM((LANE, s), F32), pltpu.VMEM((LANE, nc), F32)] * 2, vmem_mb=56)(q, k, k, v, v, o, do)


def _gating(vn, ws_ref, bias_ref, s_scr, tm):
    lane = lax.broadcasted_iota(jnp.int32, (CH, LANE), 1)
    for c in range(tm // CH):
        rs = slice(c * CH, (c + 1) * CH)
        for j in range(G // 2):
            ls = slice(j * LANE, (j + 1) * LANE)
            vp = vn[rs, ls]
            s_scr[rs, ls] = jnp.where(lane < GD, _mm(ws_ref[2 * j], vp), _mm(ws_ref[2 * j + 1], vp)) + bias_ref[:, ls]


def _mix_fwd(u, v, attn, x1, gate, wv, ws, bias, wout, cs, *, tm, n_lat, tpe):
    nrows = gate.shape[0]

    def body(u_ref, v_ref, attn_ref, x_ref, gate_ref, wv_ref, ws_ref, bias_ref, wout_ref, seg, segt,
             x2_ref, mix_ref, s_scr):
        vg = _gelu(v_ref[...])
        rg = lax.rsqrt(_dot_hl(vg * vg, seg[...]) * (1.0 / GD) + EPS)
        vn = (vg * _dot_hl(rg, segt[...]) * wv_ref[...]).astype(BF16)
        _gating(vn, ws_ref, bias_ref, s_scr, tm)
        sg = (_gelu(u_ref[...]) * s_scr[...]).astype(BF16)
        mix = _mm(attn_ref[...], wout_ref[0:HP, :]) + _mm(sg, wout_ref[HP:, :])
        mix_ref[...] = mix.astype(BF16)
        x2_ref[...] = x_ref[...] + gate_ref[0] * mix

    row = lambda cols: pl.BlockSpec((tm, cols), lambda t: (t, 0))
    r = n_lat * tm
    return _pcall(
        body, name="mix_fwd", grid=(n_lat,),
        out_shape=(_sds((r, D), F32), _sds((r, D), BF16)),
        in_specs=[row(G * GD), row(G * GD), row(HP), row(D), _mod_spec(1, tpe, nrows), _const((1, G * GD)),
                  _const((G, CH, CH)), _const((CH, G * GD)), _const((HP + G * GD, D)), _const((G * GD, LANE)),
                  _const((LANE, G * GD))],
        out_specs=(row(D), row(D)), scratch=[pltpu.VMEM((tm, G * GD), F32)], vmem_mb=40)(
            u, v, attn, x1, gate, wv, ws, bias, wout, cs["seg_g"], cs["seg_gt"])


def _mix_bwd(dx2, mix, u, v, attn, gate, wv, ws, wst, bias, wout, cs, *, tm, n_lat, tpe):
    nrows = gate.shape[0]
    wrows = HP + G * GD

    def body(dx2_ref, mix_ref, u_ref, v_ref, attn_ref, gate_ref, wv_ref, ws_ref, wst_ref, bias_ref, wout_ref, seg, segt,
             dattn_ref, du_ref, dv_ref, dgate_ref, dwout_ref, dws_ref, dbs_ref, dwv_ref, s_scr, dvn_scr, dbias_scr):
        t = pl.program_id(0)
        dx2 = dx2_ref[...]
        dmix = (dx2 * gate_ref[0]).astype(BF16)
        dcat = _mm_nt(dmix, wout_ref[...])
        dattn_ref[...] = dcat[:, :HP].astype(BF16)
        dsg = dcat[:, HP:]

        vraw = v_ref[...]
        vg = _gelu(vraw)
        rg = lax.rsqrt(_dot_hl(vg * vg, seg[...]) * (1.0 / GD) + EPS)
        r64 = _dot_hl(rg, segt[...])
        y = vg * r64
        wv_v = wv_ref[...]
        vn = (y * wv_v).astype(BF16)
        _gating(vn, ws_ref, bias_ref, s_scr, tm)
        uraw = u_ref[...]
        ug = _gelu(uraw)
        s = s_scr[...]
        sg = (ug * s).astype(BF16)
        du_ref[...] = (dsg * s * _gelu_grad(uraw)).astype(BF16)
        ds = dsg * ug

        @pl.when(t == 0)
        def _():
            dwout_ref[...] = jnp.zeros_like(dwout_ref)
            dws_ref[...] = jnp.zeros_like(dws_ref)
            dwv_ref[...] = jnp.zeros_like(dwv_ref)
            dbias_scr[...] = jnp.zeros_like(dbias_scr)

        @pl.when(t % tpe == 0)
        def _():
            dgate_ref[...] = jnp.zeros_like(dgate_ref)

        dgate_ref[0] += _rowsum(dx2 * mix_ref[...].astype(F32))
        dwout_ref[...] += _mm_tn(jnp.concatenate([attn_ref[...], sg], axis=-1), dmix)

        lane = lax.broadcasted_iota(jnp.int32, (CH, LANE), 1)
        for c in range(tm // CH):
            rs = slice(c * CH, (c + 1) * CH)
            dbias_scr[...] += ds[rs, :]
            for j in range(G // 2):
                ls = slice(j * LANE, (j + 1) * LANE)
                dsp32 = ds[rs, ls]
                dsp = dsp32.astype(BF16)
                vp = vn[rs, ls]
                dvn_scr[rs, ls] = jnp.where(lane < GD, _mm(wst_ref[2 * j], dsp), _mm(wst_ref[2 * j + 1], dsp))
                dws_ref[2 * j] += _mm_nt(jnp.where(lane < GD, dsp32, 0.0).astype(BF16), vp)
                dws_ref[2 * j + 1] += _mm_nt(jnp.where(lane < GD, 0.0, dsp32).astype(BF16), vp)

        dvn = dvn_scr[...]
        dwv_ref[...] += _rowsum(dvn * y)
        dy = dvn * wv_v
        mean_g = _dot_hl(dy * y, seg[...]) * (1.0 / GD)
        dvg = r64 * (dy - y * _dot_hl(mean_g, segt[...]))
        dv_ref[...] = (dvg * _gelu_grad(vraw)).astype(BF16)

        @pl.when(t == n_lat - 1)
        def _():
            dbs_ref[...] = _dot_hl(dbias_scr[...], seg[...])

    row = lambda cols: pl.BlockSpec((tm, cols), lambda t: (t, 0))
    r = n_lat * tm
    return _pcall(
        body, name="mix_bwd", grid=(n_lat,),
        out_shape=(_sds((r, HP), BF16), _sds((r, G * GD), BF16), _sds((r, G * GD), BF16), _sds((nrows, 1, D), F32),
                   _sds((wrows, D), F32), _sds((G, CH, CH), F32), _sds((CH, LANE), F32), _sds((1, G * GD), F32)),
        in_specs=[row(D), row(D), row(G * GD), row(G * GD), row(HP), _mod_spec(1, tpe, nrows), _const((1, G * GD)),
                  _const((G, CH, CH)), _const((G, CH, CH)), _const((CH, G * GD)), _const((wrows, D)),
                  _const((G * GD, LANE)), _const((LANE, G * GD))],
        out_specs=(row(HP), row(G * GD), row(G * GD), _mod_spec(1, tpe, nrows), _const((wrows, D)),
                   _const((G, CH, CH)), _const((CH, LANE)), _const((1, G * GD))),
        scratch=[pltpu.VMEM((tm, G * GD), F32), pltpu.VMEM((tm, G * GD), F32), pltpu.VMEM((CH, G * GD), F32)],
        vmem_mb=56)(dx2, mix, u, v, attn, gate, wv, ws, wst, bias, wout, cs["seg_g"], cs["seg_gt"])


def _adamw_math(w, g, m, v):
    m2 = ADAM_B1 * m + (1.0 - ADAM_B1) * g
    v2 = ADAM_B2 * v + (1.0 - ADAM_B2) * (g * g)
    m_hat = m2 / (1.0 - ADAM_B1 ** ADAM_STEP)
    v_hat = v2 / (1.0 - ADAM_B2 ** ADAM_STEP)
    delta = -ADAM_LR * (m_hat / (jnp.sqrt(v_hat) + ADAM_EPS) + ADAM_WD * w)
    return delta, m2, v2


def _row_tile(r, c):
    best = r
    for tr in range(8, r, 8):
        if r % tr == 0 and tr * c * 4 <= MIB:
            best = tr
    return best


def _adamw(w, g, m, v, name):
    r, c = w.shape
    tr = _row_tile(r, c)

    def body(w_ref, g_ref, m_ref, v_ref, d_ref, mo_ref, vo_ref):
        d_ref[...], mo_ref[...], vo_ref[...] = _adamw_math(w_ref[...], g_ref[...], m_ref[...], v_ref[...])

    blk = pl.BlockSpec((tr, c), lambda t: (t, 0))
    return _pcall(body, name=name, grid=(r // tr,), out_shape=(_sds((r, c), F32),) * 3,
                  in_specs=[blk] * 4, out_specs=(blk,) * 3)(w, g, m, v)


def _adamw_small(params):
    n = len(params)

    def body(*refs):
        ins, outs = refs[:4 * n], refs[4 * n:]
        for i in range(n):
            w, g, m, v = (ins[4 * i + k][...] for k in range(4))
            if i == 0:
                sig = _sigmoid(w)
                g = g * (sig * (1.0 + w * (1.0 - sig)))
            d, m2, v2 = _adamw_math(w, g, m, v)
            outs[4 * i][...] = g
            outs[4 * i + 1][...] = d
            outs[4 * i + 2][...] = m2
            outs[4 * i + 3][...] = v2

    flat = [a for p in params for a in p]
    out_shape = tuple(_sds(p[0].shape, F32) for p in params for _ in range(4))
    res = _pcall(body, name="adamw_small", out_shape=out_shape, in_specs=[VMEM] * (4 * n),
                 out_specs=(VMEM,) * (4 * n))(*flat)
    return [res[4 * i:4 * i + 4] for i in range(n)]


def _rope_tables(s):
    rows = jnp.repeat(jnp.arange(s // GRID_W, dtype=F32), GRID_W)
    cols = jnp.tile(jnp.arange(GRID_W, dtype=F32), s // GRID_W)
    half = DR // 2
    inv = ROPE_BASE ** (-jnp.arange(0, half, 2, dtype=F32) / half)
    ang_r = rows[:, None] * inv
    ang_c = cols[:, None] * inv
    ang = jnp.concatenate([ang_r, ang_r, ang_c, ang_c], axis=-1)
    return jnp.cos(ang), jnp.sin(ang)


def _head_pad(a, real):
    return jnp.pad(a, ((0, 0), (0, LANE - real), (0, 0))).reshape(HP, a.shape[2])


def kernel(x, c, ctx, c_ctx, w_ada, b_ada, norm1_w, ffn1_w1, ffn1_w3, ffn1_w2, norm2_w, w_in, q_a_norm_w, w_uq, kv_a_norm_w, w_ukv, q_norm_w, k_norm_w, v_norm_w, w_s, b_s, w_out, norm3_w, ffn2_w1, ffn2_w3, ffn2_w2, loss_target, m_c_ctx, m_w_ada, m_b_ada, m_norm1_w, m_ffn1_w1, m_ffn1_w3, m_ffn1_w2, m_norm2_w, m_w_in, m_q_a_norm_w, m_w_uq, m_kv_a_norm_w, m_w_ukv, m_q_norm_w, m_k_norm_w, m_v_norm_w, m_w_s, m_b_s, m_w_out, m_norm3_w, m_ffn2_w1, m_ffn2_w3, m_ffn2_w2, v_c_ctx, v_w_ada, v_b_ada, v_norm1_w, v_ffn1_w1, v_ffn1_w3, v_ffn1_w2, v_norm2_w, v_w_in, v_q_a_norm_w, v_w_uq, v_kv_a_norm_w, v_w_ukv, v_q_norm_w, v_k_norm_w, v_v_norm_w, v_w_s, v_b_s, v_w_out, v_norm3_w, v_ffn2_w1, v_ffn2_w3, v_ffn2_w2):
    nb, s, _ = x.shape
    nc = ctx.shape[1]
    tm = 256 if nc % 256 == 0 else 128
    tpe = s // tm
    n_lat = nb * tpe
    n_all = n_lat + nb * nc // tm
    r_lat = nb * s
    me = 4 * lax.axis_index("x") + 2 * lax.axis_index("y") + lax.axis_index("c")
    cs = _consts()
    ncol = w_ada.shape[2]
    fsh = ffn1_w1.shape[2]
    assert nb + 1 <= 8 and NDEV * fsh == FF and NDEV * ncol == NMOD * D and s % nc == 0 and nc % tm == 0

    a_loc = jnp.concatenate([c, c_ctx[None, :], jnp.zeros((7 - nb, D), F32)], axis=0)
    a_raw = _all_gather(a_loc, "gather_c").reshape(NDEV * 8, D)
    mod_cols = _ada_fwd(a_raw, w_ada[0], lax.dynamic_slice_in_dim(b_ada, me * ncol, ncol, axis=1))
    mod_all = _all_gather(mod_cols, "gather_mod")
    mod_mine = lax.dynamic_slice_in_dim(mod_all, 8 * me, 8, axis=1)
    modtab = mod_mine.transpose(1, 0, 2).reshape(8, NMOD, D)[:nb + 1]

    def t16(a):
        return a.T.astype(BF16)

    wpack = jnp.concatenate([
        t16(ffn1_w1[0]), t16(ffn1_w3[0]), ffn1_w2[0].astype(BF16),
        t16(ffn2_w1[0]), t16(ffn2_w3[0]), ffn2_w2[0].astype(BF16),
        t16(w_in[0]), jnp.zeros((12, D), BF16),
        w_out[0].astype(BF16),
        t16(w_uq[0]).reshape(24, D), jnp.zeros((8, D), BF16),
        t16(w_ukv[0]).reshape(16, D)], axis=0)
    wall = _all_gather(wpack, "gather_weights")

    o0 = NFFN_W * fsh
    wint = wall[:, o0:o0 + 180].reshape(IN_COLS, D)
    z = lambda n: jnp.zeros((n, D), BF16)
    wint = jnp.concatenate([wint[0:128], wint[160:416], wint[416:928], wint[928:1440],
                            z(DN), wint[128:160], z(LANE - DH)], axis=0)
    wout = wall[:, o0 + 192:o0 + 320].reshape(D, D)
    wout = jnp.concatenate([_head_pad(wout[:H * DV].reshape(H, DV, D), DV), wout[H * DV:]], axis=0)
    wuq = _head_pad(wall[:, o0 + 320:o0 + 344].reshape(H, DH, QL), DH)
    wukvt = wall[:, o0 + 352:o0 + 368].reshape(H, DN + DV, KVL)
    wukv = jnp.concatenate([_head_pad(wukvt[:, :DN], DN), _head_pad(wukvt[:, DN:], DV)], axis=0)

    def head_w(wn):
        return jnp.tile(jnp.pad(wn, ((0, 0), (0, LANE - DH))), (1, H))

    wq, wk = head_w(q_norm_w), head_w(k_norm_w)
    wv = v_norm_w.reshape(1, G * GD)
    ws16 = w_s[0].astype(BF16)
    wst16 = w_s[0].transpose(0, 2, 1).astype(BF16)
    bias = jnp.repeat(b_s[0].T, GD, axis=1)
    cos, sin = _rope_tables(s)
    cos = jnp.pad(cos, ((0, 0), (DN, LANE - DH)), constant_values=1.0)
    sin = jnp.pad(sin, ((0, 0), (DN, LANE - DH)))
    cos_k = jnp.concatenate([cos, jnp.ones((tm, LANE), F32)], axis=0)
    sin_k = jnp.concatenate([sin, jnp.zeros((tm, LANE), F32)], axis=0)

    xs = (x.reshape(r_lat, D), ctx.reshape(nb * nc, D))
    x1, a1, b1, o1 = _ffn_fwd(xs, modtab[:, 0:3], norm1_w, wall, 0,
                              tm=tm, n_tiles=n_all, tpe=tpe, n_lat=n_lat, name="ffn1_fwd")
    ckv, qp, u_raw, v_raw, kpe = _proj_fwd(x1, modtab[:, 3:5], norm2_w, wint, tm=tm, n_tiles=n_all, tpe=tpe)
    q = _q_prep_fwd(qp, q_a_norm_w, wuq, wq, cos, sin, cs, tm=tm, n_lat=n_lat, tpe=tpe)
    k, v = _kv_prep_fwd(ckv, kpe, kv_a_norm_w, wukv, wk, cos_k, sin_k, cs, tm=tm, n_tiles=n_all, tpe=tpe, n_lat=n_lat)
    attn = _attn_fwd(q, k, v, nb=nb, s=s, nc=nc, tq=tm)
    x2, mix = _mix_fwd(u_raw, v_raw, attn, x1, modtab[:nb, 5:6], wv, ws16, bias, wout, cs,
                       tm=tm, n_lat=n_lat, tpe=tpe)
    dy, a2, b2, o2, lsum = _ffn_fwd((x2,), modtab[:nb, 6:9], norm3_w, wall, 3, tm=tm, n_tiles=n_lat, tpe=tpe,
                                    n_lat=n_lat, name="ffn2_fwd", target=loss_target.reshape(r_lat, D))
    loss = lax.psum(lsum[0, 0] * (0.5 / D), ("x", "y", "c"))

    tr = 2 * tm if n_lat % 2 == 0 and n_all % 2 == 0 else tm
    gpack = jnp.zeros((NDEV, NFFN_W * fsh, D), BF16)
    dx2, da2, db2, g2, do2, h2, dmod678, dnorm3 = _ffn_bwd_dx(
        dy, (x2,), a2, b2, o2, modtab[:nb, 6:9], norm3_w, wall, 3,
        tm=tm, n_tiles=n_lat, tpe=tpe, n_lat=n_lat, name="ffn2_bwd_dx")
    gpack = _ffn_bwd_dw(h2, do2, da2, db2, g2, gpack, 1, tr=tr, name="ffn2_bwd_dw")

    dattn, du, dv, dgate5, dwout, dws, dbs, dwv = _mix_bwd(
        dx2, mix, u_raw, v_raw, attn, modtab[:nb, 5:6], wv, ws16, wst16, bias, wout, cs, tm=tm, n_lat=n_lat, tpe=tpe)
    dq, dk_l, dk_c, dv_l, dv_c = _attn_bwd(q, k, v, attn, dattn, nb=nb, s=s, nc=nc, tq=tm)
    dqp, dwuq, dqa, dwq = _q_prep_bwd(dq, qp, q_a_norm_w, wuq, wq, cos, sin, cs, tm=tm, n_lat=n_lat, tpe=tpe)
    dckv, dkpe, dwukv, dkva, dwk = _kv_prep_bwd((dk_l, dk_c), (dv_l, dv_c), ckv, kpe, kv_a_norm_w, wukv, wk,
                                                cos_k, sin_k, cs, tm=tm, n_tiles=n_all, tpe=tpe, n_lat=n_lat)
    dx1, dwin, dmod34, dnorm2 = _proj_bwd(dckv, dkpe, dqp, du, dv, dx2, x1, modtab[:, 3:5], norm2_w, wint,
                                          tm=tm, n_tiles=n_all, tpe=tpe, n_lat=n_lat)
    dx0, da1, db1, g1, do1, h1, dmod012, dnorm1 = _ffn_bwd_dx(
        dx1, xs, a1, b1, o1, modtab[:, 0:3], norm1_w, wall, 0,
        tm=tm, n_tiles=n_all, tpe=tpe, n_lat=n_lat, name="ffn1_bwd_dx")
    gpack = _ffn_bwd_dw(h1, do1, da1, db1, g1, gpack, 0, tr=tr, name="ffn1_bwd_dw")
    grad_x = dx0.reshape(nb, s, D)

    zrow = jnp.zeros((1, D), F32)
    g_lat = jnp.concatenate([dmod012[:nb, 0], dmod012[:nb, 1], dmod012[:nb, 2], dmod34[:nb, 0], dmod34[:nb, 1],
                             dgate5[:, 0], dmod678[:, 0], dmod678[:, 1], dmod678[:, 2]], axis=1)
    g_ctx = jnp.concatenate([dmod012[nb:, 0], dmod012[nb:, 1], dmod012[nb:, 2], dmod34[nb:, 0], dmod34[nb:, 1],
                             zrow, zrow, zrow, zrow], axis=1)
    g_loc = jnp.concatenate([g_lat, g_ctx, jnp.zeros((7 - nb, NMOD * D), F32)], axis=0)
    g_all = _all_gather(g_loc, "gather_gmod").reshape(NDEV * 8, NMOD * D)
    g_cols = lax.dynamic_slice_in_dim(g_all, me * ncol, ncol, axis=1)
    g_w_ada, pc_ctx, g_b_ada = _ada_bwd(a_raw, c_ctx.reshape(D, 1), g_all, g_cols, w_ada[0], nb)

    def blocks(a):
        return a.reshape(NDEV, a.shape[0] // NDEV, D)

    dwin_o = jnp.concatenate([dwin[0:128], dwin[KPE_LO:KPE_LO + DR], dwin[128:384], dwin[384:896], dwin[896:1408]],
                             axis=0)
    dwout_o = jnp.concatenate([dwout[:HP].reshape(H, LANE, D)[:, :DV].reshape(H * DV, D), dwout[HP:]], axis=0)
    dwuq_o = dwuq.reshape(H, LANE, QL)[:, :DH]
    dwukv_o = jnp.concatenate([dwukv[:HP].reshape(H, LANE, KVL)[:, :DN], dwukv[HP:].reshape(H, LANE, KVL)[:, :DV]],
                              axis=1)
    gmisc = jnp.concatenate([
        blocks(dwin_o).astype(BF16), jnp.zeros((NDEV, 12, D), BF16),
        blocks(dwout_o).astype(BF16),
        dwuq_o.reshape(NDEV, 24, D).astype(BF16), jnp.zeros((NDEV, 8, D), BF16),
        dwukv_o.reshape(NDEV, 16, D).astype(BF16)], axis=1)
    keeps, gots = _scatter_sibling([gpack, gmisc], "scatter_sibling")
    parts = [_add_pair(keeps[0], gots[0], 176, "add_pair_ffn"), _add_pair(keeps[1], gots[1], 368, "add_pair_misc")]
    recv = _scatter_chips(parts, "scatter_chips")
    gsum = _sum_slots(recv[0], 176, "sum_grads_ffn")
    msum = _sum_slots(recv[1], 368, "sum_grads_misc")

    g_big = {
        "ffn1_w1": gsum[0:fsh].T, "ffn1_w3": gsum[fsh:2 * fsh].T, "ffn1_w2": gsum[2 * fsh:3 * fsh],
        "ffn2_w1": gsum[3 * fsh:4 * fsh].T, "ffn2_w3": gsum[4 * fsh:5 * fsh].T, "ffn2_w2": gsum[5 * fsh:6 * fsh],
        "w_in": msum[0:180].T, "w_out": msum[192:320],
        "w_uq": msum[320:344].reshape(DH, QL).T, "w_ukv": msum[352:368].reshape(DN + DV, KVL).T,
        "w_ada": g_w_ada,
    }

    def prow(a):
        a = a.reshape(1, -1)
        return jnp.concatenate([a, jnp.zeros((1, D - a.shape[1]), F32)], axis=1)

    g_qn = dwq.reshape(H, LANE)[:, :DH].sum(0)
    g_kn = dwk.reshape(H, LANE)[:, :DH].sum(0)
    spack = jnp.concatenate([
        dnorm1, dnorm2, dnorm3, prow(dqa), prow(dkva), prow(g_qn), prow(g_kn), prow(dwv),
        prow(dbs[:, :G].T), prow(pc_ctx), jnp.zeros((6, D), F32), dws.reshape(CH, D)], axis=0)
    ssum = _sum_slots(_all_gather(spack, "gather_small"), 144, "sum_small")

    big_in = {
        "w_ada": (w_ada, m_w_ada, v_w_ada), "ffn1_w1": (ffn1_w1, m_ffn1_w1, v_ffn1_w1),
        "ffn1_w3": (ffn1_w3, m_ffn1_w3, v_ffn1_w3), "ffn1_w2": (ffn1_w2, m_ffn1_w2, v_ffn1_w2),
        "w_in": (w_in, m_w_in, v_w_in), "w_uq": (w_uq, m_w_uq, v_w_uq), "w_ukv": (w_ukv, m_w_ukv, v_w_ukv),
        "w_out": (w_out, m_w_out, v_w_out), "ffn2_w1": (ffn2_w1, m_ffn2_w1, v_ffn2_w1),
        "ffn2_w3": (ffn2_w3, m_ffn2_w3, v_ffn2_w3), "ffn2_w2": (ffn2_w2, m_ffn2_w2, v_ffn2_w2),
    }
    res = {}
    for nm, (w, m, v_) in big_in.items():
        g = g_big[nm]
        d_, m_, v2_ = _adamw(w[0], g, m[0], v_[0], "adamw_" + nm)
        res[nm] = tuple(a[None] for a in (g, d_, m_, v2_))

    small_in = [
        ("c_ctx", c_ctx, m_c_ctx, v_c_ctx, ssum[9:10], (1, D)),
        ("b_ada", b_ada, m_b_ada, v_b_ada, g_b_ada, (1, NMOD * D)),
        ("norm1_w", norm1_w, m_norm1_w, v_norm1_w, ssum[0:1], (1, D)),
        ("norm2_w", norm2_w, m_norm2_w, v_norm2_w, ssum[1:2], (1, D)),
        ("norm3_w", norm3_w, m_norm3_w, v_norm3_w, ssum[2:3], (1, D)),
        ("q_a_norm_w", q_a_norm_w, m_q_a_norm_w, v_q_a_norm_w, ssum[3:4, :QL], (1, QL)),
        ("kv_a_norm_w", kv_a_norm_w, m_kv_a_norm_w, v_kv_a_norm_w, ssum[4:5, :KVL], (1, KVL)),
        ("q_norm_w", q_norm_w, m_q_norm_w, v_q_norm_w, ssum[5:6, :DH], (1, DH)),
        ("k_norm_w", k_norm_w, m_k_norm_w, v_k_norm_w, ssum[6:7, :DH], (1, DH)),
        ("v_norm_w", v_norm_w, m_v_norm_w, v_v_norm_w, ssum[7:8, :G * GD], (G, GD)),
        ("b_s", b_s, m_b_s, v_b_s, ssum[8:9], (G, CH)),
        ("w_s", w_s, m_w_s, v_w_s, ssum[16:144], (G * CH, CH)),
    ]
    small_out = _adamw_small(
        [(w.reshape(sh), g.reshape(sh), m.reshape(sh), v_.reshape(sh)) for _, w, m, v_, g, sh in small_in])
    for (nm, w, *_), outs in zip(small_in, small_out):
        res[nm] = tuple(a.reshape(w.shape) for a in outs)

    order = ["c_ctx", "w_ada", "b_ada", "norm1_w", "ffn1_w1", "ffn1_w3", "ffn1_w2", "norm2_w", "w_in", "q_a_norm_w",
             "w_uq", "kv_a_norm_w", "w_ukv", "q_norm_w", "k_norm_w", "v_norm_w", "w_s", "b_s", "w_out", "norm3_w",
             "ffn2_w1", "ffn2_w3", "ffn2_w2"]
    return (loss, grad_x, *[res[n][0] for n in order], *[res[n][1] for n in order],
            *[res[n][2] for n in order], *[res[n][3] for n in order])
```

```python
import numpy as np
import jax
import jax.numpy as jnp
from jax import lax
from jax.experimental import pallas as pl
from jax.experimental.pallas import tpu as pltpu

F32 = jnp.float32
BF16 = jnp.bfloat16

D = 1024
FF = 2816
FC = 256
H = 8
DN, DR, DV = 64, 32, 64
DH = DN + DR
QL, KVL = 256, 128
G, GD, CH = 8, 64, 128
NMOD = 9
EPS = 1e-6
GRID_W = 64
ROPE_BASE = 10000.0
NDEV = 8
LANE = 128
HP = H * LANE
IN_COLS = 1440
WIN_ROWS = 1536
KPE_LO = 1408 + DN
NFFN_W = 6
MIB = 1 << 20

ADAM_LR, ADAM_B1, ADAM_B2, ADAM_EPS, ADAM_WD, ADAM_STEP = 0.001, 0.9, 0.999, 1e-08, 0.01, 10

MESH = pl.DeviceIdType.MESH
ANY = pl.BlockSpec(memory_space=pl.ANY)
VMEM = pl.BlockSpec(memory_space=pltpu.VMEM)


def _mm(a, b):
    return jnp.dot(a, b, preferred_element_type=F32)


def _mm_nt(a, b):
    return lax.dot_general(a, b, (((1,), (1,)), ((), ())), preferred_element_type=F32)


def _mm_tn(a, b):
    return lax.dot_general(a, b, (((0,), (0,)), ((), ())), preferred_element_type=F32)


def _dot_hl(x, m):
    hi = x.astype(BF16)
    lo = (x - hi.astype(F32)).astype(BF16)
    return _mm(hi, m) + _mm(lo, m)


def _sigmoid(a):
    return 1.0 / (1.0 + jnp.exp(-a))


_G0 = 0.7978845608028654
_G1 = 0.044715


def _gelu(x):
    return 0.5 * x * (1.0 + jnp.tanh(_G0 * (x + _G1 * (x * x * x))))


def _gelu_grad(x):
    th = jnp.tanh(_G0 * (x + _G1 * (x * x * x)))
    return 0.5 * (1.0 + th) + 0.5 * x * (1.0 - th * th) * (_G0 * (1.0 + 3.0 * _G1 * x * x))


def _rowsum(y):
    return jnp.sum(y, axis=0, keepdims=True)


def _rms(x):
    return lax.rsqrt(jnp.mean(x * x, axis=-1, keepdims=True) + EPS)


def _pcall(body, *, name, out_shape, in_specs, out_specs, grid=None, scratch=(), vmem_mb=32, aliases=None):
    kw = {}
    if grid is not None:
        kw["grid"] = grid
        sem = ("arbitrary",) * len(grid)
    else:
        sem = None
    if aliases:
        kw["input_output_aliases"] = aliases
    return pl.pallas_call(
        body, name=name, out_shape=out_shape, in_specs=in_specs, out_specs=out_specs,
        scratch_shapes=list(scratch),
        compiler_params=pltpu.CompilerParams(dimension_semantics=sem, vmem_limit_bytes=vmem_mb * MIB),
        **kw)


def _const(shape):
    nd = len(shape)
    return pl.BlockSpec(shape, lambda *_: (0,) * nd)


def _sds(shape, dt):
    return jax.ShapeDtypeStruct(shape, dt)


def _consts():
    seg_h = np.zeros((HP, LANE), np.float32)
    seg_h[np.arange(HP), np.arange(HP) // LANE] = 1.0
    seg_g = np.zeros((G * GD, LANE), np.float32)
    seg_g[np.arange(G * GD), np.arange(G * GD) // GD] = 1.0
    rot = np.zeros((LANE, LANE), np.float32)
    for base in (DN, DN + 16):
        for j in range(8):
            rot[base + j + 8, base + j] = -1.0
            rot[base + j, base + j + 8] = 1.0
    c = dict(seg_h=seg_h, seg_ht=seg_h.T, seg_g=seg_g, seg_gt=seg_g.T, rot=rot, rot_t=rot.T)
    return {k: jnp.asarray(v, BF16) for k, v in c.items()}


def _all_gather(x, name):
    r, c = x.shape

    def body(x_ref, out_ref, send_sems, recv_sems, local_sem):
        mx, my, mc = lax.axis_index("x"), lax.axis_index("y"), lax.axis_index("c")
        me, sibling = (mx, my, mc), (mx, my, 1 - mc)
        chips = [(1 - mx, my), (mx, 1 - my), (1 - mx, 1 - my)]

        def blk(px, py, pc):
            return out_ref.at[4 * px + 2 * py + pc]

        def copy(k, block, to, src=None):
            return pltpu.make_async_remote_copy(
                src_ref=blk(*block) if src is None else src, dst_ref=blk(*block),
                send_sem=send_sems.at[k], recv_sem=recv_sems.at[k], device_id=to, device_id_type=MESH)

        mine = pltpu.make_async_copy(x_ref, blk(*me), local_sem)
        mine.start()
        first = [copy(0, me, sibling, src=x_ref)]
        first += [copy(1 + j, me, (*chip, mc), src=x_ref) for j, chip in enumerate(chips)]
        for cp in first:
            cp.start()
        passed = [copy(4 + j, (*chip, mc), sibling) for j, chip in enumerate(chips)]
        for j, chip in enumerate(chips):
            copy(1 + j, (*chip, mc), me).wait_recv()
            passed[j].start()
        copy(0, sibling, me).wait_recv()
        for j, chip in enumerate(chips):
            copy(4 + j, (*chip, 1 - mc), me).wait_recv()
        for cp in first + passed:
            cp.wait_send()
        mine.wait()

    return pl.pallas_call(
        body, name=name, out_shape=_sds((NDEV, r, c), x.dtype), in_specs=[ANY], out_specs=ANY,
        scratch_shapes=[pltpu.SemaphoreType.DMA((7,)), pltpu.SemaphoreType.DMA((7,)), pltpu.SemaphoreType.DMA(())],
    )(x)


def _scatter_sibling(xs, name):
    n = len(xs)

    def body(*refs):
        x_refs, got_refs = refs[:n], refs[n:2 * n]
        send_sems, recv_sems = refs[2 * n:]
        mx, my, mc = lax.axis_index("x"), lax.axis_index("y"), lax.axis_index("c")
        sibling = (mx, my, 1 - mc)
        remote = []
        for i in range(n):
            for j in range(4):
                k = 4 * i + j
                remote.append(pltpu.make_async_remote_copy(
                    src_ref=x_refs[i].at[2 * j + 1 - mc], dst_ref=got_refs[i].at[j],
                    send_sem=send_sems.at[k], recv_sem=recv_sems.at[k], device_id=sibling, device_id_type=MESH))
        for cp in remote:
            cp.start()
        for cp in remote:
            cp.wait_recv()
        for cp in remote:
            cp.wait_send()

    shapes = tuple(_sds((4,) + x.shape[1:], x.dtype) for x in xs)
    return pl.pallas_call(
        body, name=name, out_shape=shapes, in_specs=[ANY] * n, out_specs=(ANY,) * n,
        scratch_shapes=[pltpu.SemaphoreType.DMA((4 * n,)), pltpu.SemaphoreType.DMA((4 * n,))],
    )(*xs)


def _scatter_chips(ps, name):
    n = len(ps)

    def body(*refs):
        p_refs, out_refs = refs[:n], refs[n:2 * n]
        send_sems, recv_sems = refs[2 * n:]
        mx, my, mc = lax.axis_index("x"), lax.axis_index("y"), lax.axis_index("c")
        peers = [(1 - mx, my), (mx, 1 - my), (1 - mx, 1 - my)]
        sends = []
        for i in range(n):
            for j, (px, py) in enumerate(peers):
                k = 3 * i + j
                sends.append(pltpu.make_async_remote_copy(
                    src_ref=p_refs[i].at[2 * px + py], dst_ref=out_refs[i].at[j],
                    send_sem=send_sems.at[k], recv_sem=recv_sems.at[k], device_id=(px, py, mc), device_id_type=MESH))
        for cp in sends:
            cp.start()
        for cp in sends:
            cp.wait_recv()
        for cp in sends:
            cp.wait_send()

    shapes = tuple(_sds((3,) + p.shape[1:], p.dtype) for p in ps)
    return pl.pallas_call(
        body, name=name, out_shape=shapes, in_specs=[ANY] * n, out_specs=(ANY,) * n,
        scratch_shapes=[pltpu.SemaphoreType.DMA((3 * n,)), pltpu.SemaphoreType.DMA((3 * n,))],
    )(*ps)


def _add_sibling(x, got, tr, name):
    _, r, c = x.shape

    def body(x_ref, g_ref, o_ref):
        mc = lax.axis_index("c")
        for j in range(4):
            mine = jnp.where(mc == 0, x_ref[2 * j].astype(F32), x_ref[2 * j + 1].astype(F32))
            o_ref[j] = (mine + g_ref[j].astype(F32)).astype(o_ref.dtype)

    return _pcall(body, name=name, grid=(r // tr,), out_shape=_sds(got.shape, got.dtype),
                  in_specs=[pl.BlockSpec((NDEV, tr, c), lambda t: (0, t, 0)), pl.BlockSpec((4, tr, c), lambda t: (0, t, 0))],
                  out_specs=pl.BlockSpec((4, tr, c), lambda t: (0, t, 0)))(x, got)


def _sum_chips(part, recv, tr, name):
    _, r, c = part.shape

    def body(p_ref, r_ref, o_ref):
        slot = 2 * lax.axis_index("x") + lax.axis_index("y")
        acc = p_ref[0].astype(F32)
        for j in range(1, 4):
            acc = jnp.where(slot == j, p_ref[j].astype(F32), acc)
        for j in range(3):
            acc = acc + r_ref[j].astype(F32)
        o_ref[...] = acc

    return _pcall(body, name=name, grid=(r // tr,), out_shape=_sds((r, c), F32),
                  in_specs=[pl.BlockSpec((4, tr, c), lambda t: (0, t, 0)), pl.BlockSpec((3, tr, c), lambda t: (0, t, 0))],
                  out_specs=pl.BlockSpec((tr, c), lambda t: (t, 0)))(part, recv)


def _sum_slots(x, tr, name):
    n, r, c = x.shape

    def body(x_ref, o_ref):
        acc = x_ref[0].astype(F32)
        for s in range(1, n):
            acc = acc + x_ref[s].astype(F32)
        o_ref[...] = acc

    return _pcall(body, name=name, grid=(r // tr,), out_shape=_sds((r, c), F32),
                  in_specs=[pl.BlockSpec((n, tr, c), lambda t: (0, t, 0))],
                  out_specs=pl.BlockSpec((tr, c), lambda t: (t, 0)))(x)


def _ada_fwd(a_raw, w_loc, b_loc):
    ncol = w_loc.shape[1]

    def body(a_ref, w_ref, b_ref, o_ref):
        a = a_ref[...]
        act = (a * _sigmoid(a)).astype(BF16)
        o_ref[...] = _mm(act, w_ref[...].astype(BF16)) + b_ref[...]

    return _pcall(body, name="ada_fwd", out_shape=_sds((a_raw.shape[0], ncol), F32),
                  in_specs=[VMEM] * 3, out_specs=VMEM)(a_raw, w_loc, b_loc)


def _ada_bwd(a_raw, cctx_col, g_all, g_cols, w_loc, nb):
    nrow = a_raw.shape[0]
    ncol = w_loc.shape[1]

    def body(a_ref, cc_ref, gall_ref, g_ref, w_ref, dw_ref, pc_ref, gb_ref):
        a = a_ref[...]
        rowid = lax.broadcasted_iota(jnp.int32, (nrow, 1), 0) % 8
        act = jnp.where(rowid < nb, a * _sigmoid(a), 0.0).astype(BF16)
        g = g_ref[...]
        gc = _rowsum(jnp.where(rowid == nb, g, 0.0))
        cc = cc_ref[...]
        dw_ref[...] = _mm_tn(act, g.astype(BF16)) + (cc * _sigmoid(cc)) * gc
        pc_ref[...] = jnp.sum(w_ref[...] * gc, axis=1, keepdims=True)
        gb_ref[...] = _rowsum(gall_ref[...])

    return _pcall(body, name="ada_bwd",
                  out_shape=(_sds((D, ncol), F32), _sds((D, 1), F32), _sds((1, g_all.shape[1]), F32)),
                  in_specs=[VMEM] * 5, out_specs=(VMEM,) * 3, vmem_mb=48)(a_raw, cctx_col, g_all, g_cols, w_loc)


def _mod_spec(k, tpe, nrows):
    return pl.BlockSpec((1, k, D), lambda t: (jnp.minimum(t // tpe, nrows - 1), 0, 0))


def _load_ffn_weights(wall_ref, first, bufs, sems):
    fsh = FF // NDEV
    cps = []
    for j, buf in enumerate(bufs):
        for d in range(NDEV):
            cps.append(pltpu.make_async_copy(wall_ref.at[d, pl.ds((first + j) * fsh, fsh)],
                                             buf.at[pl.ds(d * fsh, fsh)], sems.at[j * NDEV + d]))
    for cp in cps:
        cp.start()
    for cp in cps:
        cp.wait()


def _token_specs(xs, tm, n_lat):
    specs = [pl.BlockSpec((tm, D), lambda t: (jnp.minimum(t, n_lat - 1), 0))]
    if len(xs) == 2:
        specs.append(pl.BlockSpec((tm, D), lambda t: (jnp.maximum(t - n_lat, 0), 0)))
    return specs


def _ffn_fwd(xs, mod3, norm_w, wall, first, *, tm, n_tiles, tpe, n_lat, name, target=None):
    nrows = mod3.shape[0]
    r = n_tiles * tm
    nx = len(xs)
    with_loss = target is not None

    def body(*refs):
        x_refs = refs[:nx]
        pos = nx
        if with_loss:
            tgt_ref = refs[pos]
            pos += 1
        mod_ref, nw_ref, wall_ref = refs[pos:pos + 3]
        pos += 3
        xo_ref, a_ref, b_ref, o_ref = refs[pos:pos + 4]
        pos += 4
        if with_loss:
            ls_ref = refs[pos]
            pos += 1
        w1_ref, w3_ref, w2_ref, wsem, acc_ref = refs[pos:]
        t = pl.program_id(0)

        @pl.when(t == 0)
        def _():
            _load_ffn_weights(wall_ref, first, (w1_ref, w3_ref, w2_ref), wsem)
            if with_loss:
                ls_ref[...] = jnp.zeros_like(ls_ref)

        x = x_refs[0][...]
        if nx == 2:
            x = jnp.where(t < n_lat, x, x_refs[1][...])
        n = x * _rms(x) * nw_ref[...]
        shift, scale, gate = mod_ref[0, 0:1, :], mod_ref[0, 1:2, :], mod_ref[0, 2:3, :]
        h = (n * (1.0 + scale) + shift).astype(BF16)
        for j in range(FF // FC):
            sl = slice(j * FC, (j + 1) * FC)
            a = _mm_nt(h, w1_ref[sl, :])
            b = _mm_nt(h, w3_ref[sl, :])
            a_ref[:, sl] = a.astype(BF16)
            b_ref[:, sl] = b.astype(BF16)
            g = (a * _sigmoid(a) * b).astype(BF16)
            part = _mm(g, w2_ref[sl, :])
            if j == 0:
                acc_ref[...] = part
            else:
                acc_ref[...] += part
        o = acc_ref[...]
        o_ref[...] = o.astype(BF16)
        out = x + (0.5 * gate) * o
        if with_loss:
            d = out - tgt_ref[...]
            xo_ref[...] = d * (1.0 / D)
            ls_ref[...] += jnp.sum(d * d)
        else:
            xo_ref[...] = out

    row = lambda cols: pl.BlockSpec((tm, cols), lambda t: (t, 0))
    in_specs = _token_specs(xs, tm, n_lat) + ([row(D)] if with_loss else []) + [
        _mod_spec(3, tpe, nrows), _const((1, D)), ANY]
    out_shape = [_sds((r, D), F32), _sds((r, FF), BF16), _sds((r, FF), BF16), _sds((r, D), BF16)]
    out_specs = [row(D), row(FF), row(FF), row(D)]
    if with_loss:
        out_shape.append(_sds((8, LANE), F32))
        out_specs.append(_const((8, LANE)))
    args = list(xs) + ([target] if with_loss else []) + [mod3, norm_w, wall]
    return _pcall(
        body, name=name, grid=(n_tiles,), out_shape=tuple(out_shape), in_specs=in_specs, out_specs=tuple(out_specs),
        scratch=[pltpu.VMEM((FF, D), BF16)] * 3 + [pltpu.SemaphoreType.DMA((3 * NDEV,)), pltpu.VMEM((tm, D), F32)],
        vmem_mb=56)(*args)


def _ffn_bwd_dx(dout, xs, a, b, o, mod3, norm_w, wall, first, *, tm, n_tiles, tpe, n_lat, name):
    nrows = mod3.shape[0]
    r = n_tiles * tm
    nx = len(xs)

    def body(*refs):
        dout_ref = refs[0]
        x_refs = refs[1:1 + nx]
        (a_ref, b_ref, o_ref, mod_ref, nw_ref, wall_ref,
         dx_ref, da_ref, db_ref, g_ref, do_ref, h_ref, dmod_ref, dnw_ref,
         w1_ref, w3_ref, w2_ref, wsem, acc_ref) = refs[1 + nx:]
        t = pl.program_id(0)

        @pl.when(t == 0)
        def _():
            _load_ffn_weights(wall_ref, first, (w1_ref, w3_ref, w2_ref), wsem)
            dnw_ref[...] = jnp.zeros_like(dnw_ref)

        x = x_refs[0][...]
        if nx == 2:
            x = jnp.where(t < n_lat, x, x_refs[1][...])
        dout = dout_ref[...]
        rr = _rms(x)
        xh = x * rr
        nw = nw_ref[...]
        n = xh * nw
        shift, scale, gate = mod_ref[0, 0:1, :], mod_ref[0, 1:2, :], mod_ref[0, 2:3, :]
        h = (n * (1.0 + scale) + shift).astype(BF16)
        h_ref[...] = h
        d_o = ((0.5 * gate) * dout).astype(BF16)
        do_ref[...] = d_o
        dgate = _rowsum(0.5 * o_ref[...].astype(F32) * dout)
        for j in range(FF // FC):
            sl = slice(j * FC, (j + 1) * FC)
            av = a_ref[:, sl].astype(F32)
            bv = b_ref[:, sl].astype(F32)
            dg = _mm_nt(d_o, w2_ref[sl, :])
            sig = _sigmoid(av)
            sa = av * sig
            g_ref[:, sl] = (sa * bv).astype(BF16)
            da = (dg * bv * (sig * (1.0 + av * (1.0 - sig)))).astype(BF16)
            db = (dg * sa).astype(BF16)
            da_ref[:, sl] = da
            db_ref[:, sl] = db
            part = _mm(da, w1_ref[sl, :]) + _mm(db, w3_ref[sl, :])
            if j == 0:
                acc_ref[...] = part
            else:
                acc_ref[...] += part
        dh = acc_ref[...]
        dn = dh * (1.0 + scale)
        dxh = dn * nw

        @pl.when(t < n_lat)
        def _():
            dx_ref[...] = dout + rr * (dxh - xh * jnp.mean(dxh * xh, axis=-1, keepdims=True))

        first_visit = jnp.where(t < n_lat, t % tpe == 0, t == n_lat)

        @pl.when(first_visit)
        def _():
            dmod_ref[...] = jnp.zeros_like(dmod_ref)

        dmod_ref[0, 0:1, :] += _rowsum(dh)
        dmod_ref[0, 1:2, :] += _rowsum(dh * n)
        dmod_ref[0, 2:3, :] += dgate
        dnw_ref[...] += _rowsum(dn * xh)

    row = lambda cols: pl.BlockSpec((tm, cols), lambda t: (t, 0))
    lat = pl.BlockSpec((tm, D), lambda t: (jnp.minimum(t, n_lat - 1), 0))
    return _pcall(
        body, name=name, grid=(n_tiles,),
        out_shape=(_sds((n_lat * tm, D), F32), _sds((r, FF), BF16), _sds((r, FF), BF16), _sds((r, FF), BF16),
                   _sds((r, D), BF16), _sds((r, D), BF16), _sds((nrows, 3, D), F32), _sds((1, D), F32)),
        in_specs=[row(D)] + _token_specs(xs, tm, n_lat) + [row(FF), row(FF), row(D), _mod_spec(3, tpe, nrows),
                                                            _const((1, D)), ANY],
        out_specs=(lat, row(FF), row(FF), row(FF), row(D), row(D), _mod_spec(3, tpe, nrows), _const((1, D))),
        scratch=[pltpu.VMEM((FF, D), BF16)] * 3 + [pltpu.SemaphoreType.DMA((3 * NDEV,)), pltpu.VMEM((tm, D), F32)],
        vmem_mb=60)(dout, *xs, a, b, o, mod3, norm_w, wall)


def _ffn_bwd_dw(h, d_o, da, db, g, gpack, slot, *, tr, name):
    r = h.shape[0]
    fh = FF // 2
    fsh = FF // NDEV
    nk = r // tr

    def body(h_ref, do_ref, da_ref, db_ref, g_ref, gin_ref, out_ref, acc1, acc3, acc2):
        k = pl.program_id(1)

        @pl.when(k == 0)
        def _():
            acc1[...] = jnp.zeros_like(acc1)
            acc3[...] = jnp.zeros_like(acc3)
            acc2[...] = jnp.zeros_like(acc2)

        hv = h_ref[...]
        acc1[...] += _mm_tn(da_ref[...], hv)
        acc3[...] += _mm_tn(db_ref[...], hv)
        acc2[...] += _mm_tn(g_ref[...], do_ref[...])

        @pl.when(k == nk - 1)
        def _():
            for i, acc in enumerate((acc1, acc3, acc2)):
                out_ref[:, i * fsh:(i + 1) * fsh, :] = acc[...].reshape(NDEV // 2, fsh, D).astype(BF16)

    rowd = pl.BlockSpec((tr, D), lambda f, k: (k, 0))
    rowf = pl.BlockSpec((tr, fh), lambda f, k: (k, f))
    return _pcall(
        body, name=name, grid=(2, nk), out_shape=_sds(gpack.shape, BF16),
        in_specs=[rowd, rowd, rowf, rowf, rowf, ANY],
        out_specs=pl.BlockSpec((NDEV // 2, 3 * fsh, D), lambda f, k: (f, slot, 0)),
        scratch=[pltpu.VMEM((fh, D), F32)] * 3, vmem_mb=56, aliases={5: 0})(h, d_o, da, db, g, gpack)


_PIECES = ((0, 128), (128, 384), (384, 896), (896, 1408), (1408, 1536))


def _proj_fwd(x1, mod2, norm_w, wint, *, tm, n_tiles, tpe, name="proj_fwd"):
    nrows = mod2.shape[0]
    r = n_tiles * tm

    def body(x_ref, mod_ref, nw_ref, w_ref, ckv_ref, q_ref, u_ref, v_ref, kpe_ref):
        x = x_ref[...]
        n = x * _rms(x) * nw_ref[...]
        h = (n * (1.0 + mod_ref[0, 1:2, :]) + mod_ref[0, 0:1, :]).astype(BF16)
        for (lo, hi), ref in zip(_PIECES, (ckv_ref, q_ref, u_ref, v_ref, kpe_ref)):
            ref[...] = _mm_nt(h, w_ref[lo:hi, :])

    row = lambda cols: pl.BlockSpec((tm, cols), lambda t: (t, 0))
    widths = [hi - lo for lo, hi in _PIECES]
    return _pcall(
        body, name=name, grid=(n_tiles,),
        out_shape=tuple(_sds((r, w), F32) for w in widths),
        in_specs=[row(D), _mod_spec(2, tpe, nrows), _const((1, D)), _const((WIN_ROWS, D))],
        out_specs=tuple(row(w) for w in widths), vmem_mb=40)(x1, mod2, norm_w, wint)


def _proj_bwd(dckv, dkpe, dq, du, dv, dx2, x1, mod2, norm_w, wint, *, tm, n_tiles, tpe, n_lat, name="proj_bwd"):
    nrows = mod2.shape[0]
    r = n_tiles * tm

    def body(dckv_ref, dkpe_ref, dq_ref, du_ref, dv_ref, dx2_ref, x_ref, mod_ref, nw_ref, w_ref,
             dx_ref, dw_ref, dmod_ref, dnw_ref, acc_ref):
        t = pl.program_id(0)
        is_lat = t < n_lat
        x = x_ref[...]
        rr = _rms(x)
        xh = x * rr
        nw = nw_ref[...]
        n = xh * nw
        scale = mod_ref[0, 1:2, :]
        h = (n * (1.0 + scale) + mod_ref[0, 0:1, :]).astype(BF16)

        @pl.when(t == 0)
        def _():
            dw_ref[...] = jnp.zeros_like(dw_ref)
            dnw_ref[...] = jnp.zeros_like(dnw_ref)

        dckv_v, dkpe_v = dckv_ref[...], dkpe_ref[...]
        acc_ref[...] = _mm(dckv_v, w_ref[0:128, :]) + _mm(dkpe_v, w_ref[1408:1536, :])
        dw_ref[0:128, :] += _mm_tn(dckv_v, h)
        dw_ref[1408:1536, :] += _mm_tn(dkpe_v, h)

        @pl.when(is_lat)
        def _():
            dq_v, du_v, dv_v = dq_ref[...], du_ref[...], dv_ref[...]
            acc_ref[...] += (_mm(dq_v, w_ref[128:384, :]) + _mm(du_v, w_ref[384:896, :])
                             + _mm(dv_v, w_ref[896:1408, :]))
            dw_ref[128:384, :] += _mm_tn(dq_v, h)
            dw_ref[384:896, :] += _mm_tn(du_v, h)
            dw_ref[896:1408, :] += _mm_tn(dv_v, h)

        dh = acc_ref[...]
        dn = dh * (1.0 + scale)
        dxh = dn * nw
        dx = rr * (dxh - xh * jnp.mean(dxh * xh, axis=-1, keepdims=True))
        dx_ref[...] = dx + jnp.where(is_lat, dx2_ref[...], 0.0)

        first = jnp.where(is_lat, t % tpe == 0, t == n_lat)

        @pl.when(first)
        def _():
            dmod_ref[...] = jnp.zeros_like(dmod_ref)

        dmod_ref[0, 0:1, :] += _rowsum(dh)
        dmod_ref[0, 1:2, :] += _rowsum(dh * n)
        dnw_ref[...] += _rowsum(dn * xh)

    row = lambda cols: pl.BlockSpec((tm, cols), lambda t: (t, 0))
    lat = lambda cols: pl.BlockSpec((tm, cols), lambda t: (jnp.minimum(t, n_lat - 1), 0))
    return _pcall(
        body, name=name, grid=(n_tiles,),
        out_shape=(_sds((r, D), F32), _sds((WIN_ROWS, D), F32), _sds((nrows, 2, D), F32), _sds((1, D), F32)),
        in_specs=[row(128), row(128), lat(256), lat(512), lat(512), lat(D), row(D), _mod_spec(2, tpe, nrows),
                  _const((1, D)), _const((WIN_ROWS, D))],
        out_specs=(row(D), _const((WIN_ROWS, D)), _mod_spec(2, tpe, nrows), _const((1, D))),
        scratch=[pltpu.VMEM((tm, D), F32)], vmem_mb=48)(dckv, dkpe, dq, du, dv, dx2, x1, mod2, norm_w, wint)


def _head_norm_rope(x, w_pad, cos, sin, seg, segt, rot):
    rh = lax.rsqrt(_dot_hl(x * x, seg) * (1.0 / DH) + EPS)
    rb = _dot_hl(rh, segt)
    y = x * rb
    t = y * w_pad
    out = []
    for h in range(H):
        th = t[:, h * LANE:(h + 1) * LANE]
        out.append(th * cos + _dot_hl(th, rot) * sin)
    return jnp.concatenate(out, axis=-1), y, rb


def _head_norm_rope_bwd(dout, y, rb, w_pad, cos, sin, seg, segt, rot_t):
    dt = []
    for h in range(H):
        dh = dout[:, h * LANE:(h + 1) * LANE]
        dt.append(dh * cos + _dot_hl(dh * sin, rot_t))
    dt = jnp.concatenate(dt, axis=-1)
    dw = _rowsum(dt * y)
    dy = dt * w_pad
    mean_h = _dot_hl(dy * y, seg) * (1.0 / DH)
    return rb * (dy - y * _dot_hl(mean_h, segt)), dw


def _q_prep_fwd(qp, qa_w, wuq, wq, cos, sin, cs, *, tm, n_lat, tpe):
    def body(qp_ref, qa_ref, wuq_ref, wq_ref, cos_ref, sin_ref, seg, segt, rot, q_ref):
        x = qp_ref[...]
        cq = (x * _rms(x) * qa_ref[...]).astype(BF16)
        q, _, _ = _head_norm_rope(_mm_nt(cq, wuq_ref[...]), wq_ref[...], cos_ref[...], sin_ref[...],
                                  seg[...], segt[...], rot[...])
        q_ref[...] = q.astype(BF16)

    row = lambda cols: pl.BlockSpec((tm, cols), lambda t: (t, 0))
    tab = pl.BlockSpec((tm, LANE), lambda t: (t % tpe, 0))
    return _pcall(
        body, name="q_prep_fwd", grid=(n_lat,), out_shape=_sds((n_lat * tm, HP), BF16),
        in_specs=[row(QL), _const((1, QL)), _const((HP, QL)), _const((1, HP)), tab, tab,
                  _const((HP, LANE)), _const((LANE, HP)), _const((LANE, LANE))],
        out_specs=row(HP))(qp, qa_w, wuq, wq, cos, sin, cs["seg_h"], cs["seg_ht"], cs["rot"])


def _q_prep_bwd(dq, qp, qa_w, wuq, wq, cos, sin, cs, *, tm, n_lat, tpe):
    def body(dq_ref, qp_ref, qa_ref, wuq_ref, wq_ref, cos_ref, sin_ref, seg, segt, rot, rot_t,
             dqp_ref, dwuq_ref, dqa_ref, dwq_ref):
        t = pl.program_id(0)
        x = qp_ref[...]
        ra = _rms(x)
        xh = x * ra
        qa = qa_ref[...]
        cq = (xh * qa).astype(BF16)
        wuq_v = wuq_ref[...]
        wq_v, cos_v, sin_v = wq_ref[...], cos_ref[...], sin_ref[...]
        _, y, rb = _head_norm_rope(_mm_nt(cq, wuq_v), wq_v, cos_v, sin_v, seg[...], segt[...], rot[...])
        dqraw, dwq = _head_norm_rope_bwd(dq_ref[...], y, rb, wq_v, cos_v, sin_v, seg[...], segt[...], rot_t[...])
        dqraw = dqraw.astype(BF16)
        dcq = _mm(dqraw, wuq_v)
        dxh = dcq * qa
        dqp_ref[...] = (ra * (dxh - xh * jnp.mean(dxh * xh, axis=-1, keepdims=True))).astype(BF16)

        @pl.when(t == 0)
        def _():
            dwuq_ref[...] = jnp.zeros_like(dwuq_ref)
            dqa_ref[...] = jnp.zeros_like(dqa_ref)
            dwq_ref[...] = jnp.zeros_like(dwq_ref)

        dwuq_ref[...] += _mm_tn(dqraw, cq)
        dqa_ref[...] += _rowsum(dcq * xh)
        dwq_ref[...] += dwq

    row = lambda cols: pl.BlockSpec((tm, cols), lambda t: (t, 0))
    tab = pl.BlockSpec((tm, LANE), lambda t: (t % tpe, 0))
    return _pcall(
        body, name="q_prep_bwd", grid=(n_lat,),
        out_shape=(_sds((n_lat * tm, QL), BF16), _sds((HP, QL), F32), _sds((1, QL), F32), _sds((1, HP), F32)),
        in_specs=[row(HP), row(QL), _const((1, QL)), _const((HP, QL)), _const((1, HP)), tab, tab,
                  _const((HP, LANE)), _const((LANE, HP)), _const((LANE, LANE)), _const((LANE, LANE))],
        out_specs=(row(QL), _const((HP, QL)), _const((1, QL)), _const((1, HP))), vmem_mb=40)(
            dq, qp, qa_w, wuq, wq, cos, sin, cs["seg_h"], cs["seg_ht"], cs["rot"], cs["rot_t"])


def _kv_tab_spec(tm, tpe, n_lat):
    return pl.BlockSpec((tm, LANE), lambda t: (jnp.where(t < n_lat, t % tpe, tpe), 0))


def _kv_prep_fwd(ckv, kpe, kva_w, wukv, wk, cosk, sink, cs, *, tm, n_tiles, tpe, n_lat):
    def body(ckv_ref, kpe_ref, kva_ref, wukv_ref, wk_ref, cos_ref, sin_ref, seg, segt, rot, k_ref, v_ref):
        x = ckv_ref[...]
        ckvn = (x * _rms(x) * kva_ref[...]).astype(BF16)
        kv = _mm_nt(ckvn, wukv_ref[...])
        kx = kv[:, :HP] + jnp.concatenate([kpe_ref[...]] * H, axis=-1)
        k, _, _ = _head_norm_rope(kx, wk_ref[...], cos_ref[...], sin_ref[...], seg[...], segt[...], rot[...])
        k_ref[...] = k.astype(BF16)
        v_ref[...] = kv[:, HP:].astype(BF16)

    row = lambda cols: pl.BlockSpec((tm, cols), lambda t: (t, 0))
    tab = _kv_tab_spec(tm, tpe, n_lat)
    r = n_tiles * tm
    return _pcall(
        body, name="kv_prep_fwd", grid=(n_tiles,), out_shape=(_sds((r, HP), BF16), _sds((r, HP), BF16)),
        in_specs=[row(KVL), row(LANE), _const((1, KVL)), _const((2 * HP, KVL)), _const((1, HP)), tab, tab,
                  _const((HP, LANE)), _const((LANE, HP)), _const((LANE, LANE))],
        out_specs=(row(HP), row(HP)), vmem_mb=40)(
            ckv, kpe, kva_w, wukv, wk, cosk, sink, cs["seg_h"], cs["seg_ht"], cs["rot"])


def _kv_prep_bwd(dks, dvs, ckv, kpe, kva_w, wukv, wk, cosk, sink, cs, *, tm, n_tiles, tpe, n_lat):
    def body(dkl_ref, dkc_ref, dvl_ref, dvc_ref, ckv_ref, kpe_ref, kva_ref, wukv_ref, wk_ref, cos_ref, sin_ref,
             seg, segt, rot, rot_t, dckv_ref, dkpe_ref, dwukv_ref, dkva_ref, dwk_ref):
        t = pl.program_id(0)
        is_lat = t < n_lat
        dk = jnp.where(is_lat, dkl_ref[...], dkc_ref[...])
        dv = jnp.where(is_lat, dvl_ref[...], dvc_ref[...])
        x = ckv_ref[...]
        ra = _rms(x)
        xh = x * ra
        kva = kva_ref[...]
        ckvn = (xh * kva).astype(BF16)
        wukv_v = wukv_ref[...]
        wk_v, cos_v, sin_v = wk_ref[...], cos_ref[...], sin_ref[...]
        kv = _mm_nt(ckvn, wukv_v)
        kx = kv[:, :HP] + jnp.concatenate([kpe_ref[...]] * H, axis=-1)
        _, y, rb = _head_norm_rope(kx, wk_v, cos_v, sin_v, seg[...], segt[...], rot[...])
        dkx, dwk = _head_norm_rope_bwd(dk, y, rb, wk_v, cos_v, sin_v, seg[...], segt[...], rot_t[...])
        dkpe = dkx[:, 0:LANE]
        for h in range(1, H):
            dkpe = dkpe + dkx[:, h * LANE:(h + 1) * LANE]
        lane = lax.broadcasted_iota(jnp.int32, (tm, LANE), 1)
        dkpe_ref[...] = jnp.where((lane >= DN) & (lane < DH), dkpe, 0.0).astype(BF16)
        dkv = jnp.concatenate([dkx, dv], axis=-1).astype(BF16)
        dckvn = _mm(dkv, wukv_v)
        dxh = dckvn * kva
        dckv_ref[...] = (ra * (dxh - xh * jnp.mean(dxh * xh, axis=-1, keepdims=True))).astype(BF16)

        @pl.when(t == 0)
        def _():
            dwukv_ref[...] = jnp.zeros_like(dwukv_ref)
            dkva_ref[...] = jnp.zeros_like(dkva_ref)
            dwk_ref[...] = jnp.zeros_like(dwk_ref)

        dwukv_ref[...] += _mm_tn(dkv, ckvn)
        dkva_ref[...] += _rowsum(dckvn * xh)
        dwk_ref[...] += dwk

    row = lambda cols: pl.BlockSpec((tm, cols), lambda t: (t, 0))
    lat = pl.BlockSpec((tm, HP), lambda t: (jnp.minimum(t, n_lat - 1), 0))
    ctx = pl.BlockSpec((tm, HP), lambda t: (jnp.maximum(t - n_lat, 0), 0))
    tab = _kv_tab_spec(tm, tpe, n_lat)
    r = n_tiles * tm
    return _pcall(
        body, name="kv_prep_bwd", grid=(n_tiles,),
        out_shape=(_sds((r, KVL), BF16), _sds((r, LANE), BF16), _sds((2 * HP, KVL), F32), _sds((1, KVL), F32),
                   _sds((1, HP), F32)),
        in_specs=[lat, ctx, lat, ctx, row(KVL), row(LANE), _const((1, KVL)), _const((2 * HP, KVL)), _const((1, HP)),
                  tab, tab, _const((HP, LANE)), _const((LANE, HP)), _const((LANE, LANE)), _const((LANE, LANE))],
        out_specs=(row(KVL), row(LANE), _const((2 * HP, KVL)), _const((1, KVL)), _const((1, HP))), vmem_mb=48)(
            dks[0], dks[1], dvs[0], dvs[1], ckv, kpe, kva_w, wukv, wk, cosk, sink,
            cs["seg_h"], cs["seg_ht"], cs["rot"], cs["rot_t"])


_SCALE = DH ** -0.5


def _attn_specs(tq, s, nc, tpe, n_lat_rows):
    qs = pl.BlockSpec((tq, LANE), lambda i, j, t: (i * tpe + t, j))
    kl = pl.BlockSpec((s, LANE), lambda i, j, t: (i, j))
    kc = pl.BlockSpec((nc, LANE), lambda i, j, t: (n_lat_rows // nc + i, j))
    return qs, kl, kc


def _softmax_parts(q, kl, kc):
    s1 = _mm_nt(q, kl) * _SCALE
    s2 = _mm_nt(q, kc) * _SCALE
    m = jnp.maximum(jnp.max(s1, axis=-1, keepdims=True), jnp.max(s2, axis=-1, keepdims=True))
    e1 = jnp.exp(s1 - m)
    e2 = jnp.exp(s2 - m)
    l = jnp.sum(e1, axis=-1, keepdims=True) + jnp.sum(e2, axis=-1, keepdims=True)
    return e1, e2, l


def _attn_fwd(q, k, v, *, nb, s, nc, tq):
    tpe = s // tq
    r_lat = nb * s

    def body(q_ref, kl_ref, kc_ref, vl_ref, vc_ref, o_ref):
        e1, e2, l = _softmax_parts(q_ref[...], kl_ref[...], kc_ref[...])
        o = _mm(e1.astype(BF16), vl_ref[...]) + _mm(e2.astype(BF16), vc_ref[...])
        o_ref[...] = (o / l).astype(BF16)

    qs, kl, kc = _attn_specs(tq, s, nc, tpe, r_lat)
    return _pcall(body, name="attn_fwd", grid=(nb, H, tpe), out_shape=_sds((r_lat, HP), BF16),
                  in_specs=[qs, kl, kc, kl, kc], out_specs=qs, vmem_mb=48)(q, k, k, v, v)


def _attn_bwd(q, k, v, o, do, *, nb, s, nc, tq):
    tpe = s // tq
    r_lat = nb * s

    def body(q_ref, kl_ref, kc_ref, vl_ref, vc_ref, o_ref, do_ref,
             dq_ref, dkl_ref, dkc_ref, dvl_ref, dvc_ref, akl, akc, avl, avc):
        t = pl.program_id(2)
        qv, klv, kcv = q_ref[...], kl_ref[...], kc_ref[...]
        e1, e2, l = _softmax_parts(qv, klv, kcv)
        inv = 1.0 / l
        p1, p2 = e1 * inv, e2 * inv
        dov = do_ref[...]
        delta = jnp.sum(dov.astype(F32) * o_ref[...].astype(F32), axis=-1, keepdims=True)
        ds1 = (p1 * (_mm_nt(dov, vl_ref[...]) - delta) * _SCALE).astype(BF16)
        ds2 = (p2 * (_mm_nt(dov, vc_ref[...]) - delta) * _SCALE).astype(BF16)
        dq_ref[...] = _mm(ds1, klv) + _mm(ds2, kcv)

        @pl.when(t == 0)
        def _():
            akl[...] = jnp.zeros_like(akl)
            akc[...] = jnp.zeros_like(akc)
            avl[...] = jnp.zeros_like(avl)
            avc[...] = jnp.zeros_like(avc)

        akl[...] += _mm_tn(qv, ds1)
        akc[...] += _mm_tn(qv, ds2)
        avl[...] += _mm_tn(dov, p1.astype(BF16))
        avc[...] += _mm_tn(dov, p2.astype(BF16))

        @pl.when(t == tpe - 1)
        def _():
            dkl_ref[...] = akl[...].T
            dkc_ref[...] = akc[...].T
            dvl_ref[...] = avl[...].T
            dvc_ref[...] = avc[...].T

    qs, kl, kc = _attn_specs(tq, s, nc, tpe, r_lat)
    kc_out = pl.BlockSpec((nc, LANE), lambda i, j, t: (i, j))
    return _pcall(
        body, name="attn_bwd", grid=(nb, H, tpe),
        out_shape=(_sds((r_lat, HP), F32), _sds((r_lat, HP), F32), _sds((nb * nc, HP), F32),
                   _sds((r_lat, HP), F32), _sds((nb * nc, HP), F32)),
        in_specs=[qs, kl, kc, kl, kc, qs, qs], out_specs=(qs, kl, kc_out, kl, kc_out),
        scratch=[pltpu.VMEM((LANE, s), F32), pltpu.VMEM((LANE, nc), F32)] * 2, vmem_mb=56)(q, k, k, v, v, o, do)


def _gating(vn, ws_ref, bias_ref, s_scr, tm):
    lane = lax.broadcasted_iota(jnp.int32, (CH, LANE), 1)
    for c in range(tm // CH):
        rs = slice(c * CH, (c + 1) * CH)
        for j in range(G // 2):
            ls = slice(j * LANE, (j + 1) * LANE)
            vp = vn[rs, ls]
            s_scr[rs, ls] = jnp.where(lane < GD, _mm(ws_ref[2 * j], vp), _mm(ws_ref[2 * j + 1], vp)) + bias_ref[:, ls]


def _mix_fwd(u, v, attn, x1, gate, wv, ws, bias, wout, cs, *, tm, n_lat, tpe):
    nrows = gate.shape[0]

    def body(u_ref, v_ref, attn_ref, x_ref, gate_ref, wv_ref, ws_ref, bias_ref, wout_ref, seg, segt,
             x2_ref, mix_ref, s_scr):
        vg = _gelu(v_ref[...])
        rg = lax.rsqrt(_dot_hl(vg * vg, seg[...]) * (1.0 / GD) + EPS)
        vn = (vg * _dot_hl(rg, segt[...]) * wv_ref[...]).astype(BF16)
        _gating(vn, ws_ref, bias_ref, s_scr, tm)
        sg = (_gelu(u_ref[...]) * s_scr[...]).astype(BF16)
        mix = _mm(attn_ref[...], wout_ref[0:HP, :]) + _mm(sg, wout_ref[HP:, :])
        mix_ref[...] = mix.astype(BF16)
        x2_ref[...] = x_ref[...] + gate_ref[0] * mix

    row = lambda cols: pl.BlockSpec((tm, cols), lambda t: (t, 0))
    r = n_lat * tm
    return _pcall(
        body, name="mix_fwd", grid=(n_lat,),
        out_shape=(_sds((r, D), F32), _sds((r, D), BF16)),
        in_specs=[row(G * GD), row(G * GD), row(HP), row(D), _mod_spec(1, tpe, nrows), _const((1, G * GD)),
                  _const((G, CH, CH)), _const((CH, G * GD)), _const((HP + G * GD, D)), _const((G * GD, LANE)),
                  _const((LANE, G * GD))],
        out_specs=(row(D), row(D)), scratch=[pltpu.VMEM((tm, G * GD), F32)], vmem_mb=40)(
            u, v, attn, x1, gate, wv, ws, bias, wout, cs["seg_g"], cs["seg_gt"])


def _mix_bwd(dx2, mix, u, v, attn, gate, wv, ws, wst, bias, wout, cs, *, tm, n_lat, tpe):
    nrows = gate.shape[0]
    wrows = HP + G * GD

    def body(dx2_ref, mix_ref, u_ref, v_ref, attn_ref, gate_ref, wv_ref, ws_ref, wst_ref, bias_ref, wout_ref, seg, segt,
             dattn_ref, du_ref, dv_ref, dgate_ref, dwout_ref, dws_ref, dbs_ref, dwv_ref, s_scr, dvn_scr, dbias_scr):
        t = pl.program_id(0)
        dx2 = dx2_ref[...]
        dmix = (dx2 * gate_ref[0]).astype(BF16)
        dcat = _mm_nt(dmix, wout_ref[...])
        dattn_ref[...] = dcat[:, :HP].astype(BF16)
        dsg = dcat[:, HP:]

        vraw = v_ref[...]
        vg = _gelu(vraw)
        rg = lax.rsqrt(_dot_hl(vg * vg, seg[...]) * (1.0 / GD) + EPS)
        r64 = _dot_hl(rg, segt[...])
        y = vg * r64
        wv_v = wv_ref[...]
        vn = (y * wv_v).astype(BF16)
        _gating(vn, ws_ref, bias_ref, s_scr, tm)
        uraw = u_ref[...]
        ug = _gelu(uraw)
        s = s_scr[...]
        sg = (ug * s).astype(BF16)
        du_ref[...] = (dsg * s * _gelu_grad(uraw)).astype(BF16)
        ds = dsg * ug

        @pl.when(t == 0)
        def _():
            dwout_ref[...] = jnp.zeros_like(dwout_ref)
            dws_ref[...] = jnp.zeros_like(dws_ref)
            dwv_ref[...] = jnp.zeros_like(dwv_ref)
            dbias_scr[...] = jnp.zeros_like(dbias_scr)

        @pl.when(t % tpe == 0)
        def _():
            dgate_ref[...] = jnp.zeros_like(dgate_ref)

        dgate_ref[0] += _rowsum(dx2 * mix_ref[...].astype(F32))
        dwout_ref[...] += _mm_tn(jnp.concatenate([attn_ref[...], sg], axis=-1), dmix)

        lane = lax.broadcasted_iota(jnp.int32, (CH, LANE), 1)
        for c in range(tm // CH):
            rs = slice(c * CH, (c + 1) * CH)
            dbias_scr[...] += ds[rs, :]
            for j in range(G // 2):
                ls = slice(j * LANE, (j + 1) * LANE)
                dsp32 = ds[rs, ls]
                dsp = dsp32.astype(BF16)
                vp = vn[rs, ls]
                dvn_scr[rs, ls] = jnp.where(lane < GD, _mm(wst_ref[2 * j], dsp), _mm(wst_ref[2 * j + 1], dsp))
                dws_ref[2 * j] += _mm_nt(jnp.where(lane < GD, dsp32, 0.0).astype(BF16), vp)
                dws_ref[2 * j + 1] += _mm_nt(jnp.where(lane < GD, 0.0, dsp32).astype(BF16), vp)

        dvn = dvn_scr[...]
        dwv_ref[...] += _rowsum(dvn * y)
        dy = dvn * wv_v
        mean_g = _dot_hl(dy * y, seg[...]) * (1.0 / GD)
        dvg = r64 * (dy - y * _dot_hl(mean_g, segt[...]))
        dv_ref[...] = (dvg * _gelu_grad(vraw)).astype(BF16)

        @pl.when(t == n_lat - 1)
        def _():
            dbs_ref[...] = _dot_hl(dbias_scr[...], seg[...])

    row = lambda cols: pl.BlockSpec((tm, cols), lambda t: (t, 0))
    r = n_lat * tm
    return _pcall(
        body, name="mix_bwd", grid=(n_lat,),
        out_shape=(_sds((r, HP), BF16), _sds((r, G * GD), BF16), _sds((r, G * GD), BF16), _sds((nrows, 1, D), F32),
                   _sds((wrows, D), F32), _sds((G, CH, CH), F32), _sds((CH, LANE), F32), _sds((1, G * GD), F32)),
        in_specs=[row(D), row(D), row(G * GD), row(G * GD), row(HP), _mod_spec(1, tpe, nrows), _const((1, G * GD)),
                  _const((G, CH, CH)), _const((G, CH, CH)), _const((CH, G * GD)), _const((wrows, D)),
                  _const((G * GD, LANE)), _const((LANE, G * GD))],
        out_specs=(row(HP), row(G * GD), row(G * GD), _mod_spec(1, tpe, nrows), _const((wrows, D)),
                   _const((G, CH, CH)), _const((CH, LANE)), _const((1, G * GD))),
        scratch=[pltpu.VMEM((tm, G * GD), F32), pltpu.VMEM((tm, G * GD), F32), pltpu.VMEM((CH, G * GD), F32)],
        vmem_mb=56)(dx2, mix, u, v, attn, gate, wv, ws, wst, bias, wout, cs["seg_g"], cs["seg_gt"])


def _adamw_math(w, g, m, v):
    m2 = ADAM_B1 * m + (1.0 - ADAM_B1) * g
    v2 = ADAM_B2 * v + (1.0 - ADAM_B2) * (g * g)
    m_hat = m2 / (1.0 - ADAM_B1 ** ADAM_STEP)
    v_hat = v2 / (1.0 - ADAM_B2 ** ADAM_STEP)
    delta = -ADAM_LR * (m_hat / (jnp.sqrt(v_hat) + ADAM_EPS) + ADAM_WD * w)
    return delta, m2, v2


def _row_tile(r, c):
    best = r
    for tr in range(8, r, 8):
        if r % tr == 0 and tr * c * 4 <= MIB:
            best = tr
    return best


def _adamw(w, g, m, v, name):
    r, c = w.shape
    tr = _row_tile(r, c)

    def body(w_ref, g_ref, m_ref, v_ref, d_ref, mo_ref, vo_ref):
        d_ref[...], mo_ref[...], vo_ref[...] = _adamw_math(w_ref[...], g_ref[...], m_ref[...], v_ref[...])

    blk = pl.BlockSpec((tr, c), lambda t: (t, 0))
    return _pcall(body, name=name, grid=(r // tr,), out_shape=(_sds((r, c), F32),) * 3,
                  in_specs=[blk] * 4, out_specs=(blk,) * 3)(w, g, m, v)


def _adamw_small(params):
    n = len(params)

    def body(*refs):
        ins, outs = refs[:4 * n], refs[4 * n:]
        for i in range(n):
            w, g, m, v = (ins[4 * i + k][...] for k in range(4))
            if i == 0:
                sig = _sigmoid(w)
                g = g * (sig * (1.0 + w * (1.0 - sig)))
            d, m2, v2 = _adamw_math(w, g, m, v)
            outs[4 * i][...] = g
            outs[4 * i + 1][...] = d
            outs[4 * i + 2][...] = m2
            outs[4 * i + 3][...] = v2

    flat = [a for p in params for a in p]
    out_shape = tuple(_sds(p[0].shape, F32) for p in params for _ in range(4))
    res = _pcall(body, name="adamw_small", out_shape=out_shape, in_specs=[VMEM] * (4 * n),
                 out_specs=(VMEM,) * (4 * n))(*flat)
    return [res[4 * i:4 * i + 4] for i in range(n)]


def _rope_tables(s):
    rows = jnp.repeat(jnp.arange(s // GRID_W, dtype=F32), GRID_W)
    cols = jnp.tile(jnp.arange(GRID_W, dtype=F32), s // GRID_W)
    half = DR // 2
    inv = ROPE_BASE ** (-jnp.arange(0, half, 2, dtype=F32) / half)
    ang_r = rows[:, None] * inv
    ang_c = cols[:, None] * inv
    ang = jnp.concatenate([ang_r, ang_r, ang_c, ang_c], axis=-1)
    return jnp.cos(ang), jnp.sin(ang)


def _head_pad(a, real):
    return jnp.pad(a, ((0, 0), (0, LANE - real), (0, 0))).reshape(HP, a.shape[2])


def kernel(x, c, ctx, c_ctx, w_ada, b_ada, norm1_w, ffn1_w1, ffn1_w3, ffn1_w2, norm2_w, w_in, q_a_norm_w, w_uq, kv_a_norm_w, w_ukv, q_norm_w, k_norm_w, v_norm_w, w_s, b_s, w_out, norm3_w, ffn2_w1, ffn2_w3, ffn2_w2, loss_target, m_c_ctx, m_w_ada, m_b_ada, m_norm1_w, m_ffn1_w1, m_ffn1_w3, m_ffn1_w2, m_norm2_w, m_w_in, m_q_a_norm_w, m_w_uq, m_kv_a_norm_w, m_w_ukv, m_q_norm_w, m_k_norm_w, m_v_norm_w, m_w_s, m_b_s, m_w_out, m_norm3_w, m_ffn2_w1, m_ffn2_w3, m_ffn2_w2, v_c_ctx, v_w_ada, v_b_ada, v_norm1_w, v_ffn1_w1, v_ffn1_w3, v_ffn1_w2, v_norm2_w, v_w_in, v_q_a_norm_w, v_w_uq, v_kv_a_norm_w, v_w_ukv, v_q_norm_w, v_k_norm_w, v_v_norm_w, v_w_s, v_b_s, v_w_out, v_norm3_w, v_ffn2_w1, v_ffn2_w3, v_ffn2_w2):
    nb, s, _ = x.shape
    nc = ctx.shape[1]
    tm = 256 if nc % 256 == 0 else 128
    tpe = s // tm
    n_lat = nb * tpe
    n_all = n_lat + nb * nc // tm
    r_lat = nb * s
    me = 4 * lax.axis_index("x") + 2 * lax.axis_index("y") + lax.axis_index("c")
    cs = _consts()
    ncol = w_ada.shape[2]
    fsh = ffn1_w1.shape[2]
    assert nb + 1 <= 8 and NDEV * fsh == FF and NDEV * ncol == NMOD * D and s % nc == 0 and nc % tm == 0

    a_loc = jnp.concatenate([c, c_ctx[None, :], jnp.zeros((7 - nb, D), F32)], axis=0)
    a_raw = _all_gather(a_loc, "gather_c").reshape(NDEV * 8, D)
    mod_cols = _ada_fwd(a_raw, w_ada[0], lax.dynamic_slice_in_dim(b_ada, me * ncol, ncol, axis=1))
    mod_all = _all_gather(mod_cols, "gather_mod")
    mod_mine = lax.dynamic_slice_in_dim(mod_all, 8 * me, 8, axis=1)
    modtab = mod_mine.transpose(1, 0, 2).reshape(8, NMOD, D)[:nb + 1]

    def t16(a):
        return a.T.astype(BF16)

    wpack = jnp.concatenate([
        t16(ffn1_w1[0]), t16(ffn1_w3[0]), ffn1_w2[0].astype(BF16),
        t16(ffn2_w1[0]), t16(ffn2_w3[0]), ffn2_w2[0].astype(BF16),
        t16(w_in[0]), jnp.zeros((12, D), BF16),
        w_out[0].astype(BF16),
        t16(w_uq[0]).reshape(24, D), jnp.zeros((8, D), BF16),
        t16(w_ukv[0]).reshape(16, D)], axis=0)
    wall = _all_gather(wpack, "gather_weights")

    o0 = NFFN_W * fsh
    wint = wall[:, o0:o0 + 180].reshape(IN_COLS, D)
    z = lambda n: jnp.zeros((n, D), BF16)
    wint = jnp.concatenate([wint[0:128], wint[160:416], wint[416:928], wint[928:1440],
                            z(DN), wint[128:160], z(LANE - DH)], axis=0)
    wout = wall[:, o0 + 192:o0 + 320].reshape(D, D)
    wout = jnp.concatenate([_head_pad(wout[:H * DV].reshape(H, DV, D), DV), wout[H * DV:]], axis=0)
    wuq = _head_pad(wall[:, o0 + 320:o0 + 344].reshape(H, DH, QL), DH)
    wukvt = wall[:, o0 + 352:o0 + 368].reshape(H, DN + DV, KVL)
    wukv = jnp.concatenate([_head_pad(wukvt[:, :DN], DN), _head_pad(wukvt[:, DN:], DV)], axis=0)

    def head_w(wn):
        return jnp.tile(jnp.pad(wn, ((0, 0), (0, LANE - DH))), (1, H))

    wq, wk = head_w(q_norm_w), head_w(k_norm_w)
    wv = v_norm_w.reshape(1, G * GD)
    ws16 = w_s[0].astype(BF16)
    wst16 = w_s[0].transpose(0, 2, 1).astype(BF16)
    bias = jnp.repeat(b_s[0].T, GD, axis=1)
    cos, sin = _rope_tables(s)
    cos = jnp.pad(cos, ((0, 0), (DN, LANE - DH)), constant_values=1.0)
    sin = jnp.pad(sin, ((0, 0), (DN, LANE - DH)))
    cos_k = jnp.concatenate([cos, jnp.ones((tm, LANE), F32)], axis=0)
    sin_k = jnp.concatenate([sin, jnp.zeros((tm, LANE), F32)], axis=0)

    xs = (x.reshape(r_lat, D), ctx.reshape(nb * nc, D))
    tmf = 2 * tm if s % (2 * tm) == 0 and (nb * nc) % (2 * tm) == 0 else tm
    tq = 2 * tm if s % (2 * tm) == 0 else tm
    x1, a1, b1, o1 = _ffn_fwd(xs, modtab[:, 0:3], norm1_w, wall, 0, tm=tmf, n_tiles=(r_lat + nb * nc) // tmf,
                              tpe=s // tmf, n_lat=r_lat // tmf, name="ffn1_fwd")
    ckv, qp, u_raw, v_raw, kpe = _proj_fwd(x1, modtab[:, 3:5], norm2_w, wint, tm=tm, n_tiles=n_all, tpe=tpe)
    q = _q_prep_fwd(qp, q_a_norm_w, wuq, wq, cos, sin, cs, tm=tm, n_lat=n_lat, tpe=tpe)
    k, v = _kv_prep_fwd(ckv, kpe, kv_a_norm_w, wukv, wk, cos_k, sin_k, cs, tm=tm, n_tiles=n_all, tpe=tpe, n_lat=n_lat)
    attn = _attn_fwd(q, k, v, nb=nb, s=s, nc=nc, tq=tq)
    x2, mix = _mix_fwd(u_raw, v_raw, attn, x1, modtab[:nb, 5:6], wv, ws16, bias, wout, cs,
                       tm=tm, n_lat=n_lat, tpe=tpe)
    dy, a2, b2, o2, lsum = _ffn_fwd((x2,), modtab[:nb, 6:9], norm3_w, wall, 3, tm=tmf, n_tiles=r_lat // tmf,
                                    tpe=s // tmf, n_lat=r_lat // tmf, name="ffn2_fwd",
                                    target=loss_target.reshape(r_lat, D))
    loss = lax.psum(lsum[0, 0] * (0.5 / D), ("x", "y", "c"))

    tr = 2 * tm if n_lat % 2 == 0 and n_all % 2 == 0 else tm
    gpack = jnp.zeros((NDEV, NFFN_W * fsh, D), BF16)
    dx2, da2, db2, g2, do2, h2, dmod678, dnorm3 = _ffn_bwd_dx(
        dy, (x2,), a2, b2, o2, modtab[:nb, 6:9], norm3_w, wall, 3,
        tm=tm, n_tiles=n_lat, tpe=tpe, n_lat=n_lat, name="ffn2_bwd_dx")
    gpack = _ffn_bwd_dw(h2, do2, da2, db2, g2, gpack, 1, tr=tr, name="ffn2_bwd_dw")

    dattn, du, dv, dgate5, dwout, dws, dbs, dwv = _mix_bwd(
        dx2, mix, u_raw, v_raw, attn, modtab[:nb, 5:6], wv, ws16, wst16, bias, wout, cs, tm=tm, n_lat=n_lat, tpe=tpe)
    dq, dk_l, dk_c, dv_l, dv_c = _attn_bwd(q, k, v, attn, dattn, nb=nb, s=s, nc=nc, tq=tq)
    dqp, dwuq, dqa, dwq = _q_prep_bwd(dq, qp, q_a_norm_w, wuq, wq, cos, sin, cs, tm=tm, n_lat=n_lat, tpe=tpe)
    dckv, dkpe, dwukv, dkva, dwk = _kv_prep_bwd((dk_l, dk_c), (dv_l, dv_c), ckv, kpe, kv_a_norm_w, wukv, wk,
                                                cos_k, sin_k, cs, tm=tm, n_tiles=n_all, tpe=tpe, n_lat=n_lat)
    dx1, dwin, dmod34, dnorm2 = _proj_bwd(dckv, dkpe, dqp, du, dv, dx2, x1, modtab[:, 3:5], norm2_w, wint,
                                          tm=tm, n_tiles=n_all, tpe=tpe, n_lat=n_lat)
    dx0, da1, db1, g1, do1, h1, dmod012, dnorm1 = _ffn_bwd_dx(
        dx1, xs, a1, b1, o1, modtab[:, 0:3], norm1_w, wall, 0,
        tm=tm, n_tiles=n_all, tpe=tpe, n_lat=n_lat, name="ffn1_bwd_dx")
    gpack = _ffn_bwd_dw(h1, do1, da1, db1, g1, gpack, 0, tr=tr, name="ffn1_bwd_dw")
    grad_x = dx0.reshape(nb, s, D)

    zrow = jnp.zeros((1, D), F32)
    g_lat = jnp.concatenate([dmod012[:nb, 0], dmod012[:nb, 1], dmod012[:nb, 2], dmod34[:nb, 0], dmod34[:nb, 1],
                             dgate5[:, 0], dmod678[:, 0], dmod678[:, 1], dmod678[:, 2]], axis=1)
    g_ctx = jnp.concatenate([dmod012[nb:, 0], dmod012[nb:, 1], dmod012[nb:, 2], dmod34[nb:, 0], dmod34[nb:, 1],
                             zrow, zrow, zrow, zrow], axis=1)
    g_loc = jnp.concatenate([g_lat, g_ctx, jnp.zeros((7 - nb, NMOD * D), F32)], axis=0)
    g_all = _all_gather(g_loc, "gather_gmod").reshape(NDEV * 8, NMOD * D)
    g_cols = lax.dynamic_slice_in_dim(g_all, me * ncol, ncol, axis=1)
    g_w_ada, pc_ctx, g_b_ada = _ada_bwd(a_raw, c_ctx.reshape(D, 1), g_all, g_cols, w_ada[0], nb)

    def blocks(a):
        return a.reshape(NDEV, a.shape[0] // NDEV, D)

    dwin_o = jnp.concatenate([dwin[0:128], dwin[KPE_LO:KPE_LO + DR], dwin[128:384], dwin[384:896], dwin[896:1408]],
                             axis=0)
    dwout_o = jnp.concatenate([dwout[:HP].reshape(H, LANE, D)[:, :DV].reshape(H * DV, D), dwout[HP:]], axis=0)
    dwuq_o = dwuq.reshape(H, LANE, QL)[:, :DH]
    dwukv_o = jnp.concatenate([dwukv[:HP].reshape(H, LANE, KVL)[:, :DN], dwukv[HP:].reshape(H, LANE, KVL)[:, :DV]],
                              axis=1)
    gmisc = jnp.concatenate([
        blocks(dwin_o).astype(BF16), jnp.zeros((NDEV, 12, D), BF16),
        blocks(dwout_o).astype(BF16),
        dwuq_o.reshape(NDEV, 24, D).astype(BF16), jnp.zeros((NDEV, 8, D), BF16),
        dwukv_o.reshape(NDEV, 16, D).astype(BF16)], axis=1)
    gots = _scatter_sibling([gpack, gmisc], "scatter_sibling")
    parts = [_add_sibling(gpack, gots[0], 176, "add_sibling_ffn"), _add_sibling(gmisc, gots[1], 368, "add_sibling_misc")]
    recv = _scatter_chips(parts, "scatter_chips")
    gsum = _sum_chips(parts[0], recv[0], 176, "sum_grads_ffn")
    msum = _sum_chips(parts[1], recv[1], 368, "sum_grads_misc")

    g_big = {
        "ffn1_w1": gsum[0:fsh].T, "ffn1_w3": gsum[fsh:2 * fsh].T, "ffn1_w2": gsum[2 * fsh:3 * fsh],
        "ffn2_w1": gsum[3 * fsh:4 * fsh].T, "ffn2_w3": gsum[4 * fsh:5 * fsh].T, "ffn2_w2": gsum[5 * fsh:6 * fsh],
        "w_in": msum[0:180].T, "w_out": msum[192:320],
        "w_uq": msum[320:344].reshape(DH, QL).T, "w_ukv": msum[352:368].reshape(DN + DV, KVL).T,
        "w_ada": g_w_ada,
    }

    def prow(a):
        a = a.reshape(1, -1)
        return jnp.concatenate([a, jnp.zeros((1, D - a.shape[1]), F32)], axis=1)

    g_qn = dwq.reshape(H, LANE)[:, :DH].sum(0)
    g_kn = dwk.reshape(H, LANE)[:, :DH].sum(0)
    spack = jnp.concatenate([
        dnorm1, dnorm2, dnorm3, prow(dqa), prow(dkva), prow(g_qn), prow(g_kn), prow(dwv),
        prow(dbs[:, :G].T), prow(pc_ctx), jnp.zeros((6, D), F32), dws.reshape(CH, D)], axis=0)
    ssum = _sum_slots(_all_gather(spack, "gather_small"), 144, "sum_small")

    big_in = {
        "w_ada": (w_ada, m_w_ada, v_w_ada), "ffn1_w1": (ffn1_w1, m_ffn1_w1, v_ffn1_w1),
        "ffn1_w3": (ffn1_w3, m_ffn1_w3, v_ffn1_w3), "ffn1_w2": (ffn1_w2, m_ffn1_w2, v_ffn1_w2),
        "w_in": (w_in, m_w_in, v_w_in), "w_uq": (w_uq, m_w_uq, v_w_uq), "w_ukv": (w_ukv, m_w_ukv, v_w_ukv),
        "w_out": (w_out, m_w_out, v_w_out), "ffn2_w1": (ffn2_w1, m_ffn2_w1, v_ffn2_w1),
        "ffn2_w3": (ffn2_w3, m_ffn2_w3, v_ffn2_w3), "ffn2_w2": (ffn2_w2, m_ffn2_w2, v_ffn2_w2),
    }
    res = {}
    for nm, (w, m, v_) in big_in.items():
        g = g_big[nm]
        d_, m_, v2_ = _adamw(w[0], g, m[0], v_[0], "adamw_" + nm)
        res[nm] = tuple(a[None] for a in (g, d_, m_, v2_))

    small_in = [
        ("c_ctx", c_ctx, m_c_ctx, v_c_ctx, ssum[9:10], (1, D)),
        ("b_ada", b_ada, m_b_ada, v_b_ada, g_b_ada, (1, NMOD * D)),
        ("norm1_w", norm1_w, m_norm1_w, v_norm1_w, ssum[0:1], (1, D)),
        ("norm2_w", norm2_w, m_norm2_w, v_norm2_w, ssum[1:2], (1, D)),
        ("norm3_w", norm3_w, m_norm3_w, v_norm3_w, ssum[2:3], (1, D)),
        ("q_a_norm_w", q_a_norm_w, m_q_a_norm_w, v_q_a_norm_w, ssum[3:4, :QL], (1, QL)),
        ("kv_a_norm_w", kv_a_norm_w, m_kv_a_norm_w, v_kv_a_norm_w, ssum[4:5, :KVL], (1, KVL)),
        ("q_norm_w", q_norm_w, m_q_norm_w, v_q_norm_w, ssum[5:6, :DH], (1, DH)),
        ("k_norm_w", k_norm_w, m_k_norm_w, v_k_norm_w, ssum[6:7, :DH], (1, DH)),
        ("v_norm_w", v_norm_w, m_v_norm_w, v_v_norm_w, ssum[7:8, :G * GD], (G, GD)),
        ("b_s", b_s, m_b_s, v_b_s, ssum[8:9], (G, CH)),
        ("w_s", w_s, m_w_s, v_w_s, ssum[16:144], (G * CH, CH)),
    ]
    small_out = _adamw_small(
        [(w.reshape(sh), g.reshape(sh), m.reshape(sh), v_.reshape(sh)) for _, w, m, v_, g, sh in small_in])
    for (nm, w, *_), outs in zip(small_in, small_out):
        res[nm] = tuple(a.reshape(w.shape) for a in outs)

    order = ["c_ctx", "w_ada", "b_ada", "norm1_w", "ffn1_w1", "ffn1_w3", "ffn1_w2", "norm2_w", "w_in", "q_a_norm_w",
             "w_uq", "kv_a_norm_w", "w_ukv", "q_norm_w", "k_norm_w", "v_norm_w", "w_s", "b_s", "w_out", "norm3_w",
             "ffn2_w1", "ffn2_w3", "ffn2_w2"]
    return (loss, grad_x, *[res[n][0] for n in order], *[res[n][1] for n in order],
            *[res[n][2] for n in order], *[res[n][3] for n in order])
```

```python
import numpy as np
import jax
import jax.numpy as jnp
from jax import lax
from jax.experimental import pallas as pl
from jax.experimental.pallas import tpu as pltpu

F32 = jnp.float32
BF16 = jnp.bfloat16

D = 1024
FF = 2816
FC = 256
H = 8
DN, DR, DV = 64, 32, 64
DH = DN + DR
QL, KVL = 256, 128
G, GD, CH = 8, 64, 128
NMOD = 9
EPS = 1e-6
GRID_W = 64
ROPE_BASE = 10000.0
NDEV = 8
LANE = 128
HP = H * LANE
IN_COLS = 1440
WIN_ROWS = 1536
KPE_LO = 1408 + DN
NFFN_W = 6
MIB = 1 << 20

ADAM_LR, ADAM_B1, ADAM_B2, ADAM_EPS, ADAM_WD, ADAM_STEP = 0.001, 0.9, 0.999, 1e-08, 0.01, 10

MESH = pl.DeviceIdType.MESH
ANY = pl.BlockSpec(memory_space=pl.ANY)
VMEM = pl.BlockSpec(memory_space=pltpu.VMEM)


def _mm(a, b):
    return jnp.dot(a, b, preferred_element_type=F32)


def _mm_nt(a, b):
    return lax.dot_general(a, b, (((1,), (1,)), ((), ())), preferred_element_type=F32)


def _mm_tn(a, b):
    return lax.dot_general(a, b, (((0,), (0,)), ((), ())), preferred_element_type=F32)


def _dot_hl(x, m):
    hi = x.astype(BF16)
    lo = (x - hi.astype(F32)).astype(BF16)
    return _mm(hi, m) + _mm(lo, m)


def _sigmoid(a):
    return 1.0 / (1.0 + jnp.exp(-a))


_G0 = 0.7978845608028654
_G1 = 0.044715


def _gelu(x):
    return 0.5 * x * (1.0 + jnp.tanh(_G0 * (x + _G1 * (x * x * x))))


def _gelu_grad(x):
    th = jnp.tanh(_G0 * (x + _G1 * (x * x * x)))
    return 0.5 * (1.0 + th) + 0.5 * x * (1.0 - th * th) * (_G0 * (1.0 + 3.0 * _G1 * x * x))


def _rowsum(y):
    return jnp.sum(y, axis=0, keepdims=True)


def _rms(x):
    return lax.rsqrt(jnp.mean(x * x, axis=-1, keepdims=True) + EPS)


def _pcall(body, *, name, out_shape, in_specs, out_specs, grid=None, scratch=(), vmem_mb=32, aliases=None):
    kw = {}
    if grid is not None:
        kw["grid"] = grid
        sem = ("arbitrary",) * len(grid)
    else:
        sem = None
    if aliases:
        kw["input_output_aliases"] = aliases
    return pl.pallas_call(
        body, name=name, out_shape=out_shape, in_specs=in_specs, out_specs=out_specs,
        scratch_shapes=list(scratch),
        compiler_params=pltpu.CompilerParams(dimension_semantics=sem, vmem_limit_bytes=vmem_mb * MIB),
        **kw)


def _const(shape):
    nd = len(shape)
    return pl.BlockSpec(shape, lambda *_: (0,) * nd)


def _sds(shape, dt):
    return jax.ShapeDtypeStruct(shape, dt)


def _consts():
    seg_h = np.zeros((HP, LANE), np.float32)
    seg_h[np.arange(HP), np.arange(HP) // LANE] = 1.0
    seg_g = np.zeros((G * GD, LANE), np.float32)
    seg_g[np.arange(G * GD), np.arange(G * GD) // GD] = 1.0
    rot = np.zeros((LANE, LANE), np.float32)
    for base in (DN, DN + 16):
        for j in range(8):
            rot[base + j + 8, base + j] = -1.0
            rot[base + j, base + j + 8] = 1.0
    c = dict(seg_h=seg_h, seg_ht=seg_h.T, seg_g=seg_g, seg_gt=seg_g.T, rot=rot, rot_t=rot.T)
    return {k: jnp.asarray(v, BF16) for k, v in c.items()}


_GATHER_SEMS = [pltpu.SemaphoreType.DMA((7,)), pltpu.SemaphoreType.DMA((7,)), pltpu.SemaphoreType.DMA(())]


def _gather_phases(x_ref, out_ref, send_sems, recv_sems, local_sem):
    mx, my, mc = lax.axis_index("x"), lax.axis_index("y"), lax.axis_index("c")
    me, sibling = (mx, my, mc), (mx, my, 1 - mc)
    chips = [(1 - mx, my), (mx, 1 - my), (1 - mx, 1 - my)]

    def blk(px, py, pc):
        return out_ref.at[4 * px + 2 * py + pc]

    def copy(k, block, to, src=None):
        return pltpu.make_async_remote_copy(
            src_ref=blk(*block) if src is None else src, dst_ref=blk(*block),
            send_sem=send_sems.at[k], recv_sem=recv_sems.at[k], device_id=to, device_id_type=MESH)

    mine = pltpu.make_async_copy(x_ref, blk(*me), local_sem)
    first = [copy(0, me, sibling, src=x_ref)]
    first += [copy(1 + j, me, (*chip, mc), src=x_ref) for j, chip in enumerate(chips)]
    passed = [copy(4 + j, (*chip, mc), sibling) for j, chip in enumerate(chips)]

    def start():
        mine.start()
        for cp in first:
            cp.start()

    def forward():
        for j, chip in enumerate(chips):
            copy(1 + j, (*chip, mc), me).wait_recv()
            passed[j].start()

    def finish():
        copy(0, sibling, me).wait_recv()
        for j, chip in enumerate(chips):
            copy(4 + j, (*chip, 1 - mc), me).wait_recv()
        for cp in first + passed:
            cp.wait_send()
        mine.wait()

    return start, forward, finish


def _all_gather(x, name):
    r, c = x.shape

    def body(x_ref, out_ref, send_sems, recv_sems, local_sem):
        start, forward, finish = _gather_phases(x_ref, out_ref, send_sems, recv_sems, local_sem)
        start()
        forward()
        finish()

    return pl.pallas_call(
        body, name=name, out_shape=_sds((NDEV, r, c), x.dtype), in_specs=[ANY], out_specs=ANY,
        scratch_shapes=list(_GATHER_SEMS),
    )(x)


def _chip_sends(p_ref, out_ref, send_sems, recv_sems):
    mx, my, mc = lax.axis_index("x"), lax.axis_index("y"), lax.axis_index("c")
    peers = [(1 - mx, my), (mx, 1 - my), (1 - mx, 1 - my)]
    return [pltpu.make_async_remote_copy(
        src_ref=p_ref.at[2 * px + py], dst_ref=out_ref.at[j], send_sem=send_sems.at[j], recv_sem=recv_sems.at[j],
        device_id=(px, py, mc), device_id_type=MESH) for j, (px, py) in enumerate(peers)]


def _scatter_sibling(xs, name):
    n = len(xs)

    def body(*refs):
        x_refs, got_refs = refs[:n], refs[n:2 * n]
        send_sems, recv_sems = refs[2 * n:]
        mx, my, mc = lax.axis_index("x"), lax.axis_index("y"), lax.axis_index("c")
        sibling = (mx, my, 1 - mc)
        remote = []
        for i in range(n):
            for j in range(4):
                k = 4 * i + j
                remote.append(pltpu.make_async_remote_copy(
                    src_ref=x_refs[i].at[2 * j + 1 - mc], dst_ref=got_refs[i].at[j],
                    send_sem=send_sems.at[k], recv_sem=recv_sems.at[k], device_id=sibling, device_id_type=MESH))
        for cp in remote:
            cp.start()
        for cp in remote:
            cp.wait_recv()
        for cp in remote:
            cp.wait_send()

    shapes = tuple(_sds((4,) + x.shape[1:], x.dtype) for x in xs)
    return pl.pallas_call(
        body, name=name, out_shape=shapes, in_specs=[ANY] * n, out_specs=(ANY,) * n,
        scratch_shapes=[pltpu.SemaphoreType.DMA((4 * n,)), pltpu.SemaphoreType.DMA((4 * n,))],
    )(*xs)


def _scatter_chips(ps, name):
    n = len(ps)

    def body(*refs):
        p_refs, out_refs, sems = refs[:n], refs[n:2 * n], refs[2 * n:]
        sends = []
        for i in range(n):
            sends += _chip_sends(p_refs[i], out_refs[i], sems[2 * i], sems[2 * i + 1])
        for cp in sends:
            cp.start()
        for cp in sends:
            cp.wait_recv()
        for cp in sends:
            cp.wait_send()

    shapes = tuple(_sds((3,) + p.shape[1:], p.dtype) for p in ps)
    return pl.pallas_call(
        body, name=name, out_shape=shapes, in_specs=[ANY] * n, out_specs=(ANY,) * n,
        scratch_shapes=[pltpu.SemaphoreType.DMA((3,))] * (2 * n),
    )(*ps)


def _add_sibling(x, got, tr, name):
    _, r, c = x.shape

    def body(x_ref, g_ref, o_ref):
        mc = lax.axis_index("c")
        for j in range(4):
            mine = jnp.where(mc == 0, x_ref[2 * j].astype(F32), x_ref[2 * j + 1].astype(F32))
            o_ref[j] = (mine + g_ref[j].astype(F32)).astype(o_ref.dtype)

    return _pcall(body, name=name, grid=(r // tr,), out_shape=_sds(got.shape, got.dtype),
                  in_specs=[pl.BlockSpec((NDEV, tr, c), lambda t: (0, t, 0)), pl.BlockSpec((4, tr, c), lambda t: (0, t, 0))],
                  out_specs=pl.BlockSpec((4, tr, c), lambda t: (0, t, 0)))(x, got)


def _sum_chips(part, recv, tr, name):
    _, r, c = part.shape

    def body(p_ref, r_ref, o_ref):
        slot = 2 * lax.axis_index("x") + lax.axis_index("y")
        acc = p_ref[0].astype(F32)
        for j in range(1, 4):
            acc = jnp.where(slot == j, p_ref[j].astype(F32), acc)
        for j in range(3):
            acc = acc + r_ref[j].astype(F32)
        o_ref[...] = acc

    return _pcall(body, name=name, grid=(r // tr,), out_shape=_sds((r, c), F32),
                  in_specs=[pl.BlockSpec((4, tr, c), lambda t: (0, t, 0)), pl.BlockSpec((3, tr, c), lambda t: (0, t, 0))],
                  out_specs=pl.BlockSpec((tr, c), lambda t: (t, 0)))(part, recv)


def _sum_slots(x, tr, name):
    n, r, c = x.shape

    def body(x_ref, o_ref):
        acc = x_ref[0].astype(F32)
        for s in range(1, n):
            acc = acc + x_ref[s].astype(F32)
        o_ref[...] = acc

    return _pcall(body, name=name, grid=(r // tr,), out_shape=_sds((r, c), F32),
                  in_specs=[pl.BlockSpec((n, tr, c), lambda t: (0, t, 0))],
                  out_specs=pl.BlockSpec((tr, c), lambda t: (t, 0)))(x)


def _ada_fwd(a_raw, w_loc, b_loc):
    ncol = w_loc.shape[1]

    def body(a_ref, w_ref, b_ref, o_ref):
        a = a_ref[...]
        act = (a * _sigmoid(a)).astype(BF16)
        o_ref[...] = _mm(act, w_ref[...].astype(BF16)) + b_ref[...]

    return _pcall(body, name="ada_fwd", out_shape=_sds((a_raw.shape[0], ncol), F32),
                  in_specs=[VMEM] * 3, out_specs=VMEM)(a_raw, w_loc, b_loc)


def _ada_bwd(a_raw, cctx_col, g_all, g_cols, w_loc, nb):
    nrow = a_raw.shape[0]
    ncol = w_loc.shape[1]

    def body(a_ref, cc_ref, gall_ref, g_ref, w_ref, dw_ref, pc_ref, gb_ref):
        a = a_ref[...]
        rowid = lax.broadcasted_iota(jnp.int32, (nrow, 1), 0) % 8
        act = jnp.where(rowid < nb, a * _sigmoid(a), 0.0).astype(BF16)
        g = g_ref[...]
        gc = _rowsum(jnp.where(rowid == nb, g, 0.0))
        cc = cc_ref[...]
        dw_ref[...] = _mm_tn(act, g.astype(BF16)) + (cc * _sigmoid(cc)) * gc
        pc_ref[...] = jnp.sum(w_ref[...] * gc, axis=1, keepdims=True)
        gb_ref[...] = _rowsum(gall_ref[...])

    return _pcall(body, name="ada_bwd",
                  out_shape=(_sds((D, ncol), F32), _sds((D, 1), F32), _sds((1, g_all.shape[1]), F32)),
                  in_specs=[VMEM] * 5, out_specs=(VMEM,) * 3, vmem_mb=48)(a_raw, cctx_col, g_all, g_cols, w_loc)


def _mod_spec(k, tpe, nrows):
    return pl.BlockSpec((1, k, D), lambda t: (jnp.minimum(t // tpe, nrows - 1), 0, 0))


def _load_ffn_weights(wall_ref, first, bufs, sems):
    fsh = FF // NDEV
    cps = []
    for j, buf in enumerate(bufs):
        for d in range(NDEV):
            cps.append(pltpu.make_async_copy(wall_ref.at[d, pl.ds((first + j) * fsh, fsh)],
                                             buf.at[pl.ds(d * fsh, fsh)], sems.at[j * NDEV + d]))
    for cp in cps:
        cp.start()
    for cp in cps:
        cp.wait()


def _token_specs(xs, tm, n_lat):
    specs = [pl.BlockSpec((tm, D), lambda t: (jnp.minimum(t, n_lat - 1), 0))]
    if len(xs) == 2:
        specs.append(pl.BlockSpec((tm, D), lambda t: (jnp.maximum(t - n_lat, 0), 0)))
    return specs


def _ffn_fwd(xs, mod3, norm_w, wall, first, *, tm, n_tiles, tpe, n_lat, name, target=None, gather=None):
    nrows = mod3.shape[0]
    r = n_tiles * tm
    nx = len(xs)
    with_loss = target is not None
    with_gather = gather is not None
    fwd_step = max(n_tiles // 3, 1)

    def body(*refs):
        x_refs = refs[:nx]
        pos = nx
        if with_loss:
            tgt_ref = refs[pos]
            pos += 1
        mod_ref, nw_ref, wall_ref = refs[pos:pos + 3]
        pos += 3
        if with_gather:
            gin_ref = refs[pos]
            pos += 1
        xo_ref, a_ref, b_ref, o_ref = refs[pos:pos + 4]
        pos += 4
        if with_loss:
            ls_ref = refs[pos]
            pos += 1
        if with_gather:
            gout_ref = refs[pos]
            pos += 1
        w1_ref, w3_ref, w2_ref, wsem, acc_ref = refs[pos:pos + 5]
        t = pl.program_id(0)
        if with_gather:
            g_start, g_forward, g_finish = _gather_phases(gin_ref, gout_ref, *refs[pos + 5:])

        @pl.when(t == 0)
        def _():
            if with_gather:
                g_start()
            _load_ffn_weights(wall_ref, first, (w1_ref, w3_ref, w2_ref), wsem)
            if with_loss:
                ls_ref[...] = jnp.zeros_like(ls_ref)

        if with_gather:
            @pl.when(t == fwd_step)
            def _():
                g_forward()

            @pl.when(t == n_tiles - 1)
            def _():
                g_finish()

        x = x_refs[0][...]
        if nx == 2:
            x = jnp.where(t < n_lat, x, x_refs[1][...])
        n = x * _rms(x) * nw_ref[...]
        shift, scale, gate = mod_ref[0, 0:1, :], mod_ref[0, 1:2, :], mod_ref[0, 2:3, :]
        h = (n * (1.0 + scale) + shift).astype(BF16)
        for j in range(FF // FC):
            sl = slice(j * FC, (j + 1) * FC)
            a = _mm_nt(h, w1_ref[sl, :])
            b = _mm_nt(h, w3_ref[sl, :])
            a_ref[:, sl] = a.astype(BF16)
            b_ref[:, sl] = b.astype(BF16)
            g = (a * _sigmoid(a) * b).astype(BF16)
            part = _mm(g, w2_ref[sl, :])
            if j == 0:
                acc_ref[...] = part
            else:
                acc_ref[...] += part
        o = acc_ref[...]
        o_ref[...] = o.astype(BF16)
        out = x + (0.5 * gate) * o
        if with_loss:
            d = out - tgt_ref[...]
            xo_ref[...] = d * (1.0 / D)
            ls_ref[...] += jnp.sum(d * d)
        else:
            xo_ref[...] = out

    row = lambda cols: pl.BlockSpec((tm, cols), lambda t: (t, 0))
    in_specs = _token_specs(xs, tm, n_lat) + ([row(D)] if with_loss else []) + [
        _mod_spec(3, tpe, nrows), _const((1, D)), ANY]
    out_shape = [_sds((r, D), F32), _sds((r, FF), BF16), _sds((r, FF), BF16), _sds((r, D), BF16)]
    out_specs = [row(D), row(FF), row(FF), row(D)]
    scratch = [pltpu.VMEM((FF, D), BF16)] * 3 + [pltpu.SemaphoreType.DMA((3 * NDEV,)), pltpu.VMEM((tm, D), F32)]
    if with_loss:
        out_shape.append(_sds((8, LANE), F32))
        out_specs.append(_const((8, LANE)))
    args = list(xs) + ([target] if with_loss else []) + [mod3, norm_w, wall]
    if with_gather:
        assert n_tiles >= 2
        in_specs.append(ANY)
        args.append(gather)
        out_shape.append(_sds((NDEV,) + gather.shape, gather.dtype))
        out_specs.append(ANY)
        scratch += _GATHER_SEMS
    return _pcall(
        body, name=name, grid=(n_tiles,), out_shape=tuple(out_shape), in_specs=in_specs, out_specs=tuple(out_specs),
        scratch=scratch, vmem_mb=56)(*args)


def _ffn_bwd_dx(dout, xs, a, b, o, mod3, norm_w, wall, first, *, tm, n_tiles, tpe, n_lat, name):
    nrows = mod3.shape[0]
    r = n_tiles * tm
    nx = len(xs)

    def body(*refs):
        dout_ref = refs[0]
        x_refs = refs[1:1 + nx]
        (a_ref, b_ref, o_ref, mod_ref, nw_ref, wall_ref,
         dx_ref, da_ref, db_ref, g_ref, do_ref, h_ref, dmod_ref, dnw_ref,
         w1_ref, w3_ref, w2_ref, wsem, acc_ref) = refs[1 + nx:]
        t = pl.program_id(0)

        @pl.when(t == 0)
        def _():
            _load_ffn_weights(wall_ref, first, (w1_ref, w3_ref, w2_ref), wsem)
            dnw_ref[...] = jnp.zeros_like(dnw_ref)

        x = x_refs[0][...]
        if nx == 2:
            x = jnp.where(t < n_lat, x, x_refs[1][...])
        dout = dout_ref[...]
        rr = _rms(x)
        xh = x * rr
        nw = nw_ref[...]
        n = xh * nw
        shift, scale, gate = mod_ref[0, 0:1, :], mod_ref[0, 1:2, :], mod_ref[0, 2:3, :]
        h = (n * (1.0 + scale) + shift).astype(BF16)
        h_ref[...] = h
        d_o = ((0.5 * gate) * dout).astype(BF16)
        do_ref[...] = d_o
        dgate = _rowsum(0.5 * o_ref[...].astype(F32) * dout)
        for j in range(FF // FC):
            sl = slice(j * FC, (j + 1) * FC)
            av = a_ref[:, sl].astype(F32)
            bv = b_ref[:, sl].astype(F32)
            dg = _mm_nt(d_o, w2_ref[sl, :])
            sig = _sigmoid(av)
            sa = av * sig
            g_ref[:, sl] = (sa * bv).astype(BF16)
            da = (dg * bv * (sig * (1.0 + av * (1.0 - sig)))).astype(BF16)
            db = (dg * sa).astype(BF16)
            da_ref[:, sl] = da
            db_ref[:, sl] = db
            part = _mm(da, w1_ref[sl, :]) + _mm(db, w3_ref[sl, :])
            if j == 0:
                acc_ref[...] = part
            else:
                acc_ref[...] += part
        dh = acc_ref[...]
        dn = dh * (1.0 + scale)
        dxh = dn * nw

        @pl.when(t < n_lat)
        def _():
            dx_ref[...] = dout + rr * (dxh - xh * jnp.mean(dxh * xh, axis=-1, keepdims=True))

        first_visit = jnp.where(t < n_lat, t % tpe == 0, t == n_lat)

        @pl.when(first_visit)
        def _():
            dmod_ref[...] = jnp.zeros_like(dmod_ref)

        dmod_ref[0, 0:1, :] += _rowsum(dh)
        dmod_ref[0, 1:2, :] += _rowsum(dh * n)
        dmod_ref[0, 2:3, :] += dgate
        dnw_ref[...] += _rowsum(dn * xh)

    row = lambda cols: pl.BlockSpec((tm, cols), lambda t: (t, 0))
    lat = pl.BlockSpec((tm, D), lambda t: (jnp.minimum(t, n_lat - 1), 0))
    return _pcall(
        body, name=name, grid=(n_tiles,),
        out_shape=(_sds((n_lat * tm, D), F32), _sds((r, FF), BF16), _sds((r, FF), BF16), _sds((r, FF), BF16),
                   _sds((r, D), BF16), _sds((r, D), BF16), _sds((nrows, 3, D), F32), _sds((1, D), F32)),
        in_specs=[row(D)] + _token_specs(xs, tm, n_lat) + [row(FF), row(FF), row(D), _mod_spec(3, tpe, nrows),
                                                            _const((1, D)), ANY],
        out_specs=(lat, row(FF), row(FF), row(FF), row(D), row(D), _mod_spec(3, tpe, nrows), _const((1, D))),
        scratch=[pltpu.VMEM((FF, D), BF16)] * 3 + [pltpu.SemaphoreType.DMA((3 * NDEV,)), pltpu.VMEM((tm, D), F32)],
        vmem_mb=60)(dout, *xs, a, b, o, mod3, norm_w, wall)


def _ffn_bwd_dw(h, d_o, da, db, g, *, tr, name):
    r = h.shape[0]
    fh = FF // 2
    fsh = FF // NDEV
    nk = r // tr

    def body(h_ref, do_ref, da_ref, db_ref, g_ref, out_ref, acc1, acc3, acc2):
        k = pl.program_id(1)

        @pl.when(k == 0)
        def _():
            acc1[...] = jnp.zeros_like(acc1)
            acc3[...] = jnp.zeros_like(acc3)
            acc2[...] = jnp.zeros_like(acc2)

        hv = h_ref[...]
        acc1[...] += _mm_tn(da_ref[...], hv)
        acc3[...] += _mm_tn(db_ref[...], hv)
        acc2[...] += _mm_tn(g_ref[...], do_ref[...])

        @pl.when(k == nk - 1)
        def _():
            for i, acc in enumerate((acc1, acc3, acc2)):
                out_ref[:, i * fsh:(i + 1) * fsh, :] = acc[...].reshape(NDEV // 2, fsh, D).astype(BF16)

    rowd = pl.BlockSpec((tr, D), lambda f, k: (k, 0))
    rowf = pl.BlockSpec((tr, fh), lambda f, k: (k, f))
    return _pcall(
        body, name=name, grid=(2, nk), out_shape=_sds((NDEV, 3 * fsh, D), BF16),
        in_specs=[rowd, rowd, rowf, rowf, rowf],
        out_specs=pl.BlockSpec((NDEV // 2, 3 * fsh, D), lambda f, k: (f, 0, 0)),
        scratch=[pltpu.VMEM((fh, D), F32)] * 3, vmem_mb=56)(h, d_o, da, db, g)


_PIECES = ((0, 128), (128, 384), (384, 896), (896, 1408), (1408, 1536))


def _proj_fwd(x1, mod2, norm_w, wint, *, tm, n_tiles, tpe, name="proj_fwd"):
    nrows = mod2.shape[0]
    r = n_tiles * tm

    def body(x_ref, mod_ref, nw_ref, w_ref, ckv_ref, q_ref, u_ref, v_ref, kpe_ref):
        x = x_ref[...]
        n = x * _rms(x) * nw_ref[...]
        h = (n * (1.0 + mod_ref[0, 1:2, :]) + mod_ref[0, 0:1, :]).astype(BF16)
        for (lo, hi), ref in zip(_PIECES, (ckv_ref, q_ref, u_ref, v_ref, kpe_ref)):
            ref[...] = _mm_nt(h, w_ref[lo:hi, :])

    row = lambda cols: pl.BlockSpec((tm, cols), lambda t: (t, 0))
    widths = [hi - lo for lo, hi in _PIECES]
    return _pcall(
        body, name=name, grid=(n_tiles,),
        out_shape=tuple(_sds((r, w), F32) for w in widths),
        in_specs=[row(D), _mod_spec(2, tpe, nrows), _const((1, D)), _const((WIN_ROWS, D))],
        out_specs=tuple(row(w) for w in widths), vmem_mb=40)(x1, mod2, norm_w, wint)


def _proj_bwd(dckv, dkpe, dq, du, dv, dx2, x1, mod2, norm_w, wint, *, tm, n_tiles, tpe, n_lat, name="proj_bwd"):
    nrows = mod2.shape[0]
    r = n_tiles * tm

    def body(dckv_ref, dkpe_ref, dq_ref, du_ref, dv_ref, dx2_ref, x_ref, mod_ref, nw_ref, w_ref,
             dx_ref, dw_ref, dmod_ref, dnw_ref, acc_ref):
        t = pl.program_id(0)
        is_lat = t < n_lat
        x = x_ref[...]
        rr = _rms(x)
        xh = x * rr
        nw = nw_ref[...]
        n = xh * nw
        scale = mod_ref[0, 1:2, :]
        h = (n * (1.0 + scale) + mod_ref[0, 0:1, :]).astype(BF16)

        @pl.when(t == 0)
        def _():
            dw_ref[...] = jnp.zeros_like(dw_ref)
            dnw_ref[...] = jnp.zeros_like(dnw_ref)

        dckv_v, dkpe_v = dckv_ref[...], dkpe_ref[...]
        acc_ref[...] = _mm(dckv_v, w_ref[0:128, :]) + _mm(dkpe_v, w_ref[1408:1536, :])
        dw_ref[0:128, :] += _mm_tn(dckv_v, h)
        dw_ref[1408:1536, :] += _mm_tn(dkpe_v, h)

        @pl.when(is_lat)
        def _():
            dq_v, du_v, dv_v = dq_ref[...], du_ref[...], dv_ref[...]
            acc_ref[...] += (_mm(dq_v, w_ref[128:384, :]) + _mm(du_v, w_ref[384:896, :])
                             + _mm(dv_v, w_ref[896:1408, :]))
            dw_ref[128:384, :] += _mm_tn(dq_v, h)
            dw_ref[384:896, :] += _mm_tn(du_v, h)
            dw_ref[896:1408, :] += _mm_tn(dv_v, h)

        dh = acc_ref[...]
        dn = dh * (1.0 + scale)
        dxh = dn * nw
        dx = rr * (dxh - xh * jnp.mean(dxh * xh, axis=-1, keepdims=True))
        dx_ref[...] = dx + jnp.where(is_lat, dx2_ref[...], 0.0)

        first = jnp.where(is_lat, t % tpe == 0, t == n_lat)

        @pl.when(first)
        def _():
            dmod_ref[...] = jnp.zeros_like(dmod_ref)

        dmod_ref[0, 0:1, :] += _rowsum(dh)
        dmod_ref[0, 1:2, :] += _rowsum(dh * n)
        dnw_ref[...] += _rowsum(dn * xh)

    row = lambda cols: pl.BlockSpec((tm, cols), lambda t: (t, 0))
    lat = lambda cols: pl.BlockSpec((tm, cols), lambda t: (jnp.minimum(t, n_lat - 1), 0))
    return _pcall(
        body, name=name, grid=(n_tiles,),
        out_shape=(_sds((r, D), F32), _sds((WIN_ROWS, D), F32), _sds((nrows, 2, D), F32), _sds((1, D), F32)),
        in_specs=[row(128), row(128), lat(256), lat(512), lat(512), lat(D), row(D), _mod_spec(2, tpe, nrows),
                  _const((1, D)), _const((WIN_ROWS, D))],
        out_specs=(row(D), _const((WIN_ROWS, D)), _mod_spec(2, tpe, nrows), _const((1, D))),
        scratch=[pltpu.VMEM((tm, D), F32)], vmem_mb=48)(dckv, dkpe, dq, du, dv, dx2, x1, mod2, norm_w, wint)


def _head_norm_rope(x, w_pad, cos, sin, seg, segt, rot):
    rh = lax.rsqrt(_dot_hl(x * x, seg) * (1.0 / DH) + EPS)
    rb = _dot_hl(rh, segt)
    y = x * rb
    t = y * w_pad
    out = []
    for h in range(H):
        th = t[:, h * LANE:(h + 1) * LANE]
        out.append(th * cos + _dot_hl(th, rot) * sin)
    return jnp.concatenate(out, axis=-1), y, rb


def _head_norm_rope_bwd(dout, y, rb, w_pad, cos, sin, seg, segt, rot_t):
    dt = []
    for h in range(H):
        dh = dout[:, h * LANE:(h + 1) * LANE]
        dt.append(dh * cos + _dot_hl(dh * sin, rot_t))
    dt = jnp.concatenate(dt, axis=-1)
    dw = _rowsum(dt * y)
    dy = dt * w_pad
    mean_h = _dot_hl(dy * y, seg) * (1.0 / DH)
    return rb * (dy - y * _dot_hl(mean_h, segt)), dw


def _q_prep_fwd(qp, qa_w, wuq, wq, cos, sin, cs, *, tm, n_lat, tpe):
    def body(qp_ref, qa_ref, wuq_ref, wq_ref, cos_ref, sin_ref, seg, segt, rot, q_ref):
        x = qp_ref[...]
        cq = (x * _rms(x) * qa_ref[...]).astype(BF16)
        q, _, _ = _head_norm_rope(_mm_nt(cq, wuq_ref[...]), wq_ref[...], cos_ref[...], sin_ref[...],
                                  seg[...], segt[...], rot[...])
        q_ref[...] = q.astype(BF16)

    row = lambda cols: pl.BlockSpec((tm, cols), lambda t: (t, 0))
    tab = pl.BlockSpec((tm, LANE), lambda t: (t % tpe, 0))
    return _pcall(
        body, name="q_prep_fwd", grid=(n_lat,), out_shape=_sds((n_lat * tm, HP), BF16),
        in_specs=[row(QL), _const((1, QL)), _const((HP, QL)), _const((1, HP)), tab, tab,
                  _const((HP, LANE)), _const((LANE, HP)), _const((LANE, LANE))],
        out_specs=row(HP))(qp, qa_w, wuq, wq, cos, sin, cs["seg_h"], cs["seg_ht"], cs["rot"])


def _q_prep_bwd(dq, qp, qa_w, wuq, wq, cos, sin, cs, *, tm, n_lat, tpe):
    def body(dq_ref, qp_ref, qa_ref, wuq_ref, wq_ref, cos_ref, sin_ref, seg, segt, rot, rot_t,
             dqp_ref, dwuq_ref, dqa_ref, dwq_ref):
        t = pl.program_id(0)
        x = qp_ref[...]
        ra = _rms(x)
        xh = x * ra
        qa = qa_ref[...]
        cq = (xh * qa).astype(BF16)
        wuq_v = wuq_ref[...]
        wq_v, cos_v, sin_v = wq_ref[...], cos_ref[...], sin_ref[...]
        _, y, rb = _head_norm_rope(_mm_nt(cq, wuq_v), wq_v, cos_v, sin_v, seg[...], segt[...], rot[...])
        dqraw, dwq = _head_norm_rope_bwd(dq_ref[...], y, rb, wq_v, cos_v, sin_v, seg[...], segt[...], rot_t[...])
        dqraw = dqraw.astype(BF16)
        dcq = _mm(dqraw, wuq_v)
        dxh = dcq * qa
        dqp_ref[...] = (ra * (dxh - xh * jnp.mean(dxh * xh, axis=-1, keepdims=True))).astype(BF16)

        @pl.when(t == 0)
        def _():
            dwuq_ref[...] = jnp.zeros_like(dwuq_ref)
            dqa_ref[...] = jnp.zeros_like(dqa_ref)
            dwq_ref[...] = jnp.zeros_like(dwq_ref)

        dwuq_ref[...] += _mm_tn(dqraw, cq)
        dqa_ref[...] += _rowsum(dcq * xh)
        dwq_ref[...] += dwq

    row = lambda cols: pl.BlockSpec((tm, cols), lambda t: (t, 0))
    tab = pl.BlockSpec((tm, LANE), lambda t: (t % tpe, 0))
    return _pcall(
        body, name="q_prep_bwd", grid=(n_lat,),
        out_shape=(_sds((n_lat * tm, QL), BF16), _sds((HP, QL), F32), _sds((1, QL), F32), _sds((1, HP), F32)),
        in_specs=[row(HP), row(QL), _const((1, QL)), _const((HP, QL)), _const((1, HP)), tab, tab,
                  _const((HP, LANE)), _const((LANE, HP)), _const((LANE, LANE)), _const((LANE, LANE))],
        out_specs=(row(QL), _const((HP, QL)), _const((1, QL)), _const((1, HP))), vmem_mb=40)(
            dq, qp, qa_w, wuq, wq, cos, sin, cs["seg_h"], cs["seg_ht"], cs["rot"], cs["rot_t"])


def _kv_tab_spec(tm, tpe, n_lat):
    return pl.BlockSpec((tm, LANE), lambda t: (jnp.where(t < n_lat, t % tpe, tpe), 0))


def _kv_prep_fwd(ckv, kpe, kva_w, wukv, wk, cosk, sink, cs, *, tm, n_tiles, tpe, n_lat):
    def body(ckv_ref, kpe_ref, kva_ref, wukv_ref, wk_ref, cos_ref, sin_ref, seg, segt, rot, k_ref, v_ref):
        x = ckv_ref[...]
        ckvn = (x * _rms(x) * kva_ref[...]).astype(BF16)
        kv = _mm_nt(ckvn, wukv_ref[...])
        kx = kv[:, :HP] + jnp.concatenate([kpe_ref[...]] * H, axis=-1)
        k, _, _ = _head_norm_rope(kx, wk_ref[...], cos_ref[...], sin_ref[...], seg[...], segt[...], rot[...])
        k_ref[...] = k.astype(BF16)
        v_ref[...] = kv[:, HP:].astype(BF16)

    row = lambda cols: pl.BlockSpec((tm, cols), lambda t: (t, 0))
    tab = _kv_tab_spec(tm, tpe, n_lat)
    r = n_tiles * tm
    return _pcall(
        body, name="kv_prep_fwd", grid=(n_tiles,), out_shape=(_sds((r, HP), BF16), _sds((r, HP), BF16)),
        in_specs=[row(KVL), row(LANE), _const((1, KVL)), _const((2 * HP, KVL)), _const((1, HP)), tab, tab,
                  _const((HP, LANE)), _const((LANE, HP)), _const((LANE, LANE))],
        out_specs=(row(HP), row(HP)), vmem_mb=40)(
            ckv, kpe, kva_w, wukv, wk, cosk, sink, cs["seg_h"], cs["seg_ht"], cs["rot"])


def _kv_prep_bwd(dks, dvs, ckv, kpe, kva_w, wukv, wk, cosk, sink, cs, *, tm, n_tiles, tpe, n_lat):
    def body(dkl_ref, dkc_ref, dvl_ref, dvc_ref, ckv_ref, kpe_ref, kva_ref, wukv_ref, wk_ref, cos_ref, sin_ref,
             seg, segt, rot, rot_t, dckv_ref, dkpe_ref, dwukv_ref, dkva_ref, dwk_ref):
        t = pl.program_id(0)
        is_lat = t < n_lat
        dk = jnp.where(is_lat, dkl_ref[...], dkc_ref[...])
        dv = jnp.where(is_lat, dvl_ref[...], dvc_ref[...])
        x = ckv_ref[...]
        ra = _rms(x)
        xh = x * ra
        kva = kva_ref[...]
        ckvn = (xh * kva).astype(BF16)
        wukv_v = wukv_ref[...]
        wk_v, cos_v, sin_v = wk_ref[...], cos_ref[...], sin_ref[...]
        kv = _mm_nt(ckvn, wukv_v)
        kx = kv[:, :HP] + jnp.concatenate([kpe_ref[...]] * H, axis=-1)
        _, y, rb = _head_norm_rope(kx, wk_v, cos_v, sin_v, seg[...], segt[...], rot[...])
        dkx, dwk = _head_norm_rope_bwd(dk, y, rb, wk_v, cos_v, sin_v, seg[...], segt[...], rot_t[...])
        dkpe = dkx[:, 0:LANE]
        for h in range(1, H):
            dkpe = dkpe + dkx[:, h * LANE:(h + 1) * LANE]
        lane = lax.broadcasted_iota(jnp.int32, (tm, LANE), 1)
        dkpe_ref[...] = jnp.where((lane >= DN) & (lane < DH), dkpe, 0.0).astype(BF16)
        dkv = jnp.concatenate([dkx, dv], axis=-1).astype(BF16)
        dckvn = _mm(dkv, wukv_v)
        dxh = dckvn * kva
        dckv_ref[...] = (ra * (dxh - xh * jnp.mean(dxh * xh, axis=-1, keepdims=True))).astype(BF16)

        @pl.when(t == 0)
        def _():
            dwukv_ref[...] = jnp.zeros_like(dwukv_ref)
            dkva_ref[...] = jnp.zeros_like(dkva_ref)
            dwk_ref[...] = jnp.zeros_like(dwk_ref)

        dwukv_ref[...] += _mm_tn(dkv, ckvn)
        dkva_ref[...] += _rowsum(dckvn * xh)
        dwk_ref[...] += dwk

    row = lambda cols: pl.BlockSpec((tm, cols), lambda t: (t, 0))
    lat = pl.BlockSpec((tm, HP), lambda t: (jnp.minimum(t, n_lat - 1), 0))
    ctx = pl.BlockSpec((tm, HP), lambda t: (jnp.maximum(t - n_lat, 0), 0))
    tab = _kv_tab_spec(tm, tpe, n_lat)
    r = n_tiles * tm
    return _pcall(
        body, name="kv_prep_bwd", grid=(n_tiles,),
        out_shape=(_sds((r, KVL), BF16), _sds((r, LANE), BF16), _sds((2 * HP, KVL), F32), _sds((1, KVL), F32),
                   _sds((1, HP), F32)),
        in_specs=[lat, ctx, lat, ctx, row(KVL), row(LANE), _const((1, KVL)), _const((2 * HP, KVL)), _const((1, HP)),
                  tab, tab, _const((HP, LANE)), _const((LANE, HP)), _const((LANE, LANE)), _const((LANE, LANE))],
        out_specs=(row(KVL), row(LANE), _const((2 * HP, KVL)), _const((1, KVL)), _const((1, HP))), vmem_mb=48)(
            dks[0], dks[1], dvs[0], dvs[1], ckv, kpe, kva_w, wukv, wk, cosk, sink,
            cs["seg_h"], cs["seg_ht"], cs["rot"], cs["rot_t"])


_SCALE = DH ** -0.5


def _attn_specs(tq, s, nc, tpe, n_lat_rows):
    qs = pl.BlockSpec((tq, LANE), lambda i, j, t: (i * tpe + t, j))
    kl = pl.BlockSpec((s, LANE), lambda i, j, t: (i, j))
    kc = pl.BlockSpec((nc, LANE), lambda i, j, t: (n_lat_rows // nc + i, j))
    return qs, kl, kc


def _softmax_parts(q, kl, kc):
    s1 = _mm_nt(q, kl) * _SCALE
    s2 = _mm_nt(q, kc) * _SCALE
    m = jnp.maximum(jnp.max(s1, axis=-1, keepdims=True), jnp.max(s2, axis=-1, keepdims=True))
    e1 = jnp.exp(s1 - m)
    e2 = jnp.exp(s2 - m)
    l = jnp.sum(e1, axis=-1, keepdims=True) + jnp.sum(e2, axis=-1, keepdims=True)
    return e1, e2, l


def _attn_fwd(q, k, v, *, nb, s, nc, tq):
    tpe = s // tq
    r_lat = nb * s

    def body(q_ref, kl_ref, kc_ref, vl_ref, vc_ref, o_ref):
        e1, e2, l = _softmax_parts(q_ref[...], kl_ref[...], kc_ref[...])
        o = _mm(e1.astype(BF16), vl_ref[...]) + _mm(e2.astype(BF16), vc_ref[...])
        o_ref[...] = (o / l).astype(BF16)

    qs, kl, kc = _attn_specs(tq, s, nc, tpe, r_lat)
    return _pcall(body, name="attn_fwd", grid=(nb, H, tpe), out_shape=_sds((r_lat, HP), BF16),
                  in_specs=[qs, kl, kc, kl, kc], out_specs=qs, vmem_mb=48)(q, k, k, v, v)


def _attn_bwd(q, k, v, o, do, part, *, nb, s, nc, tq):
    tpe = s // tq
    r_lat = nb * s

    def body(q_ref, kl_ref, kc_ref, vl_ref, vc_ref, o_ref, do_ref, part_ref,
             dq_ref, dkl_ref, dkc_ref, dvl_ref, dvc_ref, recv_ref, akl, akc, avl, avc, send_sems, recv_sems):
        t = pl.program_id(2)
        step = (pl.program_id(0) * H + pl.program_id(1)) * tpe + t
        sends = _chip_sends(part_ref, recv_ref, send_sems, recv_sems)

        @pl.when(step == 0)
        def _():
            for cp in sends:
                cp.start()

        @pl.when(step == nb * H * tpe - 1)
        def _():
            for cp in sends:
                cp.wait_recv()
            for cp in sends:
                cp.wait_send()

        qv, klv, kcv = q_ref[...], kl_ref[...], kc_ref[...]
        e1, e2, l = _softmax_parts(qv, klv, kcv)
        inv = 1.0 / l
        p1, p2 = e1 * inv, e2 * inv
        dov = do_ref[...]
        delta = jnp.sum(dov.astype(F32) * o_ref[...].astype(F32), axis=-1, keepdims=True)
        ds1 = (p1 * (_mm_nt(dov, vl_ref[...]) - delta) * _SCALE).astype(BF16)
        ds2 = (p2 * (_mm_nt(dov, vc_ref[...]) - delta) * _SCALE).astype(BF16)
        dq_ref[...] = _mm(ds1, klv) + _mm(ds2, kcv)

        @pl.when(t == 0)
        def _():
            akl[...] = jnp.zeros_like(akl)
            akc[...] = jnp.zeros_like(akc)
            avl[...] = jnp.zeros_like(avl)
            avc[...] = jnp.zeros_like(avc)

        akl[...] += _mm_tn(qv, ds1)
        akc[...] += _mm_tn(qv, ds2)
        avl[...] += _mm_tn(dov, p1.astype(BF16))
        avc[...] += _mm_tn(dov, p2.astype(BF16))

        @pl.when(t == tpe - 1)
        def _():
            dkl_ref[...] = akl[...].T
            dkc_ref[...] = akc[...].T
            dvl_ref[...] = avl[...].T
            dvc_ref[...] = avc[...].T

    qs, kl, kc = _attn_specs(tq, s, nc, tpe, r_lat)
    kc_out = pl.BlockSpec((nc, LANE), lambda i, j, t: (i, j))
    return _pcall(
        body, name="attn_bwd", grid=(nb, H, tpe),
        out_shape=(_sds((r_lat, HP), F32), _sds((r_lat, HP), F32), _sds((nb * nc, HP), F32),
                   _sds((r_lat, HP), F32), _sds((nb * nc, HP), F32), _sds((3,) + part.shape[1:], part.dtype)),
        in_specs=[qs, kl, kc, kl, kc, qs, qs, ANY], out_specs=(qs, kl, kc_out, kl, kc_out, ANY),
        scratch=[pltpu.VMEM((LANE, s), F32), pltpu.VMEM((LANE, nc), F32)] * 2 + [pltpu.SemaphoreType.DMA((3,))] * 2,
        vmem_mb=56)(q, k, k, v, v, o, do, part)


def _gating(vn, ws_ref, bias_ref, s_scr, tm):
    lane = lax.broadcasted_iota(jnp.int32, (CH, LANE), 1)
    for c in range(tm // CH):
        rs = slice(c * CH, (c + 1) * CH)
        for j in range(G // 2):
            ls = slice(j * LANE, (j + 1) * LANE)
            vp = vn[rs, ls]
            s_scr[rs, ls] = jnp.where(lane < GD, _mm(ws_ref[2 * j], vp), _mm(ws_ref[2 * j + 1], vp)) + bias_ref[:, ls]


def _mix_fwd(u, v, attn, x1, gate, wv, ws, bias, wout, cs, *, tm, n_lat, tpe):
    nrows = gate.shape[0]

    def body(u_ref, v_ref, attn_ref, x_ref, gate_ref, wv_ref, ws_ref, bias_ref, wout_ref, seg, segt,
             x2_ref, mix_ref, s_scr):
        vg = _gelu(v_ref[...])
        rg = lax.rsqrt(_dot_hl(vg * vg, seg[...]) * (1.0 / GD) + EPS)
        vn = (vg * _dot_hl(rg, segt[...]) * wv_ref[...]).astype(BF16)
        _gating(vn, ws_ref, bias_ref, s_scr, tm)
        sg = (_gelu(u_ref[...]) * s_scr[...]).astype(BF16)
        mix = _mm(attn_ref[...], wout_ref[0:HP, :]) + _mm(sg, wout_ref[HP:, :])
        mix_ref[...] = mix.astype(BF16)
        x2_ref[...] = x_ref[...] + gate_ref[0] * mix

    row = lambda cols: pl.BlockSpec((tm, cols), lambda t: (t, 0))
    r = n_lat * tm
    return _pcall(
        body, name="mix_fwd", grid=(n_lat,),
        out_shape=(_sds((r, D), F32), _sds((r, D), BF16)),
        in_specs=[row(G * GD), row(G * GD), row(HP), row(D), _mod_spec(1, tpe, nrows), _const((1, G * GD)),
                  _const((G, CH, CH)), _const((CH, G * GD)), _const((HP + G * GD, D)), _const((G * GD, LANE)),
                  _const((LANE, G * GD))],
        out_specs=(row(D), row(D)), scratch=[pltpu.VMEM((tm, G * GD), F32)], vmem_mb=40)(
            u, v, attn, x1, gate, wv, ws, bias, wout, cs["seg_g"], cs["seg_gt"])


def _mix_bwd(dx2, mix, u, v, attn, gate, wv, ws, wst, bias, wout, cs, *, tm, n_lat, tpe):
    nrows = gate.shape[0]
    wrows = HP + G * GD

    def body(dx2_ref, mix_ref, u_ref, v_ref, attn_ref, gate_ref, wv_ref, ws_ref, wst_ref, bias_ref, wout_ref, seg, segt,
             dattn_ref, du_ref, dv_ref, dgate_ref, dwout_ref, dws_ref, dbs_ref, dwv_ref, s_scr, dvn_scr, dbias_scr):
        t = pl.program_id(0)
        dx2 = dx2_ref[...]
        dmix = (dx2 * gate_ref[0]).astype(BF16)
        dcat = _mm_nt(dmix, wout_ref[...])
        dattn_ref[...] = dcat[:, :HP].astype(BF16)
        dsg = dcat[:, HP:]

        vraw = v_ref[...]
        vg = _gelu(vraw)
        rg = lax.rsqrt(_dot_hl(vg * vg, seg[...]) * (1.0 / GD) + EPS)
        r64 = _dot_hl(rg, segt[...])
        y = vg * r64
        wv_v = wv_ref[...]
        vn = (y * wv_v).astype(BF16)
        _gating(vn, ws_ref, bias_ref, s_scr, tm)
        uraw = u_ref[...]
        ug = _gelu(uraw)
        s = s_scr[...]
        sg = (ug * s).astype(BF16)
        du_ref[...] = (dsg * s * _gelu_grad(uraw)).astype(BF16)
        ds = dsg * ug

        @pl.when(t == 0)
        def _():
            dwout_ref[...] = jnp.zeros_like(dwout_ref)
            dws_ref[...] = jnp.zeros_like(dws_ref)
            dwv_ref[...] = jnp.zeros_like(dwv_ref)
            dbias_scr[...] = jnp.zeros_like(dbias_scr)

        @pl.when(t % tpe == 0)
        def _():
            dgate_ref[...] = jnp.zeros_like(dgate_ref)

        dgate_ref[0] += _rowsum(dx2 * mix_ref[...].astype(F32))
        dwout_ref[...] += _mm_tn(jnp.concatenate([attn_ref[...], sg], axis=-1), dmix)

        lane = lax.broadcasted_iota(jnp.int32, (CH, LANE), 1)
        for c in range(tm // CH):
            rs = slice(c * CH, (c + 1) * CH)
            dbias_scr[...] += ds[rs, :]
            for j in range(G // 2):
                ls = slice(j * LANE, (j + 1) * LANE)
                dsp32 = ds[rs, ls]
                dsp = dsp32.astype(BF16)
                vp = vn[rs, ls]
                dvn_scr[rs, ls] = jnp.where(lane < GD, _mm(wst_ref[2 * j], dsp), _mm(wst_ref[2 * j + 1], dsp))
                dws_ref[2 * j] += _mm_nt(jnp.where(lane < GD, dsp32, 0.0).astype(BF16), vp)
                dws_ref[2 * j + 1] += _mm_nt(jnp.where(lane < GD, 0.0, dsp32).astype(BF16), vp)

        dvn = dvn_scr[...]
        dwv_ref[...] += _rowsum(dvn * y)
        dy = dvn * wv_v
        mean_g = _dot_hl(dy * y, seg[...]) * (1.0 / GD)
        dvg = r64 * (dy - y * _dot_hl(mean_g, segt[...]))
        dv_ref[...] = (dvg * _gelu_grad(vraw)).astype(BF16)

        @pl.when(t == n_lat - 1)
        def _():
            dbs_ref[...] = _dot_hl(dbias_scr[...], seg[...])

    row = lambda cols: pl.BlockSpec((tm, cols), lambda t: (t, 0))
    r = n_lat * tm
    return _pcall(
        body, name="mix_bwd", grid=(n_lat,),
        out_shape=(_sds((r, HP), BF16), _sds((r, G * GD), BF16), _sds((r, G * GD), BF16), _sds((nrows, 1, D), F32),
                   _sds((wrows, D), F32), _sds((G, CH, CH), F32), _sds((CH, LANE), F32), _sds((1, G * GD), F32)),
        in_specs=[row(D), row(D), row(G * GD), row(G * GD), row(HP), _mod_spec(1, tpe, nrows), _const((1, G * GD)),
                  _const((G, CH, CH)), _const((G, CH, CH)), _const((CH, G * GD)), _const((wrows, D)),
                  _const((G * GD, LANE)), _const((LANE, G * GD))],
        out_specs=(row(HP), row(G * GD), row(G * GD), _mod_spec(1, tpe, nrows), _const((wrows, D)),
                   _const((G, CH, CH)), _const((CH, LANE)), _const((1, G * GD))),
        scratch=[pltpu.VMEM((tm, G * GD), F32), pltpu.VMEM((tm, G * GD), F32), pltpu.VMEM((CH, G * GD), F32)],
        vmem_mb=56)(dx2, mix, u, v, attn, gate, wv, ws, wst, bias, wout, cs["seg_g"], cs["seg_gt"])


def _adamw_math(w, g, m, v):
    m2 = ADAM_B1 * m + (1.0 - ADAM_B1) * g
    v2 = ADAM_B2 * v + (1.0 - ADAM_B2) * (g * g)
    m_hat = m2 / (1.0 - ADAM_B1 ** ADAM_STEP)
    v_hat = v2 / (1.0 - ADAM_B2 ** ADAM_STEP)
    delta = -ADAM_LR * (m_hat / (jnp.sqrt(v_hat) + ADAM_EPS) + ADAM_WD * w)
    return delta, m2, v2


def _row_tile(r, c):
    best = r
    for tr in range(8, r, 8):
        if r % tr == 0 and tr * c * 4 <= MIB:
            best = tr
    return best


def _adamw(w, g, m, v, name):
    r, c = w.shape
    tr = _row_tile(r, c)

    def body(w_ref, g_ref, m_ref, v_ref, d_ref, mo_ref, vo_ref):
        d_ref[...], mo_ref[...], vo_ref[...] = _adamw_math(w_ref[...], g_ref[...], m_ref[...], v_ref[...])

    blk = pl.BlockSpec((tr, c), lambda t: (t, 0))
    return _pcall(body, name=name, grid=(r // tr,), out_shape=(_sds((r, c), F32),) * 3,
                  in_specs=[blk] * 4, out_specs=(blk,) * 3)(w, g, m, v)


def _adamw_small(params):
    n = len(params)

    def body(*refs):
        ins, outs = refs[:4 * n], refs[4 * n:]
        for i in range(n):
            w, g, m, v = (ins[4 * i + k][...] for k in range(4))
            if i == 0:
                sig = _sigmoid(w)
                g = g * (sig * (1.0 + w * (1.0 - sig)))
            d, m2, v2 = _adamw_math(w, g, m, v)
            outs[4 * i][...] = g
            outs[4 * i + 1][...] = d
            outs[4 * i + 2][...] = m2
            outs[4 * i + 3][...] = v2

    flat = [a for p in params for a in p]
    out_shape = tuple(_sds(p[0].shape, F32) for p in params for _ in range(4))
    res = _pcall(body, name="adamw_small", out_shape=out_shape, in_specs=[VMEM] * (4 * n),
                 out_specs=(VMEM,) * (4 * n))(*flat)
    return [res[4 * i:4 * i + 4] for i in range(n)]


def _rope_tables(s):
    rows = jnp.repeat(jnp.arange(s // GRID_W, dtype=F32), GRID_W)
    cols = jnp.tile(jnp.arange(GRID_W, dtype=F32), s // GRID_W)
    half = DR // 2
    inv = ROPE_BASE ** (-jnp.arange(0, half, 2, dtype=F32) / half)
    ang_r = rows[:, None] * inv
    ang_c = cols[:, None] * inv
    ang = jnp.concatenate([ang_r, ang_r, ang_c, ang_c], axis=-1)
    return jnp.cos(ang), jnp.sin(ang)


def _head_pad(a, real):
    return jnp.pad(a, ((0, 0), (0, LANE - real), (0, 0))).reshape(HP, a.shape[2])


def kernel(x, c, ctx, c_ctx, w_ada, b_ada, norm1_w, ffn1_w1, ffn1_w3, ffn1_w2, norm2_w, w_in, q_a_norm_w, w_uq, kv_a_norm_w, w_ukv, q_norm_w, k_norm_w, v_norm_w, w_s, b_s, w_out, norm3_w, ffn2_w1, ffn2_w3, ffn2_w2, loss_target, m_c_ctx, m_w_ada, m_b_ada, m_norm1_w, m_ffn1_w1, m_ffn1_w3, m_ffn1_w2, m_norm2_w, m_w_in, m_q_a_norm_w, m_w_uq, m_kv_a_norm_w, m_w_ukv, m_q_norm_w, m_k_norm_w, m_v_norm_w, m_w_s, m_b_s, m_w_out, m_norm3_w, m_ffn2_w1, m_ffn2_w3, m_ffn2_w2, v_c_ctx, v_w_ada, v_b_ada, v_norm1_w, v_ffn1_w1, v_ffn1_w3, v_ffn1_w2, v_norm2_w, v_w_in, v_q_a_norm_w, v_w_uq, v_kv_a_norm_w, v_w_ukv, v_q_norm_w, v_k_norm_w, v_v_norm_w, v_w_s, v_b_s, v_w_out, v_norm3_w, v_ffn2_w1, v_ffn2_w3, v_ffn2_w2):
    nb, s, _ = x.shape
    nc = ctx.shape[1]
    tm = 256 if nc % 256 == 0 else 128
    tpe = s // tm
    n_lat = nb * tpe
    n_all = n_lat + nb * nc // tm
    r_lat = nb * s
    me = 4 * lax.axis_index("x") + 2 * lax.axis_index("y") + lax.axis_index("c")
    cs = _consts()
    ncol = w_ada.shape[2]
    fsh = ffn1_w1.shape[2]
    assert nb + 1 <= 8 and NDEV * fsh == FF and NDEV * ncol == NMOD * D and s % nc == 0 and nc % tm == 0

    a_loc = jnp.concatenate([c, c_ctx[None, :], jnp.zeros((7 - nb, D), F32)], axis=0)
    a_raw = _all_gather(a_loc, "gather_c").reshape(NDEV * 8, D)
    mod_cols = _ada_fwd(a_raw, w_ada[0], lax.dynamic_slice_in_dim(b_ada, me * ncol, ncol, axis=1))
    mod_all = _all_gather(mod_cols, "gather_mod")
    mod_mine = lax.dynamic_slice_in_dim(mod_all, 8 * me, 8, axis=1)
    modtab = mod_mine.transpose(1, 0, 2).reshape(8, NMOD, D)[:nb + 1]

    def t16(a):
        return a.T.astype(BF16)

    wpack1 = jnp.concatenate([t16(ffn1_w1[0]), t16(ffn1_w3[0]), ffn1_w2[0].astype(BF16)], axis=0)
    wpack2 = jnp.concatenate([
        t16(ffn2_w1[0]), t16(ffn2_w3[0]), ffn2_w2[0].astype(BF16),
        t16(w_in[0]), jnp.zeros((12, D), BF16),
        w_out[0].astype(BF16),
        t16(w_uq[0]).reshape(24, D), jnp.zeros((8, D), BF16),
        t16(w_ukv[0]).reshape(16, D)], axis=0)
    wall1 = _all_gather(wpack1, "gather_w_ffn1")

    def head_w(wn):
        return jnp.tile(jnp.pad(wn, ((0, 0), (0, LANE - DH))), (1, H))

    wq, wk = head_w(q_norm_w), head_w(k_norm_w)
    wv = v_norm_w.reshape(1, G * GD)
    ws16 = w_s[0].astype(BF16)
    wst16 = w_s[0].transpose(0, 2, 1).astype(BF16)
    bias = jnp.repeat(b_s[0].T, GD, axis=1)
    cos, sin = _rope_tables(s)
    cos = jnp.pad(cos, ((0, 0), (DN, LANE - DH)), constant_values=1.0)
    sin = jnp.pad(sin, ((0, 0), (DN, LANE - DH)))
    cos_k = jnp.concatenate([cos, jnp.ones((tm, LANE), F32)], axis=0)
    sin_k = jnp.concatenate([sin, jnp.zeros((tm, LANE), F32)], axis=0)

    xs = (x.reshape(r_lat, D), ctx.reshape(nb * nc, D))
    tmf = 2 * tm if s % (2 * tm) == 0 and (nb * nc) % (2 * tm) == 0 else tm
    x1, a1, b1, o1, wall2 = _ffn_fwd(xs, modtab[:, 0:3], norm1_w, wall1, 0, tm=tmf, n_tiles=(r_lat + nb * nc) // tmf,
                                     tpe=s // tmf, n_lat=r_lat // tmf, name="ffn1_fwd", gather=wpack2)

    o0 = 3 * fsh
    wint = wall2[:, o0:o0 + 180].reshape(IN_COLS, D)
    z = lambda n: jnp.zeros((n, D), BF16)
    wint = jnp.concatenate([wint[0:128], wint[160:416], wint[416:928], wint[928:1440],
                            z(DN), wint[128:160], z(LANE - DH)], axis=0)
    wout = wall2[:, o0 + 192:o0 + 320].reshape(D, D)
    wout = jnp.concatenate([_head_pad(wout[:H * DV].reshape(H, DV, D), DV), wout[H * DV:]], axis=0)
    wuq = _head_pad(wall2[:, o0 + 320:o0 + 344].reshape(H, DH, QL), DH)
    wukvt = wall2[:, o0 + 352:o0 + 368].reshape(H, DN + DV, KVL)
    wukv = jnp.concatenate([_head_pad(wukvt[:, :DN], DN), _head_pad(wukvt[:, DN:], DV)], axis=0)

    ckv, qp, u_raw, v_raw, kpe = _proj_fwd(x1, modtab[:, 3:5], norm2_w, wint, tm=tm, n_tiles=n_all, tpe=tpe)
    q = _q_prep_fwd(qp, q_a_norm_w, wuq, wq, cos, sin, cs, tm=tm, n_lat=n_lat, tpe=tpe)
    k, v = _kv_prep_fwd(ckv, kpe, kv_a_norm_w, wukv, wk, cos_k, sin_k, cs, tm=tm, n_tiles=n_all, tpe=tpe, n_lat=n_lat)
    attn = _attn_fwd(q, k, v, nb=nb, s=s, nc=nc, tq=tm)
    x2, mix = _mix_fwd(u_raw, v_raw, attn, x1, modtab[:nb, 5:6], wv, ws16, bias, wout, cs,
                       tm=tm, n_lat=n_lat, tpe=tpe)
    dy, a2, b2, o2, lsum = _ffn_fwd((x2,), modtab[:nb, 6:9], norm3_w, wall2, 0, tm=tmf, n_tiles=r_lat // tmf,
                                    tpe=s // tmf, n_lat=r_lat // tmf, name="ffn2_fwd",
                                    target=loss_target.reshape(r_lat, D))
    loss = lax.psum(lsum[0, 0] * (0.5 / D), ("x", "y", "c"))

    tr = 2 * tm if n_lat % 2 == 0 and n_all % 2 == 0 else tm
    dx2, da2, db2, g2, do2, h2, dmod678, dnorm3 = _ffn_bwd_dx(
        dy, (x2,), a2, b2, o2, modtab[:nb, 6:9], norm3_w, wall2, 0,
        tm=tm, n_tiles=n_lat, tpe=tpe, n_lat=n_lat, name="ffn2_bwd_dx")
    g_ffn2 = _ffn_bwd_dw(h2, do2, da2, db2, g2, tr=tr, name="ffn2_bwd_dw")
    part_ffn2 = _add_sibling(g_ffn2, _scatter_sibling([g_ffn2], "scatter_sibling_ffn2")[0], 176, "add_sibling_ffn2")

    dattn, du, dv, dgate5, dwout, dws, dbs, dwv = _mix_bwd(
        dx2, mix, u_raw, v_raw, attn, modtab[:nb, 5:6], wv, ws16, wst16, bias, wout, cs, tm=tm, n_lat=n_lat, tpe=tpe)
    tq = 2 * tm if s % (2 * tm) == 0 else tm
    dq, dk_l, dk_c, dv_l, dv_c, recv_ffn2 = _attn_bwd(q, k, v, attn, dattn, part_ffn2, nb=nb, s=s, nc=nc, tq=tq)
    dqp, dwuq, dqa, dwq = _q_prep_bwd(dq, qp, q_a_norm_w, wuq, wq, cos, sin, cs, tm=tm, n_lat=n_lat, tpe=tpe)
    dckv, dkpe, dwukv, dkva, dwk = _kv_prep_bwd((dk_l, dk_c), (dv_l, dv_c), ckv, kpe, kv_a_norm_w, wukv, wk,
                                                cos_k, sin_k, cs, tm=tm, n_tiles=n_all, tpe=tpe, n_lat=n_lat)
    dx1, dwin, dmod34, dnorm2 = _proj_bwd(dckv, dkpe, dqp, du, dv, dx2, x1, modtab[:, 3:5], norm2_w, wint,
                                          tm=tm, n_tiles=n_all, tpe=tpe, n_lat=n_lat)
    dx0, da1, db1, g1, do1, h1, dmod012, dnorm1 = _ffn_bwd_dx(
        dx1, xs, a1, b1, o1, modtab[:, 0:3], norm1_w, wall1, 0,
        tm=tm, n_tiles=n_all, tpe=tpe, n_lat=n_lat, name="ffn1_bwd_dx")
    g_ffn1 = _ffn_bwd_dw(h1, do1, da1, db1, g1, tr=tr, name="ffn1_bwd_dw")
    grad_x = dx0.reshape(nb, s, D)

    zrow = jnp.zeros((1, D), F32)
    g_lat = jnp.concatenate([dmod012[:nb, 0], dmod012[:nb, 1], dmod012[:nb, 2], dmod34[:nb, 0], dmod34[:nb, 1],
                             dgate5[:, 0], dmod678[:, 0], dmod678[:, 1], dmod678[:, 2]], axis=1)
    g_ctx = jnp.concatenate([dmod012[nb:, 0], dmod012[nb:, 1], dmod012[nb:, 2], dmod34[nb:, 0], dmod34[nb:, 1],
                             zrow, zrow, zrow, zrow], axis=1)
    g_loc = jnp.concatenate([g_lat, g_ctx, jnp.zeros((7 - nb, NMOD * D), F32)], axis=0)
    g_all = _all_gather(g_loc, "gather_gmod").reshape(NDEV * 8, NMOD * D)
    g_cols = lax.dynamic_slice_in_dim(g_all, me * ncol, ncol, axis=1)
    g_w_ada, pc_ctx, g_b_ada = _ada_bwd(a_raw, c_ctx.reshape(D, 1), g_all, g_cols, w_ada[0], nb)

    def blocks(a):
        return a.reshape(NDEV, a.shape[0] // NDEV, D)

    dwin_o = jnp.concatenate([dwin[0:128], dwin[KPE_LO:KPE_LO + DR], dwin[128:384], dwin[384:896], dwin[896:1408]],
                             axis=0)
    dwout_o = jnp.concatenate([dwout[:HP].reshape(H, LANE, D)[:, :DV].reshape(H * DV, D), dwout[HP:]], axis=0)
    dwuq_o = dwuq.reshape(H, LANE, QL)[:, :DH]
    dwukv_o = jnp.concatenate([dwukv[:HP].reshape(H, LANE, KVL)[:, :DN], dwukv[HP:].reshape(H, LANE, KVL)[:, :DV]],
                              axis=1)
    gmisc = jnp.concatenate([
        blocks(dwin_o).astype(BF16), jnp.zeros((NDEV, 12, D), BF16),
        blocks(dwout_o).astype(BF16),
        dwuq_o.reshape(NDEV, 24, D).astype(BF16), jnp.zeros((NDEV, 8, D), BF16),
        dwukv_o.reshape(NDEV, 16, D).astype(BF16)], axis=1)
    gots = _scatter_sibling([g_ffn1, gmisc], "scatter_sibling")
    parts = [_add_sibling(g_ffn1, gots[0], 176, "add_sibling_ffn1"), _add_sibling(gmisc, gots[1], 368, "add_sibling_misc")]
    recv = _scatter_chips(parts, "scatter_chips")
    gsum1 = _sum_chips(parts[0], recv[0], 176, "sum_grads_ffn1")
    gsum2 = _sum_chips(part_ffn2, recv_ffn2, 176, "sum_grads_ffn2")
    msum = _sum_chips(parts[1], recv[1], 368, "sum_grads_misc")

    g_big = {
        "ffn1_w1": gsum1[0:fsh].T, "ffn1_w3": gsum1[fsh:2 * fsh].T, "ffn1_w2": gsum1[2 * fsh:3 * fsh],
        "ffn2_w1": gsum2[0:fsh].T, "ffn2_w3": gsum2[fsh:2 * fsh].T, "ffn2_w2": gsum2[2 * fsh:3 * fsh],
        "w_in": msum[0:180].T, "w_out": msum[192:320],
        "w_uq": msum[320:344].reshape(DH, QL).T, "w_ukv": msum[352:368].reshape(DN + DV, KVL).T,
        "w_ada": g_w_ada,
    }

    def prow(a):
        a = a.reshape(1, -1)
        return jnp.concatenate([a, jnp.zeros((1, D - a.shape[1]), F32)], axis=1)

    g_qn = dwq.reshape(H, LANE)[:, :DH].sum(0)
    g_kn = dwk.reshape(H, LANE)[:, :DH].sum(0)
    spack = jnp.concatenate([
        dnorm1, dnorm2, dnorm3, prow(dqa), prow(dkva), prow(g_qn), prow(g_kn), prow(dwv),
        prow(dbs[:, :G].T), prow(pc_ctx), jnp.zeros((6, D), F32), dws.reshape(CH, D)], axis=0)
    ssum = _sum_slots(_all_gather(spack, "gather_small"), 144, "sum_small")

    big_in = {
        "w_ada": (w_ada, m_w_ada, v_w_ada), "ffn1_w1": (ffn1_w1, m_ffn1_w1, v_ffn1_w1),
        "ffn1_w3": (ffn1_w3, m_ffn1_w3, v_ffn1_w3), "ffn1_w2": (ffn1_w2, m_ffn1_w2, v_ffn1_w2),
        "w_in": (w_in, m_w_in, v_w_in), "w_uq": (w_uq, m_w_uq, v_w_uq), "w_ukv": (w_ukv, m_w_ukv, v_w_ukv),
        "w_out": (w_out, m_w_out, v_w_out), "ffn2_w1": (ffn2_w1, m_ffn2_w1, v_ffn2_w1),
        "ffn2_w3": (ffn2_w3, m_ffn2_w3, v_ffn2_w3), "ffn2_w2": (ffn2_w2, m_ffn2_w2, v_ffn2_w2),
    }
    res = {}
    for nm, (w, m, v_) in big_in.items():
        g = g_big[nm]
        d_, m_, v2_ = _adamw(w[0], g, m[0], v_[0], "adamw_" + nm)
        res[nm] = tuple(a[None] for a in (g, d_, m_, v2_))

    small_in = [
        ("c_ctx", c_ctx, m_c_ctx, v_c_ctx, ssum[9:10], (1, D)),
        ("b_ada", b_ada, m_b_ada, v_b_ada, g_b_ada, (1, NMOD * D)),
        ("norm1_w", norm1_w, m_norm1_w, v_norm1_w, ssum[0:1], (1, D)),
        ("norm2_w", norm2_w, m_norm2_w, v_norm2_w, ssum[1:2], (1, D)),
        ("norm3_w", norm3_w, m_norm3_w, v_norm3_w, ssum[2:3], (1, D)),
        ("q_a_norm_w", q_a_norm_w, m_q_a_norm_w, v_q_a_norm_w, ssum[3:4, :QL], (1, QL)),
        ("kv_a_norm_w", kv_a_norm_w, m_kv_a_norm_w, v_kv_a_norm_w, ssum[4:5, :KVL], (1, KVL)),
        ("q_norm_w", q_norm_w, m_q_norm_w, v_q_norm_w, ssum[5:6, :DH], (1, DH)),
        ("k_norm_w", k_norm_w, m_k_norm_w, v_k_norm_w, ssum[6:7, :DH], (1, DH)),
        ("v_norm_w", v_norm_w, m_v_norm_w, v_v_norm_w, ssum[7:8, :G * GD], (G, GD)),
        ("b_s", b_s, m_b_s, v_b_s, ssum[8:9], (G, CH)),
        ("w_s", w_s, m_w_s, v_w_s, ssum[16:144], (G * CH, CH)),
    ]
    small_out = _adamw_small(
        [(w.reshape(sh), g.reshape(sh), m.reshape(sh), v_.reshape(sh)) for _, w, m, v_, g, sh in small_in])
    for (nm, w, *_), outs in zip(small_in, small_out):
        res[nm] = tuple(a.reshape(w.shape) for a in outs)

    order = ["c_ctx", "w_ada", "b_ada", "norm1_w", "ffn1_w1", "ffn1_w3", "ffn1_w2", "norm2_w", "w_in", "q_a_norm_w",
             "w_uq", "kv_a_norm_w", "w_ukv", "q_norm_w", "k_norm_w", "v_norm_w", "w_s", "b_s", "w_out", "norm3_w",
             "ffn2_w1", "ffn2_w3", "ffn2_w2"]
    return (loss, grad_x, *[res[n][0] for n in order], *[res[n][1] for n in order],
            *[res[n][2] for n in order], *[res[n][3] for n in order])
```

```python
import numpy as np
import jax
import jax.numpy as jnp
from jax import lax
from jax.experimental import pallas as pl
from jax.experimental.pallas import tpu as pltpu

F32 = jnp.float32
BF16 = jnp.bfloat16

D = 1024
FF = 2816
FC = 256
H = 8
DN, DR, DV = 64, 32, 64
DH = DN + DR
QL, KVL = 256, 128
G, GD, CH = 8, 64, 128
NMOD = 9
EPS = 1e-6
GRID_W = 64
ROPE_BASE = 10000.0
NDEV = 8
LANE = 128
HP = H * LANE
IN_COLS = 1440
WIN_ROWS = 1536
KPE_LO = 1408 + DN
NFFN_W = 6
MIB = 1 << 20

ADAM_LR, ADAM_B1, ADAM_B2, ADAM_EPS, ADAM_WD, ADAM_STEP = 0.001, 0.9, 0.999, 1e-08, 0.01, 10

MESH = pl.DeviceIdType.MESH
ANY = pl.BlockSpec(memory_space=pl.ANY)
VMEM = pl.BlockSpec(memory_space=pltpu.VMEM)


def _mm(a, b):
    return jnp.dot(a, b, preferred_element_type=F32)


def _mm_nt(a, b):
    return lax.dot_general(a, b, (((1,), (1,)), ((), ())), preferred_element_type=F32)


def _mm_tn(a, b):
    return lax.dot_general(a, b, (((0,), (0,)), ((), ())), preferred_element_type=F32)


def _dot_hl(x, m):
    hi = x.astype(BF16)
    lo = (x - hi.astype(F32)).astype(BF16)
    return _mm(hi, m) + _mm(lo, m)


def _sigmoid(a):
    return 1.0 / (1.0 + jnp.exp(-a))


_G0 = 0.7978845608028654
_G1 = 0.044715


def _gelu(x):
    return 0.5 * x * (1.0 + jnp.tanh(_G0 * (x + _G1 * (x * x * x))))


def _gelu_grad(x):
    th = jnp.tanh(_G0 * (x + _G1 * (x * x * x)))
    return 0.5 * (1.0 + th) + 0.5 * x * (1.0 - th * th) * (_G0 * (1.0 + 3.0 * _G1 * x * x))


def _rowsum(y):
    return jnp.sum(y, axis=0, keepdims=True)


def _rms(x):
    return lax.rsqrt(jnp.mean(x * x, axis=-1, keepdims=True) + EPS)


def _pcall(body, *, name, out_shape, in_specs, out_specs, grid=None, scratch=(), vmem_mb=32, aliases=None):
    kw = {}
    if grid is not None:
        kw["grid"] = grid
        sem = ("arbitrary",) * len(grid)
    else:
        sem = None
    if aliases:
        kw["input_output_aliases"] = aliases
    return pl.pallas_call(
        body, name=name, out_shape=out_shape, in_specs=in_specs, out_specs=out_specs,
        scratch_shapes=list(scratch),
        compiler_params=pltpu.CompilerParams(dimension_semantics=sem, vmem_limit_bytes=vmem_mb * MIB),
        **kw)


def _const(shape):
    nd = len(shape)
    return pl.BlockSpec(shape, lambda *_: (0,) * nd)


def _sds(shape, dt):
    return jax.ShapeDtypeStruct(shape, dt)


def _consts():
    seg_h = np.zeros((HP, LANE), np.float32)
    seg_h[np.arange(HP), np.arange(HP) // LANE] = 1.0
    seg_g = np.zeros((G * GD, LANE), np.float32)
    seg_g[np.arange(G * GD), np.arange(G * GD) // GD] = 1.0
    rot = np.zeros((LANE, LANE), np.float32)
    for base in (DN, DN + 16):
        for j in range(8):
            rot[base + j + 8, base + j] = -1.0
            rot[base + j, base + j + 8] = 1.0
    c = dict(seg_h=seg_h, seg_ht=seg_h.T, seg_g=seg_g, seg_gt=seg_g.T, rot=rot, rot_t=rot.T)
    return {k: jnp.asarray(v, BF16) for k, v in c.items()}


_GATHER_SEMS = [pltpu.SemaphoreType.DMA((7,)), pltpu.SemaphoreType.DMA((7,)), pltpu.SemaphoreType.DMA(())]


def _gather_phases(x_ref, out_ref, send_sems, recv_sems, local_sem):
    mx, my, mc = lax.axis_index("x"), lax.axis_index("y"), lax.axis_index("c")
    me, sibling = (mx, my, mc), (mx, my, 1 - mc)
    chips = [(1 - mx, my), (mx, 1 - my), (1 - mx, 1 - my)]

    def blk(px, py, pc):
        return out_ref.at[4 * px + 2 * py + pc]

    def copy(k, block, to, src=None):
        return pltpu.make_async_remote_copy(
            src_ref=blk(*block) if src is None else src, dst_ref=blk(*block),
            send_sem=send_sems.at[k], recv_sem=recv_sems.at[k], device_id=to, device_id_type=MESH)

    mine = pltpu.make_async_copy(x_ref, blk(*me), local_sem)
    first = [copy(0, me, sibling, src=x_ref)]
    first += [copy(1 + j, me, (*chip, mc), src=x_ref) for j, chip in enumerate(chips)]
    passed = [copy(4 + j, (*chip, mc), sibling) for j, chip in enumerate(chips)]

    def start():
        mine.start()
        for cp in first:
            cp.start()

    def forward():
        for j, chip in enumerate(chips):
            copy(1 + j, (*chip, mc), me).wait_recv()
            passed[j].start()

    def finish():
        copy(0, sibling, me).wait_recv()
        for j, chip in enumerate(chips):
            copy(4 + j, (*chip, 1 - mc), me).wait_recv()
        for cp in first + passed:
            cp.wait_send()
        mine.wait()

    return start, forward, finish


def _all_gather(x, name):
    r, c = x.shape

    def body(x_ref, out_ref, send_sems, recv_sems, local_sem):
        start, forward, finish = _gather_phases(x_ref, out_ref, send_sems, recv_sems, local_sem)
        start()
        forward()
        finish()

    return pl.pallas_call(
        body, name=name, out_shape=_sds((NDEV, r, c), x.dtype), in_specs=[ANY], out_specs=ANY,
        scratch_shapes=list(_GATHER_SEMS),
    )(x)


def _chip_sends(p_ref, out_ref, send_sems, recv_sems):
    mx, my, mc = lax.axis_index("x"), lax.axis_index("y"), lax.axis_index("c")
    peers = [(1 - mx, my), (mx, 1 - my), (1 - mx, 1 - my)]
    return [pltpu.make_async_remote_copy(
        src_ref=p_ref.at[2 * px + py], dst_ref=out_ref.at[j], send_sem=send_sems.at[j], recv_sem=recv_sems.at[j],
        device_id=(px, py, mc), device_id_type=MESH) for j, (px, py) in enumerate(peers)]


def _scatter_sibling(xs, name):
    n = len(xs)

    def body(*refs):
        x_refs, got_refs = refs[:n], refs[n:2 * n]
        send_sems, recv_sems = refs[2 * n:]
        mx, my, mc = lax.axis_index("x"), lax.axis_index("y"), lax.axis_index("c")
        sibling = (mx, my, 1 - mc)
        remote = []
        for i in range(n):
            for j in range(4):
                k = 4 * i + j
                remote.append(pltpu.make_async_remote_copy(
                    src_ref=x_refs[i].at[2 * j + 1 - mc], dst_ref=got_refs[i].at[j],
                    send_sem=send_sems.at[k], recv_sem=recv_sems.at[k], device_id=sibling, device_id_type=MESH))
        for cp in remote:
            cp.start()
        for cp in remote:
            cp.wait_recv()
        for cp in remote:
            cp.wait_send()

    shapes = tuple(_sds((4,) + x.shape[1:], x.dtype) for x in xs)
    return pl.pallas_call(
        body, name=name, out_shape=shapes, in_specs=[ANY] * n, out_specs=(ANY,) * n,
        scratch_shapes=[pltpu.SemaphoreType.DMA((4 * n,)), pltpu.SemaphoreType.DMA((4 * n,))],
    )(*xs)


def _scatter_chips(ps, name):
    n = len(ps)

    def body(*refs):
        p_refs, out_refs, sems = refs[:n], refs[n:2 * n], refs[2 * n:]
        sends = []
        for i in range(n):
            sends += _chip_sends(p_refs[i], out_refs[i], sems[2 * i], sems[2 * i + 1])
        for cp in sends:
            cp.start()
        for cp in sends:
            cp.wait_recv()
        for cp in sends:
            cp.wait_send()

    shapes = tuple(_sds((3,) + p.shape[1:], p.dtype) for p in ps)
    return pl.pallas_call(
        body, name=name, out_shape=shapes, in_specs=[ANY] * n, out_specs=(ANY,) * n,
        scratch_shapes=[pltpu.SemaphoreType.DMA((3,))] * (2 * n),
    )(*ps)


def _add_sibling(x, got, tr, name):
    _, r, c = x.shape

    def body(x_ref, g_ref, o_ref):
        mc = lax.axis_index("c")
        for j in range(4):
            mine = jnp.where(mc == 0, x_ref[2 * j].astype(F32), x_ref[2 * j + 1].astype(F32))
            o_ref[j] = (mine + g_ref[j].astype(F32)).astype(o_ref.dtype)

    return _pcall(body, name=name, grid=(r // tr,), out_shape=_sds(got.shape, got.dtype),
                  in_specs=[pl.BlockSpec((NDEV, tr, c), lambda t: (0, t, 0)), pl.BlockSpec((4, tr, c), lambda t: (0, t, 0))],
                  out_specs=pl.BlockSpec((4, tr, c), lambda t: (0, t, 0)))(x, got)


def _sum_chips(part, recv, tr, name):
    _, r, c = part.shape

    def body(p_ref, r_ref, o_ref):
        slot = 2 * lax.axis_index("x") + lax.axis_index("y")
        acc = p_ref[0].astype(F32)
        for j in range(1, 4):
            acc = jnp.where(slot == j, p_ref[j].astype(F32), acc)
        for j in range(3):
            acc = acc + r_ref[j].astype(F32)
        o_ref[...] = acc

    return _pcall(body, name=name, grid=(r // tr,), out_shape=_sds((r, c), F32),
                  in_specs=[pl.BlockSpec((4, tr, c), lambda t: (0, t, 0)), pl.BlockSpec((3, tr, c), lambda t: (0, t, 0))],
                  out_specs=pl.BlockSpec((tr, c), lambda t: (t, 0)))(part, recv)


def _sum_slots(x, tr, name):
    n, r, c = x.shape

    def body(x_ref, o_ref):
        acc = x_ref[0].astype(F32)
        for s in range(1, n):
            acc = acc + x_ref[s].astype(F32)
        o_ref[...] = acc

    return _pcall(body, name=name, grid=(r // tr,), out_shape=_sds((r, c), F32),
                  in_specs=[pl.BlockSpec((n, tr, c), lambda t: (0, t, 0))],
                  out_specs=pl.BlockSpec((tr, c), lambda t: (t, 0)))(x)


def _ada_fwd(a_raw, w_loc, b_loc):
    ncol = w_loc.shape[1]

    def body(a_ref, w_ref, b_ref, o_ref):
        a = a_ref[...]
        act = (a * _sigmoid(a)).astype(BF16)
        o_ref[...] = _mm(act, w_ref[...].astype(BF16)) + b_ref[...]

    return _pcall(body, name="ada_fwd", out_shape=_sds((a_raw.shape[0], ncol), F32),
                  in_specs=[VMEM] * 3, out_specs=VMEM)(a_raw, w_loc, b_loc)


def _ada_bwd(a_raw, cctx_col, g_all, g_cols, w_loc, nb):
    nrow = a_raw.shape[0]
    ncol = w_loc.shape[1]

    def body(a_ref, cc_ref, gall_ref, g_ref, w_ref, dw_ref, pc_ref, gb_ref):
        a = a_ref[...]
        rowid = lax.broadcasted_iota(jnp.int32, (nrow, 1), 0) % 8
        act = jnp.where(rowid < nb, a * _sigmoid(a), 0.0).astype(BF16)
        g = g_ref[...]
        gc = _rowsum(jnp.where(rowid == nb, g, 0.0))
        cc = cc_ref[...]
        dw_ref[...] = _mm_tn(act, g.astype(BF16)) + (cc * _sigmoid(cc)) * gc
        pc_ref[...] = jnp.sum(w_ref[...] * gc, axis=1, keepdims=True)
        gb_ref[...] = _rowsum(gall_ref[...])

    return _pcall(body, name="ada_bwd",
                  out_shape=(_sds((D, ncol), F32), _sds((D, 1), F32), _sds((1, g_all.shape[1]), F32)),
                  in_specs=[VMEM] * 5, out_specs=(VMEM,) * 3, vmem_mb=48)(a_raw, cctx_col, g_all, g_cols, w_loc)


def _mod_spec(k, tpe, nrows):
    return pl.BlockSpec((1, k, D), lambda t: (jnp.minimum(t // tpe, nrows - 1), 0, 0))


def _load_ffn_weights(wall_ref, first, bufs, sems):
    fsh = FF // NDEV
    cps = []
    for j, buf in enumerate(bufs):
        for d in range(NDEV):
            cps.append(pltpu.make_async_copy(wall_ref.at[d, pl.ds((first + j) * fsh, fsh)],
                                             buf.at[pl.ds(d * fsh, fsh)], sems.at[j * NDEV + d]))
    for cp in cps:
        cp.start()
    for cp in cps:
        cp.wait()


def _token_specs(xs, tm, n_lat):
    specs = [pl.BlockSpec((tm, D), lambda t: (jnp.minimum(t, n_lat - 1), 0))]
    if len(xs) == 2:
        specs.append(pl.BlockSpec((tm, D), lambda t: (jnp.maximum(t - n_lat, 0), 0)))
    return specs


def _ffn_fwd(xs, mod3, norm_w, wall, first, *, tm, n_tiles, tpe, n_lat, name, target=None, gather=None):
    nrows = mod3.shape[0]
    r = n_tiles * tm
    nx = len(xs)
    with_loss = target is not None
    with_gather = gather is not None
    fwd_step = max(2 * n_tiles // 3, 1)

    def body(*refs):
        x_refs = refs[:nx]
        pos = nx
        if with_loss:
            tgt_ref = refs[pos]
            pos += 1
        mod_ref, nw_ref, wall_ref = refs[pos:pos + 3]
        pos += 3
        if with_gather:
            gin_ref = refs[pos]
            pos += 1
        xo_ref, a_ref, b_ref, o_ref = refs[pos:pos + 4]
        pos += 4
        if with_loss:
            ls_ref = refs[pos]
            pos += 1
        if with_gather:
            gout_ref = refs[pos]
            pos += 1
        w1_ref, w3_ref, w2_ref, wsem, acc_ref = refs[pos:pos + 5]
        t = pl.program_id(0)
        if with_gather:
            g_start, g_forward, g_finish = _gather_phases(gin_ref, gout_ref, *refs[pos + 5:])

        @pl.when(t == 0)
        def _():
            if with_gather:
                g_start()
            _load_ffn_weights(wall_ref, first, (w1_ref, w3_ref, w2_ref), wsem)
            if with_loss:
                ls_ref[...] = jnp.zeros_like(ls_ref)

        if with_gather:
            @pl.when(t == fwd_step)
            def _():
                g_forward()

            @pl.when(t == n_tiles - 1)
            def _():
                g_finish()

        x = x_refs[0][...]
        if nx == 2:
            x = jnp.where(t < n_lat, x, x_refs[1][...])
        n = x * _rms(x) * nw_ref[...]
        shift, scale, gate = mod_ref[0, 0:1, :], mod_ref[0, 1:2, :], mod_ref[0, 2:3, :]
        h = (n * (1.0 + scale) + shift).astype(BF16)
        for j in range(FF // FC):
            sl = slice(j * FC, (j + 1) * FC)
            a = _mm_nt(h, w1_ref[sl, :])
            b = _mm_nt(h, w3_ref[sl, :])
            a_ref[:, sl] = a.astype(BF16)
            b_ref[:, sl] = b.astype(BF16)
            g = (a * _sigmoid(a) * b).astype(BF16)
            part = _mm(g, w2_ref[sl, :])
            if j == 0:
                acc_ref[...] = part
            else:
                acc_ref[...] += part
        o = acc_ref[...]
        o_ref[...] = o.astype(BF16)
        out = x + (0.5 * gate) * o
        if with_loss:
            d = out - tgt_ref[...]
            xo_ref[...] = d * (1.0 / D)
            ls_ref[...] += jnp.sum(d * d)
        else:
            xo_ref[...] = out

    row = lambda cols: pl.BlockSpec((tm, cols), lambda t: (t, 0))
    in_specs = _token_specs(xs, tm, n_lat) + ([row(D)] if with_loss else []) + [
        _mod_spec(3, tpe, nrows), _const((1, D)), ANY]
    out_shape = [_sds((r, D), F32), _sds((r, FF), BF16), _sds((r, FF), BF16), _sds((r, D), BF16)]
    out_specs = [row(D), row(FF), row(FF), row(D)]
    scratch = [pltpu.VMEM((FF, D), BF16)] * 3 + [pltpu.SemaphoreType.DMA((3 * NDEV,)), pltpu.VMEM((tm, D), F32)]
    if with_loss:
        out_shape.append(_sds((8, LANE), F32))
        out_specs.append(_const((8, LANE)))
    args = list(xs) + ([target] if with_loss else []) + [mod3, norm_w, wall]
    if with_gather:
        assert n_tiles >= 2
        in_specs.append(ANY)
        args.append(gather)
        out_shape.append(_sds((NDEV,) + gather.shape, gather.dtype))
        out_specs.append(ANY)
        scratch += _GATHER_SEMS
    return _pcall(
        body, name=name, grid=(n_tiles,), out_shape=tuple(out_shape), in_specs=in_specs, out_specs=tuple(out_specs),
        scratch=scratch, vmem_mb=56)(*args)


def _ffn_bwd_dx(dout, xs, a, b, o, mod3, norm_w, wall, first, *, tm, n_tiles, tpe, n_lat, name):
    nrows = mod3.shape[0]
    r = n_tiles * tm
    nx = len(xs)

    def body(*refs):
        dout_ref = refs[0]
        x_refs = refs[1:1 + nx]
        (a_ref, b_ref, o_ref, mod_ref, nw_ref, wall_ref,
         dx_ref, da_ref, db_ref, g_ref, do_ref, h_ref, dmod_ref, dnw_ref,
         w1_ref, w3_ref, w2_ref, wsem, acc_ref) = refs[1 + nx:]
        t = pl.program_id(0)

        @pl.when(t == 0)
        def _():
            _load_ffn_weights(wall_ref, first, (w1_ref, w3_ref, w2_ref), wsem)
            dnw_ref[...] = jnp.zeros_like(dnw_ref)

        x = x_refs[0][...]
        if nx == 2:
            x = jnp.where(t < n_lat, x, x_refs[1][...])
        dout = dout_ref[...]
        rr = _rms(x)
        xh = x * rr
        nw = nw_ref[...]
        n = xh * nw
        shift, scale, gate = mod_ref[0, 0:1, :], mod_ref[0, 1:2, :], mod_ref[0, 2:3, :]
        h = (n * (1.0 + scale) + shift).astype(BF16)
        h_ref[...] = h
        d_o = ((0.5 * gate) * dout).astype(BF16)
        do_ref[...] = d_o
        dgate = _rowsum(0.5 * o_ref[...].astype(F32) * dout)
        for j in range(FF // FC):
            sl = slice(j * FC, (j + 1) * FC)
            av = a_ref[:, sl].astype(F32)
            bv = b_ref[:, sl].astype(F32)
            dg = _mm_nt(d_o, w2_ref[sl, :])
            sig = _sigmoid(av)
            sa = av * sig
            g_ref[:, sl] = (sa * bv).astype(BF16)
            da = (dg * bv * (sig * (1.0 + av * (1.0 - sig)))).astype(BF16)
            db = (dg * sa).astype(BF16)
            da_ref[:, sl] = da
            db_ref[:, sl] = db
            part = _mm(da, w1_ref[sl, :]) + _mm(db, w3_ref[sl, :])
            if j == 0:
                acc_ref[...] = part
            else:
                acc_ref[...] += part
        dh = acc_ref[...]
        dn = dh * (1.0 + scale)
        dxh = dn * nw

        @pl.when(t < n_lat)
        def _():
            dx_ref[...] = dout + rr * (dxh - xh * jnp.mean(dxh * xh, axis=-1, keepdims=True))

        first_visit = jnp.where(t < n_lat, t % tpe == 0, t == n_lat)

        @pl.when(first_visit)
        def _():
            dmod_ref[...] = jnp.zeros_like(dmod_ref)

        dmod_ref[0, 0:1, :] += _rowsum(dh)
        dmod_ref[0, 1:2, :] += _rowsum(dh * n)
        dmod_ref[0, 2:3, :] += dgate
        dnw_ref[...] += _rowsum(dn * xh)

    row = lambda cols: pl.BlockSpec((tm, cols), lambda t: (t, 0))
    lat = pl.BlockSpec((tm, D), lambda t: (jnp.minimum(t, n_lat - 1), 0))
    return _pcall(
        body, name=name, grid=(n_tiles,),
        out_shape=(_sds((n_lat * tm, D), F32), _sds((r, FF), BF16), _sds((r, FF), BF16), _sds((r, FF), BF16),
                   _sds((r, D), BF16), _sds((r, D), BF16), _sds((nrows, 3, D), F32), _sds((1, D), F32)),
        in_specs=[row(D)] + _token_specs(xs, tm, n_lat) + [row(FF), row(FF), row(D), _mod_spec(3, tpe, nrows),
                                                            _const((1, D)), ANY],
        out_specs=(lat, row(FF), row(FF), row(FF), row(D), row(D), _mod_spec(3, tpe, nrows), _const((1, D))),
        scratch=[pltpu.VMEM((FF, D), BF16)] * 3 + [pltpu.SemaphoreType.DMA((3 * NDEV,)), pltpu.VMEM((tm, D), F32)],
        vmem_mb=60)(dout, *xs, a, b, o, mod3, norm_w, wall)


def _ffn_bwd_dw(h, d_o, da, db, g, *, tr, name):
    r = h.shape[0]
    fh = FF // 2
    fsh = FF // NDEV
    nk = r // tr

    def body(h_ref, do_ref, da_ref, db_ref, g_ref, out_ref, acc1, acc3, acc2):
        k = pl.program_id(1)

        @pl.when(k == 0)
        def _():
            acc1[...] = jnp.zeros_like(acc1)
            acc3[...] = jnp.zeros_like(acc3)
            acc2[...] = jnp.zeros_like(acc2)

        hv = h_ref[...]
        acc1[...] += _mm_tn(da_ref[...], hv)
        acc3[...] += _mm_tn(db_ref[...], hv)
        acc2[...] += _mm_tn(g_ref[...], do_ref[...])

        @pl.when(k == nk - 1)
        def _():
            for i, acc in enumerate((acc1, acc3, acc2)):
                out_ref[:, i * fsh:(i + 1) * fsh, :] = acc[...].reshape(NDEV // 2, fsh, D).astype(BF16)

    rowd = pl.BlockSpec((tr, D), lambda f, k: (k, 0))
    rowf = pl.BlockSpec((tr, fh), lambda f, k: (k, f))
    return _pcall(
        body, name=name, grid=(2, nk), out_shape=_sds((NDEV, 3 * fsh, D), BF16),
        in_specs=[rowd, rowd, rowf, rowf, rowf],
        out_specs=pl.BlockSpec((NDEV // 2, 3 * fsh, D), lambda f, k: (f, 0, 0)),
        scratch=[pltpu.VMEM((fh, D), F32)] * 3, vmem_mb=56)(h, d_o, da, db, g)


_PIECES = ((0, 128), (128, 384), (384, 896), (896, 1408), (1408, 1536))


def _proj_fwd(x1, mod2, norm_w, wint, *, tm, n_tiles, tpe, name="proj_fwd"):
    nrows = mod2.shape[0]
    r = n_tiles * tm

    def body(x_ref, mod_ref, nw_ref, w_ref, ckv_ref, q_ref, u_ref, v_ref, kpe_ref):
        x = x_ref[...]
        n = x * _rms(x) * nw_ref[...]
        h = (n * (1.0 + mod_ref[0, 1:2, :]) + mod_ref[0, 0:1, :]).astype(BF16)
        for (lo, hi), ref in zip(_PIECES, (ckv_ref, q_ref, u_ref, v_ref, kpe_ref)):
            ref[...] = _mm_nt(h, w_ref[lo:hi, :])

    row = lambda cols: pl.BlockSpec((tm, cols), lambda t: (t, 0))
    widths = [hi - lo for lo, hi in _PIECES]
    return _pcall(
        body, name=name, grid=(n_tiles,),
        out_shape=tuple(_sds((r, w), F32) for w in widths),
        in_specs=[row(D), _mod_spec(2, tpe, nrows), _const((1, D)), _const((WIN_ROWS, D))],
        out_specs=tuple(row(w) for w in widths), vmem_mb=40)(x1, mod2, norm_w, wint)


def _proj_bwd(dckv, dkpe, dq, du, dv, dx2, x1, mod2, norm_w, wint, *, tm, n_tiles, tpe, n_lat, name="proj_bwd"):
    nrows = mod2.shape[0]
    r = n_tiles * tm

    def body(dckv_ref, dkpe_ref, dq_ref, du_ref, dv_ref, dx2_ref, x_ref, mod_ref, nw_ref, w_ref,
             dx_ref, dw_ref, dmod_ref, dnw_ref, acc_ref):
        t = pl.program_id(0)
        is_lat = t < n_lat
        x = x_ref[...]
        rr = _rms(x)
        xh = x * rr
        nw = nw_ref[...]
        n = xh * nw
        scale = mod_ref[0, 1:2, :]
        h = (n * (1.0 + scale) + mod_ref[0, 0:1, :]).astype(BF16)

        @pl.when(t == 0)
        def _():
            dw_ref[...] = jnp.zeros_like(dw_ref)
            dnw_ref[...] = jnp.zeros_like(dnw_ref)

        dckv_v, dkpe_v = dckv_ref[...], dkpe_ref[...]
        acc_ref[...] = _mm(dckv_v, w_ref[0:128, :]) + _mm(dkpe_v, w_ref[1408:1536, :])
        dw_ref[0:128, :] += _mm_tn(dckv_v, h)
        dw_ref[1408:1536, :] += _mm_tn(dkpe_v, h)

        @pl.when(is_lat)
        def _():
            dq_v, du_v, dv_v = dq_ref[...], du_ref[...], dv_ref[...]
            acc_ref[...] += (_mm(dq_v, w_ref[128:384, :]) + _mm(du_v, w_ref[384:896, :])
                             + _mm(dv_v, w_ref[896:1408, :]))
            dw_ref[128:384, :] += _mm_tn(dq_v, h)
            dw_ref[384:896, :] += _mm_tn(du_v, h)
            dw_ref[896:1408, :] += _mm_tn(dv_v, h)

        dh = acc_ref[...]
        dn = dh * (1.0 + scale)
        dxh = dn * nw
        dx = rr * (dxh - xh * jnp.mean(dxh * xh, axis=-1, keepdims=True))
        dx_ref[...] = dx + jnp.where(is_lat, dx2_ref[...], 0.0)

        first = jnp.where(is_lat, t % tpe == 0, t == n_lat)

        @pl.when(first)
        def _():
            dmod_ref[...] = jnp.zeros_like(dmod_ref)

        dmod_ref[0, 0:1, :] += _rowsum(dh)
        dmod_ref[0, 1:2, :] += _rowsum(dh * n)
        dnw_ref[...] += _rowsum(dn * xh)

    row = lambda cols: pl.BlockSpec((tm, cols), lambda t: (t, 0))
    lat = lambda cols: pl.BlockSpec((tm, cols), lambda t: (jnp.minimum(t, n_lat - 1), 0))
    return _pcall(
        body, name=name, grid=(n_tiles,),
        out_shape=(_sds((r, D), F32), _sds((WIN_ROWS, D), F32), _sds((nrows, 2, D), F32), _sds((1, D), F32)),
        in_specs=[row(128), row(128), lat(256), lat(512), lat(512), lat(D), row(D), _mod_spec(2, tpe, nrows),
                  _const((1, D)), _const((WIN_ROWS, D))],
        out_specs=(row(D), _const((WIN_ROWS, D)), _mod_spec(2, tpe, nrows), _const((1, D))),
        scratch=[pltpu.VMEM((tm, D), F32)], vmem_mb=48)(dckv, dkpe, dq, du, dv, dx2, x1, mod2, norm_w, wint)


def _head_norm_rope(x, w_pad, cos, sin, seg, segt, rot):
    rh = lax.rsqrt(_dot_hl(x * x, seg) * (1.0 / DH) + EPS)
    rb = _dot_hl(rh, segt)
    y = x * rb
    t = y * w_pad
    out = []
    for h in range(H):
        th = t[:, h * LANE:(h + 1) * LANE]
        out.append(th * cos + _dot_hl(th, rot) * sin)
    return jnp.concatenate(out, axis=-1), y, rb


def _head_norm_rope_bwd(dout, y, rb, w_pad, cos, sin, seg, segt, rot_t):
    dt = []
    for h in range(H):
        dh = dout[:, h * LANE:(h + 1) * LANE]
        dt.append(dh * cos + _dot_hl(dh * sin, rot_t))
    dt = jnp.concatenate(dt, axis=-1)
    dw = _rowsum(dt * y)
    dy = dt * w_pad
    mean_h = _dot_hl(dy * y, seg) * (1.0 / DH)
    return rb * (dy - y * _dot_hl(mean_h, segt)), dw


def _q_prep_fwd(qp, qa_w, wuq, wq, cos, sin, cs, *, tm, n_lat, tpe):
    def body(qp_ref, qa_ref, wuq_ref, wq_ref, cos_ref, sin_ref, seg, segt, rot, q_ref):
        x = qp_ref[...]
        cq = (x * _rms(x) * qa_ref[...]).astype(BF16)
        q, _, _ = _head_norm_rope(_mm_nt(cq, wuq_ref[...]), wq_ref[...], cos_ref[...], sin_ref[...],
                                  seg[...], segt[...], rot[...])
        q_ref[...] = q.astype(BF16)

    row = lambda cols: pl.BlockSpec((tm, cols), lambda t: (t, 0))
    tab = pl.BlockSpec((tm, LANE), lambda t: (t % tpe, 0))
    return _pcall(
        body, name="q_prep_fwd", grid=(n_lat,), out_shape=_sds((n_lat * tm, HP), BF16),
        in_specs=[row(QL), _const((1, QL)), _const((HP, QL)), _const((1, HP)), tab, tab,
                  _const((HP, LANE)), _const((LANE, HP)), _const((LANE, LANE))],
        out_specs=row(HP))(qp, qa_w, wuq, wq, cos, sin, cs["seg_h"], cs["seg_ht"], cs["rot"])


def _q_prep_bwd(dq, qp, qa_w, wuq, wq, cos, sin, cs, *, tm, n_lat, tpe):
    def body(dq_ref, qp_ref, qa_ref, wuq_ref, wq_ref, cos_ref, sin_ref, seg, segt, rot, rot_t,
             dqp_ref, dwuq_ref, dqa_ref, dwq_ref):
        t = pl.program_id(0)
        x = qp_ref[...]
        ra = _rms(x)
        xh = x * ra
        qa = qa_ref[...]
        cq = (xh * qa).astype(BF16)
        wuq_v = wuq_ref[...]
        wq_v, cos_v, sin_v = wq_ref[...], cos_ref[...], sin_ref[...]
        _, y, rb = _head_norm_rope(_mm_nt(cq, wuq_v), wq_v, cos_v, sin_v, seg[...], segt[...], rot[...])
        dqraw, dwq = _head_norm_rope_bwd(dq_ref[...], y, rb, wq_v, cos_v, sin_v, seg[...], segt[...], rot_t[...])
        dqraw = dqraw.astype(BF16)
        dcq = _mm(dqraw, wuq_v)
        dxh = dcq * qa
        dqp_ref[...] = (ra * (dxh - xh * jnp.mean(dxh * xh, axis=-1, keepdims=True))).astype(BF16)

        @pl.when(t == 0)
        def _():
            dwuq_ref[...] = jnp.zeros_like(dwuq_ref)
            dqa_ref[...] = jnp.zeros_like(dqa_ref)
            dwq_ref[...] = jnp.zeros_like(dwq_ref)

        dwuq_ref[...] += _mm_tn(dqraw, cq)
        dqa_ref[...] += _rowsum(dcq * xh)
        dwq_ref[...] += dwq

    row = lambda cols: pl.BlockSpec((tm, cols), lambda t: (t, 0))
    tab = pl.BlockSpec((tm, LANE), lambda t: (t % tpe, 0))
    return _pcall(
        body, name="q_prep_bwd", grid=(n_lat,),
        out_shape=(_sds((n_lat * tm, QL), BF16), _sds((HP, QL), F32), _sds((1, QL), F32), _sds((1, HP), F32)),
        in_specs=[row(HP), row(QL), _const((1, QL)), _const((HP, QL)), _const((1, HP)), tab, tab,
                  _const((HP, LANE)), _const((LANE, HP)), _const((LANE, LANE)), _const((LANE, LANE))],
        out_specs=(row(QL), _const((HP, QL)), _const((1, QL)), _const((1, HP))), vmem_mb=40)(
            dq, qp, qa_w, wuq, wq, cos, sin, cs["seg_h"], cs["seg_ht"], cs["rot"], cs["rot_t"])


def _kv_tab_spec(tm, tpe, n_lat):
    return pl.BlockSpec((tm, LANE), lambda t: (jnp.where(t < n_lat, t % tpe, tpe), 0))


def _kv_prep_fwd(ckv, kpe, kva_w, wukv, wk, cosk, sink, cs, *, tm, n_tiles, tpe, n_lat):
    def body(ckv_ref, kpe_ref, kva_ref, wukv_ref, wk_ref, cos_ref, sin_ref, seg, segt, rot, k_ref, v_ref):
        x = ckv_ref[...]
        ckvn = (x * _rms(x) * kva_ref[...]).astype(BF16)
        kv = _mm_nt(ckvn, wukv_ref[...])
        kx = kv[:, :HP] + jnp.concatenate([kpe_ref[...]] * H, axis=-1)
        k, _, _ = _head_norm_rope(kx, wk_ref[...], cos_ref[...], sin_ref[...], seg[...], segt[...], rot[...])
        k_ref[...] = k.astype(BF16)
        v_ref[...] = kv[:, HP:].astype(BF16)

    row = lambda cols: pl.BlockSpec((tm, cols), lambda t: (t, 0))
    tab = _kv_tab_spec(tm, tpe, n_lat)
    r = n_tiles * tm
    return _pcall(
        body, name="kv_prep_fwd", grid=(n_tiles,), out_shape=(_sds((r, HP), BF16), _sds((r, HP), BF16)),
        in_specs=[row(KVL), row(LANE), _const((1, KVL)), _const((2 * HP, KVL)), _const((1, HP)), tab, tab,
                  _const((HP, LANE)), _const((LANE, HP)), _const((LANE, LANE))],
        out_specs=(row(HP), row(HP)), vmem_mb=40)(
            ckv, kpe, kva_w, wukv, wk, cosk, sink, cs["seg_h"], cs["seg_ht"], cs["rot"])


def _kv_prep_bwd(dks, dvs, ckv, kpe, kva_w, wukv, wk, cosk, sink, cs, *, tm, n_tiles, tpe, n_lat):
    def body(dkl_ref, dkc_ref, dvl_ref, dvc_ref, ckv_ref, kpe_ref, kva_ref, wukv_ref, wk_ref, cos_ref, sin_ref,
             seg, segt, rot, rot_t, dckv_ref, dkpe_ref, dwukv_ref, dkva_ref, dwk_ref):
        t = pl.program_id(0)
        is_lat = t < n_lat
        dk = jnp.where(is_lat, dkl_ref[...], dkc_ref[...])
        dv = jnp.where(is_lat, dvl_ref[...], dvc_ref[...])
        x = ckv_ref[...]
        ra = _rms(x)
        xh = x * ra
        kva = kva_ref[...]
        ckvn = (xh * kva).astype(BF16)
        wukv_v = wukv_ref[...]
        wk_v, cos_v, sin_v = wk_ref[...], cos_ref[...], sin_ref[...]
        kv = _mm_nt(ckvn, wukv_v)
        kx = kv[:, :HP] + jnp.concatenate([kpe_ref[...]] * H, axis=-1)
        _, y, rb = _head_norm_rope(kx, wk_v, cos_v, sin_v, seg[...], segt[...], rot[...])
        dkx, dwk = _head_norm_rope_bwd(dk, y, rb, wk_v, cos_v, sin_v, seg[...], segt[...], rot_t[...])
        dkpe = dkx[:, 0:LANE]
        for h in range(1, H):
            dkpe = dkpe + dkx[:, h * LANE:(h + 1) * LANE]
        lane = lax.broadcasted_iota(jnp.int32, (tm, LANE), 1)
        dkpe_ref[...] = jnp.where((lane >= DN) & (lane < DH), dkpe, 0.0).astype(BF16)
        dkv = jnp.concatenate([dkx, dv], axis=-1).astype(BF16)
        dckvn = _mm(dkv, wukv_v)
        dxh = dckvn * kva
        dckv_ref[...] = (ra * (dxh - xh * jnp.mean(dxh * xh, axis=-1, keepdims=True))).astype(BF16)

        @pl.when(t == 0)
        def _():
            dwukv_ref[...] = jnp.zeros_like(dwukv_ref)
            dkva_ref[...] = jnp.zeros_like(dkva_ref)
            dwk_ref[...] = jnp.zeros_like(dwk_ref)

        dwukv_ref[...] += _mm_tn(dkv, ckvn)
        dkva_ref[...] += _rowsum(dckvn * xh)
        dwk_ref[...] += dwk

    row = lambda cols: pl.BlockSpec((tm, cols), lambda t: (t, 0))
    lat = pl.BlockSpec((tm, HP), lambda t: (jnp.minimum(t, n_lat - 1), 0))
    ctx = pl.BlockSpec((tm, HP), lambda t: (jnp.maximum(t - n_lat, 0), 0))
    tab = _kv_tab_spec(tm, tpe, n_lat)
    r = n_tiles * tm
    return _pcall(
        body, name="kv_prep_bwd", grid=(n_tiles,),
        out_shape=(_sds((r, KVL), BF16), _sds((r, LANE), BF16), _sds((2 * HP, KVL), F32), _sds((1, KVL), F32),
                   _sds((1, HP), F32)),
        in_specs=[lat, ctx, lat, ctx, row(KVL), row(LANE), _const((1, KVL)), _const((2 * HP, KVL)), _const((1, HP)),
                  tab, tab, _const((HP, LANE)), _const((LANE, HP)), _const((LANE, LANE)), _const((LANE, LANE))],
        out_specs=(row(KVL), row(LANE), _const((2 * HP, KVL)), _const((1, KVL)), _const((1, HP))), vmem_mb=48)(
            dks[0], dks[1], dvs[0], dvs[1], ckv, kpe, kva_w, wukv, wk, cosk, sink,
            cs["seg_h"], cs["seg_ht"], cs["rot"], cs["rot_t"])


_SCALE = DH ** -0.5
_SCALE_LOG2E = _SCALE * 1.4426950408889634


def _attn_specs(tq, s, nc, tpe, n_lat_rows):
    qs = pl.BlockSpec((tq, LANE), lambda i, j, t: (i * tpe + t, j))
    kl = pl.BlockSpec((s, LANE), lambda i, j, t: (i, j))
    kc = pl.BlockSpec((nc, LANE), lambda i, j, t: (n_lat_rows // nc + i, j))
    return qs, kl, kc


def _key_chunks(s, nc, ck):
    return ([(0, lo, min(lo + ck, s)) for lo in range(0, s, ck)]
            + [(1, lo, min(lo + ck, nc)) for lo in range(0, nc, ck)])


def _lse_spec(tq):
    return pl.BlockSpec((1, 8, tq), lambda i, j, t: (i * H + j, 0, t))


def _attn_fwd(q, k, v, *, nb, s, nc, tq, ck):
    tpe = s // tq
    r_lat = nb * s
    chunks = _key_chunks(s, nc, ck)
    hp = 4

    def body(q_ref, kl_ref, kc_ref, vl_ref, vc_ref, o_ref, lse_ref):
        k_refs, v_refs = (kl_ref, kc_ref), (vl_ref, vc_ref)
        for hh in range(hp):
            hs = slice(hh * LANE, (hh + 1) * LANE)
            qv = q_ref[:, hs]
            xs = [_mm_nt(qv, k_refs[w][lo:hi, hs]) for w, lo, hi in chunks]
            m = jnp.max(xs[0], axis=-1, keepdims=True)
            for x in xs[1:]:
                m = jnp.maximum(m, jnp.max(x, axis=-1, keepdims=True))
            l = acc = None
            for x, (w, lo, hi) in zip(xs, chunks):
                e = jnp.exp2((x - m) * _SCALE_LOG2E)
                lc = jnp.sum(e, axis=-1, keepdims=True)
                pv = _mm(e.astype(BF16), v_refs[w][lo:hi, hs])
                l = lc if l is None else l + lc
                acc = pv if acc is None else acc + pv
            o_ref[:, hs] = (acc / l).astype(BF16)
            lse = m * _SCALE_LOG2E + jnp.log2(l)
            lse_ref[hh] = jnp.transpose(jnp.broadcast_to(lse, (tq, LANE)))[0:8, :]

    qs = pl.BlockSpec((tq, hp * LANE), lambda i, j, t: (i * tpe + t, j))
    kl = pl.BlockSpec((s, hp * LANE), lambda i, j, t: (i, j))
    kc = pl.BlockSpec((nc, hp * LANE), lambda i, j, t: (r_lat // nc + i, j))
    ls = pl.BlockSpec((hp, 8, tq), lambda i, j, t: (i * (H // hp) + j, 0, t))
    return _pcall(body, name="attn_fwd", grid=(nb, H // hp, tpe),
                  out_shape=(_sds((r_lat, HP), BF16), _sds((nb * H, 8, s), F32)),
                  in_specs=[qs, kl, kc, kl, kc], out_specs=(qs, ls), vmem_mb=48)(q, k, k, v, v)


def _attn_bwd(q, k, v, o, do, lse, part, *, nb, s, nc, tq, ck):
    tpe = s // tq
    r_lat = nb * s
    chunks = _key_chunks(s, nc, ck)
    hp = 2
    n_steps = nb * (H // hp) * tpe

    def body(q_ref, kl_ref, kc_ref, vl_ref, vc_ref, o_ref, do_ref, lse_ref, part_ref,
             dq_ref, dkl_ref, dkc_ref, dvl_ref, dvc_ref, recv_ref, akl, akc, avl, avc, send_sems, recv_sems):
        t = pl.program_id(2)
        step = (pl.program_id(0) * (H // hp) + pl.program_id(1)) * tpe + t
        sends = _chip_sends(part_ref, recv_ref, send_sems, recv_sems)

        @pl.when(step == 0)
        def _():
            for cp in sends:
                cp.start()

        @pl.when(step == n_steps - 1)
        def _():
            for cp in sends:
                cp.wait_recv()
            for cp in sends:
                cp.wait_send()

        @pl.when(t == 0)
        def _():
            akl[...] = jnp.zeros_like(akl)
            akc[...] = jnp.zeros_like(akc)
            avl[...] = jnp.zeros_like(avl)
            avc[...] = jnp.zeros_like(avc)

        k_refs, v_refs, ak, av = (kl_ref, kc_ref), (vl_ref, vc_ref), (akl, akc), (avl, avc)
        for hh in range(hp):
            hs = slice(hh * LANE, (hh + 1) * LANE)
            qv = q_ref[:, hs]
            lse = jnp.transpose(jnp.concatenate([lse_ref[hh]] * (LANE // 8), axis=0))[:, 0:1]
            dov = do_ref[:, hs]
            delta = jnp.sum(dov.astype(F32) * o_ref[:, hs].astype(F32), axis=-1, keepdims=True)
            dq = None
            for w, lo, hi in chunks:
                kc_v = k_refs[w][lo:hi, hs]
                p = jnp.exp2(_mm_nt(qv, kc_v) * _SCALE_LOG2E - lse)
                ds = (p * (_mm_nt(dov, v_refs[w][lo:hi, hs]) - delta)).astype(BF16)
                part = _mm(ds, kc_v)
                dq = part if dq is None else dq + part
                ak[w][hs, lo:hi] += _mm_tn(qv, ds)
                av[w][hs, lo:hi] += _mm_tn(dov, p.astype(BF16))
            dq_ref[:, hs] = dq * _SCALE

        @pl.when(t == tpe - 1)
        def _():
            dkl_ref[...] = akl[...].T * _SCALE
            dkc_ref[...] = akc[...].T * _SCALE
            dvl_ref[...] = avl[...].T
            dvc_ref[...] = avc[...].T

    qs = pl.BlockSpec((tq, hp * LANE), lambda i, j, t: (i * tpe + t, j))
    kl = pl.BlockSpec((s, hp * LANE), lambda i, j, t: (i, j))
    kc = pl.BlockSpec((nc, hp * LANE), lambda i, j, t: (r_lat // nc + i, j))
    kc_out = pl.BlockSpec((nc, hp * LANE), lambda i, j, t: (i, j))
    ls = pl.BlockSpec((hp, 8, tq), lambda i, j, t: (i * (H // hp) + j, 0, t))
    return _pcall(
        body, name="attn_bwd", grid=(nb, H // hp, tpe),
        out_shape=(_sds((r_lat, HP), F32), _sds((r_lat, HP), F32), _sds((nb * nc, HP), F32),
                   _sds((r_lat, HP), F32), _sds((nb * nc, HP), F32), _sds((3,) + part.shape[1:], part.dtype)),
        in_specs=[qs, kl, kc, kl, kc, qs, qs, ls, ANY], out_specs=(qs, kl, kc_out, kl, kc_out, ANY),
        scratch=[pltpu.VMEM((hp * LANE, s), F32), pltpu.VMEM((hp * LANE, nc), F32)] * 2
        + [pltpu.SemaphoreType.DMA((3,))] * 2,
        vmem_mb=60)(q, k, k, v, v, o, do, lse, part)


def _gating(vn, ws_ref, bias_ref, s_scr, tm):
    lane = lax.broadcasted_iota(jnp.int32, (CH, LANE), 1)
    for c in range(tm // CH):
        rs = slice(c * CH, (c + 1) * CH)
        for j in range(G // 2):
            ls = slice(j * LANE, (j + 1) * LANE)
            vp = vn[rs, ls]
            s_scr[rs, ls] = jnp.where(lane < GD, _mm(ws_ref[2 * j], vp), _mm(ws_ref[2 * j + 1], vp)) + bias_ref[:, ls]


def _mix_fwd(u, v, attn, x1, gate, wv, ws, bias, wout, cs, *, tm, n_lat, tpe):
    nrows = gate.shape[0]

    def body(u_ref, v_ref, attn_ref, x_ref, gate_ref, wv_ref, ws_ref, bias_ref, wout_ref, seg, segt,
             x2_ref, mix_ref, s_scr):
        vg = _gelu(v_ref[...])
        rg = lax.rsqrt(_dot_hl(vg * vg, seg[...]) * (1.0 / GD) + EPS)
        vn = (vg * _dot_hl(rg, segt[...]) * wv_ref[...]).astype(BF16)
        _gating(vn, ws_ref, bias_ref, s_scr, tm)
        sg = (_gelu(u_ref[...]) * s_scr[...]).astype(BF16)
        mix = _mm(attn_ref[...], wout_ref[0:HP, :]) + _mm(sg, wout_ref[HP:, :])
        mix_ref[...] = mix.astype(BF16)
        x2_ref[...] = x_ref[...] + gate_ref[0] * mix

    row = lambda cols: pl.BlockSpec((tm, cols), lambda t: (t, 0))
    r = n_lat * tm
    return _pcall(
        body, name="mix_fwd", grid=(n_lat,),
        out_shape=(_sds((r, D), F32), _sds((r, D), BF16)),
        in_specs=[row(G * GD), row(G * GD), row(HP), row(D), _mod_spec(1, tpe, nrows), _const((1, G * GD)),
                  _const((G, CH, CH)), _const((CH, G * GD)), _const((HP + G * GD, D)), _const((G * GD, LANE)),
                  _const((LANE, G * GD))],
        out_specs=(row(D), row(D)), scratch=[pltpu.VMEM((tm, G * GD), F32)], vmem_mb=40)(
            u, v, attn, x1, gate, wv, ws, bias, wout, cs["seg_g"], cs["seg_gt"])


def _mix_bwd(dx2, mix, u, v, attn, gate, wv, ws, wst, bias, wout, cs, *, tm, n_lat, tpe):
    nrows = gate.shape[0]
    wrows = HP + G * GD

    def body(dx2_ref, mix_ref, u_ref, v_ref, attn_ref, gate_ref, wv_ref, ws_ref, wst_ref, bias_ref, wout_ref, seg, segt,
             dattn_ref, du_ref, dv_ref, dgate_ref, dwout_ref, dws_ref, dbs_ref, dwv_ref, s_scr, dvn_scr, dbias_scr):
        t = pl.program_id(0)
        dx2 = dx2_ref[...]
        dmix = (dx2 * gate_ref[0]).astype(BF16)
        dcat = _mm_nt(dmix, wout_ref[...])
        dattn_ref[...] = dcat[:, :HP].astype(BF16)
        dsg = dcat[:, HP:]

        vraw = v_ref[...]
        vg = _gelu(vraw)
        rg = lax.rsqrt(_dot_hl(vg * vg, seg[...]) * (1.0 / GD) + EPS)
        r64 = _dot_hl(rg, segt[...])
        y = vg * r64
        wv_v = wv_ref[...]
        vn = (y * wv_v).astype(BF16)
        _gating(vn, ws_ref, bias_ref, s_scr, tm)
        uraw = u_ref[...]
        ug = _gelu(uraw)
        s = s_scr[...]
        sg = (ug * s).astype(BF16)
        du_ref[...] = (dsg * s * _gelu_grad(uraw)).astype(BF16)
        ds = dsg * ug

        @pl.when(t == 0)
        def _():
            dwout_ref[...] = jnp.zeros_like(dwout_ref)
            dws_ref[...] = jnp.zeros_like(dws_ref)
            dwv_ref[...] = jnp.zeros_like(dwv_ref)
            dbias_scr[...] = jnp.zeros_like(dbias_scr)

        @pl.when(t % tpe == 0)
        def _():
            dgate_ref[...] = jnp.zeros_like(dgate_ref)

        dgate_ref[0] += _rowsum(dx2 * mix_ref[...].astype(F32))
        dwout_ref[...] += _mm_tn(jnp.concatenate([attn_ref[...], sg], axis=-1), dmix)

        lane = lax.broadcasted_iota(jnp.int32, (CH, LANE), 1)
        for c in range(tm // CH):
            rs = slice(c * CH, (c + 1) * CH)
            dbias_scr[...] += ds[rs, :]
            for j in range(G // 2):
                ls = slice(j * LANE, (j + 1) * LANE)
                dsp32 = ds[rs, ls]
                dsp = dsp32.astype(BF16)
                vp = vn[rs, ls]
                dvn_scr[rs, ls] = jnp.where(lane < GD, _mm(wst_ref[2 * j], dsp), _mm(wst_ref[2 * j + 1], dsp))
                dws_ref[2 * j] += _mm_nt(jnp.where(lane < GD, dsp32, 0.0).astype(BF16), vp)
                dws_ref[2 * j + 1] += _mm_nt(jnp.where(lane < GD, 0.0, dsp32).astype(BF16), vp)

        dvn = dvn_scr[...]
        dwv_ref[...] += _rowsum(dvn * y)
        dy = dvn * wv_v
        mean_g = _dot_hl(dy * y, seg[...]) * (1.0 / GD)
        dvg = r64 * (dy - y * _dot_hl(mean_g, segt[...]))
        dv_ref[...] = (dvg * _gelu_grad(vraw)).astype(BF16)

        @pl.when(t == n_lat - 1)
        def _():
            dbs_ref[...] = _dot_hl(dbias_scr[...], seg[...])

    row = lambda cols: pl.BlockSpec((tm, cols), lambda t: (t, 0))
    r = n_lat * tm
    return _pcall(
        body, name="mix_bwd", grid=(n_lat,),
        out_shape=(_sds((r, HP), BF16), _sds((r, G * GD), BF16), _sds((r, G * GD), BF16), _sds((nrows, 1, D), F32),
                   _sds((wrows, D), F32), _sds((G, CH, CH), F32), _sds((CH, LANE), F32), _sds((1, G * GD), F32)),
        in_specs=[row(D), row(D), row(G * GD), row(G * GD), row(HP), _mod_spec(1, tpe, nrows), _const((1, G * GD)),
                  _const((G, CH, CH)), _const((G, CH, CH)), _const((CH, G * GD)), _const((wrows, D)),
                  _const((G * GD, LANE)), _const((LANE, G * GD))],
        out_specs=(row(HP), row(G * GD), row(G * GD), _mod_spec(1, tpe, nrows), _const((wrows, D)),
                   _const((G, CH, CH)), _const((CH, LANE)), _const((1, G * GD))),
        scratch=[pltpu.VMEM((tm, G * GD), F32), pltpu.VMEM((tm, G * GD), F32), pltpu.VMEM((CH, G * GD), F32)],
        vmem_mb=56)(dx2, mix, u, v, attn, gate, wv, ws, wst, bias, wout, cs["seg_g"], cs["seg_gt"])


def _adamw_math(w, g, m, v):
    m2 = ADAM_B1 * m + (1.0 - ADAM_B1) * g
    v2 = ADAM_B2 * v + (1.0 - ADAM_B2) * (g * g)
    m_hat = m2 / (1.0 - ADAM_B1 ** ADAM_STEP)
    v_hat = v2 / (1.0 - ADAM_B2 ** ADAM_STEP)
    delta = -ADAM_LR * (m_hat / (jnp.sqrt(v_hat) + ADAM_EPS) + ADAM_WD * w)
    return delta, m2, v2


def _row_tile(r, c):
    best = r
    for tr in range(8, r, 8):
        if r % tr == 0 and tr * c * 4 <= MIB:
            best = tr
    return best


def _adamw(w, g, m, v, name):
    r, c = w.shape
    tr = _row_tile(r, c)

    def body(w_ref, g_ref, m_ref, v_ref, d_ref, mo_ref, vo_ref):
        d_ref[...], mo_ref[...], vo_ref[...] = _adamw_math(w_ref[...], g_ref[...], m_ref[...], v_ref[...])

    blk = pl.BlockSpec((tr, c), lambda t: (t, 0))
    return _pcall(body, name=name, grid=(r // tr,), out_shape=(_sds((r, c), F32),) * 3,
                  in_specs=[blk] * 4, out_specs=(blk,) * 3)(w, g, m, v)


def _adamw_small(params):
    n = len(params)

    def body(*refs):
        ins, outs = refs[:4 * n], refs[4 * n:]
        for i in range(n):
            w, g, m, v = (ins[4 * i + k][...] for k in range(4))
            if i == 0:
                sig = _sigmoid(w)
                g = g * (sig * (1.0 + w * (1.0 - sig)))
            d, m2, v2 = _adamw_math(w, g, m, v)
            outs[4 * i][...] = g
            outs[4 * i + 1][...] = d
            outs[4 * i + 2][...] = m2
            outs[4 * i + 3][...] = v2

    flat = [a for p in params for a in p]
    out_shape = tuple(_sds(p[0].shape, F32) for p in params for _ in range(4))
    res = _pcall(body, name="adamw_small", out_shape=out_shape, in_specs=[VMEM] * (4 * n),
                 out_specs=(VMEM,) * (4 * n))(*flat)
    return [res[4 * i:4 * i + 4] for i in range(n)]


def _rope_tables(s):
    rows = jnp.repeat(jnp.arange(s // GRID_W, dtype=F32), GRID_W)
    cols = jnp.tile(jnp.arange(GRID_W, dtype=F32), s // GRID_W)
    half = DR // 2
    inv = ROPE_BASE ** (-jnp.arange(0, half, 2, dtype=F32) / half)
    ang_r = rows[:, None] * inv
    ang_c = cols[:, None] * inv
    ang = jnp.concatenate([ang_r, ang_r, ang_c, ang_c], axis=-1)
    return jnp.cos(ang), jnp.sin(ang)


def _head_pad(a, real):
    return jnp.pad(a, ((0, 0), (0, LANE - real), (0, 0))).reshape(HP, a.shape[2])


def kernel(x, c, ctx, c_ctx, w_ada, b_ada, norm1_w, ffn1_w1, ffn1_w3, ffn1_w2, norm2_w, w_in, q_a_norm_w, w_uq, kv_a_norm_w, w_ukv, q_norm_w, k_norm_w, v_norm_w, w_s, b_s, w_out, norm3_w, ffn2_w1, ffn2_w3, ffn2_w2, loss_target, m_c_ctx, m_w_ada, m_b_ada, m_norm1_w, m_ffn1_w1, m_ffn1_w3, m_ffn1_w2, m_norm2_w, m_w_in, m_q_a_norm_w, m_w_uq, m_kv_a_norm_w, m_w_ukv, m_q_norm_w, m_k_norm_w, m_v_norm_w, m_w_s, m_b_s, m_w_out, m_norm3_w, m_ffn2_w1, m_ffn2_w3, m_ffn2_w2, v_c_ctx, v_w_ada, v_b_ada, v_norm1_w, v_ffn1_w1, v_ffn1_w3, v_ffn1_w2, v_norm2_w, v_w_in, v_q_a_norm_w, v_w_uq, v_kv_a_norm_w, v_w_ukv, v_q_norm_w, v_k_norm_w, v_v_norm_w, v_w_s, v_b_s, v_w_out, v_norm3_w, v_ffn2_w1, v_ffn2_w3, v_ffn2_w2):
    nb, s, _ = x.shape
    nc = ctx.shape[1]
    tm = 256 if nc % 256 == 0 else 128
    tpe = s // tm
    n_lat = nb * tpe
    n_all = n_lat + nb * nc // tm
    r_lat = nb * s
    me = 4 * lax.axis_index("x") + 2 * lax.axis_index("y") + lax.axis_index("c")
    cs = _consts()
    ncol = w_ada.shape[2]
    fsh = ffn1_w1.shape[2]
    assert nb + 1 <= 8 and NDEV * fsh == FF and NDEV * ncol == NMOD * D and s % nc == 0 and nc % tm == 0

    a_loc = jnp.concatenate([c, c_ctx[None, :], jnp.zeros((7 - nb, D), F32)], axis=0)
    a_raw = _all_gather(a_loc, "gather_c").reshape(NDEV * 8, D)
    mod_cols = _ada_fwd(a_raw, w_ada[0], lax.dynamic_slice_in_dim(b_ada, me * ncol, ncol, axis=1))
    mod_all = _all_gather(mod_cols, "gather_mod")
    mod_mine = lax.dynamic_slice_in_dim(mod_all, 8 * me, 8, axis=1)
    modtab = mod_mine.transpose(1, 0, 2).reshape(8, NMOD, D)[:nb + 1]

    def t16(a):
        return a.T.astype(BF16)

    wpack1 = jnp.concatenate([t16(ffn1_w1[0]), t16(ffn1_w3[0]), ffn1_w2[0].astype(BF16)], axis=0)
    wpack2 = jnp.concatenate([
        t16(ffn2_w1[0]), t16(ffn2_w3[0]), ffn2_w2[0].astype(BF16),
        t16(w_in[0]), jnp.zeros((12, D), BF16),
        w_out[0].astype(BF16),
        t16(w_uq[0]).reshape(24, D), jnp.zeros((8, D), BF16),
        t16(w_ukv[0]).reshape(16, D)], axis=0)
    wall1 = _all_gather(wpack1, "gather_w_ffn1")

    def head_w(wn):
        return jnp.tile(jnp.pad(wn, ((0, 0), (0, LANE - DH))), (1, H))

    wq, wk = head_w(q_norm_w), head_w(k_norm_w)
    wv = v_norm_w.reshape(1, G * GD)
    ws16 = w_s[0].astype(BF16)
    wst16 = w_s[0].transpose(0, 2, 1).astype(BF16)
    bias = jnp.repeat(b_s[0].T, GD, axis=1)
    cos, sin = _rope_tables(s)
    cos = jnp.pad(cos, ((0, 0), (DN, LANE - DH)), constant_values=1.0)
    sin = jnp.pad(sin, ((0, 0), (DN, LANE - DH)))
    cos_k = jnp.concatenate([cos, jnp.ones((tm, LANE), F32)], axis=0)
    sin_k = jnp.concatenate([sin, jnp.zeros((tm, LANE), F32)], axis=0)

    xs = (x.reshape(r_lat, D), ctx.reshape(nb * nc, D))
    tmf = 2 * tm if s % (2 * tm) == 0 and (nb * nc) % (2 * tm) == 0 else tm
    x1, a1, b1, o1, wall2 = _ffn_fwd(xs, modtab[:, 0:3], norm1_w, wall1, 0, tm=tmf, n_tiles=(r_lat + nb * nc) // tmf,
                                     tpe=s // tmf, n_lat=r_lat // tmf, name="ffn1_fwd", gather=wpack2)

    o0 = 3 * fsh
    wint = wall2[:, o0:o0 + 180].reshape(IN_COLS, D)
    z = lambda n: jnp.zeros((n, D), BF16)
    wint = jnp.concatenate([wint[0:128], wint[160:416], wint[416:928], wint[928:1440],
                            z(DN), wint[128:160], z(LANE - DH)], axis=0)
    wout = wall2[:, o0 + 192:o0 + 320].reshape(D, D)
    wout = jnp.concatenate([_head_pad(wout[:H * DV].reshape(H, DV, D), DV), wout[H * DV:]], axis=0)
    wuq = _head_pad(wall2[:, o0 + 320:o0 + 344].reshape(H, DH, QL), DH)
    wukvt = wall2[:, o0 + 352:o0 + 368].reshape(H, DN + DV, KVL)
    wukv = jnp.concatenate([_head_pad(wukvt[:, :DN], DN), _head_pad(wukvt[:, DN:], DV)], axis=0)

    ckv, qp, u_raw, v_raw, kpe = _proj_fwd(x1, modtab[:, 3:5], norm2_w, wint, tm=tm, n_tiles=n_all, tpe=tpe)
    q = _q_prep_fwd(qp, q_a_norm_w, wuq, wq, cos, sin, cs, tm=tm, n_lat=n_lat, tpe=tpe)
    k, v = _kv_prep_fwd(ckv, kpe, kv_a_norm_w, wukv, wk, cos_k, sin_k, cs, tm=tm, n_tiles=n_all, tpe=tpe, n_lat=n_lat)
    attn, lse = _attn_fwd(q, k, v, nb=nb, s=s, nc=nc, tq=tm, ck=512)
    x2, mix = _mix_fwd(u_raw, v_raw, attn, x1, modtab[:nb, 5:6], wv, ws16, bias, wout, cs,
                       tm=tm, n_lat=n_lat, tpe=tpe)
    dy, a2, b2, o2, lsum = _ffn_fwd((x2,), modtab[:nb, 6:9], norm3_w, wall2, 0, tm=tmf, n_tiles=r_lat // tmf,
                                    tpe=s // tmf, n_lat=r_lat // tmf, name="ffn2_fwd",
                                    target=loss_target.reshape(r_lat, D))
    loss = lax.psum(lsum[0, 0] * (0.5 / D), ("x", "y", "c"))

    tr = 2 * tm if n_lat % 2 == 0 and n_all % 2 == 0 else tm
    dx2, da2, db2, g2, do2, h2, dmod678, dnorm3 = _ffn_bwd_dx(
        dy, (x2,), a2, b2, o2, modtab[:nb, 6:9], norm3_w, wall2, 0,
        tm=tm, n_tiles=n_lat, tpe=tpe, n_lat=n_lat, name="ffn2_bwd_dx")
    g_ffn2 = _ffn_bwd_dw(h2, do2, da2, db2, g2, tr=tr, name="ffn2_bwd_dw")
    part_ffn2 = _add_sibling(g_ffn2, _scatter_sibling([g_ffn2], "scatter_sibling_ffn2")[0], 176, "add_sibling_ffn2")

    dattn, du, dv, dgate5, dwout, dws, dbs, dwv = _mix_bwd(
        dx2, mix, u_raw, v_raw, attn, modtab[:nb, 5:6], wv, ws16, wst16, bias, wout, cs, tm=tm, n_lat=n_lat, tpe=tpe)
    tq = 2 * tm if s % (2 * tm) == 0 else tm
    dq, dk_l, dk_c, dv_l, dv_c, recv_ffn2 = _attn_bwd(q, k, v, attn, dattn, lse, part_ffn2,
                                                      nb=nb, s=s, nc=nc, tq=tq, ck=1024)
    dqp, dwuq, dqa, dwq = _q_prep_bwd(dq, qp, q_a_norm_w, wuq, wq, cos, sin, cs, tm=tm, n_lat=n_lat, tpe=tpe)
    dckv, dkpe, dwukv, dkva, dwk = _kv_prep_bwd((dk_l, dk_c), (dv_l, dv_c), ckv, kpe, kv_a_norm_w, wukv, wk,
                                                cos_k, sin_k, cs, tm=tm, n_tiles=n_all, tpe=tpe, n_lat=n_lat)
    dx1, dwin, dmod34, dnorm2 = _proj_bwd(dckv, dkpe, dqp, du, dv, dx2, x1, modtab[:, 3:5], norm2_w, wint,
                                          tm=tm, n_tiles=n_all, tpe=tpe, n_lat=n_lat)
    dx0, da1, db1, g1, do1, h1, dmod012, dnorm1 = _ffn_bwd_dx(
        dx1, xs, a1, b1, o1, modtab[:, 0:3], norm1_w, wall1, 0,
        tm=tm, n_tiles=n_all, tpe=tpe, n_lat=n_lat, name="ffn1_bwd_dx")
    g_ffn1 = _ffn_bwd_dw(h1, do1, da1, db1, g1, tr=tr, name="ffn1_bwd_dw")
    grad_x = dx0.reshape(nb, s, D)

    zrow = jnp.zeros((1, D), F32)
    g_lat = jnp.concatenate([dmod012[:nb, 0], dmod012[:nb, 1], dmod012[:nb, 2], dmod34[:nb, 0], dmod34[:nb, 1],
                             dgate5[:, 0], dmod678[:, 0], dmod678[:, 1], dmod678[:, 2]], axis=1)
    g_ctx = jnp.concatenate([dmod012[nb:, 0], dmod012[nb:, 1], dmod012[nb:, 2], dmod34[nb:, 0], dmod34[nb:, 1],
                             zrow, zrow, zrow, zrow], axis=1)
    g_loc = jnp.concatenate([g_lat, g_ctx, jnp.zeros((7 - nb, NMOD * D), F32)], axis=0)
    g_all = _all_gather(g_loc, "gather_gmod").reshape(NDEV * 8, NMOD * D)
    g_cols = lax.dynamic_slice_in_dim(g_all, me * ncol, ncol, axis=1)
    g_w_ada, pc_ctx, g_b_ada = _ada_bwd(a_raw, c_ctx.reshape(D, 1), g_all, g_cols, w_ada[0], nb)

    def blocks(a):
        return a.reshape(NDEV, a.shape[0] // NDEV, D)

    dwin_o = jnp.concatenate([dwin[0:128], dwin[KPE_LO:KPE_LO + DR], dwin[128:384], dwin[384:896], dwin[896:1408]],
                             axis=0)
    dwout_o = jnp.concatenate([dwout[:HP].reshape(H, LANE, D)[:, :DV].reshape(H * DV, D), dwout[HP:]], axis=0)
    dwuq_o = dwuq.reshape(H, LANE, QL)[:, :DH]
    dwukv_o = jnp.concatenate([dwukv[:HP].reshape(H, LANE, KVL)[:, :DN], dwukv[HP:].reshape(H, LANE, KVL)[:, :DV]],
                              axis=1)
    gmisc = jnp.concatenate([
        blocks(dwin_o).astype(BF16), jnp.zeros((NDEV, 12, D), BF16),
        blocks(dwout_o).astype(BF16),
        dwuq_o.reshape(NDEV, 24, D).astype(BF16), jnp.zeros((NDEV, 8, D), BF16),
        dwukv_o.reshape(NDEV, 16, D).astype(BF16)], axis=1)
    gots = _scatter_sibling([g_ffn1, gmisc], "scatter_sibling")
    parts = [_add_sibling(g_ffn1, gots[0], 176, "add_sibling_ffn1"), _add_sibling(gmisc, gots[1], 368, "add_sibling_misc")]
    recv = _scatter_chips(parts, "scatter_chips")
    gsum1 = _sum_chips(parts[0], recv[0], 176, "sum_grads_ffn1")
    gsum2 = _sum_chips(part_ffn2, recv_ffn2, 176, "sum_grads_ffn2")
    msum = _sum_chips(parts[1], recv[1], 368, "sum_grads_misc")

    g_big = {
        "ffn1_w1": gsum1[0:fsh].T, "ffn1_w3": gsum1[fsh:2 * fsh].T, "ffn1_w2": gsum1[2 * fsh:3 * fsh],
        "ffn2_w1": gsum2[0:fsh].T, "ffn2_w3": gsum2[fsh:2 * fsh].T, "ffn2_w2": gsum2[2 * fsh:3 * fsh],
        "w_in": msum[0:180].T, "w_out": msum[192:320],
        "w_uq": msum[320:344].reshape(DH, QL).T, "w_ukv": msum[352:368].reshape(DN + DV, KVL).T,
        "w_ada": g_w_ada,
    }

    def prow(a):
        a = a.reshape(1, -1)
        return jnp.concatenate([a, jnp.zeros((1, D - a.shape[1]), F32)], axis=1)

    g_qn = dwq.reshape(H, LANE)[:, :DH].sum(0)
    g_kn = dwk.reshape(H, LANE)[:, :DH].sum(0)
    spack = jnp.concatenate([
        dnorm1, dnorm2, dnorm3, prow(dqa), prow(dkva), prow(g_qn), prow(g_kn), prow(dwv),
        prow(dbs[:, :G].T), prow(pc_ctx), jnp.zeros((6, D), F32), dws.reshape(CH, D)], axis=0)
    ssum = _sum_slots(_all_gather(spack, "gather_small"), 144, "sum_small")

    big_in = {
        "w_ada": (w_ada, m_w_ada, v_w_ada), "ffn1_w1": (ffn1_w1, m_ffn1_w1, v_ffn1_w1),
        "ffn1_w3": (ffn1_w3, m_ffn1_w3, v_ffn1_w3), "ffn1_w2": (ffn1_w2, m_ffn1_w2, v_ffn1_w2),
        "w_in": (w_in, m_w_in, v_w_in), "w_uq": (w_uq, m_w_uq, v_w_uq), "w_ukv": (w_ukv, m_w_ukv, v_w_ukv),
        "w_out": (w_out, m_w_out, v_w_out), "ffn2_w1": (ffn2_w1, m_ffn2_w1, v_ffn2_w1),
        "ffn2_w3": (ffn2_w3, m_ffn2_w3, v_ffn2_w3), "ffn2_w2": (ffn2_w2, m_ffn2_w2, v_ffn2_w2),
    }
    res = {}
    for nm, (w, m, v_) in big_in.items():
        g = g_big[nm]
        d_, m_, v2_ = _adamw(w[0], g, m[0], v_[0], "adamw_" + nm)
        res[nm] = tuple(a[None] for a in (g, d_, m_, v2_))

    small_in = [
        ("c_ctx", c_ctx, m_c_ctx, v_c_ctx, ssum[9:10], (1, D)),
        ("b_ada", b_ada, m_b_ada, v_b_ada, g_b_ada, (1, NMOD * D)),
        ("norm1_w", norm1_w, m_norm1_w, v_norm1_w, ssum[0:1], (1, D)),
        ("norm2_w", norm2_w, m_norm2_w, v_norm2_w, ssum[1:2], (1, D)),
        ("norm3_w", norm3_w, m_norm3_w, v_norm3_w, ssum[2:3], (1, D)),
        ("q_a_norm_w", q_a_norm_w, m_q_a_norm_w, v_q_a_norm_w, ssum[3:4, :QL], (1, QL)),
        ("kv_a_norm_w", kv_a_norm_w, m_kv_a_norm_w, v_kv_a_norm_w, ssum[4:5, :KVL], (1, KVL)),
        ("q_norm_w", q_norm_w, m_q_norm_w, v_q_norm_w, ssum[5:6, :DH], (1, DH)),
        ("k_norm_w", k_norm_w, m_k_norm_w, v_k_norm_w, ssum[6:7, :DH], (1, DH)),
        ("v_norm_w", v_norm_w, m_v_norm_w, v_v_norm_w, ssum[7:8, :G * GD], (G, GD)),
        ("b_s", b_s, m_b_s, v_b_s, ssum[8:9], (G, CH)),
        ("w_s", w_s, m_w_s, v_w_s, ssum[16:144], (G * CH, CH)),
    ]
    small_out = _adamw_small(
        [(w.reshape(sh), g.reshape(sh), m.reshape(sh), v_.reshape(sh)) for _, w, m, v_, g, sh in small_in])
    for (nm, w, *_), outs in zip(small_in, small_out):
        res[nm] = tuple(a.reshape(w.shape) for a in outs)

    order = ["c_ctx", "w_ada", "b_ada", "norm1_w", "ffn1_w1", "ffn1_w3", "ffn1_w2", "norm2_w", "w_in", "q_a_norm_w",
             "w_uq", "kv_a_norm_w", "w_ukv", "q_norm_w", "k_norm_w", "v_norm_w", "w_s", "b_s", "w_out", "norm3_w",
             "ffn2_w1", "ffn2_w3", "ffn2_w2"]
    return (loss, grad_x, *[res[n][0] for n in order], *[res[n][1] for n in order],
            *[res[n][2] for n in order], *[res[n][3] for n in order])
```

```python
import numpy as np
import jax
import jax.numpy as jnp
from jax import lax
from jax.experimental import pallas as pl
from jax.experimental.pallas import tpu as pltpu

F32 = jnp.float32
BF16 = jnp.bfloat16

D = 1024
FF = 2816
FC = 256
H = 8
DN, DR, DV = 64, 32, 64
DH = DN + DR
QL, KVL = 256, 128
G, GD, CH = 8, 64, 128
NMOD = 9
EPS = 1e-6
GRID_W = 64
ROPE_BASE = 10000.0
NDEV = 8
LANE = 128
HP = H * LANE
IN_COLS = 1440
WIN_ROWS = 1536
KPE_LO = 1408 + DN
NFFN_W = 6
MIB = 1 << 20

ADAM_LR, ADAM_B1, ADAM_B2, ADAM_EPS, ADAM_WD, ADAM_STEP = 0.001, 0.9, 0.999, 1e-08, 0.01, 10

MESH = pl.DeviceIdType.MESH
ANY = pl.BlockSpec(memory_space=pl.ANY)
VMEM = pl.BlockSpec(memory_space=pltpu.VMEM)


def _mm(a, b):
    return jnp.dot(a, b, preferred_element_type=F32)


def _mm_nt(a, b):
    return lax.dot_general(a, b, (((1,), (1,)), ((), ())), preferred_element_type=F32)


def _mm_tn(a, b):
    return lax.dot_general(a, b, (((0,), (0,)), ((), ())), preferred_element_type=F32)


def _dot_hl(x, m):
    hi = x.astype(BF16)
    lo = (x - hi.astype(F32)).astype(BF16)
    return _mm(hi, m) + _mm(lo, m)


def _sigmoid(a):
    return 1.0 / (1.0 + jnp.exp(-a))


_G0 = 0.7978845608028654
_G1 = 0.044715


def _gelu(x):
    return 0.5 * x * (1.0 + jnp.tanh(_G0 * (x + _G1 * (x * x * x))))


def _gelu_grad(x):
    th = jnp.tanh(_G0 * (x + _G1 * (x * x * x)))
    return 0.5 * (1.0 + th) + 0.5 * x * (1.0 - th * th) * (_G0 * (1.0 + 3.0 * _G1 * x * x))


def _rowsum(y):
    return jnp.sum(y, axis=0, keepdims=True)


def _rms(x):
    return lax.rsqrt(jnp.mean(x * x, axis=-1, keepdims=True) + EPS)


def _pcall(body, *, name, out_shape, in_specs, out_specs, grid=None, scratch=(), vmem_mb=32, aliases=None):
    kw = {}
    if grid is not None:
        kw["grid"] = grid
        sem = ("arbitrary",) * len(grid)
    else:
        sem = None
    if aliases:
        kw["input_output_aliases"] = aliases
    return pl.pallas_call(
        body, name=name, out_shape=out_shape, in_specs=in_specs, out_specs=out_specs,
        scratch_shapes=list(scratch),
        compiler_params=pltpu.CompilerParams(dimension_semantics=sem, vmem_limit_bytes=vmem_mb * MIB),
        **kw)


def _const(shape):
    nd = len(shape)
    return pl.BlockSpec(shape, lambda *_: (0,) * nd)


def _sds(shape, dt):
    return jax.ShapeDtypeStruct(shape, dt)


def _consts():
    seg_h = np.zeros((HP, LANE), np.float32)
    seg_h[np.arange(HP), np.arange(HP) // LANE] = 1.0
    seg_g = np.zeros((G * GD, LANE), np.float32)
    seg_g[np.arange(G * GD), np.arange(G * GD) // GD] = 1.0
    rot = np.zeros((LANE, LANE), np.float32)
    for base in (DN, DN + 16):
        for j in range(8):
            rot[base + j + 8, base + j] = -1.0
            rot[base + j, base + j + 8] = 1.0
    c = dict(seg_h=seg_h, seg_ht=seg_h.T, seg_g=seg_g, seg_gt=seg_g.T, rot=rot, rot_t=rot.T)
    return {k: jnp.asarray(v, BF16) for k, v in c.items()}


_GATHER_SEMS = [pltpu.SemaphoreType.DMA((7,)), pltpu.SemaphoreType.DMA((7,)), pltpu.SemaphoreType.DMA(())]


def _gather_phases(x_ref, out_ref, send_sems, recv_sems, local_sem):
    mx, my, mc = lax.axis_index("x"), lax.axis_index("y"), lax.axis_index("c")
    me, sibling = (mx, my, mc), (mx, my, 1 - mc)
    chips = [(1 - mx, my), (mx, 1 - my), (1 - mx, 1 - my)]

    def blk(px, py, pc):
        return out_ref.at[4 * px + 2 * py + pc]

    def copy(k, block, to, src=None):
        return pltpu.make_async_remote_copy(
            src_ref=blk(*block) if src is None else src, dst_ref=blk(*block),
            send_sem=send_sems.at[k], recv_sem=recv_sems.at[k], device_id=to, device_id_type=MESH)

    mine = pltpu.make_async_copy(x_ref, blk(*me), local_sem)
    first = [copy(0, me, sibling, src=x_ref)]
    first += [copy(1 + j, me, (*chip, mc), src=x_ref) for j, chip in enumerate(chips)]
    passed = [copy(4 + j, (*chip, mc), sibling) for j, chip in enumerate(chips)]

    def start():
        mine.start()
        for cp in first:
            cp.start()

    def forward():
        for j, chip in enumerate(chips):
            copy(1 + j, (*chip, mc), me).wait_recv()
            passed[j].start()

    def finish():
        copy(0, sibling, me).wait_recv()
        for j, chip in enumerate(chips):
            copy(4 + j, (*chip, 1 - mc), me).wait_recv()
        for cp in first + passed:
            cp.wait_send()
        mine.wait()

    return start, forward, finish


def _all_gather(x, name):
    r, c = x.shape

    def body(x_ref, out_ref, send_sems, recv_sems, local_sem):
        start, forward, finish = _gather_phases(x_ref, out_ref, send_sems, recv_sems, local_sem)
        start()
        forward()
        finish()

    return pl.pallas_call(
        body, name=name, out_shape=_sds((NDEV, r, c), x.dtype), in_specs=[ANY], out_specs=ANY,
        scratch_shapes=list(_GATHER_SEMS),
    )(x)


def _chip_sends(p_ref, out_ref, send_sems, recv_sems):
    mx, my, mc = lax.axis_index("x"), lax.axis_index("y"), lax.axis_index("c")
    peers = [(1 - mx, my), (mx, 1 - my), (1 - mx, 1 - my)]
    return [pltpu.make_async_remote_copy(
        src_ref=p_ref.at[2 * px + py], dst_ref=out_ref.at[j], send_sem=send_sems.at[j], recv_sem=recv_sems.at[j],
        device_id=(px, py, mc), device_id_type=MESH) for j, (px, py) in enumerate(peers)]


def _scatter_sibling(xs, name):
    n = len(xs)

    def body(*refs):
        x_refs, got_refs = refs[:n], refs[n:2 * n]
        send_sems, recv_sems = refs[2 * n:]
        mx, my, mc = lax.axis_index("x"), lax.axis_index("y"), lax.axis_index("c")
        sibling = (mx, my, 1 - mc)
        remote = []
        for i in range(n):
            for j in range(4):
                k = 4 * i + j
                remote.append(pltpu.make_async_remote_copy(
                    src_ref=x_refs[i].at[2 * j + 1 - mc], dst_ref=got_refs[i].at[j],
                    send_sem=send_sems.at[k], recv_sem=recv_sems.at[k], device_id=sibling, device_id_type=MESH))
        for cp in remote:
            cp.start()
        for cp in remote:
            cp.wait_recv()
        for cp in remote:
            cp.wait_send()

    shapes = tuple(_sds((4,) + x.shape[1:], x.dtype) for x in xs)
    return pl.pallas_call(
        body, name=name, out_shape=shapes, in_specs=[ANY] * n, out_specs=(ANY,) * n,
        scratch_shapes=[pltpu.SemaphoreType.DMA((4 * n,)), pltpu.SemaphoreType.DMA((4 * n,))],
    )(*xs)


def _scatter_chips(ps, name):
    n = len(ps)

    def body(*refs):
        p_refs, out_refs, sems = refs[:n], refs[n:2 * n], refs[2 * n:]
        sends = []
        for i in range(n):
            sends += _chip_sends(p_refs[i], out_refs[i], sems[2 * i], sems[2 * i + 1])
        for cp in sends:
            cp.start()
        for cp in sends:
            cp.wait_recv()
        for cp in sends:
            cp.wait_send()

    shapes = tuple(_sds((3,) + p.shape[1:], p.dtype) for p in ps)
    return pl.pallas_call(
        body, name=name, out_shape=shapes, in_specs=[ANY] * n, out_specs=(ANY,) * n,
        scratch_shapes=[pltpu.SemaphoreType.DMA((3,))] * (2 * n),
    )(*ps)


def _add_sibling(x, got, tr, name):
    _, r, c = x.shape

    def body(x_ref, g_ref, o_ref):
        mc = lax.axis_index("c")
        for j in range(4):
            mine = jnp.where(mc == 0, x_ref[2 * j].astype(F32), x_ref[2 * j + 1].astype(F32))
            o_ref[j] = (mine + g_ref[j].astype(F32)).astype(o_ref.dtype)

    return _pcall(body, name=name, grid=(r // tr,), out_shape=_sds(got.shape, got.dtype),
                  in_specs=[pl.BlockSpec((NDEV, tr, c), lambda t: (0, t, 0)), pl.BlockSpec((4, tr, c), lambda t: (0, t, 0))],
                  out_specs=pl.BlockSpec((4, tr, c), lambda t: (0, t, 0)))(x, got)


def _sum_chips(part, recv, tr, name):
    _, r, c = part.shape

    def body(p_ref, r_ref, o_ref):
        slot = 2 * lax.axis_index("x") + lax.axis_index("y")
        acc = p_ref[0].astype(F32)
        for j in range(1, 4):
            acc = jnp.where(slot == j, p_ref[j].astype(F32), acc)
        for j in range(3):
            acc = acc + r_ref[j].astype(F32)
        o_ref[...] = acc

    return _pcall(body, name=name, grid=(r // tr,), out_shape=_sds((r, c), F32),
                  in_specs=[pl.BlockSpec((4, tr, c), lambda t: (0, t, 0)), pl.BlockSpec((3, tr, c), lambda t: (0, t, 0))],
                  out_specs=pl.BlockSpec((tr, c), lambda t: (t, 0)))(part, recv)


def _sum_slots(x, tr, name):
    n, r, c = x.shape

    def body(x_ref, o_ref):
        acc = x_ref[0].astype(F32)
        for s in range(1, n):
            acc = acc + x_ref[s].astype(F32)
        o_ref[...] = acc

    return _pcall(body, name=name, grid=(r // tr,), out_shape=_sds((r, c), F32),
                  in_specs=[pl.BlockSpec((n, tr, c), lambda t: (0, t, 0))],
                  out_specs=pl.BlockSpec((tr, c), lambda t: (t, 0)))(x)


def _ada_fwd(a_raw, w_loc, b_loc):
    ncol = w_loc.shape[1]

    def body(a_ref, w_ref, b_ref, o_ref):
        a = a_ref[...]
        act = (a * _sigmoid(a)).astype(BF16)
        o_ref[...] = _mm(act, w_ref[...].astype(BF16)) + b_ref[...]

    return _pcall(body, name="ada_fwd", out_shape=_sds((a_raw.shape[0], ncol), F32),
                  in_specs=[VMEM] * 3, out_specs=VMEM)(a_raw, w_loc, b_loc)


def _ada_bwd(a_raw, cctx_col, g_all, g_cols, w_loc, nb):
    nrow = a_raw.shape[0]
    ncol = w_loc.shape[1]

    def body(a_ref, cc_ref, gall_ref, g_ref, w_ref, dw_ref, pc_ref, gb_ref):
        a = a_ref[...]
        rowid = lax.broadcasted_iota(jnp.int32, (nrow, 1), 0) % 8
        act = jnp.where(rowid < nb, a * _sigmoid(a), 0.0).astype(BF16)
        g = g_ref[...]
        gc = _rowsum(jnp.where(rowid == nb, g, 0.0))
        cc = cc_ref[...]
        dw_ref[...] = _mm_tn(act, g.astype(BF16)) + (cc * _sigmoid(cc)) * gc
        pc_ref[...] = jnp.sum(w_ref[...] * gc, axis=1, keepdims=True)
        gb_ref[...] = _rowsum(gall_ref[...])

    return _pcall(body, name="ada_bwd",
                  out_shape=(_sds((D, ncol), F32), _sds((D, 1), F32), _sds((1, g_all.shape[1]), F32)),
                  in_specs=[VMEM] * 5, out_specs=(VMEM,) * 3, vmem_mb=48)(a_raw, cctx_col, g_all, g_cols, w_loc)


def _mod_spec(k, tpe, nrows):
    return pl.BlockSpec((1, k, D), lambda t: (jnp.minimum(t // tpe, nrows - 1), 0, 0))


def _load_ffn_weights(wall_ref, first, bufs, sems):
    fsh = FF // NDEV
    cps = []
    for j, buf in enumerate(bufs):
        for d in range(NDEV):
            cps.append(pltpu.make_async_copy(wall_ref.at[d, pl.ds((first + j) * fsh, fsh)],
                                             buf.at[pl.ds(d * fsh, fsh)], sems.at[j * NDEV + d]))
    for cp in cps:
        cp.start()
    for cp in cps:
        cp.wait()


def _token_specs(xs, tm, n_lat):
    specs = [pl.BlockSpec((tm, D), lambda t: (jnp.minimum(t, n_lat - 1), 0))]
    if len(xs) == 2:
        specs.append(pl.BlockSpec((tm, D), lambda t: (jnp.maximum(t - n_lat, 0), 0)))
    return specs


def _ffn_fwd(xs, mod3, norm_w, wall, first, *, tm, n_tiles, tpe, n_lat, name, target=None, gather=None):
    nrows = mod3.shape[0]
    r = n_tiles * tm
    nx = len(xs)
    with_loss = target is not None
    with_gather = gather is not None
    fwd_step = max(2 * n_tiles // 3, 1)

    def body(*refs):
        x_refs = refs[:nx]
        pos = nx
        if with_loss:
            tgt_ref = refs[pos]
            pos += 1
        mod_ref, nw_ref, wall_ref = refs[pos:pos + 3]
        pos += 3
        if with_gather:
            gin_ref = refs[pos]
            pos += 1
        xo_ref, a_ref, b_ref, o_ref = refs[pos:pos + 4]
        pos += 4
        if with_loss:
            ls_ref = refs[pos]
            pos += 1
        if with_gather:
            gout_ref = refs[pos]
            pos += 1
        w1_ref, w3_ref, w2_ref, wsem, acc_ref = refs[pos:pos + 5]
        t = pl.program_id(0)
        if with_gather:
            g_start, g_forward, g_finish = _gather_phases(gin_ref, gout_ref, *refs[pos + 5:])

        @pl.when(t == 0)
        def _():
            if with_gather:
                g_start()
            _load_ffn_weights(wall_ref, first, (w1_ref, w3_ref, w2_ref), wsem)
            if with_loss:
                ls_ref[...] = jnp.zeros_like(ls_ref)

        if with_gather:
            @pl.when(t == fwd_step)
            def _():
                g_forward()

            @pl.when(t == n_tiles - 1)
            def _():
                g_finish()

        x = x_refs[0][...]
        if nx == 2:
            x = jnp.where(t < n_lat, x, x_refs[1][...])
        n = x * _rms(x) * nw_ref[...]
        shift, scale, gate = mod_ref[0, 0:1, :], mod_ref[0, 1:2, :], mod_ref[0, 2:3, :]
        h = (n * (1.0 + scale) + shift).astype(BF16)
        nch = FF // FC
        o = None
        for lo_c, hi_c in ((0, nch // 2), (nch // 2, nch)):
            for j in range(lo_c, hi_c):
                sl = slice(j * FC, (j + 1) * FC)
                a = _mm_nt(h, w1_ref[sl, :])
                b = _mm_nt(h, w3_ref[sl, :])
                a_ref[:, sl] = a.astype(BF16)
                b_ref[:, sl] = b.astype(BF16)
                acc_ref[:, sl] = (a * _sigmoid(a) * b).astype(BF16)
            gs = slice(lo_c * FC, hi_c * FC)
            part = _mm(acc_ref[:, gs], w2_ref[gs, :])
            o = part if o is None else o + part
        o_ref[...] = o.astype(BF16)
        out = x + (0.5 * gate) * o
        if with_loss:
            d = out - tgt_ref[...]
            xo_ref[...] = d * (1.0 / D)
            ls_ref[...] += jnp.sum(d * d)
        else:
            xo_ref[...] = out

    row = lambda cols: pl.BlockSpec((tm, cols), lambda t: (t, 0))
    in_specs = _token_specs(xs, tm, n_lat) + ([row(D)] if with_loss else []) + [
        _mod_spec(3, tpe, nrows), _const((1, D)), ANY]
    out_shape = [_sds((r, D), F32), _sds((r, FF), BF16), _sds((r, FF), BF16), _sds((r, D), BF16)]
    out_specs = [row(D), row(FF), row(FF), row(D)]
    scratch = [pltpu.VMEM((FF, D), BF16)] * 3 + [pltpu.SemaphoreType.DMA((3 * NDEV,)), pltpu.VMEM((tm, FF), BF16)]
    if with_loss:
        out_shape.append(_sds((8, LANE), F32))
        out_specs.append(_const((8, LANE)))
    args = list(xs) + ([target] if with_loss else []) + [mod3, norm_w, wall]
    if with_gather:
        assert n_tiles >= 2
        in_specs.append(ANY)
        args.append(gather)
        out_shape.append(_sds((NDEV,) + gather.shape, gather.dtype))
        out_specs.append(ANY)
        scratch += _GATHER_SEMS
    return _pcall(
        body, name=name, grid=(n_tiles,), out_shape=tuple(out_shape), in_specs=in_specs, out_specs=tuple(out_specs),
        scratch=scratch, vmem_mb=56)(*args)


def _ffn_bwd_dx(dout, xs, a, b, o, mod3, norm_w, wall, first, *, tm, n_tiles, tpe, n_lat, name):
    nrows = mod3.shape[0]
    r = n_tiles * tm
    nx = len(xs)

    def body(*refs):
        dout_ref = refs[0]
        x_refs = refs[1:1 + nx]
        (a_ref, b_ref, o_ref, mod_ref, nw_ref, wall_ref,
         dx_ref, da_ref, db_ref, g_ref, do_ref, h_ref, dmod_ref, dnw_ref,
         w1_ref, w3_ref, w2_ref, wsem) = refs[1 + nx:]
        t = pl.program_id(0)

        @pl.when(t == 0)
        def _():
            _load_ffn_weights(wall_ref, first, (w1_ref, w3_ref, w2_ref), wsem)
            dnw_ref[...] = jnp.zeros_like(dnw_ref)

        x = x_refs[0][...]
        if nx == 2:
            x = jnp.where(t < n_lat, x, x_refs[1][...])
        dout = dout_ref[...]
        rr = _rms(x)
        xh = x * rr
        nw = nw_ref[...]
        n = xh * nw
        shift, scale, gate = mod_ref[0, 0:1, :], mod_ref[0, 1:2, :], mod_ref[0, 2:3, :]
        h = (n * (1.0 + scale) + shift).astype(BF16)
        h_ref[...] = h
        d_o = ((0.5 * gate) * dout).astype(BF16)
        do_ref[...] = d_o
        dgate = _rowsum(0.5 * o_ref[...].astype(F32) * dout)
        nch = FF // FC
        groups = ((0, nch // 2), (nch // 2, nch))
        dh = None
        for lo_c, hi_c in groups:
            for j in range(lo_c, hi_c):
                sl = slice(j * FC, (j + 1) * FC)
                av = a_ref[:, sl].astype(F32)
                bv = b_ref[:, sl].astype(F32)
                dg = _mm_nt(d_o, w2_ref[sl, :])
                sig = _sigmoid(av)
                sa = av * sig
                g_ref[:, sl] = (sa * bv).astype(BF16)
                da_ref[:, sl] = (dg * bv * (sig * (1.0 + av * (1.0 - sig)))).astype(BF16)
                db_ref[:, sl] = (dg * sa).astype(BF16)
            gs = slice(lo_c * FC, hi_c * FC)
            part = _mm(da_ref[:, gs], w1_ref[gs, :]) + _mm(db_ref[:, gs], w3_ref[gs, :])
            dh = part if dh is None else dh + part
        dn = dh * (1.0 + scale)
        dxh = dn * nw

        @pl.when(t < n_lat)
        def _():
            dx_ref[...] = dout + rr * (dxh - xh * jnp.mean(dxh * xh, axis=-1, keepdims=True))

        first_visit = jnp.where(t < n_lat, t % tpe == 0, t == n_lat)

        @pl.when(first_visit)
        def _():
            dmod_ref[...] = jnp.zeros_like(dmod_ref)

        dmod_ref[0, 0:1, :] += _rowsum(dh)
        dmod_ref[0, 1:2, :] += _rowsum(dh * n)
        dmod_ref[0, 2:3, :] += dgate
        dnw_ref[...] += _rowsum(dn * xh)

    row = lambda cols: pl.BlockSpec((tm, cols), lambda t: (t, 0))
    lat = pl.BlockSpec((tm, D), lambda t: (jnp.minimum(t, n_lat - 1), 0))
    return _pcall(
        body, name=name, grid=(n_tiles,),
        out_shape=(_sds((n_lat * tm, D), F32), _sds((r, FF), BF16), _sds((r, FF), BF16), _sds((r, FF), BF16),
                   _sds((r, D), BF16), _sds((r, D), BF16), _sds((nrows, 3, D), F32), _sds((1, D), F32)),
        in_specs=[row(D)] + _token_specs(xs, tm, n_lat) + [row(FF), row(FF), row(D), _mod_spec(3, tpe, nrows),
                                                            _const((1, D)), ANY],
        out_specs=(lat, row(FF), row(FF), row(FF), row(D), row(D), _mod_spec(3, tpe, nrows), _const((1, D))),
        scratch=[pltpu.VMEM((FF, D), BF16)] * 3 + [pltpu.SemaphoreType.DMA((3 * NDEV,))],
        vmem_mb=60)(dout, *xs, a, b, o, mod3, norm_w, wall)


def _ffn_bwd_dw(h, d_o, da, db, g, *, tr, name):
    r = h.shape[0]
    fh = FF // 2
    fsh = FF // NDEV
    nk = r // tr

    def body(h_ref, do_ref, da_ref, db_ref, g_ref, out_ref, acc1, acc3, acc2):
        k = pl.program_id(1)

        @pl.when(k == 0)
        def _():
            acc1[...] = jnp.zeros_like(acc1)
            acc3[...] = jnp.zeros_like(acc3)
            acc2[...] = jnp.zeros_like(acc2)

        hv = h_ref[...]
        acc1[...] += _mm_tn(da_ref[...], hv)
        acc3[...] += _mm_tn(db_ref[...], hv)
        acc2[...] += _mm_tn(g_ref[...], do_ref[...])

        @pl.when(k == nk - 1)
        def _():
            for i, acc in enumerate((acc1, acc3, acc2)):
                out_ref[:, i * fsh:(i + 1) * fsh, :] = acc[...].reshape(NDEV // 2, fsh, D).astype(BF16)

    rowd = pl.BlockSpec((tr, D), lambda f, k: (k, 0))
    rowf = pl.BlockSpec((tr, fh), lambda f, k: (k, f))
    return _pcall(
        body, name=name, grid=(2, nk), out_shape=_sds((NDEV, 3 * fsh, D), BF16),
        in_specs=[rowd, rowd, rowf, rowf, rowf],
        out_specs=pl.BlockSpec((NDEV // 2, 3 * fsh, D), lambda f, k: (f, 0, 0)),
        scratch=[pltpu.VMEM((fh, D), F32)] * 3, vmem_mb=56)(h, d_o, da, db, g)


_PIECES = ((0, 128), (128, 384), (384, 896), (896, 1408), (1408, 1536))


def _proj_fwd(x1, mod2, norm_w, wint, *, tm, n_tiles, tpe, name="proj_fwd"):
    nrows = mod2.shape[0]
    r = n_tiles * tm

    def body(x_ref, mod_ref, nw_ref, w_ref, ckv_ref, q_ref, u_ref, v_ref, kpe_ref):
        x = x_ref[...]
        n = x * _rms(x) * nw_ref[...]
        h = (n * (1.0 + mod_ref[0, 1:2, :]) + mod_ref[0, 0:1, :]).astype(BF16)
        for (lo, hi), ref in zip(_PIECES, (ckv_ref, q_ref, u_ref, v_ref, kpe_ref)):
            ref[...] = _mm_nt(h, w_ref[lo:hi, :])

    row = lambda cols: pl.BlockSpec((tm, cols), lambda t: (t, 0))
    widths = [hi - lo for lo, hi in _PIECES]
    return _pcall(
        body, name=name, grid=(n_tiles,),
        out_shape=tuple(_sds((r, w), F32) for w in widths),
        in_specs=[row(D), _mod_spec(2, tpe, nrows), _const((1, D)), _const((WIN_ROWS, D))],
        out_specs=tuple(row(w) for w in widths), vmem_mb=40)(x1, mod2, norm_w, wint)


def _proj_bwd(dckv, dkpe, dq, du, dv, dx2, x1, mod2, norm_w, wint, *, tm, n_tiles, tpe, n_lat, name="proj_bwd"):
    nrows = mod2.shape[0]
    r = n_tiles * tm

    def body(dckv_ref, dkpe_ref, dq_ref, du_ref, dv_ref, dx2_ref, x_ref, mod_ref, nw_ref, w_ref,
             dx_ref, dw_ref, dmod_ref, dnw_ref, acc_ref):
        t = pl.program_id(0)
        is_lat = t < n_lat
        x = x_ref[...]
        rr = _rms(x)
        xh = x * rr
        nw = nw_ref[...]
        n = xh * nw
        scale = mod_ref[0, 1:2, :]
        h = (n * (1.0 + scale) + mod_ref[0, 0:1, :]).astype(BF16)

        @pl.when(t == 0)
        def _():
            dw_ref[...] = jnp.zeros_like(dw_ref)
            dnw_ref[...] = jnp.zeros_like(dnw_ref)

        dckv_v, dkpe_v = dckv_ref[...], dkpe_ref[...]
        acc_ref[...] = _mm(dckv_v, w_ref[0:128, :]) + _mm(dkpe_v, w_ref[1408:1536, :])
        dw_ref[0:128, :] += _mm_tn(dckv_v, h)
        dw_ref[1408:1536, :] += _mm_tn(dkpe_v, h)

        @pl.when(is_lat)
        def _():
            dq_v, du_v, dv_v = dq_ref[...], du_ref[...], dv_ref[...]
            acc_ref[...] += (_mm(dq_v, w_ref[128:384, :]) + _mm(du_v, w_ref[384:896, :])
                             + _mm(dv_v, w_ref[896:1408, :]))
            dw_ref[128:384, :] += _mm_tn(dq_v, h)
            dw_ref[384:896, :] += _mm_tn(du_v, h)
            dw_ref[896:1408, :] += _mm_tn(dv_v, h)

        dh = acc_ref[...]
        dn = dh * (1.0 + scale)
        dxh = dn * nw
        dx = rr * (dxh - xh * jnp.mean(dxh * xh, axis=-1, keepdims=True))
        dx_ref[...] = dx + jnp.where(is_lat, dx2_ref[...], 0.0)

        first = jnp.where(is_lat, t % tpe == 0, t == n_lat)

        @pl.when(first)
        def _():
            dmod_ref[...] = jnp.zeros_like(dmod_ref)

        dmod_ref[0, 0:1, :] += _rowsum(dh)
        dmod_ref[0, 1:2, :] += _rowsum(dh * n)
        dnw_ref[...] += _rowsum(dn * xh)

    row = lambda cols: pl.BlockSpec((tm, cols), lambda t: (t, 0))
    lat = lambda cols: pl.BlockSpec((tm, cols), lambda t: (jnp.minimum(t, n_lat - 1), 0))
    return _pcall(
        body, name=name, grid=(n_tiles,),
        out_shape=(_sds((r, D), F32), _sds((WIN_ROWS, D), F32), _sds((nrows, 2, D), F32), _sds((1, D), F32)),
        in_specs=[row(128), row(128), lat(256), lat(512), lat(512), lat(D), row(D), _mod_spec(2, tpe, nrows),
                  _const((1, D)), _const((WIN_ROWS, D))],
        out_specs=(row(D), _const((WIN_ROWS, D)), _mod_spec(2, tpe, nrows), _const((1, D))),
        scratch=[pltpu.VMEM((tm, D), F32)], vmem_mb=48)(dckv, dkpe, dq, du, dv, dx2, x1, mod2, norm_w, wint)


def _head_norm_rope(x, w_pad, cos, sin, seg, segt, rot):
    rh = lax.rsqrt(_dot_hl(x * x, seg) * (1.0 / DH) + EPS)
    rb = _dot_hl(rh, segt)
    y = x * rb
    t = y * w_pad
    out = []
    for h in range(H):
        th = t[:, h * LANE:(h + 1) * LANE]
        out.append(th * cos + _dot_hl(th, rot) * sin)
    return jnp.concatenate(out, axis=-1), y, rb


def _head_norm_rope_bwd(dout, y, rb, w_pad, cos, sin, seg, segt, rot_t):
    dt = []
    for h in range(H):
        dh = dout[:, h * LANE:(h + 1) * LANE]
        dt.append(dh * cos + _dot_hl(dh * sin, rot_t))
    dt = jnp.concatenate(dt, axis=-1)
    dw = _rowsum(dt * y)
    dy = dt * w_pad
    mean_h = _dot_hl(dy * y, seg) * (1.0 / DH)
    return rb * (dy - y * _dot_hl(mean_h, segt)), dw


def _q_prep_fwd(qp, qa_w, wuq, wq, cos, sin, cs, *, tm, n_lat, tpe):
    def body(qp_ref, qa_ref, wuq_ref, wq_ref, cos_ref, sin_ref, seg, segt, rot, q_ref):
        x = qp_ref[...]
        cq = (x * _rms(x) * qa_ref[...]).astype(BF16)
        q, _, _ = _head_norm_rope(_mm_nt(cq, wuq_ref[...]), wq_ref[...], cos_ref[...], sin_ref[...],
                                  seg[...], segt[...], rot[...])
        q_ref[...] = q.astype(BF16)

    row = lambda cols: pl.BlockSpec((tm, cols), lambda t: (t, 0))
    tab = pl.BlockSpec((tm, LANE), lambda t: (t % tpe, 0))
    return _pcall(
        body, name="q_prep_fwd", grid=(n_lat,), out_shape=_sds((n_lat * tm, HP), BF16),
        in_specs=[row(QL), _const((1, QL)), _const((HP, QL)), _const((1, HP)), tab, tab,
                  _const((HP, LANE)), _const((LANE, HP)), _const((LANE, LANE))],
        out_specs=row(HP))(qp, qa_w, wuq, wq, cos, sin, cs["seg_h"], cs["seg_ht"], cs["rot"])


def _q_prep_bwd(dq, qp, qa_w, wuq, wq, cos, sin, cs, *, tm, n_lat, tpe):
    def body(dq_ref, qp_ref, qa_ref, wuq_ref, wq_ref, cos_ref, sin_ref, seg, segt, rot, rot_t,
             dqp_ref, dwuq_ref, dqa_ref, dwq_ref):
        t = pl.program_id(0)
        x = qp_ref[...]
        ra = _rms(x)
        xh = x * ra
        qa = qa_ref[...]
        cq = (xh * qa).astype(BF16)
        wuq_v = wuq_ref[...]
        wq_v, cos_v, sin_v = wq_ref[...], cos_ref[...], sin_ref[...]
        _, y, rb = _head_norm_rope(_mm_nt(cq, wuq_v), wq_v, cos_v, sin_v, seg[...], segt[...], rot[...])
        dqraw, dwq = _head_norm_rope_bwd(dq_ref[...], y, rb, wq_v, cos_v, sin_v, seg[...], segt[...], rot_t[...])
        dqraw = dqraw.astype(BF16)
        dcq = _mm(dqraw, wuq_v)
        dxh = dcq * qa
        dqp_ref[...] = (ra * (dxh - xh * jnp.mean(dxh * xh, axis=-1, keepdims=True))).astype(BF16)

        @pl.when(t == 0)
        def _():
            dwuq_ref[...] = jnp.zeros_like(dwuq_ref)
            dqa_ref[...] = jnp.zeros_like(dqa_ref)
            dwq_ref[...] = jnp.zeros_like(dwq_ref)

        dwuq_ref[...] += _mm_tn(dqraw, cq)
        dqa_ref[...] += _rowsum(dcq * xh)
        dwq_ref[...] += dwq

    row = lambda cols: pl.BlockSpec((tm, cols), lambda t: (t, 0))
    tab = pl.BlockSpec((tm, LANE), lambda t: (t % tpe, 0))
    return _pcall(
        body, name="q_prep_bwd", grid=(n_lat,),
        out_shape=(_sds((n_lat * tm, QL), BF16), _sds((HP, QL), F32), _sds((1, QL), F32), _sds((1, HP), F32)),
        in_specs=[row(HP), row(QL), _const((1, QL)), _const((HP, QL)), _const((1, HP)), tab, tab,
                  _const((HP, LANE)), _const((LANE, HP)), _const((LANE, LANE)), _const((LANE, LANE))],
        out_specs=(row(QL), _const((HP, QL)), _const((1, QL)), _const((1, HP))), vmem_mb=40)(
            dq, qp, qa_w, wuq, wq, cos, sin, cs["seg_h"], cs["seg_ht"], cs["rot"], cs["rot_t"])


def _kv_tab_spec(tm, tpe, n_lat):
    return pl.BlockSpec((tm, LANE), lambda t: (jnp.where(t < n_lat, t % tpe, tpe), 0))


def _kv_prep_fwd(ckv, kpe, kva_w, wukv, wk, cosk, sink, cs, *, tm, n_tiles, tpe, n_lat):
    def body(ckv_ref, kpe_ref, kva_ref, wukv_ref, wk_ref, cos_ref, sin_ref, seg, segt, rot, k_ref, v_ref):
        x = ckv_ref[...]
        ckvn = (x * _rms(x) * kva_ref[...]).astype(BF16)
        kv = _mm_nt(ckvn, wukv_ref[...])
        kx = kv[:, :HP] + jnp.concatenate([kpe_ref[...]] * H, axis=-1)
        k, _, _ = _head_norm_rope(kx, wk_ref[...], cos_ref[...], sin_ref[...], seg[...], segt[...], rot[...])
        k_ref[...] = k.astype(BF16)
        v_ref[...] = kv[:, HP:].astype(BF16)

    row = lambda cols: pl.BlockSpec((tm, cols), lambda t: (t, 0))
    tab = _kv_tab_spec(tm, tpe, n_lat)
    r = n_tiles * tm
    return _pcall(
        body, name="kv_prep_fwd", grid=(n_tiles,), out_shape=(_sds((r, HP), BF16), _sds((r, HP), BF16)),
        in_specs=[row(KVL), row(LANE), _const((1, KVL)), _const((2 * HP, KVL)), _const((1, HP)), tab, tab,
                  _const((HP, LANE)), _const((LANE, HP)), _const((LANE, LANE))],
        out_specs=(row(HP), row(HP)), vmem_mb=40)(
            ckv, kpe, kva_w, wukv, wk, cosk, sink, cs["seg_h"], cs["seg_ht"], cs["rot"])


def _kv_prep_bwd(dks, dvs, ckv, kpe, kva_w, wukv, wk, cosk, sink, cs, *, tm, n_tiles, tpe, n_lat):
    def body(dkl_ref, dkc_ref, dvl_ref, dvc_ref, ckv_ref, kpe_ref, kva_ref, wukv_ref, wk_ref, cos_ref, sin_ref,
             seg, segt, rot, rot_t, dckv_ref, dkpe_ref, dwukv_ref, dkva_ref, dwk_ref):
        t = pl.program_id(0)
        is_lat = t < n_lat
        dk = jnp.where(is_lat, dkl_ref[...], dkc_ref[...])
        dv = jnp.where(is_lat, dvl_ref[...], dvc_ref[...])
        x = ckv_ref[...]
        ra = _rms(x)
        xh = x * ra
        kva = kva_ref[...]
        ckvn = (xh * kva).astype(BF16)
        wukv_v = wukv_ref[...]
        wk_v, cos_v, sin_v = wk_ref[...], cos_ref[...], sin_ref[...]
        kv = _mm_nt(ckvn, wukv_v)
        kx = kv[:, :HP] + jnp.concatenate([kpe_ref[...]] * H, axis=-1)
        _, y, rb = _head_norm_rope(kx, wk_v, cos_v, sin_v, seg[...], segt[...], rot[...])
        dkx, dwk = _head_norm_rope_bwd(dk, y, rb, wk_v, cos_v, sin_v, seg[...], segt[...], rot_t[...])
        dkpe = dkx[:, 0:LANE]
        for h in range(1, H):
            dkpe = dkpe + dkx[:, h * LANE:(h + 1) * LANE]
        lane = lax.broadcasted_iota(jnp.int32, (tm, LANE), 1)
        dkpe_ref[...] = jnp.where((lane >= DN) & (lane < DH), dkpe, 0.0).astype(BF16)
        dkv = jnp.concatenate([dkx, dv], axis=-1).astype(BF16)
        dckvn = _mm(dkv, wukv_v)
        dxh = dckvn * kva
        dckv_ref[...] = (ra * (dxh - xh * jnp.mean(dxh * xh, axis=-1, keepdims=True))).astype(BF16)

        @pl.when(t == 0)
        def _():
            dwukv_ref[...] = jnp.zeros_like(dwukv_ref)
            dkva_ref[...] = jnp.zeros_like(dkva_ref)
            dwk_ref[...] = jnp.zeros_like(dwk_ref)

        dwukv_ref[...] += _mm_tn(dkv, ckvn)
        dkva_ref[...] += _rowsum(dckvn * xh)
        dwk_ref[...] += dwk

    row = lambda cols: pl.BlockSpec((tm, cols), lambda t: (t, 0))
    lat = pl.BlockSpec((tm, HP), lambda t: (jnp.minimum(t, n_lat - 1), 0))
    ctx = pl.BlockSpec((tm, HP), lambda t: (jnp.maximum(t - n_lat, 0), 0))
    tab = _kv_tab_spec(tm, tpe, n_lat)
    r = n_tiles * tm
    return _pcall(
        body, name="kv_prep_bwd", grid=(n_tiles,),
        out_shape=(_sds((r, KVL), BF16), _sds((r, LANE), BF16), _sds((2 * HP, KVL), F32), _sds((1, KVL), F32),
                   _sds((1, HP), F32)),
        in_specs=[lat, ctx, lat, ctx, row(KVL), row(LANE), _const((1, KVL)), _const((2 * HP, KVL)), _const((1, HP)),
                  tab, tab, _const((HP, LANE)), _const((LANE, HP)), _const((LANE, LANE)), _const((LANE, LANE))],
        out_specs=(row(KVL), row(LANE), _const((2 * HP, KVL)), _const((1, KVL)), _const((1, HP))), vmem_mb=48)(
            dks[0], dks[1], dvs[0], dvs[1], ckv, kpe, kva_w, wukv, wk, cosk, sink,
            cs["seg_h"], cs["seg_ht"], cs["rot"], cs["rot_t"])


_SCALE = DH ** -0.5
_SCALE_LOG2E = _SCALE * 1.4426950408889634


def _attn_specs(tq, s, nc, tpe, n_lat_rows):
    qs = pl.BlockSpec((tq, LANE), lambda i, j, t: (i * tpe + t, j))
    kl = pl.BlockSpec((s, LANE), lambda i, j, t: (i, j))
    kc = pl.BlockSpec((nc, LANE), lambda i, j, t: (n_lat_rows // nc + i, j))
    return qs, kl, kc


def _key_chunks(s, nc, ck):
    return ([(0, lo, min(lo + ck, s)) for lo in range(0, s, ck)]
            + [(1, lo, min(lo + ck, nc)) for lo in range(0, nc, ck)])


def _lse_spec(tq):
    return pl.BlockSpec((1, 8, tq), lambda i, j, t: (i * H + j, 0, t))


def _attn_fwd(q, k, v, *, nb, s, nc, tq, ck):
    tpe = s // tq
    r_lat = nb * s
    chunks = _key_chunks(s, nc, ck)
    hp = 4

    def body(q_ref, kl_ref, kc_ref, vl_ref, vc_ref, o_ref, lse_ref):
        k_refs, v_refs = (kl_ref, kc_ref), (vl_ref, vc_ref)
        for hh in range(hp):
            hs = slice(hh * LANE, (hh + 1) * LANE)
            qv = q_ref[:, hs]
            xs = [_mm_nt(qv, k_refs[w][lo:hi, hs]) for w, lo, hi in chunks]
            m = jnp.max(xs[0], axis=-1, keepdims=True)
            for x in xs[1:]:
                m = jnp.maximum(m, jnp.max(x, axis=-1, keepdims=True))
            l = acc = None
            for x, (w, lo, hi) in zip(xs, chunks):
                e = jnp.exp2((x - m) * _SCALE_LOG2E)
                lc = jnp.sum(e, axis=-1, keepdims=True)
                pv = _mm(e.astype(BF16), v_refs[w][lo:hi, hs])
                l = lc if l is None else l + lc
                acc = pv if acc is None else acc + pv
            o_ref[:, hs] = (acc / l).astype(BF16)
            lse = m * _SCALE_LOG2E + jnp.log2(l)
            lse_ref[hh] = jnp.transpose(jnp.broadcast_to(lse, (tq, LANE)))[0:8, :]

    qs = pl.BlockSpec((tq, hp * LANE), lambda i, j, t: (i * tpe + t, j))
    kl = pl.BlockSpec((s, hp * LANE), lambda i, j, t: (i, j))
    kc = pl.BlockSpec((nc, hp * LANE), lambda i, j, t: (r_lat // nc + i, j))
    ls = pl.BlockSpec((hp, 8, tq), lambda i, j, t: (i * (H // hp) + j, 0, t))
    return _pcall(body, name="attn_fwd", grid=(nb, H // hp, tpe),
                  out_shape=(_sds((r_lat, HP), BF16), _sds((nb * H, 8, s), F32)),
                  in_specs=[qs, kl, kc, kl, kc], out_specs=(qs, ls), vmem_mb=48)(q, k, k, v, v)


def _attn_bwd(q, k, v, o, do, lse, part, *, nb, s, nc, tq, ck):
    tpe = s // tq
    r_lat = nb * s
    chunks = _key_chunks(s, nc, ck)
    hp = 2
    n_steps = nb * (H // hp) * tpe

    def body(q_ref, kl_ref, kc_ref, vl_ref, vc_ref, o_ref, do_ref, lse_ref, part_ref,
             dq_ref, dkl_ref, dkc_ref, dvl_ref, dvc_ref, recv_ref, akl, akc, avl, avc, send_sems, recv_sems):
        t = pl.program_id(2)
        step = (pl.program_id(0) * (H // hp) + pl.program_id(1)) * tpe + t
        sends = _chip_sends(part_ref, recv_ref, send_sems, recv_sems)

        @pl.when(step == 0)
        def _():
            for cp in sends:
                cp.start()

        @pl.when(step == n_steps - 1)
        def _():
            for cp in sends:
                cp.wait_recv()
            for cp in sends:
                cp.wait_send()

        @pl.when(t == 0)
        def _():
            akl[...] = jnp.zeros_like(akl)
            akc[...] = jnp.zeros_like(akc)
            avl[...] = jnp.zeros_like(avl)
            avc[...] = jnp.zeros_like(avc)

        k_refs, v_refs, ak, av = (kl_ref, kc_ref), (vl_ref, vc_ref), (akl, akc), (avl, avc)
        for hh in range(hp):
            hs = slice(hh * LANE, (hh + 1) * LANE)
            qv = q_ref[:, hs]
            lse = jnp.transpose(jnp.concatenate([lse_ref[hh]] * (LANE // 8), axis=0))[:, 0:1]
            dov = do_ref[:, hs]
            delta = jnp.sum(dov.astype(F32) * o_ref[:, hs].astype(F32), axis=-1, keepdims=True)
            dq = None
            for w, lo, hi in chunks:
                kc_v = k_refs[w][lo:hi, hs]
                p = jnp.exp2(_mm_nt(qv, kc_v) * _SCALE_LOG2E - lse)
                ds = (p * (_mm_nt(dov, v_refs[w][lo:hi, hs]) - delta)).astype(BF16)
                part = _mm(ds, kc_v)
                dq = part if dq is None else dq + part
                ak[w][hs, lo:hi] += _mm_tn(qv, ds)
                av[w][hs, lo:hi] += _mm_tn(dov, p.astype(BF16))
            dq_ref[:, hs] = dq * _SCALE

        @pl.when(t == tpe - 1)
        def _():
            dkl_ref[...] = akl[...].T * _SCALE
            dkc_ref[...] = akc[...].T * _SCALE
            dvl_ref[...] = avl[...].T
            dvc_ref[...] = avc[...].T

    qs = pl.BlockSpec((tq, hp * LANE), lambda i, j, t: (i * tpe + t, j))
    kl = pl.BlockSpec((s, hp * LANE), lambda i, j, t: (i, j))
    kc = pl.BlockSpec((nc, hp * LANE), lambda i, j, t: (r_lat // nc + i, j))
    kc_out = pl.BlockSpec((nc, hp * LANE), lambda i, j, t: (i, j))
    ls = pl.BlockSpec((hp, 8, tq), lambda i, j, t: (i * (H // hp) + j, 0, t))
    return _pcall(
        body, name="attn_bwd", grid=(nb, H // hp, tpe),
        out_shape=(_sds((r_lat, HP), F32), _sds((r_lat, HP), F32), _sds((nb * nc, HP), F32),
                   _sds((r_lat, HP), F32), _sds((nb * nc, HP), F32), _sds((3,) + part.shape[1:], part.dtype)),
        in_specs=[qs, kl, kc, kl, kc, qs, qs, ls, ANY], out_specs=(qs, kl, kc_out, kl, kc_out, ANY),
        scratch=[pltpu.VMEM((hp * LANE, s), F32), pltpu.VMEM((hp * LANE, nc), F32)] * 2
        + [pltpu.SemaphoreType.DMA((3,))] * 2,
        vmem_mb=60)(q, k, k, v, v, o, do, lse, part)


def _gating(vn, ws_ref, bias_ref, s_scr, tm):
    lane = lax.broadcasted_iota(jnp.int32, (CH, LANE), 1)
    for c in range(tm // CH):
        rs = slice(c * CH, (c + 1) * CH)
        for j in range(G // 2):
            ls = slice(j * LANE, (j + 1) * LANE)
            vp = vn[rs, ls]
            s_scr[rs, ls] = jnp.where(lane < GD, _mm(ws_ref[2 * j], vp), _mm(ws_ref[2 * j + 1], vp)) + bias_ref[:, ls]


def _mix_fwd(u, v, attn, x1, gate, wv, ws, bias, wout, cs, *, tm, n_lat, tpe):
    nrows = gate.shape[0]

    def body(u_ref, v_ref, attn_ref, x_ref, gate_ref, wv_ref, ws_ref, bias_ref, wout_ref, seg, segt,
             x2_ref, mix_ref, s_scr):
        vg = _gelu(v_ref[...])
        rg = lax.rsqrt(_dot_hl(vg * vg, seg[...]) * (1.0 / GD) + EPS)
        vn = (vg * _dot_hl(rg, segt[...]) * wv_ref[...]).astype(BF16)
        _gating(vn, ws_ref, bias_ref, s_scr, tm)
        sg = (_gelu(u_ref[...]) * s_scr[...]).astype(BF16)
        mix = _mm(attn_ref[...], wout_ref[0:HP, :]) + _mm(sg, wout_ref[HP:, :])
        mix_ref[...] = mix.astype(BF16)
        x2_ref[...] = x_ref[...] + gate_ref[0] * mix

    row = lambda cols: pl.BlockSpec((tm, cols), lambda t: (t, 0))
    r = n_lat * tm
    return _pcall(
        body, name="mix_fwd", grid=(n_lat,),
        out_shape=(_sds((r, D), F32), _sds((r, D), BF16)),
        in_specs=[row(G * GD), row(G * GD), row(HP), row(D), _mod_spec(1, tpe, nrows), _const((1, G * GD)),
                  _const((G, CH, CH)), _const((CH, G * GD)), _const((HP + G * GD, D)), _const((G * GD, LANE)),
                  _const((LANE, G * GD))],
        out_specs=(row(D), row(D)), scratch=[pltpu.VMEM((tm, G * GD), F32)], vmem_mb=40)(
            u, v, attn, x1, gate, wv, ws, bias, wout, cs["seg_g"], cs["seg_gt"])


def _mix_bwd(dx2, mix, u, v, attn, gate, wv, ws, wst, bias, wout, cs, *, tm, n_lat, tpe):
    nrows = gate.shape[0]
    wrows = HP + G * GD

    def body(dx2_ref, mix_ref, u_ref, v_ref, attn_ref, gate_ref, wv_ref, ws_ref, wst_ref, bias_ref, wout_ref, seg, segt,
             dattn_ref, du_ref, dv_ref, dgate_ref, dwout_ref, dws_ref, dbs_ref, dwv_ref, s_scr, dvn_scr, dbias_scr):
        t = pl.program_id(0)
        dx2 = dx2_ref[...]
        dmix = (dx2 * gate_ref[0]).astype(BF16)
        dcat = _mm_nt(dmix, wout_ref[...])
        dattn_ref[...] = dcat[:, :HP].astype(BF16)
        dsg = dcat[:, HP:]

        vraw = v_ref[...]
        vg = _gelu(vraw)
        rg = lax.rsqrt(_dot_hl(vg * vg, seg[...]) * (1.0 / GD) + EPS)
        r64 = _dot_hl(rg, segt[...])
        y = vg * r64
        wv_v = wv_ref[...]
        vn = (y * wv_v).astype(BF16)
        _gating(vn, ws_ref, bias_ref, s_scr, tm)
        uraw = u_ref[...]
        ug = _gelu(uraw)
        s = s_scr[...]
        sg = (ug * s).astype(BF16)
        du_ref[...] = (dsg * s * _gelu_grad(uraw)).astype(BF16)
        ds = dsg * ug

        @pl.when(t == 0)
        def _():
            dwout_ref[...] = jnp.zeros_like(dwout_ref)
            dws_ref[...] = jnp.zeros_like(dws_ref)
            dwv_ref[...] = jnp.zeros_like(dwv_ref)
            dbias_scr[...] = jnp.zeros_like(dbias_scr)

        @pl.when(t % tpe == 0)
        def _():
            dgate_ref[...] = jnp.zeros_like(dgate_ref)

        dgate_ref[0] += _rowsum(dx2 * mix_ref[...].astype(F32))
        dwout_ref[...] += _mm_tn(jnp.concatenate([attn_ref[...], sg], axis=-1), dmix)

        lane = lax.broadcasted_iota(jnp.int32, (CH, LANE), 1)
        for c in range(tm // CH):
            rs = slice(c * CH, (c + 1) * CH)
            dbias_scr[...] += ds[rs, :]
            for j in range(G // 2):
                ls = slice(j * LANE, (j + 1) * LANE)
                dsp32 = ds[rs, ls]
                dsp = dsp32.astype(BF16)
                vp = vn[rs, ls]
                dvn_scr[rs, ls] = jnp.where(lane < GD, _mm(wst_ref[2 * j], dsp), _mm(wst_ref[2 * j + 1], dsp))
                dws_ref[2 * j] += _mm_nt(jnp.where(lane < GD, dsp32, 0.0).astype(BF16), vp)
                dws_ref[2 * j + 1] += _mm_nt(jnp.where(lane < GD, 0.0, dsp32).astype(BF16), vp)

        dvn = dvn_scr[...]
        dwv_ref[...] += _rowsum(dvn * y)
        dy = dvn * wv_v
        mean_g = _dot_hl(dy * y, seg[...]) * (1.0 / GD)
        dvg = r64 * (dy - y * _dot_hl(mean_g, segt[...]))
        dv_ref[...] = (dvg * _gelu_grad(vraw)).astype(BF16)

        @pl.when(t == n_lat - 1)
        def _():
            dbs_ref[...] = _dot_hl(dbias_scr[...], seg[...])

    row = lambda cols: pl.BlockSpec((tm, cols), lambda t: (t, 0))
    r = n_lat * tm
    return _pcall(
        body, name="mix_bwd", grid=(n_lat,),
        out_shape=(_sds((r, HP), BF16), _sds((r, G * GD), BF16), _sds((r, G * GD), BF16), _sds((nrows, 1, D), F32),
                   _sds((wrows, D), F32), _sds((G, CH, CH), F32), _sds((CH, LANE), F32), _sds((1, G * GD), F32)),
        in_specs=[row(D), row(D), row(G * GD), row(G * GD), row(HP), _mod_spec(1, tpe, nrows), _const((1, G * GD)),
                  _const((G, CH, CH)), _const((G, CH, CH)), _const((CH, G * GD)), _const((wrows, D)),
                  _const((G * GD, LANE)), _const((LANE, G * GD))],
        out_specs=(row(HP), row(G * GD), row(G * GD), _mod_spec(1, tpe, nrows), _const((wrows, D)),
                   _const((G, CH, CH)), _const((CH, LANE)), _const((1, G * GD))),
        scratch=[pltpu.VMEM((tm, G * GD), F32), pltpu.VMEM((tm, G * GD), F32), pltpu.VMEM((CH, G * GD), F32)],
        vmem_mb=56)(dx2, mix, u, v, attn, gate, wv, ws, wst, bias, wout, cs["seg_g"], cs["seg_gt"])


def _adamw_math(w, g, m, v):
    m2 = ADAM_B1 * m + (1.0 - ADAM_B1) * g
    v2 = ADAM_B2 * v + (1.0 - ADAM_B2) * (g * g)
    m_hat = m2 / (1.0 - ADAM_B1 ** ADAM_STEP)
    v_hat = v2 / (1.0 - ADAM_B2 ** ADAM_STEP)
    delta = -ADAM_LR * (m_hat / (jnp.sqrt(v_hat) + ADAM_EPS) + ADAM_WD * w)
    return delta, m2, v2


def _row_tile(r, c):
    best = r
    for tr in range(8, r, 8):
        if r % tr == 0 and tr * c * 4 <= MIB:
            best = tr
    return best


def _adamw(w, g, m, v, name):
    r, c = w.shape
    tr = _row_tile(r, c)

    def body(w_ref, g_ref, m_ref, v_ref, d_ref, mo_ref, vo_ref):
        d_ref[...], mo_ref[...], vo_ref[...] = _adamw_math(w_ref[...], g_ref[...], m_ref[...], v_ref[...])

    blk = pl.BlockSpec((tr, c), lambda t: (t, 0))
    return _pcall(body, name=name, grid=(r // tr,), out_shape=(_sds((r, c), F32),) * 3,
                  in_specs=[blk] * 4, out_specs=(blk,) * 3)(w, g, m, v)


def _adamw_small(params):
    n = len(params)

    def body(*refs):
        ins, outs = refs[:4 * n], refs[4 * n:]
        for i in range(n):
            w, g, m, v = (ins[4 * i + k][...] for k in range(4))
            if i == 0:
                sig = _sigmoid(w)
                g = g * (sig * (1.0 + w * (1.0 - sig)))
            d, m2, v2 = _adamw_math(w, g, m, v)
            outs[4 * i][...] = g
            outs[4 * i + 1][...] = d
            outs[4 * i + 2][...] = m2
            outs[4 * i + 3][...] = v2

    flat = [a for p in params for a in p]
    out_shape = tuple(_sds(p[0].shape, F32) for p in params for _ in range(4))
    res = _pcall(body, name="adamw_small", out_shape=out_shape, in_specs=[VMEM] * (4 * n),
                 out_specs=(VMEM,) * (4 * n))(*flat)
    return [res[4 * i:4 * i + 4] for i in range(n)]


def _rope_tables(s):
    rows = jnp.repeat(jnp.arange(s // GRID_W, dtype=F32), GRID_W)
    cols = jnp.tile(jnp.arange(GRID_W, dtype=F32), s // GRID_W)
    half = DR // 2
    inv = ROPE_BASE ** (-jnp.arange(0, half, 2, dtype=F32) / half)
    ang_r = rows[:, None] * inv
    ang_c = cols[:, None] * inv
    ang = jnp.concatenate([ang_r, ang_r, ang_c, ang_c], axis=-1)
    return jnp.cos(ang), jnp.sin(ang)


def _head_pad(a, real):
    return jnp.pad(a, ((0, 0), (0, LANE - real), (0, 0))).reshape(HP, a.shape[2])


def kernel(x, c, ctx, c_ctx, w_ada, b_ada, norm1_w, ffn1_w1, ffn1_w3, ffn1_w2, norm2_w, w_in, q_a_norm_w, w_uq, kv_a_norm_w, w_ukv, q_norm_w, k_norm_w, v_norm_w, w_s, b_s, w_out, norm3_w, ffn2_w1, ffn2_w3, ffn2_w2, loss_target, m_c_ctx, m_w_ada, m_b_ada, m_norm1_w, m_ffn1_w1, m_ffn1_w3, m_ffn1_w2, m_norm2_w, m_w_in, m_q_a_norm_w, m_w_uq, m_kv_a_norm_w, m_w_ukv, m_q_norm_w, m_k_norm_w, m_v_norm_w, m_w_s, m_b_s, m_w_out, m_norm3_w, m_ffn2_w1, m_ffn2_w3, m_ffn2_w2, v_c_ctx, v_w_ada, v_b_ada, v_norm1_w, v_ffn1_w1, v_ffn1_w3, v_ffn1_w2, v_norm2_w, v_w_in, v_q_a_norm_w, v_w_uq, v_kv_a_norm_w, v_w_ukv, v_q_norm_w, v_k_norm_w, v_v_norm_w, v_w_s, v_b_s, v_w_out, v_norm3_w, v_ffn2_w1, v_ffn2_w3, v_ffn2_w2):
    nb, s, _ = x.shape
    nc = ctx.shape[1]
    tm = 256 if nc % 256 == 0 else 128
    tpe = s // tm
    n_lat = nb * tpe
    n_all = n_lat + nb * nc // tm
    r_lat = nb * s
    me = 4 * lax.axis_index("x") + 2 * lax.axis_index("y") + lax.axis_index("c")
    cs = _consts()
    ncol = w_ada.shape[2]
    fsh = ffn1_w1.shape[2]
    assert nb + 1 <= 8 and NDEV * fsh == FF and NDEV * ncol == NMOD * D and s % nc == 0 and nc % tm == 0

    a_loc = jnp.concatenate([c, c_ctx[None, :], jnp.zeros((7 - nb, D), F32)], axis=0)
    a_raw = _all_gather(a_loc, "gather_c").reshape(NDEV * 8, D)
    mod_cols = _ada_fwd(a_raw, w_ada[0], lax.dynamic_slice_in_dim(b_ada, me * ncol, ncol, axis=1))
    mod_all = _all_gather(mod_cols, "gather_mod")
    mod_mine = lax.dynamic_slice_in_dim(mod_all, 8 * me, 8, axis=1)
    modtab = mod_mine.transpose(1, 0, 2).reshape(8, NMOD, D)[:nb + 1]

    def t16(a):
        return a.T.astype(BF16)

    wpack1 = jnp.concatenate([t16(ffn1_w1[0]), t16(ffn1_w3[0]), ffn1_w2[0].astype(BF16)], axis=0)
    wpack2 = jnp.concatenate([
        t16(ffn2_w1[0]), t16(ffn2_w3[0]), ffn2_w2[0].astype(BF16),
        t16(w_in[0]), jnp.zeros((12, D), BF16),
        w_out[0].astype(BF16),
        t16(w_uq[0]).reshape(24, D), jnp.zeros((8, D), BF16),
        t16(w_ukv[0]).reshape(16, D)], axis=0)
    wall1 = _all_gather(wpack1, "gather_w_ffn1")

    def head_w(wn):
        return jnp.tile(jnp.pad(wn, ((0, 0), (0, LANE - DH))), (1, H))

    wq, wk = head_w(q_norm_w), head_w(k_norm_w)
    wv = v_norm_w.reshape(1, G * GD)
    ws16 = w_s[0].astype(BF16)
    wst16 = w_s[0].transpose(0, 2, 1).astype(BF16)
    bias = jnp.repeat(b_s[0].T, GD, axis=1)
    cos, sin = _rope_tables(s)
    cos = jnp.pad(cos, ((0, 0), (DN, LANE - DH)), constant_values=1.0)
    sin = jnp.pad(sin, ((0, 0), (DN, LANE - DH)))
    cos_k = jnp.concatenate([cos, jnp.ones((tm, LANE), F32)], axis=0)
    sin_k = jnp.concatenate([sin, jnp.zeros((tm, LANE), F32)], axis=0)

    xs = (x.reshape(r_lat, D), ctx.reshape(nb * nc, D))
    tmf = 2 * tm if s % (2 * tm) == 0 and (nb * nc) % (2 * tm) == 0 else tm
    x1, a1, b1, o1, wall2 = _ffn_fwd(xs, modtab[:, 0:3], norm1_w, wall1, 0, tm=tmf, n_tiles=(r_lat + nb * nc) // tmf,
                                     tpe=s // tmf, n_lat=r_lat // tmf, name="ffn1_fwd", gather=wpack2)

    o0 = 3 * fsh
    wint = wall2[:, o0:o0 + 180].reshape(IN_COLS, D)
    z = lambda n: jnp.zeros((n, D), BF16)
    wint = jnp.concatenate([wint[0:128], wint[160:416], wint[416:928], wint[928:1440],
                            z(DN), wint[128:160], z(LANE - DH)], axis=0)
    wout = wall2[:, o0 + 192:o0 + 320].reshape(D, D)
    wout = jnp.concatenate([_head_pad(wout[:H * DV].reshape(H, DV, D), DV), wout[H * DV:]], axis=0)
    wuq = _head_pad(wall2[:, o0 + 320:o0 + 344].reshape(H, DH, QL), DH)
    wukvt = wall2[:, o0 + 352:o0 + 368].reshape(H, DN + DV, KVL)
    wukv = jnp.concatenate([_head_pad(wukvt[:, :DN], DN), _head_pad(wukvt[:, DN:], DV)], axis=0)

    ckv, qp, u_raw, v_raw, kpe = _proj_fwd(x1, modtab[:, 3:5], norm2_w, wint, tm=tm, n_tiles=n_all, tpe=tpe)
    q = _q_prep_fwd(qp, q_a_norm_w, wuq, wq, cos, sin, cs, tm=tm, n_lat=n_lat, tpe=tpe)
    k, v = _kv_prep_fwd(ckv, kpe, kv_a_norm_w, wukv, wk, cos_k, sin_k, cs, tm=tm, n_tiles=n_all, tpe=tpe, n_lat=n_lat)
    attn, lse = _attn_fwd(q, k, v, nb=nb, s=s, nc=nc, tq=tm, ck=512)
    x2, mix = _mix_fwd(u_raw, v_raw, attn, x1, modtab[:nb, 5:6], wv, ws16, bias, wout, cs,
                       tm=tm, n_lat=n_lat, tpe=tpe)
    dy, a2, b2, o2, lsum = _ffn_fwd((x2,), modtab[:nb, 6:9], norm3_w, wall2, 0, tm=tmf, n_tiles=r_lat // tmf,
                                    tpe=s // tmf, n_lat=r_lat // tmf, name="ffn2_fwd",
                                    target=loss_target.reshape(r_lat, D))
    loss = lax.psum(lsum[0, 0] * (0.5 / D), ("x", "y", "c"))

    tr = 2 * tm if n_lat % 2 == 0 and n_all % 2 == 0 else tm
    dx2, da2, db2, g2, do2, h2, dmod678, dnorm3 = _ffn_bwd_dx(
        dy, (x2,), a2, b2, o2, modtab[:nb, 6:9], norm3_w, wall2, 0,
        tm=tm, n_tiles=n_lat, tpe=tpe, n_lat=n_lat, name="ffn2_bwd_dx")
    g_ffn2 = _ffn_bwd_dw(h2, do2, da2, db2, g2, tr=tr, name="ffn2_bwd_dw")
    part_ffn2 = _add_sibling(g_ffn2, _scatter_sibling([g_ffn2], "scatter_sibling_ffn2")[0], 176, "add_sibling_ffn2")

    dattn, du, dv, dgate5, dwout, dws, dbs, dwv = _mix_bwd(
        dx2, mix, u_raw, v_raw, attn, modtab[:nb, 5:6], wv, ws16, wst16, bias, wout, cs, tm=tm, n_lat=n_lat, tpe=tpe)
    tq = 2 * tm if s % (2 * tm) == 0 else tm
    dq, dk_l, dk_c, dv_l, dv_c, recv_ffn2 = _attn_bwd(q, k, v, attn, dattn, lse, part_ffn2,
                                                      nb=nb, s=s, nc=nc, tq=tq, ck=1024)
    dqp, dwuq, dqa, dwq = _q_prep_bwd(dq, qp, q_a_norm_w, wuq, wq, cos, sin, cs, tm=tm, n_lat=n_lat, tpe=tpe)
    dckv, dkpe, dwukv, dkva, dwk = _kv_prep_bwd((dk_l, dk_c), (dv_l, dv_c), ckv, kpe, kv_a_norm_w, wukv, wk,
                                                cos_k, sin_k, cs, tm=tm, n_tiles=n_all, tpe=tpe, n_lat=n_lat)
    dx1, dwin, dmod34, dnorm2 = _proj_bwd(dckv, dkpe, dqp, du, dv, dx2, x1, modtab[:, 3:5], norm2_w, wint,
                                          tm=tm, n_tiles=n_all, tpe=tpe, n_lat=n_lat)
    dx0, da1, db1, g1, do1, h1, dmod012, dnorm1 = _ffn_bwd_dx(
        dx1, xs, a1, b1, o1, modtab[:, 0:3], norm1_w, wall1, 0,
        tm=tm, n_tiles=n_all, tpe=tpe, n_lat=n_lat, name="ffn1_bwd_dx")
    g_ffn1 = _ffn_bwd_dw(h1, do1, da1, db1, g1, tr=tr, name="ffn1_bwd_dw")
    grad_x = dx0.reshape(nb, s, D)

    zrow = jnp.zeros((1, D), F32)
    g_lat = jnp.concatenate([dmod012[:nb, 0], dmod012[:nb, 1], dmod012[:nb, 2], dmod34[:nb, 0], dmod34[:nb, 1],
                             dgate5[:, 0], dmod678[:, 0], dmod678[:, 1], dmod678[:, 2]], axis=1)
    g_ctx = jnp.concatenate([dmod012[nb:, 0], dmod012[nb:, 1], dmod012[nb:, 2], dmod34[nb:, 0], dmod34[nb:, 1],
                             zrow, zrow, zrow, zrow], axis=1)
    g_loc = jnp.concatenate([g_lat, g_ctx, jnp.zeros((7 - nb, NMOD * D), F32)], axis=0)
    g_all = _all_gather(g_loc, "gather_gmod").reshape(NDEV * 8, NMOD * D)
    g_cols = lax.dynamic_slice_in_dim(g_all, me * ncol, ncol, axis=1)
    g_w_ada, pc_ctx, g_b_ada = _ada_bwd(a_raw, c_ctx.reshape(D, 1), g_all, g_cols, w_ada[0], nb)

    def blocks(a):
        return a.reshape(NDEV, a.shape[0] // NDEV, D)

    dwin_o = jnp.concatenate([dwin[0:128], dwin[KPE_LO:KPE_LO + DR], dwin[128:384], dwin[384:896], dwin[896:1408]],
                             axis=0)
    dwout_o = jnp.concatenate([dwout[:HP].reshape(H, LANE, D)[:, :DV].reshape(H * DV, D), dwout[HP:]], axis=0)
    dwuq_o = dwuq.reshape(H, LANE, QL)[:, :DH]
    dwukv_o = jnp.concatenate([dwukv[:HP].reshape(H, LANE, KVL)[:, :DN], dwukv[HP:].reshape(H, LANE, KVL)[:, :DV]],
                              axis=1)
    gmisc = jnp.concatenate([
        blocks(dwin_o).astype(BF16), jnp.zeros((NDEV, 12, D), BF16),
        blocks(dwout_o).astype(BF16),
        dwuq_o.reshape(NDEV, 24, D).astype(BF16), jnp.zeros((NDEV, 8, D), BF16),
        dwukv_o.reshape(NDEV, 16, D).astype(BF16)], axis=1)
    gots = _scatter_sibling([g_ffn1, gmisc], "scatter_sibling")
    parts = [_add_sibling(g_ffn1, gots[0], 176, "add_sibling_ffn1"), _add_sibling(gmisc, gots[1], 368, "add_sibling_misc")]
    recv = _scatter_chips(parts, "scatter_chips")
    gsum1 = _sum_chips(parts[0], recv[0], 176, "sum_grads_ffn1")
    gsum2 = _sum_chips(part_ffn2, recv_ffn2, 176, "sum_grads_ffn2")
    msum = _sum_chips(parts[1], recv[1], 368, "sum_grads_misc")

    g_big = {
        "ffn1_w1": gsum1[0:fsh].T, "ffn1_w3": gsum1[fsh:2 * fsh].T, "ffn1_w2": gsum1[2 * fsh:3 * fsh],
        "ffn2_w1": gsum2[0:fsh].T, "ffn2_w3": gsum2[fsh:2 * fsh].T, "ffn2_w2": gsum2[2 * fsh:3 * fsh],
        "w_in": msum[0:180].T, "w_out": msum[192:320],
        "w_uq": msum[320:344].reshape(DH, QL).T, "w_ukv": msum[352:368].reshape(DN + DV, KVL).T,
        "w_ada": g_w_ada,
    }

    def prow(a):
        a = a.reshape(1, -1)
        return jnp.concatenate([a, jnp.zeros((1, D - a.shape[1]), F32)], axis=1)

    g_qn = dwq.reshape(H, LANE)[:, :DH].sum(0)
    g_kn = dwk.reshape(H, LANE)[:, :DH].sum(0)
    spack = jnp.concatenate([
        dnorm1, dnorm2, dnorm3, prow(dqa), prow(dkva), prow(g_qn), prow(g_kn), prow(dwv),
        prow(dbs[:, :G].T), prow(pc_ctx), jnp.zeros((6, D), F32), dws.reshape(CH, D)], axis=0)
    ssum = _sum_slots(_all_gather(spack, "gather_small"), 144, "sum_small")

    big_in = {
        "w_ada": (w_ada, m_w_ada, v_w_ada), "ffn1_w1": (ffn1_w1, m_ffn1_w1, v_ffn1_w1),
        "ffn1_w3": (ffn1_w3, m_ffn1_w3, v_ffn1_w3), "ffn1_w2": (ffn1_w2, m_ffn1_w2, v_ffn1_w2),
        "w_in": (w_in, m_w_in, v_w_in), "w_uq": (w_uq, m_w_uq, v_w_uq), "w_ukv": (w_ukv, m_w_ukv, v_w_ukv),
        "w_out": (w_out, m_w_out, v_w_out), "ffn2_w1": (ffn2_w1, m_ffn2_w1, v_ffn2_w1),
        "ffn2_w3": (ffn2_w3, m_ffn2_w3, v_ffn2_w3), "ffn2_w2": (ffn2_w2, m_ffn2_w2, v_ffn2_w2),
    }
    res = {}
    for nm, (w, m, v_) in big_in.items():
        g = g_big[nm]
        d_, m_, v2_ = _adamw(w[0], g, m[0], v_[0], "adamw_" + nm)
        res[nm] = tuple(a[None] for a in (g, d_, m_, v2_))

    small_in = [
        ("c_ctx", c_ctx, m_c_ctx, v_c_ctx, ssum[9:10], (1, D)),
        ("b_ada", b_ada, m_b_ada, v_b_ada, g_b_ada, (1, NMOD * D)),
        ("norm1_w", norm1_w, m_norm1_w, v_norm1_w, ssum[0:1], (1, D)),
        ("norm2_w", norm2_w, m_norm2_w, v_norm2_w, ssum[1:2], (1, D)),
        ("norm3_w", norm3_w, m_norm3_w, v_norm3_w, ssum[2:3], (1, D)),
        ("q_a_norm_w", q_a_norm_w, m_q_a_norm_w, v_q_a_norm_w, ssum[3:4, :QL], (1, QL)),
        ("kv_a_norm_w", kv_a_norm_w, m_kv_a_norm_w, v_kv_a_norm_w, ssum[4:5, :KVL], (1, KVL)),
        ("q_norm_w", q_norm_w, m_q_norm_w, v_q_norm_w, ssum[5:6, :DH], (1, DH)),
        ("k_norm_w", k_norm_w, m_k_norm_w, v_k_norm_w, ssum[6:7, :DH], (1, DH)),
        ("v_norm_w", v_norm_w, m_v_norm_w, v_v_norm_w, ssum[7:8, :G * GD], (G, GD)),
        ("b_s", b_s, m_b_s, v_b_s, ssum[8:9], (G, CH)),
        ("w_s", w_s, m_w_s, v_w_s, ssum[16:144], (G * CH, CH)),
    ]
    small_out = _adamw_small(
        [(w.reshape(sh), g.reshape(sh), m.reshape(sh), v_.reshape(sh)) for _, w, m, v_, g, sh in small_in])
    for (nm, w, *_), outs in zip(small_in, small_out):
        res[nm] = tuple(a.reshape(w.shape) for a in outs)

    order = ["c_ctx", "w_ada", "b_ada", "norm1_w", "ffn1_w1", "ffn1_w3", "ffn1_w2", "norm2_w", "w_in", "q_a_norm_w",
             "w_uq", "kv_a_norm_w", "w_ukv", "q_norm_w", "k_norm_w", "v_norm_w", "w_s", "b_s", "w_out", "norm3_w",
             "ffn2_w1", "ffn2_w3", "ffn2_w2"]
    return (loss, grad_x, *[res[n][0] for n in order], *[res[n][1] for n in order],
            *[res[n][2] for n in order], *[res[n][3] for n in order])
```

```python
import numpy as np
import jax
import jax.numpy as jnp
from jax import lax
from jax.experimental import pallas as pl
from jax.experimental.pallas import tpu as pltpu

F32 = jnp.float32
BF16 = jnp.bfloat16

D = 1024
FF = 2816
FC = 256
H = 8
DN, DR, DV = 64, 32, 64
DH = DN + DR
QL, KVL = 256, 128
G, GD, CH = 8, 64, 128
NMOD = 9
EPS = 1e-6
GRID_W = 64
ROPE_BASE = 10000.0
NDEV = 8
LANE = 128
HP = H * LANE
IN_COLS = 1440
WIN_ROWS = 1536
KPE_LO = 1408 + DN
NFFN_W = 6
MIB = 1 << 20

ADAM_LR, ADAM_B1, ADAM_B2, ADAM_EPS, ADAM_WD, ADAM_STEP = 0.001, 0.9, 0.999, 1e-08, 0.01, 10

MESH = pl.DeviceIdType.MESH
ANY = pl.BlockSpec(memory_space=pl.ANY)
VMEM = pl.BlockSpec(memory_space=pltpu.VMEM)


def _mm(a, b):
    return jnp.dot(a, b, preferred_element_type=F32)


def _mm_nt(a, b):
    return lax.dot_general(a, b, (((1,), (1,)), ((), ())), preferred_element_type=F32)


def _mm_tn(a, b):
    return lax.dot_general(a, b, (((0,), (0,)), ((), ())), preferred_element_type=F32)


def _dot_hl(x, m):
    hi = x.astype(BF16)
    lo = (x - hi.astype(F32)).astype(BF16)
    return _mm(hi, m) + _mm(lo, m)


def _sigmoid(a):
    return 1.0 / (1.0 + jnp.exp(-a))


_G0 = 0.7978845608028654
_G1 = 0.044715


def _gelu(x):
    return 0.5 * x * (1.0 + jnp.tanh(_G0 * (x + _G1 * (x * x * x))))


def _gelu_grad(x):
    th = jnp.tanh(_G0 * (x + _G1 * (x * x * x)))
    return 0.5 * (1.0 + th) + 0.5 * x * (1.0 - th * th) * (_G0 * (1.0 + 3.0 * _G1 * x * x))


def _rowsum(y):
    return jnp.sum(y, axis=0, keepdims=True)


def _rms(x):
    return lax.rsqrt(jnp.mean(x * x, axis=-1, keepdims=True) + EPS)


def _pcall(body, *, name, out_shape, in_specs, out_specs, grid=None, scratch=(), vmem_mb=32, aliases=None):
    kw = {}
    if grid is not None:
        kw["grid"] = grid
        sem = ("arbitrary",) * len(grid)
    else:
        sem = None
    if aliases:
        kw["input_output_aliases"] = aliases
    return pl.pallas_call(
        body, name=name, out_shape=out_shape, in_specs=in_specs, out_specs=out_specs,
        scratch_shapes=list(scratch),
        compiler_params=pltpu.CompilerParams(dimension_semantics=sem, vmem_limit_bytes=vmem_mb * MIB),
        **kw)


def _const(shape):
    nd = len(shape)
    return pl.BlockSpec(shape, lambda *_: (0,) * nd)


def _sds(shape, dt):
    return jax.ShapeDtypeStruct(shape, dt)


def _consts():
    seg_h = np.zeros((HP, LANE), np.float32)
    seg_h[np.arange(HP), np.arange(HP) // LANE] = 1.0
    seg_g = np.zeros((G * GD, LANE), np.float32)
    seg_g[np.arange(G * GD), np.arange(G * GD) // GD] = 1.0
    rot = np.zeros((LANE, LANE), np.float32)
    for base in (DN, DN + 16):
        for j in range(8):
            rot[base + j + 8, base + j] = -1.0
            rot[base + j, base + j + 8] = 1.0
    c = dict(seg_h=seg_h, seg_ht=seg_h.T, seg_g=seg_g, seg_gt=seg_g.T, rot=rot, rot_t=rot.T)
    return {k: jnp.asarray(v, BF16) for k, v in c.items()}


_GATHER_SEMS = [pltpu.SemaphoreType.DMA((7,)), pltpu.SemaphoreType.DMA((7,)), pltpu.SemaphoreType.DMA(())]


def _gather_phases(x_ref, out_ref, send_sems, recv_sems, local_sem):
    mx, my, mc = lax.axis_index("x"), lax.axis_index("y"), lax.axis_index("c")
    me, sibling = (mx, my, mc), (mx, my, 1 - mc)
    chips = [(1 - mx, my), (mx, 1 - my), (1 - mx, 1 - my)]

    def blk(px, py, pc):
        return out_ref.at[4 * px + 2 * py + pc]

    def copy(k, block, to, src=None):
        return pltpu.make_async_remote_copy(
            src_ref=blk(*block) if src is None else src, dst_ref=blk(*block),
            send_sem=send_sems.at[k], recv_sem=recv_sems.at[k], device_id=to, device_id_type=MESH)

    mine = pltpu.make_async_copy(x_ref, blk(*me), local_sem)
    first = [copy(0, me, sibling, src=x_ref)]
    first += [copy(1 + j, me, (*chip, mc), src=x_ref) for j, chip in enumerate(chips)]
    passed = [copy(4 + j, (*chip, mc), sibling) for j, chip in enumerate(chips)]

    def start():
        mine.start()
        for cp in first:
            cp.start()

    def forward():
        for j, chip in enumerate(chips):
            copy(1 + j, (*chip, mc), me).wait_recv()
            passed[j].start()

    def finish():
        copy(0, sibling, me).wait_recv()
        for j, chip in enumerate(chips):
            copy(4 + j, (*chip, 1 - mc), me).wait_recv()
        for cp in first + passed:
            cp.wait_send()
        mine.wait()

    return start, forward, finish


def _all_gather(x, name):
    r, c = x.shape

    def body(x_ref, out_ref, send_sems, recv_sems, local_sem):
        start, forward, finish = _gather_phases(x_ref, out_ref, send_sems, recv_sems, local_sem)
        start()
        forward()
        finish()

    return pl.pallas_call(
        body, name=name, out_shape=_sds((NDEV, r, c), x.dtype), in_specs=[ANY], out_specs=ANY,
        scratch_shapes=list(_GATHER_SEMS),
    )(x)


def _chip_sends(p_ref, out_ref, send_sems, recv_sems):
    mx, my, mc = lax.axis_index("x"), lax.axis_index("y"), lax.axis_index("c")
    peers = [(1 - mx, my), (mx, 1 - my), (1 - mx, 1 - my)]
    return [pltpu.make_async_remote_copy(
        src_ref=p_ref.at[2 * px + py], dst_ref=out_ref.at[j], send_sem=send_sems.at[j], recv_sem=recv_sems.at[j],
        device_id=(px, py, mc), device_id_type=MESH) for j, (px, py) in enumerate(peers)]


def _with_gather(copies_of, n, shapes, sems, gather, name, args):
    ns = len(sems)

    def body(*refs):
        ng = 1 if gather is not None else 0
        ins, outs = refs[:n], refs[n + ng:2 * n + ng]
        copies = copies_of(ins, outs, refs[2 * n + 2 * ng:2 * n + 2 * ng + ns])
        if ng:
            start, forward, finish = _gather_phases(refs[n], refs[2 * n + 1], *refs[2 * n + 2 + ns:])
            start()
        for cp in copies:
            cp.start()
        if ng:
            forward()
        for cp in copies:
            cp.wait_recv()
        for cp in copies:
            cp.wait_send()
        if ng:
            finish()

    in_specs, out_shape, scratch = [ANY] * n, list(shapes), list(sems)
    if gather is not None:
        in_specs.append(ANY)
        args = list(args) + [gather]
        out_shape.append(_sds((NDEV,) + gather.shape, gather.dtype))
        scratch += _GATHER_SEMS
    return pl.pallas_call(body, name=name, out_shape=tuple(out_shape), in_specs=in_specs,
                          out_specs=(ANY,) * len(out_shape), scratch_shapes=scratch)(*args)


def _scatter_sibling(xs, name, gather=None):
    n = len(xs)

    def copies_of(x_refs, got_refs, sems):
        send_sems, recv_sems = sems
        mx, my, mc = lax.axis_index("x"), lax.axis_index("y"), lax.axis_index("c")
        return [pltpu.make_async_remote_copy(
            src_ref=x_refs[i].at[2 * j + 1 - mc], dst_ref=got_refs[i].at[j],
            send_sem=send_sems.at[4 * i + j], recv_sem=recv_sems.at[4 * i + j],
            device_id=(mx, my, 1 - mc), device_id_type=MESH) for i in range(n) for j in range(4)]

    shapes = tuple(_sds((4,) + x.shape[1:], x.dtype) for x in xs)
    return _with_gather(copies_of, n, shapes, [pltpu.SemaphoreType.DMA((4 * n,))] * 2, gather, name, xs)


def _scatter_chips(ps, name, gather=None):
    n = len(ps)

    def copies_of(p_refs, out_refs, sems):
        sends = []
        for i in range(n):
            sends += _chip_sends(p_refs[i], out_refs[i], sems[2 * i], sems[2 * i + 1])
        return sends

    shapes = tuple(_sds((3,) + p.shape[1:], p.dtype) for p in ps)
    return _with_gather(copies_of, n, shapes, [pltpu.SemaphoreType.DMA((3,))] * (2 * n), gather, name, ps)


def _add_sibling(x, got, tr, name):
    _, r, c = x.shape

    def body(x_ref, g_ref, o_ref):
        mc = lax.axis_index("c")
        for j in range(4):
            mine = jnp.where(mc == 0, x_ref[2 * j].astype(F32), x_ref[2 * j + 1].astype(F32))
            o_ref[j] = (mine + g_ref[j].astype(F32)).astype(o_ref.dtype)

    return _pcall(body, name=name, grid=(r // tr,), out_shape=_sds(got.shape, got.dtype),
                  in_specs=[pl.BlockSpec((NDEV, tr, c), lambda t: (0, t, 0)), pl.BlockSpec((4, tr, c), lambda t: (0, t, 0))],
                  out_specs=pl.BlockSpec((4, tr, c), lambda t: (0, t, 0)))(x, got)


def _sum_chips(part, recv, tr, name):
    _, r, c = part.shape

    def body(p_ref, r_ref, o_ref):
        slot = 2 * lax.axis_index("x") + lax.axis_index("y")
        acc = p_ref[0].astype(F32)
        for j in range(1, 4):
            acc = jnp.where(slot == j, p_ref[j].astype(F32), acc)
        for j in range(3):
            acc = acc + r_ref[j].astype(F32)
        o_ref[...] = acc

    return _pcall(body, name=name, grid=(r // tr,), out_shape=_sds((r, c), F32),
                  in_specs=[pl.BlockSpec((4, tr, c), lambda t: (0, t, 0)), pl.BlockSpec((3, tr, c), lambda t: (0, t, 0))],
                  out_specs=pl.BlockSpec((tr, c), lambda t: (t, 0)))(part, recv)


def _sum_slots(x, tr, name):
    n, r, c = x.shape

    def body(x_ref, o_ref):
        acc = x_ref[0].astype(F32)
        for s in range(1, n):
            acc = acc + x_ref[s].astype(F32)
        o_ref[...] = acc

    return _pcall(body, name=name, grid=(r // tr,), out_shape=_sds((r, c), F32),
                  in_specs=[pl.BlockSpec((n, tr, c), lambda t: (0, t, 0))],
                  out_specs=pl.BlockSpec((tr, c), lambda t: (t, 0)))(x)


def _ada_fwd(a_raw, w_loc, b_loc):
    ncol = w_loc.shape[1]

    def body(a_ref, w_ref, b_ref, o_ref):
        a = a_ref[...]
        act = (a * _sigmoid(a)).astype(BF16)
        o_ref[...] = _mm(act, w_ref[...].astype(BF16)) + b_ref[...]

    return _pcall(body, name="ada_fwd", out_shape=_sds((a_raw.shape[0], ncol), F32),
                  in_specs=[VMEM] * 3, out_specs=VMEM)(a_raw, w_loc, b_loc)


def _ada_front(a_loc, w_loc, b_loc, wpack):
    ncol = w_loc.shape[1]
    nrow = NDEV * a_loc.shape[0]

    def body(a_ref, w_ref, b_ref, wp_ref, araw_ref, mloc_ref, mall_ref, wall_ref,
             a_vm, w_vm, m_vm, lsem, *sems):
        a_start, a_forward, a_finish = _gather_phases(a_ref, araw_ref, *sems[0:3])
        m_start, m_forward, m_finish = _gather_phases(mloc_ref, mall_ref, *sems[3:6])
        w_start, w_forward, w_finish = _gather_phases(wp_ref, wall_ref, *sems[6:9])
        w_in = pltpu.make_async_copy(w_ref, w_vm, lsem.at[0])
        w_in.start()
        a_start()
        w_start()
        a_forward()
        a_finish()
        a_in = pltpu.make_async_copy(araw_ref, a_vm, lsem.at[1])
        a_in.start()
        a_in.wait()
        w_in.wait()
        a = a_vm[...].reshape(nrow, D)
        act = (a * _sigmoid(a)).astype(BF16)
        m_vm[...] = _mm(act, w_vm[...].astype(BF16)) + b_ref[...]
        m_out = pltpu.make_async_copy(m_vm, mloc_ref, lsem.at[2])
        m_out.start()
        m_out.wait()
        m_start()
        m_forward()
        m_finish()
        w_forward()
        w_finish()

    return pl.pallas_call(
        body, name="ada_front",
        out_shape=(_sds((NDEV,) + a_loc.shape, F32), _sds((nrow, ncol), F32), _sds((NDEV, nrow, ncol), F32),
                   _sds((NDEV,) + wpack.shape, wpack.dtype)),
        in_specs=[ANY, ANY, VMEM, ANY], out_specs=(ANY, ANY, ANY, ANY),
        scratch_shapes=[pltpu.VMEM((NDEV,) + a_loc.shape, F32), pltpu.VMEM(w_loc.shape, F32),
                        pltpu.VMEM((nrow, ncol), F32), pltpu.SemaphoreType.DMA((3,))] + _GATHER_SEMS * 3,
        compiler_params=pltpu.CompilerParams(vmem_limit_bytes=32 * MIB),
    )(a_loc, w_loc, b_loc, wpack)


def _ada_bwd(a_raw, cctx_col, g_all, g_cols, w_loc, nb):
    nrow = a_raw.shape[0]
    ncol = w_loc.shape[1]

    def body(a_ref, cc_ref, gall_ref, g_ref, w_ref, dw_ref, pc_ref, gb_ref):
        a = a_ref[...]
        rowid = lax.broadcasted_iota(jnp.int32, (nrow, 1), 0) % 8
        act = jnp.where(rowid < nb, a * _sigmoid(a), 0.0).astype(BF16)
        g = g_ref[...]
        gc = _rowsum(jnp.where(rowid == nb, g, 0.0))
        cc = cc_ref[...]
        dw_ref[...] = _mm_tn(act, g.astype(BF16)) + (cc * _sigmoid(cc)) * gc
        pc_ref[...] = jnp.sum(w_ref[...] * gc, axis=1, keepdims=True)
        gb_ref[...] = _rowsum(gall_ref[...])

    return _pcall(body, name="ada_bwd",
                  out_shape=(_sds((D, ncol), F32), _sds((D, 1), F32), _sds((1, g_all.shape[1]), F32)),
                  in_specs=[VMEM] * 5, out_specs=(VMEM,) * 3, vmem_mb=48)(a_raw, cctx_col, g_all, g_cols, w_loc)


def _mod_spec(k, tpe, nrows):
    return pl.BlockSpec((1, k, D), lambda t: (jnp.minimum(t // tpe, nrows - 1), 0, 0))


def _load_ffn_weights(wall_ref, first, bufs, sems):
    fsh = FF // NDEV
    cps = []
    for j, buf in enumerate(bufs):
        for d in range(NDEV):
            cps.append(pltpu.make_async_copy(wall_ref.at[d, pl.ds((first + j) * fsh, fsh)],
                                             buf.at[pl.ds(d * fsh, fsh)], sems.at[j * NDEV + d]))
    for cp in cps:
        cp.start()
    for cp in cps:
        cp.wait()


def _token_specs(xs, tm, n_lat):
    specs = [pl.BlockSpec((tm, D), lambda t: (jnp.minimum(t, n_lat - 1), 0))]
    if len(xs) == 2:
        specs.append(pl.BlockSpec((tm, D), lambda t: (jnp.maximum(t - n_lat, 0), 0)))
    return specs


def _ffn_fwd(xs, mod3, norm_w, wall, first, *, tm, n_tiles, tpe, n_lat, name, target=None, gather=None):
    nrows = mod3.shape[0]
    r = n_tiles * tm
    nx = len(xs)
    with_loss = target is not None
    with_gather = gather is not None
    fwd_step = max(2 * n_tiles // 3, 1)

    def body(*refs):
        x_refs = refs[:nx]
        pos = nx
        if with_loss:
            tgt_ref = refs[pos]
            pos += 1
        mod_ref, nw_ref, wall_ref = refs[pos:pos + 3]
        pos += 3
        if with_gather:
            gin_ref = refs[pos]
            pos += 1
        xo_ref, a_ref, b_ref, o_ref = refs[pos:pos + 4]
        pos += 4
        if with_loss:
            ls_ref = refs[pos]
            pos += 1
        if with_gather:
            gout_ref = refs[pos]
            pos += 1
        w1_ref, w3_ref, w2_ref, wsem, acc_ref = refs[pos:pos + 5]
        t = pl.program_id(0)
        if with_gather:
            g_start, g_forward, g_finish = _gather_phases(gin_ref, gout_ref, *refs[pos + 5:])

        @pl.when(t == 0)
        def _():
            if with_gather:
                g_start()
            _load_ffn_weights(wall_ref, first, (w1_ref, w3_ref, w2_ref), wsem)
            if with_loss:
                ls_ref[...] = jnp.zeros_like(ls_ref)

        if with_gather:
            @pl.when(t == fwd_step)
            def _():
                g_forward()

            @pl.when(t == n_tiles - 1)
            def _():
                g_finish()

        x = x_refs[0][...]
        if nx == 2:
            x = jnp.where(t < n_lat, x, x_refs[1][...])
        n = x * _rms(x) * nw_ref[...]
        shift, scale, gate = mod_ref[0, 0:1, :], mod_ref[0, 1:2, :], mod_ref[0, 2:3, :]
        h = (n * (1.0 + scale) + shift).astype(BF16)
        nch = FF // FC
        o = None
        for lo_c, hi_c in ((0, nch // 2), (nch // 2, nch)):
            for j in range(lo_c, hi_c):
                sl = slice(j * FC, (j + 1) * FC)
                a = _mm_nt(h, w1_ref[sl, :])
                b = _mm_nt(h, w3_ref[sl, :])
                a_ref[:, sl] = a.astype(BF16)
                b_ref[:, sl] = b.astype(BF16)
                acc_ref[:, sl] = (a * _sigmoid(a) * b).astype(BF16)
            gs = slice(lo_c * FC, hi_c * FC)
            part = _mm(acc_ref[:, gs], w2_ref[gs, :])
            o = part if o is None else o + part
        o_ref[...] = o.astype(BF16)
        out = x + (0.5 * gate) * o
        if with_loss:
            d = out - tgt_ref[...]
            xo_ref[...] = d * (1.0 / D)
            ls_ref[...] += jnp.sum(d * d)
        else:
            xo_ref[...] = out

    row = lambda cols: pl.BlockSpec((tm, cols), lambda t: (t, 0))
    in_specs = _token_specs(xs, tm, n_lat) + ([row(D)] if with_loss else []) + [
        _mod_spec(3, tpe, nrows), _const((1, D)), ANY]
    out_shape = [_sds((r, D), F32), _sds((r, FF), BF16), _sds((r, FF), BF16), _sds((r, D), BF16)]
    out_specs = [row(D), row(FF), row(FF), row(D)]
    scratch = [pltpu.VMEM((FF, D), BF16)] * 3 + [pltpu.SemaphoreType.DMA((3 * NDEV,)), pltpu.VMEM((tm, FF), BF16)]
    if with_loss:
        out_shape.append(_sds((8, LANE), F32))
        out_specs.append(_const((8, LANE)))
    args = list(xs) + ([target] if with_loss else []) + [mod3, norm_w, wall]
    if with_gather:
        assert n_tiles >= 2
        in_specs.append(ANY)
        args.append(gather)
        out_shape.append(_sds((NDEV,) + gather.shape, gather.dtype))
        out_specs.append(ANY)
        scratch += _GATHER_SEMS
    return _pcall(
        body, name=name, grid=(n_tiles,), out_shape=tuple(out_shape), in_specs=in_specs, out_specs=tuple(out_specs),
        scratch=scratch, vmem_mb=56)(*args)


def _ffn_bwd_dx(dout, xs, a, b, o, mod3, norm_w, wall, first, *, tm, n_tiles, tpe, n_lat, name):
    nrows = mod3.shape[0]
    r = n_tiles * tm
    nx = len(xs)

    def body(*refs):
        dout_ref = refs[0]
        x_refs = refs[1:1 + nx]
        (a_ref, b_ref, o_ref, mod_ref, nw_ref, wall_ref,
         dx_ref, da_ref, db_ref, g_ref, do_ref, h_ref, dmod_ref, dnw_ref,
         w1_ref, w3_ref, w2_ref, wsem) = refs[1 + nx:]
        t = pl.program_id(0)

        @pl.when(t == 0)
        def _():
            _load_ffn_weights(wall_ref, first, (w1_ref, w3_ref, w2_ref), wsem)
            dnw_ref[...] = jnp.zeros_like(dnw_ref)

        x = x_refs[0][...]
        if nx == 2:
            x = jnp.where(t < n_lat, x, x_refs[1][...])
        dout = dout_ref[...]
        rr = _rms(x)
        xh = x * rr
        nw = nw_ref[...]
        n = xh * nw
        shift, scale, gate = mod_ref[0, 0:1, :], mod_ref[0, 1:2, :], mod_ref[0, 2:3, :]
        h = (n * (1.0 + scale) + shift).astype(BF16)
        h_ref[...] = h
        d_o = ((0.5 * gate) * dout).astype(BF16)
        do_ref[...] = d_o
        dgate = _rowsum(0.5 * o_ref[...].astype(F32) * dout)
        nch = FF // FC
        groups = ((0, nch // 2), (nch // 2, nch))
        dh = None
        for lo_c, hi_c in groups:
            for j in range(lo_c, hi_c):
                sl = slice(j * FC, (j + 1) * FC)
                av = a_ref[:, sl].astype(F32)
                bv = b_ref[:, sl].astype(F32)
                dg = _mm_nt(d_o, w2_ref[sl, :])
                sig = _sigmoid(av)
                sa = av * sig
                g_ref[:, sl] = (sa * bv).astype(BF16)
                da_ref[:, sl] = (dg * bv * (sig * (1.0 + av * (1.0 - sig)))).astype(BF16)
                db_ref[:, sl] = (dg * sa).astype(BF16)
            gs = slice(lo_c * FC, hi_c * FC)
            part = _mm(da_ref[:, gs], w1_ref[gs, :]) + _mm(db_ref[:, gs], w3_ref[gs, :])
            dh = part if dh is None else dh + part
        dn = dh * (1.0 + scale)
        dxh = dn * nw

        @pl.when(t < n_lat)
        def _():
            dx_ref[...] = dout + rr * (dxh - xh * jnp.mean(dxh * xh, axis=-1, keepdims=True))

        first_visit = jnp.where(t < n_lat, t % tpe == 0, t == n_lat)

        @pl.when(first_visit)
        def _():
            dmod_ref[...] = jnp.zeros_like(dmod_ref)

        dmod_ref[0, 0:1, :] += _rowsum(dh)
        dmod_ref[0, 1:2, :] += _rowsum(dh * n)
        dmod_ref[0, 2:3, :] += dgate
        dnw_ref[...] += _rowsum(dn * xh)

    row = lambda cols: pl.BlockSpec((tm, cols), lambda t: (t, 0))
    lat = pl.BlockSpec((tm, D), lambda t: (jnp.minimum(t, n_lat - 1), 0))
    return _pcall(
        body, name=name, grid=(n_tiles,),
        out_shape=(_sds((n_lat * tm, D), F32), _sds((r, FF), BF16), _sds((r, FF), BF16), _sds((r, FF), BF16),
                   _sds((r, D), BF16), _sds((r, D), BF16), _sds((nrows, 3, D), F32), _sds((1, D), F32)),
        in_specs=[row(D)] + _token_specs(xs, tm, n_lat) + [row(FF), row(FF), row(D), _mod_spec(3, tpe, nrows),
                                                            _const((1, D)), ANY],
        out_specs=(lat, row(FF), row(FF), row(FF), row(D), row(D), _mod_spec(3, tpe, nrows), _const((1, D))),
        scratch=[pltpu.VMEM((FF, D), BF16)] * 3 + [pltpu.SemaphoreType.DMA((3 * NDEV,))],
        vmem_mb=60)(dout, *xs, a, b, o, mod3, norm_w, wall)


def _ffn_bwd_dw(h, d_o, da, db, g, *, tr, name):
    r = h.shape[0]
    fh = FF // 2
    fsh = FF // NDEV
    nk = r // tr

    def body(h_ref, do_ref, da_ref, db_ref, g_ref, out_ref, acc1, acc3, acc2):
        k = pl.program_id(1)

        @pl.when(k == 0)
        def _():
            acc1[...] = jnp.zeros_like(acc1)
            acc3[...] = jnp.zeros_like(acc3)
            acc2[...] = jnp.zeros_like(acc2)

        hv = h_ref[...]
        acc1[...] += _mm_tn(da_ref[...], hv)
        acc3[...] += _mm_tn(db_ref[...], hv)
        acc2[...] += _mm_tn(g_ref[...], do_ref[...])

        @pl.when(k == nk - 1)
        def _():
            for i, acc in enumerate((acc1, acc3, acc2)):
                out_ref[:, i * fsh:(i + 1) * fsh, :] = acc[...].reshape(NDEV // 2, fsh, D).astype(BF16)

    rowd = pl.BlockSpec((tr, D), lambda f, k: (k, 0))
    rowf = pl.BlockSpec((tr, fh), lambda f, k: (k, f))
    return _pcall(
        body, name=name, grid=(2, nk), out_shape=_sds((NDEV, 3 * fsh, D), BF16),
        in_specs=[rowd, rowd, rowf, rowf, rowf],
        out_specs=pl.BlockSpec((NDEV // 2, 3 * fsh, D), lambda f, k: (f, 0, 0)),
        scratch=[pltpu.VMEM((fh, D), F32)] * 3, vmem_mb=56)(h, d_o, da, db, g)


_PIECES = ((0, 128), (128, 384), (384, 896), (896, 1408), (1408, 1536))


def _proj_fwd(x1, mod2, norm_w, wint, *, tm, n_tiles, tpe, name="proj_fwd"):
    nrows = mod2.shape[0]
    r = n_tiles * tm

    def body(x_ref, mod_ref, nw_ref, w_ref, ckv_ref, q_ref, u_ref, v_ref, kpe_ref):
        x = x_ref[...]
        n = x * _rms(x) * nw_ref[...]
        h = (n * (1.0 + mod_ref[0, 1:2, :]) + mod_ref[0, 0:1, :]).astype(BF16)
        for (lo, hi), ref in zip(_PIECES, (ckv_ref, q_ref, u_ref, v_ref, kpe_ref)):
            ref[...] = _mm_nt(h, w_ref[lo:hi, :])

    row = lambda cols: pl.BlockSpec((tm, cols), lambda t: (t, 0))
    widths = [hi - lo for lo, hi in _PIECES]
    return _pcall(
        body, name=name, grid=(n_tiles,),
        out_shape=tuple(_sds((r, w), F32) for w in widths),
        in_specs=[row(D), _mod_spec(2, tpe, nrows), _const((1, D)), _const((WIN_ROWS, D))],
        out_specs=tuple(row(w) for w in widths), vmem_mb=40)(x1, mod2, norm_w, wint)


def _proj_bwd(dckv, dkpe, dq, du, dv, dx2, x1, mod2, norm_w, wint, *, tm, n_tiles, tpe, n_lat, name="proj_bwd"):
    nrows = mod2.shape[0]
    r = n_tiles * tm

    def body(dckv_ref, dkpe_ref, dq_ref, du_ref, dv_ref, dx2_ref, x_ref, mod_ref, nw_ref, w_ref,
             dx_ref, dw_ref, dmod_ref, dnw_ref, acc_ref):
        t = pl.program_id(0)
        is_lat = t < n_lat
        x = x_ref[...]
        rr = _rms(x)
        xh = x * rr
        nw = nw_ref[...]
        n = xh * nw
        scale = mod_ref[0, 1:2, :]
        h = (n * (1.0 + scale) + mod_ref[0, 0:1, :]).astype(BF16)

        @pl.when(t == 0)
        def _():
            dw_ref[...] = jnp.zeros_like(dw_ref)
            dnw_ref[...] = jnp.zeros_like(dnw_ref)

        dckv_v, dkpe_v = dckv_ref[...], dkpe_ref[...]
        acc_ref[...] = _mm(dckv_v, w_ref[0:128, :]) + _mm(dkpe_v, w_ref[1408:1536, :])
        dw_ref[0:128, :] += _mm_tn(dckv_v, h)
        dw_ref[1408:1536, :] += _mm_tn(dkpe_v, h)

        @pl.when(is_lat)
        def _():
            dq_v, du_v, dv_v = dq_ref[...], du_ref[...], dv_ref[...]
            acc_ref[...] += (_mm(dq_v, w_ref[128:384, :]) + _mm(du_v, w_ref[384:896, :])
                             + _mm(dv_v, w_ref[896:1408, :]))
            dw_ref[128:384, :] += _mm_tn(dq_v, h)
            dw_ref[384:896, :] += _mm_tn(du_v, h)
            dw_ref[896:1408, :] += _mm_tn(dv_v, h)

        dh = acc_ref[...]
        dn = dh * (1.0 + scale)
        dxh = dn * nw
        dx = rr * (dxh - xh * jnp.mean(dxh * xh, axis=-1, keepdims=True))
        dx_ref[...] = dx + jnp.where(is_lat, dx2_ref[...], 0.0)

        first = jnp.where(is_lat, t % tpe == 0, t == n_lat)

        @pl.when(first)
        def _():
            dmod_ref[...] = jnp.zeros_like(dmod_ref)

        dmod_ref[0, 0:1, :] += _rowsum(dh)
        dmod_ref[0, 1:2, :] += _rowsum(dh * n)
        dnw_ref[...] += _rowsum(dn * xh)

    row = lambda cols: pl.BlockSpec((tm, cols), lambda t: (t, 0))
    lat = lambda cols: pl.BlockSpec((tm, cols), lambda t: (jnp.minimum(t, n_lat - 1), 0))
    return _pcall(
        body, name=name, grid=(n_tiles,),
        out_shape=(_sds((r, D), F32), _sds((WIN_ROWS, D), F32), _sds((nrows, 2, D), F32), _sds((1, D), F32)),
        in_specs=[row(128), row(128), lat(256), lat(512), lat(512), lat(D), row(D), _mod_spec(2, tpe, nrows),
                  _const((1, D)), _const((WIN_ROWS, D))],
        out_specs=(row(D), _const((WIN_ROWS, D)), _mod_spec(2, tpe, nrows), _const((1, D))),
        scratch=[pltpu.VMEM((tm, D), F32)], vmem_mb=48)(dckv, dkpe, dq, du, dv, dx2, x1, mod2, norm_w, wint)


def _head_norm_rope(x, w_pad, cos, sin, seg, segt, rot):
    rh = lax.rsqrt(_dot_hl(x * x, seg) * (1.0 / DH) + EPS)
    rb = _dot_hl(rh, segt)
    y = x * rb
    t = y * w_pad
    out = []
    for h in range(H):
        th = t[:, h * LANE:(h + 1) * LANE]
        out.append(th * cos + _dot_hl(th, rot) * sin)
    return jnp.concatenate(out, axis=-1), y, rb


def _head_norm_rope_bwd(dout, y, rb, w_pad, cos, sin, seg, segt, rot_t):
    dt = []
    for h in range(H):
        dh = dout[:, h * LANE:(h + 1) * LANE]
        dt.append(dh * cos + _dot_hl(dh * sin, rot_t))
    dt = jnp.concatenate(dt, axis=-1)
    dw = _rowsum(dt * y)
    dy = dt * w_pad
    mean_h = _dot_hl(dy * y, seg) * (1.0 / DH)
    return rb * (dy - y * _dot_hl(mean_h, segt)), dw


def _q_prep_fwd(qp, qa_w, wuq, wq, cos, sin, cs, *, tm, n_lat, tpe):
    def body(qp_ref, qa_ref, wuq_ref, wq_ref, cos_ref, sin_ref, seg, segt, rot, q_ref):
        x = qp_ref[...]
        cq = (x * _rms(x) * qa_ref[...]).astype(BF16)
        q, _, _ = _head_norm_rope(_mm_nt(cq, wuq_ref[...]), wq_ref[...], cos_ref[...], sin_ref[...],
                                  seg[...], segt[...], rot[...])
        q_ref[...] = q.astype(BF16)

    row = lambda cols: pl.BlockSpec((tm, cols), lambda t: (t, 0))
    tab = pl.BlockSpec((tm, LANE), lambda t: (t % tpe, 0))
    return _pcall(
        body, name="q_prep_fwd", grid=(n_lat,), out_shape=_sds((n_lat * tm, HP), BF16),
        in_specs=[row(QL), _const((1, QL)), _const((HP, QL)), _const((1, HP)), tab, tab,
                  _const((HP, LANE)), _const((LANE, HP)), _const((LANE, LANE))],
        out_specs=row(HP))(qp, qa_w, wuq, wq, cos, sin, cs["seg_h"], cs["seg_ht"], cs["rot"])


def _q_prep_bwd(dq, qp, qa_w, wuq, wq, cos, sin, cs, *, tm, n_lat, tpe):
    def body(dq_ref, qp_ref, qa_ref, wuq_ref, wq_ref, cos_ref, sin_ref, seg, segt, rot, rot_t,
             dqp_ref, dwuq_ref, dqa_ref, dwq_ref):
        t = pl.program_id(0)
        x = qp_ref[...]
        ra = _rms(x)
        xh = x * ra
        qa = qa_ref[...]
        cq = (xh * qa).astype(BF16)
        wuq_v = wuq_ref[...]
        wq_v, cos_v, sin_v = wq_ref[...], cos_ref[...], sin_ref[...]
        _, y, rb = _head_norm_rope(_mm_nt(cq, wuq_v), wq_v, cos_v, sin_v, seg[...], segt[...], rot[...])
        dqraw, dwq = _head_norm_rope_bwd(dq_ref[...], y, rb, wq_v, cos_v, sin_v, seg[...], segt[...], rot_t[...])
        dqraw = dqraw.astype(BF16)
        dcq = _mm(dqraw, wuq_v)
        dxh = dcq * qa
        dqp_ref[...] = (ra * (dxh - xh * jnp.mean(dxh * xh, axis=-1, keepdims=True))).astype(BF16)

        @pl.when(t == 0)
        def _():
            dwuq_ref[...] = jnp.zeros_like(dwuq_ref)
            dqa_ref[...] = jnp.zeros_like(dqa_ref)
            dwq_ref[...] = jnp.zeros_like(dwq_ref)

        dwuq_ref[...] += _mm_tn(dqraw, cq)
        dqa_ref[...] += _rowsum(dcq * xh)
        dwq_ref[...] += dwq

    row = lambda cols: pl.BlockSpec((tm, cols), lambda t: (t, 0))
    tab = pl.BlockSpec((tm, LANE), lambda t: (t % tpe, 0))
    return _pcall(
        body, name="q_prep_bwd", grid=(n_lat,),
        out_shape=(_sds((n_lat * tm, QL), BF16), _sds((HP, QL), F32), _sds((1, QL), F32), _sds((1, HP), F32)),
        in_specs=[row(HP), row(QL), _const((1, QL)), _const((HP, QL)), _const((1, HP)), tab, tab,
                  _const((HP, LANE)), _const((LANE, HP)), _const((LANE, LANE)), _const((LANE, LANE))],
        out_specs=(row(QL), _const((HP, QL)), _const((1, QL)), _const((1, HP))), vmem_mb=40)(
            dq, qp, qa_w, wuq, wq, cos, sin, cs["seg_h"], cs["seg_ht"], cs["rot"], cs["rot_t"])


def _kv_tab_spec(tm, tpe, n_lat):
    return pl.BlockSpec((tm, LANE), lambda t: (jnp.where(t < n_lat, t % tpe, tpe), 0))


def _kv_prep_fwd(ckv, kpe, kva_w, wukv, wk, cosk, sink, cs, *, tm, n_tiles, tpe, n_lat):
    def body(ckv_ref, kpe_ref, kva_ref, wukv_ref, wk_ref, cos_ref, sin_ref, seg, segt, rot, k_ref, v_ref):
        x = ckv_ref[...]
        ckvn = (x * _rms(x) * kva_ref[...]).astype(BF16)
        kv = _mm_nt(ckvn, wukv_ref[...])
        kx = kv[:, :HP] + jnp.concatenate([kpe_ref[...]] * H, axis=-1)
        k, _, _ = _head_norm_rope(kx, wk_ref[...], cos_ref[...], sin_ref[...], seg[...], segt[...], rot[...])
        k_ref[...] = k.astype(BF16)
        v_ref[...] = kv[:, HP:].astype(BF16)

    row = lambda cols: pl.BlockSpec((tm, cols), lambda t: (t, 0))
    tab = _kv_tab_spec(tm, tpe, n_lat)
    r = n_tiles * tm
    return _pcall(
        body, name="kv_prep_fwd", grid=(n_tiles,), out_shape=(_sds((r, HP), BF16), _sds((r, HP), BF16)),
        in_specs=[row(KVL), row(LANE), _const((1, KVL)), _const((2 * HP, KVL)), _const((1, HP)), tab, tab,
                  _const((HP, LANE)), _const((LANE, HP)), _const((LANE, LANE))],
        out_specs=(row(HP), row(HP)), vmem_mb=40)(
            ckv, kpe, kva_w, wukv, wk, cosk, sink, cs["seg_h"], cs["seg_ht"], cs["rot"])


def _kv_prep_bwd(dks, dvs, ckv, kpe, kva_w, wukv, wk, cosk, sink, cs, *, tm, n_tiles, tpe, n_lat):
    def body(dkl_ref, dkc_ref, dvl_ref, dvc_ref, ckv_ref, kpe_ref, kva_ref, wukv_ref, wk_ref, cos_ref, sin_ref,
             seg, segt, rot, rot_t, dckv_ref, dkpe_ref, dwukv_ref, dkva_ref, dwk_ref):
        t = pl.program_id(0)
        is_lat = t < n_lat
        dk = jnp.where(is_lat, dkl_ref[...], dkc_ref[...])
        dv = jnp.where(is_lat, dvl_ref[...], dvc_ref[...])
        x = ckv_ref[...]
        ra = _rms(x)
        xh = x * ra
        kva = kva_ref[...]
        ckvn = (xh * kva).astype(BF16)
        wukv_v = wukv_ref[...]
        wk_v, cos_v, sin_v = wk_ref[...], cos_ref[...], sin_ref[...]
        kv = _mm_nt(ckvn, wukv_v)
        kx = kv[:, :HP] + jnp.concatenate([kpe_ref[...]] * H, axis=-1)
        _, y, rb = _head_norm_rope(kx, wk_v, cos_v, sin_v, seg[...], segt[...], rot[...])
        dkx, dwk = _head_norm_rope_bwd(dk, y, rb, wk_v, cos_v, sin_v, seg[...], segt[...], rot_t[...])
        dkpe = dkx[:, 0:LANE]
        for h in range(1, H):
            dkpe = dkpe + dkx[:, h * LANE:(h + 1) * LANE]
        lane = lax.broadcasted_iota(jnp.int32, (tm, LANE), 1)
        dkpe_ref[...] = jnp.where((lane >= DN) & (lane < DH), dkpe, 0.0).astype(BF16)
        dkv = jnp.concatenate([dkx, dv], axis=-1).astype(BF16)
        dckvn = _mm(dkv, wukv_v)
        dxh = dckvn * kva
        dckv_ref[...] = (ra * (dxh - xh * jnp.mean(dxh * xh, axis=-1, keepdims=True))).astype(BF16)

        @pl.when(t == 0)
        def _():
            dwukv_ref[...] = jnp.zeros_like(dwukv_ref)
            dkva_ref[...] = jnp.zeros_like(dkva_ref)
            dwk_ref[...] = jnp.zeros_like(dwk_ref)

        dwukv_ref[...] += _mm_tn(dkv, ckvn)
        dkva_ref[...] += _rowsum(dckvn * xh)
        dwk_ref[...] += dwk

    row = lambda cols: pl.BlockSpec((tm, cols), lambda t: (t, 0))
    lat = pl.BlockSpec((tm, HP), lambda t: (jnp.minimum(t, n_lat - 1), 0))
    ctx = pl.BlockSpec((tm, HP), lambda t: (jnp.maximum(t - n_lat, 0), 0))
    tab = _kv_tab_spec(tm, tpe, n_lat)
    r = n_tiles * tm
    return _pcall(
        body, name="kv_prep_bwd", grid=(n_tiles,),
        out_shape=(_sds((r, KVL), BF16), _sds((r, LANE), BF16), _sds((2 * HP, KVL), F32), _sds((1, KVL), F32),
                   _sds((1, HP), F32)),
        in_specs=[lat, ctx, lat, ctx, row(KVL), row(LANE), _const((1, KVL)), _const((2 * HP, KVL)), _const((1, HP)),
                  tab, tab, _const((HP, LANE)), _const((LANE, HP)), _const((LANE, LANE)), _const((LANE, LANE))],
        out_specs=(row(KVL), row(LANE), _const((2 * HP, KVL)), _const((1, KVL)), _const((1, HP))), vmem_mb=48)(
            dks[0], dks[1], dvs[0], dvs[1], ckv, kpe, kva_w, wukv, wk, cosk, sink,
            cs["seg_h"], cs["seg_ht"], cs["rot"], cs["rot_t"])


_SCALE = DH ** -0.5
_SCALE_LOG2E = _SCALE * 1.4426950408889634


def _attn_specs(tq, s, nc, tpe, n_lat_rows):
    qs = pl.BlockSpec((tq, LANE), lambda i, j, t: (i * tpe + t, j))
    kl = pl.BlockSpec((s, LANE), lambda i, j, t: (i, j))
    kc = pl.BlockSpec((nc, LANE), lambda i, j, t: (n_lat_rows // nc + i, j))
    return qs, kl, kc


def _key_chunks(s, nc, ck):
    return ([(0, lo, min(lo + ck, s)) for lo in range(0, s, ck)]
            + [(1, lo, min(lo + ck, nc)) for lo in range(0, nc, ck)])


def _lse_spec(tq):
    return pl.BlockSpec((1, 8, tq), lambda i, j, t: (i * H + j, 0, t))


def _attn_fwd(q, k, v, *, nb, s, nc, tq, ck):
    tpe = s // tq
    r_lat = nb * s
    chunks = _key_chunks(s, nc, ck)
    hp = 4

    def body(q_ref, kl_ref, kc_ref, vl_ref, vc_ref, o_ref, lse_ref):
        k_refs, v_refs = (kl_ref, kc_ref), (vl_ref, vc_ref)
        for hh in range(hp):
            hs = slice(hh * LANE, (hh + 1) * LANE)
            qv = q_ref[:, hs]
            xs = [_mm_nt(qv, k_refs[w][lo:hi, hs]) for w, lo, hi in chunks]
            m = jnp.max(xs[0], axis=-1, keepdims=True)
            for x in xs[1:]:
                m = jnp.maximum(m, jnp.max(x, axis=-1, keepdims=True))
            l = acc = None
            for x, (w, lo, hi) in zip(xs, chunks):
                e = jnp.exp2((x - m) * _SCALE_LOG2E)
                lc = jnp.sum(e, axis=-1, keepdims=True)
                pv = _mm(e.astype(BF16), v_refs[w][lo:hi, hs])
                l = lc if l is None else l + lc
                acc = pv if acc is None else acc + pv
            o_ref[:, hs] = (acc / l).astype(BF16)
            lse = m * _SCALE_LOG2E + jnp.log2(l)
            lse_ref[hh] = jnp.transpose(jnp.broadcast_to(lse, (tq, LANE)))[0:8, :]

    qs = pl.BlockSpec((tq, hp * LANE), lambda i, j, t: (i * tpe + t, j))
    kl = pl.BlockSpec((s, hp * LANE), lambda i, j, t: (i, j))
    kc = pl.BlockSpec((nc, hp * LANE), lambda i, j, t: (r_lat // nc + i, j))
    ls = pl.BlockSpec((hp, 8, tq), lambda i, j, t: (i * (H // hp) + j, 0, t))
    return _pcall(body, name="attn_fwd", grid=(nb, H // hp, tpe),
                  out_shape=(_sds((r_lat, HP), BF16), _sds((nb * H, 8, s), F32)),
                  in_specs=[qs, kl, kc, kl, kc], out_specs=(qs, ls), vmem_mb=48)(q, k, k, v, v)


def _attn_bwd(q, k, v, o, do, lse, part, *, nb, s, nc, tq, ck):
    tpe = s // tq
    r_lat = nb * s
    chunks = _key_chunks(s, nc, ck)
    hp = 2
    n_steps = nb * (H // hp) * tpe

    def body(q_ref, kl_ref, kc_ref, vl_ref, vc_ref, o_ref, do_ref, lse_ref, part_ref,
             dq_ref, dkl_ref, dkc_ref, dvl_ref, dvc_ref, recv_ref, akl, akc, avl, avc, send_sems, recv_sems):
        t = pl.program_id(2)
        step = (pl.program_id(0) * (H // hp) + pl.program_id(1)) * tpe + t
        sends = _chip_sends(part_ref, recv_ref, send_sems, recv_sems)

        @pl.when(step == 0)
        def _():
            for cp in sends:
                cp.start()

        @pl.when(step == n_steps - 1)
        def _():
            for cp in sends:
                cp.wait_recv()
            for cp in sends:
                cp.wait_send()

        @pl.when(t == 0)
        def _():
            akl[...] = jnp.zeros_like(akl)
            akc[...] = jnp.zeros_like(akc)
            avl[...] = jnp.zeros_like(avl)
            avc[...] = jnp.zeros_like(avc)

        k_refs, v_refs, ak, av = (kl_ref, kc_ref), (vl_ref, vc_ref), (akl, akc), (avl, avc)
        for hh in range(hp):
            hs = slice(hh * LANE, (hh + 1) * LANE)
            qv = q_ref[:, hs]
            lse = jnp.transpose(jnp.concatenate([lse_ref[hh]] * (LANE // 8), axis=0))[:, 0:1]
            dov = do_ref[:, hs]
            delta = jnp.sum(dov.astype(F32) * o_ref[:, hs].astype(F32), axis=-1, keepdims=True)
            dq = None
            for w, lo, hi in chunks:
                kc_v = k_refs[w][lo:hi, hs]
                p = jnp.exp2(_mm_nt(qv, kc_v) * _SCALE_LOG2E - lse)
                ds = (p * (_mm_nt(dov, v_refs[w][lo:hi, hs]) - delta)).astype(BF16)
                part = _mm(ds, kc_v)
                dq = part if dq is None else dq + part
                ak[w][hs, lo:hi] += _mm_tn(qv, ds)
                av[w][hs, lo:hi] += _mm_tn(dov, p.astype(BF16))
            dq_ref[:, hs] = dq * _SCALE

        @pl.when(t == tpe - 1)
        def _():
            dkl_ref[...] = akl[...].T * _SCALE
            dkc_ref[...] = akc[...].T * _SCALE
            dvl_ref[...] = avl[...].T
            dvc_ref[...] = avc[...].T

    qs = pl.BlockSpec((tq, hp * LANE), lambda i, j, t: (i * tpe + t, j))
    kl = pl.BlockSpec((s, hp * LANE), lambda i, j, t: (i, j))
    kc = pl.BlockSpec((nc, hp * LANE), lambda i, j, t: (r_lat // nc + i, j))
    kc_out = pl.BlockSpec((nc, hp * LANE), lambda i, j, t: (i, j))
    ls = pl.BlockSpec((hp, 8, tq), lambda i, j, t: (i * (H // hp) + j, 0, t))
    return _pcall(
        body, name="attn_bwd", grid=(nb, H // hp, tpe),
        out_shape=(_sds((r_lat, HP), F32), _sds((r_lat, HP), F32), _sds((nb * nc, HP), F32),
                   _sds((r_lat, HP), F32), _sds((nb * nc, HP), F32), _sds((3,) + part.shape[1:], part.dtype)),
        in_specs=[qs, kl, kc, kl, kc, qs, qs, ls, ANY], out_specs=(qs, kl, kc_out, kl, kc_out, ANY),
        scratch=[pltpu.VMEM((hp * LANE, s), F32), pltpu.VMEM((hp * LANE, nc), F32)] * 2
        + [pltpu.SemaphoreType.DMA((3,))] * 2,
        vmem_mb=60)(q, k, k, v, v, o, do, lse, part)


def _gating(vn, ws_ref, bias_ref, s_scr, tm):
    lane = lax.broadcasted_iota(jnp.int32, (CH, LANE), 1)
    for c in range(tm // CH):
        rs = slice(c * CH, (c + 1) * CH)
        for j in range(G // 2):
            ls = slice(j * LANE, (j + 1) * LANE)
            vp = vn[rs, ls]
            s_scr[rs, ls] = jnp.where(lane < GD, _mm(ws_ref[2 * j], vp), _mm(ws_ref[2 * j + 1], vp)) + bias_ref[:, ls]


def _mix_fwd(u, v, attn, x1, gate, wv, ws, bias, wout, cs, *, tm, n_lat, tpe):
    nrows = gate.shape[0]

    def body(u_ref, v_ref, attn_ref, x_ref, gate_ref, wv_ref, ws_ref, bias_ref, wout_ref, seg, segt,
             x2_ref, mix_ref, s_scr):
        vg = _gelu(v_ref[...])
        rg = lax.rsqrt(_dot_hl(vg * vg, seg[...]) * (1.0 / GD) + EPS)
        vn = (vg * _dot_hl(rg, segt[...]) * wv_ref[...]).astype(BF16)
        _gating(vn, ws_ref, bias_ref, s_scr, tm)
        sg = (_gelu(u_ref[...]) * s_scr[...]).astype(BF16)
        mix = _mm(attn_ref[...], wout_ref[0:HP, :]) + _mm(sg, wout_ref[HP:, :])
        mix_ref[...] = mix.astype(BF16)
        x2_ref[...] = x_ref[...] + gate_ref[0] * mix

    row = lambda cols: pl.BlockSpec((tm, cols), lambda t: (t, 0))
    r = n_lat * tm
    return _pcall(
        body, name="mix_fwd", grid=(n_lat,),
        out_shape=(_sds((r, D), F32), _sds((r, D), BF16)),
        in_specs=[row(G * GD), row(G * GD), row(HP), row(D), _mod_spec(1, tpe, nrows), _const((1, G * GD)),
                  _const((G, CH, CH)), _const((CH, G * GD)), _const((HP + G * GD, D)), _const((G * GD, LANE)),
                  _const((LANE, G * GD))],
        out_specs=(row(D), row(D)), scratch=[pltpu.VMEM((tm, G * GD), F32)], vmem_mb=40)(
            u, v, attn, x1, gate, wv, ws, bias, wout, cs["seg_g"], cs["seg_gt"])


def _mix_bwd(dx2, mix, u, v, attn, gate, wv, ws, wst, bias, wout, cs, *, tm, n_lat, tpe):
    nrows = gate.shape[0]
    wrows = HP + G * GD

    def body(dx2_ref, mix_ref, u_ref, v_ref, attn_ref, gate_ref, wv_ref, ws_ref, wst_ref, bias_ref, wout_ref, seg, segt,
             dattn_ref, du_ref, dv_ref, dgate_ref, dwout_ref, dws_ref, dbs_ref, dwv_ref, s_scr, dvn_scr, dbias_scr):
        t = pl.program_id(0)
        dx2 = dx2_ref[...]
        dmix = (dx2 * gate_ref[0]).astype(BF16)
        dcat = _mm_nt(dmix, wout_ref[...])
        dattn_ref[...] = dcat[:, :HP].astype(BF16)
        dsg = dcat[:, HP:]

        vraw = v_ref[...]
        vg = _gelu(vraw)
        rg = lax.rsqrt(_dot_hl(vg * vg, seg[...]) * (1.0 / GD) + EPS)
        r64 = _dot_hl(rg, segt[...])
        y = vg * r64
        wv_v = wv_ref[...]
        vn = (y * wv_v).astype(BF16)
        _gating(vn, ws_ref, bias_ref, s_scr, tm)
        uraw = u_ref[...]
        ug = _gelu(uraw)
        s = s_scr[...]
        sg = (ug * s).astype(BF16)
        du_ref[...] = (dsg * s * _gelu_grad(uraw)).astype(BF16)
        ds = dsg * ug

        @pl.when(t == 0)
        def _():
            dwout_ref[...] = jnp.zeros_like(dwout_ref)
            dws_ref[...] = jnp.zeros_like(dws_ref)
            dwv_ref[...] = jnp.zeros_like(dwv_ref)
            dbias_scr[...] = jnp.zeros_like(dbias_scr)

        @pl.when(t % tpe == 0)
        def _():
            dgate_ref[...] = jnp.zeros_like(dgate_ref)

        dgate_ref[0] += _rowsum(dx2 * mix_ref[...].astype(F32))
        dwout_ref[...] += _mm_tn(jnp.concatenate([attn_ref[...], sg], axis=-1), dmix)

        lane = lax.broadcasted_iota(jnp.int32, (CH, LANE), 1)
        for c in range(tm // CH):
            rs = slice(c * CH, (c + 1) * CH)
            dbias_scr[...] += ds[rs, :]
            for j in range(G // 2):
                ls = slice(j * LANE, (j + 1) * LANE)
                dsp32 = ds[rs, ls]
                dsp = dsp32.astype(BF16)
                vp = vn[rs, ls]
                dvn_scr[rs, ls] = jnp.where(lane < GD, _mm(wst_ref[2 * j], dsp), _mm(wst_ref[2 * j + 1], dsp))
                dws_ref[2 * j] += _mm_nt(jnp.where(lane < GD, dsp32, 0.0).astype(BF16), vp)
                dws_ref[2 * j + 1] += _mm_nt(jnp.where(lane < GD, 0.0, dsp32).astype(BF16), vp)

        dvn = dvn_scr[...]
        dwv_ref[...] += _rowsum(dvn * y)
        dy = dvn * wv_v
        mean_g = _dot_hl(dy * y, seg[...]) * (1.0 / GD)
        dvg = r64 * (dy - y * _dot_hl(mean_g, segt[...]))
        dv_ref[...] = (dvg * _gelu_grad(vraw)).astype(BF16)

        @pl.when(t == n_lat - 1)
        def _():
            dbs_ref[...] = _dot_hl(dbias_scr[...], seg[...])

    row = lambda cols: pl.BlockSpec((tm, cols), lambda t: (t, 0))
    r = n_lat * tm
    return _pcall(
        body, name="mix_bwd", grid=(n_lat,),
        out_shape=(_sds((r, HP), BF16), _sds((r, G * GD), BF16), _sds((r, G * GD), BF16), _sds((nrows, 1, D), F32),
                   _sds((wrows, D), F32), _sds((G, CH, CH), F32), _sds((CH, LANE), F32), _sds((1, G * GD), F32)),
        in_specs=[row(D), row(D), row(G * GD), row(G * GD), row(HP), _mod_spec(1, tpe, nrows), _const((1, G * GD)),
                  _const((G, CH, CH)), _const((G, CH, CH)), _const((CH, G * GD)), _const((wrows, D)),
                  _const((G * GD, LANE)), _const((LANE, G * GD))],
        out_specs=(row(HP), row(G * GD), row(G * GD), _mod_spec(1, tpe, nrows), _const((wrows, D)),
                   _const((G, CH, CH)), _const((CH, LANE)), _const((1, G * GD))),
        scratch=[pltpu.VMEM((tm, G * GD), F32), pltpu.VMEM((tm, G * GD), F32), pltpu.VMEM((CH, G * GD), F32)],
        vmem_mb=56)(dx2, mix, u, v, attn, gate, wv, ws, wst, bias, wout, cs["seg_g"], cs["seg_gt"])


def _adamw_math(w, g, m, v):
    m2 = ADAM_B1 * m + (1.0 - ADAM_B1) * g
    v2 = ADAM_B2 * v + (1.0 - ADAM_B2) * (g * g)
    m_hat = m2 / (1.0 - ADAM_B1 ** ADAM_STEP)
    v_hat = v2 / (1.0 - ADAM_B2 ** ADAM_STEP)
    delta = -ADAM_LR * (m_hat / (jnp.sqrt(v_hat) + ADAM_EPS) + ADAM_WD * w)
    return delta, m2, v2


def _row_tile(r, c):
    best = r
    for tr in range(8, r, 8):
        if r % tr == 0 and tr * c * 4 <= MIB:
            best = tr
    return best


def _adamw(w, g, m, v, name):
    r, c = w.shape
    tr = _row_tile(r, c)

    def body(w_ref, g_ref, m_ref, v_ref, d_ref, mo_ref, vo_ref):
        d_ref[...], mo_ref[...], vo_ref[...] = _adamw_math(w_ref[...], g_ref[...], m_ref[...], v_ref[...])

    blk = pl.BlockSpec((tr, c), lambda t: (t, 0))
    return _pcall(body, name=name, grid=(r // tr,), out_shape=(_sds((r, c), F32),) * 3,
                  in_specs=[blk] * 4, out_specs=(blk,) * 3)(w, g, m, v)


def _adamw_small(params):
    n = len(params)

    def body(*refs):
        ins, outs = refs[:4 * n], refs[4 * n:]
        for i in range(n):
            w, g, m, v = (ins[4 * i + k][...] for k in range(4))
            if i == 0:
                sig = _sigmoid(w)
                g = g * (sig * (1.0 + w * (1.0 - sig)))
            d, m2, v2 = _adamw_math(w, g, m, v)
            outs[4 * i][...] = g
            outs[4 * i + 1][...] = d
            outs[4 * i + 2][...] = m2
            outs[4 * i + 3][...] = v2

    flat = [a for p in params for a in p]
    out_shape = tuple(_sds(p[0].shape, F32) for p in params for _ in range(4))
    res = _pcall(body, name="adamw_small", out_shape=out_shape, in_specs=[VMEM] * (4 * n),
                 out_specs=(VMEM,) * (4 * n))(*flat)
    return [res[4 * i:4 * i + 4] for i in range(n)]


def _rope_tables(s):
    rows = jnp.repeat(jnp.arange(s // GRID_W, dtype=F32), GRID_W)
    cols = jnp.tile(jnp.arange(GRID_W, dtype=F32), s // GRID_W)
    half = DR // 2
    inv = ROPE_BASE ** (-jnp.arange(0, half, 2, dtype=F32) / half)
    ang_r = rows[:, None] * inv
    ang_c = cols[:, None] * inv
    ang = jnp.concatenate([ang_r, ang_r, ang_c, ang_c], axis=-1)
    return jnp.cos(ang), jnp.sin(ang)


def _head_pad(a, real):
    return jnp.pad(a, ((0, 0), (0, LANE - real), (0, 0))).reshape(HP, a.shape[2])


def kernel(x, c, ctx, c_ctx, w_ada, b_ada, norm1_w, ffn1_w1, ffn1_w3, ffn1_w2, norm2_w, w_in, q_a_norm_w, w_uq, kv_a_norm_w, w_ukv, q_norm_w, k_norm_w, v_norm_w, w_s, b_s, w_out, norm3_w, ffn2_w1, ffn2_w3, ffn2_w2, loss_target, m_c_ctx, m_w_ada, m_b_ada, m_norm1_w, m_ffn1_w1, m_ffn1_w3, m_ffn1_w2, m_norm2_w, m_w_in, m_q_a_norm_w, m_w_uq, m_kv_a_norm_w, m_w_ukv, m_q_norm_w, m_k_norm_w, m_v_norm_w, m_w_s, m_b_s, m_w_out, m_norm3_w, m_ffn2_w1, m_ffn2_w3, m_ffn2_w2, v_c_ctx, v_w_ada, v_b_ada, v_norm1_w, v_ffn1_w1, v_ffn1_w3, v_ffn1_w2, v_norm2_w, v_w_in, v_q_a_norm_w, v_w_uq, v_kv_a_norm_w, v_w_ukv, v_q_norm_w, v_k_norm_w, v_v_norm_w, v_w_s, v_b_s, v_w_out, v_norm3_w, v_ffn2_w1, v_ffn2_w3, v_ffn2_w2):
    nb, s, _ = x.shape
    nc = ctx.shape[1]
    tm = 256 if nc % 256 == 0 else 128
    tpe = s // tm
    n_lat = nb * tpe
    n_all = n_lat + nb * nc // tm
    r_lat = nb * s
    me = 4 * lax.axis_index("x") + 2 * lax.axis_index("y") + lax.axis_index("c")
    cs = _consts()
    ncol = w_ada.shape[2]
    fsh = ffn1_w1.shape[2]
    assert nb + 1 <= 8 and NDEV * fsh == FF and NDEV * ncol == NMOD * D and s % nc == 0 and nc % tm == 0

    def t16(a):
        return a.T.astype(BF16)

    wpack1 = jnp.concatenate([t16(ffn1_w1[0]), t16(ffn1_w3[0]), ffn1_w2[0].astype(BF16)], axis=0)
    a_loc = jnp.concatenate([c, c_ctx[None, :], jnp.zeros((7 - nb, D), F32)], axis=0)
    a_raw, _, mod_all, wall1 = _ada_front(a_loc, w_ada[0], lax.dynamic_slice_in_dim(b_ada, me * ncol, ncol, axis=1),
                                          wpack1)
    a_raw = a_raw.reshape(NDEV * 8, D)
    mod_mine = lax.dynamic_slice_in_dim(mod_all, 8 * me, 8, axis=1)
    modtab = mod_mine.transpose(1, 0, 2).reshape(8, NMOD, D)[:nb + 1]
    wpack2 = jnp.concatenate([
        t16(ffn2_w1[0]), t16(ffn2_w3[0]), ffn2_w2[0].astype(BF16),
        t16(w_in[0]), jnp.zeros((12, D), BF16),
        w_out[0].astype(BF16),
        t16(w_uq[0]).reshape(24, D), jnp.zeros((8, D), BF16),
        t16(w_ukv[0]).reshape(16, D)], axis=0)

    def head_w(wn):
        return jnp.tile(jnp.pad(wn, ((0, 0), (0, LANE - DH))), (1, H))

    wq, wk = head_w(q_norm_w), head_w(k_norm_w)
    wv = v_norm_w.reshape(1, G * GD)
    ws16 = w_s[0].astype(BF16)
    wst16 = w_s[0].transpose(0, 2, 1).astype(BF16)
    bias = jnp.repeat(b_s[0].T, GD, axis=1)
    cos, sin = _rope_tables(s)
    cos = jnp.pad(cos, ((0, 0), (DN, LANE - DH)), constant_values=1.0)
    sin = jnp.pad(sin, ((0, 0), (DN, LANE - DH)))
    cos_k = jnp.concatenate([cos, jnp.ones((tm, LANE), F32)], axis=0)
    sin_k = jnp.concatenate([sin, jnp.zeros((tm, LANE), F32)], axis=0)

    xs = (x.reshape(r_lat, D), ctx.reshape(nb * nc, D))
    tmf = 2 * tm if s % (2 * tm) == 0 and (nb * nc) % (2 * tm) == 0 else tm
    x1, a1, b1, o1, wall2 = _ffn_fwd(xs, modtab[:, 0:3], norm1_w, wall1, 0, tm=tmf, n_tiles=(r_lat + nb * nc) // tmf,
                                     tpe=s // tmf, n_lat=r_lat // tmf, name="ffn1_fwd", gather=wpack2)

    o0 = 3 * fsh
    wint = wall2[:, o0:o0 + 180].reshape(IN_COLS, D)
    z = lambda n: jnp.zeros((n, D), BF16)
    wint = jnp.concatenate([wint[0:128], wint[160:416], wint[416:928], wint[928:1440],
                            z(DN), wint[128:160], z(LANE - DH)], axis=0)
    wout = wall2[:, o0 + 192:o0 + 320].reshape(D, D)
    wout = jnp.concatenate([_head_pad(wout[:H * DV].reshape(H, DV, D), DV), wout[H * DV:]], axis=0)
    wuq = _head_pad(wall2[:, o0 + 320:o0 + 344].reshape(H, DH, QL), DH)
    wukvt = wall2[:, o0 + 352:o0 + 368].reshape(H, DN + DV, KVL)
    wukv = jnp.concatenate([_head_pad(wukvt[:, :DN], DN), _head_pad(wukvt[:, DN:], DV)], axis=0)

    ckv, qp, u_raw, v_raw, kpe = _proj_fwd(x1, modtab[:, 3:5], norm2_w, wint, tm=tm, n_tiles=n_all, tpe=tpe)
    q = _q_prep_fwd(qp, q_a_norm_w, wuq, wq, cos, sin, cs, tm=tm, n_lat=n_lat, tpe=tpe)
    k, v = _kv_prep_fwd(ckv, kpe, kv_a_norm_w, wukv, wk, cos_k, sin_k, cs, tm=tm, n_tiles=n_all, tpe=tpe, n_lat=n_lat)
    attn, lse = _attn_fwd(q, k, v, nb=nb, s=s, nc=nc, tq=tm, ck=512)
    x2, mix = _mix_fwd(u_raw, v_raw, attn, x1, modtab[:nb, 5:6], wv, ws16, bias, wout, cs,
                       tm=tm, n_lat=n_lat, tpe=tpe)
    dy, a2, b2, o2, lsum = _ffn_fwd((x2,), modtab[:nb, 6:9], norm3_w, wall2, 0, tm=tmf, n_tiles=r_lat // tmf,
                                    tpe=s // tmf, n_lat=r_lat // tmf, name="ffn2_fwd",
                                    target=loss_target.reshape(r_lat, D))
    loss = lax.psum(lsum[0, 0] * (0.5 / D), ("x", "y", "c"))

    tr = 2 * tm if n_lat % 2 == 0 and n_all % 2 == 0 else tm
    dx2, da2, db2, g2, do2, h2, dmod678, dnorm3 = _ffn_bwd_dx(
        dy, (x2,), a2, b2, o2, modtab[:nb, 6:9], norm3_w, wall2, 0,
        tm=tm, n_tiles=n_lat, tpe=tpe, n_lat=n_lat, name="ffn2_bwd_dx")
    g_ffn2 = _ffn_bwd_dw(h2, do2, da2, db2, g2, tr=tr, name="ffn2_bwd_dw")
    part_ffn2 = _add_sibling(g_ffn2, _scatter_sibling([g_ffn2], "scatter_sibling_ffn2")[0], 176, "add_sibling_ffn2")

    dattn, du, dv, dgate5, dwout, dws, dbs, dwv = _mix_bwd(
        dx2, mix, u_raw, v_raw, attn, modtab[:nb, 5:6], wv, ws16, wst16, bias, wout, cs, tm=tm, n_lat=n_lat, tpe=tpe)
    tq = 2 * tm if s % (2 * tm) == 0 else tm
    dq, dk_l, dk_c, dv_l, dv_c, recv_ffn2 = _attn_bwd(q, k, v, attn, dattn, lse, part_ffn2,
                                                      nb=nb, s=s, nc=nc, tq=tq, ck=1024)
    dqp, dwuq, dqa, dwq = _q_prep_bwd(dq, qp, q_a_norm_w, wuq, wq, cos, sin, cs, tm=tm, n_lat=n_lat, tpe=tpe)
    dckv, dkpe, dwukv, dkva, dwk = _kv_prep_bwd((dk_l, dk_c), (dv_l, dv_c), ckv, kpe, kv_a_norm_w, wukv, wk,
                                                cos_k, sin_k, cs, tm=tm, n_tiles=n_all, tpe=tpe, n_lat=n_lat)
    dx1, dwin, dmod34, dnorm2 = _proj_bwd(dckv, dkpe, dqp, du, dv, dx2, x1, modtab[:, 3:5], norm2_w, wint,
                                          tm=tm, n_tiles=n_all, tpe=tpe, n_lat=n_lat)
    dx0, da1, db1, g1, do1, h1, dmod012, dnorm1 = _ffn_bwd_dx(
        dx1, xs, a1, b1, o1, modtab[:, 0:3], norm1_w, wall1, 0,
        tm=tm, n_tiles=n_all, tpe=tpe, n_lat=n_lat, name="ffn1_bwd_dx")
    g_ffn1 = _ffn_bwd_dw(h1, do1, da1, db1, g1, tr=tr, name="ffn1_bwd_dw")
    grad_x = dx0.reshape(nb, s, D)

    zrow = jnp.zeros((1, D), F32)
    g_lat = jnp.concatenate([dmod012[:nb, 0], dmod012[:nb, 1], dmod012[:nb, 2], dmod34[:nb, 0], dmod34[:nb, 1],
                             dgate5[:, 0], dmod678[:, 0], dmod678[:, 1], dmod678[:, 2]], axis=1)
    g_ctx = jnp.concatenate([dmod012[nb:, 0], dmod012[nb:, 1], dmod012[nb:, 2], dmod34[nb:, 0], dmod34[nb:, 1],
                             zrow, zrow, zrow, zrow], axis=1)
    g_loc = jnp.concatenate([g_lat, g_ctx, jnp.zeros((7 - nb, NMOD * D), F32)], axis=0)

    def blocks(a):
        return a.reshape(NDEV, a.shape[0] // NDEV, D)

    dwin_o = jnp.concatenate([dwin[0:128], dwin[KPE_LO:KPE_LO + DR], dwin[128:384], dwin[384:896], dwin[896:1408]],
                             axis=0)
    dwout_o = jnp.concatenate([dwout[:HP].reshape(H, LANE, D)[:, :DV].reshape(H * DV, D), dwout[HP:]], axis=0)
    dwuq_o = dwuq.reshape(H, LANE, QL)[:, :DH]
    dwukv_o = jnp.concatenate([dwukv[:HP].reshape(H, LANE, KVL)[:, :DN], dwukv[HP:].reshape(H, LANE, KVL)[:, :DV]],
                              axis=1)
    gmisc = jnp.concatenate([
        blocks(dwin_o).astype(BF16), jnp.zeros((NDEV, 12, D), BF16),
        blocks(dwout_o).astype(BF16),
        dwuq_o.reshape(NDEV, 24, D).astype(BF16), jnp.zeros((NDEV, 8, D), BF16),
        dwukv_o.reshape(NDEV, 16, D).astype(BF16)], axis=1)
    got_ffn1, got_misc, g_all = _scatter_sibling([g_ffn1, gmisc], "scatter_sibling", gather=g_loc)
    g_all = g_all.reshape(NDEV * 8, NMOD * D)
    g_cols = lax.dynamic_slice_in_dim(g_all, me * ncol, ncol, axis=1)
    g_w_ada, pc_ctx, g_b_ada = _ada_bwd(a_raw, c_ctx.reshape(D, 1), g_all, g_cols, w_ada[0], nb)
    parts = [_add_sibling(g_ffn1, got_ffn1, 176, "add_sibling_ffn1"), _add_sibling(gmisc, got_misc, 368, "add_sibling_misc")]

    def prow(a):
        a = a.reshape(1, -1)
        return jnp.concatenate([a, jnp.zeros((1, D - a.shape[1]), F32)], axis=1)

    g_qn = dwq.reshape(H, LANE)[:, :DH].sum(0)
    g_kn = dwk.reshape(H, LANE)[:, :DH].sum(0)
    spack = jnp.concatenate([
        dnorm1, dnorm2, dnorm3, prow(dqa), prow(dkva), prow(g_qn), prow(g_kn), prow(dwv),
        prow(dbs[:, :G].T), prow(pc_ctx), jnp.zeros((6, D), F32), dws.reshape(CH, D)], axis=0)
    recv_ffn1, recv_misc, small_all = _scatter_chips(parts, "scatter_chips", gather=spack)
    recv = (recv_ffn1, recv_misc)
    ssum = _sum_slots(small_all, 144, "sum_small")
    gsum1 = _sum_chips(parts[0], recv[0], 176, "sum_grads_ffn1")
    gsum2 = _sum_chips(part_ffn2, recv_ffn2, 176, "sum_grads_ffn2")
    msum = _sum_chips(parts[1], recv[1], 368, "sum_grads_misc")

    g_big = {
        "ffn1_w1": gsum1[0:fsh].T, "ffn1_w3": gsum1[fsh:2 * fsh].T, "ffn1_w2": gsum1[2 * fsh:3 * fsh],
        "ffn2_w1": gsum2[0:fsh].T, "ffn2_w3": gsum2[fsh:2 * fsh].T, "ffn2_w2": gsum2[2 * fsh:3 * fsh],
        "w_in": msum[0:180].T, "w_out": msum[192:320],
        "w_uq": msum[320:344].reshape(DH, QL).T, "w_ukv": msum[352:368].reshape(DN + DV, KVL).T,
        "w_ada": g_w_ada,
    }

    big_in = {
        "w_ada": (w_ada, m_w_ada, v_w_ada), "ffn1_w1": (ffn1_w1, m_ffn1_w1, v_ffn1_w1),
        "ffn1_w3": (ffn1_w3, m_ffn1_w3, v_ffn1_w3), "ffn1_w2": (ffn1_w2, m_ffn1_w2, v_ffn1_w2),
        "w_in": (w_in, m_w_in, v_w_in), "w_uq": (w_uq, m_w_uq, v_w_uq), "w_ukv": (w_ukv, m_w_ukv, v_w_ukv),
        "w_out": (w_out, m_w_out, v_w_out), "ffn2_w1": (ffn2_w1, m_ffn2_w1, v_ffn2_w1),
        "ffn2_w3": (ffn2_w3, m_ffn2_w3, v_ffn2_w3), "ffn2_w2": (ffn2_w2, m_ffn2_w2, v_ffn2_w2),
    }
    res = {}
    for nm, (w, m, v_) in big_in.items():
        g = g_big[nm]
        d_, m_, v2_ = _adamw(w[0], g, m[0], v_[0], "adamw_" + nm)
        res[nm] = tuple(a[None] for a in (g, d_, m_, v2_))

    small_in = [
        ("c_ctx", c_ctx, m_c_ctx, v_c_ctx, ssum[9:10], (1, D)),
        ("b_ada", b_ada, m_b_ada, v_b_ada, g_b_ada, (1, NMOD * D)),
        ("norm1_w", norm1_w, m_norm1_w, v_norm1_w, ssum[0:1], (1, D)),
        ("norm2_w", norm2_w, m_norm2_w, v_norm2_w, ssum[1:2], (1, D)),
        ("norm3_w", norm3_w, m_norm3_w, v_norm3_w, ssum[2:3], (1, D)),
        ("q_a_norm_w", q_a_norm_w, m_q_a_norm_w, v_q_a_norm_w, ssum[3:4, :QL], (1, QL)),
        ("kv_a_norm_w", kv_a_norm_w, m_kv_a_norm_w, v_kv_a_norm_w, ssum[4:5, :KVL], (1, KVL)),
        ("q_norm_w", q_norm_w, m_q_norm_w, v_q_norm_w, ssum[5:6, :DH], (1, DH)),
        ("k_norm_w", k_norm_w, m_k_norm_w, v_k_norm_w, ssum[6:7, :DH], (1, DH)),
        ("v_norm_w", v_norm_w, m_v_norm_w, v_v_norm_w, ssum[7:8, :G * GD], (G, GD)),
        ("b_s", b_s, m_b_s, v_b_s, ssum[8:9], (G, CH)),
        ("w_s", w_s, m_w_s, v_w_s, ssum[16:144], (G * CH, CH)),
    ]
    small_out = _adamw_small(
        [(w.reshape(sh), g.reshape(sh), m.reshape(sh), v_.reshape(sh)) for _, w, m, v_, g, sh in small_in])
    for (nm, w, *_), outs in zip(small_in, small_out):
        res[nm] = tuple(a.reshape(w.shape) for a in outs)

    order = ["c_ctx", "w_ada", "b_ada", "norm1_w", "ffn1_w1", "ffn1_w3", "ffn1_w2", "norm2_w", "w_in", "q_a_norm_w",
             "w_uq", "kv_a_norm_w", "w_ukv", "q_norm_w", "k_norm_w", "v_norm_w", "w_s", "b_s", "w_out", "norm3_w",
             "ffn2_w1", "ffn2_w3", "ffn2_w2"]
    return (loss, grad_x, *[res[n][0] for n in order], *[res[n][1] for n in order],
            *[res[n][2] for n in order], *[res[n][3] for n in order])
```

```python
import numpy as np
import jax
import jax.numpy as jnp
from jax import lax
from jax.experimental import pallas as pl
from jax.experimental.pallas import tpu as pltpu

F32 = jnp.float32
BF16 = jnp.bfloat16

D = 1024
FF = 2816
FC = 256
H = 8
DN, DR, DV = 64, 32, 64
DH = DN + DR
QL, KVL = 256, 128
G, GD, CH = 8, 64, 128
NMOD = 9
EPS = 1e-6
GRID_W = 64
ROPE_BASE = 10000.0
NDEV = 8
LANE = 128
HP = H * LANE
IN_COLS = 1440
WIN_ROWS = 1536
KPE_LO = 1408 + DN
NFFN_W = 6
MIB = 1 << 20

ADAM_LR, ADAM_B1, ADAM_B2, ADAM_EPS, ADAM_WD, ADAM_STEP = 0.001, 0.9, 0.999, 1e-08, 0.01, 10

MESH = pl.DeviceIdType.MESH
ANY = pl.BlockSpec(memory_space=pl.ANY)
VMEM = pl.BlockSpec(memory_space=pltpu.VMEM)


def _mm(a, b):
    return jnp.dot(a, b, preferred_element_type=F32)


def _mm_nt(a, b):
    return lax.dot_general(a, b, (((1,), (1,)), ((), ())), preferred_element_type=F32)


def _mm_tn(a, b):
    return lax.dot_general(a, b, (((0,), (0,)), ((), ())), preferred_element_type=F32)


def _dot_hl(x, m):
    hi = x.astype(BF16)
    lo = (x - hi.astype(F32)).astype(BF16)
    return _mm(hi, m) + _mm(lo, m)


def _sigmoid(a):
    return 1.0 / (1.0 + jnp.exp(-a))


_G0 = 0.7978845608028654
_G1 = 0.044715


def _gelu(x):
    return 0.5 * x * (1.0 + jnp.tanh(_G0 * (x + _G1 * (x * x * x))))


def _gelu_grad(x):
    th = jnp.tanh(_G0 * (x + _G1 * (x * x * x)))
    return 0.5 * (1.0 + th) + 0.5 * x * (1.0 - th * th) * (_G0 * (1.0 + 3.0 * _G1 * x * x))


def _rowsum(y):
    return jnp.sum(y, axis=0, keepdims=True)


def _rms(x):
    return lax.rsqrt(jnp.mean(x * x, axis=-1, keepdims=True) + EPS)


def _pcall(body, *, name, out_shape, in_specs, out_specs, grid=None, scratch=(), vmem_mb=32, aliases=None):
    kw = {}
    if grid is not None:
        kw["grid"] = grid
        sem = ("arbitrary",) * len(grid)
    else:
        sem = None
    if aliases:
        kw["input_output_aliases"] = aliases
    return pl.pallas_call(
        body, name=name, out_shape=out_shape, in_specs=in_specs, out_specs=out_specs,
        scratch_shapes=list(scratch),
        compiler_params=pltpu.CompilerParams(dimension_semantics=sem, vmem_limit_bytes=vmem_mb * MIB),
        **kw)


def _const(shape):
    nd = len(shape)
    return pl.BlockSpec(shape, lambda *_: (0,) * nd)


def _sds(shape, dt):
    return jax.ShapeDtypeStruct(shape, dt)


def _consts():
    seg_h = np.zeros((HP, LANE), np.float32)
    seg_h[np.arange(HP), np.arange(HP) // LANE] = 1.0
    seg_g = np.zeros((G * GD, LANE), np.float32)
    seg_g[np.arange(G * GD), np.arange(G * GD) // GD] = 1.0
    rot = np.zeros((LANE, LANE), np.float32)
    for base in (DN, DN + 16):
        for j in range(8):
            rot[base + j + 8, base + j] = -1.0
            rot[base + j, base + j + 8] = 1.0
    rot2 = np.zeros((2 * LANE, 2 * LANE), np.float32)
    rot2[:LANE, :LANE] = rot
    rot2[LANE:, LANE:] = rot
    twice = lambda m: np.concatenate([m, m], axis=0)
    c = dict(seg_h=seg_h, seg_ht=twice(seg_h.T), seg_g=seg_g, seg_gt=twice(seg_g.T), rot=rot2, rot_t=rot2.T)
    return {k: jnp.asarray(v, BF16) for k, v in c.items()}


_GATHER_SEMS = [pltpu.SemaphoreType.DMA((7,)), pltpu.SemaphoreType.DMA((7,)), pltpu.SemaphoreType.DMA(())]


def _gather_phases(x_ref, out_ref, send_sems, recv_sems, local_sem):
    mx, my, mc = lax.axis_index("x"), lax.axis_index("y"), lax.axis_index("c")
    me, sibling = (mx, my, mc), (mx, my, 1 - mc)
    chips = [(1 - mx, my), (mx, 1 - my), (1 - mx, 1 - my)]

    def blk(px, py, pc):
        return out_ref.at[4 * px + 2 * py + pc]

    def copy(k, block, to, src=None):
        return pltpu.make_async_remote_copy(
            src_ref=blk(*block) if src is None else src, dst_ref=blk(*block),
            send_sem=send_sems.at[k], recv_sem=recv_sems.at[k], device_id=to, device_id_type=MESH)

    mine = pltpu.make_async_copy(x_ref, blk(*me), local_sem)
    first = [copy(0, me, sibling, src=x_ref)]
    first += [copy(1 + j, me, (*chip, mc), src=x_ref) for j, chip in enumerate(chips)]
    passed = [copy(4 + j, (*chip, mc), sibling) for j, chip in enumerate(chips)]

    def start():
        mine.start()
        for cp in first:
            cp.start()

    def forward():
        for j, chip in enumerate(chips):
            copy(1 + j, (*chip, mc), me).wait_recv()
            passed[j].start()

    def finish():
        copy(0, sibling, me).wait_recv()
        for j, chip in enumerate(chips):
            copy(4 + j, (*chip, 1 - mc), me).wait_recv()
        for cp in first + passed:
            cp.wait_send()
        mine.wait()

    return start, forward, finish


def _all_gather(x, name):
    r, c = x.shape

    def body(x_ref, out_ref, send_sems, recv_sems, local_sem):
        start, forward, finish = _gather_phases(x_ref, out_ref, send_sems, recv_sems, local_sem)
        start()
        forward()
        finish()

    return pl.pallas_call(
        body, name=name, out_shape=_sds((NDEV, r, c), x.dtype), in_specs=[ANY], out_specs=ANY,
        scratch_shapes=list(_GATHER_SEMS),
    )(x)


def _chip_sends(p_ref, out_ref, send_sems, recv_sems):
    mx, my, mc = lax.axis_index("x"), lax.axis_index("y"), lax.axis_index("c")
    peers = [(1 - mx, my), (mx, 1 - my), (1 - mx, 1 - my)]
    return [pltpu.make_async_remote_copy(
        src_ref=p_ref.at[2 * px + py], dst_ref=out_ref.at[j], send_sem=send_sems.at[j], recv_sem=recv_sems.at[j],
        device_id=(px, py, mc), device_id_type=MESH) for j, (px, py) in enumerate(peers)]


def _with_gather(copies_of, n, shapes, sems, gather, name, args):
    ns = len(sems)

    def body(*refs):
        ng = 1 if gather is not None else 0
        ins, outs = refs[:n], refs[n + ng:2 * n + ng]
        copies = copies_of(ins, outs, refs[2 * n + 2 * ng:2 * n + 2 * ng + ns])
        if ng:
            start, forward, finish = _gather_phases(refs[n], refs[2 * n + 1], *refs[2 * n + 2 + ns:])
            start()
        for cp in copies:
            cp.start()
        if ng:
            forward()
        for cp in copies:
            cp.wait_recv()
        for cp in copies:
            cp.wait_send()
        if ng:
            finish()

    in_specs, out_shape, scratch = [ANY] * n, list(shapes), list(sems)
    if gather is not None:
        in_specs.append(ANY)
        args = list(args) + [gather]
        out_shape.append(_sds((NDEV,) + gather.shape, gather.dtype))
        scratch += _GATHER_SEMS
    return pl.pallas_call(body, name=name, out_shape=tuple(out_shape), in_specs=in_specs,
                          out_specs=(ANY,) * len(out_shape), scratch_shapes=scratch)(*args)


def _scatter_sibling(xs, name, gather=None):
    n = len(xs)

    def copies_of(x_refs, got_refs, sems):
        send_sems, recv_sems = sems
        mx, my, mc = lax.axis_index("x"), lax.axis_index("y"), lax.axis_index("c")
        return [pltpu.make_async_remote_copy(
            src_ref=x_refs[i].at[2 * j + 1 - mc], dst_ref=got_refs[i].at[j],
            send_sem=send_sems.at[4 * i + j], recv_sem=recv_sems.at[4 * i + j],
            device_id=(mx, my, 1 - mc), device_id_type=MESH) for i in range(n) for j in range(4)]

    shapes = tuple(_sds((4,) + x.shape[1:], x.dtype) for x in xs)
    return _with_gather(copies_of, n, shapes, [pltpu.SemaphoreType.DMA((4 * n,))] * 2, gather, name, xs)


def _scatter_chips(ps, name, gather=None):
    n = len(ps)

    def copies_of(p_refs, out_refs, sems):
        sends = []
        for i in range(n):
            sends += _chip_sends(p_refs[i], out_refs[i], sems[2 * i], sems[2 * i + 1])
        return sends

    shapes = tuple(_sds((3,) + p.shape[1:], p.dtype) for p in ps)
    return _with_gather(copies_of, n, shapes, [pltpu.SemaphoreType.DMA((3,))] * (2 * n), gather, name, ps)


def _add_sibling(x, got, tr, name):
    _, r, c = x.shape

    def body(x_ref, g_ref, o_ref):
        mc = lax.axis_index("c")
        for j in range(4):
            mine = jnp.where(mc == 0, x_ref[2 * j].astype(F32), x_ref[2 * j + 1].astype(F32))
            o_ref[j] = (mine + g_ref[j].astype(F32)).astype(o_ref.dtype)

    return _pcall(body, name=name, grid=(r // tr,), out_shape=_sds(got.shape, got.dtype),
                  in_specs=[pl.BlockSpec((NDEV, tr, c), lambda t: (0, t, 0)), pl.BlockSpec((4, tr, c), lambda t: (0, t, 0))],
                  out_specs=pl.BlockSpec((4, tr, c), lambda t: (0, t, 0)))(x, got)


def _sum_chips(part, recv, tr, name):
    _, r, c = part.shape

    def body(p_ref, r_ref, o_ref):
        slot = 2 * lax.axis_index("x") + lax.axis_index("y")
        acc = p_ref[0].astype(F32)
        for j in range(1, 4):
            acc = jnp.where(slot == j, p_ref[j].astype(F32), acc)
        for j in range(3):
            acc = acc + r_ref[j].astype(F32)
        o_ref[...] = acc

    return _pcall(body, name=name, grid=(r // tr,), out_shape=_sds((r, c), F32),
                  in_specs=[pl.BlockSpec((4, tr, c), lambda t: (0, t, 0)), pl.BlockSpec((3, tr, c), lambda t: (0, t, 0))],
                  out_specs=pl.BlockSpec((tr, c), lambda t: (t, 0)))(part, recv)


def _sum_slots(x, tr, name):
    n, r, c = x.shape

    def body(x_ref, o_ref):
        acc = x_ref[0].astype(F32)
        for s in range(1, n):
            acc = acc + x_ref[s].astype(F32)
        o_ref[...] = acc

    return _pcall(body, name=name, grid=(r // tr,), out_shape=_sds((r, c), F32),
                  in_specs=[pl.BlockSpec((n, tr, c), lambda t: (0, t, 0))],
                  out_specs=pl.BlockSpec((tr, c), lambda t: (t, 0)))(x)


def _ada_fwd(a_raw, w_loc, b_loc):
    ncol = w_loc.shape[1]

    def body(a_ref, w_ref, b_ref, o_ref):
        a = a_ref[...]
        act = (a * _sigmoid(a)).astype(BF16)
        o_ref[...] = _mm(act, w_ref[...].astype(BF16)) + b_ref[...]

    return _pcall(body, name="ada_fwd", out_shape=_sds((a_raw.shape[0], ncol), F32),
                  in_specs=[VMEM] * 3, out_specs=VMEM)(a_raw, w_loc, b_loc)


def _ada_front(a_loc, w_loc, b_loc, wpack):
    ncol = w_loc.shape[1]
    nrow = NDEV * a_loc.shape[0]

    def body(a_ref, w_ref, b_ref, wp_ref, araw_ref, mloc_ref, mall_ref, wall_ref,
             a_vm, w_vm, m_vm, lsem, *sems):
        a_start, a_forward, a_finish = _gather_phases(a_ref, araw_ref, *sems[0:3])
        m_start, m_forward, m_finish = _gather_phases(mloc_ref, mall_ref, *sems[3:6])
        w_start, w_forward, w_finish = _gather_phases(wp_ref, wall_ref, *sems[6:9])
        w_in = pltpu.make_async_copy(w_ref, w_vm, lsem.at[0])
        w_in.start()
        a_start()
        w_start()
        a_forward()
        a_finish()
        a_in = pltpu.make_async_copy(araw_ref, a_vm, lsem.at[1])
        a_in.start()
        a_in.wait()
        w_in.wait()
        a = a_vm[...].reshape(nrow, D)
        act = (a * _sigmoid(a)).astype(BF16)
        m_vm[...] = _mm(act, w_vm[...].astype(BF16)) + b_ref[...]
        m_out = pltpu.make_async_copy(m_vm, mloc_ref, lsem.at[2])
        m_out.start()
        m_out.wait()
        m_start()
        m_forward()
        m_finish()
        w_forward()
        w_finish()

    return pl.pallas_call(
        body, name="ada_front",
        out_shape=(_sds((NDEV,) + a_loc.shape, F32), _sds((nrow, ncol), F32), _sds((NDEV, nrow, ncol), F32),
                   _sds((NDEV,) + wpack.shape, wpack.dtype)),
        in_specs=[ANY, ANY, VMEM, ANY], out_specs=(ANY, ANY, ANY, ANY),
        scratch_shapes=[pltpu.VMEM((NDEV,) + a_loc.shape, F32), pltpu.VMEM(w_loc.shape, F32),
                        pltpu.VMEM((nrow, ncol), F32), pltpu.SemaphoreType.DMA((3,))] + _GATHER_SEMS * 3,
        compiler_params=pltpu.CompilerParams(vmem_limit_bytes=32 * MIB),
    )(a_loc, w_loc, b_loc, wpack)


def _ada_bwd(a_raw, cctx_col, g_all, g_cols, w_loc, nb):
    nrow = a_raw.shape[0]
    ncol = w_loc.shape[1]

    def body(a_ref, cc_ref, gall_ref, g_ref, w_ref, dw_ref, pc_ref, gb_ref):
        a = a_ref[...]
        rowid = lax.broadcasted_iota(jnp.int32, (nrow, 1), 0) % 8
        act = jnp.where(rowid < nb, a * _sigmoid(a), 0.0).astype(BF16)
        g = g_ref[...]
        gc = _rowsum(jnp.where(rowid == nb, g, 0.0))
        cc = cc_ref[...]
        dw_ref[...] = _mm_tn(act, g.astype(BF16)) + (cc * _sigmoid(cc)) * gc
        pc_ref[...] = jnp.sum(w_ref[...] * gc, axis=1, keepdims=True)
        gb_ref[...] = _rowsum(gall_ref[...])

    return _pcall(body, name="ada_bwd",
                  out_shape=(_sds((D, ncol), F32), _sds((D, 1), F32), _sds((1, g_all.shape[1]), F32)),
                  in_specs=[VMEM] * 5, out_specs=(VMEM,) * 3, vmem_mb=48)(a_raw, cctx_col, g_all, g_cols, w_loc)


def _mod_spec(k, tpe, nrows):
    return pl.BlockSpec((1, k, D), lambda t: (jnp.minimum(t // tpe, nrows - 1), 0, 0))


def _load_ffn_weights(wall_ref, first, bufs, sems):
    fsh = FF // NDEV
    cps = []
    for j, buf in enumerate(bufs):
        for d in range(NDEV):
            cps.append(pltpu.make_async_copy(wall_ref.at[d, pl.ds((first + j) * fsh, fsh)],
                                             buf.at[pl.ds(d * fsh, fsh)], sems.at[j * NDEV + d]))
    for cp in cps:
        cp.start()
    for cp in cps:
        cp.wait()


def _token_specs(xs, tm, n_lat):
    specs = [pl.BlockSpec((tm, D), lambda t: (jnp.minimum(t, n_lat - 1), 0))]
    if len(xs) == 2:
        specs.append(pl.BlockSpec((tm, D), lambda t: (jnp.maximum(t - n_lat, 0), 0)))
    return specs


def _ffn_fwd(xs, mod3, norm_w, wall, first, *, tm, n_tiles, tpe, n_lat, name, target=None, gather=None):
    nrows = mod3.shape[0]
    r = n_tiles * tm
    nx = len(xs)
    with_loss = target is not None
    with_gather = gather is not None
    fwd_step = max(2 * n_tiles // 3, 1)

    def body(*refs):
        x_refs = refs[:nx]
        pos = nx
        if with_loss:
            tgt_ref = refs[pos]
            pos += 1
        mod_ref, nw_ref, wall_ref = refs[pos:pos + 3]
        pos += 3
        if with_gather:
            gin_ref = refs[pos]
            pos += 1
        xo_ref, a_ref, b_ref, o_ref = refs[pos:pos + 4]
        pos += 4
        if with_loss:
            ls_ref = refs[pos]
            pos += 1
        if with_gather:
            gout_ref = refs[pos]
            pos += 1
        w1_ref, w3_ref, w2_ref, wsem, acc_ref = refs[pos:pos + 5]
        t = pl.program_id(0)
        if with_gather:
            g_start, g_forward, g_finish = _gather_phases(gin_ref, gout_ref, *refs[pos + 5:])

        @pl.when(t == 0)
        def _():
            if with_gather:
                g_start()
            _load_ffn_weights(wall_ref, first, (w1_ref, w3_ref, w2_ref), wsem)
            if with_loss:
                ls_ref[...] = jnp.zeros_like(ls_ref)

        if with_gather:
            @pl.when(t == fwd_step)
            def _():
                g_forward()

            @pl.when(t == n_tiles - 1)
            def _():
                g_finish()

        x = x_refs[0][...]
        if nx == 2:
            x = jnp.where(t < n_lat, x, x_refs[1][...])
        n = x * _rms(x) * nw_ref[...]
        shift, scale, gate = mod_ref[0, 0:1, :], mod_ref[0, 1:2, :], mod_ref[0, 2:3, :]
        h = (n * (1.0 + scale) + shift).astype(BF16)
        nch = FF // FC
        o = None
        for lo_c, hi_c in ((0, nch // 2), (nch // 2, nch)):
            for j in range(lo_c, hi_c):
                sl = slice(j * FC, (j + 1) * FC)
                a = _mm_nt(h, w1_ref[sl, :])
                b = _mm_nt(h, w3_ref[sl, :])
                a_ref[:, sl] = a.astype(BF16)
                b_ref[:, sl] = b.astype(BF16)
                acc_ref[:, sl] = (a * _sigmoid(a) * b).astype(BF16)
            gs = slice(lo_c * FC, hi_c * FC)
            part = _mm(acc_ref[:, gs], w2_ref[gs, :])
            o = part if o is None else o + part
        o_ref[...] = o.astype(BF16)
        out = x + (0.5 * gate) * o
        if with_loss:
            d = out - tgt_ref[...]
            xo_ref[...] = d * (1.0 / D)
            ls_ref[...] += jnp.sum(d * d)
        else:
            xo_ref[...] = out

    row = lambda cols: pl.BlockSpec((tm, cols), lambda t: (t, 0))
    in_specs = _token_specs(xs, tm, n_lat) + ([row(D)] if with_loss else []) + [
        _mod_spec(3, tpe, nrows), _const((1, D)), ANY]
    out_shape = [_sds((r, D), F32), _sds((r, FF), BF16), _sds((r, FF), BF16), _sds((r, D), BF16)]
    out_specs = [row(D), row(FF), row(FF), row(D)]
    scratch = [pltpu.VMEM((FF, D), BF16)] * 3 + [pltpu.SemaphoreType.DMA((3 * NDEV,)), pltpu.VMEM((tm, FF), BF16)]
    if with_loss:
        out_shape.append(_sds((8, LANE), F32))
        out_specs.append(_const((8, LANE)))
    args = list(xs) + ([target] if with_loss else []) + [mod3, norm_w, wall]
    if with_gather:
        assert n_tiles >= 2
        in_specs.append(ANY)
        args.append(gather)
        out_shape.append(_sds((NDEV,) + gather.shape, gather.dtype))
        out_specs.append(ANY)
        scratch += _GATHER_SEMS
    return _pcall(
        body, name=name, grid=(n_tiles,), out_shape=tuple(out_shape), in_specs=in_specs, out_specs=tuple(out_specs),
        scratch=scratch, vmem_mb=56)(*args)


def _ffn_bwd_dx(dout, xs, a, b, o, mod3, norm_w, wall, first, *, tm, n_tiles, tpe, n_lat, name):
    nrows = mod3.shape[0]
    r = n_tiles * tm
    nx = len(xs)

    def body(*refs):
        dout_ref = refs[0]
        x_refs = refs[1:1 + nx]
        (a_ref, b_ref, o_ref, mod_ref, nw_ref, wall_ref,
         dx_ref, da_ref, db_ref, g_ref, do_ref, h_ref, dmod_ref, dnw_ref,
         w1_ref, w3_ref, w2_ref, wsem) = refs[1 + nx:]
        t = pl.program_id(0)

        @pl.when(t == 0)
        def _():
            _load_ffn_weights(wall_ref, first, (w1_ref, w3_ref, w2_ref), wsem)
            dnw_ref[...] = jnp.zeros_like(dnw_ref)

        x = x_refs[0][...]
        if nx == 2:
            x = jnp.where(t < n_lat, x, x_refs[1][...])
        dout = dout_ref[...]
        rr = _rms(x)
        xh = x * rr
        nw = nw_ref[...]
        n = xh * nw
        shift, scale, gate = mod_ref[0, 0:1, :], mod_ref[0, 1:2, :], mod_ref[0, 2:3, :]
        h = (n * (1.0 + scale) + shift).astype(BF16)
        h_ref[...] = h
        d_o = ((0.5 * gate) * dout).astype(BF16)
        do_ref[...] = d_o
        dgate = _rowsum(0.5 * o_ref[...].astype(F32) * dout)
        nch = FF // FC
        groups = ((0, nch // 2), (nch // 2, nch))
        dh = None
        for lo_c, hi_c in groups:
            for j in range(lo_c, hi_c):
                sl = slice(j * FC, (j + 1) * FC)
                av = a_ref[:, sl].astype(F32)
                bv = b_ref[:, sl].astype(F32)
                dg = _mm_nt(d_o, w2_ref[sl, :])
                sig = _sigmoid(av)
                sa = av * sig
                g_ref[:, sl] = (sa * bv).astype(BF16)
                da_ref[:, sl] = (dg * bv * (sig * (1.0 + av * (1.0 - sig)))).astype(BF16)
                db_ref[:, sl] = (dg * sa).astype(BF16)
            gs = slice(lo_c * FC, hi_c * FC)
            part = _mm(da_ref[:, gs], w1_ref[gs, :]) + _mm(db_ref[:, gs], w3_ref[gs, :])
            dh = part if dh is None else dh + part
        dn = dh * (1.0 + scale)
        dxh = dn * nw

        @pl.when(t < n_lat)
        def _():
            dx_ref[...] = dout + rr * (dxh - xh * jnp.mean(dxh * xh, axis=-1, keepdims=True))

        first_visit = jnp.where(t < n_lat, t % tpe == 0, t == n_lat)

        @pl.when(first_visit)
        def _():
            dmod_ref[...] = jnp.zeros_like(dmod_ref)

        dmod_ref[0, 0:1, :] += _rowsum(dh)
        dmod_ref[0, 1:2, :] += _rowsum(dh * n)
        dmod_ref[0, 2:3, :] += dgate
        dnw_ref[...] += _rowsum(dn * xh)

    row = lambda cols: pl.BlockSpec((tm, cols), lambda t: (t, 0))
    lat = pl.BlockSpec((tm, D), lambda t: (jnp.minimum(t, n_lat - 1), 0))
    return _pcall(
        body, name=name, grid=(n_tiles,),
        out_shape=(_sds((n_lat * tm, D), F32), _sds((r, FF), BF16), _sds((r, FF), BF16), _sds((r, FF), BF16),
                   _sds((r, D), BF16), _sds((r, D), BF16), _sds((nrows, 3, D), F32), _sds((1, D), F32)),
        in_specs=[row(D)] + _token_specs(xs, tm, n_lat) + [row(FF), row(FF), row(D), _mod_spec(3, tpe, nrows),
                                                            _const((1, D)), ANY],
        out_specs=(lat, row(FF), row(FF), row(FF), row(D), row(D), _mod_spec(3, tpe, nrows), _const((1, D))),
        scratch=[pltpu.VMEM((FF, D), BF16)] * 3 + [pltpu.SemaphoreType.DMA((3 * NDEV,))],
        vmem_mb=60)(dout, *xs, a, b, o, mod3, norm_w, wall)


def _ffn_bwd_dw(h, d_o, da, db, g, *, tr, name):
    r = h.shape[0]
    fh = FF // 2
    fsh = FF // NDEV
    nk = r // tr

    def body(h_ref, do_ref, da_ref, db_ref, g_ref, out_ref, acc1, acc3, acc2):
        k = pl.program_id(1)

        @pl.when(k == 0)
        def _():
            acc1[...] = jnp.zeros_like(acc1)
            acc3[...] = jnp.zeros_like(acc3)
            acc2[...] = jnp.zeros_like(acc2)

        hv = h_ref[...]
        acc1[...] += _mm_tn(da_ref[...], hv)
        acc3[...] += _mm_tn(db_ref[...], hv)
        acc2[...] += _mm_tn(g_ref[...], do_ref[...])

        @pl.when(k == nk - 1)
        def _():
            for i, acc in enumerate((acc1, acc3, acc2)):
                out_ref[:, i * fsh:(i + 1) * fsh, :] = acc[...].reshape(NDEV // 2, fsh, D).astype(BF16)

    rowd = pl.BlockSpec((tr, D), lambda f, k: (k, 0))
    rowf = pl.BlockSpec((tr, fh), lambda f, k: (k, f))
    return _pcall(
        body, name=name, grid=(2, nk), out_shape=_sds((NDEV, 3 * fsh, D), BF16),
        in_specs=[rowd, rowd, rowf, rowf, rowf],
        out_specs=pl.BlockSpec((NDEV // 2, 3 * fsh, D), lambda f, k: (f, 0, 0)),
        scratch=[pltpu.VMEM((fh, D), F32)] * 3, vmem_mb=56)(h, d_o, da, db, g)


_PIECES = ((0, 128), (128, 384), (384, 896), (896, 1408), (1408, 1536))


def _proj_fwd(x1, mod2, norm_w, wint, *, tm, n_tiles, tpe, name="proj_fwd"):
    nrows = mod2.shape[0]
    r = n_tiles * tm

    def body(x_ref, mod_ref, nw_ref, w_ref, ckv_ref, q_ref, u_ref, v_ref, kpe_ref):
        x = x_ref[...]
        n = x * _rms(x) * nw_ref[...]
        h = (n * (1.0 + mod_ref[0, 1:2, :]) + mod_ref[0, 0:1, :]).astype(BF16)
        for (lo, hi), ref in zip(_PIECES, (ckv_ref, q_ref, u_ref, v_ref, kpe_ref)):
            ref[...] = _mm_nt(h, w_ref[lo:hi, :])

    row = lambda cols: pl.BlockSpec((tm, cols), lambda t: (t, 0))
    widths = [hi - lo for lo, hi in _PIECES]
    return _pcall(
        body, name=name, grid=(n_tiles,),
        out_shape=tuple(_sds((r, w), F32) for w in widths),
        in_specs=[row(D), _mod_spec(2, tpe, nrows), _const((1, D)), _const((WIN_ROWS, D))],
        out_specs=tuple(row(w) for w in widths), vmem_mb=40)(x1, mod2, norm_w, wint)


def _proj_bwd(dckv, dkpe, dq, du, dv, dx2, x1, mod2, norm_w, wint, *, tm, n_tiles, tpe, n_lat, name="proj_bwd"):
    nrows = mod2.shape[0]
    r = n_tiles * tm

    def body(dckv_ref, dkpe_ref, dq_ref, du_ref, dv_ref, dx2_ref, x_ref, mod_ref, nw_ref, w_ref,
             dx_ref, dw_ref, dmod_ref, dnw_ref, acc_ref):
        t = pl.program_id(0)
        is_lat = t < n_lat
        x = x_ref[...]
        rr = _rms(x)
        xh = x * rr
        nw = nw_ref[...]
        n = xh * nw
        scale = mod_ref[0, 1:2, :]
        h = (n * (1.0 + scale) + mod_ref[0, 0:1, :]).astype(BF16)

        @pl.when(t == 0)
        def _():
            dw_ref[...] = jnp.zeros_like(dw_ref)
            dnw_ref[...] = jnp.zeros_like(dnw_ref)

        dckv_v, dkpe_v = dckv_ref[...], dkpe_ref[...]
        acc_ref[...] = _mm(dckv_v, w_ref[0:128, :]) + _mm(dkpe_v, w_ref[1408:1536, :])
        dw_ref[0:128, :] += _mm_tn(dckv_v, h)
        dw_ref[1408:1536, :] += _mm_tn(dkpe_v, h)

        @pl.when(is_lat)
        def _():
            dq_v, du_v, dv_v = dq_ref[...], du_ref[...], dv_ref[...]
            acc_ref[...] += (_mm(dq_v, w_ref[128:384, :]) + _mm(du_v, w_ref[384:896, :])
                             + _mm(dv_v, w_ref[896:1408, :]))
            dw_ref[128:384, :] += _mm_tn(dq_v, h)
            dw_ref[384:896, :] += _mm_tn(du_v, h)
            dw_ref[896:1408, :] += _mm_tn(dv_v, h)

        dh = acc_ref[...]
        dn = dh * (1.0 + scale)
        dxh = dn * nw
        dx = rr * (dxh - xh * jnp.mean(dxh * xh, axis=-1, keepdims=True))
        dx_ref[...] = dx + jnp.where(is_lat, dx2_ref[...], 0.0)

        first = jnp.where(is_lat, t % tpe == 0, t == n_lat)

        @pl.when(first)
        def _():
            dmod_ref[...] = jnp.zeros_like(dmod_ref)

        dmod_ref[0, 0:1, :] += _rowsum(dh)
        dmod_ref[0, 1:2, :] += _rowsum(dh * n)
        dnw_ref[...] += _rowsum(dn * xh)

    row = lambda cols: pl.BlockSpec((tm, cols), lambda t: (t, 0))
    lat = lambda cols: pl.BlockSpec((tm, cols), lambda t: (jnp.minimum(t, n_lat - 1), 0))
    return _pcall(
        body, name=name, grid=(n_tiles,),
        out_shape=(_sds((r, D), F32), _sds((WIN_ROWS, D), F32), _sds((nrows, 2, D), F32), _sds((1, D), F32)),
        in_specs=[row(128), row(128), lat(256), lat(512), lat(512), lat(D), row(D), _mod_spec(2, tpe, nrows),
                  _const((1, D)), _const((WIN_ROWS, D))],
        out_specs=(row(D), _const((WIN_ROWS, D)), _mod_spec(2, tpe, nrows), _const((1, D))),
        scratch=[pltpu.VMEM((tm, D), F32)], vmem_mb=48)(dckv, dkpe, dq, du, dv, dx2, x1, mod2, norm_w, wint)


def _seg_sum(x, seg):
    return _mm(x.astype(BF16), seg)


def _seg_bcast(v, segt2):
    hi = v.astype(BF16)
    lo = (v - hi.astype(F32)).astype(BF16)
    return _mm(jnp.concatenate([hi, lo], axis=-1), segt2)


def _rope_pairs(t, cos, sin, rot2):
    cos2, sin2 = jnp.concatenate([cos, cos], axis=-1), jnp.concatenate([sin, sin], axis=-1)
    out = []
    for j in range(H // 2):
        tj = t[:, 2 * j * LANE:2 * (j + 1) * LANE]
        out.append(tj * cos2 + _dot_hl(tj, rot2) * sin2)
    return jnp.concatenate(out, axis=-1)


def _head_norm_rope(x, w_pad, cos, sin, seg, segt2, rot2, rope=True):
    rh = lax.rsqrt(_seg_sum(x * x, seg) * (1.0 / DH) + EPS)
    rb = _seg_bcast(rh, segt2)
    y = x * rb
    out = _rope_pairs(y * w_pad, cos, sin, rot2) if rope else None
    return out, y, rb


def _head_norm_rope_bwd(dout, y, rb, w_pad, cos, sin, seg, segt2, rot2_t):
    cos2, sin2 = jnp.concatenate([cos, cos], axis=-1), jnp.concatenate([sin, sin], axis=-1)
    dt = []
    for j in range(H // 2):
        dj = dout[:, 2 * j * LANE:2 * (j + 1) * LANE]
        dt.append(dj * cos2 + _dot_hl(dj * sin2, rot2_t))
    dt = jnp.concatenate(dt, axis=-1)
    dw = _rowsum(dt * y)
    dy = dt * w_pad
    mean_h = _seg_sum(dy * y, seg) * (1.0 / DH)
    return rb * (dy - y * _seg_bcast(mean_h, segt2)), dw


def _q_prep_fwd(qp, qa_w, wuq, wq, cos, sin, cs, *, tm, n_lat, tpe):
    def body(qp_ref, qa_ref, wuq_ref, wq_ref, cos_ref, sin_ref, seg, segt, rot, q_ref):
        x = qp_ref[...]
        cq = (x * _rms(x) * qa_ref[...]).astype(BF16)
        q, _, _ = _head_norm_rope(_mm_nt(cq, wuq_ref[...]), wq_ref[...], cos_ref[...], sin_ref[...],
                                  seg[...], segt[...], rot[...])
        q_ref[...] = q.astype(BF16)

    row = lambda cols: pl.BlockSpec((tm, cols), lambda t: (t, 0))
    tab = pl.BlockSpec((tm, LANE), lambda t: (t % tpe, 0))
    return _pcall(
        body, name="q_prep_fwd", grid=(n_lat,), out_shape=_sds((n_lat * tm, HP), BF16),
        in_specs=[row(QL), _const((1, QL)), _const((HP, QL)), _const((1, HP)), tab, tab,
                  _const((HP, LANE)), _const((2 * LANE, HP)), _const((2 * LANE, 2 * LANE))],
        out_specs=row(HP))(qp, qa_w, wuq, wq, cos, sin, cs["seg_h"], cs["seg_ht"], cs["rot"])


def _q_prep_bwd(dq, qp, qa_w, wuq, wq, cos, sin, cs, *, tm, n_lat, tpe):
    def body(dq_ref, qp_ref, qa_ref, wuq_ref, wq_ref, cos_ref, sin_ref, seg, segt, rot, rot_t,
             dqp_ref, dwuq_ref, dqa_ref, dwq_ref):
        t = pl.program_id(0)
        x = qp_ref[...]
        ra = _rms(x)
        xh = x * ra
        qa = qa_ref[...]
        cq = (xh * qa).astype(BF16)
        wuq_v = wuq_ref[...]
        wq_v, cos_v, sin_v = wq_ref[...], cos_ref[...], sin_ref[...]
        _, y, rb = _head_norm_rope(_mm_nt(cq, wuq_v), wq_v, cos_v, sin_v, seg[...], segt[...], rot[...], rope=False)
        dqraw, dwq = _head_norm_rope_bwd(dq_ref[...], y, rb, wq_v, cos_v, sin_v, seg[...], segt[...], rot_t[...])
        dqraw = dqraw.astype(BF16)
        dcq = _mm(dqraw, wuq_v)
        dxh = dcq * qa
        dqp_ref[...] = (ra * (dxh - xh * jnp.mean(dxh * xh, axis=-1, keepdims=True))).astype(BF16)

        @pl.when(t == 0)
        def _():
            dwuq_ref[...] = jnp.zeros_like(dwuq_ref)
            dqa_ref[...] = jnp.zeros_like(dqa_ref)
            dwq_ref[...] = jnp.zeros_like(dwq_ref)

        dwuq_ref[...] += _mm_tn(dqraw, cq)
        dqa_ref[...] += _rowsum(dcq * xh)
        dwq_ref[...] += dwq

    row = lambda cols: pl.BlockSpec((tm, cols), lambda t: (t, 0))
    tab = pl.BlockSpec((tm, LANE), lambda t: (t % tpe, 0))
    return _pcall(
        body, name="q_prep_bwd", grid=(n_lat,),
        out_shape=(_sds((n_lat * tm, QL), BF16), _sds((HP, QL), F32), _sds((1, QL), F32), _sds((1, HP), F32)),
        in_specs=[row(HP), row(QL), _const((1, QL)), _const((HP, QL)), _const((1, HP)), tab, tab,
                  _const((HP, LANE)), _const((2 * LANE, HP)), _const((2 * LANE, 2 * LANE)), _const((2 * LANE, 2 * LANE))],
        out_specs=(row(QL), _const((HP, QL)), _const((1, QL)), _const((1, HP))), vmem_mb=40)(
            dq, qp, qa_w, wuq, wq, cos, sin, cs["seg_h"], cs["seg_ht"], cs["rot"], cs["rot_t"])


def _kv_tab_spec(tm, tpe, n_lat):
    return pl.BlockSpec((tm, LANE), lambda t: (jnp.where(t < n_lat, t % tpe, tpe), 0))


def _kv_prep_fwd(ckv, kpe, kva_w, wukv, wk, cosk, sink, cs, *, tm, n_tiles, tpe, n_lat):
    def body(ckv_ref, kpe_ref, kva_ref, wukv_ref, wk_ref, cos_ref, sin_ref, seg, segt, rot, k_ref, v_ref):
        x = ckv_ref[...]
        ckvn = (x * _rms(x) * kva_ref[...]).astype(BF16)
        kv = _mm_nt(ckvn, wukv_ref[...])
        kx = kv[:, :HP] + jnp.concatenate([kpe_ref[...]] * H, axis=-1)
        k, _, _ = _head_norm_rope(kx, wk_ref[...], cos_ref[...], sin_ref[...], seg[...], segt[...], rot[...])
        k_ref[...] = k.astype(BF16)
        v_ref[...] = kv[:, HP:].astype(BF16)

    row = lambda cols: pl.BlockSpec((tm, cols), lambda t: (t, 0))
    tab = _kv_tab_spec(tm, tpe, n_lat)
    r = n_tiles * tm
    return _pcall(
        body, name="kv_prep_fwd", grid=(n_tiles,), out_shape=(_sds((r, HP), BF16), _sds((r, HP), BF16)),
        in_specs=[row(KVL), row(LANE), _const((1, KVL)), _const((2 * HP, KVL)), _const((1, HP)), tab, tab,
                  _const((HP, LANE)), _const((2 * LANE, HP)), _const((2 * LANE, 2 * LANE))],
        out_specs=(row(HP), row(HP)), vmem_mb=40)(
            ckv, kpe, kva_w, wukv, wk, cosk, sink, cs["seg_h"], cs["seg_ht"], cs["rot"])


def _kv_prep_bwd(dks, dvs, ckv, kpe, kva_w, wukv, wk, cosk, sink, cs, *, tm, n_tiles, tpe, n_lat):
    def body(dkl_ref, dkc_ref, dvl_ref, dvc_ref, ckv_ref, kpe_ref, kva_ref, wukv_ref, wk_ref, cos_ref, sin_ref,
             seg, segt, rot, rot_t, dckv_ref, dkpe_ref, dwukv_ref, dkva_ref, dwk_ref):
        t = pl.program_id(0)
        is_lat = t < n_lat
        dk = jnp.where(is_lat, dkl_ref[...], dkc_ref[...])
        dv = jnp.where(is_lat, dvl_ref[...], dvc_ref[...])
        x = ckv_ref[...]
        ra = _rms(x)
        xh = x * ra
        kva = kva_ref[...]
        ckvn = (xh * kva).astype(BF16)
        wukv_v = wukv_ref[...]
        wk_v, cos_v, sin_v = wk_ref[...], cos_ref[...], sin_ref[...]
        kv = _mm_nt(ckvn, wukv_v)
        kx = kv[:, :HP] + jnp.concatenate([kpe_ref[...]] * H, axis=-1)
        _, y, rb = _head_norm_rope(kx, wk_v, cos_v, sin_v, seg[...], segt[...], rot[...], rope=False)
        dkx, dwk = _head_norm_rope_bwd(dk, y, rb, wk_v, cos_v, sin_v, seg[...], segt[...], rot_t[...])
        dkpe = dkx[:, 0:LANE]
        for h in range(1, H):
            dkpe = dkpe + dkx[:, h * LANE:(h + 1) * LANE]
        lane = lax.broadcasted_iota(jnp.int32, (tm, LANE), 1)
        dkpe_ref[...] = jnp.where((lane >= DN) & (lane < DH), dkpe, 0.0).astype(BF16)
        dkv = jnp.concatenate([dkx, dv], axis=-1).astype(BF16)
        dckvn = _mm(dkv, wukv_v)
        dxh = dckvn * kva
        dckv_ref[...] = (ra * (dxh - xh * jnp.mean(dxh * xh, axis=-1, keepdims=True))).astype(BF16)

        @pl.when(t == 0)
        def _():
            dwukv_ref[...] = jnp.zeros_like(dwukv_ref)
            dkva_ref[...] = jnp.zeros_like(dkva_ref)
            dwk_ref[...] = jnp.zeros_like(dwk_ref)

        dwukv_ref[...] += _mm_tn(dkv, ckvn)
        dkva_ref[...] += _rowsum(dckvn * xh)
        dwk_ref[...] += dwk

    row = lambda cols: pl.BlockSpec((tm, cols), lambda t: (t, 0))
    lat = pl.BlockSpec((tm, HP), lambda t: (jnp.minimum(t, n_lat - 1), 0))
    ctx = pl.BlockSpec((tm, HP), lambda t: (jnp.maximum(t - n_lat, 0), 0))
    tab = _kv_tab_spec(tm, tpe, n_lat)
    r = n_tiles * tm
    return _pcall(
        body, name="kv_prep_bwd", grid=(n_tiles,),
        out_shape=(_sds((r, KVL), BF16), _sds((r, LANE), BF16), _sds((2 * HP, KVL), F32), _sds((1, KVL), F32),
                   _sds((1, HP), F32)),
        in_specs=[lat, ctx, lat, ctx, row(KVL), row(LANE), _const((1, KVL)), _const((2 * HP, KVL)), _const((1, HP)),
                  tab, tab, _const((HP, LANE)), _const((2 * LANE, HP)), _const((2 * LANE, 2 * LANE)), _const((2 * LANE, 2 * LANE))],
        out_specs=(row(KVL), row(LANE), _const((2 * HP, KVL)), _const((1, KVL)), _const((1, HP))), vmem_mb=48)(
            dks[0], dks[1], dvs[0], dvs[1], ckv, kpe, kva_w, wukv, wk, cosk, sink,
            cs["seg_h"], cs["seg_ht"], cs["rot"], cs["rot_t"])


_SCALE = DH ** -0.5
_SCALE_LOG2E = _SCALE * 1.4426950408889634


def _attn_specs(tq, s, nc, tpe, n_lat_rows):
    qs = pl.BlockSpec((tq, LANE), lambda i, j, t: (i * tpe + t, j))
    kl = pl.BlockSpec((s, LANE), lambda i, j, t: (i, j))
    kc = pl.BlockSpec((nc, LANE), lambda i, j, t: (n_lat_rows // nc + i, j))
    return qs, kl, kc


def _key_chunks(s, nc, ck):
    return ([(0, lo, min(lo + ck, s)) for lo in range(0, s, ck)]
            + [(1, lo, min(lo + ck, nc)) for lo in range(0, nc, ck)])


def _lse_spec(tq):
    return pl.BlockSpec((1, 8, tq), lambda i, j, t: (i * H + j, 0, t))


def _attn_fwd(q, k, v, *, nb, s, nc, tq, ck):
    tpe = s // tq
    r_lat = nb * s
    chunks = _key_chunks(s, nc, ck)
    hp = 4

    def body(q_ref, kl_ref, kc_ref, vl_ref, vc_ref, o_ref, lse_ref):
        k_refs, v_refs = (kl_ref, kc_ref), (vl_ref, vc_ref)
        for hh in range(hp):
            hs = slice(hh * LANE, (hh + 1) * LANE)
            qv = q_ref[:, hs]
            xs = [_mm_nt(qv, k_refs[w][lo:hi, hs]) for w, lo, hi in chunks]
            m = jnp.max(xs[0], axis=-1, keepdims=True)
            for x in xs[1:]:
                m = jnp.maximum(m, jnp.max(x, axis=-1, keepdims=True))
            l = acc = None
            for x, (w, lo, hi) in zip(xs, chunks):
                e = jnp.exp2((x - m) * _SCALE_LOG2E)
                lc = jnp.sum(e, axis=-1, keepdims=True)
                pv = _mm(e.astype(BF16), v_refs[w][lo:hi, hs])
                l = lc if l is None else l + lc
                acc = pv if acc is None else acc + pv
            o_ref[:, hs] = (acc / l).astype(BF16)
            lse = m * _SCALE_LOG2E + jnp.log2(l)
            lse_ref[hh] = jnp.transpose(jnp.broadcast_to(lse, (tq, LANE)))[0:8, :]

    qs = pl.BlockSpec((tq, hp * LANE), lambda i, j, t: (i * tpe + t, j))
    kl = pl.BlockSpec((s, hp * LANE), lambda i, j, t: (i, j))
    kc = pl.BlockSpec((nc, hp * LANE), lambda i, j, t: (r_lat // nc + i, j))
    ls = pl.BlockSpec((hp, 8, tq), lambda i, j, t: (i * (H // hp) + j, 0, t))
    return _pcall(body, name="attn_fwd", grid=(nb, H // hp, tpe),
                  out_shape=(_sds((r_lat, HP), BF16), _sds((nb * H, 8, s), F32)),
                  in_specs=[qs, kl, kc, kl, kc], out_specs=(qs, ls), vmem_mb=48)(q, k, k, v, v)


def _attn_bwd(q, k, v, o, do, lse, part, *, nb, s, nc, tq, ck):
    tpe = s // tq
    r_lat = nb * s
    chunks = _key_chunks(s, nc, ck)
    hp = 2
    n_steps = nb * (H // hp) * tpe

    def body(q_ref, kl_ref, kc_ref, vl_ref, vc_ref, o_ref, do_ref, lse_ref, part_ref,
             dq_ref, dkl_ref, dkc_ref, dvl_ref, dvc_ref, recv_ref, akl, akc, avl, avc, send_sems, recv_sems):
        t = pl.program_id(2)
        step = (pl.program_id(0) * (H // hp) + pl.program_id(1)) * tpe + t
        sends = _chip_sends(part_ref, recv_ref, send_sems, recv_sems)

        @pl.when(step == 0)
        def _():
            for cp in sends:
                cp.start()

        @pl.when(step == n_steps - 1)
        def _():
            for cp in sends:
                cp.wait_recv()
            for cp in sends:
                cp.wait_send()

        @pl.when(t == 0)
        def _():
            akl[...] = jnp.zeros_like(akl)
            akc[...] = jnp.zeros_like(akc)
            avl[...] = jnp.zeros_like(avl)
            avc[...] = jnp.zeros_like(avc)

        k_refs, v_refs, ak, av = (kl_ref, kc_ref), (vl_ref, vc_ref), (akl, akc), (avl, avc)
        for hh in range(hp):
            hs = slice(hh * LANE, (hh + 1) * LANE)
            qv = q_ref[:, hs]
            lse = jnp.transpose(jnp.concatenate([lse_ref[hh]] * (LANE // 8), axis=0))[:, 0:1]
            dov = do_ref[:, hs]
            delta = jnp.sum(dov.astype(F32) * o_ref[:, hs].astype(F32), axis=-1, keepdims=True)
            dq = None
            for w, lo, hi in chunks:
                kc_v = k_refs[w][lo:hi, hs]
                p = jnp.exp2(_mm_nt(qv, kc_v) * _SCALE_LOG2E - lse)
                ds = (p * (_mm_nt(dov, v_refs[w][lo:hi, hs]) - delta)).astype(BF16)
                part = _mm(ds, kc_v)
                dq = part if dq is None else dq + part
                ak[w][hs, lo:hi] += _mm_tn(qv, ds)
                av[w][hs, lo:hi] += _mm_tn(dov, p.astype(BF16))
            dq_ref[:, hs] = dq * _SCALE

        @pl.when(t == tpe - 1)
        def _():
            dkl_ref[...] = akl[...].T * _SCALE
            dkc_ref[...] = akc[...].T * _SCALE
            dvl_ref[...] = avl[...].T
            dvc_ref[...] = avc[...].T

    qs = pl.BlockSpec((tq, hp * LANE), lambda i, j, t: (i * tpe + t, j))
    kl = pl.BlockSpec((s, hp * LANE), lambda i, j, t: (i, j))
    kc = pl.BlockSpec((nc, hp * LANE), lambda i, j, t: (r_lat // nc + i, j))
    kc_out = pl.BlockSpec((nc, hp * LANE), lambda i, j, t: (i, j))
    ls = pl.BlockSpec((hp, 8, tq), lambda i, j, t: (i * (H // hp) + j, 0, t))
    return _pcall(
        body, name="attn_bwd", grid=(nb, H // hp, tpe),
        out_shape=(_sds((r_lat, HP), F32), _sds((r_lat, HP), F32), _sds((nb * nc, HP), F32),
                   _sds((r_lat, HP), F32), _sds((nb * nc, HP), F32), _sds((3,) + part.shape[1:], part.dtype)),
        in_specs=[qs, kl, kc, kl, kc, qs, qs, ls, ANY], out_specs=(qs, kl, kc_out, kl, kc_out, ANY),
        scratch=[pltpu.VMEM((hp * LANE, s), F32), pltpu.VMEM((hp * LANE, nc), F32)] * 2
        + [pltpu.SemaphoreType.DMA((3,))] * 2,
        vmem_mb=60)(q, k, k, v, v, o, do, lse, part)


def _gating(vn, ws_ref, bias_ref, s_scr, tm):
    lane = lax.broadcasted_iota(jnp.int32, (CH, LANE), 1)
    for c in range(tm // CH):
        rs = slice(c * CH, (c + 1) * CH)
        for j in range(G // 2):
            ls = slice(j * LANE, (j + 1) * LANE)
            vp = vn[rs, ls]
            s_scr[rs, ls] = jnp.where(lane < GD, _mm(ws_ref[2 * j], vp), _mm(ws_ref[2 * j + 1], vp)) + bias_ref[:, ls]


def _mix_fwd(u, v, attn, x1, gate, wv, ws, bias, wout, cs, *, tm, n_lat, tpe):
    nrows = gate.shape[0]

    def body(u_ref, v_ref, attn_ref, x_ref, gate_ref, wv_ref, ws_ref, bias_ref, wout_ref, seg, segt,
             x2_ref, mix_ref, s_scr):
        vg = _gelu(v_ref[...])
        rg = lax.rsqrt(_seg_sum(vg * vg, seg[...]) * (1.0 / GD) + EPS)
        vn = (vg * _seg_bcast(rg, segt[...]) * wv_ref[...]).astype(BF16)
        _gating(vn, ws_ref, bias_ref, s_scr, tm)
        sg = (_gelu(u_ref[...]) * s_scr[...]).astype(BF16)
        mix = _mm(attn_ref[...], wout_ref[0:HP, :]) + _mm(sg, wout_ref[HP:, :])
        mix_ref[...] = mix.astype(BF16)
        x2_ref[...] = x_ref[...] + gate_ref[0] * mix

    row = lambda cols: pl.BlockSpec((tm, cols), lambda t: (t, 0))
    r = n_lat * tm
    return _pcall(
        body, name="mix_fwd", grid=(n_lat,),
        out_shape=(_sds((r, D), F32), _sds((r, D), BF16)),
        in_specs=[row(G * GD), row(G * GD), row(HP), row(D), _mod_spec(1, tpe, nrows), _const((1, G * GD)),
                  _const((G, CH, CH)), _const((CH, G * GD)), _const((HP + G * GD, D)), _const((G * GD, LANE)),
                  _const((2 * LANE, G * GD))],
        out_specs=(row(D), row(D)), scratch=[pltpu.VMEM((tm, G * GD), F32)], vmem_mb=40)(
            u, v, attn, x1, gate, wv, ws, bias, wout, cs["seg_g"], cs["seg_gt"])


def _mix_bwd(dx2, mix, u, v, attn, gate, wv, ws, wst, bias, wout, cs, *, tm, n_lat, tpe):
    nrows = gate.shape[0]
    wrows = HP + G * GD

    def body(dx2_ref, mix_ref, u_ref, v_ref, attn_ref, gate_ref, wv_ref, ws_ref, wst_ref, bias_ref, wout_ref, seg, segt,
             dattn_ref, du_ref, dv_ref, dgate_ref, dwout_ref, dws_ref, dbs_ref, dwv_ref, s_scr, dvn_scr, dbias_scr):
        t = pl.program_id(0)
        dx2 = dx2_ref[...]
        dmix = (dx2 * gate_ref[0]).astype(BF16)
        dcat = _mm_nt(dmix, wout_ref[...])
        dattn_ref[...] = dcat[:, :HP].astype(BF16)
        dsg = dcat[:, HP:]

        vraw = v_ref[...]
        vg = _gelu(vraw)
        rg = lax.rsqrt(_seg_sum(vg * vg, seg[...]) * (1.0 / GD) + EPS)
        r64 = _seg_bcast(rg, segt[...])
        y = vg * r64
        wv_v = wv_ref[...]
        vn = (y * wv_v).astype(BF16)
        _gating(vn, ws_ref, bias_ref, s_scr, tm)
        uraw = u_ref[...]
        ug = _gelu(uraw)
        s = s_scr[...]
        sg = (ug * s).astype(BF16)
        du_ref[...] = (dsg * s * _gelu_grad(uraw)).astype(BF16)
        ds = dsg * ug

        @pl.when(t == 0)
        def _():
            dwout_ref[...] = jnp.zeros_like(dwout_ref)
            dws_ref[...] = jnp.zeros_like(dws_ref)
            dwv_ref[...] = jnp.zeros_like(dwv_ref)
            dbias_scr[...] = jnp.zeros_like(dbias_scr)

        @pl.when(t % tpe == 0)
        def _():
            dgate_ref[...] = jnp.zeros_like(dgate_ref)

        dgate_ref[0] += _rowsum(dx2 * mix_ref[...].astype(F32))
        dwout_ref[...] += _mm_tn(jnp.concatenate([attn_ref[...], sg], axis=-1), dmix)

        lane = lax.broadcasted_iota(jnp.int32, (CH, LANE), 1)
        for c in range(tm // CH):
            rs = slice(c * CH, (c + 1) * CH)
            dbias_scr[...] += ds[rs, :]
            for j in range(G // 2):
                ls = slice(j * LANE, (j + 1) * LANE)
                dsp32 = ds[rs, ls]
                dsp = dsp32.astype(BF16)
                vp = vn[rs, ls]
                dvn_scr[rs, ls] = jnp.where(lane < GD, _mm(wst_ref[2 * j], dsp), _mm(wst_ref[2 * j + 1], dsp))
                dws_ref[2 * j] += _mm_nt(jnp.where(lane < GD, dsp32, 0.0).astype(BF16), vp)
                dws_ref[2 * j + 1] += _mm_nt(jnp.where(lane < GD, 0.0, dsp32).astype(BF16), vp)

        dvn = dvn_scr[...]
        dwv_ref[...] += _rowsum(dvn * y)
        dy = dvn * wv_v
        mean_g = _seg_sum(dy * y, seg[...]) * (1.0 / GD)
        dvg = r64 * (dy - y * _seg_bcast(mean_g, segt[...]))
        dv_ref[...] = (dvg * _gelu_grad(vraw)).astype(BF16)

        @pl.when(t == n_lat - 1)
        def _():
            dbs_ref[...] = _dot_hl(dbias_scr[...], seg[...])

    row = lambda cols: pl.BlockSpec((tm, cols), lambda t: (t, 0))
    r = n_lat * tm
    return _pcall(
        body, name="mix_bwd", grid=(n_lat,),
        out_shape=(_sds((r, HP), BF16), _sds((r, G * GD), BF16), _sds((r, G * GD), BF16), _sds((nrows, 1, D), F32),
                   _sds((wrows, D), F32), _sds((G, CH, CH), F32), _sds((CH, LANE), F32), _sds((1, G * GD), F32)),
        in_specs=[row(D), row(D), row(G * GD), row(G * GD), row(HP), _mod_spec(1, tpe, nrows), _const((1, G * GD)),
                  _const((G, CH, CH)), _const((G, CH, CH)), _const((CH, G * GD)), _const((wrows, D)),
                  _const((G * GD, LANE)), _const((2 * LANE, G * GD))],
        out_specs=(row(HP), row(G * GD), row(G * GD), _mod_spec(1, tpe, nrows), _const((wrows, D)),
                   _const((G, CH, CH)), _const((CH, LANE)), _const((1, G * GD))),
        scratch=[pltpu.VMEM((tm, G * GD), F32), pltpu.VMEM((tm, G * GD), F32), pltpu.VMEM((CH, G * GD), F32)],
        vmem_mb=56)(dx2, mix, u, v, attn, gate, wv, ws, wst, bias, wout, cs["seg_g"], cs["seg_gt"])


def _adamw_math(w, g, m, v):
    m2 = ADAM_B1 * m + (1.0 - ADAM_B1) * g
    v2 = ADAM_B2 * v + (1.0 - ADAM_B2) * (g * g)
    m_hat = m2 / (1.0 - ADAM_B1 ** ADAM_STEP)
    v_hat = v2 / (1.0 - ADAM_B2 ** ADAM_STEP)
    delta = -ADAM_LR * (m_hat / (jnp.sqrt(v_hat) + ADAM_EPS) + ADAM_WD * w)
    return delta, m2, v2


def _row_tile(r, c):
    best = r
    for tr in range(8, r, 8):
        if r % tr == 0 and tr * c * 4 <= MIB:
            best = tr
    return best


def _adamw(w, g, m, v, name):
    r, c = w.shape
    tr = _row_tile(r, c)

    def body(w_ref, g_ref, m_ref, v_ref, d_ref, mo_ref, vo_ref):
        d_ref[...], mo_ref[...], vo_ref[...] = _adamw_math(w_ref[...], g_ref[...], m_ref[...], v_ref[...])

    blk = pl.BlockSpec((tr, c), lambda t: (t, 0))
    return _pcall(body, name=name, grid=(r // tr,), out_shape=(_sds((r, c), F32),) * 3,
                  in_specs=[blk] * 4, out_specs=(blk,) * 3)(w, g, m, v)


def _adamw_small(params):
    n = len(params)

    def body(*refs):
        ins, outs = refs[:4 * n], refs[4 * n:]
        for i in range(n):
            w, g, m, v = (ins[4 * i + k][...] for k in range(4))
            if i == 0:
                sig = _sigmoid(w)
                g = g * (sig * (1.0 + w * (1.0 - sig)))
            d, m2, v2 = _adamw_math(w, g, m, v)
            outs[4 * i][...] = g
            outs[4 * i + 1][...] = d
            outs[4 * i + 2][...] = m2
            outs[4 * i + 3][...] = v2

    flat = [a for p in params for a in p]
    out_shape = tuple(_sds(p[0].shape, F32) for p in params for _ in range(4))
    res = _pcall(body, name="adamw_small", out_shape=out_shape, in_specs=[VMEM] * (4 * n),
                 out_specs=(VMEM,) * (4 * n))(*flat)
    return [res[4 * i:4 * i + 4] for i in range(n)]


def _rope_tables(s):
    rows = jnp.repeat(jnp.arange(s // GRID_W, dtype=F32), GRID_W)
    cols = jnp.tile(jnp.arange(GRID_W, dtype=F32), s // GRID_W)
    half = DR // 2
    inv = ROPE_BASE ** (-jnp.arange(0, half, 2, dtype=F32) / half)
    ang_r = rows[:, None] * inv
    ang_c = cols[:, None] * inv
    ang = jnp.concatenate([ang_r, ang_r, ang_c, ang_c], axis=-1)
    return jnp.cos(ang), jnp.sin(ang)


def _head_pad(a, real):
    return jnp.pad(a, ((0, 0), (0, LANE - real), (0, 0))).reshape(HP, a.shape[2])


def kernel(x, c, ctx, c_ctx, w_ada, b_ada, norm1_w, ffn1_w1, ffn1_w3, ffn1_w2, norm2_w, w_in, q_a_norm_w, w_uq, kv_a_norm_w, w_ukv, q_norm_w, k_norm_w, v_norm_w, w_s, b_s, w_out, norm3_w, ffn2_w1, ffn2_w3, ffn2_w2, loss_target, m_c_ctx, m_w_ada, m_b_ada, m_norm1_w, m_ffn1_w1, m_ffn1_w3, m_ffn1_w2, m_norm2_w, m_w_in, m_q_a_norm_w, m_w_uq, m_kv_a_norm_w, m_w_ukv, m_q_norm_w, m_k_norm_w, m_v_norm_w, m_w_s, m_b_s, m_w_out, m_norm3_w, m_ffn2_w1, m_ffn2_w3, m_ffn2_w2, v_c_ctx, v_w_ada, v_b_ada, v_norm1_w, v_ffn1_w1, v_ffn1_w3, v_ffn1_w2, v_norm2_w, v_w_in, v_q_a_norm_w, v_w_uq, v_kv_a_norm_w, v_w_ukv, v_q_norm_w, v_k_norm_w, v_v_norm_w, v_w_s, v_b_s, v_w_out, v_norm3_w, v_ffn2_w1, v_ffn2_w3, v_ffn2_w2):
    nb, s, _ = x.shape
    nc = ctx.shape[1]
    tm = 256 if nc % 256 == 0 else 128
    tpe = s // tm
    n_lat = nb * tpe
    n_all = n_lat + nb * nc // tm
    tmf = 2 * tm if s % (2 * tm) == 0 and (nb * nc) % (2 * tm) == 0 else tm
    tp = tmf
    r_lat = nb * s
    tpe_p, n_lat_p, n_all_p = s // tp, r_lat // tp, (r_lat + nb * nc) // tp
    me = 4 * lax.axis_index("x") + 2 * lax.axis_index("y") + lax.axis_index("c")
    cs = _consts()
    ncol = w_ada.shape[2]
    fsh = ffn1_w1.shape[2]
    assert nb + 1 <= 8 and NDEV * fsh == FF and NDEV * ncol == NMOD * D and s % nc == 0 and nc % tm == 0

    def t16(a):
        return a.T.astype(BF16)

    wpack1 = jnp.concatenate([t16(ffn1_w1[0]), t16(ffn1_w3[0]), ffn1_w2[0].astype(BF16)], axis=0)
    a_loc = jnp.concatenate([c, c_ctx[None, :], jnp.zeros((7 - nb, D), F32)], axis=0)
    a_raw, _, mod_all, wall1 = _ada_front(a_loc, w_ada[0], lax.dynamic_slice_in_dim(b_ada, me * ncol, ncol, axis=1),
                                          wpack1)
    a_raw = a_raw.reshape(NDEV * 8, D)
    mod_mine = lax.dynamic_slice_in_dim(mod_all, 8 * me, 8, axis=1)
    modtab = mod_mine.transpose(1, 0, 2).reshape(8, NMOD, D)[:nb + 1]
    wpack2 = jnp.concatenate([
        t16(ffn2_w1[0]), t16(ffn2_w3[0]), ffn2_w2[0].astype(BF16),
        t16(w_in[0]), jnp.zeros((12, D), BF16),
        w_out[0].astype(BF16),
        t16(w_uq[0]).reshape(24, D), jnp.zeros((8, D), BF16),
        t16(w_ukv[0]).reshape(16, D)], axis=0)

    def head_w(wn):
        return jnp.tile(jnp.pad(wn, ((0, 0), (0, LANE - DH))), (1, H))

    wq, wk = head_w(q_norm_w), head_w(k_norm_w)
    wv = v_norm_w.reshape(1, G * GD)
    ws16 = w_s[0].astype(BF16)
    wst16 = w_s[0].transpose(0, 2, 1).astype(BF16)
    bias = jnp.repeat(b_s[0].T, GD, axis=1)
    cos, sin = _rope_tables(s)
    cos = jnp.pad(cos, ((0, 0), (DN, LANE - DH)), constant_values=1.0)
    sin = jnp.pad(sin, ((0, 0), (DN, LANE - DH)))
    cos_k = jnp.concatenate([cos, jnp.ones((tm, LANE), F32)], axis=0)
    sin_k = jnp.concatenate([sin, jnp.zeros((tm, LANE), F32)], axis=0)

    xs = (x.reshape(r_lat, D), ctx.reshape(nb * nc, D))
    x1, a1, b1, o1, wall2 = _ffn_fwd(xs, modtab[:, 0:3], norm1_w, wall1, 0, tm=tmf, n_tiles=(r_lat + nb * nc) // tmf,
                                     tpe=s // tmf, n_lat=r_lat // tmf, name="ffn1_fwd", gather=wpack2)

    o0 = 3 * fsh
    wint = wall2[:, o0:o0 + 180].reshape(IN_COLS, D)
    z = lambda n: jnp.zeros((n, D), BF16)
    wint = jnp.concatenate([wint[0:128], wint[160:416], wint[416:928], wint[928:1440],
                            z(DN), wint[128:160], z(LANE - DH)], axis=0)
    wout = wall2[:, o0 + 192:o0 + 320].reshape(D, D)
    wout = jnp.concatenate([_head_pad(wout[:H * DV].reshape(H, DV, D), DV), wout[H * DV:]], axis=0)
    wuq = _head_pad(wall2[:, o0 + 320:o0 + 344].reshape(H, DH, QL), DH)
    wukvt = wall2[:, o0 + 352:o0 + 368].reshape(H, DN + DV, KVL)
    wukv = jnp.concatenate([_head_pad(wukvt[:, :DN], DN), _head_pad(wukvt[:, DN:], DV)], axis=0)

    ckv, qp, u_raw, v_raw, kpe = _proj_fwd(x1, modtab[:, 3:5], norm2_w, wint, tm=tp, n_tiles=n_all_p, tpe=tpe_p)
    q = _q_prep_fwd(qp, q_a_norm_w, wuq, wq, cos, sin, cs, tm=tm, n_lat=n_lat, tpe=tpe)
    k, v = _kv_prep_fwd(ckv, kpe, kv_a_norm_w, wukv, wk, cos_k, sin_k, cs,
                        tm=tm, n_tiles=n_all, tpe=tpe, n_lat=n_lat)
    attn, lse = _attn_fwd(q, k, v, nb=nb, s=s, nc=nc, tq=tm, ck=512)
    x2, mix = _mix_fwd(u_raw, v_raw, attn, x1, modtab[:nb, 5:6], wv, ws16, bias, wout, cs,
                       tm=tp, n_lat=n_lat_p, tpe=tpe_p)
    dy, a2, b2, o2, lsum = _ffn_fwd((x2,), modtab[:nb, 6:9], norm3_w, wall2, 0, tm=tmf, n_tiles=r_lat // tmf,
                                    tpe=s // tmf, n_lat=r_lat // tmf, name="ffn2_fwd",
                                    target=loss_target.reshape(r_lat, D))
    loss = lax.psum(lsum[0, 0] * (0.5 / D), ("x", "y", "c"))

    tr = 2 * tm if n_lat % 2 == 0 and n_all % 2 == 0 else tm
    dx2, da2, db2, g2, do2, h2, dmod678, dnorm3 = _ffn_bwd_dx(
        dy, (x2,), a2, b2, o2, modtab[:nb, 6:9], norm3_w, wall2, 0,
        tm=tm, n_tiles=n_lat, tpe=tpe, n_lat=n_lat, name="ffn2_bwd_dx")
    g_ffn2 = _ffn_bwd_dw(h2, do2, da2, db2, g2, tr=tr, name="ffn2_bwd_dw")
    part_ffn2 = _add_sibling(g_ffn2, _scatter_sibling([g_ffn2], "scatter_sibling_ffn2")[0], 176, "add_sibling_ffn2")

    dattn, du, dv, dgate5, dwout, dws, dbs, dwv = _mix_bwd(
        dx2, mix, u_raw, v_raw, attn, modtab[:nb, 5:6], wv, ws16, wst16, bias, wout, cs, tm=tp, n_lat=n_lat_p, tpe=tpe_p)
    tq = 2 * tm if s % (2 * tm) == 0 else tm
    dq, dk_l, dk_c, dv_l, dv_c, recv_ffn2 = _attn_bwd(q, k, v, attn, dattn, lse, part_ffn2,
                                                      nb=nb, s=s, nc=nc, tq=tq, ck=1024)
    dqp, dwuq, dqa, dwq = _q_prep_bwd(dq, qp, q_a_norm_w, wuq, wq, cos, sin, cs, tm=tm, n_lat=n_lat, tpe=tpe)
    dckv, dkpe, dwukv, dkva, dwk = _kv_prep_bwd((dk_l, dk_c), (dv_l, dv_c), ckv, kpe, kv_a_norm_w, wukv, wk,
                                                cos_k, sin_k, cs, tm=tm, n_tiles=n_all, tpe=tpe, n_lat=n_lat)
    dx1, dwin, dmod34, dnorm2 = _proj_bwd(dckv, dkpe, dqp, du, dv, dx2, x1, modtab[:, 3:5], norm2_w, wint,
                                          tm=tp, n_tiles=n_all_p, tpe=tpe_p, n_lat=n_lat_p)
    dx0, da1, db1, g1, do1, h1, dmod012, dnorm1 = _ffn_bwd_dx(
        dx1, xs, a1, b1, o1, modtab[:, 0:3], norm1_w, wall1, 0,
        tm=tm, n_tiles=n_all, tpe=tpe, n_lat=n_lat, name="ffn1_bwd_dx")
    g_ffn1 = _ffn_bwd_dw(h1, do1, da1, db1, g1, tr=tr, name="ffn1_bwd_dw")
    grad_x = dx0.reshape(nb, s, D)

    zrow = jnp.zeros((1, D), F32)
    g_lat = jnp.concatenate([dmod012[:nb, 0], dmod012[:nb, 1], dmod012[:nb, 2], dmod34[:nb, 0], dmod34[:nb, 1],
                             dgate5[:, 0], dmod678[:, 0], dmod678[:, 1], dmod678[:, 2]], axis=1)
    g_ctx = jnp.concatenate([dmod012[nb:, 0], dmod012[nb:, 1], dmod012[nb:, 2], dmod34[nb:, 0], dmod34[nb:, 1],
                             zrow, zrow, zrow, zrow], axis=1)
    g_loc = jnp.concatenate([g_lat, g_ctx, jnp.zeros((7 - nb, NMOD * D), F32)], axis=0)

    def blocks(a):
        return a.reshape(NDEV, a.shape[0] // NDEV, D)

    dwin_o = jnp.concatenate([dwin[0:128], dwin[KPE_LO:KPE_LO + DR], dwin[128:384], dwin[384:896], dwin[896:1408]],
                             axis=0)
    dwout_o = jnp.concatenate([dwout[:HP].reshape(H, LANE, D)[:, :DV].reshape(H * DV, D), dwout[HP:]], axis=0)
    dwuq_o = dwuq.reshape(H, LANE, QL)[:, :DH]
    dwukv_o = jnp.concatenate([dwukv[:HP].reshape(H, LANE, KVL)[:, :DN], dwukv[HP:].reshape(H, LANE, KVL)[:, :DV]],
                              axis=1)
    gmisc = jnp.concatenate([
        blocks(dwin_o).astype(BF16), jnp.zeros((NDEV, 12, D), BF16),
        blocks(dwout_o).astype(BF16),
        dwuq_o.reshape(NDEV, 24, D).astype(BF16), jnp.zeros((NDEV, 8, D), BF16),
        dwukv_o.reshape(NDEV, 16, D).astype(BF16)], axis=1)
    got_ffn1, got_misc, g_all = _scatter_sibling([g_ffn1, gmisc], "scatter_sibling", gather=g_loc)
    g_all = g_all.reshape(NDEV * 8, NMOD * D)
    g_cols = lax.dynamic_slice_in_dim(g_all, me * ncol, ncol, axis=1)
    g_w_ada, pc_ctx, g_b_ada = _ada_bwd(a_raw, c_ctx.reshape(D, 1), g_all, g_cols, w_ada[0], nb)
    parts = [_add_sibling(g_ffn1, got_ffn1, 176, "add_sibling_ffn1"), _add_sibling(gmisc, got_misc, 368, "add_sibling_misc")]

    def prow(a):
        a = a.reshape(1, -1)
        return jnp.concatenate([a, jnp.zeros((1, D - a.shape[1]), F32)], axis=1)

    g_qn = dwq.reshape(H, LANE)[:, :DH].sum(0)
    g_kn = dwk.reshape(H, LANE)[:, :DH].sum(0)
    spack = jnp.concatenate([
        dnorm1, dnorm2, dnorm3, prow(dqa), prow(dkva), prow(g_qn), prow(g_kn), prow(dwv),
        prow(dbs[:, :G].T), prow(pc_ctx), jnp.zeros((6, D), F32), dws.reshape(CH, D)], axis=0)
    recv_ffn1, recv_misc, small_all = _scatter_chips(parts, "scatter_chips", gather=spack)
    recv = (recv_ffn1, recv_misc)
    ssum = _sum_slots(small_all, 144, "sum_small")
    gsum1 = _sum_chips(parts[0], recv[0], 176, "sum_grads_ffn1")
    gsum2 = _sum_chips(part_ffn2, recv_ffn2, 176, "sum_grads_ffn2")
    msum = _sum_chips(parts[1], recv[1], 368, "sum_grads_misc")

    g_big = {
        "ffn1_w1": gsum1[0:fsh].T, "ffn1_w3": gsum1[fsh:2 * fsh].T, "ffn1_w2": gsum1[2 * fsh:3 * fsh],
        "ffn2_w1": gsum2[0:fsh].T, "ffn2_w3": gsum2[fsh:2 * fsh].T, "ffn2_w2": gsum2[2 * fsh:3 * fsh],
        "w_in": msum[0:180].T, "w_out": msum[192:320],
        "w_uq": msum[320:344].reshape(DH, QL).T, "w_ukv": msum[352:368].reshape(DN + DV, KVL).T,
        "w_ada": g_w_ada,
    }

    big_in = {
        "w_ada": (w_ada, m_w_ada, v_w_ada), "ffn1_w1": (ffn1_w1, m_ffn1_w1, v_ffn1_w1),
        "ffn1_w3": (ffn1_w3, m_ffn1_w3, v_ffn1_w3), "ffn1_w2": (ffn1_w2, m_ffn1_w2, v_ffn1_w2),
        "w_in": (w_in, m_w_in, v_w_in), "w_uq": (w_uq, m_w_uq, v_w_uq), "w_ukv": (w_ukv, m_w_ukv, v_w_ukv),
        "w_out": (w_out, m_w_out, v_w_out), "ffn2_w1": (ffn2_w1, m_ffn2_w1, v_ffn2_w1),
        "ffn2_w3": (ffn2_w3, m_ffn2_w3, v_ffn2_w3), "ffn2_w2": (ffn2_w2, m_ffn2_w2, v_ffn2_w2),
    }
    res = {}
    for nm, (w, m, v_) in big_in.items():
        g = g_big[nm]
        d_, m_, v2_ = _adamw(w[0], g, m[0], v_[0], "adamw_" + nm)
        res[nm] = tuple(a[None] for a in (g, d_, m_, v2_))

    small_in = [
        ("c_ctx", c_ctx, m_c_ctx, v_c_ctx, ssum[9:10], (1, D)),
        ("b_ada", b_ada, m_b_ada, v_b_ada, g_b_ada, (1, NMOD * D)),
        ("norm1_w", norm1_w, m_norm1_w, v_norm1_w, ssum[0:1], (1, D)),
        ("norm2_w", norm2_w, m_norm2_w, v_norm2_w, ssum[1:2], (1, D)),
        ("norm3_w", norm3_w, m_norm3_w, v_norm3_w, ssum[2:3], (1, D)),
        ("q_a_norm_w", q_a_norm_w, m_q_a_norm_w, v_q_a_norm_w, ssum[3:4, :QL], (1, QL)),
        ("kv_a_norm_w", kv_a_norm_w, m_kv_a_norm_w, v_kv_a_norm_w, ssum[4:5, :KVL], (1, KVL)),
        ("q_norm_w", q_norm_w, m_q_norm_w, v_q_norm_w, ssum[5:6, :DH], (1, DH)),
        ("k_norm_w", k_norm_w, m_k_norm_w, v_k_norm_w, ssum[6:7, :DH], (1, DH)),
        ("v_norm_w", v_norm_w, m_v_norm_w, v_v_norm_w, ssum[7:8, :G * GD], (G, GD)),
        ("b_s", b_s, m_b_s, v_b_s, ssum[8:9], (G, CH)),
        ("w_s", w_s, m_w_s, v_w_s, ssum[16:144], (G * CH, CH)),
    ]
    small_out = _adamw_small(
        [(w.reshape(sh), g.reshape(sh), m.reshape(sh), v_.reshape(sh)) for _, w, m, v_, g, sh in small_in])
    for (nm, w, *_), outs in zip(small_in, small_out):
        res[nm] = tuple(a.reshape(w.shape) for a in outs)

    order = ["c_ctx", "w_ada", "b_ada", "norm1_w", "ffn1_w1", "ffn1_w3", "ffn1_w2", "norm2_w", "w_in", "q_a_norm_w",
             "w_uq", "kv_a_norm_w", "w_ukv", "q_norm_w", "k_norm_w", "v_norm_w", "w_s", "b_s", "w_out", "norm3_w",
             "ffn2_w1", "ffn2_w3", "ffn2_w2"]
    return (loss, grad_x, *[res[n][0] for n in order], *[res[n][1] for n in order],
            *[res[n][2] for n in order], *[res[n][3] for n in order])
```

```python
import numpy as np
import jax
import jax.numpy as jnp
from jax import lax
from jax.experimental import pallas as pl
from jax.experimental.pallas import tpu as pltpu

F32 = jnp.float32
BF16 = jnp.bfloat16

D = 1024
FF = 2816
FC = 256
H = 8
DN, DR, DV = 64, 32, 64
DH = DN + DR
QL, KVL = 256, 128
G, GD, CH = 8, 64, 128
NMOD = 9
EPS = 1e-6
GRID_W = 64
ROPE_BASE = 10000.0
NDEV = 8
LANE = 128
HP = H * LANE
IN_COLS = 1440
WIN_ROWS = 1536
KPE_LO = 1408 + DN
MIB = 1 << 20

ADAM_LR, ADAM_B1, ADAM_B2, ADAM_EPS, ADAM_WD, ADAM_STEP = 0.001, 0.9, 0.999, 1e-08, 0.01, 10

MESH = pl.DeviceIdType.MESH
ANY = pl.BlockSpec(memory_space=pl.ANY)
VMEM = pl.BlockSpec(memory_space=pltpu.VMEM)


def _mm(a, b):
    return jnp.dot(a, b, preferred_element_type=F32)


def _mm_nt(a, b):
    return lax.dot_general(a, b, (((1,), (1,)), ((), ())), preferred_element_type=F32)


def _mm_tn(a, b):
    return lax.dot_general(a, b, (((0,), (0,)), ((), ())), preferred_element_type=F32)


def _dot_hl(x, m):
    hi = x.astype(BF16)
    lo = (x - hi.astype(F32)).astype(BF16)
    return _mm(hi, m) + _mm(lo, m)


def _sigmoid(a):
    return 1.0 / (1.0 + jnp.exp(-a))


_G0 = 0.7978845608028654
_G1 = 0.044715


def _gelu(x):
    return 0.5 * x * (1.0 + jnp.tanh(_G0 * (x + _G1 * (x * x * x))))


def _gelu_grad(x):
    th = jnp.tanh(_G0 * (x + _G1 * (x * x * x)))
    return 0.5 * (1.0 + th) + 0.5 * x * (1.0 - th * th) * (_G0 * (1.0 + 3.0 * _G1 * x * x))


def _rowsum(y):
    return jnp.sum(y, axis=0, keepdims=True)


def _rms(x):
    return lax.rsqrt(jnp.mean(x * x, axis=-1, keepdims=True) + EPS)


def _pcall(body, *, name, out_shape, in_specs, out_specs, grid=None, scratch=(), vmem_mb=32, aliases=None):
    kw = {}
    if grid is not None:
        kw["grid"] = grid
        sem = ("arbitrary",) * len(grid)
    else:
        sem = None
    if aliases:
        kw["input_output_aliases"] = aliases
    return pl.pallas_call(
        body, name=name, out_shape=out_shape, in_specs=in_specs, out_specs=out_specs,
        scratch_shapes=list(scratch),
        compiler_params=pltpu.CompilerParams(dimension_semantics=sem, vmem_limit_bytes=vmem_mb * MIB),
        **kw)


def _const(shape):
    nd = len(shape)
    return pl.BlockSpec(shape, lambda *_: (0,) * nd)


def _sds(shape, dt):
    return jax.ShapeDtypeStruct(shape, dt)


def _consts():
    seg_h = np.zeros((HP, LANE), np.float32)
    seg_h[np.arange(HP), np.arange(HP) // LANE] = 1.0
    seg_g = np.zeros((G * GD, LANE), np.float32)
    seg_g[np.arange(G * GD), np.arange(G * GD) // GD] = 1.0
    rot = np.zeros((LANE, LANE), np.float32)
    for base in (DN, DN + 16):
        for j in range(8):
            rot[base + j + 8, base + j] = -1.0
            rot[base + j, base + j + 8] = 1.0
    rot2 = np.zeros((2 * LANE, 2 * LANE), np.float32)
    rot2[:LANE, :LANE] = rot
    rot2[LANE:, LANE:] = rot
    twice = lambda m: np.concatenate([m, m], axis=0)
    c = dict(seg_h=seg_h, seg_ht=twice(seg_h.T), seg_g=seg_g, seg_gt=twice(seg_g.T), rot=rot2, rot_t=rot2.T)
    return {k: jnp.asarray(v, BF16) for k, v in c.items()}


_GATHER_SEMS = [pltpu.SemaphoreType.DMA((7,)), pltpu.SemaphoreType.DMA((7,)), pltpu.SemaphoreType.DMA(())]


def _gather_phases(x_ref, out_ref, send_sems, recv_sems, local_sem):
    mx, my, mc = lax.axis_index("x"), lax.axis_index("y"), lax.axis_index("c")
    me, sibling = (mx, my, mc), (mx, my, 1 - mc)
    chips = [(1 - mx, my), (mx, 1 - my), (1 - mx, 1 - my)]

    def blk(px, py, pc):
        return out_ref.at[4 * px + 2 * py + pc]

    def copy(k, block, to, src=None):
        return pltpu.make_async_remote_copy(
            src_ref=blk(*block) if src is None else src, dst_ref=blk(*block),
            send_sem=send_sems.at[k], recv_sem=recv_sems.at[k], device_id=to, device_id_type=MESH)

    mine = pltpu.make_async_copy(x_ref, blk(*me), local_sem)
    first = [copy(0, me, sibling, src=x_ref)]
    first += [copy(1 + j, me, (*chip, mc), src=x_ref) for j, chip in enumerate(chips)]
    passed = [copy(4 + j, (*chip, mc), sibling) for j, chip in enumerate(chips)]

    def start():
        mine.start()
        for cp in first:
            cp.start()

    def forward():
        for j, chip in enumerate(chips):
            copy(1 + j, (*chip, mc), me).wait_recv()
            passed[j].start()

    def finish():
        copy(0, sibling, me).wait_recv()
        for j, chip in enumerate(chips):
            copy(4 + j, (*chip, 1 - mc), me).wait_recv()
        for cp in first + passed:
            cp.wait_send()
        mine.wait()

    return start, forward, finish


def _chip_sends(p_ref, out_ref, send_sems, recv_sems):
    mx, my, mc = lax.axis_index("x"), lax.axis_index("y"), lax.axis_index("c")
    peers = [(1 - mx, my), (mx, 1 - my), (1 - mx, 1 - my)]
    return [pltpu.make_async_remote_copy(
        src_ref=p_ref.at[2 * px + py], dst_ref=out_ref.at[j], send_sem=send_sems.at[j], recv_sem=recv_sems.at[j],
        device_id=(px, py, mc), device_id_type=MESH) for j, (px, py) in enumerate(peers)]


def _with_gather(copies_of, n, shapes, sems, gather, name, args):
    ns = len(sems)

    def body(*refs):
        ng = 1 if gather is not None else 0
        ins, outs = refs[:n], refs[n + ng:2 * n + ng]
        copies = copies_of(ins, outs, refs[2 * n + 2 * ng:2 * n + 2 * ng + ns])
        if ng:
            start, forward, finish = _gather_phases(refs[n], refs[2 * n + 1], *refs[2 * n + 2 + ns:])
            start()
        for cp in copies:
            cp.start()
        if ng:
            forward()
        for cp in copies:
            cp.wait_recv()
        for cp in copies:
            cp.wait_send()
        if ng:
            finish()

    in_specs, out_shape, scratch = [ANY] * n, list(shapes), list(sems)
    if gather is not None:
        in_specs.append(ANY)
        args = list(args) + [gather]
        out_shape.append(_sds((NDEV,) + gather.shape, gather.dtype))
        scratch += _GATHER_SEMS
    return pl.pallas_call(body, name=name, out_shape=tuple(out_shape), in_specs=in_specs,
                          out_specs=(ANY,) * len(out_shape), scratch_shapes=scratch)(*args)


def _scatter_sibling(xs, name, gather=None):
    n = len(xs)

    def copies_of(x_refs, got_refs, sems):
        send_sems, recv_sems = sems
        mx, my, mc = lax.axis_index("x"), lax.axis_index("y"), lax.axis_index("c")
        return [pltpu.make_async_remote_copy(
            src_ref=x_refs[i].at[2 * j + 1 - mc], dst_ref=got_refs[i].at[j],
            send_sem=send_sems.at[4 * i + j], recv_sem=recv_sems.at[4 * i + j],
            device_id=(mx, my, 1 - mc), device_id_type=MESH) for i in range(n) for j in range(4)]

    shapes = tuple(_sds((4,) + x.shape[1:], x.dtype) for x in xs)
    return _with_gather(copies_of, n, shapes, [pltpu.SemaphoreType.DMA((4 * n,))] * 2, gather, name, xs)


def _scatter_chips(ps, name, gather=None):
    n = len(ps)

    def copies_of(p_refs, out_refs, sems):
        sends = []
        for i in range(n):
            sends += _chip_sends(p_refs[i], out_refs[i], sems[2 * i], sems[2 * i + 1])
        return sends

    shapes = tuple(_sds((3,) + p.shape[1:], p.dtype) for p in ps)
    return _with_gather(copies_of, n, shapes, [pltpu.SemaphoreType.DMA((3,))] * (2 * n), gather, name, ps)


def _add_sibling(x, got, tr, name):
    _, r, c = x.shape

    def body(x_ref, g_ref, o_ref):
        mc = lax.axis_index("c")
        for j in range(4):
            mine = jnp.where(mc == 0, x_ref[2 * j].astype(F32), x_ref[2 * j + 1].astype(F32))
            o_ref[j] = (mine + g_ref[j].astype(F32)).astype(o_ref.dtype)

    return _pcall(body, name=name, grid=(r // tr,), out_shape=_sds(got.shape, got.dtype),
                  in_specs=[pl.BlockSpec((NDEV, tr, c), lambda t: (0, t, 0)), pl.BlockSpec((4, tr, c), lambda t: (0, t, 0))],
                  out_specs=pl.BlockSpec((4, tr, c), lambda t: (0, t, 0)))(x, got)


def _sum_chips(part, recv, tr, name):
    _, r, c = part.shape

    def body(p_ref, r_ref, o_ref):
        slot = 2 * lax.axis_index("x") + lax.axis_index("y")
        acc = p_ref[0].astype(F32)
        for j in range(1, 4):
            acc = jnp.where(slot == j, p_ref[j].astype(F32), acc)
        for j in range(3):
            acc = acc + r_ref[j].astype(F32)
        o_ref[...] = acc

    return _pcall(body, name=name, grid=(r // tr,), out_shape=_sds((r, c), F32),
                  in_specs=[pl.BlockSpec((4, tr, c), lambda t: (0, t, 0)), pl.BlockSpec((3, tr, c), lambda t: (0, t, 0))],
                  out_specs=pl.BlockSpec((tr, c), lambda t: (t, 0)))(part, recv)


def _sum_slots(x, tr, name):
    n, r, c = x.shape

    def body(x_ref, o_ref):
        acc = x_ref[0].astype(F32)
        for s in range(1, n):
            acc = acc + x_ref[s].astype(F32)
        o_ref[...] = acc

    return _pcall(body, name=name, grid=(r // tr,), out_shape=_sds((r, c), F32),
                  in_specs=[pl.BlockSpec((n, tr, c), lambda t: (0, t, 0))],
                  out_specs=pl.BlockSpec((tr, c), lambda t: (t, 0)))(x)


def _ada_front(a_loc, w_loc, b_loc, wpack):
    ncol = w_loc.shape[1]
    nrow = NDEV * a_loc.shape[0]

    def body(a_ref, w_ref, b_ref, wp_ref, araw_ref, mloc_ref, mall_ref, wall_ref,
             a_vm, w_vm, m_vm, lsem, *sems):
        a_start, a_forward, a_finish = _gather_phases(a_ref, araw_ref, *sems[0:3])
        m_start, m_forward, m_finish = _gather_phases(mloc_ref, mall_ref, *sems[3:6])
        w_start, w_forward, w_finish = _gather_phases(wp_ref, wall_ref, *sems[6:9])
        w_in = pltpu.make_async_copy(w_ref, w_vm, lsem.at[0])
        w_in.start()
        a_start()
        w_start()
        a_forward()
        a_finish()
        a_in = pltpu.make_async_copy(araw_ref, a_vm, lsem.at[1])
        a_in.start()
        a_in.wait()
        w_in.wait()
        a = a_vm[...].reshape(nrow, D)
        act = (a * _sigmoid(a)).astype(BF16)
        m_vm[...] = _mm(act, w_vm[...].astype(BF16)) + b_ref[...]
        m_out = pltpu.make_async_copy(m_vm, mloc_ref, lsem.at[2])
        m_out.start()
        m_out.wait()
        m_start()
        m_forward()
        m_finish()
        w_forward()
        w_finish()

    return pl.pallas_call(
        body, name="ada_front",
        out_shape=(_sds((NDEV,) + a_loc.shape, F32), _sds((nrow, ncol), F32), _sds((NDEV, nrow, ncol), F32),
                   _sds((NDEV,) + wpack.shape, wpack.dtype)),
        in_specs=[ANY, ANY, VMEM, ANY], out_specs=(ANY, ANY, ANY, ANY),
        scratch_shapes=[pltpu.VMEM((NDEV,) + a_loc.shape, F32), pltpu.VMEM(w_loc.shape, F32),
                        pltpu.VMEM((nrow, ncol), F32), pltpu.SemaphoreType.DMA((3,))] + _GATHER_SEMS * 3,
        compiler_params=pltpu.CompilerParams(vmem_limit_bytes=32 * MIB),
    )(a_loc, w_loc, b_loc, wpack)


def _ada_bwd(a_raw, cctx_col, g_all, g_cols, w_loc, nb):
    nrow = a_raw.shape[0]
    ncol = w_loc.shape[1]

    def body(a_ref, cc_ref, gall_ref, g_ref, w_ref, dw_ref, pc_ref, gb_ref):
        a = a_ref[...]
        rowid = lax.broadcasted_iota(jnp.int32, (nrow, 1), 0) % 8
        act = jnp.where(rowid < nb, a * _sigmoid(a), 0.0).astype(BF16)
        g = g_ref[...]
        gc = _rowsum(jnp.where(rowid == nb, g, 0.0))
        cc = cc_ref[...]
        dw_ref[...] = _mm_tn(act, g.astype(BF16)) + (cc * _sigmoid(cc)) * gc
        pc_ref[...] = jnp.sum(w_ref[...] * gc, axis=1, keepdims=True)
        gb_ref[...] = _rowsum(gall_ref[...])

    return _pcall(body, name="ada_bwd",
                  out_shape=(_sds((D, ncol), F32), _sds((D, 1), F32), _sds((1, g_all.shape[1]), F32)),
                  in_specs=[VMEM] * 5, out_specs=(VMEM,) * 3, vmem_mb=48)(a_raw, cctx_col, g_all, g_cols, w_loc)


def _mod_spec(k, tpe, nrows):
    return pl.BlockSpec((1, k, D), lambda t: (jnp.minimum(t // tpe, nrows - 1), 0, 0))


def _load_ffn_weights(wall_ref, first, bufs, sems):
    fsh = FF // NDEV
    cps = []
    for j, buf in enumerate(bufs):
        for d in range(NDEV):
            cps.append(pltpu.make_async_copy(wall_ref.at[d, pl.ds((first + j) * fsh, fsh)],
                                             buf.at[pl.ds(d * fsh, fsh)], sems.at[j * NDEV + d]))
    for cp in cps:
        cp.start()
    for cp in cps:
        cp.wait()


def _token_specs(xs, tm, n_lat):
    specs = [pl.BlockSpec((tm, D), lambda t: (jnp.minimum(t, n_lat - 1), 0))]
    if len(xs) == 2:
        specs.append(pl.BlockSpec((tm, D), lambda t: (jnp.maximum(t - n_lat, 0), 0)))
    return specs


def _ffn_fwd(xs, mod3, norm_w, wall, first, *, tm, n_tiles, tpe, n_lat, name, target=None, gather=None):
    nrows = mod3.shape[0]
    r = n_tiles * tm
    nx = len(xs)
    with_loss = target is not None
    with_gather = gather is not None
    fwd_step = max(2 * n_tiles // 3, 1)

    def body(*refs):
        x_refs = refs[:nx]
        pos = nx
        if with_loss:
            tgt_ref = refs[pos]
            pos += 1
        mod_ref, nw_ref, wall_ref = refs[pos:pos + 3]
        pos += 3
        if with_gather:
            gin_ref = refs[pos]
            pos += 1
        xo_ref, a_ref, b_ref, o_ref = refs[pos:pos + 4]
        pos += 4
        if with_loss:
            ls_ref = refs[pos]
            pos += 1
        if with_gather:
            gout_ref = refs[pos]
            pos += 1
        w1_ref, w3_ref, w2_ref, wsem, acc_ref = refs[pos:pos + 5]
        t = pl.program_id(0)
        if with_gather:
            g_start, g_forward, g_finish = _gather_phases(gin_ref, gout_ref, *refs[pos + 5:])

        @pl.when(t == 0)
        def _():
            if with_gather:
                g_start()
            _load_ffn_weights(wall_ref, first, (w1_ref, w3_ref, w2_ref), wsem)
            if with_loss:
                ls_ref[...] = jnp.zeros_like(ls_ref)

        if with_gather:
            @pl.when(t == fwd_step)
            def _():
                g_forward()

            @pl.when(t == n_tiles - 1)
            def _():
                g_finish()

        x = x_refs[0][...]
        if nx == 2:
            x = jnp.where(t < n_lat, x, x_refs[1][...])
        n = x * _rms(x) * nw_ref[...]
        shift, scale, gate = mod_ref[0, 0:1, :], mod_ref[0, 1:2, :], mod_ref[0, 2:3, :]
        h = (n * (1.0 + scale) + shift).astype(BF16)
        nch = FF // FC
        o = None
        for lo_c, hi_c in ((0, nch // 2), (nch // 2, nch)):
            for j in range(lo_c, hi_c):
                sl = slice(j * FC, (j + 1) * FC)
                a = _mm_nt(h, w1_ref[sl, :])
                b = _mm_nt(h, w3_ref[sl, :])
                a_ref[:, sl] = a.astype(BF16)
                b_ref[:, sl] = b.astype(BF16)
                acc_ref[:, sl] = (a * _sigmoid(a) * b).astype(BF16)
            gs = slice(lo_c * FC, hi_c * FC)
            part = _mm(acc_ref[:, gs], w2_ref[gs, :])
            o = part if o is None else o + part
        o_ref[...] = o.astype(BF16)
        out = x + (0.5 * gate) * o
        if with_loss:
            d = out - tgt_ref[...]
            xo_ref[...] = d * (1.0 / D)
            ls_ref[...] += jnp.sum(d * d)
        else:
            xo_ref[...] = out

    row = lambda cols: pl.BlockSpec((tm, cols), lambda t: (t, 0))
    in_specs = _token_specs(xs, tm, n_lat) + ([row(D)] if with_loss else []) + [
        _mod_spec(3, tpe, nrows), _const((1, D)), ANY]
    out_shape = [_sds((r, D), F32), _sds((r, FF), BF16), _sds((r, FF), BF16), _sds((r, D), BF16)]
    out_specs = [row(D), row(FF), row(FF), row(D)]
    scratch = [pltpu.VMEM((FF, D), BF16)] * 3 + [pltpu.SemaphoreType.DMA((3 * NDEV,)), pltpu.VMEM((tm, FF), BF16)]
    if with_loss:
        out_shape.append(_sds((8, LANE), F32))
        out_specs.append(_const((8, LANE)))
    args = list(xs) + ([target] if with_loss else []) + [mod3, norm_w, wall]
    if with_gather:
        assert n_tiles >= 2
        in_specs.append(ANY)
        args.append(gather)
        out_shape.append(_sds((NDEV,) + gather.shape, gather.dtype))
        out_specs.append(ANY)
        scratch += _GATHER_SEMS
    return _pcall(
        body, name=name, grid=(n_tiles,), out_shape=tuple(out_shape), in_specs=in_specs, out_specs=tuple(out_specs),
        scratch=scratch, vmem_mb=56)(*args)


def _ffn_bwd_dx(dout, xs, a, b, o, mod3, norm_w, wall, first, *, tm, n_tiles, tpe, n_lat, name):
    nrows = mod3.shape[0]
    r = n_tiles * tm
    nx = len(xs)

    def body(*refs):
        dout_ref = refs[0]
        x_refs = refs[1:1 + nx]
        (a_ref, b_ref, o_ref, mod_ref, nw_ref, wall_ref,
         dx_ref, da_ref, db_ref, g_ref, do_ref, h_ref, dmod_ref, dnw_ref,
         w1_ref, w3_ref, w2_ref, wsem) = refs[1 + nx:]
        t = pl.program_id(0)

        @pl.when(t == 0)
        def _():
            _load_ffn_weights(wall_ref, first, (w1_ref, w3_ref, w2_ref), wsem)
            dnw_ref[...] = jnp.zeros_like(dnw_ref)

        x = x_refs[0][...]
        if nx == 2:
            x = jnp.where(t < n_lat, x, x_refs[1][...])
        dout = dout_ref[...]
        rr = _rms(x)
        xh = x * rr
        nw = nw_ref[...]
        n = xh * nw
        shift, scale, gate = mod_ref[0, 0:1, :], mod_ref[0, 1:2, :], mod_ref[0, 2:3, :]
        h = (n * (1.0 + scale) + shift).astype(BF16)
        h_ref[...] = h
        d_o = ((0.5 * gate) * dout).astype(BF16)
        do_ref[...] = d_o
        dgate = _rowsum(0.5 * o_ref[...].astype(F32) * dout)
        nch = FF // FC
        groups = ((0, nch // 2), (nch // 2, nch))
        dh = None
        for lo_c, hi_c in groups:
            for j in range(lo_c, hi_c):
                sl = slice(j * FC, (j + 1) * FC)
                av = a_ref[:, sl].astype(F32)
                bv = b_ref[:, sl].astype(F32)
                dg = _mm_nt(d_o, w2_ref[sl, :])
                sig = _sigmoid(av)
                sa = av * sig
                g_ref[:, sl] = (sa * bv).astype(BF16)
                da_ref[:, sl] = (dg * bv * (sig * (1.0 + av * (1.0 - sig)))).astype(BF16)
                db_ref[:, sl] = (dg * sa).astype(BF16)
            gs = slice(lo_c * FC, hi_c * FC)
            part = _mm(da_ref[:, gs], w1_ref[gs, :]) + _mm(db_ref[:, gs], w3_ref[gs, :])
            dh = part if dh is None else dh + part
        dn = dh * (1.0 + scale)
        dxh = dn * nw

        @pl.when(t < n_lat)
        def _():
            dx_ref[...] = dout + rr * (dxh - xh * jnp.mean(dxh * xh, axis=-1, keepdims=True))

        first_visit = jnp.where(t < n_lat, t % tpe == 0, t == n_lat)

        @pl.when(first_visit)
        def _():
            dmod_ref[...] = jnp.zeros_like(dmod_ref)

        dmod_ref[0, 0:1, :] += _rowsum(dh)
        dmod_ref[0, 1:2, :] += _rowsum(dh * n)
        dmod_ref[0, 2:3, :] += dgate
        dnw_ref[...] += _rowsum(dn * xh)

    row = lambda cols: pl.BlockSpec((tm, cols), lambda t: (t, 0))
    lat = pl.BlockSpec((tm, D), lambda t: (jnp.minimum(t, n_lat - 1), 0))
    return _pcall(
        body, name=name, grid=(n_tiles,),
        out_shape=(_sds((n_lat * tm, D), F32), _sds((r, FF), BF16), _sds((r, FF), BF16), _sds((r, FF), BF16),
                   _sds((r, D), BF16), _sds((r, D), BF16), _sds((nrows, 3, D), F32), _sds((1, D), F32)),
        in_specs=[row(D)] + _token_specs(xs, tm, n_lat) + [row(FF), row(FF), row(D), _mod_spec(3, tpe, nrows),
                                                            _const((1, D)), ANY],
        out_specs=(lat, row(FF), row(FF), row(FF), row(D), row(D), _mod_spec(3, tpe, nrows), _const((1, D))),
        scratch=[pltpu.VMEM((FF, D), BF16)] * 3 + [pltpu.SemaphoreType.DMA((3 * NDEV,))],
        vmem_mb=60)(dout, *xs, a, b, o, mod3, norm_w, wall)


def _ffn_bwd_dw(h, d_o, da, db, g, *, tr, name):
    r = h.shape[0]
    fh = FF // 2
    fsh = FF // NDEV
    nk = r // tr

    def body(h_ref, do_ref, da_ref, db_ref, g_ref, out_ref, acc1, acc3, acc2):
        k = pl.program_id(1)

        @pl.when(k == 0)
        def _():
            acc1[...] = jnp.zeros_like(acc1)
            acc3[...] = jnp.zeros_like(acc3)
            acc2[...] = jnp.zeros_like(acc2)

        hv = h_ref[...]
        acc1[...] += _mm_tn(da_ref[...], hv)
        acc3[...] += _mm_tn(db_ref[...], hv)
        acc2[...] += _mm_tn(g_ref[...], do_ref[...])

        @pl.when(k == nk - 1)
        def _():
            for i, acc in enumerate((acc1, acc3, acc2)):
                out_ref[:, i * fsh:(i + 1) * fsh, :] = acc[...].reshape(NDEV // 2, fsh, D).astype(BF16)

    rowd = pl.BlockSpec((tr, D), lambda f, k: (k, 0))
    rowf = pl.BlockSpec((tr, fh), lambda f, k: (k, f))
    return _pcall(
        body, name=name, grid=(2, nk), out_shape=_sds((NDEV, 3 * fsh, D), BF16),
        in_specs=[rowd, rowd, rowf, rowf, rowf],
        out_specs=pl.BlockSpec((NDEV // 2, 3 * fsh, D), lambda f, k: (f, 0, 0)),
        scratch=[pltpu.VMEM((fh, D), F32)] * 3, vmem_mb=56)(h, d_o, da, db, g)


_PIECES = ((0, 128), (128, 384), (384, 896), (896, 1408), (1408, 1536))


def _proj_fwd(x1, mod2, norm_w, wint, *, tm, n_tiles, tpe, name="proj_fwd"):
    nrows = mod2.shape[0]
    r = n_tiles * tm

    def body(x_ref, mod_ref, nw_ref, w_ref, ckv_ref, q_ref, u_ref, v_ref, kpe_ref):
        x = x_ref[...]
        n = x * _rms(x) * nw_ref[...]
        h = (n * (1.0 + mod_ref[0, 1:2, :]) + mod_ref[0, 0:1, :]).astype(BF16)
        for (lo, hi), ref in zip(_PIECES, (ckv_ref, q_ref, u_ref, v_ref, kpe_ref)):
            ref[...] = _mm_nt(h, w_ref[lo:hi, :])

    row = lambda cols: pl.BlockSpec((tm, cols), lambda t: (t, 0))
    widths = [hi - lo for lo, hi in _PIECES]
    return _pcall(
        body, name=name, grid=(n_tiles,),
        out_shape=tuple(_sds((r, w), F32) for w in widths),
        in_specs=[row(D), _mod_spec(2, tpe, nrows), _const((1, D)), _const((WIN_ROWS, D))],
        out_specs=tuple(row(w) for w in widths), vmem_mb=40)(x1, mod2, norm_w, wint)


def _proj_bwd(dckv, dkpe, dq, du, dv, dx2, x1, mod2, norm_w, wint, *, tm, n_tiles, tpe, n_lat, name="proj_bwd"):
    nrows = mod2.shape[0]
    r = n_tiles * tm

    def body(dckv_ref, dkpe_ref, dq_ref, du_ref, dv_ref, dx2_ref, x_ref, mod_ref, nw_ref, w_ref,
             dx_ref, dw_ref, dmod_ref, dnw_ref, acc_ref):
        t = pl.program_id(0)
        is_lat = t < n_lat
        x = x_ref[...]
        rr = _rms(x)
        xh = x * rr
        nw = nw_ref[...]
        n = xh * nw
        scale = mod_ref[0, 1:2, :]
        h = (n * (1.0 + scale) + mod_ref[0, 0:1, :]).astype(BF16)

        @pl.when(t == 0)
        def _():
            dw_ref[...] = jnp.zeros_like(dw_ref)
            dnw_ref[...] = jnp.zeros_like(dnw_ref)

        dckv_v, dkpe_v = dckv_ref[...], dkpe_ref[...]
        acc_ref[...] = _mm(dckv_v, w_ref[0:128, :]) + _mm(dkpe_v, w_ref[1408:1536, :])
        dw_ref[0:128, :] += _mm_tn(dckv_v, h)
        dw_ref[1408:1536, :] += _mm_tn(dkpe_v, h)

        @pl.when(is_lat)
        def _():
            dq_v, du_v, dv_v = dq_ref[...], du_ref[...], dv_ref[...]
            acc_ref[...] += (_mm(dq_v, w_ref[128:384, :]) + _mm(du_v, w_ref[384:896, :])
                             + _mm(dv_v, w_ref[896:1408, :]))
            dw_ref[128:384, :] += _mm_tn(dq_v, h)
            dw_ref[384:896, :] += _mm_tn(du_v, h)
            dw_ref[896:1408, :] += _mm_tn(dv_v, h)

        dh = acc_ref[...]
        dn = dh * (1.0 + scale)
        dxh = dn * nw
        dx = rr * (dxh - xh * jnp.mean(dxh * xh, axis=-1, keepdims=True))
        dx_ref[...] = dx + jnp.where(is_lat, dx2_ref[...], 0.0)

        first = jnp.where(is_lat, t % tpe == 0, t == n_lat)

        @pl.when(first)
        def _():
            dmod_ref[...] = jnp.zeros_like(dmod_ref)

        dmod_ref[0, 0:1, :] += _rowsum(dh)
        dmod_ref[0, 1:2, :] += _rowsum(dh * n)
        dnw_ref[...] += _rowsum(dn * xh)

    row = lambda cols: pl.BlockSpec((tm, cols), lambda t: (t, 0))
    lat = lambda cols: pl.BlockSpec((tm, cols), lambda t: (jnp.minimum(t, n_lat - 1), 0))
    return _pcall(
        body, name=name, grid=(n_tiles,),
        out_shape=(_sds((r, D), F32), _sds((WIN_ROWS, D), F32), _sds((nrows, 2, D), F32), _sds((1, D), F32)),
        in_specs=[row(128), row(128), lat(256), lat(512), lat(512), lat(D), row(D), _mod_spec(2, tpe, nrows),
                  _const((1, D)), _const((WIN_ROWS, D))],
        out_specs=(row(D), _const((WIN_ROWS, D)), _mod_spec(2, tpe, nrows), _const((1, D))),
        scratch=[pltpu.VMEM((tm, D), F32)], vmem_mb=48)(dckv, dkpe, dq, du, dv, dx2, x1, mod2, norm_w, wint)


def _seg_sum(x, seg):
    return _mm(x.astype(BF16), seg)


def _seg_bcast(v, segt2):
    hi = v.astype(BF16)
    lo = (v - hi.astype(F32)).astype(BF16)
    return _mm(jnp.concatenate([hi, lo], axis=-1), segt2)


def _rope_pairs(t, cos, sin, rot2):
    cos2, sin2 = jnp.concatenate([cos, cos], axis=-1), jnp.concatenate([sin, sin], axis=-1)
    out = []
    for j in range(H // 2):
        tj = t[:, 2 * j * LANE:2 * (j + 1) * LANE]
        out.append(tj * cos2 + _dot_hl(tj, rot2) * sin2)
    return jnp.concatenate(out, axis=-1)


def _head_norm_rope(x, w_pad, cos, sin, seg, segt2, rot2, rope=True):
    rh = lax.rsqrt(_seg_sum(x * x, seg) * (1.0 / DH) + EPS)
    rb = _seg_bcast(rh, segt2)
    y = x * rb
    out = _rope_pairs(y * w_pad, cos, sin, rot2) if rope else None
    return out, y, rb


def _head_norm_rope_bwd(dout, y, rb, w_pad, cos, sin, seg, segt2, rot2_t):
    cos2, sin2 = jnp.concatenate([cos, cos], axis=-1), jnp.concatenate([sin, sin], axis=-1)
    dt = []
    for j in range(H // 2):
        dj = dout[:, 2 * j * LANE:2 * (j + 1) * LANE]
        dt.append(dj * cos2 + _dot_hl(dj * sin2, rot2_t))
    dt = jnp.concatenate(dt, axis=-1)
    dw = _rowsum(dt * y)
    dy = dt * w_pad
    mean_h = _seg_sum(dy * y, seg) * (1.0 / DH)
    return rb * (dy - y * _seg_bcast(mean_h, segt2)), dw


def _q_prep_fwd(qp, qa_w, wuq, wq, cos, sin, cs, *, tm, n_lat, tpe):
    def body(qp_ref, qa_ref, wuq_ref, wq_ref, cos_ref, sin_ref, seg, segt, rot, q_ref):
        x = qp_ref[...]
        cq = (x * _rms(x) * qa_ref[...]).astype(BF16)
        q, _, _ = _head_norm_rope(_mm_nt(cq, wuq_ref[...]), wq_ref[...], cos_ref[...], sin_ref[...],
                                  seg[...], segt[...], rot[...])
        q_ref[...] = q.astype(BF16)

    row = lambda cols: pl.BlockSpec((tm, cols), lambda t: (t, 0))
    tab = pl.BlockSpec((tm, LANE), lambda t: (t % tpe, 0))
    return _pcall(
        body, name="q_prep_fwd", grid=(n_lat,), out_shape=_sds((n_lat * tm, HP), BF16),
        in_specs=[row(QL), _const((1, QL)), _const((HP, QL)), _const((1, HP)), tab, tab,
                  _const((HP, LANE)), _const((2 * LANE, HP)), _const((2 * LANE, 2 * LANE))],
        out_specs=row(HP))(qp, qa_w, wuq, wq, cos, sin, cs["seg_h"], cs["seg_ht"], cs["rot"])


def _q_prep_bwd(dq, qp, qa_w, wuq, wq, cos, sin, cs, *, tm, n_lat, tpe):
    def body(dq_ref, qp_ref, qa_ref, wuq_ref, wq_ref, cos_ref, sin_ref, seg, segt, rot, rot_t,
             dqp_ref, dwuq_ref, dqa_ref, dwq_ref):
        t = pl.program_id(0)
        x = qp_ref[...]
        ra = _rms(x)
        xh = x * ra
        qa = qa_ref[...]
        cq = (xh * qa).astype(BF16)
        wuq_v = wuq_ref[...]
        wq_v, cos_v, sin_v = wq_ref[...], cos_ref[...], sin_ref[...]
        _, y, rb = _head_norm_rope(_mm_nt(cq, wuq_v), wq_v, cos_v, sin_v, seg[...], segt[...], rot[...], rope=False)
        dqraw, dwq = _head_norm_rope_bwd(dq_ref[...], y, rb, wq_v, cos_v, sin_v, seg[...], segt[...], rot_t[...])
        dqraw = dqraw.astype(BF16)
        dcq = _mm(dqraw, wuq_v)
        dxh = dcq * qa
        dqp_ref[...] = (ra * (dxh - xh * jnp.mean(dxh * xh, axis=-1, keepdims=True))).astype(BF16)

        @pl.when(t == 0)
        def _():
            dwuq_ref[...] = jnp.zeros_like(dwuq_ref)
            dqa_ref[...] = jnp.zeros_like(dqa_ref)
            dwq_ref[...] = jnp.zeros_like(dwq_ref)

        dwuq_ref[...] += _mm_tn(dqraw, cq)
        dqa_ref[...] += _rowsum(dcq * xh)
        dwq_ref[...] += dwq

    row = lambda cols: pl.BlockSpec((tm, cols), lambda t: (t, 0))
    tab = pl.BlockSpec((tm, LANE), lambda t: (t % tpe, 0))
    return _pcall(
        body, name="q_prep_bwd", grid=(n_lat,),
        out_shape=(_sds((n_lat * tm, QL), BF16), _sds((HP, QL), F32), _sds((1, QL), F32), _sds((1, HP), F32)),
        in_specs=[row(HP), row(QL), _const((1, QL)), _const((HP, QL)), _const((1, HP)), tab, tab,
                  _const((HP, LANE)), _const((2 * LANE, HP)), _const((2 * LANE, 2 * LANE)), _const((2 * LANE, 2 * LANE))],
        out_specs=(row(QL), _const((HP, QL)), _const((1, QL)), _const((1, HP))), vmem_mb=40)(
            dq, qp, qa_w, wuq, wq, cos, sin, cs["seg_h"], cs["seg_ht"], cs["rot"], cs["rot_t"])


def _kv_tab_spec(tm, tpe, n_lat):
    return pl.BlockSpec((tm, LANE), lambda t: (jnp.where(t < n_lat, t % tpe, tpe), 0))


def _kv_prep_fwd(ckv, kpe, kva_w, wukv, wk, cosk, sink, cs, *, tm, n_tiles, tpe, n_lat):
    def body(ckv_ref, kpe_ref, kva_ref, wukv_ref, wk_ref, cos_ref, sin_ref, seg, segt, rot, k_ref, v_ref):
        x = ckv_ref[...]
        ckvn = (x * _rms(x) * kva_ref[...]).astype(BF16)
        kv = _mm_nt(ckvn, wukv_ref[...])
        kx = kv[:, :HP] + jnp.concatenate([kpe_ref[...]] * H, axis=-1)
        k, _, _ = _head_norm_rope(kx, wk_ref[...], cos_ref[...], sin_ref[...], seg[...], segt[...], rot[...])
        k_ref[...] = k.astype(BF16)
        v_ref[...] = kv[:, HP:].astype(BF16)

    row = lambda cols: pl.BlockSpec((tm, cols), lambda t: (t, 0))
    tab = _kv_tab_spec(tm, tpe, n_lat)
    r = n_tiles * tm
    return _pcall(
        body, name="kv_prep_fwd", grid=(n_tiles,), out_shape=(_sds((r, HP), BF16), _sds((r, HP), BF16)),
        in_specs=[row(KVL), row(LANE), _const((1, KVL)), _const((2 * HP, KVL)), _const((1, HP)), tab, tab,
                  _const((HP, LANE)), _const((2 * LANE, HP)), _const((2 * LANE, 2 * LANE))],
        out_specs=(row(HP), row(HP)), vmem_mb=40)(
            ckv, kpe, kva_w, wukv, wk, cosk, sink, cs["seg_h"], cs["seg_ht"], cs["rot"])


def _kv_prep_bwd(dks, dvs, ckv, kpe, kva_w, wukv, wk, cosk, sink, cs, *, tm, n_tiles, tpe, n_lat):
    def body(dkl_ref, dkc_ref, dvl_ref, dvc_ref, ckv_ref, kpe_ref, kva_ref, wukv_ref, wk_ref, cos_ref, sin_ref,
             seg, segt, rot, rot_t, dckv_ref, dkpe_ref, dwukv_ref, dkva_ref, dwk_ref):
        t = pl.program_id(0)
        is_lat = t < n_lat
        dk = jnp.where(is_lat, dkl_ref[...], dkc_ref[...])
        dv = jnp.where(is_lat, dvl_ref[...], dvc_ref[...])
        x = ckv_ref[...]
        ra = _rms(x)
        xh = x * ra
        kva = kva_ref[...]
        ckvn = (xh * kva).astype(BF16)
        wukv_v = wukv_ref[...]
        wk_v, cos_v, sin_v = wk_ref[...], cos_ref[...], sin_ref[...]
        kv = _mm_nt(ckvn, wukv_v)
        kx = kv[:, :HP] + jnp.concatenate([kpe_ref[...]] * H, axis=-1)
        _, y, rb = _head_norm_rope(kx, wk_v, cos_v, sin_v, seg[...], segt[...], rot[...], rope=False)
        dkx, dwk = _head_norm_rope_bwd(dk, y, rb, wk_v, cos_v, sin_v, seg[...], segt[...], rot_t[...])
        dkpe = dkx[:, 0:LANE]
        for h in range(1, H):
            dkpe = dkpe + dkx[:, h * LANE:(h + 1) * LANE]
        lane = lax.broadcasted_iota(jnp.int32, (tm, LANE), 1)
        dkpe_ref[...] = jnp.where((lane >= DN) & (lane < DH), dkpe, 0.0).astype(BF16)
        dkv = jnp.concatenate([dkx, dv], axis=-1).astype(BF16)
        dckvn = _mm(dkv, wukv_v)
        dxh = dckvn * kva
        dckv_ref[...] = (ra * (dxh - xh * jnp.mean(dxh * xh, axis=-1, keepdims=True))).astype(BF16)

        @pl.when(t == 0)
        def _():
            dwukv_ref[...] = jnp.zeros_like(dwukv_ref)
            dkva_ref[...] = jnp.zeros_like(dkva_ref)
            dwk_ref[...] = jnp.zeros_like(dwk_ref)

        dwukv_ref[...] += _mm_tn(dkv, ckvn)
        dkva_ref[...] += _rowsum(dckvn * xh)
        dwk_ref[...] += dwk

    row = lambda cols: pl.BlockSpec((tm, cols), lambda t: (t, 0))
    lat = pl.BlockSpec((tm, HP), lambda t: (jnp.minimum(t, n_lat - 1), 0))
    ctx = pl.BlockSpec((tm, HP), lambda t: (jnp.maximum(t - n_lat, 0), 0))
    tab = _kv_tab_spec(tm, tpe, n_lat)
    r = n_tiles * tm
    return _pcall(
        body, name="kv_prep_bwd", grid=(n_tiles,),
        out_shape=(_sds((r, KVL), BF16), _sds((r, LANE), BF16), _sds((2 * HP, KVL), F32), _sds((1, KVL), F32),
                   _sds((1, HP), F32)),
        in_specs=[lat, ctx, lat, ctx, row(KVL), row(LANE), _const((1, KVL)), _const((2 * HP, KVL)), _const((1, HP)),
                  tab, tab, _const((HP, LANE)), _const((2 * LANE, HP)), _const((2 * LANE, 2 * LANE)), _const((2 * LANE, 2 * LANE))],
        out_specs=(row(KVL), row(LANE), _const((2 * HP, KVL)), _const((1, KVL)), _const((1, HP))), vmem_mb=48)(
            dks[0], dks[1], dvs[0], dvs[1], ckv, kpe, kva_w, wukv, wk, cosk, sink,
            cs["seg_h"], cs["seg_ht"], cs["rot"], cs["rot_t"])


_SCALE = DH ** -0.5
_SCALE_LOG2E = _SCALE * 1.4426950408889634


def _key_chunks(s, nc, ck):
    return ([(0, lo, min(lo + ck, s)) for lo in range(0, s, ck)]
            + [(1, lo, min(lo + ck, nc)) for lo in range(0, nc, ck)])


def _attn_fwd(q, k, v, *, nb, s, nc, tq, ck):
    tpe = s // tq
    r_lat = nb * s
    chunks = _key_chunks(s, nc, ck)
    hp = 4

    def body(q_ref, kl_ref, kc_ref, vl_ref, vc_ref, o_ref, lse_ref):
        k_refs, v_refs = (kl_ref, kc_ref), (vl_ref, vc_ref)
        for hh in range(hp):
            hs = slice(hh * LANE, (hh + 1) * LANE)
            qv = q_ref[:, hs]
            xs = [_mm_nt(qv, k_refs[w][lo:hi, hs]) for w, lo, hi in chunks]
            m = jnp.max(xs[0], axis=-1, keepdims=True)
            for x in xs[1:]:
                m = jnp.maximum(m, jnp.max(x, axis=-1, keepdims=True))
            l = acc = None
            for x, (w, lo, hi) in zip(xs, chunks):
                e = jnp.exp2((x - m) * _SCALE_LOG2E)
                lc = jnp.sum(e, axis=-1, keepdims=True)
                pv = _mm(e.astype(BF16), v_refs[w][lo:hi, hs])
                l = lc if l is None else l + lc
                acc = pv if acc is None else acc + pv
            o_ref[:, hs] = (acc / l).astype(BF16)
            lse = m * _SCALE_LOG2E + jnp.log2(l)
            lse_ref[hh] = jnp.transpose(jnp.broadcast_to(lse, (tq, LANE)))[0:8, :]

    qs = pl.BlockSpec((tq, hp * LANE), lambda i, j, t: (i * tpe + t, j))
    kl = pl.BlockSpec((s, hp * LANE), lambda i, j, t: (i, j))
    kc = pl.BlockSpec((nc, hp * LANE), lambda i, j, t: (r_lat // nc + i, j))
    ls = pl.BlockSpec((hp, 8, tq), lambda i, j, t: (i * (H // hp) + j, 0, t))
    return _pcall(body, name="attn_fwd", grid=(nb, H // hp, tpe),
                  out_shape=(_sds((r_lat, HP), BF16), _sds((nb * H, 8, s), F32)),
                  in_specs=[qs, kl, kc, kl, kc], out_specs=(qs, ls), vmem_mb=48)(q, k, k, v, v)


def _attn_bwd(q, k, v, o, do, lse, part, *, nb, s, nc, tq, ck):
    tpe = s // tq
    r_lat = nb * s
    chunks = _key_chunks(s, nc, ck)
    hp = 2
    n_steps = nb * (H // hp) * tpe

    def body(q_ref, kl_ref, kc_ref, vl_ref, vc_ref, o_ref, do_ref, lse_ref, part_ref,
             dq_ref, dkl_ref, dkc_ref, dvl_ref, dvc_ref, recv_ref, akl, akc, avl, avc, send_sems, recv_sems):
        t = pl.program_id(2)
        step = (pl.program_id(0) * (H // hp) + pl.program_id(1)) * tpe + t
        sends = _chip_sends(part_ref, recv_ref, send_sems, recv_sems)

        @pl.when(step == 0)
        def _():
            for cp in sends:
                cp.start()

        @pl.when(step == n_steps - 1)
        def _():
            for cp in sends:
                cp.wait_recv()
            for cp in sends:
                cp.wait_send()

        @pl.when(t == 0)
        def _():
            akl[...] = jnp.zeros_like(akl)
            akc[...] = jnp.zeros_like(akc)
            avl[...] = jnp.zeros_like(avl)
            avc[...] = jnp.zeros_like(avc)

        k_refs, v_refs, ak, av = (kl_ref, kc_ref), (vl_ref, vc_ref), (akl, akc), (avl, avc)
        for hh in range(hp):
            hs = slice(hh * LANE, (hh + 1) * LANE)
            qv = q_ref[:, hs]
            lse = jnp.transpose(jnp.concatenate([lse_ref[hh]] * (LANE // 8), axis=0))[:, 0:1]
            dov = do_ref[:, hs]
            delta = jnp.sum(dov.astype(F32) * o_ref[:, hs].astype(F32), axis=-1, keepdims=True)
            dq = None
            for w, lo, hi in chunks:
                kc_v = k_refs[w][lo:hi, hs]
                p = jnp.exp2(_mm_nt(qv, kc_v) * _SCALE_LOG2E - lse)
                ds = (p * (_mm_nt(dov, v_refs[w][lo:hi, hs]) - delta)).astype(BF16)
                part = _mm(ds, kc_v)
                dq = part if dq is None else dq + part
                ak[w][hs, lo:hi] += _mm_tn(qv, ds)
                av[w][hs, lo:hi] += _mm_tn(dov, p.astype(BF16))
            dq_ref[:, hs] = dq * _SCALE

        @pl.when(t == tpe - 1)
        def _():
            dkl_ref[...] = akl[...].T * _SCALE
            dkc_ref[...] = akc[...].T * _SCALE
            dvl_ref[...] = avl[...].T
            dvc_ref[...] = avc[...].T

    qs = pl.BlockSpec((tq, hp * LANE), lambda i, j, t: (i * tpe + t, j))
    kl = pl.BlockSpec((s, hp * LANE), lambda i, j, t: (i, j))
    kc = pl.BlockSpec((nc, hp * LANE), lambda i, j, t: (r_lat // nc + i, j))
    kc_out = pl.BlockSpec((nc, hp * LANE), lambda i, j, t: (i, j))
    ls = pl.BlockSpec((hp, 8, tq), lambda i, j, t: (i * (H // hp) + j, 0, t))
    return _pcall(
        body, name="attn_bwd", grid=(nb, H // hp, tpe),
        out_shape=(_sds((r_lat, HP), F32), _sds((r_lat, HP), F32), _sds((nb * nc, HP), F32),
                   _sds((r_lat, HP), F32), _sds((nb * nc, HP), F32), _sds((3,) + part.shape[1:], part.dtype)),
        in_specs=[qs, kl, kc, kl, kc, qs, qs, ls, ANY], out_specs=(qs, kl, kc_out, kl, kc_out, ANY),
        scratch=[pltpu.VMEM((hp * LANE, s), F32), pltpu.VMEM((hp * LANE, nc), F32)] * 2
        + [pltpu.SemaphoreType.DMA((3,))] * 2,
        vmem_mb=60)(q, k, k, v, v, o, do, lse, part)


def _gating(vn, ws_ref, bias_ref, s_scr, tm):
    lane = lax.broadcasted_iota(jnp.int32, (CH, LANE), 1)
    for c in range(tm // CH):
        rs = slice(c * CH, (c + 1) * CH)
        for j in range(G // 2):
            ls = slice(j * LANE, (j + 1) * LANE)
            vp = vn[rs, ls]
            s_scr[rs, ls] = jnp.where(lane < GD, _mm(ws_ref[2 * j], vp), _mm(ws_ref[2 * j + 1], vp)) + bias_ref[:, ls]


def _mix_fwd(u, v, attn, x1, gate, wv, ws, bias, wout, cs, *, tm, n_lat, tpe):
    nrows = gate.shape[0]

    def body(u_ref, v_ref, attn_ref, x_ref, gate_ref, wv_ref, ws_ref, bias_ref, wout_ref, seg, segt,
             x2_ref, mix_ref, s_scr):
        vg = _gelu(v_ref[...])
        rg = lax.rsqrt(_seg_sum(vg * vg, seg[...]) * (1.0 / GD) + EPS)
        vn = (vg * _seg_bcast(rg, segt[...]) * wv_ref[...]).astype(BF16)
        _gating(vn, ws_ref, bias_ref, s_scr, tm)
        sg = (_gelu(u_ref[...]) * s_scr[...]).astype(BF16)
        mix = _mm(attn_ref[...], wout_ref[0:HP, :]) + _mm(sg, wout_ref[HP:, :])
        mix_ref[...] = mix.astype(BF16)
        x2_ref[...] = x_ref[...] + gate_ref[0] * mix

    row = lambda cols: pl.BlockSpec((tm, cols), lambda t: (t, 0))
    r = n_lat * tm
    return _pcall(
        body, name="mix_fwd", grid=(n_lat,),
        out_shape=(_sds((r, D), F32), _sds((r, D), BF16)),
        in_specs=[row(G * GD), row(G * GD), row(HP), row(D), _mod_spec(1, tpe, nrows), _const((1, G * GD)),
                  _const((G, CH, CH)), _const((CH, G * GD)), _const((HP + G * GD, D)), _const((G * GD, LANE)),
                  _const((2 * LANE, G * GD))],
        out_specs=(row(D), row(D)), scratch=[pltpu.VMEM((tm, G * GD), F32)], vmem_mb=40)(
            u, v, attn, x1, gate, wv, ws, bias, wout, cs["seg_g"], cs["seg_gt"])


def _mix_bwd(dx2, mix, u, v, attn, gate, wv, ws, wst, bias, wout, cs, *, tm, n_lat, tpe):
    nrows = gate.shape[0]
    wrows = HP + G * GD

    def body(dx2_ref, mix_ref, u_ref, v_ref, attn_ref, gate_ref, wv_ref, ws_ref, wst_ref, bias_ref, wout_ref, seg, segt,
             dattn_ref, du_ref, dv_ref, dgate_ref, dwout_ref, dws_ref, dbs_ref, dwv_ref, s_scr, dvn_scr, dbias_scr):
        t = pl.program_id(0)
        dx2 = dx2_ref[...]
        dmix = (dx2 * gate_ref[0]).astype(BF16)
        dcat = _mm_nt(dmix, wout_ref[...])
        dattn_ref[...] = dcat[:, :HP].astype(BF16)
        dsg = dcat[:, HP:]

        vraw = v_ref[...]
        vg = _gelu(vraw)
        rg = lax.rsqrt(_seg_sum(vg * vg, seg[...]) * (1.0 / GD) + EPS)
        r64 = _seg_bcast(rg, segt[...])
        y = vg * r64
        wv_v = wv_ref[...]
        vn = (y * wv_v).astype(BF16)
        _gating(vn, ws_ref, bias_ref, s_scr, tm)
        uraw = u_ref[...]
        ug = _gelu(uraw)
        s = s_scr[...]
        sg = (ug * s).astype(BF16)
        du_ref[...] = (dsg * s * _gelu_grad(uraw)).astype(BF16)
        ds = dsg * ug

        @pl.when(t == 0)
        def _():
            dwout_ref[...] = jnp.zeros_like(dwout_ref)
            dws_ref[...] = jnp.zeros_like(dws_ref)
            dwv_ref[...] = jnp.zeros_like(dwv_ref)
            dbias_scr[...] = jnp.zeros_like(dbias_scr)

        @pl.when(t % tpe == 0)
        def _():
            dgate_ref[...] = jnp.zeros_like(dgate_ref)

        dgate_ref[0] += _rowsum(dx2 * mix_ref[...].astype(F32))
        dwout_ref[...] += _mm_tn(jnp.concatenate([attn_ref[...], sg], axis=-1), dmix)

        lane = lax.broadcasted_iota(jnp.int32, (CH, LANE), 1)
        for c in range(tm // CH):
            rs = slice(c * CH, (c + 1) * CH)
            dbias_scr[...] += ds[rs, :]
            for j in range(G // 2):
                ls = slice(j * LANE, (j + 1) * LANE)
                dsp32 = ds[rs, ls]
                dsp = dsp32.astype(BF16)
                vp = vn[rs, ls]
                dvn_scr[rs, ls] = jnp.where(lane < GD, _mm(wst_ref[2 * j], dsp), _mm(wst_ref[2 * j + 1], dsp))
                dws_ref[2 * j] += _mm_nt(jnp.where(lane < GD, dsp32, 0.0).astype(BF16), vp)
                dws_ref[2 * j + 1] += _mm_nt(jnp.where(lane < GD, 0.0, dsp32).astype(BF16), vp)

        dvn = dvn_scr[...]
        dwv_ref[...] += _rowsum(dvn * y)
        dy = dvn * wv_v
        mean_g = _seg_sum(dy * y, seg[...]) * (1.0 / GD)
        dvg = r64 * (dy - y * _seg_bcast(mean_g, segt[...]))
        dv_ref[...] = (dvg * _gelu_grad(vraw)).astype(BF16)

        @pl.when(t == n_lat - 1)
        def _():
            dbs_ref[...] = _dot_hl(dbias_scr[...], seg[...])

    row = lambda cols: pl.BlockSpec((tm, cols), lambda t: (t, 0))
    r = n_lat * tm
    return _pcall(
        body, name="mix_bwd", grid=(n_lat,),
        out_shape=(_sds((r, HP), BF16), _sds((r, G * GD), BF16), _sds((r, G * GD), BF16), _sds((nrows, 1, D), F32),
                   _sds((wrows, D), F32), _sds((G, CH, CH), F32), _sds((CH, LANE), F32), _sds((1, G * GD), F32)),
        in_specs=[row(D), row(D), row(G * GD), row(G * GD), row(HP), _mod_spec(1, tpe, nrows), _const((1, G * GD)),
                  _const((G, CH, CH)), _const((G, CH, CH)), _const((CH, G * GD)), _const((wrows, D)),
                  _const((G * GD, LANE)), _const((2 * LANE, G * GD))],
        out_specs=(row(HP), row(G * GD), row(G * GD), _mod_spec(1, tpe, nrows), _const((wrows, D)),
                   _const((G, CH, CH)), _const((CH, LANE)), _const((1, G * GD))),
        scratch=[pltpu.VMEM((tm, G * GD), F32), pltpu.VMEM((tm, G * GD), F32), pltpu.VMEM((CH, G * GD), F32)],
        vmem_mb=56)(dx2, mix, u, v, attn, gate, wv, ws, wst, bias, wout, cs["seg_g"], cs["seg_gt"])


def _adamw_math(w, g, m, v):
    m2 = ADAM_B1 * m + (1.0 - ADAM_B1) * g
    v2 = ADAM_B2 * v + (1.0 - ADAM_B2) * (g * g)
    m_hat = m2 / (1.0 - ADAM_B1 ** ADAM_STEP)
    v_hat = v2 / (1.0 - ADAM_B2 ** ADAM_STEP)
    delta = -ADAM_LR * (m_hat / (jnp.sqrt(v_hat) + ADAM_EPS) + ADAM_WD * w)
    return delta, m2, v2


def _row_tile(r, c):
    best = r
    for tr in range(8, r, 8):
        if r % tr == 0 and tr * c * 4 <= MIB:
            best = tr
    return best


def _adamw(w, g, m, v, name):
    r, c = w.shape
    tr = _row_tile(r, c)

    def body(w_ref, g_ref, m_ref, v_ref, d_ref, mo_ref, vo_ref):
        d_ref[...], mo_ref[...], vo_ref[...] = _adamw_math(w_ref[...], g_ref[...], m_ref[...], v_ref[...])

    blk = pl.BlockSpec((tr, c), lambda t: (t, 0))
    return _pcall(body, name=name, grid=(r // tr,), out_shape=(_sds((r, c), F32),) * 3,
                  in_specs=[blk] * 4, out_specs=(blk,) * 3)(w, g, m, v)


def _adamw_small(params):
    n = len(params)

    def body(*refs):
        ins, outs = refs[:4 * n], refs[4 * n:]
        for i in range(n):
            w, g, m, v = (ins[4 * i + k][...] for k in range(4))
            if i == 0:
                sig = _sigmoid(w)
                g = g * (sig * (1.0 + w * (1.0 - sig)))
            d, m2, v2 = _adamw_math(w, g, m, v)
            outs[4 * i][...] = g
            outs[4 * i + 1][...] = d
            outs[4 * i + 2][...] = m2
            outs[4 * i + 3][...] = v2

    flat = [a for p in params for a in p]
    out_shape = tuple(_sds(p[0].shape, F32) for p in params for _ in range(4))
    res = _pcall(body, name="adamw_small", out_shape=out_shape, in_specs=[VMEM] * (4 * n),
                 out_specs=(VMEM,) * (4 * n))(*flat)
    return [res[4 * i:4 * i + 4] for i in range(n)]


def _rope_tables(s):
    rows = jnp.repeat(jnp.arange(s // GRID_W, dtype=F32), GRID_W)
    cols = jnp.tile(jnp.arange(GRID_W, dtype=F32), s // GRID_W)
    half = DR // 2
    inv = ROPE_BASE ** (-jnp.arange(0, half, 2, dtype=F32) / half)
    ang_r = rows[:, None] * inv
    ang_c = cols[:, None] * inv
    ang = jnp.concatenate([ang_r, ang_r, ang_c, ang_c], axis=-1)
    return jnp.cos(ang), jnp.sin(ang)


def _head_pad(a, real):
    return jnp.pad(a, ((0, 0), (0, LANE - real), (0, 0))).reshape(HP, a.shape[2])


def kernel(x, c, ctx, c_ctx, w_ada, b_ada, norm1_w, ffn1_w1, ffn1_w3, ffn1_w2, norm2_w, w_in, q_a_norm_w, w_uq, kv_a_norm_w, w_ukv, q_norm_w, k_norm_w, v_norm_w, w_s, b_s, w_out, norm3_w, ffn2_w1, ffn2_w3, ffn2_w2, loss_target, m_c_ctx, m_w_ada, m_b_ada, m_norm1_w, m_ffn1_w1, m_ffn1_w3, m_ffn1_w2, m_norm2_w, m_w_in, m_q_a_norm_w, m_w_uq, m_kv_a_norm_w, m_w_ukv, m_q_norm_w, m_k_norm_w, m_v_norm_w, m_w_s, m_b_s, m_w_out, m_norm3_w, m_ffn2_w1, m_ffn2_w3, m_ffn2_w2, v_c_ctx, v_w_ada, v_b_ada, v_norm1_w, v_ffn1_w1, v_ffn1_w3, v_ffn1_w2, v_norm2_w, v_w_in, v_q_a_norm_w, v_w_uq, v_kv_a_norm_w, v_w_ukv, v_q_norm_w, v_k_norm_w, v_v_norm_w, v_w_s, v_b_s, v_w_out, v_norm3_w, v_ffn2_w1, v_ffn2_w3, v_ffn2_w2):
    nb, s, _ = x.shape
    nc = ctx.shape[1]
    tm = 256 if nc % 256 == 0 else 128
    tpe = s // tm
    n_lat = nb * tpe
    n_all = n_lat + nb * nc // tm
    tmf = 2 * tm if s % (2 * tm) == 0 and (nb * nc) % (2 * tm) == 0 else tm
    tp = tmf
    r_lat = nb * s
    tpe_p, n_lat_p, n_all_p = s // tp, r_lat // tp, (r_lat + nb * nc) // tp
    me = 4 * lax.axis_index("x") + 2 * lax.axis_index("y") + lax.axis_index("c")
    cs = _consts()
    ncol = w_ada.shape[2]
    fsh = ffn1_w1.shape[2]
    assert nb + 1 <= 8 and NDEV * fsh == FF and NDEV * ncol == NMOD * D and s % nc == 0 and nc % tm == 0

    def t16(a):
        return a.T.astype(BF16)

    wpack1 = jnp.concatenate([t16(ffn1_w1[0]), t16(ffn1_w3[0]), ffn1_w2[0].astype(BF16)], axis=0)
    a_loc = jnp.concatenate([c, c_ctx[None, :], jnp.zeros((7 - nb, D), F32)], axis=0)
    a_raw, _, mod_all, wall1 = _ada_front(a_loc, w_ada[0], lax.dynamic_slice_in_dim(b_ada, me * ncol, ncol, axis=1),
                                          wpack1)
    a_raw = a_raw.reshape(NDEV * 8, D)
    mod_mine = lax.dynamic_slice_in_dim(mod_all, 8 * me, 8, axis=1)
    modtab = mod_mine.transpose(1, 0, 2).reshape(8, NMOD, D)[:nb + 1]
    wpack2 = jnp.concatenate([
        t16(ffn2_w1[0]), t16(ffn2_w3[0]), ffn2_w2[0].astype(BF16),
        t16(w_in[0]), jnp.zeros((12, D), BF16),
        w_out[0].astype(BF16),
        t16(w_uq[0]).reshape(24, D), jnp.zeros((8, D), BF16),
        t16(w_ukv[0]).reshape(16, D)], axis=0)

    def head_w(wn):
        return jnp.tile(jnp.pad(wn, ((0, 0), (0, LANE - DH))), (1, H))

    wq, wk = head_w(q_norm_w), head_w(k_norm_w)
    wv = v_norm_w.reshape(1, G * GD)
    ws16 = w_s[0].astype(BF16)
    wst16 = w_s[0].transpose(0, 2, 1).astype(BF16)
    bias = jnp.repeat(b_s[0].T, GD, axis=1)
    cos, sin = _rope_tables(s)
    cos = jnp.pad(cos, ((0, 0), (DN, LANE - DH)), constant_values=1.0)
    sin = jnp.pad(sin, ((0, 0), (DN, LANE - DH)))
    cos_k = jnp.concatenate([cos, jnp.ones((tm, LANE), F32)], axis=0)
    sin_k = jnp.concatenate([sin, jnp.zeros((tm, LANE), F32)], axis=0)

    xs = (x.reshape(r_lat, D), ctx.reshape(nb * nc, D))
    x1, a1, b1, o1, wall2 = _ffn_fwd(xs, modtab[:, 0:3], norm1_w, wall1, 0, tm=tmf, n_tiles=(r_lat + nb * nc) // tmf,
                                     tpe=s // tmf, n_lat=r_lat // tmf, name="ffn1_fwd", gather=wpack2)

    o0 = 3 * fsh
    wint = wall2[:, o0:o0 + 180].reshape(IN_COLS, D)
    z = lambda n: jnp.zeros((n, D), BF16)
    wint = jnp.concatenate([wint[0:128], wint[160:416], wint[416:928], wint[928:1440],
                            z(DN), wint[128:160], z(LANE - DH)], axis=0)
    wout = wall2[:, o0 + 192:o0 + 320].reshape(D, D)
    wout = jnp.concatenate([_head_pad(wout[:H * DV].reshape(H, DV, D), DV), wout[H * DV:]], axis=0)
    wuq = _head_pad(wall2[:, o0 + 320:o0 + 344].reshape(H, DH, QL), DH)
    wukvt = wall2[:, o0 + 352:o0 + 368].reshape(H, DN + DV, KVL)
    wukv = jnp.concatenate([_head_pad(wukvt[:, :DN], DN), _head_pad(wukvt[:, DN:], DV)], axis=0)

    ckv, qp, u_raw, v_raw, kpe = _proj_fwd(x1, modtab[:, 3:5], norm2_w, wint, tm=tp, n_tiles=n_all_p, tpe=tpe_p)
    q = _q_prep_fwd(qp, q_a_norm_w, wuq, wq, cos, sin, cs, tm=tm, n_lat=n_lat, tpe=tpe)
    k, v = _kv_prep_fwd(ckv, kpe, kv_a_norm_w, wukv, wk, cos_k, sin_k, cs,
                        tm=tm, n_tiles=n_all, tpe=tpe, n_lat=n_lat)
    attn, lse = _attn_fwd(q, k, v, nb=nb, s=s, nc=nc, tq=tm, ck=2048)
    x2, mix = _mix_fwd(u_raw, v_raw, attn, x1, modtab[:nb, 5:6], wv, ws16, bias, wout, cs,
                       tm=tp, n_lat=n_lat_p, tpe=tpe_p)
    dy, a2, b2, o2, lsum = _ffn_fwd((x2,), modtab[:nb, 6:9], norm3_w, wall2, 0, tm=tmf, n_tiles=r_lat // tmf,
                                    tpe=s // tmf, n_lat=r_lat // tmf, name="ffn2_fwd",
                                    target=loss_target.reshape(r_lat, D))
    loss = lax.psum(lsum[0, 0] * (0.5 / D), ("x", "y", "c"))

    tr = 2 * tm if n_lat % 2 == 0 and n_all % 2 == 0 else tm
    dx2, da2, db2, g2, do2, h2, dmod678, dnorm3 = _ffn_bwd_dx(
        dy, (x2,), a2, b2, o2, modtab[:nb, 6:9], norm3_w, wall2, 0,
        tm=tm, n_tiles=n_lat, tpe=tpe, n_lat=n_lat, name="ffn2_bwd_dx")
    g_ffn2 = _ffn_bwd_dw(h2, do2, da2, db2, g2, tr=tr, name="ffn2_bwd_dw")
    part_ffn2 = _add_sibling(g_ffn2, _scatter_sibling([g_ffn2], "scatter_sibling_ffn2")[0], 176, "add_sibling_ffn2")

    dattn, du, dv, dgate5, dwout, dws, dbs, dwv = _mix_bwd(
        dx2, mix, u_raw, v_raw, attn, modtab[:nb, 5:6], wv, ws16, wst16, bias, wout, cs, tm=tp, n_lat=n_lat_p, tpe=tpe_p)
    tq = 2 * tm if s % (2 * tm) == 0 else tm
    dq, dk_l, dk_c, dv_l, dv_c, recv_ffn2 = _attn_bwd(q, k, v, attn, dattn, lse, part_ffn2,
                                                      nb=nb, s=s, nc=nc, tq=tq, ck=1024)
    dqp, dwuq, dqa, dwq = _q_prep_bwd(dq, qp, q_a_norm_w, wuq, wq, cos, sin, cs, tm=tm, n_lat=n_lat, tpe=tpe)
    dckv, dkpe, dwukv, dkva, dwk = _kv_prep_bwd((dk_l, dk_c), (dv_l, dv_c), ckv, kpe, kv_a_norm_w, wukv, wk,
                                                cos_k, sin_k, cs, tm=tm, n_tiles=n_all, tpe=tpe, n_lat=n_lat)
    dx1, dwin, dmod34, dnorm2 = _proj_bwd(dckv, dkpe, dqp, du, dv, dx2, x1, modtab[:, 3:5], norm2_w, wint,
                                          tm=tp, n_tiles=n_all_p, tpe=tpe_p, n_lat=n_lat_p)
    dx0, da1, db1, g1, do1, h1, dmod012, dnorm1 = _ffn_bwd_dx(
        dx1, xs, a1, b1, o1, modtab[:, 0:3], norm1_w, wall1, 0,
        tm=tm, n_tiles=n_all, tpe=tpe, n_lat=n_lat, name="ffn1_bwd_dx")
    g_ffn1 = _ffn_bwd_dw(h1, do1, da1, db1, g1, tr=tr, name="ffn1_bwd_dw")
    grad_x = dx0.reshape(nb, s, D)

    zrow = jnp.zeros((1, D), F32)
    g_lat = jnp.concatenate([dmod012[:nb, 0], dmod012[:nb, 1], dmod012[:nb, 2], dmod34[:nb, 0], dmod34[:nb, 1],
                             dgate5[:, 0], dmod678[:, 0], dmod678[:, 1], dmod678[:, 2]], axis=1)
    g_ctx = jnp.concatenate([dmod012[nb:, 0], dmod012[nb:, 1], dmod012[nb:, 2], dmod34[nb:, 0], dmod34[nb:, 1],
                             zrow, zrow, zrow, zrow], axis=1)
    g_loc = jnp.concatenate([g_lat, g_ctx, jnp.zeros((7 - nb, NMOD * D), F32)], axis=0)

    def blocks(a):
        return a.reshape(NDEV, a.shape[0] // NDEV, D)

    dwin_o = jnp.concatenate([dwin[0:128], dwin[KPE_LO:KPE_LO + DR], dwin[128:384], dwin[384:896], dwin[896:1408]],
                             axis=0)
    dwout_o = jnp.concatenate([dwout[:HP].reshape(H, LANE, D)[:, :DV].reshape(H * DV, D), dwout[HP:]], axis=0)
    dwuq_o = dwuq.reshape(H, LANE, QL)[:, :DH]
    dwukv_o = jnp.concatenate([dwukv[:HP].reshape(H, LANE, KVL)[:, :DN], dwukv[HP:].reshape(H, LANE, KVL)[:, :DV]],
                              axis=1)
    gmisc = jnp.concatenate([
        blocks(dwin_o).astype(BF16), jnp.zeros((NDEV, 12, D), BF16),
        blocks(dwout_o).astype(BF16),
        dwuq_o.reshape(NDEV, 24, D).astype(BF16), jnp.zeros((NDEV, 8, D), BF16),
        dwukv_o.reshape(NDEV, 16, D).astype(BF16)], axis=1)
    got_ffn1, got_misc, g_all = _scatter_sibling([g_ffn1, gmisc], "scatter_sibling", gather=g_loc)
    g_all = g_all.reshape(NDEV * 8, NMOD * D)
    g_cols = lax.dynamic_slice_in_dim(g_all, me * ncol, ncol, axis=1)
    g_w_ada, pc_ctx, g_b_ada = _ada_bwd(a_raw, c_ctx.reshape(D, 1), g_all, g_cols, w_ada[0], nb)
    parts = [_add_sibling(g_ffn1, got_ffn1, 176, "add_sibling_ffn1"), _add_sibling(gmisc, got_misc, 368, "add_sibling_misc")]

    def prow(a):
        a = a.reshape(1, -1)
        return jnp.concatenate([a, jnp.zeros((1, D - a.shape[1]), F32)], axis=1)

    g_qn = dwq.reshape(H, LANE)[:, :DH].sum(0)
    g_kn = dwk.reshape(H, LANE)[:, :DH].sum(0)
    spack = jnp.concatenate([
        dnorm1, dnorm2, dnorm3, prow(dqa), prow(dkva), prow(g_qn), prow(g_kn), prow(dwv),
        prow(dbs[:, :G].T), prow(pc_ctx), jnp.zeros((6, D), F32), dws.reshape(CH, D)], axis=0)
    recv_ffn1, recv_misc, small_all = _scatter_chips(parts, "scatter_chips", gather=spack)
    recv = (recv_ffn1, recv_misc)
    ssum = _sum_slots(small_all, 144, "sum_small")
    gsum1 = _sum_chips(parts[0], recv[0], 176, "sum_grads_ffn1")
    gsum2 = _sum_chips(part_ffn2, recv_ffn2, 176, "sum_grads_ffn2")
    msum = _sum_chips(parts[1], recv[1], 368, "sum_grads_misc")

    transposed = ("ffn1_w1", "ffn1_w3", "ffn2_w1", "ffn2_w3", "w_in", "w_uq")
    g_big = {
        "ffn1_w1": gsum1[0:fsh], "ffn1_w3": gsum1[fsh:2 * fsh], "ffn1_w2": gsum1[2 * fsh:3 * fsh],
        "ffn2_w1": gsum2[0:fsh], "ffn2_w3": gsum2[fsh:2 * fsh], "ffn2_w2": gsum2[2 * fsh:3 * fsh],
        "w_in": msum[0:180], "w_out": msum[192:320],
        "w_uq": msum[320:344].reshape(DH, QL), "w_ukv": msum[352:368].reshape(DN + DV, KVL).T,
        "w_ada": g_w_ada,
    }

    big_in = {
        "w_ada": (w_ada, m_w_ada, v_w_ada), "ffn1_w1": (ffn1_w1, m_ffn1_w1, v_ffn1_w1),
        "ffn1_w3": (ffn1_w3, m_ffn1_w3, v_ffn1_w3), "ffn1_w2": (ffn1_w2, m_ffn1_w2, v_ffn1_w2),
        "w_in": (w_in, m_w_in, v_w_in), "w_uq": (w_uq, m_w_uq, v_w_uq), "w_ukv": (w_ukv, m_w_ukv, v_w_ukv),
        "w_out": (w_out, m_w_out, v_w_out), "ffn2_w1": (ffn2_w1, m_ffn2_w1, v_ffn2_w1),
        "ffn2_w3": (ffn2_w3, m_ffn2_w3, v_ffn2_w3), "ffn2_w2": (ffn2_w2, m_ffn2_w2, v_ffn2_w2),
    }
    res = {}
    for nm, (w, m, v_) in big_in.items():
        g = g_big[nm]
        if nm in transposed:
            d_, m_, v2_ = _adamw(w[0].T, g, m[0].T, v_[0].T, "adamw_" + nm)
            res[nm] = tuple(a.T[None] for a in (g, d_, m_, v2_))
        else:
            d_, m_, v2_ = _adamw(w[0], g, m[0], v_[0], "adamw_" + nm)
            res[nm] = tuple(a[None] for a in (g, d_, m_, v2_))

    small_in = [
        ("c_ctx", c_ctx, m_c_ctx, v_c_ctx, ssum[9:10], (1, D)),
        ("b_ada", b_ada, m_b_ada, v_b_ada, g_b_ada, (1, NMOD * D)),
        ("norm1_w", norm1_w, m_norm1_w, v_norm1_w, ssum[0:1], (1, D)),
        ("norm2_w", norm2_w, m_norm2_w, v_norm2_w, ssum[1:2], (1, D)),
        ("norm3_w", norm3_w, m_norm3_w, v_norm3_w, ssum[2:3], (1, D)),
        ("q_a_norm_w", q_a_norm_w, m_q_a_norm_w, v_q_a_norm_w, ssum[3:4, :QL], (1, QL)),
        ("kv_a_norm_w", kv_a_norm_w, m_kv_a_norm_w, v_kv_a_norm_w, ssum[4:5, :KVL], (1, KVL)),
        ("q_norm_w", q_norm_w, m_q_norm_w, v_q_norm_w, ssum[5:6, :DH], (1, DH)),
        ("k_norm_w", k_norm_w, m_k_norm_w, v_k_norm_w, ssum[6:7, :DH], (1, DH)),
        ("v_norm_w", v_norm_w, m_v_norm_w, v_v_norm_w, ssum[7:8, :G * GD], (G, GD)),
        ("b_s", b_s, m_b_s, v_b_s, ssum[8:9], (G, CH)),
        ("w_s", w_s, m_w_s, v_w_s, ssum[16:144], (G * CH, CH)),
    ]
    small_out = _adamw_small(
        [(w.reshape(sh), g.reshape(sh), m.reshape(sh), v_.reshape(sh)) for _, w, m, v_, g, sh in small_in])
    for (nm, w, *_), outs in zip(small_in, small_out):
        res[nm] = tuple(a.reshape(w.shape) for a in outs)

    order = ["c_ctx", "w_ada", "b_ada", "norm1_w", "ffn1_w1", "ffn1_w3", "ffn1_w2", "norm2_w", "w_in", "q_a_norm_w",
             "w_uq", "kv_a_norm_w", "w_ukv", "q_norm_w", "k_norm_w", "v_norm_w", "w_s", "b_s", "w_out", "norm3_w",
             "ffn2_w1", "ffn2_w3", "ffn2_w2"]
    return (loss, grad_x, *[res[n][0] for n in order], *[res[n][1] for n in order],
            *[res[n][2] for n in order], *[res[n][3] for n in order])
```

```python
import numpy as np
import jax
import jax.numpy as jnp
from jax import lax
from jax.experimental import pallas as pl
from jax.experimental.pallas import tpu as pltpu

F32 = jnp.float32
BF16 = jnp.bfloat16

D = 1024
FF = 2816
FC = 256
H = 8
DN, DR, DV = 64, 32, 64
DH = DN + DR
QL, KVL = 256, 128
G, GD, CH = 8, 64, 128
NMOD = 9
EPS = 1e-6
GRID_W = 64
ROPE_BASE = 10000.0
NDEV = 8
LANE = 128
HP = H * LANE
IN_COLS = 1440
WIN_ROWS = 1536
KPE_LO = 1408 + DN
MIB = 1 << 20

ADAM_LR, ADAM_B1, ADAM_B2, ADAM_EPS, ADAM_WD, ADAM_STEP = 0.001, 0.9, 0.999, 1e-08, 0.01, 10

MESH = pl.DeviceIdType.MESH
ANY = pl.BlockSpec(memory_space=pl.ANY)
VMEM = pl.BlockSpec(memory_space=pltpu.VMEM)


def _mm(a, b):
    return jnp.dot(a, b, preferred_element_type=F32)


def _mm_nt(a, b):
    return lax.dot_general(a, b, (((1,), (1,)), ((), ())), preferred_element_type=F32)


def _mm_tn(a, b):
    return lax.dot_general(a, b, (((0,), (0,)), ((), ())), preferred_element_type=F32)


def _dot_hl(x, m):
    hi = x.astype(BF16)
    lo = (x - hi.astype(F32)).astype(BF16)
    return _mm(hi, m) + _mm(lo, m)


def _sigmoid(a):
    return 1.0 / (1.0 + jnp.exp(-a))


_G0 = 0.7978845608028654
_G1 = 0.044715


def _gelu(x):
    return 0.5 * x * (1.0 + jnp.tanh(_G0 * (x + _G1 * (x * x * x))))


def _gelu_grad(x):
    th = jnp.tanh(_G0 * (x + _G1 * (x * x * x)))
    return 0.5 * (1.0 + th) + 0.5 * x * (1.0 - th * th) * (_G0 * (1.0 + 3.0 * _G1 * x * x))


def _rowsum(y):
    return jnp.sum(y, axis=0, keepdims=True)


def _rms(x):
    return lax.rsqrt(jnp.mean(x * x, axis=-1, keepdims=True) + EPS)


def _pcall(body, *, name, out_shape, in_specs, out_specs, grid=None, scratch=(), vmem_mb=32, aliases=None):
    kw = {}
    if grid is not None:
        kw["grid"] = grid
        sem = ("arbitrary",) * len(grid)
    else:
        sem = None
    if aliases:
        kw["input_output_aliases"] = aliases
    return pl.pallas_call(
        body, name=name, out_shape=out_shape, in_specs=in_specs, out_specs=out_specs,
        scratch_shapes=list(scratch),
        compiler_params=pltpu.CompilerParams(dimension_semantics=sem, vmem_limit_bytes=vmem_mb * MIB),
        **kw)


def _const(shape):
    nd = len(shape)
    return pl.BlockSpec(shape, lambda *_: (0,) * nd)


def _sds(shape, dt):
    return jax.ShapeDtypeStruct(shape, dt)


def _consts():
    seg_h = np.zeros((HP, LANE), np.float32)
    seg_h[np.arange(HP), np.arange(HP) // LANE] = 1.0
    seg_g = np.zeros((G * GD, LANE), np.float32)
    seg_g[np.arange(G * GD), np.arange(G * GD) // GD] = 1.0
    rot = np.zeros((LANE, LANE), np.float32)
    for base in (DN, DN + 16):
        for j in range(8):
            rot[base + j + 8, base + j] = -1.0
            rot[base + j, base + j + 8] = 1.0
    rot2 = np.zeros((2 * LANE, 2 * LANE), np.float32)
    rot2[:LANE, :LANE] = rot
    rot2[LANE:, LANE:] = rot
    twice = lambda m: np.concatenate([m, m], axis=0)
    c = dict(seg_h=seg_h, seg_ht=twice(seg_h.T), seg_g=seg_g, seg_gt=twice(seg_g.T), rot=rot2, rot_t=rot2.T)
    return {k: jnp.asarray(v, BF16) for k, v in c.items()}


_GATHER_SEMS = [pltpu.SemaphoreType.DMA((7,)), pltpu.SemaphoreType.DMA((7,)), pltpu.SemaphoreType.DMA(())]


def _gather_phases(x_ref, out_ref, send_sems, recv_sems, local_sem):
    mx, my, mc = lax.axis_index("x"), lax.axis_index("y"), lax.axis_index("c")
    me, sibling = (mx, my, mc), (mx, my, 1 - mc)
    chips = [(1 - mx, my), (mx, 1 - my), (1 - mx, 1 - my)]

    def blk(px, py, pc):
        return out_ref.at[4 * px + 2 * py + pc]

    def copy(k, block, to, src=None):
        return pltpu.make_async_remote_copy(
            src_ref=blk(*block) if src is None else src, dst_ref=blk(*block),
            send_sem=send_sems.at[k], recv_sem=recv_sems.at[k], device_id=to, device_id_type=MESH)

    mine = pltpu.make_async_copy(x_ref, blk(*me), local_sem)
    first = [copy(0, me, sibling, src=x_ref)]
    first += [copy(1 + j, me, (*chip, mc), src=x_ref) for j, chip in enumerate(chips)]
    passed = [copy(4 + j, (*chip, mc), sibling) for j, chip in enumerate(chips)]

    def start():
        mine.start()
        for cp in first:
            cp.start()

    def forward():
        for j, chip in enumerate(chips):
            copy(1 + j, (*chip, mc), me).wait_recv()
            passed[j].start()

    def finish():
        copy(0, sibling, me).wait_recv()
        for j, chip in enumerate(chips):
            copy(4 + j, (*chip, 1 - mc), me).wait_recv()
        for cp in first + passed:
            cp.wait_send()
        mine.wait()

    return start, forward, finish


def _chip_sends(p_ref, out_ref, send_sems, recv_sems):
    mx, my, mc = lax.axis_index("x"), lax.axis_index("y"), lax.axis_index("c")
    peers = [(1 - mx, my), (mx, 1 - my), (1 - mx, 1 - my)]
    return [pltpu.make_async_remote_copy(
        src_ref=p_ref.at[2 * px + py], dst_ref=out_ref.at[j], send_sem=send_sems.at[j], recv_sem=recv_sems.at[j],
        device_id=(px, py, mc), device_id_type=MESH) for j, (px, py) in enumerate(peers)]


def _with_gather(copies_of, n, shapes, sems, gather, name, args):
    ns = len(sems)

    def body(*refs):
        ng = 1 if gather is not None else 0
        ins, outs = refs[:n], refs[n + ng:2 * n + ng]
        copies = copies_of(ins, outs, refs[2 * n + 2 * ng:2 * n + 2 * ng + ns])
        if ng:
            start, forward, finish = _gather_phases(refs[n], refs[2 * n + 1], *refs[2 * n + 2 + ns:])
            start()
        for cp in copies:
            cp.start()
        if ng:
            forward()
        for cp in copies:
            cp.wait_recv()
        for cp in copies:
            cp.wait_send()
        if ng:
            finish()

    in_specs, out_shape, scratch = [ANY] * n, list(shapes), list(sems)
    if gather is not None:
        in_specs.append(ANY)
        args = list(args) + [gather]
        out_shape.append(_sds((NDEV,) + gather.shape, gather.dtype))
        scratch += _GATHER_SEMS
    return pl.pallas_call(body, name=name, out_shape=tuple(out_shape), in_specs=in_specs,
                          out_specs=(ANY,) * len(out_shape), scratch_shapes=scratch)(*args)


def _scatter_sibling(xs, name, gather=None):
    n = len(xs)

    def copies_of(x_refs, got_refs, sems):
        send_sems, recv_sems = sems
        mx, my, mc = lax.axis_index("x"), lax.axis_index("y"), lax.axis_index("c")
        return [pltpu.make_async_remote_copy(
            src_ref=x_refs[i].at[2 * j + 1 - mc], dst_ref=got_refs[i].at[j],
            send_sem=send_sems.at[4 * i + j], recv_sem=recv_sems.at[4 * i + j],
            device_id=(mx, my, 1 - mc), device_id_type=MESH) for i in range(n) for j in range(4)]

    shapes = tuple(_sds((4,) + x.shape[1:], x.dtype) for x in xs)
    return _with_gather(copies_of, n, shapes, [pltpu.SemaphoreType.DMA((4 * n,))] * 2, gather, name, xs)


def _scatter_chips(ps, name, gather=None):
    n = len(ps)

    def copies_of(p_refs, out_refs, sems):
        sends = []
        for i in range(n):
            sends += _chip_sends(p_refs[i], out_refs[i], sems[2 * i], sems[2 * i + 1])
        return sends

    shapes = tuple(_sds((3,) + p.shape[1:], p.dtype) for p in ps)
    return _with_gather(copies_of, n, shapes, [pltpu.SemaphoreType.DMA((3,))] * (2 * n), gather, name, ps)


def _add_sibling(x, got, tr, name):
    _, r, c = x.shape

    def body(x_ref, g_ref, o_ref):
        mc = lax.axis_index("c")
        for j in range(4):
            mine = jnp.where(mc == 0, x_ref[2 * j].astype(F32), x_ref[2 * j + 1].astype(F32))
            o_ref[j] = (mine + g_ref[j].astype(F32)).astype(o_ref.dtype)

    return _pcall(body, name=name, grid=(r // tr,), out_shape=_sds(got.shape, got.dtype),
                  in_specs=[pl.BlockSpec((NDEV, tr, c), lambda t: (0, t, 0)), pl.BlockSpec((4, tr, c), lambda t: (0, t, 0))],
                  out_specs=pl.BlockSpec((4, tr, c), lambda t: (0, t, 0)))(x, got)


def _sum_chips(part, recv, tr, name):
    _, r, c = part.shape

    def body(p_ref, r_ref, o_ref):
        slot = 2 * lax.axis_index("x") + lax.axis_index("y")
        acc = p_ref[0].astype(F32)
        for j in range(1, 4):
            acc = jnp.where(slot == j, p_ref[j].astype(F32), acc)
        for j in range(3):
            acc = acc + r_ref[j].astype(F32)
        o_ref[...] = acc

    return _pcall(body, name=name, grid=(r // tr,), out_shape=_sds((r, c), F32),
                  in_specs=[pl.BlockSpec((4, tr, c), lambda t: (0, t, 0)), pl.BlockSpec((3, tr, c), lambda t: (0, t, 0))],
                  out_specs=pl.BlockSpec((tr, c), lambda t: (t, 0)))(part, recv)


def _sum_slots(x, tr, name):
    n, r, c = x.shape

    def body(x_ref, o_ref):
        acc = x_ref[0].astype(F32)
        for s in range(1, n):
            acc = acc + x_ref[s].astype(F32)
        o_ref[...] = acc

    return _pcall(body, name=name, grid=(r // tr,), out_shape=_sds((r, c), F32),
                  in_specs=[pl.BlockSpec((n, tr, c), lambda t: (0, t, 0))],
                  out_specs=pl.BlockSpec((tr, c), lambda t: (t, 0)))(x)


def _ada_front(a_loc, w_loc, b_loc, wpack):
    ncol = w_loc.shape[1]
    nrow = NDEV * a_loc.shape[0]

    def body(a_ref, w_ref, b_ref, wp_ref, araw_ref, mloc_ref, mall_ref, wall_ref,
             a_vm, w_vm, m_vm, lsem, *sems):
        a_start, a_forward, a_finish = _gather_phases(a_ref, araw_ref, *sems[0:3])
        m_start, m_forward, m_finish = _gather_phases(mloc_ref, mall_ref, *sems[3:6])
        w_start, w_forward, w_finish = _gather_phases(wp_ref, wall_ref, *sems[6:9])
        w_in = pltpu.make_async_copy(w_ref, w_vm, lsem.at[0])
        w_in.start()
        a_start()
        w_start()
        a_forward()
        a_finish()
        a_in = pltpu.make_async_copy(araw_ref, a_vm, lsem.at[1])
        a_in.start()
        a_in.wait()
        w_in.wait()
        a = a_vm[...].reshape(nrow, D)
        act = (a * _sigmoid(a)).astype(BF16)
        m_vm[...] = _mm(act, w_vm[...].astype(BF16)) + b_ref[...]
        m_out = pltpu.make_async_copy(m_vm, mloc_ref, lsem.at[2])
        m_out.start()
        m_out.wait()
        m_start()
        m_forward()
        m_finish()
        w_forward()
        w_finish()

    return pl.pallas_call(
        body, name="ada_front",
        out_shape=(_sds((NDEV,) + a_loc.shape, F32), _sds((nrow, ncol), F32), _sds((NDEV, nrow, ncol), F32),
                   _sds((NDEV,) + wpack.shape, wpack.dtype)),
        in_specs=[ANY, ANY, VMEM, ANY], out_specs=(ANY, ANY, ANY, ANY),
        scratch_shapes=[pltpu.VMEM((NDEV,) + a_loc.shape, F32), pltpu.VMEM(w_loc.shape, F32),
                        pltpu.VMEM((nrow, ncol), F32), pltpu.SemaphoreType.DMA((3,))] + _GATHER_SEMS * 3,
        compiler_params=pltpu.CompilerParams(vmem_limit_bytes=32 * MIB),
    )(a_loc, w_loc, b_loc, wpack)


def _ada_bwd(a_raw, cctx_col, g_all, g_cols, w_loc, nb):
    nrow = a_raw.shape[0]
    ncol = w_loc.shape[1]

    def body(a_ref, cc_ref, gall_ref, g_ref, w_ref, dw_ref, pc_ref, gb_ref):
        a = a_ref[...]
        rowid = lax.broadcasted_iota(jnp.int32, (nrow, 1), 0) % 8
        act = jnp.where(rowid < nb, a * _sigmoid(a), 0.0).astype(BF16)
        g = g_ref[...]
        gc = _rowsum(jnp.where(rowid == nb, g, 0.0))
        cc = cc_ref[...]
        dw_ref[...] = _mm_tn(act, g.astype(BF16)) + (cc * _sigmoid(cc)) * gc
        pc_ref[...] = jnp.sum(w_ref[...] * gc, axis=1, keepdims=True)
        gb_ref[...] = _rowsum(gall_ref[...])

    return _pcall(body, name="ada_bwd",
                  out_shape=(_sds((D, ncol), F32), _sds((D, 1), F32), _sds((1, g_all.shape[1]), F32)),
                  in_specs=[VMEM] * 5, out_specs=(VMEM,) * 3, vmem_mb=48)(a_raw, cctx_col, g_all, g_cols, w_loc)


def _mod_spec(k, tpe, nrows):
    return pl.BlockSpec((1, k, D), lambda t: (jnp.minimum(t // tpe, nrows - 1), 0, 0))


def _load_ffn_weights(wall_ref, first, bufs, sems):
    fsh = FF // NDEV
    cps = []
    for j, buf in enumerate(bufs):
        for d in range(NDEV):
            cps.append(pltpu.make_async_copy(wall_ref.at[d, pl.ds((first + j) * fsh, fsh)],
                                             buf.at[pl.ds(d * fsh, fsh)], sems.at[j * NDEV + d]))
    for cp in cps:
        cp.start()
    for cp in cps:
        cp.wait()


def _token_specs(xs, tm, n_lat):
    specs = [pl.BlockSpec((tm, D), lambda t: (jnp.minimum(t, n_lat - 1), 0))]
    if len(xs) == 2:
        specs.append(pl.BlockSpec((tm, D), lambda t: (jnp.maximum(t - n_lat, 0), 0)))
    return specs


def _ffn_fwd(xs, mod3, norm_w, wall, first, *, tm, n_tiles, tpe, n_lat, name, target=None, gather=None):
    nrows = mod3.shape[0]
    r = n_tiles * tm
    nx = len(xs)
    with_loss = target is not None
    with_gather = gather is not None
    fwd_step = max(2 * n_tiles // 3, 1)

    def body(*refs):
        x_refs = refs[:nx]
        pos = nx
        if with_loss:
            tgt_ref = refs[pos]
            pos += 1
        mod_ref, nw_ref, wall_ref = refs[pos:pos + 3]
        pos += 3
        if with_gather:
            gin_ref = refs[pos]
            pos += 1
        xo_ref, a_ref, b_ref, o_ref = refs[pos:pos + 4]
        pos += 4
        if with_loss:
            ls_ref = refs[pos]
            pos += 1
        if with_gather:
            gout_ref = refs[pos]
            pos += 1
        w1_ref, w3_ref, w2_ref, wsem, acc_ref = refs[pos:pos + 5]
        t = pl.program_id(0)
        if with_gather:
            g_start, g_forward, g_finish = _gather_phases(gin_ref, gout_ref, *refs[pos + 5:])

        @pl.when(t == 0)
        def _():
            if with_gather:
                g_start()
            _load_ffn_weights(wall_ref, first, (w1_ref, w3_ref, w2_ref), wsem)
            if with_loss:
                ls_ref[...] = jnp.zeros_like(ls_ref)

        if with_gather:
            @pl.when(t == fwd_step)
            def _():
                g_forward()

            @pl.when(t == n_tiles - 1)
            def _():
                g_finish()

        x = x_refs[0][...]
        if nx == 2:
            x = jnp.where(t < n_lat, x, x_refs[1][...])
        n = x * _rms(x) * nw_ref[...]
        shift, scale, gate = mod_ref[0, 0:1, :], mod_ref[0, 1:2, :], mod_ref[0, 2:3, :]
        h = (n * (1.0 + scale) + shift).astype(BF16)
        nch = FF // FC
        o = None
        for lo_c, hi_c in ((0, nch // 2), (nch // 2, nch)):
            for j in range(lo_c, hi_c):
                sl = slice(j * FC, (j + 1) * FC)
                a = _mm_nt(h, w1_ref[sl, :])
                b = _mm_nt(h, w3_ref[sl, :])
                a_ref[:, sl] = a.astype(BF16)
                b_ref[:, sl] = b.astype(BF16)
                acc_ref[:, sl] = (a * _sigmoid(a) * b).astype(BF16)
            gs = slice(lo_c * FC, hi_c * FC)
            part = _mm(acc_ref[:, gs], w2_ref[gs, :])
            o = part if o is None else o + part
        o_ref[...] = o.astype(BF16)
        out = x + (0.5 * gate) * o
        if with_loss:
            d = out - tgt_ref[...]
            xo_ref[...] = d * (1.0 / D)
            ls_ref[...] += jnp.sum(d * d)
        else:
            xo_ref[...] = out

    row = lambda cols: pl.BlockSpec((tm, cols), lambda t: (t, 0))
    in_specs = _token_specs(xs, tm, n_lat) + ([row(D)] if with_loss else []) + [
        _mod_spec(3, tpe, nrows), _const((1, D)), ANY]
    out_shape = [_sds((r, D), F32), _sds((r, FF), BF16), _sds((r, FF), BF16), _sds((r, D), BF16)]
    out_specs = [row(D), row(FF), row(FF), row(D)]
    scratch = [pltpu.VMEM((FF, D), BF16)] * 3 + [pltpu.SemaphoreType.DMA((3 * NDEV,)), pltpu.VMEM((tm, FF), BF16)]
    if with_loss:
        out_shape.append(_sds((8, LANE), F32))
        out_specs.append(_const((8, LANE)))
    args = list(xs) + ([target] if with_loss else []) + [mod3, norm_w, wall]
    if with_gather:
        assert n_tiles >= 2
        in_specs.append(ANY)
        args.append(gather)
        out_shape.append(_sds((NDEV,) + gather.shape, gather.dtype))
        out_specs.append(ANY)
        scratch += _GATHER_SEMS
    return _pcall(
        body, name=name, grid=(n_tiles,), out_shape=tuple(out_shape), in_specs=in_specs, out_specs=tuple(out_specs),
        scratch=scratch, vmem_mb=56)(*args)


def _ffn_bwd_dx(dout, xs, a, b, o, mod3, norm_w, wall, first, *, tm, n_tiles, tpe, n_lat, name):
    nrows = mod3.shape[0]
    r = n_tiles * tm
    nx = len(xs)

    def body(*refs):
        dout_ref = refs[0]
        x_refs = refs[1:1 + nx]
        (a_ref, b_ref, o_ref, mod_ref, nw_ref, wall_ref,
         dx_ref, da_ref, db_ref, g_ref, do_ref, h_ref, dmod_ref, dnw_ref,
         w1_ref, w3_ref, w2_ref, wsem) = refs[1 + nx:]
        t = pl.program_id(0)

        @pl.when(t == 0)
        def _():
            _load_ffn_weights(wall_ref, first, (w1_ref, w3_ref, w2_ref), wsem)
            dnw_ref[...] = jnp.zeros_like(dnw_ref)

        @pl.when(jnp.where(t < n_lat, t % tpe == 0, t == n_lat))
        def _():
            dmod_ref[...] = jnp.zeros_like(dmod_ref)

        x = x_refs[0][...]
        if nx == 2:
            x = jnp.where(t < n_lat, x, x_refs[1][...])
        dout = dout_ref[...]
        shift, scale, gate = mod_ref[0, 0:1, :], mod_ref[0, 1:2, :], mod_ref[0, 2:3, :]
        d_o = ((0.5 * gate) * dout).astype(BF16)
        do_ref[...] = d_o
        nch = FF // FC
        groups = ((0, nch // 2), (nch // 2, nch))
        dh = None
        for lo_c, hi_c in groups:
            for j in range(lo_c, hi_c):
                sl = slice(j * FC, (j + 1) * FC)
                av = a_ref[:, sl].astype(F32)
                bv = b_ref[:, sl].astype(F32)
                dg = _mm_nt(d_o, w2_ref[sl, :])
                sig = _sigmoid(av)
                sa = av * sig
                g_ref[:, sl] = (sa * bv).astype(BF16)
                da_ref[:, sl] = (dg * bv * (sig * (1.0 + av * (1.0 - sig)))).astype(BF16)
                db_ref[:, sl] = (dg * sa).astype(BF16)
            gs = slice(lo_c * FC, hi_c * FC)
            part = _mm(da_ref[:, gs], w1_ref[gs, :]) + _mm(db_ref[:, gs], w3_ref[gs, :])
            dh = part if dh is None else dh + part
        rr = _rms(x)
        xh = x * rr
        nw = nw_ref[...]
        n = xh * nw
        h_ref[...] = (n * (1.0 + scale) + shift).astype(BF16)
        dgate = _rowsum(0.5 * o_ref[...].astype(F32) * dout)
        dn = dh * (1.0 + scale)
        dxh = dn * nw
        dmod_ref[0, 0:1, :] += _rowsum(dh)
        dmod_ref[0, 1:2, :] += _rowsum(dh * n)
        dmod_ref[0, 2:3, :] += dgate
        dnw_ref[...] += _rowsum(dn * xh)
        dx = dout + rr * (dxh - xh * jnp.mean(dxh * xh, axis=-1, keepdims=True))
        if n_tiles == n_lat:
            dx_ref[...] = dx
        else:
            @pl.when(t < n_lat)
            def _():
                dx_ref[...] = dx

    row = lambda cols: pl.BlockSpec((tm, cols), lambda t: (t, 0))
    lat = pl.BlockSpec((tm, D), lambda t: (jnp.minimum(t, n_lat - 1), 0))
    return _pcall(
        body, name=name, grid=(n_tiles,),
        out_shape=(_sds((n_lat * tm, D), F32), _sds((r, FF), BF16), _sds((r, FF), BF16), _sds((r, FF), BF16),
                   _sds((r, D), BF16), _sds((r, D), BF16), _sds((nrows, 3, D), F32), _sds((1, D), F32)),
        in_specs=[row(D)] + _token_specs(xs, tm, n_lat) + [row(FF), row(FF), row(D), _mod_spec(3, tpe, nrows),
                                                            _const((1, D)), ANY],
        out_specs=(lat, row(FF), row(FF), row(FF), row(D), row(D), _mod_spec(3, tpe, nrows), _const((1, D))),
        scratch=[pltpu.VMEM((FF, D), BF16)] * 3 + [pltpu.SemaphoreType.DMA((3 * NDEV,))],
        vmem_mb=60)(dout, *xs, a, b, o, mod3, norm_w, wall)


def _ffn_bwd_dw(h, d_o, da, db, g, *, tr, name):
    r = h.shape[0]
    fh = FF // 2
    fsh = FF // NDEV
    nk = r // tr

    def body(h_ref, do_ref, da_ref, db_ref, g_ref, out_ref, acc1, acc3, acc2):
        k = pl.program_id(1)

        @pl.when(k == 0)
        def _():
            acc1[...] = jnp.zeros_like(acc1)
            acc3[...] = jnp.zeros_like(acc3)
            acc2[...] = jnp.zeros_like(acc2)

        hv = h_ref[...]
        acc1[...] += _mm_tn(da_ref[...], hv)
        acc3[...] += _mm_tn(db_ref[...], hv)
        acc2[...] += _mm_tn(g_ref[...], do_ref[...])

        @pl.when(k == nk - 1)
        def _():
            for i, acc in enumerate((acc1, acc3, acc2)):
                out_ref[:, i * fsh:(i + 1) * fsh, :] = acc[...].reshape(NDEV // 2, fsh, D).astype(BF16)

    rowd = pl.BlockSpec((tr, D), lambda f, k: (k, 0))
    rowf = pl.BlockSpec((tr, fh), lambda f, k: (k, f))
    return _pcall(
        body, name=name, grid=(2, nk), out_shape=_sds((NDEV, 3 * fsh, D), BF16),
        in_specs=[rowd, rowd, rowf, rowf, rowf],
        out_specs=pl.BlockSpec((NDEV // 2, 3 * fsh, D), lambda f, k: (f, 0, 0)),
        scratch=[pltpu.VMEM((fh, D), F32)] * 3, vmem_mb=56)(h, d_o, da, db, g)


_PIECES = ((0, 128), (128, 384), (384, 896), (896, 1408), (1408, 1536))


def _proj_fwd(x1, mod2, norm_w, wint, *, tm, n_tiles, tpe, name="proj_fwd"):
    nrows = mod2.shape[0]
    r = n_tiles * tm

    def body(x_ref, mod_ref, nw_ref, w_ref, ckv_ref, q_ref, u_ref, v_ref, kpe_ref):
        x = x_ref[...]
        n = x * _rms(x) * nw_ref[...]
        h = (n * (1.0 + mod_ref[0, 1:2, :]) + mod_ref[0, 0:1, :]).astype(BF16)
        for (lo, hi), ref in zip(_PIECES, (ckv_ref, q_ref, u_ref, v_ref, kpe_ref)):
            ref[...] = _mm_nt(h, w_ref[lo:hi, :])

    row = lambda cols: pl.BlockSpec((tm, cols), lambda t: (t, 0))
    widths = [hi - lo for lo, hi in _PIECES]
    return _pcall(
        body, name=name, grid=(n_tiles,),
        out_shape=tuple(_sds((r, w), F32) for w in widths),
        in_specs=[row(D), _mod_spec(2, tpe, nrows), _const((1, D)), _const((WIN_ROWS, D))],
        out_specs=tuple(row(w) for w in widths), vmem_mb=40)(x1, mod2, norm_w, wint)


def _proj_bwd(dckv, dkpe, dq, du, dv, dx2, x1, mod2, norm_w, wint, *, tm, n_tiles, tpe, n_lat, name="proj_bwd"):
    nrows = mod2.shape[0]
    r = n_tiles * tm

    def body(dckv_ref, dkpe_ref, dq_ref, du_ref, dv_ref, dx2_ref, x_ref, mod_ref, nw_ref, w_ref,
             dx_ref, dw_ref, dmod_ref, dnw_ref):
        t = pl.program_id(0)
        is_lat = t < n_lat

        @pl.when(t == 0)
        def _():
            dw_ref[...] = jnp.zeros_like(dw_ref)
            dnw_ref[...] = jnp.zeros_like(dnw_ref)

        @pl.when(jnp.where(is_lat, t % tpe == 0, t == n_lat))
        def _():
            dmod_ref[...] = jnp.zeros_like(dmod_ref)

        x = x_ref[...]
        rr = _rms(x)
        xh = x * rr
        nw = nw_ref[...]
        n = xh * nw
        scale = mod_ref[0, 1:2, :]
        h = (n * (1.0 + scale) + mod_ref[0, 0:1, :]).astype(BF16)
        zero = jnp.zeros((), BF16)
        pieces = (dckv_ref[...], jnp.where(is_lat, dq_ref[...], zero), jnp.where(is_lat, du_ref[...], zero),
                  jnp.where(is_lat, dv_ref[...], zero), dkpe_ref[...])
        dh = None
        for (lo, hi), piece in zip(_PIECES, pieces):
            part = _mm(piece, w_ref[lo:hi, :])
            dh = part if dh is None else dh + part
        dn = dh * (1.0 + scale)
        dxh = dn * nw
        dx = rr * (dxh - xh * jnp.mean(dxh * xh, axis=-1, keepdims=True))
        dx_ref[...] = dx + jnp.where(is_lat, dx2_ref[...], 0.0)
        dmod_ref[0, 0:1, :] += _rowsum(dh)
        dmod_ref[0, 1:2, :] += _rowsum(dh * n)
        dnw_ref[...] += _rowsum(dn * xh)
        for (lo, hi), piece in zip(_PIECES, pieces):
            dw_ref[lo:hi, :] += _mm_tn(piece, h)

    row = lambda cols: pl.BlockSpec((tm, cols), lambda t: (t, 0))
    lat = lambda cols: pl.BlockSpec((tm, cols), lambda t: (jnp.minimum(t, n_lat - 1), 0))
    return _pcall(
        body, name=name, grid=(n_tiles,),
        out_shape=(_sds((r, D), F32), _sds((WIN_ROWS, D), F32), _sds((nrows, 2, D), F32), _sds((1, D), F32)),
        in_specs=[row(128), row(128), lat(256), lat(512), lat(512), lat(D), row(D), _mod_spec(2, tpe, nrows),
                  _const((1, D)), _const((WIN_ROWS, D))],
        out_specs=(row(D), _const((WIN_ROWS, D)), _mod_spec(2, tpe, nrows), _const((1, D))),
        vmem_mb=48)(dckv, dkpe, dq, du, dv, dx2, x1, mod2, norm_w, wint)


def _seg_sum(x, seg):
    return _mm(x.astype(BF16), seg)


def _seg_bcast(v, segt2):
    hi = v.astype(BF16)
    lo = (v - hi.astype(F32)).astype(BF16)
    return _mm(jnp.concatenate([hi, lo], axis=-1), segt2)


def _rope_pairs(t, cos, sin, rot2):
    cos2, sin2 = jnp.concatenate([cos, cos], axis=-1), jnp.concatenate([sin, sin], axis=-1)
    out = []
    for j in range(H // 2):
        tj = t[:, 2 * j * LANE:2 * (j + 1) * LANE]
        out.append(tj * cos2 + _dot_hl(tj, rot2) * sin2)
    return jnp.concatenate(out, axis=-1)


def _head_norm_rope(x, w_pad, cos, sin, seg, segt2, rot2, rope=True):
    rh = lax.rsqrt(_seg_sum(x * x, seg) * (1.0 / DH) + EPS)
    rb = _seg_bcast(rh, segt2)
    y = x * rb
    out = _rope_pairs(y * w_pad, cos, sin, rot2) if rope else None
    return out, y, rb


def _head_norm_rope_bwd(dout, y, rb, w_pad, cos, sin, seg, segt2, rot2_t):
    cos2, sin2 = jnp.concatenate([cos, cos], axis=-1), jnp.concatenate([sin, sin], axis=-1)
    dt = []
    for j in range(H // 2):
        dj = dout[:, 2 * j * LANE:2 * (j + 1) * LANE]
        dt.append(dj * cos2 + _dot_hl(dj * sin2, rot2_t))
    dt = jnp.concatenate(dt, axis=-1)
    dw = _rowsum(dt * y)
    dy = dt * w_pad
    mean_h = _seg_sum(dy * y, seg) * (1.0 / DH)
    return rb * (dy - y * _seg_bcast(mean_h, segt2)), dw


def _q_prep_fwd(qp, qa_w, wuq, wq, cos, sin, cs, *, tm, n_lat, tpe):
    def body(qp_ref, qa_ref, wuq_ref, wq_ref, cos_ref, sin_ref, seg, segt, rot, q_ref):
        x = qp_ref[...]
        cq = (x * _rms(x) * qa_ref[...]).astype(BF16)
        q, _, _ = _head_norm_rope(_mm_nt(cq, wuq_ref[...]), wq_ref[...], cos_ref[...], sin_ref[...],
                                  seg[...], segt[...], rot[...])
        q_ref[...] = q.astype(BF16)

    row = lambda cols: pl.BlockSpec((tm, cols), lambda t: (t, 0))
    tab = pl.BlockSpec((tm, LANE), lambda t: (t % tpe, 0))
    return _pcall(
        body, name="q_prep_fwd", grid=(n_lat,), out_shape=_sds((n_lat * tm, HP), BF16),
        in_specs=[row(QL), _const((1, QL)), _const((HP, QL)), _const((1, HP)), tab, tab,
                  _const((HP, LANE)), _const((2 * LANE, HP)), _const((2 * LANE, 2 * LANE))],
        out_specs=row(HP))(qp, qa_w, wuq, wq, cos, sin, cs["seg_h"], cs["seg_ht"], cs["rot"])


def _q_prep_bwd(dq, qp, qa_w, wuq, wq, cos, sin, cs, *, tm, n_lat, tpe):
    def body(dq_ref, qp_ref, qa_ref, wuq_ref, wq_ref, cos_ref, sin_ref, seg, segt, rot, rot_t,
             dqp_ref, dwuq_ref, dqa_ref, dwq_ref):
        @pl.when(pl.program_id(0) == 0)
        def _():
            dwuq_ref[...] = jnp.zeros_like(dwuq_ref)
            dqa_ref[...] = jnp.zeros_like(dqa_ref)
            dwq_ref[...] = jnp.zeros_like(dwq_ref)

        x = qp_ref[...]
        ra = _rms(x)
        xh = x * ra
        qa = qa_ref[...]
        cq = (xh * qa).astype(BF16)
        wuq_v = wuq_ref[...]
        wq_v, cos_v, sin_v = wq_ref[...], cos_ref[...], sin_ref[...]
        _, y, rb = _head_norm_rope(_mm_nt(cq, wuq_v), wq_v, cos_v, sin_v, seg[...], segt[...], rot[...], rope=False)
        dqraw, dwq = _head_norm_rope_bwd(dq_ref[...], y, rb, wq_v, cos_v, sin_v, seg[...], segt[...], rot_t[...])
        dqraw = dqraw.astype(BF16)
        dcq = _mm(dqraw, wuq_v)
        dxh = dcq * qa
        dqp_ref[...] = (ra * (dxh - xh * jnp.mean(dxh * xh, axis=-1, keepdims=True))).astype(BF16)
        dwuq_ref[...] += _mm_tn(dqraw, cq)
        dqa_ref[...] += _rowsum(dcq * xh)
        dwq_ref[...] += dwq

    row = lambda cols: pl.BlockSpec((tm, cols), lambda t: (t, 0))
    tab = pl.BlockSpec((tm, LANE), lambda t: (t % tpe, 0))
    return _pcall(
        body, name="q_prep_bwd", grid=(n_lat,),
        out_shape=(_sds((n_lat * tm, QL), BF16), _sds((HP, QL), F32), _sds((1, QL), F32), _sds((1, HP), F32)),
        in_specs=[row(HP), row(QL), _const((1, QL)), _const((HP, QL)), _const((1, HP)), tab, tab,
                  _const((HP, LANE)), _const((2 * LANE, HP)), _const((2 * LANE, 2 * LANE)), _const((2 * LANE, 2 * LANE))],
        out_specs=(row(QL), _const((HP, QL)), _const((1, QL)), _const((1, HP))), vmem_mb=40)(
            dq, qp, qa_w, wuq, wq, cos, sin, cs["seg_h"], cs["seg_ht"], cs["rot"], cs["rot_t"])


def _kv_tab_spec(tm, tpe, n_lat):
    return pl.BlockSpec((tm, LANE), lambda t: (jnp.where(t < n_lat, t % tpe, tpe), 0))


def _kv_prep_fwd(ckv, kpe, kva_w, wukv, wk, cosk, sink, cs, *, tm, n_tiles, tpe, n_lat):
    def body(ckv_ref, kpe_ref, kva_ref, wukv_ref, wk_ref, cos_ref, sin_ref, seg, segt, rot, k_ref, v_ref):
        x = ckv_ref[...]
        ckvn = (x * _rms(x) * kva_ref[...]).astype(BF16)
        kv = _mm_nt(ckvn, wukv_ref[...])
        kx = kv[:, :HP] + jnp.concatenate([kpe_ref[...]] * H, axis=-1)
        k, _, _ = _head_norm_rope(kx, wk_ref[...], cos_ref[...], sin_ref[...], seg[...], segt[...], rot[...])
        k_ref[...] = k.astype(BF16)
        v_ref[...] = kv[:, HP:].astype(BF16)

    row = lambda cols: pl.BlockSpec((tm, cols), lambda t: (t, 0))
    tab = _kv_tab_spec(tm, tpe, n_lat)
    r = n_tiles * tm
    return _pcall(
        body, name="kv_prep_fwd", grid=(n_tiles,), out_shape=(_sds((r, HP), BF16), _sds((r, HP), BF16)),
        in_specs=[row(KVL), row(LANE), _const((1, KVL)), _const((2 * HP, KVL)), _const((1, HP)), tab, tab,
                  _const((HP, LANE)), _const((2 * LANE, HP)), _const((2 * LANE, 2 * LANE))],
        out_specs=(row(HP), row(HP)), vmem_mb=40)(
            ckv, kpe, kva_w, wukv, wk, cosk, sink, cs["seg_h"], cs["seg_ht"], cs["rot"])


def _kv_prep_bwd(dks, dvs, ckv, kpe, kva_w, wukv, wk, cosk, sink, cs, *, tm, n_tiles, tpe, n_lat):
    def body(dkl_ref, dkc_ref, dvl_ref, dvc_ref, ckv_ref, kpe_ref, kva_ref, wukv_ref, wk_ref, cos_ref, sin_ref,
             seg, segt, rot, rot_t, dckv_ref, dkpe_ref, dwukv_ref, dkva_ref, dwk_ref):
        t = pl.program_id(0)
        is_lat = t < n_lat

        @pl.when(t == 0)
        def _():
            dwukv_ref[...] = jnp.zeros_like(dwukv_ref)
            dkva_ref[...] = jnp.zeros_like(dkva_ref)
            dwk_ref[...] = jnp.zeros_like(dwk_ref)

        dk = jnp.where(is_lat, dkl_ref[...], dkc_ref[...])
        dv = jnp.where(is_lat, dvl_ref[...], dvc_ref[...])
        x = ckv_ref[...]
        ra = _rms(x)
        xh = x * ra
        kva = kva_ref[...]
        ckvn = (xh * kva).astype(BF16)
        wukv_v = wukv_ref[...]
        wk_v, cos_v, sin_v = wk_ref[...], cos_ref[...], sin_ref[...]
        kv = _mm_nt(ckvn, wukv_v)
        kx = kv[:, :HP] + jnp.concatenate([kpe_ref[...]] * H, axis=-1)
        _, y, rb = _head_norm_rope(kx, wk_v, cos_v, sin_v, seg[...], segt[...], rot[...], rope=False)
        dkx, dwk = _head_norm_rope_bwd(dk, y, rb, wk_v, cos_v, sin_v, seg[...], segt[...], rot_t[...])
        dkpe = dkx[:, 0:LANE]
        for h in range(1, H):
            dkpe = dkpe + dkx[:, h * LANE:(h + 1) * LANE]
        lane = lax.broadcasted_iota(jnp.int32, (tm, LANE), 1)
        dkpe_ref[...] = jnp.where((lane >= DN) & (lane < DH), dkpe, 0.0).astype(BF16)
        dkv = jnp.concatenate([dkx, dv], axis=-1).astype(BF16)
        dckvn = _mm(dkv, wukv_v)
        dxh = dckvn * kva
        dckv_ref[...] = (ra * (dxh - xh * jnp.mean(dxh * xh, axis=-1, keepdims=True))).astype(BF16)
        dwukv_ref[...] += _mm_tn(dkv, ckvn)
        dkva_ref[...] += _rowsum(dckvn * xh)
        dwk_ref[...] += dwk

    row = lambda cols: pl.BlockSpec((tm, cols), lambda t: (t, 0))
    lat = pl.BlockSpec((tm, HP), lambda t: (jnp.minimum(t, n_lat - 1), 0))
    ctx = pl.BlockSpec((tm, HP), lambda t: (jnp.maximum(t - n_lat, 0), 0))
    tab = _kv_tab_spec(tm, tpe, n_lat)
    r = n_tiles * tm
    return _pcall(
        body, name="kv_prep_bwd", grid=(n_tiles,),
        out_shape=(_sds((r, KVL), BF16), _sds((r, LANE), BF16), _sds((2 * HP, KVL), F32), _sds((1, KVL), F32),
                   _sds((1, HP), F32)),
        in_specs=[lat, ctx, lat, ctx, row(KVL), row(LANE), _const((1, KVL)), _const((2 * HP, KVL)), _const((1, HP)),
                  tab, tab, _const((HP, LANE)), _const((2 * LANE, HP)), _const((2 * LANE, 2 * LANE)), _const((2 * LANE, 2 * LANE))],
        out_specs=(row(KVL), row(LANE), _const((2 * HP, KVL)), _const((1, KVL)), _const((1, HP))), vmem_mb=48)(
            dks[0], dks[1], dvs[0], dvs[1], ckv, kpe, kva_w, wukv, wk, cosk, sink,
            cs["seg_h"], cs["seg_ht"], cs["rot"], cs["rot_t"])


_SCALE = DH ** -0.5
_SCALE_LOG2E = _SCALE * 1.4426950408889634


def _key_chunks(s, nc, ck):
    return ([(0, lo, min(lo + ck, s)) for lo in range(0, s, ck)]
            + [(1, lo, min(lo + ck, nc)) for lo in range(0, nc, ck)])


def _attn_fwd(q, k, v, *, nb, s, nc, tq, ck):
    tpe = s // tq
    r_lat = nb * s
    chunks = _key_chunks(s, nc, ck)
    hp = 4

    def body(q_ref, kl_ref, kc_ref, vl_ref, vc_ref, o_ref, lse_ref):
        k_refs, v_refs = (kl_ref, kc_ref), (vl_ref, vc_ref)
        for hh in range(hp):
            hs = slice(hh * LANE, (hh + 1) * LANE)
            qv = q_ref[:, hs]
            xs = [_mm_nt(qv, k_refs[w][lo:hi, hs]) for w, lo, hi in chunks]
            m = jnp.max(xs[0], axis=-1, keepdims=True)
            for x in xs[1:]:
                m = jnp.maximum(m, jnp.max(x, axis=-1, keepdims=True))
            l = acc = None
            for x, (w, lo, hi) in zip(xs, chunks):
                e = jnp.exp2((x - m) * _SCALE_LOG2E)
                lc = jnp.sum(e, axis=-1, keepdims=True)
                pv = _mm(e.astype(BF16), v_refs[w][lo:hi, hs])
                l = lc if l is None else l + lc
                acc = pv if acc is None else acc + pv
            o_ref[:, hs] = (acc / l).astype(BF16)
            lse = m * _SCALE_LOG2E + jnp.log2(l)
            lse_ref[hh] = jnp.transpose(jnp.broadcast_to(lse, (tq, LANE)))[0:8, :]

    qs = pl.BlockSpec((tq, hp * LANE), lambda i, j, t: (i * tpe + t, j))
    kl = pl.BlockSpec((s, hp * LANE), lambda i, j, t: (i, j))
    kc = pl.BlockSpec((nc, hp * LANE), lambda i, j, t: (r_lat // nc + i, j))
    ls = pl.BlockSpec((hp, 8, tq), lambda i, j, t: (i * (H // hp) + j, 0, t))
    return _pcall(body, name="attn_fwd", grid=(nb, H // hp, tpe),
                  out_shape=(_sds((r_lat, HP), BF16), _sds((nb * H, 8, s), F32)),
                  in_specs=[qs, kl, kc, kl, kc], out_specs=(qs, ls), vmem_mb=48)(q, k, k, v, v)


def _attn_bwd(q, k, v, o, do, lse, part, *, nb, s, nc, tq, ck):
    tpe = s // tq
    r_lat = nb * s
    chunks = _key_chunks(s, nc, ck)
    hp = 2
    n_steps = nb * (H // hp) * tpe

    def body(q_ref, kl_ref, kc_ref, vl_ref, vc_ref, o_ref, do_ref, lse_ref, part_ref,
             dq_ref, dkl_ref, dkc_ref, dvl_ref, dvc_ref, recv_ref, akl, akc, avl, avc, send_sems, recv_sems):
        t = pl.program_id(2)
        step = (pl.program_id(0) * (H // hp) + pl.program_id(1)) * tpe + t
        sends = _chip_sends(part_ref, recv_ref, send_sems, recv_sems)

        @pl.when(step == 0)
        def _():
            for cp in sends:
                cp.start()

        @pl.when(step == n_steps - 1)
        def _():
            for cp in sends:
                cp.wait_recv()
            for cp in sends:
                cp.wait_send()

        @pl.when(t == 0)
        def _():
            akl[...] = jnp.zeros_like(akl)
            akc[...] = jnp.zeros_like(akc)
            avl[...] = jnp.zeros_like(avl)
            avc[...] = jnp.zeros_like(avc)

        k_refs, v_refs, ak, av = (kl_ref, kc_ref), (vl_ref, vc_ref), (akl, akc), (avl, avc)
        for hh in range(hp):
            hs = slice(hh * LANE, (hh + 1) * LANE)
            qv = q_ref[:, hs]
            lse = jnp.transpose(jnp.concatenate([lse_ref[hh]] * (LANE // 8), axis=0))[:, 0:1]
            dov = do_ref[:, hs]
            delta = jnp.sum(dov.astype(F32) * o_ref[:, hs].astype(F32), axis=-1, keepdims=True)
            dq = None
            for w, lo, hi in chunks:
                kc_v = k_refs[w][lo:hi, hs]
                p = jnp.exp2(_mm_nt(qv, kc_v) * _SCALE_LOG2E - lse)
                ds = (p * (_mm_nt(dov, v_refs[w][lo:hi, hs]) - delta)).astype(BF16)
                part = _mm(ds, kc_v)
                dq = part if dq is None else dq + part
                ak[w][hs, lo:hi] += _mm_tn(qv, ds)
                av[w][hs, lo:hi] += _mm_tn(dov, p.astype(BF16))
            dq_ref[:, hs] = dq * _SCALE

        @pl.when(t == tpe - 1)
        def _():
            dkl_ref[...] = akl[...].T * _SCALE
            dkc_ref[...] = akc[...].T * _SCALE
            dvl_ref[...] = avl[...].T
            dvc_ref[...] = avc[...].T

    qs = pl.BlockSpec((tq, hp * LANE), lambda i, j, t: (i * tpe + t, j))
    kl = pl.BlockSpec((s, hp * LANE), lambda i, j, t: (i, j))
    kc = pl.BlockSpec((nc, hp * LANE), lambda i, j, t: (r_lat // nc + i, j))
    kc_out = pl.BlockSpec((nc, hp * LANE), lambda i, j, t: (i, j))
    ls = pl.BlockSpec((hp, 8, tq), lambda i, j, t: (i * (H // hp) + j, 0, t))
    return _pcall(
        body, name="attn_bwd", grid=(nb, H // hp, tpe),
        out_shape=(_sds((r_lat, HP), F32), _sds((r_lat, HP), F32), _sds((nb * nc, HP), F32),
                   _sds((r_lat, HP), F32), _sds((nb * nc, HP), F32), _sds((3,) + part.shape[1:], part.dtype)),
        in_specs=[qs, kl, kc, kl, kc, qs, qs, ls, ANY], out_specs=(qs, kl, kc_out, kl, kc_out, ANY),
        scratch=[pltpu.VMEM((hp * LANE, s), F32), pltpu.VMEM((hp * LANE, nc), F32)] * 2
        + [pltpu.SemaphoreType.DMA((3,))] * 2,
        vmem_mb=60)(q, k, k, v, v, o, do, lse, part)


def _gating(vn, ws_ref, bias_ref, s_scr, tm):
    lane = lax.broadcasted_iota(jnp.int32, (CH, LANE), 1)
    for c in range(tm // CH):
        rs = slice(c * CH, (c + 1) * CH)
        for j in range(G // 2):
            ls = slice(j * LANE, (j + 1) * LANE)
            vp = vn[rs, ls]
            s_scr[rs, ls] = jnp.where(lane < GD, _mm(ws_ref[2 * j], vp), _mm(ws_ref[2 * j + 1], vp)) + bias_ref[:, ls]


def _mix_fwd(u, v, attn, x1, gate, wv, ws, bias, wout, cs, *, tm, n_lat, tpe):
    nrows = gate.shape[0]

    def body(u_ref, v_ref, attn_ref, x_ref, gate_ref, wv_ref, ws_ref, bias_ref, wout_ref, seg, segt,
             x2_ref, mix_ref, s_scr):
        vg = _gelu(v_ref[...])
        rg = lax.rsqrt(_seg_sum(vg * vg, seg[...]) * (1.0 / GD) + EPS)
        vn = (vg * _seg_bcast(rg, segt[...]) * wv_ref[...]).astype(BF16)
        _gating(vn, ws_ref, bias_ref, s_scr, tm)
        sg = (_gelu(u_ref[...]) * s_scr[...]).astype(BF16)
        mix = _mm(attn_ref[...], wout_ref[0:HP, :]) + _mm(sg, wout_ref[HP:, :])
        mix_ref[...] = mix.astype(BF16)
        x2_ref[...] = x_ref[...] + gate_ref[0] * mix

    row = lambda cols: pl.BlockSpec((tm, cols), lambda t: (t, 0))
    r = n_lat * tm
    return _pcall(
        body, name="mix_fwd", grid=(n_lat,),
        out_shape=(_sds((r, D), F32), _sds((r, D), BF16)),
        in_specs=[row(G * GD), row(G * GD), row(HP), row(D), _mod_spec(1, tpe, nrows), _const((1, G * GD)),
                  _const((G, CH, CH)), _const((CH, G * GD)), _const((HP + G * GD, D)), _const((G * GD, LANE)),
                  _const((2 * LANE, G * GD))],
        out_specs=(row(D), row(D)), scratch=[pltpu.VMEM((tm, G * GD), F32)], vmem_mb=40)(
            u, v, attn, x1, gate, wv, ws, bias, wout, cs["seg_g"], cs["seg_gt"])


def _mix_bwd(dx2, mix, u, v, attn, gate, wv, ws, wst, bias, wout, cs, *, tm, n_lat, tpe):
    nrows = gate.shape[0]
    wrows = HP + G * GD

    def body(dx2_ref, mix_ref, u_ref, v_ref, attn_ref, gate_ref, wv_ref, ws_ref, wst_ref, bias_ref, wout_ref, seg, segt,
             dattn_ref, du_ref, dv_ref, dgate_ref, dwout_ref, dws_ref, dbs_ref, dwv_ref, s_scr, dvn_scr, dbias_scr):
        t = pl.program_id(0)

        @pl.when(t == 0)
        def _():
            dwout_ref[...] = jnp.zeros_like(dwout_ref)
            dws_ref[...] = jnp.zeros_like(dws_ref)
            dwv_ref[...] = jnp.zeros_like(dwv_ref)
            dbias_scr[...] = jnp.zeros_like(dbias_scr)

        @pl.when(t % tpe == 0)
        def _():
            dgate_ref[...] = jnp.zeros_like(dgate_ref)

        dx2 = dx2_ref[...]
        dmix = (dx2 * gate_ref[0]).astype(BF16)
        dcat = _mm_nt(dmix, wout_ref[...])
        dattn_ref[...] = dcat[:, :HP].astype(BF16)
        dsg = dcat[:, HP:]

        vraw = v_ref[...]
        vg = _gelu(vraw)
        rg = lax.rsqrt(_seg_sum(vg * vg, seg[...]) * (1.0 / GD) + EPS)
        r64 = _seg_bcast(rg, segt[...])
        y = vg * r64
        wv_v = wv_ref[...]
        vn = (y * wv_v).astype(BF16)
        _gating(vn, ws_ref, bias_ref, s_scr, tm)
        uraw = u_ref[...]
        ug = _gelu(uraw)
        s = s_scr[...]
        sg = (ug * s).astype(BF16)
        du_ref[...] = (dsg * s * _gelu_grad(uraw)).astype(BF16)
        ds = dsg * ug
        dgate_ref[0] += _rowsum(dx2 * mix_ref[...].astype(F32))
        dwout_ref[...] += _mm_tn(jnp.concatenate([attn_ref[...], sg], axis=-1), dmix)

        lane = lax.broadcasted_iota(jnp.int32, (CH, LANE), 1)
        for c in range(tm // CH):
            rs = slice(c * CH, (c + 1) * CH)
            dbias_scr[...] += ds[rs, :]
            for j in range(G // 2):
                ls = slice(j * LANE, (j + 1) * LANE)
                dsp32 = ds[rs, ls]
                dsp = dsp32.astype(BF16)
                vp = vn[rs, ls]
                dvn_scr[rs, ls] = jnp.where(lane < GD, _mm(wst_ref[2 * j], dsp), _mm(wst_ref[2 * j + 1], dsp))
                dws_ref[2 * j] += _mm_nt(jnp.where(lane < GD, dsp32, 0.0).astype(BF16), vp)
                dws_ref[2 * j + 1] += _mm_nt(jnp.where(lane < GD, 0.0, dsp32).astype(BF16), vp)

        dvn = dvn_scr[...]
        dwv_ref[...] += _rowsum(dvn * y)
        dy = dvn * wv_v
        mean_g = _seg_sum(dy * y, seg[...]) * (1.0 / GD)
        dvg = r64 * (dy - y * _seg_bcast(mean_g, segt[...]))
        dv_ref[...] = (dvg * _gelu_grad(vraw)).astype(BF16)

        @pl.when(t == n_lat - 1)
        def _():
            dbs_ref[...] = _dot_hl(dbias_scr[...], seg[...])

    row = lambda cols: pl.BlockSpec((tm, cols), lambda t: (t, 0))
    r = n_lat * tm
    return _pcall(
        body, name="mix_bwd", grid=(n_lat,),
        out_shape=(_sds((r, HP), BF16), _sds((r, G * GD), BF16), _sds((r, G * GD), BF16), _sds((nrows, 1, D), F32),
                   _sds((wrows, D), F32), _sds((G, CH, CH), F32), _sds((CH, LANE), F32), _sds((1, G * GD), F32)),
        in_specs=[row(D), row(D), row(G * GD), row(G * GD), row(HP), _mod_spec(1, tpe, nrows), _const((1, G * GD)),
                  _const((G, CH, CH)), _const((G, CH, CH)), _const((CH, G * GD)), _const((wrows, D)),
                  _const((G * GD, LANE)), _const((2 * LANE, G * GD))],
        out_specs=(row(HP), row(G * GD), row(G * GD), _mod_spec(1, tpe, nrows), _const((wrows, D)),
                   _const((G, CH, CH)), _const((CH, LANE)), _const((1, G * GD))),
        scratch=[pltpu.VMEM((tm, G * GD), F32), pltpu.VMEM((tm, G * GD), F32), pltpu.VMEM((CH, G * GD), F32)],
        vmem_mb=56)(dx2, mix, u, v, attn, gate, wv, ws, wst, bias, wout, cs["seg_g"], cs["seg_gt"])


def _adamw_math(w, g, m, v):
    m2 = ADAM_B1 * m + (1.0 - ADAM_B1) * g
    v2 = ADAM_B2 * v + (1.0 - ADAM_B2) * (g * g)
    m_hat = m2 / (1.0 - ADAM_B1 ** ADAM_STEP)
    v_hat = v2 / (1.0 - ADAM_B2 ** ADAM_STEP)
    delta = -ADAM_LR * (m_hat / (jnp.sqrt(v_hat) + ADAM_EPS) + ADAM_WD * w)
    return delta, m2, v2


def _row_tile(r, c):
    best = r
    for tr in range(8, r, 8):
        if r % tr == 0 and tr * c * 4 <= MIB:
            best = tr
    return best


def _adamw(w, g, m, v, name):
    r, c = w.shape
    tr = _row_tile(r, c)

    def body(w_ref, g_ref, m_ref, v_ref, d_ref, mo_ref, vo_ref):
        d_ref[...], mo_ref[...], vo_ref[...] = _adamw_math(w_ref[...], g_ref[...], m_ref[...], v_ref[...])

    blk = pl.BlockSpec((tr, c), lambda t: (t, 0))
    return _pcall(body, name=name, grid=(r // tr,), out_shape=(_sds((r, c), F32),) * 3,
                  in_specs=[blk] * 4, out_specs=(blk,) * 3)(w, g, m, v)


def _adamw_small(params):
    n = len(params)

    def body(*refs):
        ins, outs = refs[:4 * n], refs[4 * n:]
        for i in range(n):
            w, g, m, v = (ins[4 * i + k][...] for k in range(4))
            if i == 0:
                sig = _sigmoid(w)
                g = g * (sig * (1.0 + w * (1.0 - sig)))
            d, m2, v2 = _adamw_math(w, g, m, v)
            outs[4 * i][...] = g
            outs[4 * i + 1][...] = d
            outs[4 * i + 2][...] = m2
            outs[4 * i + 3][...] = v2

    flat = [a for p in params for a in p]
    out_shape = tuple(_sds(p[0].shape, F32) for p in params for _ in range(4))
    res = _pcall(body, name="adamw_small", out_shape=out_shape, in_specs=[VMEM] * (4 * n),
                 out_specs=(VMEM,) * (4 * n))(*flat)
    return [res[4 * i:4 * i + 4] for i in range(n)]


def _rope_tables(s):
    rows = jnp.repeat(jnp.arange(s // GRID_W, dtype=F32), GRID_W)
    cols = jnp.tile(jnp.arange(GRID_W, dtype=F32), s // GRID_W)
    half = DR // 2
    inv = ROPE_BASE ** (-jnp.arange(0, half, 2, dtype=F32) / half)
    ang_r = rows[:, None] * inv
    ang_c = cols[:, None] * inv
    ang = jnp.concatenate([ang_r, ang_r, ang_c, ang_c], axis=-1)
    return jnp.cos(ang), jnp.sin(ang)


def _head_pad(a, real):
    return jnp.pad(a, ((0, 0), (0, LANE - real), (0, 0))).reshape(HP, a.shape[2])


def kernel(x, c, ctx, c_ctx, w_ada, b_ada, norm1_w, ffn1_w1, ffn1_w3, ffn1_w2, norm2_w, w_in, q_a_norm_w, w_uq, kv_a_norm_w, w_ukv, q_norm_w, k_norm_w, v_norm_w, w_s, b_s, w_out, norm3_w, ffn2_w1, ffn2_w3, ffn2_w2, loss_target, m_c_ctx, m_w_ada, m_b_ada, m_norm1_w, m_ffn1_w1, m_ffn1_w3, m_ffn1_w2, m_norm2_w, m_w_in, m_q_a_norm_w, m_w_uq, m_kv_a_norm_w, m_w_ukv, m_q_norm_w, m_k_norm_w, m_v_norm_w, m_w_s, m_b_s, m_w_out, m_norm3_w, m_ffn2_w1, m_ffn2_w3, m_ffn2_w2, v_c_ctx, v_w_ada, v_b_ada, v_norm1_w, v_ffn1_w1, v_ffn1_w3, v_ffn1_w2, v_norm2_w, v_w_in, v_q_a_norm_w, v_w_uq, v_kv_a_norm_w, v_w_ukv, v_q_norm_w, v_k_norm_w, v_v_norm_w, v_w_s, v_b_s, v_w_out, v_norm3_w, v_ffn2_w1, v_ffn2_w3, v_ffn2_w2):
    nb, s, _ = x.shape
    nc = ctx.shape[1]
    tm = 256 if nc % 256 == 0 else 128
    tpe = s // tm
    n_lat = nb * tpe
    n_all = n_lat + nb * nc // tm
    tmf = 2 * tm if s % (2 * tm) == 0 and (nb * nc) % (2 * tm) == 0 else tm
    tp = tmf
    r_lat = nb * s
    tpe_p, n_lat_p, n_all_p = s // tp, r_lat // tp, (r_lat + nb * nc) // tp
    me = 4 * lax.axis_index("x") + 2 * lax.axis_index("y") + lax.axis_index("c")
    cs = _consts()
    ncol = w_ada.shape[2]
    fsh = ffn1_w1.shape[2]
    assert nb + 1 <= 8 and NDEV * fsh == FF and NDEV * ncol == NMOD * D and s % nc == 0 and nc % tm == 0

    def t16(a):
        return a.T.astype(BF16)

    wpack1 = jnp.concatenate([t16(ffn1_w1[0]), t16(ffn1_w3[0]), ffn1_w2[0].astype(BF16)], axis=0)
    a_loc = jnp.concatenate([c, c_ctx[None, :], jnp.zeros((7 - nb, D), F32)], axis=0)
    a_raw, _, mod_all, wall1 = _ada_front(a_loc, w_ada[0], lax.dynamic_slice_in_dim(b_ada, me * ncol, ncol, axis=1),
                                          wpack1)
    a_raw = a_raw.reshape(NDEV * 8, D)
    mod_mine = lax.dynamic_slice_in_dim(mod_all, 8 * me, 8, axis=1)
    modtab = mod_mine.transpose(1, 0, 2).reshape(8, NMOD, D)[:nb + 1]
    wpack2 = jnp.concatenate([
        t16(ffn2_w1[0]), t16(ffn2_w3[0]), ffn2_w2[0].astype(BF16),
        t16(w_in[0]), jnp.zeros((12, D), BF16),
        w_out[0].astype(BF16),
        t16(w_uq[0]).reshape(24, D), jnp.zeros((8, D), BF16),
        t16(w_ukv[0]).reshape(16, D)], axis=0)

    def head_w(wn):
        return jnp.tile(jnp.pad(wn, ((0, 0), (0, LANE - DH))), (1, H))

    wq, wk = head_w(q_norm_w), head_w(k_norm_w)
    wv = v_norm_w.reshape(1, G * GD)
    ws16 = w_s[0].astype(BF16)
    wst16 = w_s[0].transpose(0, 2, 1).astype(BF16)
    bias = jnp.repeat(b_s[0].T, GD, axis=1)
    cos, sin = _rope_tables(s)
    cos = jnp.pad(cos, ((0, 0), (DN, LANE - DH)), constant_values=1.0)
    sin = jnp.pad(sin, ((0, 0), (DN, LANE - DH)))
    cos_k = jnp.concatenate([cos, jnp.ones((tm, LANE), F32)], axis=0)
    sin_k = jnp.concatenate([sin, jnp.zeros((tm, LANE), F32)], axis=0)

    xs = (x.reshape(r_lat, D), ctx.reshape(nb * nc, D))
    x1, a1, b1, o1, wall2 = _ffn_fwd(xs, modtab[:, 0:3], norm1_w, wall1, 0, tm=tmf, n_tiles=(r_lat + nb * nc) // tmf,
                                     tpe=s // tmf, n_lat=r_lat // tmf, name="ffn1_fwd", gather=wpack2)

    o0 = 3 * fsh
    wint = wall2[:, o0:o0 + 180].reshape(IN_COLS, D)
    z = lambda n: jnp.zeros((n, D), BF16)
    wint = jnp.concatenate([wint[0:128], wint[160:416], wint[416:928], wint[928:1440],
                            z(DN), wint[128:160], z(LANE - DH)], axis=0)
    wout = wall2[:, o0 + 192:o0 + 320].reshape(D, D)
    wout = jnp.concatenate([_head_pad(wout[:H * DV].reshape(H, DV, D), DV), wout[H * DV:]], axis=0)
    wuq = _head_pad(wall2[:, o0 + 320:o0 + 344].reshape(H, DH, QL), DH)
    wukvt = wall2[:, o0 + 352:o0 + 368].reshape(H, DN + DV, KVL)
    wukv = jnp.concatenate([_head_pad(wukvt[:, :DN], DN), _head_pad(wukvt[:, DN:], DV)], axis=0)

    ckv, qp, u_raw, v_raw, kpe = _proj_fwd(x1, modtab[:, 3:5], norm2_w, wint, tm=tp, n_tiles=n_all_p, tpe=tpe_p)
    q = _q_prep_fwd(qp, q_a_norm_w, wuq, wq, cos, sin, cs, tm=tm, n_lat=n_lat, tpe=tpe)
    k, v = _kv_prep_fwd(ckv, kpe, kv_a_norm_w, wukv, wk, cos_k, sin_k, cs,
                        tm=tm, n_tiles=n_all, tpe=tpe, n_lat=n_lat)
    attn, lse = _attn_fwd(q, k, v, nb=nb, s=s, nc=nc, tq=tm, ck=2048)
    x2, mix = _mix_fwd(u_raw, v_raw, attn, x1, modtab[:nb, 5:6], wv, ws16, bias, wout, cs,
                       tm=tp, n_lat=n_lat_p, tpe=tpe_p)
    dy, a2, b2, o2, lsum = _ffn_fwd((x2,), modtab[:nb, 6:9], norm3_w, wall2, 0, tm=tmf, n_tiles=r_lat // tmf,
                                    tpe=s // tmf, n_lat=r_lat // tmf, name="ffn2_fwd",
                                    target=loss_target.reshape(r_lat, D))
    loss = lax.psum(lsum[0, 0] * (0.5 / D), ("x", "y", "c"))

    tr = 2 * tm if n_lat % 2 == 0 and n_all % 2 == 0 else tm
    dx2, da2, db2, g2, do2, h2, dmod678, dnorm3 = _ffn_bwd_dx(
        dy, (x2,), a2, b2, o2, modtab[:nb, 6:9], norm3_w, wall2, 0,
        tm=tm, n_tiles=n_lat, tpe=tpe, n_lat=n_lat, name="ffn2_bwd_dx")
    g_ffn2 = _ffn_bwd_dw(h2, do2, da2, db2, g2, tr=tr, name="ffn2_bwd_dw")
    part_ffn2 = _add_sibling(g_ffn2, _scatter_sibling([g_ffn2], "scatter_sibling_ffn2")[0], 176, "add_sibling_ffn2")

    dattn, du, dv, dgate5, dwout, dws, dbs, dwv = _mix_bwd(
        dx2, mix, u_raw, v_raw, attn, modtab[:nb, 5:6], wv, ws16, wst16, bias, wout, cs, tm=tp, n_lat=n_lat_p, tpe=tpe_p)
    tq = 2 * tm if s % (2 * tm) == 0 else tm
    dq, dk_l, dk_c, dv_l, dv_c, recv_ffn2 = _attn_bwd(q, k, v, attn, dattn, lse, part_ffn2,
                                                      nb=nb, s=s, nc=nc, tq=tq, ck=1024)
    dqp, dwuq, dqa, dwq = _q_prep_bwd(dq, qp, q_a_norm_w, wuq, wq, cos, sin, cs, tm=tm, n_lat=n_lat, tpe=tpe)
    dckv, dkpe, dwukv, dkva, dwk = _kv_prep_bwd((dk_l, dk_c), (dv_l, dv_c), ckv, kpe, kv_a_norm_w, wukv, wk,
                                                cos_k, sin_k, cs, tm=tm, n_tiles=n_all, tpe=tpe, n_lat=n_lat)
    dx1, dwin, dmod34, dnorm2 = _proj_bwd(dckv, dkpe, dqp, du, dv, dx2, x1, modtab[:, 3:5], norm2_w, wint,
                                          tm=tp, n_tiles=n_all_p, tpe=tpe_p, n_lat=n_lat_p)
    dx0, da1, db1, g1, do1, h1, dmod012, dnorm1 = _ffn_bwd_dx(
        dx1, xs, a1, b1, o1, modtab[:, 0:3], norm1_w, wall1, 0,
        tm=tm, n_tiles=n_all, tpe=tpe, n_lat=n_lat, name="ffn1_bwd_dx")
    g_ffn1 = _ffn_bwd_dw(h1, do1, da1, db1, g1, tr=tr, name="ffn1_bwd_dw")
    grad_x = dx0.reshape(nb, s, D)

    zrow = jnp.zeros((1, D), F32)
    g_lat = jnp.concatenate([dmod012[:nb, 0], dmod012[:nb, 1], dmod012[:nb, 2], dmod34[:nb, 0], dmod34[:nb, 1],
                             dgate5[:, 0], dmod678[:, 0], dmod678[:, 1], dmod678[:, 2]], axis=1)
    g_ctx = jnp.concatenate([dmod012[nb:, 0], dmod012[nb:, 1], dmod012[nb:, 2], dmod34[nb:, 0], dmod34[nb:, 1],
                             zrow, zrow, zrow, zrow], axis=1)
    g_loc = jnp.concatenate([g_lat, g_ctx, jnp.zeros((7 - nb, NMOD * D), F32)], axis=0)

    def blocks(a):
        return a.reshape(NDEV, a.shape[0] // NDEV, D)

    dwin_o = jnp.concatenate([dwin[0:128], dwin[KPE_LO:KPE_LO + DR], dwin[128:384], dwin[384:896], dwin[896:1408]],
                             axis=0)
    dwout_o = jnp.concatenate([dwout[:HP].reshape(H, LANE, D)[:, :DV].reshape(H * DV, D), dwout[HP:]], axis=0)
    dwuq_o = dwuq.reshape(H, LANE, QL)[:, :DH]
    dwukv_o = jnp.concatenate([dwukv[:HP].reshape(H, LANE, KVL)[:, :DN], dwukv[HP:].reshape(H, LANE, KVL)[:, :DV]],
                              axis=1)
    gmisc = jnp.concatenate([
        blocks(dwin_o).astype(BF16), jnp.zeros((NDEV, 12, D), BF16),
        blocks(dwout_o).astype(BF16),
        dwuq_o.reshape(NDEV, 24, D).astype(BF16), jnp.zeros((NDEV, 8, D), BF16),
        dwukv_o.reshape(NDEV, 16, D).astype(BF16)], axis=1)
    got_ffn1, got_misc, g_all = _scatter_sibling([g_ffn1, gmisc], "scatter_sibling", gather=g_loc)
    g_all = g_all.reshape(NDEV * 8, NMOD * D)
    g_cols = lax.dynamic_slice_in_dim(g_all, me * ncol, ncol, axis=1)
    g_w_ada, pc_ctx, g_b_ada = _ada_bwd(a_raw, c_ctx.reshape(D, 1), g_all, g_cols, w_ada[0], nb)
    parts = [_add_sibling(g_ffn1, got_ffn1, 176, "add_sibling_ffn1"), _add_sibling(gmisc, got_misc, 368, "add_sibling_misc")]

    def prow(a):
        a = a.reshape(1, -1)
        return jnp.concatenate([a, jnp.zeros((1, D - a.shape[1]), F32)], axis=1)

    g_qn = dwq.reshape(H, LANE)[:, :DH].sum(0)
    g_kn = dwk.reshape(H, LANE)[:, :DH].sum(0)
    spack = jnp.concatenate([
        dnorm1, dnorm2, dnorm3, prow(dqa), prow(dkva), prow(g_qn), prow(g_kn), prow(dwv),
        prow(dbs[:, :G].T), prow(pc_ctx), jnp.zeros((6, D), F32), dws.reshape(CH, D)], axis=0)
    recv_ffn1, recv_misc, small_all = _scatter_chips(parts, "scatter_chips", gather=spack)
    recv = (recv_ffn1, recv_misc)
    ssum = _sum_slots(small_all, 144, "sum_small")
    gsum1 = _sum_chips(parts[0], recv[0], 176, "sum_grads_ffn1")
    gsum2 = _sum_chips(part_ffn2, recv_ffn2, 176, "sum_grads_ffn2")
    msum = _sum_chips(parts[1], recv[1], 368, "sum_grads_misc")

    transposed = ("ffn1_w1", "ffn1_w3", "ffn2_w1", "ffn2_w3", "w_in", "w_uq")
    g_big = {
        "ffn1_w1": gsum1[0:fsh], "ffn1_w3": gsum1[fsh:2 * fsh], "ffn1_w2": gsum1[2 * fsh:3 * fsh],
        "ffn2_w1": gsum2[0:fsh], "ffn2_w3": gsum2[fsh:2 * fsh], "ffn2_w2": gsum2[2 * fsh:3 * fsh],
        "w_in": msum[0:180], "w_out": msum[192:320],
        "w_uq": msum[320:344].reshape(DH, QL), "w_ukv": msum[352:368].reshape(DN + DV, KVL).T,
        "w_ada": g_w_ada,
    }

    big_in = {
        "w_ada": (w_ada, m_w_ada, v_w_ada), "ffn1_w1": (ffn1_w1, m_ffn1_w1, v_ffn1_w1),
        "ffn1_w3": (ffn1_w3, m_ffn1_w3, v_ffn1_w3), "ffn1_w2": (ffn1_w2, m_ffn1_w2, v_ffn1_w2),
        "w_in": (w_in, m_w_in, v_w_in), "w_uq": (w_uq, m_w_uq, v_w_uq), "w_ukv": (w_ukv, m_w_ukv, v_w_ukv),
        "w_out": (w_out, m_w_out, v_w_out), "ffn2_w1": (ffn2_w1, m_ffn2_w1, v_ffn2_w1),
        "ffn2_w3": (ffn2_w3, m_ffn2_w3, v_ffn2_w3), "ffn2_w2": (ffn2_w2, m_ffn2_w2, v_ffn2_w2),
    }
    res = {}
    for nm, (w, m, v_) in big_in.items():
        g = g_big[nm]
        if nm in transposed:
            d_, m_, v2_ = _adamw(w[0].T, g, m[0].T, v_[0].T, "adamw_" + nm)
            res[nm] = tuple(a.T[None] for a in (g, d_, m_, v2_))
        else:
            d_, m_, v2_ = _adamw(w[0], g, m[0], v_[0], "adamw_" + nm)
            res[nm] = tuple(a[None] for a in (g, d_, m_, v2_))

    small_in = [
        ("c_ctx", c_ctx, m_c_ctx, v_c_ctx, ssum[9:10], (1, D)),
        ("b_ada", b_ada, m_b_ada, v_b_ada, g_b_ada, (1, NMOD * D)),
        ("norm1_w", norm1_w, m_norm1_w, v_norm1_w, ssum[0:1], (1, D)),
        ("norm2_w", norm2_w, m_norm2_w, v_norm2_w, ssum[1:2], (1, D)),
        ("norm3_w", norm3_w, m_norm3_w, v_norm3_w, ssum[2:3], (1, D)),
        ("q_a_norm_w", q_a_norm_w, m_q_a_norm_w, v_q_a_norm_w, ssum[3:4, :QL], (1, QL)),
        ("kv_a_norm_w", kv_a_norm_w, m_kv_a_norm_w, v_kv_a_norm_w, ssum[4:5, :KVL], (1, KVL)),
        ("q_norm_w", q_norm_w, m_q_norm_w, v_q_norm_w, ssum[5:6, :DH], (1, DH)),
        ("k_norm_w", k_norm_w, m_k_norm_w, v_k_norm_w, ssum[6:7, :DH], (1, DH)),
        ("v_norm_w", v_norm_w, m_v_norm_w, v_v_norm_w, ssum[7:8, :G * GD], (G, GD)),
        ("b_s", b_s, m_b_s, v_b_s, ssum[8:9], (G, CH)),
        ("w_s", w_s, m_w_s, v_w_s, ssum[16:144], (G * CH, CH)),
    ]
    small_out = _adamw_small(
        [(w.reshape(sh), g.reshape(sh), m.reshape(sh), v_.reshape(sh)) for _, w, m, v_, g, sh in small_in])
    for (nm, w, *_), outs in zip(small_in, small_out):
        res[nm] = tuple(a.reshape(w.shape) for a in outs)

    order = ["c_ctx", "w_ada", "b_ada", "norm1_w", "ffn1_w1", "ffn1_w3", "ffn1_w2", "norm2_w", "w_in", "q_a_norm_w",
             "w_uq", "kv_a_norm_w", "w_ukv", "q_norm_w", "k_norm_w", "v_norm_w", "w_s", "b_s", "w_out", "norm3_w",
             "ffn2_w1", "ffn2_w3", "ffn2_w2"]
    return (loss, grad_x, *[res[n][0] for n in order], *[res[n][1] for n in order],
            *[res[n][2] for n in order], *[res[n][3] for n in order])
```

```python
import numpy as np
import jax
import jax.numpy as jnp
from jax import lax
from jax.experimental import pallas as pl
from jax.experimental.pallas import tpu as pltpu

F32 = jnp.float32
BF16 = jnp.bfloat16

D = 1024
FF = 2816
FC = 256
H = 8
DN, DR, DV = 64, 32, 64
DH = DN + DR
QL, KVL = 256, 128
G, GD, CH = 8, 64, 128
NMOD = 9
EPS = 1e-6
GRID_W = 64
ROPE_BASE = 10000.0
NDEV = 8
LANE = 128
HP = H * LANE
IN_COLS = 1440
WIN_ROWS = 1536
KPE_LO = 1408 + DN
MIB = 1 << 20

ADAM_LR, ADAM_B1, ADAM_B2, ADAM_EPS, ADAM_WD, ADAM_STEP = 0.001, 0.9, 0.999, 1e-08, 0.01, 10

MESH = pl.DeviceIdType.MESH
ANY = pl.BlockSpec(memory_space=pl.ANY)
VMEM = pl.BlockSpec(memory_space=pltpu.VMEM)


def _mm(a, b):
    return jnp.dot(a, b, preferred_element_type=F32)


def _mm_nt(a, b):
    return lax.dot_general(a, b, (((1,), (1,)), ((), ())), preferred_element_type=F32)


def _mm_tn(a, b):
    return lax.dot_general(a, b, (((0,), (0,)), ((), ())), preferred_element_type=F32)


def _dot_hl(x, m):
    hi = x.astype(BF16)
    lo = (x - hi.astype(F32)).astype(BF16)
    return _mm(hi, m) + _mm(lo, m)


def _sigmoid(a):
    return 1.0 / (1.0 + jnp.exp(-a))


_G0 = 0.7978845608028654
_G1 = 0.044715


def _gelu(x):
    return 0.5 * x * (1.0 + jnp.tanh(_G0 * (x + _G1 * (x * x * x))))


def _gelu_grad(x):
    th = jnp.tanh(_G0 * (x + _G1 * (x * x * x)))
    return 0.5 * (1.0 + th) + 0.5 * x * (1.0 - th * th) * (_G0 * (1.0 + 3.0 * _G1 * x * x))


def _rowsum(y):
    return jnp.sum(y, axis=0, keepdims=True)


def _rms(x):
    return lax.rsqrt(jnp.mean(x * x, axis=-1, keepdims=True) + EPS)


def _pcall(body, *, name, out_shape, in_specs, out_specs, grid=None, scratch=(), vmem_mb=32, aliases=None):
    kw = {}
    if grid is not None:
        kw["grid"] = grid
        sem = ("arbitrary",) * len(grid)
    else:
        sem = None
    if aliases:
        kw["input_output_aliases"] = aliases
    return pl.pallas_call(
        body, name=name, out_shape=out_shape, in_specs=in_specs, out_specs=out_specs,
        scratch_shapes=list(scratch),
        compiler_params=pltpu.CompilerParams(dimension_semantics=sem, vmem_limit_bytes=vmem_mb * MIB),
        **kw)


def _const(shape):
    nd = len(shape)
    return pl.BlockSpec(shape, lambda *_: (0,) * nd)


def _sds(shape, dt):
    return jax.ShapeDtypeStruct(shape, dt)


def _consts():
    seg_h = np.zeros((HP, LANE), np.float32)
    seg_h[np.arange(HP), np.arange(HP) // LANE] = 1.0
    seg_g = np.zeros((G * GD, LANE), np.float32)
    seg_g[np.arange(G * GD), np.arange(G * GD) // GD] = 1.0
    rot = np.zeros((LANE, LANE), np.float32)
    for base in (DN, DN + 16):
        for j in range(8):
            rot[base + j + 8, base + j] = -1.0
            rot[base + j, base + j + 8] = 1.0
    rot2 = np.zeros((2 * LANE, 2 * LANE), np.float32)
    rot2[:LANE, :LANE] = rot
    rot2[LANE:, LANE:] = rot
    twice = lambda m: np.concatenate([m, m], axis=0)
    c = dict(seg_h=seg_h, seg_ht=twice(seg_h.T), seg_g=seg_g, seg_gt=twice(seg_g.T), rot=rot2, rot_t=rot2.T)
    return {k: jnp.asarray(v, BF16) for k, v in c.items()}


_GATHER_SEMS = [pltpu.SemaphoreType.DMA((7,)), pltpu.SemaphoreType.DMA((7,)), pltpu.SemaphoreType.DMA(())]


def _gather_phases(x_ref, out_ref, send_sems, recv_sems, local_sem):
    mx, my, mc = lax.axis_index("x"), lax.axis_index("y"), lax.axis_index("c")
    me, sibling = (mx, my, mc), (mx, my, 1 - mc)
    chips = [(1 - mx, my), (mx, 1 - my), (1 - mx, 1 - my)]

    def blk(px, py, pc):
        return out_ref.at[4 * px + 2 * py + pc]

    def copy(k, block, to, src=None):
        return pltpu.make_async_remote_copy(
            src_ref=blk(*block) if src is None else src, dst_ref=blk(*block),
            send_sem=send_sems.at[k], recv_sem=recv_sems.at[k], device_id=to, device_id_type=MESH)

    mine = pltpu.make_async_copy(x_ref, blk(*me), local_sem)
    first = [copy(0, me, sibling, src=x_ref)]
    first += [copy(1 + j, me, (*chip, mc), src=x_ref) for j, chip in enumerate(chips)]
    passed = [copy(4 + j, (*chip, mc), sibling) for j, chip in enumerate(chips)]

    def start():
        mine.start()
        for cp in first:
            cp.start()

    def forward():
        for j, chip in enumerate(chips):
            copy(1 + j, (*chip, mc), me).wait_recv()
            passed[j].start()

    def finish():
        copy(0, sibling, me).wait_recv()
        for j, chip in enumerate(chips):
            copy(4 + j, (*chip, 1 - mc), me).wait_recv()
        for cp in first + passed:
            cp.wait_send()
        mine.wait()

    return start, forward, finish


def _chip_sends(p_ref, out_ref, send_sems, recv_sems):
    mx, my, mc = lax.axis_index("x"), lax.axis_index("y"), lax.axis_index("c")
    peers = [(1 - mx, my), (mx, 1 - my), (1 - mx, 1 - my)]
    return [pltpu.make_async_remote_copy(
        src_ref=p_ref.at[2 * px + py], dst_ref=out_ref.at[j], send_sem=send_sems.at[j], recv_sem=recv_sems.at[j],
        device_id=(px, py, mc), device_id_type=MESH) for j, (px, py) in enumerate(peers)]


def _with_gather(copies_of, n, shapes, sems, gather, name, args):
    ns = len(sems)

    def body(*refs):
        ng = 1 if gather is not None else 0
        ins, outs = refs[:n], refs[n + ng:2 * n + ng]
        copies = copies_of(ins, outs, refs[2 * n + 2 * ng:2 * n + 2 * ng + ns])
        if ng:
            start, forward, finish = _gather_phases(refs[n], refs[2 * n + 1], *refs[2 * n + 2 + ns:])
            start()
        for cp in copies:
            cp.start()
        if ng:
            forward()
        for cp in copies:
            cp.wait_recv()
        for cp in copies:
            cp.wait_send()
        if ng:
            finish()

    in_specs, out_shape, scratch = [ANY] * n, list(shapes), list(sems)
    if gather is not None:
        in_specs.append(ANY)
        args = list(args) + [gather]
        out_shape.append(_sds((NDEV,) + gather.shape, gather.dtype))
        scratch += _GATHER_SEMS
    return pl.pallas_call(body, name=name, out_shape=tuple(out_shape), in_specs=in_specs,
                          out_specs=(ANY,) * len(out_shape), scratch_shapes=scratch)(*args)


def _scatter_sibling(xs, name, gather=None):
    n = len(xs)

    def copies_of(x_refs, got_refs, sems):
        send_sems, recv_sems = sems
        mx, my, mc = lax.axis_index("x"), lax.axis_index("y"), lax.axis_index("c")
        return [pltpu.make_async_remote_copy(
            src_ref=x_refs[i].at[2 * j + 1 - mc], dst_ref=got_refs[i].at[j],
            send_sem=send_sems.at[4 * i + j], recv_sem=recv_sems.at[4 * i + j],
            device_id=(mx, my, 1 - mc), device_id_type=MESH) for i in range(n) for j in range(4)]

    shapes = tuple(_sds((4,) + x.shape[1:], x.dtype) for x in xs)
    return _with_gather(copies_of, n, shapes, [pltpu.SemaphoreType.DMA((4 * n,))] * 2, gather, name, xs)


def _scatter_chips(ps, name, gather=None):
    n = len(ps)

    def copies_of(p_refs, out_refs, sems):
        sends = []
        for i in range(n):
            sends += _chip_sends(p_refs[i], out_refs[i], sems[2 * i], sems[2 * i + 1])
        return sends

    shapes = tuple(_sds((3,) + p.shape[1:], p.dtype) for p in ps)
    return _with_gather(copies_of, n, shapes, [pltpu.SemaphoreType.DMA((3,))] * (2 * n), gather, name, ps)


def _add_sibling(x, got, tr, name):
    _, r, c = x.shape

    def body(x_ref, g_ref, o_ref):
        mc = lax.axis_index("c")
        for j in range(4):
            mine = jnp.where(mc == 0, x_ref[2 * j].astype(F32), x_ref[2 * j + 1].astype(F32))
            o_ref[j] = (mine + g_ref[j].astype(F32)).astype(o_ref.dtype)

    return _pcall(body, name=name, grid=(r // tr,), out_shape=_sds(got.shape, got.dtype),
                  in_specs=[pl.BlockSpec((NDEV, tr, c), lambda t: (0, t, 0)), pl.BlockSpec((4, tr, c), lambda t: (0, t, 0))],
                  out_specs=pl.BlockSpec((4, tr, c), lambda t: (0, t, 0)))(x, got)


def _sum_chips(part, recv, tr, name):
    _, r, c = part.shape

    def body(p_ref, r_ref, o_ref):
        slot = 2 * lax.axis_index("x") + lax.axis_index("y")
        acc = p_ref[0].astype(F32)
        for j in range(1, 4):
            acc = jnp.where(slot == j, p_ref[j].astype(F32), acc)
        for j in range(3):
            acc = acc + r_ref[j].astype(F32)
        o_ref[...] = acc

    return _pcall(body, name=name, grid=(r // tr,), out_shape=_sds((r, c), F32),
                  in_specs=[pl.BlockSpec((4, tr, c), lambda t: (0, t, 0)), pl.BlockSpec((3, tr, c), lambda t: (0, t, 0))],
                  out_specs=pl.BlockSpec((tr, c), lambda t: (t, 0)))(part, recv)


def _sum_slots(x, tr, name):
    n, r, c = x.shape

    def body(x_ref, o_ref):
        acc = x_ref[0].astype(F32)
        for s in range(1, n):
            acc = acc + x_ref[s].astype(F32)
        o_ref[...] = acc

    return _pcall(body, name=name, grid=(r // tr,), out_shape=_sds((r, c), F32),
                  in_specs=[pl.BlockSpec((n, tr, c), lambda t: (0, t, 0))],
                  out_specs=pl.BlockSpec((tr, c), lambda t: (t, 0)))(x)


def _ada_front(a_loc, w_loc, b_loc, wpack):
    ncol = w_loc.shape[1]
    nrow = NDEV * a_loc.shape[0]

    def body(a_ref, w_ref, b_ref, wp_ref, araw_ref, mloc_ref, mall_ref, wall_ref,
             a_vm, w_vm, m_vm, lsem, *sems):
        a_start, a_forward, a_finish = _gather_phases(a_ref, araw_ref, *sems[0:3])
        m_start, m_forward, m_finish = _gather_phases(mloc_ref, mall_ref, *sems[3:6])
        w_start, w_forward, w_finish = _gather_phases(wp_ref, wall_ref, *sems[6:9])
        w_in = pltpu.make_async_copy(w_ref, w_vm, lsem.at[0])
        w_in.start()
        a_start()
        w_start()
        a_forward()
        a_finish()
        a_in = pltpu.make_async_copy(araw_ref, a_vm, lsem.at[1])
        a_in.start()
        a_in.wait()
        w_in.wait()
        a = a_vm[...].reshape(nrow, D)
        act = (a * _sigmoid(a)).astype(BF16)
        m_vm[...] = _mm(act, w_vm[...].astype(BF16)) + b_ref[...]
        m_out = pltpu.make_async_copy(m_vm, mloc_ref, lsem.at[2])
        m_out.start()
        m_out.wait()
        m_start()
        m_forward()
        m_finish()
        w_forward()
        w_finish()

    return pl.pallas_call(
        body, name="ada_front",
        out_shape=(_sds((NDEV,) + a_loc.shape, F32), _sds((nrow, ncol), F32), _sds((NDEV, nrow, ncol), F32),
                   _sds((NDEV,) + wpack.shape, wpack.dtype)),
        in_specs=[ANY, ANY, VMEM, ANY], out_specs=(ANY, ANY, ANY, ANY),
        scratch_shapes=[pltpu.VMEM((NDEV,) + a_loc.shape, F32), pltpu.VMEM(w_loc.shape, F32),
                        pltpu.VMEM((nrow, ncol), F32), pltpu.SemaphoreType.DMA((3,))] + _GATHER_SEMS * 3,
        compiler_params=pltpu.CompilerParams(vmem_limit_bytes=32 * MIB),
    )(a_loc, w_loc, b_loc, wpack)


def _ada_bwd(a_raw, cctx_col, g_all, g_cols, w_loc, nb):
    nrow = a_raw.shape[0]
    ncol = w_loc.shape[1]

    def body(a_ref, cc_ref, gall_ref, g_ref, w_ref, dw_ref, pc_ref, gb_ref):
        a = a_ref[...]
        rowid = lax.broadcasted_iota(jnp.int32, (nrow, 1), 0) % 8
        act = jnp.where(rowid < nb, a * _sigmoid(a), 0.0).astype(BF16)
        g = g_ref[...]
        gc = _rowsum(jnp.where(rowid == nb, g, 0.0))
        cc = cc_ref[...]
        dw_ref[...] = _mm_tn(act, g.astype(BF16)) + (cc * _sigmoid(cc)) * gc
        pc_ref[...] = jnp.sum(w_ref[...] * gc, axis=1, keepdims=True)
        gb_ref[...] = _rowsum(gall_ref[...])

    return _pcall(body, name="ada_bwd",
                  out_shape=(_sds((D, ncol), F32), _sds((D, 1), F32), _sds((1, g_all.shape[1]), F32)),
                  in_specs=[VMEM] * 5, out_specs=(VMEM,) * 3, vmem_mb=48)(a_raw, cctx_col, g_all, g_cols, w_loc)


def _mod_spec(k, tpe, nrows):
    return pl.BlockSpec((1, k, D), lambda t: (jnp.minimum(t // tpe, nrows - 1), 0, 0))


def _load_ffn_weights(wall_ref, first, bufs, sems):
    fsh = FF // NDEV
    cps = []
    for j, buf in enumerate(bufs):
        for d in range(NDEV):
            cps.append(pltpu.make_async_copy(wall_ref.at[d, pl.ds((first + j) * fsh, fsh)],
                                             buf.at[pl.ds(d * fsh, fsh)], sems.at[j * NDEV + d]))
    for cp in cps:
        cp.start()
    for cp in cps:
        cp.wait()


def _token_specs(xs, tm, n_lat):
    specs = [pl.BlockSpec((tm, D), lambda t: (jnp.minimum(t, n_lat - 1), 0))]
    if len(xs) == 2:
        specs.append(pl.BlockSpec((tm, D), lambda t: (jnp.maximum(t - n_lat, 0), 0)))
    return specs


def _ffn_fwd(xs, mod3, norm_w, wall, first, *, tm, n_tiles, tpe, n_lat, name, target=None, gather=None):
    nrows = mod3.shape[0]
    r = n_tiles * tm
    nx = len(xs)
    with_loss = target is not None
    with_gather = gather is not None
    fwd_step = max(2 * n_tiles // 3, 1)

    def body(*refs):
        x_refs = refs[:nx]
        pos = nx
        if with_loss:
            tgt_ref = refs[pos]
            pos += 1
        mod_ref, nw_ref, wall_ref = refs[pos:pos + 3]
        pos += 3
        if with_gather:
            gin_ref = refs[pos]
            pos += 1
        xo_ref, a_ref, b_ref, o_ref = refs[pos:pos + 4]
        pos += 4
        if with_loss:
            ls_ref = refs[pos]
            pos += 1
        if with_gather:
            gout_ref = refs[pos]
            pos += 1
        w1_ref, w3_ref, w2_ref, wsem, acc_ref = refs[pos:pos + 5]
        t = pl.program_id(0)
        if with_gather:
            g_start, g_forward, g_finish = _gather_phases(gin_ref, gout_ref, *refs[pos + 5:])

        @pl.when(t == 0)
        def _():
            if with_gather:
                g_start()
            _load_ffn_weights(wall_ref, first, (w1_ref, w3_ref, w2_ref), wsem)
            if with_loss:
                ls_ref[...] = jnp.zeros_like(ls_ref)

        if with_gather:
            @pl.when(t == fwd_step)
            def _():
                g_forward()

            @pl.when(t == n_tiles - 1)
            def _():
                g_finish()

        x = x_refs[0][...]
        if nx == 2:
            x = jnp.where(t < n_lat, x, x_refs[1][...])
        n = x * _rms(x) * nw_ref[...]
        shift, scale, gate = mod_ref[0, 0:1, :], mod_ref[0, 1:2, :], mod_ref[0, 2:3, :]
        h = (n * (1.0 + scale) + shift).astype(BF16)
        nch = FF // FC
        o = None
        for lo_c, hi_c in ((0, nch // 2), (nch // 2, nch)):
            for j in range(lo_c, hi_c):
                sl = slice(j * FC, (j + 1) * FC)
                a = _mm_nt(h, w1_ref[sl, :])
                b = _mm_nt(h, w3_ref[sl, :])
                a_ref[:, sl] = a.astype(BF16)
                b_ref[:, sl] = b.astype(BF16)
                acc_ref[:, sl] = (a * _sigmoid(a) * b).astype(BF16)
            gs = slice(lo_c * FC, hi_c * FC)
            part = _mm(acc_ref[:, gs], w2_ref[gs, :])
            o = part if o is None else o + part
        o_ref[...] = o.astype(BF16)
        out = x + (0.5 * gate) * o
        if with_loss:
            d = out - tgt_ref[...]
            xo_ref[...] = d * (1.0 / D)
            ls_ref[...] += jnp.sum(d * d)
        else:
            xo_ref[...] = out

    row = lambda cols: pl.BlockSpec((tm, cols), lambda t: (t, 0))
    in_specs = _token_specs(xs, tm, n_lat) + ([row(D)] if with_loss else []) + [
        _mod_spec(3, tpe, nrows), _const((1, D)), ANY]
    out_shape = [_sds((r, D), F32), _sds((r, FF), BF16), _sds((r, FF), BF16), _sds((r, D), BF16)]
    out_specs = [row(D), row(FF), row(FF), row(D)]
    scratch = [pltpu.VMEM((FF, D), BF16)] * 3 + [pltpu.SemaphoreType.DMA((3 * NDEV,)), pltpu.VMEM((tm, FF), BF16)]
    if with_loss:
        out_shape.append(_sds((8, LANE), F32))
        out_specs.append(_const((8, LANE)))
    args = list(xs) + ([target] if with_loss else []) + [mod3, norm_w, wall]
    if with_gather:
        assert n_tiles >= 2
        in_specs.append(ANY)
        args.append(gather)
        out_shape.append(_sds((NDEV,) + gather.shape, gather.dtype))
        out_specs.append(ANY)
        scratch += _GATHER_SEMS
    return _pcall(
        body, name=name, grid=(n_tiles,), out_shape=tuple(out_shape), in_specs=in_specs, out_specs=tuple(out_specs),
        scratch=scratch, vmem_mb=56)(*args)


def _ffn_bwd_dx(dout, xs, a, b, o, mod3, norm_w, wall, first, *, tm, n_tiles, tpe, n_lat, name, part=None):
    nrows = mod3.shape[0]
    r = n_tiles * tm
    nx = len(xs)
    fused = part is not None

    def body(*refs):
        dout_ref = refs[0]
        x_refs = refs[1:1 + nx]
        rest = refs[1 + nx:]
        a_ref, b_ref, o_ref, mod_ref, nw_ref, wall_ref = rest[:6]
        rest = rest[6:]
        if fused:
            part_ref, rest = rest[0], rest[1:]
        dx_ref, da_ref, db_ref, g_ref, do_ref, h_ref, dmod_ref, dnw_ref = rest[:8]
        rest = rest[8:]
        if fused:
            recv_ref, rest = rest[0], rest[1:]
        w1_ref, w3_ref, w2_ref, wsem = rest[:4]
        t = pl.program_id(0)
        if fused:
            _chip_exchange_behind(part_ref, recv_ref, rest[4], rest[5], t == 0, t == n_tiles - 1)

        @pl.when(t == 0)
        def _():
            _load_ffn_weights(wall_ref, first, (w1_ref, w3_ref, w2_ref), wsem)
            dnw_ref[...] = jnp.zeros_like(dnw_ref)

        @pl.when(jnp.where(t < n_lat, t % tpe == 0, t == n_lat))
        def _():
            dmod_ref[...] = jnp.zeros_like(dmod_ref)

        x = x_refs[0][...]
        if nx == 2:
            x = jnp.where(t < n_lat, x, x_refs[1][...])
        dout = dout_ref[...]
        shift, scale, gate = mod_ref[0, 0:1, :], mod_ref[0, 1:2, :], mod_ref[0, 2:3, :]
        d_o = ((0.5 * gate) * dout).astype(BF16)
        do_ref[...] = d_o
        nch = FF // FC
        groups = ((0, nch // 2), (nch // 2, nch))
        dh = None
        for lo_c, hi_c in groups:
            for j in range(lo_c, hi_c):
                sl = slice(j * FC, (j + 1) * FC)
                av = a_ref[:, sl].astype(F32)
                bv = b_ref[:, sl].astype(F32)
                dg = _mm_nt(d_o, w2_ref[sl, :])
                sig = _sigmoid(av)
                sa = av * sig
                g_ref[:, sl] = (sa * bv).astype(BF16)
                da_ref[:, sl] = (dg * bv * (sig * (1.0 + av * (1.0 - sig)))).astype(BF16)
                db_ref[:, sl] = (dg * sa).astype(BF16)
            gs = slice(lo_c * FC, hi_c * FC)
            part = _mm(da_ref[:, gs], w1_ref[gs, :]) + _mm(db_ref[:, gs], w3_ref[gs, :])
            dh = part if dh is None else dh + part
        rr = _rms(x)
        xh = x * rr
        nw = nw_ref[...]
        n = xh * nw
        h_ref[...] = (n * (1.0 + scale) + shift).astype(BF16)
        dgate = _rowsum(0.5 * o_ref[...].astype(F32) * dout)
        dn = dh * (1.0 + scale)
        dxh = dn * nw
        dmod_ref[0, 0:1, :] += _rowsum(dh)
        dmod_ref[0, 1:2, :] += _rowsum(dh * n)
        dmod_ref[0, 2:3, :] += dgate
        dnw_ref[...] += _rowsum(dn * xh)
        dx = dout + rr * (dxh - xh * jnp.mean(dxh * xh, axis=-1, keepdims=True))
        if n_tiles == n_lat:
            dx_ref[...] = dx
        else:
            @pl.when(t < n_lat)
            def _():
                dx_ref[...] = dx

    row = lambda cols: pl.BlockSpec((tm, cols), lambda t: (t, 0))
    lat = pl.BlockSpec((tm, D), lambda t: (jnp.minimum(t, n_lat - 1), 0))
    out_shape = [_sds((n_lat * tm, D), F32), _sds((r, FF), BF16), _sds((r, FF), BF16), _sds((r, FF), BF16),
                 _sds((r, D), BF16), _sds((r, D), BF16), _sds((nrows, 3, D), F32), _sds((1, D), F32)]
    in_specs = [row(D)] + _token_specs(xs, tm, n_lat) + [row(FF), row(FF), row(D), _mod_spec(3, tpe, nrows),
                                                          _const((1, D)), ANY]
    out_specs = [lat, row(FF), row(FF), row(FF), row(D), row(D), _mod_spec(3, tpe, nrows), _const((1, D))]
    scratch = [pltpu.VMEM((FF, D), BF16)] * 3 + [pltpu.SemaphoreType.DMA((3 * NDEV,))]
    args = [dout, *xs, a, b, o, mod3, norm_w, wall]
    if fused:
        in_specs.append(ANY)
        args.append(part)
        out_shape.append(_sds((3,) + part.shape[1:], part.dtype))
        out_specs.append(ANY)
        scratch += [pltpu.SemaphoreType.DMA((3,))] * 2
    return _pcall(body, name=name, grid=(n_tiles,), out_shape=tuple(out_shape), in_specs=in_specs,
                  out_specs=tuple(out_specs), scratch=scratch, vmem_mb=60)(*args)


def _ffn_bwd_dw(h, d_o, da, db, g, *, tr, name):
    r = h.shape[0]
    fh = FF // 2
    fsh = FF // NDEV
    nk = r // tr

    def body(h_ref, do_ref, da_ref, db_ref, g_ref, out_ref, acc1, acc3, acc2):
        k = pl.program_id(1)

        @pl.when(k == 0)
        def _():
            acc1[...] = jnp.zeros_like(acc1)
            acc3[...] = jnp.zeros_like(acc3)
            acc2[...] = jnp.zeros_like(acc2)

        hv = h_ref[...]
        acc1[...] += _mm_tn(da_ref[...], hv)
        acc3[...] += _mm_tn(db_ref[...], hv)
        acc2[...] += _mm_tn(g_ref[...], do_ref[...])

        @pl.when(k == nk - 1)
        def _():
            for i, acc in enumerate((acc1, acc3, acc2)):
                out_ref[:, i * fsh:(i + 1) * fsh, :] = acc[...].reshape(NDEV // 2, fsh, D).astype(BF16)

    rowd = pl.BlockSpec((tr, D), lambda f, k: (k, 0))
    rowf = pl.BlockSpec((tr, fh), lambda f, k: (k, f))
    return _pcall(
        body, name=name, grid=(2, nk), out_shape=_sds((NDEV, 3 * fsh, D), BF16),
        in_specs=[rowd, rowd, rowf, rowf, rowf],
        out_specs=pl.BlockSpec((NDEV // 2, 3 * fsh, D), lambda f, k: (f, 0, 0)),
        scratch=[pltpu.VMEM((fh, D), F32)] * 3, vmem_mb=56)(h, d_o, da, db, g)


def _chip_exchange_behind(part_ref, recv_ref, send_sems, recv_sems, first, last):
    sends = _chip_sends(part_ref, recv_ref, send_sems, recv_sems)

    @pl.when(first)
    def _():
        for cp in sends:
            cp.start()

    @pl.when(last)
    def _():
        for cp in sends:
            cp.wait_recv()
        for cp in sends:
            cp.wait_send()


def _ffn_bwd_dw_one(lhs, rhs, *, tr, name, part=None):
    r = lhs.shape[0]
    fsh = FF // NDEV
    nk = r // tr
    fused = part is not None

    def body(*refs):
        if fused:
            lhs_ref, rhs_ref, part_ref, out_ref, recv_ref, acc, send_sems, recv_sems = refs
        else:
            lhs_ref, rhs_ref, out_ref, acc = refs
        k = pl.program_id(0)
        if fused:
            _chip_exchange_behind(part_ref, recv_ref, send_sems, recv_sems, k == 0, k == nk - 1)

        @pl.when(k == 0)
        def _():
            acc[...] = jnp.zeros_like(acc)

        acc[...] += _mm_tn(lhs_ref[...], rhs_ref[...])

        @pl.when(k == nk - 1)
        def _():
            out_ref[...] = acc[...].reshape(NDEV, fsh, D).astype(BF16)

    in_specs = [pl.BlockSpec((tr, FF), lambda k: (k, 0)), pl.BlockSpec((tr, D), lambda k: (k, 0))]
    out_shape = [_sds((NDEV, fsh, D), BF16)]
    out_specs = [_const((NDEV, fsh, D))]
    scratch = [pltpu.VMEM((FF, D), F32)]
    args = [lhs, rhs]
    if fused:
        in_specs.append(ANY)
        args.append(part)
        out_shape.append(_sds((3,) + part.shape[1:], part.dtype))
        out_specs.append(ANY)
        scratch += [pltpu.SemaphoreType.DMA((3,))] * 2
    res = _pcall(body, name=name, grid=(nk,), out_shape=tuple(out_shape), in_specs=in_specs,
                 out_specs=tuple(out_specs), scratch=scratch, vmem_mb=48)(*args)
    return res if fused else res[0]


_PIECES =((0, 128), (128, 384), (384, 896), (896, 1408), (1408, 1536))


def _proj_fwd(x1, mod2, norm_w, wint, *, tm, n_tiles, tpe, name="proj_fwd"):
    nrows = mod2.shape[0]
    r = n_tiles * tm

    def body(x_ref, mod_ref, nw_ref, w_ref, ckv_ref, q_ref, u_ref, v_ref, kpe_ref):
        x = x_ref[...]
        n = x * _rms(x) * nw_ref[...]
        h = (n * (1.0 + mod_ref[0, 1:2, :]) + mod_ref[0, 0:1, :]).astype(BF16)
        for (lo, hi), ref in zip(_PIECES, (ckv_ref, q_ref, u_ref, v_ref, kpe_ref)):
            ref[...] = _mm_nt(h, w_ref[lo:hi, :])

    row = lambda cols: pl.BlockSpec((tm, cols), lambda t: (t, 0))
    widths = [hi - lo for lo, hi in _PIECES]
    return _pcall(
        body, name=name, grid=(n_tiles,),
        out_shape=tuple(_sds((r, w), F32) for w in widths),
        in_specs=[row(D), _mod_spec(2, tpe, nrows), _const((1, D)), _const((WIN_ROWS, D))],
        out_specs=tuple(row(w) for w in widths), vmem_mb=40)(x1, mod2, norm_w, wint)


def _proj_bwd(dckv, dkpe, dq, du, dv, dx2, x1, mod2, norm_w, wint, *, tm, n_tiles, tpe, n_lat, name="proj_bwd"):
    nrows = mod2.shape[0]
    r = n_tiles * tm

    def body(dckv_ref, dkpe_ref, dq_ref, du_ref, dv_ref, dx2_ref, x_ref, mod_ref, nw_ref, w_ref,
             dx_ref, dw_ref, dmod_ref, dnw_ref):
        t = pl.program_id(0)
        is_lat = t < n_lat

        @pl.when(t == 0)
        def _():
            dw_ref[...] = jnp.zeros_like(dw_ref)
            dnw_ref[...] = jnp.zeros_like(dnw_ref)

        @pl.when(jnp.where(is_lat, t % tpe == 0, t == n_lat))
        def _():
            dmod_ref[...] = jnp.zeros_like(dmod_ref)

        x = x_ref[...]
        rr = _rms(x)
        xh = x * rr
        nw = nw_ref[...]
        n = xh * nw
        scale = mod_ref[0, 1:2, :]
        h = (n * (1.0 + scale) + mod_ref[0, 0:1, :]).astype(BF16)
        zero = jnp.zeros((), BF16)
        pieces = (dckv_ref[...], jnp.where(is_lat, dq_ref[...], zero), jnp.where(is_lat, du_ref[...], zero),
                  jnp.where(is_lat, dv_ref[...], zero), dkpe_ref[...])
        dh = None
        for (lo, hi), piece in zip(_PIECES, pieces):
            part = _mm(piece, w_ref[lo:hi, :])
            dh = part if dh is None else dh + part
        dn = dh * (1.0 + scale)
        dxh = dn * nw
        dx = rr * (dxh - xh * jnp.mean(dxh * xh, axis=-1, keepdims=True))
        dx_ref[...] = dx + jnp.where(is_lat, dx2_ref[...], 0.0)
        dmod_ref[0, 0:1, :] += _rowsum(dh)
        dmod_ref[0, 1:2, :] += _rowsum(dh * n)
        dnw_ref[...] += _rowsum(dn * xh)
        for (lo, hi), piece in zip(_PIECES, pieces):
            dw_ref[lo:hi, :] += _mm_tn(piece, h)

    row = lambda cols: pl.BlockSpec((tm, cols), lambda t: (t, 0))
    lat = lambda cols: pl.BlockSpec((tm, cols), lambda t: (jnp.minimum(t, n_lat - 1), 0))
    return _pcall(
        body, name=name, grid=(n_tiles,),
        out_shape=(_sds((r, D), F32), _sds((WIN_ROWS, D), F32), _sds((nrows, 2, D), F32), _sds((1, D), F32)),
        in_specs=[row(128), row(128), lat(256), lat(512), lat(512), lat(D), row(D), _mod_spec(2, tpe, nrows),
                  _const((1, D)), _const((WIN_ROWS, D))],
        out_specs=(row(D), _const((WIN_ROWS, D)), _mod_spec(2, tpe, nrows), _const((1, D))),
        vmem_mb=48)(dckv, dkpe, dq, du, dv, dx2, x1, mod2, norm_w, wint)


def _seg_sum(x, seg):
    return _mm(x.astype(BF16), seg)


def _seg_bcast(v, segt2):
    hi = v.astype(BF16)
    lo = (v - hi.astype(F32)).astype(BF16)
    return _mm(jnp.concatenate([hi, lo], axis=-1), segt2)


def _rope_pairs(t, cos, sin, rot2):
    cos2, sin2 = jnp.concatenate([cos, cos], axis=-1), jnp.concatenate([sin, sin], axis=-1)
    out = []
    for j in range(H // 2):
        tj = t[:, 2 * j * LANE:2 * (j + 1) * LANE]
        out.append(tj * cos2 + _dot_hl(tj, rot2) * sin2)
    return jnp.concatenate(out, axis=-1)


def _head_norm_rope(x, w_pad, cos, sin, seg, segt2, rot2, rope=True):
    rh = lax.rsqrt(_seg_sum(x * x, seg) * (1.0 / DH) + EPS)
    rb = _seg_bcast(rh, segt2)
    y = x * rb
    out = _rope_pairs(y * w_pad, cos, sin, rot2) if rope else None
    return out, y, rb


def _head_norm_rope_bwd(dout, y, rb, w_pad, cos, sin, seg, segt2, rot2_t):
    cos2, sin2 = jnp.concatenate([cos, cos], axis=-1), jnp.concatenate([sin, sin], axis=-1)
    dt = []
    for j in range(H // 2):
        dj = dout[:, 2 * j * LANE:2 * (j + 1) * LANE]
        dt.append(dj * cos2 + _dot_hl(dj * sin2, rot2_t))
    dt = jnp.concatenate(dt, axis=-1)
    dw = _rowsum(dt * y)
    dy = dt * w_pad
    mean_h = _seg_sum(dy * y, seg) * (1.0 / DH)
    return rb * (dy - y * _seg_bcast(mean_h, segt2)), dw


def _q_prep_fwd(qp, qa_w, wuq, wq, cos, sin, cs, *, tm, n_lat, tpe):
    def body(qp_ref, qa_ref, wuq_ref, wq_ref, cos_ref, sin_ref, seg, segt, rot, q_ref):
        x = qp_ref[...]
        cq = (x * _rms(x) * qa_ref[...]).astype(BF16)
        q, _, _ = _head_norm_rope(_mm_nt(cq, wuq_ref[...]), wq_ref[...], cos_ref[...], sin_ref[...],
                                  seg[...], segt[...], rot[...])
        q_ref[...] = q.astype(BF16)

    row = lambda cols: pl.BlockSpec((tm, cols), lambda t: (t, 0))
    tab = pl.BlockSpec((tm, LANE), lambda t: (t % tpe, 0))
    return _pcall(
        body, name="q_prep_fwd", grid=(n_lat,), out_shape=_sds((n_lat * tm, HP), BF16),
        in_specs=[row(QL), _const((1, QL)), _const((HP, QL)), _const((1, HP)), tab, tab,
                  _const((HP, LANE)), _const((2 * LANE, HP)), _const((2 * LANE, 2 * LANE))],
        out_specs=row(HP))(qp, qa_w, wuq, wq, cos, sin, cs["seg_h"], cs["seg_ht"], cs["rot"])


def _q_prep_bwd(dq, qp, qa_w, wuq, wq, cos, sin, cs, *, tm, n_lat, tpe):
    def body(dq_ref, qp_ref, qa_ref, wuq_ref, wq_ref, cos_ref, sin_ref, seg, segt, rot, rot_t,
             dqp_ref, dwuq_ref, dqa_ref, dwq_ref):
        @pl.when(pl.program_id(0) == 0)
        def _():
            dwuq_ref[...] = jnp.zeros_like(dwuq_ref)
            dqa_ref[...] = jnp.zeros_like(dqa_ref)
            dwq_ref[...] = jnp.zeros_like(dwq_ref)

        x = qp_ref[...]
        ra = _rms(x)
        xh = x * ra
        qa = qa_ref[...]
        cq = (xh * qa).astype(BF16)
        wuq_v = wuq_ref[...]
        wq_v, cos_v, sin_v = wq_ref[...], cos_ref[...], sin_ref[...]
        _, y, rb = _head_norm_rope(_mm_nt(cq, wuq_v), wq_v, cos_v, sin_v, seg[...], segt[...], rot[...], rope=False)
        dqraw, dwq = _head_norm_rope_bwd(dq_ref[...], y, rb, wq_v, cos_v, sin_v, seg[...], segt[...], rot_t[...])
        dqraw = dqraw.astype(BF16)
        dcq = _mm(dqraw, wuq_v)
        dxh = dcq * qa
        dqp_ref[...] = (ra * (dxh - xh * jnp.mean(dxh * xh, axis=-1, keepdims=True))).astype(BF16)
        dwuq_ref[...] += _mm_tn(dqraw, cq)
        dqa_ref[...] += _rowsum(dcq * xh)
        dwq_ref[...] += dwq

    row = lambda cols: pl.BlockSpec((tm, cols), lambda t: (t, 0))
    tab = pl.BlockSpec((tm, LANE), lambda t: (t % tpe, 0))
    return _pcall(
        body, name="q_prep_bwd", grid=(n_lat,),
        out_shape=(_sds((n_lat * tm, QL), BF16), _sds((HP, QL), F32), _sds((1, QL), F32), _sds((1, HP), F32)),
        in_specs=[row(HP), row(QL), _const((1, QL)), _const((HP, QL)), _const((1, HP)), tab, tab,
                  _const((HP, LANE)), _const((2 * LANE, HP)), _const((2 * LANE, 2 * LANE)), _const((2 * LANE, 2 * LANE))],
        out_specs=(row(QL), _const((HP, QL)), _const((1, QL)), _const((1, HP))), vmem_mb=40)(
            dq, qp, qa_w, wuq, wq, cos, sin, cs["seg_h"], cs["seg_ht"], cs["rot"], cs["rot_t"])


def _kv_tab_spec(tm, tpe, n_lat):
    return pl.BlockSpec((tm, LANE), lambda t: (jnp.where(t < n_lat, t % tpe, tpe), 0))


def _kv_prep_fwd(ckv, kpe, kva_w, wukv, wk, cosk, sink, cs, *, tm, n_tiles, tpe, n_lat):
    def body(ckv_ref, kpe_ref, kva_ref, wukv_ref, wk_ref, cos_ref, sin_ref, seg, segt, rot, k_ref, v_ref):
        x = ckv_ref[...]
        ckvn = (x * _rms(x) * kva_ref[...]).astype(BF16)
        kv = _mm_nt(ckvn, wukv_ref[...])
        kx = kv[:, :HP] + jnp.concatenate([kpe_ref[...]] * H, axis=-1)
        k, _, _ = _head_norm_rope(kx, wk_ref[...], cos_ref[...], sin_ref[...], seg[...], segt[...], rot[...])
        k_ref[...] = k.astype(BF16)
        v_ref[...] = kv[:, HP:].astype(BF16)

    row = lambda cols: pl.BlockSpec((tm, cols), lambda t: (t, 0))
    tab = _kv_tab_spec(tm, tpe, n_lat)
    r = n_tiles * tm
    return _pcall(
        body, name="kv_prep_fwd", grid=(n_tiles,), out_shape=(_sds((r, HP), BF16), _sds((r, HP), BF16)),
        in_specs=[row(KVL), row(LANE), _const((1, KVL)), _const((2 * HP, KVL)), _const((1, HP)), tab, tab,
                  _const((HP, LANE)), _const((2 * LANE, HP)), _const((2 * LANE, 2 * LANE))],
        out_specs=(row(HP), row(HP)), vmem_mb=40)(
            ckv, kpe, kva_w, wukv, wk, cosk, sink, cs["seg_h"], cs["seg_ht"], cs["rot"])


def _kv_prep_bwd(dks, dvs, ckv, kpe, kva_w, wukv, wk, cosk, sink, cs, *, tm, n_tiles, tpe, n_lat):
    def body(dkl_ref, dkc_ref, dvl_ref, dvc_ref, ckv_ref, kpe_ref, kva_ref, wukv_ref, wk_ref, cos_ref, sin_ref,
             seg, segt, rot, rot_t, dckv_ref, dkpe_ref, dwukv_ref, dkva_ref, dwk_ref):
        t = pl.program_id(0)
        is_lat = t < n_lat

        @pl.when(t == 0)
        def _():
            dwukv_ref[...] = jnp.zeros_like(dwukv_ref)
            dkva_ref[...] = jnp.zeros_like(dkva_ref)
            dwk_ref[...] = jnp.zeros_like(dwk_ref)

        dk = jnp.where(is_lat, dkl_ref[...], dkc_ref[...])
        dv = jnp.where(is_lat, dvl_ref[...], dvc_ref[...])
        x = ckv_ref[...]
        ra = _rms(x)
        xh = x * ra
        kva = kva_ref[...]
        ckvn = (xh * kva).astype(BF16)
        wukv_v = wukv_ref[...]
        wk_v, cos_v, sin_v = wk_ref[...], cos_ref[...], sin_ref[...]
        kv = _mm_nt(ckvn, wukv_v)
        kx = kv[:, :HP] + jnp.concatenate([kpe_ref[...]] * H, axis=-1)
        _, y, rb = _head_norm_rope(kx, wk_v, cos_v, sin_v, seg[...], segt[...], rot[...], rope=False)
        dkx, dwk = _head_norm_rope_bwd(dk, y, rb, wk_v, cos_v, sin_v, seg[...], segt[...], rot_t[...])
        dkpe = dkx[:, 0:LANE]
        for h in range(1, H):
            dkpe = dkpe + dkx[:, h * LANE:(h + 1) * LANE]
        lane = lax.broadcasted_iota(jnp.int32, (tm, LANE), 1)
        dkpe_ref[...] = jnp.where((lane >= DN) & (lane < DH), dkpe, 0.0).astype(BF16)
        dkv = jnp.concatenate([dkx, dv], axis=-1).astype(BF16)
        dckvn = _mm(dkv, wukv_v)
        dxh = dckvn * kva
        dckv_ref[...] = (ra * (dxh - xh * jnp.mean(dxh * xh, axis=-1, keepdims=True))).astype(BF16)
        dwukv_ref[...] += _mm_tn(dkv, ckvn)
        dkva_ref[...] += _rowsum(dckvn * xh)
        dwk_ref[...] += dwk

    row = lambda cols: pl.BlockSpec((tm, cols), lambda t: (t, 0))
    lat = pl.BlockSpec((tm, HP), lambda t: (jnp.minimum(t, n_lat - 1), 0))
    ctx = pl.BlockSpec((tm, HP), lambda t: (jnp.maximum(t - n_lat, 0), 0))
    tab = _kv_tab_spec(tm, tpe, n_lat)
    r = n_tiles * tm
    return _pcall(
        body, name="kv_prep_bwd", grid=(n_tiles,),
        out_shape=(_sds((r, KVL), BF16), _sds((r, LANE), BF16), _sds((2 * HP, KVL), F32), _sds((1, KVL), F32),
                   _sds((1, HP), F32)),
        in_specs=[lat, ctx, lat, ctx, row(KVL), row(LANE), _const((1, KVL)), _const((2 * HP, KVL)), _const((1, HP)),
                  tab, tab, _const((HP, LANE)), _const((2 * LANE, HP)), _const((2 * LANE, 2 * LANE)), _const((2 * LANE, 2 * LANE))],
        out_specs=(row(KVL), row(LANE), _const((2 * HP, KVL)), _const((1, KVL)), _const((1, HP))), vmem_mb=48)(
            dks[0], dks[1], dvs[0], dvs[1], ckv, kpe, kva_w, wukv, wk, cosk, sink,
            cs["seg_h"], cs["seg_ht"], cs["rot"], cs["rot_t"])


_SCALE = DH ** -0.5
_SCALE_LOG2E = _SCALE * 1.4426950408889634


def _key_chunks(s, nc, ck):
    return ([(0, lo, min(lo + ck, s)) for lo in range(0, s, ck)]
            + [(1, lo, min(lo + ck, nc)) for lo in range(0, nc, ck)])


def _attn_fwd(q, k, v, *, nb, s, nc, tq, ck):
    tpe = s // tq
    r_lat = nb * s
    chunks = _key_chunks(s, nc, ck)
    hp = 4

    def body(q_ref, kl_ref, kc_ref, vl_ref, vc_ref, o_ref, lse_ref):
        k_refs, v_refs = (kl_ref, kc_ref), (vl_ref, vc_ref)
        for hh in range(hp):
            hs = slice(hh * LANE, (hh + 1) * LANE)
            qv = q_ref[:, hs]
            xs = [_mm_nt(qv, k_refs[w][lo:hi, hs]) for w, lo, hi in chunks]
            m = jnp.max(xs[0], axis=-1, keepdims=True)
            for x in xs[1:]:
                m = jnp.maximum(m, jnp.max(x, axis=-1, keepdims=True))
            l = acc = None
            for x, (w, lo, hi) in zip(xs, chunks):
                e = jnp.exp2((x - m) * _SCALE_LOG2E)
                lc = jnp.sum(e, axis=-1, keepdims=True)
                pv = _mm(e.astype(BF16), v_refs[w][lo:hi, hs])
                l = lc if l is None else l + lc
                acc = pv if acc is None else acc + pv
            o_ref[:, hs] = (acc / l).astype(BF16)
            lse = m * _SCALE_LOG2E + jnp.log2(l)
            lse_ref[hh] = jnp.transpose(jnp.broadcast_to(lse, (tq, LANE)))[0:8, :]

    qs = pl.BlockSpec((tq, hp * LANE), lambda i, j, t: (i * tpe + t, j))
    kl = pl.BlockSpec((s, hp * LANE), lambda i, j, t: (i, j))
    kc = pl.BlockSpec((nc, hp * LANE), lambda i, j, t: (r_lat // nc + i, j))
    ls = pl.BlockSpec((hp, 8, tq), lambda i, j, t: (i * (H // hp) + j, 0, t))
    return _pcall(body, name="attn_fwd", grid=(nb, H // hp, tpe),
                  out_shape=(_sds((r_lat, HP), BF16), _sds((nb * H, 8, s), F32)),
                  in_specs=[qs, kl, kc, kl, kc], out_specs=(qs, ls), vmem_mb=48)(q, k, k, v, v)


def _attn_bwd(q, k, v, o, do, lse, part, *, nb, s, nc, tq, ck):
    tpe = s // tq
    r_lat = nb * s
    chunks = _key_chunks(s, nc, ck)
    hp = 2
    n_steps = nb * (H // hp) * tpe

    def body(q_ref, kl_ref, kc_ref, vl_ref, vc_ref, o_ref, do_ref, lse_ref, part_ref,
             dq_ref, dkl_ref, dkc_ref, dvl_ref, dvc_ref, recv_ref, akl, akc, avl, avc, send_sems, recv_sems):
        t = pl.program_id(2)
        step = (pl.program_id(0) * (H // hp) + pl.program_id(1)) * tpe + t
        sends = _chip_sends(part_ref, recv_ref, send_sems, recv_sems)

        @pl.when(step == 0)
        def _():
            for cp in sends:
                cp.start()

        @pl.when(step == n_steps - 1)
        def _():
            for cp in sends:
                cp.wait_recv()
            for cp in sends:
                cp.wait_send()

        @pl.when(t == 0)
        def _():
            akl[...] = jnp.zeros_like(akl)
            akc[...] = jnp.zeros_like(akc)
            avl[...] = jnp.zeros_like(avl)
            avc[...] = jnp.zeros_like(avc)

        k_refs, v_refs, ak, av = (kl_ref, kc_ref), (vl_ref, vc_ref), (akl, akc), (avl, avc)
        for hh in range(hp):
            hs = slice(hh * LANE, (hh + 1) * LANE)
            qv = q_ref[:, hs]
            lse = jnp.transpose(jnp.concatenate([lse_ref[hh]] * (LANE // 8), axis=0))[:, 0:1]
            dov = do_ref[:, hs]
            delta = jnp.sum(dov.astype(F32) * o_ref[:, hs].astype(F32), axis=-1, keepdims=True)
            dq = None
            for w, lo, hi in chunks:
                kc_v = k_refs[w][lo:hi, hs]
                p = jnp.exp2(_mm_nt(qv, kc_v) * _SCALE_LOG2E - lse)
                ds = (p * (_mm_nt(dov, v_refs[w][lo:hi, hs]) - delta)).astype(BF16)
                part = _mm(ds, kc_v)
                dq = part if dq is None else dq + part
                ak[w][hs, lo:hi] += _mm_tn(qv, ds)
                av[w][hs, lo:hi] += _mm_tn(dov, p.astype(BF16))
            dq_ref[:, hs] = dq * _SCALE

        @pl.when(t == tpe - 1)
        def _():
            dkl_ref[...] = akl[...].T * _SCALE
            dkc_ref[...] = akc[...].T * _SCALE
            dvl_ref[...] = avl[...].T
            dvc_ref[...] = avc[...].T

    qs = pl.BlockSpec((tq, hp * LANE), lambda i, j, t: (i * tpe + t, j))
    kl = pl.BlockSpec((s, hp * LANE), lambda i, j, t: (i, j))
    kc = pl.BlockSpec((nc, hp * LANE), lambda i, j, t: (r_lat // nc + i, j))
    kc_out = pl.BlockSpec((nc, hp * LANE), lambda i, j, t: (i, j))
    ls = pl.BlockSpec((hp, 8, tq), lambda i, j, t: (i * (H // hp) + j, 0, t))
    return _pcall(
        body, name="attn_bwd", grid=(nb, H // hp, tpe),
        out_shape=(_sds((r_lat, HP), F32), _sds((r_lat, HP), F32), _sds((nb * nc, HP), F32),
                   _sds((r_lat, HP), F32), _sds((nb * nc, HP), F32), _sds((3,) + part.shape[1:], part.dtype)),
        in_specs=[qs, kl, kc, kl, kc, qs, qs, ls, ANY], out_specs=(qs, kl, kc_out, kl, kc_out, ANY),
        scratch=[pltpu.VMEM((hp * LANE, s), F32), pltpu.VMEM((hp * LANE, nc), F32)] * 2
        + [pltpu.SemaphoreType.DMA((3,))] * 2,
        vmem_mb=60)(q, k, k, v, v, o, do, lse, part)


def _chunks_side_by_side(x, j, nch):
    return jnp.concatenate([x[c * CH:(c + 1) * CH, j * LANE:(j + 1) * LANE] for c in range(nch)], axis=-1)


def _first_group_lanes(nch):
    return (lax.broadcasted_iota(jnp.int32, (CH, nch * LANE), 1) & (LANE - 1)) < GD


def _gating(vn, ws_ref, bias_ref, s_scr, tm):
    nch = tm // CH
    first = _first_group_lanes(nch)
    for j in range(G // 2):
        ls = slice(j * LANE, (j + 1) * LANE)
        vst = _chunks_side_by_side(vn, j, nch)
        st = jnp.where(first, _mm(ws_ref[2 * j], vst), _mm(ws_ref[2 * j + 1], vst))
        for c in range(nch):
            s_scr[c * CH:(c + 1) * CH, ls] = st[:, c * LANE:(c + 1) * LANE] + bias_ref[:, ls]


def _mix_fwd(u, v, attn, x1, gate, wv, ws, bias, wout, cs, *, tm, n_lat, tpe):
    nrows = gate.shape[0]

    def body(u_ref, v_ref, attn_ref, x_ref, gate_ref, wv_ref, ws_ref, bias_ref, wout_ref, seg, segt,
             x2_ref, mix_ref, s_scr):
        vg = _gelu(v_ref[...])
        rg = lax.rsqrt(_seg_sum(vg * vg, seg[...]) * (1.0 / GD) + EPS)
        vn = (vg * _seg_bcast(rg, segt[...]) * wv_ref[...]).astype(BF16)
        _gating(vn, ws_ref, bias_ref, s_scr, tm)
        sg = (_gelu(u_ref[...]) * s_scr[...]).astype(BF16)
        mix = _mm(attn_ref[...], wout_ref[0:HP, :]) + _mm(sg, wout_ref[HP:, :])
        mix_ref[...] = mix.astype(BF16)
        x2_ref[...] = x_ref[...] + gate_ref[0] * mix

    row = lambda cols: pl.BlockSpec((tm, cols), lambda t: (t, 0))
    r = n_lat * tm
    return _pcall(
        body, name="mix_fwd", grid=(n_lat,),
        out_shape=(_sds((r, D), F32), _sds((r, D), BF16)),
        in_specs=[row(G * GD), row(G * GD), row(HP), row(D), _mod_spec(1, tpe, nrows), _const((1, G * GD)),
                  _const((G, CH, CH)), _const((CH, G * GD)), _const((HP + G * GD, D)), _const((G * GD, LANE)),
                  _const((2 * LANE, G * GD))],
        out_specs=(row(D), row(D)), scratch=[pltpu.VMEM((tm, G * GD), F32)], vmem_mb=40)(
            u, v, attn, x1, gate, wv, ws, bias, wout, cs["seg_g"], cs["seg_gt"])


def _mix_bwd(dx2, mix, u, v, attn, gate, wv, ws, wst, bias, wout, cs, *, tm, n_lat, tpe):
    nrows = gate.shape[0]
    wrows = HP + G * GD

    def body(dx2_ref, mix_ref, u_ref, v_ref, attn_ref, gate_ref, wv_ref, ws_ref, wst_ref, bias_ref, wout_ref, seg, segt,
             dattn_ref, du_ref, dv_ref, dgate_ref, dwout_ref, dws_ref, dbs_ref, dwv_ref, s_scr, dvn_scr, dbias_scr):
        t = pl.program_id(0)

        @pl.when(t == 0)
        def _():
            dwout_ref[...] = jnp.zeros_like(dwout_ref)
            dws_ref[...] = jnp.zeros_like(dws_ref)
            dwv_ref[...] = jnp.zeros_like(dwv_ref)
            dbias_scr[...] = jnp.zeros_like(dbias_scr)

        @pl.when(t % tpe == 0)
        def _():
            dgate_ref[...] = jnp.zeros_like(dgate_ref)

        dx2 = dx2_ref[...]
        dmix = (dx2 * gate_ref[0]).astype(BF16)
        dcat = _mm_nt(dmix, wout_ref[...])
        dattn_ref[...] = dcat[:, :HP].astype(BF16)
        dsg = dcat[:, HP:]

        vraw = v_ref[...]
        vg = _gelu(vraw)
        rg = lax.rsqrt(_seg_sum(vg * vg, seg[...]) * (1.0 / GD) + EPS)
        r64 = _seg_bcast(rg, segt[...])
        y = vg * r64
        wv_v = wv_ref[...]
        vn = (y * wv_v).astype(BF16)
        _gating(vn, ws_ref, bias_ref, s_scr, tm)
        uraw = u_ref[...]
        ug = _gelu(uraw)
        s = s_scr[...]
        sg = (ug * s).astype(BF16)
        du_ref[...] = (dsg * s * _gelu_grad(uraw)).astype(BF16)
        ds = dsg * ug
        dgate_ref[0] += _rowsum(dx2 * mix_ref[...].astype(F32))
        dwout_ref[...] += _mm_tn(jnp.concatenate([attn_ref[...], sg], axis=-1), dmix)

        nch = tm // CH
        first = _first_group_lanes(nch)
        for c in range(nch):
            dbias_scr[...] += ds[c * CH:(c + 1) * CH, :]
        for j in range(G // 2):
            ls = slice(j * LANE, (j + 1) * LANE)
            dst32 = _chunks_side_by_side(ds, j, nch)
            dst = dst32.astype(BF16)
            vst = _chunks_side_by_side(vn, j, nch)
            dvn_st = jnp.where(first, _mm(wst_ref[2 * j], dst), _mm(wst_ref[2 * j + 1], dst))
            for c in range(nch):
                dvn_scr[c * CH:(c + 1) * CH, ls] = dvn_st[:, c * LANE:(c + 1) * LANE]
            dws_ref[2 * j] += _mm_nt(jnp.where(first, dst32, 0.0).astype(BF16), vst)
            dws_ref[2 * j + 1] += _mm_nt(jnp.where(first, 0.0, dst32).astype(BF16), vst)

        dvn = dvn_scr[...]
        dwv_ref[...] += _rowsum(dvn * y)
        dy = dvn * wv_v
        mean_g = _seg_sum(dy * y, seg[...]) * (1.0 / GD)
        dvg = r64 * (dy - y * _seg_bcast(mean_g, segt[...]))
        dv_ref[...] = (dvg * _gelu_grad(vraw)).astype(BF16)

        @pl.when(t == n_lat - 1)
        def _():
            dbs_ref[...] = _dot_hl(dbias_scr[...], seg[...])

    row = lambda cols: pl.BlockSpec((tm, cols), lambda t: (t, 0))
    r = n_lat * tm
    return _pcall(
        body, name="mix_bwd", grid=(n_lat,),
        out_shape=(_sds((r, HP), BF16), _sds((r, G * GD), BF16), _sds((r, G * GD), BF16), _sds((nrows, 1, D), F32),
                   _sds((wrows, D), F32), _sds((G, CH, CH), F32), _sds((CH, LANE), F32), _sds((1, G * GD), F32)),
        in_specs=[row(D), row(D), row(G * GD), row(G * GD), row(HP), _mod_spec(1, tpe, nrows), _const((1, G * GD)),
                  _const((G, CH, CH)), _const((G, CH, CH)), _const((CH, G * GD)), _const((wrows, D)),
                  _const((G * GD, LANE)), _const((2 * LANE, G * GD))],
        out_specs=(row(HP), row(G * GD), row(G * GD), _mod_spec(1, tpe, nrows), _const((wrows, D)),
                   _const((G, CH, CH)), _const((CH, LANE)), _const((1, G * GD))),
        scratch=[pltpu.VMEM((tm, G * GD), F32), pltpu.VMEM((tm, G * GD), F32), pltpu.VMEM((CH, G * GD), F32)],
        vmem_mb=56)(dx2, mix, u, v, attn, gate, wv, ws, wst, bias, wout, cs["seg_g"], cs["seg_gt"])


def _adamw_math(w, g, m, v):
    m2 = ADAM_B1 * m + (1.0 - ADAM_B1) * g
    v2 = ADAM_B2 * v + (1.0 - ADAM_B2) * (g * g)
    m_hat = m2 / (1.0 - ADAM_B1 ** ADAM_STEP)
    v_hat = v2 / (1.0 - ADAM_B2 ** ADAM_STEP)
    delta = -ADAM_LR * (m_hat / (jnp.sqrt(v_hat) + ADAM_EPS) + ADAM_WD * w)
    return delta, m2, v2


def _row_tile(r, c):
    best = r
    for tr in range(8, r, 8):
        if r % tr == 0 and tr * c * 4 <= MIB:
            best = tr
    return best


def _adamw(w, g, m, v, name):
    r, c = w.shape
    tr = _row_tile(r, c)

    def body(w_ref, g_ref, m_ref, v_ref, d_ref, mo_ref, vo_ref):
        d_ref[...], mo_ref[...], vo_ref[...] = _adamw_math(w_ref[...], g_ref[...], m_ref[...], v_ref[...])

    blk = pl.BlockSpec((tr, c), lambda t: (t, 0))
    return _pcall(body, name=name, grid=(r // tr,), out_shape=(_sds((r, c), F32),) * 3,
                  in_specs=[blk] * 4, out_specs=(blk,) * 3)(w, g, m, v)


def _adamw_small(params):
    n = len(params)

    def body(*refs):
        ins, outs = refs[:4 * n], refs[4 * n:]
        for i in range(n):
            w, g, m, v = (ins[4 * i + k][...] for k in range(4))
            if i == 0:
                sig = _sigmoid(w)
                g = g * (sig * (1.0 + w * (1.0 - sig)))
            d, m2, v2 = _adamw_math(w, g, m, v)
            outs[4 * i][...] = g
            outs[4 * i + 1][...] = d
            outs[4 * i + 2][...] = m2
            outs[4 * i + 3][...] = v2

    flat = [a for p in params for a in p]
    out_shape = tuple(_sds(p[0].shape, F32) for p in params for _ in range(4))
    res = _pcall(body, name="adamw_small", out_shape=out_shape, in_specs=[VMEM] * (4 * n),
                 out_specs=(VMEM,) * (4 * n))(*flat)
    return [res[4 * i:4 * i + 4] for i in range(n)]


def _rope_tables(s):
    rows = jnp.repeat(jnp.arange(s // GRID_W, dtype=F32), GRID_W)
    cols = jnp.tile(jnp.arange(GRID_W, dtype=F32), s // GRID_W)
    half = DR // 2
    inv = ROPE_BASE ** (-jnp.arange(0, half, 2, dtype=F32) / half)
    ang_r = rows[:, None] * inv
    ang_c = cols[:, None] * inv
    ang = jnp.concatenate([ang_r, ang_r, ang_c, ang_c], axis=-1)
    return jnp.cos(ang), jnp.sin(ang)


def _head_pad(a, real):
    return jnp.pad(a, ((0, 0), (0, LANE - real), (0, 0))).reshape(HP, a.shape[2])


def kernel(x, c, ctx, c_ctx, w_ada, b_ada, norm1_w, ffn1_w1, ffn1_w3, ffn1_w2, norm2_w, w_in, q_a_norm_w, w_uq, kv_a_norm_w, w_ukv, q_norm_w, k_norm_w, v_norm_w, w_s, b_s, w_out, norm3_w, ffn2_w1, ffn2_w3, ffn2_w2, loss_target, m_c_ctx, m_w_ada, m_b_ada, m_norm1_w, m_ffn1_w1, m_ffn1_w3, m_ffn1_w2, m_norm2_w, m_w_in, m_q_a_norm_w, m_w_uq, m_kv_a_norm_w, m_w_ukv, m_q_norm_w, m_k_norm_w, m_v_norm_w, m_w_s, m_b_s, m_w_out, m_norm3_w, m_ffn2_w1, m_ffn2_w3, m_ffn2_w2, v_c_ctx, v_w_ada, v_b_ada, v_norm1_w, v_ffn1_w1, v_ffn1_w3, v_ffn1_w2, v_norm2_w, v_w_in, v_q_a_norm_w, v_w_uq, v_kv_a_norm_w, v_w_ukv, v_q_norm_w, v_k_norm_w, v_v_norm_w, v_w_s, v_b_s, v_w_out, v_norm3_w, v_ffn2_w1, v_ffn2_w3, v_ffn2_w2):
    nb, s, _ = x.shape
    nc = ctx.shape[1]
    tm = 256 if nc % 256 == 0 else 128
    tpe = s // tm
    n_lat = nb * tpe
    n_all = n_lat + nb * nc // tm
    tmf = 2 * tm if s % (2 * tm) == 0 and (nb * nc) % (2 * tm) == 0 else tm
    tp = tmf
    r_lat = nb * s
    tpe_p, n_lat_p, n_all_p = s // tp, r_lat // tp, (r_lat + nb * nc) // tp
    me = 4 * lax.axis_index("x") + 2 * lax.axis_index("y") + lax.axis_index("c")
    cs = _consts()
    ncol = w_ada.shape[2]
    fsh = ffn1_w1.shape[2]
    assert nb + 1 <= 8 and NDEV * fsh == FF and NDEV * ncol == NMOD * D and s % nc == 0 and nc % tm == 0

    def t16(a):
        return a.T.astype(BF16)

    wpack1 = jnp.concatenate([t16(ffn1_w1[0]), t16(ffn1_w3[0]), ffn1_w2[0].astype(BF16)], axis=0)
    a_loc = jnp.concatenate([c, c_ctx[None, :], jnp.zeros((7 - nb, D), F32)], axis=0)
    a_raw, _, mod_all, wall1 = _ada_front(a_loc, w_ada[0], lax.dynamic_slice_in_dim(b_ada, me * ncol, ncol, axis=1),
                                          wpack1)
    a_raw = a_raw.reshape(NDEV * 8, D)
    mod_mine = lax.dynamic_slice_in_dim(mod_all, 8 * me, 8, axis=1)
    modtab = mod_mine.transpose(1, 0, 2).reshape(8, NMOD, D)[:nb + 1]
    wpack2 = jnp.concatenate([
        t16(ffn2_w1[0]), t16(ffn2_w3[0]), ffn2_w2[0].astype(BF16),
        t16(w_in[0]), jnp.zeros((12, D), BF16),
        w_out[0].astype(BF16),
        t16(w_uq[0]).reshape(24, D), jnp.zeros((8, D), BF16),
        t16(w_ukv[0]).reshape(16, D)], axis=0)

    def head_w(wn):
        return jnp.tile(jnp.pad(wn, ((0, 0), (0, LANE - DH))), (1, H))

    wq, wk = head_w(q_norm_w), head_w(k_norm_w)
    wv = v_norm_w.reshape(1, G * GD)
    ws16 = w_s[0].astype(BF16)
    wst16 = w_s[0].transpose(0, 2, 1).astype(BF16)
    bias = jnp.repeat(b_s[0].T, GD, axis=1)
    cos, sin = _rope_tables(s)
    cos = jnp.pad(cos, ((0, 0), (DN, LANE - DH)), constant_values=1.0)
    sin = jnp.pad(sin, ((0, 0), (DN, LANE - DH)))
    cos_k = jnp.concatenate([cos, jnp.ones((tm, LANE), F32)], axis=0)
    sin_k = jnp.concatenate([sin, jnp.zeros((tm, LANE), F32)], axis=0)

    xs = (x.reshape(r_lat, D), ctx.reshape(nb * nc, D))
    x1, a1, b1, o1, wall2 = _ffn_fwd(xs, modtab[:, 0:3], norm1_w, wall1, 0, tm=tmf, n_tiles=(r_lat + nb * nc) // tmf,
                                     tpe=s // tmf, n_lat=r_lat // tmf, name="ffn1_fwd", gather=wpack2)

    o0 = 3 * fsh
    wint = wall2[:, o0:o0 + 180].reshape(IN_COLS, D)
    z = lambda n: jnp.zeros((n, D), BF16)
    wint = jnp.concatenate([wint[0:128], wint[160:416], wint[416:928], wint[928:1440],
                            z(DN), wint[128:160], z(LANE - DH)], axis=0)
    wout = wall2[:, o0 + 192:o0 + 320].reshape(D, D)
    wout = jnp.concatenate([_head_pad(wout[:H * DV].reshape(H, DV, D), DV), wout[H * DV:]], axis=0)
    wuq = _head_pad(wall2[:, o0 + 320:o0 + 344].reshape(H, DH, QL), DH)
    wukvt = wall2[:, o0 + 352:o0 + 368].reshape(H, DN + DV, KVL)
    wukv = jnp.concatenate([_head_pad(wukvt[:, :DN], DN), _head_pad(wukvt[:, DN:], DV)], axis=0)

    ckv, qp, u_raw, v_raw, kpe = _proj_fwd(x1, modtab[:, 3:5], norm2_w, wint, tm=tp, n_tiles=n_all_p, tpe=tpe_p)
    q = _q_prep_fwd(qp, q_a_norm_w, wuq, wq, cos, sin, cs, tm=tm, n_lat=n_lat, tpe=tpe)
    k, v = _kv_prep_fwd(ckv, kpe, kv_a_norm_w, wukv, wk, cos_k, sin_k, cs,
                        tm=tm, n_tiles=n_all, tpe=tpe, n_lat=n_lat)
    attn, lse = _attn_fwd(q, k, v, nb=nb, s=s, nc=nc, tq=tm, ck=2048)
    x2, mix = _mix_fwd(u_raw, v_raw, attn, x1, modtab[:nb, 5:6], wv, ws16, bias, wout, cs,
                       tm=tp, n_lat=n_lat_p, tpe=tpe_p)
    dy, a2, b2, o2, lsum = _ffn_fwd((x2,), modtab[:nb, 6:9], norm3_w, wall2, 0, tm=tmf, n_tiles=r_lat // tmf,
                                    tpe=s // tmf, n_lat=r_lat // tmf, name="ffn2_fwd",
                                    target=loss_target.reshape(r_lat, D))
    loss = lax.psum(lsum[0, 0] * (0.5 / D), ("x", "y", "c"))

    tr = 2 * tm if n_lat % 2 == 0 and n_all % 2 == 0 else tm
    dx2, da2, db2, g2, do2, h2, dmod678, dnorm3 = _ffn_bwd_dx(
        dy, (x2,), a2, b2, o2, modtab[:nb, 6:9], norm3_w, wall2, 0,
        tm=tm, n_tiles=n_lat, tpe=tpe, n_lat=n_lat, name="ffn2_bwd_dx")
    g_ffn2 = _ffn_bwd_dw(h2, do2, da2, db2, g2, tr=tr, name="ffn2_bwd_dw")
    part_ffn2 = _add_sibling(g_ffn2, _scatter_sibling([g_ffn2], "scatter_sibling_ffn2")[0], 176, "add_sibling_ffn2")

    dattn, du, dv, dgate5, dwout, dws, dbs, dwv = _mix_bwd(
        dx2, mix, u_raw, v_raw, attn, modtab[:nb, 5:6], wv, ws16, wst16, bias, wout, cs, tm=tp, n_lat=n_lat_p, tpe=tpe_p)
    tq = 2 * tm if s % (2 * tm) == 0 else tm
    dq, dk_l, dk_c, dv_l, dv_c, recv_ffn2 = _attn_bwd(q, k, v, attn, dattn, lse, part_ffn2,
                                                      nb=nb, s=s, nc=nc, tq=tq, ck=1024)
    dqp, dwuq, dqa, dwq = _q_prep_bwd(dq, qp, q_a_norm_w, wuq, wq, cos, sin, cs, tm=tm, n_lat=n_lat, tpe=tpe)
    dckv, dkpe, dwukv, dkva, dwk = _kv_prep_bwd((dk_l, dk_c), (dv_l, dv_c), ckv, kpe, kv_a_norm_w, wukv, wk,
                                                cos_k, sin_k, cs, tm=tm, n_tiles=n_all, tpe=tpe, n_lat=n_lat)
    dx1, dwin, dmod34, dnorm2 = _proj_bwd(dckv, dkpe, dqp, du, dv, dx2, x1, modtab[:, 3:5], norm2_w, wint,
                                          tm=tp, n_tiles=n_all_p, tpe=tpe_p, n_lat=n_lat_p)

    def blocks(a):
        return a.reshape(NDEV, a.shape[0] // NDEV, D)

    dwin_o = jnp.concatenate([dwin[0:128], dwin[KPE_LO:KPE_LO + DR], dwin[128:384], dwin[384:896], dwin[896:1408]],
                             axis=0)
    dwout_o = jnp.concatenate([dwout[:HP].reshape(H, LANE, D)[:, :DV].reshape(H * DV, D), dwout[HP:]], axis=0)
    dwuq_o = dwuq.reshape(H, LANE, QL)[:, :DH]
    dwukv_o = jnp.concatenate([dwukv[:HP].reshape(H, LANE, KVL)[:, :DN], dwukv[HP:].reshape(H, LANE, KVL)[:, :DV]],
                              axis=1)
    gmisc = jnp.concatenate([
        blocks(dwin_o).astype(BF16), jnp.zeros((NDEV, 12, D), BF16),
        blocks(dwout_o).astype(BF16),
        dwuq_o.reshape(NDEV, 24, D).astype(BF16), jnp.zeros((NDEV, 8, D), BF16),
        dwukv_o.reshape(NDEV, 16, D).astype(BF16)], axis=1)
    part_misc = _add_sibling(gmisc, _scatter_sibling([gmisc], "scatter_sibling_misc")[0], 368, "add_sibling_misc")

    dx0, da1, db1, g1, do1, h1, dmod012, dnorm1, recv_misc = _ffn_bwd_dx(
        dx1, xs, a1, b1, o1, modtab[:, 0:3], norm1_w, wall1, 0,
        tm=tm, n_tiles=n_all, tpe=tpe, n_lat=n_lat, name="ffn1_bwd_dx", part=part_misc)
    grad_x = dx0.reshape(nb, s, D)
    g_w1 = _ffn_bwd_dw_one(da1, h1, tr=tr, name="ffn1_bwd_dw1")
    part_w1 = _add_sibling(g_w1, _scatter_sibling([g_w1], "scatter_sibling_w1")[0], 176, "add_sibling_w1")
    g_w3, recv_w1 = _ffn_bwd_dw_one(db1, h1, tr=tr, name="ffn1_bwd_dw3", part=part_w1)
    part_w3 = _add_sibling(g_w3, _scatter_sibling([g_w3], "scatter_sibling_w3")[0], 176, "add_sibling_w3")
    g_w2, recv_w3 = _ffn_bwd_dw_one(g1, do1, tr=tr, name="ffn1_bwd_dw2", part=part_w3)

    zrow = jnp.zeros((1, D), F32)
    g_lat = jnp.concatenate([dmod012[:nb, 0], dmod012[:nb, 1], dmod012[:nb, 2], dmod34[:nb, 0], dmod34[:nb, 1],
                             dgate5[:, 0], dmod678[:, 0], dmod678[:, 1], dmod678[:, 2]], axis=1)
    g_ctx = jnp.concatenate([dmod012[nb:, 0], dmod012[nb:, 1], dmod012[nb:, 2], dmod34[nb:, 0], dmod34[nb:, 1],
                             zrow, zrow, zrow, zrow], axis=1)
    g_loc = jnp.concatenate([g_lat, g_ctx, jnp.zeros((7 - nb, NMOD * D), F32)], axis=0)

    got_w2, g_all = _scatter_sibling([g_w2], "scatter_sibling_w2", gather=g_loc)
    g_all = g_all.reshape(NDEV * 8, NMOD * D)
    g_cols = lax.dynamic_slice_in_dim(g_all, me * ncol, ncol, axis=1)
    g_w_ada, pc_ctx, g_b_ada = _ada_bwd(a_raw, c_ctx.reshape(D, 1), g_all, g_cols, w_ada[0], nb)
    part_w2 = _add_sibling(g_w2, got_w2, 176, "add_sibling_w2")

    def prow(a):
        a = a.reshape(1, -1)
        return jnp.concatenate([a, jnp.zeros((1, D - a.shape[1]), F32)], axis=1)

    g_qn = dwq.reshape(H, LANE)[:, :DH].sum(0)
    g_kn = dwk.reshape(H, LANE)[:, :DH].sum(0)
    spack = jnp.concatenate([
        dnorm1, dnorm2, dnorm3, prow(dqa), prow(dkva), prow(g_qn), prow(g_kn), prow(dwv),
        prow(dbs[:, :G].T), prow(pc_ctx), jnp.zeros((6, D), F32), dws.reshape(CH, D)], axis=0)
    recv_w2, small_all = _scatter_chips([part_w2], "scatter_chips", gather=spack)
    ssum = _sum_slots(small_all, 144, "sum_small")
    gsum2 = _sum_chips(part_ffn2, recv_ffn2, 176, "sum_grads_ffn2")
    msum = _sum_chips(part_misc, recv_misc, 368, "sum_grads_misc")

    transposed = ("ffn1_w1", "ffn1_w3", "ffn2_w1", "ffn2_w3", "w_in", "w_uq")
    g_big = {
        "ffn1_w1": _sum_chips(part_w1, recv_w1, 176, "sum_grads_w1"),
        "ffn1_w3": _sum_chips(part_w3, recv_w3, 176, "sum_grads_w3"),
        "ffn1_w2": _sum_chips(part_w2, recv_w2, 176, "sum_grads_w2"),
        "ffn2_w1": gsum2[0:fsh], "ffn2_w3": gsum2[fsh:2 * fsh], "ffn2_w2": gsum2[2 * fsh:3 * fsh],
        "w_in": msum[0:180], "w_out": msum[192:320],
        "w_uq": msum[320:344].reshape(DH, QL), "w_ukv": msum[352:368].reshape(DN + DV, KVL).T,
        "w_ada": g_w_ada,
    }

    big_in = {
        "w_ada": (w_ada, m_w_ada, v_w_ada), "ffn1_w1": (ffn1_w1, m_ffn1_w1, v_ffn1_w1),
        "ffn1_w3": (ffn1_w3, m_ffn1_w3, v_ffn1_w3), "ffn1_w2": (ffn1_w2, m_ffn1_w2, v_ffn1_w2),
        "w_in": (w_in, m_w_in, v_w_in), "w_uq": (w_uq, m_w_uq, v_w_uq), "w_ukv": (w_ukv, m_w_ukv, v_w_ukv),
        "w_out": (w_out, m_w_out, v_w_out), "ffn2_w1": (ffn2_w1, m_ffn2_w1, v_ffn2_w1),
        "ffn2_w3": (ffn2_w3, m_ffn2_w3, v_ffn2_w3), "ffn2_w2": (ffn2_w2, m_ffn2_w2, v_ffn2_w2),
    }
    res = {}
    for nm, (w, m, v_) in big_in.items():
        g = g_big[nm]
        if nm in transposed:
            d_, m_, v2_ = _adamw(w[0].T, g, m[0].T, v_[0].T, "adamw_" + nm)
            res[nm] = tuple(a.T[None] for a in (g, d_, m_, v2_))
        else:
            d_, m_, v2_ = _adamw(w[0], g, m[0], v_[0], "adamw_" + nm)
            res[nm] = tuple(a[None] for a in (g, d_, m_, v2_))

    small_in = [
        ("c_ctx", c_ctx, m_c_ctx, v_c_ctx, ssum[9:10], (1, D)),
        ("b_ada", b_ada, m_b_ada, v_b_ada, g_b_ada, (1, NMOD * D)),
        ("norm1_w", norm1_w, m_norm1_w, v_norm1_w, ssum[0:1], (1, D)),
        ("norm2_w", norm2_w, m_norm2_w, v_norm2_w, ssum[1:2], (1, D)),
        ("norm3_w", norm3_w, m_norm3_w, v_norm3_w, ssum[2:3], (1, D)),
        ("q_a_norm_w", q_a_norm_w, m_q_a_norm_w, v_q_a_norm_w, ssum[3:4, :QL], (1, QL)),
        ("kv_a_norm_w", kv_a_norm_w, m_kv_a_norm_w, v_kv_a_norm_w, ssum[4:5, :KVL], (1, KVL)),
        ("q_norm_w", q_norm_w, m_q_norm_w, v_q_norm_w, ssum[5:6, :DH], (1, DH)),
        ("k_norm_w", k_norm_w, m_k_norm_w, v_k_norm_w, ssum[6:7, :DH], (1, DH)),
        ("v_norm_w", v_norm_w, m_v_norm_w, v_v_norm_w, ssum[7:8, :G * GD], (G, GD)),
        ("b_s", b_s, m_b_s, v_b_s, ssum[8:9], (G, CH)),
        ("w_s", w_s, m_w_s, v_w_s, ssum[16:144], (G * CH, CH)),
    ]
    small_out = _adamw_small(
        [(w.reshape(sh), g.reshape(sh), m.reshape(sh), v_.reshape(sh)) for _, w, m, v_, g, sh in small_in])
    for (nm, w, *_), outs in zip(small_in, small_out):
        res[nm] = tuple(a.reshape(w.shape) for a in outs)

    order = ["c_ctx", "w_ada", "b_ada", "norm1_w", "ffn1_w1", "ffn1_w3", "ffn1_w2", "norm2_w", "w_in", "q_a_norm_w",
             "w_uq", "kv_a_norm_w", "w_ukv", "q_norm_w", "k_norm_w", "v_norm_w", "w_s", "b_s", "w_out", "norm3_w",
             "ffn2_w1", "ffn2_w3", "ffn2_w2"]
    return (loss, grad_x, *[res[n][0] for n in order], *[res[n][1] for n in order],
            *[res[n][2] for n in order], *[res[n][3] for n in order])
```

```python
import numpy as np
import jax
import jax.numpy as jnp
from jax import lax
from jax.experimental import pallas as pl
from jax.experimental.pallas import tpu as pltpu

F32 = jnp.float32
BF16 = jnp.bfloat16

D = 1024
FF = 2816
FC = 256
H = 8
DN, DR, DV = 64, 32, 64
DH = DN + DR
QL, KVL = 256, 128
G, GD, CH = 8, 64, 128
NMOD = 9
EPS = 1e-6
GRID_W = 64
ROPE_BASE = 10000.0
NDEV = 8
LANE = 128
HP = H * LANE
IN_COLS = 1440
WIN_ROWS = 1536
KPE_LO = 1408 + DN
MIB = 1 << 20

ADAM_LR, ADAM_B1, ADAM_B2, ADAM_EPS, ADAM_WD, ADAM_STEP = 0.001, 0.9, 0.999, 1e-08, 0.01, 10

MESH = pl.DeviceIdType.MESH
ANY = pl.BlockSpec(memory_space=pl.ANY)
VMEM = pl.BlockSpec(memory_space=pltpu.VMEM)


def _mm(a, b):
    return jnp.dot(a, b, preferred_element_type=F32)


def _mm_nt(a, b):
    return lax.dot_general(a, b, (((1,), (1,)), ((), ())), preferred_element_type=F32)


def _mm_tn(a, b):
    return lax.dot_general(a, b, (((0,), (0,)), ((), ())), preferred_element_type=F32)


def _dot_hl(x, m):
    hi = x.astype(BF16)
    lo = (x - hi.astype(F32)).astype(BF16)
    return _mm(hi, m) + _mm(lo, m)


def _sigmoid(a):
    return 1.0 / (1.0 + jnp.exp(-a))


_G0 = 0.7978845608028654
_G1 = 0.044715


def _gelu(x):
    return 0.5 * x * (1.0 + jnp.tanh(_G0 * (x + _G1 * (x * x * x))))


def _gelu_grad(x):
    th = jnp.tanh(_G0 * (x + _G1 * (x * x * x)))
    return 0.5 * (1.0 + th) + 0.5 * x * (1.0 - th * th) * (_G0 * (1.0 + 3.0 * _G1 * x * x))


def _rowsum(y):
    return jnp.sum(y, axis=0, keepdims=True)


def _rms(x):
    return lax.rsqrt(jnp.mean(x * x, axis=-1, keepdims=True) + EPS)


def _pcall(body, *, name, out_shape, in_specs, out_specs, grid=None, scratch=(), vmem_mb=32, aliases=None):
    kw = {}
    if grid is not None:
        kw["grid"] = grid
        sem = ("arbitrary",) * len(grid)
    else:
        sem = None
    if aliases:
        kw["input_output_aliases"] = aliases
    return pl.pallas_call(
        body, name=name, out_shape=out_shape, in_specs=in_specs, out_specs=out_specs,
        scratch_shapes=list(scratch),
        compiler_params=pltpu.CompilerParams(dimension_semantics=sem, vmem_limit_bytes=vmem_mb * MIB),
        **kw)


def _const(shape):
    nd = len(shape)
    return pl.BlockSpec(shape, lambda *_: (0,) * nd)


def _sds(shape, dt):
    return jax.ShapeDtypeStruct(shape, dt)


def _consts():
    seg_h = np.zeros((HP, LANE), np.float32)
    seg_h[np.arange(HP), np.arange(HP) // LANE] = 1.0
    seg_g = np.zeros((G * GD, LANE), np.float32)
    seg_g[np.arange(G * GD), np.arange(G * GD) // GD] = 1.0
    rot = np.zeros((LANE, LANE), np.float32)
    for base in (DN, DN + 16):
        for j in range(8):
            rot[base + j + 8, base + j] = -1.0
            rot[base + j, base + j + 8] = 1.0
    rot2 = np.zeros((2 * LANE, 2 * LANE), np.float32)
    rot2[:LANE, :LANE] = rot
    rot2[LANE:, LANE:] = rot
    twice = lambda m: np.concatenate([m, m], axis=0)
    c = dict(seg_h=seg_h, seg_ht=twice(seg_h.T), seg_g=seg_g, seg_gt=twice(seg_g.T), rot=rot2, rot_t=rot2.T)
    return {k: jnp.asarray(v, BF16) for k, v in c.items()}


_GATHER_SEMS = [pltpu.SemaphoreType.DMA((7,)), pltpu.SemaphoreType.DMA((7,)), pltpu.SemaphoreType.DMA(())]


def _gather_phases(x_ref, out_ref, send_sems, recv_sems, local_sem):
    mx, my, mc = lax.axis_index("x"), lax.axis_index("y"), lax.axis_index("c")
    me, sibling = (mx, my, mc), (mx, my, 1 - mc)
    chips = [(1 - mx, my), (mx, 1 - my), (1 - mx, 1 - my)]

    def blk(px, py, pc):
        return out_ref.at[4 * px + 2 * py + pc]

    def copy(k, block, to, src=None):
        return pltpu.make_async_remote_copy(
            src_ref=blk(*block) if src is None else src, dst_ref=blk(*block),
            send_sem=send_sems.at[k], recv_sem=recv_sems.at[k], device_id=to, device_id_type=MESH)

    mine = pltpu.make_async_copy(x_ref, blk(*me), local_sem)
    first = [copy(0, me, sibling, src=x_ref)]
    first += [copy(1 + j, me, (*chip, mc), src=x_ref) for j, chip in enumerate(chips)]
    passed = [copy(4 + j, (*chip, mc), sibling) for j, chip in enumerate(chips)]

    def start():
        mine.start()
        for cp in first:
            cp.start()

    def forward():
        for j, chip in enumerate(chips):
            copy(1 + j, (*chip, mc), me).wait_recv()
            passed[j].start()

    def finish():
        copy(0, sibling, me).wait_recv()
        for j, chip in enumerate(chips):
            copy(4 + j, (*chip, 1 - mc), me).wait_recv()
        for cp in first + passed:
            cp.wait_send()
        mine.wait()

    return start, forward, finish


def _chip_sends(p_ref, out_ref, send_sems, recv_sems):
    mx, my, mc = lax.axis_index("x"), lax.axis_index("y"), lax.axis_index("c")
    peers = [(1 - mx, my), (mx, 1 - my), (1 - mx, 1 - my)]
    return [pltpu.make_async_remote_copy(
        src_ref=p_ref.at[2 * px + py], dst_ref=out_ref.at[j], send_sem=send_sems.at[j], recv_sem=recv_sems.at[j],
        device_id=(px, py, mc), device_id_type=MESH) for j, (px, py) in enumerate(peers)]


def _with_gather(copies_of, n, shapes, sems, gather, name, args):
    ns = len(sems)

    def body(*refs):
        ng = 1 if gather is not None else 0
        ins, outs = refs[:n], refs[n + ng:2 * n + ng]
        copies = copies_of(ins, outs, refs[2 * n + 2 * ng:2 * n + 2 * ng + ns])
        if ng:
            start, forward, finish = _gather_phases(refs[n], refs[2 * n + 1], *refs[2 * n + 2 + ns:])
            start()
        for cp in copies:
            cp.start()
        if ng:
            forward()
        for cp in copies:
            cp.wait_recv()
        for cp in copies:
            cp.wait_send()
        if ng:
            finish()

    in_specs, out_shape, scratch = [ANY] * n, list(shapes), list(sems)
    if gather is not None:
        in_specs.append(ANY)
        args = list(args) + [gather]
        out_shape.append(_sds((NDEV,) + gather.shape, gather.dtype))
        scratch += _GATHER_SEMS
    return pl.pallas_call(body, name=name, out_shape=tuple(out_shape), in_specs=in_specs,
                          out_specs=(ANY,) * len(out_shape), scratch_shapes=scratch)(*args)


def _scatter_sibling(xs, name, gather=None):
    n = len(xs)

    def copies_of(x_refs, got_refs, sems):
        send_sems, recv_sems = sems
        mx, my, mc = lax.axis_index("x"), lax.axis_index("y"), lax.axis_index("c")
        return [pltpu.make_async_remote_copy(
            src_ref=x_refs[i].at[2 * j + 1 - mc], dst_ref=got_refs[i].at[j],
            send_sem=send_sems.at[4 * i + j], recv_sem=recv_sems.at[4 * i + j],
            device_id=(mx, my, 1 - mc), device_id_type=MESH) for i in range(n) for j in range(4)]

    shapes = tuple(_sds((4,) + x.shape[1:], x.dtype) for x in xs)
    return _with_gather(copies_of, n, shapes, [pltpu.SemaphoreType.DMA((4 * n,))] * 2, gather, name, xs)


def _scatter_chips(ps, name, gather=None):
    n = len(ps)

    def copies_of(p_refs, out_refs, sems):
        sends = []
        for i in range(n):
            sends += _chip_sends(p_refs[i], out_refs[i], sems[2 * i], sems[2 * i + 1])
        return sends

    shapes = tuple(_sds((3,) + p.shape[1:], p.dtype) for p in ps)
    return _with_gather(copies_of, n, shapes, [pltpu.SemaphoreType.DMA((3,))] * (2 * n), gather, name, ps)


def _add_sibling(x, got, tr, name):
    _, r, c = x.shape

    def body(x_ref, g_ref, o_ref):
        mc = lax.axis_index("c")
        for j in range(4):
            mine = jnp.where(mc == 0, x_ref[2 * j].astype(F32), x_ref[2 * j + 1].astype(F32))
            o_ref[j] = (mine + g_ref[j].astype(F32)).astype(o_ref.dtype)

    return _pcall(body, name=name, grid=(r // tr,), out_shape=_sds(got.shape, got.dtype),
                  in_specs=[pl.BlockSpec((NDEV, tr, c), lambda t: (0, t, 0)), pl.BlockSpec((4, tr, c), lambda t: (0, t, 0))],
                  out_specs=pl.BlockSpec((4, tr, c), lambda t: (0, t, 0)))(x, got)


def _sum_chips(part, recv, tr, name):
    _, r, c = part.shape

    def body(p_ref, r_ref, o_ref):
        slot = 2 * lax.axis_index("x") + lax.axis_index("y")
        acc = p_ref[0].astype(F32)
        for j in range(1, 4):
            acc = jnp.where(slot == j, p_ref[j].astype(F32), acc)
        for j in range(3):
            acc = acc + r_ref[j].astype(F32)
        o_ref[...] = acc

    return _pcall(body, name=name, grid=(r // tr,), out_shape=_sds((r, c), F32),
                  in_specs=[pl.BlockSpec((4, tr, c), lambda t: (0, t, 0)), pl.BlockSpec((3, tr, c), lambda t: (0, t, 0))],
                  out_specs=pl.BlockSpec((tr, c), lambda t: (t, 0)))(part, recv)


def _sum_slots(x, tr, name):
    n, r, c = x.shape

    def body(x_ref, o_ref):
        acc = x_ref[0].astype(F32)
        for s in range(1, n):
            acc = acc + x_ref[s].astype(F32)
        o_ref[...] = acc

    return _pcall(body, name=name, grid=(r // tr,), out_shape=_sds((r, c), F32),
                  in_specs=[pl.BlockSpec((n, tr, c), lambda t: (0, t, 0))],
                  out_specs=pl.BlockSpec((tr, c), lambda t: (t, 0)))(x)


def _ada_front(a_loc, w_loc, b_loc, wpack):
    ncol = w_loc.shape[1]
    nrow = NDEV * a_loc.shape[0]

    def body(a_ref, w_ref, b_ref, wp_ref, araw_ref, mloc_ref, mall_ref, wall_ref,
             a_vm, w_vm, m_vm, lsem, *sems):
        a_start, a_forward, a_finish = _gather_phases(a_ref, araw_ref, *sems[0:3])
        m_start, m_forward, m_finish = _gather_phases(mloc_ref, mall_ref, *sems[3:6])
        w_start, w_forward, w_finish = _gather_phases(wp_ref, wall_ref, *sems[6:9])
        w_in = pltpu.make_async_copy(w_ref, w_vm, lsem.at[0])
        w_in.start()
        a_start()
        w_start()
        a_forward()
        a_finish()
        a_in = pltpu.make_async_copy(araw_ref, a_vm, lsem.at[1])
        a_in.start()
        a_in.wait()
        w_in.wait()
        a = a_vm[...].reshape(nrow, D)
        act = (a * _sigmoid(a)).astype(BF16)
        m_vm[...] = _mm(act, w_vm[...].astype(BF16)) + b_ref[...]
        m_out = pltpu.make_async_copy(m_vm, mloc_ref, lsem.at[2])
        m_out.start()
        m_out.wait()
        m_start()
        m_forward()
        m_finish()
        w_forward()
        w_finish()

    return pl.pallas_call(
        body, name="ada_front",
        out_shape=(_sds((NDEV,) + a_loc.shape, F32), _sds((nrow, ncol), F32), _sds((NDEV, nrow, ncol), F32),
                   _sds((NDEV,) + wpack.shape, wpack.dtype)),
        in_specs=[ANY, ANY, VMEM, ANY], out_specs=(ANY, ANY, ANY, ANY),
        scratch_shapes=[pltpu.VMEM((NDEV,) + a_loc.shape, F32), pltpu.VMEM(w_loc.shape, F32),
                        pltpu.VMEM((nrow, ncol), F32), pltpu.SemaphoreType.DMA((3,))] + _GATHER_SEMS * 3,
        compiler_params=pltpu.CompilerParams(vmem_limit_bytes=32 * MIB),
    )(a_loc, w_loc, b_loc, wpack)


def _ada_bwd(a_raw, cctx_col, g_all, g_cols, w_loc, nb):
    nrow = a_raw.shape[0]
    ncol = w_loc.shape[1]

    def body(a_ref, cc_ref, gall_ref, g_ref, w_ref, dw_ref, pc_ref, gb_ref):
        a = a_ref[...]
        rowid = lax.broadcasted_iota(jnp.int32, (nrow, 1), 0) % 8
        act = jnp.where(rowid < nb, a * _sigmoid(a), 0.0).astype(BF16)
        g = g_ref[...]
        gc = _rowsum(jnp.where(rowid == nb, g, 0.0))
        cc = cc_ref[...]
        dw_ref[...] = _mm_tn(act, g.astype(BF16)) + (cc * _sigmoid(cc)) * gc
        pc_ref[...] = jnp.sum(w_ref[...] * gc, axis=1, keepdims=True)
        gb_ref[...] = _rowsum(gall_ref[...])

    return _pcall(body, name="ada_bwd",
                  out_shape=(_sds((D, ncol), F32), _sds((D, 1), F32), _sds((1, g_all.shape[1]), F32)),
                  in_specs=[VMEM] * 5, out_specs=(VMEM,) * 3, vmem_mb=48)(a_raw, cctx_col, g_all, g_cols, w_loc)


def _mod_spec(k, tpe, nrows):
    return pl.BlockSpec((1, k, D), lambda t: (jnp.minimum(t // tpe, nrows - 1), 0, 0))


def _load_ffn_weights(wall_ref, first, bufs, sems):
    fsh = FF // NDEV
    cps = []
    for j, buf in enumerate(bufs):
        for d in range(NDEV):
            cps.append(pltpu.make_async_copy(wall_ref.at[d, pl.ds((first + j) * fsh, fsh)],
                                             buf.at[pl.ds(d * fsh, fsh)], sems.at[j * NDEV + d]))
    for cp in cps:
        cp.start()
    for cp in cps:
        cp.wait()


def _token_specs(xs, tm, n_lat):
    specs = [pl.BlockSpec((tm, D), lambda t: (jnp.minimum(t, n_lat - 1), 0))]
    if len(xs) == 2:
        specs.append(pl.BlockSpec((tm, D), lambda t: (jnp.maximum(t - n_lat, 0), 0)))
    return specs


def _ffn_fwd(xs, mod3, norm_w, wall, first, *, tm, n_tiles, tpe, n_lat, name, target=None, gather=None):
    nrows = mod3.shape[0]
    r = n_tiles * tm
    nx = len(xs)
    with_loss = target is not None
    with_gather = gather is not None
    fwd_step = max(2 * n_tiles // 3, 1)

    def body(*refs):
        x_refs = refs[:nx]
        pos = nx
        if with_loss:
            tgt_ref = refs[pos]
            pos += 1
        mod_ref, nw_ref, wall_ref = refs[pos:pos + 3]
        pos += 3
        if with_gather:
            gin_ref = refs[pos]
            pos += 1
        xo_ref, a_ref, b_ref, o_ref = refs[pos:pos + 4]
        pos += 4
        if with_loss:
            ls_ref = refs[pos]
            pos += 1
        if with_gather:
            gout_ref = refs[pos]
            pos += 1
        w1_ref, w3_ref, w2_ref, wsem, acc_ref = refs[pos:pos + 5]
        t = pl.program_id(0)
        if with_gather:
            g_start, g_forward, g_finish = _gather_phases(gin_ref, gout_ref, *refs[pos + 5:])

        @pl.when(t == 0)
        def _():
            if with_gather:
                g_start()
            _load_ffn_weights(wall_ref, first, (w1_ref, w3_ref, w2_ref), wsem)
            if with_loss:
                ls_ref[...] = jnp.zeros_like(ls_ref)

        if with_gather:
            @pl.when(t == fwd_step)
            def _():
                g_forward()

            @pl.when(t == n_tiles - 1)
            def _():
                g_finish()

        x = x_refs[0][...]
        if nx == 2:
            x = jnp.where(t < n_lat, x, x_refs[1][...])
        n = x * _rms(x) * nw_ref[...]
        shift, scale, gate = mod_ref[0, 0:1, :], mod_ref[0, 1:2, :], mod_ref[0, 2:3, :]
        h = (n * (1.0 + scale) + shift).astype(BF16)
        nch = FF // FC
        o = None
        for lo_c, hi_c in ((0, nch // 2), (nch // 2, nch)):
            for j in range(lo_c, hi_c):
                sl = slice(j * FC, (j + 1) * FC)
                a = _mm_nt(h, w1_ref[sl, :])
                b = _mm_nt(h, w3_ref[sl, :])
                a_ref[:, sl] = a.astype(BF16)
                b_ref[:, sl] = b.astype(BF16)
                acc_ref[:, sl] = (a * _sigmoid(a) * b).astype(BF16)
            gs = slice(lo_c * FC, hi_c * FC)
            part = _mm(acc_ref[:, gs], w2_ref[gs, :])
            o = part if o is None else o + part
        o_ref[...] = o.astype(BF16)
        out = x + (0.5 * gate) * o
        if with_loss:
            d = out - tgt_ref[...]
            xo_ref[...] = d * (1.0 / D)
            ls_ref[...] += jnp.sum(d * d)
        else:
            xo_ref[...] = out

    row = lambda cols: pl.BlockSpec((tm, cols), lambda t: (t, 0))
    in_specs = _token_specs(xs, tm, n_lat) + ([row(D)] if with_loss else []) + [
        _mod_spec(3, tpe, nrows), _const((1, D)), ANY]
    out_shape = [_sds((r, D), F32), _sds((r, FF), BF16), _sds((r, FF), BF16), _sds((r, D), BF16)]
    out_specs = [row(D), row(FF), row(FF), row(D)]
    scratch = [pltpu.VMEM((FF, D), BF16)] * 3 + [pltpu.SemaphoreType.DMA((3 * NDEV,)), pltpu.VMEM((tm, FF), BF16)]
    if with_loss:
        out_shape.append(_sds((8, LANE), F32))
        out_specs.append(_const((8, LANE)))
    args = list(xs) + ([target] if with_loss else []) + [mod3, norm_w, wall]
    if with_gather:
        assert n_tiles >= 2
        in_specs.append(ANY)
        args.append(gather)
        out_shape.append(_sds((NDEV,) + gather.shape, gather.dtype))
        out_specs.append(ANY)
        scratch += _GATHER_SEMS
    return _pcall(
        body, name=name, grid=(n_tiles,), out_shape=tuple(out_shape), in_specs=in_specs, out_specs=tuple(out_specs),
        scratch=scratch, vmem_mb=56)(*args)


def _ffn_bwd_dx(dout, xs, a, b, o, mod3, norm_w, wall, first, *, tm, n_tiles, tpe, n_lat, name):
    nrows = mod3.shape[0]
    r = n_tiles * tm
    nx = len(xs)

    def body(*refs):
        dout_ref = refs[0]
        x_refs = refs[1:1 + nx]
        (a_ref, b_ref, o_ref, mod_ref, nw_ref, wall_ref,
         dx_ref, da_ref, db_ref, g_ref, do_ref, h_ref, dmod_ref, dnw_ref,
         w1_ref, w3_ref, w2_ref, wsem) = refs[1 + nx:]
        t = pl.program_id(0)

        @pl.when(t == 0)
        def _():
            _load_ffn_weights(wall_ref, first, (w1_ref, w3_ref, w2_ref), wsem)
            dnw_ref[...] = jnp.zeros_like(dnw_ref)

        @pl.when(jnp.where(t < n_lat, t % tpe == 0, t == n_lat))
        def _():
            dmod_ref[...] = jnp.zeros_like(dmod_ref)

        x = x_refs[0][...]
        if nx == 2:
            x = jnp.where(t < n_lat, x, x_refs[1][...])
        dout = dout_ref[...]
        shift, scale, gate = mod_ref[0, 0:1, :], mod_ref[0, 1:2, :], mod_ref[0, 2:3, :]
        d_o = ((0.5 * gate) * dout).astype(BF16)
        do_ref[...] = d_o
        nch = FF // FC
        groups = ((0, nch // 2), (nch // 2, nch))
        dh = None
        for lo_c, hi_c in groups:
            for j in range(lo_c, hi_c):
                sl = slice(j * FC, (j + 1) * FC)
                av = a_ref[:, sl].astype(F32)
                bv = b_ref[:, sl].astype(F32)
                dg = _mm_nt(d_o, w2_ref[sl, :])
                sig = _sigmoid(av)
                sa = av * sig
                g_ref[:, sl] = (sa * bv).astype(BF16)
                da_ref[:, sl] = (dg * bv * (sig * (1.0 + av * (1.0 - sig)))).astype(BF16)
                db_ref[:, sl] = (dg * sa).astype(BF16)
            gs = slice(lo_c * FC, hi_c * FC)
            part = _mm(da_ref[:, gs], w1_ref[gs, :]) + _mm(db_ref[:, gs], w3_ref[gs, :])
            dh = part if dh is None else dh + part
        rr = _rms(x)
        xh = x * rr
        nw = nw_ref[...]
        n = xh * nw
        h_ref[...] = (n * (1.0 + scale) + shift).astype(BF16)
        dgate = _rowsum(0.5 * o_ref[...].astype(F32) * dout)
        dn = dh * (1.0 + scale)
        dxh = dn * nw
        dmod_ref[0, 0:1, :] += _rowsum(dh)
        dmod_ref[0, 1:2, :] += _rowsum(dh * n)
        dmod_ref[0, 2:3, :] += dgate
        dnw_ref[...] += _rowsum(dn * xh)
        dx = dout + rr * (dxh - xh * jnp.mean(dxh * xh, axis=-1, keepdims=True))
        if n_tiles == n_lat:
            dx_ref[...] = dx
        else:
            @pl.when(t < n_lat)
            def _():
                dx_ref[...] = dx

    row = lambda cols: pl.BlockSpec((tm, cols), lambda t: (t, 0))
    lat = pl.BlockSpec((tm, D), lambda t: (jnp.minimum(t, n_lat - 1), 0))
    out_shape = [_sds((n_lat * tm, D), F32), _sds((r, FF), BF16), _sds((r, FF), BF16), _sds((r, FF), BF16),
                 _sds((r, D), BF16), _sds((r, D), BF16), _sds((nrows, 3, D), F32), _sds((1, D), F32)]
    in_specs = [row(D)] + _token_specs(xs, tm, n_lat) + [row(FF), row(FF), row(D), _mod_spec(3, tpe, nrows),
                                                          _const((1, D)), ANY]
    out_specs = [lat, row(FF), row(FF), row(FF), row(D), row(D), _mod_spec(3, tpe, nrows), _const((1, D))]
    scratch = [pltpu.VMEM((FF, D), BF16)] * 3 + [pltpu.SemaphoreType.DMA((3 * NDEV,))]
    args = [dout, *xs, a, b, o, mod3, norm_w, wall]
    return _pcall(body, name=name, grid=(n_tiles,), out_shape=tuple(out_shape), in_specs=in_specs,
                  out_specs=tuple(out_specs), scratch=scratch, vmem_mb=60)(*args)


def _ffn_bwd_dw(h, d_o, da, db, g, *, tr, name):
    r = h.shape[0]
    fh = FF // 2
    fsh = FF // NDEV
    nk = r // tr

    def body(h_ref, do_ref, da_ref, db_ref, g_ref, out_ref, acc1, acc3, acc2):
        k = pl.program_id(1)

        @pl.when(k == 0)
        def _():
            acc1[...] = jnp.zeros_like(acc1)
            acc3[...] = jnp.zeros_like(acc3)
            acc2[...] = jnp.zeros_like(acc2)

        hv = h_ref[...]
        acc1[...] += _mm_tn(da_ref[...], hv)
        acc3[...] += _mm_tn(db_ref[...], hv)
        acc2[...] += _mm_tn(g_ref[...], do_ref[...])

        @pl.when(k == nk - 1)
        def _():
            for i, acc in enumerate((acc1, acc3, acc2)):
                out_ref[:, i * fsh:(i + 1) * fsh, :] = acc[...].reshape(NDEV // 2, fsh, D).astype(BF16)

    rowd = pl.BlockSpec((tr, D), lambda f, k: (k, 0))
    rowf = pl.BlockSpec((tr, fh), lambda f, k: (k, f))
    return _pcall(
        body, name=name, grid=(2, nk), out_shape=_sds((NDEV, 3 * fsh, D), BF16),
        in_specs=[rowd, rowd, rowf, rowf, rowf],
        out_specs=pl.BlockSpec((NDEV // 2, 3 * fsh, D), lambda f, k: (f, 0, 0)),
        scratch=[pltpu.VMEM((fh, D), F32)] * 3, vmem_mb=56)(h, d_o, da, db, g)


def _direct_sends(x_ref, out_ref, send_sems, recv_sems):
    mx, my, mc = lax.axis_index("x"), lax.axis_index("y"), lax.axis_index("c")
    sends = []
    for k in range(1, NDEV):
        px = 1 - mx if (k & 4) else mx
        py = 1 - my if (k & 2) else my
        pc = 1 - mc if (k & 1) else mc
        sends.append(pltpu.make_async_remote_copy(
            src_ref=x_ref.at[4 * px + 2 * py + pc], dst_ref=out_ref.at[k - 1],
            send_sem=send_sems.at[k - 1], recv_sem=recv_sems.at[k - 1], device_id=(px, py, pc), device_id_type=MESH))
    return sends


def _sum_direct(x, recv, tr, name):
    _, r, c = x.shape

    def body(x_ref, r_ref, o_ref):
        me = 4 * lax.axis_index("x") + 2 * lax.axis_index("y") + lax.axis_index("c")
        acc = x_ref[0].astype(F32)
        for j in range(1, NDEV):
            acc = jnp.where(me == j, x_ref[j].astype(F32), acc)
        for j in range(NDEV - 1):
            acc = acc + r_ref[j].astype(F32)
        o_ref[...] = acc

    return _pcall(body, name=name, grid=(r // tr,), out_shape=_sds((r, c), F32),
                  in_specs=[pl.BlockSpec((NDEV, tr, c), lambda t: (0, t, 0)),
                            pl.BlockSpec((NDEV - 1, tr, c), lambda t: (0, t, 0))],
                  out_specs=pl.BlockSpec((tr, c), lambda t: (t, 0)))(x, recv)


def _chip_exchange_behind(part_ref, recv_ref, send_sems, recv_sems, first, last, direct=False):
    sends = (_direct_sends if direct else _chip_sends)(part_ref, recv_ref, send_sems, recv_sems)

    @pl.when(first)
    def _():
        for cp in sends:
            cp.start()

    @pl.when(last)
    def _():
        for cp in sends:
            cp.wait_recv()
        for cp in sends:
            cp.wait_send()


def _ffn_bwd_dw_one(lhs, rhs, *, tr, name, part=None):
    r = lhs.shape[0]
    fsh = FF // NDEV
    nk = r // tr
    fused = part is not None
    nslot = NDEV - 1

    def body(*refs):
        if fused:
            lhs_ref, rhs_ref, part_ref, out_ref, recv_ref, acc, send_sems, recv_sems = refs
        else:
            lhs_ref, rhs_ref, out_ref, acc = refs
        k = pl.program_id(0)
        if fused:
            _chip_exchange_behind(part_ref, recv_ref, send_sems, recv_sems, k == 0, k == nk - 1, direct=True)

        @pl.when(k == 0)
        def _():
            acc[...] = jnp.zeros_like(acc)

        acc[...] += _mm_tn(lhs_ref[...], rhs_ref[...])

        @pl.when(k == nk - 1)
        def _():
            out_ref[...] = acc[...].reshape(NDEV, fsh, D).astype(BF16)

    in_specs = [pl.BlockSpec((tr, FF), lambda k: (k, 0)), pl.BlockSpec((tr, D), lambda k: (k, 0))]
    out_shape = [_sds((NDEV, fsh, D), BF16)]
    out_specs = [_const((NDEV, fsh, D))]
    scratch = [pltpu.VMEM((FF, D), F32)]
    args = [lhs, rhs]
    if fused:
        in_specs.append(ANY)
        args.append(part)
        out_shape.append(_sds((nslot,) + part.shape[1:], part.dtype))
        out_specs.append(ANY)
        scratch += [pltpu.SemaphoreType.DMA((nslot,))] * 2
    res = _pcall(body, name=name, grid=(nk,), out_shape=tuple(out_shape), in_specs=in_specs,
                 out_specs=tuple(out_specs), scratch=scratch, vmem_mb=48)(*args)
    return res if fused else res[0]


_PIECES =((0, 128), (128, 384), (384, 896), (896, 1408), (1408, 1536))


def _proj_fwd(x1, mod2, norm_w, wint, *, tm, n_tiles, tpe, name="proj_fwd"):
    nrows = mod2.shape[0]
    r = n_tiles * tm

    def body(x_ref, mod_ref, nw_ref, w_ref, ckv_ref, q_ref, u_ref, v_ref, kpe_ref):
        x = x_ref[...]
        n = x * _rms(x) * nw_ref[...]
        h = (n * (1.0 + mod_ref[0, 1:2, :]) + mod_ref[0, 0:1, :]).astype(BF16)
        for (lo, hi), ref in zip(_PIECES, (ckv_ref, q_ref, u_ref, v_ref, kpe_ref)):
            ref[...] = _mm_nt(h, w_ref[lo:hi, :])

    row = lambda cols: pl.BlockSpec((tm, cols), lambda t: (t, 0))
    widths = [hi - lo for lo, hi in _PIECES]
    return _pcall(
        body, name=name, grid=(n_tiles,),
        out_shape=tuple(_sds((r, w), F32) for w in widths),
        in_specs=[row(D), _mod_spec(2, tpe, nrows), _const((1, D)), _const((WIN_ROWS, D))],
        out_specs=tuple(row(w) for w in widths), vmem_mb=40)(x1, mod2, norm_w, wint)


def _proj_bwd(dckv, dkpe, dq, du, dv, dx2, x1, mod2, norm_w, wint, *, tm, n_tiles, tpe, n_lat, name="proj_bwd"):
    nrows = mod2.shape[0]
    r = n_tiles * tm

    def body(dckv_ref, dkpe_ref, dq_ref, du_ref, dv_ref, dx2_ref, x_ref, mod_ref, nw_ref, w_ref,
             dx_ref, dw_ref, dmod_ref, dnw_ref):
        t = pl.program_id(0)
        is_lat = t < n_lat

        @pl.when(t == 0)
        def _():
            dw_ref[...] = jnp.zeros_like(dw_ref)
            dnw_ref[...] = jnp.zeros_like(dnw_ref)

        @pl.when(jnp.where(is_lat, t % tpe == 0, t == n_lat))
        def _():
            dmod_ref[...] = jnp.zeros_like(dmod_ref)

        x = x_ref[...]
        rr = _rms(x)
        xh = x * rr
        nw = nw_ref[...]
        n = xh * nw
        scale = mod_ref[0, 1:2, :]
        h = (n * (1.0 + scale) + mod_ref[0, 0:1, :]).astype(BF16)
        zero = jnp.zeros((), BF16)
        pieces = (dckv_ref[...], jnp.where(is_lat, dq_ref[...], zero), jnp.where(is_lat, du_ref[...], zero),
                  jnp.where(is_lat, dv_ref[...], zero), dkpe_ref[...])
        dh = None
        for (lo, hi), piece in zip(_PIECES, pieces):
            part = _mm(piece, w_ref[lo:hi, :])
            dh = part if dh is None else dh + part
        dn = dh * (1.0 + scale)
        dxh = dn * nw
        dx = rr * (dxh - xh * jnp.mean(dxh * xh, axis=-1, keepdims=True))
        dx_ref[...] = dx + jnp.where(is_lat, dx2_ref[...], 0.0)
        dmod_ref[0, 0:1, :] += _rowsum(dh)
        dmod_ref[0, 1:2, :] += _rowsum(dh * n)
        dnw_ref[...] += _rowsum(dn * xh)
        for (lo, hi), piece in zip(_PIECES, pieces):
            dw_ref[lo:hi, :] += _mm_tn(piece, h)

    row = lambda cols: pl.BlockSpec((tm, cols), lambda t: (t, 0))
    lat = lambda cols: pl.BlockSpec((tm, cols), lambda t: (jnp.minimum(t, n_lat - 1), 0))
    return _pcall(
        body, name=name, grid=(n_tiles,),
        out_shape=(_sds((r, D), F32), _sds((WIN_ROWS, D), F32), _sds((nrows, 2, D), F32), _sds((1, D), F32)),
        in_specs=[row(128), row(128), lat(256), lat(512), lat(512), lat(D), row(D), _mod_spec(2, tpe, nrows),
                  _const((1, D)), _const((WIN_ROWS, D))],
        out_specs=(row(D), _const((WIN_ROWS, D)), _mod_spec(2, tpe, nrows), _const((1, D))),
        vmem_mb=48)(dckv, dkpe, dq, du, dv, dx2, x1, mod2, norm_w, wint)


def _seg_sum(x, seg):
    return _mm(x.astype(BF16), seg)


def _seg_bcast(v, segt2):
    hi = v.astype(BF16)
    lo = (v - hi.astype(F32)).astype(BF16)
    return _mm(jnp.concatenate([hi, lo], axis=-1), segt2)


def _rope_pairs(t, cos, sin, rot2):
    cos2, sin2 = jnp.concatenate([cos, cos], axis=-1), jnp.concatenate([sin, sin], axis=-1)
    out = []
    for j in range(H // 2):
        tj = t[:, 2 * j * LANE:2 * (j + 1) * LANE]
        out.append(tj * cos2 + _dot_hl(tj, rot2) * sin2)
    return jnp.concatenate(out, axis=-1)


def _head_norm_rope(x, w_pad, cos, sin, seg, segt2, rot2, rope=True):
    rh = lax.rsqrt(_seg_sum(x * x, seg) * (1.0 / DH) + EPS)
    rb = _seg_bcast(rh, segt2)
    y = x * rb
    out = _rope_pairs(y * w_pad, cos, sin, rot2) if rope else None
    return out, y, rb


def _head_norm_rope_bwd(dout, y, rb, w_pad, cos, sin, seg, segt2, rot2_t):
    cos2, sin2 = jnp.concatenate([cos, cos], axis=-1), jnp.concatenate([sin, sin], axis=-1)
    dt = []
    for j in range(H // 2):
        dj = dout[:, 2 * j * LANE:2 * (j + 1) * LANE]
        dt.append(dj * cos2 + _dot_hl(dj * sin2, rot2_t))
    dt = jnp.concatenate(dt, axis=-1)
    dw = _rowsum(dt * y)
    dy = dt * w_pad
    mean_h = _seg_sum(dy * y, seg) * (1.0 / DH)
    return rb * (dy - y * _seg_bcast(mean_h, segt2)), dw


def _q_prep_fwd(qp, qa_w, wuq, wq, cos, sin, cs, *, tm, n_lat, tpe):
    def body(qp_ref, qa_ref, wuq_ref, wq_ref, cos_ref, sin_ref, seg, segt, rot, q_ref):
        x = qp_ref[...]
        cq = (x * _rms(x) * qa_ref[...]).astype(BF16)
        q, _, _ = _head_norm_rope(_mm_nt(cq, wuq_ref[...]), wq_ref[...], cos_ref[...], sin_ref[...],
                                  seg[...], segt[...], rot[...])
        q_ref[...] = q.astype(BF16)

    row = lambda cols: pl.BlockSpec((tm, cols), lambda t: (t, 0))
    tab = pl.BlockSpec((tm, LANE), lambda t: (t % tpe, 0))
    return _pcall(
        body, name="q_prep_fwd", grid=(n_lat,), out_shape=_sds((n_lat * tm, HP), BF16),
        in_specs=[row(QL), _const((1, QL)), _const((HP, QL)), _const((1, HP)), tab, tab,
                  _const((HP, LANE)), _const((2 * LANE, HP)), _const((2 * LANE, 2 * LANE))],
        out_specs=row(HP))(qp, qa_w, wuq, wq, cos, sin, cs["seg_h"], cs["seg_ht"], cs["rot"])


def _q_prep_bwd(dq, qp, qa_w, wuq, wq, cos, sin, cs, *, tm, n_lat, tpe):
    def body(dq_ref, qp_ref, qa_ref, wuq_ref, wq_ref, cos_ref, sin_ref, seg, segt, rot, rot_t,
             dqp_ref, dwuq_ref, dqa_ref, dwq_ref):
        @pl.when(pl.program_id(0) == 0)
        def _():
            dwuq_ref[...] = jnp.zeros_like(dwuq_ref)
            dqa_ref[...] = jnp.zeros_like(dqa_ref)
            dwq_ref[...] = jnp.zeros_like(dwq_ref)

        x = qp_ref[...]
        ra = _rms(x)
        xh = x * ra
        qa = qa_ref[...]
        cq = (xh * qa).astype(BF16)
        wuq_v = wuq_ref[...]
        wq_v, cos_v, sin_v = wq_ref[...], cos_ref[...], sin_ref[...]
        _, y, rb = _head_norm_rope(_mm_nt(cq, wuq_v), wq_v, cos_v, sin_v, seg[...], segt[...], rot[...], rope=False)
        dqraw, dwq = _head_norm_rope_bwd(dq_ref[...], y, rb, wq_v, cos_v, sin_v, seg[...], segt[...], rot_t[...])
        dqraw = dqraw.astype(BF16)
        dcq = _mm(dqraw, wuq_v)
        dxh = dcq * qa
        dqp_ref[...] = (ra * (dxh - xh * jnp.mean(dxh * xh, axis=-1, keepdims=True))).astype(BF16)
        dwuq_ref[...] += _mm_tn(dqraw, cq)
        dqa_ref[...] += _rowsum(dcq * xh)
        dwq_ref[...] += dwq

    row = lambda cols: pl.BlockSpec((tm, cols), lambda t: (t, 0))
    tab = pl.BlockSpec((tm, LANE), lambda t: (t % tpe, 0))
    return _pcall(
        body, name="q_prep_bwd", grid=(n_lat,),
        out_shape=(_sds((n_lat * tm, QL), BF16), _sds((HP, QL), F32), _sds((1, QL), F32), _sds((1, HP), F32)),
        in_specs=[row(HP), row(QL), _const((1, QL)), _const((HP, QL)), _const((1, HP)), tab, tab,
                  _const((HP, LANE)), _const((2 * LANE, HP)), _const((2 * LANE, 2 * LANE)), _const((2 * LANE, 2 * LANE))],
        out_specs=(row(QL), _const((HP, QL)), _const((1, QL)), _const((1, HP))), vmem_mb=40)(
            dq, qp, qa_w, wuq, wq, cos, sin, cs["seg_h"], cs["seg_ht"], cs["rot"], cs["rot_t"])


def _kv_tab_spec(tm, tpe, n_lat):
    return pl.BlockSpec((tm, LANE), lambda t: (jnp.where(t < n_lat, t % tpe, tpe), 0))


def _kv_prep_fwd(ckv, kpe, kva_w, wukv, wk, cosk, sink, cs, *, tm, n_tiles, tpe, n_lat):
    def body(ckv_ref, kpe_ref, kva_ref, wukv_ref, wk_ref, cos_ref, sin_ref, seg, segt, rot, k_ref, v_ref):
        x = ckv_ref[...]
        ckvn = (x * _rms(x) * kva_ref[...]).astype(BF16)
        kv = _mm_nt(ckvn, wukv_ref[...])
        kx = kv[:, :HP] + jnp.concatenate([kpe_ref[...]] * H, axis=-1)
        k, _, _ = _head_norm_rope(kx, wk_ref[...], cos_ref[...], sin_ref[...], seg[...], segt[...], rot[...])
        k_ref[...] = k.astype(BF16)
        v_ref[...] = kv[:, HP:].astype(BF16)

    row = lambda cols: pl.BlockSpec((tm, cols), lambda t: (t, 0))
    tab = _kv_tab_spec(tm, tpe, n_lat)
    r = n_tiles * tm
    return _pcall(
        body, name="kv_prep_fwd", grid=(n_tiles,), out_shape=(_sds((r, HP), BF16), _sds((r, HP), BF16)),
        in_specs=[row(KVL), row(LANE), _const((1, KVL)), _const((2 * HP, KVL)), _const((1, HP)), tab, tab,
                  _const((HP, LANE)), _const((2 * LANE, HP)), _const((2 * LANE, 2 * LANE))],
        out_specs=(row(HP), row(HP)), vmem_mb=40)(
            ckv, kpe, kva_w, wukv, wk, cosk, sink, cs["seg_h"], cs["seg_ht"], cs["rot"])


def _kv_prep_bwd(dks, dvs, ckv, kpe, kva_w, wukv, wk, cosk, sink, cs, *, tm, n_tiles, tpe, n_lat):
    def body(dkl_ref, dkc_ref, dvl_ref, dvc_ref, ckv_ref, kpe_ref, kva_ref, wukv_ref, wk_ref, cos_ref, sin_ref,
             seg, segt, rot, rot_t, dckv_ref, dkpe_ref, dwukv_ref, dkva_ref, dwk_ref):
        t = pl.program_id(0)
        is_lat = t < n_lat

        @pl.when(t == 0)
        def _():
            dwukv_ref[...] = jnp.zeros_like(dwukv_ref)
            dkva_ref[...] = jnp.zeros_like(dkva_ref)
            dwk_ref[...] = jnp.zeros_like(dwk_ref)

        dk = jnp.where(is_lat, dkl_ref[...], dkc_ref[...])
        dv = jnp.where(is_lat, dvl_ref[...], dvc_ref[...])
        x = ckv_ref[...]
        ra = _rms(x)
        xh = x * ra
        kva = kva_ref[...]
        ckvn = (xh * kva).astype(BF16)
        wukv_v = wukv_ref[...]
        wk_v, cos_v, sin_v = wk_ref[...], cos_ref[...], sin_ref[...]
        kv = _mm_nt(ckvn, wukv_v)
        kx = kv[:, :HP] + jnp.concatenate([kpe_ref[...]] * H, axis=-1)
        _, y, rb = _head_norm_rope(kx, wk_v, cos_v, sin_v, seg[...], segt[...], rot[...], rope=False)
        dkx, dwk = _head_norm_rope_bwd(dk, y, rb, wk_v, cos_v, sin_v, seg[...], segt[...], rot_t[...])
        dkpe = dkx[:, 0:LANE]
        for h in range(1, H):
            dkpe = dkpe + dkx[:, h * LANE:(h + 1) * LANE]
        lane = lax.broadcasted_iota(jnp.int32, (tm, LANE), 1)
        dkpe_ref[...] = jnp.where((lane >= DN) & (lane < DH), dkpe, 0.0).astype(BF16)
        dkv = jnp.concatenate([dkx, dv], axis=-1).astype(BF16)
        dckvn = _mm(dkv, wukv_v)
        dxh = dckvn * kva
        dckv_ref[...] = (ra * (dxh - xh * jnp.mean(dxh * xh, axis=-1, keepdims=True))).astype(BF16)
        dwukv_ref[...] += _mm_tn(dkv, ckvn)
        dkva_ref[...] += _rowsum(dckvn * xh)
        dwk_ref[...] += dwk

    row = lambda cols: pl.BlockSpec((tm, cols), lambda t: (t, 0))
    lat = pl.BlockSpec((tm, HP), lambda t: (jnp.minimum(t, n_lat - 1), 0))
    ctx = pl.BlockSpec((tm, HP), lambda t: (jnp.maximum(t - n_lat, 0), 0))
    tab = _kv_tab_spec(tm, tpe, n_lat)
    r = n_tiles * tm
    return _pcall(
        body, name="kv_prep_bwd", grid=(n_tiles,),
        out_shape=(_sds((r, KVL), BF16), _sds((r, LANE), BF16), _sds((2 * HP, KVL), F32), _sds((1, KVL), F32),
                   _sds((1, HP), F32)),
        in_specs=[lat, ctx, lat, ctx, row(KVL), row(LANE), _const((1, KVL)), _const((2 * HP, KVL)), _const((1, HP)),
                  tab, tab, _const((HP, LANE)), _const((2 * LANE, HP)), _const((2 * LANE, 2 * LANE)), _const((2 * LANE, 2 * LANE))],
        out_specs=(row(KVL), row(LANE), _const((2 * HP, KVL)), _const((1, KVL)), _const((1, HP))), vmem_mb=48)(
            dks[0], dks[1], dvs[0], dvs[1], ckv, kpe, kva_w, wukv, wk, cosk, sink,
            cs["seg_h"], cs["seg_ht"], cs["rot"], cs["rot_t"])


_SCALE = DH ** -0.5
_SCALE_LOG2E = _SCALE * 1.4426950408889634


def _key_chunks(s, nc, ck):
    return ([(0, lo, min(lo + ck, s)) for lo in range(0, s, ck)]
            + [(1, lo, min(lo + ck, nc)) for lo in range(0, nc, ck)])


def _attn_fwd(q, k, v, *, nb, s, nc, tq, ck):
    tpe = s // tq
    r_lat = nb * s
    chunks = _key_chunks(s, nc, ck)
    hp = 4

    def body(q_ref, kl_ref, kc_ref, vl_ref, vc_ref, o_ref, lse_ref):
        k_refs, v_refs = (kl_ref, kc_ref), (vl_ref, vc_ref)
        for hh in range(hp):
            hs = slice(hh * LANE, (hh + 1) * LANE)
            qv = q_ref[:, hs]
            xs = [_mm_nt(qv, k_refs[w][lo:hi, hs]) for w, lo, hi in chunks]
            m = jnp.max(xs[0], axis=-1, keepdims=True)
            for x in xs[1:]:
                m = jnp.maximum(m, jnp.max(x, axis=-1, keepdims=True))
            l = acc = None
            for x, (w, lo, hi) in zip(xs, chunks):
                e = jnp.exp2((x - m) * _SCALE_LOG2E)
                lc = jnp.sum(e, axis=-1, keepdims=True)
                pv = _mm(e.astype(BF16), v_refs[w][lo:hi, hs])
                l = lc if l is None else l + lc
                acc = pv if acc is None else acc + pv
            o_ref[:, hs] = (acc / l).astype(BF16)
            lse = m * _SCALE_LOG2E + jnp.log2(l)
            lse_ref[hh] = jnp.transpose(jnp.broadcast_to(lse, (tq, LANE)))[0:8, :]

    qs = pl.BlockSpec((tq, hp * LANE), lambda i, j, t: (i * tpe + t, j))
    kl = pl.BlockSpec((s, hp * LANE), lambda i, j, t: (i, j))
    kc = pl.BlockSpec((nc, hp * LANE), lambda i, j, t: (r_lat // nc + i, j))
    ls = pl.BlockSpec((hp, 8, tq), lambda i, j, t: (i * (H // hp) + j, 0, t))
    return _pcall(body, name="attn_fwd", grid=(nb, H // hp, tpe),
                  out_shape=(_sds((r_lat, HP), BF16), _sds((nb * H, 8, s), F32)),
                  in_specs=[qs, kl, kc, kl, kc], out_specs=(qs, ls), vmem_mb=48)(q, k, k, v, v)


def _attn_bwd(q, k, v, o, do, lse, part, *, nb, s, nc, tq, ck):
    tpe = s // tq
    r_lat = nb * s
    chunks = _key_chunks(s, nc, ck)
    hp = 2
    n_steps = nb * (H // hp) * tpe

    def body(q_ref, kl_ref, kc_ref, vl_ref, vc_ref, o_ref, do_ref, lse_ref, part_ref,
             dq_ref, dkl_ref, dkc_ref, dvl_ref, dvc_ref, recv_ref, akl, akc, avl, avc, send_sems, recv_sems):
        t = pl.program_id(2)
        step = (pl.program_id(0) * (H // hp) + pl.program_id(1)) * tpe + t
        sends = _chip_sends(part_ref, recv_ref, send_sems, recv_sems)

        @pl.when(step == 0)
        def _():
            for cp in sends:
                cp.start()

        @pl.when(step == n_steps - 1)
        def _():
            for cp in sends:
                cp.wait_recv()
            for cp in sends:
                cp.wait_send()

        @pl.when(t == 0)
        def _():
            akl[...] = jnp.zeros_like(akl)
            akc[...] = jnp.zeros_like(akc)
            avl[...] = jnp.zeros_like(avl)
            avc[...] = jnp.zeros_like(avc)

        k_refs, v_refs, ak, av = (kl_ref, kc_ref), (vl_ref, vc_ref), (akl, akc), (avl, avc)
        for hh in range(hp):
            hs = slice(hh * LANE, (hh + 1) * LANE)
            qv = q_ref[:, hs]
            lse = jnp.transpose(jnp.concatenate([lse_ref[hh]] * (LANE // 8), axis=0))[:, 0:1]
            dov = do_ref[:, hs]
            delta = jnp.sum(dov.astype(F32) * o_ref[:, hs].astype(F32), axis=-1, keepdims=True)
            dq = None
            for w, lo, hi in chunks:
                kc_v = k_refs[w][lo:hi, hs]
                p = jnp.exp2(_mm_nt(qv, kc_v) * _SCALE_LOG2E - lse)
                ds = (p * (_mm_nt(dov, v_refs[w][lo:hi, hs]) - delta)).astype(BF16)
                part = _mm(ds, kc_v)
                dq = part if dq is None else dq + part
                ak[w][hs, lo:hi] += _mm_tn(qv, ds)
                av[w][hs, lo:hi] += _mm_tn(dov, p.astype(BF16))
            dq_ref[:, hs] = dq * _SCALE

        @pl.when(t == tpe - 1)
        def _():
            dkl_ref[...] = akl[...].T * _SCALE
            dkc_ref[...] = akc[...].T * _SCALE
            dvl_ref[...] = avl[...].T
            dvc_ref[...] = avc[...].T

    qs = pl.BlockSpec((tq, hp * LANE), lambda i, j, t: (i * tpe + t, j))
    kl = pl.BlockSpec((s, hp * LANE), lambda i, j, t: (i, j))
    kc = pl.BlockSpec((nc, hp * LANE), lambda i, j, t: (r_lat // nc + i, j))
    kc_out = pl.BlockSpec((nc, hp * LANE), lambda i, j, t: (i, j))
    ls = pl.BlockSpec((hp, 8, tq), lambda i, j, t: (i * (H // hp) + j, 0, t))
    return _pcall(
        body, name="attn_bwd", grid=(nb, H // hp, tpe),
        out_shape=(_sds((r_lat, HP), F32), _sds((r_lat, HP), F32), _sds((nb * nc, HP), F32),
                   _sds((r_lat, HP), F32), _sds((nb * nc, HP), F32), _sds((3,) + part.shape[1:], part.dtype)),
        in_specs=[qs, kl, kc, kl, kc, qs, qs, ls, ANY], out_specs=(qs, kl, kc_out, kl, kc_out, ANY),
        scratch=[pltpu.VMEM((hp * LANE, s), F32), pltpu.VMEM((hp * LANE, nc), F32)] * 2
        + [pltpu.SemaphoreType.DMA((3,))] * 2,
        vmem_mb=60)(q, k, k, v, v, o, do, lse, part)


def _chunks_side_by_side(x, j, nch):
    return jnp.concatenate([x[c * CH:(c + 1) * CH, j * LANE:(j + 1) * LANE] for c in range(nch)], axis=-1)


def _first_group_lanes(nch):
    return (lax.broadcasted_iota(jnp.int32, (CH, nch * LANE), 1) & (LANE - 1)) < GD


def _gating(vn, ws_ref, bias_ref, s_scr, tm):
    nch = tm // CH
    first = _first_group_lanes(nch)
    for j in range(G // 2):
        ls = slice(j * LANE, (j + 1) * LANE)
        vst = _chunks_side_by_side(vn, j, nch)
        st = jnp.where(first, _mm(ws_ref[2 * j], vst), _mm(ws_ref[2 * j + 1], vst))
        for c in range(nch):
            s_scr[c * CH:(c + 1) * CH, ls] = st[:, c * LANE:(c + 1) * LANE] + bias_ref[:, ls]


def _mix_fwd(u, v, attn, x1, gate, wv, ws, bias, wout, cs, *, tm, n_lat, tpe):
    nrows = gate.shape[0]

    def body(u_ref, v_ref, attn_ref, x_ref, gate_ref, wv_ref, ws_ref, bias_ref, wout_ref, seg, segt,
             x2_ref, mix_ref, s_scr):
        vg = _gelu(v_ref[...])
        rg = lax.rsqrt(_seg_sum(vg * vg, seg[...]) * (1.0 / GD) + EPS)
        vn = (vg * _seg_bcast(rg, segt[...]) * wv_ref[...]).astype(BF16)
        _gating(vn, ws_ref, bias_ref, s_scr, tm)
        sg = (_gelu(u_ref[...]) * s_scr[...]).astype(BF16)
        mix = _mm(attn_ref[...], wout_ref[0:HP, :]) + _mm(sg, wout_ref[HP:, :])
        mix_ref[...] = mix.astype(BF16)
        x2_ref[...] = x_ref[...] + gate_ref[0] * mix

    row = lambda cols: pl.BlockSpec((tm, cols), lambda t: (t, 0))
    r = n_lat * tm
    return _pcall(
        body, name="mix_fwd", grid=(n_lat,),
        out_shape=(_sds((r, D), F32), _sds((r, D), BF16)),
        in_specs=[row(G * GD), row(G * GD), row(HP), row(D), _mod_spec(1, tpe, nrows), _const((1, G * GD)),
                  _const((G, CH, CH)), _const((CH, G * GD)), _const((HP + G * GD, D)), _const((G * GD, LANE)),
                  _const((2 * LANE, G * GD))],
        out_specs=(row(D), row(D)), scratch=[pltpu.VMEM((tm, G * GD), F32)], vmem_mb=40)(
            u, v, attn, x1, gate, wv, ws, bias, wout, cs["seg_g"], cs["seg_gt"])


def _mix_bwd(dx2, mix, u, v, attn, gate, wv, ws, wst, bias, wout, cs, *, tm, n_lat, tpe):
    nrows = gate.shape[0]
    wrows = HP + G * GD

    def body(dx2_ref, mix_ref, u_ref, v_ref, attn_ref, gate_ref, wv_ref, ws_ref, wst_ref, bias_ref, wout_ref, seg, segt,
             dattn_ref, du_ref, dv_ref, dgate_ref, dwout_ref, dws_ref, dbs_ref, dwv_ref, s_scr, dvn_scr, dbias_scr):
        t = pl.program_id(0)

        @pl.when(t == 0)
        def _():
            dwout_ref[...] = jnp.zeros_like(dwout_ref)
            dws_ref[...] = jnp.zeros_like(dws_ref)
            dwv_ref[...] = jnp.zeros_like(dwv_ref)
            dbias_scr[...] = jnp.zeros_like(dbias_scr)

        @pl.when(t % tpe == 0)
        def _():
            dgate_ref[...] = jnp.zeros_like(dgate_ref)

        dx2 = dx2_ref[...]
        dmix = (dx2 * gate_ref[0]).astype(BF16)
        dcat = _mm_nt(dmix, wout_ref[...])
        dattn_ref[...] = dcat[:, :HP].astype(BF16)
        dsg = dcat[:, HP:]

        vraw = v_ref[...]
        vg = _gelu(vraw)
        rg = lax.rsqrt(_seg_sum(vg * vg, seg[...]) * (1.0 / GD) + EPS)
        r64 = _seg_bcast(rg, segt[...])
        y = vg * r64
        wv_v = wv_ref[...]
        vn = (y * wv_v).astype(BF16)
        _gating(vn, ws_ref, bias_ref, s_scr, tm)
        uraw = u_ref[...]
        ug = _gelu(uraw)
        s = s_scr[...]
        sg = (ug * s).astype(BF16)
        du_ref[...] = (dsg * s * _gelu_grad(uraw)).astype(BF16)
        ds = dsg * ug
        dgate_ref[0] += _rowsum(dx2 * mix_ref[...].astype(F32))
        dwout_ref[...] += _mm_tn(jnp.concatenate([attn_ref[...], sg], axis=-1), dmix)

        nch = tm // CH
        first = _first_group_lanes(nch)
        for c in range(nch):
            dbias_scr[...] += ds[c * CH:(c + 1) * CH, :]
        for j in range(G // 2):
            ls = slice(j * LANE, (j + 1) * LANE)
            dst32 = _chunks_side_by_side(ds, j, nch)
            dst = dst32.astype(BF16)
            vst = _chunks_side_by_side(vn, j, nch)
            dvn_st = jnp.where(first, _mm(wst_ref[2 * j], dst), _mm(wst_ref[2 * j + 1], dst))
            for c in range(nch):
                dvn_scr[c * CH:(c + 1) * CH, ls] = dvn_st[:, c * LANE:(c + 1) * LANE]
            dws_ref[2 * j] += _mm_nt(jnp.where(first, dst32, 0.0).astype(BF16), vst)
            dws_ref[2 * j + 1] += _mm_nt(jnp.where(first, 0.0, dst32).astype(BF16), vst)

        dvn = dvn_scr[...]
        dwv_ref[...] += _rowsum(dvn * y)
        dy = dvn * wv_v
        mean_g = _seg_sum(dy * y, seg[...]) * (1.0 / GD)
        dvg = r64 * (dy - y * _seg_bcast(mean_g, segt[...]))
        dv_ref[...] = (dvg * _gelu_grad(vraw)).astype(BF16)

        @pl.when(t == n_lat - 1)
        def _():
            dbs_ref[...] = _dot_hl(dbias_scr[...], seg[...])

    row = lambda cols: pl.BlockSpec((tm, cols), lambda t: (t, 0))
    r = n_lat * tm
    return _pcall(
        body, name="mix_bwd", grid=(n_lat,),
        out_shape=(_sds((r, HP), BF16), _sds((r, G * GD), BF16), _sds((r, G * GD), BF16), _sds((nrows, 1, D), F32),
                   _sds((wrows, D), F32), _sds((G, CH, CH), F32), _sds((CH, LANE), F32), _sds((1, G * GD), F32)),
        in_specs=[row(D), row(D), row(G * GD), row(G * GD), row(HP), _mod_spec(1, tpe, nrows), _const((1, G * GD)),
                  _const((G, CH, CH)), _const((G, CH, CH)), _const((CH, G * GD)), _const((wrows, D)),
                  _const((G * GD, LANE)), _const((2 * LANE, G * GD))],
        out_specs=(row(HP), row(G * GD), row(G * GD), _mod_spec(1, tpe, nrows), _const((wrows, D)),
                   _const((G, CH, CH)), _const((CH, LANE)), _const((1, G * GD))),
        scratch=[pltpu.VMEM((tm, G * GD), F32), pltpu.VMEM((tm, G * GD), F32), pltpu.VMEM((CH, G * GD), F32)],
        vmem_mb=56)(dx2, mix, u, v, attn, gate, wv, ws, wst, bias, wout, cs["seg_g"], cs["seg_gt"])


def _adamw_math(w, g, m, v):
    m2 = ADAM_B1 * m + (1.0 - ADAM_B1) * g
    v2 = ADAM_B2 * v + (1.0 - ADAM_B2) * (g * g)
    m_hat = m2 / (1.0 - ADAM_B1 ** ADAM_STEP)
    v_hat = v2 / (1.0 - ADAM_B2 ** ADAM_STEP)
    delta = -ADAM_LR * (m_hat / (jnp.sqrt(v_hat) + ADAM_EPS) + ADAM_WD * w)
    return delta, m2, v2


def _row_tile(r, c):
    best = r
    for tr in range(8, r, 8):
        if r % tr == 0 and tr * c * 4 <= MIB:
            best = tr
    return best


def _adamw(w, g, m, v, name):
    r, c = w.shape
    tr = _row_tile(r, c)

    def body(w_ref, g_ref, m_ref, v_ref, d_ref, mo_ref, vo_ref):
        d_ref[...], mo_ref[...], vo_ref[...] = _adamw_math(w_ref[...], g_ref[...], m_ref[...], v_ref[...])

    blk = pl.BlockSpec((tr, c), lambda t: (t, 0))
    return _pcall(body, name=name, grid=(r // tr,), out_shape=(_sds((r, c), F32),) * 3,
                  in_specs=[blk] * 4, out_specs=(blk,) * 3)(w, g, m, v)


def _adamw_small(params):
    n = len(params)

    def body(*refs):
        ins, outs = refs[:4 * n], refs[4 * n:]
        for i in range(n):
            w, g, m, v = (ins[4 * i + k][...] for k in range(4))
            if i == 0:
                sig = _sigmoid(w)
                g = g * (sig * (1.0 + w * (1.0 - sig)))
            d, m2, v2 = _adamw_math(w, g, m, v)
            outs[4 * i][...] = g
            outs[4 * i + 1][...] = d
            outs[4 * i + 2][...] = m2
            outs[4 * i + 3][...] = v2

    flat = [a for p in params for a in p]
    out_shape = tuple(_sds(p[0].shape, F32) for p in params for _ in range(4))
    res = _pcall(body, name="adamw_small", out_shape=out_shape, in_specs=[VMEM] * (4 * n),
                 out_specs=(VMEM,) * (4 * n))(*flat)
    return [res[4 * i:4 * i + 4] for i in range(n)]


def _rope_tables(s):
    rows = jnp.repeat(jnp.arange(s // GRID_W, dtype=F32), GRID_W)
    cols = jnp.tile(jnp.arange(GRID_W, dtype=F32), s // GRID_W)
    half = DR // 2
    inv = ROPE_BASE ** (-jnp.arange(0, half, 2, dtype=F32) / half)
    ang_r = rows[:, None] * inv
    ang_c = cols[:, None] * inv
    ang = jnp.concatenate([ang_r, ang_r, ang_c, ang_c], axis=-1)
    return jnp.cos(ang), jnp.sin(ang)


def _head_pad(a, real):
    return jnp.pad(a, ((0, 0), (0, LANE - real), (0, 0))).reshape(HP, a.shape[2])


def kernel(x, c, ctx, c_ctx, w_ada, b_ada, norm1_w, ffn1_w1, ffn1_w3, ffn1_w2, norm2_w, w_in, q_a_norm_w, w_uq, kv_a_norm_w, w_ukv, q_norm_w, k_norm_w, v_norm_w, w_s, b_s, w_out, norm3_w, ffn2_w1, ffn2_w3, ffn2_w2, loss_target, m_c_ctx, m_w_ada, m_b_ada, m_norm1_w, m_ffn1_w1, m_ffn1_w3, m_ffn1_w2, m_norm2_w, m_w_in, m_q_a_norm_w, m_w_uq, m_kv_a_norm_w, m_w_ukv, m_q_norm_w, m_k_norm_w, m_v_norm_w, m_w_s, m_b_s, m_w_out, m_norm3_w, m_ffn2_w1, m_ffn2_w3, m_ffn2_w2, v_c_ctx, v_w_ada, v_b_ada, v_norm1_w, v_ffn1_w1, v_ffn1_w3, v_ffn1_w2, v_norm2_w, v_w_in, v_q_a_norm_w, v_w_uq, v_kv_a_norm_w, v_w_ukv, v_q_norm_w, v_k_norm_w, v_v_norm_w, v_w_s, v_b_s, v_w_out, v_norm3_w, v_ffn2_w1, v_ffn2_w3, v_ffn2_w2):
    nb, s, _ = x.shape
    nc = ctx.shape[1]
    tm = 256 if nc % 256 == 0 else 128
    tpe = s // tm
    n_lat = nb * tpe
    n_all = n_lat + nb * nc // tm
    tmf = 2 * tm if s % (2 * tm) == 0 and (nb * nc) % (2 * tm) == 0 else tm
    tp = tmf
    r_lat = nb * s
    tpe_p, n_lat_p, n_all_p = s // tp, r_lat // tp, (r_lat + nb * nc) // tp
    me = 4 * lax.axis_index("x") + 2 * lax.axis_index("y") + lax.axis_index("c")
    cs = _consts()
    ncol = w_ada.shape[2]
    fsh = ffn1_w1.shape[2]
    assert nb + 1 <= 8 and NDEV * fsh == FF and NDEV * ncol == NMOD * D and s % nc == 0 and nc % tm == 0

    def t16(a):
        return a.T.astype(BF16)

    wpack1 = jnp.concatenate([t16(ffn1_w1[0]), t16(ffn1_w3[0]), ffn1_w2[0].astype(BF16)], axis=0)
    a_loc = jnp.concatenate([c, c_ctx[None, :], jnp.zeros((7 - nb, D), F32)], axis=0)
    a_raw, _, mod_all, wall1 = _ada_front(a_loc, w_ada[0], lax.dynamic_slice_in_dim(b_ada, me * ncol, ncol, axis=1),
                                          wpack1)
    a_raw = a_raw.reshape(NDEV * 8, D)
    mod_mine = lax.dynamic_slice_in_dim(mod_all, 8 * me, 8, axis=1)
    modtab = mod_mine.transpose(1, 0, 2).reshape(8, NMOD, D)[:nb + 1]
    wpack2 = jnp.concatenate([
        t16(ffn2_w1[0]), t16(ffn2_w3[0]), ffn2_w2[0].astype(BF16),
        t16(w_in[0]), jnp.zeros((12, D), BF16),
        w_out[0].astype(BF16),
        t16(w_uq[0]).reshape(24, D), jnp.zeros((8, D), BF16),
        t16(w_ukv[0]).reshape(16, D)], axis=0)

    def head_w(wn):
        return jnp.tile(jnp.pad(wn, ((0, 0), (0, LANE - DH))), (1, H))

    wq, wk = head_w(q_norm_w), head_w(k_norm_w)
    wv = v_norm_w.reshape(1, G * GD)
    ws16 = w_s[0].astype(BF16)
    wst16 = w_s[0].transpose(0, 2, 1).astype(BF16)
    bias = jnp.repeat(b_s[0].T, GD, axis=1)
    cos, sin = _rope_tables(s)
    cos = jnp.pad(cos, ((0, 0), (DN, LANE - DH)), constant_values=1.0)
    sin = jnp.pad(sin, ((0, 0), (DN, LANE - DH)))
    cos_k = jnp.concatenate([cos, jnp.ones((tm, LANE), F32)], axis=0)
    sin_k = jnp.concatenate([sin, jnp.zeros((tm, LANE), F32)], axis=0)

    xs = (x.reshape(r_lat, D), ctx.reshape(nb * nc, D))
    x1, a1, b1, o1, wall2 = _ffn_fwd(xs, modtab[:, 0:3], norm1_w, wall1, 0, tm=tmf, n_tiles=(r_lat + nb * nc) // tmf,
                                     tpe=s // tmf, n_lat=r_lat // tmf, name="ffn1_fwd", gather=wpack2)

    o0 = 3 * fsh
    wint = wall2[:, o0:o0 + 180].reshape(IN_COLS, D)
    z = lambda n: jnp.zeros((n, D), BF16)
    wint = jnp.concatenate([wint[0:128], wint[160:416], wint[416:928], wint[928:1440],
                            z(DN), wint[128:160], z(LANE - DH)], axis=0)
    wout = wall2[:, o0 + 192:o0 + 320].reshape(D, D)
    wout = jnp.concatenate([_head_pad(wout[:H * DV].reshape(H, DV, D), DV), wout[H * DV:]], axis=0)
    wuq = _head_pad(wall2[:, o0 + 320:o0 + 344].reshape(H, DH, QL), DH)
    wukvt = wall2[:, o0 + 352:o0 + 368].reshape(H, DN + DV, KVL)
    wukv = jnp.concatenate([_head_pad(wukvt[:, :DN], DN), _head_pad(wukvt[:, DN:], DV)], axis=0)

    ckv, qp, u_raw, v_raw, kpe = _proj_fwd(x1, modtab[:, 3:5], norm2_w, wint, tm=tp, n_tiles=n_all_p, tpe=tpe_p)
    q = _q_prep_fwd(qp, q_a_norm_w, wuq, wq, cos, sin, cs, tm=tm, n_lat=n_lat, tpe=tpe)
    k, v = _kv_prep_fwd(ckv, kpe, kv_a_norm_w, wukv, wk, cos_k, sin_k, cs,
                        tm=tm, n_tiles=n_all, tpe=tpe, n_lat=n_lat)
    attn, lse = _attn_fwd(q, k, v, nb=nb, s=s, nc=nc, tq=tm, ck=2048)
    x2, mix = _mix_fwd(u_raw, v_raw, attn, x1, modtab[:nb, 5:6], wv, ws16, bias, wout, cs,
                       tm=tp, n_lat=n_lat_p, tpe=tpe_p)
    dy, a2, b2, o2, lsum = _ffn_fwd((x2,), modtab[:nb, 6:9], norm3_w, wall2, 0, tm=tmf, n_tiles=r_lat // tmf,
                                    tpe=s // tmf, n_lat=r_lat // tmf, name="ffn2_fwd",
                                    target=loss_target.reshape(r_lat, D))
    loss = lax.psum(lsum[0, 0] * (0.5 / D), ("x", "y", "c"))

    tr = 2 * tm if n_lat % 2 == 0 and n_all % 2 == 0 else tm
    dx2, da2, db2, g2, do2, h2, dmod678, dnorm3 = _ffn_bwd_dx(
        dy, (x2,), a2, b2, o2, modtab[:nb, 6:9], norm3_w, wall2, 0,
        tm=tm, n_tiles=n_lat, tpe=tpe, n_lat=n_lat, name="ffn2_bwd_dx")
    g_ffn2 = _ffn_bwd_dw(h2, do2, da2, db2, g2, tr=tr, name="ffn2_bwd_dw")
    part_ffn2 = _add_sibling(g_ffn2, _scatter_sibling([g_ffn2], "scatter_sibling_ffn2")[0], 176, "add_sibling_ffn2")

    dattn, du, dv, dgate5, dwout, dws, dbs, dwv = _mix_bwd(
        dx2, mix, u_raw, v_raw, attn, modtab[:nb, 5:6], wv, ws16, wst16, bias, wout, cs, tm=tp, n_lat=n_lat_p, tpe=tpe_p)
    tq = 2 * tm if s % (2 * tm) == 0 else tm
    dq, dk_l, dk_c, dv_l, dv_c, recv_ffn2 = _attn_bwd(q, k, v, attn, dattn, lse, part_ffn2,
                                                      nb=nb, s=s, nc=nc, tq=tq, ck=1024)
    dqp, dwuq, dqa, dwq = _q_prep_bwd(dq, qp, q_a_norm_w, wuq, wq, cos, sin, cs, tm=tm, n_lat=n_lat, tpe=tpe)
    dckv, dkpe, dwukv, dkva, dwk = _kv_prep_bwd((dk_l, dk_c), (dv_l, dv_c), ckv, kpe, kv_a_norm_w, wukv, wk,
                                                cos_k, sin_k, cs, tm=tm, n_tiles=n_all, tpe=tpe, n_lat=n_lat)
    dx1, dwin, dmod34, dnorm2 = _proj_bwd(dckv, dkpe, dqp, du, dv, dx2, x1, modtab[:, 3:5], norm2_w, wint,
                                          tm=tp, n_tiles=n_all_p, tpe=tpe_p, n_lat=n_lat_p)

    def blocks(a):
        return a.reshape(NDEV, a.shape[0] // NDEV, D)

    dwin_o = jnp.concatenate([dwin[0:128], dwin[KPE_LO:KPE_LO + DR], dwin[128:384], dwin[384:896], dwin[896:1408]],
                             axis=0)
    dwout_o = jnp.concatenate([dwout[:HP].reshape(H, LANE, D)[:, :DV].reshape(H * DV, D), dwout[HP:]], axis=0)
    dwuq_o = dwuq.reshape(H, LANE, QL)[:, :DH]
    dwukv_o = jnp.concatenate([dwukv[:HP].reshape(H, LANE, KVL)[:, :DN], dwukv[HP:].reshape(H, LANE, KVL)[:, :DV]],
                              axis=1)
    gmisc = jnp.concatenate([
        blocks(dwin_o).astype(BF16), jnp.zeros((NDEV, 12, D), BF16),
        blocks(dwout_o).astype(BF16),
        dwuq_o.reshape(NDEV, 24, D).astype(BF16), jnp.zeros((NDEV, 8, D), BF16),
        dwukv_o.reshape(NDEV, 16, D).astype(BF16)], axis=1)

    dx0, da1, db1, g1, do1, h1, dmod012, dnorm1 = _ffn_bwd_dx(
        dx1, xs, a1, b1, o1, modtab[:, 0:3], norm1_w, wall1, 0,
        tm=tm, n_tiles=n_all, tpe=tpe, n_lat=n_lat, name="ffn1_bwd_dx")
    grad_x = dx0.reshape(nb, s, D)
    g_w1, recv_misc = _ffn_bwd_dw_one(da1, h1, tr=tr, name="ffn1_bwd_dw1", part=gmisc)
    g_w3, recv_w1 = _ffn_bwd_dw_one(db1, h1, tr=tr, name="ffn1_bwd_dw3", part=g_w1)
    g_w2, recv_w3 = _ffn_bwd_dw_one(g1, do1, tr=tr, name="ffn1_bwd_dw2", part=g_w3)

    zrow = jnp.zeros((1, D), F32)
    g_lat = jnp.concatenate([dmod012[:nb, 0], dmod012[:nb, 1], dmod012[:nb, 2], dmod34[:nb, 0], dmod34[:nb, 1],
                             dgate5[:, 0], dmod678[:, 0], dmod678[:, 1], dmod678[:, 2]], axis=1)
    g_ctx = jnp.concatenate([dmod012[nb:, 0], dmod012[nb:, 1], dmod012[nb:, 2], dmod34[nb:, 0], dmod34[nb:, 1],
                             zrow, zrow, zrow, zrow], axis=1)
    g_loc = jnp.concatenate([g_lat, g_ctx, jnp.zeros((7 - nb, NMOD * D), F32)], axis=0)

    got_w2, g_all = _scatter_sibling([g_w2], "scatter_sibling_w2", gather=g_loc)
    g_all = g_all.reshape(NDEV * 8, NMOD * D)
    g_cols = lax.dynamic_slice_in_dim(g_all, me * ncol, ncol, axis=1)
    g_w_ada, pc_ctx, g_b_ada = _ada_bwd(a_raw, c_ctx.reshape(D, 1), g_all, g_cols, w_ada[0], nb)
    part_w2 = _add_sibling(g_w2, got_w2, 176, "add_sibling_w2")

    def prow(a):
        a = a.reshape(1, -1)
        return jnp.concatenate([a, jnp.zeros((1, D - a.shape[1]), F32)], axis=1)

    g_qn = dwq.reshape(H, LANE)[:, :DH].sum(0)
    g_kn = dwk.reshape(H, LANE)[:, :DH].sum(0)
    spack = jnp.concatenate([
        dnorm1, dnorm2, dnorm3, prow(dqa), prow(dkva), prow(g_qn), prow(g_kn), prow(dwv),
        prow(dbs[:, :G].T), prow(pc_ctx), jnp.zeros((6, D), F32), dws.reshape(CH, D)], axis=0)
    recv_w2, small_all = _scatter_chips([part_w2], "scatter_chips", gather=spack)
    ssum = _sum_slots(small_all, 144, "sum_small")
    gsum2 = _sum_chips(part_ffn2, recv_ffn2, 176, "sum_grads_ffn2")
    msum = _sum_direct(gmisc, recv_misc, 368, "sum_grads_misc")

    transposed = ("ffn1_w1", "ffn1_w3", "ffn2_w1", "ffn2_w3", "w_in", "w_uq")
    g_big = {
        "ffn1_w1": _sum_direct(g_w1, recv_w1, 176, "sum_grads_w1"),
        "ffn1_w3": _sum_direct(g_w3, recv_w3, 176, "sum_grads_w3"),
        "ffn1_w2": _sum_chips(part_w2, recv_w2, 176, "sum_grads_w2"),
        "ffn2_w1": gsum2[0:fsh], "ffn2_w3": gsum2[fsh:2 * fsh], "ffn2_w2": gsum2[2 * fsh:3 * fsh],
        "w_in": msum[0:180], "w_out": msum[192:320],
        "w_uq": msum[320:344].reshape(DH, QL), "w_ukv": msum[352:368].reshape(DN + DV, KVL).T,
        "w_ada": g_w_ada,
    }

    big_in = {
        "w_ada": (w_ada, m_w_ada, v_w_ada), "ffn1_w1": (ffn1_w1, m_ffn1_w1, v_ffn1_w1),
        "ffn1_w3": (ffn1_w3, m_ffn1_w3, v_ffn1_w3), "ffn1_w2": (ffn1_w2, m_ffn1_w2, v_ffn1_w2),
        "w_in": (w_in, m_w_in, v_w_in), "w_uq": (w_uq, m_w_uq, v_w_uq), "w_ukv": (w_ukv, m_w_ukv, v_w_ukv),
        "w_out": (w_out, m_w_out, v_w_out), "ffn2_w1": (ffn2_w1, m_ffn2_w1, v_ffn2_w1),
        "ffn2_w3": (ffn2_w3, m_ffn2_w3, v_ffn2_w3), "ffn2_w2": (ffn2_w2, m_ffn2_w2, v_ffn2_w2),
    }
    res = {}
    for nm, (w, m, v_) in big_in.items():
        g = g_big[nm]
        if nm in transposed:
            d_, m_, v2_ = _adamw(w[0].T, g, m[0].T, v_[0].T, "adamw_" + nm)
            res[nm] = tuple(a.T[None] for a in (g, d_, m_, v2_))
        else:
            d_, m_, v2_ = _adamw(w[0], g, m[0], v_[0], "adamw_" + nm)
            res[nm] = tuple(a[None] for a in (g, d_, m_, v2_))

    small_in = [
        ("c_ctx", c_ctx, m_c_ctx, v_c_ctx, ssum[9:10], (1, D)),
        ("b_ada", b_ada, m_b_ada, v_b_ada, g_b_ada, (1, NMOD * D)),
        ("norm1_w", norm1_w, m_norm1_w, v_norm1_w, ssum[0:1], (1, D)),
        ("norm2_w", norm2_w, m_norm2_w, v_norm2_w, ssum[1:2], (1, D)),
        ("norm3_w", norm3_w, m_norm3_w, v_norm3_w, ssum[2:3], (1, D)),
        ("q_a_norm_w", q_a_norm_w, m_q_a_norm_w, v_q_a_norm_w, ssum[3:4, :QL], (1, QL)),
        ("kv_a_norm_w", kv_a_norm_w, m_kv_a_norm_w, v_kv_a_norm_w, ssum[4:5, :KVL], (1, KVL)),
        ("q_norm_w", q_norm_w, m_q_norm_w, v_q_norm_w, ssum[5:6, :DH], (1, DH)),
        ("k_norm_w", k_norm_w, m_k_norm_w, v_k_norm_w, ssum[6:7, :DH], (1, DH)),
        ("v_norm_w", v_norm_w, m_v_norm_w, v_v_norm_w, ssum[7:8, :G * GD], (G, GD)),
        ("b_s", b_s, m_b_s, v_b_s, ssum[8:9], (G, CH)),
        ("w_s", w_s, m_w_s, v_w_s, ssum[16:144], (G * CH, CH)),
    ]
    small_out = _adamw_small(
        [(w.reshape(sh), g.reshape(sh), m.reshape(sh), v_.reshape(sh)) for _, w, m, v_, g, sh in small_in])
    for (nm, w, *_), outs in zip(small_in, small_out):
        res[nm] = tuple(a.reshape(w.shape) for a in outs)

    order = ["c_ctx", "w_ada", "b_ada", "norm1_w", "ffn1_w1", "ffn1_w3", "ffn1_w2", "norm2_w", "w_in", "q_a_norm_w",
             "w_uq", "kv_a_norm_w", "w_ukv", "q_norm_w", "k_norm_w", "v_norm_w", "w_s", "b_s", "w_out", "norm3_w",
             "ffn2_w1", "ffn2_w3", "ffn2_w2"]
    return (loss, grad_x, *[res[n][0] for n in order], *[res[n][1] for n in order],
            *[res[n][2] for n in order], *[res[n][3] for n in order])
```

```python
import numpy as np
import jax
import jax.numpy as jnp
from jax import lax
from jax.experimental import pallas as pl
from jax.experimental.pallas import tpu as pltpu

F32 = jnp.float32
BF16 = jnp.bfloat16

D = 1024
FF = 2816
FC = 256
H = 8
DN, DR, DV = 64, 32, 64
DH = DN + DR
QL, KVL = 256, 128
G, GD, CH = 8, 64, 128
NMOD = 9
EPS = 1e-6
GRID_W = 64
ROPE_BASE = 10000.0
NDEV = 8
LANE = 128
HP = H * LANE
IN_COLS = 1440
WIN_ROWS = 1536
KPE_LO = 1408 + DN
MIB = 1 << 20

ADAM_LR, ADAM_B1, ADAM_B2, ADAM_EPS, ADAM_WD, ADAM_STEP = 0.001, 0.9, 0.999, 1e-08, 0.01, 10

MESH = pl.DeviceIdType.MESH
ANY = pl.BlockSpec(memory_space=pl.ANY)
VMEM = pl.BlockSpec(memory_space=pltpu.VMEM)


def _mm(a, b):
    return jnp.dot(a, b, preferred_element_type=F32)


def _mm_nt(a, b):
    return lax.dot_general(a, b, (((1,), (1,)), ((), ())), preferred_element_type=F32)


def _mm_tn(a, b):
    return lax.dot_general(a, b, (((0,), (0,)), ((), ())), preferred_element_type=F32)


def _dot_hl(x, m):
    hi = x.astype(BF16)
    lo = (x - hi.astype(F32)).astype(BF16)
    return _mm(hi, m) + _mm(lo, m)


def _sigmoid(a):
    return 1.0 / (1.0 + jnp.exp(-a))


_G0 = 0.7978845608028654
_G1 = 0.044715


def _gelu(x):
    return 0.5 * x * (1.0 + jnp.tanh(_G0 * (x + _G1 * (x * x * x))))


def _gelu_grad(x):
    th = jnp.tanh(_G0 * (x + _G1 * (x * x * x)))
    return 0.5 * (1.0 + th) + 0.5 * x * (1.0 - th * th) * (_G0 * (1.0 + 3.0 * _G1 * x * x))


def _rowsum(y):
    return jnp.sum(y, axis=0, keepdims=True)


def _rms(x):
    return lax.rsqrt(jnp.mean(x * x, axis=-1, keepdims=True) + EPS)


def _pcall(body, *, name, out_shape, in_specs, out_specs, grid=None, scratch=(), vmem_mb=32, aliases=None):
    kw = {}
    if grid is not None:
        kw["grid"] = grid
        sem = ("arbitrary",) * len(grid)
    else:
        sem = None
    if aliases:
        kw["input_output_aliases"] = aliases
    return pl.pallas_call(
        body, name=name, out_shape=out_shape, in_specs=in_specs, out_specs=out_specs,
        scratch_shapes=list(scratch),
        compiler_params=pltpu.CompilerParams(dimension_semantics=sem, vmem_limit_bytes=vmem_mb * MIB),
        **kw)


def _const(shape):
    nd = len(shape)
    return pl.BlockSpec(shape, lambda *_: (0,) * nd)


def _sds(shape, dt):
    return jax.ShapeDtypeStruct(shape, dt)


def _consts():
    seg_h = np.zeros((HP, LANE), np.float32)
    seg_h[np.arange(HP), np.arange(HP) // LANE] = 1.0
    seg_g = np.zeros((G * GD, LANE), np.float32)
    seg_g[np.arange(G * GD), np.arange(G * GD) // GD] = 1.0
    rot = np.zeros((LANE, LANE), np.float32)
    for base in (DN, DN + 16):
        for j in range(8):
            rot[base + j + 8, base + j] = -1.0
            rot[base + j, base + j + 8] = 1.0
    rot2 = np.zeros((2 * LANE, 2 * LANE), np.float32)
    rot2[:LANE, :LANE] = rot
    rot2[LANE:, LANE:] = rot
    twice = lambda m: np.concatenate([m, m], axis=0)
    c = dict(seg_h=seg_h, seg_ht=twice(seg_h.T), seg_g=seg_g, seg_gt=twice(seg_g.T), rot=rot2, rot_t=rot2.T)
    return {k: jnp.asarray(v, BF16) for k, v in c.items()}


_GATHER_SEMS = [pltpu.SemaphoreType.DMA((7,)), pltpu.SemaphoreType.DMA((7,)), pltpu.SemaphoreType.DMA(())]


def _gather_phases(x_ref, out_ref, send_sems, recv_sems, local_sem):
    mx, my, mc = lax.axis_index("x"), lax.axis_index("y"), lax.axis_index("c")
    me, sibling = (mx, my, mc), (mx, my, 1 - mc)
    chips = [(1 - mx, my), (mx, 1 - my), (1 - mx, 1 - my)]

    def blk(px, py, pc):
        return out_ref.at[4 * px + 2 * py + pc]

    def copy(k, block, to, src=None):
        return pltpu.make_async_remote_copy(
            src_ref=blk(*block) if src is None else src, dst_ref=blk(*block),
            send_sem=send_sems.at[k], recv_sem=recv_sems.at[k], device_id=to, device_id_type=MESH)

    mine = pltpu.make_async_copy(x_ref, blk(*me), local_sem)
    first = [copy(0, me, sibling, src=x_ref)]
    first += [copy(1 + j, me, (*chip, mc), src=x_ref) for j, chip in enumerate(chips)]
    passed = [copy(4 + j, (*chip, mc), sibling) for j, chip in enumerate(chips)]

    def start():
        mine.start()
        for cp in first:
            cp.start()

    def forward():
        for j, chip in enumerate(chips):
            copy(1 + j, (*chip, mc), me).wait_recv()
            passed[j].start()

    def finish():
        copy(0, sibling, me).wait_recv()
        for j, chip in enumerate(chips):
            copy(4 + j, (*chip, 1 - mc), me).wait_recv()
        for cp in first + passed:
            cp.wait_send()
        mine.wait()

    return start, forward, finish


def _chip_sends(p_ref, out_ref, send_sems, recv_sems):
    mx, my, mc = lax.axis_index("x"), lax.axis_index("y"), lax.axis_index("c")
    peers = [(1 - mx, my), (mx, 1 - my), (1 - mx, 1 - my)]
    return [pltpu.make_async_remote_copy(
        src_ref=p_ref.at[2 * px + py], dst_ref=out_ref.at[j], send_sem=send_sems.at[j], recv_sem=recv_sems.at[j],
        device_id=(px, py, mc), device_id_type=MESH) for j, (px, py) in enumerate(peers)]


def _with_gather(copies_of, n, shapes, sems, gather, name, args):
    ns = len(sems)

    def body(*refs):
        ng = 1 if gather is not None else 0
        ins, outs = refs[:n], refs[n + ng:2 * n + ng]
        copies = copies_of(ins, outs, refs[2 * n + 2 * ng:2 * n + 2 * ng + ns])
        if ng:
            start, forward, finish = _gather_phases(refs[n], refs[2 * n + 1], *refs[2 * n + 2 + ns:])
            start()
        for cp in copies:
            cp.start()
        if ng:
            forward()
        for cp in copies:
            cp.wait_recv()
        for cp in copies:
            cp.wait_send()
        if ng:
            finish()

    in_specs, out_shape, scratch = [ANY] * n, list(shapes), list(sems)
    if gather is not None:
        in_specs.append(ANY)
        args = list(args) + [gather]
        out_shape.append(_sds((NDEV,) + gather.shape, gather.dtype))
        scratch += _GATHER_SEMS
    return pl.pallas_call(body, name=name, out_shape=tuple(out_shape), in_specs=in_specs,
                          out_specs=(ANY,) * len(out_shape), scratch_shapes=scratch)(*args)


def _scatter_sibling(xs, name, gather=None):
    n = len(xs)

    def copies_of(x_refs, got_refs, sems):
        send_sems, recv_sems = sems
        mx, my, mc = lax.axis_index("x"), lax.axis_index("y"), lax.axis_index("c")
        return [pltpu.make_async_remote_copy(
            src_ref=x_refs[i].at[2 * j + 1 - mc], dst_ref=got_refs[i].at[j],
            send_sem=send_sems.at[4 * i + j], recv_sem=recv_sems.at[4 * i + j],
            device_id=(mx, my, 1 - mc), device_id_type=MESH) for i in range(n) for j in range(4)]

    shapes = tuple(_sds((4,) + x.shape[1:], x.dtype) for x in xs)
    return _with_gather(copies_of, n, shapes, [pltpu.SemaphoreType.DMA((4 * n,))] * 2, gather, name, xs)


def _scatter_chips(ps, name, gather=None):
    n = len(ps)

    def copies_of(p_refs, out_refs, sems):
        sends = []
        for i in range(n):
            sends += _chip_sends(p_refs[i], out_refs[i], sems[2 * i], sems[2 * i + 1])
        return sends

    shapes = tuple(_sds((3,) + p.shape[1:], p.dtype) for p in ps)
    return _with_gather(copies_of, n, shapes, [pltpu.SemaphoreType.DMA((3,))] * (2 * n), gather, name, ps)


def _add_sibling(x, got, tr, name):
    _, r, c = x.shape

    def body(x_ref, g_ref, o_ref):
        mc = lax.axis_index("c")
        for j in range(4):
            mine = jnp.where(mc == 0, x_ref[2 * j].astype(F32), x_ref[2 * j + 1].astype(F32))
            o_ref[j] = (mine + g_ref[j].astype(F32)).astype(o_ref.dtype)

    return _pcall(body, name=name, grid=(r // tr,), out_shape=_sds(got.shape, got.dtype),
                  in_specs=[pl.BlockSpec((NDEV, tr, c), lambda t: (0, t, 0)), pl.BlockSpec((4, tr, c), lambda t: (0, t, 0))],
                  out_specs=pl.BlockSpec((4, tr, c), lambda t: (0, t, 0)))(x, got)


def _sum_chips(part, recv, tr, name):
    _, r, c = part.shape

    def body(p_ref, r_ref, o_ref):
        slot = 2 * lax.axis_index("x") + lax.axis_index("y")
        acc = p_ref[0].astype(F32)
        for j in range(1, 4):
            acc = jnp.where(slot == j, p_ref[j].astype(F32), acc)
        for j in range(3):
            acc = acc + r_ref[j].astype(F32)
        o_ref[...] = acc

    return _pcall(body, name=name, grid=(r // tr,), out_shape=_sds((r, c), F32),
                  in_specs=[pl.BlockSpec((4, tr, c), lambda t: (0, t, 0)), pl.BlockSpec((3, tr, c), lambda t: (0, t, 0))],
                  out_specs=pl.BlockSpec((tr, c), lambda t: (t, 0)))(part, recv)


def _sum_slots(x, tr, name):
    n, r, c = x.shape

    def body(x_ref, o_ref):
        acc = x_ref[0].astype(F32)
        for s in range(1, n):
            acc = acc + x_ref[s].astype(F32)
        o_ref[...] = acc

    return _pcall(body, name=name, grid=(r // tr,), out_shape=_sds((r, c), F32),
                  in_specs=[pl.BlockSpec((n, tr, c), lambda t: (0, t, 0))],
                  out_specs=pl.BlockSpec((tr, c), lambda t: (t, 0)))(x)


def _ada_front(a_loc, w_loc, b_loc, wpack):
    ncol = w_loc.shape[1]
    nrow = NDEV * a_loc.shape[0]

    def body(a_ref, w_ref, b_ref, wp_ref, araw_ref, mloc_ref, mall_ref, wall_ref,
             a_vm, w_vm, m_vm, lsem, *sems):
        a_start, a_forward, a_finish = _gather_phases(a_ref, araw_ref, *sems[0:3])
        m_start, m_forward, m_finish = _gather_phases(mloc_ref, mall_ref, *sems[3:6])
        w_start, w_forward, w_finish = _gather_phases(wp_ref, wall_ref, *sems[6:9])
        w_in = pltpu.make_async_copy(w_ref, w_vm, lsem.at[0])
        w_in.start()
        a_start()
        w_start()
        a_forward()
        a_finish()
        a_in = pltpu.make_async_copy(araw_ref, a_vm, lsem.at[1])
        a_in.start()
        a_in.wait()
        w_in.wait()
        a = a_vm[...].reshape(nrow, D)
        act = (a * _sigmoid(a)).astype(BF16)
        m_vm[...] = _mm(act, w_vm[...].astype(BF16)) + b_ref[...]
        m_out = pltpu.make_async_copy(m_vm, mloc_ref, lsem.at[2])
        m_out.start()
        m_out.wait()
        m_start()
        m_forward()
        m_finish()
        w_forward()
        w_finish()

    return pl.pallas_call(
        body, name="ada_front",
        out_shape=(_sds((NDEV,) + a_loc.shape, F32), _sds((nrow, ncol), F32), _sds((NDEV, nrow, ncol), F32),
                   _sds((NDEV,) + wpack.shape, wpack.dtype)),
        in_specs=[ANY, ANY, VMEM, ANY], out_specs=(ANY, ANY, ANY, ANY),
        scratch_shapes=[pltpu.VMEM((NDEV,) + a_loc.shape, F32), pltpu.VMEM(w_loc.shape, F32),
                        pltpu.VMEM((nrow, ncol), F32), pltpu.SemaphoreType.DMA((3,))] + _GATHER_SEMS * 3,
        compiler_params=pltpu.CompilerParams(vmem_limit_bytes=32 * MIB),
    )(a_loc, w_loc, b_loc, wpack)


def _ada_bwd(a_raw, cctx_col, g_all, g_cols, w_loc, nb):
    nrow = a_raw.shape[0]
    ncol = w_loc.shape[1]

    def body(a_ref, cc_ref, gall_ref, g_ref, w_ref, dw_ref, pc_ref, gb_ref):
        a = a_ref[...]
        rowid = lax.broadcasted_iota(jnp.int32, (nrow, 1), 0) % 8
        act = jnp.where(rowid < nb, a * _sigmoid(a), 0.0).astype(BF16)
        g = g_ref[...]
        gc = _rowsum(jnp.where(rowid == nb, g, 0.0))
        cc = cc_ref[...]
        dw_ref[...] = _mm_tn(act, g.astype(BF16)) + (cc * _sigmoid(cc)) * gc
        pc_ref[...] = jnp.sum(w_ref[...] * gc, axis=1, keepdims=True)
        gb_ref[...] = _rowsum(gall_ref[...])

    return _pcall(body, name="ada_bwd",
                  out_shape=(_sds((D, ncol), F32), _sds((D, 1), F32), _sds((1, g_all.shape[1]), F32)),
                  in_specs=[VMEM] * 5, out_specs=(VMEM,) * 3, vmem_mb=48)(a_raw, cctx_col, g_all, g_cols, w_loc)


def _mod_spec(k, tpe, nrows):
    return pl.BlockSpec((1, k, D), lambda t: (jnp.minimum(t // tpe, nrows - 1), 0, 0))


def _load_ffn_weights(wall_ref, first, bufs, sems):
    fsh = FF // NDEV
    cps = []
    for j, buf in enumerate(bufs):
        for d in range(NDEV):
            cps.append(pltpu.make_async_copy(wall_ref.at[d, pl.ds((first + j) * fsh, fsh)],
                                             buf.at[pl.ds(d * fsh, fsh)], sems.at[j * NDEV + d]))
    for cp in cps:
        cp.start()
    for cp in cps:
        cp.wait()


def _token_specs(xs, tm, n_lat):
    specs = [pl.BlockSpec((tm, D), lambda t: (jnp.minimum(t, n_lat - 1), 0))]
    if len(xs) == 2:
        specs.append(pl.BlockSpec((tm, D), lambda t: (jnp.maximum(t - n_lat, 0), 0)))
    return specs


def _ffn_fwd(xs, mod3, norm_w, wall, first, *, tm, n_tiles, tpe, n_lat, name, target=None, gather=None):
    nrows = mod3.shape[0]
    r = n_tiles * tm
    nx = len(xs)
    with_loss = target is not None
    with_gather = gather is not None
    fwd_step = max(2 * n_tiles // 3, 1)

    def body(*refs):
        x_refs = refs[:nx]
        pos = nx
        if with_loss:
            tgt_ref = refs[pos]
            pos += 1
        mod_ref, nw_ref, wall_ref = refs[pos:pos + 3]
        pos += 3
        if with_gather:
            gin_ref = refs[pos]
            pos += 1
        xo_ref, a_ref, b_ref, o_ref = refs[pos:pos + 4]
        pos += 4
        if with_loss:
            ls_ref = refs[pos]
            pos += 1
        if with_gather:
            gout_ref = refs[pos]
            pos += 1
        w1_ref, w3_ref, w2_ref, wsem, acc_ref = refs[pos:pos + 5]
        t = pl.program_id(0)
        if with_gather:
            g_start, g_forward, g_finish = _gather_phases(gin_ref, gout_ref, *refs[pos + 5:])

        @pl.when(t == 0)
        def _():
            if with_gather:
                g_start()
            _load_ffn_weights(wall_ref, first, (w1_ref, w3_ref, w2_ref), wsem)
            if with_loss:
                ls_ref[...] = jnp.zeros_like(ls_ref)

        if with_gather:
            @pl.when(t == fwd_step)
            def _():
                g_forward()

            @pl.when(t == n_tiles - 1)
            def _():
                g_finish()

        x = x_refs[0][...]
        if nx == 2:
            x = jnp.where(t < n_lat, x, x_refs[1][...])
        n = x * _rms(x) * nw_ref[...]
        shift, scale, gate = mod_ref[0, 0:1, :], mod_ref[0, 1:2, :], mod_ref[0, 2:3, :]
        h = (n * (1.0 + scale) + shift).astype(BF16)
        nch = FF // FC
        o = None
        for lo_c, hi_c in ((0, nch // 2), (nch // 2, nch)):
            for j in range(lo_c, hi_c):
                sl = slice(j * FC, (j + 1) * FC)
                a = _mm_nt(h, w1_ref[sl, :])
                b = _mm_nt(h, w3_ref[sl, :])
                a_ref[:, sl] = a.astype(BF16)
                b_ref[:, sl] = b.astype(BF16)
                acc_ref[:, sl] = (a * _sigmoid(a) * b).astype(BF16)
            gs = slice(lo_c * FC, hi_c * FC)
            part = _mm(acc_ref[:, gs], w2_ref[gs, :])
            o = part if o is None else o + part
        o_ref[...] = o.astype(BF16)
        out = x + (0.5 * gate) * o
        if with_loss:
            d = out - tgt_ref[...]
            xo_ref[...] = d * (1.0 / D)
            ls_ref[...] += jnp.sum(d * d)
        else:
            xo_ref[...] = out

    row = lambda cols: pl.BlockSpec((tm, cols), lambda t: (t, 0))
    in_specs = _token_specs(xs, tm, n_lat) + ([row(D)] if with_loss else []) + [
        _mod_spec(3, tpe, nrows), _const((1, D)), ANY]
    out_shape = [_sds((r, D), F32), _sds((r, FF), BF16), _sds((r, FF), BF16), _sds((r, D), BF16)]
    out_specs = [row(D), row(FF), row(FF), row(D)]
    scratch = [pltpu.VMEM((FF, D), BF16)] * 3 + [pltpu.SemaphoreType.DMA((3 * NDEV,)), pltpu.VMEM((tm, FF), BF16)]
    if with_loss:
        out_shape.append(_sds((8, LANE), F32))
        out_specs.append(_const((8, LANE)))
    args = list(xs) + ([target] if with_loss else []) + [mod3, norm_w, wall]
    if with_gather:
        assert n_tiles >= 2
        in_specs.append(ANY)
        args.append(gather)
        out_shape.append(_sds((NDEV,) + gather.shape, gather.dtype))
        out_specs.append(ANY)
        scratch += _GATHER_SEMS
    return _pcall(
        body, name=name, grid=(n_tiles,), out_shape=tuple(out_shape), in_specs=in_specs, out_specs=tuple(out_specs),
        scratch=scratch, vmem_mb=56)(*args)


def _ffn_bwd_dx(dout, xs, a, b, o, mod3, norm_w, wall, first, *, tm, n_tiles, tpe, n_lat, name):
    nrows = mod3.shape[0]
    r = n_tiles * tm
    nx = len(xs)

    def body(*refs):
        dout_ref = refs[0]
        x_refs = refs[1:1 + nx]
        (a_ref, b_ref, o_ref, mod_ref, nw_ref, wall_ref,
         dx_ref, da_ref, db_ref, g_ref, do_ref, h_ref, dmod_ref, dnw_ref,
         w1_ref, w3_ref, w2_ref, wsem) = refs[1 + nx:]
        t = pl.program_id(0)

        @pl.when(t == 0)
        def _():
            _load_ffn_weights(wall_ref, first, (w1_ref, w3_ref, w2_ref), wsem)
            dnw_ref[...] = jnp.zeros_like(dnw_ref)

        @pl.when(jnp.where(t < n_lat, t % tpe == 0, t == n_lat))
        def _():
            dmod_ref[...] = jnp.zeros_like(dmod_ref)

        x = x_refs[0][...]
        if nx == 2:
            x = jnp.where(t < n_lat, x, x_refs[1][...])
        dout = dout_ref[...]
        shift, scale, gate = mod_ref[0, 0:1, :], mod_ref[0, 1:2, :], mod_ref[0, 2:3, :]
        d_o = ((0.5 * gate) * dout).astype(BF16)
        do_ref[...] = d_o
        nch = FF // FC
        groups = ((0, nch // 2), (nch // 2, nch))
        dh = None
        for lo_c, hi_c in groups:
            for j in range(lo_c, hi_c):
                sl = slice(j * FC, (j + 1) * FC)
                av = a_ref[:, sl].astype(F32)
                bv = b_ref[:, sl].astype(F32)
                dg = _mm_nt(d_o, w2_ref[sl, :])
                sig = _sigmoid(av)
                sa = av * sig
                g_ref[:, sl] = (sa * bv).astype(BF16)
                da_ref[:, sl] = (dg * bv * (sig * (1.0 + av * (1.0 - sig)))).astype(BF16)
                db_ref[:, sl] = (dg * sa).astype(BF16)
            gs = slice(lo_c * FC, hi_c * FC)
            part = _mm(da_ref[:, gs], w1_ref[gs, :]) + _mm(db_ref[:, gs], w3_ref[gs, :])
            dh = part if dh is None else dh + part
        rr = _rms(x)
        xh = x * rr
        nw = nw_ref[...]
        n = xh * nw
        h_ref[...] = (n * (1.0 + scale) + shift).astype(BF16)
        dgate = _rowsum(0.5 * o_ref[...].astype(F32) * dout)
        dn = dh * (1.0 + scale)
        dxh = dn * nw
        dmod_ref[0, 0:1, :] += _rowsum(dh)
        dmod_ref[0, 1:2, :] += _rowsum(dh * n)
        dmod_ref[0, 2:3, :] += dgate
        dnw_ref[...] += _rowsum(dn * xh)
        dx = dout + rr * (dxh - xh * jnp.mean(dxh * xh, axis=-1, keepdims=True))
        if n_tiles == n_lat:
            dx_ref[...] = dx
        else:
            @pl.when(t < n_lat)
            def _():
                dx_ref[...] = dx

    row = lambda cols: pl.BlockSpec((tm, cols), lambda t: (t, 0))
    lat = pl.BlockSpec((tm, D), lambda t: (jnp.minimum(t, n_lat - 1), 0))
    out_shape = [_sds((n_lat * tm, D), F32), _sds((r, FF), BF16), _sds((r, FF), BF16), _sds((r, FF), BF16),
                 _sds((r, D), BF16), _sds((r, D), BF16), _sds((nrows, 3, D), F32), _sds((1, D), F32)]
    in_specs = [row(D)] + _token_specs(xs, tm, n_lat) + [row(FF), row(FF), row(D), _mod_spec(3, tpe, nrows),
                                                          _const((1, D)), ANY]
    out_specs = [lat, row(FF), row(FF), row(FF), row(D), row(D), _mod_spec(3, tpe, nrows), _const((1, D))]
    scratch = [pltpu.VMEM((FF, D), BF16)] * 3 + [pltpu.SemaphoreType.DMA((3 * NDEV,))]
    args = [dout, *xs, a, b, o, mod3, norm_w, wall]
    return _pcall(body, name=name, grid=(n_tiles,), out_shape=tuple(out_shape), in_specs=in_specs,
                  out_specs=tuple(out_specs), scratch=scratch, vmem_mb=60)(*args)


def _ffn_bwd_dw(h, d_o, da, db, g, *, tr, name):
    r = h.shape[0]
    fh = FF // 2
    fsh = FF // NDEV
    nk = r // tr

    def body(h_ref, do_ref, da_ref, db_ref, g_ref, out_ref, acc1, acc3, acc2):
        k = pl.program_id(1)

        @pl.when(k == 0)
        def _():
            acc1[...] = jnp.zeros_like(acc1)
            acc3[...] = jnp.zeros_like(acc3)
            acc2[...] = jnp.zeros_like(acc2)

        hv = h_ref[...]
        acc1[...] += _mm_tn(da_ref[...], hv)
        acc3[...] += _mm_tn(db_ref[...], hv)
        acc2[...] += _mm_tn(g_ref[...], do_ref[...])

        @pl.when(k == nk - 1)
        def _():
            for i, acc in enumerate((acc1, acc3, acc2)):
                out_ref[:, i * fsh:(i + 1) * fsh, :] = acc[...].reshape(NDEV // 2, fsh, D).astype(BF16)

    rowd = pl.BlockSpec((tr, D), lambda f, k: (k, 0))
    rowf = pl.BlockSpec((tr, fh), lambda f, k: (k, f))
    return _pcall(
        body, name=name, grid=(2, nk), out_shape=_sds((NDEV, 3 * fsh, D), BF16),
        in_specs=[rowd, rowd, rowf, rowf, rowf],
        out_specs=pl.BlockSpec((NDEV // 2, 3 * fsh, D), lambda f, k: (f, 0, 0)),
        scratch=[pltpu.VMEM((fh, D), F32)] * 3, vmem_mb=56)(h, d_o, da, db, g)


def _direct_sends(x_ref, out_ref, send_sems, recv_sems):
    mx, my, mc = lax.axis_index("x"), lax.axis_index("y"), lax.axis_index("c")
    sends = []
    for k in range(1, NDEV):
        px = 1 - mx if (k & 4) else mx
        py = 1 - my if (k & 2) else my
        pc = 1 - mc if (k & 1) else mc
        sends.append(pltpu.make_async_remote_copy(
            src_ref=x_ref.at[4 * px + 2 * py + pc], dst_ref=out_ref.at[k - 1],
            send_sem=send_sems.at[k - 1], recv_sem=recv_sems.at[k - 1], device_id=(px, py, pc), device_id_type=MESH))
    return sends


def _sum_direct(x, recv, tr, name):
    _, r, c = x.shape

    def body(x_ref, r_ref, o_ref):
        me = 4 * lax.axis_index("x") + 2 * lax.axis_index("y") + lax.axis_index("c")
        acc = x_ref[0].astype(F32)
        for j in range(1, NDEV):
            acc = jnp.where(me == j, x_ref[j].astype(F32), acc)
        for j in range(NDEV - 1):
            acc = acc + r_ref[j].astype(F32)
        o_ref[...] = acc

    return _pcall(body, name=name, grid=(r // tr,), out_shape=_sds((r, c), F32),
                  in_specs=[pl.BlockSpec((NDEV, tr, c), lambda t: (0, t, 0)),
                            pl.BlockSpec((NDEV - 1, tr, c), lambda t: (0, t, 0))],
                  out_specs=pl.BlockSpec((tr, c), lambda t: (t, 0)))(x, recv)


def _exchange_behind(x_ref, recv_ref, send_sems, recv_sems, first, last):
    sends = _direct_sends(x_ref, recv_ref, send_sems, recv_sems)

    @pl.when(first)
    def _():
        for cp in sends:
            cp.start()

    @pl.when(last)
    def _():
        for cp in sends:
            cp.wait_recv()
        for cp in sends:
            cp.wait_send()


def _ffn_bwd_dw_one(lhs, rhs, *, tr, name, part=None):
    r = lhs.shape[0]
    fsh = FF // NDEV
    nk = r // tr
    fused = part is not None
    nslot = NDEV - 1

    def body(*refs):
        if fused:
            lhs_ref, rhs_ref, part_ref, out_ref, recv_ref, acc, send_sems, recv_sems = refs
        else:
            lhs_ref, rhs_ref, out_ref, acc = refs
        k = pl.program_id(0)
        if fused:
            _exchange_behind(part_ref, recv_ref, send_sems, recv_sems, k == 0, k == nk - 1)

        @pl.when(k == 0)
        def _():
            acc[...] = jnp.zeros_like(acc)

        acc[...] += _mm_tn(lhs_ref[...], rhs_ref[...])

        @pl.when(k == nk - 1)
        def _():
            out_ref[...] = acc[...].reshape(NDEV, fsh, D).astype(BF16)

    in_specs = [pl.BlockSpec((tr, FF), lambda k: (k, 0)), pl.BlockSpec((tr, D), lambda k: (k, 0))]
    out_shape = [_sds((NDEV, fsh, D), BF16)]
    out_specs = [_const((NDEV, fsh, D))]
    scratch = [pltpu.VMEM((FF, D), F32)]
    args = [lhs, rhs]
    if fused:
        in_specs.append(ANY)
        args.append(part)
        out_shape.append(_sds((nslot,) + part.shape[1:], part.dtype))
        out_specs.append(ANY)
        scratch += [pltpu.SemaphoreType.DMA((nslot,))] * 2
    res = _pcall(body, name=name, grid=(nk,), out_shape=tuple(out_shape), in_specs=in_specs,
                 out_specs=tuple(out_specs), scratch=scratch, vmem_mb=48)(*args)
    return res if fused else res[0]


_PIECES =((0, 128), (128, 384), (384, 896), (896, 1408), (1408, 1536))


def _proj_fwd(x1, mod2, norm_w, wint, *, tm, n_tiles, tpe, name="proj_fwd"):
    nrows = mod2.shape[0]
    r = n_tiles * tm

    def body(x_ref, mod_ref, nw_ref, w_ref, ckv_ref, q_ref, u_ref, v_ref, kpe_ref):
        x = x_ref[...]
        n = x * _rms(x) * nw_ref[...]
        h = (n * (1.0 + mod_ref[0, 1:2, :]) + mod_ref[0, 0:1, :]).astype(BF16)
        for (lo, hi), ref in zip(_PIECES, (ckv_ref, q_ref, u_ref, v_ref, kpe_ref)):
            ref[...] = _mm_nt(h, w_ref[lo:hi, :])

    row = lambda cols: pl.BlockSpec((tm, cols), lambda t: (t, 0))
    widths = [hi - lo for lo, hi in _PIECES]
    return _pcall(
        body, name=name, grid=(n_tiles,),
        out_shape=tuple(_sds((r, w), F32) for w in widths),
        in_specs=[row(D), _mod_spec(2, tpe, nrows), _const((1, D)), _const((WIN_ROWS, D))],
        out_specs=tuple(row(w) for w in widths), vmem_mb=40)(x1, mod2, norm_w, wint)


def _proj_bwd(dckv, dkpe, dq, du, dv, dx2, x1, mod2, norm_w, wint, *, tm, n_tiles, tpe, n_lat, name="proj_bwd"):
    nrows = mod2.shape[0]
    r = n_tiles * tm

    def body(dckv_ref, dkpe_ref, dq_ref, du_ref, dv_ref, dx2_ref, x_ref, mod_ref, nw_ref, w_ref,
             dx_ref, dw_ref, dmod_ref, dnw_ref):
        t = pl.program_id(0)
        is_lat = t < n_lat

        @pl.when(t == 0)
        def _():
            dw_ref[...] = jnp.zeros_like(dw_ref)
            dnw_ref[...] = jnp.zeros_like(dnw_ref)

        @pl.when(jnp.where(is_lat, t % tpe == 0, t == n_lat))
        def _():
            dmod_ref[...] = jnp.zeros_like(dmod_ref)

        x = x_ref[...]
        rr = _rms(x)
        xh = x * rr
        nw = nw_ref[...]
        n = xh * nw
        scale = mod_ref[0, 1:2, :]
        h = (n * (1.0 + scale) + mod_ref[0, 0:1, :]).astype(BF16)
        zero = jnp.zeros((), BF16)
        pieces = (dckv_ref[...], jnp.where(is_lat, dq_ref[...], zero), jnp.where(is_lat, du_ref[...], zero),
                  jnp.where(is_lat, dv_ref[...], zero), dkpe_ref[...])
        dh = None
        for (lo, hi), piece in zip(_PIECES, pieces):
            part = _mm(piece, w_ref[lo:hi, :])
            dh = part if dh is None else dh + part
        dn = dh * (1.0 + scale)
        dxh = dn * nw
        dx = rr * (dxh - xh * jnp.mean(dxh * xh, axis=-1, keepdims=True))
        dx_ref[...] = dx + jnp.where(is_lat, dx2_ref[...], 0.0)
        dmod_ref[0, 0:1, :] += _rowsum(dh)
        dmod_ref[0, 1:2, :] += _rowsum(dh * n)
        dnw_ref[...] += _rowsum(dn * xh)
        for (lo, hi), piece in zip(_PIECES, pieces):
            dw_ref[lo:hi, :] += _mm_tn(piece, h)

    row = lambda cols: pl.BlockSpec((tm, cols), lambda t: (t, 0))
    lat = lambda cols: pl.BlockSpec((tm, cols), lambda t: (jnp.minimum(t, n_lat - 1), 0))
    return _pcall(
        body, name=name, grid=(n_tiles,),
        out_shape=(_sds((r, D), F32), _sds((WIN_ROWS, D), F32), _sds((nrows, 2, D), F32), _sds((1, D), F32)),
        in_specs=[row(128), row(128), lat(256), lat(512), lat(512), lat(D), row(D), _mod_spec(2, tpe, nrows),
                  _const((1, D)), _const((WIN_ROWS, D))],
        out_specs=(row(D), _const((WIN_ROWS, D)), _mod_spec(2, tpe, nrows), _const((1, D))),
        vmem_mb=48)(dckv, dkpe, dq, du, dv, dx2, x1, mod2, norm_w, wint)


def _seg_sum(x, seg):
    return _mm(x.astype(BF16), seg)


def _seg_bcast(v, segt2):
    hi = v.astype(BF16)
    lo = (v - hi.astype(F32)).astype(BF16)
    return _mm(jnp.concatenate([hi, lo], axis=-1), segt2)


def _rope_pairs(t, cos, sin, rot2):
    cos2, sin2 = jnp.concatenate([cos, cos], axis=-1), jnp.concatenate([sin, sin], axis=-1)
    out = []
    for j in range(H // 2):
        tj = t[:, 2 * j * LANE:2 * (j + 1) * LANE]
        out.append(tj * cos2 + _dot_hl(tj, rot2) * sin2)
    return jnp.concatenate(out, axis=-1)


def _head_norm_rope(x, w_pad, cos, sin, seg, segt2, rot2, rope=True):
    rh = lax.rsqrt(_seg_sum(x * x, seg) * (1.0 / DH) + EPS)
    rb = _seg_bcast(rh, segt2)
    y = x * rb
    out = _rope_pairs(y * w_pad, cos, sin, rot2) if rope else None
    return out, y, rb


def _head_norm_rope_bwd(dout, y, rb, w_pad, cos, sin, seg, segt2, rot2_t):
    cos2, sin2 = jnp.concatenate([cos, cos], axis=-1), jnp.concatenate([sin, sin], axis=-1)
    dt = []
    for j in range(H // 2):
        dj = dout[:, 2 * j * LANE:2 * (j + 1) * LANE]
        dt.append(dj * cos2 + _dot_hl(dj * sin2, rot2_t))
    dt = jnp.concatenate(dt, axis=-1)
    dw = _rowsum(dt * y)
    dy = dt * w_pad
    mean_h = _seg_sum(dy * y, seg) * (1.0 / DH)
    return rb * (dy - y * _seg_bcast(mean_h, segt2)), dw


def _q_prep_fwd(qp, qa_w, wuq, wq, cos, sin, cs, *, tm, n_lat, tpe):
    def body(qp_ref, qa_ref, wuq_ref, wq_ref, cos_ref, sin_ref, seg, segt, rot, q_ref):
        x = qp_ref[...]
        cq = (x * _rms(x) * qa_ref[...]).astype(BF16)
        q, _, _ = _head_norm_rope(_mm_nt(cq, wuq_ref[...]), wq_ref[...], cos_ref[...], sin_ref[...],
                                  seg[...], segt[...], rot[...])
        q_ref[...] = q.astype(BF16)

    row = lambda cols: pl.BlockSpec((tm, cols), lambda t: (t, 0))
    tab = pl.BlockSpec((tm, LANE), lambda t: (t % tpe, 0))
    return _pcall(
        body, name="q_prep_fwd", grid=(n_lat,), out_shape=_sds((n_lat * tm, HP), BF16),
        in_specs=[row(QL), _const((1, QL)), _const((HP, QL)), _const((1, HP)), tab, tab,
                  _const((HP, LANE)), _const((2 * LANE, HP)), _const((2 * LANE, 2 * LANE))],
        out_specs=row(HP))(qp, qa_w, wuq, wq, cos, sin, cs["seg_h"], cs["seg_ht"], cs["rot"])


def _q_prep_bwd(dq, qp, qa_w, wuq, wq, cos, sin, cs, *, tm, n_lat, tpe):
    def body(dq_ref, qp_ref, qa_ref, wuq_ref, wq_ref, cos_ref, sin_ref, seg, segt, rot, rot_t,
             dqp_ref, dwuq_ref, dqa_ref, dwq_ref):
        @pl.when(pl.program_id(0) == 0)
        def _():
            dwuq_ref[...] = jnp.zeros_like(dwuq_ref)
            dqa_ref[...] = jnp.zeros_like(dqa_ref)
            dwq_ref[...] = jnp.zeros_like(dwq_ref)

        x = qp_ref[...]
        ra = _rms(x)
        xh = x * ra
        qa = qa_ref[...]
        cq = (xh * qa).astype(BF16)
        wuq_v = wuq_ref[...]
        wq_v, cos_v, sin_v = wq_ref[...], cos_ref[...], sin_ref[...]
        _, y, rb = _head_norm_rope(_mm_nt(cq, wuq_v), wq_v, cos_v, sin_v, seg[...], segt[...], rot[...], rope=False)
        dqraw, dwq = _head_norm_rope_bwd(dq_ref[...], y, rb, wq_v, cos_v, sin_v, seg[...], segt[...], rot_t[...])
        dqraw = dqraw.astype(BF16)
        dcq = _mm(dqraw, wuq_v)
        dxh = dcq * qa
        dqp_ref[...] = (ra * (dxh - xh * jnp.mean(dxh * xh, axis=-1, keepdims=True))).astype(BF16)
        dwuq_ref[...] += _mm_tn(dqraw, cq)
        dqa_ref[...] += _rowsum(dcq * xh)
        dwq_ref[...] += dwq

    row = lambda cols: pl.BlockSpec((tm, cols), lambda t: (t, 0))
    tab = pl.BlockSpec((tm, LANE), lambda t: (t % tpe, 0))
    return _pcall(
        body, name="q_prep_bwd", grid=(n_lat,),
        out_shape=(_sds((n_lat * tm, QL), BF16), _sds((HP, QL), F32), _sds((1, QL), F32), _sds((1, HP), F32)),
        in_specs=[row(HP), row(QL), _const((1, QL)), _const((HP, QL)), _const((1, HP)), tab, tab,
                  _const((HP, LANE)), _const((2 * LANE, HP)), _const((2 * LANE, 2 * LANE)), _const((2 * LANE, 2 * LANE))],
        out_specs=(row(QL), _const((HP, QL)), _const((1, QL)), _const((1, HP))), vmem_mb=40)(
            dq, qp, qa_w, wuq, wq, cos, sin, cs["seg_h"], cs["seg_ht"], cs["rot"], cs["rot_t"])


def _kv_tab_spec(tm, tpe, n_lat):
    return pl.BlockSpec((tm, LANE), lambda t: (jnp.where(t < n_lat, t % tpe, tpe), 0))


def _kv_prep_fwd(ckv, kpe, kva_w, wukv, wk, cosk, sink, cs, *, tm, n_tiles, tpe, n_lat):
    def body(ckv_ref, kpe_ref, kva_ref, wukv_ref, wk_ref, cos_ref, sin_ref, seg, segt, rot, k_ref, v_ref):
        x = ckv_ref[...]
        ckvn = (x * _rms(x) * kva_ref[...]).astype(BF16)
        kv = _mm_nt(ckvn, wukv_ref[...])
        kx = kv[:, :HP] + jnp.concatenate([kpe_ref[...]] * H, axis=-1)
        k, _, _ = _head_norm_rope(kx, wk_ref[...], cos_ref[...], sin_ref[...], seg[...], segt[...], rot[...])
        k_ref[...] = k.astype(BF16)
        v_ref[...] = kv[:, HP:].astype(BF16)

    row = lambda cols: pl.BlockSpec((tm, cols), lambda t: (t, 0))
    tab = _kv_tab_spec(tm, tpe, n_lat)
    r = n_tiles * tm
    return _pcall(
        body, name="kv_prep_fwd", grid=(n_tiles,), out_shape=(_sds((r, HP), BF16), _sds((r, HP), BF16)),
        in_specs=[row(KVL), row(LANE), _const((1, KVL)), _const((2 * HP, KVL)), _const((1, HP)), tab, tab,
                  _const((HP, LANE)), _const((2 * LANE, HP)), _const((2 * LANE, 2 * LANE))],
        out_specs=(row(HP), row(HP)), vmem_mb=40)(
            ckv, kpe, kva_w, wukv, wk, cosk, sink, cs["seg_h"], cs["seg_ht"], cs["rot"])


def _kv_prep_bwd(dks, dvs, ckv, kpe, kva_w, wukv, wk, cosk, sink, cs, *, tm, n_tiles, tpe, n_lat):
    def body(dkl_ref, dkc_ref, dvl_ref, dvc_ref, ckv_ref, kpe_ref, kva_ref, wukv_ref, wk_ref, cos_ref, sin_ref,
             seg, segt, rot, rot_t, dckv_ref, dkpe_ref, dwukv_ref, dkva_ref, dwk_ref):
        t = pl.program_id(0)
        is_lat = t < n_lat

        @pl.when(t == 0)
        def _():
            dwukv_ref[...] = jnp.zeros_like(dwukv_ref)
            dkva_ref[...] = jnp.zeros_like(dkva_ref)
            dwk_ref[...] = jnp.zeros_like(dwk_ref)

        dk = jnp.where(is_lat, dkl_ref[...], dkc_ref[...])
        dv = jnp.where(is_lat, dvl_ref[...], dvc_ref[...])
        x = ckv_ref[...]
        ra = _rms(x)
        xh = x * ra
        kva = kva_ref[...]
        ckvn = (xh * kva).astype(BF16)
        wukv_v = wukv_ref[...]
        wk_v, cos_v, sin_v = wk_ref[...], cos_ref[...], sin_ref[...]
        kv = _mm_nt(ckvn, wukv_v)
        kx = kv[:, :HP] + jnp.concatenate([kpe_ref[...]] * H, axis=-1)
        _, y, rb = _head_norm_rope(kx, wk_v, cos_v, sin_v, seg[...], segt[...], rot[...], rope=False)
        dkx, dwk = _head_norm_rope_bwd(dk, y, rb, wk_v, cos_v, sin_v, seg[...], segt[...], rot_t[...])
        dkpe = dkx[:, 0:LANE]
        for h in range(1, H):
            dkpe = dkpe + dkx[:, h * LANE:(h + 1) * LANE]
        lane = lax.broadcasted_iota(jnp.int32, (tm, LANE), 1)
        dkpe_ref[...] = jnp.where((lane >= DN) & (lane < DH), dkpe, 0.0).astype(BF16)
        dkv = jnp.concatenate([dkx, dv], axis=-1).astype(BF16)
        dckvn = _mm(dkv, wukv_v)
        dxh = dckvn * kva
        dckv_ref[...] = (ra * (dxh - xh * jnp.mean(dxh * xh, axis=-1, keepdims=True))).astype(BF16)
        dwukv_ref[...] += _mm_tn(dkv, ckvn)
        dkva_ref[...] += _rowsum(dckvn * xh)
        dwk_ref[...] += dwk

    row = lambda cols: pl.BlockSpec((tm, cols), lambda t: (t, 0))
    lat = pl.BlockSpec((tm, HP), lambda t: (jnp.minimum(t, n_lat - 1), 0))
    ctx = pl.BlockSpec((tm, HP), lambda t: (jnp.maximum(t - n_lat, 0), 0))
    tab = _kv_tab_spec(tm, tpe, n_lat)
    r = n_tiles * tm
    return _pcall(
        body, name="kv_prep_bwd", grid=(n_tiles,),
        out_shape=(_sds((r, KVL), BF16), _sds((r, LANE), BF16), _sds((2 * HP, KVL), F32), _sds((1, KVL), F32),
                   _sds((1, HP), F32)),
        in_specs=[lat, ctx, lat, ctx, row(KVL), row(LANE), _const((1, KVL)), _const((2 * HP, KVL)), _const((1, HP)),
                  tab, tab, _const((HP, LANE)), _const((2 * LANE, HP)), _const((2 * LANE, 2 * LANE)), _const((2 * LANE, 2 * LANE))],
        out_specs=(row(KVL), row(LANE), _const((2 * HP, KVL)), _const((1, KVL)), _const((1, HP))), vmem_mb=48)(
            dks[0], dks[1], dvs[0], dvs[1], ckv, kpe, kva_w, wukv, wk, cosk, sink,
            cs["seg_h"], cs["seg_ht"], cs["rot"], cs["rot_t"])


_SCALE = DH ** -0.5
_SCALE_LOG2E = _SCALE * 1.4426950408889634


def _key_chunks(s, nc, ck):
    return ([(0, lo, min(lo + ck, s)) for lo in range(0, s, ck)]
            + [(1, lo, min(lo + ck, nc)) for lo in range(0, nc, ck)])


def _attn_fwd(q, k, v, *, nb, s, nc, tq, ck):
    tpe = s // tq
    r_lat = nb * s
    chunks = _key_chunks(s, nc, ck)
    hp = 4

    def body(q_ref, kl_ref, kc_ref, vl_ref, vc_ref, o_ref, lse_ref):
        k_refs, v_refs = (kl_ref, kc_ref), (vl_ref, vc_ref)
        for hh in range(hp):
            hs = slice(hh * LANE, (hh + 1) * LANE)
            qv = q_ref[:, hs]
            xs = [_mm_nt(qv, k_refs[w][lo:hi, hs]) for w, lo, hi in chunks]
            m = jnp.max(xs[0], axis=-1, keepdims=True)
            for x in xs[1:]:
                m = jnp.maximum(m, jnp.max(x, axis=-1, keepdims=True))
            l = acc = None
            for x, (w, lo, hi) in zip(xs, chunks):
                e = jnp.exp2((x - m) * _SCALE_LOG2E)
                lc = jnp.sum(e, axis=-1, keepdims=True)
                pv = _mm(e.astype(BF16), v_refs[w][lo:hi, hs])
                l = lc if l is None else l + lc
                acc = pv if acc is None else acc + pv
            o_ref[:, hs] = (acc / l).astype(BF16)
            lse = m * _SCALE_LOG2E + jnp.log2(l)
            lse_ref[hh] = jnp.transpose(jnp.broadcast_to(lse, (tq, LANE)))[0:8, :]

    qs = pl.BlockSpec((tq, hp * LANE), lambda i, j, t: (i * tpe + t, j))
    kl = pl.BlockSpec((s, hp * LANE), lambda i, j, t: (i, j))
    kc = pl.BlockSpec((nc, hp * LANE), lambda i, j, t: (r_lat // nc + i, j))
    ls = pl.BlockSpec((hp, 8, tq), lambda i, j, t: (i * (H // hp) + j, 0, t))
    return _pcall(body, name="attn_fwd", grid=(nb, H // hp, tpe),
                  out_shape=(_sds((r_lat, HP), BF16), _sds((nb * H, 8, s), F32)),
                  in_specs=[qs, kl, kc, kl, kc], out_specs=(qs, ls), vmem_mb=48)(q, k, k, v, v)


def _attn_bwd(q, k, v, o, do, lse, part, *, nb, s, nc, tq, ck):
    tpe = s // tq
    r_lat = nb * s
    chunks = _key_chunks(s, nc, ck)
    hp = 2
    n_steps = nb * (H // hp) * tpe

    def body(q_ref, kl_ref, kc_ref, vl_ref, vc_ref, o_ref, do_ref, lse_ref, part_ref,
             dq_ref, dkl_ref, dkc_ref, dvl_ref, dvc_ref, recv_ref, akl, akc, avl, avc, send_sems, recv_sems):
        t = pl.program_id(2)
        step = (pl.program_id(0) * (H // hp) + pl.program_id(1)) * tpe + t
        _exchange_behind(part_ref, recv_ref, send_sems, recv_sems, step == 0, step == n_steps - 1)

        @pl.when(t == 0)
        def _():
            akl[...] = jnp.zeros_like(akl)
            akc[...] = jnp.zeros_like(akc)
            avl[...] = jnp.zeros_like(avl)
            avc[...] = jnp.zeros_like(avc)

        k_refs, v_refs, ak, av = (kl_ref, kc_ref), (vl_ref, vc_ref), (akl, akc), (avl, avc)
        for hh in range(hp):
            hs = slice(hh * LANE, (hh + 1) * LANE)
            qv = q_ref[:, hs]
            lse = jnp.transpose(jnp.concatenate([lse_ref[hh]] * (LANE // 8), axis=0))[:, 0:1]
            dov = do_ref[:, hs]
            delta = jnp.sum(dov.astype(F32) * o_ref[:, hs].astype(F32), axis=-1, keepdims=True)
            dq = None
            for w, lo, hi in chunks:
                kc_v = k_refs[w][lo:hi, hs]
                p = jnp.exp2(_mm_nt(qv, kc_v) * _SCALE_LOG2E - lse)
                ds = (p * (_mm_nt(dov, v_refs[w][lo:hi, hs]) - delta)).astype(BF16)
                part = _mm(ds, kc_v)
                dq = part if dq is None else dq + part
                ak[w][hs, lo:hi] += _mm_tn(qv, ds)
                av[w][hs, lo:hi] += _mm_tn(dov, p.astype(BF16))
            dq_ref[:, hs] = dq * _SCALE

        @pl.when(t == tpe - 1)
        def _():
            dkl_ref[...] = akl[...].T * _SCALE
            dkc_ref[...] = akc[...].T * _SCALE
            dvl_ref[...] = avl[...].T
            dvc_ref[...] = avc[...].T

    qs = pl.BlockSpec((tq, hp * LANE), lambda i, j, t: (i * tpe + t, j))
    kl = pl.BlockSpec((s, hp * LANE), lambda i, j, t: (i, j))
    kc = pl.BlockSpec((nc, hp * LANE), lambda i, j, t: (r_lat // nc + i, j))
    kc_out = pl.BlockSpec((nc, hp * LANE), lambda i, j, t: (i, j))
    ls = pl.BlockSpec((hp, 8, tq), lambda i, j, t: (i * (H // hp) + j, 0, t))
    return _pcall(
        body, name="attn_bwd", grid=(nb, H // hp, tpe),
        out_shape=(_sds((r_lat, HP), F32), _sds((r_lat, HP), F32), _sds((nb * nc, HP), F32),
                   _sds((r_lat, HP), F32), _sds((nb * nc, HP), F32), _sds((NDEV - 1,) + part.shape[1:], part.dtype)),
        in_specs=[qs, kl, kc, kl, kc, qs, qs, ls, ANY], out_specs=(qs, kl, kc_out, kl, kc_out, ANY),
        scratch=[pltpu.VMEM((hp * LANE, s), F32), pltpu.VMEM((hp * LANE, nc), F32)] * 2
        + [pltpu.SemaphoreType.DMA((NDEV - 1,))] * 2,
        vmem_mb=60)(q, k, k, v, v, o, do, lse, part)


def _chunks_side_by_side(x, j, nch):
    return jnp.concatenate([x[c * CH:(c + 1) * CH, j * LANE:(j + 1) * LANE] for c in range(nch)], axis=-1)


def _first_group_lanes(nch):
    return (lax.broadcasted_iota(jnp.int32, (CH, nch * LANE), 1) & (LANE - 1)) < GD


def _gating(vn, ws_ref, bias_ref, s_scr, tm):
    nch = tm // CH
    first = _first_group_lanes(nch)
    for j in range(G // 2):
        ls = slice(j * LANE, (j + 1) * LANE)
        vst = _chunks_side_by_side(vn, j, nch)
        st = jnp.where(first, _mm(ws_ref[2 * j], vst), _mm(ws_ref[2 * j + 1], vst))
        for c in range(nch):
            s_scr[c * CH:(c + 1) * CH, ls] = st[:, c * LANE:(c + 1) * LANE] + bias_ref[:, ls]


def _mix_fwd(u, v, attn, x1, gate, wv, ws, bias, wout, cs, *, tm, n_lat, tpe):
    nrows = gate.shape[0]

    def body(u_ref, v_ref, attn_ref, x_ref, gate_ref, wv_ref, ws_ref, bias_ref, wout_ref, seg, segt,
             x2_ref, mix_ref, s_scr):
        vg = _gelu(v_ref[...])
        rg = lax.rsqrt(_seg_sum(vg * vg, seg[...]) * (1.0 / GD) + EPS)
        vn = (vg * _seg_bcast(rg, segt[...]) * wv_ref[...]).astype(BF16)
        _gating(vn, ws_ref, bias_ref, s_scr, tm)
        sg = (_gelu(u_ref[...]) * s_scr[...]).astype(BF16)
        mix = _mm(attn_ref[...], wout_ref[0:HP, :]) + _mm(sg, wout_ref[HP:, :])
        mix_ref[...] = mix.astype(BF16)
        x2_ref[...] = x_ref[...] + gate_ref[0] * mix

    row = lambda cols: pl.BlockSpec((tm, cols), lambda t: (t, 0))
    r = n_lat * tm
    return _pcall(
        body, name="mix_fwd", grid=(n_lat,),
        out_shape=(_sds((r, D), F32), _sds((r, D), BF16)),
        in_specs=[row(G * GD), row(G * GD), row(HP), row(D), _mod_spec(1, tpe, nrows), _const((1, G * GD)),
                  _const((G, CH, CH)), _const((CH, G * GD)), _const((HP + G * GD, D)), _const((G * GD, LANE)),
                  _const((2 * LANE, G * GD))],
        out_specs=(row(D), row(D)), scratch=[pltpu.VMEM((tm, G * GD), F32)], vmem_mb=40)(
            u, v, attn, x1, gate, wv, ws, bias, wout, cs["seg_g"], cs["seg_gt"])


def _mix_bwd(dx2, mix, u, v, attn, gate, wv, ws, wst, bias, wout, cs, *, tm, n_lat, tpe):
    nrows = gate.shape[0]
    wrows = HP + G * GD

    def body(dx2_ref, mix_ref, u_ref, v_ref, attn_ref, gate_ref, wv_ref, ws_ref, wst_ref, bias_ref, wout_ref, seg, segt,
             dattn_ref, du_ref, dv_ref, dgate_ref, dwout_ref, dws_ref, dbs_ref, dwv_ref, s_scr, dvn_scr, dbias_scr):
        t = pl.program_id(0)

        @pl.when(t == 0)
        def _():
            dwout_ref[...] = jnp.zeros_like(dwout_ref)
            dws_ref[...] = jnp.zeros_like(dws_ref)
            dwv_ref[...] = jnp.zeros_like(dwv_ref)
            dbias_scr[...] = jnp.zeros_like(dbias_scr)

        @pl.when(t % tpe == 0)
        def _():
            dgate_ref[...] = jnp.zeros_like(dgate_ref)

        dx2 = dx2_ref[...]
        dmix = (dx2 * gate_ref[0]).astype(BF16)
        dcat = _mm_nt(dmix, wout_ref[...])
        dattn_ref[...] = dcat[:, :HP].astype(BF16)
        dsg = dcat[:, HP:]

        vraw = v_ref[...]
        vg = _gelu(vraw)
        rg = lax.rsqrt(_seg_sum(vg * vg, seg[...]) * (1.0 / GD) + EPS)
        r64 = _seg_bcast(rg, segt[...])
        y = vg * r64
        wv_v = wv_ref[...]
        vn = (y * wv_v).astype(BF16)
        _gating(vn, ws_ref, bias_ref, s_scr, tm)
        uraw = u_ref[...]
        ug = _gelu(uraw)
        s = s_scr[...]
        sg = (ug * s).astype(BF16)
        du_ref[...] = (dsg * s * _gelu_grad(uraw)).astype(BF16)
        ds = dsg * ug
        dgate_ref[0] += _rowsum(dx2 * mix_ref[...].astype(F32))
        dwout_ref[...] += _mm_tn(jnp.concatenate([attn_ref[...], sg], axis=-1), dmix)

        nch = tm // CH
        first = _first_group_lanes(nch)
        for c in range(nch):
            dbias_scr[...] += ds[c * CH:(c + 1) * CH, :]
        for j in range(G // 2):
            ls = slice(j * LANE, (j + 1) * LANE)
            dst32 = _chunks_side_by_side(ds, j, nch)
            dst = dst32.astype(BF16)
            vst = _chunks_side_by_side(vn, j, nch)
            dvn_st = jnp.where(first, _mm(wst_ref[2 * j], dst), _mm(wst_ref[2 * j + 1], dst))
            for c in range(nch):
                dvn_scr[c * CH:(c + 1) * CH, ls] = dvn_st[:, c * LANE:(c + 1) * LANE]
            dws_ref[2 * j] += _mm_nt(jnp.where(first, dst32, 0.0).astype(BF16), vst)
            dws_ref[2 * j + 1] += _mm_nt(jnp.where(first, 0.0, dst32).astype(BF16), vst)

        dvn = dvn_scr[...]
        dwv_ref[...] += _rowsum(dvn * y)
        dy = dvn * wv_v
        mean_g = _seg_sum(dy * y, seg[...]) * (1.0 / GD)
        dvg = r64 * (dy - y * _seg_bcast(mean_g, segt[...]))
        dv_ref[...] = (dvg * _gelu_grad(vraw)).astype(BF16)

        @pl.when(t == n_lat - 1)
        def _():
            dbs_ref[...] = _dot_hl(dbias_scr[...], seg[...])

    row = lambda cols: pl.BlockSpec((tm, cols), lambda t: (t, 0))
    r = n_lat * tm
    return _pcall(
        body, name="mix_bwd", grid=(n_lat,),
        out_shape=(_sds((r, HP), BF16), _sds((r, G * GD), BF16), _sds((r, G * GD), BF16), _sds((nrows, 1, D), F32),
                   _sds((wrows, D), F32), _sds((G, CH, CH), F32), _sds((CH, LANE), F32), _sds((1, G * GD), F32)),
        in_specs=[row(D), row(D), row(G * GD), row(G * GD), row(HP), _mod_spec(1, tpe, nrows), _const((1, G * GD)),
                  _const((G, CH, CH)), _const((G, CH, CH)), _const((CH, G * GD)), _const((wrows, D)),
                  _const((G * GD, LANE)), _const((2 * LANE, G * GD))],
        out_specs=(row(HP), row(G * GD), row(G * GD), _mod_spec(1, tpe, nrows), _const((wrows, D)),
                   _const((G, CH, CH)), _const((CH, LANE)), _const((1, G * GD))),
        scratch=[pltpu.VMEM((tm, G * GD), F32), pltpu.VMEM((tm, G * GD), F32), pltpu.VMEM((CH, G * GD), F32)],
        vmem_mb=56)(dx2, mix, u, v, attn, gate, wv, ws, wst, bias, wout, cs["seg_g"], cs["seg_gt"])


def _adamw_math(w, g, m, v):
    m2 = ADAM_B1 * m + (1.0 - ADAM_B1) * g
    v2 = ADAM_B2 * v + (1.0 - ADAM_B2) * (g * g)
    m_hat = m2 / (1.0 - ADAM_B1 ** ADAM_STEP)
    v_hat = v2 / (1.0 - ADAM_B2 ** ADAM_STEP)
    delta = -ADAM_LR * (m_hat / (jnp.sqrt(v_hat) + ADAM_EPS) + ADAM_WD * w)
    return delta, m2, v2


def _row_tile(r, c):
    best = r
    for tr in range(8, r, 8):
        if r % tr == 0 and tr * c * 4 <= MIB:
            best = tr
    return best


def _adamw(w, g, m, v, name):
    r, c = w.shape
    tr = _row_tile(r, c)

    def body(w_ref, g_ref, m_ref, v_ref, d_ref, mo_ref, vo_ref):
        d_ref[...], mo_ref[...], vo_ref[...] = _adamw_math(w_ref[...], g_ref[...], m_ref[...], v_ref[...])

    blk = pl.BlockSpec((tr, c), lambda t: (t, 0))
    return _pcall(body, name=name, grid=(r // tr,), out_shape=(_sds((r, c), F32),) * 3,
                  in_specs=[blk] * 4, out_specs=(blk,) * 3)(w, g, m, v)


def _adamw_small(params):
    n = len(params)

    def body(*refs):
        ins, outs = refs[:4 * n], refs[4 * n:]
        for i in range(n):
            w, g, m, v = (ins[4 * i + k][...] for k in range(4))
            if i == 0:
                sig = _sigmoid(w)
                g = g * (sig * (1.0 + w * (1.0 - sig)))
            d, m2, v2 = _adamw_math(w, g, m, v)
            outs[4 * i][...] = g
            outs[4 * i + 1][...] = d
            outs[4 * i + 2][...] = m2
            outs[4 * i + 3][...] = v2

    flat = [a for p in params for a in p]
    out_shape = tuple(_sds(p[0].shape, F32) for p in params for _ in range(4))
    res = _pcall(body, name="adamw_small", out_shape=out_shape, in_specs=[VMEM] * (4 * n),
                 out_specs=(VMEM,) * (4 * n))(*flat)
    return [res[4 * i:4 * i + 4] for i in range(n)]


def _rope_tables(s):
    rows = jnp.repeat(jnp.arange(s // GRID_W, dtype=F32), GRID_W)
    cols = jnp.tile(jnp.arange(GRID_W, dtype=F32), s // GRID_W)
    half = DR // 2
    inv = ROPE_BASE ** (-jnp.arange(0, half, 2, dtype=F32) / half)
    ang_r = rows[:, None] * inv
    ang_c = cols[:, None] * inv
    ang = jnp.concatenate([ang_r, ang_r, ang_c, ang_c], axis=-1)
    return jnp.cos(ang), jnp.sin(ang)


def _head_pad(a, real):
    return jnp.pad(a, ((0, 0), (0, LANE - real), (0, 0))).reshape(HP, a.shape[2])


def kernel(x, c, ctx, c_ctx, w_ada, b_ada, norm1_w, ffn1_w1, ffn1_w3, ffn1_w2, norm2_w, w_in, q_a_norm_w, w_uq, kv_a_norm_w, w_ukv, q_norm_w, k_norm_w, v_norm_w, w_s, b_s, w_out, norm3_w, ffn2_w1, ffn2_w3, ffn2_w2, loss_target, m_c_ctx, m_w_ada, m_b_ada, m_norm1_w, m_ffn1_w1, m_ffn1_w3, m_ffn1_w2, m_norm2_w, m_w_in, m_q_a_norm_w, m_w_uq, m_kv_a_norm_w, m_w_ukv, m_q_norm_w, m_k_norm_w, m_v_norm_w, m_w_s, m_b_s, m_w_out, m_norm3_w, m_ffn2_w1, m_ffn2_w3, m_ffn2_w2, v_c_ctx, v_w_ada, v_b_ada, v_norm1_w, v_ffn1_w1, v_ffn1_w3, v_ffn1_w2, v_norm2_w, v_w_in, v_q_a_norm_w, v_w_uq, v_kv_a_norm_w, v_w_ukv, v_q_norm_w, v_k_norm_w, v_v_norm_w, v_w_s, v_b_s, v_w_out, v_norm3_w, v_ffn2_w1, v_ffn2_w3, v_ffn2_w2):
    nb, s, _ = x.shape
    nc = ctx.shape[1]
    tm = 256 if nc % 256 == 0 else 128
    tpe = s // tm
    n_lat = nb * tpe
    n_all = n_lat + nb * nc // tm
    tmf = 2 * tm if s % (2 * tm) == 0 and (nb * nc) % (2 * tm) == 0 else tm
    tp = tmf
    r_lat = nb * s
    tpe_p, n_lat_p, n_all_p = s // tp, r_lat // tp, (r_lat + nb * nc) // tp
    me = 4 * lax.axis_index("x") + 2 * lax.axis_index("y") + lax.axis_index("c")
    cs = _consts()
    ncol = w_ada.shape[2]
    fsh = ffn1_w1.shape[2]
    assert nb + 1 <= 8 and NDEV * fsh == FF and NDEV * ncol == NMOD * D and s % nc == 0 and nc % tm == 0

    def t16(a):
        return a.T.astype(BF16)

    wpack1 = jnp.concatenate([t16(ffn1_w1[0]), t16(ffn1_w3[0]), ffn1_w2[0].astype(BF16)], axis=0)
    a_loc = jnp.concatenate([c, c_ctx[None, :], jnp.zeros((7 - nb, D), F32)], axis=0)
    a_raw, _, mod_all, wall1 = _ada_front(a_loc, w_ada[0], lax.dynamic_slice_in_dim(b_ada, me * ncol, ncol, axis=1),
                                          wpack1)
    a_raw = a_raw.reshape(NDEV * 8, D)
    mod_mine = lax.dynamic_slice_in_dim(mod_all, 8 * me, 8, axis=1)
    modtab = mod_mine.transpose(1, 0, 2).reshape(8, NMOD, D)[:nb + 1]
    wpack2 = jnp.concatenate([
        t16(ffn2_w1[0]), t16(ffn2_w3[0]), ffn2_w2[0].astype(BF16),
        t16(w_in[0]), jnp.zeros((12, D), BF16),
        w_out[0].astype(BF16),
        t16(w_uq[0]).reshape(24, D), jnp.zeros((8, D), BF16),
        t16(w_ukv[0]).reshape(16, D)], axis=0)

    def head_w(wn):
        return jnp.tile(jnp.pad(wn, ((0, 0), (0, LANE - DH))), (1, H))

    wq, wk = head_w(q_norm_w), head_w(k_norm_w)
    wv = v_norm_w.reshape(1, G * GD)
    ws16 = w_s[0].astype(BF16)
    wst16 = w_s[0].transpose(0, 2, 1).astype(BF16)
    bias = jnp.repeat(b_s[0].T, GD, axis=1)
    cos, sin = _rope_tables(s)
    cos = jnp.pad(cos, ((0, 0), (DN, LANE - DH)), constant_values=1.0)
    sin = jnp.pad(sin, ((0, 0), (DN, LANE - DH)))
    cos_k = jnp.concatenate([cos, jnp.ones((tm, LANE), F32)], axis=0)
    sin_k = jnp.concatenate([sin, jnp.zeros((tm, LANE), F32)], axis=0)

    xs = (x.reshape(r_lat, D), ctx.reshape(nb * nc, D))
    x1, a1, b1, o1, wall2 = _ffn_fwd(xs, modtab[:, 0:3], norm1_w, wall1, 0, tm=tmf, n_tiles=(r_lat + nb * nc) // tmf,
                                     tpe=s // tmf, n_lat=r_lat // tmf, name="ffn1_fwd", gather=wpack2)

    o0 = 3 * fsh
    wint = wall2[:, o0:o0 + 180].reshape(IN_COLS, D)
    z = lambda n: jnp.zeros((n, D), BF16)
    wint = jnp.concatenate([wint[0:128], wint[160:416], wint[416:928], wint[928:1440],
                            z(DN), wint[128:160], z(LANE - DH)], axis=0)
    wout = wall2[:, o0 + 192:o0 + 320].reshape(D, D)
    wout = jnp.concatenate([_head_pad(wout[:H * DV].reshape(H, DV, D), DV), wout[H * DV:]], axis=0)
    wuq = _head_pad(wall2[:, o0 + 320:o0 + 344].reshape(H, DH, QL), DH)
    wukvt = wall2[:, o0 + 352:o0 + 368].reshape(H, DN + DV, KVL)
    wukv = jnp.concatenate([_head_pad(wukvt[:, :DN], DN), _head_pad(wukvt[:, DN:], DV)], axis=0)

    ckv, qp, u_raw, v_raw, kpe = _proj_fwd(x1, modtab[:, 3:5], norm2_w, wint, tm=tp, n_tiles=n_all_p, tpe=tpe_p)
    q = _q_prep_fwd(qp, q_a_norm_w, wuq, wq, cos, sin, cs, tm=tm, n_lat=n_lat, tpe=tpe)
    k, v = _kv_prep_fwd(ckv, kpe, kv_a_norm_w, wukv, wk, cos_k, sin_k, cs,
                        tm=tm, n_tiles=n_all, tpe=tpe, n_lat=n_lat)
    attn, lse = _attn_fwd(q, k, v, nb=nb, s=s, nc=nc, tq=tm, ck=2048)
    x2, mix = _mix_fwd(u_raw, v_raw, attn, x1, modtab[:nb, 5:6], wv, ws16, bias, wout, cs,
                       tm=tp, n_lat=n_lat_p, tpe=tpe_p)
    dy, a2, b2, o2, lsum = _ffn_fwd((x2,), modtab[:nb, 6:9], norm3_w, wall2, 0, tm=tmf, n_tiles=r_lat // tmf,
                                    tpe=s // tmf, n_lat=r_lat // tmf, name="ffn2_fwd",
                                    target=loss_target.reshape(r_lat, D))

    tr = 2 * tm if n_lat % 2 == 0 and n_all % 2 == 0 else tm
    dx2, da2, db2, g2, do2, h2, dmod678, dnorm3 = _ffn_bwd_dx(
        dy, (x2,), a2, b2, o2, modtab[:nb, 6:9], norm3_w, wall2, 0,
        tm=tm, n_tiles=n_lat, tpe=tpe, n_lat=n_lat, name="ffn2_bwd_dx")
    g_ffn2 = _ffn_bwd_dw(h2, do2, da2, db2, g2, tr=tr, name="ffn2_bwd_dw")

    dattn, du, dv, dgate5, dwout, dws, dbs, dwv = _mix_bwd(
        dx2, mix, u_raw, v_raw, attn, modtab[:nb, 5:6], wv, ws16, wst16, bias, wout, cs, tm=tp, n_lat=n_lat_p, tpe=tpe_p)
    tq = 2 * tm if s % (2 * tm) == 0 else tm
    dq, dk_l, dk_c, dv_l, dv_c, recv_ffn2 = _attn_bwd(q, k, v, attn, dattn, lse, g_ffn2,
                                                      nb=nb, s=s, nc=nc, tq=tq, ck=1024)
    dqp, dwuq, dqa, dwq = _q_prep_bwd(dq, qp, q_a_norm_w, wuq, wq, cos, sin, cs, tm=tm, n_lat=n_lat, tpe=tpe)
    dckv, dkpe, dwukv, dkva, dwk = _kv_prep_bwd((dk_l, dk_c), (dv_l, dv_c), ckv, kpe, kv_a_norm_w, wukv, wk,
                                                cos_k, sin_k, cs, tm=tm, n_tiles=n_all, tpe=tpe, n_lat=n_lat)
    dx1, dwin, dmod34, dnorm2 = _proj_bwd(dckv, dkpe, dqp, du, dv, dx2, x1, modtab[:, 3:5], norm2_w, wint,
                                          tm=tp, n_tiles=n_all_p, tpe=tpe_p, n_lat=n_lat_p)

    def blocks(a):
        return a.reshape(NDEV, a.shape[0] // NDEV, D)

    dwin_o = jnp.concatenate([dwin[0:128], dwin[KPE_LO:KPE_LO + DR], dwin[128:384], dwin[384:896], dwin[896:1408]],
                             axis=0)
    dwout_o = jnp.concatenate([dwout[:HP].reshape(H, LANE, D)[:, :DV].reshape(H * DV, D), dwout[HP:]], axis=0)
    dwuq_o = dwuq.reshape(H, LANE, QL)[:, :DH]
    dwukv_o = jnp.concatenate([dwukv[:HP].reshape(H, LANE, KVL)[:, :DN], dwukv[HP:].reshape(H, LANE, KVL)[:, :DV]],
                              axis=1)
    gmisc = jnp.concatenate([
        blocks(dwin_o).astype(BF16), jnp.zeros((NDEV, 12, D), BF16),
        blocks(dwout_o).astype(BF16),
        dwuq_o.reshape(NDEV, 24, D).astype(BF16), jnp.zeros((NDEV, 8, D), BF16),
        dwukv_o.reshape(NDEV, 16, D).astype(BF16)], axis=1)

    dx0, da1, db1, g1, do1, h1, dmod012, dnorm1 = _ffn_bwd_dx(
        dx1, xs, a1, b1, o1, modtab[:, 0:3], norm1_w, wall1, 0,
        tm=tm, n_tiles=n_all, tpe=tpe, n_lat=n_lat, name="ffn1_bwd_dx")
    grad_x = dx0.reshape(nb, s, D)
    g_w1, recv_misc = _ffn_bwd_dw_one(da1, h1, tr=tr, name="ffn1_bwd_dw1", part=gmisc)
    g_w3, recv_w1 = _ffn_bwd_dw_one(db1, h1, tr=tr, name="ffn1_bwd_dw3", part=g_w1)
    g_w2, recv_w3 = _ffn_bwd_dw_one(g1, do1, tr=tr, name="ffn1_bwd_dw2", part=g_w3)

    zrow = jnp.zeros((1, D), F32)
    g_lat = jnp.concatenate([dmod012[:nb, 0], dmod012[:nb, 1], dmod012[:nb, 2], dmod34[:nb, 0], dmod34[:nb, 1],
                             dgate5[:, 0], dmod678[:, 0], dmod678[:, 1], dmod678[:, 2]], axis=1)
    g_ctx = jnp.concatenate([dmod012[nb:, 0], dmod012[nb:, 1], dmod012[nb:, 2], dmod34[nb:, 0], dmod34[nb:, 1],
                             zrow, zrow, zrow, zrow], axis=1)
    g_loc = jnp.concatenate([g_lat, g_ctx, jnp.zeros((7 - nb, NMOD * D), F32)], axis=0)

    got_w2, g_all = _scatter_sibling([g_w2], "scatter_sibling_w2", gather=g_loc)
    g_all = g_all.reshape(NDEV * 8, NMOD * D)
    g_cols = lax.dynamic_slice_in_dim(g_all, me * ncol, ncol, axis=1)
    g_w_ada, pc_ctx, g_b_ada = _ada_bwd(a_raw, c_ctx.reshape(D, 1), g_all, g_cols, w_ada[0], nb)
    part_w2 = _add_sibling(g_w2, got_w2, 176, "add_sibling_w2")

    def prow(a):
        a = a.reshape(1, -1)
        return jnp.concatenate([a, jnp.zeros((1, D - a.shape[1]), F32)], axis=1)

    g_qn = dwq.reshape(H, LANE)[:, :DH].sum(0)
    g_kn = dwk.reshape(H, LANE)[:, :DH].sum(0)
    spack = jnp.concatenate([
        dnorm1, dnorm2, dnorm3, prow(dqa), prow(dkva), prow(g_qn), prow(g_kn), prow(dwv),
        prow(dbs[:, :G].T), prow(pc_ctx), prow(lsum[0:1]), jnp.zeros((5, D), F32), dws.reshape(CH, D)],
        axis=0)
    recv_w2, small_all = _scatter_chips([part_w2], "scatter_chips", gather=spack)
    ssum = _sum_slots(small_all, 144, "sum_small")
    loss = ssum[10, 0] * (0.5 / D)
    gsum2 = _sum_direct(g_ffn2, recv_ffn2, 176, "sum_grads_ffn2")
    msum = _sum_direct(gmisc, recv_misc, 368, "sum_grads_misc")

    transposed = ("ffn1_w1", "ffn1_w3", "ffn2_w1", "ffn2_w3", "w_in", "w_uq")
    g_big = {
        "ffn1_w1": _sum_direct(g_w1, recv_w1, 176, "sum_grads_w1"),
        "ffn1_w3": _sum_direct(g_w3, recv_w3, 176, "sum_grads_w3"),
        "ffn1_w2": _sum_chips(part_w2, recv_w2, 176, "sum_grads_w2"),
        "ffn2_w1": gsum2[0:fsh], "ffn2_w3": gsum2[fsh:2 * fsh], "ffn2_w2": gsum2[2 * fsh:3 * fsh],
        "w_in": msum[0:180], "w_out": msum[192:320],
        "w_uq": msum[320:344].reshape(DH, QL), "w_ukv": msum[352:368].reshape(DN + DV, KVL).T,
        "w_ada": g_w_ada,
    }

    big_in = {
        "w_ada": (w_ada, m_w_ada, v_w_ada), "ffn1_w1": (ffn1_w1, m_ffn1_w1, v_ffn1_w1),
        "ffn1_w3": (ffn1_w3, m_ffn1_w3, v_ffn1_w3), "ffn1_w2": (ffn1_w2, m_ffn1_w2, v_ffn1_w2),
        "w_in": (w_in, m_w_in, v_w_in), "w_uq": (w_uq, m_w_uq, v_w_uq), "w_ukv": (w_ukv, m_w_ukv, v_w_ukv),
        "w_out": (w_out, m_w_out, v_w_out), "ffn2_w1": (ffn2_w1, m_ffn2_w1, v_ffn2_w1),
        "ffn2_w3": (ffn2_w3, m_ffn2_w3, v_ffn2_w3), "ffn2_w2": (ffn2_w2, m_ffn2_w2, v_ffn2_w2),
    }
    res = {}
    for nm, (w, m, v_) in big_in.items():
        g = g_big[nm]
        if nm in transposed:
            d_, m_, v2_ = _adamw(w[0].T, g, m[0].T, v_[0].T, "adamw_" + nm)
            res[nm] = tuple(a.T[None] for a in (g, d_, m_, v2_))
        else:
            d_, m_, v2_ = _adamw(w[0], g, m[0], v_[0], "adamw_" + nm)
            res[nm] = tuple(a[None] for a in (g, d_, m_, v2_))

    small_in = [
        ("c_ctx", c_ctx, m_c_ctx, v_c_ctx, ssum[9:10], (1, D)),
        ("b_ada", b_ada, m_b_ada, v_b_ada, g_b_ada, (1, NMOD * D)),
        ("norm1_w", norm1_w, m_norm1_w, v_norm1_w, ssum[0:1], (1, D)),
        ("norm2_w", norm2_w, m_norm2_w, v_norm2_w, ssum[1:2], (1, D)),
        ("norm3_w", norm3_w, m_norm3_w, v_norm3_w, ssum[2:3], (1, D)),
        ("q_a_norm_w", q_a_norm_w, m_q_a_norm_w, v_q_a_norm_w, ssum[3:4, :QL], (1, QL)),
        ("kv_a_norm_w", kv_a_norm_w, m_kv_a_norm_w, v_kv_a_norm_w, ssum[4:5, :KVL], (1, KVL)),
        ("q_norm_w", q_norm_w, m_q_norm_w, v_q_norm_w, ssum[5:6, :DH], (1, DH)),
        ("k_norm_w", k_norm_w, m_k_norm_w, v_k_norm_w, ssum[6:7, :DH], (1, DH)),
        ("v_norm_w", v_norm_w, m_v_norm_w, v_v_norm_w, ssum[7:8, :G * GD], (G, GD)),
        ("b_s", b_s, m_b_s, v_b_s, ssum[8:9], (G, CH)),
        ("w_s", w_s, m_w_s, v_w_s, ssum[16:144], (G * CH, CH)),
    ]
    small_out = _adamw_small(
        [(w.reshape(sh), g.reshape(sh), m.reshape(sh), v_.reshape(sh)) for _, w, m, v_, g, sh in small_in])
    for (nm, w, *_), outs in zip(small_in, small_out):
        res[nm] = tuple(a.reshape(w.shape) for a in outs)

    order = ["c_ctx", "w_ada", "b_ada", "norm1_w", "ffn1_w1", "ffn1_w3", "ffn1_w2", "norm2_w", "w_in", "q_a_norm_w",
             "w_uq", "kv_a_norm_w", "w_ukv", "q_norm_w", "k_norm_w", "v_norm_w", "w_s", "b_s", "w_out", "norm3_w",
             "ffn2_w1", "ffn2_w3", "ffn2_w2"]
    return (loss, grad_x, *[res[n][0] for n in order], *[res[n][1] for n in order],
            *[res[n][2] for n in order], *[res[n][3] for n in order])
```

```python
import numpy as np
import jax
import jax.numpy as jnp
from jax import lax
from jax.experimental import pallas as pl
from jax.experimental.pallas import tpu as pltpu

F32 = jnp.float32
BF16 = jnp.bfloat16

D = 1024
FF = 2816
FC = 256
H = 8
DN, DR, DV = 64, 32, 64
DH = DN + DR
QL, KVL = 256, 128
G, GD, CH = 8, 64, 128
NMOD = 9
EPS = 1e-6
GRID_W = 64
ROPE_BASE = 10000.0
NDEV = 8
LANE = 128
HP = H * LANE
IN_COLS = 1440
WIN_ROWS = 1536
KPE_LO = 1408 + DN
MIB = 1 << 20

ADAM_LR, ADAM_B1, ADAM_B2, ADAM_EPS, ADAM_WD, ADAM_STEP = 0.001, 0.9, 0.999, 1e-08, 0.01, 10

MESH = pl.DeviceIdType.MESH
ANY = pl.BlockSpec(memory_space=pl.ANY)
VMEM = pl.BlockSpec(memory_space=pltpu.VMEM)


def _mm(a, b):
    return jnp.dot(a, b, preferred_element_type=F32)


def _mm_nt(a, b):
    return lax.dot_general(a, b, (((1,), (1,)), ((), ())), preferred_element_type=F32)


def _mm_tn(a, b):
    return lax.dot_general(a, b, (((0,), (0,)), ((), ())), preferred_element_type=F32)


def _dot_hl(x, m):
    hi = x.astype(BF16)
    lo = (x - hi.astype(F32)).astype(BF16)
    return _mm(hi, m) + _mm(lo, m)


def _sigmoid(a):
    return 1.0 / (1.0 + jnp.exp(-a))


_G0 = 0.7978845608028654
_G1 = 0.044715


def _gelu(x):
    return 0.5 * x * (1.0 + jnp.tanh(_G0 * (x + _G1 * (x * x * x))))


def _gelu_grad(x):
    th = jnp.tanh(_G0 * (x + _G1 * (x * x * x)))
    return 0.5 * (1.0 + th) + 0.5 * x * (1.0 - th * th) * (_G0 * (1.0 + 3.0 * _G1 * x * x))


def _rowsum(y):
    return jnp.sum(y, axis=0, keepdims=True)


def _rms(x):
    return lax.rsqrt(jnp.mean(x * x, axis=-1, keepdims=True) + EPS)


def _pcall(body, *, name, out_shape, in_specs, out_specs, grid=None, scratch=(), vmem_mb=32, aliases=None):
    kw = {}
    if grid is not None:
        kw["grid"] = grid
        sem = ("arbitrary",) * len(grid)
    else:
        sem = None
    if aliases:
        kw["input_output_aliases"] = aliases
    return pl.pallas_call(
        body, name=name, out_shape=out_shape, in_specs=in_specs, out_specs=out_specs,
        scratch_shapes=list(scratch),
        compiler_params=pltpu.CompilerParams(dimension_semantics=sem, vmem_limit_bytes=vmem_mb * MIB),
        **kw)


def _const(shape):
    nd = len(shape)
    return pl.BlockSpec(shape, lambda *_: (0,) * nd)


def _sds(shape, dt):
    return jax.ShapeDtypeStruct(shape, dt)


def _consts():
    seg_h = np.zeros((HP, LANE), np.float32)
    seg_h[np.arange(HP), np.arange(HP) // LANE] = 1.0
    seg_g = np.zeros((G * GD, LANE), np.float32)
    seg_g[np.arange(G * GD), np.arange(G * GD) // GD] = 1.0
    rot = np.zeros((LANE, LANE), np.float32)
    for base in (DN, DN + 16):
        for j in range(8):
            rot[base + j + 8, base + j] = -1.0
            rot[base + j, base + j + 8] = 1.0
    rot2 = np.zeros((2 * LANE, 2 * LANE), np.float32)
    rot2[:LANE, :LANE] = rot
    rot2[LANE:, LANE:] = rot
    twice = lambda m: np.concatenate([m, m], axis=0)
    c = dict(seg_h=seg_h, seg_ht=twice(seg_h.T), seg_g=seg_g, seg_gt=twice(seg_g.T), rot=rot2, rot_t=rot2.T)
    return {k: jnp.asarray(v, BF16) for k, v in c.items()}


_GATHER_SEMS = [pltpu.SemaphoreType.DMA((7,)), pltpu.SemaphoreType.DMA((7,)), pltpu.SemaphoreType.DMA(())]


def _gather_phases(x_ref, out_ref, send_sems, recv_sems, local_sem):
    mx, my, mc = lax.axis_index("x"), lax.axis_index("y"), lax.axis_index("c")
    me, sibling = (mx, my, mc), (mx, my, 1 - mc)
    chips = [(1 - mx, my), (mx, 1 - my), (1 - mx, 1 - my)]

    def blk(px, py, pc):
        return out_ref.at[4 * px + 2 * py + pc]

    def copy(k, block, to, src=None):
        return pltpu.make_async_remote_copy(
            src_ref=blk(*block) if src is None else src, dst_ref=blk(*block),
            send_sem=send_sems.at[k], recv_sem=recv_sems.at[k], device_id=to, device_id_type=MESH)

    mine = pltpu.make_async_copy(x_ref, blk(*me), local_sem)
    first = [copy(0, me, sibling, src=x_ref)]
    first += [copy(1 + j, me, (*chip, mc), src=x_ref) for j, chip in enumerate(chips)]
    passed = [copy(4 + j, (*chip, mc), sibling) for j, chip in enumerate(chips)]

    def start():
        mine.start()
        for cp in first:
            cp.start()

    def forward():
        for j, chip in enumerate(chips):
            copy(1 + j, (*chip, mc), me).wait_recv()
            passed[j].start()

    def finish():
        copy(0, sibling, me).wait_recv()
        for j, chip in enumerate(chips):
            copy(4 + j, (*chip, 1 - mc), me).wait_recv()
        for cp in first + passed:
            cp.wait_send()
        mine.wait()

    return start, forward, finish


def _chip_sends(p_ref, out_ref, send_sems, recv_sems):
    mx, my, mc = lax.axis_index("x"), lax.axis_index("y"), lax.axis_index("c")
    peers = [(1 - mx, my), (mx, 1 - my), (1 - mx, 1 - my)]
    return [pltpu.make_async_remote_copy(
        src_ref=p_ref.at[2 * px + py], dst_ref=out_ref.at[j], send_sem=send_sems.at[j], recv_sem=recv_sems.at[j],
        device_id=(px, py, mc), device_id_type=MESH) for j, (px, py) in enumerate(peers)]


def _with_gather(copies_of, n, shapes, sems, gather, name, args):
    ns = len(sems)

    def body(*refs):
        ng = 1 if gather is not None else 0
        ins, outs = refs[:n], refs[n + ng:2 * n + ng]
        copies = copies_of(ins, outs, refs[2 * n + 2 * ng:2 * n + 2 * ng + ns])
        if ng:
            start, forward, finish = _gather_phases(refs[n], refs[2 * n + 1], *refs[2 * n + 2 + ns:])
            start()
        for cp in copies:
            cp.start()
        if ng:
            forward()
        for cp in copies:
            cp.wait_recv()
        for cp in copies:
            cp.wait_send()
        if ng:
            finish()

    in_specs, out_shape, scratch = [ANY] * n, list(shapes), list(sems)
    if gather is not None:
        in_specs.append(ANY)
        args = list(args) + [gather]
        out_shape.append(_sds((NDEV,) + gather.shape, gather.dtype))
        scratch += _GATHER_SEMS
    return pl.pallas_call(body, name=name, out_shape=tuple(out_shape), in_specs=in_specs,
                          out_specs=(ANY,) * len(out_shape), scratch_shapes=scratch)(*args)


def _scatter_sibling(xs, name, gather=None):
    n = len(xs)

    def copies_of(x_refs, got_refs, sems):
        send_sems, recv_sems = sems
        mx, my, mc = lax.axis_index("x"), lax.axis_index("y"), lax.axis_index("c")
        return [pltpu.make_async_remote_copy(
            src_ref=x_refs[i].at[2 * j + 1 - mc], dst_ref=got_refs[i].at[j],
            send_sem=send_sems.at[4 * i + j], recv_sem=recv_sems.at[4 * i + j],
            device_id=(mx, my, 1 - mc), device_id_type=MESH) for i in range(n) for j in range(4)]

    shapes = tuple(_sds((4,) + x.shape[1:], x.dtype) for x in xs)
    return _with_gather(copies_of, n, shapes, [pltpu.SemaphoreType.DMA((4 * n,))] * 2, gather, name, xs)


def _scatter_chips(ps, name, gather=None):
    n = len(ps)

    def copies_of(p_refs, out_refs, sems):
        sends = []
        for i in range(n):
            sends += _chip_sends(p_refs[i], out_refs[i], sems[2 * i], sems[2 * i + 1])
        return sends

    shapes = tuple(_sds((3,) + p.shape[1:], p.dtype) for p in ps)
    return _with_gather(copies_of, n, shapes, [pltpu.SemaphoreType.DMA((3,))] * (2 * n), gather, name, ps)


def _add_sibling(x, got, tr, name):
    _, r, c = x.shape

    def body(x_ref, g_ref, o_ref):
        mc = lax.axis_index("c")
        for j in range(4):
            mine = jnp.where(mc == 0, x_ref[2 * j].astype(F32), x_ref[2 * j + 1].astype(F32))
            o_ref[j] = (mine + g_ref[j].astype(F32)).astype(o_ref.dtype)

    return _pcall(body, name=name, grid=(r // tr,), out_shape=_sds(got.shape, got.dtype),
                  in_specs=[pl.BlockSpec((NDEV, tr, c), lambda t: (0, t, 0)), pl.BlockSpec((4, tr, c), lambda t: (0, t, 0))],
                  out_specs=pl.BlockSpec((4, tr, c), lambda t: (0, t, 0)))(x, got)


def _sum_chips(part, recv, tr, name):
    _, r, c = part.shape

    def body(p_ref, r_ref, o_ref):
        slot = 2 * lax.axis_index("x") + lax.axis_index("y")
        acc = p_ref[0].astype(F32)
        for j in range(1, 4):
            acc = jnp.where(slot == j, p_ref[j].astype(F32), acc)
        for j in range(3):
            acc = acc + r_ref[j].astype(F32)
        o_ref[...] = acc

    return _pcall(body, name=name, grid=(r // tr,), out_shape=_sds((r, c), F32),
                  in_specs=[pl.BlockSpec((4, tr, c), lambda t: (0, t, 0)), pl.BlockSpec((3, tr, c), lambda t: (0, t, 0))],
                  out_specs=pl.BlockSpec((tr, c), lambda t: (t, 0)))(part, recv)


def _sum_slots(x, tr, name):
    n, r, c = x.shape

    def body(x_ref, o_ref):
        acc = x_ref[0].astype(F32)
        for s in range(1, n):
            acc = acc + x_ref[s].astype(F32)
        o_ref[...] = acc

    return _pcall(body, name=name, grid=(r // tr,), out_shape=_sds((r, c), F32),
                  in_specs=[pl.BlockSpec((n, tr, c), lambda t: (0, t, 0))],
                  out_specs=pl.BlockSpec((tr, c), lambda t: (t, 0)))(x)


def _ada_front(a_loc, w_loc, b_loc, wpack):
    ncol = w_loc.shape[1]
    nrow = NDEV * a_loc.shape[0]

    def body(a_ref, w_ref, b_ref, wp_ref, araw_ref, mloc_ref, mall_ref, wall_ref,
             a_vm, w_vm, m_vm, lsem, *sems):
        a_start, a_forward, a_finish = _gather_phases(a_ref, araw_ref, *sems[0:3])
        m_start, m_forward, m_finish = _gather_phases(mloc_ref, mall_ref, *sems[3:6])
        w_start, w_forward, w_finish = _gather_phases(wp_ref, wall_ref, *sems[6:9])
        w_in = pltpu.make_async_copy(w_ref, w_vm, lsem.at[0])
        w_in.start()
        a_start()
        w_start()
        a_forward()
        a_finish()
        a_in = pltpu.make_async_copy(araw_ref, a_vm, lsem.at[1])
        a_in.start()
        a_in.wait()
        w_in.wait()
        a = a_vm[...].reshape(nrow, D)
        act = (a * _sigmoid(a)).astype(BF16)
        m_vm[...] = _mm(act, w_vm[...].astype(BF16)) + b_ref[...]
        m_out = pltpu.make_async_copy(m_vm, mloc_ref, lsem.at[2])
        m_out.start()
        m_out.wait()
        m_start()
        m_forward()
        m_finish()
        w_forward()
        w_finish()

    return pl.pallas_call(
        body, name="ada_front",
        out_shape=(_sds((NDEV,) + a_loc.shape, F32), _sds((nrow, ncol), F32), _sds((NDEV, nrow, ncol), F32),
                   _sds((NDEV,) + wpack.shape, wpack.dtype)),
        in_specs=[ANY, ANY, VMEM, ANY], out_specs=(ANY, ANY, ANY, ANY),
        scratch_shapes=[pltpu.VMEM((NDEV,) + a_loc.shape, F32), pltpu.VMEM(w_loc.shape, F32),
                        pltpu.VMEM((nrow, ncol), F32), pltpu.SemaphoreType.DMA((3,))] + _GATHER_SEMS * 3,
        compiler_params=pltpu.CompilerParams(vmem_limit_bytes=32 * MIB),
    )(a_loc, w_loc, b_loc, wpack)


def _ada_bwd(a_raw, cctx_col, g_all, g_cols, w_loc, nb):
    nrow = a_raw.shape[0]
    ncol = w_loc.shape[1]

    def body(a_ref, cc_ref, gall_ref, g_ref, w_ref, dw_ref, pc_ref, gb_ref):
        a = a_ref[...]
        rowid = lax.broadcasted_iota(jnp.int32, (nrow, 1), 0) % 8
        act = jnp.where(rowid < nb, a * _sigmoid(a), 0.0).astype(BF16)
        g = g_ref[...]
        gc = _rowsum(jnp.where(rowid == nb, g, 0.0))
        cc = cc_ref[...]
        dw_ref[...] = _mm_tn(act, g.astype(BF16)) + (cc * _sigmoid(cc)) * gc
        pc_ref[...] = jnp.sum(w_ref[...] * gc, axis=1, keepdims=True)
        gb_ref[...] = _rowsum(gall_ref[...])

    return _pcall(body, name="ada_bwd",
                  out_shape=(_sds((D, ncol), F32), _sds((D, 1), F32), _sds((1, g_all.shape[1]), F32)),
                  in_specs=[VMEM] * 5, out_specs=(VMEM,) * 3, vmem_mb=48)(a_raw, cctx_col, g_all, g_cols, w_loc)


def _mod_spec(k, tpe, nrows):
    return pl.BlockSpec((1, k, D), lambda t: (jnp.minimum(t // tpe, nrows - 1), 0, 0))


def _load_ffn_weights(wall_ref, first, bufs, sems):
    fsh = FF // NDEV
    cps = []
    for j, buf in enumerate(bufs):
        for d in range(NDEV):
            cps.append(pltpu.make_async_copy(wall_ref.at[d, pl.ds((first + j) * fsh, fsh)],
                                             buf.at[pl.ds(d * fsh, fsh)], sems.at[j * NDEV + d]))
    for cp in cps:
        cp.start()
    for cp in cps:
        cp.wait()


def _token_specs(xs, tm, n_lat):
    specs = [pl.BlockSpec((tm, D), lambda t: (jnp.minimum(t, n_lat - 1), 0))]
    if len(xs) == 2:
        specs.append(pl.BlockSpec((tm, D), lambda t: (jnp.maximum(t - n_lat, 0), 0)))
    return specs


def _ffn_fwd(xs, mod3, norm_w, wall, first, *, tm, n_tiles, tpe, n_lat, name, target=None, gather=None):
    nrows = mod3.shape[0]
    r = n_tiles * tm
    nx = len(xs)
    with_loss = target is not None
    with_gather = gather is not None
    fwd_step = max(2 * n_tiles // 3, 1)

    def body(*refs):
        x_refs = refs[:nx]
        pos = nx
        if with_loss:
            tgt_ref = refs[pos]
            pos += 1
        mod_ref, nw_ref, wall_ref = refs[pos:pos + 3]
        pos += 3
        if with_gather:
            gin_ref = refs[pos]
            pos += 1
        xo_ref, a_ref, b_ref, o_ref = refs[pos:pos + 4]
        pos += 4
        if with_loss:
            ls_ref = refs[pos]
            pos += 1
        if with_gather:
            gout_ref = refs[pos]
            pos += 1
        w1_ref, w3_ref, w2_ref, wsem, acc_ref = refs[pos:pos + 5]
        t = pl.program_id(0)
        if with_gather:
            g_start, g_forward, g_finish = _gather_phases(gin_ref, gout_ref, *refs[pos + 5:])

        @pl.when(t == 0)
        def _():
            if with_gather:
                g_start()
            _load_ffn_weights(wall_ref, first, (w1_ref, w3_ref, w2_ref), wsem)
            if with_loss:
                ls_ref[...] = jnp.zeros_like(ls_ref)

        if with_gather:
            @pl.when(t == fwd_step)
            def _():
                g_forward()

            @pl.when(t == n_tiles - 1)
            def _():
                g_finish()

        x = x_refs[0][...]
        if nx == 2:
            x = jnp.where(t < n_lat, x, x_refs[1][...])
        n = x * _rms(x) * nw_ref[...]
        shift, scale, gate = mod_ref[0, 0:1, :], mod_ref[0, 1:2, :], mod_ref[0, 2:3, :]
        h = (n * (1.0 + scale) + shift).astype(BF16)
        nch = FF // FC
        o = None
        for lo_c, hi_c in ((0, nch // 2), (nch // 2, nch)):
            for j in range(lo_c, hi_c):
                sl = slice(j * FC, (j + 1) * FC)
                a = _mm_nt(h, w1_ref[sl, :])
                b = _mm_nt(h, w3_ref[sl, :])
                a_ref[:, sl] = a.astype(BF16)
                b_ref[:, sl] = b.astype(BF16)
                acc_ref[:, sl] = (a * _sigmoid(a) * b).astype(BF16)
            gs = slice(lo_c * FC, hi_c * FC)
            part = _mm(acc_ref[:, gs], w2_ref[gs, :])
            o = part if o is None else o + part
        o_ref[...] = o.astype(BF16)
        out = x + (0.5 * gate) * o
        if with_loss:
            d = out - tgt_ref[...]
            xo_ref[...] = d * (1.0 / D)
            ls_ref[...] += jnp.sum(d * d)
        else:
            xo_ref[...] = out

    row = lambda cols: pl.BlockSpec((tm, cols), lambda t: (t, 0))
    in_specs = _token_specs(xs, tm, n_lat) + ([row(D)] if with_loss else []) + [
        _mod_spec(3, tpe, nrows), _const((1, D)), ANY]
    out_shape = [_sds((r, D), F32), _sds((r, FF), BF16), _sds((r, FF), BF16), _sds((r, D), BF16)]
    out_specs = [row(D), row(FF), row(FF), row(D)]
    scratch = [pltpu.VMEM((FF, D), BF16)] * 3 + [pltpu.SemaphoreType.DMA((3 * NDEV,)), pltpu.VMEM((tm, FF), BF16)]
    if with_loss:
        out_shape.append(_sds((8, LANE), F32))
        out_specs.append(_const((8, LANE)))
    args = list(xs) + ([target] if with_loss else []) + [mod3, norm_w, wall]
    if with_gather:
        assert n_tiles >= 2
        in_specs.append(ANY)
        args.append(gather)
        out_shape.append(_sds((NDEV,) + gather.shape, gather.dtype))
        out_specs.append(ANY)
        scratch += _GATHER_SEMS
    return _pcall(
        body, name=name, grid=(n_tiles,), out_shape=tuple(out_shape), in_specs=in_specs, out_specs=tuple(out_specs),
        scratch=scratch, vmem_mb=56)(*args)


def _ffn_bwd_dx(dout, xs, a, b, o, mod3, norm_w, wall, first, *, tm, n_tiles, tpe, n_lat, name):
    nrows = mod3.shape[0]
    r = n_tiles * tm
    nx = len(xs)

    def body(*refs):
        dout_ref = refs[0]
        x_refs = refs[1:1 + nx]
        (a_ref, b_ref, o_ref, mod_ref, nw_ref, wall_ref,
         dx_ref, da_ref, db_ref, g_ref, do_ref, h_ref, dmod_ref, dnw_ref,
         w1_ref, w3_ref, w2_ref, wsem) = refs[1 + nx:]
        t = pl.program_id(0)

        @pl.when(t == 0)
        def _():
            _load_ffn_weights(wall_ref, first, (w1_ref, w3_ref, w2_ref), wsem)
            dnw_ref[...] = jnp.zeros_like(dnw_ref)

        @pl.when(jnp.where(t < n_lat, t % tpe == 0, t == n_lat))
        def _():
            dmod_ref[...] = jnp.zeros_like(dmod_ref)

        x = x_refs[0][...]
        if nx == 2:
            x = jnp.where(t < n_lat, x, x_refs[1][...])
        dout = dout_ref[...]
        shift, scale, gate = mod_ref[0, 0:1, :], mod_ref[0, 1:2, :], mod_ref[0, 2:3, :]
        d_o = ((0.5 * gate) * dout).astype(BF16)
        do_ref[...] = d_o
        nch = FF // FC
        groups = ((0, nch // 2), (nch // 2, nch))
        dh = None
        for lo_c, hi_c in groups:
            for j in range(lo_c, hi_c):
                sl = slice(j * FC, (j + 1) * FC)
                av = a_ref[:, sl].astype(F32)
                bv = b_ref[:, sl].astype(F32)
                dg = _mm_nt(d_o, w2_ref[sl, :])
                sig = _sigmoid(av)
                sa = av * sig
                g_ref[:, sl] = (sa * bv).astype(BF16)
                da_ref[:, sl] = (dg * bv * (sig * (1.0 + av * (1.0 - sig)))).astype(BF16)
                db_ref[:, sl] = (dg * sa).astype(BF16)
            gs = slice(lo_c * FC, hi_c * FC)
            part = _mm(da_ref[:, gs], w1_ref[gs, :]) + _mm(db_ref[:, gs], w3_ref[gs, :])
            dh = part if dh is None else dh + part
        rr = _rms(x)
        xh = x * rr
        nw = nw_ref[...]
        n = xh * nw
        h_ref[...] = (n * (1.0 + scale) + shift).astype(BF16)
        dgate = _rowsum(0.5 * o_ref[...].astype(F32) * dout)
        dn = dh * (1.0 + scale)
        dxh = dn * nw
        dmod_ref[0, 0:1, :] += _rowsum(dh)
        dmod_ref[0, 1:2, :] += _rowsum(dh * n)
        dmod_ref[0, 2:3, :] += dgate
        dnw_ref[...] += _rowsum(dn * xh)
        dx = dout + rr * (dxh - xh * jnp.mean(dxh * xh, axis=-1, keepdims=True))
        if n_tiles == n_lat:
            dx_ref[...] = dx
        else:
            @pl.when(t < n_lat)
            def _():
                dx_ref[...] = dx

    row = lambda cols: pl.BlockSpec((tm, cols), lambda t: (t, 0))
    lat = pl.BlockSpec((tm, D), lambda t: (jnp.minimum(t, n_lat - 1), 0))
    out_shape = [_sds((n_lat * tm, D), F32), _sds((r, FF), BF16), _sds((r, FF), BF16), _sds((r, FF), BF16),
                 _sds((r, D), BF16), _sds((r, D), BF16), _sds((nrows, 3, D), F32), _sds((1, D), F32)]
    in_specs = [row(D)] + _token_specs(xs, tm, n_lat) + [row(FF), row(FF), row(D), _mod_spec(3, tpe, nrows),
                                                          _const((1, D)), ANY]
    out_specs = [lat, row(FF), row(FF), row(FF), row(D), row(D), _mod_spec(3, tpe, nrows), _const((1, D))]
    scratch = [pltpu.VMEM((FF, D), BF16)] * 3 + [pltpu.SemaphoreType.DMA((3 * NDEV,))]
    args = [dout, *xs, a, b, o, mod3, norm_w, wall]
    return _pcall(body, name=name, grid=(n_tiles,), out_shape=tuple(out_shape), in_specs=in_specs,
                  out_specs=tuple(out_specs), scratch=scratch, vmem_mb=60)(*args)


def _ffn_bwd_dw(h, d_o, da, db, g, *, tr, name):
    r = h.shape[0]
    fh = FF // 2
    fsh = FF // NDEV
    nk = r // tr

    def body(h_ref, do_ref, da_ref, db_ref, g_ref, out_ref, acc1, acc3, acc2):
        k = pl.program_id(1)

        @pl.when(k == 0)
        def _():
            acc1[...] = jnp.zeros_like(acc1)
            acc3[...] = jnp.zeros_like(acc3)
            acc2[...] = jnp.zeros_like(acc2)

        hv = h_ref[...]
        acc1[...] += _mm_tn(da_ref[...], hv)
        acc3[...] += _mm_tn(db_ref[...], hv)
        acc2[...] += _mm_tn(g_ref[...], do_ref[...])

        @pl.when(k == nk - 1)
        def _():
            for i, acc in enumerate((acc1, acc3, acc2)):
                out_ref[:, i * fsh:(i + 1) * fsh, :] = acc[...].reshape(NDEV // 2, fsh, D).astype(BF16)

    rowd = pl.BlockSpec((tr, D), lambda f, k: (k, 0))
    rowf = pl.BlockSpec((tr, fh), lambda f, k: (k, f))
    return _pcall(
        body, name=name, grid=(2, nk), out_shape=_sds((NDEV, 3 * fsh, D), BF16),
        in_specs=[rowd, rowd, rowf, rowf, rowf],
        out_specs=pl.BlockSpec((NDEV // 2, 3 * fsh, D), lambda f, k: (f, 0, 0)),
        scratch=[pltpu.VMEM((fh, D), F32)] * 3, vmem_mb=56)(h, d_o, da, db, g)


def _direct_sends(x_ref, out_ref, send_sems, recv_sems):
    mx, my, mc = lax.axis_index("x"), lax.axis_index("y"), lax.axis_index("c")
    sends = []
    for k in range(1, NDEV):
        px = 1 - mx if (k & 4) else mx
        py = 1 - my if (k & 2) else my
        pc = 1 - mc if (k & 1) else mc
        sends.append(pltpu.make_async_remote_copy(
            src_ref=x_ref.at[4 * px + 2 * py + pc], dst_ref=out_ref.at[k - 1],
            send_sem=send_sems.at[k - 1], recv_sem=recv_sems.at[k - 1], device_id=(px, py, pc), device_id_type=MESH))
    return sends


def _sum_direct(x, recv, tr, name):
    _, r, c = x.shape

    def body(x_ref, r_ref, o_ref):
        me = 4 * lax.axis_index("x") + 2 * lax.axis_index("y") + lax.axis_index("c")
        acc = x_ref[0].astype(F32)
        for j in range(1, NDEV):
            acc = jnp.where(me == j, x_ref[j].astype(F32), acc)
        for j in range(NDEV - 1):
            acc = acc + r_ref[j].astype(F32)
        o_ref[...] = acc

    return _pcall(body, name=name, grid=(r // tr,), out_shape=_sds((r, c), F32),
                  in_specs=[pl.BlockSpec((NDEV, tr, c), lambda t: (0, t, 0)),
                            pl.BlockSpec((NDEV - 1, tr, c), lambda t: (0, t, 0))],
                  out_specs=pl.BlockSpec((tr, c), lambda t: (t, 0)))(x, recv)


def _exchange_behind(x_ref, recv_ref, send_sems, recv_sems, first, last):
    sends = _direct_sends(x_ref, recv_ref, send_sems, recv_sems)

    @pl.when(first)
    def _():
        for cp in sends:
            cp.start()

    @pl.when(last)
    def _():
        for cp in sends:
            cp.wait_recv()
        for cp in sends:
            cp.wait_send()


def _ffn_bwd_dw_one(lhs, rhs, *, tr, name, part=None):
    r = lhs.shape[0]
    fsh = FF // NDEV
    nk = r // tr
    fused = part is not None
    nslot = NDEV - 1

    def body(*refs):
        if fused:
            lhs_ref, rhs_ref, part_ref, out_ref, recv_ref, acc, send_sems, recv_sems = refs
        else:
            lhs_ref, rhs_ref, out_ref, acc = refs
        k = pl.program_id(0)
        if fused:
            _exchange_behind(part_ref, recv_ref, send_sems, recv_sems, k == 0, k == nk - 1)

        @pl.when(k == 0)
        def _():
            acc[...] = jnp.zeros_like(acc)

        acc[...] += _mm_tn(lhs_ref[...], rhs_ref[...])

        @pl.when(k == nk - 1)
        def _():
            out_ref[...] = acc[...].reshape(NDEV, fsh, D).astype(BF16)

    in_specs = [pl.BlockSpec((tr, FF), lambda k: (k, 0)), pl.BlockSpec((tr, D), lambda k: (k, 0))]
    out_shape = [_sds((NDEV, fsh, D), BF16)]
    out_specs = [_const((NDEV, fsh, D))]
    scratch = [pltpu.VMEM((FF, D), F32)]
    args = [lhs, rhs]
    if fused:
        in_specs.append(ANY)
        args.append(part)
        out_shape.append(_sds((nslot,) + part.shape[1:], part.dtype))
        out_specs.append(ANY)
        scratch += [pltpu.SemaphoreType.DMA((nslot,))] * 2
    res = _pcall(body, name=name, grid=(nk,), out_shape=tuple(out_shape), in_specs=in_specs,
                 out_specs=tuple(out_specs), scratch=scratch, vmem_mb=48)(*args)
    return res if fused else res[0]


_PIECES =((0, 128), (128, 384), (384, 896), (896, 1408), (1408, 1536))


def _proj_fwd(x1, mod2, norm_w, wint, *, tm, n_tiles, tpe, name="proj_fwd"):
    nrows = mod2.shape[0]
    r = n_tiles * tm

    def body(x_ref, mod_ref, nw_ref, w_ref, ckv_ref, q_ref, u_ref, v_ref, kpe_ref):
        x = x_ref[...]
        n = x * _rms(x) * nw_ref[...]
        h = (n * (1.0 + mod_ref[0, 1:2, :]) + mod_ref[0, 0:1, :]).astype(BF16)
        for (lo, hi), ref in zip(_PIECES, (ckv_ref, q_ref, u_ref, v_ref, kpe_ref)):
            ref[...] = _mm_nt(h, w_ref[lo:hi, :])

    row = lambda cols: pl.BlockSpec((tm, cols), lambda t: (t, 0))
    widths = [hi - lo for lo, hi in _PIECES]
    return _pcall(
        body, name=name, grid=(n_tiles,),
        out_shape=tuple(_sds((r, w), F32) for w in widths),
        in_specs=[row(D), _mod_spec(2, tpe, nrows), _const((1, D)), _const((WIN_ROWS, D))],
        out_specs=tuple(row(w) for w in widths), vmem_mb=40)(x1, mod2, norm_w, wint)


def _proj_bwd(dckv, dkpe, dq, du, dv, dx2, x1, mod2, norm_w, wint, *, tm, n_tiles, tpe, n_lat, name="proj_bwd"):
    nrows = mod2.shape[0]
    r = n_tiles * tm

    def body(dckv_ref, dkpe_ref, dq_ref, du_ref, dv_ref, dx2_ref, x_ref, mod_ref, nw_ref, w_ref,
             dx_ref, dw_ref, dmod_ref, dnw_ref):
        t = pl.program_id(0)
        is_lat = t < n_lat

        @pl.when(t == 0)
        def _():
            dw_ref[...] = jnp.zeros_like(dw_ref)
            dnw_ref[...] = jnp.zeros_like(dnw_ref)

        @pl.when(jnp.where(is_lat, t % tpe == 0, t == n_lat))
        def _():
            dmod_ref[...] = jnp.zeros_like(dmod_ref)

        x = x_ref[...]
        rr = _rms(x)
        xh = x * rr
        nw = nw_ref[...]
        n = xh * nw
        scale = mod_ref[0, 1:2, :]
        h = (n * (1.0 + scale) + mod_ref[0, 0:1, :]).astype(BF16)
        zero = jnp.zeros((), BF16)
        pieces = (dckv_ref[...], jnp.where(is_lat, dq_ref[...], zero), jnp.where(is_lat, du_ref[...], zero),
                  jnp.where(is_lat, dv_ref[...], zero), dkpe_ref[...])
        dh = None
        for (lo, hi), piece in zip(_PIECES, pieces):
            part = _mm(piece, w_ref[lo:hi, :])
            dh = part if dh is None else dh + part
        dn = dh * (1.0 + scale)
        dxh = dn * nw
        dx = rr * (dxh - xh * jnp.mean(dxh * xh, axis=-1, keepdims=True))
        dx_ref[...] = dx + jnp.where(is_lat, dx2_ref[...], 0.0)
        dmod_ref[0, 0:1, :] += _rowsum(dh)
        dmod_ref[0, 1:2, :] += _rowsum(dh * n)
        dnw_ref[...] += _rowsum(dn * xh)
        for (lo, hi), piece in zip(_PIECES, pieces):
            dw_ref[lo:hi, :] += _mm_tn(piece, h)

    row = lambda cols: pl.BlockSpec((tm, cols), lambda t: (t, 0))
    lat = lambda cols: pl.BlockSpec((tm, cols), lambda t: (jnp.minimum(t, n_lat - 1), 0))
    return _pcall(
        body, name=name, grid=(n_tiles,),
        out_shape=(_sds((r, D), F32), _sds((WIN_ROWS, D), F32), _sds((nrows, 2, D), F32), _sds((1, D), F32)),
        in_specs=[row(128), row(128), lat(256), lat(512), lat(512), lat(D), row(D), _mod_spec(2, tpe, nrows),
                  _const((1, D)), _const((WIN_ROWS, D))],
        out_specs=(row(D), _const((WIN_ROWS, D)), _mod_spec(2, tpe, nrows), _const((1, D))),
        vmem_mb=48)(dckv, dkpe, dq, du, dv, dx2, x1, mod2, norm_w, wint)


def _seg_sum(x, seg):
    return _mm(x.astype(BF16), seg)


def _seg_bcast(v, segt2):
    hi = v.astype(BF16)
    lo = (v - hi.astype(F32)).astype(BF16)
    return _mm(jnp.concatenate([hi, lo], axis=-1), segt2)


def _rope_pairs(t, cos, sin, rot2):
    cos2, sin2 = jnp.concatenate([cos, cos], axis=-1), jnp.concatenate([sin, sin], axis=-1)
    out = []
    for j in range(H // 2):
        tj = t[:, 2 * j * LANE:2 * (j + 1) * LANE]
        out.append(tj * cos2 + _dot_hl(tj, rot2) * sin2)
    return jnp.concatenate(out, axis=-1)


def _head_norm_rope(x, w_pad, cos, sin, seg, segt2, rot2, rope=True):
    rh = lax.rsqrt(_seg_sum(x * x, seg) * (1.0 / DH) + EPS)
    rb = _seg_bcast(rh, segt2)
    y = x * rb
    out = _rope_pairs(y * w_pad, cos, sin, rot2) if rope else None
    return out, y, rb


def _head_norm_rope_bwd(dout, y, rb, w_pad, cos, sin, seg, segt2, rot2_t):
    cos2, sin2 = jnp.concatenate([cos, cos], axis=-1), jnp.concatenate([sin, sin], axis=-1)
    dt = []
    for j in range(H // 2):
        dj = dout[:, 2 * j * LANE:2 * (j + 1) * LANE]
        dt.append(dj * cos2 + _dot_hl(dj * sin2, rot2_t))
    dt = jnp.concatenate(dt, axis=-1)
    dw = _rowsum(dt * y)
    dy = dt * w_pad
    mean_h = _seg_sum(dy * y, seg) * (1.0 / DH)
    return rb * (dy - y * _seg_bcast(mean_h, segt2)), dw


def _q_prep_fwd(qp, qa_w, wuq, wq, cos, sin, cs, *, tm, n_lat, tpe):
    def body(qp_ref, qa_ref, wuq_ref, wq_ref, cos_ref, sin_ref, seg, segt, rot, q_ref):
        x = qp_ref[...]
        cq = (x * _rms(x) * qa_ref[...]).astype(BF16)
        q, _, _ = _head_norm_rope(_mm_nt(cq, wuq_ref[...]), wq_ref[...], cos_ref[...], sin_ref[...],
                                  seg[...], segt[...], rot[...])
        q_ref[...] = q.astype(BF16)

    row = lambda cols: pl.BlockSpec((tm, cols), lambda t: (t, 0))
    tab = pl.BlockSpec((tm, LANE), lambda t: (t % tpe, 0))
    return _pcall(
        body, name="q_prep_fwd", grid=(n_lat,), out_shape=_sds((n_lat * tm, HP), BF16),
        in_specs=[row(QL), _const((1, QL)), _const((HP, QL)), _const((1, HP)), tab, tab,
                  _const((HP, LANE)), _const((2 * LANE, HP)), _const((2 * LANE, 2 * LANE))],
        out_specs=row(HP))(qp, qa_w, wuq, wq, cos, sin, cs["seg_h"], cs["seg_ht"], cs["rot"])


def _q_prep_bwd(dq, qp, qa_w, wuq, wq, cos, sin, cs, *, tm, n_lat, tpe):
    def body(dq_ref, qp_ref, qa_ref, wuq_ref, wq_ref, cos_ref, sin_ref, seg, segt, rot, rot_t,
             dqp_ref, dwuq_ref, dqa_ref, dwq_ref):
        @pl.when(pl.program_id(0) == 0)
        def _():
            dwuq_ref[...] = jnp.zeros_like(dwuq_ref)
            dqa_ref[...] = jnp.zeros_like(dqa_ref)
            dwq_ref[...] = jnp.zeros_like(dwq_ref)

        x = qp_ref[...]
        ra = _rms(x)
        xh = x * ra
        qa = qa_ref[...]
        cq = (xh * qa).astype(BF16)
        wuq_v = wuq_ref[...]
        wq_v, cos_v, sin_v = wq_ref[...], cos_ref[...], sin_ref[...]
        _, y, rb = _head_norm_rope(_mm_nt(cq, wuq_v), wq_v, cos_v, sin_v, seg[...], segt[...], rot[...], rope=False)
        dqraw, dwq = _head_norm_rope_bwd(dq_ref[...], y, rb, wq_v, cos_v, sin_v, seg[...], segt[...], rot_t[...])
        dqraw = dqraw.astype(BF16)
        dcq = _mm(dqraw, wuq_v)
        dxh = dcq * qa
        dqp_ref[...] = (ra * (dxh - xh * jnp.mean(dxh * xh, axis=-1, keepdims=True))).astype(BF16)
        dwuq_ref[...] += _mm_tn(dqraw, cq)
        dqa_ref[...] += _rowsum(dcq * xh)
        dwq_ref[...] += dwq

    row = lambda cols: pl.BlockSpec((tm, cols), lambda t: (t, 0))
    tab = pl.BlockSpec((tm, LANE), lambda t: (t % tpe, 0))
    return _pcall(
        body, name="q_prep_bwd", grid=(n_lat,),
        out_shape=(_sds((n_lat * tm, QL), BF16), _sds((HP, QL), F32), _sds((1, QL), F32), _sds((1, HP), F32)),
        in_specs=[row(HP), row(QL), _const((1, QL)), _const((HP, QL)), _const((1, HP)), tab, tab,
                  _const((HP, LANE)), _const((2 * LANE, HP)), _const((2 * LANE, 2 * LANE)), _const((2 * LANE, 2 * LANE))],
        out_specs=(row(QL), _const((HP, QL)), _const((1, QL)), _const((1, HP))), vmem_mb=40)(
            dq, qp, qa_w, wuq, wq, cos, sin, cs["seg_h"], cs["seg_ht"], cs["rot"], cs["rot_t"])


def _kv_tab_spec(tm, tpe, n_lat):
    return pl.BlockSpec((tm, LANE), lambda t: (jnp.where(t < n_lat, t % tpe, tpe), 0))


def _kv_prep_fwd(ckv, kpe, kva_w, wukv, wk, cosk, sink, cs, *, tm, n_tiles, tpe, n_lat):
    def body(ckv_ref, kpe_ref, kva_ref, wukv_ref, wk_ref, cos_ref, sin_ref, seg, segt, rot, k_ref, v_ref):
        x = ckv_ref[...]
        ckvn = (x * _rms(x) * kva_ref[...]).astype(BF16)
        kv = _mm_nt(ckvn, wukv_ref[...])
        kx = kv[:, :HP] + jnp.concatenate([kpe_ref[...]] * H, axis=-1)
        k, _, _ = _head_norm_rope(kx, wk_ref[...], cos_ref[...], sin_ref[...], seg[...], segt[...], rot[...])
        k_ref[...] = k.astype(BF16)
        v_ref[...] = kv[:, HP:].astype(BF16)

    row = lambda cols: pl.BlockSpec((tm, cols), lambda t: (t, 0))
    tab = _kv_tab_spec(tm, tpe, n_lat)
    r = n_tiles * tm
    return _pcall(
        body, name="kv_prep_fwd", grid=(n_tiles,), out_shape=(_sds((r, HP), BF16), _sds((r, HP), BF16)),
        in_specs=[row(KVL), row(LANE), _const((1, KVL)), _const((2 * HP, KVL)), _const((1, HP)), tab, tab,
                  _const((HP, LANE)), _const((2 * LANE, HP)), _const((2 * LANE, 2 * LANE))],
        out_specs=(row(HP), row(HP)), vmem_mb=40)(
            ckv, kpe, kva_w, wukv, wk, cosk, sink, cs["seg_h"], cs["seg_ht"], cs["rot"])


def _kv_prep_bwd(dks, dvs, ckv, kpe, kva_w, wukv, wk, cosk, sink, cs, *, tm, n_tiles, tpe, n_lat):
    def body(dkl_ref, dkc_ref, dvl_ref, dvc_ref, ckv_ref, kpe_ref, kva_ref, wukv_ref, wk_ref, cos_ref, sin_ref,
             seg, segt, rot, rot_t, dckv_ref, dkpe_ref, dwukv_ref, dkva_ref, dwk_ref):
        t = pl.program_id(0)
        is_lat = t < n_lat

        @pl.when(t == 0)
        def _():
            dwukv_ref[...] = jnp.zeros_like(dwukv_ref)
            dkva_ref[...] = jnp.zeros_like(dkva_ref)
            dwk_ref[...] = jnp.zeros_like(dwk_ref)

        dk = jnp.where(is_lat, dkl_ref[...], dkc_ref[...])
        dv = jnp.where(is_lat, dvl_ref[...], dvc_ref[...])
        x = ckv_ref[...]
        ra = _rms(x)
        xh = x * ra
        kva = kva_ref[...]
        ckvn = (xh * kva).astype(BF16)
        wukv_v = wukv_ref[...]
        wk_v, cos_v, sin_v = wk_ref[...], cos_ref[...], sin_ref[...]
        kv = _mm_nt(ckvn, wukv_v)
        kx = kv[:, :HP] + jnp.concatenate([kpe_ref[...]] * H, axis=-1)
        _, y, rb = _head_norm_rope(kx, wk_v, cos_v, sin_v, seg[...], segt[...], rot[...], rope=False)
        dkx, dwk = _head_norm_rope_bwd(dk, y, rb, wk_v, cos_v, sin_v, seg[...], segt[...], rot_t[...])
        dkpe = dkx[:, 0:LANE]
        for h in range(1, H):
            dkpe = dkpe + dkx[:, h * LANE:(h + 1) * LANE]
        lane = lax.broadcasted_iota(jnp.int32, (tm, LANE), 1)
        dkpe_ref[...] = jnp.where((lane >= DN) & (lane < DH), dkpe, 0.0).astype(BF16)
        dkv = jnp.concatenate([dkx, dv], axis=-1).astype(BF16)
        dckvn = _mm(dkv, wukv_v)
        dxh = dckvn * kva
        dckv_ref[...] = (ra * (dxh - xh * jnp.mean(dxh * xh, axis=-1, keepdims=True))).astype(BF16)
        dwukv_ref[...] += _mm_tn(dkv, ckvn)
        dkva_ref[...] += _rowsum(dckvn * xh)
        dwk_ref[...] += dwk

    row = lambda cols: pl.BlockSpec((tm, cols), lambda t: (t, 0))
    lat = pl.BlockSpec((tm, HP), lambda t: (jnp.minimum(t, n_lat - 1), 0))
    ctx = pl.BlockSpec((tm, HP), lambda t: (jnp.maximum(t - n_lat, 0), 0))
    tab = _kv_tab_spec(tm, tpe, n_lat)
    r = n_tiles * tm
    return _pcall(
        body, name="kv_prep_bwd", grid=(n_tiles,),
        out_shape=(_sds((r, KVL), BF16), _sds((r, LANE), BF16), _sds((2 * HP, KVL), F32), _sds((1, KVL), F32),
                   _sds((1, HP), F32)),
        in_specs=[lat, ctx, lat, ctx, row(KVL), row(LANE), _const((1, KVL)), _const((2 * HP, KVL)), _const((1, HP)),
                  tab, tab, _const((HP, LANE)), _const((2 * LANE, HP)), _const((2 * LANE, 2 * LANE)), _const((2 * LANE, 2 * LANE))],
        out_specs=(row(KVL), row(LANE), _const((2 * HP, KVL)), _const((1, KVL)), _const((1, HP))), vmem_mb=48)(
            dks[0], dks[1], dvs[0], dvs[1], ckv, kpe, kva_w, wukv, wk, cosk, sink,
            cs["seg_h"], cs["seg_ht"], cs["rot"], cs["rot_t"])


_SCALE = DH ** -0.5
_SCALE_LOG2E = _SCALE * 1.4426950408889634


def _key_chunks(s, nc, ck):
    return ([(0, lo, min(lo + ck, s)) for lo in range(0, s, ck)]
            + [(1, lo, min(lo + ck, nc)) for lo in range(0, nc, ck)])


def _attn_fwd(q, k, v, *, nb, s, nc, tq, ck):
    tpe = s // tq
    r_lat = nb * s
    chunks = _key_chunks(s, nc, ck)
    hp = 4

    def body(q_ref, kl_ref, kc_ref, vl_ref, vc_ref, o_ref, lse_ref):
        k_refs, v_refs = (kl_ref, kc_ref), (vl_ref, vc_ref)
        for hh in range(hp):
            hs = slice(hh * LANE, (hh + 1) * LANE)
            qv = q_ref[:, hs]
            xs = [_mm_nt(qv, k_refs[w][lo:hi, hs]) for w, lo, hi in chunks]
            m = jnp.max(xs[0], axis=-1, keepdims=True)
            for x in xs[1:]:
                m = jnp.maximum(m, jnp.max(x, axis=-1, keepdims=True))
            l = acc = None
            for x, (w, lo, hi) in zip(xs, chunks):
                e = jnp.exp2((x - m) * _SCALE_LOG2E)
                lc = jnp.sum(e, axis=-1, keepdims=True)
                pv = _mm(e.astype(BF16), v_refs[w][lo:hi, hs])
                l = lc if l is None else l + lc
                acc = pv if acc is None else acc + pv
            o_ref[:, hs] = (acc / l).astype(BF16)
            lse = m * _SCALE_LOG2E + jnp.log2(l)
            lse_ref[hh] = jnp.transpose(jnp.broadcast_to(lse, (tq, LANE)))[0:8, :]

    qs = pl.BlockSpec((tq, hp * LANE), lambda i, j, t: (i * tpe + t, j))
    kl = pl.BlockSpec((s, hp * LANE), lambda i, j, t: (i, j))
    kc = pl.BlockSpec((nc, hp * LANE), lambda i, j, t: (r_lat // nc + i, j))
    ls = pl.BlockSpec((hp, 8, tq), lambda i, j, t: (i * (H // hp) + j, 0, t))
    return _pcall(body, name="attn_fwd", grid=(nb, H // hp, tpe),
                  out_shape=(_sds((r_lat, HP), BF16), _sds((nb * H, 8, s), F32)),
                  in_specs=[qs, kl, kc, kl, kc], out_specs=(qs, ls), vmem_mb=48)(q, k, k, v, v)


def _attn_bwd(q, k, v, o, do, lse, part, *, nb, s, nc, tq, ck):
    tpe = s // tq
    r_lat = nb * s
    chunks = _key_chunks(s, nc, ck)
    hp = 2
    n_steps = nb * (H // hp) * tpe

    def body(q_ref, kl_ref, kc_ref, vl_ref, vc_ref, o_ref, do_ref, lse_ref, part_ref,
             dq_ref, dkl_ref, dkc_ref, dvl_ref, dvc_ref, recv_ref, akl, akc, avl, avc, send_sems, recv_sems):
        t = pl.program_id(2)
        step = (pl.program_id(0) * (H // hp) + pl.program_id(1)) * tpe + t
        _exchange_behind(part_ref, recv_ref, send_sems, recv_sems, step == 0, step == n_steps - 1)

        @pl.when(t == 0)
        def _():
            akl[...] = jnp.zeros_like(akl)
            akc[...] = jnp.zeros_like(akc)
            avl[...] = jnp.zeros_like(avl)
            avc[...] = jnp.zeros_like(avc)

        k_refs, v_refs, ak, av = (kl_ref, kc_ref), (vl_ref, vc_ref), (akl, akc), (avl, avc)
        for hh in range(hp):
            hs = slice(hh * LANE, (hh + 1) * LANE)
            qv = q_ref[:, hs]
            lse = jnp.transpose(jnp.concatenate([lse_ref[hh]] * (LANE // 8), axis=0))[:, 0:1]
            dov = do_ref[:, hs]
            delta = jnp.sum(dov.astype(F32) * o_ref[:, hs].astype(F32), axis=-1, keepdims=True)
            dq = None
            for w, lo, hi in chunks:
                kc_v = k_refs[w][lo:hi, hs]
                p = jnp.exp2(_mm_nt(qv, kc_v) * _SCALE_LOG2E - lse)
                ds = (p * (_mm_nt(dov, v_refs[w][lo:hi, hs]) - delta)).astype(BF16)
                part = _mm(ds, kc_v)
                dq = part if dq is None else dq + part
                ak[w][hs, lo:hi] += _mm_tn(qv, ds)
                av[w][hs, lo:hi] += _mm_tn(dov, p.astype(BF16))
            dq_ref[:, hs] = dq * _SCALE

        @pl.when(t == tpe - 1)
        def _():
            dkl_ref[...] = akl[...].T * _SCALE
            dkc_ref[...] = akc[...].T * _SCALE
            dvl_ref[...] = avl[...].T
            dvc_ref[...] = avc[...].T

    qs = pl.BlockSpec((tq, hp * LANE), lambda i, j, t: (i * tpe + t, j))
    kl = pl.BlockSpec((s, hp * LANE), lambda i, j, t: (i, j))
    kc = pl.BlockSpec((nc, hp * LANE), lambda i, j, t: (r_lat // nc + i, j))
    kc_out = pl.BlockSpec((nc, hp * LANE), lambda i, j, t: (i, j))
    ls = pl.BlockSpec((hp, 8, tq), lambda i, j, t: (i * (H // hp) + j, 0, t))
    return _pcall(
        body, name="attn_bwd", grid=(nb, H // hp, tpe),
        out_shape=(_sds((r_lat, HP), F32), _sds((r_lat, HP), F32), _sds((nb * nc, HP), F32),
                   _sds((r_lat, HP), F32), _sds((nb * nc, HP), F32), _sds((NDEV - 1,) + part.shape[1:], part.dtype)),
        in_specs=[qs, kl, kc, kl, kc, qs, qs, ls, ANY], out_specs=(qs, kl, kc_out, kl, kc_out, ANY),
        scratch=[pltpu.VMEM((hp * LANE, s), F32), pltpu.VMEM((hp * LANE, nc), F32)] * 2
        + [pltpu.SemaphoreType.DMA((NDEV - 1,))] * 2,
        vmem_mb=60)(q, k, k, v, v, o, do, lse, part)


def _chunks_side_by_side(x, j, nch):
    return jnp.concatenate([x[c * CH:(c + 1) * CH, j * LANE:(j + 1) * LANE] for c in range(nch)], axis=-1)


def _first_group_lanes(nch):
    return (lax.broadcasted_iota(jnp.int32, (CH, nch * LANE), 1) & (LANE - 1)) < GD


def _gating(vn, ws_ref, bias_ref, s_scr, tm):
    nch = tm // CH
    first = _first_group_lanes(nch)
    for j in range(G // 2):
        ls = slice(j * LANE, (j + 1) * LANE)
        vst = _chunks_side_by_side(vn, j, nch)
        st = jnp.where(first, _mm(ws_ref[2 * j], vst), _mm(ws_ref[2 * j + 1], vst))
        for c in range(nch):
            s_scr[c * CH:(c + 1) * CH, ls] = st[:, c * LANE:(c + 1) * LANE] + bias_ref[:, ls]


def _compact_heads(x):
    low = lax.broadcasted_iota(jnp.int32, (x.shape[0], LANE), 1) < DV
    out = []
    for j in range(H // 2):
        even = x[:, 2 * j * LANE:(2 * j + 1) * LANE].astype(F32)
        odd = x[:, (2 * j + 1) * LANE:(2 * j + 2) * LANE].astype(F32)
        out.append(jnp.where(low, even, pltpu.roll(odd, DV, 1)))
    return jnp.concatenate(out, axis=-1)


def _expand_heads(x):
    low = lax.broadcasted_iota(jnp.int32, (x.shape[0], LANE), 1) < DV
    out = []
    for j in range(H // 2):
        blk = x[:, j * LANE:(j + 1) * LANE]
        out.append(jnp.where(low, blk, 0.0))
        out.append(jnp.where(low, pltpu.roll(blk, DV, 1), 0.0))
    return jnp.concatenate(out, axis=-1)


def _mix_fwd(u, v, attn, x1, gate, wv, ws, bias, wout, cs, *, tm, n_lat, tpe):
    nrows = gate.shape[0]

    def body(u_ref, v_ref, attn_ref, x_ref, gate_ref, wv_ref, ws_ref, bias_ref, wout_ref, seg, segt,
             x2_ref, mix_ref, s_scr):
        vg = _gelu(v_ref[...])
        rg = lax.rsqrt(_seg_sum(vg * vg, seg[...]) * (1.0 / GD) + EPS)
        vn = (vg * _seg_bcast(rg, segt[...]) * wv_ref[...]).astype(BF16)
        _gating(vn, ws_ref, bias_ref, s_scr, tm)
        sg = (_gelu(u_ref[...]) * s_scr[...]).astype(BF16)
        attn_c = _compact_heads(attn_ref[...]).astype(BF16)
        mix = _mm(attn_c, wout_ref[0:H * DV, :]) + _mm(sg, wout_ref[H * DV:, :])
        mix_ref[...] = mix.astype(BF16)
        x2_ref[...] = x_ref[...] + gate_ref[0] * mix

    row = lambda cols: pl.BlockSpec((tm, cols), lambda t: (t, 0))
    r = n_lat * tm
    return _pcall(
        body, name="mix_fwd", grid=(n_lat,),
        out_shape=(_sds((r, D), F32), _sds((r, D), BF16)),
        in_specs=[row(G * GD), row(G * GD), row(HP), row(D), _mod_spec(1, tpe, nrows), _const((1, G * GD)),
                  _const((G, CH, CH)), _const((CH, G * GD)), _const((D, D)), _const((G * GD, LANE)),
                  _const((2 * LANE, G * GD))],
        out_specs=(row(D), row(D)), scratch=[pltpu.VMEM((tm, G * GD), F32)], vmem_mb=40)(
            u, v, attn, x1, gate, wv, ws, bias, wout, cs["seg_g"], cs["seg_gt"])


def _mix_bwd(dx2, mix, u, v, attn, gate, wv, ws, wst, bias, wout, cs, *, tm, n_lat, tpe):
    nrows = gate.shape[0]
    wrows = H * DV + G * GD

    def body(dx2_ref, mix_ref, u_ref, v_ref, attn_ref, gate_ref, wv_ref, ws_ref, wst_ref, bias_ref, wout_ref, seg, segt,
             dattn_ref, du_ref, dv_ref, dgate_ref, dwout_ref, dws_ref, dbs_ref, dwv_ref, s_scr, dvn_scr, dbias_scr):
        t = pl.program_id(0)

        @pl.when(t == 0)
        def _():
            dwout_ref[...] = jnp.zeros_like(dwout_ref)
            dws_ref[...] = jnp.zeros_like(dws_ref)
            dwv_ref[...] = jnp.zeros_like(dwv_ref)
            dbias_scr[...] = jnp.zeros_like(dbias_scr)

        @pl.when(t % tpe == 0)
        def _():
            dgate_ref[...] = jnp.zeros_like(dgate_ref)

        dx2 = dx2_ref[...]
        dmix = (dx2 * gate_ref[0]).astype(BF16)
        dcat = _mm_nt(dmix, wout_ref[...])
        dattn_ref[...] = _expand_heads(dcat[:, :H * DV]).astype(BF16)
        dsg = dcat[:, H * DV:]

        vraw = v_ref[...]
        vg = _gelu(vraw)
        rg = lax.rsqrt(_seg_sum(vg * vg, seg[...]) * (1.0 / GD) + EPS)
        r64 = _seg_bcast(rg, segt[...])
        y = vg * r64
        wv_v = wv_ref[...]
        vn = (y * wv_v).astype(BF16)
        _gating(vn, ws_ref, bias_ref, s_scr, tm)
        uraw = u_ref[...]
        ug = _gelu(uraw)
        s = s_scr[...]
        sg = (ug * s).astype(BF16)
        du_ref[...] = (dsg * s * _gelu_grad(uraw)).astype(BF16)
        ds = dsg * ug
        dgate_ref[0] += _rowsum(dx2 * mix_ref[...].astype(F32))
        attn_c = _compact_heads(attn_ref[...]).astype(BF16)
        dwout_ref[...] += _mm_tn(jnp.concatenate([attn_c, sg], axis=-1), dmix)

        nch = tm // CH
        first = _first_group_lanes(nch)
        for c in range(nch):
            dbias_scr[...] += ds[c * CH:(c + 1) * CH, :]
        for j in range(G // 2):
            ls = slice(j * LANE, (j + 1) * LANE)
            dst32 = _chunks_side_by_side(ds, j, nch)
            dst = dst32.astype(BF16)
            vst = _chunks_side_by_side(vn, j, nch)
            dvn_st = jnp.where(first, _mm(wst_ref[2 * j], dst), _mm(wst_ref[2 * j + 1], dst))
            for c in range(nch):
                dvn_scr[c * CH:(c + 1) * CH, ls] = dvn_st[:, c * LANE:(c + 1) * LANE]
            dws_ref[2 * j] += _mm_nt(jnp.where(first, dst32, 0.0).astype(BF16), vst)
            dws_ref[2 * j + 1] += _mm_nt(jnp.where(first, 0.0, dst32).astype(BF16), vst)

        dvn = dvn_scr[...]
        dwv_ref[...] += _rowsum(dvn * y)
        dy = dvn * wv_v
        mean_g = _seg_sum(dy * y, seg[...]) * (1.0 / GD)
        dvg = r64 * (dy - y * _seg_bcast(mean_g, segt[...]))
        dv_ref[...] = (dvg * _gelu_grad(vraw)).astype(BF16)

        @pl.when(t == n_lat - 1)
        def _():
            dbs_ref[...] = _dot_hl(dbias_scr[...], seg[...])

    row = lambda cols: pl.BlockSpec((tm, cols), lambda t: (t, 0))
    r = n_lat * tm
    return _pcall(
        body, name="mix_bwd", grid=(n_lat,),
        out_shape=(_sds((r, HP), BF16), _sds((r, G * GD), BF16), _sds((r, G * GD), BF16), _sds((nrows, 1, D), F32),
                   _sds((wrows, D), F32), _sds((G, CH, CH), F32), _sds((CH, LANE), F32), _sds((1, G * GD), F32)),
        in_specs=[row(D), row(D), row(G * GD), row(G * GD), row(HP), _mod_spec(1, tpe, nrows), _const((1, G * GD)),
                  _const((G, CH, CH)), _const((G, CH, CH)), _const((CH, G * GD)), _const((wrows, D)),
                  _const((G * GD, LANE)), _const((2 * LANE, G * GD))],
        out_specs=(row(HP), row(G * GD), row(G * GD), _mod_spec(1, tpe, nrows), _const((wrows, D)),
                   _const((G, CH, CH)), _const((CH, LANE)), _const((1, G * GD))),
        scratch=[pltpu.VMEM((tm, G * GD), F32), pltpu.VMEM((tm, G * GD), F32), pltpu.VMEM((CH, G * GD), F32)],
        vmem_mb=56)(dx2, mix, u, v, attn, gate, wv, ws, wst, bias, wout, cs["seg_g"], cs["seg_gt"])


def _adamw_math(w, g, m, v):
    m2 = ADAM_B1 * m + (1.0 - ADAM_B1) * g
    v2 = ADAM_B2 * v + (1.0 - ADAM_B2) * (g * g)
    m_hat = m2 / (1.0 - ADAM_B1 ** ADAM_STEP)
    v_hat = v2 / (1.0 - ADAM_B2 ** ADAM_STEP)
    delta = -ADAM_LR * (m_hat / (jnp.sqrt(v_hat) + ADAM_EPS) + ADAM_WD * w)
    return delta, m2, v2


def _row_tile(r, c):
    best = r
    for tr in range(8, r, 8):
        if r % tr == 0 and tr * c * 4 <= MIB:
            best = tr
    return best


def _adamw(w, g, m, v, name):
    r, c = w.shape
    tr = _row_tile(r, c)

    def body(w_ref, g_ref, m_ref, v_ref, d_ref, mo_ref, vo_ref):
        d_ref[...], mo_ref[...], vo_ref[...] = _adamw_math(w_ref[...], g_ref[...], m_ref[...], v_ref[...])

    blk = pl.BlockSpec((tr, c), lambda t: (t, 0))
    return _pcall(body, name=name, grid=(r // tr,), out_shape=(_sds((r, c), F32),) * 3,
                  in_specs=[blk] * 4, out_specs=(blk,) * 3)(w, g, m, v)


def _adamw_small(params):
    n = len(params)

    def body(*refs):
        ins, outs = refs[:4 * n], refs[4 * n:]
        for i in range(n):
            w, g, m, v = (ins[4 * i + k][...] for k in range(4))
            if i == 0:
                sig = _sigmoid(w)
                g = g * (sig * (1.0 + w * (1.0 - sig)))
            d, m2, v2 = _adamw_math(w, g, m, v)
            outs[4 * i][...] = g
            outs[4 * i + 1][...] = d
            outs[4 * i + 2][...] = m2
            outs[4 * i + 3][...] = v2

    flat = [a for p in params for a in p]
    out_shape = tuple(_sds(p[0].shape, F32) for p in params for _ in range(4))
    res = _pcall(body, name="adamw_small", out_shape=out_shape, in_specs=[VMEM] * (4 * n),
                 out_specs=(VMEM,) * (4 * n))(*flat)
    return [res[4 * i:4 * i + 4] for i in range(n)]


def _rope_tables(s):
    rows = jnp.repeat(jnp.arange(s // GRID_W, dtype=F32), GRID_W)
    cols = jnp.tile(jnp.arange(GRID_W, dtype=F32), s // GRID_W)
    half = DR // 2
    inv = ROPE_BASE ** (-jnp.arange(0, half, 2, dtype=F32) / half)
    ang_r = rows[:, None] * inv
    ang_c = cols[:, None] * inv
    ang = jnp.concatenate([ang_r, ang_r, ang_c, ang_c], axis=-1)
    return jnp.cos(ang), jnp.sin(ang)


def _head_pad(a, real):
    return jnp.pad(a, ((0, 0), (0, LANE - real), (0, 0))).reshape(HP, a.shape[2])


def kernel(x, c, ctx, c_ctx, w_ada, b_ada, norm1_w, ffn1_w1, ffn1_w3, ffn1_w2, norm2_w, w_in, q_a_norm_w, w_uq, kv_a_norm_w, w_ukv, q_norm_w, k_norm_w, v_norm_w, w_s, b_s, w_out, norm3_w, ffn2_w1, ffn2_w3, ffn2_w2, loss_target, m_c_ctx, m_w_ada, m_b_ada, m_norm1_w, m_ffn1_w1, m_ffn1_w3, m_ffn1_w2, m_norm2_w, m_w_in, m_q_a_norm_w, m_w_uq, m_kv_a_norm_w, m_w_ukv, m_q_norm_w, m_k_norm_w, m_v_norm_w, m_w_s, m_b_s, m_w_out, m_norm3_w, m_ffn2_w1, m_ffn2_w3, m_ffn2_w2, v_c_ctx, v_w_ada, v_b_ada, v_norm1_w, v_ffn1_w1, v_ffn1_w3, v_ffn1_w2, v_norm2_w, v_w_in, v_q_a_norm_w, v_w_uq, v_kv_a_norm_w, v_w_ukv, v_q_norm_w, v_k_norm_w, v_v_norm_w, v_w_s, v_b_s, v_w_out, v_norm3_w, v_ffn2_w1, v_ffn2_w3, v_ffn2_w2):
    nb, s, _ = x.shape
    nc = ctx.shape[1]
    tm = 256 if nc % 256 == 0 else 128
    tpe = s // tm
    n_lat = nb * tpe
    n_all = n_lat + nb * nc // tm
    tmf = 2 * tm if s % (2 * tm) == 0 and (nb * nc) % (2 * tm) == 0 else tm
    tp = tmf
    r_lat = nb * s
    tpe_p, n_lat_p, n_all_p = s // tp, r_lat // tp, (r_lat + nb * nc) // tp
    me = 4 * lax.axis_index("x") + 2 * lax.axis_index("y") + lax.axis_index("c")
    cs = _consts()
    ncol = w_ada.shape[2]
    fsh = ffn1_w1.shape[2]
    assert nb + 1 <= 8 and NDEV * fsh == FF and NDEV * ncol == NMOD * D and s % nc == 0 and nc % tm == 0

    def t16(a):
        return a.T.astype(BF16)

    wpack1 = jnp.concatenate([t16(ffn1_w1[0]), t16(ffn1_w3[0]), ffn1_w2[0].astype(BF16)], axis=0)
    a_loc = jnp.concatenate([c, c_ctx[None, :], jnp.zeros((7 - nb, D), F32)], axis=0)
    a_raw, _, mod_all, wall1 = _ada_front(a_loc, w_ada[0], lax.dynamic_slice_in_dim(b_ada, me * ncol, ncol, axis=1),
                                          wpack1)
    a_raw = a_raw.reshape(NDEV * 8, D)
    mod_mine = lax.dynamic_slice_in_dim(mod_all, 8 * me, 8, axis=1)
    modtab = mod_mine.transpose(1, 0, 2).reshape(8, NMOD, D)[:nb + 1]
    wpack2 = jnp.concatenate([
        t16(ffn2_w1[0]), t16(ffn2_w3[0]), ffn2_w2[0].astype(BF16),
        t16(w_in[0]), jnp.zeros((12, D), BF16),
        w_out[0].astype(BF16),
        t16(w_uq[0]).reshape(24, D), jnp.zeros((8, D), BF16),
        t16(w_ukv[0]).reshape(16, D)], axis=0)

    def head_w(wn):
        return jnp.tile(jnp.pad(wn, ((0, 0), (0, LANE - DH))), (1, H))

    wq, wk = head_w(q_norm_w), head_w(k_norm_w)
    wv = v_norm_w.reshape(1, G * GD)
    ws16 = w_s[0].astype(BF16)
    wst16 = w_s[0].transpose(0, 2, 1).astype(BF16)
    bias = jnp.repeat(b_s[0].T, GD, axis=1)
    cos, sin = _rope_tables(s)
    cos = jnp.pad(cos, ((0, 0), (DN, LANE - DH)), constant_values=1.0)
    sin = jnp.pad(sin, ((0, 0), (DN, LANE - DH)))
    cos_k = jnp.concatenate([cos, jnp.ones((tm, LANE), F32)], axis=0)
    sin_k = jnp.concatenate([sin, jnp.zeros((tm, LANE), F32)], axis=0)

    xs = (x.reshape(r_lat, D), ctx.reshape(nb * nc, D))
    x1, a1, b1, o1, wall2 = _ffn_fwd(xs, modtab[:, 0:3], norm1_w, wall1, 0, tm=tmf, n_tiles=(r_lat + nb * nc) // tmf,
                                     tpe=s // tmf, n_lat=r_lat // tmf, name="ffn1_fwd", gather=wpack2)

    o0 = 3 * fsh
    wint = wall2[:, o0:o0 + 180].reshape(IN_COLS, D)
    z = lambda n: jnp.zeros((n, D), BF16)
    wint = jnp.concatenate([wint[0:128], wint[160:416], wint[416:928], wint[928:1440],
                            z(DN), wint[128:160], z(LANE - DH)], axis=0)
    wout = wall2[:, o0 + 192:o0 + 320].reshape(D, D)
    wuq = _head_pad(wall2[:, o0 + 320:o0 + 344].reshape(H, DH, QL), DH)
    wukvt = wall2[:, o0 + 352:o0 + 368].reshape(H, DN + DV, KVL)
    wukv = jnp.concatenate([_head_pad(wukvt[:, :DN], DN), _head_pad(wukvt[:, DN:], DV)], axis=0)

    ckv, qp, u_raw, v_raw, kpe = _proj_fwd(x1, modtab[:, 3:5], norm2_w, wint, tm=tp, n_tiles=n_all_p, tpe=tpe_p)
    q = _q_prep_fwd(qp, q_a_norm_w, wuq, wq, cos, sin, cs, tm=tm, n_lat=n_lat, tpe=tpe)
    k, v = _kv_prep_fwd(ckv, kpe, kv_a_norm_w, wukv, wk, cos_k, sin_k, cs,
                        tm=tm, n_tiles=n_all, tpe=tpe, n_lat=n_lat)
    attn, lse = _attn_fwd(q, k, v, nb=nb, s=s, nc=nc, tq=tm, ck=2048)
    x2, mix = _mix_fwd(u_raw, v_raw, attn, x1, modtab[:nb, 5:6], wv, ws16, bias, wout, cs,
                       tm=tp, n_lat=n_lat_p, tpe=tpe_p)
    dy, a2, b2, o2, lsum = _ffn_fwd((x2,), modtab[:nb, 6:9], norm3_w, wall2, 0, tm=tmf, n_tiles=r_lat // tmf,
                                    tpe=s // tmf, n_lat=r_lat // tmf, name="ffn2_fwd",
                                    target=loss_target.reshape(r_lat, D))

    tr = 2 * tm if n_lat % 2 == 0 and n_all % 2 == 0 else tm
    dx2, da2, db2, g2, do2, h2, dmod678, dnorm3 = _ffn_bwd_dx(
        dy, (x2,), a2, b2, o2, modtab[:nb, 6:9], norm3_w, wall2, 0,
        tm=tm, n_tiles=n_lat, tpe=tpe, n_lat=n_lat, name="ffn2_bwd_dx")
    g_ffn2 = _ffn_bwd_dw(h2, do2, da2, db2, g2, tr=tr, name="ffn2_bwd_dw")

    dattn, du, dv, dgate5, dwout, dws, dbs, dwv = _mix_bwd(
        dx2, mix, u_raw, v_raw, attn, modtab[:nb, 5:6], wv, ws16, wst16, bias, wout, cs, tm=tp, n_lat=n_lat_p, tpe=tpe_p)
    tq = 2 * tm if s % (2 * tm) == 0 else tm
    dq, dk_l, dk_c, dv_l, dv_c, recv_ffn2 = _attn_bwd(q, k, v, attn, dattn, lse, g_ffn2,
                                                      nb=nb, s=s, nc=nc, tq=tq, ck=1024)
    dqp, dwuq, dqa, dwq = _q_prep_bwd(dq, qp, q_a_norm_w, wuq, wq, cos, sin, cs, tm=tm, n_lat=n_lat, tpe=tpe)
    dckv, dkpe, dwukv, dkva, dwk = _kv_prep_bwd((dk_l, dk_c), (dv_l, dv_c), ckv, kpe, kv_a_norm_w, wukv, wk,
                                                cos_k, sin_k, cs, tm=tm, n_tiles=n_all, tpe=tpe, n_lat=n_lat)
    dx1, dwin, dmod34, dnorm2 = _proj_bwd(dckv, dkpe, dqp, du, dv, dx2, x1, modtab[:, 3:5], norm2_w, wint,
                                          tm=tp, n_tiles=n_all_p, tpe=tpe_p, n_lat=n_lat_p)

    def blocks(a):
        return a.reshape(NDEV, a.shape[0] // NDEV, D)

    dwin_o = jnp.concatenate([dwin[0:128], dwin[KPE_LO:KPE_LO + DR], dwin[128:384], dwin[384:896], dwin[896:1408]],
                             axis=0)
    dwuq_o = dwuq.reshape(H, LANE, QL)[:, :DH]
    dwukv_o = jnp.concatenate([dwukv[:HP].reshape(H, LANE, KVL)[:, :DN], dwukv[HP:].reshape(H, LANE, KVL)[:, :DV]],
                              axis=1)
    gmisc = jnp.concatenate([
        blocks(dwin_o).astype(BF16), jnp.zeros((NDEV, 12, D), BF16),
        blocks(dwout).astype(BF16),
        dwuq_o.reshape(NDEV, 24, D).astype(BF16), jnp.zeros((NDEV, 8, D), BF16),
        dwukv_o.reshape(NDEV, 16, D).astype(BF16)], axis=1)

    dx0, da1, db1, g1, do1, h1, dmod012, dnorm1 = _ffn_bwd_dx(
        dx1, xs, a1, b1, o1, modtab[:, 0:3], norm1_w, wall1, 0,
        tm=tm, n_tiles=n_all, tpe=tpe, n_lat=n_lat, name="ffn1_bwd_dx")
    grad_x = dx0.reshape(nb, s, D)
    g_w1, recv_misc = _ffn_bwd_dw_one(da1, h1, tr=tr, name="ffn1_bwd_dw1", part=gmisc)
    g_w3, recv_w1 = _ffn_bwd_dw_one(db1, h1, tr=tr, name="ffn1_bwd_dw3", part=g_w1)
    g_w2, recv_w3 = _ffn_bwd_dw_one(g1, do1, tr=tr, name="ffn1_bwd_dw2", part=g_w3)

    zrow = jnp.zeros((1, D), F32)
    g_lat = jnp.concatenate([dmod012[:nb, 0], dmod012[:nb, 1], dmod012[:nb, 2], dmod34[:nb, 0], dmod34[:nb, 1],
                             dgate5[:, 0], dmod678[:, 0], dmod678[:, 1], dmod678[:, 2]], axis=1)
    g_ctx = jnp.concatenate([dmod012[nb:, 0], dmod012[nb:, 1], dmod012[nb:, 2], dmod34[nb:, 0], dmod34[nb:, 1],
                             zrow, zrow, zrow, zrow], axis=1)
    g_loc = jnp.concatenate([g_lat, g_ctx, jnp.zeros((7 - nb, NMOD * D), F32)], axis=0)

    got_w2, g_all = _scatter_sibling([g_w2], "scatter_sibling_w2", gather=g_loc)
    g_all = g_all.reshape(NDEV * 8, NMOD * D)
    g_cols = lax.dynamic_slice_in_dim(g_all, me * ncol, ncol, axis=1)
    g_w_ada, pc_ctx, g_b_ada = _ada_bwd(a_raw, c_ctx.reshape(D, 1), g_all, g_cols, w_ada[0], nb)
    part_w2 = _add_sibling(g_w2, got_w2, 176, "add_sibling_w2")

    def prow(a):
        a = a.reshape(1, -1)
        return jnp.concatenate([a, jnp.zeros((1, D - a.shape[1]), F32)], axis=1)

    g_qn = dwq.reshape(H, LANE)[:, :DH].sum(0)
    g_kn = dwk.reshape(H, LANE)[:, :DH].sum(0)
    spack = jnp.concatenate([
        dnorm1, dnorm2, dnorm3, prow(dqa), prow(dkva), prow(g_qn), prow(g_kn), prow(dwv),
        prow(dbs[:, :G].T), prow(pc_ctx), prow(lsum[0:1]), jnp.zeros((5, D), F32), dws.reshape(CH, D)],
        axis=0)
    recv_w2, small_all = _scatter_chips([part_w2], "scatter_chips", gather=spack)
    ssum = _sum_slots(small_all, 144, "sum_small")
    loss = ssum[10, 0] * (0.5 / D)
    gsum2 = _sum_direct(g_ffn2, recv_ffn2, 176, "sum_grads_ffn2")
    msum = _sum_direct(gmisc, recv_misc, 368, "sum_grads_misc")

    transposed = ("ffn1_w1", "ffn1_w3", "ffn2_w1", "ffn2_w3", "w_in", "w_uq")
    g_big = {
        "ffn1_w1": _sum_direct(g_w1, recv_w1, 176, "sum_grads_w1"),
        "ffn1_w3": _sum_direct(g_w3, recv_w3, 176, "sum_grads_w3"),
        "ffn1_w2": _sum_chips(part_w2, recv_w2, 176, "sum_grads_w2"),
        "ffn2_w1": gsum2[0:fsh], "ffn2_w3": gsum2[fsh:2 * fsh], "ffn2_w2": gsum2[2 * fsh:3 * fsh],
        "w_in": msum[0:180], "w_out": msum[192:320],
        "w_uq": msum[320:344].reshape(DH, QL), "w_ukv": msum[352:368].reshape(DN + DV, KVL).T,
        "w_ada": g_w_ada,
    }

    big_in = {
        "w_ada": (w_ada, m_w_ada, v_w_ada), "ffn1_w1": (ffn1_w1, m_ffn1_w1, v_ffn1_w1),
        "ffn1_w3": (ffn1_w3, m_ffn1_w3, v_ffn1_w3), "ffn1_w2": (ffn1_w2, m_ffn1_w2, v_ffn1_w2),
        "w_in": (w_in, m_w_in, v_w_in), "w_uq": (w_uq, m_w_uq, v_w_uq), "w_ukv": (w_ukv, m_w_ukv, v_w_ukv),
        "w_out": (w_out, m_w_out, v_w_out), "ffn2_w1": (ffn2_w1, m_ffn2_w1, v_ffn2_w1),
        "ffn2_w3": (ffn2_w3, m_ffn2_w3, v_ffn2_w3), "ffn2_w2": (ffn2_w2, m_ffn2_w2, v_ffn2_w2),
    }
    res = {}
    for nm, (w, m, v_) in big_in.items():
        g = g_big[nm]
        if nm in transposed:
            d_, m_, v2_ = _adamw(w[0].T, g, m[0].T, v_[0].T, "adamw_" + nm)
            res[nm] = tuple(a.T[None] for a in (g, d_, m_, v2_))
        else:
            d_, m_, v2_ = _adamw(w[0], g, m[0], v_[0], "adamw_" + nm)
            res[nm] = tuple(a[None] for a in (g, d_, m_, v2_))

    small_in = [
        ("c_ctx", c_ctx, m_c_ctx, v_c_ctx, ssum[9:10], (1, D)),
        ("b_ada", b_ada, m_b_ada, v_b_ada, g_b_ada, (1, NMOD * D)),
        ("norm1_w", norm1_w, m_norm1_w, v_norm1_w, ssum[0:1], (1, D)),
        ("norm2_w", norm2_w, m_norm2_w, v_norm2_w, ssum[1:2], (1, D)),
        ("norm3_w", norm3_w, m_norm3_w, v_norm3_w, ssum[2:3], (1, D)),
        ("q_a_norm_w", q_a_norm_w, m_q_a_norm_w, v_q_a_norm_w, ssum[3:4, :QL], (1, QL)),
        ("kv_a_norm_w", kv_a_norm_w, m_kv_a_norm_w, v_kv_a_norm_w, ssum[4:5, :KVL], (1, KVL)),
        ("q_norm_w", q_norm_w, m_q_norm_w, v_q_norm_w, ssum[5:6, :DH], (1, DH)),
        ("k_norm_w", k_norm_w, m_k_norm_w, v_k_norm_w, ssum[6:7, :DH], (1, DH)),
        ("v_norm_w", v_norm_w, m_v_norm_w, v_v_norm_w, ssum[7:8, :G * GD], (G, GD)),
        ("b_s", b_s, m_b_s, v_b_s, ssum[8:9], (G, CH)),
        ("w_s", w_s, m_w_s, v_w_s, ssum[16:144], (G * CH, CH)),
    ]
    small_out = _adamw_small(
        [(w.reshape(sh), g.reshape(sh), m.reshape(sh), v_.reshape(sh)) for _, w, m, v_, g, sh in small_in])
    for (nm, w, *_), outs in zip(small_in, small_out):
        res[nm] = tuple(a.reshape(w.shape) for a in outs)

    order = ["c_ctx", "w_ada", "b_ada", "norm1_w", "ffn1_w1", "ffn1_w3", "ffn1_w2", "norm2_w", "w_in", "q_a_norm_w",
             "w_uq", "kv_a_norm_w", "w_ukv", "q_norm_w", "k_norm_w", "v_norm_w", "w_s", "b_s", "w_out", "norm3_w",
             "ffn2_w1", "ffn2_w3", "ffn2_w2"]
    return (loss, grad_x, *[res[n][0] for n in order], *[res[n][1] for n in order],
            *[res[n][2] for n in order], *[res[n][3] for n in order])
```

```python
import numpy as np
import jax
import jax.numpy as jnp
from jax import lax
from jax.experimental import pallas as pl
from jax.experimental.pallas import tpu as pltpu

F32 = jnp.float32
BF16 = jnp.bfloat16

D = 1024
FF = 2816
FC = 256
H = 8
DN, DR, DV = 64, 32, 64
DH = DN + DR
QL, KVL = 256, 128
G, GD, CH = 8, 64, 128
NMOD = 9
EPS = 1e-6
GRID_W = 64
ROPE_BASE = 10000.0
NDEV = 8
LANE = 128
HP = H * LANE
IN_COLS = 1440
WIN_ROWS = 1536
KPE_LO = 1408 + DN
MIB = 1 << 20

ADAM_LR, ADAM_B1, ADAM_B2, ADAM_EPS, ADAM_WD, ADAM_STEP = 0.001, 0.9, 0.999, 1e-08, 0.01, 10

MESH = pl.DeviceIdType.MESH
ANY = pl.BlockSpec(memory_space=pl.ANY)
VMEM = pl.BlockSpec(memory_space=pltpu.VMEM)


def _mm(a, b):
    return jnp.dot(a, b, preferred_element_type=F32)


def _mm_nt(a, b):
    return lax.dot_general(a, b, (((1,), (1,)), ((), ())), preferred_element_type=F32)


def _mm_tn(a, b):
    return lax.dot_general(a, b, (((0,), (0,)), ((), ())), preferred_element_type=F32)


def _dot_hl(x, m):
    hi = x.astype(BF16)
    lo = (x - hi.astype(F32)).astype(BF16)
    return _mm(hi, m) + _mm(lo, m)


def _sigmoid(a):
    return 1.0 / (1.0 + jnp.exp(-a))


_G0 = 0.7978845608028654
_G1 = 0.044715


def _gelu(x):
    return 0.5 * x * (1.0 + jnp.tanh(_G0 * (x + _G1 * (x * x * x))))


def _gelu_grad(x):
    th = jnp.tanh(_G0 * (x + _G1 * (x * x * x)))
    return 0.5 * (1.0 + th) + 0.5 * x * (1.0 - th * th) * (_G0 * (1.0 + 3.0 * _G1 * x * x))


def _rowsum(y):
    return jnp.sum(y, axis=0, keepdims=True)


def _rms(x):
    return lax.rsqrt(jnp.mean(x * x, axis=-1, keepdims=True) + EPS)


def _pcall(body, *, name, out_shape, in_specs, out_specs, grid=None, scratch=(), vmem_mb=32, aliases=None):
    kw = {}
    if grid is not None:
        kw["grid"] = grid
        sem = ("arbitrary",) * len(grid)
    else:
        sem = None
    if aliases:
        kw["input_output_aliases"] = aliases
    return pl.pallas_call(
        body, name=name, out_shape=out_shape, in_specs=in_specs, out_specs=out_specs,
        scratch_shapes=list(scratch),
        compiler_params=pltpu.CompilerParams(dimension_semantics=sem, vmem_limit_bytes=vmem_mb * MIB),
        **kw)


def _const(shape):
    nd = len(shape)
    return pl.BlockSpec(shape, lambda *_: (0,) * nd)


def _sds(shape, dt):
    return jax.ShapeDtypeStruct(shape, dt)


def _consts():
    seg_h = np.zeros((HP, LANE), np.float32)
    seg_h[np.arange(HP), np.arange(HP) // LANE] = 1.0
    seg_g = np.zeros((G * GD, LANE), np.float32)
    seg_g[np.arange(G * GD), np.arange(G * GD) // GD] = 1.0
    rot = np.zeros((LANE, LANE), np.float32)
    for base in (DN, DN + 16):
        for j in range(8):
            rot[base + j + 8, base + j] = -1.0
            rot[base + j, base + j + 8] = 1.0
    rot2 = np.zeros((2 * LANE, 2 * LANE), np.float32)
    rot2[:LANE, :LANE] = rot
    rot2[LANE:, LANE:] = rot
    twice = lambda m: np.concatenate([m, m], axis=0)
    c = dict(seg_h=seg_h, seg_ht=twice(seg_h.T), seg_g=seg_g, seg_gt=twice(seg_g.T), rot=rot2, rot_t=rot2.T)
    return {k: jnp.asarray(v, BF16) for k, v in c.items()}


_GATHER_SEMS = [pltpu.SemaphoreType.DMA((7,)), pltpu.SemaphoreType.DMA((7,)), pltpu.SemaphoreType.DMA(())]


def _gather_phases(x_ref, out_ref, send_sems, recv_sems, local_sem):
    mx, my, mc = lax.axis_index("x"), lax.axis_index("y"), lax.axis_index("c")
    me, sibling = (mx, my, mc), (mx, my, 1 - mc)
    chips = [(1 - mx, my), (mx, 1 - my), (1 - mx, 1 - my)]

    def blk(px, py, pc):
        return out_ref.at[4 * px + 2 * py + pc]

    def copy(k, block, to, src=None):
        return pltpu.make_async_remote_copy(
            src_ref=blk(*block) if src is None else src, dst_ref=blk(*block),
            send_sem=send_sems.at[k], recv_sem=recv_sems.at[k], device_id=to, device_id_type=MESH)

    mine = pltpu.make_async_copy(x_ref, blk(*me), local_sem)
    first = [copy(0, me, sibling, src=x_ref)]
    first += [copy(1 + j, me, (*chip, mc), src=x_ref) for j, chip in enumerate(chips)]
    passed = [copy(4 + j, (*chip, mc), sibling) for j, chip in enumerate(chips)]

    def start():
        mine.start()
        for cp in first:
            cp.start()

    def forward():
        for j, chip in enumerate(chips):
            copy(1 + j, (*chip, mc), me).wait_recv()
            passed[j].start()

    def finish():
        copy(0, sibling, me).wait_recv()
        for j, chip in enumerate(chips):
            copy(4 + j, (*chip, 1 - mc), me).wait_recv()
        for cp in first + passed:
            cp.wait_send()
        mine.wait()

    return start, forward, finish


def _chip_sends(p_ref, out_ref, send_sems, recv_sems):
    mx, my, mc = lax.axis_index("x"), lax.axis_index("y"), lax.axis_index("c")
    peers = [(1 - mx, my), (mx, 1 - my), (1 - mx, 1 - my)]
    return [pltpu.make_async_remote_copy(
        src_ref=p_ref.at[2 * px + py], dst_ref=out_ref.at[j], send_sem=send_sems.at[j], recv_sem=recv_sems.at[j],
        device_id=(px, py, mc), device_id_type=MESH) for j, (px, py) in enumerate(peers)]


def _with_gather(copies_of, n, shapes, sems, gather, name, args):
    ns = len(sems)

    def body(*refs):
        ng = 1 if gather is not None else 0
        ins, outs = refs[:n], refs[n + ng:2 * n + ng]
        copies = copies_of(ins, outs, refs[2 * n + 2 * ng:2 * n + 2 * ng + ns])
        if ng:
            start, forward, finish = _gather_phases(refs[n], refs[2 * n + 1], *refs[2 * n + 2 + ns:])
            start()
        for cp in copies:
            cp.start()
        if ng:
            forward()
        for cp in copies:
            cp.wait_recv()
        for cp in copies:
            cp.wait_send()
        if ng:
            finish()

    in_specs, out_shape, scratch = [ANY] * n, list(shapes), list(sems)
    if gather is not None:
        in_specs.append(ANY)
        args = list(args) + [gather]
        out_shape.append(_sds((NDEV,) + gather.shape, gather.dtype))
        scratch += _GATHER_SEMS
    return pl.pallas_call(body, name=name, out_shape=tuple(out_shape), in_specs=in_specs,
                          out_specs=(ANY,) * len(out_shape), scratch_shapes=scratch)(*args)


def _scatter_sibling(xs, name, gather=None):
    n = len(xs)

    def copies_of(x_refs, got_refs, sems):
        send_sems, recv_sems = sems
        mx, my, mc = lax.axis_index("x"), lax.axis_index("y"), lax.axis_index("c")
        return [pltpu.make_async_remote_copy(
            src_ref=x_refs[i].at[2 * j + 1 - mc], dst_ref=got_refs[i].at[j],
            send_sem=send_sems.at[4 * i + j], recv_sem=recv_sems.at[4 * i + j],
            device_id=(mx, my, 1 - mc), device_id_type=MESH) for i in range(n) for j in range(4)]

    shapes = tuple(_sds((4,) + x.shape[1:], x.dtype) for x in xs)
    return _with_gather(copies_of, n, shapes, [pltpu.SemaphoreType.DMA((4 * n,))] * 2, gather, name, xs)


def _scatter_chips(ps, name, gather=None):
    n = len(ps)

    def copies_of(p_refs, out_refs, sems):
        sends = []
        for i in range(n):
            sends += _chip_sends(p_refs[i], out_refs[i], sems[2 * i], sems[2 * i + 1])
        return sends

    shapes = tuple(_sds((3,) + p.shape[1:], p.dtype) for p in ps)
    return _with_gather(copies_of, n, shapes, [pltpu.SemaphoreType.DMA((3,))] * (2 * n), gather, name, ps)


def _add_sibling(x, got, tr, name):
    _, r, c = x.shape

    def body(x_ref, g_ref, o_ref):
        mc = lax.axis_index("c")
        for j in range(4):
            mine = jnp.where(mc == 0, x_ref[2 * j].astype(F32), x_ref[2 * j + 1].astype(F32))
            o_ref[j] = (mine + g_ref[j].astype(F32)).astype(o_ref.dtype)

    return _pcall(body, name=name, grid=(r // tr,), out_shape=_sds(got.shape, got.dtype),
                  in_specs=[pl.BlockSpec((NDEV, tr, c), lambda t: (0, t, 0)), pl.BlockSpec((4, tr, c), lambda t: (0, t, 0))],
                  out_specs=pl.BlockSpec((4, tr, c), lambda t: (0, t, 0)))(x, got)


def _sum_chips(part, recv, tr, name):
    _, r, c = part.shape

    def body(p_ref, r_ref, o_ref):
        slot = 2 * lax.axis_index("x") + lax.axis_index("y")
        acc = p_ref[0].astype(F32)
        for j in range(1, 4):
            acc = jnp.where(slot == j, p_ref[j].astype(F32), acc)
        for j in range(3):
            acc = acc + r_ref[j].astype(F32)
        o_ref[...] = acc

    return _pcall(body, name=name, grid=(r // tr,), out_shape=_sds((r, c), F32),
                  in_specs=[pl.BlockSpec((4, tr, c), lambda t: (0, t, 0)), pl.BlockSpec((3, tr, c), lambda t: (0, t, 0))],
                  out_specs=pl.BlockSpec((tr, c), lambda t: (t, 0)))(part, recv)


def _sum_slots(x, tr, name):
    n, r, c = x.shape

    def body(x_ref, o_ref):
        acc = x_ref[0].astype(F32)
        for s in range(1, n):
            acc = acc + x_ref[s].astype(F32)
        o_ref[...] = acc

    return _pcall(body, name=name, grid=(r // tr,), out_shape=_sds((r, c), F32),
                  in_specs=[pl.BlockSpec((n, tr, c), lambda t: (0, t, 0))],
                  out_specs=pl.BlockSpec((tr, c), lambda t: (t, 0)))(x)


def _ada_front(a_loc, w_loc, b_loc, wpack):
    ncol = w_loc.shape[1]
    nrow = NDEV * a_loc.shape[0]

    def body(a_ref, w_ref, b_ref, wp_ref, araw_ref, mloc_ref, mall_ref, wall_ref,
             a_vm, w_vm, m_vm, lsem, *sems):
        a_start, a_forward, a_finish = _gather_phases(a_ref, araw_ref, *sems[0:3])
        m_start, m_forward, m_finish = _gather_phases(mloc_ref, mall_ref, *sems[3:6])
        w_start, w_forward, w_finish = _gather_phases(wp_ref, wall_ref, *sems[6:9])
        w_in = pltpu.make_async_copy(w_ref, w_vm, lsem.at[0])
        w_in.start()
        a_start()
        w_start()
        a_forward()
        a_finish()
        a_in = pltpu.make_async_copy(araw_ref, a_vm, lsem.at[1])
        a_in.start()
        a_in.wait()
        w_in.wait()
        a = a_vm[...].reshape(nrow, D)
        act = (a * _sigmoid(a)).astype(BF16)
        m_vm[...] = _mm(act, w_vm[...].astype(BF16)) + b_ref[...]
        m_out = pltpu.make_async_copy(m_vm, mloc_ref, lsem.at[2])
        m_out.start()
        m_out.wait()
        m_start()
        m_forward()
        m_finish()
        w_forward()
        w_finish()

    return pl.pallas_call(
        body, name="ada_front",
        out_shape=(_sds((NDEV,) + a_loc.shape, F32), _sds((nrow, ncol), F32), _sds((NDEV, nrow, ncol), F32),
                   _sds((NDEV,) + wpack.shape, wpack.dtype)),
        in_specs=[ANY, ANY, VMEM, ANY], out_specs=(ANY, ANY, ANY, ANY),
        scratch_shapes=[pltpu.VMEM((NDEV,) + a_loc.shape, F32), pltpu.VMEM(w_loc.shape, F32),
                        pltpu.VMEM((nrow, ncol), F32), pltpu.SemaphoreType.DMA((3,))] + _GATHER_SEMS * 3,
        compiler_params=pltpu.CompilerParams(vmem_limit_bytes=32 * MIB),
    )(a_loc, w_loc, b_loc, wpack)


def _ada_bwd(a_raw, cctx_col, g_all, g_cols, w_loc, nb):
    nrow = a_raw.shape[0]
    ncol = w_loc.shape[1]

    def body(a_ref, cc_ref, gall_ref, g_ref, w_ref, dw_ref, pc_ref, gb_ref):
        a = a_ref[...]
        rowid = lax.broadcasted_iota(jnp.int32, (nrow, 1), 0) % 8
        act = jnp.where(rowid < nb, a * _sigmoid(a), 0.0).astype(BF16)
        g = g_ref[...]
        gc = _rowsum(jnp.where(rowid == nb, g, 0.0))
        cc = cc_ref[...]
        dw_ref[...] = _mm_tn(act, g.astype(BF16)) + (cc * _sigmoid(cc)) * gc
        pc_ref[...] = jnp.sum(w_ref[...] * gc, axis=1, keepdims=True)
        gb_ref[...] = _rowsum(gall_ref[...])

    return _pcall(body, name="ada_bwd",
                  out_shape=(_sds((D, ncol), F32), _sds((D, 1), F32), _sds((1, g_all.shape[1]), F32)),
                  in_specs=[VMEM] * 5, out_specs=(VMEM,) * 3, vmem_mb=48)(a_raw, cctx_col, g_all, g_cols, w_loc)


def _mod_spec(k, tpe, nrows):
    return pl.BlockSpec((1, k, D), lambda t: (jnp.minimum(t // tpe, nrows - 1), 0, 0))


def _load_ffn_weights(wall_ref, first, bufs, sems):
    fsh = FF // NDEV
    cps = []
    for j, buf in enumerate(bufs):
        for d in range(NDEV):
            cps.append(pltpu.make_async_copy(wall_ref.at[d, pl.ds((first + j) * fsh, fsh)],
                                             buf.at[pl.ds(d * fsh, fsh)], sems.at[j * NDEV + d]))
    for cp in cps:
        cp.start()
    for cp in cps:
        cp.wait()


def _token_specs(xs, tm, n_lat):
    specs = [pl.BlockSpec((tm, D), lambda t: (jnp.minimum(t, n_lat - 1), 0))]
    if len(xs) == 2:
        specs.append(pl.BlockSpec((tm, D), lambda t: (jnp.maximum(t - n_lat, 0), 0)))
    return specs


def _ffn_fwd(xs, mod3, norm_w, wall, first, *, tm, n_tiles, tpe, n_lat, name, target=None, gather=None):
    nrows = mod3.shape[0]
    r = n_tiles * tm
    nx = len(xs)
    with_loss = target is not None
    with_gather = gather is not None
    fwd_step = max(2 * n_tiles // 3, 1)

    def body(*refs):
        x_refs = refs[:nx]
        pos = nx
        if with_loss:
            tgt_ref = refs[pos]
            pos += 1
        mod_ref, nw_ref, wall_ref = refs[pos:pos + 3]
        pos += 3
        if with_gather:
            gin_ref = refs[pos]
            pos += 1
        xo_ref, a_ref, b_ref, o_ref = refs[pos:pos + 4]
        pos += 4
        if with_loss:
            ls_ref = refs[pos]
            pos += 1
        if with_gather:
            gout_ref = refs[pos]
            pos += 1
        w1_ref, w3_ref, w2_ref, wsem, acc_ref = refs[pos:pos + 5]
        t = pl.program_id(0)
        if with_gather:
            g_start, g_forward, g_finish = _gather_phases(gin_ref, gout_ref, *refs[pos + 5:])

        @pl.when(t == 0)
        def _():
            if with_gather:
                g_start()
            _load_ffn_weights(wall_ref, first, (w1_ref, w3_ref, w2_ref), wsem)
            if with_loss:
                ls_ref[...] = jnp.zeros_like(ls_ref)

        if with_gather:
            @pl.when(t == fwd_step)
            def _():
                g_forward()

            @pl.when(t == n_tiles - 1)
            def _():
                g_finish()

        x = x_refs[0][...]
        if nx == 2:
            x = jnp.where(t < n_lat, x, x_refs[1][...])
        n = x * _rms(x) * nw_ref[...]
        shift, scale, gate = mod_ref[0, 0:1, :], mod_ref[0, 1:2, :], mod_ref[0, 2:3, :]
        h = (n * (1.0 + scale) + shift).astype(BF16)
        nch = FF // FC
        o = None
        for lo_c, hi_c in ((0, nch // 2), (nch // 2, nch)):
            for j in range(lo_c, hi_c):
                sl = slice(j * FC, (j + 1) * FC)
                a = _mm_nt(h, w1_ref[sl, :])
                b = _mm_nt(h, w3_ref[sl, :])
                a_ref[:, sl] = a.astype(BF16)
                b_ref[:, sl] = b.astype(BF16)
                acc_ref[:, sl] = (a * _sigmoid(a) * b).astype(BF16)
            gs = slice(lo_c * FC, hi_c * FC)
            part = _mm(acc_ref[:, gs], w2_ref[gs, :])
            o = part if o is None else o + part
        o_ref[...] = o.astype(BF16)
        out = x + (0.5 * gate) * o
        if with_loss:
            d = out - tgt_ref[...]
            xo_ref[...] = d * (1.0 / D)
            ls_ref[...] += jnp.sum(d * d)
        else:
            xo_ref[...] = out

    row = lambda cols: pl.BlockSpec((tm, cols), lambda t: (t, 0))
    in_specs = _token_specs(xs, tm, n_lat) + ([row(D)] if with_loss else []) + [
        _mod_spec(3, tpe, nrows), _const((1, D)), ANY]
    out_shape = [_sds((r, D), F32), _sds((r, FF), BF16), _sds((r, FF), BF16), _sds((r, D), BF16)]
    out_specs = [row(D), row(FF), row(FF), row(D)]
    scratch = [pltpu.VMEM((FF, D), BF16)] * 3 + [pltpu.SemaphoreType.DMA((3 * NDEV,)), pltpu.VMEM((tm, FF), BF16)]
    if with_loss:
        out_shape.append(_sds((8, LANE), F32))
        out_specs.append(_const((8, LANE)))
    args = list(xs) + ([target] if with_loss else []) + [mod3, norm_w, wall]
    if with_gather:
        assert n_tiles >= 2
        in_specs.append(ANY)
        args.append(gather)
        out_shape.append(_sds((NDEV,) + gather.shape, gather.dtype))
        out_specs.append(ANY)
        scratch += _GATHER_SEMS
    return _pcall(
        body, name=name, grid=(n_tiles,), out_shape=tuple(out_shape), in_specs=in_specs, out_specs=tuple(out_specs),
        scratch=scratch, vmem_mb=56)(*args)


def _ffn_bwd_dx(dout, xs, a, b, o, mod3, norm_w, wall, first, *, tm, n_tiles, tpe, n_lat, name):
    nrows = mod3.shape[0]
    r = n_tiles * tm
    nx = len(xs)

    def body(*refs):
        dout_ref = refs[0]
        x_refs = refs[1:1 + nx]
        (a_ref, b_ref, o_ref, mod_ref, nw_ref, wall_ref,
         dx_ref, da_ref, db_ref, g_ref, do_ref, h_ref, dmod_ref, dnw_ref,
         w1_ref, w3_ref, w2_ref, wsem) = refs[1 + nx:]
        t = pl.program_id(0)

        @pl.when(t == 0)
        def _():
            _load_ffn_weights(wall_ref, first, (w1_ref, w3_ref, w2_ref), wsem)
            dnw_ref[...] = jnp.zeros_like(dnw_ref)

        @pl.when(jnp.where(t < n_lat, t % tpe == 0, t == n_lat))
        def _():
            dmod_ref[...] = jnp.zeros_like(dmod_ref)

        x = x_refs[0][...]
        if nx == 2:
            x = jnp.where(t < n_lat, x, x_refs[1][...])
        dout = dout_ref[...]
        shift, scale, gate = mod_ref[0, 0:1, :], mod_ref[0, 1:2, :], mod_ref[0, 2:3, :]
        d_o = ((0.5 * gate) * dout).astype(BF16)
        do_ref[...] = d_o
        nch = FF // FC
        groups = ((0, nch // 2), (nch // 2, nch))
        dh = None
        for lo_c, hi_c in groups:
            for j in range(lo_c, hi_c):
                sl = slice(j * FC, (j + 1) * FC)
                av = a_ref[:, sl].astype(F32)
                bv = b_ref[:, sl].astype(F32)
                dg = _mm_nt(d_o, w2_ref[sl, :])
                sig = _sigmoid(av)
                sa = av * sig
                g_ref[:, sl] = (sa * bv).astype(BF16)
                da_ref[:, sl] = (dg * bv * (sig * (1.0 + av * (1.0 - sig)))).astype(BF16)
                db_ref[:, sl] = (dg * sa).astype(BF16)
            gs = slice(lo_c * FC, hi_c * FC)
            part = _mm(da_ref[:, gs], w1_ref[gs, :]) + _mm(db_ref[:, gs], w3_ref[gs, :])
            dh = part if dh is None else dh + part
        rr = _rms(x)
        xh = x * rr
        nw = nw_ref[...]
        n = xh * nw
        h_ref[...] = (n * (1.0 + scale) + shift).astype(BF16)
        dgate = _rowsum(0.5 * o_ref[...].astype(F32) * dout)
        dn = dh * (1.0 + scale)
        dxh = dn * nw
        dmod_ref[0, 0:1, :] += _rowsum(dh)
        dmod_ref[0, 1:2, :] += _rowsum(dh * n)
        dmod_ref[0, 2:3, :] += dgate
        dnw_ref[...] += _rowsum(dn * xh)
        dx = dout + rr * (dxh - xh * jnp.mean(dxh * xh, axis=-1, keepdims=True))
        if n_tiles == n_lat:
            dx_ref[...] = dx
        else:
            @pl.when(t < n_lat)
            def _():
                dx_ref[...] = dx

    row = lambda cols: pl.BlockSpec((tm, cols), lambda t: (t, 0))
    lat = pl.BlockSpec((tm, D), lambda t: (jnp.minimum(t, n_lat - 1), 0))
    out_shape = [_sds((n_lat * tm, D), F32), _sds((r, FF), BF16), _sds((r, FF), BF16), _sds((r, FF), BF16),
                 _sds((r, D), BF16), _sds((r, D), BF16), _sds((nrows, 3, D), F32), _sds((1, D), F32)]
    in_specs = [row(D)] + _token_specs(xs, tm, n_lat) + [row(FF), row(FF), row(D), _mod_spec(3, tpe, nrows),
                                                          _const((1, D)), ANY]
    out_specs = [lat, row(FF), row(FF), row(FF), row(D), row(D), _mod_spec(3, tpe, nrows), _const((1, D))]
    scratch = [pltpu.VMEM((FF, D), BF16)] * 3 + [pltpu.SemaphoreType.DMA((3 * NDEV,))]
    args = [dout, *xs, a, b, o, mod3, norm_w, wall]
    return _pcall(body, name=name, grid=(n_tiles,), out_shape=tuple(out_shape), in_specs=in_specs,
                  out_specs=tuple(out_specs), scratch=scratch, vmem_mb=60)(*args)


def _ffn_bwd_dw(h, d_o, da, db, g, *, tr, name):
    r = h.shape[0]
    fh = FF // 2
    fsh = FF // NDEV
    nk = r // tr

    def body(h_ref, do_ref, da_ref, db_ref, g_ref, out_ref, acc1, acc3, acc2):
        k = pl.program_id(1)

        @pl.when(k == 0)
        def _():
            acc1[...] = jnp.zeros_like(acc1)
            acc3[...] = jnp.zeros_like(acc3)
            acc2[...] = jnp.zeros_like(acc2)

        hv = h_ref[...]
        acc1[...] += _mm_tn(da_ref[...], hv)
        acc3[...] += _mm_tn(db_ref[...], hv)
        acc2[...] += _mm_tn(g_ref[...], do_ref[...])

        @pl.when(k == nk - 1)
        def _():
            for i, acc in enumerate((acc1, acc3, acc2)):
                out_ref[:, i * fsh:(i + 1) * fsh, :] = acc[...].reshape(NDEV // 2, fsh, D).astype(BF16)

    rowd = pl.BlockSpec((tr, D), lambda f, k: (k, 0))
    rowf = pl.BlockSpec((tr, fh), lambda f, k: (k, f))
    return _pcall(
        body, name=name, grid=(2, nk), out_shape=_sds((NDEV, 3 * fsh, D), BF16),
        in_specs=[rowd, rowd, rowf, rowf, rowf],
        out_specs=pl.BlockSpec((NDEV // 2, 3 * fsh, D), lambda f, k: (f, 0, 0)),
        scratch=[pltpu.VMEM((fh, D), F32)] * 3, vmem_mb=56)(h, d_o, da, db, g)


def _direct_sends(x_ref, out_ref, send_sems, recv_sems):
    mx, my, mc = lax.axis_index("x"), lax.axis_index("y"), lax.axis_index("c")
    sends = []
    for k in range(1, NDEV):
        px = 1 - mx if (k & 4) else mx
        py = 1 - my if (k & 2) else my
        pc = 1 - mc if (k & 1) else mc
        sends.append(pltpu.make_async_remote_copy(
            src_ref=x_ref.at[4 * px + 2 * py + pc], dst_ref=out_ref.at[k - 1],
            send_sem=send_sems.at[k - 1], recv_sem=recv_sems.at[k - 1], device_id=(px, py, pc), device_id_type=MESH))
    return sends


def _sum_direct(x, recv, tr, name):
    _, r, c = x.shape

    def body(x_ref, r_ref, o_ref):
        me = 4 * lax.axis_index("x") + 2 * lax.axis_index("y") + lax.axis_index("c")
        acc = x_ref[0].astype(F32)
        for j in range(1, NDEV):
            acc = jnp.where(me == j, x_ref[j].astype(F32), acc)
        for j in range(NDEV - 1):
            acc = acc + r_ref[j].astype(F32)
        o_ref[...] = acc

    return _pcall(body, name=name, grid=(r // tr,), out_shape=_sds((r, c), F32),
                  in_specs=[pl.BlockSpec((NDEV, tr, c), lambda t: (0, t, 0)),
                            pl.BlockSpec((NDEV - 1, tr, c), lambda t: (0, t, 0))],
                  out_specs=pl.BlockSpec((tr, c), lambda t: (t, 0)))(x, recv)


def _exchange_behind(x_ref, recv_ref, send_sems, recv_sems, first, last):
    sends = _direct_sends(x_ref, recv_ref, send_sems, recv_sems)

    @pl.when(first)
    def _():
        for cp in sends:
            cp.start()

    @pl.when(last)
    def _():
        for cp in sends:
            cp.wait_recv()
        for cp in sends:
            cp.wait_send()


def _ffn_bwd_dw_one(lhs, rhs, *, tr, name, part=None):
    r = lhs.shape[0]
    fsh = FF // NDEV
    nk = r // tr
    fused = part is not None
    nslot = NDEV - 1

    def body(*refs):
        if fused:
            lhs_ref, rhs_ref, part_ref, out_ref, recv_ref, acc, send_sems, recv_sems = refs
        else:
            lhs_ref, rhs_ref, out_ref, acc = refs
        k = pl.program_id(0)
        if fused:
            _exchange_behind(part_ref, recv_ref, send_sems, recv_sems, k == 0, k == nk - 1)

        @pl.when(k == 0)
        def _():
            acc[...] = jnp.zeros_like(acc)

        acc[...] += _mm_tn(lhs_ref[...], rhs_ref[...])

        @pl.when(k == nk - 1)
        def _():
            out_ref[...] = acc[...].reshape(NDEV, fsh, D).astype(BF16)

    in_specs = [pl.BlockSpec((tr, FF), lambda k: (k, 0)), pl.BlockSpec((tr, D), lambda k: (k, 0))]
    out_shape = [_sds((NDEV, fsh, D), BF16)]
    out_specs = [_const((NDEV, fsh, D))]
    scratch = [pltpu.VMEM((FF, D), F32)]
    args = [lhs, rhs]
    if fused:
        in_specs.append(ANY)
        args.append(part)
        out_shape.append(_sds((nslot,) + part.shape[1:], part.dtype))
        out_specs.append(ANY)
        scratch += [pltpu.SemaphoreType.DMA((nslot,))] * 2
    res = _pcall(body, name=name, grid=(nk,), out_shape=tuple(out_shape), in_specs=in_specs,
                 out_specs=tuple(out_specs), scratch=scratch, vmem_mb=48)(*args)
    return res if fused else res[0]


_PIECES =((0, 128), (128, 384), (384, 896), (896, 1408), (1408, 1536))


def _proj_fwd(x1, mod2, norm_w, wint, *, tm, n_tiles, tpe, name="proj_fwd"):
    nrows = mod2.shape[0]
    r = n_tiles * tm

    def body(x_ref, mod_ref, nw_ref, w_ref, ckv_ref, q_ref, u_ref, v_ref, kpe_ref):
        x = x_ref[...]
        n = x * _rms(x) * nw_ref[...]
        h = (n * (1.0 + mod_ref[0, 1:2, :]) + mod_ref[0, 0:1, :]).astype(BF16)
        for (lo, hi), ref in zip(_PIECES, (ckv_ref, q_ref, u_ref, v_ref, kpe_ref)):
            ref[...] = _mm_nt(h, w_ref[lo:hi, :])

    row = lambda cols: pl.BlockSpec((tm, cols), lambda t: (t, 0))
    widths = [hi - lo for lo, hi in _PIECES]
    return _pcall(
        body, name=name, grid=(n_tiles,),
        out_shape=tuple(_sds((r, w), F32) for w in widths),
        in_specs=[row(D), _mod_spec(2, tpe, nrows), _const((1, D)), _const((WIN_ROWS, D))],
        out_specs=tuple(row(w) for w in widths), vmem_mb=40)(x1, mod2, norm_w, wint)


def _proj_bwd(dckv, dkpe, dq, du, dv, dx2, x1, mod2, norm_w, wint, *, tm, n_tiles, tpe, n_lat, name="proj_bwd"):
    nrows = mod2.shape[0]
    r = n_tiles * tm

    def body(dckv_ref, dkpe_ref, dq_ref, du_ref, dv_ref, dx2_ref, x_ref, mod_ref, nw_ref, w_ref,
             dx_ref, dw_ref, dmod_ref, dnw_ref):
        t = pl.program_id(0)
        is_lat = t < n_lat

        @pl.when(t == 0)
        def _():
            dw_ref[...] = jnp.zeros_like(dw_ref)
            dnw_ref[...] = jnp.zeros_like(dnw_ref)

        @pl.when(jnp.where(is_lat, t % tpe == 0, t == n_lat))
        def _():
            dmod_ref[...] = jnp.zeros_like(dmod_ref)

        x = x_ref[...]
        rr = _rms(x)
        xh = x * rr
        nw = nw_ref[...]
        n = xh * nw
        scale = mod_ref[0, 1:2, :]
        h = (n * (1.0 + scale) + mod_ref[0, 0:1, :]).astype(BF16)
        zero = jnp.zeros((), BF16)
        pieces = (dckv_ref[...], jnp.where(is_lat, dq_ref[...], zero), jnp.where(is_lat, du_ref[...], zero),
                  jnp.where(is_lat, dv_ref[...], zero), dkpe_ref[...])
        dh = None
        for (lo, hi), piece in zip(_PIECES, pieces):
            part = _mm(piece, w_ref[lo:hi, :])
            dh = part if dh is None else dh + part
        dn = dh * (1.0 + scale)
        dxh = dn * nw
        dx = rr * (dxh - xh * jnp.mean(dxh * xh, axis=-1, keepdims=True))
        dx_ref[...] = dx + jnp.where(is_lat, dx2_ref[...], 0.0)
        dmod_ref[0, 0:1, :] += _rowsum(dh)
        dmod_ref[0, 1:2, :] += _rowsum(dh * n)
        dnw_ref[...] += _rowsum(dn * xh)
        for (lo, hi), piece in zip(_PIECES, pieces):
            dw_ref[lo:hi, :] += _mm_tn(piece, h)

    row = lambda cols: pl.BlockSpec((tm, cols), lambda t: (t, 0))
    lat = lambda cols: pl.BlockSpec((tm, cols), lambda t: (jnp.minimum(t, n_lat - 1), 0))
    return _pcall(
        body, name=name, grid=(n_tiles,),
        out_shape=(_sds((r, D), F32), _sds((WIN_ROWS, D), F32), _sds((nrows, 2, D), F32), _sds((1, D), F32)),
        in_specs=[row(128), row(128), lat(256), lat(512), lat(512), lat(D), row(D), _mod_spec(2, tpe, nrows),
                  _const((1, D)), _const((WIN_ROWS, D))],
        out_specs=(row(D), _const((WIN_ROWS, D)), _mod_spec(2, tpe, nrows), _const((1, D))),
        vmem_mb=48)(dckv, dkpe, dq, du, dv, dx2, x1, mod2, norm_w, wint)


def _seg_sum(x, seg):
    return _mm(x.astype(BF16), seg)


def _seg_bcast(v, segt2):
    hi = v.astype(BF16)
    lo = (v - hi.astype(F32)).astype(BF16)
    return _mm(jnp.concatenate([hi, lo], axis=-1), segt2)


def _rope_pairs(t, cos, sin, rot2):
    cos2, sin2 = jnp.concatenate([cos, cos], axis=-1), jnp.concatenate([sin, sin], axis=-1)
    out = []
    for j in range(H // 2):
        tj = t[:, 2 * j * LANE:2 * (j + 1) * LANE]
        out.append(tj * cos2 + _dot_hl(tj, rot2) * sin2)
    return jnp.concatenate(out, axis=-1)


def _head_norm_rope(x, w_pad, cos, sin, seg, segt2, rot2, rope=True):
    rh = lax.rsqrt(_seg_sum(x * x, seg) * (1.0 / DH) + EPS)
    rb = _seg_bcast(rh, segt2)
    y = x * rb
    out = _rope_pairs(y * w_pad, cos, sin, rot2) if rope else None
    return out, y, rb


def _head_norm_rope_bwd(dout, y, rb, w_pad, cos, sin, seg, segt2, rot2_t):
    cos2, sin2 = jnp.concatenate([cos, cos], axis=-1), jnp.concatenate([sin, sin], axis=-1)
    dt = []
    for j in range(H // 2):
        dj = dout[:, 2 * j * LANE:2 * (j + 1) * LANE]
        dt.append(dj * cos2 + _dot_hl(dj * sin2, rot2_t))
    dt = jnp.concatenate(dt, axis=-1)
    dw = _rowsum(dt * y)
    dy = dt * w_pad
    mean_h = _seg_sum(dy * y, seg) * (1.0 / DH)
    return rb * (dy - y * _seg_bcast(mean_h, segt2)), dw


def _q_prep_fwd(qp, qa_w, wuq, wq, cos, sin, cs, *, tm, n_lat, tpe):
    def body(qp_ref, qa_ref, wuq_ref, wq_ref, cos_ref, sin_ref, seg, segt, rot, q_ref):
        x = qp_ref[...]
        cq = (x * _rms(x) * qa_ref[...]).astype(BF16)
        q, _, _ = _head_norm_rope(_mm_nt(cq, wuq_ref[...]), wq_ref[...], cos_ref[...], sin_ref[...],
                                  seg[...], segt[...], rot[...])
        q_ref[...] = q.astype(BF16)

    row = lambda cols: pl.BlockSpec((tm, cols), lambda t: (t, 0))
    tab = pl.BlockSpec((tm, LANE), lambda t: (t % tpe, 0))
    return _pcall(
        body, name="q_prep_fwd", grid=(n_lat,), out_shape=_sds((n_lat * tm, HP), BF16),
        in_specs=[row(QL), _const((1, QL)), _const((HP, QL)), _const((1, HP)), tab, tab,
                  _const((HP, LANE)), _const((2 * LANE, HP)), _const((2 * LANE, 2 * LANE))],
        out_specs=row(HP))(qp, qa_w, wuq, wq, cos, sin, cs["seg_h"], cs["seg_ht"], cs["rot"])


def _q_prep_bwd(dq, qp, qa_w, wuq, wq, cos, sin, cs, *, tm, n_lat, tpe):
    def body(dq_ref, qp_ref, qa_ref, wuq_ref, wq_ref, cos_ref, sin_ref, seg, segt, rot, rot_t,
             dqp_ref, dwuq_ref, dqa_ref, dwq_ref):
        @pl.when(pl.program_id(0) == 0)
        def _():
            dwuq_ref[...] = jnp.zeros_like(dwuq_ref)
            dqa_ref[...] = jnp.zeros_like(dqa_ref)
            dwq_ref[...] = jnp.zeros_like(dwq_ref)

        x = qp_ref[...]
        ra = _rms(x)
        xh = x * ra
        qa = qa_ref[...]
        cq = (xh * qa).astype(BF16)
        wuq_v = wuq_ref[...]
        wq_v, cos_v, sin_v = wq_ref[...], cos_ref[...], sin_ref[...]
        _, y, rb = _head_norm_rope(_mm_nt(cq, wuq_v), wq_v, cos_v, sin_v, seg[...], segt[...], rot[...], rope=False)
        dqraw, dwq = _head_norm_rope_bwd(dq_ref[...], y, rb, wq_v, cos_v, sin_v, seg[...], segt[...], rot_t[...])
        dqraw = dqraw.astype(BF16)
        dcq = _mm(dqraw, wuq_v)
        dxh = dcq * qa
        dqp_ref[...] = (ra * (dxh - xh * jnp.mean(dxh * xh, axis=-1, keepdims=True))).astype(BF16)
        dwuq_ref[...] += _mm_tn(dqraw, cq)
        dqa_ref[...] += _rowsum(dcq * xh)
        dwq_ref[...] += dwq

    row = lambda cols: pl.BlockSpec((tm, cols), lambda t: (t, 0))
    tab = pl.BlockSpec((tm, LANE), lambda t: (t % tpe, 0))
    return _pcall(
        body, name="q_prep_bwd", grid=(n_lat,),
        out_shape=(_sds((n_lat * tm, QL), BF16), _sds((HP, QL), F32), _sds((1, QL), F32), _sds((1, HP), F32)),
        in_specs=[row(HP), row(QL), _const((1, QL)), _const((HP, QL)), _const((1, HP)), tab, tab,
                  _const((HP, LANE)), _const((2 * LANE, HP)), _const((2 * LANE, 2 * LANE)), _const((2 * LANE, 2 * LANE))],
        out_specs=(row(QL), _const((HP, QL)), _const((1, QL)), _const((1, HP))), vmem_mb=40)(
            dq, qp, qa_w, wuq, wq, cos, sin, cs["seg_h"], cs["seg_ht"], cs["rot"], cs["rot_t"])


def _kv_tab_spec(tm, tpe, n_lat):
    return pl.BlockSpec((tm, LANE), lambda t: (jnp.where(t < n_lat, t % tpe, tpe), 0))


def _split_kv(kv, kpe):
    low = lax.broadcasted_iota(jnp.int32, (kv.shape[0], LANE), 1) < DN
    kx, v = [], []
    for h in range(H):
        blk = kv[:, h * LANE:(h + 1) * LANE]
        kx.append(jnp.where(low, blk, kpe))
        v.append(jnp.where(low, pltpu.roll(blk, DN, 1), 0.0))
    return jnp.concatenate(kx, axis=-1), jnp.concatenate(v, axis=-1)


def _kv_prep_fwd(ckv, kpe, kva_w, wukv, wk, cosk, sink, cs, *, tm, n_tiles, tpe, n_lat):
    def body(ckv_ref, kpe_ref, kva_ref, wukv_ref, wk_ref, cos_ref, sin_ref, seg, segt, rot, k_ref, v_ref):
        x = ckv_ref[...]
        ckvn = (x * _rms(x) * kva_ref[...]).astype(BF16)
        kx, v = _split_kv(_mm_nt(ckvn, wukv_ref[...]), kpe_ref[...])
        k, _, _ = _head_norm_rope(kx, wk_ref[...], cos_ref[...], sin_ref[...], seg[...], segt[...], rot[...])
        k_ref[...] = k.astype(BF16)
        v_ref[...] = v.astype(BF16)

    row = lambda cols: pl.BlockSpec((tm, cols), lambda t: (t, 0))
    tab = _kv_tab_spec(tm, tpe, n_lat)
    r = n_tiles * tm
    return _pcall(
        body, name="kv_prep_fwd", grid=(n_tiles,), out_shape=(_sds((r, HP), BF16), _sds((r, HP), BF16)),
        in_specs=[row(KVL), row(LANE), _const((1, KVL)), _const((HP, KVL)), _const((1, HP)), tab, tab,
                  _const((HP, LANE)), _const((2 * LANE, HP)), _const((2 * LANE, 2 * LANE))],
        out_specs=(row(HP), row(HP)), vmem_mb=40)(
            ckv, kpe, kva_w, wukv, wk, cosk, sink, cs["seg_h"], cs["seg_ht"], cs["rot"])


def _kv_prep_bwd(dks, dvs, ckv, kpe, kva_w, wukv, wk, cosk, sink, cs, *, tm, n_tiles, tpe, n_lat):
    def body(dkl_ref, dkc_ref, dvl_ref, dvc_ref, ckv_ref, kpe_ref, kva_ref, wukv_ref, wk_ref, cos_ref, sin_ref,
             seg, segt, rot, rot_t, dckv_ref, dkpe_ref, dwukv_ref, dkva_ref, dwk_ref):
        t = pl.program_id(0)
        is_lat = t < n_lat

        @pl.when(t == 0)
        def _():
            dwukv_ref[...] = jnp.zeros_like(dwukv_ref)
            dkva_ref[...] = jnp.zeros_like(dkva_ref)
            dwk_ref[...] = jnp.zeros_like(dwk_ref)

        dk = jnp.where(is_lat, dkl_ref[...], dkc_ref[...])
        dv = jnp.where(is_lat, dvl_ref[...], dvc_ref[...])
        x = ckv_ref[...]
        ra = _rms(x)
        xh = x * ra
        kva = kva_ref[...]
        ckvn = (xh * kva).astype(BF16)
        wukv_v = wukv_ref[...]
        wk_v, cos_v, sin_v = wk_ref[...], cos_ref[...], sin_ref[...]
        kx, _ = _split_kv(_mm_nt(ckvn, wukv_v), kpe_ref[...])
        _, y, rb = _head_norm_rope(kx, wk_v, cos_v, sin_v, seg[...], segt[...], rot[...], rope=False)
        dkx, dwk = _head_norm_rope_bwd(dk, y, rb, wk_v, cos_v, sin_v, seg[...], segt[...], rot_t[...])
        dkpe = dkx[:, 0:LANE]
        for h in range(1, H):
            dkpe = dkpe + dkx[:, h * LANE:(h + 1) * LANE]
        lane = lax.broadcasted_iota(jnp.int32, (tm, LANE), 1)
        dkpe_ref[...] = jnp.where((lane >= DN) & (lane < DH), dkpe, 0.0).astype(BF16)
        dkv = jnp.concatenate([jnp.where(lane < DN, dkx[:, h * LANE:(h + 1) * LANE],
                                         pltpu.roll(dv[:, h * LANE:(h + 1) * LANE], DN, 1)) for h in range(H)],
                              axis=-1).astype(BF16)
        dckvn = _mm(dkv, wukv_v)
        dxh = dckvn * kva
        dckv_ref[...] = (ra * (dxh - xh * jnp.mean(dxh * xh, axis=-1, keepdims=True))).astype(BF16)
        dwukv_ref[...] += _mm_tn(dkv, ckvn)
        dkva_ref[...] += _rowsum(dckvn * xh)
        dwk_ref[...] += dwk

    row = lambda cols: pl.BlockSpec((tm, cols), lambda t: (t, 0))
    lat = pl.BlockSpec((tm, HP), lambda t: (jnp.minimum(t, n_lat - 1), 0))
    ctx = pl.BlockSpec((tm, HP), lambda t: (jnp.maximum(t - n_lat, 0), 0))
    tab = _kv_tab_spec(tm, tpe, n_lat)
    r = n_tiles * tm
    return _pcall(
        body, name="kv_prep_bwd", grid=(n_tiles,),
        out_shape=(_sds((r, KVL), BF16), _sds((r, LANE), BF16), _sds((HP, KVL), F32), _sds((1, KVL), F32),
                   _sds((1, HP), F32)),
        in_specs=[lat, ctx, lat, ctx, row(KVL), row(LANE), _const((1, KVL)), _const((HP, KVL)), _const((1, HP)),
                  tab, tab, _const((HP, LANE)), _const((2 * LANE, HP)), _const((2 * LANE, 2 * LANE)),
                  _const((2 * LANE, 2 * LANE))],
        out_specs=(row(KVL), row(LANE), _const((HP, KVL)), _const((1, KVL)), _const((1, HP))), vmem_mb=48)(
            dks[0], dks[1], dvs[0], dvs[1], ckv, kpe, kva_w, wukv, wk, cosk, sink,
            cs["seg_h"], cs["seg_ht"], cs["rot"], cs["rot_t"])


_SCALE = DH ** -0.5
_SCALE_LOG2E = _SCALE * 1.4426950408889634


def _key_chunks(s, nc, ck):
    return ([(0, lo, min(lo + ck, s)) for lo in range(0, s, ck)]
            + [(1, lo, min(lo + ck, nc)) for lo in range(0, nc, ck)])


def _attn_fwd(q, k, v, *, nb, s, nc, tq, ck):
    tpe = s // tq
    r_lat = nb * s
    chunks = _key_chunks(s, nc, ck)
    hp = 4

    def body(q_ref, kl_ref, kc_ref, vl_ref, vc_ref, o_ref, lse_ref):
        k_refs, v_refs = (kl_ref, kc_ref), (vl_ref, vc_ref)
        for hh in range(hp):
            hs = slice(hh * LANE, (hh + 1) * LANE)
            qv = q_ref[:, hs]
            xs = [_mm_nt(qv, k_refs[w][lo:hi, hs]) for w, lo, hi in chunks]
            m = jnp.max(xs[0], axis=-1, keepdims=True)
            for x in xs[1:]:
                m = jnp.maximum(m, jnp.max(x, axis=-1, keepdims=True))
            l = acc = None
            for x, (w, lo, hi) in zip(xs, chunks):
                e = jnp.exp2((x - m) * _SCALE_LOG2E)
                lc = jnp.sum(e, axis=-1, keepdims=True)
                pv = _mm(e.astype(BF16), v_refs[w][lo:hi, hs])
                l = lc if l is None else l + lc
                acc = pv if acc is None else acc + pv
            o_ref[:, hs] = (acc / l).astype(BF16)
            lse = m * _SCALE_LOG2E + jnp.log2(l)
            lse_ref[hh] = jnp.transpose(jnp.broadcast_to(lse, (tq, LANE)))[0:8, :]

    qs = pl.BlockSpec((tq, hp * LANE), lambda i, j, t: (i * tpe + t, j))
    kl = pl.BlockSpec((s, hp * LANE), lambda i, j, t: (i, j))
    kc = pl.BlockSpec((nc, hp * LANE), lambda i, j, t: (r_lat // nc + i, j))
    ls = pl.BlockSpec((hp, 8, tq), lambda i, j, t: (i * (H // hp) + j, 0, t))
    return _pcall(body, name="attn_fwd", grid=(nb, H // hp, tpe),
                  out_shape=(_sds((r_lat, HP), BF16), _sds((nb * H, 8, s), F32)),
                  in_specs=[qs, kl, kc, kl, kc], out_specs=(qs, ls), vmem_mb=48)(q, k, k, v, v)


def _attn_bwd(q, k, v, o, do, lse, part, *, nb, s, nc, tq, ck):
    tpe = s // tq
    r_lat = nb * s
    chunks = _key_chunks(s, nc, ck)
    hp = 2
    n_steps = nb * (H // hp) * tpe

    def body(q_ref, kl_ref, kc_ref, vl_ref, vc_ref, o_ref, do_ref, lse_ref, part_ref,
             dq_ref, dkl_ref, dkc_ref, dvl_ref, dvc_ref, recv_ref, akl, akc, avl, avc, send_sems, recv_sems):
        t = pl.program_id(2)
        step = (pl.program_id(0) * (H // hp) + pl.program_id(1)) * tpe + t
        _exchange_behind(part_ref, recv_ref, send_sems, recv_sems, step == 0, step == n_steps - 1)

        @pl.when(t == 0)
        def _():
            akl[...] = jnp.zeros_like(akl)
            akc[...] = jnp.zeros_like(akc)
            avl[...] = jnp.zeros_like(avl)
            avc[...] = jnp.zeros_like(avc)

        k_refs, v_refs, ak, av = (kl_ref, kc_ref), (vl_ref, vc_ref), (akl, akc), (avl, avc)
        for hh in range(hp):
            hs = slice(hh * LANE, (hh + 1) * LANE)
            qv = q_ref[:, hs]
            lse = jnp.transpose(jnp.concatenate([lse_ref[hh]] * (LANE // 8), axis=0))[:, 0:1]
            dov = do_ref[:, hs]
            delta = jnp.sum(dov.astype(F32) * o_ref[:, hs].astype(F32), axis=-1, keepdims=True)
            dq = None
            for w, lo, hi in chunks:
                kc_v = k_refs[w][lo:hi, hs]
                p = jnp.exp2(_mm_nt(qv, kc_v) * _SCALE_LOG2E - lse)
                ds = (p * (_mm_nt(dov, v_refs[w][lo:hi, hs]) - delta)).astype(BF16)
                part = _mm(ds, kc_v)
                dq = part if dq is None else dq + part
                ak[w][hs, lo:hi] += _mm_tn(qv, ds)
                av[w][hs, lo:hi] += _mm_tn(dov, p.astype(BF16))
            dq_ref[:, hs] = dq * _SCALE

        @pl.when(t == tpe - 1)
        def _():
            dkl_ref[...] = akl[...].T * _SCALE
            dkc_ref[...] = akc[...].T * _SCALE
            dvl_ref[...] = avl[...].T
            dvc_ref[...] = avc[...].T

    qs = pl.BlockSpec((tq, hp * LANE), lambda i, j, t: (i * tpe + t, j))
    kl = pl.BlockSpec((s, hp * LANE), lambda i, j, t: (i, j))
    kc = pl.BlockSpec((nc, hp * LANE), lambda i, j, t: (r_lat // nc + i, j))
    kc_out = pl.BlockSpec((nc, hp * LANE), lambda i, j, t: (i, j))
    ls = pl.BlockSpec((hp, 8, tq), lambda i, j, t: (i * (H // hp) + j, 0, t))
    return _pcall(
        body, name="attn_bwd", grid=(nb, H // hp, tpe),
        out_shape=(_sds((r_lat, HP), F32), _sds((r_lat, HP), F32), _sds((nb * nc, HP), F32),
                   _sds((r_lat, HP), F32), _sds((nb * nc, HP), F32), _sds((NDEV - 1,) + part.shape[1:], part.dtype)),
        in_specs=[qs, kl, kc, kl, kc, qs, qs, ls, ANY], out_specs=(qs, kl, kc_out, kl, kc_out, ANY),
        scratch=[pltpu.VMEM((hp * LANE, s), F32), pltpu.VMEM((hp * LANE, nc), F32)] * 2
        + [pltpu.SemaphoreType.DMA((NDEV - 1,))] * 2,
        vmem_mb=60)(q, k, k, v, v, o, do, lse, part)


def _chunks_side_by_side(x, j, nch):
    return jnp.concatenate([x[c * CH:(c + 1) * CH, j * LANE:(j + 1) * LANE] for c in range(nch)], axis=-1)


def _first_group_lanes(nch):
    return (lax.broadcasted_iota(jnp.int32, (CH, nch * LANE), 1) & (LANE - 1)) < GD


def _gating(vn, ws_ref, bias_ref, s_scr, tm):
    nch = tm // CH
    first = _first_group_lanes(nch)
    for j in range(G // 2):
        ls = slice(j * LANE, (j + 1) * LANE)
        vst = _chunks_side_by_side(vn, j, nch)
        st = jnp.where(first, _mm(ws_ref[2 * j], vst), _mm(ws_ref[2 * j + 1], vst))
        for c in range(nch):
            s_scr[c * CH:(c + 1) * CH, ls] = st[:, c * LANE:(c + 1) * LANE] + bias_ref[:, ls]


def _compact_heads(x):
    low = lax.broadcasted_iota(jnp.int32, (x.shape[0], LANE), 1) < DV
    out = []
    for j in range(H // 2):
        even = x[:, 2 * j * LANE:(2 * j + 1) * LANE].astype(F32)
        odd = x[:, (2 * j + 1) * LANE:(2 * j + 2) * LANE].astype(F32)
        out.append(jnp.where(low, even, pltpu.roll(odd, DV, 1)))
    return jnp.concatenate(out, axis=-1)


def _expand_heads(x):
    low = lax.broadcasted_iota(jnp.int32, (x.shape[0], LANE), 1) < DV
    out = []
    for j in range(H // 2):
        blk = x[:, j * LANE:(j + 1) * LANE]
        out.append(jnp.where(low, blk, 0.0))
        out.append(jnp.where(low, pltpu.roll(blk, DV, 1), 0.0))
    return jnp.concatenate(out, axis=-1)


def _mix_fwd(u, v, attn, x1, gate, wv, ws, bias, wout, cs, *, tm, n_lat, tpe):
    nrows = gate.shape[0]

    def body(u_ref, v_ref, attn_ref, x_ref, gate_ref, wv_ref, ws_ref, bias_ref, wout_ref, seg, segt,
             x2_ref, mix_ref, s_scr):
        vg = _gelu(v_ref[...])
        rg = lax.rsqrt(_seg_sum(vg * vg, seg[...]) * (1.0 / GD) + EPS)
        vn = (vg * _seg_bcast(rg, segt[...]) * wv_ref[...]).astype(BF16)
        _gating(vn, ws_ref, bias_ref, s_scr, tm)
        sg = (_gelu(u_ref[...]) * s_scr[...]).astype(BF16)
        attn_c = _compact_heads(attn_ref[...]).astype(BF16)
        mix = _mm(attn_c, wout_ref[0:H * DV, :]) + _mm(sg, wout_ref[H * DV:, :])
        mix_ref[...] = mix.astype(BF16)
        x2_ref[...] = x_ref[...] + gate_ref[0] * mix

    row = lambda cols: pl.BlockSpec((tm, cols), lambda t: (t, 0))
    r = n_lat * tm
    return _pcall(
        body, name="mix_fwd", grid=(n_lat,),
        out_shape=(_sds((r, D), F32), _sds((r, D), BF16)),
        in_specs=[row(G * GD), row(G * GD), row(HP), row(D), _mod_spec(1, tpe, nrows), _const((1, G * GD)),
                  _const((G, CH, CH)), _const((CH, G * GD)), _const((D, D)), _const((G * GD, LANE)),
                  _const((2 * LANE, G * GD))],
        out_specs=(row(D), row(D)), scratch=[pltpu.VMEM((tm, G * GD), F32)], vmem_mb=40)(
            u, v, attn, x1, gate, wv, ws, bias, wout, cs["seg_g"], cs["seg_gt"])


def _mix_bwd(dx2, mix, u, v, attn, gate, wv, ws, wst, bias, wout, cs, *, tm, n_lat, tpe):
    nrows = gate.shape[0]
    wrows = H * DV + G * GD

    def body(dx2_ref, mix_ref, u_ref, v_ref, attn_ref, gate_ref, wv_ref, ws_ref, wst_ref, bias_ref, wout_ref, seg, segt,
             dattn_ref, du_ref, dv_ref, dgate_ref, dwout_ref, dws_ref, dbs_ref, dwv_ref, s_scr, dvn_scr, dbias_scr):
        t = pl.program_id(0)

        @pl.when(t == 0)
        def _():
            dwout_ref[...] = jnp.zeros_like(dwout_ref)
            dws_ref[...] = jnp.zeros_like(dws_ref)
            dwv_ref[...] = jnp.zeros_like(dwv_ref)
            dbias_scr[...] = jnp.zeros_like(dbias_scr)

        @pl.when(t % tpe == 0)
        def _():
            dgate_ref[...] = jnp.zeros_like(dgate_ref)

        dx2 = dx2_ref[...]
        dmix = (dx2 * gate_ref[0]).astype(BF16)
        dcat = _mm_nt(dmix, wout_ref[...])
        dattn_ref[...] = _expand_heads(dcat[:, :H * DV]).astype(BF16)
        dsg = dcat[:, H * DV:]

        vraw = v_ref[...]
        vg = _gelu(vraw)
        rg = lax.rsqrt(_seg_sum(vg * vg, seg[...]) * (1.0 / GD) + EPS)
        r64 = _seg_bcast(rg, segt[...])
        y = vg * r64
        wv_v = wv_ref[...]
        vn = (y * wv_v).astype(BF16)
        _gating(vn, ws_ref, bias_ref, s_scr, tm)
        uraw = u_ref[...]
        ug = _gelu(uraw)
        s = s_scr[...]
        sg = (ug * s).astype(BF16)
        du_ref[...] = (dsg * s * _gelu_grad(uraw)).astype(BF16)
        ds = dsg * ug
        dgate_ref[0] += _rowsum(dx2 * mix_ref[...].astype(F32))
        attn_c = _compact_heads(attn_ref[...]).astype(BF16)
        dwout_ref[...] += _mm_tn(jnp.concatenate([attn_c, sg], axis=-1), dmix)

        nch = tm // CH
        first = _first_group_lanes(nch)
        for c in range(nch):
            dbias_scr[...] += ds[c * CH:(c + 1) * CH, :]
        for j in range(G // 2):
            ls = slice(j * LANE, (j + 1) * LANE)
            dst32 = _chunks_side_by_side(ds, j, nch)
            dst = dst32.astype(BF16)
            vst = _chunks_side_by_side(vn, j, nch)
            dvn_st = jnp.where(first, _mm(wst_ref[2 * j], dst), _mm(wst_ref[2 * j + 1], dst))
            for c in range(nch):
                dvn_scr[c * CH:(c + 1) * CH, ls] = dvn_st[:, c * LANE:(c + 1) * LANE]
            dws_ref[2 * j] += _mm_nt(jnp.where(first, dst32, 0.0).astype(BF16), vst)
            dws_ref[2 * j + 1] += _mm_nt(jnp.where(first, 0.0, dst32).astype(BF16), vst)

        dvn = dvn_scr[...]
        dwv_ref[...] += _rowsum(dvn * y)
        dy = dvn * wv_v
        mean_g = _seg_sum(dy * y, seg[...]) * (1.0 / GD)
        dvg = r64 * (dy - y * _seg_bcast(mean_g, segt[...]))
        dv_ref[...] = (dvg * _gelu_grad(vraw)).astype(BF16)

        @pl.when(t == n_lat - 1)
        def _():
            dbs_ref[...] = _dot_hl(dbias_scr[...], seg[...])

    row = lambda cols: pl.BlockSpec((tm, cols), lambda t: (t, 0))
    r = n_lat * tm
    return _pcall(
        body, name="mix_bwd", grid=(n_lat,),
        out_shape=(_sds((r, HP), BF16), _sds((r, G * GD), BF16), _sds((r, G * GD), BF16), _sds((nrows, 1, D), F32),
                   _sds((wrows, D), F32), _sds((G, CH, CH), F32), _sds((CH, LANE), F32), _sds((1, G * GD), F32)),
        in_specs=[row(D), row(D), row(G * GD), row(G * GD), row(HP), _mod_spec(1, tpe, nrows), _const((1, G * GD)),
                  _const((G, CH, CH)), _const((G, CH, CH)), _const((CH, G * GD)), _const((wrows, D)),
                  _const((G * GD, LANE)), _const((2 * LANE, G * GD))],
        out_specs=(row(HP), row(G * GD), row(G * GD), _mod_spec(1, tpe, nrows), _const((wrows, D)),
                   _const((G, CH, CH)), _const((CH, LANE)), _const((1, G * GD))),
        scratch=[pltpu.VMEM((tm, G * GD), F32), pltpu.VMEM((tm, G * GD), F32), pltpu.VMEM((CH, G * GD), F32)],
        vmem_mb=56)(dx2, mix, u, v, attn, gate, wv, ws, wst, bias, wout, cs["seg_g"], cs["seg_gt"])


def _adamw_math(w, g, m, v):
    m2 = ADAM_B1 * m + (1.0 - ADAM_B1) * g
    v2 = ADAM_B2 * v + (1.0 - ADAM_B2) * (g * g)
    m_hat = m2 / (1.0 - ADAM_B1 ** ADAM_STEP)
    v_hat = v2 / (1.0 - ADAM_B2 ** ADAM_STEP)
    delta = -ADAM_LR * (m_hat / (jnp.sqrt(v_hat) + ADAM_EPS) + ADAM_WD * w)
    return delta, m2, v2


def _row_tile(r, c):
    best = r
    for tr in range(8, r, 8):
        if r % tr == 0 and tr * c * 4 <= MIB:
            best = tr
    return best


def _adamw(w, g, m, v, name):
    r, c = w.shape
    tr = _row_tile(r, c)

    def body(w_ref, g_ref, m_ref, v_ref, d_ref, mo_ref, vo_ref):
        d_ref[...], mo_ref[...], vo_ref[...] = _adamw_math(w_ref[...], g_ref[...], m_ref[...], v_ref[...])

    blk = pl.BlockSpec((tr, c), lambda t: (t, 0))
    return _pcall(body, name=name, grid=(r // tr,), out_shape=(_sds((r, c), F32),) * 3,
                  in_specs=[blk] * 4, out_specs=(blk,) * 3)(w, g, m, v)


def _adamw_small(params):
    n = len(params)

    def body(*refs):
        ins, outs = refs[:4 * n], refs[4 * n:]
        for i in range(n):
            w, g, m, v = (ins[4 * i + k][...] for k in range(4))
            if i == 0:
                sig = _sigmoid(w)
                g = g * (sig * (1.0 + w * (1.0 - sig)))
            d, m2, v2 = _adamw_math(w, g, m, v)
            outs[4 * i][...] = g
            outs[4 * i + 1][...] = d
            outs[4 * i + 2][...] = m2
            outs[4 * i + 3][...] = v2

    flat = [a for p in params for a in p]
    out_shape = tuple(_sds(p[0].shape, F32) for p in params for _ in range(4))
    res = _pcall(body, name="adamw_small", out_shape=out_shape, in_specs=[VMEM] * (4 * n),
                 out_specs=(VMEM,) * (4 * n))(*flat)
    return [res[4 * i:4 * i + 4] for i in range(n)]


def _rope_tables(s):
    rows = jnp.repeat(jnp.arange(s // GRID_W, dtype=F32), GRID_W)
    cols = jnp.tile(jnp.arange(GRID_W, dtype=F32), s // GRID_W)
    half = DR // 2
    inv = ROPE_BASE ** (-jnp.arange(0, half, 2, dtype=F32) / half)
    ang_r = rows[:, None] * inv
    ang_c = cols[:, None] * inv
    ang = jnp.concatenate([ang_r, ang_r, ang_c, ang_c], axis=-1)
    return jnp.cos(ang), jnp.sin(ang)


def _head_pad(a, real):
    return jnp.pad(a, ((0, 0), (0, LANE - real), (0, 0))).reshape(HP, a.shape[2])


def kernel(x, c, ctx, c_ctx, w_ada, b_ada, norm1_w, ffn1_w1, ffn1_w3, ffn1_w2, norm2_w, w_in, q_a_norm_w, w_uq, kv_a_norm_w, w_ukv, q_norm_w, k_norm_w, v_norm_w, w_s, b_s, w_out, norm3_w, ffn2_w1, ffn2_w3, ffn2_w2, loss_target, m_c_ctx, m_w_ada, m_b_ada, m_norm1_w, m_ffn1_w1, m_ffn1_w3, m_ffn1_w2, m_norm2_w, m_w_in, m_q_a_norm_w, m_w_uq, m_kv_a_norm_w, m_w_ukv, m_q_norm_w, m_k_norm_w, m_v_norm_w, m_w_s, m_b_s, m_w_out, m_norm3_w, m_ffn2_w1, m_ffn2_w3, m_ffn2_w2, v_c_ctx, v_w_ada, v_b_ada, v_norm1_w, v_ffn1_w1, v_ffn1_w3, v_ffn1_w2, v_norm2_w, v_w_in, v_q_a_norm_w, v_w_uq, v_kv_a_norm_w, v_w_ukv, v_q_norm_w, v_k_norm_w, v_v_norm_w, v_w_s, v_b_s, v_w_out, v_norm3_w, v_ffn2_w1, v_ffn2_w3, v_ffn2_w2):
    nb, s, _ = x.shape
    nc = ctx.shape[1]
    tm = 256 if nc % 256 == 0 else 128
    tpe = s // tm
    n_lat = nb * tpe
    n_all = n_lat + nb * nc // tm
    tmf = 2 * tm if s % (2 * tm) == 0 and (nb * nc) % (2 * tm) == 0 else tm
    tp = tmf
    r_lat = nb * s
    tpe_p, n_lat_p, n_all_p = s // tp, r_lat // tp, (r_lat + nb * nc) // tp
    me = 4 * lax.axis_index("x") + 2 * lax.axis_index("y") + lax.axis_index("c")
    cs = _consts()
    ncol = w_ada.shape[2]
    fsh = ffn1_w1.shape[2]
    assert nb + 1 <= 8 and NDEV * fsh == FF and NDEV * ncol == NMOD * D and s % nc == 0 and nc % tm == 0

    def t16(a):
        return a.T.astype(BF16)

    wpack1 = jnp.concatenate([t16(ffn1_w1[0]), t16(ffn1_w3[0]), ffn1_w2[0].astype(BF16)], axis=0)
    a_loc = jnp.concatenate([c, c_ctx[None, :], jnp.zeros((7 - nb, D), F32)], axis=0)
    a_raw, _, mod_all, wall1 = _ada_front(a_loc, w_ada[0], lax.dynamic_slice_in_dim(b_ada, me * ncol, ncol, axis=1),
                                          wpack1)
    a_raw = a_raw.reshape(NDEV * 8, D)
    mod_mine = lax.dynamic_slice_in_dim(mod_all, 8 * me, 8, axis=1)
    modtab = mod_mine.transpose(1, 0, 2).reshape(8, NMOD, D)[:nb + 1]
    wpack2 = jnp.concatenate([
        t16(ffn2_w1[0]), t16(ffn2_w3[0]), ffn2_w2[0].astype(BF16),
        t16(w_in[0]), jnp.zeros((12, D), BF16),
        w_out[0].astype(BF16),
        t16(w_uq[0]).reshape(24, D), jnp.zeros((8, D), BF16),
        t16(w_ukv[0]).reshape(16, D)], axis=0)

    def head_w(wn):
        return jnp.tile(jnp.pad(wn, ((0, 0), (0, LANE - DH))), (1, H))

    wq, wk = head_w(q_norm_w), head_w(k_norm_w)
    wv = v_norm_w.reshape(1, G * GD)
    ws16 = w_s[0].astype(BF16)
    wst16 = w_s[0].transpose(0, 2, 1).astype(BF16)
    bias = jnp.repeat(b_s[0].T, GD, axis=1)
    cos, sin = _rope_tables(s)
    cos = jnp.pad(cos, ((0, 0), (DN, LANE - DH)), constant_values=1.0)
    sin = jnp.pad(sin, ((0, 0), (DN, LANE - DH)))
    cos_k = jnp.concatenate([cos, jnp.ones((tm, LANE), F32)], axis=0)
    sin_k = jnp.concatenate([sin, jnp.zeros((tm, LANE), F32)], axis=0)

    xs = (x.reshape(r_lat, D), ctx.reshape(nb * nc, D))
    x1, a1, b1, o1, wall2 = _ffn_fwd(xs, modtab[:, 0:3], norm1_w, wall1, 0, tm=tmf, n_tiles=(r_lat + nb * nc) // tmf,
                                     tpe=s // tmf, n_lat=r_lat // tmf, name="ffn1_fwd", gather=wpack2)

    o0 = 3 * fsh
    wint = wall2[:, o0:o0 + 180].reshape(IN_COLS, D)
    z = lambda n: jnp.zeros((n, D), BF16)
    wint = jnp.concatenate([wint[0:128], wint[160:416], wint[416:928], wint[928:1440],
                            z(DN), wint[128:160], z(LANE - DH)], axis=0)
    wout = wall2[:, o0 + 192:o0 + 320].reshape(D, D)
    wuq = _head_pad(wall2[:, o0 + 320:o0 + 344].reshape(H, DH, QL), DH)
    wukv = wall2[:, o0 + 352:o0 + 368].reshape(HP, KVL)

    ckv, qp, u_raw, v_raw, kpe = _proj_fwd(x1, modtab[:, 3:5], norm2_w, wint, tm=tp, n_tiles=n_all_p, tpe=tpe_p)
    q = _q_prep_fwd(qp, q_a_norm_w, wuq, wq, cos, sin, cs, tm=tm, n_lat=n_lat, tpe=tpe)
    k, v = _kv_prep_fwd(ckv, kpe, kv_a_norm_w, wukv, wk, cos_k, sin_k, cs,
                        tm=tm, n_tiles=n_all, tpe=tpe, n_lat=n_lat)
    attn, lse = _attn_fwd(q, k, v, nb=nb, s=s, nc=nc, tq=tm, ck=2048)
    x2, mix = _mix_fwd(u_raw, v_raw, attn, x1, modtab[:nb, 5:6], wv, ws16, bias, wout, cs,
                       tm=tp, n_lat=n_lat_p, tpe=tpe_p)
    dy, a2, b2, o2, lsum = _ffn_fwd((x2,), modtab[:nb, 6:9], norm3_w, wall2, 0, tm=tmf, n_tiles=r_lat // tmf,
                                    tpe=s // tmf, n_lat=r_lat // tmf, name="ffn2_fwd",
                                    target=loss_target.reshape(r_lat, D))

    tr = 2 * tm if n_lat % 2 == 0 and n_all % 2 == 0 else tm
    dx2, da2, db2, g2, do2, h2, dmod678, dnorm3 = _ffn_bwd_dx(
        dy, (x2,), a2, b2, o2, modtab[:nb, 6:9], norm3_w, wall2, 0,
        tm=tm, n_tiles=n_lat, tpe=tpe, n_lat=n_lat, name="ffn2_bwd_dx")
    g_ffn2 = _ffn_bwd_dw(h2, do2, da2, db2, g2, tr=tr, name="ffn2_bwd_dw")

    dattn, du, dv, dgate5, dwout, dws, dbs, dwv = _mix_bwd(
        dx2, mix, u_raw, v_raw, attn, modtab[:nb, 5:6], wv, ws16, wst16, bias, wout, cs, tm=tp, n_lat=n_lat_p, tpe=tpe_p)
    tq = 2 * tm if s % (2 * tm) == 0 else tm
    dq, dk_l, dk_c, dv_l, dv_c, recv_ffn2 = _attn_bwd(q, k, v, attn, dattn, lse, g_ffn2,
                                                      nb=nb, s=s, nc=nc, tq=tq, ck=1024)
    dqp, dwuq, dqa, dwq = _q_prep_bwd(dq, qp, q_a_norm_w, wuq, wq, cos, sin, cs, tm=tm, n_lat=n_lat, tpe=tpe)
    dckv, dkpe, dwukv, dkva, dwk = _kv_prep_bwd((dk_l, dk_c), (dv_l, dv_c), ckv, kpe, kv_a_norm_w, wukv, wk,
                                                cos_k, sin_k, cs, tm=tm, n_tiles=n_all, tpe=tpe, n_lat=n_lat)
    dx1, dwin, dmod34, dnorm2 = _proj_bwd(dckv, dkpe, dqp, du, dv, dx2, x1, modtab[:, 3:5], norm2_w, wint,
                                          tm=tp, n_tiles=n_all_p, tpe=tpe_p, n_lat=n_lat_p)

    def blocks(a):
        return a.reshape(NDEV, a.shape[0] // NDEV, D)

    dwin_o = jnp.concatenate([dwin[0:128], dwin[KPE_LO:KPE_LO + DR], dwin[128:384], dwin[384:896], dwin[896:1408]],
                             axis=0)
    dwuq_o = dwuq.reshape(H, LANE, QL)[:, :DH]
    gmisc = jnp.concatenate([
        blocks(dwin_o).astype(BF16), jnp.zeros((NDEV, 12, D), BF16),
        blocks(dwout).astype(BF16),
        dwuq_o.reshape(NDEV, 24, D).astype(BF16), jnp.zeros((NDEV, 8, D), BF16),
        dwukv.reshape(NDEV, 16, D).astype(BF16)], axis=1)

    dx0, da1, db1, g1, do1, h1, dmod012, dnorm1 = _ffn_bwd_dx(
        dx1, xs, a1, b1, o1, modtab[:, 0:3], norm1_w, wall1, 0,
        tm=tm, n_tiles=n_all, tpe=tpe, n_lat=n_lat, name="ffn1_bwd_dx")
    grad_x = dx0.reshape(nb, s, D)
    g_w1, recv_misc = _ffn_bwd_dw_one(da1, h1, tr=tr, name="ffn1_bwd_dw1", part=gmisc)
    g_w3, recv_w1 = _ffn_bwd_dw_one(db1, h1, tr=tr, name="ffn1_bwd_dw3", part=g_w1)
    g_w2, recv_w3 = _ffn_bwd_dw_one(g1, do1, tr=tr, name="ffn1_bwd_dw2", part=g_w3)

    zrow = jnp.zeros((1, D), F32)
    g_lat = jnp.concatenate([dmod012[:nb, 0], dmod012[:nb, 1], dmod012[:nb, 2], dmod34[:nb, 0], dmod34[:nb, 1],
                             dgate5[:, 0], dmod678[:, 0], dmod678[:, 1], dmod678[:, 2]], axis=1)
    g_ctx = jnp.concatenate([dmod012[nb:, 0], dmod012[nb:, 1], dmod012[nb:, 2], dmod34[nb:, 0], dmod34[nb:, 1],
                             zrow, zrow, zrow, zrow], axis=1)
    g_loc = jnp.concatenate([g_lat, g_ctx, jnp.zeros((7 - nb, NMOD * D), F32)], axis=0)

    got_w2, g_all = _scatter_sibling([g_w2], "scatter_sibling_w2", gather=g_loc)
    g_all = g_all.reshape(NDEV * 8, NMOD * D)
    g_cols = lax.dynamic_slice_in_dim(g_all, me * ncol, ncol, axis=1)
    g_w_ada, pc_ctx, g_b_ada = _ada_bwd(a_raw, c_ctx.reshape(D, 1), g_all, g_cols, w_ada[0], nb)
    part_w2 = _add_sibling(g_w2, got_w2, 176, "add_sibling_w2")

    def prow(a):
        a = a.reshape(1, -1)
        return jnp.concatenate([a, jnp.zeros((1, D - a.shape[1]), F32)], axis=1)

    g_qn = dwq.reshape(H, LANE)[:, :DH].sum(0)
    g_kn = dwk.reshape(H, LANE)[:, :DH].sum(0)
    spack = jnp.concatenate([
        dnorm1, dnorm2, dnorm3, prow(dqa), prow(dkva), prow(g_qn), prow(g_kn), prow(dwv),
        prow(dbs[:, :G].T), prow(pc_ctx), prow(lsum[0:1]), jnp.zeros((5, D), F32), dws.reshape(CH, D)],
        axis=0)
    recv_w2, small_all = _scatter_chips([part_w2], "scatter_chips", gather=spack)
    ssum = _sum_slots(small_all, 144, "sum_small")
    loss = ssum[10, 0] * (0.5 / D)
    gsum2 = _sum_direct(g_ffn2, recv_ffn2, 176, "sum_grads_ffn2")
    msum = _sum_direct(gmisc, recv_misc, 368, "sum_grads_misc")

    transposed = ("ffn1_w1", "ffn1_w3", "ffn2_w1", "ffn2_w3", "w_in", "w_uq")
    g_big = {
        "ffn1_w1": _sum_direct(g_w1, recv_w1, 176, "sum_grads_w1"),
        "ffn1_w3": _sum_direct(g_w3, recv_w3, 176, "sum_grads_w3"),
        "ffn1_w2": _sum_chips(part_w2, recv_w2, 176, "sum_grads_w2"),
        "ffn2_w1": gsum2[0:fsh], "ffn2_w3": gsum2[fsh:2 * fsh], "ffn2_w2": gsum2[2 * fsh:3 * fsh],
        "w_in": msum[0:180], "w_out": msum[192:320],
        "w_uq": msum[320:344].reshape(DH, QL), "w_ukv": msum[352:368].reshape(DN + DV, KVL).T,
        "w_ada": g_w_ada,
    }

    big_in = {
        "w_ada": (w_ada, m_w_ada, v_w_ada), "ffn1_w1": (ffn1_w1, m_ffn1_w1, v_ffn1_w1),
        "ffn1_w3": (ffn1_w3, m_ffn1_w3, v_ffn1_w3), "ffn1_w2": (ffn1_w2, m_ffn1_w2, v_ffn1_w2),
        "w_in": (w_in, m_w_in, v_w_in), "w_uq": (w_uq, m_w_uq, v_w_uq), "w_ukv": (w_ukv, m_w_ukv, v_w_ukv),
        "w_out": (w_out, m_w_out, v_w_out), "ffn2_w1": (ffn2_w1, m_ffn2_w1, v_ffn2_w1),
        "ffn2_w3": (ffn2_w3, m_ffn2_w3, v_ffn2_w3), "ffn2_w2": (ffn2_w2, m_ffn2_w2, v_ffn2_w2),
    }
    res = {}
    for nm, (w, m, v_) in big_in.items():
        g = g_big[nm]
        if nm in transposed:
            d_, m_, v2_ = _adamw(w[0].T, g, m[0].T, v_[0].T, "adamw_" + nm)
            res[nm] = tuple(a.T[None] for a in (g, d_, m_, v2_))
        else:
            d_, m_, v2_ = _adamw(w[0], g, m[0], v_[0], "adamw_" + nm)
            res[nm] = tuple(a[None] for a in (g, d_, m_, v2_))

    small_in = [
        ("c_ctx", c_ctx, m_c_ctx, v_c_ctx, ssum[9:10], (1, D)),
        ("b_ada", b_ada, m_b_ada, v_b_ada, g_b_ada, (1, NMOD * D)),
        ("norm1_w", norm1_w, m_norm1_w, v_norm1_w, ssum[0:1], (1, D)),
        ("norm2_w", norm2_w, m_norm2_w, v_norm2_w, ssum[1:2], (1, D)),
        ("norm3_w", norm3_w, m_norm3_w, v_norm3_w, ssum[2:3], (1, D)),
        ("q_a_norm_w", q_a_norm_w, m_q_a_norm_w, v_q_a_norm_w, ssum[3:4, :QL], (1, QL)),
        ("kv_a_norm_w", kv_a_norm_w, m_kv_a_norm_w, v_kv_a_norm_w, ssum[4:5, :KVL], (1, KVL)),
        ("q_norm_w", q_norm_w, m_q_norm_w, v_q_norm_w, ssum[5:6, :DH], (1, DH)),
        ("k_norm_w", k_norm_w, m_k_norm_w, v_k_norm_w, ssum[6:7, :DH], (1, DH)),
        ("v_norm_w", v_norm_w, m_v_norm_w, v_v_norm_w, ssum[7:8, :G * GD], (G, GD)),
        ("b_s", b_s, m_b_s, v_b_s, ssum[8:9], (G, CH)),
        ("w_s", w_s, m_w_s, v_w_s, ssum[16:144], (G * CH, CH)),
    ]
    small_out = _adamw_small(
        [(w.reshape(sh), g.reshape(sh), m.reshape(sh), v_.reshape(sh)) for _, w, m, v_, g, sh in small_in])
    for (nm, w, *_), outs in zip(small_in, small_out):
        res[nm] = tuple(a.reshape(w.shape) for a in outs)

    order = ["c_ctx", "w_ada", "b_ada", "norm1_w", "ffn1_w1", "ffn1_w3", "ffn1_w2", "norm2_w", "w_in", "q_a_norm_w",
             "w_uq", "kv_a_norm_w", "w_ukv", "q_norm_w", "k_norm_w", "v_norm_w", "w_s", "b_s", "w_out", "norm3_w",
             "ffn2_w1", "ffn2_w3", "ffn2_w2"]
    return (loss, grad_x, *[res[n][0] for n in order], *[res[n][1] for n in order],
            *[res[n][2] for n in order], *[res[n][3] for n in order])
```

```python
import numpy as np
import jax
import jax.numpy as jnp
from jax import lax
from jax.experimental import pallas as pl
from jax.experimental.pallas import tpu as pltpu

F32 = jnp.float32
BF16 = jnp.bfloat16

D = 1024
FF = 2816
FC = 256
H = 8
DN, DR, DV = 64, 32, 64
DH = DN + DR
QL, KVL = 256, 128
G, GD, CH = 8, 64, 128
NMOD = 9
EPS = 1e-6
GRID_W = 64
ROPE_BASE = 10000.0
NDEV = 8
LANE = 128
HP = H * LANE
IN_COLS = 1440
WIN_ROWS = 1536
KPE_LO = 1408 + DN
MIB = 1 << 20

ADAM_LR, ADAM_B1, ADAM_B2, ADAM_EPS, ADAM_WD, ADAM_STEP = 0.001, 0.9, 0.999, 1e-08, 0.01, 10

MESH = pl.DeviceIdType.MESH
ANY = pl.BlockSpec(memory_space=pl.ANY)
VMEM = pl.BlockSpec(memory_space=pltpu.VMEM)


def _mm(a, b):
    return jnp.dot(a, b, preferred_element_type=F32)


def _mm_nt(a, b):
    return lax.dot_general(a, b, (((1,), (1,)), ((), ())), preferred_element_type=F32)


def _mm_tn(a, b):
    return lax.dot_general(a, b, (((0,), (0,)), ((), ())), preferred_element_type=F32)


def _dot_hl(x, m):
    hi = x.astype(BF16)
    lo = (x - hi.astype(F32)).astype(BF16)
    return _mm(hi, m) + _mm(lo, m)


def _sigmoid(a):
    return 1.0 / (1.0 + jnp.exp(-a))


_G0 = 0.7978845608028654
_G1 = 0.044715


def _gelu(x):
    return 0.5 * x * (1.0 + jnp.tanh(_G0 * (x + _G1 * (x * x * x))))


def _gelu_grad(x):
    th = jnp.tanh(_G0 * (x + _G1 * (x * x * x)))
    return 0.5 * (1.0 + th) + 0.5 * x * (1.0 - th * th) * (_G0 * (1.0 + 3.0 * _G1 * x * x))


def _rowsum(y):
    return jnp.sum(y, axis=0, keepdims=True)


def _rms(x):
    return lax.rsqrt(jnp.mean(x * x, axis=-1, keepdims=True) + EPS)


def _pcall(body, *, name, out_shape, in_specs, out_specs, grid=None, scratch=(), vmem_mb=32, aliases=None):
    kw = {}
    if grid is not None:
        kw["grid"] = grid
        sem = ("arbitrary",) * len(grid)
    else:
        sem = None
    if aliases:
        kw["input_output_aliases"] = aliases
    return pl.pallas_call(
        body, name=name, out_shape=out_shape, in_specs=in_specs, out_specs=out_specs,
        scratch_shapes=list(scratch),
        compiler_params=pltpu.CompilerParams(dimension_semantics=sem, vmem_limit_bytes=vmem_mb * MIB),
        **kw)


def _const(shape):
    nd = len(shape)
    return pl.BlockSpec(shape, lambda *_: (0,) * nd)


def _sds(shape, dt):
    return jax.ShapeDtypeStruct(shape, dt)


def _consts():
    seg_h = np.zeros((HP, LANE), np.float32)
    seg_h[np.arange(HP), np.arange(HP) // LANE] = 1.0
    seg_g = np.zeros((G * GD, LANE), np.float32)
    seg_g[np.arange(G * GD), np.arange(G * GD) // GD] = 1.0
    rot = np.zeros((LANE, LANE), np.float32)
    for base in (DN, DN + 16):
        for j in range(8):
            rot[base + j + 8, base + j] = -1.0
            rot[base + j, base + j + 8] = 1.0
    rot2 = np.zeros((2 * LANE, 2 * LANE), np.float32)
    rot2[:LANE, :LANE] = rot
    rot2[LANE:, LANE:] = rot
    twice = lambda m: np.concatenate([m, m], axis=0)
    c = dict(seg_h=seg_h, seg_ht=twice(seg_h.T), seg_g=seg_g, seg_gt=twice(seg_g.T), rot=rot2, rot_t=rot2.T)
    return {k: jnp.asarray(v, BF16) for k, v in c.items()}


_GATHER_SEMS = [pltpu.SemaphoreType.DMA((7,)), pltpu.SemaphoreType.DMA((7,)), pltpu.SemaphoreType.DMA(())]


def _gather_phases(x_ref, out_ref, send_sems, recv_sems, local_sem):
    mx, my, mc = lax.axis_index("x"), lax.axis_index("y"), lax.axis_index("c")
    me, sibling = (mx, my, mc), (mx, my, 1 - mc)
    chips = [(1 - mx, my), (mx, 1 - my), (1 - mx, 1 - my)]

    def blk(px, py, pc):
        return out_ref.at[4 * px + 2 * py + pc]

    def copy(k, block, to, src=None):
        return pltpu.make_async_remote_copy(
            src_ref=blk(*block) if src is None else src, dst_ref=blk(*block),
            send_sem=send_sems.at[k], recv_sem=recv_sems.at[k], device_id=to, device_id_type=MESH)

    mine = pltpu.make_async_copy(x_ref, blk(*me), local_sem)
    first = [copy(0, me, sibling, src=x_ref)]
    first += [copy(1 + j, me, (*chip, mc), src=x_ref) for j, chip in enumerate(chips)]
    passed = [copy(4 + j, (*chip, mc), sibling) for j, chip in enumerate(chips)]

    def start():
        mine.start()
        for cp in first:
            cp.start()

    def forward():
        for j, chip in enumerate(chips):
            copy(1 + j, (*chip, mc), me).wait_recv()
            passed[j].start()

    def finish():
        copy(0, sibling, me).wait_recv()
        for j, chip in enumerate(chips):
            copy(4 + j, (*chip, 1 - mc), me).wait_recv()
        for cp in first + passed:
            cp.wait_send()
        mine.wait()

    return start, forward, finish


def _chip_sends(p_ref, out_ref, send_sems, recv_sems):
    mx, my, mc = lax.axis_index("x"), lax.axis_index("y"), lax.axis_index("c")
    peers = [(1 - mx, my), (mx, 1 - my), (1 - mx, 1 - my)]
    return [pltpu.make_async_remote_copy(
        src_ref=p_ref.at[2 * px + py], dst_ref=out_ref.at[j], send_sem=send_sems.at[j], recv_sem=recv_sems.at[j],
        device_id=(px, py, mc), device_id_type=MESH) for j, (px, py) in enumerate(peers)]


def _with_gather(copies_of, n, shapes, sems, gather, name, args):
    ns = len(sems)

    def body(*refs):
        ng = 1 if gather is not None else 0
        ins, outs = refs[:n], refs[n + ng:2 * n + ng]
        copies = copies_of(ins, outs, refs[2 * n + 2 * ng:2 * n + 2 * ng + ns])
        if ng:
            start, forward, finish = _gather_phases(refs[n], refs[2 * n + 1], *refs[2 * n + 2 + ns:])
            start()
        for cp in copies:
            cp.start()
        if ng:
            forward()
        for cp in copies:
            cp.wait_recv()
        for cp in copies:
            cp.wait_send()
        if ng:
            finish()

    in_specs, out_shape, scratch = [ANY] * n, list(shapes), list(sems)
    if gather is not None:
        in_specs.append(ANY)
        args = list(args) + [gather]
        out_shape.append(_sds((NDEV,) + gather.shape, gather.dtype))
        scratch += _GATHER_SEMS
    return pl.pallas_call(body, name=name, out_shape=tuple(out_shape), in_specs=in_specs,
                          out_specs=(ANY,) * len(out_shape), scratch_shapes=scratch)(*args)


def _scatter_sibling(xs, name, gather=None):
    n = len(xs)

    def copies_of(x_refs, got_refs, sems):
        send_sems, recv_sems = sems
        mx, my, mc = lax.axis_index("x"), lax.axis_index("y"), lax.axis_index("c")
        return [pltpu.make_async_remote_copy(
            src_ref=x_refs[i].at[2 * j + 1 - mc], dst_ref=got_refs[i].at[j],
            send_sem=send_sems.at[4 * i + j], recv_sem=recv_sems.at[4 * i + j],
            device_id=(mx, my, 1 - mc), device_id_type=MESH) for i in range(n) for j in range(4)]

    shapes = tuple(_sds((4,) + x.shape[1:], x.dtype) for x in xs)
    return _with_gather(copies_of, n, shapes, [pltpu.SemaphoreType.DMA((4 * n,))] * 2, gather, name, xs)


def _scatter_chips(ps, name, gather=None):
    n = len(ps)

    def copies_of(p_refs, out_refs, sems):
        sends = []
        for i in range(n):
            sends += _chip_sends(p_refs[i], out_refs[i], sems[2 * i], sems[2 * i + 1])
        return sends

    shapes = tuple(_sds((3,) + p.shape[1:], p.dtype) for p in ps)
    return _with_gather(copies_of, n, shapes, [pltpu.SemaphoreType.DMA((3,))] * (2 * n), gather, name, ps)


def _add_sibling(x, got, tr, name):
    _, r, c = x.shape

    def body(x_ref, g_ref, o_ref):
        mc = lax.axis_index("c")
        for j in range(4):
            mine = jnp.where(mc == 0, x_ref[2 * j].astype(F32), x_ref[2 * j + 1].astype(F32))
            o_ref[j] = (mine + g_ref[j].astype(F32)).astype(o_ref.dtype)

    return _pcall(body, name=name, grid=(r // tr,), out_shape=_sds(got.shape, got.dtype),
                  in_specs=[pl.BlockSpec((NDEV, tr, c), lambda t: (0, t, 0)), pl.BlockSpec((4, tr, c), lambda t: (0, t, 0))],
                  out_specs=pl.BlockSpec((4, tr, c), lambda t: (0, t, 0)))(x, got)


def _sum_slots(x, tr, name):
    n, r, c = x.shape

    def body(x_ref, o_ref):
        acc = x_ref[0].astype(F32)
        for s in range(1, n):
            acc = acc + x_ref[s].astype(F32)
        o_ref[...] = acc

    return _pcall(body, name=name, grid=(r // tr,), out_shape=_sds((r, c), F32),
                  in_specs=[pl.BlockSpec((n, tr, c), lambda t: (0, t, 0))],
                  out_specs=pl.BlockSpec((tr, c), lambda t: (t, 0)))(x)


def _ada_front(a_loc, w_loc, b_loc, wpack):
    ncol = w_loc.shape[1]
    nrow = NDEV * a_loc.shape[0]

    def body(a_ref, w_ref, b_ref, wp_ref, araw_ref, mloc_ref, mall_ref, wall_ref,
             a_vm, w_vm, m_vm, lsem, *sems):
        a_start, a_forward, a_finish = _gather_phases(a_ref, araw_ref, *sems[0:3])
        m_start, m_forward, m_finish = _gather_phases(mloc_ref, mall_ref, *sems[3:6])
        w_start, w_forward, w_finish = _gather_phases(wp_ref, wall_ref, *sems[6:9])
        w_in = pltpu.make_async_copy(w_ref, w_vm, lsem.at[0])
        w_in.start()
        a_start()
        w_start()
        a_forward()
        a_finish()
        a_in = pltpu.make_async_copy(araw_ref, a_vm, lsem.at[1])
        a_in.start()
        a_in.wait()
        w_in.wait()
        a = a_vm[...].reshape(nrow, D)
        act = (a * _sigmoid(a)).astype(BF16)
        m_vm[...] = _mm(act, w_vm[...].astype(BF16)) + b_ref[...]
        m_out = pltpu.make_async_copy(m_vm, mloc_ref, lsem.at[2])
        m_out.start()
        m_out.wait()
        m_start()
        m_forward()
        m_finish()
        w_forward()
        w_finish()

    return pl.pallas_call(
        body, name="ada_front",
        out_shape=(_sds((NDEV,) + a_loc.shape, F32), _sds((nrow, ncol), F32), _sds((NDEV, nrow, ncol), F32),
                   _sds((NDEV,) + wpack.shape, wpack.dtype)),
        in_specs=[ANY, ANY, VMEM, ANY], out_specs=(ANY, ANY, ANY, ANY),
        scratch_shapes=[pltpu.VMEM((NDEV,) + a_loc.shape, F32), pltpu.VMEM(w_loc.shape, F32),
                        pltpu.VMEM((nrow, ncol), F32), pltpu.SemaphoreType.DMA((3,))] + _GATHER_SEMS * 3,
        compiler_params=pltpu.CompilerParams(vmem_limit_bytes=32 * MIB),
    )(a_loc, w_loc, b_loc, wpack)


def _ada_bwd(a_raw, cctx_col, g_all, g_cols, w_loc, nb):
    nrow = a_raw.shape[0]
    ncol = w_loc.shape[1]

    def body(a_ref, cc_ref, gall_ref, g_ref, w_ref, dw_ref, pc_ref, gb_ref):
        a = a_ref[...]
        rowid = lax.broadcasted_iota(jnp.int32, (nrow, 1), 0) % 8
        act = jnp.where(rowid < nb, a * _sigmoid(a), 0.0).astype(BF16)
        g = g_ref[...]
        gc = _rowsum(jnp.where(rowid == nb, g, 0.0))
        cc = cc_ref[...]
        dw_ref[...] = _mm_tn(act, g.astype(BF16)) + (cc * _sigmoid(cc)) * gc
        pc_ref[...] = jnp.sum(w_ref[...] * gc, axis=1, keepdims=True)
        gb_ref[...] = _rowsum(gall_ref[...])

    return _pcall(body, name="ada_bwd",
                  out_shape=(_sds((D, ncol), F32), _sds((D, 1), F32), _sds((1, g_all.shape[1]), F32)),
                  in_specs=[VMEM] * 5, out_specs=(VMEM,) * 3, vmem_mb=48)(a_raw, cctx_col, g_all, g_cols, w_loc)


def _mod_spec(k, tpe, nrows):
    return pl.BlockSpec((1, k, D), lambda t: (jnp.minimum(t // tpe, nrows - 1), 0, 0))


def _load_ffn_weights(wall_ref, first, bufs, sems):
    fsh = FF // NDEV
    cps = []
    for j, buf in enumerate(bufs):
        for d in range(NDEV):
            cps.append(pltpu.make_async_copy(wall_ref.at[d, pl.ds((first + j) * fsh, fsh)],
                                             buf.at[pl.ds(d * fsh, fsh)], sems.at[j * NDEV + d]))
    for cp in cps:
        cp.start()
    for cp in cps:
        cp.wait()


def _token_specs(xs, tm, n_lat):
    specs = [pl.BlockSpec((tm, D), lambda t: (jnp.minimum(t, n_lat - 1), 0))]
    if len(xs) == 2:
        specs.append(pl.BlockSpec((tm, D), lambda t: (jnp.maximum(t - n_lat, 0), 0)))
    return specs


def _ffn_fwd(xs, mod3, norm_w, wall, first, *, tm, n_tiles, tpe, n_lat, name, target=None, gather=None):
    nrows = mod3.shape[0]
    r = n_tiles * tm
    nx = len(xs)
    with_loss = target is not None
    with_gather = gather is not None
    fwd_step = max(2 * n_tiles // 3, 1)

    def body(*refs):
        x_refs = refs[:nx]
        pos = nx
        if with_loss:
            tgt_ref = refs[pos]
            pos += 1
        mod_ref, nw_ref, wall_ref = refs[pos:pos + 3]
        pos += 3
        if with_gather:
            gin_ref = refs[pos]
            pos += 1
        xo_ref, a_ref, b_ref, o_ref = refs[pos:pos + 4]
        pos += 4
        if with_loss:
            ls_ref = refs[pos]
            pos += 1
        if with_gather:
            gout_ref = refs[pos]
            pos += 1
        w1_ref, w3_ref, w2_ref, wsem, acc_ref = refs[pos:pos + 5]
        t = pl.program_id(0)
        if with_gather:
            g_start, g_forward, g_finish = _gather_phases(gin_ref, gout_ref, *refs[pos + 5:])

        @pl.when(t == 0)
        def _():
            if with_gather:
                g_start()
            _load_ffn_weights(wall_ref, first, (w1_ref, w3_ref, w2_ref), wsem)
            if with_loss:
                ls_ref[...] = jnp.zeros_like(ls_ref)

        if with_gather:
            @pl.when(t == fwd_step)
            def _():
                g_forward()

            @pl.when(t == n_tiles - 1)
            def _():
                g_finish()

        x = x_refs[0][...]
        if nx == 2:
            x = jnp.where(t < n_lat, x, x_refs[1][...])
        n = x * _rms(x) * nw_ref[...]
        shift, scale, gate = mod_ref[0, 0:1, :], mod_ref[0, 1:2, :], mod_ref[0, 2:3, :]
        h = (n * (1.0 + scale) + shift).astype(BF16)
        nch = FF // FC
        o = None
        for lo_c, hi_c in ((0, nch // 2), (nch // 2, nch)):
            for j in range(lo_c, hi_c):
                sl = slice(j * FC, (j + 1) * FC)
                a = _mm_nt(h, w1_ref[sl, :])
                b = _mm_nt(h, w3_ref[sl, :])
                a_ref[:, sl] = a.astype(BF16)
                b_ref[:, sl] = b.astype(BF16)
                acc_ref[:, sl] = (a * _sigmoid(a) * b).astype(BF16)
            gs = slice(lo_c * FC, hi_c * FC)
            part = _mm(acc_ref[:, gs], w2_ref[gs, :])
            o = part if o is None else o + part
        o_ref[...] = o.astype(BF16)
        out = x + (0.5 * gate) * o
        if with_loss:
            d = out - tgt_ref[...]
            xo_ref[...] = d * (1.0 / D)
            ls_ref[...] += jnp.sum(d * d)
        else:
            xo_ref[...] = out

    row = lambda cols: pl.BlockSpec((tm, cols), lambda t: (t, 0))
    in_specs = _token_specs(xs, tm, n_lat) + ([row(D)] if with_loss else []) + [
        _mod_spec(3, tpe, nrows), _const((1, D)), ANY]
    out_shape = [_sds((r, D), F32), _sds((r, FF), BF16), _sds((r, FF), BF16), _sds((r, D), BF16)]
    out_specs = [row(D), row(FF), row(FF), row(D)]
    scratch = [pltpu.VMEM((FF, D), BF16)] * 3 + [pltpu.SemaphoreType.DMA((3 * NDEV,)), pltpu.VMEM((tm, FF), BF16)]
    if with_loss:
        out_shape.append(_sds((8, LANE), F32))
        out_specs.append(_const((8, LANE)))
    args = list(xs) + ([target] if with_loss else []) + [mod3, norm_w, wall]
    if with_gather:
        assert n_tiles >= 2
        in_specs.append(ANY)
        args.append(gather)
        out_shape.append(_sds((NDEV,) + gather.shape, gather.dtype))
        out_specs.append(ANY)
        scratch += _GATHER_SEMS
    return _pcall(
        body, name=name, grid=(n_tiles,), out_shape=tuple(out_shape), in_specs=in_specs, out_specs=tuple(out_specs),
        scratch=scratch, vmem_mb=56)(*args)


def _ffn_bwd_dx(dout, xs, a, b, o, mod3, norm_w, wall, first, *, tm, n_tiles, tpe, n_lat, name):
    nrows = mod3.shape[0]
    r = n_tiles * tm
    nx = len(xs)

    def body(*refs):
        dout_ref = refs[0]
        x_refs = refs[1:1 + nx]
        (a_ref, b_ref, o_ref, mod_ref, nw_ref, wall_ref,
         dx_ref, da_ref, db_ref, g_ref, do_ref, h_ref, dmod_ref, dnw_ref,
         w1_ref, w3_ref, w2_ref, wsem) = refs[1 + nx:]
        t = pl.program_id(0)

        @pl.when(t == 0)
        def _():
            _load_ffn_weights(wall_ref, first, (w1_ref, w3_ref, w2_ref), wsem)
            dnw_ref[...] = jnp.zeros_like(dnw_ref)

        @pl.when(jnp.where(t < n_lat, t % tpe == 0, t == n_lat))
        def _():
            dmod_ref[...] = jnp.zeros_like(dmod_ref)

        x = x_refs[0][...]
        if nx == 2:
            x = jnp.where(t < n_lat, x, x_refs[1][...])
        dout = dout_ref[...]
        shift, scale, gate = mod_ref[0, 0:1, :], mod_ref[0, 1:2, :], mod_ref[0, 2:3, :]
        d_o = ((0.5 * gate) * dout).astype(BF16)
        do_ref[...] = d_o
        nch = FF // FC
        groups = ((0, nch // 2), (nch // 2, nch))
        dh = None
        for lo_c, hi_c in groups:
            for j in range(lo_c, hi_c):
                sl = slice(j * FC, (j + 1) * FC)
                av = a_ref[:, sl].astype(F32)
                bv = b_ref[:, sl].astype(F32)
                dg = _mm_nt(d_o, w2_ref[sl, :])
                sig = _sigmoid(av)
                sa = av * sig
                g_ref[:, sl] = (sa * bv).astype(BF16)
                da_ref[:, sl] = (dg * bv * (sig * (1.0 + av * (1.0 - sig)))).astype(BF16)
                db_ref[:, sl] = (dg * sa).astype(BF16)
            gs = slice(lo_c * FC, hi_c * FC)
            part = _mm(da_ref[:, gs], w1_ref[gs, :]) + _mm(db_ref[:, gs], w3_ref[gs, :])
            dh = part if dh is None else dh + part
        rr = _rms(x)
        xh = x * rr
        nw = nw_ref[...]
        n = xh * nw
        h_ref[...] = (n * (1.0 + scale) + shift).astype(BF16)
        dgate = _rowsum(0.5 * o_ref[...].astype(F32) * dout)
        dn = dh * (1.0 + scale)
        dxh = dn * nw
        dmod_ref[0, 0:1, :] += _rowsum(dh)
        dmod_ref[0, 1:2, :] += _rowsum(dh * n)
        dmod_ref[0, 2:3, :] += dgate
        dnw_ref[...] += _rowsum(dn * xh)
        dx = dout + rr * (dxh - xh * jnp.mean(dxh * xh, axis=-1, keepdims=True))
        if n_tiles == n_lat:
            dx_ref[...] = dx
        else:
            @pl.when(t < n_lat)
            def _():
                dx_ref[...] = dx

    row = lambda cols: pl.BlockSpec((tm, cols), lambda t: (t, 0))
    lat = pl.BlockSpec((tm, D), lambda t: (jnp.minimum(t, n_lat - 1), 0))
    out_shape = [_sds((n_lat * tm, D), F32), _sds((r, FF), BF16), _sds((r, FF), BF16), _sds((r, FF), BF16),
                 _sds((r, D), BF16), _sds((r, D), BF16), _sds((nrows, 3, D), F32), _sds((1, D), F32)]
    in_specs = [row(D)] + _token_specs(xs, tm, n_lat) + [row(FF), row(FF), row(D), _mod_spec(3, tpe, nrows),
                                                          _const((1, D)), ANY]
    out_specs = [lat, row(FF), row(FF), row(FF), row(D), row(D), _mod_spec(3, tpe, nrows), _const((1, D))]
    scratch = [pltpu.VMEM((FF, D), BF16)] * 3 + [pltpu.SemaphoreType.DMA((3 * NDEV,))]
    args = [dout, *xs, a, b, o, mod3, norm_w, wall]
    return _pcall(body, name=name, grid=(n_tiles,), out_shape=tuple(out_shape), in_specs=in_specs,
                  out_specs=tuple(out_specs), scratch=scratch, vmem_mb=60)(*args)


def _ffn_bwd_dw(h, d_o, da, db, g, *, tr, name):
    r = h.shape[0]
    fh = FF // 2
    fsh = FF // NDEV
    nk = r // tr

    def body(h_ref, do_ref, da_ref, db_ref, g_ref, out_ref, acc1, acc3, acc2):
        k = pl.program_id(1)

        @pl.when(k == 0)
        def _():
            acc1[...] = jnp.zeros_like(acc1)
            acc3[...] = jnp.zeros_like(acc3)
            acc2[...] = jnp.zeros_like(acc2)

        hv = h_ref[...]
        acc1[...] += _mm_tn(da_ref[...], hv)
        acc3[...] += _mm_tn(db_ref[...], hv)
        acc2[...] += _mm_tn(g_ref[...], do_ref[...])

        @pl.when(k == nk - 1)
        def _():
            for i, acc in enumerate((acc1, acc3, acc2)):
                out_ref[:, i * fsh:(i + 1) * fsh, :] = acc[...].reshape(NDEV // 2, fsh, D).astype(BF16)

    rowd = pl.BlockSpec((tr, D), lambda f, k: (k, 0))
    rowf = pl.BlockSpec((tr, fh), lambda f, k: (k, f))
    return _pcall(
        body, name=name, grid=(2, nk), out_shape=_sds((NDEV, 3 * fsh, D), BF16),
        in_specs=[rowd, rowd, rowf, rowf, rowf],
        out_specs=pl.BlockSpec((NDEV // 2, 3 * fsh, D), lambda f, k: (f, 0, 0)),
        scratch=[pltpu.VMEM((fh, D), F32)] * 3, vmem_mb=56)(h, d_o, da, db, g)


def _direct_sends(x_ref, out_ref, send_sems, recv_sems):
    mx, my, mc = lax.axis_index("x"), lax.axis_index("y"), lax.axis_index("c")
    sends = []
    for k in range(1, NDEV):
        px = 1 - mx if (k & 4) else mx
        py = 1 - my if (k & 2) else my
        pc = 1 - mc if (k & 1) else mc
        sends.append(pltpu.make_async_remote_copy(
            src_ref=x_ref.at[4 * px + 2 * py + pc], dst_ref=out_ref.at[k - 1],
            send_sem=send_sems.at[k - 1], recv_sem=recv_sems.at[k - 1], device_id=(px, py, pc), device_id_type=MESH))
    return sends


def _sum_direct(x, recv, tr, name):
    _, r, c = x.shape

    def body(x_ref, r_ref, o_ref):
        me = 4 * lax.axis_index("x") + 2 * lax.axis_index("y") + lax.axis_index("c")
        acc = x_ref[0].astype(F32)
        for j in range(1, NDEV):
            acc = jnp.where(me == j, x_ref[j].astype(F32), acc)
        for j in range(NDEV - 1):
            acc = acc + r_ref[j].astype(F32)
        o_ref[...] = acc

    return _pcall(body, name=name, grid=(r // tr,), out_shape=_sds((r, c), F32),
                  in_specs=[pl.BlockSpec((NDEV, tr, c), lambda t: (0, t, 0)),
                            pl.BlockSpec((NDEV - 1, tr, c), lambda t: (0, t, 0))],
                  out_specs=pl.BlockSpec((tr, c), lambda t: (t, 0)))(x, recv)


def _exchange_behind(x_ref, recv_ref, send_sems, recv_sems, first, last):
    sends = _direct_sends(x_ref, recv_ref, send_sems, recv_sems)

    @pl.when(first)
    def _():
        for cp in sends:
            cp.start()

    @pl.when(last)
    def _():
        for cp in sends:
            cp.wait_recv()
        for cp in sends:
            cp.wait_send()


def _ffn_bwd_dw_one(lhs, rhs, *, tr, name, part=None):
    r = lhs.shape[0]
    fsh = FF // NDEV
    nk = r // tr
    fused = part is not None
    nslot = NDEV - 1

    def body(*refs):
        if fused:
            lhs_ref, rhs_ref, part_ref, out_ref, recv_ref, acc, send_sems, recv_sems = refs
        else:
            lhs_ref, rhs_ref, out_ref, acc = refs
        k = pl.program_id(0)
        if fused:
            _exchange_behind(part_ref, recv_ref, send_sems, recv_sems, k == 0, k == nk - 1)

        @pl.when(k == 0)
        def _():
            acc[...] = jnp.zeros_like(acc)

        acc[...] += _mm_tn(lhs_ref[...], rhs_ref[...])

        @pl.when(k == nk - 1)
        def _():
            out_ref[...] = acc[...].reshape(NDEV, fsh, D).astype(BF16)

    in_specs = [pl.BlockSpec((tr, FF), lambda k: (k, 0)), pl.BlockSpec((tr, D), lambda k: (k, 0))]
    out_shape = [_sds((NDEV, fsh, D), BF16)]
    out_specs = [_const((NDEV, fsh, D))]
    scratch = [pltpu.VMEM((FF, D), F32)]
    args = [lhs, rhs]
    if fused:
        in_specs.append(ANY)
        args.append(part)
        out_shape.append(_sds((nslot,) + part.shape[1:], part.dtype))
        out_specs.append(ANY)
        scratch += [pltpu.SemaphoreType.DMA((nslot,))] * 2
    res = _pcall(body, name=name, grid=(nk,), out_shape=tuple(out_shape), in_specs=in_specs,
                 out_specs=tuple(out_specs), scratch=scratch, vmem_mb=48)(*args)
    return res if fused else res[0]


_PIECES =((0, 128), (128, 384), (384, 896), (896, 1408), (1408, 1536))


def _proj_fwd(x1, mod2, norm_w, wint, *, tm, n_tiles, tpe, name="proj_fwd"):
    nrows = mod2.shape[0]
    r = n_tiles * tm

    def body(x_ref, mod_ref, nw_ref, w_ref, ckv_ref, q_ref, u_ref, v_ref, kpe_ref):
        x = x_ref[...]
        n = x * _rms(x) * nw_ref[...]
        h = (n * (1.0 + mod_ref[0, 1:2, :]) + mod_ref[0, 0:1, :]).astype(BF16)
        for (lo, hi), ref in zip(_PIECES, (ckv_ref, q_ref, u_ref, v_ref, kpe_ref)):
            ref[...] = _mm_nt(h, w_ref[lo:hi, :])

    row = lambda cols: pl.BlockSpec((tm, cols), lambda t: (t, 0))
    widths = [hi - lo for lo, hi in _PIECES]
    return _pcall(
        body, name=name, grid=(n_tiles,),
        out_shape=tuple(_sds((r, w), F32) for w in widths),
        in_specs=[row(D), _mod_spec(2, tpe, nrows), _const((1, D)), _const((WIN_ROWS, D))],
        out_specs=tuple(row(w) for w in widths), vmem_mb=40)(x1, mod2, norm_w, wint)


def _proj_bwd(dckv, dkpe, dq, du, dv, dx2, x1, mod2, norm_w, wint, *, tm, n_tiles, tpe, n_lat, name="proj_bwd"):
    nrows = mod2.shape[0]
    r = n_tiles * tm

    def body(dckv_ref, dkpe_ref, dq_ref, du_ref, dv_ref, dx2_ref, x_ref, mod_ref, nw_ref, w_ref,
             dx_ref, dw_ref, dmod_ref, dnw_ref):
        t = pl.program_id(0)
        is_lat = t < n_lat

        @pl.when(t == 0)
        def _():
            dw_ref[...] = jnp.zeros_like(dw_ref)
            dnw_ref[...] = jnp.zeros_like(dnw_ref)

        @pl.when(jnp.where(is_lat, t % tpe == 0, t == n_lat))
        def _():
            dmod_ref[...] = jnp.zeros_like(dmod_ref)

        x = x_ref[...]
        rr = _rms(x)
        xh = x * rr
        nw = nw_ref[...]
        n = xh * nw
        scale = mod_ref[0, 1:2, :]
        h = (n * (1.0 + scale) + mod_ref[0, 0:1, :]).astype(BF16)
        zero = jnp.zeros((), BF16)
        pieces = (dckv_ref[...], jnp.where(is_lat, dq_ref[...], zero), jnp.where(is_lat, du_ref[...], zero),
                  jnp.where(is_lat, dv_ref[...], zero), dkpe_ref[...])
        dh = None
        for (lo, hi), piece in zip(_PIECES, pieces):
            part = _mm(piece, w_ref[lo:hi, :])
            dh = part if dh is None else dh + part
        dn = dh * (1.0 + scale)
        dxh = dn * nw
        dx = rr * (dxh - xh * jnp.mean(dxh * xh, axis=-1, keepdims=True))
        dx_ref[...] = dx + jnp.where(is_lat, dx2_ref[...], 0.0)
        dmod_ref[0, 0:1, :] += _rowsum(dh)
        dmod_ref[0, 1:2, :] += _rowsum(dh * n)
        dnw_ref[...] += _rowsum(dn * xh)
        for (lo, hi), piece in zip(_PIECES, pieces):
            dw_ref[lo:hi, :] += _mm_tn(piece, h)

    row = lambda cols: pl.BlockSpec((tm, cols), lambda t: (t, 0))
    lat = lambda cols: pl.BlockSpec((tm, cols), lambda t: (jnp.minimum(t, n_lat - 1), 0))
    return _pcall(
        body, name=name, grid=(n_tiles,),
        out_shape=(_sds((r, D), F32), _sds((WIN_ROWS, D), F32), _sds((nrows, 2, D), F32), _sds((1, D), F32)),
        in_specs=[row(128), row(128), lat(256), lat(512), lat(512), lat(D), row(D), _mod_spec(2, tpe, nrows),
                  _const((1, D)), _const((WIN_ROWS, D))],
        out_specs=(row(D), _const((WIN_ROWS, D)), _mod_spec(2, tpe, nrows), _const((1, D))),
        vmem_mb=48)(dckv, dkpe, dq, du, dv, dx2, x1, mod2, norm_w, wint)


def _seg_sum(x, seg):
    return _mm(x.astype(BF16), seg)


def _seg_bcast(v, segt2):
    hi = v.astype(BF16)
    lo = (v - hi.astype(F32)).astype(BF16)
    return _mm(jnp.concatenate([hi, lo], axis=-1), segt2)


def _rope_pairs(t, cos, sin, rot2):
    cos2, sin2 = jnp.concatenate([cos, cos], axis=-1), jnp.concatenate([sin, sin], axis=-1)
    out = []
    for j in range(H // 2):
        tj = t[:, 2 * j * LANE:2 * (j + 1) * LANE]
        out.append(tj * cos2 + _dot_hl(tj, rot2) * sin2)
    return jnp.concatenate(out, axis=-1)


def _head_norm_rope(x, w_pad, cos, sin, seg, segt2, rot2, rope=True):
    rh = lax.rsqrt(_seg_sum(x * x, seg) * (1.0 / DH) + EPS)
    rb = _seg_bcast(rh, segt2)
    y = x * rb
    out = _rope_pairs(y * w_pad, cos, sin, rot2) if rope else None
    return out, y, rb


def _head_norm_rope_bwd(dout, y, rb, w_pad, cos, sin, seg, segt2, rot2_t):
    cos2, sin2 = jnp.concatenate([cos, cos], axis=-1), jnp.concatenate([sin, sin], axis=-1)
    dt = []
    for j in range(H // 2):
        dj = dout[:, 2 * j * LANE:2 * (j + 1) * LANE]
        dt.append(dj * cos2 + _dot_hl(dj * sin2, rot2_t))
    dt = jnp.concatenate(dt, axis=-1)
    dw = _rowsum(dt * y)
    dy = dt * w_pad
    mean_h = _seg_sum(dy * y, seg) * (1.0 / DH)
    return rb * (dy - y * _seg_bcast(mean_h, segt2)), dw


def _q_prep_fwd(qp, qa_w, wuq, wq, cos, sin, cs, *, tm, n_lat, tpe):
    def body(qp_ref, qa_ref, wuq_ref, wq_ref, cos_ref, sin_ref, seg, segt, rot, q_ref):
        x = qp_ref[...]
        cq = (x * _rms(x) * qa_ref[...]).astype(BF16)
        q, _, _ = _head_norm_rope(_mm_nt(cq, wuq_ref[...]), wq_ref[...], cos_ref[...], sin_ref[...],
                                  seg[...], segt[...], rot[...])
        q_ref[...] = q.astype(BF16)

    row = lambda cols: pl.BlockSpec((tm, cols), lambda t: (t, 0))
    tab = pl.BlockSpec((tm, LANE), lambda t: (t % tpe, 0))
    return _pcall(
        body, name="q_prep_fwd", grid=(n_lat,), out_shape=_sds((n_lat * tm, HP), BF16),
        in_specs=[row(QL), _const((1, QL)), _const((HP, QL)), _const((1, HP)), tab, tab,
                  _const((HP, LANE)), _const((2 * LANE, HP)), _const((2 * LANE, 2 * LANE))],
        out_specs=row(HP))(qp, qa_w, wuq, wq, cos, sin, cs["seg_h"], cs["seg_ht"], cs["rot"])


def _q_prep_bwd(dq, qp, qa_w, wuq, wq, cos, sin, cs, *, tm, n_lat, tpe):
    def body(dq_ref, qp_ref, qa_ref, wuq_ref, wq_ref, cos_ref, sin_ref, seg, segt, rot, rot_t,
             dqp_ref, dwuq_ref, dqa_ref, dwq_ref):
        @pl.when(pl.program_id(0) == 0)
        def _():
            dwuq_ref[...] = jnp.zeros_like(dwuq_ref)
            dqa_ref[...] = jnp.zeros_like(dqa_ref)
            dwq_ref[...] = jnp.zeros_like(dwq_ref)

        x = qp_ref[...]
        ra = _rms(x)
        xh = x * ra
        qa = qa_ref[...]
        cq = (xh * qa).astype(BF16)
        wuq_v = wuq_ref[...]
        wq_v, cos_v, sin_v = wq_ref[...], cos_ref[...], sin_ref[...]
        _, y, rb = _head_norm_rope(_mm_nt(cq, wuq_v), wq_v, cos_v, sin_v, seg[...], segt[...], rot[...], rope=False)
        dqraw, dwq = _head_norm_rope_bwd(dq_ref[...], y, rb, wq_v, cos_v, sin_v, seg[...], segt[...], rot_t[...])
        dqraw = dqraw.astype(BF16)
        dcq = _mm(dqraw, wuq_v)
        dxh = dcq * qa
        dqp_ref[...] = (ra * (dxh - xh * jnp.mean(dxh * xh, axis=-1, keepdims=True))).astype(BF16)
        dwuq_ref[...] += _mm_tn(dqraw, cq)
        dqa_ref[...] += _rowsum(dcq * xh)
        dwq_ref[...] += dwq

    row = lambda cols: pl.BlockSpec((tm, cols), lambda t: (t, 0))
    tab = pl.BlockSpec((tm, LANE), lambda t: (t % tpe, 0))
    return _pcall(
        body, name="q_prep_bwd", grid=(n_lat,),
        out_shape=(_sds((n_lat * tm, QL), BF16), _sds((HP, QL), F32), _sds((1, QL), F32), _sds((1, HP), F32)),
        in_specs=[row(HP), row(QL), _const((1, QL)), _const((HP, QL)), _const((1, HP)), tab, tab,
                  _const((HP, LANE)), _const((2 * LANE, HP)), _const((2 * LANE, 2 * LANE)), _const((2 * LANE, 2 * LANE))],
        out_specs=(row(QL), _const((HP, QL)), _const((1, QL)), _const((1, HP))), vmem_mb=40)(
            dq, qp, qa_w, wuq, wq, cos, sin, cs["seg_h"], cs["seg_ht"], cs["rot"], cs["rot_t"])


def _kv_tab_spec(tm, tpe, n_lat):
    return pl.BlockSpec((tm, LANE), lambda t: (jnp.where(t < n_lat, t % tpe, tpe), 0))


def _split_kv(kv, kpe):
    low = lax.broadcasted_iota(jnp.int32, (kv.shape[0], LANE), 1) < DN
    kx, v = [], []
    for h in range(H):
        blk = kv[:, h * LANE:(h + 1) * LANE]
        kx.append(jnp.where(low, blk, kpe))
        v.append(jnp.where(low, pltpu.roll(blk, DN, 1), 0.0))
    return jnp.concatenate(kx, axis=-1), jnp.concatenate(v, axis=-1)


def _kv_prep_fwd(ckv, kpe, kva_w, wukv, wk, cosk, sink, cs, *, tm, n_tiles, tpe, n_lat):
    def body(ckv_ref, kpe_ref, kva_ref, wukv_ref, wk_ref, cos_ref, sin_ref, seg, segt, rot, k_ref, v_ref):
        x = ckv_ref[...]
        ckvn = (x * _rms(x) * kva_ref[...]).astype(BF16)
        kx, v = _split_kv(_mm_nt(ckvn, wukv_ref[...]), kpe_ref[...])
        k, _, _ = _head_norm_rope(kx, wk_ref[...], cos_ref[...], sin_ref[...], seg[...], segt[...], rot[...])
        k_ref[...] = k.astype(BF16)
        v_ref[...] = v.astype(BF16)

    row = lambda cols: pl.BlockSpec((tm, cols), lambda t: (t, 0))
    tab = _kv_tab_spec(tm, tpe, n_lat)
    r = n_tiles * tm
    return _pcall(
        body, name="kv_prep_fwd", grid=(n_tiles,), out_shape=(_sds((r, HP), BF16), _sds((r, HP), BF16)),
        in_specs=[row(KVL), row(LANE), _const((1, KVL)), _const((HP, KVL)), _const((1, HP)), tab, tab,
                  _const((HP, LANE)), _const((2 * LANE, HP)), _const((2 * LANE, 2 * LANE))],
        out_specs=(row(HP), row(HP)), vmem_mb=40)(
            ckv, kpe, kva_w, wukv, wk, cosk, sink, cs["seg_h"], cs["seg_ht"], cs["rot"])


def _kv_prep_bwd(dks, dvs, ckv, kpe, kva_w, wukv, wk, cosk, sink, cs, *, tm, n_tiles, tpe, n_lat):
    def body(dkl_ref, dkc_ref, dvl_ref, dvc_ref, ckv_ref, kpe_ref, kva_ref, wukv_ref, wk_ref, cos_ref, sin_ref,
             seg, segt, rot, rot_t, dckv_ref, dkpe_ref, dwukv_ref, dkva_ref, dwk_ref):
        t = pl.program_id(0)
        is_lat = t < n_lat

        @pl.when(t == 0)
        def _():
            dwukv_ref[...] = jnp.zeros_like(dwukv_ref)
            dkva_ref[...] = jnp.zeros_like(dkva_ref)
            dwk_ref[...] = jnp.zeros_like(dwk_ref)

        dk = jnp.where(is_lat, dkl_ref[...], dkc_ref[...])
        dv = jnp.where(is_lat, dvl_ref[...], dvc_ref[...])
        x = ckv_ref[...]
        ra = _rms(x)
        xh = x * ra
        kva = kva_ref[...]
        ckvn = (xh * kva).astype(BF16)
        wukv_v = wukv_ref[...]
        wk_v, cos_v, sin_v = wk_ref[...], cos_ref[...], sin_ref[...]
        kx, _ = _split_kv(_mm_nt(ckvn, wukv_v), kpe_ref[...])
        _, y, rb = _head_norm_rope(kx, wk_v, cos_v, sin_v, seg[...], segt[...], rot[...], rope=False)
        dkx, dwk = _head_norm_rope_bwd(dk, y, rb, wk_v, cos_v, sin_v, seg[...], segt[...], rot_t[...])
        dkpe = dkx[:, 0:LANE]
        for h in range(1, H):
            dkpe = dkpe + dkx[:, h * LANE:(h + 1) * LANE]
        lane = lax.broadcasted_iota(jnp.int32, (tm, LANE), 1)
        dkpe_ref[...] = jnp.where((lane >= DN) & (lane < DH), dkpe, 0.0).astype(BF16)
        dkv = jnp.concatenate([jnp.where(lane < DN, dkx[:, h * LANE:(h + 1) * LANE],
                                         pltpu.roll(dv[:, h * LANE:(h + 1) * LANE], DN, 1)) for h in range(H)],
                              axis=-1).astype(BF16)
        dckvn = _mm(dkv, wukv_v)
        dxh = dckvn * kva
        dckv_ref[...] = (ra * (dxh - xh * jnp.mean(dxh * xh, axis=-1, keepdims=True))).astype(BF16)
        dwukv_ref[...] += _mm_tn(dkv, ckvn)
        dkva_ref[...] += _rowsum(dckvn * xh)
        dwk_ref[...] += dwk

    row = lambda cols: pl.BlockSpec((tm, cols), lambda t: (t, 0))
    lat = pl.BlockSpec((tm, HP), lambda t: (jnp.minimum(t, n_lat - 1), 0))
    ctx = pl.BlockSpec((tm, HP), lambda t: (jnp.maximum(t - n_lat, 0), 0))
    tab = _kv_tab_spec(tm, tpe, n_lat)
    r = n_tiles * tm
    return _pcall(
        body, name="kv_prep_bwd", grid=(n_tiles,),
        out_shape=(_sds((r, KVL), BF16), _sds((r, LANE), BF16), _sds((HP, KVL), F32), _sds((1, KVL), F32),
                   _sds((1, HP), F32)),
        in_specs=[lat, ctx, lat, ctx, row(KVL), row(LANE), _const((1, KVL)), _const((HP, KVL)), _const((1, HP)),
                  tab, tab, _const((HP, LANE)), _const((2 * LANE, HP)), _const((2 * LANE, 2 * LANE)),
                  _const((2 * LANE, 2 * LANE))],
        out_specs=(row(KVL), row(LANE), _const((HP, KVL)), _const((1, KVL)), _const((1, HP))), vmem_mb=48)(
            dks[0], dks[1], dvs[0], dvs[1], ckv, kpe, kva_w, wukv, wk, cosk, sink,
            cs["seg_h"], cs["seg_ht"], cs["rot"], cs["rot_t"])


_SCALE = DH ** -0.5
_SCALE_LOG2E = _SCALE * 1.4426950408889634


def _key_chunks(s, nc, ck):
    return ([(0, lo, min(lo + ck, s)) for lo in range(0, s, ck)]
            + [(1, lo, min(lo + ck, nc)) for lo in range(0, nc, ck)])


def _attn_fwd(q, k, v, *, nb, s, nc, tq, ck):
    tpe = s // tq
    r_lat = nb * s
    chunks = _key_chunks(s, nc, ck)
    hp = 4

    def body(q_ref, kl_ref, kc_ref, vl_ref, vc_ref, o_ref, lse_ref):
        k_refs, v_refs = (kl_ref, kc_ref), (vl_ref, vc_ref)
        for hh in range(hp):
            hs = slice(hh * LANE, (hh + 1) * LANE)
            qv = q_ref[:, hs]
            xs = [_mm_nt(qv, k_refs[w][lo:hi, hs]) for w, lo, hi in chunks]
            m = jnp.max(xs[0], axis=-1, keepdims=True)
            for x in xs[1:]:
                m = jnp.maximum(m, jnp.max(x, axis=-1, keepdims=True))
            l = acc = None
            for x, (w, lo, hi) in zip(xs, chunks):
                e = jnp.exp2((x - m) * _SCALE_LOG2E)
                lc = jnp.sum(e, axis=-1, keepdims=True)
                pv = _mm(e.astype(BF16), v_refs[w][lo:hi, hs])
                l = lc if l is None else l + lc
                acc = pv if acc is None else acc + pv
            o_ref[:, hs] = (acc / l).astype(BF16)
            lse = m * _SCALE_LOG2E + jnp.log2(l)
            lse_ref[hh] = jnp.transpose(jnp.broadcast_to(lse, (tq, LANE)))[0:8, :]

    qs = pl.BlockSpec((tq, hp * LANE), lambda i, j, t: (i * tpe + t, j))
    kl = pl.BlockSpec((s, hp * LANE), lambda i, j, t: (i, j))
    kc = pl.BlockSpec((nc, hp * LANE), lambda i, j, t: (r_lat // nc + i, j))
    ls = pl.BlockSpec((hp, 8, tq), lambda i, j, t: (i * (H // hp) + j, 0, t))
    return _pcall(body, name="attn_fwd", grid=(nb, H // hp, tpe),
                  out_shape=(_sds((r_lat, HP), BF16), _sds((nb * H, 8, s), F32)),
                  in_specs=[qs, kl, kc, kl, kc], out_specs=(qs, ls), vmem_mb=48)(q, k, k, v, v)


def _attn_bwd(q, k, v, o, do, lse, part, *, nb, s, nc, tq, ck):
    tpe = s // tq
    r_lat = nb * s
    chunks = _key_chunks(s, nc, ck)
    hp = 2
    n_steps = nb * (H // hp) * tpe

    def body(q_ref, kl_ref, kc_ref, vl_ref, vc_ref, o_ref, do_ref, lse_ref, part_ref,
             dq_ref, dkl_ref, dkc_ref, dvl_ref, dvc_ref, recv_ref, akl, akc, avl, avc, send_sems, recv_sems):
        t = pl.program_id(2)
        step = (pl.program_id(0) * (H // hp) + pl.program_id(1)) * tpe + t
        _exchange_behind(part_ref, recv_ref, send_sems, recv_sems, step == 0, step == n_steps - 1)

        @pl.when(t == 0)
        def _():
            akl[...] = jnp.zeros_like(akl)
            akc[...] = jnp.zeros_like(akc)
            avl[...] = jnp.zeros_like(avl)
            avc[...] = jnp.zeros_like(avc)

        k_refs, v_refs, ak, av = (kl_ref, kc_ref), (vl_ref, vc_ref), (akl, akc), (avl, avc)
        for hh in range(hp):
            hs = slice(hh * LANE, (hh + 1) * LANE)
            qv = q_ref[:, hs]
            lse = jnp.transpose(jnp.concatenate([lse_ref[hh]] * (LANE // 8), axis=0))[:, 0:1]
            dov = do_ref[:, hs]
            delta = jnp.sum(dov.astype(F32) * o_ref[:, hs].astype(F32), axis=-1, keepdims=True)
            dq = None
            for w, lo, hi in chunks:
                kc_v = k_refs[w][lo:hi, hs]
                p = jnp.exp2(_mm_nt(qv, kc_v) * _SCALE_LOG2E - lse)
                ds = (p * (_mm_nt(dov, v_refs[w][lo:hi, hs]) - delta)).astype(BF16)
                part = _mm(ds, kc_v)
                dq = part if dq is None else dq + part
                ak[w][hs, lo:hi] += _mm_tn(qv, ds)
                av[w][hs, lo:hi] += _mm_tn(dov, p.astype(BF16))
            dq_ref[:, hs] = dq * _SCALE

        @pl.when(t == tpe - 1)
        def _():
            dkl_ref[...] = akl[...].T * _SCALE
            dkc_ref[...] = akc[...].T * _SCALE
            dvl_ref[...] = avl[...].T
            dvc_ref[...] = avc[...].T

    qs = pl.BlockSpec((tq, hp * LANE), lambda i, j, t: (i * tpe + t, j))
    kl = pl.BlockSpec((s, hp * LANE), lambda i, j, t: (i, j))
    kc = pl.BlockSpec((nc, hp * LANE), lambda i, j, t: (r_lat // nc + i, j))
    kc_out = pl.BlockSpec((nc, hp * LANE), lambda i, j, t: (i, j))
    ls = pl.BlockSpec((hp, 8, tq), lambda i, j, t: (i * (H // hp) + j, 0, t))
    return _pcall(
        body, name="attn_bwd", grid=(nb, H // hp, tpe),
        out_shape=(_sds((r_lat, HP), F32), _sds((r_lat, HP), F32), _sds((nb * nc, HP), F32),
                   _sds((r_lat, HP), F32), _sds((nb * nc, HP), F32), _sds((NDEV - 1,) + part.shape[1:], part.dtype)),
        in_specs=[qs, kl, kc, kl, kc, qs, qs, ls, ANY], out_specs=(qs, kl, kc_out, kl, kc_out, ANY),
        scratch=[pltpu.VMEM((hp * LANE, s), F32), pltpu.VMEM((hp * LANE, nc), F32)] * 2
        + [pltpu.SemaphoreType.DMA((NDEV - 1,))] * 2,
        vmem_mb=60)(q, k, k, v, v, o, do, lse, part)


def _chunks_side_by_side(x, j, nch):
    return jnp.concatenate([x[c * CH:(c + 1) * CH, j * LANE:(j + 1) * LANE] for c in range(nch)], axis=-1)


def _first_group_lanes(nch):
    return (lax.broadcasted_iota(jnp.int32, (CH, nch * LANE), 1) & (LANE - 1)) < GD


def _gating(vn, ws_ref, bias_ref, s_scr, tm):
    nch = tm // CH
    first = _first_group_lanes(nch)
    for j in range(G // 2):
        ls = slice(j * LANE, (j + 1) * LANE)
        vst = _chunks_side_by_side(vn, j, nch)
        st = jnp.where(first, _mm(ws_ref[2 * j], vst), _mm(ws_ref[2 * j + 1], vst))
        for c in range(nch):
            s_scr[c * CH:(c + 1) * CH, ls] = st[:, c * LANE:(c + 1) * LANE] + bias_ref[:, ls]


def _compact_heads(x):
    low = lax.broadcasted_iota(jnp.int32, (x.shape[0], LANE), 1) < DV
    out = []
    for j in range(H // 2):
        even = x[:, 2 * j * LANE:(2 * j + 1) * LANE].astype(F32)
        odd = x[:, (2 * j + 1) * LANE:(2 * j + 2) * LANE].astype(F32)
        out.append(jnp.where(low, even, pltpu.roll(odd, DV, 1)))
    return jnp.concatenate(out, axis=-1)


def _expand_heads(x):
    low = lax.broadcasted_iota(jnp.int32, (x.shape[0], LANE), 1) < DV
    out = []
    for j in range(H // 2):
        blk = x[:, j * LANE:(j + 1) * LANE]
        out.append(jnp.where(low, blk, 0.0))
        out.append(jnp.where(low, pltpu.roll(blk, DV, 1), 0.0))
    return jnp.concatenate(out, axis=-1)


def _mix_fwd(u, v, attn, x1, gate, wv, ws, bias, wout, cs, *, tm, n_lat, tpe):
    nrows = gate.shape[0]

    def body(u_ref, v_ref, attn_ref, x_ref, gate_ref, wv_ref, ws_ref, bias_ref, wout_ref, seg, segt,
             x2_ref, mix_ref, s_scr):
        vg = _gelu(v_ref[...])
        rg = lax.rsqrt(_seg_sum(vg * vg, seg[...]) * (1.0 / GD) + EPS)
        vn = (vg * _seg_bcast(rg, segt[...]) * wv_ref[...]).astype(BF16)
        _gating(vn, ws_ref, bias_ref, s_scr, tm)
        sg = (_gelu(u_ref[...]) * s_scr[...]).astype(BF16)
        attn_c = _compact_heads(attn_ref[...]).astype(BF16)
        mix = _mm(attn_c, wout_ref[0:H * DV, :]) + _mm(sg, wout_ref[H * DV:, :])
        mix_ref[...] = mix.astype(BF16)
        x2_ref[...] = x_ref[...] + gate_ref[0] * mix

    row = lambda cols: pl.BlockSpec((tm, cols), lambda t: (t, 0))
    r = n_lat * tm
    return _pcall(
        body, name="mix_fwd", grid=(n_lat,),
        out_shape=(_sds((r, D), F32), _sds((r, D), BF16)),
        in_specs=[row(G * GD), row(G * GD), row(HP), row(D), _mod_spec(1, tpe, nrows), _const((1, G * GD)),
                  _const((G, CH, CH)), _const((CH, G * GD)), _const((D, D)), _const((G * GD, LANE)),
                  _const((2 * LANE, G * GD))],
        out_specs=(row(D), row(D)), scratch=[pltpu.VMEM((tm, G * GD), F32)], vmem_mb=40)(
            u, v, attn, x1, gate, wv, ws, bias, wout, cs["seg_g"], cs["seg_gt"])


def _mix_bwd(dx2, mix, u, v, attn, gate, wv, ws, wst, bias, wout, cs, *, tm, n_lat, tpe):
    nrows = gate.shape[0]
    wrows = H * DV + G * GD

    def body(dx2_ref, mix_ref, u_ref, v_ref, attn_ref, gate_ref, wv_ref, ws_ref, wst_ref, bias_ref, wout_ref, seg, segt,
             dattn_ref, du_ref, dv_ref, dgate_ref, dwout_ref, dws_ref, dbs_ref, dwv_ref, s_scr, dvn_scr, dbias_scr):
        t = pl.program_id(0)

        @pl.when(t == 0)
        def _():
            dwout_ref[...] = jnp.zeros_like(dwout_ref)
            dws_ref[...] = jnp.zeros_like(dws_ref)
            dwv_ref[...] = jnp.zeros_like(dwv_ref)
            dbias_scr[...] = jnp.zeros_like(dbias_scr)

        @pl.when(t % tpe == 0)
        def _():
            dgate_ref[...] = jnp.zeros_like(dgate_ref)

        dx2 = dx2_ref[...]
        dmix = (dx2 * gate_ref[0]).astype(BF16)
        dcat = _mm_nt(dmix, wout_ref[...])
        dattn_ref[...] = _expand_heads(dcat[:, :H * DV]).astype(BF16)
        dsg = dcat[:, H * DV:]

        vraw = v_ref[...]
        vg = _gelu(vraw)
        rg = lax.rsqrt(_seg_sum(vg * vg, seg[...]) * (1.0 / GD) + EPS)
        r64 = _seg_bcast(rg, segt[...])
        y = vg * r64
        wv_v = wv_ref[...]
        vn = (y * wv_v).astype(BF16)
        _gating(vn, ws_ref, bias_ref, s_scr, tm)
        uraw = u_ref[...]
        ug = _gelu(uraw)
        s = s_scr[...]
        sg = (ug * s).astype(BF16)
        du_ref[...] = (dsg * s * _gelu_grad(uraw)).astype(BF16)
        ds = dsg * ug
        dgate_ref[0] += _rowsum(dx2 * mix_ref[...].astype(F32))
        attn_c = _compact_heads(attn_ref[...]).astype(BF16)
        dwout_ref[...] += _mm_tn(jnp.concatenate([attn_c, sg], axis=-1), dmix)

        nch = tm // CH
        first = _first_group_lanes(nch)
        for c in range(nch):
            dbias_scr[...] += ds[c * CH:(c + 1) * CH, :]
        for j in range(G // 2):
            ls = slice(j * LANE, (j + 1) * LANE)
            dst32 = _chunks_side_by_side(ds, j, nch)
            dst = dst32.astype(BF16)
            vst = _chunks_side_by_side(vn, j, nch)
            dvn_st = jnp.where(first, _mm(wst_ref[2 * j], dst), _mm(wst_ref[2 * j + 1], dst))
            for c in range(nch):
                dvn_scr[c * CH:(c + 1) * CH, ls] = dvn_st[:, c * LANE:(c + 1) * LANE]
            dws_ref[2 * j] += _mm_nt(jnp.where(first, dst32, 0.0).astype(BF16), vst)
            dws_ref[2 * j + 1] += _mm_nt(jnp.where(first, 0.0, dst32).astype(BF16), vst)

        dvn = dvn_scr[...]
        dwv_ref[...] += _rowsum(dvn * y)
        dy = dvn * wv_v
        mean_g = _seg_sum(dy * y, seg[...]) * (1.0 / GD)
        dvg = r64 * (dy - y * _seg_bcast(mean_g, segt[...]))
        dv_ref[...] = (dvg * _gelu_grad(vraw)).astype(BF16)

        @pl.when(t == n_lat - 1)
        def _():
            dbs_ref[...] = _dot_hl(dbias_scr[...], seg[...])

    row = lambda cols: pl.BlockSpec((tm, cols), lambda t: (t, 0))
    r = n_lat * tm
    return _pcall(
        body, name="mix_bwd", grid=(n_lat,),
        out_shape=(_sds((r, HP), BF16), _sds((r, G * GD), BF16), _sds((r, G * GD), BF16), _sds((nrows, 1, D), F32),
                   _sds((wrows, D), F32), _sds((G, CH, CH), F32), _sds((CH, LANE), F32), _sds((1, G * GD), F32)),
        in_specs=[row(D), row(D), row(G * GD), row(G * GD), row(HP), _mod_spec(1, tpe, nrows), _const((1, G * GD)),
                  _const((G, CH, CH)), _const((G, CH, CH)), _const((CH, G * GD)), _const((wrows, D)),
                  _const((G * GD, LANE)), _const((2 * LANE, G * GD))],
        out_specs=(row(HP), row(G * GD), row(G * GD), _mod_spec(1, tpe, nrows), _const((wrows, D)),
                   _const((G, CH, CH)), _const((CH, LANE)), _const((1, G * GD))),
        scratch=[pltpu.VMEM((tm, G * GD), F32), pltpu.VMEM((tm, G * GD), F32), pltpu.VMEM((CH, G * GD), F32)],
        vmem_mb=56)(dx2, mix, u, v, attn, gate, wv, ws, wst, bias, wout, cs["seg_g"], cs["seg_gt"])


def _adamw_math(w, g, m, v):
    m2 = ADAM_B1 * m + (1.0 - ADAM_B1) * g
    v2 = ADAM_B2 * v + (1.0 - ADAM_B2) * (g * g)
    m_hat = m2 / (1.0 - ADAM_B1 ** ADAM_STEP)
    v_hat = v2 / (1.0 - ADAM_B2 ** ADAM_STEP)
    delta = -ADAM_LR * (m_hat / (jnp.sqrt(v_hat) + ADAM_EPS) + ADAM_WD * w)
    return delta, m2, v2


def _row_tile(r, c):
    best = r
    for tr in range(8, r, 8):
        if r % tr == 0 and tr * c * 4 <= MIB:
            best = tr
    return best


def _adamw(w, g, m, v, name):
    r, c = w.shape
    tr = _row_tile(r, c)

    def body(w_ref, g_ref, m_ref, v_ref, d_ref, mo_ref, vo_ref):
        d_ref[...], mo_ref[...], vo_ref[...] = _adamw_math(w_ref[...], g_ref[...], m_ref[...], v_ref[...])

    blk = pl.BlockSpec((tr, c), lambda t: (t, 0))
    return _pcall(body, name=name, grid=(r // tr,), out_shape=(_sds((r, c), F32),) * 3,
                  in_specs=[blk] * 4, out_specs=(blk,) * 3)(w, g, m, v)


def _adamw_exchanged(w, m, v, own, recv, row0, name):
    r, c = w.shape
    nown, nrecv = own.shape[0], recv.shape[0]
    tr = _row_tile(r, c)
    assert row0 % tr == 0 and (nown, nrecv) in ((NDEV, NDEV - 1), (4, 3))

    def body(w_ref, m_ref, v_ref, own_ref, recv_ref, g_ref, d_ref, mo_ref, vo_ref):
        if nown == NDEV:
            me = 4 * lax.axis_index("x") + 2 * lax.axis_index("y") + lax.axis_index("c")
        else:
            me = 2 * lax.axis_index("x") + lax.axis_index("y")
        g = own_ref[0].astype(F32)
        for j in range(1, nown):
            g = jnp.where(me == j, own_ref[j].astype(F32), g)
        for j in range(nrecv):
            g = g + recv_ref[j].astype(F32)
        g_ref[...] = g
        d_ref[...], mo_ref[...], vo_ref[...] = _adamw_math(w_ref[...], g, m_ref[...], v_ref[...])

    blk = pl.BlockSpec((tr, c), lambda t: (t, 0))
    xblk = lambda n: pl.BlockSpec((n, tr, c), lambda t: (0, row0 // tr + t, 0))
    return _pcall(body, name=name, grid=(r // tr,), out_shape=(_sds((r, c), F32),) * 4,
                  in_specs=[blk, blk, blk, xblk(nown), xblk(nrecv)], out_specs=(blk,) * 4)(w, m, v, own, recv)


def _adamw_small(params):
    n = len(params)

    def body(*refs):
        ins, outs = refs[:4 * n], refs[4 * n:]
        for i in range(n):
            w, g, m, v = (ins[4 * i + k][...] for k in range(4))
            if i == 0:
                sig = _sigmoid(w)
                g = g * (sig * (1.0 + w * (1.0 - sig)))
            d, m2, v2 = _adamw_math(w, g, m, v)
            outs[4 * i][...] = g
            outs[4 * i + 1][...] = d
            outs[4 * i + 2][...] = m2
            outs[4 * i + 3][...] = v2

    flat = [a for p in params for a in p]
    out_shape = tuple(_sds(p[0].shape, F32) for p in params for _ in range(4))
    res = _pcall(body, name="adamw_small", out_shape=out_shape, in_specs=[VMEM] * (4 * n),
                 out_specs=(VMEM,) * (4 * n))(*flat)
    return [res[4 * i:4 * i + 4] for i in range(n)]


def _rope_tables(s):
    rows = jnp.repeat(jnp.arange(s // GRID_W, dtype=F32), GRID_W)
    cols = jnp.tile(jnp.arange(GRID_W, dtype=F32), s // GRID_W)
    half = DR // 2
    inv = ROPE_BASE ** (-jnp.arange(0, half, 2, dtype=F32) / half)
    ang_r = rows[:, None] * inv
    ang_c = cols[:, None] * inv
    ang = jnp.concatenate([ang_r, ang_r, ang_c, ang_c], axis=-1)
    return jnp.cos(ang), jnp.sin(ang)


def _head_pad(a, real):
    return jnp.pad(a, ((0, 0), (0, LANE - real), (0, 0))).reshape(HP, a.shape[2])


def kernel(x, c, ctx, c_ctx, w_ada, b_ada, norm1_w, ffn1_w1, ffn1_w3, ffn1_w2, norm2_w, w_in, q_a_norm_w, w_uq, kv_a_norm_w, w_ukv, q_norm_w, k_norm_w, v_norm_w, w_s, b_s, w_out, norm3_w, ffn2_w1, ffn2_w3, ffn2_w2, loss_target, m_c_ctx, m_w_ada, m_b_ada, m_norm1_w, m_ffn1_w1, m_ffn1_w3, m_ffn1_w2, m_norm2_w, m_w_in, m_q_a_norm_w, m_w_uq, m_kv_a_norm_w, m_w_ukv, m_q_norm_w, m_k_norm_w, m_v_norm_w, m_w_s, m_b_s, m_w_out, m_norm3_w, m_ffn2_w1, m_ffn2_w3, m_ffn2_w2, v_c_ctx, v_w_ada, v_b_ada, v_norm1_w, v_ffn1_w1, v_ffn1_w3, v_ffn1_w2, v_norm2_w, v_w_in, v_q_a_norm_w, v_w_uq, v_kv_a_norm_w, v_w_ukv, v_q_norm_w, v_k_norm_w, v_v_norm_w, v_w_s, v_b_s, v_w_out, v_norm3_w, v_ffn2_w1, v_ffn2_w3, v_ffn2_w2):
    nb, s, _ = x.shape
    nc = ctx.shape[1]
    tm = 256 if nc % 256 == 0 else 128
    tpe = s // tm
    n_lat = nb * tpe
    n_all = n_lat + nb * nc // tm
    tmf = 2 * tm if s % (2 * tm) == 0 and (nb * nc) % (2 * tm) == 0 else tm
    tp = tmf
    r_lat = nb * s
    tpe_p, n_lat_p, n_all_p = s // tp, r_lat // tp, (r_lat + nb * nc) // tp
    me = 4 * lax.axis_index("x") + 2 * lax.axis_index("y") + lax.axis_index("c")
    cs = _consts()
    ncol = w_ada.shape[2]
    fsh = ffn1_w1.shape[2]
    assert nb + 1 <= 8 and NDEV * fsh == FF and NDEV * ncol == NMOD * D and s % nc == 0 and nc % tm == 0

    def t16(a):
        return a.T.astype(BF16)

    wpack1 = jnp.concatenate([t16(ffn1_w1[0]), t16(ffn1_w3[0]), ffn1_w2[0].astype(BF16)], axis=0)
    a_loc = jnp.concatenate([c, c_ctx[None, :], jnp.zeros((7 - nb, D), F32)], axis=0)
    a_raw, _, mod_all, wall1 = _ada_front(a_loc, w_ada[0], lax.dynamic_slice_in_dim(b_ada, me * ncol, ncol, axis=1),
                                          wpack1)
    a_raw = a_raw.reshape(NDEV * 8, D)
    mod_mine = lax.dynamic_slice_in_dim(mod_all, 8 * me, 8, axis=1)
    modtab = mod_mine.transpose(1, 0, 2).reshape(8, NMOD, D)[:nb + 1]
    wpack2 = jnp.concatenate([
        t16(ffn2_w1[0]), t16(ffn2_w3[0]), ffn2_w2[0].astype(BF16),
        t16(w_in[0]), jnp.zeros((12, D), BF16),
        w_out[0].astype(BF16),
        t16(w_uq[0]).reshape(24, D), jnp.zeros((8, D), BF16),
        t16(w_ukv[0]).reshape(16, D)], axis=0)

    def head_w(wn):
        return jnp.tile(jnp.pad(wn, ((0, 0), (0, LANE - DH))), (1, H))

    wq, wk = head_w(q_norm_w), head_w(k_norm_w)
    wv = v_norm_w.reshape(1, G * GD)
    ws16 = w_s[0].astype(BF16)
    wst16 = w_s[0].transpose(0, 2, 1).astype(BF16)
    bias = jnp.repeat(b_s[0].T, GD, axis=1)
    cos, sin = _rope_tables(s)
    cos = jnp.pad(cos, ((0, 0), (DN, LANE - DH)), constant_values=1.0)
    sin = jnp.pad(sin, ((0, 0), (DN, LANE - DH)))
    cos_k = jnp.concatenate([cos, jnp.ones((tm, LANE), F32)], axis=0)
    sin_k = jnp.concatenate([sin, jnp.zeros((tm, LANE), F32)], axis=0)

    xs = (x.reshape(r_lat, D), ctx.reshape(nb * nc, D))
    x1, a1, b1, o1, wall2 = _ffn_fwd(xs, modtab[:, 0:3], norm1_w, wall1, 0, tm=tmf, n_tiles=(r_lat + nb * nc) // tmf,
                                     tpe=s // tmf, n_lat=r_lat // tmf, name="ffn1_fwd", gather=wpack2)

    o0 = 3 * fsh
    wint = wall2[:, o0:o0 + 180].reshape(IN_COLS, D)
    z = lambda n: jnp.zeros((n, D), BF16)
    wint = jnp.concatenate([wint[0:128], wint[160:416], wint[416:928], wint[928:1440],
                            z(DN), wint[128:160], z(LANE - DH)], axis=0)
    wout = wall2[:, o0 + 192:o0 + 320].reshape(D, D)
    wuq = _head_pad(wall2[:, o0 + 320:o0 + 344].reshape(H, DH, QL), DH)
    wukv = wall2[:, o0 + 352:o0 + 368].reshape(HP, KVL)

    ckv, qp, u_raw, v_raw, kpe = _proj_fwd(x1, modtab[:, 3:5], norm2_w, wint, tm=tp, n_tiles=n_all_p, tpe=tpe_p)
    q = _q_prep_fwd(qp, q_a_norm_w, wuq, wq, cos, sin, cs, tm=tm, n_lat=n_lat, tpe=tpe)
    k, v = _kv_prep_fwd(ckv, kpe, kv_a_norm_w, wukv, wk, cos_k, sin_k, cs,
                        tm=tm, n_tiles=n_all, tpe=tpe, n_lat=n_lat)
    attn, lse = _attn_fwd(q, k, v, nb=nb, s=s, nc=nc, tq=tm, ck=2048)
    x2, mix = _mix_fwd(u_raw, v_raw, attn, x1, modtab[:nb, 5:6], wv, ws16, bias, wout, cs,
                       tm=tp, n_lat=n_lat_p, tpe=tpe_p)
    dy, a2, b2, o2, lsum = _ffn_fwd((x2,), modtab[:nb, 6:9], norm3_w, wall2, 0, tm=tmf, n_tiles=r_lat // tmf,
                                    tpe=s // tmf, n_lat=r_lat // tmf, name="ffn2_fwd",
                                    target=loss_target.reshape(r_lat, D))

    tr = 2 * tm if n_lat % 2 == 0 and n_all % 2 == 0 else tm
    dx2, da2, db2, g2, do2, h2, dmod678, dnorm3 = _ffn_bwd_dx(
        dy, (x2,), a2, b2, o2, modtab[:nb, 6:9], norm3_w, wall2, 0,
        tm=tm, n_tiles=n_lat, tpe=tpe, n_lat=n_lat, name="ffn2_bwd_dx")
    g_ffn2 = _ffn_bwd_dw(h2, do2, da2, db2, g2, tr=tr, name="ffn2_bwd_dw")

    dattn, du, dv, dgate5, dwout, dws, dbs, dwv = _mix_bwd(
        dx2, mix, u_raw, v_raw, attn, modtab[:nb, 5:6], wv, ws16, wst16, bias, wout, cs, tm=tp, n_lat=n_lat_p, tpe=tpe_p)
    tq = 2 * tm if s % (2 * tm) == 0 else tm
    dq, dk_l, dk_c, dv_l, dv_c, recv_ffn2 = _attn_bwd(q, k, v, attn, dattn, lse, g_ffn2,
                                                      nb=nb, s=s, nc=nc, tq=tq, ck=1024)
    dqp, dwuq, dqa, dwq = _q_prep_bwd(dq, qp, q_a_norm_w, wuq, wq, cos, sin, cs, tm=tm, n_lat=n_lat, tpe=tpe)
    dckv, dkpe, dwukv, dkva, dwk = _kv_prep_bwd((dk_l, dk_c), (dv_l, dv_c), ckv, kpe, kv_a_norm_w, wukv, wk,
                                                cos_k, sin_k, cs, tm=tm, n_tiles=n_all, tpe=tpe, n_lat=n_lat)
    dx1, dwin, dmod34, dnorm2 = _proj_bwd(dckv, dkpe, dqp, du, dv, dx2, x1, modtab[:, 3:5], norm2_w, wint,
                                          tm=tp, n_tiles=n_all_p, tpe=tpe_p, n_lat=n_lat_p)

    def blocks(a):
        return a.reshape(NDEV, a.shape[0] // NDEV, D)

    dwin_o = jnp.concatenate([dwin[0:128], dwin[KPE_LO:KPE_LO + DR], dwin[128:384], dwin[384:896], dwin[896:1408]],
                             axis=0)
    dwuq_o = dwuq.reshape(H, LANE, QL)[:, :DH]
    gmisc = jnp.concatenate([
        blocks(dwin_o).astype(BF16), jnp.zeros((NDEV, 12, D), BF16),
        blocks(dwout).astype(BF16),
        dwuq_o.reshape(NDEV, 24, D).astype(BF16), jnp.zeros((NDEV, 8, D), BF16),
        dwukv.reshape(NDEV, 16, D).astype(BF16)], axis=1)

    dx0, da1, db1, g1, do1, h1, dmod012, dnorm1 = _ffn_bwd_dx(
        dx1, xs, a1, b1, o1, modtab[:, 0:3], norm1_w, wall1, 0,
        tm=tm, n_tiles=n_all, tpe=tpe, n_lat=n_lat, name="ffn1_bwd_dx")
    grad_x = dx0.reshape(nb, s, D)
    g_w1, recv_misc = _ffn_bwd_dw_one(da1, h1, tr=tr, name="ffn1_bwd_dw1", part=gmisc)
    g_w3, recv_w1 = _ffn_bwd_dw_one(db1, h1, tr=tr, name="ffn1_bwd_dw3", part=g_w1)
    g_w2, recv_w3 = _ffn_bwd_dw_one(g1, do1, tr=tr, name="ffn1_bwd_dw2", part=g_w3)

    zrow = jnp.zeros((1, D), F32)
    g_lat = jnp.concatenate([dmod012[:nb, 0], dmod012[:nb, 1], dmod012[:nb, 2], dmod34[:nb, 0], dmod34[:nb, 1],
                             dgate5[:, 0], dmod678[:, 0], dmod678[:, 1], dmod678[:, 2]], axis=1)
    g_ctx = jnp.concatenate([dmod012[nb:, 0], dmod012[nb:, 1], dmod012[nb:, 2], dmod34[nb:, 0], dmod34[nb:, 1],
                             zrow, zrow, zrow, zrow], axis=1)
    g_loc = jnp.concatenate([g_lat, g_ctx, jnp.zeros((7 - nb, NMOD * D), F32)], axis=0)

    got_w2, g_all = _scatter_sibling([g_w2], "scatter_sibling_w2", gather=g_loc)
    g_all = g_all.reshape(NDEV * 8, NMOD * D)
    g_cols = lax.dynamic_slice_in_dim(g_all, me * ncol, ncol, axis=1)
    g_w_ada, pc_ctx, g_b_ada = _ada_bwd(a_raw, c_ctx.reshape(D, 1), g_all, g_cols, w_ada[0], nb)
    part_w2 = _add_sibling(g_w2, got_w2, 176, "add_sibling_w2")

    def prow(a):
        a = a.reshape(1, -1)
        return jnp.concatenate([a, jnp.zeros((1, D - a.shape[1]), F32)], axis=1)

    g_qn = dwq.reshape(H, LANE)[:, :DH].sum(0)
    g_kn = dwk.reshape(H, LANE)[:, :DH].sum(0)
    spack = jnp.concatenate([
        dnorm1, dnorm2, dnorm3, prow(dqa), prow(dkva), prow(g_qn), prow(g_kn), prow(dwv),
        prow(dbs[:, :G].T), prow(pc_ctx), prow(lsum[0:1]), jnp.zeros((5, D), F32), dws.reshape(CH, D)],
        axis=0)
    recv_w2, small_all = _scatter_chips([part_w2], "scatter_chips", gather=spack)
    ssum = _sum_slots(small_all, 144, "sum_small")
    loss = ssum[10, 0] * (0.5 / D)
    msum = _sum_direct(gmisc, recv_misc, 368, "sum_grads_misc")

    transposed = ("ffn1_w1", "ffn1_w3", "ffn2_w1", "ffn2_w3", "w_in", "w_uq")
    g_big = {
        "ffn1_w1": (g_w1, recv_w1, 0), "ffn1_w3": (g_w3, recv_w3, 0), "ffn1_w2": (part_w2, recv_w2, 0),
        "ffn2_w1": (g_ffn2, recv_ffn2, 0), "ffn2_w3": (g_ffn2, recv_ffn2, fsh), "ffn2_w2": (g_ffn2, recv_ffn2, 2 * fsh),
        "w_in": msum[0:180], "w_out": msum[192:320],
        "w_uq": msum[320:344].reshape(DH, QL), "w_ukv": msum[352:368].reshape(DN + DV, KVL).T,
        "w_ada": g_w_ada,
    }

    big_in = {
        "w_ada": (w_ada, m_w_ada, v_w_ada), "ffn1_w1": (ffn1_w1, m_ffn1_w1, v_ffn1_w1),
        "ffn1_w3": (ffn1_w3, m_ffn1_w3, v_ffn1_w3), "ffn1_w2": (ffn1_w2, m_ffn1_w2, v_ffn1_w2),
        "w_in": (w_in, m_w_in, v_w_in), "w_uq": (w_uq, m_w_uq, v_w_uq), "w_ukv": (w_ukv, m_w_ukv, v_w_ukv),
        "w_out": (w_out, m_w_out, v_w_out), "ffn2_w1": (ffn2_w1, m_ffn2_w1, v_ffn2_w1),
        "ffn2_w3": (ffn2_w3, m_ffn2_w3, v_ffn2_w3), "ffn2_w2": (ffn2_w2, m_ffn2_w2, v_ffn2_w2),
    }
    res = {}
    for nm, (w, m, v_) in big_in.items():
        g = g_big[nm]
        wt, mt, vt = (w[0].T, m[0].T, v_[0].T) if nm in transposed else (w[0], m[0], v_[0])
        if isinstance(g, tuple):
            outs = _adamw_exchanged(wt, mt, vt, *g, "adamw_" + nm)
        else:
            outs = (g,) + tuple(_adamw(wt, g, mt, vt, "adamw_" + nm))
        res[nm] = tuple((a.T if nm in transposed else a)[None] for a in outs)

    small_in = [
        ("c_ctx", c_ctx, m_c_ctx, v_c_ctx, ssum[9:10], (1, D)),
        ("b_ada", b_ada, m_b_ada, v_b_ada, g_b_ada, (1, NMOD * D)),
        ("norm1_w", norm1_w, m_norm1_w, v_norm1_w, ssum[0:1], (1, D)),
        ("norm2_w", norm2_w, m_norm2_w, v_norm2_w, ssum[1:2], (1, D)),
        ("norm3_w", norm3_w, m_norm3_w, v_norm3_w, ssum[2:3], (1, D)),
        ("q_a_norm_w", q_a_norm_w, m_q_a_norm_w, v_q_a_norm_w, ssum[3:4, :QL], (1, QL)),
        ("kv_a_norm_w", kv_a_norm_w, m_kv_a_norm_w, v_kv_a_norm_w, ssum[4:5, :KVL], (1, KVL)),
        ("q_norm_w", q_norm_w, m_q_norm_w, v_q_norm_w, ssum[5:6, :DH], (1, DH)),
        ("k_norm_w", k_norm_w, m_k_norm_w, v_k_norm_w, ssum[6:7, :DH], (1, DH)),
        ("v_norm_w", v_norm_w, m_v_norm_w, v_v_norm_w, ssum[7:8, :G * GD], (G, GD)),
        ("b_s", b_s, m_b_s, v_b_s, ssum[8:9], (G, CH)),
        ("w_s", w_s, m_w_s, v_w_s, ssum[16:144], (G * CH, CH)),
    ]
    small_out = _adamw_small(
        [(w.reshape(sh), g.reshape(sh), m.reshape(sh), v_.reshape(sh)) for _, w, m, v_, g, sh in small_in])
    for (nm, w, *_), outs in zip(small_in, small_out):
        res[nm] = tuple(a.reshape(w.shape) for a in outs)

    order = ["c_ctx", "w_ada", "b_ada", "norm1_w", "ffn1_w1", "ffn1_w3", "ffn1_w2", "norm2_w", "w_in", "q_a_norm_w",
             "w_uq", "kv_a_norm_w", "w_ukv", "q_norm_w", "k_norm_w", "v_norm_w", "w_s", "b_s", "w_out", "norm3_w",
             "ffn2_w1", "ffn2_w3", "ffn2_w2"]
    return (loss, grad_x, *[res[n][0] for n in order], *[res[n][1] for n in order],
            *[res[n][2] for n in order], *[res[n][3] for n in order])
```

```python
import numpy as np
import jax
import jax.numpy as jnp
from jax import lax
from jax.experimental import pallas as pl
from jax.experimental.pallas import tpu as pltpu

F32 = jnp.float32
BF16 = jnp.bfloat16

D = 1024
FF = 2816
FC = 256
H = 8
DN, DR, DV = 64, 32, 64
DH = DN + DR
QL, KVL = 256, 128
G, GD, CH = 8, 64, 128
NMOD = 9
EPS = 1e-6
GRID_W = 64
ROPE_BASE = 10000.0
NDEV = 8
LANE = 128
HP = H * LANE
IN_COLS = 1440
WIN_ROWS = 1536
KPE_LO = 1408 + DN
MIB = 1 << 20

ADAM_LR, ADAM_B1, ADAM_B2, ADAM_EPS, ADAM_WD, ADAM_STEP = 0.001, 0.9, 0.999, 1e-08, 0.01, 10

MESH = pl.DeviceIdType.MESH
ANY = pl.BlockSpec(memory_space=pl.ANY)
VMEM = pl.BlockSpec(memory_space=pltpu.VMEM)


def _mm(a, b):
    return jnp.dot(a, b, preferred_element_type=F32)


def _mm_nt(a, b):
    return lax.dot_general(a, b, (((1,), (1,)), ((), ())), preferred_element_type=F32)


def _mm_tn(a, b):
    return lax.dot_general(a, b, (((0,), (0,)), ((), ())), preferred_element_type=F32)


def _dot_hl(x, m):
    hi = x.astype(BF16)
    lo = (x - hi.astype(F32)).astype(BF16)
    return _mm(hi, m) + _mm(lo, m)


def _sigmoid(a):
    return 1.0 / (1.0 + jnp.exp(-a))


_G0 = 0.7978845608028654
_G1 = 0.044715


def _gelu(x):
    return 0.5 * x * (1.0 + jnp.tanh(_G0 * (x + _G1 * (x * x * x))))


def _gelu_grad(x):
    th = jnp.tanh(_G0 * (x + _G1 * (x * x * x)))
    return 0.5 * (1.0 + th) + 0.5 * x * (1.0 - th * th) * (_G0 * (1.0 + 3.0 * _G1 * x * x))


def _rowsum(y):
    return jnp.sum(y, axis=0, keepdims=True)


def _rms(x):
    return lax.rsqrt(jnp.mean(x * x, axis=-1, keepdims=True) + EPS)


def _pcall(body, *, name, out_shape, in_specs, out_specs, grid=None, scratch=(), vmem_mb=32, aliases=None):
    kw = {}
    if grid is not None:
        kw["grid"] = grid
        sem = ("arbitrary",) * len(grid)
    else:
        sem = None
    if aliases:
        kw["input_output_aliases"] = aliases
    return pl.pallas_call(
        body, name=name, out_shape=out_shape, in_specs=in_specs, out_specs=out_specs,
        scratch_shapes=list(scratch),
        compiler_params=pltpu.CompilerParams(dimension_semantics=sem, vmem_limit_bytes=vmem_mb * MIB),
        **kw)


def _const(shape):
    nd = len(shape)
    return pl.BlockSpec(shape, lambda *_: (0,) * nd)


def _sds(shape, dt):
    return jax.ShapeDtypeStruct(shape, dt)


def _consts():
    seg_h = np.zeros((HP, LANE), np.float32)
    seg_h[np.arange(HP), np.arange(HP) // LANE] = 1.0
    seg_g = np.zeros((G * GD, LANE), np.float32)
    seg_g[np.arange(G * GD), np.arange(G * GD) // GD] = 1.0
    rot = np.zeros((LANE, LANE), np.float32)
    for base in (DN, DN + 16):
        for j in range(8):
            rot[base + j + 8, base + j] = -1.0
            rot[base + j, base + j + 8] = 1.0
    rot2 = np.zeros((2 * LANE, 2 * LANE), np.float32)
    rot2[:LANE, :LANE] = rot
    rot2[LANE:, LANE:] = rot
    twice = lambda m: np.concatenate([m, m], axis=0)
    c = dict(seg_h=seg_h, seg_ht=twice(seg_h.T), seg_g=seg_g, seg_gt=twice(seg_g.T), rot=rot2, rot_t=rot2.T)
    return {k: jnp.asarray(v, BF16) for k, v in c.items()}


_GATHER_SEMS = [pltpu.SemaphoreType.DMA((7,)), pltpu.SemaphoreType.DMA((7,)), pltpu.SemaphoreType.DMA(())]


def _gather_phases(x_ref, out_ref, send_sems, recv_sems, local_sem):
    mx, my, mc = lax.axis_index("x"), lax.axis_index("y"), lax.axis_index("c")
    me, sibling = (mx, my, mc), (mx, my, 1 - mc)
    chips = [(1 - mx, my), (mx, 1 - my), (1 - mx, 1 - my)]

    def blk(px, py, pc):
        return out_ref.at[4 * px + 2 * py + pc]

    def copy(k, block, to, src=None):
        return pltpu.make_async_remote_copy(
            src_ref=blk(*block) if src is None else src, dst_ref=blk(*block),
            send_sem=send_sems.at[k], recv_sem=recv_sems.at[k], device_id=to, device_id_type=MESH)

    mine = pltpu.make_async_copy(x_ref, blk(*me), local_sem)
    first = [copy(0, me, sibling, src=x_ref)]
    first += [copy(1 + j, me, (*chip, mc), src=x_ref) for j, chip in enumerate(chips)]
    passed = [copy(4 + j, (*chip, mc), sibling) for j, chip in enumerate(chips)]

    def start():
        mine.start()
        for cp in first:
            cp.start()

    def forward():
        for j, chip in enumerate(chips):
            copy(1 + j, (*chip, mc), me).wait_recv()
            passed[j].start()

    def finish():
        copy(0, sibling, me).wait_recv()
        for j, chip in enumerate(chips):
            copy(4 + j, (*chip, 1 - mc), me).wait_recv()
        for cp in first + passed:
            cp.wait_send()
        mine.wait()

    return start, forward, finish


def _chip_sends(p_ref, out_ref, send_sems, recv_sems):
    mx, my, mc = lax.axis_index("x"), lax.axis_index("y"), lax.axis_index("c")
    peers = [(1 - mx, my), (mx, 1 - my), (1 - mx, 1 - my)]
    return [pltpu.make_async_remote_copy(
        src_ref=p_ref.at[2 * px + py], dst_ref=out_ref.at[j], send_sem=send_sems.at[j], recv_sem=recv_sems.at[j],
        device_id=(px, py, mc), device_id_type=MESH) for j, (px, py) in enumerate(peers)]


def _with_gather(copies_of, n, shapes, sems, gather, name, args):
    ns = len(sems)

    def body(*refs):
        ng = 1 if gather is not None else 0
        ins, outs = refs[:n], refs[n + ng:2 * n + ng]
        copies = copies_of(ins, outs, refs[2 * n + 2 * ng:2 * n + 2 * ng + ns])
        if ng:
            start, forward, finish = _gather_phases(refs[n], refs[2 * n + 1], *refs[2 * n + 2 + ns:])
            start()
        for cp in copies:
            cp.start()
        if ng:
            forward()
        for cp in copies:
            cp.wait_recv()
        for cp in copies:
            cp.wait_send()
        if ng:
            finish()

    in_specs, out_shape, scratch = [ANY] * n, list(shapes), list(sems)
    if gather is not None:
        in_specs.append(ANY)
        args = list(args) + [gather]
        out_shape.append(_sds((NDEV,) + gather.shape, gather.dtype))
        scratch += _GATHER_SEMS
    return pl.pallas_call(body, name=name, out_shape=tuple(out_shape), in_specs=in_specs,
                          out_specs=(ANY,) * len(out_shape), scratch_shapes=scratch)(*args)


def _scatter_sibling(xs, name, gather=None):
    n = len(xs)

    def copies_of(x_refs, got_refs, sems):
        send_sems, recv_sems = sems
        mx, my, mc = lax.axis_index("x"), lax.axis_index("y"), lax.axis_index("c")
        return [pltpu.make_async_remote_copy(
            src_ref=x_refs[i].at[2 * j + 1 - mc], dst_ref=got_refs[i].at[j],
            send_sem=send_sems.at[4 * i + j], recv_sem=recv_sems.at[4 * i + j],
            device_id=(mx, my, 1 - mc), device_id_type=MESH) for i in range(n) for j in range(4)]

    shapes = tuple(_sds((4,) + x.shape[1:], x.dtype) for x in xs)
    return _with_gather(copies_of, n, shapes, [pltpu.SemaphoreType.DMA((4 * n,))] * 2, gather, name, xs)


def _scatter_chips(ps, name, gather=None):
    n = len(ps)

    def copies_of(p_refs, out_refs, sems):
        sends = []
        for i in range(n):
            sends += _chip_sends(p_refs[i], out_refs[i], sems[2 * i], sems[2 * i + 1])
        return sends

    shapes = tuple(_sds((3,) + p.shape[1:], p.dtype) for p in ps)
    return _with_gather(copies_of, n, shapes, [pltpu.SemaphoreType.DMA((3,))] * (2 * n), gather, name, ps)


def _add_sibling(x, got, tr, name):
    _, r, c = x.shape

    def body(x_ref, g_ref, o_ref):
        mc = lax.axis_index("c")
        for j in range(4):
            mine = jnp.where(mc == 0, x_ref[2 * j].astype(F32), x_ref[2 * j + 1].astype(F32))
            o_ref[j] = (mine + g_ref[j].astype(F32)).astype(o_ref.dtype)

    return _pcall(body, name=name, grid=(r // tr,), out_shape=_sds(got.shape, got.dtype),
                  in_specs=[pl.BlockSpec((NDEV, tr, c), lambda t: (0, t, 0)), pl.BlockSpec((4, tr, c), lambda t: (0, t, 0))],
                  out_specs=pl.BlockSpec((4, tr, c), lambda t: (0, t, 0)))(x, got)


def _sum_chips(part, recv, tr, name):
    _, r, c = part.shape

    def body(p_ref, r_ref, o_ref):
        slot = 2 * lax.axis_index("x") + lax.axis_index("y")
        acc = p_ref[0].astype(F32)
        for j in range(1, 4):
            acc = jnp.where(slot == j, p_ref[j].astype(F32), acc)
        for j in range(3):
            acc = acc + r_ref[j].astype(F32)
        o_ref[...] = acc

    return _pcall(body, name=name, grid=(r // tr,), out_shape=_sds((r, c), F32),
                  in_specs=[pl.BlockSpec((4, tr, c), lambda t: (0, t, 0)), pl.BlockSpec((3, tr, c), lambda t: (0, t, 0))],
                  out_specs=pl.BlockSpec((tr, c), lambda t: (t, 0)))(part, recv)


def _sum_slots(x, tr, name):
    n, r, c = x.shape

    def body(x_ref, o_ref):
        acc = x_ref[0].astype(F32)
        for s in range(1, n):
            acc = acc + x_ref[s].astype(F32)
        o_ref[...] = acc

    return _pcall(body, name=name, grid=(r // tr,), out_shape=_sds((r, c), F32),
                  in_specs=[pl.BlockSpec((n, tr, c), lambda t: (0, t, 0))],
                  out_specs=pl.BlockSpec((tr, c), lambda t: (t, 0)))(x)


def _ada_front(a_loc, w_loc, b_loc, wpack):
    ncol = w_loc.shape[1]
    nrow = NDEV * a_loc.shape[0]

    def body(a_ref, w_ref, b_ref, wp_ref, araw_ref, mloc_ref, mall_ref, wall_ref,
             a_vm, w_vm, m_vm, lsem, *sems):
        a_start, a_forward, a_finish = _gather_phases(a_ref, araw_ref, *sems[0:3])
        m_start, m_forward, m_finish = _gather_phases(mloc_ref, mall_ref, *sems[3:6])
        w_start, w_forward, w_finish = _gather_phases(wp_ref, wall_ref, *sems[6:9])
        w_in = pltpu.make_async_copy(w_ref, w_vm, lsem.at[0])
        w_in.start()
        a_start()
        w_start()
        a_forward()
        a_finish()
        a_in = pltpu.make_async_copy(araw_ref, a_vm, lsem.at[1])
        a_in.start()
        a_in.wait()
        w_in.wait()
        a = a_vm[...].reshape(nrow, D)
        act = (a * _sigmoid(a)).astype(BF16)
        m_vm[...] = _mm(act, w_vm[...].astype(BF16)) + b_ref[...]
        m_out = pltpu.make_async_copy(m_vm, mloc_ref, lsem.at[2])
        m_out.start()
        m_out.wait()
        m_start()
        m_forward()
        m_finish()
        w_forward()
        w_finish()

    return pl.pallas_call(
        body, name="ada_front",
        out_shape=(_sds((NDEV,) + a_loc.shape, F32), _sds((nrow, ncol), F32), _sds((NDEV, nrow, ncol), F32),
                   _sds((NDEV,) + wpack.shape, wpack.dtype)),
        in_specs=[ANY, ANY, VMEM, ANY], out_specs=(ANY, ANY, ANY, ANY),
        scratch_shapes=[pltpu.VMEM((NDEV,) + a_loc.shape, F32), pltpu.VMEM(w_loc.shape, F32),
                        pltpu.VMEM((nrow, ncol), F32), pltpu.SemaphoreType.DMA((3,))] + _GATHER_SEMS * 3,
        compiler_params=pltpu.CompilerParams(vmem_limit_bytes=32 * MIB),
    )(a_loc, w_loc, b_loc, wpack)


def _ada_bwd(a_raw, cctx_col, g_all, g_cols, w_loc, nb):
    nrow = a_raw.shape[0]
    ncol = w_loc.shape[1]

    def body(a_ref, cc_ref, gall_ref, g_ref, w_ref, dw_ref, pc_ref, gb_ref):
        a = a_ref[...]
        rowid = lax.broadcasted_iota(jnp.int32, (nrow, 1), 0) % 8
        act = jnp.where(rowid < nb, a * _sigmoid(a), 0.0).astype(BF16)
        g = g_ref[...]
        gc = _rowsum(jnp.where(rowid == nb, g, 0.0))
        cc = cc_ref[...]
        dw_ref[...] = _mm_tn(act, g.astype(BF16)) + (cc * _sigmoid(cc)) * gc
        pc_ref[...] = jnp.sum(w_ref[...] * gc, axis=1, keepdims=True)
        gb_ref[...] = _rowsum(gall_ref[...])

    return _pcall(body, name="ada_bwd",
                  out_shape=(_sds((D, ncol), F32), _sds((D, 1), F32), _sds((1, g_all.shape[1]), F32)),
                  in_specs=[VMEM] * 5, out_specs=(VMEM,) * 3, vmem_mb=48)(a_raw, cctx_col, g_all, g_cols, w_loc)


def _mod_spec(k, tpe, nrows):
    return pl.BlockSpec((1, k, D), lambda t: (jnp.minimum(t // tpe, nrows - 1), 0, 0))


def _load_ffn_weights(wall_ref, first, bufs, sems):
    fsh = FF // NDEV
    cps = []
    for j, buf in enumerate(bufs):
        for d in range(NDEV):
            cps.append(pltpu.make_async_copy(wall_ref.at[d, pl.ds((first + j) * fsh, fsh)],
                                             buf.at[pl.ds(d * fsh, fsh)], sems.at[j * NDEV + d]))
    for cp in cps:
        cp.start()
    for cp in cps:
        cp.wait()


def _token_specs(xs, tm, n_lat):
    specs = [pl.BlockSpec((tm, D), lambda t: (jnp.minimum(t, n_lat - 1), 0))]
    if len(xs) == 2:
        specs.append(pl.BlockSpec((tm, D), lambda t: (jnp.maximum(t - n_lat, 0), 0)))
    return specs


def _ffn_fwd(xs, mod3, norm_w, wall, first, *, tm, n_tiles, tpe, n_lat, name, target=None, gather=None):
    nrows = mod3.shape[0]
    r = n_tiles * tm
    nx = len(xs)
    with_loss = target is not None
    with_gather = gather is not None
    fwd_step = max(2 * n_tiles // 3, 1)

    def body(*refs):
        x_refs = refs[:nx]
        pos = nx
        if with_loss:
            tgt_ref = refs[pos]
            pos += 1
        mod_ref, nw_ref, wall_ref = refs[pos:pos + 3]
        pos += 3
        if with_gather:
            gin_ref = refs[pos]
            pos += 1
        xo_ref, a_ref, b_ref, o_ref = refs[pos:pos + 4]
        pos += 4
        if with_loss:
            ls_ref = refs[pos]
            pos += 1
        if with_gather:
            gout_ref = refs[pos]
            pos += 1
        w1_ref, w3_ref, w2_ref, wsem, acc_ref = refs[pos:pos + 5]
        t = pl.program_id(0)
        if with_gather:
            g_start, g_forward, g_finish = _gather_phases(gin_ref, gout_ref, *refs[pos + 5:])

        @pl.when(t == 0)
        def _():
            if with_gather:
                g_start()
            _load_ffn_weights(wall_ref, first, (w1_ref, w3_ref, w2_ref), wsem)
            if with_loss:
                ls_ref[...] = jnp.zeros_like(ls_ref)

        if with_gather:
            @pl.when(t == fwd_step)
            def _():
                g_forward()

            @pl.when(t == n_tiles - 1)
            def _():
                g_finish()

        x = x_refs[0][...]
        if nx == 2:
            x = jnp.where(t < n_lat, x, x_refs[1][...])
        n = x * _rms(x) * nw_ref[...]
        shift, scale, gate = mod_ref[0, 0:1, :], mod_ref[0, 1:2, :], mod_ref[0, 2:3, :]
        h = (n * (1.0 + scale) + shift).astype(BF16)
        nch = FF // FC
        o = None
        for lo_c, hi_c in ((0, nch // 2), (nch // 2, nch)):
            for j in range(lo_c, hi_c):
                sl = slice(j * FC, (j + 1) * FC)
                a = _mm_nt(h, w1_ref[sl, :])
                b = _mm_nt(h, w3_ref[sl, :])
                a_ref[:, sl] = a.astype(BF16)
                b_ref[:, sl] = b.astype(BF16)
                acc_ref[:, sl] = (a * _sigmoid(a) * b).astype(BF16)
            gs = slice(lo_c * FC, hi_c * FC)
            part = _mm(acc_ref[:, gs], w2_ref[gs, :])
            o = part if o is None else o + part
        o_ref[...] = o.astype(BF16)
        out = x + (0.5 * gate) * o
        if with_loss:
            d = out - tgt_ref[...]
            xo_ref[...] = d * (1.0 / D)
            ls_ref[...] += jnp.sum(d * d)
        else:
            xo_ref[...] = out

    row = lambda cols: pl.BlockSpec((tm, cols), lambda t: (t, 0))
    in_specs = _token_specs(xs, tm, n_lat) + ([row(D)] if with_loss else []) + [
        _mod_spec(3, tpe, nrows), _const((1, D)), ANY]
    out_shape = [_sds((r, D), F32), _sds((r, FF), BF16), _sds((r, FF), BF16), _sds((r, D), BF16)]
    out_specs = [row(D), row(FF), row(FF), row(D)]
    scratch = [pltpu.VMEM((FF, D), BF16)] * 3 + [pltpu.SemaphoreType.DMA((3 * NDEV,)), pltpu.VMEM((tm, FF), BF16)]
    if with_loss:
        out_shape.append(_sds((8, LANE), F32))
        out_specs.append(_const((8, LANE)))
    args = list(xs) + ([target] if with_loss else []) + [mod3, norm_w, wall]
    if with_gather:
        assert n_tiles >= 2
        in_specs.append(ANY)
        args.append(gather)
        out_shape.append(_sds((NDEV,) + gather.shape, gather.dtype))
        out_specs.append(ANY)
        scratch += _GATHER_SEMS
    return _pcall(
        body, name=name, grid=(n_tiles,), out_shape=tuple(out_shape), in_specs=in_specs, out_specs=tuple(out_specs),
        scratch=scratch, vmem_mb=56)(*args)


def _ffn_bwd_dx(dout, xs, a, b, o, mod3, norm_w, wall, first, *, tm, n_tiles, tpe, n_lat, name):
    nrows = mod3.shape[0]
    r = n_tiles * tm
    nx = len(xs)

    def body(*refs):
        dout_ref = refs[0]
        x_refs = refs[1:1 + nx]
        (a_ref, b_ref, o_ref, mod_ref, nw_ref, wall_ref,
         dx_ref, da_ref, db_ref, g_ref, do_ref, h_ref, dmod_ref, dnw_ref,
         w1_ref, w3_ref, w2_ref, wsem) = refs[1 + nx:]
        t = pl.program_id(0)

        @pl.when(t == 0)
        def _():
            _load_ffn_weights(wall_ref, first, (w1_ref, w3_ref, w2_ref), wsem)
            dnw_ref[...] = jnp.zeros_like(dnw_ref)

        @pl.when(jnp.where(t < n_lat, t % tpe == 0, t == n_lat))
        def _():
            dmod_ref[...] = jnp.zeros_like(dmod_ref)

        x = x_refs[0][...]
        if nx == 2:
            x = jnp.where(t < n_lat, x, x_refs[1][...])
        dout = dout_ref[...]
        shift, scale, gate = mod_ref[0, 0:1, :], mod_ref[0, 1:2, :], mod_ref[0, 2:3, :]
        d_o = ((0.5 * gate) * dout).astype(BF16)
        do_ref[...] = d_o
        nch = FF // FC
        groups = ((0, 4), (4, 8), (8, nch))
        dh = None
        for lo_c, hi_c in groups:
            for j in range(lo_c, hi_c):
                sl = slice(j * FC, (j + 1) * FC)
                av = a_ref[:, sl].astype(F32)
                bv = b_ref[:, sl].astype(F32)
                dg = _mm_nt(d_o, w2_ref[sl, :])
                sig = _sigmoid(av)
                sa = av * sig
                g_ref[:, sl] = (sa * bv).astype(BF16)
                da_ref[:, sl] = (dg * bv * (sig * (1.0 + av * (1.0 - sig)))).astype(BF16)
                db_ref[:, sl] = (dg * sa).astype(BF16)
            gs = slice(lo_c * FC, hi_c * FC)
            part = _mm(da_ref[:, gs], w1_ref[gs, :]) + _mm(db_ref[:, gs], w3_ref[gs, :])
            dh = part if dh is None else dh + part
        rr = _rms(x)
        xh = x * rr
        nw = nw_ref[...]
        n = xh * nw
        h_ref[...] = (n * (1.0 + scale) + shift).astype(BF16)
        dgate = _rowsum(0.5 * o_ref[...].astype(F32) * dout)
        dn = dh * (1.0 + scale)
        dxh = dn * nw
        dmod_ref[0, 0:1, :] += _rowsum(dh)
        dmod_ref[0, 1:2, :] += _rowsum(dh * n)
        dmod_ref[0, 2:3, :] += dgate
        dnw_ref[...] += _rowsum(dn * xh)
        dx = dout + rr * (dxh - xh * jnp.mean(dxh * xh, axis=-1, keepdims=True))
        if n_tiles == n_lat:
            dx_ref[...] = dx
        else:
            @pl.when(t < n_lat)
            def _():
                dx_ref[...] = dx

    row = lambda cols: pl.BlockSpec((tm, cols), lambda t: (t, 0))
    lat = pl.BlockSpec((tm, D), lambda t: (jnp.minimum(t, n_lat - 1), 0))
    out_shape = [_sds((n_lat * tm, D), F32), _sds((r, FF), BF16), _sds((r, FF), BF16), _sds((r, FF), BF16),
                 _sds((r, D), BF16), _sds((r, D), BF16), _sds((nrows, 3, D), F32), _sds((1, D), F32)]
    in_specs = [row(D)] + _token_specs(xs, tm, n_lat) + [row(FF), row(FF), row(D), _mod_spec(3, tpe, nrows),
                                                          _const((1, D)), ANY]
    out_specs = [lat, row(FF), row(FF), row(FF), row(D), row(D), _mod_spec(3, tpe, nrows), _const((1, D))]
    scratch = [pltpu.VMEM((FF, D), BF16)] * 3 + [pltpu.SemaphoreType.DMA((3 * NDEV,))]
    args = [dout, *xs, a, b, o, mod3, norm_w, wall]
    return _pcall(body, name=name, grid=(n_tiles,), out_shape=tuple(out_shape), in_specs=in_specs,
                  out_specs=tuple(out_specs), scratch=scratch, vmem_mb=60)(*args)


def _ffn_bwd_dw(h, d_o, da, db, g, *, tr, name):
    r = h.shape[0]
    fh = FF // 2
    fsh = FF // NDEV
    nk = r // tr

    def body(h_ref, do_ref, da_ref, db_ref, g_ref, out_ref, acc1, acc3, acc2):
        k = pl.program_id(1)

        @pl.when(k == 0)
        def _():
            acc1[...] = jnp.zeros_like(acc1)
            acc3[...] = jnp.zeros_like(acc3)
            acc2[...] = jnp.zeros_like(acc2)

        hv = h_ref[...]
        acc1[...] += _mm_tn(da_ref[...], hv)
        acc3[...] += _mm_tn(db_ref[...], hv)
        acc2[...] += _mm_tn(g_ref[...], do_ref[...])

        @pl.when(k == nk - 1)
        def _():
            for i, acc in enumerate((acc1, acc3, acc2)):
                out_ref[:, i * fsh:(i + 1) * fsh, :] = acc[...].reshape(NDEV // 2, fsh, D).astype(BF16)

    rowd = pl.BlockSpec((tr, D), lambda f, k: (k, 0))
    rowf = pl.BlockSpec((tr, fh), lambda f, k: (k, f))
    return _pcall(
        body, name=name, grid=(2, nk), out_shape=_sds((NDEV, 3 * fsh, D), BF16),
        in_specs=[rowd, rowd, rowf, rowf, rowf],
        out_specs=pl.BlockSpec((NDEV // 2, 3 * fsh, D), lambda f, k: (f, 0, 0)),
        scratch=[pltpu.VMEM((fh, D), F32)] * 3, vmem_mb=56)(h, d_o, da, db, g)


def _direct_sends(x_ref, out_ref, send_sems, recv_sems):
    mx, my, mc = lax.axis_index("x"), lax.axis_index("y"), lax.axis_index("c")
    sends = []
    for k in range(1, NDEV):
        px = 1 - mx if (k & 4) else mx
        py = 1 - my if (k & 2) else my
        pc = 1 - mc if (k & 1) else mc
        sends.append(pltpu.make_async_remote_copy(
            src_ref=x_ref.at[4 * px + 2 * py + pc], dst_ref=out_ref.at[k - 1],
            send_sem=send_sems.at[k - 1], recv_sem=recv_sems.at[k - 1], device_id=(px, py, pc), device_id_type=MESH))
    return sends


def _sum_direct(x, recv, tr, name):
    _, r, c = x.shape

    def body(x_ref, r_ref, o_ref):
        me = 4 * lax.axis_index("x") + 2 * lax.axis_index("y") + lax.axis_index("c")
        acc = x_ref[0].astype(F32)
        for j in range(1, NDEV):
            acc = jnp.where(me == j, x_ref[j].astype(F32), acc)
        for j in range(NDEV - 1):
            acc = acc + r_ref[j].astype(F32)
        o_ref[...] = acc

    return _pcall(body, name=name, grid=(r // tr,), out_shape=_sds((r, c), F32),
                  in_specs=[pl.BlockSpec((NDEV, tr, c), lambda t: (0, t, 0)),
                            pl.BlockSpec((NDEV - 1, tr, c), lambda t: (0, t, 0))],
                  out_specs=pl.BlockSpec((tr, c), lambda t: (t, 0)))(x, recv)


def _exchange_behind(x_ref, recv_ref, send_sems, recv_sems, first, last):
    sends = _direct_sends(x_ref, recv_ref, send_sems, recv_sems)

    @pl.when(first)
    def _():
        for cp in sends:
            cp.start()

    @pl.when(last)
    def _():
        for cp in sends:
            cp.wait_recv()
        for cp in sends:
            cp.wait_send()


def _ffn_bwd_dw_one(lhs, rhs, *, tr, name, part=None):
    r = lhs.shape[0]
    fsh = FF // NDEV
    nk = r // tr
    fused = part is not None
    nslot = NDEV - 1

    def body(*refs):
        if fused:
            lhs_ref, rhs_ref, part_ref, out_ref, recv_ref, acc, send_sems, recv_sems = refs
        else:
            lhs_ref, rhs_ref, out_ref, acc = refs
        k = pl.program_id(0)
        if fused:
            _exchange_behind(part_ref, recv_ref, send_sems, recv_sems, k == 0, k == nk - 1)

        @pl.when(k == 0)
        def _():
            acc[...] = jnp.zeros_like(acc)

        acc[...] += _mm_tn(lhs_ref[...], rhs_ref[...])

        @pl.when(k == nk - 1)
        def _():
            out_ref[...] = acc[...].reshape(NDEV, fsh, D).astype(BF16)

    in_specs = [pl.BlockSpec((tr, FF), lambda k: (k, 0)), pl.BlockSpec((tr, D), lambda k: (k, 0))]
    out_shape = [_sds((NDEV, fsh, D), BF16)]
    out_specs = [_const((NDEV, fsh, D))]
    scratch = [pltpu.VMEM((FF, D), F32)]
    args = [lhs, rhs]
    if fused:
        in_specs.append(ANY)
        args.append(part)
        out_shape.append(_sds((nslot,) + part.shape[1:], part.dtype))
        out_specs.append(ANY)
        scratch += [pltpu.SemaphoreType.DMA((nslot,))] * 2
    res = _pcall(body, name=name, grid=(nk,), out_shape=tuple(out_shape), in_specs=in_specs,
                 out_specs=tuple(out_specs), scratch=scratch, vmem_mb=48)(*args)
    return res if fused else res[0]


_PIECES =((0, 128), (128, 384), (384, 896), (896, 1408), (1408, 1536))


def _proj_fwd(x1, mod2, norm_w, wint, *, tm, n_tiles, tpe, name="proj_fwd"):
    nrows = mod2.shape[0]
    r = n_tiles * tm

    def body(x_ref, mod_ref, nw_ref, w_ref, ckv_ref, q_ref, u_ref, v_ref, kpe_ref):
        x = x_ref[...]
        n = x * _rms(x) * nw_ref[...]
        h = (n * (1.0 + mod_ref[0, 1:2, :]) + mod_ref[0, 0:1, :]).astype(BF16)
        for (lo, hi), ref in zip(_PIECES, (ckv_ref, q_ref, u_ref, v_ref, kpe_ref)):
            ref[...] = _mm_nt(h, w_ref[lo:hi, :])

    row = lambda cols: pl.BlockSpec((tm, cols), lambda t: (t, 0))
    widths = [hi - lo for lo, hi in _PIECES]
    return _pcall(
        body, name=name, grid=(n_tiles,),
        out_shape=tuple(_sds((r, w), F32) for w in widths),
        in_specs=[row(D), _mod_spec(2, tpe, nrows), _const((1, D)), _const((WIN_ROWS, D))],
        out_specs=tuple(row(w) for w in widths), vmem_mb=40)(x1, mod2, norm_w, wint)


def _proj_bwd(dckv, dkpe, dq, du, dv, dx2, x1, mod2, norm_w, wint, *, tm, n_tiles, tpe, n_lat, name="proj_bwd"):
    nrows = mod2.shape[0]
    r = n_tiles * tm

    def body(dckv_ref, dkpe_ref, dq_ref, du_ref, dv_ref, dx2_ref, x_ref, mod_ref, nw_ref, w_ref,
             dx_ref, dw_ref, dmod_ref, dnw_ref):
        t = pl.program_id(0)
        is_lat = t < n_lat

        @pl.when(t == 0)
        def _():
            dw_ref[...] = jnp.zeros_like(dw_ref)
            dnw_ref[...] = jnp.zeros_like(dnw_ref)

        @pl.when(jnp.where(is_lat, t % tpe == 0, t == n_lat))
        def _():
            dmod_ref[...] = jnp.zeros_like(dmod_ref)

        x = x_ref[...]
        rr = _rms(x)
        xh = x * rr
        nw = nw_ref[...]
        n = xh * nw
        scale = mod_ref[0, 1:2, :]
        h = (n * (1.0 + scale) + mod_ref[0, 0:1, :]).astype(BF16)
        zero = jnp.zeros((), BF16)
        pieces = (dckv_ref[...], jnp.where(is_lat, dq_ref[...], zero), jnp.where(is_lat, du_ref[...], zero),
                  jnp.where(is_lat, dv_ref[...], zero), dkpe_ref[...])
        dh = None
        for (lo, hi), piece in zip(_PIECES, pieces):
            part = _mm(piece, w_ref[lo:hi, :])
            dh = part if dh is None else dh + part
        dn = dh * (1.0 + scale)
        dxh = dn * nw
        dx = rr * (dxh - xh * jnp.mean(dxh * xh, axis=-1, keepdims=True))
        dx_ref[...] = dx + jnp.where(is_lat, dx2_ref[...], 0.0)
        dmod_ref[0, 0:1, :] += _rowsum(dh)
        dmod_ref[0, 1:2, :] += _rowsum(dh * n)
        dnw_ref[...] += _rowsum(dn * xh)
        for (lo, hi), piece in zip(_PIECES, pieces):
            dw_ref[lo:hi, :] += _mm_tn(piece, h)

    row = lambda cols: pl.BlockSpec((tm, cols), lambda t: (t, 0))
    lat = lambda cols: pl.BlockSpec((tm, cols), lambda t: (jnp.minimum(t, n_lat - 1), 0))
    return _pcall(
        body, name=name, grid=(n_tiles,),
        out_shape=(_sds((r, D), F32), _sds((WIN_ROWS, D), F32), _sds((nrows, 2, D), F32), _sds((1, D), F32)),
        in_specs=[row(128), row(128), lat(256), lat(512), lat(512), lat(D), row(D), _mod_spec(2, tpe, nrows),
                  _const((1, D)), _const((WIN_ROWS, D))],
        out_specs=(row(D), _const((WIN_ROWS, D)), _mod_spec(2, tpe, nrows), _const((1, D))),
        vmem_mb=48)(dckv, dkpe, dq, du, dv, dx2, x1, mod2, norm_w, wint)


def _seg_sum(x, seg):
    return _mm(x.astype(BF16), seg)


def _seg_bcast(v, segt2):
    hi = v.astype(BF16)
    lo = (v - hi.astype(F32)).astype(BF16)
    return _mm(jnp.concatenate([hi, lo], axis=-1), segt2)


def _rope_pairs(t, cos, sin, rot2):
    cos2, sin2 = jnp.concatenate([cos, cos], axis=-1), jnp.concatenate([sin, sin], axis=-1)
    out = []
    for j in range(H // 2):
        tj = t[:, 2 * j * LANE:2 * (j + 1) * LANE]
        out.append(tj * cos2 + _dot_hl(tj, rot2) * sin2)
    return jnp.concatenate(out, axis=-1)


def _head_norm_rope(x, w_pad, cos, sin, seg, segt2, rot2, rope=True):
    rh = lax.rsqrt(_seg_sum(x * x, seg) * (1.0 / DH) + EPS)
    rb = _seg_bcast(rh, segt2)
    y = x * rb
    out = _rope_pairs(y * w_pad, cos, sin, rot2) if rope else None
    return out, y, rb


def _head_norm_rope_bwd(dout, y, rb, w_pad, cos, sin, seg, segt2, rot2_t):
    cos2, sin2 = jnp.concatenate([cos, cos], axis=-1), jnp.concatenate([sin, sin], axis=-1)
    dt = []
    for j in range(H // 2):
        dj = dout[:, 2 * j * LANE:2 * (j + 1) * LANE]
        dt.append(dj * cos2 + _dot_hl(dj * sin2, rot2_t))
    dt = jnp.concatenate(dt, axis=-1)
    dw = _rowsum(dt * y)
    dy = dt * w_pad
    mean_h = _seg_sum(dy * y, seg) * (1.0 / DH)
    return rb * (dy - y * _seg_bcast(mean_h, segt2)), dw


def _q_prep_fwd(qp, qa_w, wuq, wq, cos, sin, cs, *, tm, n_lat, tpe):
    def body(qp_ref, qa_ref, wuq_ref, wq_ref, cos_ref, sin_ref, seg, segt, rot, q_ref):
        x = qp_ref[...]
        cq = (x * _rms(x) * qa_ref[...]).astype(BF16)
        q, _, _ = _head_norm_rope(_mm_nt(cq, wuq_ref[...]), wq_ref[...], cos_ref[...], sin_ref[...],
                                  seg[...], segt[...], rot[...])
        q_ref[...] = q.astype(BF16)

    row = lambda cols: pl.BlockSpec((tm, cols), lambda t: (t, 0))
    tab = pl.BlockSpec((tm, LANE), lambda t: (t % tpe, 0))
    return _pcall(
        body, name="q_prep_fwd", grid=(n_lat,), out_shape=_sds((n_lat * tm, HP), BF16),
        in_specs=[row(QL), _const((1, QL)), _const((HP, QL)), _const((1, HP)), tab, tab,
                  _const((HP, LANE)), _const((2 * LANE, HP)), _const((2 * LANE, 2 * LANE))],
        out_specs=row(HP))(qp, qa_w, wuq, wq, cos, sin, cs["seg_h"], cs["seg_ht"], cs["rot"])


def _q_prep_bwd(dq, qp, qa_w, wuq, wq, cos, sin, cs, *, tm, n_lat, tpe):
    def body(dq_ref, qp_ref, qa_ref, wuq_ref, wq_ref, cos_ref, sin_ref, seg, segt, rot, rot_t,
             dqp_ref, dwuq_ref, dqa_ref, dwq_ref):
        @pl.when(pl.program_id(0) == 0)
        def _():
            dwuq_ref[...] = jnp.zeros_like(dwuq_ref)
            dqa_ref[...] = jnp.zeros_like(dqa_ref)
            dwq_ref[...] = jnp.zeros_like(dwq_ref)

        x = qp_ref[...]
        ra = _rms(x)
        xh = x * ra
        qa = qa_ref[...]
        cq = (xh * qa).astype(BF16)
        wuq_v = wuq_ref[...]
        wq_v, cos_v, sin_v = wq_ref[...], cos_ref[...], sin_ref[...]
        _, y, rb = _head_norm_rope(_mm_nt(cq, wuq_v), wq_v, cos_v, sin_v, seg[...], segt[...], rot[...], rope=False)
        dqraw, dwq = _head_norm_rope_bwd(dq_ref[...], y, rb, wq_v, cos_v, sin_v, seg[...], segt[...], rot_t[...])
        dqraw = dqraw.astype(BF16)
        dcq = _mm(dqraw, wuq_v)
        dxh = dcq * qa
        dqp_ref[...] = (ra * (dxh - xh * jnp.mean(dxh * xh, axis=-1, keepdims=True))).astype(BF16)
        dwuq_ref[...] += _mm_tn(dqraw, cq)
        dqa_ref[...] += _rowsum(dcq * xh)
        dwq_ref[...] += dwq

    row = lambda cols: pl.BlockSpec((tm, cols), lambda t: (t, 0))
    tab = pl.BlockSpec((tm, LANE), lambda t: (t % tpe, 0))
    return _pcall(
        body, name="q_prep_bwd", grid=(n_lat,),
        out_shape=(_sds((n_lat * tm, QL), BF16), _sds((HP, QL), F32), _sds((1, QL), F32), _sds((1, HP), F32)),
        in_specs=[row(HP), row(QL), _const((1, QL)), _const((HP, QL)), _const((1, HP)), tab, tab,
                  _const((HP, LANE)), _const((2 * LANE, HP)), _const((2 * LANE, 2 * LANE)), _const((2 * LANE, 2 * LANE))],
        out_specs=(row(QL), _const((HP, QL)), _const((1, QL)), _const((1, HP))), vmem_mb=40)(
            dq, qp, qa_w, wuq, wq, cos, sin, cs["seg_h"], cs["seg_ht"], cs["rot"], cs["rot_t"])


def _kv_tab_spec(tm, tpe, n_lat):
    return pl.BlockSpec((tm, LANE), lambda t: (jnp.where(t < n_lat, t % tpe, tpe), 0))


def _split_kv(kv, kpe):
    low = lax.broadcasted_iota(jnp.int32, (kv.shape[0], LANE), 1) < DN
    kx, v = [], []
    for h in range(H):
        blk = kv[:, h * LANE:(h + 1) * LANE]
        kx.append(jnp.where(low, blk, kpe))
        v.append(jnp.where(low, pltpu.roll(blk, DN, 1), 0.0))
    return jnp.concatenate(kx, axis=-1), jnp.concatenate(v, axis=-1)


def _kv_prep_fwd(ckv, kpe, kva_w, wukv, wk, cosk, sink, cs, *, tm, n_tiles, tpe, n_lat):
    def body(ckv_ref, kpe_ref, kva_ref, wukv_ref, wk_ref, cos_ref, sin_ref, seg, segt, rot, k_ref, v_ref):
        x = ckv_ref[...]
        ckvn = (x * _rms(x) * kva_ref[...]).astype(BF16)
        kx, v = _split_kv(_mm_nt(ckvn, wukv_ref[...]), kpe_ref[...])
        k, _, _ = _head_norm_rope(kx, wk_ref[...], cos_ref[...], sin_ref[...], seg[...], segt[...], rot[...])
        k_ref[...] = k.astype(BF16)
        v_ref[...] = v.astype(BF16)

    row = lambda cols: pl.BlockSpec((tm, cols), lambda t: (t, 0))
    tab = _kv_tab_spec(tm, tpe, n_lat)
    r = n_tiles * tm
    return _pcall(
        body, name="kv_prep_fwd", grid=(n_tiles,), out_shape=(_sds((r, HP), BF16), _sds((r, HP), BF16)),
        in_specs=[row(KVL), row(LANE), _const((1, KVL)), _const((HP, KVL)), _const((1, HP)), tab, tab,
                  _const((HP, LANE)), _const((2 * LANE, HP)), _const((2 * LANE, 2 * LANE))],
        out_specs=(row(HP), row(HP)), vmem_mb=40)(
            ckv, kpe, kva_w, wukv, wk, cosk, sink, cs["seg_h"], cs["seg_ht"], cs["rot"])


def _kv_prep_bwd(dks, dvs, ckv, kpe, kva_w, wukv, wk, cosk, sink, cs, *, tm, n_tiles, tpe, n_lat):
    def body(dkl_ref, dkc_ref, dvl_ref, dvc_ref, ckv_ref, kpe_ref, kva_ref, wukv_ref, wk_ref, cos_ref, sin_ref,
             seg, segt, rot, rot_t, dckv_ref, dkpe_ref, dwukv_ref, dkva_ref, dwk_ref):
        t = pl.program_id(0)
        is_lat = t < n_lat

        @pl.when(t == 0)
        def _():
            dwukv_ref[...] = jnp.zeros_like(dwukv_ref)
            dkva_ref[...] = jnp.zeros_like(dkva_ref)
            dwk_ref[...] = jnp.zeros_like(dwk_ref)

        dk = jnp.where(is_lat, dkl_ref[...], dkc_ref[...])
        dv = jnp.where(is_lat, dvl_ref[...], dvc_ref[...])
        x = ckv_ref[...]
        ra = _rms(x)
        xh = x * ra
        kva = kva_ref[...]
        ckvn = (xh * kva).astype(BF16)
        wukv_v = wukv_ref[...]
        wk_v, cos_v, sin_v = wk_ref[...], cos_ref[...], sin_ref[...]
        kx, _ = _split_kv(_mm_nt(ckvn, wukv_v), kpe_ref[...])
        _, y, rb = _head_norm_rope(kx, wk_v, cos_v, sin_v, seg[...], segt[...], rot[...], rope=False)
        dkx, dwk = _head_norm_rope_bwd(dk, y, rb, wk_v, cos_v, sin_v, seg[...], segt[...], rot_t[...])
        dkpe = dkx[:, 0:LANE]
        for h in range(1, H):
            dkpe = dkpe + dkx[:, h * LANE:(h + 1) * LANE]
        lane = lax.broadcasted_iota(jnp.int32, (tm, LANE), 1)
        dkpe_ref[...] = jnp.where((lane >= DN) & (lane < DH), dkpe, 0.0).astype(BF16)
        dkv = jnp.concatenate([jnp.where(lane < DN, dkx[:, h * LANE:(h + 1) * LANE],
                                         pltpu.roll(dv[:, h * LANE:(h + 1) * LANE], DN, 1)) for h in range(H)],
                              axis=-1).astype(BF16)
        dckvn = _mm(dkv, wukv_v)
        dxh = dckvn * kva
        dckv_ref[...] = (ra * (dxh - xh * jnp.mean(dxh * xh, axis=-1, keepdims=True))).astype(BF16)
        dwukv_ref[...] += _mm_tn(dkv, ckvn)
        dkva_ref[...] += _rowsum(dckvn * xh)
        dwk_ref[...] += dwk

    row = lambda cols: pl.BlockSpec((tm, cols), lambda t: (t, 0))
    lat = pl.BlockSpec((tm, HP), lambda t: (jnp.minimum(t, n_lat - 1), 0))
    ctx = pl.BlockSpec((tm, HP), lambda t: (jnp.maximum(t - n_lat, 0), 0))
    tab = _kv_tab_spec(tm, tpe, n_lat)
    r = n_tiles * tm
    return _pcall(
        body, name="kv_prep_bwd", grid=(n_tiles,),
        out_shape=(_sds((r, KVL), BF16), _sds((r, LANE), BF16), _sds((HP, KVL), F32), _sds((1, KVL), F32),
                   _sds((1, HP), F32)),
        in_specs=[lat, ctx, lat, ctx, row(KVL), row(LANE), _const((1, KVL)), _const((HP, KVL)), _const((1, HP)),
                  tab, tab, _const((HP, LANE)), _const((2 * LANE, HP)), _const((2 * LANE, 2 * LANE)),
                  _const((2 * LANE, 2 * LANE))],
        out_specs=(row(KVL), row(LANE), _const((HP, KVL)), _const((1, KVL)), _const((1, HP))), vmem_mb=48)(
            dks[0], dks[1], dvs[0], dvs[1], ckv, kpe, kva_w, wukv, wk, cosk, sink,
            cs["seg_h"], cs["seg_ht"], cs["rot"], cs["rot_t"])


_SCALE = DH ** -0.5
_SCALE_LOG2E = _SCALE * 1.4426950408889634


def _key_chunks(s, nc, ck):
    return ([(0, lo, min(lo + ck, s)) for lo in range(0, s, ck)]
            + [(1, lo, min(lo + ck, nc)) for lo in range(0, nc, ck)])


def _attn_fwd(q, k, v, *, nb, s, nc, tq, ck):
    tpe = s // tq
    r_lat = nb * s
    chunks = _key_chunks(s, nc, ck)
    hp = 4

    def body(q_ref, kl_ref, kc_ref, vl_ref, vc_ref, o_ref, lse_ref):
        k_refs, v_refs = (kl_ref, kc_ref), (vl_ref, vc_ref)
        for hh in range(hp):
            hs = slice(hh * LANE, (hh + 1) * LANE)
            qv = q_ref[:, hs]
            xs = [_mm_nt(qv, k_refs[w][lo:hi, hs]) for w, lo, hi in chunks]
            m = jnp.max(xs[0], axis=-1, keepdims=True)
            for x in xs[1:]:
                m = jnp.maximum(m, jnp.max(x, axis=-1, keepdims=True))
            l = acc = None
            for x, (w, lo, hi) in zip(xs, chunks):
                e = jnp.exp2((x - m) * _SCALE_LOG2E)
                lc = jnp.sum(e, axis=-1, keepdims=True)
                pv = _mm(e.astype(BF16), v_refs[w][lo:hi, hs])
                l = lc if l is None else l + lc
                acc = pv if acc is None else acc + pv
            o_ref[:, hs] = (acc / l).astype(BF16)
            lse = m * _SCALE_LOG2E + jnp.log2(l)
            lse_ref[hh] = jnp.transpose(jnp.broadcast_to(lse, (tq, LANE)))[0:8, :]

    qs = pl.BlockSpec((tq, hp * LANE), lambda i, j, t: (i * tpe + t, j))
    kl = pl.BlockSpec((s, hp * LANE), lambda i, j, t: (i, j))
    kc = pl.BlockSpec((nc, hp * LANE), lambda i, j, t: (r_lat // nc + i, j))
    ls = pl.BlockSpec((hp, 8, tq), lambda i, j, t: (i * (H // hp) + j, 0, t))
    return _pcall(body, name="attn_fwd", grid=(nb, H // hp, tpe),
                  out_shape=(_sds((r_lat, HP), BF16), _sds((nb * H, 8, s), F32)),
                  in_specs=[qs, kl, kc, kl, kc], out_specs=(qs, ls), vmem_mb=48)(q, k, k, v, v)


def _attn_bwd(q, k, v, o, do, lse, part, *, nb, s, nc, tq, ck):
    tpe = s // tq
    r_lat = nb * s
    chunks = _key_chunks(s, nc, ck)
    hp = 2
    n_steps = nb * (H // hp) * tpe

    def body(q_ref, kl_ref, kc_ref, vl_ref, vc_ref, o_ref, do_ref, lse_ref, part_ref,
             dq_ref, dkl_ref, dkc_ref, dvl_ref, dvc_ref, recv_ref, akl, akc, avl, avc, send_sems, recv_sems):
        t = pl.program_id(2)
        step = (pl.program_id(0) * (H // hp) + pl.program_id(1)) * tpe + t
        _exchange_behind(part_ref, recv_ref, send_sems, recv_sems, step == 0, step == n_steps - 1)

        @pl.when(t == 0)
        def _():
            akl[...] = jnp.zeros_like(akl)
            akc[...] = jnp.zeros_like(akc)
            avl[...] = jnp.zeros_like(avl)
            avc[...] = jnp.zeros_like(avc)

        k_refs, v_refs, ak, av = (kl_ref, kc_ref), (vl_ref, vc_ref), (akl, akc), (avl, avc)
        for hh in range(hp):
            hs = slice(hh * LANE, (hh + 1) * LANE)
            qv = q_ref[:, hs]
            lse = jnp.transpose(jnp.concatenate([lse_ref[hh]] * (LANE // 8), axis=0))[:, 0:1]
            dov = do_ref[:, hs]
            delta = jnp.sum(dov.astype(F32) * o_ref[:, hs].astype(F32), axis=-1, keepdims=True)
            dq = None
            for w, lo, hi in chunks:
                kc_v = k_refs[w][lo:hi, hs]
                p = jnp.exp2(_mm_nt(qv, kc_v) * _SCALE_LOG2E - lse)
                ds = (p * (_mm_nt(dov, v_refs[w][lo:hi, hs]) - delta)).astype(BF16)
                part = _mm(ds, kc_v)
                dq = part if dq is None else dq + part
                ak[w][hs, lo:hi] += _mm_tn(qv, ds)
                av[w][hs, lo:hi] += _mm_tn(dov, p.astype(BF16))
            dq_ref[:, hs] = dq * _SCALE

        @pl.when(t == tpe - 1)
        def _():
            dkl_ref[...] = akl[...].T * _SCALE
            dkc_ref[...] = akc[...].T * _SCALE
            dvl_ref[...] = avl[...].T
            dvc_ref[...] = avc[...].T

    qs = pl.BlockSpec((tq, hp * LANE), lambda i, j, t: (i * tpe + t, j))
    kl = pl.BlockSpec((s, hp * LANE), lambda i, j, t: (i, j))
    kc = pl.BlockSpec((nc, hp * LANE), lambda i, j, t: (r_lat // nc + i, j))
    kc_out = pl.BlockSpec((nc, hp * LANE), lambda i, j, t: (i, j))
    ls = pl.BlockSpec((hp, 8, tq), lambda i, j, t: (i * (H // hp) + j, 0, t))
    return _pcall(
        body, name="attn_bwd", grid=(nb, H // hp, tpe),
        out_shape=(_sds((r_lat, HP), F32), _sds((r_lat, HP), F32), _sds((nb * nc, HP), F32),
                   _sds((r_lat, HP), F32), _sds((nb * nc, HP), F32), _sds((NDEV - 1,) + part.shape[1:], part.dtype)),
        in_specs=[qs, kl, kc, kl, kc, qs, qs, ls, ANY], out_specs=(qs, kl, kc_out, kl, kc_out, ANY),
        scratch=[pltpu.VMEM((hp * LANE, s), F32), pltpu.VMEM((hp * LANE, nc), F32)] * 2
        + [pltpu.SemaphoreType.DMA((NDEV - 1,))] * 2,
        vmem_mb=60)(q, k, k, v, v, o, do, lse, part)


def _chunks_side_by_side(x, j, nch):
    return jnp.concatenate([x[c * CH:(c + 1) * CH, j * LANE:(j + 1) * LANE] for c in range(nch)], axis=-1)


def _first_group_lanes(nch):
    return (lax.broadcasted_iota(jnp.int32, (CH, nch * LANE), 1) & (LANE - 1)) < GD


def _gating(vn, ws_ref, bias_ref, s_scr, tm):
    nch = tm // CH
    first = _first_group_lanes(nch)
    for j in range(G // 2):
        ls = slice(j * LANE, (j + 1) * LANE)
        vst = _chunks_side_by_side(vn, j, nch)
        st = jnp.where(first, _mm(ws_ref[2 * j], vst), _mm(ws_ref[2 * j + 1], vst))
        for c in range(nch):
            s_scr[c * CH:(c + 1) * CH, ls] = st[:, c * LANE:(c + 1) * LANE] + bias_ref[:, ls]


def _compact_heads(x):
    low = lax.broadcasted_iota(jnp.int32, (x.shape[0], LANE), 1) < DV
    out = []
    for j in range(H // 2):
        even = x[:, 2 * j * LANE:(2 * j + 1) * LANE].astype(F32)
        odd = x[:, (2 * j + 1) * LANE:(2 * j + 2) * LANE].astype(F32)
        out.append(jnp.where(low, even, pltpu.roll(odd, DV, 1)))
    return jnp.concatenate(out, axis=-1)


def _expand_heads(x):
    low = lax.broadcasted_iota(jnp.int32, (x.shape[0], LANE), 1) < DV
    out = []
    for j in range(H // 2):
        blk = x[:, j * LANE:(j + 1) * LANE]
        out.append(jnp.where(low, blk, 0.0))
        out.append(jnp.where(low, pltpu.roll(blk, DV, 1), 0.0))
    return jnp.concatenate(out, axis=-1)


def _mix_fwd(u, v, attn, x1, gate, wv, ws, bias, wout, cs, *, tm, n_lat, tpe):
    nrows = gate.shape[0]

    def body(u_ref, v_ref, attn_ref, x_ref, gate_ref, wv_ref, ws_ref, bias_ref, wout_ref, seg, segt,
             x2_ref, mix_ref, s_scr):
        vg = _gelu(v_ref[...])
        rg = lax.rsqrt(_seg_sum(vg * vg, seg[...]) * (1.0 / GD) + EPS)
        vn = (vg * _seg_bcast(rg, segt[...]) * wv_ref[...]).astype(BF16)
        _gating(vn, ws_ref, bias_ref, s_scr, tm)
        sg = (_gelu(u_ref[...]) * s_scr[...]).astype(BF16)
        attn_c = _compact_heads(attn_ref[...]).astype(BF16)
        mix = _mm(attn_c, wout_ref[0:H * DV, :]) + _mm(sg, wout_ref[H * DV:, :])
        mix_ref[...] = mix.astype(BF16)
        x2_ref[...] = x_ref[...] + gate_ref[0] * mix

    row = lambda cols: pl.BlockSpec((tm, cols), lambda t: (t, 0))
    r = n_lat * tm
    return _pcall(
        body, name="mix_fwd", grid=(n_lat,),
        out_shape=(_sds((r, D), F32), _sds((r, D), BF16)),
        in_specs=[row(G * GD), row(G * GD), row(HP), row(D), _mod_spec(1, tpe, nrows), _const((1, G * GD)),
                  _const((G, CH, CH)), _const((CH, G * GD)), _const((D, D)), _const((G * GD, LANE)),
                  _const((2 * LANE, G * GD))],
        out_specs=(row(D), row(D)), scratch=[pltpu.VMEM((tm, G * GD), F32)], vmem_mb=40)(
            u, v, attn, x1, gate, wv, ws, bias, wout, cs["seg_g"], cs["seg_gt"])


def _mix_bwd(dx2, mix, u, v, attn, gate, wv, ws, wst, bias, wout, cs, *, tm, n_lat, tpe):
    nrows = gate.shape[0]
    wrows = H * DV + G * GD

    def body(dx2_ref, mix_ref, u_ref, v_ref, attn_ref, gate_ref, wv_ref, ws_ref, wst_ref, bias_ref, wout_ref, seg, segt,
             dattn_ref, du_ref, dv_ref, dgate_ref, dwout_ref, dws_ref, dbs_ref, dwv_ref, s_scr, dvn_scr, dbias_scr):
        t = pl.program_id(0)

        @pl.when(t == 0)
        def _():
            dwout_ref[...] = jnp.zeros_like(dwout_ref)
            dws_ref[...] = jnp.zeros_like(dws_ref)
            dwv_ref[...] = jnp.zeros_like(dwv_ref)
            dbias_scr[...] = jnp.zeros_like(dbias_scr)

        @pl.when(t % tpe == 0)
        def _():
            dgate_ref[...] = jnp.zeros_like(dgate_ref)

        dx2 = dx2_ref[...]
        dmix = (dx2 * gate_ref[0]).astype(BF16)
        dcat = _mm_nt(dmix, wout_ref[...])
        dattn_ref[...] = _expand_heads(dcat[:, :H * DV]).astype(BF16)
        dsg = dcat[:, H * DV:]

        vraw = v_ref[...]
        vg = _gelu(vraw)
        rg = lax.rsqrt(_seg_sum(vg * vg, seg[...]) * (1.0 / GD) + EPS)
        r64 = _seg_bcast(rg, segt[...])
        y = vg * r64
        wv_v = wv_ref[...]
        vn = (y * wv_v).astype(BF16)
        _gating(vn, ws_ref, bias_ref, s_scr, tm)
        uraw = u_ref[...]
        ug = _gelu(uraw)
        s = s_scr[...]
        sg = (ug * s).astype(BF16)
        du_ref[...] = (dsg * s * _gelu_grad(uraw)).astype(BF16)
        ds = dsg * ug
        dgate_ref[0] += _rowsum(dx2 * mix_ref[...].astype(F32))
        attn_c = _compact_heads(attn_ref[...]).astype(BF16)
        dwout_ref[...] += _mm_tn(jnp.concatenate([attn_c, sg], axis=-1), dmix)

        nch = tm // CH
        first = _first_group_lanes(nch)
        for c in range(nch):
            dbias_scr[...] += ds[c * CH:(c + 1) * CH, :]
        for j in range(G // 2):
            ls = slice(j * LANE, (j + 1) * LANE)
            dst32 = _chunks_side_by_side(ds, j, nch)
            dst = dst32.astype(BF16)
            vst = _chunks_side_by_side(vn, j, nch)
            dvn_st = jnp.where(first, _mm(wst_ref[2 * j], dst), _mm(wst_ref[2 * j + 1], dst))
            for c in range(nch):
                dvn_scr[c * CH:(c + 1) * CH, ls] = dvn_st[:, c * LANE:(c + 1) * LANE]
            dws_ref[2 * j] += _mm_nt(jnp.where(first, dst32, 0.0).astype(BF16), vst)
            dws_ref[2 * j + 1] += _mm_nt(jnp.where(first, 0.0, dst32).astype(BF16), vst)

        dvn = dvn_scr[...]
        dwv_ref[...] += _rowsum(dvn * y)
        dy = dvn * wv_v
        mean_g = _seg_sum(dy * y, seg[...]) * (1.0 / GD)
        dvg = r64 * (dy - y * _seg_bcast(mean_g, segt[...]))
        dv_ref[...] = (dvg * _gelu_grad(vraw)).astype(BF16)

        @pl.when(t == n_lat - 1)
        def _():
            dbs_ref[...] = _dot_hl(dbias_scr[...], seg[...])

    row = lambda cols: pl.BlockSpec((tm, cols), lambda t: (t, 0))
    r = n_lat * tm
    return _pcall(
        body, name="mix_bwd", grid=(n_lat,),
        out_shape=(_sds((r, HP), BF16), _sds((r, G * GD), BF16), _sds((r, G * GD), BF16), _sds((nrows, 1, D), F32),
                   _sds((wrows, D), F32), _sds((G, CH, CH), F32), _sds((CH, LANE), F32), _sds((1, G * GD), F32)),
        in_specs=[row(D), row(D), row(G * GD), row(G * GD), row(HP), _mod_spec(1, tpe, nrows), _const((1, G * GD)),
                  _const((G, CH, CH)), _const((G, CH, CH)), _const((CH, G * GD)), _const((wrows, D)),
                  _const((G * GD, LANE)), _const((2 * LANE, G * GD))],
        out_specs=(row(HP), row(G * GD), row(G * GD), _mod_spec(1, tpe, nrows), _const((wrows, D)),
                   _const((G, CH, CH)), _const((CH, LANE)), _const((1, G * GD))),
        scratch=[pltpu.VMEM((tm, G * GD), F32), pltpu.VMEM((tm, G * GD), F32), pltpu.VMEM((CH, G * GD), F32)],
        vmem_mb=56)(dx2, mix, u, v, attn, gate, wv, ws, wst, bias, wout, cs["seg_g"], cs["seg_gt"])


def _adamw_math(w, g, m, v):
    m2 = ADAM_B1 * m + (1.0 - ADAM_B1) * g
    v2 = ADAM_B2 * v + (1.0 - ADAM_B2) * (g * g)
    m_hat = m2 / (1.0 - ADAM_B1 ** ADAM_STEP)
    v_hat = v2 / (1.0 - ADAM_B2 ** ADAM_STEP)
    delta = -ADAM_LR * (m_hat / (jnp.sqrt(v_hat) + ADAM_EPS) + ADAM_WD * w)
    return delta, m2, v2


def _row_tile(r, c):
    best = r
    for tr in range(8, r, 8):
        if r % tr == 0 and tr * c * 4 <= MIB:
            best = tr
    return best


def _adamw(w, g, m, v, name):
    r, c = w.shape
    tr = _row_tile(r, c)

    def body(w_ref, g_ref, m_ref, v_ref, d_ref, mo_ref, vo_ref):
        d_ref[...], mo_ref[...], vo_ref[...] = _adamw_math(w_ref[...], g_ref[...], m_ref[...], v_ref[...])

    blk = pl.BlockSpec((tr, c), lambda t: (t, 0))
    return _pcall(body, name=name, grid=(r // tr,), out_shape=(_sds((r, c), F32),) * 3,
                  in_specs=[blk] * 4, out_specs=(blk,) * 3)(w, g, m, v)


def _adamw_small(params):
    n = len(params)

    def body(*refs):
        ins, outs = refs[:4 * n], refs[4 * n:]
        for i in range(n):
            w, g, m, v = (ins[4 * i + k][...] for k in range(4))
            if i == 0:
                sig = _sigmoid(w)
                g = g * (sig * (1.0 + w * (1.0 - sig)))
            d, m2, v2 = _adamw_math(w, g, m, v)
            outs[4 * i][...] = g
            outs[4 * i + 1][...] = d
            outs[4 * i + 2][...] = m2
            outs[4 * i + 3][...] = v2

    flat = [a for p in params for a in p]
    out_shape = tuple(_sds(p[0].shape, F32) for p in params for _ in range(4))
    res = _pcall(body, name="adamw_small", out_shape=out_shape, in_specs=[VMEM] * (4 * n),
                 out_specs=(VMEM,) * (4 * n))(*flat)
    return [res[4 * i:4 * i + 4] for i in range(n)]


def _rope_tables(s):
    rows = jnp.repeat(jnp.arange(s // GRID_W, dtype=F32), GRID_W)
    cols = jnp.tile(jnp.arange(GRID_W, dtype=F32), s // GRID_W)
    half = DR // 2
    inv = ROPE_BASE ** (-jnp.arange(0, half, 2, dtype=F32) / half)
    ang_r = rows[:, None] * inv
    ang_c = cols[:, None] * inv
    ang = jnp.concatenate([ang_r, ang_r, ang_c, ang_c], axis=-1)
    return jnp.cos(ang), jnp.sin(ang)


def _head_pad(a, real):
    return jnp.pad(a, ((0, 0), (0, LANE - real), (0, 0))).reshape(HP, a.shape[2])


def kernel(x, c, ctx, c_ctx, w_ada, b_ada, norm1_w, ffn1_w1, ffn1_w3, ffn1_w2, norm2_w, w_in, q_a_norm_w, w_uq, kv_a_norm_w, w_ukv, q_norm_w, k_norm_w, v_norm_w, w_s, b_s, w_out, norm3_w, ffn2_w1, ffn2_w3, ffn2_w2, loss_target, m_c_ctx, m_w_ada, m_b_ada, m_norm1_w, m_ffn1_w1, m_ffn1_w3, m_ffn1_w2, m_norm2_w, m_w_in, m_q_a_norm_w, m_w_uq, m_kv_a_norm_w, m_w_ukv, m_q_norm_w, m_k_norm_w, m_v_norm_w, m_w_s, m_b_s, m_w_out, m_norm3_w, m_ffn2_w1, m_ffn2_w3, m_ffn2_w2, v_c_ctx, v_w_ada, v_b_ada, v_norm1_w, v_ffn1_w1, v_ffn1_w3, v_ffn1_w2, v_norm2_w, v_w_in, v_q_a_norm_w, v_w_uq, v_kv_a_norm_w, v_w_ukv, v_q_norm_w, v_k_norm_w, v_v_norm_w, v_w_s, v_b_s, v_w_out, v_norm3_w, v_ffn2_w1, v_ffn2_w3, v_ffn2_w2):
    nb, s, _ = x.shape
    nc = ctx.shape[1]
    tm = 256 if nc % 256 == 0 else 128
    tpe = s // tm
    n_lat = nb * tpe
    n_all = n_lat + nb * nc // tm
    tmf = 2 * tm if s % (2 * tm) == 0 and (nb * nc) % (2 * tm) == 0 else tm
    tp = tmf
    r_lat = nb * s
    tpe_p, n_lat_p, n_all_p = s // tp, r_lat // tp, (r_lat + nb * nc) // tp
    me = 4 * lax.axis_index("x") + 2 * lax.axis_index("y") + lax.axis_index("c")
    cs = _consts()
    ncol = w_ada.shape[2]
    fsh = ffn1_w1.shape[2]
    assert nb + 1 <= 8 and NDEV * fsh == FF and NDEV * ncol == NMOD * D and s % nc == 0 and nc % tm == 0

    def t16(a):
        return a.T.astype(BF16)

    wpack1 = jnp.concatenate([t16(ffn1_w1[0]), t16(ffn1_w3[0]), ffn1_w2[0].astype(BF16)], axis=0)
    a_loc = jnp.concatenate([c, c_ctx[None, :], jnp.zeros((7 - nb, D), F32)], axis=0)
    a_raw, _, mod_all, wall1 = _ada_front(a_loc, w_ada[0], lax.dynamic_slice_in_dim(b_ada, me * ncol, ncol, axis=1),
                                          wpack1)
    a_raw = a_raw.reshape(NDEV * 8, D)
    mod_mine = lax.dynamic_slice_in_dim(mod_all, 8 * me, 8, axis=1)
    modtab = mod_mine.transpose(1, 0, 2).reshape(8, NMOD, D)[:nb + 1]
    wpack2 = jnp.concatenate([
        t16(ffn2_w1[0]), t16(ffn2_w3[0]), ffn2_w2[0].astype(BF16),
        t16(w_in[0]), jnp.zeros((12, D), BF16),
        w_out[0].astype(BF16),
        t16(w_uq[0]).reshape(24, D), jnp.zeros((8, D), BF16),
        t16(w_ukv[0]).reshape(16, D)], axis=0)

    def head_w(wn):
        return jnp.tile(jnp.pad(wn, ((0, 0), (0, LANE - DH))), (1, H))

    wq, wk = head_w(q_norm_w), head_w(k_norm_w)
    wv = v_norm_w.reshape(1, G * GD)
    ws16 = w_s[0].astype(BF16)
    wst16 = w_s[0].transpose(0, 2, 1).astype(BF16)
    bias = jnp.repeat(b_s[0].T, GD, axis=1)
    cos, sin = _rope_tables(s)
    cos = jnp.pad(cos, ((0, 0), (DN, LANE - DH)), constant_values=1.0)
    sin = jnp.pad(sin, ((0, 0), (DN, LANE - DH)))
    cos_k = jnp.concatenate([cos, jnp.ones((tm, LANE), F32)], axis=0)
    sin_k = jnp.concatenate([sin, jnp.zeros((tm, LANE), F32)], axis=0)

    xs = (x.reshape(r_lat, D), ctx.reshape(nb * nc, D))
    x1, a1, b1, o1, wall2 = _ffn_fwd(xs, modtab[:, 0:3], norm1_w, wall1, 0, tm=tmf, n_tiles=(r_lat + nb * nc) // tmf,
                                     tpe=s // tmf, n_lat=r_lat // tmf, name="ffn1_fwd", gather=wpack2)

    o0 = 3 * fsh
    wint = wall2[:, o0:o0 + 180].reshape(IN_COLS, D)
    z = lambda n: jnp.zeros((n, D), BF16)
    wint = jnp.concatenate([wint[0:128], wint[160:416], wint[416:928], wint[928:1440],
                            z(DN), wint[128:160], z(LANE - DH)], axis=0)
    wout = wall2[:, o0 + 192:o0 + 320].reshape(D, D)
    wuq = _head_pad(wall2[:, o0 + 320:o0 + 344].reshape(H, DH, QL), DH)
    wukv = wall2[:, o0 + 352:o0 + 368].reshape(HP, KVL)

    ckv, qp, u_raw, v_raw, kpe = _proj_fwd(x1, modtab[:, 3:5], norm2_w, wint, tm=tp, n_tiles=n_all_p, tpe=tpe_p)
    q = _q_prep_fwd(qp, q_a_norm_w, wuq, wq, cos, sin, cs, tm=tm, n_lat=n_lat, tpe=tpe)
    k, v = _kv_prep_fwd(ckv, kpe, kv_a_norm_w, wukv, wk, cos_k, sin_k, cs,
                        tm=tm, n_tiles=n_all, tpe=tpe, n_lat=n_lat)
    attn, lse = _attn_fwd(q, k, v, nb=nb, s=s, nc=nc, tq=tm, ck=2048)
    x2, mix = _mix_fwd(u_raw, v_raw, attn, x1, modtab[:nb, 5:6], wv, ws16, bias, wout, cs,
                       tm=tp, n_lat=n_lat_p, tpe=tpe_p)
    dy, a2, b2, o2, lsum = _ffn_fwd((x2,), modtab[:nb, 6:9], norm3_w, wall2, 0, tm=tmf, n_tiles=r_lat // tmf,
                                    tpe=s // tmf, n_lat=r_lat // tmf, name="ffn2_fwd",
                                    target=loss_target.reshape(r_lat, D))

    tr = 2 * tm if n_lat % 2 == 0 and n_all % 2 == 0 else tm
    dx2, da2, db2, g2, do2, h2, dmod678, dnorm3 = _ffn_bwd_dx(
        dy, (x2,), a2, b2, o2, modtab[:nb, 6:9], norm3_w, wall2, 0,
        tm=tm, n_tiles=n_lat, tpe=tpe, n_lat=n_lat, name="ffn2_bwd_dx")
    g_ffn2 = _ffn_bwd_dw(h2, do2, da2, db2, g2, tr=tr, name="ffn2_bwd_dw")

    dattn, du, dv, dgate5, dwout, dws, dbs, dwv = _mix_bwd(
        dx2, mix, u_raw, v_raw, attn, modtab[:nb, 5:6], wv, ws16, wst16, bias, wout, cs, tm=tp, n_lat=n_lat_p, tpe=tpe_p)
    tq = 2 * tm if s % (2 * tm) == 0 else tm
    dq, dk_l, dk_c, dv_l, dv_c, recv_ffn2 = _attn_bwd(q, k, v, attn, dattn, lse, g_ffn2,
                                                      nb=nb, s=s, nc=nc, tq=tq, ck=1024)
    dqp, dwuq, dqa, dwq = _q_prep_bwd(dq, qp, q_a_norm_w, wuq, wq, cos, sin, cs, tm=tm, n_lat=n_lat, tpe=tpe)
    dckv, dkpe, dwukv, dkva, dwk = _kv_prep_bwd((dk_l, dk_c), (dv_l, dv_c), ckv, kpe, kv_a_norm_w, wukv, wk,
                                                cos_k, sin_k, cs, tm=tm, n_tiles=n_all, tpe=tpe, n_lat=n_lat)
    dx1, dwin, dmod34, dnorm2 = _proj_bwd(dckv, dkpe, dqp, du, dv, dx2, x1, modtab[:, 3:5], norm2_w, wint,
                                          tm=tp, n_tiles=n_all_p, tpe=tpe_p, n_lat=n_lat_p)

    def blocks(a):
        return a.reshape(NDEV, a.shape[0] // NDEV, D)

    dwin_o = jnp.concatenate([dwin[0:128], dwin[KPE_LO:KPE_LO + DR], dwin[128:384], dwin[384:896], dwin[896:1408]],
                             axis=0)
    dwuq_o = dwuq.reshape(H, LANE, QL)[:, :DH]
    gmisc = jnp.concatenate([
        blocks(dwin_o).astype(BF16), jnp.zeros((NDEV, 12, D), BF16),
        blocks(dwout).astype(BF16),
        dwuq_o.reshape(NDEV, 24, D).astype(BF16), jnp.zeros((NDEV, 8, D), BF16),
        dwukv.reshape(NDEV, 16, D).astype(BF16)], axis=1)

    dx0, da1, db1, g1, do1, h1, dmod012, dnorm1 = _ffn_bwd_dx(
        dx1, xs, a1, b1, o1, modtab[:, 0:3], norm1_w, wall1, 0,
        tm=tm, n_tiles=n_all, tpe=tpe, n_lat=n_lat, name="ffn1_bwd_dx")
    grad_x = dx0.reshape(nb, s, D)
    g_w1, recv_misc = _ffn_bwd_dw_one(da1, h1, tr=tr, name="ffn1_bwd_dw1", part=gmisc)
    g_w3, recv_w1 = _ffn_bwd_dw_one(db1, h1, tr=tr, name="ffn1_bwd_dw3", part=g_w1)
    g_w2, recv_w3 = _ffn_bwd_dw_one(g1, do1, tr=tr, name="ffn1_bwd_dw2", part=g_w3)

    zrow = jnp.zeros((1, D), F32)
    g_lat = jnp.concatenate([dmod012[:nb, 0], dmod012[:nb, 1], dmod012[:nb, 2], dmod34[:nb, 0], dmod34[:nb, 1],
                             dgate5[:, 0], dmod678[:, 0], dmod678[:, 1], dmod678[:, 2]], axis=1)
    g_ctx = jnp.concatenate([dmod012[nb:, 0], dmod012[nb:, 1], dmod012[nb:, 2], dmod34[nb:, 0], dmod34[nb:, 1],
                             zrow, zrow, zrow, zrow], axis=1)
    g_loc = jnp.concatenate([g_lat, g_ctx, jnp.zeros((7 - nb, NMOD * D), F32)], axis=0)

    got_w2, g_all = _scatter_sibling([g_w2], "scatter_sibling_w2", gather=g_loc)
    g_all = g_all.reshape(NDEV * 8, NMOD * D)
    g_cols = lax.dynamic_slice_in_dim(g_all, me * ncol, ncol, axis=1)
    g_w_ada, pc_ctx, g_b_ada = _ada_bwd(a_raw, c_ctx.reshape(D, 1), g_all, g_cols, w_ada[0], nb)
    part_w2 = _add_sibling(g_w2, got_w2, 176, "add_sibling_w2")

    def prow(a):
        a = a.reshape(1, -1)
        return jnp.concatenate([a, jnp.zeros((1, D - a.shape[1]), F32)], axis=1)

    g_qn = dwq.reshape(H, LANE)[:, :DH].sum(0)
    g_kn = dwk.reshape(H, LANE)[:, :DH].sum(0)
    spack = jnp.concatenate([
        dnorm1, dnorm2, dnorm3, prow(dqa), prow(dkva), prow(g_qn), prow(g_kn), prow(dwv),
        prow(dbs[:, :G].T), prow(pc_ctx), prow(lsum[0:1]), jnp.zeros((5, D), F32), dws.reshape(CH, D)],
        axis=0)
    recv_w2, small_all = _scatter_chips([part_w2], "scatter_chips", gather=spack)
    ssum = _sum_slots(small_all, 144, "sum_small")
    loss = ssum[10, 0] * (0.5 / D)
    gsum2 = _sum_direct(g_ffn2, recv_ffn2, 176, "sum_grads_ffn2")
    msum = _sum_direct(gmisc, recv_misc, 368, "sum_grads_misc")

    transposed = ("ffn1_w1", "ffn1_w3", "ffn2_w1", "ffn2_w3", "w_in", "w_uq")
    g_big = {
        "ffn1_w1": _sum_direct(g_w1, recv_w1, 176, "sum_grads_w1"),
        "ffn1_w3": _sum_direct(g_w3, recv_w3, 176, "sum_grads_w3"),
        "ffn1_w2": _sum_chips(part_w2, recv_w2, 176, "sum_grads_w2"),
        "ffn2_w1": gsum2[0:fsh], "ffn2_w3": gsum2[fsh:2 * fsh], "ffn2_w2": gsum2[2 * fsh:3 * fsh],
        "w_in": msum[0:180], "w_out": msum[192:320],
        "w_uq": msum[320:344].reshape(DH, QL), "w_ukv": msum[352:368].reshape(DN + DV, KVL).T,
        "w_ada": g_w_ada,
    }

    big_in = {
        "w_ada": (w_ada, m_w_ada, v_w_ada), "ffn1_w1": (ffn1_w1, m_ffn1_w1, v_ffn1_w1),
        "ffn1_w3": (ffn1_w3, m_ffn1_w3, v_ffn1_w3), "ffn1_w2": (ffn1_w2, m_ffn1_w2, v_ffn1_w2),
        "w_in": (w_in, m_w_in, v_w_in), "w_uq": (w_uq, m_w_uq, v_w_uq), "w_ukv": (w_ukv, m_w_ukv, v_w_ukv),
        "w_out": (w_out, m_w_out, v_w_out), "ffn2_w1": (ffn2_w1, m_ffn2_w1, v_ffn2_w1),
        "ffn2_w3": (ffn2_w3, m_ffn2_w3, v_ffn2_w3), "ffn2_w2": (ffn2_w2, m_ffn2_w2, v_ffn2_w2),
    }
    res = {}
    for nm, (w, m, v_) in big_in.items():
        g = g_big[nm]
        if nm in transposed:
            d_, m_, v2_ = _adamw(w[0].T, g, m[0].T, v_[0].T, "adamw_" + nm)
            res[nm] = tuple(a.T[None] for a in (g, d_, m_, v2_))
        else:
            d_, m_, v2_ = _adamw(w[0], g, m[0], v_[0], "adamw_" + nm)
            res[nm] = tuple(a[None] for a in (g, d_, m_, v2_))

    small_in = [
        ("c_ctx", c_ctx, m_c_ctx, v_c_ctx, ssum[9:10], (1, D)),
        ("b_ada", b_ada, m_b_ada, v_b_ada, g_b_ada, (1, NMOD * D)),
        ("norm1_w", norm1_w, m_norm1_w, v_norm1_w, ssum[0:1], (1, D)),
        ("norm2_w", norm2_w, m_norm2_w, v_norm2_w, ssum[1:2], (1, D)),
        ("norm3_w", norm3_w, m_norm3_w, v_norm3_w, ssum[2:3], (1, D)),
        ("q_a_norm_w", q_a_norm_w, m_q_a_norm_w, v_q_a_norm_w, ssum[3:4, :QL], (1, QL)),
        ("kv_a_norm_w", kv_a_norm_w, m_kv_a_norm_w, v_kv_a_norm_w, ssum[4:5, :KVL], (1, KVL)),
        ("q_norm_w", q_norm_w, m_q_norm_w, v_q_norm_w, ssum[5:6, :DH], (1, DH)),
        ("k_norm_w", k_norm_w, m_k_norm_w, v_k_norm_w, ssum[6:7, :DH], (1, DH)),
        ("v_norm_w", v_norm_w, m_v_norm_w, v_v_norm_w, ssum[7:8, :G * GD], (G, GD)),
        ("b_s", b_s, m_b_s, v_b_s, ssum[8:9], (G, CH)),
        ("w_s", w_s, m_w_s, v_w_s, ssum[16:144], (G * CH, CH)),
    ]
    small_out = _adamw_small(
        [(w.reshape(sh), g.reshape(sh), m.reshape(sh), v_.reshape(sh)) for _, w, m, v_, g, sh in small_in])
    for (nm, w, *_), outs in zip(small_in, small_out):
        res[nm] = tuple(a.reshape(w.shape) for a in outs)

    order = ["c_ctx", "w_ada", "b_ada", "norm1_w", "ffn1_w1", "ffn1_w3", "ffn1_w2", "norm2_w", "w_in", "q_a_norm_w",
             "w_uq", "kv_a_norm_w", "w_ukv", "q_norm_w", "k_norm_w", "v_norm_w", "w_s", "b_s", "w_out", "norm3_w",
             "ffn2_w1", "ffn2_w3", "ffn2_w2"]
    return (loss, grad_x, *[res[n][0] for n in order], *[res[n][1] for n in order],
            *[res[n][2] for n in order], *[res[n][3] for n in order])
```

```python
import numpy as np
import jax
import jax.numpy as jnp
from jax import lax
from jax.experimental import pallas as pl
from jax.experimental.pallas import tpu as pltpu

F32 = jnp.float32
BF16 = jnp.bfloat16

D = 1024
FF = 2816
FC = 256
H = 8
DN, DR, DV = 64, 32, 64
DH = DN + DR
QL, KVL = 256, 128
G, GD, CH = 8, 64, 128
NMOD = 9
EPS = 1e-6
GRID_W = 64
ROPE_BASE = 10000.0
NDEV = 8
LANE = 128
HP = H * LANE
IN_COLS = 1440
WIN_ROWS = 1536
KPE_LO = 1408 + DN
MIB = 1 << 20

ADAM_LR, ADAM_B1, ADAM_B2, ADAM_EPS, ADAM_WD, ADAM_STEP = 0.001, 0.9, 0.999, 1e-08, 0.01, 10

MESH = pl.DeviceIdType.MESH
ANY = pl.BlockSpec(memory_space=pl.ANY)
VMEM = pl.BlockSpec(memory_space=pltpu.VMEM)


def _mm(a, b):
    return jnp.dot(a, b, preferred_element_type=F32)


def _mm_nt(a, b):
    return lax.dot_general(a, b, (((1,), (1,)), ((), ())), preferred_element_type=F32)


def _mm_tn(a, b):
    return lax.dot_general(a, b, (((0,), (0,)), ((), ())), preferred_element_type=F32)


def _dot_hl(x, m):
    hi = x.astype(BF16)
    lo = (x - hi.astype(F32)).astype(BF16)
    return _mm(hi, m) + _mm(lo, m)


def _sigmoid(a):
    return 1.0 / (1.0 + jnp.exp(-a))


_G0 = 0.7978845608028654
_G1 = 0.044715


def _gelu(x):
    return 0.5 * x * (1.0 + jnp.tanh(_G0 * (x + _G1 * (x * x * x))))


def _gelu_grad(x):
    th = jnp.tanh(_G0 * (x + _G1 * (x * x * x)))
    return 0.5 * (1.0 + th) + 0.5 * x * (1.0 - th * th) * (_G0 * (1.0 + 3.0 * _G1 * x * x))


def _rowsum(y):
    return jnp.sum(y, axis=0, keepdims=True)


def _rms(x):
    return lax.rsqrt(jnp.mean(x * x, axis=-1, keepdims=True) + EPS)


def _pcall(body, *, name, out_shape, in_specs, out_specs, grid=None, scratch=(), vmem_mb=32, aliases=None):
    kw = {}
    if grid is not None:
        kw["grid"] = grid
        sem = ("arbitrary",) * len(grid)
    else:
        sem = None
    if aliases:
        kw["input_output_aliases"] = aliases
    return pl.pallas_call(
        body, name=name, out_shape=out_shape, in_specs=in_specs, out_specs=out_specs,
        scratch_shapes=list(scratch),
        compiler_params=pltpu.CompilerParams(dimension_semantics=sem, vmem_limit_bytes=vmem_mb * MIB),
        **kw)


def _const(shape):
    nd = len(shape)
    return pl.BlockSpec(shape, lambda *_: (0,) * nd)


def _sds(shape, dt):
    return jax.ShapeDtypeStruct(shape, dt)


def _consts():
    seg_h = np.zeros((HP, LANE), np.float32)
    seg_h[np.arange(HP), np.arange(HP) // LANE] = 1.0
    seg_g = np.zeros((G * GD, LANE), np.float32)
    seg_g[np.arange(G * GD), np.arange(G * GD) // GD] = 1.0
    rot = np.zeros((LANE, LANE), np.float32)
    for base in (DN, DN + 16):
        for j in range(8):
            rot[base + j + 8, base + j] = -1.0
            rot[base + j, base + j + 8] = 1.0
    rot2 = np.zeros((2 * LANE, 2 * LANE), np.float32)
    rot2[:LANE, :LANE] = rot
    rot2[LANE:, LANE:] = rot
    twice = lambda m: np.concatenate([m, m], axis=0)
    c = dict(seg_h=seg_h, seg_ht=twice(seg_h.T), seg_g=seg_g, seg_gt=twice(seg_g.T), rot=rot2, rot_t=rot2.T)
    return {k: jnp.asarray(v, BF16) for k, v in c.items()}


_GATHER_SEMS = [pltpu.SemaphoreType.DMA((7,)), pltpu.SemaphoreType.DMA((7,)), pltpu.SemaphoreType.DMA(())]


def _gather_phases(x_ref, out_ref, send_sems, recv_sems, local_sem):
    mx, my, mc = lax.axis_index("x"), lax.axis_index("y"), lax.axis_index("c")
    me, sibling = (mx, my, mc), (mx, my, 1 - mc)
    chips = [(1 - mx, my), (mx, 1 - my), (1 - mx, 1 - my)]

    def blk(px, py, pc):
        return out_ref.at[4 * px + 2 * py + pc]

    def copy(k, block, to, src=None):
        return pltpu.make_async_remote_copy(
            src_ref=blk(*block) if src is None else src, dst_ref=blk(*block),
            send_sem=send_sems.at[k], recv_sem=recv_sems.at[k], device_id=to, device_id_type=MESH)

    mine = pltpu.make_async_copy(x_ref, blk(*me), local_sem)
    first = [copy(0, me, sibling, src=x_ref)]
    first += [copy(1 + j, me, (*chip, mc), src=x_ref) for j, chip in enumerate(chips)]
    passed = [copy(4 + j, (*chip, mc), sibling) for j, chip in enumerate(chips)]

    def start():
        mine.start()
        for cp in first:
            cp.start()

    def forward():
        for j, chip in enumerate(chips):
            copy(1 + j, (*chip, mc), me).wait_recv()
            passed[j].start()

    def finish():
        copy(0, sibling, me).wait_recv()
        for j, chip in enumerate(chips):
            copy(4 + j, (*chip, 1 - mc), me).wait_recv()
        for cp in first + passed:
            cp.wait_send()
        mine.wait()

    return start, forward, finish


def _chip_sends(p_ref, out_ref, send_sems, recv_sems):
    mx, my, mc = lax.axis_index("x"), lax.axis_index("y"), lax.axis_index("c")
    peers = [(1 - mx, my), (mx, 1 - my), (1 - mx, 1 - my)]
    return [pltpu.make_async_remote_copy(
        src_ref=p_ref.at[2 * px + py], dst_ref=out_ref.at[j], send_sem=send_sems.at[j], recv_sem=recv_sems.at[j],
        device_id=(px, py, mc), device_id_type=MESH) for j, (px, py) in enumerate(peers)]


def _with_gather(copies_of, n, shapes, sems, gather, name, args):
    ns = len(sems)

    def body(*refs):
        ng = 1 if gather is not None else 0
        ins, outs = refs[:n], refs[n + ng:2 * n + ng]
        copies = copies_of(ins, outs, refs[2 * n + 2 * ng:2 * n + 2 * ng + ns])
        if ng:
            start, forward, finish = _gather_phases(refs[n], refs[2 * n + 1], *refs[2 * n + 2 + ns:])
            start()
        for cp in copies:
            cp.start()
        if ng:
            forward()
        for cp in copies:
            cp.wait_recv()
        for cp in copies:
            cp.wait_send()
        if ng:
            finish()

    in_specs, out_shape, scratch = [ANY] * n, list(shapes), list(sems)
    if gather is not None:
        in_specs.append(ANY)
        args = list(args) + [gather]
        out_shape.append(_sds((NDEV,) + gather.shape, gather.dtype))
        scratch += _GATHER_SEMS
    return pl.pallas_call(body, name=name, out_shape=tuple(out_shape), in_specs=in_specs,
                          out_specs=(ANY,) * len(out_shape), scratch_shapes=scratch)(*args)


def _scatter_sibling(xs, name, gather=None):
    n = len(xs)

    def copies_of(x_refs, got_refs, sems):
        send_sems, recv_sems = sems
        mx, my, mc = lax.axis_index("x"), lax.axis_index("y"), lax.axis_index("c")
        return [pltpu.make_async_remote_copy(
            src_ref=x_refs[i].at[2 * j + 1 - mc], dst_ref=got_refs[i].at[j],
            send_sem=send_sems.at[4 * i + j], recv_sem=recv_sems.at[4 * i + j],
            device_id=(mx, my, 1 - mc), device_id_type=MESH) for i in range(n) for j in range(4)]

    shapes = tuple(_sds((4,) + x.shape[1:], x.dtype) for x in xs)
    return _with_gather(copies_of, n, shapes, [pltpu.SemaphoreType.DMA((4 * n,))] * 2, gather, name, xs)


def _scatter_chips(ps, name, gather=None):
    n = len(ps)

    def copies_of(p_refs, out_refs, sems):
        sends = []
        for i in range(n):
            sends += _chip_sends(p_refs[i], out_refs[i], sems[2 * i], sems[2 * i + 1])
        return sends

    shapes = tuple(_sds((3,) + p.shape[1:], p.dtype) for p in ps)
    return _with_gather(copies_of, n, shapes, [pltpu.SemaphoreType.DMA((3,))] * (2 * n), gather, name, ps)


def _add_sibling(x, got, tr, name):
    _, r, c = x.shape

    def body(x_ref, g_ref, o_ref):
        mc = lax.axis_index("c")
        for j in range(4):
            mine = jnp.where(mc == 0, x_ref[2 * j].astype(F32), x_ref[2 * j + 1].astype(F32))
            o_ref[j] = (mine + g_ref[j].astype(F32)).astype(o_ref.dtype)

    return _pcall(body, name=name, grid=(r // tr,), out_shape=_sds(got.shape, got.dtype),
                  in_specs=[pl.BlockSpec((NDEV, tr, c), lambda t: (0, t, 0)), pl.BlockSpec((4, tr, c), lambda t: (0, t, 0))],
                  out_specs=pl.BlockSpec((4, tr, c), lambda t: (0, t, 0)))(x, got)


def _sum_chips(part, recv, tr, name):
    _, r, c = part.shape

    def body(p_ref, r_ref, o_ref):
        slot = 2 * lax.axis_index("x") + lax.axis_index("y")
        acc = p_ref[0].astype(F32)
        for j in range(1, 4):
            acc = jnp.where(slot == j, p_ref[j].astype(F32), acc)
        for j in range(3):
            acc = acc + r_ref[j].astype(F32)
        o_ref[...] = acc

    return _pcall(body, name=name, grid=(r // tr,), out_shape=_sds((r, c), F32),
                  in_specs=[pl.BlockSpec((4, tr, c), lambda t: (0, t, 0)), pl.BlockSpec((3, tr, c), lambda t: (0, t, 0))],
                  out_specs=pl.BlockSpec((tr, c), lambda t: (t, 0)))(part, recv)


def _sum_slots(x, tr, name):
    n, r, c = x.shape

    def body(x_ref, o_ref):
        acc = x_ref[0].astype(F32)
        for s in range(1, n):
            acc = acc + x_ref[s].astype(F32)
        o_ref[...] = acc

    return _pcall(body, name=name, grid=(r // tr,), out_shape=_sds((r, c), F32),
                  in_specs=[pl.BlockSpec((n, tr, c), lambda t: (0, t, 0))],
                  out_specs=pl.BlockSpec((tr, c), lambda t: (t, 0)))(x)


def _ada_front(a_loc, w_loc, b_loc, wpack):
    ncol = w_loc.shape[1]
    nrow = NDEV * a_loc.shape[0]

    def body(a_ref, w_ref, b_ref, wp_ref, araw_ref, mloc_ref, mall_ref, wall_ref,
             a_vm, w_vm, m_vm, lsem, *sems):
        a_start, a_forward, a_finish = _gather_phases(a_ref, araw_ref, *sems[0:3])
        m_start, m_forward, m_finish = _gather_phases(mloc_ref, mall_ref, *sems[3:6])
        w_start, w_forward, w_finish = _gather_phases(wp_ref, wall_ref, *sems[6:9])
        w_in = pltpu.make_async_copy(w_ref, w_vm, lsem.at[0])
        w_in.start()
        a_start()
        w_start()
        a_forward()
        a_finish()
        a_in = pltpu.make_async_copy(araw_ref, a_vm, lsem.at[1])
        a_in.start()
        a_in.wait()
        w_in.wait()
        a = a_vm[...].reshape(nrow, D)
        act = (a * _sigmoid(a)).astype(BF16)
        m_vm[...] = _mm(act, w_vm[...].astype(BF16)) + b_ref[...]
        m_out = pltpu.make_async_copy(m_vm, mloc_ref, lsem.at[2])
        m_out.start()
        m_out.wait()
        m_start()
        m_forward()
        m_finish()
        w_forward()
        w_finish()

    return pl.pallas_call(
        body, name="ada_front",
        out_shape=(_sds((NDEV,) + a_loc.shape, F32), _sds((nrow, ncol), F32), _sds((NDEV, nrow, ncol), F32),
                   _sds((NDEV,) + wpack.shape, wpack.dtype)),
        in_specs=[ANY, ANY, VMEM, ANY], out_specs=(ANY, ANY, ANY, ANY),
        scratch_shapes=[pltpu.VMEM((NDEV,) + a_loc.shape, F32), pltpu.VMEM(w_loc.shape, F32),
                        pltpu.VMEM((nrow, ncol), F32), pltpu.SemaphoreType.DMA((3,))] + _GATHER_SEMS * 3,
        compiler_params=pltpu.CompilerParams(vmem_limit_bytes=32 * MIB),
    )(a_loc, w_loc, b_loc, wpack)


def _ada_bwd(a_raw, cctx_col, g_all, g_cols, w_loc, nb):
    nrow = a_raw.shape[0]
    ncol = w_loc.shape[1]

    def body(a_ref, cc_ref, gall_ref, g_ref, w_ref, dw_ref, pc_ref, gb_ref):
        a = a_ref[...]
        rowid = lax.broadcasted_iota(jnp.int32, (nrow, 1), 0) % 8
        act = jnp.where(rowid < nb, a * _sigmoid(a), 0.0).astype(BF16)
        g = g_ref[...]
        gc = _rowsum(jnp.where(rowid == nb, g, 0.0))
        cc = cc_ref[...]
        dw_ref[...] = _mm_tn(act, g.astype(BF16)) + (cc * _sigmoid(cc)) * gc
        pc_ref[...] = jnp.sum(w_ref[...] * gc, axis=1, keepdims=True)
        gb_ref[...] = _rowsum(gall_ref[...])

    return _pcall(body, name="ada_bwd",
                  out_shape=(_sds((D, ncol), F32), _sds((D, 1), F32), _sds((1, g_all.shape[1]), F32)),
                  in_specs=[VMEM] * 5, out_specs=(VMEM,) * 3, vmem_mb=48)(a_raw, cctx_col, g_all, g_cols, w_loc)


def _mod_spec(k, tpe, nrows):
    return pl.BlockSpec((1, k, D), lambda t: (jnp.minimum(t // tpe, nrows - 1), 0, 0))


def _load_ffn_weights(wall_ref, first, bufs, sems):
    fsh = FF // NDEV
    cps = []
    for j, buf in enumerate(bufs):
        for d in range(NDEV):
            cps.append(pltpu.make_async_copy(wall_ref.at[d, pl.ds((first + j) * fsh, fsh)],
                                             buf.at[pl.ds(d * fsh, fsh)], sems.at[j * NDEV + d]))
    for cp in cps:
        cp.start()
    for cp in cps:
        cp.wait()


def _token_specs(xs, tm, n_lat):
    specs = [pl.BlockSpec((tm, D), lambda t: (jnp.minimum(t, n_lat - 1), 0))]
    if len(xs) == 2:
        specs.append(pl.BlockSpec((tm, D), lambda t: (jnp.maximum(t - n_lat, 0), 0)))
    return specs


def _ffn_fwd(xs, mod3, norm_w, wall, first, *, tm, n_tiles, tpe, n_lat, name, target=None, gather=None):
    nrows = mod3.shape[0]
    r = n_tiles * tm
    nx = len(xs)
    with_loss = target is not None
    with_gather = gather is not None
    fwd_step = max(2 * n_tiles // 3, 1)

    def body(*refs):
        x_refs = refs[:nx]
        pos = nx
        if with_loss:
            tgt_ref = refs[pos]
            pos += 1
        mod_ref, nw_ref, wall_ref = refs[pos:pos + 3]
        pos += 3
        if with_gather:
            gin_ref = refs[pos]
            pos += 1
        xo_ref, a_ref, b_ref, o_ref = refs[pos:pos + 4]
        pos += 4
        if with_loss:
            ls_ref = refs[pos]
            pos += 1
        if with_gather:
            gout_ref = refs[pos]
            pos += 1
        w1_ref, w3_ref, w2_ref, wsem, acc_ref = refs[pos:pos + 5]
        t = pl.program_id(0)
        if with_gather:
            g_start, g_forward, g_finish = _gather_phases(gin_ref, gout_ref, *refs[pos + 5:])

        @pl.when(t == 0)
        def _():
            if with_gather:
                g_start()
            _load_ffn_weights(wall_ref, first, (w1_ref, w3_ref, w2_ref), wsem)
            if with_loss:
                ls_ref[...] = jnp.zeros_like(ls_ref)

        if with_gather:
            @pl.when(t == fwd_step)
            def _():
                g_forward()

            @pl.when(t == n_tiles - 1)
            def _():
                g_finish()

        x = x_refs[0][...]
        if nx == 2:
            x = jnp.where(t < n_lat, x, x_refs[1][...])
        n = x * _rms(x) * nw_ref[...]
        shift, scale, gate = mod_ref[0, 0:1, :], mod_ref[0, 1:2, :], mod_ref[0, 2:3, :]
        h = (n * (1.0 + scale) + shift).astype(BF16)
        nch = FF // FC
        o = None
        for lo_c, hi_c in ((0, nch // 2), (nch // 2, nch)):
            for j in range(lo_c, hi_c):
                sl = slice(j * FC, (j + 1) * FC)
                a = _mm_nt(h, w1_ref[sl, :])
                b = _mm_nt(h, w3_ref[sl, :])
                a_ref[:, sl] = a.astype(BF16)
                b_ref[:, sl] = b.astype(BF16)
                acc_ref[:, sl] = (a * _sigmoid(a) * b).astype(BF16)
            gs = slice(lo_c * FC, hi_c * FC)
            part = _mm(acc_ref[:, gs], w2_ref[gs, :])
            o = part if o is None else o + part
        o_ref[...] = o.astype(BF16)
        out = x + (0.5 * gate) * o
        if with_loss:
            d = out - tgt_ref[...]
            xo_ref[...] = d * (1.0 / D)
            ls_ref[...] += jnp.sum(d * d)
        else:
            xo_ref[...] = out

    row = lambda cols: pl.BlockSpec((tm, cols), lambda t: (t, 0))
    in_specs = _token_specs(xs, tm, n_lat) + ([row(D)] if with_loss else []) + [
        _mod_spec(3, tpe, nrows), _const((1, D)), ANY]
    out_shape = [_sds((r, D), F32), _sds((r, FF), BF16), _sds((r, FF), BF16), _sds((r, D), BF16)]
    out_specs = [row(D), row(FF), row(FF), row(D)]
    scratch = [pltpu.VMEM((FF, D), BF16)] * 3 + [pltpu.SemaphoreType.DMA((3 * NDEV,)), pltpu.VMEM((tm, FF), BF16)]
    if with_loss:
        out_shape.append(_sds((8, LANE), F32))
        out_specs.append(_const((8, LANE)))
    args = list(xs) + ([target] if with_loss else []) + [mod3, norm_w, wall]
    if with_gather:
        assert n_tiles >= 2
        in_specs.append(ANY)
        args.append(gather)
        out_shape.append(_sds((NDEV,) + gather.shape, gather.dtype))
        out_specs.append(ANY)
        scratch += _GATHER_SEMS
    return _pcall(
        body, name=name, grid=(n_tiles,), out_shape=tuple(out_shape), in_specs=in_specs, out_specs=tuple(out_specs),
        scratch=scratch, vmem_mb=56)(*args)


def _ffn_bwd_dx(dout, xs, a, b, o, mod3, norm_w, wall, first, *, tm, n_tiles, tpe, n_lat, name):
    nrows = mod3.shape[0]
    r = n_tiles * tm
    nx = len(xs)

    def body(*refs):
        dout_ref = refs[0]
        x_refs = refs[1:1 + nx]
        (a_ref, b_ref, o_ref, mod_ref, nw_ref, wall_ref,
         dx_ref, da_ref, db_ref, g_ref, do_ref, h_ref, dmod_ref, dnw_ref,
         w1_ref, w3_ref, w2_ref, wsem) = refs[1 + nx:]
        t = pl.program_id(0)

        @pl.when(t == 0)
        def _():
            _load_ffn_weights(wall_ref, first, (w1_ref, w3_ref, w2_ref), wsem)
            dnw_ref[...] = jnp.zeros_like(dnw_ref)

        @pl.when(jnp.where(t < n_lat, t % tpe == 0, t == n_lat))
        def _():
            dmod_ref[...] = jnp.zeros_like(dmod_ref)

        x = x_refs[0][...]
        if nx == 2:
            x = jnp.where(t < n_lat, x, x_refs[1][...])
        dout = dout_ref[...]
        shift, scale, gate = mod_ref[0, 0:1, :], mod_ref[0, 1:2, :], mod_ref[0, 2:3, :]
        d_o = ((0.5 * gate) * dout).astype(BF16)
        do_ref[...] = d_o
        nch = FF // FC
        groups = ((0, 4), (4, 8), (8, nch))
        dh = None
        for lo_c, hi_c in groups:
            for j in range(lo_c, hi_c):
                sl = slice(j * FC, (j + 1) * FC)
                av = a_ref[:, sl].astype(F32)
                bv = b_ref[:, sl].astype(F32)
                dg = _mm_nt(d_o, w2_ref[sl, :])
                sig = _sigmoid(av)
                sa = av * sig
                g_ref[:, sl] = (sa * bv).astype(BF16)
                da_ref[:, sl] = (dg * bv * (sig * (1.0 + av * (1.0 - sig)))).astype(BF16)
                db_ref[:, sl] = (dg * sa).astype(BF16)
            gs = slice(lo_c * FC, hi_c * FC)
            part = _mm(da_ref[:, gs], w1_ref[gs, :]) + _mm(db_ref[:, gs], w3_ref[gs, :])
            dh = part if dh is None else dh + part
        rr = _rms(x)
        xh = x * rr
        nw = nw_ref[...]
        n = xh * nw
        h_ref[...] = (n * (1.0 + scale) + shift).astype(BF16)
        dgate = _rowsum(0.5 * o_ref[...].astype(F32) * dout)
        dn = dh * (1.0 + scale)
        dxh = dn * nw
        dmod_ref[0, 0:1, :] += _rowsum(dh)
        dmod_ref[0, 1:2, :] += _rowsum(dh * n)
        dmod_ref[0, 2:3, :] += dgate
        dnw_ref[...] += _rowsum(dn * xh)
        dx = dout + rr * (dxh - xh * jnp.mean(dxh * xh, axis=-1, keepdims=True))
        if n_tiles == n_lat:
            dx_ref[...] = dx
        else:
            @pl.when(t < n_lat)
            def _():
                dx_ref[...] = dx

    row = lambda cols: pl.BlockSpec((tm, cols), lambda t: (t, 0))
    lat = pl.BlockSpec((tm, D), lambda t: (jnp.minimum(t, n_lat - 1), 0))
    out_shape = [_sds((n_lat * tm, D), F32), _sds((r, FF), BF16), _sds((r, FF), BF16), _sds((r, FF), BF16),
                 _sds((r, D), BF16), _sds((r, D), BF16), _sds((nrows, 3, D), F32), _sds((1, D), F32)]
    in_specs = [row(D)] + _token_specs(xs, tm, n_lat) + [row(FF), row(FF), row(D), _mod_spec(3, tpe, nrows),
                                                          _const((1, D)), ANY]
    out_specs = [lat, row(FF), row(FF), row(FF), row(D), row(D), _mod_spec(3, tpe, nrows), _const((1, D))]
    scratch = [pltpu.VMEM((FF, D), BF16)] * 3 + [pltpu.SemaphoreType.DMA((3 * NDEV,))]
    args = [dout, *xs, a, b, o, mod3, norm_w, wall]
    return _pcall(body, name=name, grid=(n_tiles,), out_shape=tuple(out_shape), in_specs=in_specs,
                  out_specs=tuple(out_specs), scratch=scratch, vmem_mb=60)(*args)


def _ffn_bwd_dw(h, d_o, da, db, g, *, tr, name):
    r = h.shape[0]
    fh = FF // 2
    fsh = FF // NDEV
    nk = r // tr

    def body(h_ref, do_ref, da_ref, db_ref, g_ref, out_ref, acc1, acc3, acc2):
        k = pl.program_id(1)

        @pl.when(k == 0)
        def _():
            acc1[...] = jnp.zeros_like(acc1)
            acc3[...] = jnp.zeros_like(acc3)
            acc2[...] = jnp.zeros_like(acc2)

        hv = h_ref[...]
        acc1[...] += _mm_tn(da_ref[...], hv)
        acc3[...] += _mm_tn(db_ref[...], hv)
        acc2[...] += _mm_tn(g_ref[...], do_ref[...])

        @pl.when(k == nk - 1)
        def _():
            for i, acc in enumerate((acc1, acc3, acc2)):
                out_ref[:, i * fsh:(i + 1) * fsh, :] = acc[...].reshape(NDEV // 2, fsh, D).astype(BF16)

    rowd = pl.BlockSpec((tr, D), lambda f, k: (k, 0))
    rowf = pl.BlockSpec((tr, fh), lambda f, k: (k, f))
    return _pcall(
        body, name=name, grid=(2, nk), out_shape=_sds((NDEV, 3 * fsh, D), BF16),
        in_specs=[rowd, rowd, rowf, rowf, rowf],
        out_specs=pl.BlockSpec((NDEV // 2, 3 * fsh, D), lambda f, k: (f, 0, 0)),
        scratch=[pltpu.VMEM((fh, D), F32)] * 3, vmem_mb=56)(h, d_o, da, db, g)


def _direct_sends(x_ref, out_ref, send_sems, recv_sems):
    mx, my, mc = lax.axis_index("x"), lax.axis_index("y"), lax.axis_index("c")
    sends = []
    for k in range(1, NDEV):
        px = 1 - mx if (k & 4) else mx
        py = 1 - my if (k & 2) else my
        pc = 1 - mc if (k & 1) else mc
        sends.append(pltpu.make_async_remote_copy(
            src_ref=x_ref.at[4 * px + 2 * py + pc], dst_ref=out_ref.at[k - 1],
            send_sem=send_sems.at[k - 1], recv_sem=recv_sems.at[k - 1], device_id=(px, py, pc), device_id_type=MESH))
    return sends


def _sum_direct(x, recv, tr, name):
    _, r, c = x.shape

    def body(x_ref, r_ref, o_ref):
        me = 4 * lax.axis_index("x") + 2 * lax.axis_index("y") + lax.axis_index("c")
        acc = x_ref[0].astype(F32)
        for j in range(1, NDEV):
            acc = jnp.where(me == j, x_ref[j].astype(F32), acc)
        for j in range(NDEV - 1):
            acc = acc + r_ref[j].astype(F32)
        o_ref[...] = acc

    return _pcall(body, name=name, grid=(r // tr,), out_shape=_sds((r, c), F32),
                  in_specs=[pl.BlockSpec((NDEV, tr, c), lambda t: (0, t, 0)),
                            pl.BlockSpec((NDEV - 1, tr, c), lambda t: (0, t, 0))],
                  out_specs=pl.BlockSpec((tr, c), lambda t: (t, 0)))(x, recv)


def _exchange_behind(x_ref, recv_ref, send_sems, recv_sems, first, last):
    sends = _direct_sends(x_ref, recv_ref, send_sems, recv_sems)

    @pl.when(first)
    def _():
        for cp in sends:
            cp.start()

    @pl.when(last)
    def _():
        for cp in sends:
            cp.wait_recv()
        for cp in sends:
            cp.wait_send()


def _ffn_bwd_dw_one(lhs, rhs, *, tr, name, part=None):
    r = lhs.shape[0]
    fsh = FF // NDEV
    nk = r // tr
    fused = part is not None
    nslot = NDEV - 1

    def body(*refs):
        if fused:
            lhs_ref, rhs_ref, part_ref, out_ref, recv_ref, acc, send_sems, recv_sems = refs
        else:
            lhs_ref, rhs_ref, out_ref, acc = refs
        k = pl.program_id(0)
        if fused:
            _exchange_behind(part_ref, recv_ref, send_sems, recv_sems, k == 0, k == nk - 1)

        @pl.when(k == 0)
        def _():
            acc[...] = jnp.zeros_like(acc)

        acc[...] += _mm_tn(lhs_ref[...], rhs_ref[...])

        @pl.when(k == nk - 1)
        def _():
            out_ref[...] = acc[...].reshape(NDEV, fsh, D).astype(BF16)

    in_specs = [pl.BlockSpec((tr, FF), lambda k: (k, 0)), pl.BlockSpec((tr, D), lambda k: (k, 0))]
    out_shape = [_sds((NDEV, fsh, D), BF16)]
    out_specs = [_const((NDEV, fsh, D))]
    scratch = [pltpu.VMEM((FF, D), F32)]
    args = [lhs, rhs]
    if fused:
        in_specs.append(ANY)
        args.append(part)
        out_shape.append(_sds((nslot,) + part.shape[1:], part.dtype))
        out_specs.append(ANY)
        scratch += [pltpu.SemaphoreType.DMA((nslot,))] * 2
    res = _pcall(body, name=name, grid=(nk,), out_shape=tuple(out_shape), in_specs=in_specs,
                 out_specs=tuple(out_specs), scratch=scratch, vmem_mb=48)(*args)
    return res if fused else res[0]


_PIECES =((0, 128), (128, 384), (384, 896), (896, 1408), (1408, 1536))


def _proj_fwd(x1, mod2, norm_w, wint, *, tm, n_tiles, tpe, name="proj_fwd"):
    nrows = mod2.shape[0]
    r = n_tiles * tm

    def body(x_ref, mod_ref, nw_ref, w_ref, ckv_ref, q_ref, u_ref, v_ref, kpe_ref):
        x = x_ref[...]
        n = x * _rms(x) * nw_ref[...]
        h = (n * (1.0 + mod_ref[0, 1:2, :]) + mod_ref[0, 0:1, :]).astype(BF16)
        for (lo, hi), ref in zip(_PIECES, (ckv_ref, q_ref, u_ref, v_ref, kpe_ref)):
            ref[...] = _mm_nt(h, w_ref[lo:hi, :])

    row = lambda cols: pl.BlockSpec((tm, cols), lambda t: (t, 0))
    widths = [hi - lo for lo, hi in _PIECES]
    return _pcall(
        body, name=name, grid=(n_tiles,),
        out_shape=tuple(_sds((r, w), F32) for w in widths),
        in_specs=[row(D), _mod_spec(2, tpe, nrows), _const((1, D)), _const((WIN_ROWS, D))],
        out_specs=tuple(row(w) for w in widths), vmem_mb=40)(x1, mod2, norm_w, wint)


def _proj_bwd(dckv, dkpe, dq, du, dv, dx2, x1, mod2, norm_w, wint, *, tm, n_tiles, tpe, n_lat, name="proj_bwd"):
    nrows = mod2.shape[0]
    r = n_tiles * tm

    def body(dckv_ref, dkpe_ref, dq_ref, du_ref, dv_ref, dx2_ref, x_ref, mod_ref, nw_ref, w_ref,
             dx_ref, dw_ref, dmod_ref, dnw_ref):
        t = pl.program_id(0)
        is_lat = t < n_lat

        @pl.when(t == 0)
        def _():
            dw_ref[...] = jnp.zeros_like(dw_ref)
            dnw_ref[...] = jnp.zeros_like(dnw_ref)

        @pl.when(jnp.where(is_lat, t % tpe == 0, t == n_lat))
        def _():
            dmod_ref[...] = jnp.zeros_like(dmod_ref)

        x = x_ref[...]
        rr = _rms(x)
        xh = x * rr
        nw = nw_ref[...]
        n = xh * nw
        scale = mod_ref[0, 1:2, :]
        h = (n * (1.0 + scale) + mod_ref[0, 0:1, :]).astype(BF16)
        zero = jnp.zeros((), BF16)
        pieces = (dckv_ref[...], jnp.where(is_lat, dq_ref[...], zero), jnp.where(is_lat, du_ref[...], zero),
                  jnp.where(is_lat, dv_ref[...], zero), dkpe_ref[...])
        dproj = jnp.concatenate(pieces, axis=-1)
        dh = _mm(dproj, w_ref[...])
        dn = dh * (1.0 + scale)
        dxh = dn * nw
        dx = rr * (dxh - xh * jnp.mean(dxh * xh, axis=-1, keepdims=True))
        dx_ref[...] = dx + jnp.where(is_lat, dx2_ref[...], 0.0)
        dmod_ref[0, 0:1, :] += _rowsum(dh)
        dmod_ref[0, 1:2, :] += _rowsum(dh * n)
        dnw_ref[...] += _rowsum(dn * xh)
        dw_ref[...] += _mm_tn(dproj, h)

    row = lambda cols: pl.BlockSpec((tm, cols), lambda t: (t, 0))
    lat = lambda cols: pl.BlockSpec((tm, cols), lambda t: (jnp.minimum(t, n_lat - 1), 0))
    return _pcall(
        body, name=name, grid=(n_tiles,),
        out_shape=(_sds((r, D), F32), _sds((WIN_ROWS, D), F32), _sds((nrows, 2, D), F32), _sds((1, D), F32)),
        in_specs=[row(128), row(128), lat(256), lat(512), lat(512), lat(D), row(D), _mod_spec(2, tpe, nrows),
                  _const((1, D)), _const((WIN_ROWS, D))],
        out_specs=(row(D), _const((WIN_ROWS, D)), _mod_spec(2, tpe, nrows), _const((1, D))),
        vmem_mb=48)(dckv, dkpe, dq, du, dv, dx2, x1, mod2, norm_w, wint)


def _seg_sum(x, seg):
    return _mm(x.astype(BF16), seg)


def _seg_bcast(v, segt2):
    hi = v.astype(BF16)
    lo = (v - hi.astype(F32)).astype(BF16)
    return _mm(jnp.concatenate([hi, lo], axis=-1), segt2)


def _rope_pairs(t, cos, sin, rot2):
    cos2, sin2 = jnp.concatenate([cos, cos], axis=-1), jnp.concatenate([sin, sin], axis=-1)
    out = []
    for j in range(H // 2):
        tj = t[:, 2 * j * LANE:2 * (j + 1) * LANE]
        out.append(tj * cos2 + _dot_hl(tj, rot2) * sin2)
    return jnp.concatenate(out, axis=-1)


def _head_norm_rope(x, w_pad, cos, sin, seg, segt2, rot2, rope=True):
    rh = lax.rsqrt(_seg_sum(x * x, seg) * (1.0 / DH) + EPS)
    rb = _seg_bcast(rh, segt2)
    y = x * rb
    out = _rope_pairs(y * w_pad, cos, sin, rot2) if rope else None
    return out, y, rb


def _head_norm_rope_bwd(dout, y, rb, w_pad, cos, sin, seg, segt2, rot2_t):
    cos2, sin2 = jnp.concatenate([cos, cos], axis=-1), jnp.concatenate([sin, sin], axis=-1)
    dt = []
    for j in range(H // 2):
        dj = dout[:, 2 * j * LANE:2 * (j + 1) * LANE]
        dt.append(dj * cos2 + _dot_hl(dj * sin2, rot2_t))
    dt = jnp.concatenate(dt, axis=-1)
    dw = _rowsum(dt * y)
    dy = dt * w_pad
    mean_h = _seg_sum(dy * y, seg) * (1.0 / DH)
    return rb * (dy - y * _seg_bcast(mean_h, segt2)), dw


def _q_prep_fwd(qp, qa_w, wuq, wq, cos, sin, cs, *, tm, n_lat, tpe):
    def body(qp_ref, qa_ref, wuq_ref, wq_ref, cos_ref, sin_ref, seg, segt, rot, q_ref):
        x = qp_ref[...]
        cq = (x * _rms(x) * qa_ref[...]).astype(BF16)
        q, _, _ = _head_norm_rope(_mm_nt(cq, wuq_ref[...]), wq_ref[...], cos_ref[...], sin_ref[...],
                                  seg[...], segt[...], rot[...])
        q_ref[...] = q.astype(BF16)

    row = lambda cols: pl.BlockSpec((tm, cols), lambda t: (t, 0))
    tab = pl.BlockSpec((tm, LANE), lambda t: (t % tpe, 0))
    return _pcall(
        body, name="q_prep_fwd", grid=(n_lat,), out_shape=_sds((n_lat * tm, HP), BF16),
        in_specs=[row(QL), _const((1, QL)), _const((HP, QL)), _const((1, HP)), tab, tab,
                  _const((HP, LANE)), _const((2 * LANE, HP)), _const((2 * LANE, 2 * LANE))],
        out_specs=row(HP))(qp, qa_w, wuq, wq, cos, sin, cs["seg_h"], cs["seg_ht"], cs["rot"])


def _q_prep_bwd(dq, qp, qa_w, wuq, wq, cos, sin, cs, *, tm, n_lat, tpe):
    def body(dq_ref, qp_ref, qa_ref, wuq_ref, wq_ref, cos_ref, sin_ref, seg, segt, rot, rot_t,
             dqp_ref, dwuq_ref, dqa_ref, dwq_ref):
        @pl.when(pl.program_id(0) == 0)
        def _():
            dwuq_ref[...] = jnp.zeros_like(dwuq_ref)
            dqa_ref[...] = jnp.zeros_like(dqa_ref)
            dwq_ref[...] = jnp.zeros_like(dwq_ref)

        x = qp_ref[...]
        ra = _rms(x)
        xh = x * ra
        qa = qa_ref[...]
        cq = (xh * qa).astype(BF16)
        wuq_v = wuq_ref[...]
        wq_v, cos_v, sin_v = wq_ref[...], cos_ref[...], sin_ref[...]
        _, y, rb = _head_norm_rope(_mm_nt(cq, wuq_v), wq_v, cos_v, sin_v, seg[...], segt[...], rot[...], rope=False)
        dqraw, dwq = _head_norm_rope_bwd(dq_ref[...], y, rb, wq_v, cos_v, sin_v, seg[...], segt[...], rot_t[...])
        dqraw = dqraw.astype(BF16)
        dcq = _mm(dqraw, wuq_v)
        dxh = dcq * qa
        dqp_ref[...] = (ra * (dxh - xh * jnp.mean(dxh * xh, axis=-1, keepdims=True))).astype(BF16)
        dwuq_ref[...] += _mm_tn(dqraw, cq)
        dqa_ref[...] += _rowsum(dcq * xh)
        dwq_ref[...] += dwq

    row = lambda cols: pl.BlockSpec((tm, cols), lambda t: (t, 0))
    tab = pl.BlockSpec((tm, LANE), lambda t: (t % tpe, 0))
    return _pcall(
        body, name="q_prep_bwd", grid=(n_lat,),
        out_shape=(_sds((n_lat * tm, QL), BF16), _sds((HP, QL), F32), _sds((1, QL), F32), _sds((1, HP), F32)),
        in_specs=[row(HP), row(QL), _const((1, QL)), _const((HP, QL)), _const((1, HP)), tab, tab,
                  _const((HP, LANE)), _const((2 * LANE, HP)), _const((2 * LANE, 2 * LANE)), _const((2 * LANE, 2 * LANE))],
        out_specs=(row(QL), _const((HP, QL)), _const((1, QL)), _const((1, HP))), vmem_mb=40)(
            dq, qp, qa_w, wuq, wq, cos, sin, cs["seg_h"], cs["seg_ht"], cs["rot"], cs["rot_t"])


def _kv_tab_spec(tm, tpe, n_lat):
    return pl.BlockSpec((tm, LANE), lambda t: (jnp.where(t < n_lat, t % tpe, tpe), 0))


def _split_kv(kv, kpe):
    low = lax.broadcasted_iota(jnp.int32, (kv.shape[0], LANE), 1) < DN
    kx, v = [], []
    for h in range(H):
        blk = kv[:, h * LANE:(h + 1) * LANE]
        kx.append(jnp.where(low, blk, kpe))
        v.append(jnp.where(low, pltpu.roll(blk, DN, 1), 0.0))
    return jnp.concatenate(kx, axis=-1), jnp.concatenate(v, axis=-1)


def _kv_prep_fwd(ckv, kpe, kva_w, wukv, wk, cosk, sink, cs, *, tm, n_tiles, tpe, n_lat):
    def body(ckv_ref, kpe_ref, kva_ref, wukv_ref, wk_ref, cos_ref, sin_ref, seg, segt, rot, k_ref, v_ref):
        x = ckv_ref[...]
        ckvn = (x * _rms(x) * kva_ref[...]).astype(BF16)
        kx, v = _split_kv(_mm_nt(ckvn, wukv_ref[...]), kpe_ref[...])
        k, _, _ = _head_norm_rope(kx, wk_ref[...], cos_ref[...], sin_ref[...], seg[...], segt[...], rot[...])
        k_ref[...] = k.astype(BF16)
        v_ref[...] = v.astype(BF16)

    row = lambda cols: pl.BlockSpec((tm, cols), lambda t: (t, 0))
    tab = _kv_tab_spec(tm, tpe, n_lat)
    r = n_tiles * tm
    return _pcall(
        body, name="kv_prep_fwd", grid=(n_tiles,), out_shape=(_sds((r, HP), BF16), _sds((r, HP), BF16)),
        in_specs=[row(KVL), row(LANE), _const((1, KVL)), _const((HP, KVL)), _const((1, HP)), tab, tab,
                  _const((HP, LANE)), _const((2 * LANE, HP)), _const((2 * LANE, 2 * LANE))],
        out_specs=(row(HP), row(HP)), vmem_mb=40)(
            ckv, kpe, kva_w, wukv, wk, cosk, sink, cs["seg_h"], cs["seg_ht"], cs["rot"])


def _kv_prep_bwd(dks, dvs, ckv, kpe, kva_w, wukv, wk, cosk, sink, cs, *, tm, n_tiles, tpe, n_lat):
    def body(dkl_ref, dkc_ref, dvl_ref, dvc_ref, ckv_ref, kpe_ref, kva_ref, wukv_ref, wk_ref, cos_ref, sin_ref,
             seg, segt, rot, rot_t, dckv_ref, dkpe_ref, dwukv_ref, dkva_ref, dwk_ref):
        t = pl.program_id(0)
        is_lat = t < n_lat

        @pl.when(t == 0)
        def _():
            dwukv_ref[...] = jnp.zeros_like(dwukv_ref)
            dkva_ref[...] = jnp.zeros_like(dkva_ref)
            dwk_ref[...] = jnp.zeros_like(dwk_ref)

        dk = jnp.where(is_lat, dkl_ref[...], dkc_ref[...])
        dv = jnp.where(is_lat, dvl_ref[...], dvc_ref[...])
        x = ckv_ref[...]
        ra = _rms(x)
        xh = x * ra
        kva = kva_ref[...]
        ckvn = (xh * kva).astype(BF16)
        wukv_v = wukv_ref[...]
        wk_v, cos_v, sin_v = wk_ref[...], cos_ref[...], sin_ref[...]
        kx, _ = _split_kv(_mm_nt(ckvn, wukv_v), kpe_ref[...])
        _, y, rb = _head_norm_rope(kx, wk_v, cos_v, sin_v, seg[...], segt[...], rot[...], rope=False)
        dkx, dwk = _head_norm_rope_bwd(dk, y, rb, wk_v, cos_v, sin_v, seg[...], segt[...], rot_t[...])
        dkpe = dkx[:, 0:LANE]
        for h in range(1, H):
            dkpe = dkpe + dkx[:, h * LANE:(h + 1) * LANE]
        lane = lax.broadcasted_iota(jnp.int32, (tm, LANE), 1)
        dkpe_ref[...] = jnp.where((lane >= DN) & (lane < DH), dkpe, 0.0).astype(BF16)
        dkv = jnp.concatenate([jnp.where(lane < DN, dkx[:, h * LANE:(h + 1) * LANE],
                                         pltpu.roll(dv[:, h * LANE:(h + 1) * LANE], DN, 1)) for h in range(H)],
                              axis=-1).astype(BF16)
        dckvn = _mm(dkv, wukv_v)
        dxh = dckvn * kva
        dckv_ref[...] = (ra * (dxh - xh * jnp.mean(dxh * xh, axis=-1, keepdims=True))).astype(BF16)
        dwukv_ref[...] += _mm_tn(dkv, ckvn)
        dkva_ref[...] += _rowsum(dckvn * xh)
        dwk_ref[...] += dwk

    row = lambda cols: pl.BlockSpec((tm, cols), lambda t: (t, 0))
    lat = pl.BlockSpec((tm, HP), lambda t: (jnp.minimum(t, n_lat - 1), 0))
    ctx = pl.BlockSpec((tm, HP), lambda t: (jnp.maximum(t - n_lat, 0), 0))
    tab = _kv_tab_spec(tm, tpe, n_lat)
    r = n_tiles * tm
    return _pcall(
        body, name="kv_prep_bwd", grid=(n_tiles,),
        out_shape=(_sds((r, KVL), BF16), _sds((r, LANE), BF16), _sds((HP, KVL), F32), _sds((1, KVL), F32),
                   _sds((1, HP), F32)),
        in_specs=[lat, ctx, lat, ctx, row(KVL), row(LANE), _const((1, KVL)), _const((HP, KVL)), _const((1, HP)),
                  tab, tab, _const((HP, LANE)), _const((2 * LANE, HP)), _const((2 * LANE, 2 * LANE)),
                  _const((2 * LANE, 2 * LANE))],
        out_specs=(row(KVL), row(LANE), _const((HP, KVL)), _const((1, KVL)), _const((1, HP))), vmem_mb=48)(
            dks[0], dks[1], dvs[0], dvs[1], ckv, kpe, kva_w, wukv, wk, cosk, sink,
            cs["seg_h"], cs["seg_ht"], cs["rot"], cs["rot_t"])


_SCALE = DH ** -0.5
_SCALE_LOG2E = _SCALE * 1.4426950408889634


def _key_chunks(s, nc, ck):
    return ([(0, lo, min(lo + ck, s)) for lo in range(0, s, ck)]
            + [(1, lo, min(lo + ck, nc)) for lo in range(0, nc, ck)])


def _attn_fwd(q, k, v, *, nb, s, nc, tq, ck):
    tpe = s // tq
    r_lat = nb * s
    chunks = _key_chunks(s, nc, ck)
    hp = 4

    def body(q_ref, kl_ref, kc_ref, vl_ref, vc_ref, o_ref, lse_ref):
        k_refs, v_refs = (kl_ref, kc_ref), (vl_ref, vc_ref)
        for hh in range(hp):
            hs = slice(hh * LANE, (hh + 1) * LANE)
            qv = q_ref[:, hs]
            xs = [_mm_nt(qv, k_refs[w][lo:hi, hs]) for w, lo, hi in chunks]
            m = jnp.max(xs[0], axis=-1, keepdims=True)
            for x in xs[1:]:
                m = jnp.maximum(m, jnp.max(x, axis=-1, keepdims=True))
            l = acc = None
            for x, (w, lo, hi) in zip(xs, chunks):
                e = jnp.exp2((x - m) * _SCALE_LOG2E)
                lc = jnp.sum(e, axis=-1, keepdims=True)
                pv = _mm(e.astype(BF16), v_refs[w][lo:hi, hs])
                l = lc if l is None else l + lc
                acc = pv if acc is None else acc + pv
            o_ref[:, hs] = (acc / l).astype(BF16)
            lse = m * _SCALE_LOG2E + jnp.log2(l)
            lse_ref[hh] = jnp.transpose(jnp.broadcast_to(lse, (tq, LANE)))[0:8, :]

    qs = pl.BlockSpec((tq, hp * LANE), lambda i, j, t: (i * tpe + t, j))
    kl = pl.BlockSpec((s, hp * LANE), lambda i, j, t: (i, j))
    kc = pl.BlockSpec((nc, hp * LANE), lambda i, j, t: (r_lat // nc + i, j))
    ls = pl.BlockSpec((hp, 8, tq), lambda i, j, t: (i * (H // hp) + j, 0, t))
    return _pcall(body, name="attn_fwd", grid=(nb, H // hp, tpe),
                  out_shape=(_sds((r_lat, HP), BF16), _sds((nb * H, 8, s), F32)),
                  in_specs=[qs, kl, kc, kl, kc], out_specs=(qs, ls), vmem_mb=48)(q, k, k, v, v)


def _attn_bwd(q, k, v, o, do, lse, part, *, nb, s, nc, tq, ck):
    tpe = s // tq
    r_lat = nb * s
    chunks = _key_chunks(s, nc, ck)
    hp = 2
    n_steps = nb * (H // hp) * tpe

    def body(q_ref, kl_ref, kc_ref, vl_ref, vc_ref, o_ref, do_ref, lse_ref, part_ref,
             dq_ref, dkl_ref, dkc_ref, dvl_ref, dvc_ref, recv_ref, akl, akc, avl, avc, send_sems, recv_sems):
        t = pl.program_id(2)
        step = (pl.program_id(0) * (H // hp) + pl.program_id(1)) * tpe + t
        _exchange_behind(part_ref, recv_ref, send_sems, recv_sems, step == 0, step == n_steps - 1)

        @pl.when(t == 0)
        def _():
            akl[...] = jnp.zeros_like(akl)
            akc[...] = jnp.zeros_like(akc)
            avl[...] = jnp.zeros_like(avl)
            avc[...] = jnp.zeros_like(avc)

        k_refs, v_refs, ak, av = (kl_ref, kc_ref), (vl_ref, vc_ref), (akl, akc), (avl, avc)
        for hh in range(hp):
            hs = slice(hh * LANE, (hh + 1) * LANE)
            qv = q_ref[:, hs]
            lse = jnp.transpose(jnp.concatenate([lse_ref[hh]] * (LANE // 8), axis=0))[:, 0:1]
            dov = do_ref[:, hs]
            delta = jnp.sum(dov.astype(F32) * o_ref[:, hs].astype(F32), axis=-1, keepdims=True)
            dq = None
            for w, lo, hi in chunks:
                kc_v = k_refs[w][lo:hi, hs]
                p = jnp.exp2(_mm_nt(qv, kc_v) * _SCALE_LOG2E - lse)
                ds = (p * (_mm_nt(dov, v_refs[w][lo:hi, hs]) - delta)).astype(BF16)
                part = _mm(ds, kc_v)
                dq = part if dq is None else dq + part
                ak[w][hs, lo:hi] += _mm_tn(qv, ds)
                av[w][hs, lo:hi] += _mm_tn(dov, p.astype(BF16))
            dq_ref[:, hs] = dq * _SCALE

        @pl.when(t == tpe - 1)
        def _():
            dkl_ref[...] = akl[...].T * _SCALE
            dkc_ref[...] = akc[...].T * _SCALE
            dvl_ref[...] = avl[...].T
            dvc_ref[...] = avc[...].T

    qs = pl.BlockSpec((tq, hp * LANE), lambda i, j, t: (i * tpe + t, j))
    kl = pl.BlockSpec((s, hp * LANE), lambda i, j, t: (i, j))
    kc = pl.BlockSpec((nc, hp * LANE), lambda i, j, t: (r_lat // nc + i, j))
    kc_out = pl.BlockSpec((nc, hp * LANE), lambda i, j, t: (i, j))
    ls = pl.BlockSpec((hp, 8, tq), lambda i, j, t: (i * (H // hp) + j, 0, t))
    return _pcall(
        body, name="attn_bwd", grid=(nb, H // hp, tpe),
        out_shape=(_sds((r_lat, HP), F32), _sds((r_lat, HP), F32), _sds((nb * nc, HP), F32),
                   _sds((r_lat, HP), F32), _sds((nb * nc, HP), F32), _sds((NDEV - 1,) + part.shape[1:], part.dtype)),
        in_specs=[qs, kl, kc, kl, kc, qs, qs, ls, ANY], out_specs=(qs, kl, kc_out, kl, kc_out, ANY),
        scratch=[pltpu.VMEM((hp * LANE, s), F32), pltpu.VMEM((hp * LANE, nc), F32)] * 2
        + [pltpu.SemaphoreType.DMA((NDEV - 1,))] * 2,
        vmem_mb=60)(q, k, k, v, v, o, do, lse, part)


def _chunks_side_by_side(x, j, nch):
    return jnp.concatenate([x[c * CH:(c + 1) * CH, j * LANE:(j + 1) * LANE] for c in range(nch)], axis=-1)


def _first_group_lanes(nch):
    return (lax.broadcasted_iota(jnp.int32, (CH, nch * LANE), 1) & (LANE - 1)) < GD


def _gating(vn, ws_ref, bias_ref, s_scr, tm):
    nch = tm // CH
    first = _first_group_lanes(nch)
    for j in range(G // 2):
        ls = slice(j * LANE, (j + 1) * LANE)
        vst = _chunks_side_by_side(vn, j, nch)
        st = jnp.where(first, _mm(ws_ref[2 * j], vst), _mm(ws_ref[2 * j + 1], vst))
        for c in range(nch):
            s_scr[c * CH:(c + 1) * CH, ls] = st[:, c * LANE:(c + 1) * LANE] + bias_ref[:, ls]


def _compact_heads(x):
    low = lax.broadcasted_iota(jnp.int32, (x.shape[0], LANE), 1) < DV
    out = []
    for j in range(H // 2):
        even = x[:, 2 * j * LANE:(2 * j + 1) * LANE].astype(F32)
        odd = x[:, (2 * j + 1) * LANE:(2 * j + 2) * LANE].astype(F32)
        out.append(jnp.where(low, even, pltpu.roll(odd, DV, 1)))
    return jnp.concatenate(out, axis=-1)


def _expand_heads(x):
    low = lax.broadcasted_iota(jnp.int32, (x.shape[0], LANE), 1) < DV
    out = []
    for j in range(H // 2):
        blk = x[:, j * LANE:(j + 1) * LANE]
        out.append(jnp.where(low, blk, 0.0))
        out.append(jnp.where(low, pltpu.roll(blk, DV, 1), 0.0))
    return jnp.concatenate(out, axis=-1)


def _mix_fwd(u, v, attn, x1, gate, wv, ws, bias, wout, cs, *, tm, n_lat, tpe):
    nrows = gate.shape[0]

    def body(u_ref, v_ref, attn_ref, x_ref, gate_ref, wv_ref, ws_ref, bias_ref, wout_ref, seg, segt,
             x2_ref, mix_ref, s_scr):
        vg = _gelu(v_ref[...])
        rg = lax.rsqrt(_seg_sum(vg * vg, seg[...]) * (1.0 / GD) + EPS)
        vn = (vg * _seg_bcast(rg, segt[...]) * wv_ref[...]).astype(BF16)
        _gating(vn, ws_ref, bias_ref, s_scr, tm)
        sg = (_gelu(u_ref[...]) * s_scr[...]).astype(BF16)
        attn_c = _compact_heads(attn_ref[...]).astype(BF16)
        mix = _mm(attn_c, wout_ref[0:H * DV, :]) + _mm(sg, wout_ref[H * DV:, :])
        mix_ref[...] = mix.astype(BF16)
        x2_ref[...] = x_ref[...] + gate_ref[0] * mix

    row = lambda cols: pl.BlockSpec((tm, cols), lambda t: (t, 0))
    r = n_lat * tm
    return _pcall(
        body, name="mix_fwd", grid=(n_lat,),
        out_shape=(_sds((r, D), F32), _sds((r, D), BF16)),
        in_specs=[row(G * GD), row(G * GD), row(HP), row(D), _mod_spec(1, tpe, nrows), _const((1, G * GD)),
                  _const((G, CH, CH)), _const((CH, G * GD)), _const((D, D)), _const((G * GD, LANE)),
                  _const((2 * LANE, G * GD))],
        out_specs=(row(D), row(D)), scratch=[pltpu.VMEM((tm, G * GD), F32)], vmem_mb=40)(
            u, v, attn, x1, gate, wv, ws, bias, wout, cs["seg_g"], cs["seg_gt"])


def _mix_bwd(dx2, mix, u, v, attn, gate, wv, ws, wst, bias, wout, cs, *, tm, n_lat, tpe):
    nrows = gate.shape[0]
    wrows = H * DV + G * GD

    def body(dx2_ref, mix_ref, u_ref, v_ref, attn_ref, gate_ref, wv_ref, ws_ref, wst_ref, bias_ref, wout_ref, seg, segt,
             dattn_ref, du_ref, dv_ref, dgate_ref, dwout_ref, dws_ref, dbs_ref, dwv_ref, s_scr, dvn_scr, dbias_scr):
        t = pl.program_id(0)

        @pl.when(t == 0)
        def _():
            dwout_ref[...] = jnp.zeros_like(dwout_ref)
            dws_ref[...] = jnp.zeros_like(dws_ref)
            dwv_ref[...] = jnp.zeros_like(dwv_ref)
            dbias_scr[...] = jnp.zeros_like(dbias_scr)

        @pl.when(t % tpe == 0)
        def _():
            dgate_ref[...] = jnp.zeros_like(dgate_ref)

        dx2 = dx2_ref[...]
        dmix = (dx2 * gate_ref[0]).astype(BF16)
        dcat = _mm_nt(dmix, wout_ref[...])
        dattn_ref[...] = _expand_heads(dcat[:, :H * DV]).astype(BF16)
        dsg = dcat[:, H * DV:]

        vraw = v_ref[...]
        vg = _gelu(vraw)
        rg = lax.rsqrt(_seg_sum(vg * vg, seg[...]) * (1.0 / GD) + EPS)
        r64 = _seg_bcast(rg, segt[...])
        y = vg * r64
        wv_v = wv_ref[...]
        vn = (y * wv_v).astype(BF16)
        _gating(vn, ws_ref, bias_ref, s_scr, tm)
        uraw = u_ref[...]
        ug = _gelu(uraw)
        s = s_scr[...]
        sg = (ug * s).astype(BF16)
        du_ref[...] = (dsg * s * _gelu_grad(uraw)).astype(BF16)
        ds = dsg * ug
        dgate_ref[0] += _rowsum(dx2 * mix_ref[...].astype(F32))
        attn_c = _compact_heads(attn_ref[...]).astype(BF16)
        dwout_ref[...] += _mm_tn(jnp.concatenate([attn_c, sg], axis=-1), dmix)

        nch = tm // CH
        first = _first_group_lanes(nch)
        for c in range(nch):
            dbias_scr[...] += ds[c * CH:(c + 1) * CH, :]
        for j in range(G // 2):
            ls = slice(j * LANE, (j + 1) * LANE)
            dst32 = _chunks_side_by_side(ds, j, nch)
            dst = dst32.astype(BF16)
            vst = _chunks_side_by_side(vn, j, nch)
            dvn_st = jnp.where(first, _mm(wst_ref[2 * j], dst), _mm(wst_ref[2 * j + 1], dst))
            for c in range(nch):
                dvn_scr[c * CH:(c + 1) * CH, ls] = dvn_st[:, c * LANE:(c + 1) * LANE]
            dws_ref[2 * j] += _mm_nt(jnp.where(first, dst32, 0.0).astype(BF16), vst)
            dws_ref[2 * j + 1] += _mm_nt(jnp.where(first, 0.0, dst32).astype(BF16), vst)

        dvn = dvn_scr[...]
        dwv_ref[...] += _rowsum(dvn * y)
        dy = dvn * wv_v
        mean_g = _seg_sum(dy * y, seg[...]) * (1.0 / GD)
        dvg = r64 * (dy - y * _seg_bcast(mean_g, segt[...]))
        dv_ref[...] = (dvg * _gelu_grad(vraw)).astype(BF16)

        @pl.when(t == n_lat - 1)
        def _():
            dbs_ref[...] = _dot_hl(dbias_scr[...], seg[...])

    row = lambda cols: pl.BlockSpec((tm, cols), lambda t: (t, 0))
    r = n_lat * tm
    return _pcall(
        body, name="mix_bwd", grid=(n_lat,),
        out_shape=(_sds((r, HP), BF16), _sds((r, G * GD), BF16), _sds((r, G * GD), BF16), _sds((nrows, 1, D), F32),
                   _sds((wrows, D), F32), _sds((G, CH, CH), F32), _sds((CH, LANE), F32), _sds((1, G * GD), F32)),
        in_specs=[row(D), row(D), row(G * GD), row(G * GD), row(HP), _mod_spec(1, tpe, nrows), _const((1, G * GD)),
                  _const((G, CH, CH)), _const((G, CH, CH)), _const((CH, G * GD)), _const((wrows, D)),
                  _const((G * GD, LANE)), _const((2 * LANE, G * GD))],
        out_specs=(row(HP), row(G * GD), row(G * GD), _mod_spec(1, tpe, nrows), _const((wrows, D)),
                   _const((G, CH, CH)), _const((CH, LANE)), _const((1, G * GD))),
        scratch=[pltpu.VMEM((tm, G * GD), F32), pltpu.VMEM((tm, G * GD), F32), pltpu.VMEM((CH, G * GD), F32)],
        vmem_mb=56)(dx2, mix, u, v, attn, gate, wv, ws, wst, bias, wout, cs["seg_g"], cs["seg_gt"])


def _adamw_math(w, g, m, v):
    m2 = ADAM_B1 * m + (1.0 - ADAM_B1) * g
    v2 = ADAM_B2 * v + (1.0 - ADAM_B2) * (g * g)
    m_hat = m2 / (1.0 - ADAM_B1 ** ADAM_STEP)
    v_hat = v2 / (1.0 - ADAM_B2 ** ADAM_STEP)
    delta = -ADAM_LR * (m_hat / (jnp.sqrt(v_hat) + ADAM_EPS) + ADAM_WD * w)
    return delta, m2, v2


def _row_tile(r, c):
    best = r
    for tr in range(8, r, 8):
        if r % tr == 0 and tr * c * 4 <= MIB:
            best = tr
    return best


def _adamw(w, g, m, v, name):
    r, c = w.shape
    tr = _row_tile(r, c)

    def body(w_ref, g_ref, m_ref, v_ref, d_ref, mo_ref, vo_ref):
        d_ref[...], mo_ref[...], vo_ref[...] = _adamw_math(w_ref[...], g_ref[...], m_ref[...], v_ref[...])

    blk = pl.BlockSpec((tr, c), lambda t: (t, 0))
    return _pcall(body, name=name, grid=(r // tr,), out_shape=(_sds((r, c), F32),) * 3,
                  in_specs=[blk] * 4, out_specs=(blk,) * 3)(w, g, m, v)


def _adamw_small(params):
    n = len(params)

    def body(*refs):
        ins, outs = refs[:4 * n], refs[4 * n:]
        for i in range(n):
            w, g, m, v = (ins[4 * i + k][...] for k in range(4))
            if i == 0:
                sig = _sigmoid(w)
                g = g * (sig * (1.0 + w * (1.0 - sig)))
            d, m2, v2 = _adamw_math(w, g, m, v)
            outs[4 * i][...] = g
            outs[4 * i + 1][...] = d
            outs[4 * i + 2][...] = m2
            outs[4 * i + 3][...] = v2

    flat = [a for p in params for a in p]
    out_shape = tuple(_sds(p[0].shape, F32) for p in params for _ in range(4))
    res = _pcall(body, name="adamw_small", out_shape=out_shape, in_specs=[VMEM] * (4 * n),
                 out_specs=(VMEM,) * (4 * n))(*flat)
    return [res[4 * i:4 * i + 4] for i in range(n)]


def _rope_tables(s):
    rows = jnp.repeat(jnp.arange(s // GRID_W, dtype=F32), GRID_W)
    cols = jnp.tile(jnp.arange(GRID_W, dtype=F32), s // GRID_W)
    half = DR // 2
    inv = ROPE_BASE ** (-jnp.arange(0, half, 2, dtype=F32) / half)
    ang_r = rows[:, None] * inv
    ang_c = cols[:, None] * inv
    ang = jnp.concatenate([ang_r, ang_r, ang_c, ang_c], axis=-1)
    return jnp.cos(ang), jnp.sin(ang)


def _head_pad(a, real):
    return jnp.pad(a, ((0, 0), (0, LANE - real), (0, 0))).reshape(HP, a.shape[2])


def kernel(x, c, ctx, c_ctx, w_ada, b_ada, norm1_w, ffn1_w1, ffn1_w3, ffn1_w2, norm2_w, w_in, q_a_norm_w, w_uq, kv_a_norm_w, w_ukv, q_norm_w, k_norm_w, v_norm_w, w_s, b_s, w_out, norm3_w, ffn2_w1, ffn2_w3, ffn2_w2, loss_target, m_c_ctx, m_w_ada, m_b_ada, m_norm1_w, m_ffn1_w1, m_ffn1_w3, m_ffn1_w2, m_norm2_w, m_w_in, m_q_a_norm_w, m_w_uq, m_kv_a_norm_w, m_w_ukv, m_q_norm_w, m_k_norm_w, m_v_norm_w, m_w_s, m_b_s, m_w_out, m_norm3_w, m_ffn2_w1, m_ffn2_w3, m_ffn2_w2, v_c_ctx, v_w_ada, v_b_ada, v_norm1_w, v_ffn1_w1, v_ffn1_w3, v_ffn1_w2, v_norm2_w, v_w_in, v_q_a_norm_w, v_w_uq, v_kv_a_norm_w, v_w_ukv, v_q_norm_w, v_k_norm_w, v_v_norm_w, v_w_s, v_b_s, v_w_out, v_norm3_w, v_ffn2_w1, v_ffn2_w3, v_ffn2_w2):
    nb, s, _ = x.shape
    nc = ctx.shape[1]
    tm = 256 if nc % 256 == 0 else 128
    tpe = s // tm
    n_lat = nb * tpe
    n_all = n_lat + nb * nc // tm
    tmf = 2 * tm if s % (2 * tm) == 0 and (nb * nc) % (2 * tm) == 0 else tm
    tp = tmf
    r_lat = nb * s
    tpe_p, n_lat_p, n_all_p = s // tp, r_lat // tp, (r_lat + nb * nc) // tp
    me = 4 * lax.axis_index("x") + 2 * lax.axis_index("y") + lax.axis_index("c")
    cs = _consts()
    ncol = w_ada.shape[2]
    fsh = ffn1_w1.shape[2]
    assert nb + 1 <= 8 and NDEV * fsh == FF and NDEV * ncol == NMOD * D and s % nc == 0 and nc % tm == 0

    def t16(a):
        return a.T.astype(BF16)

    wpack1 = jnp.concatenate([t16(ffn1_w1[0]), t16(ffn1_w3[0]), ffn1_w2[0].astype(BF16)], axis=0)
    a_loc = jnp.concatenate([c, c_ctx[None, :], jnp.zeros((7 - nb, D), F32)], axis=0)
    a_raw, _, mod_all, wall1 = _ada_front(a_loc, w_ada[0], lax.dynamic_slice_in_dim(b_ada, me * ncol, ncol, axis=1),
                                          wpack1)
    a_raw = a_raw.reshape(NDEV * 8, D)
    mod_mine = lax.dynamic_slice_in_dim(mod_all, 8 * me, 8, axis=1)
    modtab = mod_mine.transpose(1, 0, 2).reshape(8, NMOD, D)[:nb + 1]
    wpack2 = jnp.concatenate([
        t16(ffn2_w1[0]), t16(ffn2_w3[0]), ffn2_w2[0].astype(BF16),
        t16(w_in[0]), jnp.zeros((12, D), BF16),
        w_out[0].astype(BF16),
        t16(w_uq[0]).reshape(24, D), jnp.zeros((8, D), BF16),
        t16(w_ukv[0]).reshape(16, D)], axis=0)

    def head_w(wn):
        return jnp.tile(jnp.pad(wn, ((0, 0), (0, LANE - DH))), (1, H))

    wq, wk = head_w(q_norm_w), head_w(k_norm_w)
    wv = v_norm_w.reshape(1, G * GD)
    ws16 = w_s[0].astype(BF16)
    wst16 = w_s[0].transpose(0, 2, 1).astype(BF16)
    bias = jnp.repeat(b_s[0].T, GD, axis=1)
    cos, sin = _rope_tables(s)
    cos = jnp.pad(cos, ((0, 0), (DN, LANE - DH)), constant_values=1.0)
    sin = jnp.pad(sin, ((0, 0), (DN, LANE - DH)))
    cos_k = jnp.concatenate([cos, jnp.ones((tm, LANE), F32)], axis=0)
    sin_k = jnp.concatenate([sin, jnp.zeros((tm, LANE), F32)], axis=0)

    xs = (x.reshape(r_lat, D), ctx.reshape(nb * nc, D))
    x1, a1, b1, o1, wall2 = _ffn_fwd(xs, modtab[:, 0:3], norm1_w, wall1, 0, tm=tmf, n_tiles=(r_lat + nb * nc) // tmf,
                                     tpe=s // tmf, n_lat=r_lat // tmf, name="ffn1_fwd", gather=wpack2)

    o0 = 3 * fsh
    wint = wall2[:, o0:o0 + 180].reshape(IN_COLS, D)
    z = lambda n: jnp.zeros((n, D), BF16)
    wint = jnp.concatenate([wint[0:128], wint[160:416], wint[416:928], wint[928:1440],
                            z(DN), wint[128:160], z(LANE - DH)], axis=0)
    wout = wall2[:, o0 + 192:o0 + 320].reshape(D, D)
    wuq = _head_pad(wall2[:, o0 + 320:o0 + 344].reshape(H, DH, QL), DH)
    wukv = wall2[:, o0 + 352:o0 + 368].reshape(HP, KVL)

    ckv, qp, u_raw, v_raw, kpe = _proj_fwd(x1, modtab[:, 3:5], norm2_w, wint, tm=tp, n_tiles=n_all_p, tpe=tpe_p)
    q = _q_prep_fwd(qp, q_a_norm_w, wuq, wq, cos, sin, cs, tm=tm, n_lat=n_lat, tpe=tpe)
    k, v = _kv_prep_fwd(ckv, kpe, kv_a_norm_w, wukv, wk, cos_k, sin_k, cs,
                        tm=tm, n_tiles=n_all, tpe=tpe, n_lat=n_lat)
    attn, lse = _attn_fwd(q, k, v, nb=nb, s=s, nc=nc, tq=tm, ck=2048)
    x2, mix = _mix_fwd(u_raw, v_raw, attn, x1, modtab[:nb, 5:6], wv, ws16, bias, wout, cs,
                       tm=tp, n_lat=n_lat_p, tpe=tpe_p)
    dy, a2, b2, o2, lsum = _ffn_fwd((x2,), modtab[:nb, 6:9], norm3_w, wall2, 0, tm=tmf, n_tiles=r_lat // tmf,
                                    tpe=s // tmf, n_lat=r_lat // tmf, name="ffn2_fwd",
                                    target=loss_target.reshape(r_lat, D))

    tr = 2 * tm if n_lat % 2 == 0 and n_all % 2 == 0 else tm
    dx2, da2, db2, g2, do2, h2, dmod678, dnorm3 = _ffn_bwd_dx(
        dy, (x2,), a2, b2, o2, modtab[:nb, 6:9], norm3_w, wall2, 0,
        tm=tm, n_tiles=n_lat, tpe=tpe, n_lat=n_lat, name="ffn2_bwd_dx")
    g_ffn2 = _ffn_bwd_dw(h2, do2, da2, db2, g2, tr=tr, name="ffn2_bwd_dw")

    dattn, du, dv, dgate5, dwout, dws, dbs, dwv = _mix_bwd(
        dx2, mix, u_raw, v_raw, attn, modtab[:nb, 5:6], wv, ws16, wst16, bias, wout, cs, tm=tp, n_lat=n_lat_p, tpe=tpe_p)
    tq = 2 * tm if s % (2 * tm) == 0 else tm
    dq, dk_l, dk_c, dv_l, dv_c, recv_ffn2 = _attn_bwd(q, k, v, attn, dattn, lse, g_ffn2,
                                                      nb=nb, s=s, nc=nc, tq=tq, ck=1024)
    dqp, dwuq, dqa, dwq = _q_prep_bwd(dq, qp, q_a_norm_w, wuq, wq, cos, sin, cs, tm=tm, n_lat=n_lat, tpe=tpe)
    dckv, dkpe, dwukv, dkva, dwk = _kv_prep_bwd((dk_l, dk_c), (dv_l, dv_c), ckv, kpe, kv_a_norm_w, wukv, wk,
                                                cos_k, sin_k, cs, tm=tm, n_tiles=n_all, tpe=tpe, n_lat=n_lat)
    dx1, dwin, dmod34, dnorm2 = _proj_bwd(dckv, dkpe, dqp, du, dv, dx2, x1, modtab[:, 3:5], norm2_w, wint,
                                          tm=tp, n_tiles=n_all_p, tpe=tpe_p, n_lat=n_lat_p)

    def blocks(a):
        return a.reshape(NDEV, a.shape[0] // NDEV, D)

    dwin_o = jnp.concatenate([dwin[0:128], dwin[KPE_LO:KPE_LO + DR], dwin[128:384], dwin[384:896], dwin[896:1408]],
                             axis=0)
    dwuq_o = dwuq.reshape(H, LANE, QL)[:, :DH]
    gmisc = jnp.concatenate([
        blocks(dwin_o).astype(BF16), jnp.zeros((NDEV, 12, D), BF16),
        blocks(dwout).astype(BF16),
        dwuq_o.reshape(NDEV, 24, D).astype(BF16), jnp.zeros((NDEV, 8, D), BF16),
        dwukv.reshape(NDEV, 16, D).astype(BF16)], axis=1)

    dx0, da1, db1, g1, do1, h1, dmod012, dnorm1 = _ffn_bwd_dx(
        dx1, xs, a1, b1, o1, modtab[:, 0:3], norm1_w, wall1, 0,
        tm=tm, n_tiles=n_all, tpe=tpe, n_lat=n_lat, name="ffn1_bwd_dx")
    grad_x = dx0.reshape(nb, s, D)
    g_w1, recv_misc = _ffn_bwd_dw_one(da1, h1, tr=tr, name="ffn1_bwd_dw1", part=gmisc)
    g_w3, recv_w1 = _ffn_bwd_dw_one(db1, h1, tr=tr, name="ffn1_bwd_dw3", part=g_w1)
    g_w2, recv_w3 = _ffn_bwd_dw_one(g1, do1, tr=tr, name="ffn1_bwd_dw2", part=g_w3)

    zrow = jnp.zeros((1, D), F32)
    g_lat = jnp.concatenate([dmod012[:nb, 0], dmod012[:nb, 1], dmod012[:nb, 2], dmod34[:nb, 0], dmod34[:nb, 1],
                             dgate5[:, 0], dmod678[:, 0], dmod678[:, 1], dmod678[:, 2]], axis=1)
    g_ctx = jnp.concatenate([dmod012[nb:, 0], dmod012[nb:, 1], dmod012[nb:, 2], dmod34[nb:, 0], dmod34[nb:, 1],
                             zrow, zrow, zrow, zrow], axis=1)
    g_loc = jnp.concatenate([g_lat, g_ctx, jnp.zeros((7 - nb, NMOD * D), F32)], axis=0)

    got_w2, g_all = _scatter_sibling([g_w2], "scatter_sibling_w2", gather=g_loc)
    g_all = g_all.reshape(NDEV * 8, NMOD * D)
    g_cols = lax.dynamic_slice_in_dim(g_all, me * ncol, ncol, axis=1)
    g_w_ada, pc_ctx, g_b_ada = _ada_bwd(a_raw, c_ctx.reshape(D, 1), g_all, g_cols, w_ada[0], nb)
    part_w2 = _add_sibling(g_w2, got_w2, 176, "add_sibling_w2")

    def prow(a):
        a = a.reshape(1, -1)
        return jnp.concatenate([a, jnp.zeros((1, D - a.shape[1]), F32)], axis=1)

    g_qn = dwq.reshape(H, LANE)[:, :DH].sum(0)
    g_kn = dwk.reshape(H, LANE)[:, :DH].sum(0)
    spack = jnp.concatenate([
        dnorm1, dnorm2, dnorm3, prow(dqa), prow(dkva), prow(g_qn), prow(g_kn), prow(dwv),
        prow(dbs[:, :G].T), prow(pc_ctx), prow(lsum[0:1]), jnp.zeros((5, D), F32), dws.reshape(CH, D)],
        axis=0)
    recv_w2, small_all = _scatter_chips([part_w2], "scatter_chips", gather=spack)
    ssum = _sum_slots(small_all, 144, "sum_small")
    loss = ssum[10, 0] * (0.5 / D)
    gsum2 = _sum_direct(g_ffn2, recv_ffn2, 176, "sum_grads_ffn2")
    msum = _sum_direct(gmisc, recv_misc, 368, "sum_grads_misc")

    transposed = ("ffn1_w1", "ffn1_w3", "ffn2_w1", "ffn2_w3", "w_in", "w_uq")
    g_big = {
        "ffn1_w1": _sum_direct(g_w1, recv_w1, 176, "sum_grads_w1"),
        "ffn1_w3": _sum_direct(g_w3, recv_w3, 176, "sum_grads_w3"),
        "ffn1_w2": _sum_chips(part_w2, recv_w2, 176, "sum_grads_w2"),
        "ffn2_w1": gsum2[0:fsh], "ffn2_w3": gsum2[fsh:2 * fsh], "ffn2_w2": gsum2[2 * fsh:3 * fsh],
        "w_in": msum[0:180], "w_out": msum[192:320],
        "w_uq": msum[320:344].reshape(DH, QL), "w_ukv": msum[352:368].reshape(DN + DV, KVL).T,
        "w_ada": g_w_ada,
    }

    big_in = {
        "w_ada": (w_ada, m_w_ada, v_w_ada), "ffn1_w1": (ffn1_w1, m_ffn1_w1, v_ffn1_w1),
        "ffn1_w3": (ffn1_w3, m_ffn1_w3, v_ffn1_w3), "ffn1_w2": (ffn1_w2, m_ffn1_w2, v_ffn1_w2),
        "w_in": (w_in, m_w_in, v_w_in), "w_uq": (w_uq, m_w_uq, v_w_uq), "w_ukv": (w_ukv, m_w_ukv, v_w_ukv),
        "w_out": (w_out, m_w_out, v_w_out), "ffn2_w1": (ffn2_w1, m_ffn2_w1, v_ffn2_w1),
        "ffn2_w3": (ffn2_w3, m_ffn2_w3, v_ffn2_w3), "ffn2_w2": (ffn2_w2, m_ffn2_w2, v_ffn2_w2),
    }
    res = {}
    for nm, (w, m, v_) in big_in.items():
        g = g_big[nm]
        if nm in transposed:
            d_, m_, v2_ = _adamw(w[0].T, g, m[0].T, v_[0].T, "adamw_" + nm)
            res[nm] = tuple(a.T[None] for a in (g, d_, m_, v2_))
        else:
            d_, m_, v2_ = _adamw(w[0], g, m[0], v_[0], "adamw_" + nm)
            res[nm] = tuple(a[None] for a in (g, d_, m_, v2_))

    small_in = [
        ("c_ctx", c_ctx, m_c_ctx, v_c_ctx, ssum[9:10], (1, D)),
        ("b_ada", b_ada, m_b_ada, v_b_ada, g_b_ada, (1, NMOD * D)),
        ("norm1_w", norm1_w, m_norm1_w, v_norm1_w, ssum[0:1], (1, D)),
        ("norm2_w", norm2_w, m_norm2_w, v_norm2_w, ssum[1:2], (1, D)),
        ("norm3_w", norm3_w, m_norm3_w, v_norm3_w, ssum[2:3], (1, D)),
        ("q_a_norm_w", q_a_norm_w, m_q_a_norm_w, v_q_a_norm_w, ssum[3:4, :QL], (1, QL)),
        ("kv_a_norm_w", kv_a_norm_w, m_kv_a_norm_w, v_kv_a_norm_w, ssum[4:5, :KVL], (1, KVL)),
        ("q_norm_w", q_norm_w, m_q_norm_w, v_q_norm_w, ssum[5:6, :DH], (1, DH)),
        ("k_norm_w", k_norm_w, m_k_norm_w, v_k_norm_w, ssum[6:7, :DH], (1, DH)),
        ("v_norm_w", v_norm_w, m_v_norm_w, v_v_norm_w, ssum[7:8, :G * GD], (G, GD)),
        ("b_s", b_s, m_b_s, v_b_s, ssum[8:9], (G, CH)),
        ("w_s", w_s, m_w_s, v_w_s, ssum[16:144], (G * CH, CH)),
    ]
    small_out = _adamw_small(
        [(w.reshape(sh), g.reshape(sh), m.reshape(sh), v_.reshape(sh)) for _, w, m, v_, g, sh in small_in])
    for (nm, w, *_), outs in zip(small_in, small_out):
        res[nm] = tuple(a.reshape(w.shape) for a in outs)

    order = ["c_ctx", "w_ada", "b_ada", "norm1_w", "ffn1_w1", "ffn1_w3", "ffn1_w2", "norm2_w", "w_in", "q_a_norm_w",
             "w_uq", "kv_a_norm_w", "w_ukv", "q_norm_w", "k_norm_w", "v_norm_w", "w_s", "b_s", "w_out", "norm3_w",
             "ffn2_w1", "ffn2_w3", "ffn2_w2"]
    return (loss, grad_x, *[res[n][0] for n in order], *[res[n][1] for n in order],
            *[res[n][2] for n in order], *[res[n][3] for n in order])
```

```python
import numpy as np
import jax
import jax.numpy as jnp
from jax import lax
from jax.experimental import pallas as pl
from jax.experimental.pallas import tpu as pltpu

F32 = jnp.float32
BF16 = jnp.bfloat16

D = 1024
FF = 2816
FC = 256
H = 8
DN, DR, DV = 64, 32, 64
DH = DN + DR
QL, KVL = 256, 128
G, GD, CH = 8, 64, 128
NMOD = 9
EPS = 1e-6
GRID_W = 64
ROPE_BASE = 10000.0
NDEV = 8
LANE = 128
HP = H * LANE
IN_COLS = 1440
WIN_ROWS = 1536
KPE_LO = 1408 + DN
MIB = 1 << 20

ADAM_LR, ADAM_B1, ADAM_B2, ADAM_EPS, ADAM_WD, ADAM_STEP = 0.001, 0.9, 0.999, 1e-08, 0.01, 10

MESH = pl.DeviceIdType.MESH
ANY = pl.BlockSpec(memory_space=pl.ANY)
VMEM = pl.BlockSpec(memory_space=pltpu.VMEM)


def _mm(a, b):
    return jnp.dot(a, b, preferred_element_type=F32)


def _mm_nt(a, b):
    return lax.dot_general(a, b, (((1,), (1,)), ((), ())), preferred_element_type=F32)


def _mm_tn(a, b):
    return lax.dot_general(a, b, (((0,), (0,)), ((), ())), preferred_element_type=F32)


def _dot_hl(x, m):
    hi = x.astype(BF16)
    lo = (x - hi.astype(F32)).astype(BF16)
    return _mm(hi, m) + _mm(lo, m)


def _sigmoid(a):
    return 1.0 / (1.0 + jnp.exp(-a))


_G0 = 0.7978845608028654
_G1 = 0.044715


def _gelu(x):
    return 0.5 * x * (1.0 + jnp.tanh(_G0 * (x + _G1 * (x * x * x))))


def _gelu_grad(x):
    th = jnp.tanh(_G0 * (x + _G1 * (x * x * x)))
    return 0.5 * (1.0 + th) + 0.5 * x * (1.0 - th * th) * (_G0 * (1.0 + 3.0 * _G1 * x * x))


def _rowsum(y):
    return jnp.sum(y, axis=0, keepdims=True)


def _rms(x):
    return lax.rsqrt(jnp.mean(x * x, axis=-1, keepdims=True) + EPS)


def _pcall(body, *, name, out_shape, in_specs, out_specs, grid=None, scratch=(), vmem_mb=32, aliases=None):
    kw = {}
    if grid is not None:
        kw["grid"] = grid
        sem = ("arbitrary",) * len(grid)
    else:
        sem = None
    if aliases:
        kw["input_output_aliases"] = aliases
    return pl.pallas_call(
        body, name=name, out_shape=out_shape, in_specs=in_specs, out_specs=out_specs,
        scratch_shapes=list(scratch),
        compiler_params=pltpu.CompilerParams(dimension_semantics=sem, vmem_limit_bytes=vmem_mb * MIB),
        **kw)


def _const(shape):
    nd = len(shape)
    return pl.BlockSpec(shape, lambda *_: (0,) * nd)


def _sds(shape, dt):
    return jax.ShapeDtypeStruct(shape, dt)


def _consts():
    seg_h = np.zeros((HP, LANE), np.float32)
    seg_h[np.arange(HP), np.arange(HP) // LANE] = 1.0
    seg_g = np.zeros((G * GD, LANE), np.float32)
    seg_g[np.arange(G * GD), np.arange(G * GD) // GD] = 1.0
    rot = np.zeros((LANE, LANE), np.float32)
    for base in (DN, DN + 16):
        for j in range(8):
            rot[base + j + 8, base + j] = -1.0
            rot[base + j, base + j + 8] = 1.0
    rot2 = np.zeros((2 * LANE, 2 * LANE), np.float32)
    rot2[:LANE, :LANE] = rot
    rot2[LANE:, LANE:] = rot
    twice = lambda m: np.concatenate([m, m], axis=0)
    c = dict(seg_h=seg_h, seg_ht=twice(seg_h.T), seg_g=seg_g, seg_gt=twice(seg_g.T), rot=rot2, rot_t=rot2.T)
    return {k: jnp.asarray(v, BF16) for k, v in c.items()}


_GATHER_SEMS = [pltpu.SemaphoreType.DMA((7,)), pltpu.SemaphoreType.DMA((7,)), pltpu.SemaphoreType.DMA(())]


def _gather_phases(x_ref, out_ref, send_sems, recv_sems, local_sem):
    mx, my, mc = lax.axis_index("x"), lax.axis_index("y"), lax.axis_index("c")
    me, sibling = (mx, my, mc), (mx, my, 1 - mc)
    chips = [(1 - mx, my), (mx, 1 - my), (1 - mx, 1 - my)]

    def blk(px, py, pc):
        return out_ref.at[4 * px + 2 * py + pc]

    def copy(k, block, to, src=None):
        return pltpu.make_async_remote_copy(
            src_ref=blk(*block) if src is None else src, dst_ref=blk(*block),
            send_sem=send_sems.at[k], recv_sem=recv_sems.at[k], device_id=to, device_id_type=MESH)

    mine = pltpu.make_async_copy(x_ref, blk(*me), local_sem)
    first = [copy(0, me, sibling, src=x_ref)]
    first += [copy(1 + j, me, (*chip, mc), src=x_ref) for j, chip in enumerate(chips)]
    passed = [copy(4 + j, (*chip, mc), sibling) for j, chip in enumerate(chips)]

    def start():
        mine.start()
        for cp in first:
            cp.start()

    def forward():
        for j, chip in enumerate(chips):
            copy(1 + j, (*chip, mc), me).wait_recv()
            passed[j].start()

    def finish():
        copy(0, sibling, me).wait_recv()
        for j, chip in enumerate(chips):
            copy(4 + j, (*chip, 1 - mc), me).wait_recv()
        for cp in first + passed:
            cp.wait_send()
        mine.wait()

    return start, forward, finish


def _chip_sends(p_ref, out_ref, send_sems, recv_sems):
    mx, my, mc = lax.axis_index("x"), lax.axis_index("y"), lax.axis_index("c")
    peers = [(1 - mx, my), (mx, 1 - my), (1 - mx, 1 - my)]
    return [pltpu.make_async_remote_copy(
        src_ref=p_ref.at[2 * px + py], dst_ref=out_ref.at[j], send_sem=send_sems.at[j], recv_sem=recv_sems.at[j],
        device_id=(px, py, mc), device_id_type=MESH) for j, (px, py) in enumerate(peers)]


def _with_gather(copies_of, n, shapes, sems, gather, name, args):
    ns = len(sems)

    def body(*refs):
        ng = 1 if gather is not None else 0
        ins, outs = refs[:n], refs[n + ng:2 * n + ng]
        copies = copies_of(ins, outs, refs[2 * n + 2 * ng:2 * n + 2 * ng + ns])
        if ng:
            start, forward, finish = _gather_phases(refs[n], refs[2 * n + 1], *refs[2 * n + 2 + ns:])
            start()
        for cp in copies:
            cp.start()
        if ng:
            forward()
        for cp in copies:
            cp.wait_recv()
        for cp in copies:
            cp.wait_send()
        if ng:
            finish()

    in_specs, out_shape, scratch = [ANY] * n, list(shapes), list(sems)
    if gather is not None:
        in_specs.append(ANY)
        args = list(args) + [gather]
        out_shape.append(_sds((NDEV,) + gather.shape, gather.dtype))
        scratch += _GATHER_SEMS
    return pl.pallas_call(body, name=name, out_shape=tuple(out_shape), in_specs=in_specs,
                          out_specs=(ANY,) * len(out_shape), scratch_shapes=scratch)(*args)


def _scatter_sibling(xs, name, gather=None):
    n = len(xs)

    def copies_of(x_refs, got_refs, sems):
        send_sems, recv_sems = sems
        mx, my, mc = lax.axis_index("x"), lax.axis_index("y"), lax.axis_index("c")
        return [pltpu.make_async_remote_copy(
            src_ref=x_refs[i].at[2 * j + 1 - mc], dst_ref=got_refs[i].at[j],
            send_sem=send_sems.at[4 * i + j], recv_sem=recv_sems.at[4 * i + j],
            device_id=(mx, my, 1 - mc), device_id_type=MESH) for i in range(n) for j in range(4)]

    shapes = tuple(_sds((4,) + x.shape[1:], x.dtype) for x in xs)
    return _with_gather(copies_of, n, shapes, [pltpu.SemaphoreType.DMA((4 * n,))] * 2, gather, name, xs)


def _scatter_chips(ps, name, gather=None):
    n = len(ps)

    def copies_of(p_refs, out_refs, sems):
        sends = []
        for i in range(n):
            sends += _chip_sends(p_refs[i], out_refs[i], sems[2 * i], sems[2 * i + 1])
        return sends

    shapes = tuple(_sds((3,) + p.shape[1:], p.dtype) for p in ps)
    return _with_gather(copies_of, n, shapes, [pltpu.SemaphoreType.DMA((3,))] * (2 * n), gather, name, ps)


def _add_sibling(x, got, tr, name):
    _, r, c = x.shape

    def body(x_ref, g_ref, o_ref):
        mc = lax.axis_index("c")
        for j in range(4):
            mine = jnp.where(mc == 0, x_ref[2 * j].astype(F32), x_ref[2 * j + 1].astype(F32))
            o_ref[j] = (mine + g_ref[j].astype(F32)).astype(o_ref.dtype)

    return _pcall(body, name=name, grid=(r // tr,), out_shape=_sds(got.shape, got.dtype),
                  in_specs=[pl.BlockSpec((NDEV, tr, c), lambda t: (0, t, 0)), pl.BlockSpec((4, tr, c), lambda t: (0, t, 0))],
                  out_specs=pl.BlockSpec((4, tr, c), lambda t: (0, t, 0)))(x, got)


def _sum_chips(part, recv, tr, name):
    _, r, c = part.shape

    def body(p_ref, r_ref, o_ref):
        slot = 2 * lax.axis_index("x") + lax.axis_index("y")
        acc = p_ref[0].astype(F32)
        for j in range(1, 4):
            acc = jnp.where(slot == j, p_ref[j].astype(F32), acc)
        for j in range(3):
            acc = acc + r_ref[j].astype(F32)
        o_ref[...] = acc

    return _pcall(body, name=name, grid=(r // tr,), out_shape=_sds((r, c), F32),
                  in_specs=[pl.BlockSpec((4, tr, c), lambda t: (0, t, 0)), pl.BlockSpec((3, tr, c), lambda t: (0, t, 0))],
                  out_specs=pl.BlockSpec((tr, c), lambda t: (t, 0)))(part, recv)


def _sum_slots(x, tr, name):
    n, r, c = x.shape

    def body(x_ref, o_ref):
        acc = x_ref[0].astype(F32)
        for s in range(1, n):
            acc = acc + x_ref[s].astype(F32)
        o_ref[...] = acc

    return _pcall(body, name=name, grid=(r // tr,), out_shape=_sds((r, c), F32),
                  in_specs=[pl.BlockSpec((n, tr, c), lambda t: (0, t, 0))],
                  out_specs=pl.BlockSpec((tr, c), lambda t: (t, 0)))(x)


def _ada_front(a_loc, w_loc, b_loc, wpack):
    ncol = w_loc.shape[1]
    nrow = NDEV * a_loc.shape[0]

    def body(a_ref, w_ref, b_ref, wp_ref, araw_ref, mloc_ref, mall_ref, wall_ref,
             a_vm, w_vm, m_vm, lsem, *sems):
        a_start, a_forward, a_finish = _gather_phases(a_ref, araw_ref, *sems[0:3])
        m_start, m_forward, m_finish = _gather_phases(mloc_ref, mall_ref, *sems[3:6])
        w_start, w_forward, w_finish = _gather_phases(wp_ref, wall_ref, *sems[6:9])
        w_in = pltpu.make_async_copy(w_ref, w_vm, lsem.at[0])
        w_in.start()
        a_start()
        w_start()
        a_forward()
        a_finish()
        a_in = pltpu.make_async_copy(araw_ref, a_vm, lsem.at[1])
        a_in.start()
        a_in.wait()
        w_in.wait()
        a = a_vm[...].reshape(nrow, D)
        act = (a * _sigmoid(a)).astype(BF16)
        m_vm[...] = _mm(act, w_vm[...].astype(BF16)) + b_ref[...]
        m_out = pltpu.make_async_copy(m_vm, mloc_ref, lsem.at[2])
        m_out.start()
        m_out.wait()
        m_start()
        m_forward()
        m_finish()
        w_forward()
        w_finish()

    return pl.pallas_call(
        body, name="ada_front",
        out_shape=(_sds((NDEV,) + a_loc.shape, F32), _sds((nrow, ncol), F32), _sds((NDEV, nrow, ncol), F32),
                   _sds((NDEV,) + wpack.shape, wpack.dtype)),
        in_specs=[ANY, ANY, VMEM, ANY], out_specs=(ANY, ANY, ANY, ANY),
        scratch_shapes=[pltpu.VMEM((NDEV,) + a_loc.shape, F32), pltpu.VMEM(w_loc.shape, F32),
                        pltpu.VMEM((nrow, ncol), F32), pltpu.SemaphoreType.DMA((3,))] + _GATHER_SEMS * 3,
        compiler_params=pltpu.CompilerParams(vmem_limit_bytes=32 * MIB),
    )(a_loc, w_loc, b_loc, wpack)


def _ada_bwd(a_raw, cctx_col, g_all, g_cols, w_loc, nb):
    nrow = a_raw.shape[0]
    ncol = w_loc.shape[1]

    def body(a_ref, cc_ref, gall_ref, g_ref, w_ref, dw_ref, pc_ref, gb_ref):
        a = a_ref[...]
        rowid = lax.broadcasted_iota(jnp.int32, (nrow, 1), 0) % 8
        act = jnp.where(rowid < nb, a * _sigmoid(a), 0.0).astype(BF16)
        g = g_ref[...]
        gc = _rowsum(jnp.where(rowid == nb, g, 0.0))
        cc = cc_ref[...]
        dw_ref[...] = _mm_tn(act, g.astype(BF16)) + (cc * _sigmoid(cc)) * gc
        pc_ref[...] = jnp.sum(w_ref[...] * gc, axis=1, keepdims=True)
        gb_ref[...] = _rowsum(gall_ref[...])

    return _pcall(body, name="ada_bwd",
                  out_shape=(_sds((D, ncol), F32), _sds((D, 1), F32), _sds((1, g_all.shape[1]), F32)),
                  in_specs=[VMEM] * 5, out_specs=(VMEM,) * 3, vmem_mb=48)(a_raw, cctx_col, g_all, g_cols, w_loc)


def _mod_spec(k, tpe, nrows):
    return pl.BlockSpec((1, k, D), lambda t: (jnp.minimum(t // tpe, nrows - 1), 0, 0))


def _load_ffn_weights(wall_ref, first, bufs, sems):
    fsh = FF // NDEV
    cps = []
    for j, buf in enumerate(bufs):
        for d in range(NDEV):
            cps.append(pltpu.make_async_copy(wall_ref.at[d, pl.ds((first + j) * fsh, fsh)],
                                             buf.at[pl.ds(d * fsh, fsh)], sems.at[j * NDEV + d]))
    for cp in cps:
        cp.start()
    for cp in cps:
        cp.wait()


def _token_specs(xs, tm, n_lat):
    specs = [pl.BlockSpec((tm, D), lambda t: (jnp.minimum(t, n_lat - 1), 0))]
    if len(xs) == 2:
        specs.append(pl.BlockSpec((tm, D), lambda t: (jnp.maximum(t - n_lat, 0), 0)))
    return specs


def _ffn_fwd(xs, mod3, norm_w, wall, first, *, tm, n_tiles, tpe, n_lat, name, target=None, gather=None):
    nrows = mod3.shape[0]
    r = n_tiles * tm
    nx = len(xs)
    with_loss = target is not None
    with_gather = gather is not None
    fwd_step = max(2 * n_tiles // 3, 1)

    def body(*refs):
        x_refs = refs[:nx]
        pos = nx
        if with_loss:
            tgt_ref = refs[pos]
            pos += 1
        mod_ref, nw_ref, wall_ref = refs[pos:pos + 3]
        pos += 3
        if with_gather:
            gin_ref = refs[pos]
            pos += 1
        xo_ref, a_ref, b_ref, o_ref = refs[pos:pos + 4]
        pos += 4
        if with_loss:
            ls_ref = refs[pos]
            pos += 1
        if with_gather:
            gout_ref = refs[pos]
            pos += 1
        w1_ref, w3_ref, w2_ref, wsem, acc_ref = refs[pos:pos + 5]
        t = pl.program_id(0)
        if with_gather:
            g_start, g_forward, g_finish = _gather_phases(gin_ref, gout_ref, *refs[pos + 5:])

        @pl.when(t == 0)
        def _():
            if with_gather:
                g_start()
            _load_ffn_weights(wall_ref, first, (w1_ref, w3_ref, w2_ref), wsem)
            if with_loss:
                ls_ref[...] = jnp.zeros_like(ls_ref)

        if with_gather:
            @pl.when(t == fwd_step)
            def _():
                g_forward()

            @pl.when(t == n_tiles - 1)
            def _():
                g_finish()

        x = x_refs[0][...]
        if nx == 2:
            x = jnp.where(t < n_lat, x, x_refs[1][...])
        n = x * _rms(x) * nw_ref[...]
        shift, scale, gate = mod_ref[0, 0:1, :], mod_ref[0, 1:2, :], mod_ref[0, 2:3, :]
        h = (n * (1.0 + scale) + shift).astype(BF16)
        nch = FF // FC
        o = None
        for lo_c, hi_c in ((0, nch // 2), (nch // 2, nch)):
            for j in range(lo_c, hi_c):
                sl = slice(j * FC, (j + 1) * FC)
                a = _mm_nt(h, w1_ref[sl, :])
                b = _mm_nt(h, w3_ref[sl, :])
                a_ref[:, sl] = a.astype(BF16)
                b_ref[:, sl] = b.astype(BF16)
                acc_ref[:, sl] = (a * _sigmoid(a) * b).astype(BF16)
            gs = slice(lo_c * FC, hi_c * FC)
            part = _mm(acc_ref[:, gs], w2_ref[gs, :])
            o = part if o is None else o + part
        o_ref[...] = o.astype(BF16)
        out = x + (0.5 * gate) * o
        if with_loss:
            d = out - tgt_ref[...]
            xo_ref[...] = d * (1.0 / D)
            ls_ref[...] += jnp.sum(d * d)
        else:
            xo_ref[...] = out

    row = lambda cols: pl.BlockSpec((tm, cols), lambda t: (t, 0))
    in_specs = _token_specs(xs, tm, n_lat) + ([row(D)] if with_loss else []) + [
        _mod_spec(3, tpe, nrows), _const((1, D)), ANY]
    out_shape = [_sds((r, D), F32), _sds((r, FF), BF16), _sds((r, FF), BF16), _sds((r, D), BF16)]
    out_specs = [row(D), row(FF), row(FF), row(D)]
    scratch = [pltpu.VMEM((FF, D), BF16)] * 3 + [pltpu.SemaphoreType.DMA((3 * NDEV,)), pltpu.VMEM((tm, FF), BF16)]
    if with_loss:
        out_shape.append(_sds((8, LANE), F32))
        out_specs.append(_const((8, LANE)))
    args = list(xs) + ([target] if with_loss else []) + [mod3, norm_w, wall]
    if with_gather:
        assert n_tiles >= 2
        in_specs.append(ANY)
        args.append(gather)
        out_shape.append(_sds((NDEV,) + gather.shape, gather.dtype))
        out_specs.append(ANY)
        scratch += _GATHER_SEMS
    return _pcall(
        body, name=name, grid=(n_tiles,), out_shape=tuple(out_shape), in_specs=in_specs, out_specs=tuple(out_specs),
        scratch=scratch, vmem_mb=56)(*args)


def _ffn_bwd_dx(dout, xs, a, b, o, mod3, norm_w, wall, first, *, tm, n_tiles, tpe, n_lat, name):
    nrows = mod3.shape[0]
    r = n_tiles * tm
    nx = len(xs)

    def body(*refs):
        dout_ref = refs[0]
        x_refs = refs[1:1 + nx]
        (a_ref, b_ref, o_ref, mod_ref, nw_ref, wall_ref,
         dx_ref, da_ref, db_ref, g_ref, do_ref, h_ref, dmod_ref, dnw_ref,
         w1_ref, w3_ref, w2_ref, wsem) = refs[1 + nx:]
        t = pl.program_id(0)

        @pl.when(t == 0)
        def _():
            _load_ffn_weights(wall_ref, first, (w1_ref, w3_ref, w2_ref), wsem)
            dnw_ref[...] = jnp.zeros_like(dnw_ref)

        @pl.when(jnp.where(t < n_lat, t % tpe == 0, t == n_lat))
        def _():
            dmod_ref[...] = jnp.zeros_like(dmod_ref)

        x = x_refs[0][...]
        if nx == 2:
            x = jnp.where(t < n_lat, x, x_refs[1][...])
        dout = dout_ref[...]
        shift, scale, gate = mod_ref[0, 0:1, :], mod_ref[0, 1:2, :], mod_ref[0, 2:3, :]
        d_o = ((0.5 * gate) * dout).astype(BF16)
        do_ref[...] = d_o
        nch = FF // FC
        groups = ((0, 4), (4, 8), (8, nch))
        dh = None
        for lo_c, hi_c in groups:
            for j in range(lo_c, hi_c):
                sl = slice(j * FC, (j + 1) * FC)
                av = a_ref[:, sl].astype(F32)
                bv = b_ref[:, sl].astype(F32)
                dg = _mm_nt(d_o, w2_ref[sl, :])
                sig = _sigmoid(av)
                sa = av * sig
                g_ref[:, sl] = (sa * bv).astype(BF16)
                da_ref[:, sl] = (dg * bv * (sig * (1.0 + av * (1.0 - sig)))).astype(BF16)
                db_ref[:, sl] = (dg * sa).astype(BF16)
            gs = slice(lo_c * FC, hi_c * FC)
            part = _mm(da_ref[:, gs], w1_ref[gs, :]) + _mm(db_ref[:, gs], w3_ref[gs, :])
            dh = part if dh is None else dh + part
        rr = _rms(x)
        xh = x * rr
        nw = nw_ref[...]
        n = xh * nw
        h_ref[...] = (n * (1.0 + scale) + shift).astype(BF16)
        dgate = _rowsum(0.5 * o_ref[...].astype(F32) * dout)
        dn = dh * (1.0 + scale)
        dxh = dn * nw
        dmod_ref[0, 0:1, :] += _rowsum(dh)
        dmod_ref[0, 1:2, :] += _rowsum(dh * n)
        dmod_ref[0, 2:3, :] += dgate
        dnw_ref[...] += _rowsum(dn * xh)
        dx = dout + rr * (dxh - xh * jnp.mean(dxh * xh, axis=-1, keepdims=True))
        if n_tiles == n_lat:
            dx_ref[...] = dx
        else:
            @pl.when(t < n_lat)
            def _():
                dx_ref[...] = dx

    row = lambda cols: pl.BlockSpec((tm, cols), lambda t: (t, 0))
    lat = pl.BlockSpec((tm, D), lambda t: (jnp.minimum(t, n_lat - 1), 0))
    out_shape = [_sds((n_lat * tm, D), F32), _sds((r, FF), BF16), _sds((r, FF), BF16), _sds((r, FF), BF16),
                 _sds((r, D), BF16), _sds((r, D), BF16), _sds((nrows, 3, D), F32), _sds((1, D), F32)]
    in_specs = [row(D)] + _token_specs(xs, tm, n_lat) + [row(FF), row(FF), row(D), _mod_spec(3, tpe, nrows),
                                                          _const((1, D)), ANY]
    out_specs = [lat, row(FF), row(FF), row(FF), row(D), row(D), _mod_spec(3, tpe, nrows), _const((1, D))]
    scratch = [pltpu.VMEM((FF, D), BF16)] * 3 + [pltpu.SemaphoreType.DMA((3 * NDEV,))]
    args = [dout, *xs, a, b, o, mod3, norm_w, wall]
    return _pcall(body, name=name, grid=(n_tiles,), out_shape=tuple(out_shape), in_specs=in_specs,
                  out_specs=tuple(out_specs), scratch=scratch, vmem_mb=60)(*args)


def _ffn_bwd_dw(h, d_o, da, db, g, *, tr, name):
    r = h.shape[0]
    fh = FF // 2
    fsh = FF // NDEV
    nk = r // tr

    def body(h_ref, do_ref, da_ref, db_ref, g_ref, out_ref, acc1, acc3, acc2):
        k = pl.program_id(1)

        @pl.when(k == 0)
        def _():
            acc1[...] = jnp.zeros_like(acc1)
            acc3[...] = jnp.zeros_like(acc3)
            acc2[...] = jnp.zeros_like(acc2)

        hv = h_ref[...]
        acc1[...] += _mm_tn(da_ref[...], hv)
        acc3[...] += _mm_tn(db_ref[...], hv)
        acc2[...] += _mm_tn(g_ref[...], do_ref[...])

        @pl.when(k == nk - 1)
        def _():
            for i, acc in enumerate((acc1, acc3, acc2)):
                out_ref[:, i * fsh:(i + 1) * fsh, :] = acc[...].reshape(NDEV // 2, fsh, D).astype(BF16)

    rowd = pl.BlockSpec((tr, D), lambda f, k: (k, 0))
    rowf = pl.BlockSpec((tr, fh), lambda f, k: (k, f))
    return _pcall(
        body, name=name, grid=(2, nk), out_shape=_sds((NDEV, 3 * fsh, D), BF16),
        in_specs=[rowd, rowd, rowf, rowf, rowf],
        out_specs=pl.BlockSpec((NDEV // 2, 3 * fsh, D), lambda f, k: (f, 0, 0)),
        scratch=[pltpu.VMEM((fh, D), F32)] * 3, vmem_mb=56)(h, d_o, da, db, g)


def _direct_sends(x_ref, out_ref, send_sems, recv_sems):
    mx, my, mc = lax.axis_index("x"), lax.axis_index("y"), lax.axis_index("c")
    sends = []
    for k in range(1, NDEV):
        px = 1 - mx if (k & 4) else mx
        py = 1 - my if (k & 2) else my
        pc = 1 - mc if (k & 1) else mc
        sends.append(pltpu.make_async_remote_copy(
            src_ref=x_ref.at[4 * px + 2 * py + pc], dst_ref=out_ref.at[k - 1],
            send_sem=send_sems.at[k - 1], recv_sem=recv_sems.at[k - 1], device_id=(px, py, pc), device_id_type=MESH))
    return sends


def _sum_direct(x, recv, tr, name):
    _, r, c = x.shape

    def body(x_ref, r_ref, o_ref):
        me = 4 * lax.axis_index("x") + 2 * lax.axis_index("y") + lax.axis_index("c")
        acc = x_ref[0].astype(F32)
        for j in range(1, NDEV):
            acc = jnp.where(me == j, x_ref[j].astype(F32), acc)
        for j in range(NDEV - 1):
            acc = acc + r_ref[j].astype(F32)
        o_ref[...] = acc

    return _pcall(body, name=name, grid=(r // tr,), out_shape=_sds((r, c), F32),
                  in_specs=[pl.BlockSpec((NDEV, tr, c), lambda t: (0, t, 0)),
                            pl.BlockSpec((NDEV - 1, tr, c), lambda t: (0, t, 0))],
                  out_specs=pl.BlockSpec((tr, c), lambda t: (t, 0)))(x, recv)


def _exchange_behind(x_ref, recv_ref, send_sems, recv_sems, first, last):
    sends = _direct_sends(x_ref, recv_ref, send_sems, recv_sems)

    @pl.when(first)
    def _():
        for cp in sends:
            cp.start()

    @pl.when(last)
    def _():
        for cp in sends:
            cp.wait_recv()
        for cp in sends:
            cp.wait_send()


def _ffn_bwd_dw_one(lhs, rhs, *, tr, name, part=None):
    r = lhs.shape[0]
    fsh = FF // NDEV
    nk = r // tr
    fused = part is not None
    nslot = NDEV - 1

    def body(*refs):
        if fused:
            lhs_ref, rhs_ref, part_ref, out_ref, recv_ref, acc, send_sems, recv_sems = refs
        else:
            lhs_ref, rhs_ref, out_ref, acc = refs
        k = pl.program_id(0)
        if fused:
            _exchange_behind(part_ref, recv_ref, send_sems, recv_sems, k == 0, k == nk - 1)

        @pl.when(k == 0)
        def _():
            acc[...] = jnp.zeros_like(acc)

        acc[...] += _mm_tn(lhs_ref[...], rhs_ref[...])

        @pl.when(k == nk - 1)
        def _():
            out_ref[...] = acc[...].reshape(NDEV, fsh, D).astype(BF16)

    in_specs = [pl.BlockSpec((tr, FF), lambda k: (k, 0)), pl.BlockSpec((tr, D), lambda k: (k, 0))]
    out_shape = [_sds((NDEV, fsh, D), BF16)]
    out_specs = [_const((NDEV, fsh, D))]
    scratch = [pltpu.VMEM((FF, D), F32)]
    args = [lhs, rhs]
    if fused:
        in_specs.append(ANY)
        args.append(part)
        out_shape.append(_sds((nslot,) + part.shape[1:], part.dtype))
        out_specs.append(ANY)
        scratch += [pltpu.SemaphoreType.DMA((nslot,))] * 2
    res = _pcall(body, name=name, grid=(nk,), out_shape=tuple(out_shape), in_specs=in_specs,
                 out_specs=tuple(out_specs), scratch=scratch, vmem_mb=48)(*args)
    return res if fused else res[0]


_PIECES =((0, 128), (128, 384), (384, 896), (896, 1408), (1408, 1536))


def _proj_fwd(x1, mod2, norm_w, wint, *, tm, n_tiles, tpe, name="proj_fwd"):
    nrows = mod2.shape[0]
    r = n_tiles * tm

    def body(x_ref, mod_ref, nw_ref, w_ref, ckv_ref, q_ref, u_ref, v_ref, kpe_ref):
        x = x_ref[...]
        n = x * _rms(x) * nw_ref[...]
        h = (n * (1.0 + mod_ref[0, 1:2, :]) + mod_ref[0, 0:1, :]).astype(BF16)
        proj = _mm_nt(h, w_ref[...])
        for (lo, hi), ref in zip(_PIECES, (ckv_ref, q_ref, u_ref, v_ref, kpe_ref)):
            ref[...] = proj[:, lo:hi]

    row = lambda cols: pl.BlockSpec((tm, cols), lambda t: (t, 0))
    widths = [hi - lo for lo, hi in _PIECES]
    return _pcall(
        body, name=name, grid=(n_tiles,),
        out_shape=tuple(_sds((r, w), F32) for w in widths),
        in_specs=[row(D), _mod_spec(2, tpe, nrows), _const((1, D)), _const((WIN_ROWS, D))],
        out_specs=tuple(row(w) for w in widths), vmem_mb=40)(x1, mod2, norm_w, wint)


def _proj_bwd(dckv, dkpe, dq, du, dv, dx2, x1, mod2, norm_w, wint, *, tm, n_tiles, tpe, n_lat, name="proj_bwd"):
    nrows = mod2.shape[0]
    r = n_tiles * tm

    def body(dckv_ref, dkpe_ref, dq_ref, du_ref, dv_ref, dx2_ref, x_ref, mod_ref, nw_ref, w_ref,
             dx_ref, dw_ref, dmod_ref, dnw_ref):
        t = pl.program_id(0)
        is_lat = t < n_lat

        @pl.when(t == 0)
        def _():
            dw_ref[...] = jnp.zeros_like(dw_ref)
            dnw_ref[...] = jnp.zeros_like(dnw_ref)

        @pl.when(jnp.where(is_lat, t % tpe == 0, t == n_lat))
        def _():
            dmod_ref[...] = jnp.zeros_like(dmod_ref)

        x = x_ref[...]
        rr = _rms(x)
        xh = x * rr
        nw = nw_ref[...]
        n = xh * nw
        scale = mod_ref[0, 1:2, :]
        h = (n * (1.0 + scale) + mod_ref[0, 0:1, :]).astype(BF16)
        zero = jnp.zeros((), BF16)
        pieces = (dckv_ref[...], jnp.where(is_lat, dq_ref[...], zero), jnp.where(is_lat, du_ref[...], zero),
                  jnp.where(is_lat, dv_ref[...], zero), dkpe_ref[...])
        dproj = jnp.concatenate(pieces, axis=-1)
        dh = _mm(dproj, w_ref[...])
        dn = dh * (1.0 + scale)
        dxh = dn * nw
        dx = rr * (dxh - xh * jnp.mean(dxh * xh, axis=-1, keepdims=True))
        dx_ref[...] = dx + jnp.where(is_lat, dx2_ref[...], 0.0)
        dmod_ref[0, 0:1, :] += _rowsum(dh)
        dmod_ref[0, 1:2, :] += _rowsum(dh * n)
        dnw_ref[...] += _rowsum(dn * xh)
        dw_ref[...] += _mm_tn(dproj, h)

    row = lambda cols: pl.BlockSpec((tm, cols), lambda t: (t, 0))
    lat = lambda cols: pl.BlockSpec((tm, cols), lambda t: (jnp.minimum(t, n_lat - 1), 0))
    return _pcall(
        body, name=name, grid=(n_tiles,),
        out_shape=(_sds((r, D), F32), _sds((WIN_ROWS, D), F32), _sds((nrows, 2, D), F32), _sds((1, D), F32)),
        in_specs=[row(128), row(128), lat(256), lat(512), lat(512), lat(D), row(D), _mod_spec(2, tpe, nrows),
                  _const((1, D)), _const((WIN_ROWS, D))],
        out_specs=(row(D), _const((WIN_ROWS, D)), _mod_spec(2, tpe, nrows), _const((1, D))),
        vmem_mb=48)(dckv, dkpe, dq, du, dv, dx2, x1, mod2, norm_w, wint)


def _seg_sum(x, seg):
    return _mm(x.astype(BF16), seg)


def _seg_bcast(v, segt2):
    hi = v.astype(BF16)
    lo = (v - hi.astype(F32)).astype(BF16)
    return _mm(jnp.concatenate([hi, lo], axis=-1), segt2)


def _rope_pairs(t, cos, sin, rot2):
    cos2, sin2 = jnp.concatenate([cos, cos], axis=-1), jnp.concatenate([sin, sin], axis=-1)
    out = []
    for j in range(H // 2):
        tj = t[:, 2 * j * LANE:2 * (j + 1) * LANE]
        out.append(tj * cos2 + _dot_hl(tj, rot2) * sin2)
    return jnp.concatenate(out, axis=-1)


def _head_norm_rope(x, w_pad, cos, sin, seg, segt2, rot2, rope=True):
    rh = lax.rsqrt(_seg_sum(x * x, seg) * (1.0 / DH) + EPS)
    rb = _seg_bcast(rh, segt2)
    y = x * rb
    out = _rope_pairs(y * w_pad, cos, sin, rot2) if rope else None
    return out, y, rb


def _head_norm_rope_bwd(dout, y, rb, w_pad, cos, sin, seg, segt2, rot2_t):
    cos2, sin2 = jnp.concatenate([cos, cos], axis=-1), jnp.concatenate([sin, sin], axis=-1)
    dt = []
    for j in range(H // 2):
        dj = dout[:, 2 * j * LANE:2 * (j + 1) * LANE]
        dt.append(dj * cos2 + _dot_hl(dj * sin2, rot2_t))
    dt = jnp.concatenate(dt, axis=-1)
    dw = _rowsum(dt * y)
    dy = dt * w_pad
    mean_h = _seg_sum(dy * y, seg) * (1.0 / DH)
    return rb * (dy - y * _seg_bcast(mean_h, segt2)), dw


def _q_prep_fwd(qp, qa_w, wuq, wq, cos, sin, cs, *, tm, n_lat, tpe):
    def body(qp_ref, qa_ref, wuq_ref, wq_ref, cos_ref, sin_ref, seg, segt, rot, q_ref):
        x = qp_ref[...]
        cq = (x * _rms(x) * qa_ref[...]).astype(BF16)
        q, _, _ = _head_norm_rope(_mm_nt(cq, wuq_ref[...]), wq_ref[...], cos_ref[...], sin_ref[...],
                                  seg[...], segt[...], rot[...])
        q_ref[...] = q.astype(BF16)

    row = lambda cols: pl.BlockSpec((tm, cols), lambda t: (t, 0))
    tab = pl.BlockSpec((tm, LANE), lambda t: (t % tpe, 0))
    return _pcall(
        body, name="q_prep_fwd", grid=(n_lat,), out_shape=_sds((n_lat * tm, HP), BF16),
        in_specs=[row(QL), _const((1, QL)), _const((HP, QL)), _const((1, HP)), tab, tab,
                  _const((HP, LANE)), _const((2 * LANE, HP)), _const((2 * LANE, 2 * LANE))],
        out_specs=row(HP))(qp, qa_w, wuq, wq, cos, sin, cs["seg_h"], cs["seg_ht"], cs["rot"])


def _q_prep_bwd(dq, qp, qa_w, wuq, wq, cos, sin, cs, *, tm, n_lat, tpe):
    def body(dq_ref, qp_ref, qa_ref, wuq_ref, wq_ref, cos_ref, sin_ref, seg, segt, rot, rot_t,
             dqp_ref, dwuq_ref, dqa_ref, dwq_ref):
        @pl.when(pl.program_id(0) == 0)
        def _():
            dwuq_ref[...] = jnp.zeros_like(dwuq_ref)
            dqa_ref[...] = jnp.zeros_like(dqa_ref)
            dwq_ref[...] = jnp.zeros_like(dwq_ref)

        x = qp_ref[...]
        ra = _rms(x)
        xh = x * ra
        qa = qa_ref[...]
        cq = (xh * qa).astype(BF16)
        wuq_v = wuq_ref[...]
        wq_v, cos_v, sin_v = wq_ref[...], cos_ref[...], sin_ref[...]
        _, y, rb = _head_norm_rope(_mm_nt(cq, wuq_v), wq_v, cos_v, sin_v, seg[...], segt[...], rot[...], rope=False)
        dqraw, dwq = _head_norm_rope_bwd(dq_ref[...], y, rb, wq_v, cos_v, sin_v, seg[...], segt[...], rot_t[...])
        dqraw = dqraw.astype(BF16)
        dcq = _mm(dqraw, wuq_v)
        dxh = dcq * qa
        dqp_ref[...] = (ra * (dxh - xh * jnp.mean(dxh * xh, axis=-1, keepdims=True))).astype(BF16)
        dwuq_ref[...] += _mm_tn(dqraw, cq)
        dqa_ref[...] += _rowsum(dcq * xh)
        dwq_ref[...] += dwq

    row = lambda cols: pl.BlockSpec((tm, cols), lambda t: (t, 0))
    tab = pl.BlockSpec((tm, LANE), lambda t: (t % tpe, 0))
    return _pcall(
        body, name="q_prep_bwd", grid=(n_lat,),
        out_shape=(_sds((n_lat * tm, QL), BF16), _sds((HP, QL), F32), _sds((1, QL), F32), _sds((1, HP), F32)),
        in_specs=[row(HP), row(QL), _const((1, QL)), _const((HP, QL)), _const((1, HP)), tab, tab,
                  _const((HP, LANE)), _const((2 * LANE, HP)), _const((2 * LANE, 2 * LANE)), _const((2 * LANE, 2 * LANE))],
        out_specs=(row(QL), _const((HP, QL)), _const((1, QL)), _const((1, HP))), vmem_mb=40)(
            dq, qp, qa_w, wuq, wq, cos, sin, cs["seg_h"], cs["seg_ht"], cs["rot"], cs["rot_t"])


def _kv_tab_spec(tm, tpe, n_lat):
    return pl.BlockSpec((tm, LANE), lambda t: (jnp.where(t < n_lat, t % tpe, tpe), 0))


def _split_kv(kv, kpe):
    low = lax.broadcasted_iota(jnp.int32, (kv.shape[0], LANE), 1) < DN
    kx, v = [], []
    for h in range(H):
        blk = kv[:, h * LANE:(h + 1) * LANE]
        kx.append(jnp.where(low, blk, kpe))
        v.append(jnp.where(low, pltpu.roll(blk, DN, 1), 0.0))
    return jnp.concatenate(kx, axis=-1), jnp.concatenate(v, axis=-1)


def _kv_prep_fwd(ckv, kpe, kva_w, wukv, wk, cosk, sink, cs, *, tm, n_tiles, tpe, n_lat):
    def body(ckv_ref, kpe_ref, kva_ref, wukv_ref, wk_ref, cos_ref, sin_ref, seg, segt, rot, k_ref, v_ref):
        x = ckv_ref[...]
        ckvn = (x * _rms(x) * kva_ref[...]).astype(BF16)
        kx, v = _split_kv(_mm_nt(ckvn, wukv_ref[...]), kpe_ref[...])
        k, _, _ = _head_norm_rope(kx, wk_ref[...], cos_ref[...], sin_ref[...], seg[...], segt[...], rot[...])
        k_ref[...] = k.astype(BF16)
        v_ref[...] = v.astype(BF16)

    row = lambda cols: pl.BlockSpec((tm, cols), lambda t: (t, 0))
    tab = _kv_tab_spec(tm, tpe, n_lat)
    r = n_tiles * tm
    return _pcall(
        body, name="kv_prep_fwd", grid=(n_tiles,), out_shape=(_sds((r, HP), BF16), _sds((r, HP), BF16)),
        in_specs=[row(KVL), row(LANE), _const((1, KVL)), _const((HP, KVL)), _const((1, HP)), tab, tab,
                  _const((HP, LANE)), _const((2 * LANE, HP)), _const((2 * LANE, 2 * LANE))],
        out_specs=(row(HP), row(HP)), vmem_mb=40)(
            ckv, kpe, kva_w, wukv, wk, cosk, sink, cs["seg_h"], cs["seg_ht"], cs["rot"])


def _kv_prep_bwd(dks, dvs, ckv, kpe, kva_w, wukv, wk, cosk, sink, cs, *, tm, n_tiles, tpe, n_lat):
    def body(dkl_ref, dkc_ref, dvl_ref, dvc_ref, ckv_ref, kpe_ref, kva_ref, wukv_ref, wk_ref, cos_ref, sin_ref,
             seg, segt, rot, rot_t, dckv_ref, dkpe_ref, dwukv_ref, dkva_ref, dwk_ref):
        t = pl.program_id(0)
        is_lat = t < n_lat

        @pl.when(t == 0)
        def _():
            dwukv_ref[...] = jnp.zeros_like(dwukv_ref)
            dkva_ref[...] = jnp.zeros_like(dkva_ref)
            dwk_ref[...] = jnp.zeros_like(dwk_ref)

        dk = jnp.where(is_lat, dkl_ref[...], dkc_ref[...])
        dv = jnp.where(is_lat, dvl_ref[...], dvc_ref[...])
        x = ckv_ref[...]
        ra = _rms(x)
        xh = x * ra
        kva = kva_ref[...]
        ckvn = (xh * kva).astype(BF16)
        wukv_v = wukv_ref[...]
        wk_v, cos_v, sin_v = wk_ref[...], cos_ref[...], sin_ref[...]
        kx, _ = _split_kv(_mm_nt(ckvn, wukv_v), kpe_ref[...])
        _, y, rb = _head_norm_rope(kx, wk_v, cos_v, sin_v, seg[...], segt[...], rot[...], rope=False)
        dkx, dwk = _head_norm_rope_bwd(dk, y, rb, wk_v, cos_v, sin_v, seg[...], segt[...], rot_t[...])
        dkpe = dkx[:, 0:LANE]
        for h in range(1, H):
            dkpe = dkpe + dkx[:, h * LANE:(h + 1) * LANE]
        lane = lax.broadcasted_iota(jnp.int32, (tm, LANE), 1)
        dkpe_ref[...] = jnp.where((lane >= DN) & (lane < DH), dkpe, 0.0).astype(BF16)
        dkv = jnp.concatenate([jnp.where(lane < DN, dkx[:, h * LANE:(h + 1) * LANE],
                                         pltpu.roll(dv[:, h * LANE:(h + 1) * LANE], DN, 1)) for h in range(H)],
                              axis=-1).astype(BF16)
        dckvn = _mm(dkv, wukv_v)
        dxh = dckvn * kva
        dckv_ref[...] = (ra * (dxh - xh * jnp.mean(dxh * xh, axis=-1, keepdims=True))).astype(BF16)
        dwukv_ref[...] += _mm_tn(dkv, ckvn)
        dkva_ref[...] += _rowsum(dckvn * xh)
        dwk_ref[...] += dwk

    row = lambda cols: pl.BlockSpec((tm, cols), lambda t: (t, 0))
    lat = pl.BlockSpec((tm, HP), lambda t: (jnp.minimum(t, n_lat - 1), 0))
    ctx = pl.BlockSpec((tm, HP), lambda t: (jnp.maximum(t - n_lat, 0), 0))
    tab = _kv_tab_spec(tm, tpe, n_lat)
    r = n_tiles * tm
    return _pcall(
        body, name="kv_prep_bwd", grid=(n_tiles,),
        out_shape=(_sds((r, KVL), BF16), _sds((r, LANE), BF16), _sds((HP, KVL), F32), _sds((1, KVL), F32),
                   _sds((1, HP), F32)),
        in_specs=[lat, ctx, lat, ctx, row(KVL), row(LANE), _const((1, KVL)), _const((HP, KVL)), _const((1, HP)),
                  tab, tab, _const((HP, LANE)), _const((2 * LANE, HP)), _const((2 * LANE, 2 * LANE)),
                  _const((2 * LANE, 2 * LANE))],
        out_specs=(row(KVL), row(LANE), _const((HP, KVL)), _const((1, KVL)), _const((1, HP))), vmem_mb=48)(
            dks[0], dks[1], dvs[0], dvs[1], ckv, kpe, kva_w, wukv, wk, cosk, sink,
            cs["seg_h"], cs["seg_ht"], cs["rot"], cs["rot_t"])


_SCALE = DH ** -0.5
_SCALE_LOG2E = _SCALE * 1.4426950408889634


def _key_chunks(s, nc, ck):
    return ([(0, lo, min(lo + ck, s)) for lo in range(0, s, ck)]
            + [(1, lo, min(lo + ck, nc)) for lo in range(0, nc, ck)])


def _attn_fwd(q, k, v, *, nb, s, nc, tq, ck):
    tpe = s // tq
    r_lat = nb * s
    chunks = _key_chunks(s, nc, ck)
    hp = 4

    def body(q_ref, kl_ref, kc_ref, vl_ref, vc_ref, o_ref, lse_ref):
        k_refs, v_refs = (kl_ref, kc_ref), (vl_ref, vc_ref)
        for hh in range(hp):
            hs = slice(hh * LANE, (hh + 1) * LANE)
            qv = q_ref[:, hs]
            xs = [_mm_nt(qv, k_refs[w][lo:hi, hs]) for w, lo, hi in chunks]
            m = jnp.max(xs[0], axis=-1, keepdims=True)
            for x in xs[1:]:
                m = jnp.maximum(m, jnp.max(x, axis=-1, keepdims=True))
            l = acc = None
            for x, (w, lo, hi) in zip(xs, chunks):
                e = jnp.exp2((x - m) * _SCALE_LOG2E)
                lc = jnp.sum(e, axis=-1, keepdims=True)
                pv = _mm(e.astype(BF16), v_refs[w][lo:hi, hs])
                l = lc if l is None else l + lc
                acc = pv if acc is None else acc + pv
            o_ref[:, hs] = (acc / l).astype(BF16)
            lse = m * _SCALE_LOG2E + jnp.log2(l)
            lse_ref[hh] = jnp.transpose(jnp.broadcast_to(lse, (tq, LANE)))[0:8, :]

    qs = pl.BlockSpec((tq, hp * LANE), lambda i, j, t: (i * tpe + t, j))
    kl = pl.BlockSpec((s, hp * LANE), lambda i, j, t: (i, j))
    kc = pl.BlockSpec((nc, hp * LANE), lambda i, j, t: (r_lat // nc + i, j))
    ls = pl.BlockSpec((hp, 8, tq), lambda i, j, t: (i * (H // hp) + j, 0, t))
    return _pcall(body, name="attn_fwd", grid=(nb, H // hp, tpe),
                  out_shape=(_sds((r_lat, HP), BF16), _sds((nb * H, 8, s), F32)),
                  in_specs=[qs, kl, kc, kl, kc], out_specs=(qs, ls), vmem_mb=48)(q, k, k, v, v)


def _attn_bwd(q, k, v, o, do, lse, part, *, nb, s, nc, tq, ck):
    tpe = s // tq
    r_lat = nb * s
    chunks = _key_chunks(s, nc, ck)
    hp = 2
    n_steps = nb * (H // hp) * tpe

    def body(q_ref, kl_ref, kc_ref, vl_ref, vc_ref, o_ref, do_ref, lse_ref, part_ref,
             dq_ref, dkl_ref, dkc_ref, dvl_ref, dvc_ref, recv_ref, akl, akc, avl, avc, send_sems, recv_sems):
        t = pl.program_id(2)
        step = (pl.program_id(0) * (H // hp) + pl.program_id(1)) * tpe + t
        _exchange_behind(part_ref, recv_ref, send_sems, recv_sems, step == 0, step == n_steps - 1)

        @pl.when(t == 0)
        def _():
            akl[...] = jnp.zeros_like(akl)
            akc[...] = jnp.zeros_like(akc)
            avl[...] = jnp.zeros_like(avl)
            avc[...] = jnp.zeros_like(avc)

        k_refs, v_refs, ak, av = (kl_ref, kc_ref), (vl_ref, vc_ref), (akl, akc), (avl, avc)
        for hh in range(hp):
            hs = slice(hh * LANE, (hh + 1) * LANE)
            qv = q_ref[:, hs]
            lse = jnp.transpose(jnp.concatenate([lse_ref[hh]] * (LANE // 8), axis=0))[:, 0:1]
            dov = do_ref[:, hs]
            delta = jnp.sum(dov.astype(F32) * o_ref[:, hs].astype(F32), axis=-1, keepdims=True)
            dq = None
            for w, lo, hi in chunks:
                kc_v = k_refs[w][lo:hi, hs]
                p = jnp.exp2(_mm_nt(qv, kc_v) * _SCALE_LOG2E - lse)
                ds = (p * (_mm_nt(dov, v_refs[w][lo:hi, hs]) - delta)).astype(BF16)
                part = _mm(ds, kc_v)
                dq = part if dq is None else dq + part
                ak[w][hs, lo:hi] += _mm_tn(qv, ds)
                av[w][hs, lo:hi] += _mm_tn(dov, p.astype(BF16))
            dq_ref[:, hs] = dq * _SCALE

        @pl.when(t == tpe - 1)
        def _():
            dkl_ref[...] = akl[...].T * _SCALE
            dkc_ref[...] = akc[...].T * _SCALE
            dvl_ref[...] = avl[...].T
            dvc_ref[...] = avc[...].T

    qs = pl.BlockSpec((tq, hp * LANE), lambda i, j, t: (i * tpe + t, j))
    kl = pl.BlockSpec((s, hp * LANE), lambda i, j, t: (i, j))
    kc = pl.BlockSpec((nc, hp * LANE), lambda i, j, t: (r_lat // nc + i, j))
    kc_out = pl.BlockSpec((nc, hp * LANE), lambda i, j, t: (i, j))
    ls = pl.BlockSpec((hp, 8, tq), lambda i, j, t: (i * (H // hp) + j, 0, t))
    return _pcall(
        body, name="attn_bwd", grid=(nb, H // hp, tpe),
        out_shape=(_sds((r_lat, HP), F32), _sds((r_lat, HP), F32), _sds((nb * nc, HP), F32),
                   _sds((r_lat, HP), F32), _sds((nb * nc, HP), F32), _sds((NDEV - 1,) + part.shape[1:], part.dtype)),
        in_specs=[qs, kl, kc, kl, kc, qs, qs, ls, ANY], out_specs=(qs, kl, kc_out, kl, kc_out, ANY),
        scratch=[pltpu.VMEM((hp * LANE, s), F32), pltpu.VMEM((hp * LANE, nc), F32)] * 2
        + [pltpu.SemaphoreType.DMA((NDEV - 1,))] * 2,
        vmem_mb=60)(q, k, k, v, v, o, do, lse, part)


def _chunks_side_by_side(x, j, nch):
    return jnp.concatenate([x[c * CH:(c + 1) * CH, j * LANE:(j + 1) * LANE] for c in range(nch)], axis=-1)


def _first_group_lanes(nch):
    return (lax.broadcasted_iota(jnp.int32, (CH, nch * LANE), 1) & (LANE - 1)) < GD


def _gating(vn, ws_ref, bias_ref, s_scr, tm):
    nch = tm // CH
    first = _first_group_lanes(nch)
    for j in range(G // 2):
        ls = slice(j * LANE, (j + 1) * LANE)
        vst = _chunks_side_by_side(vn, j, nch)
        st = jnp.where(first, _mm(ws_ref[2 * j], vst), _mm(ws_ref[2 * j + 1], vst))
        for c in range(nch):
            s_scr[c * CH:(c + 1) * CH, ls] = st[:, c * LANE:(c + 1) * LANE] + bias_ref[:, ls]


def _compact_heads(x):
    low = lax.broadcasted_iota(jnp.int32, (x.shape[0], LANE), 1) < DV
    out = []
    for j in range(H // 2):
        even = x[:, 2 * j * LANE:(2 * j + 1) * LANE].astype(F32)
        odd = x[:, (2 * j + 1) * LANE:(2 * j + 2) * LANE].astype(F32)
        out.append(jnp.where(low, even, pltpu.roll(odd, DV, 1)))
    return jnp.concatenate(out, axis=-1)


def _expand_heads(x):
    low = lax.broadcasted_iota(jnp.int32, (x.shape[0], LANE), 1) < DV
    out = []
    for j in range(H // 2):
        blk = x[:, j * LANE:(j + 1) * LANE]
        out.append(jnp.where(low, blk, 0.0))
        out.append(jnp.where(low, pltpu.roll(blk, DV, 1), 0.0))
    return jnp.concatenate(out, axis=-1)


def _mix_fwd(u, v, attn, x1, gate, wv, ws, bias, wout, cs, *, tm, n_lat, tpe):
    nrows = gate.shape[0]

    def body(u_ref, v_ref, attn_ref, x_ref, gate_ref, wv_ref, ws_ref, bias_ref, wout_ref, seg, segt,
             x2_ref, mix_ref, s_scr):
        vg = _gelu(v_ref[...])
        rg = lax.rsqrt(_seg_sum(vg * vg, seg[...]) * (1.0 / GD) + EPS)
        vn = (vg * _seg_bcast(rg, segt[...]) * wv_ref[...]).astype(BF16)
        _gating(vn, ws_ref, bias_ref, s_scr, tm)
        sg = (_gelu(u_ref[...]) * s_scr[...]).astype(BF16)
        attn_c = _compact_heads(attn_ref[...]).astype(BF16)
        mix = _mm(attn_c, wout_ref[0:H * DV, :]) + _mm(sg, wout_ref[H * DV:, :])
        mix_ref[...] = mix.astype(BF16)
        x2_ref[...] = x_ref[...] + gate_ref[0] * mix

    row = lambda cols: pl.BlockSpec((tm, cols), lambda t: (t, 0))
    r = n_lat * tm
    return _pcall(
        body, name="mix_fwd", grid=(n_lat,),
        out_shape=(_sds((r, D), F32), _sds((r, D), BF16)),
        in_specs=[row(G * GD), row(G * GD), row(HP), row(D), _mod_spec(1, tpe, nrows), _const((1, G * GD)),
                  _const((G, CH, CH)), _const((CH, G * GD)), _const((D, D)), _const((G * GD, LANE)),
                  _const((2 * LANE, G * GD))],
        out_specs=(row(D), row(D)), scratch=[pltpu.VMEM((tm, G * GD), F32)], vmem_mb=40)(
            u, v, attn, x1, gate, wv, ws, bias, wout, cs["seg_g"], cs["seg_gt"])


def _mix_bwd(dx2, mix, u, v, attn, gate, wv, ws, wst, bias, wout, cs, *, tm, n_lat, tpe):
    nrows = gate.shape[0]
    wrows = H * DV + G * GD

    def body(dx2_ref, mix_ref, u_ref, v_ref, attn_ref, gate_ref, wv_ref, ws_ref, wst_ref, bias_ref, wout_ref, seg, segt,
             dattn_ref, du_ref, dv_ref, dgate_ref, dwout_ref, dws_ref, dbs_ref, dwv_ref, s_scr, dvn_scr, dbias_scr):
        t = pl.program_id(0)

        @pl.when(t == 0)
        def _():
            dwout_ref[...] = jnp.zeros_like(dwout_ref)
            dws_ref[...] = jnp.zeros_like(dws_ref)
            dwv_ref[...] = jnp.zeros_like(dwv_ref)
            dbias_scr[...] = jnp.zeros_like(dbias_scr)

        @pl.when(t % tpe == 0)
        def _():
            dgate_ref[...] = jnp.zeros_like(dgate_ref)

        dx2 = dx2_ref[...]
        dmix = (dx2 * gate_ref[0]).astype(BF16)
        dcat = _mm_nt(dmix, wout_ref[...])
        dattn_ref[...] = _expand_heads(dcat[:, :H * DV]).astype(BF16)
        dsg = dcat[:, H * DV:]

        vraw = v_ref[...]
        vg = _gelu(vraw)
        rg = lax.rsqrt(_seg_sum(vg * vg, seg[...]) * (1.0 / GD) + EPS)
        r64 = _seg_bcast(rg, segt[...])
        y = vg * r64
        wv_v = wv_ref[...]
        vn = (y * wv_v).astype(BF16)
        _gating(vn, ws_ref, bias_ref, s_scr, tm)
        uraw = u_ref[...]
        ug = _gelu(uraw)
        s = s_scr[...]
        sg = (ug * s).astype(BF16)
        du_ref[...] = (dsg * s * _gelu_grad(uraw)).astype(BF16)
        ds = dsg * ug
        dgate_ref[0] += _rowsum(dx2 * mix_ref[...].astype(F32))
        attn_c = _compact_heads(attn_ref[...]).astype(BF16)
        dwout_ref[...] += _mm_tn(jnp.concatenate([attn_c, sg], axis=-1), dmix)

        nch = tm // CH
        first = _first_group_lanes(nch)
        for c in range(nch):
            dbias_scr[...] += ds[c * CH:(c + 1) * CH, :]
        for j in range(G // 2):
            ls = slice(j * LANE, (j + 1) * LANE)
            dst32 = _chunks_side_by_side(ds, j, nch)
            dst = dst32.astype(BF16)
            vst = _chunks_side_by_side(vn, j, nch)
            dvn_st = jnp.where(first, _mm(wst_ref[2 * j], dst), _mm(wst_ref[2 * j + 1], dst))
            for c in range(nch):
                dvn_scr[c * CH:(c + 1) * CH, ls] = dvn_st[:, c * LANE:(c + 1) * LANE]
            dws_ref[2 * j] += _mm_nt(jnp.where(first, dst32, 0.0).astype(BF16), vst)
            dws_ref[2 * j + 1] += _mm_nt(jnp.where(first, 0.0, dst32).astype(BF16), vst)

        dvn = dvn_scr[...]
        dwv_ref[...] += _rowsum(dvn * y)
        dy = dvn * wv_v
        mean_g = _seg_sum(dy * y, seg[...]) * (1.0 / GD)
        dvg = r64 * (dy - y * _seg_bcast(mean_g, segt[...]))
        dv_ref[...] = (dvg * _gelu_grad(vraw)).astype(BF16)

        @pl.when(t == n_lat - 1)
        def _():
            dbs_ref[...] = _dot_hl(dbias_scr[...], seg[...])

    row = lambda cols: pl.BlockSpec((tm, cols), lambda t: (t, 0))
    r = n_lat * tm
    return _pcall(
        body, name="mix_bwd", grid=(n_lat,),
        out_shape=(_sds((r, HP), BF16), _sds((r, G * GD), BF16), _sds((r, G * GD), BF16), _sds((nrows, 1, D), F32),
                   _sds((wrows, D), F32), _sds((G, CH, CH), F32), _sds((CH, LANE), F32), _sds((1, G * GD), F32)),
        in_specs=[row(D), row(D), row(G * GD), row(G * GD), row(HP), _mod_spec(1, tpe, nrows), _const((1, G * GD)),
                  _const((G, CH, CH)), _const((G, CH, CH)), _const((CH, G * GD)), _const((wrows, D)),
                  _const((G * GD, LANE)), _const((2 * LANE, G * GD))],
        out_specs=(row(HP), row(G * GD), row(G * GD), _mod_spec(1, tpe, nrows), _const((wrows, D)),
                   _const((G, CH, CH)), _const((CH, LANE)), _const((1, G * GD))),
        scratch=[pltpu.VMEM((tm, G * GD), F32), pltpu.VMEM((tm, G * GD), F32), pltpu.VMEM((CH, G * GD), F32)],
        vmem_mb=56)(dx2, mix, u, v, attn, gate, wv, ws, wst, bias, wout, cs["seg_g"], cs["seg_gt"])


def _adamw_math(w, g, m, v):
    m2 = ADAM_B1 * m + (1.0 - ADAM_B1) * g
    v2 = ADAM_B2 * v + (1.0 - ADAM_B2) * (g * g)
    m_hat = m2 / (1.0 - ADAM_B1 ** ADAM_STEP)
    v_hat = v2 / (1.0 - ADAM_B2 ** ADAM_STEP)
    delta = -ADAM_LR * (m_hat / (jnp.sqrt(v_hat) + ADAM_EPS) + ADAM_WD * w)
    return delta, m2, v2


def _row_tile(r, c):
    best = r
    for tr in range(8, r, 8):
        if r % tr == 0 and tr * c * 4 <= MIB:
            best = tr
    return best


def _adamw(w, g, m, v, name):
    r, c = w.shape
    tr = _row_tile(r, c)

    def body(w_ref, g_ref, m_ref, v_ref, d_ref, mo_ref, vo_ref):
        d_ref[...], mo_ref[...], vo_ref[...] = _adamw_math(w_ref[...], g_ref[...], m_ref[...], v_ref[...])

    blk = pl.BlockSpec((tr, c), lambda t: (t, 0))
    return _pcall(body, name=name, grid=(r // tr,), out_shape=(_sds((r, c), F32),) * 3,
                  in_specs=[blk] * 4, out_specs=(blk,) * 3)(w, g, m, v)


def _adamw_small(params):
    n = len(params)

    def body(*refs):
        ins, outs = refs[:4 * n], refs[4 * n:]
        for i in range(n):
            w, g, m, v = (ins[4 * i + k][...] for k in range(4))
            if i == 0:
                sig = _sigmoid(w)
                g = g * (sig * (1.0 + w * (1.0 - sig)))
            d, m2, v2 = _adamw_math(w, g, m, v)
            outs[4 * i][...] = g
            outs[4 * i + 1][...] = d
            outs[4 * i + 2][...] = m2
            outs[4 * i + 3][...] = v2

    flat = [a for p in params for a in p]
    out_shape = tuple(_sds(p[0].shape, F32) for p in params for _ in range(4))
    res = _pcall(body, name="adamw_small", out_shape=out_shape, in_specs=[VMEM] * (4 * n),
                 out_specs=(VMEM,) * (4 * n))(*flat)
    return [res[4 * i:4 * i + 4] for i in range(n)]


def _rope_tables(s):
    rows = jnp.repeat(jnp.arange(s // GRID_W, dtype=F32), GRID_W)
    cols = jnp.tile(jnp.arange(GRID_W, dtype=F32), s // GRID_W)
    half = DR // 2
    inv = ROPE_BASE ** (-jnp.arange(0, half, 2, dtype=F32) / half)
    ang_r = rows[:, None] * inv
    ang_c = cols[:, None] * inv
    ang = jnp.concatenate([ang_r, ang_r, ang_c, ang_c], axis=-1)
    return jnp.cos(ang), jnp.sin(ang)


def _head_pad(a, real):
    return jnp.pad(a, ((0, 0), (0, LANE - real), (0, 0))).reshape(HP, a.shape[2])


def kernel(x, c, ctx, c_ctx, w_ada, b_ada, norm1_w, ffn1_w1, ffn1_w3, ffn1_w2, norm2_w, w_in, q_a_norm_w, w_uq, kv_a_norm_w, w_ukv, q_norm_w, k_norm_w, v_norm_w, w_s, b_s, w_out, norm3_w, ffn2_w1, ffn2_w3, ffn2_w2, loss_target, m_c_ctx, m_w_ada, m_b_ada, m_norm1_w, m_ffn1_w1, m_ffn1_w3, m_ffn1_w2, m_norm2_w, m_w_in, m_q_a_norm_w, m_w_uq, m_kv_a_norm_w, m_w_ukv, m_q_norm_w, m_k_norm_w, m_v_norm_w, m_w_s, m_b_s, m_w_out, m_norm3_w, m_ffn2_w1, m_ffn2_w3, m_ffn2_w2, v_c_ctx, v_w_ada, v_b_ada, v_norm1_w, v_ffn1_w1, v_ffn1_w3, v_ffn1_w2, v_norm2_w, v_w_in, v_q_a_norm_w, v_w_uq, v_kv_a_norm_w, v_w_ukv, v_q_norm_w, v_k_norm_w, v_v_norm_w, v_w_s, v_b_s, v_w_out, v_norm3_w, v_ffn2_w1, v_ffn2_w3, v_ffn2_w2):
    nb, s, _ = x.shape
    nc = ctx.shape[1]
    tm = 256 if nc % 256 == 0 else 128
    tpe = s // tm
    n_lat = nb * tpe
    n_all = n_lat + nb * nc // tm
    tmf = 2 * tm if s % (2 * tm) == 0 and (nb * nc) % (2 * tm) == 0 else tm
    tp = tmf
    r_lat = nb * s
    tpe_p, n_lat_p, n_all_p = s // tp, r_lat // tp, (r_lat + nb * nc) // tp
    me = 4 * lax.axis_index("x") + 2 * lax.axis_index("y") + lax.axis_index("c")
    cs = _consts()
    ncol = w_ada.shape[2]
    fsh = ffn1_w1.shape[2]
    assert nb + 1 <= 8 and NDEV * fsh == FF and NDEV * ncol == NMOD * D and s % nc == 0 and nc % tm == 0

    def t16(a):
        return a.T.astype(BF16)

    wpack1 = jnp.concatenate([t16(ffn1_w1[0]), t16(ffn1_w3[0]), ffn1_w2[0].astype(BF16)], axis=0)
    a_loc = jnp.concatenate([c, c_ctx[None, :], jnp.zeros((7 - nb, D), F32)], axis=0)
    a_raw, _, mod_all, wall1 = _ada_front(a_loc, w_ada[0], lax.dynamic_slice_in_dim(b_ada, me * ncol, ncol, axis=1),
                                          wpack1)
    a_raw = a_raw.reshape(NDEV * 8, D)
    mod_mine = lax.dynamic_slice_in_dim(mod_all, 8 * me, 8, axis=1)
    modtab = mod_mine.transpose(1, 0, 2).reshape(8, NMOD, D)[:nb + 1]
    wpack2 = jnp.concatenate([
        t16(ffn2_w1[0]), t16(ffn2_w3[0]), ffn2_w2[0].astype(BF16),
        t16(w_in[0]), jnp.zeros((12, D), BF16),
        w_out[0].astype(BF16),
        t16(w_uq[0]).reshape(24, D), jnp.zeros((8, D), BF16),
        t16(w_ukv[0]).reshape(16, D)], axis=0)

    def head_w(wn):
        return jnp.tile(jnp.pad(wn, ((0, 0), (0, LANE - DH))), (1, H))

    wq, wk = head_w(q_norm_w), head_w(k_norm_w)
    wv = v_norm_w.reshape(1, G * GD)
    ws16 = w_s[0].astype(BF16)
    wst16 = w_s[0].transpose(0, 2, 1).astype(BF16)
    bias = jnp.repeat(b_s[0].T, GD, axis=1)
    cos, sin = _rope_tables(s)
    cos = jnp.pad(cos, ((0, 0), (DN, LANE - DH)), constant_values=1.0)
    sin = jnp.pad(sin, ((0, 0), (DN, LANE - DH)))
    cos_k = jnp.concatenate([cos, jnp.ones((tm, LANE), F32)], axis=0)
    sin_k = jnp.concatenate([sin, jnp.zeros((tm, LANE), F32)], axis=0)

    xs = (x.reshape(r_lat, D), ctx.reshape(nb * nc, D))
    x1, a1, b1, o1, wall2 = _ffn_fwd(xs, modtab[:, 0:3], norm1_w, wall1, 0, tm=tmf, n_tiles=(r_lat + nb * nc) // tmf,
                                     tpe=s // tmf, n_lat=r_lat // tmf, name="ffn1_fwd", gather=wpack2)

    o0 = 3 * fsh
    wint = wall2[:, o0:o0 + 180].reshape(IN_COLS, D)
    z = lambda n: jnp.zeros((n, D), BF16)
    wint = jnp.concatenate([wint[0:128], wint[160:416], wint[416:928], wint[928:1440],
                            z(DN), wint[128:160], z(LANE - DH)], axis=0)
    wout = wall2[:, o0 + 192:o0 + 320].reshape(D, D)
    wuq = _head_pad(wall2[:, o0 + 320:o0 + 344].reshape(H, DH, QL), DH)
    wukv = wall2[:, o0 + 352:o0 + 368].reshape(HP, KVL)

    ckv, qp, u_raw, v_raw, kpe = _proj_fwd(x1, modtab[:, 3:5], norm2_w, wint, tm=tp, n_tiles=n_all_p, tpe=tpe_p)
    q = _q_prep_fwd(qp, q_a_norm_w, wuq, wq, cos, sin, cs, tm=tm, n_lat=n_lat, tpe=tpe)
    k, v = _kv_prep_fwd(ckv, kpe, kv_a_norm_w, wukv, wk, cos_k, sin_k, cs,
                        tm=tm, n_tiles=n_all, tpe=tpe, n_lat=n_lat)
    attn, lse = _attn_fwd(q, k, v, nb=nb, s=s, nc=nc, tq=tm, ck=2048)
    x2, mix = _mix_fwd(u_raw, v_raw, attn, x1, modtab[:nb, 5:6], wv, ws16, bias, wout, cs,
                       tm=tp, n_lat=n_lat_p, tpe=tpe_p)
    dy, a2, b2, o2, lsum = _ffn_fwd((x2,), modtab[:nb, 6:9], norm3_w, wall2, 0, tm=tmf, n_tiles=r_lat // tmf,
                                    tpe=s // tmf, n_lat=r_lat // tmf, name="ffn2_fwd",
                                    target=loss_target.reshape(r_lat, D))

    tr = 2 * tm if n_lat % 2 == 0 and n_all % 2 == 0 else tm
    dx2, da2, db2, g2, do2, h2, dmod678, dnorm3 = _ffn_bwd_dx(
        dy, (x2,), a2, b2, o2, modtab[:nb, 6:9], norm3_w, wall2, 0,
        tm=tm, n_tiles=n_lat, tpe=tpe, n_lat=n_lat, name="ffn2_bwd_dx")
    g_ffn2 = _ffn_bwd_dw(h2, do2, da2, db2, g2, tr=tr, name="ffn2_bwd_dw")

    dattn, du, dv, dgate5, dwout, dws, dbs, dwv = _mix_bwd(
        dx2, mix, u_raw, v_raw, attn, modtab[:nb, 5:6], wv, ws16, wst16, bias, wout, cs, tm=tp, n_lat=n_lat_p, tpe=tpe_p)
    tq = 2 * tm if s % (2 * tm) == 0 else tm
    dq, dk_l, dk_c, dv_l, dv_c, recv_ffn2 = _attn_bwd(q, k, v, attn, dattn, lse, g_ffn2,
                                                      nb=nb, s=s, nc=nc, tq=tq, ck=1024)
    dqp, dwuq, dqa, dwq = _q_prep_bwd(dq, qp, q_a_norm_w, wuq, wq, cos, sin, cs, tm=tm, n_lat=n_lat, tpe=tpe)
    dckv, dkpe, dwukv, dkva, dwk = _kv_prep_bwd((dk_l, dk_c), (dv_l, dv_c), ckv, kpe, kv_a_norm_w, wukv, wk,
                                                cos_k, sin_k, cs, tm=tm, n_tiles=n_all, tpe=tpe, n_lat=n_lat)
    dx1, dwin, dmod34, dnorm2 = _proj_bwd(dckv, dkpe, dqp, du, dv, dx2, x1, modtab[:, 3:5], norm2_w, wint,
                                          tm=tp, n_tiles=n_all_p, tpe=tpe_p, n_lat=n_lat_p)

    def blocks(a):
        return a.reshape(NDEV, a.shape[0] // NDEV, D)

    dwin_o = jnp.concatenate([dwin[0:128], dwin[KPE_LO:KPE_LO + DR], dwin[128:384], dwin[384:896], dwin[896:1408]],
                             axis=0)
    dwuq_o = dwuq.reshape(H, LANE, QL)[:, :DH]
    gmisc = jnp.concatenate([
        blocks(dwin_o).astype(BF16), jnp.zeros((NDEV, 12, D), BF16),
        blocks(dwout).astype(BF16),
        dwuq_o.reshape(NDEV, 24, D).astype(BF16), jnp.zeros((NDEV, 8, D), BF16),
        dwukv.reshape(NDEV, 16, D).astype(BF16)], axis=1)

    dx0, da1, db1, g1, do1, h1, dmod012, dnorm1 = _ffn_bwd_dx(
        dx1, xs, a1, b1, o1, modtab[:, 0:3], norm1_w, wall1, 0,
        tm=tm, n_tiles=n_all, tpe=tpe, n_lat=n_lat, name="ffn1_bwd_dx")
    grad_x = dx0.reshape(nb, s, D)
    g_w1, recv_misc = _ffn_bwd_dw_one(da1, h1, tr=tr, name="ffn1_bwd_dw1", part=gmisc)
    g_w3, recv_w1 = _ffn_bwd_dw_one(db1, h1, tr=tr, name="ffn1_bwd_dw3", part=g_w1)
    g_w2, recv_w3 = _ffn_bwd_dw_one(g1, do1, tr=tr, name="ffn1_bwd_dw2", part=g_w3)

    zrow = jnp.zeros((1, D), F32)
    g_lat = jnp.concatenate([dmod012[:nb, 0], dmod012[:nb, 1], dmod012[:nb, 2], dmod34[:nb, 0], dmod34[:nb, 1],
                             dgate5[:, 0], dmod678[:, 0], dmod678[:, 1], dmod678[:, 2]], axis=1)
    g_ctx = jnp.concatenate([dmod012[nb:, 0], dmod012[nb:, 1], dmod012[nb:, 2], dmod34[nb:, 0], dmod34[nb:, 1],
                             zrow, zrow, zrow, zrow], axis=1)
    g_loc = jnp.concatenate([g_lat, g_ctx, jnp.zeros((7 - nb, NMOD * D), F32)], axis=0)

    got_w2, g_all = _scatter_sibling([g_w2], "scatter_sibling_w2", gather=g_loc)
    g_all = g_all.reshape(NDEV * 8, NMOD * D)
    g_cols = lax.dynamic_slice_in_dim(g_all, me * ncol, ncol, axis=1)
    g_w_ada, pc_ctx, g_b_ada = _ada_bwd(a_raw, c_ctx.reshape(D, 1), g_all, g_cols, w_ada[0], nb)
    part_w2 = _add_sibling(g_w2, got_w2, 176, "add_sibling_w2")

    def prow(a):
        a = a.reshape(1, -1)
        return jnp.concatenate([a, jnp.zeros((1, D - a.shape[1]), F32)], axis=1)

    g_qn = dwq.reshape(H, LANE)[:, :DH].sum(0)
    g_kn = dwk.reshape(H, LANE)[:, :DH].sum(0)
    spack = jnp.concatenate([
        dnorm1, dnorm2, dnorm3, prow(dqa), prow(dkva), prow(g_qn), prow(g_kn), prow(dwv),
        prow(dbs[:, :G].T), prow(pc_ctx), prow(lsum[0:1]), jnp.zeros((5, D), F32), dws.reshape(CH, D)],
        axis=0)
    recv_w2, small_all = _scatter_chips([part_w2], "scatter_chips", gather=spack)
    ssum = _sum_slots(small_all, 144, "sum_small")
    loss = ssum[10, 0] * (0.5 / D)
    gsum2 = _sum_direct(g_ffn2, recv_ffn2, 176, "sum_grads_ffn2")
    msum = _sum_direct(gmisc, recv_misc, 368, "sum_grads_misc")

    transposed = ("ffn1_w1", "ffn1_w3", "ffn2_w1", "ffn2_w3", "w_in", "w_uq")
    g_big = {
        "ffn1_w1": _sum_direct(g_w1, recv_w1, 176, "sum_grads_w1"),
        "ffn1_w3": _sum_direct(g_w3, recv_w3, 176, "sum_grads_w3"),
        "ffn1_w2": _sum_chips(part_w2, recv_w2, 176, "sum_grads_w2"),
        "ffn2_w1": gsum2[0:fsh], "ffn2_w3": gsum2[fsh:2 * fsh], "ffn2_w2": gsum2[2 * fsh:3 * fsh],
        "w_in": msum[0:180], "w_out": msum[192:320],
        "w_uq": msum[320:344].reshape(DH, QL), "w_ukv": msum[352:368].reshape(DN + DV, KVL).T,
        "w_ada": g_w_ada,
    }

    big_in = {
        "w_ada": (w_ada, m_w_ada, v_w_ada), "ffn1_w1": (ffn1_w1, m_ffn1_w1, v_ffn1_w1),
        "ffn1_w3": (ffn1_w3, m_ffn1_w3, v_ffn1_w3), "ffn1_w2": (ffn1_w2, m_ffn1_w2, v_ffn1_w2),
        "w_in": (w_in, m_w_in, v_w_in), "w_uq": (w_uq, m_w_uq, v_w_uq), "w_ukv": (w_ukv, m_w_ukv, v_w_ukv),
        "w_out": (w_out, m_w_out, v_w_out), "ffn2_w1": (ffn2_w1, m_ffn2_w1, v_ffn2_w1),
        "ffn2_w3": (ffn2_w3, m_ffn2_w3, v_ffn2_w3), "ffn2_w2": (ffn2_w2, m_ffn2_w2, v_ffn2_w2),
    }
    res = {}
    for nm, (w, m, v_) in big_in.items():
        g = g_big[nm]
        if nm in transposed:
            d_, m_, v2_ = _adamw(w[0].T, g, m[0].T, v_[0].T, "adamw_" + nm)
            res[nm] = tuple(a.T[None] for a in (g, d_, m_, v2_))
        else:
            d_, m_, v2_ = _adamw(w[0], g, m[0], v_[0], "adamw_" + nm)
            res[nm] = tuple(a[None] for a in (g, d_, m_, v2_))

    small_in = [
        ("c_ctx", c_ctx, m_c_ctx, v_c_ctx, ssum[9:10], (1, D)),
        ("b_ada", b_ada, m_b_ada, v_b_ada, g_b_ada, (1, NMOD * D)),
        ("norm1_w", norm1_w, m_norm1_w, v_norm1_w, ssum[0:1], (1, D)),
        ("norm2_w", norm2_w, m_norm2_w, v_norm2_w, ssum[1:2], (1, D)),
        ("norm3_w", norm3_w, m_norm3_w, v_norm3_w, ssum[2:3], (1, D)),
        ("q_a_norm_w", q_a_norm_w, m_q_a_norm_w, v_q_a_norm_w, ssum[3:4, :QL], (1, QL)),
        ("kv_a_norm_w", kv_a_norm_w, m_kv_a_norm_w, v_kv_a_norm_w, ssum[4:5, :KVL], (1, KVL)),
        ("q_norm_w", q_norm_w, m_q_norm_w, v_q_norm_w, ssum[5:6, :DH], (1, DH)),
        ("k_norm_w", k_norm_w, m_k_norm_w, v_k_norm_w, ssum[6:7, :DH], (1, DH)),
        ("v_norm_w", v_norm_w, m_v_norm_w, v_v_norm_w, ssum[7:8, :G * GD], (G, GD)),
        ("b_s", b_s, m_b_s, v_b_s, ssum[8:9], (G, CH)),
        ("w_s", w_s, m_w_s, v_w_s, ssum[16:144], (G * CH, CH)),
    ]
    small_out = _adamw_small(
        [(w.reshape(sh), g.reshape(sh), m.reshape(sh), v_.reshape(sh)) for _, w, m, v_, g, sh in small_in])
    for (nm, w, *_), outs in zip(small_in, small_out):
        res[nm] = tuple(a.reshape(w.shape) for a in outs)

    order = ["c_ctx", "w_ada", "b_ada", "norm1_w", "ffn1_w1", "ffn1_w3", "ffn1_w2", "norm2_w", "w_in", "q_a_norm_w",
             "w_uq", "kv_a_norm_w", "w_ukv", "q_norm_w", "k_norm_w", "v_norm_w", "w_s", "b_s", "w_out", "norm3_w",
             "ffn2_w1", "ffn2_w3", "ffn2_w2"]
    return (loss, grad_x, *[res[n][0] for n in order], *[res[n][1] for n in order],
            *[res[n][2] for n in order], *[res[n][3] for n in order])
```

```python
import numpy as np
import jax
import jax.numpy as jnp
from jax import lax
from jax.experimental import pallas as pl
from jax.experimental.pallas import tpu as pltpu

F32 = jnp.float32
BF16 = jnp.bfloat16

D = 1024
FF = 2816
FC = 256
H = 8
DN, DR, DV = 64, 32, 64
DH = DN + DR
QL, KVL = 256, 128
G, GD, CH = 8, 64, 128
NMOD = 9
EPS = 1e-6
GRID_W = 64
ROPE_BASE = 10000.0
NDEV = 8
LANE = 128
HP = H * LANE
IN_COLS = 1440
WIN_ROWS = 1536
KPE_LO = 1408 + DN
MIB = 1 << 20

ADAM_LR, ADAM_B1, ADAM_B2, ADAM_EPS, ADAM_WD, ADAM_STEP = 0.001, 0.9, 0.999, 1e-08, 0.01, 10

MESH = pl.DeviceIdType.MESH
ANY = pl.BlockSpec(memory_space=pl.ANY)
VMEM = pl.BlockSpec(memory_space=pltpu.VMEM)


def _mm(a, b):
    return jnp.dot(a, b, preferred_element_type=F32)


def _mm_nt(a, b):
    return lax.dot_general(a, b, (((1,), (1,)), ((), ())), preferred_element_type=F32)


def _mm_tn(a, b):
    return lax.dot_general(a, b, (((0,), (0,)), ((), ())), preferred_element_type=F32)


def _dot_hl(x, m):
    hi = x.astype(BF16)
    lo = (x - hi.astype(F32)).astype(BF16)
    return _mm(hi, m) + _mm(lo, m)


def _sigmoid(a):
    return 1.0 / (1.0 + jnp.exp(-a))


_G0 = 0.7978845608028654
_G1 = 0.044715


def _gelu(x):
    return 0.5 * x * (1.0 + jnp.tanh(_G0 * (x + _G1 * (x * x * x))))


def _gelu_grad(x):
    th = jnp.tanh(_G0 * (x + _G1 * (x * x * x)))
    return 0.5 * (1.0 + th) + 0.5 * x * (1.0 - th * th) * (_G0 * (1.0 + 3.0 * _G1 * x * x))


def _rowsum(y):
    return jnp.sum(y, axis=0, keepdims=True)


def _rms(x):
    return lax.rsqrt(jnp.mean(x * x, axis=-1, keepdims=True) + EPS)


def _pcall(body, *, name, out_shape, in_specs, out_specs, grid=None, scratch=(), vmem_mb=32, aliases=None):
    kw = {}
    if grid is not None:
        kw["grid"] = grid
        sem = ("arbitrary",) * len(grid)
    else:
        sem = None
    if aliases:
        kw["input_output_aliases"] = aliases
    return pl.pallas_call(
        body, name=name, out_shape=out_shape, in_specs=in_specs, out_specs=out_specs,
        scratch_shapes=list(scratch),
        compiler_params=pltpu.CompilerParams(dimension_semantics=sem, vmem_limit_bytes=vmem_mb * MIB),
        **kw)


def _const(shape):
    nd = len(shape)
    return pl.BlockSpec(shape, lambda *_: (0,) * nd)


def _sds(shape, dt):
    return jax.ShapeDtypeStruct(shape, dt)


def _consts():
    seg_h = np.zeros((HP, LANE), np.float32)
    seg_h[np.arange(HP), np.arange(HP) // LANE] = 1.0
    seg_g = np.zeros((G * GD, LANE), np.float32)
    seg_g[np.arange(G * GD), np.arange(G * GD) // GD] = 1.0
    rot = np.zeros((LANE, LANE), np.float32)
    for base in (DN, DN + 16):
        for j in range(8):
            rot[base + j + 8, base + j] = -1.0
            rot[base + j, base + j + 8] = 1.0
    rot2 = np.zeros((2 * LANE, 2 * LANE), np.float32)
    rot2[:LANE, :LANE] = rot
    rot2[LANE:, LANE:] = rot
    twice = lambda m: np.concatenate([m, m], axis=0)
    c = dict(seg_h=seg_h, seg_ht=twice(seg_h.T), seg_g=seg_g, seg_gt=twice(seg_g.T), rot=rot2, rot_t=rot2.T)
    return {k: jnp.asarray(v, BF16) for k, v in c.items()}


_GATHER_SEMS = [pltpu.SemaphoreType.DMA((7,)), pltpu.SemaphoreType.DMA((7,)), pltpu.SemaphoreType.DMA(())]


def _gather_phases(x_ref, out_ref, send_sems, recv_sems, local_sem):
    mx, my, mc = lax.axis_index("x"), lax.axis_index("y"), lax.axis_index("c")
    me, sibling = (mx, my, mc), (mx, my, 1 - mc)
    chips = [(1 - mx, my), (mx, 1 - my), (1 - mx, 1 - my)]

    def blk(px, py, pc):
        return out_ref.at[4 * px + 2 * py + pc]

    def copy(k, block, to, src=None):
        return pltpu.make_async_remote_copy(
            src_ref=blk(*block) if src is None else src, dst_ref=blk(*block),
            send_sem=send_sems.at[k], recv_sem=recv_sems.at[k], device_id=to, device_id_type=MESH)

    mine = pltpu.make_async_copy(x_ref, blk(*me), local_sem)
    first = [copy(0, me, sibling, src=x_ref)]
    first += [copy(1 + j, me, (*chip, mc), src=x_ref) for j, chip in enumerate(chips)]
    passed = [copy(4 + j, (*chip, mc), sibling) for j, chip in enumerate(chips)]

    def start():
        mine.start()
        for cp in first:
            cp.start()

    def forward():
        for j, chip in enumerate(chips):
            copy(1 + j, (*chip, mc), me).wait_recv()
            passed[j].start()

    def finish():
        copy(0, sibling, me).wait_recv()
        for j, chip in enumerate(chips):
            copy(4 + j, (*chip, 1 - mc), me).wait_recv()
        for cp in first + passed:
            cp.wait_send()
        mine.wait()

    return start, forward, finish


def _chip_sends(p_ref, out_ref, send_sems, recv_sems):
    mx, my, mc = lax.axis_index("x"), lax.axis_index("y"), lax.axis_index("c")
    peers = [(1 - mx, my), (mx, 1 - my), (1 - mx, 1 - my)]
    return [pltpu.make_async_remote_copy(
        src_ref=p_ref.at[2 * px + py], dst_ref=out_ref.at[j], send_sem=send_sems.at[j], recv_sem=recv_sems.at[j],
        device_id=(px, py, mc), device_id_type=MESH) for j, (px, py) in enumerate(peers)]


def _with_gather(copies_of, n, shapes, sems, gather, name, args):
    ns = len(sems)

    def body(*refs):
        ng = 1 if gather is not None else 0
        ins, outs = refs[:n], refs[n + ng:2 * n + ng]
        copies = copies_of(ins, outs, refs[2 * n + 2 * ng:2 * n + 2 * ng + ns])
        if ng:
            start, forward, finish = _gather_phases(refs[n], refs[2 * n + 1], *refs[2 * n + 2 + ns:])
            start()
        for cp in copies:
            cp.start()
        if ng:
            forward()
        for cp in copies:
            cp.wait_recv()
        for cp in copies:
            cp.wait_send()
        if ng:
            finish()

    in_specs, out_shape, scratch = [ANY] * n, list(shapes), list(sems)
    if gather is not None:
        in_specs.append(ANY)
        args = list(args) + [gather]
        out_shape.append(_sds((NDEV,) + gather.shape, gather.dtype))
        scratch += _GATHER_SEMS
    return pl.pallas_call(body, name=name, out_shape=tuple(out_shape), in_specs=in_specs,
                          out_specs=(ANY,) * len(out_shape), scratch_shapes=scratch)(*args)


def _scatter_sibling(xs, name, gather=None):
    n = len(xs)

    def copies_of(x_refs, got_refs, sems):
        send_sems, recv_sems = sems
        mx, my, mc = lax.axis_index("x"), lax.axis_index("y"), lax.axis_index("c")
        return [pltpu.make_async_remote_copy(
            src_ref=x_refs[i].at[2 * j + 1 - mc], dst_ref=got_refs[i].at[j],
            send_sem=send_sems.at[4 * i + j], recv_sem=recv_sems.at[4 * i + j],
            device_id=(mx, my, 1 - mc), device_id_type=MESH) for i in range(n) for j in range(4)]

    shapes = tuple(_sds((4,) + x.shape[1:], x.dtype) for x in xs)
    return _with_gather(copies_of, n, shapes, [pltpu.SemaphoreType.DMA((4 * n,))] * 2, gather, name, xs)


def _scatter_chips(ps, name, gather=None):
    n = len(ps)

    def copies_of(p_refs, out_refs, sems):
        sends = []
        for i in range(n):
            sends += _chip_sends(p_refs[i], out_refs[i], sems[2 * i], sems[2 * i + 1])
        return sends

    shapes = tuple(_sds((3,) + p.shape[1:], p.dtype) for p in ps)
    return _with_gather(copies_of, n, shapes, [pltpu.SemaphoreType.DMA((3,))] * (2 * n), gather, name, ps)


def _add_sibling(x, got, tr, name):
    _, r, c = x.shape

    def body(x_ref, g_ref, o_ref):
        mc = lax.axis_index("c")
        for j in range(4):
            mine = jnp.where(mc == 0, x_ref[2 * j].astype(F32), x_ref[2 * j + 1].astype(F32))
            o_ref[j] = (mine + g_ref[j].astype(F32)).astype(o_ref.dtype)

    return _pcall(body, name=name, grid=(r // tr,), out_shape=_sds(got.shape, got.dtype),
                  in_specs=[pl.BlockSpec((NDEV, tr, c), lambda t: (0, t, 0)), pl.BlockSpec((4, tr, c), lambda t: (0, t, 0))],
                  out_specs=pl.BlockSpec((4, tr, c), lambda t: (0, t, 0)))(x, got)


def _sum_chips(part, recv, tr, name):
    _, r, c = part.shape

    def body(p_ref, r_ref, o_ref):
        slot = 2 * lax.axis_index("x") + lax.axis_index("y")
        acc = p_ref[0].astype(F32)
        for j in range(1, 4):
            acc = jnp.where(slot == j, p_ref[j].astype(F32), acc)
        for j in range(3):
            acc = acc + r_ref[j].astype(F32)
        o_ref[...] = acc

    return _pcall(body, name=name, grid=(r // tr,), out_shape=_sds((r, c), F32),
                  in_specs=[pl.BlockSpec((4, tr, c), lambda t: (0, t, 0)), pl.BlockSpec((3, tr, c), lambda t: (0, t, 0))],
                  out_specs=pl.BlockSpec((tr, c), lambda t: (t, 0)))(part, recv)


def _sum_slots(x, tr, name):
    n, r, c = x.shape

    def body(x_ref, o_ref):
        acc = x_ref[0].astype(F32)
        for s in range(1, n):
            acc = acc + x_ref[s].astype(F32)
        o_ref[...] = acc

    return _pcall(body, name=name, grid=(r // tr,), out_shape=_sds((r, c), F32),
                  in_specs=[pl.BlockSpec((n, tr, c), lambda t: (0, t, 0))],
                  out_specs=pl.BlockSpec((tr, c), lambda t: (t, 0)))(x)


def _ada_front(a_loc, w_loc, b_loc, wpack):
    ncol = w_loc.shape[1]
    nrow = NDEV * a_loc.shape[0]

    def body(a_ref, w_ref, b_ref, wp_ref, araw_ref, mloc_ref, mall_ref, wall_ref,
             a_vm, w_vm, m_vm, lsem, *sems):
        a_start, a_forward, a_finish = _gather_phases(a_ref, araw_ref, *sems[0:3])
        m_start, m_forward, m_finish = _gather_phases(mloc_ref, mall_ref, *sems[3:6])
        w_start, w_forward, w_finish = _gather_phases(wp_ref, wall_ref, *sems[6:9])
        w_in = pltpu.make_async_copy(w_ref, w_vm, lsem.at[0])
        w_in.start()
        a_start()
        w_start()
        a_forward()
        a_finish()
        a_in = pltpu.make_async_copy(araw_ref, a_vm, lsem.at[1])
        a_in.start()
        a_in.wait()
        w_in.wait()
        a = a_vm[...].reshape(nrow, D)
        act = (a * _sigmoid(a)).astype(BF16)
        m_vm[...] = _mm(act, w_vm[...].astype(BF16)) + b_ref[...]
        m_out = pltpu.make_async_copy(m_vm, mloc_ref, lsem.at[2])
        m_out.start()
        m_out.wait()
        m_start()
        m_forward()
        m_finish()
        w_forward()
        w_finish()

    return pl.pallas_call(
        body, name="ada_front",
        out_shape=(_sds((NDEV,) + a_loc.shape, F32), _sds((nrow, ncol), F32), _sds((NDEV, nrow, ncol), F32),
                   _sds((NDEV,) + wpack.shape, wpack.dtype)),
        in_specs=[ANY, ANY, VMEM, ANY], out_specs=(ANY, ANY, ANY, ANY),
        scratch_shapes=[pltpu.VMEM((NDEV,) + a_loc.shape, F32), pltpu.VMEM(w_loc.shape, F32),
                        pltpu.VMEM((nrow, ncol), F32), pltpu.SemaphoreType.DMA((3,))] + _GATHER_SEMS * 3,
        compiler_params=pltpu.CompilerParams(vmem_limit_bytes=32 * MIB),
    )(a_loc, w_loc, b_loc, wpack)


def _ada_bwd(a_raw, cctx_col, g_all, g_cols, w_loc, nb):
    nrow = a_raw.shape[0]
    ncol = w_loc.shape[1]

    def body(a_ref, cc_ref, gall_ref, g_ref, w_ref, dw_ref, pc_ref, gb_ref):
        a = a_ref[...]
        rowid = lax.broadcasted_iota(jnp.int32, (nrow, 1), 0) % 8
        act = jnp.where(rowid < nb, a * _sigmoid(a), 0.0).astype(BF16)
        g = g_ref[...]
        gc = _rowsum(jnp.where(rowid == nb, g, 0.0))
        cc = cc_ref[...]
        dw_ref[...] = _mm_tn(act, g.astype(BF16)) + (cc * _sigmoid(cc)) * gc
        pc_ref[...] = jnp.sum(w_ref[...] * gc, axis=1, keepdims=True)
        gb_ref[...] = _rowsum(gall_ref[...])

    return _pcall(body, name="ada_bwd",
                  out_shape=(_sds((D, ncol), F32), _sds((D, 1), F32), _sds((1, g_all.shape[1]), F32)),
                  in_specs=[VMEM] * 5, out_specs=(VMEM,) * 3, vmem_mb=48)(a_raw, cctx_col, g_all, g_cols, w_loc)


def _mod_spec(k, tpe, nrows):
    return pl.BlockSpec((1, k, D), lambda t: (jnp.minimum(t // tpe, nrows - 1), 0, 0))


def _load_ffn_weights(wall_ref, first, bufs, sems):
    fsh = FF // NDEV
    cps = []
    for j, buf in enumerate(bufs):
        for d in range(NDEV):
            cps.append(pltpu.make_async_copy(wall_ref.at[d, pl.ds((first + j) * fsh, fsh)],
                                             buf.at[pl.ds(d * fsh, fsh)], sems.at[j * NDEV + d]))
    for cp in cps:
        cp.start()
    for cp in cps:
        cp.wait()


def _token_specs(xs, tm, n_lat):
    specs = [pl.BlockSpec((tm, D), lambda t: (jnp.minimum(t, n_lat - 1), 0))]
    if len(xs) == 2:
        specs.append(pl.BlockSpec((tm, D), lambda t: (jnp.maximum(t - n_lat, 0), 0)))
    return specs


def _ffn_fwd(xs, mod3, norm_w, wall, first, *, tm, n_tiles, tpe, n_lat, name, target=None, gather=None):
    nrows = mod3.shape[0]
    r = n_tiles * tm
    nx = len(xs)
    with_loss = target is not None
    with_gather = gather is not None
    fwd_step = max(2 * n_tiles // 3, 1)

    def body(*refs):
        x_refs = refs[:nx]
        pos = nx
        if with_loss:
            tgt_ref = refs[pos]
            pos += 1
        mod_ref, nw_ref, wall_ref = refs[pos:pos + 3]
        pos += 3
        if with_gather:
            gin_ref = refs[pos]
            pos += 1
        xo_ref, a_ref, b_ref, o_ref = refs[pos:pos + 4]
        pos += 4
        if with_loss:
            ls_ref = refs[pos]
            pos += 1
        if with_gather:
            gout_ref = refs[pos]
            pos += 1
        w1_ref, w3_ref, w2_ref, wsem, acc_ref = refs[pos:pos + 5]
        t = pl.program_id(0)
        if with_gather:
            g_start, g_forward, g_finish = _gather_phases(gin_ref, gout_ref, *refs[pos + 5:])

        @pl.when(t == 0)
        def _():
            if with_gather:
                g_start()
            _load_ffn_weights(wall_ref, first, (w1_ref, w3_ref, w2_ref), wsem)
            if with_loss:
                ls_ref[...] = jnp.zeros_like(ls_ref)

        if with_gather:
            @pl.when(t == fwd_step)
            def _():
                g_forward()

            @pl.when(t == n_tiles - 1)
            def _():
                g_finish()

        x = x_refs[0][...]
        if nx == 2:
            x = jnp.where(t < n_lat, x, x_refs[1][...])
        n = x * _rms(x) * nw_ref[...]
        shift, scale, gate = mod_ref[0, 0:1, :], mod_ref[0, 1:2, :], mod_ref[0, 2:3, :]
        h = (n * (1.0 + scale) + shift).astype(BF16)
        nch = FF // FC
        o = None
        for lo_c, hi_c in ((0, 4), (4, 8), (8, nch)):
            for j in range(lo_c, hi_c):
                sl = slice(j * FC, (j + 1) * FC)
                a = _mm_nt(h, w1_ref[sl, :])
                b = _mm_nt(h, w3_ref[sl, :])
                a_ref[:, sl] = a.astype(BF16)
                b_ref[:, sl] = b.astype(BF16)
                acc_ref[:, sl] = (a * _sigmoid(a) * b).astype(BF16)
            gs = slice(lo_c * FC, hi_c * FC)
            part = _mm(acc_ref[:, gs], w2_ref[gs, :])
            o = part if o is None else o + part
        o_ref[...] = o.astype(BF16)
        out = x + (0.5 * gate) * o
        if with_loss:
            d = out - tgt_ref[...]
            xo_ref[...] = d * (1.0 / D)
            ls_ref[...] += jnp.sum(d * d)
        else:
            xo_ref[...] = out

    row = lambda cols: pl.BlockSpec((tm, cols), lambda t: (t, 0))
    in_specs = _token_specs(xs, tm, n_lat) + ([row(D)] if with_loss else []) + [
        _mod_spec(3, tpe, nrows), _const((1, D)), ANY]
    out_shape = [_sds((r, D), F32), _sds((r, FF), BF16), _sds((r, FF), BF16), _sds((r, D), BF16)]
    out_specs = [row(D), row(FF), row(FF), row(D)]
    scratch = [pltpu.VMEM((FF, D), BF16)] * 3 + [pltpu.SemaphoreType.DMA((3 * NDEV,)), pltpu.VMEM((tm, FF), BF16)]
    if with_loss:
        out_shape.append(_sds((8, LANE), F32))
        out_specs.append(_const((8, LANE)))
    args = list(xs) + ([target] if with_loss else []) + [mod3, norm_w, wall]
    if with_gather:
        assert n_tiles >= 2
        in_specs.append(ANY)
        args.append(gather)
        out_shape.append(_sds((NDEV,) + gather.shape, gather.dtype))
        out_specs.append(ANY)
        scratch += _GATHER_SEMS
    return _pcall(
        body, name=name, grid=(n_tiles,), out_shape=tuple(out_shape), in_specs=in_specs, out_specs=tuple(out_specs),
        scratch=scratch, vmem_mb=56)(*args)


def _ffn_bwd_dx(dout, xs, a, b, o, mod3, norm_w, wall, first, *, tm, n_tiles, tpe, n_lat, name):
    nrows = mod3.shape[0]
    r = n_tiles * tm
    nx = len(xs)

    def body(*refs):
        dout_ref = refs[0]
        x_refs = refs[1:1 + nx]
        (a_ref, b_ref, o_ref, mod_ref, nw_ref, wall_ref,
         dx_ref, da_ref, db_ref, g_ref, do_ref, h_ref, dmod_ref, dnw_ref,
         w1_ref, w3_ref, w2_ref, wsem) = refs[1 + nx:]
        t = pl.program_id(0)

        @pl.when(t == 0)
        def _():
            _load_ffn_weights(wall_ref, first, (w1_ref, w3_ref, w2_ref), wsem)
            dnw_ref[...] = jnp.zeros_like(dnw_ref)

        @pl.when(jnp.where(t < n_lat, t % tpe == 0, t == n_lat))
        def _():
            dmod_ref[...] = jnp.zeros_like(dmod_ref)

        x = x_refs[0][...]
        if nx == 2:
            x = jnp.where(t < n_lat, x, x_refs[1][...])
        dout = dout_ref[...]
        shift, scale, gate = mod_ref[0, 0:1, :], mod_ref[0, 1:2, :], mod_ref[0, 2:3, :]
        d_o = ((0.5 * gate) * dout).astype(BF16)
        do_ref[...] = d_o
        nch = FF // FC
        groups = ((0, 4), (4, 8), (8, nch))
        dh = None
        for lo_c, hi_c in groups:
            for j in range(lo_c, hi_c):
                sl = slice(j * FC, (j + 1) * FC)
                av = a_ref[:, sl].astype(F32)
                bv = b_ref[:, sl].astype(F32)
                dg = _mm_nt(d_o, w2_ref[sl, :])
                sig = _sigmoid(av)
                sa = av * sig
                g_ref[:, sl] = (sa * bv).astype(BF16)
                da_ref[:, sl] = (dg * bv * (sig * (1.0 + av * (1.0 - sig)))).astype(BF16)
                db_ref[:, sl] = (dg * sa).astype(BF16)
            gs = slice(lo_c * FC, hi_c * FC)
            part = _mm(da_ref[:, gs], w1_ref[gs, :]) + _mm(db_ref[:, gs], w3_ref[gs, :])
            dh = part if dh is None else dh + part
        rr = _rms(x)
        xh = x * rr
        nw = nw_ref[...]
        n = xh * nw
        h_ref[...] = (n * (1.0 + scale) + shift).astype(BF16)
        dgate = _rowsum(0.5 * o_ref[...].astype(F32) * dout)
        dn = dh * (1.0 + scale)
        dxh = dn * nw
        dmod_ref[0, 0:1, :] += _rowsum(dh)
        dmod_ref[0, 1:2, :] += _rowsum(dh * n)
        dmod_ref[0, 2:3, :] += dgate
        dnw_ref[...] += _rowsum(dn * xh)
        dx = dout + rr * (dxh - xh * jnp.mean(dxh * xh, axis=-1, keepdims=True))
        if n_tiles == n_lat:
            dx_ref[...] = dx
        else:
            @pl.when(t < n_lat)
            def _():
                dx_ref[...] = dx

    row = lambda cols: pl.BlockSpec((tm, cols), lambda t: (t, 0))
    lat = pl.BlockSpec((tm, D), lambda t: (jnp.minimum(t, n_lat - 1), 0))
    out_shape = [_sds((n_lat * tm, D), F32), _sds((r, FF), BF16), _sds((r, FF), BF16), _sds((r, FF), BF16),
                 _sds((r, D), BF16), _sds((r, D), BF16), _sds((nrows, 3, D), F32), _sds((1, D), F32)]
    in_specs = [row(D)] + _token_specs(xs, tm, n_lat) + [row(FF), row(FF), row(D), _mod_spec(3, tpe, nrows),
                                                          _const((1, D)), ANY]
    out_specs = [lat, row(FF), row(FF), row(FF), row(D), row(D), _mod_spec(3, tpe, nrows), _const((1, D))]
    scratch = [pltpu.VMEM((FF, D), BF16)] * 3 + [pltpu.SemaphoreType.DMA((3 * NDEV,))]
    args = [dout, *xs, a, b, o, mod3, norm_w, wall]
    return _pcall(body, name=name, grid=(n_tiles,), out_shape=tuple(out_shape), in_specs=in_specs,
                  out_specs=tuple(out_specs), scratch=scratch, vmem_mb=60)(*args)


def _ffn_bwd_dw(h, d_o, da, db, g, *, tr, name):
    r = h.shape[0]
    fh = FF // 2
    fsh = FF // NDEV
    nk = r // tr

    def body(h_ref, do_ref, da_ref, db_ref, g_ref, out_ref, acc1, acc3, acc2):
        k = pl.program_id(1)

        @pl.when(k == 0)
        def _():
            acc1[...] = jnp.zeros_like(acc1)
            acc3[...] = jnp.zeros_like(acc3)
            acc2[...] = jnp.zeros_like(acc2)

        hv = h_ref[...]
        acc1[...] += _mm_tn(da_ref[...], hv)
        acc3[...] += _mm_tn(db_ref[...], hv)
        acc2[...] += _mm_tn(g_ref[...], do_ref[...])

        @pl.when(k == nk - 1)
        def _():
            for i, acc in enumerate((acc1, acc3, acc2)):
                out_ref[:, i * fsh:(i + 1) * fsh, :] = acc[...].reshape(NDEV // 2, fsh, D).astype(BF16)

    rowd = pl.BlockSpec((tr, D), lambda f, k: (k, 0))
    rowf = pl.BlockSpec((tr, fh), lambda f, k: (k, f))
    return _pcall(
        body, name=name, grid=(2, nk), out_shape=_sds((NDEV, 3 * fsh, D), BF16),
        in_specs=[rowd, rowd, rowf, rowf, rowf],
        out_specs=pl.BlockSpec((NDEV // 2, 3 * fsh, D), lambda f, k: (f, 0, 0)),
        scratch=[pltpu.VMEM((fh, D), F32)] * 3, vmem_mb=56)(h, d_o, da, db, g)


def _direct_sends(x_ref, out_ref, send_sems, recv_sems):
    mx, my, mc = lax.axis_index("x"), lax.axis_index("y"), lax.axis_index("c")
    sends = []
    for k in range(1, NDEV):
        px = 1 - mx if (k & 4) else mx
        py = 1 - my if (k & 2) else my
        pc = 1 - mc if (k & 1) else mc
        sends.append(pltpu.make_async_remote_copy(
            src_ref=x_ref.at[4 * px + 2 * py + pc], dst_ref=out_ref.at[k - 1],
            send_sem=send_sems.at[k - 1], recv_sem=recv_sems.at[k - 1], device_id=(px, py, pc), device_id_type=MESH))
    return sends


def _sum_direct(x, recv, tr, name):
    _, r, c = x.shape

    def body(x_ref, r_ref, o_ref):
        me = 4 * lax.axis_index("x") + 2 * lax.axis_index("y") + lax.axis_index("c")
        acc = x_ref[0].astype(F32)
        for j in range(1, NDEV):
            acc = jnp.where(me == j, x_ref[j].astype(F32), acc)
        for j in range(NDEV - 1):
            acc = acc + r_ref[j].astype(F32)
        o_ref[...] = acc

    return _pcall(body, name=name, grid=(r // tr,), out_shape=_sds((r, c), F32),
                  in_specs=[pl.BlockSpec((NDEV, tr, c), lambda t: (0, t, 0)),
                            pl.BlockSpec((NDEV - 1, tr, c), lambda t: (0, t, 0))],
                  out_specs=pl.BlockSpec((tr, c), lambda t: (t, 0)))(x, recv)


def _exchange_behind(x_ref, recv_ref, send_sems, recv_sems, first, last):
    sends = _direct_sends(x_ref, recv_ref, send_sems, recv_sems)

    @pl.when(first)
    def _():
        for cp in sends:
            cp.start()

    @pl.when(last)
    def _():
        for cp in sends:
            cp.wait_recv()
        for cp in sends:
            cp.wait_send()


def _ffn_bwd_dw_one(lhs, rhs, *, tr, name, part=None):
    r = lhs.shape[0]
    fsh = FF // NDEV
    nk = r // tr
    fused = part is not None
    nslot = NDEV - 1

    def body(*refs):
        if fused:
            lhs_ref, rhs_ref, part_ref, out_ref, recv_ref, acc, send_sems, recv_sems = refs
        else:
            lhs_ref, rhs_ref, out_ref, acc = refs
        k = pl.program_id(0)
        if fused:
            _exchange_behind(part_ref, recv_ref, send_sems, recv_sems, k == 0, k == nk - 1)

        @pl.when(k == 0)
        def _():
            acc[...] = jnp.zeros_like(acc)

        acc[...] += _mm_tn(lhs_ref[...], rhs_ref[...])

        @pl.when(k == nk - 1)
        def _():
            out_ref[...] = acc[...].reshape(NDEV, fsh, D).astype(BF16)

    in_specs = [pl.BlockSpec((tr, FF), lambda k: (k, 0)), pl.BlockSpec((tr, D), lambda k: (k, 0))]
    out_shape = [_sds((NDEV, fsh, D), BF16)]
    out_specs = [_const((NDEV, fsh, D))]
    scratch = [pltpu.VMEM((FF, D), F32)]
    args = [lhs, rhs]
    if fused:
        in_specs.append(ANY)
        args.append(part)
        out_shape.append(_sds((nslot,) + part.shape[1:], part.dtype))
        out_specs.append(ANY)
        scratch += [pltpu.SemaphoreType.DMA((nslot,))] * 2
    res = _pcall(body, name=name, grid=(nk,), out_shape=tuple(out_shape), in_specs=in_specs,
                 out_specs=tuple(out_specs), scratch=scratch, vmem_mb=48)(*args)
    return res if fused else res[0]


_PIECES =((0, 128), (128, 384), (384, 896), (896, 1408), (1408, 1536))


def _proj_fwd(x1, mod2, norm_w, wint, *, tm, n_tiles, tpe, name="proj_fwd"):
    nrows = mod2.shape[0]
    r = n_tiles * tm

    def body(x_ref, mod_ref, nw_ref, w_ref, ckv_ref, q_ref, u_ref, v_ref, kpe_ref):
        x = x_ref[...]
        n = x * _rms(x) * nw_ref[...]
        h = (n * (1.0 + mod_ref[0, 1:2, :]) + mod_ref[0, 0:1, :]).astype(BF16)
        proj = _mm_nt(h, w_ref[...])
        for (lo, hi), ref in zip(_PIECES, (ckv_ref, q_ref, u_ref, v_ref, kpe_ref)):
            ref[...] = proj[:, lo:hi]

    row = lambda cols: pl.BlockSpec((tm, cols), lambda t: (t, 0))
    widths = [hi - lo for lo, hi in _PIECES]
    return _pcall(
        body, name=name, grid=(n_tiles,),
        out_shape=tuple(_sds((r, w), F32) for w in widths),
        in_specs=[row(D), _mod_spec(2, tpe, nrows), _const((1, D)), _const((WIN_ROWS, D))],
        out_specs=tuple(row(w) for w in widths), vmem_mb=40)(x1, mod2, norm_w, wint)


def _proj_bwd(dckv, dkpe, dq, du, dv, dx2, x1, mod2, norm_w, wint, *, tm, n_tiles, tpe, n_lat, name="proj_bwd"):
    nrows = mod2.shape[0]
    r = n_tiles * tm

    def body(dckv_ref, dkpe_ref, dq_ref, du_ref, dv_ref, dx2_ref, x_ref, mod_ref, nw_ref, w_ref,
             dx_ref, dw_ref, dmod_ref, dnw_ref):
        t = pl.program_id(0)
        is_lat = t < n_lat

        @pl.when(t == 0)
        def _():
            dw_ref[...] = jnp.zeros_like(dw_ref)
            dnw_ref[...] = jnp.zeros_like(dnw_ref)

        @pl.when(jnp.where(is_lat, t % tpe == 0, t == n_lat))
        def _():
            dmod_ref[...] = jnp.zeros_like(dmod_ref)

        x = x_ref[...]
        rr = _rms(x)
        xh = x * rr
        nw = nw_ref[...]
        n = xh * nw
        scale = mod_ref[0, 1:2, :]
        h = (n * (1.0 + scale) + mod_ref[0, 0:1, :]).astype(BF16)
        zero = jnp.zeros((), BF16)
        pieces = (dckv_ref[...], jnp.where(is_lat, dq_ref[...], zero), jnp.where(is_lat, du_ref[...], zero),
                  jnp.where(is_lat, dv_ref[...], zero), dkpe_ref[...])
        dproj = jnp.concatenate(pieces, axis=-1)
        dh = _mm(dproj, w_ref[...])
        dn = dh * (1.0 + scale)
        dxh = dn * nw
        dx = rr * (dxh - xh * jnp.mean(dxh * xh, axis=-1, keepdims=True))
        dx_ref[...] = dx + jnp.where(is_lat, dx2_ref[...], 0.0)
        dmod_ref[0, 0:1, :] += _rowsum(dh)
        dmod_ref[0, 1:2, :] += _rowsum(dh * n)
        dnw_ref[...] += _rowsum(dn * xh)
        dw_ref[...] += _mm_tn(dproj, h)

    row = lambda cols: pl.BlockSpec((tm, cols), lambda t: (t, 0))
    lat = lambda cols: pl.BlockSpec((tm, cols), lambda t: (jnp.minimum(t, n_lat - 1), 0))
    return _pcall(
        body, name=name, grid=(n_tiles,),
        out_shape=(_sds((r, D), F32), _sds((WIN_ROWS, D), F32), _sds((nrows, 2, D), F32), _sds((1, D), F32)),
        in_specs=[row(128), row(128), lat(256), lat(512), lat(512), lat(D), row(D), _mod_spec(2, tpe, nrows),
                  _const((1, D)), _const((WIN_ROWS, D))],
        out_specs=(row(D), _const((WIN_ROWS, D)), _mod_spec(2, tpe, nrows), _const((1, D))),
        vmem_mb=48)(dckv, dkpe, dq, du, dv, dx2, x1, mod2, norm_w, wint)


def _seg_sum(x, seg):
    return _mm(x.astype(BF16), seg)


def _seg_bcast(v, segt2):
    hi = v.astype(BF16)
    lo = (v - hi.astype(F32)).astype(BF16)
    return _mm(jnp.concatenate([hi, lo], axis=-1), segt2)


def _rope_pairs(t, cos, sin, rot2):
    cos2, sin2 = jnp.concatenate([cos, cos], axis=-1), jnp.concatenate([sin, sin], axis=-1)
    out = []
    for j in range(H // 2):
        tj = t[:, 2 * j * LANE:2 * (j + 1) * LANE]
        out.append(tj * cos2 + _dot_hl(tj, rot2) * sin2)
    return jnp.concatenate(out, axis=-1)


def _head_norm_rope(x, w_pad, cos, sin, seg, segt2, rot2, rope=True):
    rh = lax.rsqrt(_seg_sum(x * x, seg) * (1.0 / DH) + EPS)
    rb = _seg_bcast(rh, segt2)
    y = x * rb
    out = _rope_pairs(y * w_pad, cos, sin, rot2) if rope else None
    return out, y, rb


def _head_norm_rope_bwd(dout, y, rb, w_pad, cos, sin, seg, segt2, rot2_t):
    cos2, sin2 = jnp.concatenate([cos, cos], axis=-1), jnp.concatenate([sin, sin], axis=-1)
    dt = []
    for j in range(H // 2):
        dj = dout[:, 2 * j * LANE:2 * (j + 1) * LANE]
        dt.append(dj * cos2 + _dot_hl(dj * sin2, rot2_t))
    dt = jnp.concatenate(dt, axis=-1)
    dw = _rowsum(dt * y)
    dy = dt * w_pad
    mean_h = _seg_sum(dy * y, seg) * (1.0 / DH)
    return rb * (dy - y * _seg_bcast(mean_h, segt2)), dw


def _q_prep_fwd(qp, qa_w, wuq, wq, cos, sin, cs, *, tm, n_lat, tpe):
    def body(qp_ref, qa_ref, wuq_ref, wq_ref, cos_ref, sin_ref, seg, segt, rot, q_ref):
        x = qp_ref[...]
        cq = (x * _rms(x) * qa_ref[...]).astype(BF16)
        q, _, _ = _head_norm_rope(_mm_nt(cq, wuq_ref[...]), wq_ref[...], cos_ref[...], sin_ref[...],
                                  seg[...], segt[...], rot[...])
        q_ref[...] = q.astype(BF16)

    row = lambda cols: pl.BlockSpec((tm, cols), lambda t: (t, 0))
    tab = pl.BlockSpec((tm, LANE), lambda t: (t % tpe, 0))
    return _pcall(
        body, name="q_prep_fwd", grid=(n_lat,), out_shape=_sds((n_lat * tm, HP), BF16),
        in_specs=[row(QL), _const((1, QL)), _const((HP, QL)), _const((1, HP)), tab, tab,
                  _const((HP, LANE)), _const((2 * LANE, HP)), _const((2 * LANE, 2 * LANE))],
        out_specs=row(HP))(qp, qa_w, wuq, wq, cos, sin, cs["seg_h"], cs["seg_ht"], cs["rot"])


def _q_prep_bwd(dq, qp, qa_w, wuq, wq, cos, sin, cs, *, tm, n_lat, tpe):
    def body(dq_ref, qp_ref, qa_ref, wuq_ref, wq_ref, cos_ref, sin_ref, seg, segt, rot, rot_t,
             dqp_ref, dwuq_ref, dqa_ref, dwq_ref):
        @pl.when(pl.program_id(0) == 0)
        def _():
            dwuq_ref[...] = jnp.zeros_like(dwuq_ref)
            dqa_ref[...] = jnp.zeros_like(dqa_ref)
            dwq_ref[...] = jnp.zeros_like(dwq_ref)

        x = qp_ref[...]
        ra = _rms(x)
        xh = x * ra
        qa = qa_ref[...]
        cq = (xh * qa).astype(BF16)
        wuq_v = wuq_ref[...]
        wq_v, cos_v, sin_v = wq_ref[...], cos_ref[...], sin_ref[...]
        _, y, rb = _head_norm_rope(_mm_nt(cq, wuq_v), wq_v, cos_v, sin_v, seg[...], segt[...], rot[...], rope=False)
        dqraw, dwq = _head_norm_rope_bwd(dq_ref[...], y, rb, wq_v, cos_v, sin_v, seg[...], segt[...], rot_t[...])
        dqraw = dqraw.astype(BF16)
        dcq = _mm(dqraw, wuq_v)
        dxh = dcq * qa
        dqp_ref[...] = (ra * (dxh - xh * jnp.mean(dxh * xh, axis=-1, keepdims=True))).astype(BF16)
        dwuq_ref[...] += _mm_tn(dqraw, cq)
        dqa_ref[...] += _rowsum(dcq * xh)
        dwq_ref[...] += dwq

    row = lambda cols: pl.BlockSpec((tm, cols), lambda t: (t, 0))
    tab = pl.BlockSpec((tm, LANE), lambda t: (t % tpe, 0))
    return _pcall(
        body, name="q_prep_bwd", grid=(n_lat,),
        out_shape=(_sds((n_lat * tm, QL), BF16), _sds((HP, QL), F32), _sds((1, QL), F32), _sds((1, HP), F32)),
        in_specs=[row(HP), row(QL), _const((1, QL)), _const((HP, QL)), _const((1, HP)), tab, tab,
                  _const((HP, LANE)), _const((2 * LANE, HP)), _const((2 * LANE, 2 * LANE)), _const((2 * LANE, 2 * LANE))],
        out_specs=(row(QL), _const((HP, QL)), _const((1, QL)), _const((1, HP))), vmem_mb=40)(
            dq, qp, qa_w, wuq, wq, cos, sin, cs["seg_h"], cs["seg_ht"], cs["rot"], cs["rot_t"])


def _kv_tab_spec(tm, tpe, n_lat):
    return pl.BlockSpec((tm, LANE), lambda t: (jnp.where(t < n_lat, t % tpe, tpe), 0))


def _split_kv(kv, kpe):
    low = lax.broadcasted_iota(jnp.int32, (kv.shape[0], LANE), 1) < DN
    kx, v = [], []
    for h in range(H):
        blk = kv[:, h * LANE:(h + 1) * LANE]
        kx.append(jnp.where(low, blk, kpe))
        v.append(jnp.where(low, pltpu.roll(blk, DN, 1), 0.0))
    return jnp.concatenate(kx, axis=-1), jnp.concatenate(v, axis=-1)


def _kv_prep_fwd(ckv, kpe, kva_w, wukv, wk, cosk, sink, cs, *, tm, n_tiles, tpe, n_lat):
    def body(ckv_ref, kpe_ref, kva_ref, wukv_ref, wk_ref, cos_ref, sin_ref, seg, segt, rot, k_ref, v_ref):
        x = ckv_ref[...]
        ckvn = (x * _rms(x) * kva_ref[...]).astype(BF16)
        kx, v = _split_kv(_mm_nt(ckvn, wukv_ref[...]), kpe_ref[...])
        k, _, _ = _head_norm_rope(kx, wk_ref[...], cos_ref[...], sin_ref[...], seg[...], segt[...], rot[...])
        k_ref[...] = k.astype(BF16)
        v_ref[...] = v.astype(BF16)

    row = lambda cols: pl.BlockSpec((tm, cols), lambda t: (t, 0))
    tab = _kv_tab_spec(tm, tpe, n_lat)
    r = n_tiles * tm
    return _pcall(
        body, name="kv_prep_fwd", grid=(n_tiles,), out_shape=(_sds((r, HP), BF16), _sds((r, HP), BF16)),
        in_specs=[row(KVL), row(LANE), _const((1, KVL)), _const((HP, KVL)), _const((1, HP)), tab, tab,
                  _const((HP, LANE)), _const((2 * LANE, HP)), _const((2 * LANE, 2 * LANE))],
        out_specs=(row(HP), row(HP)), vmem_mb=40)(
            ckv, kpe, kva_w, wukv, wk, cosk, sink, cs["seg_h"], cs["seg_ht"], cs["rot"])


def _kv_prep_bwd(dks, dvs, ckv, kpe, kva_w, wukv, wk, cosk, sink, cs, *, tm, n_tiles, tpe, n_lat):
    def body(dkl_ref, dkc_ref, dvl_ref, dvc_ref, ckv_ref, kpe_ref, kva_ref, wukv_ref, wk_ref, cos_ref, sin_ref,
             seg, segt, rot, rot_t, dckv_ref, dkpe_ref, dwukv_ref, dkva_ref, dwk_ref):
        t = pl.program_id(0)
        is_lat = t < n_lat

        @pl.when(t == 0)
        def _():
            dwukv_ref[...] = jnp.zeros_like(dwukv_ref)
            dkva_ref[...] = jnp.zeros_like(dkva_ref)
            dwk_ref[...] = jnp.zeros_like(dwk_ref)

        dk = jnp.where(is_lat, dkl_ref[...], dkc_ref[...])
        dv = jnp.where(is_lat, dvl_ref[...], dvc_ref[...])
        x = ckv_ref[...]
        ra = _rms(x)
        xh = x * ra
        kva = kva_ref[...]
        ckvn = (xh * kva).astype(BF16)
        wukv_v = wukv_ref[...]
        wk_v, cos_v, sin_v = wk_ref[...], cos_ref[...], sin_ref[...]
        kx, _ = _split_kv(_mm_nt(ckvn, wukv_v), kpe_ref[...])
        _, y, rb = _head_norm_rope(kx, wk_v, cos_v, sin_v, seg[...], segt[...], rot[...], rope=False)
        dkx, dwk = _head_norm_rope_bwd(dk, y, rb, wk_v, cos_v, sin_v, seg[...], segt[...], rot_t[...])
        dkpe = dkx[:, 0:LANE]
        for h in range(1, H):
            dkpe = dkpe + dkx[:, h * LANE:(h + 1) * LANE]
        lane = lax.broadcasted_iota(jnp.int32, (tm, LANE), 1)
        dkpe_ref[...] = jnp.where((lane >= DN) & (lane < DH), dkpe, 0.0).astype(BF16)
        dkv = jnp.concatenate([jnp.where(lane < DN, dkx[:, h * LANE:(h + 1) * LANE],
                                         pltpu.roll(dv[:, h * LANE:(h + 1) * LANE], DN, 1)) for h in range(H)],
                              axis=-1).astype(BF16)
        dckvn = _mm(dkv, wukv_v)
        dxh = dckvn * kva
        dckv_ref[...] = (ra * (dxh - xh * jnp.mean(dxh * xh, axis=-1, keepdims=True))).astype(BF16)
        dwukv_ref[...] += _mm_tn(dkv, ckvn)
        dkva_ref[...] += _rowsum(dckvn * xh)
        dwk_ref[...] += dwk

    row = lambda cols: pl.BlockSpec((tm, cols), lambda t: (t, 0))
    lat = pl.BlockSpec((tm, HP), lambda t: (jnp.minimum(t, n_lat - 1), 0))
    ctx = pl.BlockSpec((tm, HP), lambda t: (jnp.maximum(t - n_lat, 0), 0))
    tab = _kv_tab_spec(tm, tpe, n_lat)
    r = n_tiles * tm
    return _pcall(
        body, name="kv_prep_bwd", grid=(n_tiles,),
        out_shape=(_sds((r, KVL), BF16), _sds((r, LANE), BF16), _sds((HP, KVL), F32), _sds((1, KVL), F32),
                   _sds((1, HP), F32)),
        in_specs=[lat, ctx, lat, ctx, row(KVL), row(LANE), _const((1, KVL)), _const((HP, KVL)), _const((1, HP)),
                  tab, tab, _const((HP, LANE)), _const((2 * LANE, HP)), _const((2 * LANE, 2 * LANE)),
                  _const((2 * LANE, 2 * LANE))],
        out_specs=(row(KVL), row(LANE), _const((HP, KVL)), _const((1, KVL)), _const((1, HP))), vmem_mb=48)(
            dks[0], dks[1], dvs[0], dvs[1], ckv, kpe, kva_w, wukv, wk, cosk, sink,
            cs["seg_h"], cs["seg_ht"], cs["rot"], cs["rot_t"])


_SCALE = DH ** -0.5
_SCALE_LOG2E = _SCALE * 1.4426950408889634


def _key_chunks(s, nc, ck):
    return ([(0, lo, min(lo + ck, s)) for lo in range(0, s, ck)]
            + [(1, lo, min(lo + ck, nc)) for lo in range(0, nc, ck)])


def _attn_fwd(q, k, v, *, nb, s, nc, tq, ck):
    tpe = s // tq
    r_lat = nb * s
    chunks = _key_chunks(s, nc, ck)
    hp = 4

    def body(q_ref, kl_ref, kc_ref, vl_ref, vc_ref, o_ref, lse_ref):
        k_refs, v_refs = (kl_ref, kc_ref), (vl_ref, vc_ref)
        for hh in range(hp):
            hs = slice(hh * LANE, (hh + 1) * LANE)
            qv = q_ref[:, hs]
            xs = [_mm_nt(qv, k_refs[w][lo:hi, hs]) for w, lo, hi in chunks]
            m = jnp.max(xs[0], axis=-1, keepdims=True)
            for x in xs[1:]:
                m = jnp.maximum(m, jnp.max(x, axis=-1, keepdims=True))
            l = acc = None
            for x, (w, lo, hi) in zip(xs, chunks):
                e = jnp.exp2((x - m) * _SCALE_LOG2E)
                lc = jnp.sum(e, axis=-1, keepdims=True)
                pv = _mm(e.astype(BF16), v_refs[w][lo:hi, hs])
                l = lc if l is None else l + lc
                acc = pv if acc is None else acc + pv
            o_ref[:, hs] = (acc / l).astype(BF16)
            lse = m * _SCALE_LOG2E + jnp.log2(l)
            lse_ref[hh] = jnp.transpose(jnp.broadcast_to(lse, (tq, LANE)))[0:8, :]

    qs = pl.BlockSpec((tq, hp * LANE), lambda i, j, t: (i * tpe + t, j))
    kl = pl.BlockSpec((s, hp * LANE), lambda i, j, t: (i, j))
    kc = pl.BlockSpec((nc, hp * LANE), lambda i, j, t: (r_lat // nc + i, j))
    ls = pl.BlockSpec((hp, 8, tq), lambda i, j, t: (i * (H // hp) + j, 0, t))
    return _pcall(body, name="attn_fwd", grid=(nb, H // hp, tpe),
                  out_shape=(_sds((r_lat, HP), BF16), _sds((nb * H, 8, s), F32)),
                  in_specs=[qs, kl, kc, kl, kc], out_specs=(qs, ls), vmem_mb=48)(q, k, k, v, v)


def _attn_bwd(q, k, v, o, do, lse, part, *, nb, s, nc, tq, ck):
    tpe = s // tq
    r_lat = nb * s
    chunks = _key_chunks(s, nc, ck)
    hp = 2
    n_steps = nb * (H // hp) * tpe

    def body(q_ref, kl_ref, kc_ref, vl_ref, vc_ref, o_ref, do_ref, lse_ref, part_ref,
             dq_ref, dkl_ref, dkc_ref, dvl_ref, dvc_ref, recv_ref, akl, akc, avl, avc, send_sems, recv_sems):
        t = pl.program_id(2)
        step = (pl.program_id(0) * (H // hp) + pl.program_id(1)) * tpe + t
        _exchange_behind(part_ref, recv_ref, send_sems, recv_sems, step == 0, step == n_steps - 1)

        @pl.when(t == 0)
        def _():
            akl[...] = jnp.zeros_like(akl)
            akc[...] = jnp.zeros_like(akc)
            avl[...] = jnp.zeros_like(avl)
            avc[...] = jnp.zeros_like(avc)

        k_refs, v_refs, ak, av = (kl_ref, kc_ref), (vl_ref, vc_ref), (akl, akc), (avl, avc)
        for hh in range(hp):
            hs = slice(hh * LANE, (hh + 1) * LANE)
            qv = q_ref[:, hs]
            lse = jnp.transpose(jnp.concatenate([lse_ref[hh]] * (LANE // 8), axis=0))[:, 0:1]
            dov = do_ref[:, hs]
            delta = jnp.sum(dov.astype(F32) * o_ref[:, hs].astype(F32), axis=-1, keepdims=True)
            dq = None
            for w, lo, hi in chunks:
                kc_v = k_refs[w][lo:hi, hs]
                p = jnp.exp2(_mm_nt(qv, kc_v) * _SCALE_LOG2E - lse)
                ds = (p * (_mm_nt(dov, v_refs[w][lo:hi, hs]) - delta)).astype(BF16)
                part = _mm(ds, kc_v)
                dq = part if dq is None else dq + part
                ak[w][hs, lo:hi] += _mm_tn(qv, ds)
                av[w][hs, lo:hi] += _mm_tn(dov, p.astype(BF16))
            dq_ref[:, hs] = dq * _SCALE

        @pl.when(t == tpe - 1)
        def _():
            dkl_ref[...] = akl[...].T * _SCALE
            dkc_ref[...] = akc[...].T * _SCALE
            dvl_ref[...] = avl[...].T
            dvc_ref[...] = avc[...].T

    qs = pl.BlockSpec((tq, hp * LANE), lambda i, j, t: (i * tpe + t, j))
    kl = pl.BlockSpec((s, hp * LANE), lambda i, j, t: (i, j))
    kc = pl.BlockSpec((nc, hp * LANE), lambda i, j, t: (r_lat // nc + i, j))
    kc_out = pl.BlockSpec((nc, hp * LANE), lambda i, j, t: (i, j))
    ls = pl.BlockSpec((hp, 8, tq), lambda i, j, t: (i * (H // hp) + j, 0, t))
    return _pcall(
        body, name="attn_bwd", grid=(nb, H // hp, tpe),
        out_shape=(_sds((r_lat, HP), F32), _sds((r_lat, HP), F32), _sds((nb * nc, HP), F32),
                   _sds((r_lat, HP), F32), _sds((nb * nc, HP), F32), _sds((NDEV - 1,) + part.shape[1:], part.dtype)),
        in_specs=[qs, kl, kc, kl, kc, qs, qs, ls, ANY], out_specs=(qs, kl, kc_out, kl, kc_out, ANY),
        scratch=[pltpu.VMEM((hp * LANE, s), F32), pltpu.VMEM((hp * LANE, nc), F32)] * 2
        + [pltpu.SemaphoreType.DMA((NDEV - 1,))] * 2,
        vmem_mb=60)(q, k, k, v, v, o, do, lse, part)


def _chunks_side_by_side(x, j, nch):
    return jnp.concatenate([x[c * CH:(c + 1) * CH, j * LANE:(j + 1) * LANE] for c in range(nch)], axis=-1)


def _first_group_lanes(nch):
    return (lax.broadcasted_iota(jnp.int32, (CH, nch * LANE), 1) & (LANE - 1)) < GD


def _gating(vn, ws_ref, bias_ref, s_scr, tm):
    nch = tm // CH
    first = _first_group_lanes(nch)
    for j in range(G // 2):
        ls = slice(j * LANE, (j + 1) * LANE)
        vst = _chunks_side_by_side(vn, j, nch)
        st = jnp.where(first, _mm(ws_ref[2 * j], vst), _mm(ws_ref[2 * j + 1], vst))
        for c in range(nch):
            s_scr[c * CH:(c + 1) * CH, ls] = st[:, c * LANE:(c + 1) * LANE] + bias_ref[:, ls]


def _compact_heads(x):
    low = lax.broadcasted_iota(jnp.int32, (x.shape[0], LANE), 1) < DV
    out = []
    for j in range(H // 2):
        even = x[:, 2 * j * LANE:(2 * j + 1) * LANE].astype(F32)
        odd = x[:, (2 * j + 1) * LANE:(2 * j + 2) * LANE].astype(F32)
        out.append(jnp.where(low, even, pltpu.roll(odd, DV, 1)))
    return jnp.concatenate(out, axis=-1)


def _expand_heads(x):
    low = lax.broadcasted_iota(jnp.int32, (x.shape[0], LANE), 1) < DV
    out = []
    for j in range(H // 2):
        blk = x[:, j * LANE:(j + 1) * LANE]
        out.append(jnp.where(low, blk, 0.0))
        out.append(jnp.where(low, pltpu.roll(blk, DV, 1), 0.0))
    return jnp.concatenate(out, axis=-1)


def _mix_fwd(u, v, attn, x1, gate, wv, ws, bias, wout, cs, *, tm, n_lat, tpe):
    nrows = gate.shape[0]

    def body(u_ref, v_ref, attn_ref, x_ref, gate_ref, wv_ref, ws_ref, bias_ref, wout_ref, seg, segt,
             x2_ref, mix_ref, s_scr):
        vg = _gelu(v_ref[...])
        rg = lax.rsqrt(_seg_sum(vg * vg, seg[...]) * (1.0 / GD) + EPS)
        vn = (vg * _seg_bcast(rg, segt[...]) * wv_ref[...]).astype(BF16)
        _gating(vn, ws_ref, bias_ref, s_scr, tm)
        sg = (_gelu(u_ref[...]) * s_scr[...]).astype(BF16)
        attn_c = _compact_heads(attn_ref[...]).astype(BF16)
        mix = _mm(attn_c, wout_ref[0:H * DV, :]) + _mm(sg, wout_ref[H * DV:, :])
        mix_ref[...] = mix.astype(BF16)
        x2_ref[...] = x_ref[...] + gate_ref[0] * mix

    row = lambda cols: pl.BlockSpec((tm, cols), lambda t: (t, 0))
    r = n_lat * tm
    return _pcall(
        body, name="mix_fwd", grid=(n_lat,),
        out_shape=(_sds((r, D), F32), _sds((r, D), BF16)),
        in_specs=[row(G * GD), row(G * GD), row(HP), row(D), _mod_spec(1, tpe, nrows), _const((1, G * GD)),
                  _const((G, CH, CH)), _const((CH, G * GD)), _const((D, D)), _const((G * GD, LANE)),
                  _const((2 * LANE, G * GD))],
        out_specs=(row(D), row(D)), scratch=[pltpu.VMEM((tm, G * GD), F32)], vmem_mb=40)(
            u, v, attn, x1, gate, wv, ws, bias, wout, cs["seg_g"], cs["seg_gt"])


def _mix_bwd(dx2, mix, u, v, attn, gate, wv, ws, wst, bias, wout, cs, *, tm, n_lat, tpe):
    nrows = gate.shape[0]
    wrows = H * DV + G * GD

    def body(dx2_ref, mix_ref, u_ref, v_ref, attn_ref, gate_ref, wv_ref, ws_ref, wst_ref, bias_ref, wout_ref, seg, segt,
             dattn_ref, du_ref, dv_ref, dgate_ref, dwout_ref, dws_ref, dbs_ref, dwv_ref, s_scr, dvn_scr, dbias_scr):
        t = pl.program_id(0)

        @pl.when(t == 0)
        def _():
            dwout_ref[...] = jnp.zeros_like(dwout_ref)
            dws_ref[...] = jnp.zeros_like(dws_ref)
            dwv_ref[...] = jnp.zeros_like(dwv_ref)
            dbias_scr[...] = jnp.zeros_like(dbias_scr)

        @pl.when(t % tpe == 0)
        def _():
            dgate_ref[...] = jnp.zeros_like(dgate_ref)

        dx2 = dx2_ref[...]
        dmix = (dx2 * gate_ref[0]).astype(BF16)
        dcat = _mm_nt(dmix, wout_ref[...])
        dattn_ref[...] = _expand_heads(dcat[:, :H * DV]).astype(BF16)
        dsg = dcat[:, H * DV:]

        vraw = v_ref[...]
        vg = _gelu(vraw)
        rg = lax.rsqrt(_seg_sum(vg * vg, seg[...]) * (1.0 / GD) + EPS)
        r64 = _seg_bcast(rg, segt[...])
        y = vg * r64
        wv_v = wv_ref[...]
        vn = (y * wv_v).astype(BF16)
        _gating(vn, ws_ref, bias_ref, s_scr, tm)
        uraw = u_ref[...]
        ug = _gelu(uraw)
        s = s_scr[...]
        sg = (ug * s).astype(BF16)
        du_ref[...] = (dsg * s * _gelu_grad(uraw)).astype(BF16)
        ds = dsg * ug
        dgate_ref[0] += _rowsum(dx2 * mix_ref[...].astype(F32))
        attn_c = _compact_heads(attn_ref[...]).astype(BF16)
        dwout_ref[...] += _mm_tn(jnp.concatenate([attn_c, sg], axis=-1), dmix)

        nch = tm // CH
        first = _first_group_lanes(nch)
        for c in range(nch):
            dbias_scr[...] += ds[c * CH:(c + 1) * CH, :]
        for j in range(G // 2):
            ls = slice(j * LANE, (j + 1) * LANE)
            dst32 = _chunks_side_by_side(ds, j, nch)
            dst = dst32.astype(BF16)
            vst = _chunks_side_by_side(vn, j, nch)
            dvn_st = jnp.where(first, _mm(wst_ref[2 * j], dst), _mm(wst_ref[2 * j + 1], dst))
            for c in range(nch):
                dvn_scr[c * CH:(c + 1) * CH, ls] = dvn_st[:, c * LANE:(c + 1) * LANE]
            dws_ref[2 * j] += _mm_nt(jnp.where(first, dst32, 0.0).astype(BF16), vst)
            dws_ref[2 * j + 1] += _mm_nt(jnp.where(first, 0.0, dst32).astype(BF16), vst)

        dvn = dvn_scr[...]
        dwv_ref[...] += _rowsum(dvn * y)
        dy = dvn * wv_v
        mean_g = _seg_sum(dy * y, seg[...]) * (1.0 / GD)
        dvg = r64 * (dy - y * _seg_bcast(mean_g, segt[...]))
        dv_ref[...] = (dvg * _gelu_grad(vraw)).astype(BF16)

        @pl.when(t == n_lat - 1)
        def _():
            dbs_ref[...] = _dot_hl(dbias_scr[...], seg[...])

    row = lambda cols: pl.BlockSpec((tm, cols), lambda t: (t, 0))
    r = n_lat * tm
    return _pcall(
        body, name="mix_bwd", grid=(n_lat,),
        out_shape=(_sds((r, HP), BF16), _sds((r, G * GD), BF16), _sds((r, G * GD), BF16), _sds((nrows, 1, D), F32),
                   _sds((wrows, D), F32), _sds((G, CH, CH), F32), _sds((CH, LANE), F32), _sds((1, G * GD), F32)),
        in_specs=[row(D), row(D), row(G * GD), row(G * GD), row(HP), _mod_spec(1, tpe, nrows), _const((1, G * GD)),
                  _const((G, CH, CH)), _const((G, CH, CH)), _const((CH, G * GD)), _const((wrows, D)),
                  _const((G * GD, LANE)), _const((2 * LANE, G * GD))],
        out_specs=(row(HP), row(G * GD), row(G * GD), _mod_spec(1, tpe, nrows), _const((wrows, D)),
                   _const((G, CH, CH)), _const((CH, LANE)), _const((1, G * GD))),
        scratch=[pltpu.VMEM((tm, G * GD), F32), pltpu.VMEM((tm, G * GD), F32), pltpu.VMEM((CH, G * GD), F32)],
        vmem_mb=56)(dx2, mix, u, v, attn, gate, wv, ws, wst, bias, wout, cs["seg_g"], cs["seg_gt"])


def _adamw_math(w, g, m, v):
    m2 = ADAM_B1 * m + (1.0 - ADAM_B1) * g
    v2 = ADAM_B2 * v + (1.0 - ADAM_B2) * (g * g)
    m_hat = m2 / (1.0 - ADAM_B1 ** ADAM_STEP)
    v_hat = v2 / (1.0 - ADAM_B2 ** ADAM_STEP)
    delta = -ADAM_LR * (m_hat / (jnp.sqrt(v_hat) + ADAM_EPS) + ADAM_WD * w)
    return delta, m2, v2


def _row_tile(r, c):
    best = r
    for tr in range(8, r, 8):
        if r % tr == 0 and tr * c * 4 <= MIB:
            best = tr
    return best


def _adamw(w, g, m, v, name):
    r, c = w.shape
    tr = _row_tile(r, c)

    def body(w_ref, g_ref, m_ref, v_ref, d_ref, mo_ref, vo_ref):
        d_ref[...], mo_ref[...], vo_ref[...] = _adamw_math(w_ref[...], g_ref[...], m_ref[...], v_ref[...])

    blk = pl.BlockSpec((tr, c), lambda t: (t, 0))
    return _pcall(body, name=name, grid=(r // tr,), out_shape=(_sds((r, c), F32),) * 3,
                  in_specs=[blk] * 4, out_specs=(blk,) * 3)(w, g, m, v)


def _adamw_small(params):
    n = len(params)

    def body(*refs):
        ins, outs = refs[:4 * n], refs[4 * n:]
        for i in range(n):
            w, g, m, v = (ins[4 * i + k][...] for k in range(4))
            if i == 0:
                sig = _sigmoid(w)
                g = g * (sig * (1.0 + w * (1.0 - sig)))
            d, m2, v2 = _adamw_math(w, g, m, v)
            outs[4 * i][...] = g
            outs[4 * i + 1][...] = d
            outs[4 * i + 2][...] = m2
            outs[4 * i + 3][...] = v2

    flat = [a for p in params for a in p]
    out_shape = tuple(_sds(p[0].shape, F32) for p in params for _ in range(4))
    res = _pcall(body, name="adamw_small", out_shape=out_shape, in_specs=[VMEM] * (4 * n),
                 out_specs=(VMEM,) * (4 * n))(*flat)
    return [res[4 * i:4 * i + 4] for i in range(n)]


def _rope_tables(s):
    rows = jnp.repeat(jnp.arange(s // GRID_W, dtype=F32), GRID_W)
    cols = jnp.tile(jnp.arange(GRID_W, dtype=F32), s // GRID_W)
    half = DR // 2
    inv = ROPE_BASE ** (-jnp.arange(0, half, 2, dtype=F32) / half)
    ang_r = rows[:, None] * inv
    ang_c = cols[:, None] * inv
    ang = jnp.concatenate([ang_r, ang_r, ang_c, ang_c], axis=-1)
    return jnp.cos(ang), jnp.sin(ang)


def _head_pad(a, real):
    return jnp.pad(a, ((0, 0), (0, LANE - real), (0, 0))).reshape(HP, a.shape[2])


def kernel(x, c, ctx, c_ctx, w_ada, b_ada, norm1_w, ffn1_w1, ffn1_w3, ffn1_w2, norm2_w, w_in, q_a_norm_w, w_uq, kv_a_norm_w, w_ukv, q_norm_w, k_norm_w, v_norm_w, w_s, b_s, w_out, norm3_w, ffn2_w1, ffn2_w3, ffn2_w2, loss_target, m_c_ctx, m_w_ada, m_b_ada, m_norm1_w, m_ffn1_w1, m_ffn1_w3, m_ffn1_w2, m_norm2_w, m_w_in, m_q_a_norm_w, m_w_uq, m_kv_a_norm_w, m_w_ukv, m_q_norm_w, m_k_norm_w, m_v_norm_w, m_w_s, m_b_s, m_w_out, m_norm3_w, m_ffn2_w1, m_ffn2_w3, m_ffn2_w2, v_c_ctx, v_w_ada, v_b_ada, v_norm1_w, v_ffn1_w1, v_ffn1_w3, v_ffn1_w2, v_norm2_w, v_w_in, v_q_a_norm_w, v_w_uq, v_kv_a_norm_w, v_w_ukv, v_q_norm_w, v_k_norm_w, v_v_norm_w, v_w_s, v_b_s, v_w_out, v_norm3_w, v_ffn2_w1, v_ffn2_w3, v_ffn2_w2):
    nb, s, _ = x.shape
    nc = ctx.shape[1]
    tm = 256 if nc % 256 == 0 else 128
    tpe = s // tm
    n_lat = nb * tpe
    n_all = n_lat + nb * nc // tm
    tmf = 2 * tm if s % (2 * tm) == 0 and (nb * nc) % (2 * tm) == 0 else tm
    tp = tmf
    r_lat = nb * s
    tpe_p, n_lat_p, n_all_p = s // tp, r_lat // tp, (r_lat + nb * nc) // tp
    me = 4 * lax.axis_index("x") + 2 * lax.axis_index("y") + lax.axis_index("c")
    cs = _consts()
    ncol = w_ada.shape[2]
    fsh = ffn1_w1.shape[2]
    assert nb + 1 <= 8 and NDEV * fsh == FF and NDEV * ncol == NMOD * D and s % nc == 0 and nc % tm == 0

    def t16(a):
        return a.T.astype(BF16)

    wpack1 = jnp.concatenate([t16(ffn1_w1[0]), t16(ffn1_w3[0]), ffn1_w2[0].astype(BF16)], axis=0)
    a_loc = jnp.concatenate([c, c_ctx[None, :], jnp.zeros((7 - nb, D), F32)], axis=0)
    a_raw, _, mod_all, wall1 = _ada_front(a_loc, w_ada[0], lax.dynamic_slice_in_dim(b_ada, me * ncol, ncol, axis=1),
                                          wpack1)
    a_raw = a_raw.reshape(NDEV * 8, D)
    mod_mine = lax.dynamic_slice_in_dim(mod_all, 8 * me, 8, axis=1)
    modtab = mod_mine.transpose(1, 0, 2).reshape(8, NMOD, D)[:nb + 1]
    wpack2 = jnp.concatenate([
        t16(ffn2_w1[0]), t16(ffn2_w3[0]), ffn2_w2[0].astype(BF16),
        t16(w_in[0]), jnp.zeros((12, D), BF16),
        w_out[0].astype(BF16),
        t16(w_uq[0]).reshape(24, D), jnp.zeros((8, D), BF16),
        t16(w_ukv[0]).reshape(16, D)], axis=0)

    def head_w(wn):
        return jnp.tile(jnp.pad(wn, ((0, 0), (0, LANE - DH))), (1, H))

    wq, wk = head_w(q_norm_w), head_w(k_norm_w)
    wv = v_norm_w.reshape(1, G * GD)
    ws16 = w_s[0].astype(BF16)
    wst16 = w_s[0].transpose(0, 2, 1).astype(BF16)
    bias = jnp.repeat(b_s[0].T, GD, axis=1)
    cos, sin = _rope_tables(s)
    cos = jnp.pad(cos, ((0, 0), (DN, LANE - DH)), constant_values=1.0)
    sin = jnp.pad(sin, ((0, 0), (DN, LANE - DH)))
    cos_k = jnp.concatenate([cos, jnp.ones((tm, LANE), F32)], axis=0)
    sin_k = jnp.concatenate([sin, jnp.zeros((tm, LANE), F32)], axis=0)

    xs = (x.reshape(r_lat, D), ctx.reshape(nb * nc, D))
    x1, a1, b1, o1, wall2 = _ffn_fwd(xs, modtab[:, 0:3], norm1_w, wall1, 0, tm=tmf, n_tiles=(r_lat + nb * nc) // tmf,
                                     tpe=s // tmf, n_lat=r_lat // tmf, name="ffn1_fwd", gather=wpack2)

    o0 = 3 * fsh
    wint = wall2[:, o0:o0 + 180].reshape(IN_COLS, D)
    z = lambda n: jnp.zeros((n, D), BF16)
    wint = jnp.concatenate([wint[0:128], wint[160:416], wint[416:928], wint[928:1440],
                            z(DN), wint[128:160], z(LANE - DH)], axis=0)
    wout = wall2[:, o0 + 192:o0 + 320].reshape(D, D)
    wuq = _head_pad(wall2[:, o0 + 320:o0 + 344].reshape(H, DH, QL), DH)
    wukv = wall2[:, o0 + 352:o0 + 368].reshape(HP, KVL)

    ckv, qp, u_raw, v_raw, kpe = _proj_fwd(x1, modtab[:, 3:5], norm2_w, wint, tm=tp, n_tiles=n_all_p, tpe=tpe_p)
    q = _q_prep_fwd(qp, q_a_norm_w, wuq, wq, cos, sin, cs, tm=tm, n_lat=n_lat, tpe=tpe)
    k, v = _kv_prep_fwd(ckv, kpe, kv_a_norm_w, wukv, wk, cos_k, sin_k, cs,
                        tm=tm, n_tiles=n_all, tpe=tpe, n_lat=n_lat)
    attn, lse = _attn_fwd(q, k, v, nb=nb, s=s, nc=nc, tq=tm, ck=2048)
    x2, mix = _mix_fwd(u_raw, v_raw, attn, x1, modtab[:nb, 5:6], wv, ws16, bias, wout, cs,
                       tm=tp, n_lat=n_lat_p, tpe=tpe_p)
    dy, a2, b2, o2, lsum = _ffn_fwd((x2,), modtab[:nb, 6:9], norm3_w, wall2, 0, tm=tmf, n_tiles=r_lat // tmf,
                                    tpe=s // tmf, n_lat=r_lat // tmf, name="ffn2_fwd",
                                    target=loss_target.reshape(r_lat, D))

    tr = 2 * tm if n_lat % 2 == 0 and n_all % 2 == 0 else tm
    dx2, da2, db2, g2, do2, h2, dmod678, dnorm3 = _ffn_bwd_dx(
        dy, (x2,), a2, b2, o2, modtab[:nb, 6:9], norm3_w, wall2, 0,
        tm=tm, n_tiles=n_lat, tpe=tpe, n_lat=n_lat, name="ffn2_bwd_dx")
    g_ffn2 = _ffn_bwd_dw(h2, do2, da2, db2, g2, tr=tr, name="ffn2_bwd_dw")

    dattn, du, dv, dgate5, dwout, dws, dbs, dwv = _mix_bwd(
        dx2, mix, u_raw, v_raw, attn, modtab[:nb, 5:6], wv, ws16, wst16, bias, wout, cs, tm=tp, n_lat=n_lat_p, tpe=tpe_p)
    tq = 2 * tm if s % (2 * tm) == 0 else tm
    dq, dk_l, dk_c, dv_l, dv_c, recv_ffn2 = _attn_bwd(q, k, v, attn, dattn, lse, g_ffn2,
                                                      nb=nb, s=s, nc=nc, tq=tq, ck=1024)
    dqp, dwuq, dqa, dwq = _q_prep_bwd(dq, qp, q_a_norm_w, wuq, wq, cos, sin, cs, tm=tm, n_lat=n_lat, tpe=tpe)
    dckv, dkpe, dwukv, dkva, dwk = _kv_prep_bwd((dk_l, dk_c), (dv_l, dv_c), ckv, kpe, kv_a_norm_w, wukv, wk,
                                                cos_k, sin_k, cs, tm=tm, n_tiles=n_all, tpe=tpe, n_lat=n_lat)
    dx1, dwin, dmod34, dnorm2 = _proj_bwd(dckv, dkpe, dqp, du, dv, dx2, x1, modtab[:, 3:5], norm2_w, wint,
                                          tm=tp, n_tiles=n_all_p, tpe=tpe_p, n_lat=n_lat_p)

    def blocks(a):
        return a.reshape(NDEV, a.shape[0] // NDEV, D)

    dwin_o = jnp.concatenate([dwin[0:128], dwin[KPE_LO:KPE_LO + DR], dwin[128:384], dwin[384:896], dwin[896:1408]],
                             axis=0)
    dwuq_o = dwuq.reshape(H, LANE, QL)[:, :DH]
    gmisc = jnp.concatenate([
        blocks(dwin_o).astype(BF16), jnp.zeros((NDEV, 12, D), BF16),
        blocks(dwout).astype(BF16),
        dwuq_o.reshape(NDEV, 24, D).astype(BF16), jnp.zeros((NDEV, 8, D), BF16),
        dwukv.reshape(NDEV, 16, D).astype(BF16)], axis=1)

    dx0, da1, db1, g1, do1, h1, dmod012, dnorm1 = _ffn_bwd_dx(
        dx1, xs, a1, b1, o1, modtab[:, 0:3], norm1_w, wall1, 0,
        tm=tm, n_tiles=n_all, tpe=tpe, n_lat=n_lat, name="ffn1_bwd_dx")
    grad_x = dx0.reshape(nb, s, D)
    g_w1, recv_misc = _ffn_bwd_dw_one(da1, h1, tr=tr, name="ffn1_bwd_dw1", part=gmisc)
    g_w3, recv_w1 = _ffn_bwd_dw_one(db1, h1, tr=tr, name="ffn1_bwd_dw3", part=g_w1)
    g_w2, recv_w3 = _ffn_bwd_dw_one(g1, do1, tr=tr, name="ffn1_bwd_dw2", part=g_w3)

    zrow = jnp.zeros((1, D), F32)
    g_lat = jnp.concatenate([dmod012[:nb, 0], dmod012[:nb, 1], dmod012[:nb, 2], dmod34[:nb, 0], dmod34[:nb, 1],
                             dgate5[:, 0], dmod678[:, 0], dmod678[:, 1], dmod678[:, 2]], axis=1)
    g_ctx = jnp.concatenate([dmod012[nb:, 0], dmod012[nb:, 1], dmod012[nb:, 2], dmod34[nb:, 0], dmod34[nb:, 1],
                             zrow, zrow, zrow, zrow], axis=1)
    g_loc = jnp.concatenate([g_lat, g_ctx, jnp.zeros((7 - nb, NMOD * D), F32)], axis=0)

    got_w2, g_all = _scatter_sibling([g_w2], "scatter_sibling_w2", gather=g_loc)
    g_all = g_all.reshape(NDEV * 8, NMOD * D)
    g_cols = lax.dynamic_slice_in_dim(g_all, me * ncol, ncol, axis=1)
    g_w_ada, pc_ctx, g_b_ada = _ada_bwd(a_raw, c_ctx.reshape(D, 1), g_all, g_cols, w_ada[0], nb)
    part_w2 = _add_sibling(g_w2, got_w2, 176, "add_sibling_w2")

    def prow(a):
        a = a.reshape(1, -1)
        return jnp.concatenate([a, jnp.zeros((1, D - a.shape[1]), F32)], axis=1)

    g_qn = dwq.reshape(H, LANE)[:, :DH].sum(0)
    g_kn = dwk.reshape(H, LANE)[:, :DH].sum(0)
    spack = jnp.concatenate([
        dnorm1, dnorm2, dnorm3, prow(dqa), prow(dkva), prow(g_qn), prow(g_kn), prow(dwv),
        prow(dbs[:, :G].T), prow(pc_ctx), prow(lsum[0:1]), jnp.zeros((5, D), F32), dws.reshape(CH, D)],
        axis=0)
    recv_w2, small_all = _scatter_chips([part_w2], "scatter_chips", gather=spack)
    ssum = _sum_slots(small_all, 144, "sum_small")
    loss = ssum[10, 0] * (0.5 / D)
    gsum2 = _sum_direct(g_ffn2, recv_ffn2, 176, "sum_grads_ffn2")
    msum = _sum_direct(gmisc, recv_misc, 368, "sum_grads_misc")

    transposed = ("ffn1_w1", "ffn1_w3", "ffn2_w1", "ffn2_w3", "w_in", "w_uq")
    g_big = {
        "ffn1_w1": _sum_direct(g_w1, recv_w1, 176, "sum_grads_w1"),
        "ffn1_w3": _sum_direct(g_w3, recv_w3, 176, "sum_grads_w3"),
        "ffn1_w2": _sum_chips(part_w2, recv_w2, 176, "sum_grads_w2"),
        "ffn2_w1": gsum2[0:fsh], "ffn2_w3": gsum2[fsh:2 * fsh], "ffn2_w2": gsum2[2 * fsh:3 * fsh],
        "w_in": msum[0:180], "w_out": msum[192:320],
        "w_uq": msum[320:344].reshape(DH, QL), "w_ukv": msum[352:368].reshape(DN + DV, KVL).T,
        "w_ada": g_w_ada,
    }

    big_in = {
        "w_ada": (w_ada, m_w_ada, v_w_ada), "ffn1_w1": (ffn1_w1, m_ffn1_w1, v_ffn1_w1),
        "ffn1_w3": (ffn1_w3, m_ffn1_w3, v_ffn1_w3), "ffn1_w2": (ffn1_w2, m_ffn1_w2, v_ffn1_w2),
        "w_in": (w_in, m_w_in, v_w_in), "w_uq": (w_uq, m_w_uq, v_w_uq), "w_ukv": (w_ukv, m_w_ukv, v_w_ukv),
        "w_out": (w_out, m_w_out, v_w_out), "ffn2_w1": (ffn2_w1, m_ffn2_w1, v_ffn2_w1),
        "ffn2_w3": (ffn2_w3, m_ffn2_w3, v_ffn2_w3), "ffn2_w2": (ffn2_w2, m_ffn2_w2, v_ffn2_w2),
    }
    res = {}
    for nm, (w, m, v_) in big_in.items():
        g = g_big[nm]
        if nm in transposed:
            d_, m_, v2_ = _adamw(w[0].T, g, m[0].T, v_[0].T, "adamw_" + nm)
            res[nm] = tuple(a.T[None] for a in (g, d_, m_, v2_))
        else:
            d_, m_, v2_ = _adamw(w[0], g, m[0], v_[0], "adamw_" + nm)
            res[nm] = tuple(a[None] for a in (g, d_, m_, v2_))

    small_in = [
        ("c_ctx", c_ctx, m_c_ctx, v_c_ctx, ssum[9:10], (1, D)),
        ("b_ada", b_ada, m_b_ada, v_b_ada, g_b_ada, (1, NMOD * D)),
        ("norm1_w", norm1_w, m_norm1_w, v_norm1_w, ssum[0:1], (1, D)),
        ("norm2_w", norm2_w, m_norm2_w, v_norm2_w, ssum[1:2], (1, D)),
        ("norm3_w", norm3_w, m_norm3_w, v_norm3_w, ssum[2:3], (1, D)),
        ("q_a_norm_w", q_a_norm_w, m_q_a_norm_w, v_q_a_norm_w, ssum[3:4, :QL], (1, QL)),
        ("kv_a_norm_w", kv_a_norm_w, m_kv_a_norm_w, v_kv_a_norm_w, ssum[4:5, :KVL], (1, KVL)),
        ("q_norm_w", q_norm_w, m_q_norm_w, v_q_norm_w, ssum[5:6, :DH], (1, DH)),
        ("k_norm_w", k_norm_w, m_k_norm_w, v_k_norm_w, ssum[6:7, :DH], (1, DH)),
        ("v_norm_w", v_norm_w, m_v_norm_w, v_v_norm_w, ssum[7:8, :G * GD], (G, GD)),
        ("b_s", b_s, m_b_s, v_b_s, ssum[8:9], (G, CH)),
        ("w_s", w_s, m_w_s, v_w_s, ssum[16:144], (G * CH, CH)),
    ]
    small_out = _adamw_small(
        [(w.reshape(sh), g.reshape(sh), m.reshape(sh), v_.reshape(sh)) for _, w, m, v_, g, sh in small_in])
    for (nm, w, *_), outs in zip(small_in, small_out):
        res[nm] = tuple(a.reshape(w.shape) for a in outs)

    order = ["c_ctx", "w_ada", "b_ada", "norm1_w", "ffn1_w1", "ffn1_w3", "ffn1_w2", "norm2_w", "w_in", "q_a_norm_w",
             "w_uq", "kv_a_norm_w", "w_ukv", "q_norm_w", "k_norm_w", "v_norm_w", "w_s", "b_s", "w_out", "norm3_w",
             "ffn2_w1", "ffn2_w3", "ffn2_w2"]
    return (loss, grad_x, *[res[n][0] for n in order], *[res[n][1] for n in order],
            *[res[n][2] for n in order], *[res[n][3] for n in order])
```

```python
import numpy as np
import jax
import jax.numpy as jnp
from jax import lax
from jax.experimental import pallas as pl
from jax.experimental.pallas import tpu as pltpu

F32 = jnp.float32
BF16 = jnp.bfloat16

D = 1024
FF = 2816
FC = 256
H = 8
DN, DR, DV = 64, 32, 64
DH = DN + DR
QL, KVL = 256, 128
G, GD, CH = 8, 64, 128
NMOD = 9
EPS = 1e-6
GRID_W = 64
ROPE_BASE = 10000.0
NDEV = 8
LANE = 128
HP = H * LANE
IN_COLS = 1440
WIN_ROWS = 1536
KPE_LO = 1408 + DN
MIB = 1 << 20

ADAM_LR, ADAM_B1, ADAM_B2, ADAM_EPS, ADAM_WD, ADAM_STEP = 0.001, 0.9, 0.999, 1e-08, 0.01, 10

MESH = pl.DeviceIdType.MESH
ANY = pl.BlockSpec(memory_space=pl.ANY)
VMEM = pl.BlockSpec(memory_space=pltpu.VMEM)


def _mm(a, b):
    return jnp.dot(a, b, preferred_element_type=F32)


def _mm_nt(a, b):
    return lax.dot_general(a, b, (((1,), (1,)), ((), ())), preferred_element_type=F32)


def _mm_tn(a, b):
    return lax.dot_general(a, b, (((0,), (0,)), ((), ())), preferred_element_type=F32)


def _dot_hl(x, m):
    hi = x.astype(BF16)
    lo = (x - hi.astype(F32)).astype(BF16)
    return _mm(hi, m) + _mm(lo, m)


def _sigmoid(a):
    return 1.0 / (1.0 + jnp.exp(-a))


_G0 = 0.7978845608028654
_G1 = 0.044715


def _gelu(x):
    return 0.5 * x * (1.0 + jnp.tanh(_G0 * (x + _G1 * (x * x * x))))


def _gelu_grad(x):
    th = jnp.tanh(_G0 * (x + _G1 * (x * x * x)))
    return 0.5 * (1.0 + th) + 0.5 * x * (1.0 - th * th) * (_G0 * (1.0 + 3.0 * _G1 * x * x))


def _rowsum(y):
    return jnp.sum(y, axis=0, keepdims=True)


def _rms(x):
    return lax.rsqrt(jnp.mean(x * x, axis=-1, keepdims=True) + EPS)


def _pcall(body, *, name, out_shape, in_specs, out_specs, grid=None, scratch=(), vmem_mb=32, aliases=None):
    kw = {}
    if grid is not None:
        kw["grid"] = grid
        sem = ("arbitrary",) * len(grid)
    else:
        sem = None
    if aliases:
        kw["input_output_aliases"] = aliases
    return pl.pallas_call(
        body, name=name, out_shape=out_shape, in_specs=in_specs, out_specs=out_specs,
        scratch_shapes=list(scratch),
        compiler_params=pltpu.CompilerParams(dimension_semantics=sem, vmem_limit_bytes=vmem_mb * MIB),
        **kw)


def _const(shape):
    nd = len(shape)
    return pl.BlockSpec(shape, lambda *_: (0,) * nd)


def _sds(shape, dt):
    return jax.ShapeDtypeStruct(shape, dt)


def _consts():
    seg_h = np.zeros((HP, LANE), np.float32)
    seg_h[np.arange(HP), np.arange(HP) // LANE] = 1.0
    seg_g = np.zeros((G * GD, LANE), np.float32)
    seg_g[np.arange(G * GD), np.arange(G * GD) // GD] = 1.0
    rot = np.zeros((LANE, LANE), np.float32)
    for base in (DN, DN + 16):
        for j in range(8):
            rot[base + j + 8, base + j] = -1.0
            rot[base + j, base + j + 8] = 1.0
    rot2 = np.zeros((2 * LANE, 2 * LANE), np.float32)
    rot2[:LANE, :LANE] = rot
    rot2[LANE:, LANE:] = rot
    twice = lambda m: np.concatenate([m, m], axis=0)
    c = dict(seg_h=seg_h, seg_ht=twice(seg_h.T), seg_g=seg_g, seg_gt=twice(seg_g.T), rot=rot2, rot_t=rot2.T)
    return {k: jnp.asarray(v, BF16) for k, v in c.items()}


_GATHER_SEMS = [pltpu.SemaphoreType.DMA((7,)), pltpu.SemaphoreType.DMA((7,)), pltpu.SemaphoreType.DMA(())]


def _gather_phases(x_ref, out_ref, send_sems, recv_sems, local_sem):
    mx, my, mc = lax.axis_index("x"), lax.axis_index("y"), lax.axis_index("c")
    me, sibling = (mx, my, mc), (mx, my, 1 - mc)
    chips = [(1 - mx, my), (mx, 1 - my), (1 - mx, 1 - my)]

    def blk(px, py, pc):
        return out_ref.at[4 * px + 2 * py + pc]

    def copy(k, block, to, src=None):
        return pltpu.make_async_remote_copy(
            src_ref=blk(*block) if src is None else src, dst_ref=blk(*block),
            send_sem=send_sems.at[k], recv_sem=recv_sems.at[k], device_id=to, device_id_type=MESH)

    mine = pltpu.make_async_copy(x_ref, blk(*me), local_sem)
    first = [copy(0, me, sibling, src=x_ref)]
    first += [copy(1 + j, me, (*chip, mc), src=x_ref) for j, chip in enumerate(chips)]
    passed = [copy(4 + j, (*chip, mc), sibling) for j, chip in enumerate(chips)]

    def start():
        mine.start()
        for cp in first:
            cp.start()

    def forward():
        for j, chip in enumerate(chips):
            copy(1 + j, (*chip, mc), me).wait_recv()
            passed[j].start()

    def finish():
        copy(0, sibling, me).wait_recv()
        for j, chip in enumerate(chips):
            copy(4 + j, (*chip, 1 - mc), me).wait_recv()
        for cp in first + passed:
            cp.wait_send()
        mine.wait()

    return start, forward, finish


def _chip_sends(p_ref, out_ref, send_sems, recv_sems):
    mx, my, mc = lax.axis_index("x"), lax.axis_index("y"), lax.axis_index("c")
    peers = [(1 - mx, my), (mx, 1 - my), (1 - mx, 1 - my)]
    return [pltpu.make_async_remote_copy(
        src_ref=p_ref.at[2 * px + py], dst_ref=out_ref.at[j], send_sem=send_sems.at[j], recv_sem=recv_sems.at[j],
        device_id=(px, py, mc), device_id_type=MESH) for j, (px, py) in enumerate(peers)]


def _with_gather(copies_of, n, shapes, sems, gather, name, args):
    ns = len(sems)

    def body(*refs):
        ng = 1 if gather is not None else 0
        ins, outs = refs[:n], refs[n + ng:2 * n + ng]
        copies = copies_of(ins, outs, refs[2 * n + 2 * ng:2 * n + 2 * ng + ns])
        if ng:
            start, forward, finish = _gather_phases(refs[n], refs[2 * n + 1], *refs[2 * n + 2 + ns:])
            start()
        for cp in copies:
            cp.start()
        if ng:
            forward()
        for cp in copies:
            cp.wait_recv()
        for cp in copies:
            cp.wait_send()
        if ng:
            finish()

    in_specs, out_shape, scratch = [ANY] * n, list(shapes), list(sems)
    if gather is not None:
        in_specs.append(ANY)
        args = list(args) + [gather]
        out_shape.append(_sds((NDEV,) + gather.shape, gather.dtype))
        scratch += _GATHER_SEMS
    return pl.pallas_call(body, name=name, out_shape=tuple(out_shape), in_specs=in_specs,
                          out_specs=(ANY,) * len(out_shape), scratch_shapes=scratch)(*args)


def _scatter_sibling(xs, name, gather=None):
    n = len(xs)

    def copies_of(x_refs, got_refs, sems):
        send_sems, recv_sems = sems
        mx, my, mc = lax.axis_index("x"), lax.axis_index("y"), lax.axis_index("c")
        return [pltpu.make_async_remote_copy(
            src_ref=x_refs[i].at[2 * j + 1 - mc], dst_ref=got_refs[i].at[j],
            send_sem=send_sems.at[4 * i + j], recv_sem=recv_sems.at[4 * i + j],
            device_id=(mx, my, 1 - mc), device_id_type=MESH) for i in range(n) for j in range(4)]

    shapes = tuple(_sds((4,) + x.shape[1:], x.dtype) for x in xs)
    return _with_gather(copies_of, n, shapes, [pltpu.SemaphoreType.DMA((4 * n,))] * 2, gather, name, xs)


def _scatter_chips(ps, name, gather=None):
    n = len(ps)

    def copies_of(p_refs, out_refs, sems):
        sends = []
        for i in range(n):
            sends += _chip_sends(p_refs[i], out_refs[i], sems[2 * i], sems[2 * i + 1])
        return sends

    shapes = tuple(_sds((3,) + p.shape[1:], p.dtype) for p in ps)
    return _with_gather(copies_of, n, shapes, [pltpu.SemaphoreType.DMA((3,))] * (2 * n), gather, name, ps)


def _add_sibling(x, got, tr, name):
    _, r, c = x.shape

    def body(x_ref, g_ref, o_ref):
        mc = lax.axis_index("c")
        for j in range(4):
            mine = jnp.where(mc == 0, x_ref[2 * j].astype(F32), x_ref[2 * j + 1].astype(F32))
            o_ref[j] = (mine + g_ref[j].astype(F32)).astype(o_ref.dtype)

    return _pcall(body, name=name, grid=(r // tr,), out_shape=_sds(got.shape, got.dtype),
                  in_specs=[pl.BlockSpec((NDEV, tr, c), lambda t: (0, t, 0)), pl.BlockSpec((4, tr, c), lambda t: (0, t, 0))],
                  out_specs=pl.BlockSpec((4, tr, c), lambda t: (0, t, 0)))(x, got)


def _sum_chips(part, recv, tr, name):
    _, r, c = part.shape

    def body(p_ref, r_ref, o_ref):
        slot = 2 * lax.axis_index("x") + lax.axis_index("y")
        acc = p_ref[0].astype(F32)
        for j in range(1, 4):
            acc = jnp.where(slot == j, p_ref[j].astype(F32), acc)
        for j in range(3):
            acc = acc + r_ref[j].astype(F32)
        o_ref[...] = acc

    return _pcall(body, name=name, grid=(r // tr,), out_shape=_sds((r, c), F32),
                  in_specs=[pl.BlockSpec((4, tr, c), lambda t: (0, t, 0)), pl.BlockSpec((3, tr, c), lambda t: (0, t, 0))],
                  out_specs=pl.BlockSpec((tr, c), lambda t: (t, 0)))(part, recv)


def _sum_slots(x, tr, name):
    n, r, c = x.shape

    def body(x_ref, o_ref):
        acc = x_ref[0].astype(F32)
        for s in range(1, n):
            acc = acc + x_ref[s].astype(F32)
        o_ref[...] = acc

    return _pcall(body, name=name, grid=(r // tr,), out_shape=_sds((r, c), F32),
                  in_specs=[pl.BlockSpec((n, tr, c), lambda t: (0, t, 0))],
                  out_specs=pl.BlockSpec((tr, c), lambda t: (t, 0)))(x)


def _ada_front(a_loc, w_loc, b_loc, wpack):
    ncol = w_loc.shape[1]
    nrow = NDEV * a_loc.shape[0]

    def body(a_ref, w_ref, b_ref, wp_ref, araw_ref, mloc_ref, mall_ref, wall_ref,
             a_vm, w_vm, m_vm, lsem, *sems):
        a_start, a_forward, a_finish = _gather_phases(a_ref, araw_ref, *sems[0:3])
        m_start, m_forward, m_finish = _gather_phases(mloc_ref, mall_ref, *sems[3:6])
        w_start, w_forward, w_finish = _gather_phases(wp_ref, wall_ref, *sems[6:9])
        w_in = pltpu.make_async_copy(w_ref, w_vm, lsem.at[0])
        w_in.start()
        a_start()
        w_start()
        a_forward()
        a_finish()
        a_in = pltpu.make_async_copy(araw_ref, a_vm, lsem.at[1])
        a_in.start()
        a_in.wait()
        w_in.wait()
        a = a_vm[...].reshape(nrow, D)
        act = (a * _sigmoid(a)).astype(BF16)
        m_vm[...] = _mm(act, w_vm[...].astype(BF16)) + b_ref[...]
        m_out = pltpu.make_async_copy(m_vm, mloc_ref, lsem.at[2])
        m_out.start()
        m_out.wait()
        m_start()
        m_forward()
        m_finish()
        w_forward()
        w_finish()

    return pl.pallas_call(
        body, name="ada_front",
        out_shape=(_sds((NDEV,) + a_loc.shape, F32), _sds((nrow, ncol), F32), _sds((NDEV, nrow, ncol), F32),
                   _sds((NDEV,) + wpack.shape, wpack.dtype)),
        in_specs=[ANY, ANY, VMEM, ANY], out_specs=(ANY, ANY, ANY, ANY),
        scratch_shapes=[pltpu.VMEM((NDEV,) + a_loc.shape, F32), pltpu.VMEM(w_loc.shape, F32),
                        pltpu.VMEM((nrow, ncol), F32), pltpu.SemaphoreType.DMA((3,))] + _GATHER_SEMS * 3,
        compiler_params=pltpu.CompilerParams(vmem_limit_bytes=32 * MIB),
    )(a_loc, w_loc, b_loc, wpack)


def _ada_bwd(a_raw, cctx_col, g_all, g_cols, w_loc, nb):
    nrow = a_raw.shape[0]
    ncol = w_loc.shape[1]

    def body(a_ref, cc_ref, gall_ref, g_ref, w_ref, dw_ref, pc_ref, gb_ref):
        a = a_ref[...]
        rowid = lax.broadcasted_iota(jnp.int32, (nrow, 1), 0) % 8
        act = jnp.where(rowid < nb, a * _sigmoid(a), 0.0).astype(BF16)
        g = g_ref[...]
        gc = _rowsum(jnp.where(rowid == nb, g, 0.0))
        cc = cc_ref[...]
        dw_ref[...] = _mm_tn(act, g.astype(BF16)) + (cc * _sigmoid(cc)) * gc
        pc_ref[...] = jnp.sum(w_ref[...] * gc, axis=1, keepdims=True)
        gb_ref[...] = _rowsum(gall_ref[...])

    return _pcall(body, name="ada_bwd",
                  out_shape=(_sds((D, ncol), F32), _sds((D, 1), F32), _sds((1, g_all.shape[1]), F32)),
                  in_specs=[VMEM] * 5, out_specs=(VMEM,) * 3, vmem_mb=48)(a_raw, cctx_col, g_all, g_cols, w_loc)


def _mod_spec(k, tpe, nrows):
    return pl.BlockSpec((1, k, D), lambda t: (jnp.minimum(t // tpe, nrows - 1), 0, 0))


def _load_ffn_weights(wall_ref, first, bufs, sems):
    fsh = FF // NDEV
    cps = []
    for j, buf in enumerate(bufs):
        for d in range(NDEV):
            cps.append(pltpu.make_async_copy(wall_ref.at[d, pl.ds((first + j) * fsh, fsh)],
                                             buf.at[pl.ds(d * fsh, fsh)], sems.at[j * NDEV + d]))
    for cp in cps:
        cp.start()
    for cp in cps:
        cp.wait()


def _token_specs(xs, tm, n_lat):
    specs = [pl.BlockSpec((tm, D), lambda t: (jnp.minimum(t, n_lat - 1), 0))]
    if len(xs) == 2:
        specs.append(pl.BlockSpec((tm, D), lambda t: (jnp.maximum(t - n_lat, 0), 0)))
    return specs


def _ffn_fwd(xs, mod3, norm_w, wall, first, *, tm, n_tiles, tpe, n_lat, name, target=None, gather=None):
    nrows = mod3.shape[0]
    r = n_tiles * tm
    nx = len(xs)
    with_loss = target is not None
    with_gather = gather is not None
    fwd_step = max(2 * n_tiles // 3, 1)

    def body(*refs):
        x_refs = refs[:nx]
        pos = nx
        if with_loss:
            tgt_ref = refs[pos]
            pos += 1
        mod_ref, nw_ref, wall_ref = refs[pos:pos + 3]
        pos += 3
        if with_gather:
            gin_ref = refs[pos]
            pos += 1
        xo_ref, a_ref, b_ref, o_ref = refs[pos:pos + 4]
        pos += 4
        if with_loss:
            ls_ref = refs[pos]
            pos += 1
        if with_gather:
            gout_ref = refs[pos]
            pos += 1
        w1_ref, w3_ref, w2_ref, wsem, acc_ref = refs[pos:pos + 5]
        t = pl.program_id(0)
        if with_gather:
            g_start, g_forward, g_finish = _gather_phases(gin_ref, gout_ref, *refs[pos + 5:])

        @pl.when(t == 0)
        def _():
            if with_gather:
                g_start()
            _load_ffn_weights(wall_ref, first, (w1_ref, w3_ref, w2_ref), wsem)
            if with_loss:
                ls_ref[...] = jnp.zeros_like(ls_ref)

        if with_gather:
            @pl.when(t == fwd_step)
            def _():
                g_forward()

            @pl.when(t == n_tiles - 1)
            def _():
                g_finish()

        x = x_refs[0][...]
        if nx == 2:
            x = jnp.where(t < n_lat, x, x_refs[1][...])
        n = x * _rms(x) * nw_ref[...]
        shift, scale, gate = mod_ref[0, 0:1, :], mod_ref[0, 1:2, :], mod_ref[0, 2:3, :]
        h = (n * (1.0 + scale) + shift).astype(BF16)
        nch = FF // FC
        o = None
        for lo_c, hi_c in ((0, 4), (4, 8), (8, nch)):
            for j in range(lo_c, hi_c):
                sl = slice(j * FC, (j + 1) * FC)
                a = _mm_nt(h, w1_ref[sl, :])
                b = _mm_nt(h, w3_ref[sl, :])
                a_ref[:, sl] = a.astype(BF16)
                b_ref[:, sl] = b.astype(BF16)
                acc_ref[:, sl] = (a * _sigmoid(a) * b).astype(BF16)
            gs = slice(lo_c * FC, hi_c * FC)
            part = _mm(acc_ref[:, gs], w2_ref[gs, :])
            o = part if o is None else o + part
        o_ref[...] = o.astype(BF16)
        out = x + (0.5 * gate) * o
        if with_loss:
            d = out - tgt_ref[...]
            xo_ref[...] = d * (1.0 / D)
            ls_ref[...] += jnp.sum(d * d)
        else:
            xo_ref[...] = out

    row = lambda cols: pl.BlockSpec((tm, cols), lambda t: (t, 0))
    in_specs = _token_specs(xs, tm, n_lat) + ([row(D)] if with_loss else []) + [
        _mod_spec(3, tpe, nrows), _const((1, D)), ANY]
    out_shape = [_sds((r, D), F32), _sds((r, FF), BF16), _sds((r, FF), BF16), _sds((r, D), BF16)]
    out_specs = [row(D), row(FF), row(FF), row(D)]
    scratch = [pltpu.VMEM((FF, D), BF16)] * 3 + [pltpu.SemaphoreType.DMA((3 * NDEV,)), pltpu.VMEM((tm, FF), BF16)]
    if with_loss:
        out_shape.append(_sds((8, LANE), F32))
        out_specs.append(_const((8, LANE)))
    args = list(xs) + ([target] if with_loss else []) + [mod3, norm_w, wall]
    if with_gather:
        assert n_tiles >= 2
        in_specs.append(ANY)
        args.append(gather)
        out_shape.append(_sds((NDEV,) + gather.shape, gather.dtype))
        out_specs.append(ANY)
        scratch += _GATHER_SEMS
    return _pcall(
        body, name=name, grid=(n_tiles,), out_shape=tuple(out_shape), in_specs=in_specs, out_specs=tuple(out_specs),
        scratch=scratch, vmem_mb=56)(*args)


def _ffn_bwd_dx(dout, xs, a, b, o, mod3, norm_w, wall, first, *, tm, n_tiles, tpe, n_lat, name):
    nrows = mod3.shape[0]
    r = n_tiles * tm
    nx = len(xs)

    def body(*refs):
        dout_ref = refs[0]
        x_refs = refs[1:1 + nx]
        (a_ref, b_ref, o_ref, mod_ref, nw_ref, wall_ref,
         dx_ref, da_ref, db_ref, g_ref, do_ref, h_ref, dmod_ref, dnw_ref,
         w1_ref, w3_ref, w2_ref, wsem) = refs[1 + nx:]
        t = pl.program_id(0)

        @pl.when(t == 0)
        def _():
            _load_ffn_weights(wall_ref, first, (w1_ref, w3_ref, w2_ref), wsem)
            dnw_ref[...] = jnp.zeros_like(dnw_ref)

        @pl.when(jnp.where(t < n_lat, t % tpe == 0, t == n_lat))
        def _():
            dmod_ref[...] = jnp.zeros_like(dmod_ref)

        x = x_refs[0][...]
        if nx == 2:
            x = jnp.where(t < n_lat, x, x_refs[1][...])
        dout = dout_ref[...]
        shift, scale, gate = mod_ref[0, 0:1, :], mod_ref[0, 1:2, :], mod_ref[0, 2:3, :]
        d_o = ((0.5 * gate) * dout).astype(BF16)
        do_ref[...] = d_o
        nch = FF // FC
        groups = ((0, 4), (4, 8), (8, nch))
        dh = None
        for lo_c, hi_c in groups:
            for j in range(lo_c, hi_c):
                sl = slice(j * FC, (j + 1) * FC)
                av = a_ref[:, sl].astype(F32)
                bv = b_ref[:, sl].astype(F32)
                dg = _mm_nt(d_o, w2_ref[sl, :])
                sig = _sigmoid(av)
                sa = av * sig
                g_ref[:, sl] = (sa * bv).astype(BF16)
                da_ref[:, sl] = (dg * bv * (sig * (1.0 + av * (1.0 - sig)))).astype(BF16)
                db_ref[:, sl] = (dg * sa).astype(BF16)
            gs = slice(lo_c * FC, hi_c * FC)
            part = _mm(da_ref[:, gs], w1_ref[gs, :]) + _mm(db_ref[:, gs], w3_ref[gs, :])
            dh = part if dh is None else dh + part
        rr = _rms(x)
        xh = x * rr
        nw = nw_ref[...]
        n = xh * nw
        h_ref[...] = (n * (1.0 + scale) + shift).astype(BF16)
        dgate = _rowsum(0.5 * o_ref[...].astype(F32) * dout)
        dn = dh * (1.0 + scale)
        dxh = dn * nw
        dmod_ref[0, 0:1, :] += _rowsum(dh)
        dmod_ref[0, 1:2, :] += _rowsum(dh * n)
        dmod_ref[0, 2:3, :] += dgate
        dnw_ref[...] += _rowsum(dn * xh)
        dx = dout + rr * (dxh - xh * jnp.mean(dxh * xh, axis=-1, keepdims=True))
        if n_tiles == n_lat:
            dx_ref[...] = dx
        else:
            @pl.when(t < n_lat)
            def _():
                dx_ref[...] = dx

    row = lambda cols: pl.BlockSpec((tm, cols), lambda t: (t, 0))
    lat = pl.BlockSpec((tm, D), lambda t: (jnp.minimum(t, n_lat - 1), 0))
    out_shape = [_sds((n_lat * tm, D), F32), _sds((r, FF), BF16), _sds((r, FF), BF16), _sds((r, FF), BF16),
                 _sds((r, D), BF16), _sds((r, D), BF16), _sds((nrows, 3, D), F32), _sds((1, D), F32)]
    in_specs = [row(D)] + _token_specs(xs, tm, n_lat) + [row(FF), row(FF), row(D), _mod_spec(3, tpe, nrows),
                                                          _const((1, D)), ANY]
    out_specs = [lat, row(FF), row(FF), row(FF), row(D), row(D), _mod_spec(3, tpe, nrows), _const((1, D))]
    scratch = [pltpu.VMEM((FF, D), BF16)] * 3 + [pltpu.SemaphoreType.DMA((3 * NDEV,))]
    args = [dout, *xs, a, b, o, mod3, norm_w, wall]
    return _pcall(body, name=name, grid=(n_tiles,), out_shape=tuple(out_shape), in_specs=in_specs,
                  out_specs=tuple(out_specs), scratch=scratch, vmem_mb=60)(*args)


def _ffn_bwd_dw(h, d_o, da, db, g, *, tr, name):
    r = h.shape[0]
    fh = FF // 2
    fsh = FF // NDEV
    nk = r // tr

    def body(h_ref, do_ref, da_ref, db_ref, g_ref, out_ref, acc1, acc3, acc2):
        k = pl.program_id(1)

        @pl.when(k == 0)
        def _():
            acc1[...] = jnp.zeros_like(acc1)
            acc3[...] = jnp.zeros_like(acc3)
            acc2[...] = jnp.zeros_like(acc2)

        hv = h_ref[...]
        acc1[...] += _mm_tn(da_ref[...], hv)
        acc3[...] += _mm_tn(db_ref[...], hv)
        acc2[...] += _mm_tn(g_ref[...], do_ref[...])

        @pl.when(k == nk - 1)
        def _():
            for i, acc in enumerate((acc1, acc3, acc2)):
                out_ref[:, i * fsh:(i + 1) * fsh, :] = acc[...].reshape(NDEV // 2, fsh, D).astype(BF16)

    rowd = pl.BlockSpec((tr, D), lambda f, k: (k, 0))
    rowf = pl.BlockSpec((tr, fh), lambda f, k: (k, f))
    return _pcall(
        body, name=name, grid=(2, nk), out_shape=_sds((NDEV, 3 * fsh, D), BF16),
        in_specs=[rowd, rowd, rowf, rowf, rowf],
        out_specs=pl.BlockSpec((NDEV // 2, 3 * fsh, D), lambda f, k: (f, 0, 0)),
        scratch=[pltpu.VMEM((fh, D), F32)] * 3, vmem_mb=56)(h, d_o, da, db, g)


def _direct_sends(x_ref, out_ref, send_sems, recv_sems):
    mx, my, mc = lax.axis_index("x"), lax.axis_index("y"), lax.axis_index("c")
    sends = []
    for k in range(1, NDEV):
        px = 1 - mx if (k & 4) else mx
        py = 1 - my if (k & 2) else my
        pc = 1 - mc if (k & 1) else mc
        sends.append(pltpu.make_async_remote_copy(
            src_ref=x_ref.at[4 * px + 2 * py + pc], dst_ref=out_ref.at[k - 1],
            send_sem=send_sems.at[k - 1], recv_sem=recv_sems.at[k - 1], device_id=(px, py, pc), device_id_type=MESH))
    return sends


def _sum_direct(x, recv, tr, name):
    _, r, c = x.shape

    def body(x_ref, r_ref, o_ref):
        me = 4 * lax.axis_index("x") + 2 * lax.axis_index("y") + lax.axis_index("c")
        acc = x_ref[0].astype(F32)
        for j in range(1, NDEV):
            acc = jnp.where(me == j, x_ref[j].astype(F32), acc)
        for j in range(NDEV - 1):
            acc = acc + r_ref[j].astype(F32)
        o_ref[...] = acc

    return _pcall(body, name=name, grid=(r // tr,), out_shape=_sds((r, c), F32),
                  in_specs=[pl.BlockSpec((NDEV, tr, c), lambda t: (0, t, 0)),
                            pl.BlockSpec((NDEV - 1, tr, c), lambda t: (0, t, 0))],
                  out_specs=pl.BlockSpec((tr, c), lambda t: (t, 0)))(x, recv)


def _exchange_behind(x_ref, recv_ref, send_sems, recv_sems, first, last):
    sends = _direct_sends(x_ref, recv_ref, send_sems, recv_sems)

    @pl.when(first)
    def _():
        for cp in sends:
            cp.start()

    @pl.when(last)
    def _():
        for cp in sends:
            cp.wait_recv()
        for cp in sends:
            cp.wait_send()


def _ffn_bwd_dw_one(lhs, rhs, *, tr, name, part=None):
    r = lhs.shape[0]
    fsh = FF // NDEV
    nk = r // tr
    fused = part is not None
    nslot = NDEV - 1

    def body(*refs):
        if fused:
            lhs_ref, rhs_ref, part_ref, out_ref, recv_ref, acc, send_sems, recv_sems = refs
        else:
            lhs_ref, rhs_ref, out_ref, acc = refs
        k = pl.program_id(0)
        if fused:
            _exchange_behind(part_ref, recv_ref, send_sems, recv_sems, k == 0, k == nk - 1)

        @pl.when(k == 0)
        def _():
            acc[...] = jnp.zeros_like(acc)

        acc[...] += _mm_tn(lhs_ref[...], rhs_ref[...])

        @pl.when(k == nk - 1)
        def _():
            out_ref[...] = acc[...].reshape(NDEV, fsh, D).astype(BF16)

    in_specs = [pl.BlockSpec((tr, FF), lambda k: (k, 0)), pl.BlockSpec((tr, D), lambda k: (k, 0))]
    out_shape = [_sds((NDEV, fsh, D), BF16)]
    out_specs = [_const((NDEV, fsh, D))]
    scratch = [pltpu.VMEM((FF, D), F32)]
    args = [lhs, rhs]
    if fused:
        in_specs.append(ANY)
        args.append(part)
        out_shape.append(_sds((nslot,) + part.shape[1:], part.dtype))
        out_specs.append(ANY)
        scratch += [pltpu.SemaphoreType.DMA((nslot,))] * 2
    res = _pcall(body, name=name, grid=(nk,), out_shape=tuple(out_shape), in_specs=in_specs,
                 out_specs=tuple(out_specs), scratch=scratch, vmem_mb=48)(*args)
    return res if fused else res[0]


_PIECES =((0, 128), (128, 384), (384, 896), (896, 1408), (1408, 1536))


def _proj_fwd(x1, mod2, norm_w, wint, *, tm, n_tiles, tpe, name="proj_fwd"):
    nrows = mod2.shape[0]
    r = n_tiles * tm

    def body(x_ref, mod_ref, nw_ref, w_ref, ckv_ref, q_ref, u_ref, v_ref, kpe_ref):
        x = x_ref[...]
        n = x * _rms(x) * nw_ref[...]
        h = (n * (1.0 + mod_ref[0, 1:2, :]) + mod_ref[0, 0:1, :]).astype(BF16)
        proj = _mm_nt(h, w_ref[...])
        for (lo, hi), ref in zip(_PIECES, (ckv_ref, q_ref, u_ref, v_ref, kpe_ref)):
            ref[...] = proj[:, lo:hi]

    row = lambda cols: pl.BlockSpec((tm, cols), lambda t: (t, 0))
    widths = [hi - lo for lo, hi in _PIECES]
    return _pcall(
        body, name=name, grid=(n_tiles,),
        out_shape=tuple(_sds((r, w), F32) for w in widths),
        in_specs=[row(D), _mod_spec(2, tpe, nrows), _const((1, D)), _const((WIN_ROWS, D))],
        out_specs=tuple(row(w) for w in widths), vmem_mb=40)(x1, mod2, norm_w, wint)


def _proj_bwd(dckv, dkpe, dq, du, dv, dx2, x1, mod2, norm_w, wint, *, tm, n_tiles, tpe, n_lat, name="proj_bwd"):
    nrows = mod2.shape[0]
    r = n_tiles * tm

    def body(dckv_ref, dkpe_ref, dq_ref, du_ref, dv_ref, dx2_ref, x_ref, mod_ref, nw_ref, w_ref,
             dx_ref, dw_ref, dmod_ref, dnw_ref):
        t = pl.program_id(0)
        is_lat = t < n_lat

        @pl.when(t == 0)
        def _():
            dw_ref[...] = jnp.zeros_like(dw_ref)
            dnw_ref[...] = jnp.zeros_like(dnw_ref)

        @pl.when(jnp.where(is_lat, t % tpe == 0, t == n_lat))
        def _():
            dmod_ref[...] = jnp.zeros_like(dmod_ref)

        x = x_ref[...]
        rr = _rms(x)
        xh = x * rr
        nw = nw_ref[...]
        n = xh * nw
        scale = mod_ref[0, 1:2, :]
        h = (n * (1.0 + scale) + mod_ref[0, 0:1, :]).astype(BF16)
        zero = jnp.zeros((), BF16)
        pieces = (dckv_ref[...], jnp.where(is_lat, dq_ref[...], zero), jnp.where(is_lat, du_ref[...], zero),
                  jnp.where(is_lat, dv_ref[...], zero), dkpe_ref[...])
        dproj = jnp.concatenate(pieces, axis=-1)
        dh = _mm(dproj, w_ref[...])
        dn = dh * (1.0 + scale)
        dxh = dn * nw
        dx = rr * (dxh - xh * jnp.mean(dxh * xh, axis=-1, keepdims=True))
        dx_ref[...] = dx + jnp.where(is_lat, dx2_ref[...], 0.0)
        dmod_ref[0, 0:1, :] += _rowsum(dh)
        dmod_ref[0, 1:2, :] += _rowsum(dh * n)
        dnw_ref[...] += _rowsum(dn * xh)
        dw_ref[...] += _mm_tn(dproj, h)

    row = lambda cols: pl.BlockSpec((tm, cols), lambda t: (t, 0))
    lat = lambda cols: pl.BlockSpec((tm, cols), lambda t: (jnp.minimum(t, n_lat - 1), 0))
    return _pcall(
        body, name=name, grid=(n_tiles,),
        out_shape=(_sds((r, D), F32), _sds((WIN_ROWS, D), F32), _sds((nrows, 2, D), F32), _sds((1, D), F32)),
        in_specs=[row(128), row(128), lat(256), lat(512), lat(512), lat(D), row(D), _mod_spec(2, tpe, nrows),
                  _const((1, D)), _const((WIN_ROWS, D))],
        out_specs=(row(D), _const((WIN_ROWS, D)), _mod_spec(2, tpe, nrows), _const((1, D))),
        vmem_mb=48)(dckv, dkpe, dq, du, dv, dx2, x1, mod2, norm_w, wint)


def _seg_sum(x, seg):
    return _mm(x.astype(BF16), seg)


def _seg_bcast(v, segt2):
    hi = v.astype(BF16)
    lo = (v - hi.astype(F32)).astype(BF16)
    return _mm(jnp.concatenate([hi, lo], axis=-1), segt2)


def _rope_pairs(t, cos, sin, rot2):
    cos2, sin2 = jnp.concatenate([cos, cos], axis=-1), jnp.concatenate([sin, sin], axis=-1)
    out = []
    for j in range(H // 2):
        tj = t[:, 2 * j * LANE:2 * (j + 1) * LANE]
        out.append(tj * cos2 + _dot_hl(tj, rot2) * sin2)
    return jnp.concatenate(out, axis=-1)


def _head_norm_rope(x, w_pad, cos, sin, seg, segt2, rot2, rope=True):
    rh = lax.rsqrt(_seg_sum(x * x, seg) * (1.0 / DH) + EPS)
    rb = _seg_bcast(rh, segt2)
    y = x * rb
    out = _rope_pairs(y * w_pad, cos, sin, rot2) if rope else None
    return out, y, rb


def _head_norm_rope_bwd(dout, y, rb, w_pad, cos, sin, seg, segt2, rot2_t):
    cos2, sin2 = jnp.concatenate([cos, cos], axis=-1), jnp.concatenate([sin, sin], axis=-1)
    dt = []
    for j in range(H // 2):
        dj = dout[:, 2 * j * LANE:2 * (j + 1) * LANE]
        dt.append(dj * cos2 + _dot_hl(dj * sin2, rot2_t))
    dt = jnp.concatenate(dt, axis=-1)
    dw = _rowsum(dt * y)
    dy = dt * w_pad
    mean_h = _seg_sum(dy * y, seg) * (1.0 / DH)
    return rb * (dy - y * _seg_bcast(mean_h, segt2)), dw


def _q_prep_fwd(qp, qa_w, wuq, wq, cos, sin, cs, *, tm, n_lat, tpe):
    def body(qp_ref, qa_ref, wuq_ref, wq_ref, cos_ref, sin_ref, seg, segt, rot, q_ref):
        x = qp_ref[...]
        cq = (x * _rms(x) * qa_ref[...]).astype(BF16)
        q, _, _ = _head_norm_rope(_mm_nt(cq, wuq_ref[...]), wq_ref[...], cos_ref[...], sin_ref[...],
                                  seg[...], segt[...], rot[...])
        q_ref[...] = q.astype(BF16)

    row = lambda cols: pl.BlockSpec((tm, cols), lambda t: (t, 0))
    tab = pl.BlockSpec((tm, LANE), lambda t: (t % tpe, 0))
    return _pcall(
        body, name="q_prep_fwd", grid=(n_lat,), out_shape=_sds((n_lat * tm, HP), BF16),
        in_specs=[row(QL), _const((1, QL)), _const((HP, QL)), _const((1, HP)), tab, tab,
                  _const((HP, LANE)), _const((2 * LANE, HP)), _const((2 * LANE, 2 * LANE))],
        out_specs=row(HP))(qp, qa_w, wuq, wq, cos, sin, cs["seg_h"], cs["seg_ht"], cs["rot"])


def _q_prep_bwd(dq, qp, qa_w, wuq, wq, cos, sin, cs, *, tm, n_lat, tpe):
    def body(dq_ref, qp_ref, qa_ref, wuq_ref, wq_ref, cos_ref, sin_ref, seg, segt, rot, rot_t,
             dqp_ref, dwuq_ref, dqa_ref, dwq_ref):
        @pl.when(pl.program_id(0) == 0)
        def _():
            dwuq_ref[...] = jnp.zeros_like(dwuq_ref)
            dqa_ref[...] = jnp.zeros_like(dqa_ref)
            dwq_ref[...] = jnp.zeros_like(dwq_ref)

        x = qp_ref[...]
        ra = _rms(x)
        xh = x * ra
        qa = qa_ref[...]
        cq = (xh * qa).astype(BF16)
        wuq_v = wuq_ref[...]
        wq_v, cos_v, sin_v = wq_ref[...], cos_ref[...], sin_ref[...]
        _, y, rb = _head_norm_rope(_mm_nt(cq, wuq_v), wq_v, cos_v, sin_v, seg[...], segt[...], rot[...], rope=False)
        dqraw, dwq = _head_norm_rope_bwd(dq_ref[...].astype(F32), y, rb, wq_v, cos_v, sin_v, seg[...], segt[...], rot_t[...])
        dqraw = dqraw.astype(BF16)
        dcq = _mm(dqraw, wuq_v)
        dxh = dcq * qa
        dqp_ref[...] = (ra * (dxh - xh * jnp.mean(dxh * xh, axis=-1, keepdims=True))).astype(BF16)
        dwuq_ref[...] += _mm_tn(dqraw, cq)
        dqa_ref[...] += _rowsum(dcq * xh)
        dwq_ref[...] += dwq

    row = lambda cols: pl.BlockSpec((tm, cols), lambda t: (t, 0))
    tab = pl.BlockSpec((tm, LANE), lambda t: (t % tpe, 0))
    return _pcall(
        body, name="q_prep_bwd", grid=(n_lat,),
        out_shape=(_sds((n_lat * tm, QL), BF16), _sds((HP, QL), F32), _sds((1, QL), F32), _sds((1, HP), F32)),
        in_specs=[row(HP), row(QL), _const((1, QL)), _const((HP, QL)), _const((1, HP)), tab, tab,
                  _const((HP, LANE)), _const((2 * LANE, HP)), _const((2 * LANE, 2 * LANE)), _const((2 * LANE, 2 * LANE))],
        out_specs=(row(QL), _const((HP, QL)), _const((1, QL)), _const((1, HP))), vmem_mb=40)(
            dq, qp, qa_w, wuq, wq, cos, sin, cs["seg_h"], cs["seg_ht"], cs["rot"], cs["rot_t"])


def _kv_tab_spec(tm, tpe, n_lat):
    return pl.BlockSpec((tm, LANE), lambda t: (jnp.where(t < n_lat, t % tpe, tpe), 0))


def _split_kv(kv, kpe):
    low = lax.broadcasted_iota(jnp.int32, (kv.shape[0], LANE), 1) < DN
    kx, v = [], []
    for h in range(H):
        blk = kv[:, h * LANE:(h + 1) * LANE]
        kx.append(jnp.where(low, blk, kpe))
        v.append(jnp.where(low, pltpu.roll(blk, DN, 1), 0.0))
    return jnp.concatenate(kx, axis=-1), jnp.concatenate(v, axis=-1)


def _kv_prep_fwd(ckv, kpe, kva_w, wukv, wk, cosk, sink, cs, *, tm, n_tiles, tpe, n_lat):
    def body(ckv_ref, kpe_ref, kva_ref, wukv_ref, wk_ref, cos_ref, sin_ref, seg, segt, rot, k_ref, v_ref):
        x = ckv_ref[...]
        ckvn = (x * _rms(x) * kva_ref[...]).astype(BF16)
        kx, v = _split_kv(_mm_nt(ckvn, wukv_ref[...]), kpe_ref[...])
        k, _, _ = _head_norm_rope(kx, wk_ref[...], cos_ref[...], sin_ref[...], seg[...], segt[...], rot[...])
        k_ref[...] = k.astype(BF16)
        v_ref[...] = v.astype(BF16)

    row = lambda cols: pl.BlockSpec((tm, cols), lambda t: (t, 0))
    tab = _kv_tab_spec(tm, tpe, n_lat)
    r = n_tiles * tm
    return _pcall(
        body, name="kv_prep_fwd", grid=(n_tiles,), out_shape=(_sds((r, HP), BF16), _sds((r, HP), BF16)),
        in_specs=[row(KVL), row(LANE), _const((1, KVL)), _const((HP, KVL)), _const((1, HP)), tab, tab,
                  _const((HP, LANE)), _const((2 * LANE, HP)), _const((2 * LANE, 2 * LANE))],
        out_specs=(row(HP), row(HP)), vmem_mb=40)(
            ckv, kpe, kva_w, wukv, wk, cosk, sink, cs["seg_h"], cs["seg_ht"], cs["rot"])


def _kv_prep_bwd(dks, dvs, ckv, kpe, kva_w, wukv, wk, cosk, sink, cs, *, tm, n_tiles, tpe, n_lat):
    def body(dkl_ref, dkc_ref, dvl_ref, dvc_ref, ckv_ref, kpe_ref, kva_ref, wukv_ref, wk_ref, cos_ref, sin_ref,
             seg, segt, rot, rot_t, dckv_ref, dkpe_ref, dwukv_ref, dkva_ref, dwk_ref):
        t = pl.program_id(0)
        is_lat = t < n_lat

        @pl.when(t == 0)
        def _():
            dwukv_ref[...] = jnp.zeros_like(dwukv_ref)
            dkva_ref[...] = jnp.zeros_like(dkva_ref)
            dwk_ref[...] = jnp.zeros_like(dwk_ref)

        dk = jnp.where(is_lat, dkl_ref[...], dkc_ref[...]).astype(F32)
        dv = jnp.where(is_lat, dvl_ref[...], dvc_ref[...]).astype(F32)
        x = ckv_ref[...]
        ra = _rms(x)
        xh = x * ra
        kva = kva_ref[...]
        ckvn = (xh * kva).astype(BF16)
        wukv_v = wukv_ref[...]
        wk_v, cos_v, sin_v = wk_ref[...], cos_ref[...], sin_ref[...]
        kx, _ = _split_kv(_mm_nt(ckvn, wukv_v), kpe_ref[...])
        _, y, rb = _head_norm_rope(kx, wk_v, cos_v, sin_v, seg[...], segt[...], rot[...], rope=False)
        dkx, dwk = _head_norm_rope_bwd(dk, y, rb, wk_v, cos_v, sin_v, seg[...], segt[...], rot_t[...])
        dkpe = dkx[:, 0:LANE]
        for h in range(1, H):
            dkpe = dkpe + dkx[:, h * LANE:(h + 1) * LANE]
        lane = lax.broadcasted_iota(jnp.int32, (tm, LANE), 1)
        dkpe_ref[...] = jnp.where((lane >= DN) & (lane < DH), dkpe, 0.0).astype(BF16)
        dkv = jnp.concatenate([jnp.where(lane < DN, dkx[:, h * LANE:(h + 1) * LANE],
                                         pltpu.roll(dv[:, h * LANE:(h + 1) * LANE], DN, 1)) for h in range(H)],
                              axis=-1).astype(BF16)
        dckvn = _mm(dkv, wukv_v)
        dxh = dckvn * kva
        dckv_ref[...] = (ra * (dxh - xh * jnp.mean(dxh * xh, axis=-1, keepdims=True))).astype(BF16)
        dwukv_ref[...] += _mm_tn(dkv, ckvn)
        dkva_ref[...] += _rowsum(dckvn * xh)
        dwk_ref[...] += dwk

    row = lambda cols: pl.BlockSpec((tm, cols), lambda t: (t, 0))
    lat = pl.BlockSpec((tm, HP), lambda t: (jnp.minimum(t, n_lat - 1), 0))
    ctx = pl.BlockSpec((tm, HP), lambda t: (jnp.maximum(t - n_lat, 0), 0))
    tab = _kv_tab_spec(tm, tpe, n_lat)
    r = n_tiles * tm
    return _pcall(
        body, name="kv_prep_bwd", grid=(n_tiles,),
        out_shape=(_sds((r, KVL), BF16), _sds((r, LANE), BF16), _sds((HP, KVL), F32), _sds((1, KVL), F32),
                   _sds((1, HP), F32)),
        in_specs=[lat, ctx, lat, ctx, row(KVL), row(LANE), _const((1, KVL)), _const((HP, KVL)), _const((1, HP)),
                  tab, tab, _const((HP, LANE)), _const((2 * LANE, HP)), _const((2 * LANE, 2 * LANE)),
                  _const((2 * LANE, 2 * LANE))],
        out_specs=(row(KVL), row(LANE), _const((HP, KVL)), _const((1, KVL)), _const((1, HP))), vmem_mb=48)(
            dks[0], dks[1], dvs[0], dvs[1], ckv, kpe, kva_w, wukv, wk, cosk, sink,
            cs["seg_h"], cs["seg_ht"], cs["rot"], cs["rot_t"])


_SCALE = DH ** -0.5
_SCALE_LOG2E = _SCALE * 1.4426950408889634


def _key_chunks(s, nc, ck):
    return ([(0, lo, min(lo + ck, s)) for lo in range(0, s, ck)]
            + [(1, lo, min(lo + ck, nc)) for lo in range(0, nc, ck)])


def _attn_fwd(q, k, v, *, nb, s, nc, tq, ck):
    tpe = s // tq
    r_lat = nb * s
    chunks = _key_chunks(s, nc, ck)
    hp = 4

    def body(q_ref, kl_ref, kc_ref, vl_ref, vc_ref, o_ref, lse_ref):
        k_refs, v_refs = (kl_ref, kc_ref), (vl_ref, vc_ref)
        for hh in range(hp):
            hs = slice(hh * LANE, (hh + 1) * LANE)
            qv = q_ref[:, hs]
            xs = [_mm_nt(qv, k_refs[w][lo:hi, hs]) for w, lo, hi in chunks]
            m = jnp.max(xs[0], axis=-1, keepdims=True)
            for x in xs[1:]:
                m = jnp.maximum(m, jnp.max(x, axis=-1, keepdims=True))
            l = acc = None
            for x, (w, lo, hi) in zip(xs, chunks):
                e = jnp.exp2((x - m) * _SCALE_LOG2E)
                lc = jnp.sum(e, axis=-1, keepdims=True)
                pv = _mm(e.astype(BF16), v_refs[w][lo:hi, hs])
                l = lc if l is None else l + lc
                acc = pv if acc is None else acc + pv
            o_ref[:, hs] = (acc / l).astype(BF16)
            lse = m * _SCALE_LOG2E + jnp.log2(l)
            lse_ref[hh] = jnp.transpose(jnp.broadcast_to(lse, (tq, LANE)))[0:8, :]

    qs = pl.BlockSpec((tq, hp * LANE), lambda i, j, t: (i * tpe + t, j))
    kl = pl.BlockSpec((s, hp * LANE), lambda i, j, t: (i, j))
    kc = pl.BlockSpec((nc, hp * LANE), lambda i, j, t: (r_lat // nc + i, j))
    ls = pl.BlockSpec((hp, 8, tq), lambda i, j, t: (i * (H // hp) + j, 0, t))
    return _pcall(body, name="attn_fwd", grid=(nb, H // hp, tpe),
                  out_shape=(_sds((r_lat, HP), BF16), _sds((nb * H, 8, s), F32)),
                  in_specs=[qs, kl, kc, kl, kc], out_specs=(qs, ls), vmem_mb=48)(q, k, k, v, v)


def _attn_bwd(q, k, v, o, do, lse, part, *, nb, s, nc, tq, ck):
    tpe = s // tq
    r_lat = nb * s
    chunks = _key_chunks(s, nc, ck)
    hp = 2
    n_steps = nb * (H // hp) * tpe

    def body(q_ref, kl_ref, kc_ref, vl_ref, vc_ref, o_ref, do_ref, lse_ref, part_ref,
             dq_ref, dkl_ref, dkc_ref, dvl_ref, dvc_ref, recv_ref, akl, akc, avl, avc, send_sems, recv_sems):
        t = pl.program_id(2)
        step = (pl.program_id(0) * (H // hp) + pl.program_id(1)) * tpe + t
        _exchange_behind(part_ref, recv_ref, send_sems, recv_sems, step == 0, step == n_steps - 1)

        @pl.when(t == 0)
        def _():
            akl[...] = jnp.zeros_like(akl)
            akc[...] = jnp.zeros_like(akc)
            avl[...] = jnp.zeros_like(avl)
            avc[...] = jnp.zeros_like(avc)

        k_refs, v_refs, ak, av = (kl_ref, kc_ref), (vl_ref, vc_ref), (akl, akc), (avl, avc)
        for hh in range(hp):
            hs = slice(hh * LANE, (hh + 1) * LANE)
            qv = q_ref[:, hs]
            lse = jnp.transpose(jnp.concatenate([lse_ref[hh]] * (LANE // 8), axis=0))[:, 0:1]
            dov = do_ref[:, hs]
            delta = jnp.sum(dov.astype(F32) * o_ref[:, hs].astype(F32), axis=-1, keepdims=True)
            dq = None
            for w, lo, hi in chunks:
                kc_v = k_refs[w][lo:hi, hs]
                p = jnp.exp2(_mm_nt(qv, kc_v) * _SCALE_LOG2E - lse)
                ds = (p * (_mm_nt(dov, v_refs[w][lo:hi, hs]) - delta)).astype(BF16)
                part = _mm(ds, kc_v)
                dq = part if dq is None else dq + part
                ak[w][hs, lo:hi] += _mm_tn(qv, ds)
                av[w][hs, lo:hi] += _mm_tn(dov, p.astype(BF16))
            dq_ref[:, hs] = (dq * _SCALE).astype(BF16)

        @pl.when(t == tpe - 1)
        def _():
            dkl_ref[...] = (akl[...].T * _SCALE).astype(BF16)
            dkc_ref[...] = (akc[...].T * _SCALE).astype(BF16)
            dvl_ref[...] = avl[...].T.astype(BF16)
            dvc_ref[...] = avc[...].T.astype(BF16)

    qs = pl.BlockSpec((tq, hp * LANE), lambda i, j, t: (i * tpe + t, j))
    kl = pl.BlockSpec((s, hp * LANE), lambda i, j, t: (i, j))
    kc = pl.BlockSpec((nc, hp * LANE), lambda i, j, t: (r_lat // nc + i, j))
    kc_out = pl.BlockSpec((nc, hp * LANE), lambda i, j, t: (i, j))
    ls = pl.BlockSpec((hp, 8, tq), lambda i, j, t: (i * (H // hp) + j, 0, t))
    return _pcall(
        body, name="attn_bwd", grid=(nb, H // hp, tpe),
        out_shape=(_sds((r_lat, HP), BF16), _sds((r_lat, HP), BF16), _sds((nb * nc, HP), BF16),
                   _sds((r_lat, HP), BF16), _sds((nb * nc, HP), BF16), _sds((NDEV - 1,) + part.shape[1:], part.dtype)),
        in_specs=[qs, kl, kc, kl, kc, qs, qs, ls, ANY], out_specs=(qs, kl, kc_out, kl, kc_out, ANY),
        scratch=[pltpu.VMEM((hp * LANE, s), F32), pltpu.VMEM((hp * LANE, nc), F32)] * 2
        + [pltpu.SemaphoreType.DMA((NDEV - 1,))] * 2,
        vmem_mb=60)(q, k, k, v, v, o, do, lse, part)


def _chunks_side_by_side(x, j, nch):
    return jnp.concatenate([x[c * CH:(c + 1) * CH, j * LANE:(j + 1) * LANE] for c in range(nch)], axis=-1)


def _first_group_lanes(nch):
    return (lax.broadcasted_iota(jnp.int32, (CH, nch * LANE), 1) & (LANE - 1)) < GD


def _gating(vn, ws_ref, bias_ref, s_scr, tm):
    nch = tm // CH
    first = _first_group_lanes(nch)
    for j in range(G // 2):
        ls = slice(j * LANE, (j + 1) * LANE)
        vst = _chunks_side_by_side(vn, j, nch)
        st = jnp.where(first, _mm(ws_ref[2 * j], vst), _mm(ws_ref[2 * j + 1], vst))
        for c in range(nch):
            s_scr[c * CH:(c + 1) * CH, ls] = st[:, c * LANE:(c + 1) * LANE] + bias_ref[:, ls]


def _compact_heads(x):
    low = lax.broadcasted_iota(jnp.int32, (x.shape[0], LANE), 1) < DV
    out = []
    for j in range(H // 2):
        even = x[:, 2 * j * LANE:(2 * j + 1) * LANE].astype(F32)
        odd = x[:, (2 * j + 1) * LANE:(2 * j + 2) * LANE].astype(F32)
        out.append(jnp.where(low, even, pltpu.roll(odd, DV, 1)))
    return jnp.concatenate(out, axis=-1)


def _expand_heads(x):
    low = lax.broadcasted_iota(jnp.int32, (x.shape[0], LANE), 1) < DV
    out = []
    for j in range(H // 2):
        blk = x[:, j * LANE:(j + 1) * LANE]
        out.append(jnp.where(low, blk, 0.0))
        out.append(jnp.where(low, pltpu.roll(blk, DV, 1), 0.0))
    return jnp.concatenate(out, axis=-1)


def _mix_fwd(u, v, attn, x1, gate, wv, ws, bias, wout, cs, *, tm, n_lat, tpe):
    nrows = gate.shape[0]

    def body(u_ref, v_ref, attn_ref, x_ref, gate_ref, wv_ref, ws_ref, bias_ref, wout_ref, seg, segt,
             x2_ref, mix_ref, s_scr):
        vg = _gelu(v_ref[...])
        rg = lax.rsqrt(_seg_sum(vg * vg, seg[...]) * (1.0 / GD) + EPS)
        vn = (vg * _seg_bcast(rg, segt[...]) * wv_ref[...]).astype(BF16)
        _gating(vn, ws_ref, bias_ref, s_scr, tm)
        sg = (_gelu(u_ref[...]) * s_scr[...]).astype(BF16)
        attn_c = _compact_heads(attn_ref[...]).astype(BF16)
        mix = _mm(attn_c, wout_ref[0:H * DV, :]) + _mm(sg, wout_ref[H * DV:, :])
        mix_ref[...] = mix.astype(BF16)
        x2_ref[...] = x_ref[...] + gate_ref[0] * mix

    row = lambda cols: pl.BlockSpec((tm, cols), lambda t: (t, 0))
    r = n_lat * tm
    return _pcall(
        body, name="mix_fwd", grid=(n_lat,),
        out_shape=(_sds((r, D), F32), _sds((r, D), BF16)),
        in_specs=[row(G * GD), row(G * GD), row(HP), row(D), _mod_spec(1, tpe, nrows), _const((1, G * GD)),
                  _const((G, CH, CH)), _const((CH, G * GD)), _const((D, D)), _const((G * GD, LANE)),
                  _const((2 * LANE, G * GD))],
        out_specs=(row(D), row(D)), scratch=[pltpu.VMEM((tm, G * GD), F32)], vmem_mb=40)(
            u, v, attn, x1, gate, wv, ws, bias, wout, cs["seg_g"], cs["seg_gt"])


def _mix_bwd(dx2, mix, u, v, attn, gate, wv, ws, wst, bias, wout, cs, *, tm, n_lat, tpe):
    nrows = gate.shape[0]
    wrows = H * DV + G * GD

    def body(dx2_ref, mix_ref, u_ref, v_ref, attn_ref, gate_ref, wv_ref, ws_ref, wst_ref, bias_ref, wout_ref, seg, segt,
             dattn_ref, du_ref, dv_ref, dgate_ref, dwout_ref, dws_ref, dbs_ref, dwv_ref, s_scr, dvn_scr, dbias_scr):
        t = pl.program_id(0)

        @pl.when(t == 0)
        def _():
            dwout_ref[...] = jnp.zeros_like(dwout_ref)
            dws_ref[...] = jnp.zeros_like(dws_ref)
            dwv_ref[...] = jnp.zeros_like(dwv_ref)
            dbias_scr[...] = jnp.zeros_like(dbias_scr)

        @pl.when(t % tpe == 0)
        def _():
            dgate_ref[...] = jnp.zeros_like(dgate_ref)

        dx2 = dx2_ref[...]
        dmix = (dx2 * gate_ref[0]).astype(BF16)
        dcat = _mm_nt(dmix, wout_ref[...])
        dattn_ref[...] = _expand_heads(dcat[:, :H * DV]).astype(BF16)
        dsg = dcat[:, H * DV:]

        vraw = v_ref[...]
        vg = _gelu(vraw)
        rg = lax.rsqrt(_seg_sum(vg * vg, seg[...]) * (1.0 / GD) + EPS)
        r64 = _seg_bcast(rg, segt[...])
        y = vg * r64
        wv_v = wv_ref[...]
        vn = (y * wv_v).astype(BF16)
        _gating(vn, ws_ref, bias_ref, s_scr, tm)
        uraw = u_ref[...]
        ug = _gelu(uraw)
        s = s_scr[...]
        sg = (ug * s).astype(BF16)
        du_ref[...] = (dsg * s * _gelu_grad(uraw)).astype(BF16)
        ds = dsg * ug
        dgate_ref[0] += _rowsum(dx2 * mix_ref[...].astype(F32))
        attn_c = _compact_heads(attn_ref[...]).astype(BF16)
        dwout_ref[...] += _mm_tn(jnp.concatenate([attn_c, sg], axis=-1), dmix)

        nch = tm // CH
        first = _first_group_lanes(nch)
        for c in range(nch):
            dbias_scr[...] += ds[c * CH:(c + 1) * CH, :]
        for j in range(G // 2):
            ls = slice(j * LANE, (j + 1) * LANE)
            dst32 = _chunks_side_by_side(ds, j, nch)
            dst = dst32.astype(BF16)
            vst = _chunks_side_by_side(vn, j, nch)
            dvn_st = jnp.where(first, _mm(wst_ref[2 * j], dst), _mm(wst_ref[2 * j + 1], dst))
            for c in range(nch):
                dvn_scr[c * CH:(c + 1) * CH, ls] = dvn_st[:, c * LANE:(c + 1) * LANE]
            dws_ref[2 * j] += _mm_nt(jnp.where(first, dst32, 0.0).astype(BF16), vst)
            dws_ref[2 * j + 1] += _mm_nt(jnp.where(first, 0.0, dst32).astype(BF16), vst)

        dvn = dvn_scr[...]
        dwv_ref[...] += _rowsum(dvn * y)
        dy = dvn * wv_v
        mean_g = _seg_sum(dy * y, seg[...]) * (1.0 / GD)
        dvg = r64 * (dy - y * _seg_bcast(mean_g, segt[...]))
        dv_ref[...] = (dvg * _gelu_grad(vraw)).astype(BF16)

        @pl.when(t == n_lat - 1)
        def _():
            dbs_ref[...] = _dot_hl(dbias_scr[...], seg[...])

    row = lambda cols: pl.BlockSpec((tm, cols), lambda t: (t, 0))
    r = n_lat * tm
    return _pcall(
        body, name="mix_bwd", grid=(n_lat,),
        out_shape=(_sds((r, HP), BF16), _sds((r, G * GD), BF16), _sds((r, G * GD), BF16), _sds((nrows, 1, D), F32),
                   _sds((wrows, D), F32), _sds((G, CH, CH), F32), _sds((CH, LANE), F32), _sds((1, G * GD), F32)),
        in_specs=[row(D), row(D), row(G * GD), row(G * GD), row(HP), _mod_spec(1, tpe, nrows), _const((1, G * GD)),
                  _const((G, CH, CH)), _const((G, CH, CH)), _const((CH, G * GD)), _const((wrows, D)),
                  _const((G * GD, LANE)), _const((2 * LANE, G * GD))],
        out_specs=(row(HP), row(G * GD), row(G * GD), _mod_spec(1, tpe, nrows), _const((wrows, D)),
                   _const((G, CH, CH)), _const((CH, LANE)), _const((1, G * GD))),
        scratch=[pltpu.VMEM((tm, G * GD), F32), pltpu.VMEM((tm, G * GD), F32), pltpu.VMEM((CH, G * GD), F32)],
        vmem_mb=56)(dx2, mix, u, v, attn, gate, wv, ws, wst, bias, wout, cs["seg_g"], cs["seg_gt"])


def _adamw_math(w, g, m, v):
    m2 = ADAM_B1 * m + (1.0 - ADAM_B1) * g
    v2 = ADAM_B2 * v + (1.0 - ADAM_B2) * (g * g)
    m_hat = m2 / (1.0 - ADAM_B1 ** ADAM_STEP)
    v_hat = v2 / (1.0 - ADAM_B2 ** ADAM_STEP)
    delta = -ADAM_LR * (m_hat / (jnp.sqrt(v_hat) + ADAM_EPS) + ADAM_WD * w)
    return delta, m2, v2


def _row_tile(r, c):
    best = r
    for tr in range(8, r, 8):
        if r % tr == 0 and tr * c * 4 <= MIB:
            best = tr
    return best


def _adamw(w, g, m, v, name):
    r, c = w.shape
    tr = _row_tile(r, c)

    def body(w_ref, g_ref, m_ref, v_ref, d_ref, mo_ref, vo_ref):
        d_ref[...], mo_ref[...], vo_ref[...] = _adamw_math(w_ref[...], g_ref[...], m_ref[...], v_ref[...])

    blk = pl.BlockSpec((tr, c), lambda t: (t, 0))
    return _pcall(body, name=name, grid=(r // tr,), out_shape=(_sds((r, c), F32),) * 3,
                  in_specs=[blk] * 4, out_specs=(blk,) * 3)(w, g, m, v)


def _adamw_small(params):
    n = len(params)

    def body(*refs):
        ins, outs = refs[:4 * n], refs[4 * n:]
        for i in range(n):
            w, g, m, v = (ins[4 * i + k][...] for k in range(4))
            if i == 0:
                sig = _sigmoid(w)
                g = g * (sig * (1.0 + w * (1.0 - sig)))
            d, m2, v2 = _adamw_math(w, g, m, v)
            outs[4 * i][...] = g
            outs[4 * i + 1][...] = d
            outs[4 * i + 2][...] = m2
            outs[4 * i + 3][...] = v2

    flat = [a for p in params for a in p]
    out_shape = tuple(_sds(p[0].shape, F32) for p in params for _ in range(4))
    res = _pcall(body, name="adamw_small", out_shape=out_shape, in_specs=[VMEM] * (4 * n),
                 out_specs=(VMEM,) * (4 * n))(*flat)
    return [res[4 * i:4 * i + 4] for i in range(n)]


def _rope_tables(s):
    rows = jnp.repeat(jnp.arange(s // GRID_W, dtype=F32), GRID_W)
    cols = jnp.tile(jnp.arange(GRID_W, dtype=F32), s // GRID_W)
    half = DR // 2
    inv = ROPE_BASE ** (-jnp.arange(0, half, 2, dtype=F32) / half)
    ang_r = rows[:, None] * inv
    ang_c = cols[:, None] * inv
    ang = jnp.concatenate([ang_r, ang_r, ang_c, ang_c], axis=-1)
    return jnp.cos(ang), jnp.sin(ang)


def _head_pad(a, real):
    return jnp.pad(a, ((0, 0), (0, LANE - real), (0, 0))).reshape(HP, a.shape[2])


def kernel(x, c, ctx, c_ctx, w_ada, b_ada, norm1_w, ffn1_w1, ffn1_w3, ffn1_w2, norm2_w, w_in, q_a_norm_w, w_uq, kv_a_norm_w, w_ukv, q_norm_w, k_norm_w, v_norm_w, w_s, b_s, w_out, norm3_w, ffn2_w1, ffn2_w3, ffn2_w2, loss_target, m_c_ctx, m_w_ada, m_b_ada, m_norm1_w, m_ffn1_w1, m_ffn1_w3, m_ffn1_w2, m_norm2_w, m_w_in, m_q_a_norm_w, m_w_uq, m_kv_a_norm_w, m_w_ukv, m_q_norm_w, m_k_norm_w, m_v_norm_w, m_w_s, m_b_s, m_w_out, m_norm3_w, m_ffn2_w1, m_ffn2_w3, m_ffn2_w2, v_c_ctx, v_w_ada, v_b_ada, v_norm1_w, v_ffn1_w1, v_ffn1_w3, v_ffn1_w2, v_norm2_w, v_w_in, v_q_a_norm_w, v_w_uq, v_kv_a_norm_w, v_w_ukv, v_q_norm_w, v_k_norm_w, v_v_norm_w, v_w_s, v_b_s, v_w_out, v_norm3_w, v_ffn2_w1, v_ffn2_w3, v_ffn2_w2):
    nb, s, _ = x.shape
    nc = ctx.shape[1]
    tm = 256 if nc % 256 == 0 else 128
    tpe = s // tm
    n_lat = nb * tpe
    n_all = n_lat + nb * nc // tm
    tmf = 2 * tm if s % (2 * tm) == 0 and (nb * nc) % (2 * tm) == 0 else tm
    tp = tmf
    r_lat = nb * s
    tpe_p, n_lat_p, n_all_p = s // tp, r_lat // tp, (r_lat + nb * nc) // tp
    me = 4 * lax.axis_index("x") + 2 * lax.axis_index("y") + lax.axis_index("c")
    cs = _consts()
    ncol = w_ada.shape[2]
    fsh = ffn1_w1.shape[2]
    assert nb + 1 <= 8 and NDEV * fsh == FF and NDEV * ncol == NMOD * D and s % nc == 0 and nc % tm == 0

    def t16(a):
        return a.T.astype(BF16)

    wpack1 = jnp.concatenate([t16(ffn1_w1[0]), t16(ffn1_w3[0]), ffn1_w2[0].astype(BF16)], axis=0)
    a_loc = jnp.concatenate([c, c_ctx[None, :], jnp.zeros((7 - nb, D), F32)], axis=0)
    a_raw, _, mod_all, wall1 = _ada_front(a_loc, w_ada[0], lax.dynamic_slice_in_dim(b_ada, me * ncol, ncol, axis=1),
                                          wpack1)
    a_raw = a_raw.reshape(NDEV * 8, D)
    mod_mine = lax.dynamic_slice_in_dim(mod_all, 8 * me, 8, axis=1)
    modtab = mod_mine.transpose(1, 0, 2).reshape(8, NMOD, D)[:nb + 1]
    wpack2 = jnp.concatenate([
        t16(ffn2_w1[0]), t16(ffn2_w3[0]), ffn2_w2[0].astype(BF16),
        t16(w_in[0]), jnp.zeros((12, D), BF16),
        w_out[0].astype(BF16),
        t16(w_uq[0]).reshape(24, D), jnp.zeros((8, D), BF16),
        t16(w_ukv[0]).reshape(16, D)], axis=0)

    def head_w(wn):
        return jnp.tile(jnp.pad(wn, ((0, 0), (0, LANE - DH))), (1, H))

    wq, wk = head_w(q_norm_w), head_w(k_norm_w)
    wv = v_norm_w.reshape(1, G * GD)
    ws16 = w_s[0].astype(BF16)
    wst16 = w_s[0].transpose(0, 2, 1).astype(BF16)
    bias = jnp.repeat(b_s[0].T, GD, axis=1)
    cos, sin = _rope_tables(s)
    cos = jnp.pad(cos, ((0, 0), (DN, LANE - DH)), constant_values=1.0)
    sin = jnp.pad(sin, ((0, 0), (DN, LANE - DH)))
    cos_k = jnp.concatenate([cos, jnp.ones((tm, LANE), F32)], axis=0)
    sin_k = jnp.concatenate([sin, jnp.zeros((tm, LANE), F32)], axis=0)

    xs = (x.reshape(r_lat, D), ctx.reshape(nb * nc, D))
    x1, a1, b1, o1, wall2 = _ffn_fwd(xs, modtab[:, 0:3], norm1_w, wall1, 0, tm=tmf, n_tiles=(r_lat + nb * nc) // tmf,
                                     tpe=s // tmf, n_lat=r_lat // tmf, name="ffn1_fwd", gather=wpack2)

    o0 = 3 * fsh
    wint = wall2[:, o0:o0 + 180].reshape(IN_COLS, D)
    z = lambda n: jnp.zeros((n, D), BF16)
    wint = jnp.concatenate([wint[0:128], wint[160:416], wint[416:928], wint[928:1440],
                            z(DN), wint[128:160], z(LANE - DH)], axis=0)
    wout = wall2[:, o0 + 192:o0 + 320].reshape(D, D)
    wuq = _head_pad(wall2[:, o0 + 320:o0 + 344].reshape(H, DH, QL), DH)
    wukv = wall2[:, o0 + 352:o0 + 368].reshape(HP, KVL)

    ckv, qp, u_raw, v_raw, kpe = _proj_fwd(x1, modtab[:, 3:5], norm2_w, wint, tm=tp, n_tiles=n_all_p, tpe=tpe_p)
    q = _q_prep_fwd(qp, q_a_norm_w, wuq, wq, cos, sin, cs, tm=tm, n_lat=n_lat, tpe=tpe)
    k, v = _kv_prep_fwd(ckv, kpe, kv_a_norm_w, wukv, wk, cos_k, sin_k, cs,
                        tm=tm, n_tiles=n_all, tpe=tpe, n_lat=n_lat)
    attn, lse = _attn_fwd(q, k, v, nb=nb, s=s, nc=nc, tq=tm, ck=2048)
    x2, mix = _mix_fwd(u_raw, v_raw, attn, x1, modtab[:nb, 5:6], wv, ws16, bias, wout, cs,
                       tm=tp, n_lat=n_lat_p, tpe=tpe_p)
    dy, a2, b2, o2, lsum = _ffn_fwd((x2,), modtab[:nb, 6:9], norm3_w, wall2, 0, tm=tmf, n_tiles=r_lat // tmf,
                                    tpe=s // tmf, n_lat=r_lat // tmf, name="ffn2_fwd",
                                    target=loss_target.reshape(r_lat, D))

    tr = 2 * tm if n_lat % 2 == 0 and n_all % 2 == 0 else tm
    dx2, da2, db2, g2, do2, h2, dmod678, dnorm3 = _ffn_bwd_dx(
        dy, (x2,), a2, b2, o2, modtab[:nb, 6:9], norm3_w, wall2, 0,
        tm=tm, n_tiles=n_lat, tpe=tpe, n_lat=n_lat, name="ffn2_bwd_dx")
    g_ffn2 = _ffn_bwd_dw(h2, do2, da2, db2, g2, tr=tr, name="ffn2_bwd_dw")

    dattn, du, dv, dgate5, dwout, dws, dbs, dwv = _mix_bwd(
        dx2, mix, u_raw, v_raw, attn, modtab[:nb, 5:6], wv, ws16, wst16, bias, wout, cs, tm=tp, n_lat=n_lat_p, tpe=tpe_p)
    tq = 2 * tm if s % (2 * tm) == 0 else tm
    dq, dk_l, dk_c, dv_l, dv_c, recv_ffn2 = _attn_bwd(q, k, v, attn, dattn, lse, g_ffn2,
                                                      nb=nb, s=s, nc=nc, tq=tq, ck=1024)
    dqp, dwuq, dqa, dwq = _q_prep_bwd(dq, qp, q_a_norm_w, wuq, wq, cos, sin, cs, tm=tm, n_lat=n_lat, tpe=tpe)
    dckv, dkpe, dwukv, dkva, dwk = _kv_prep_bwd((dk_l, dk_c), (dv_l, dv_c), ckv, kpe, kv_a_norm_w, wukv, wk,
                                                cos_k, sin_k, cs, tm=tm, n_tiles=n_all, tpe=tpe, n_lat=n_lat)
    dx1, dwin, dmod34, dnorm2 = _proj_bwd(dckv, dkpe, dqp, du, dv, dx2, x1, modtab[:, 3:5], norm2_w, wint,
                                          tm=tp, n_tiles=n_all_p, tpe=tpe_p, n_lat=n_lat_p)

    def blocks(a):
        return a.reshape(NDEV, a.shape[0] // NDEV, D)

    dwin_o = jnp.concatenate([dwin[0:128], dwin[KPE_LO:KPE_LO + DR], dwin[128:384], dwin[384:896], dwin[896:1408]],
                             axis=0)
    dwuq_o = dwuq.reshape(H, LANE, QL)[:, :DH]
    gmisc = jnp.concatenate([
        blocks(dwin_o).astype(BF16), jnp.zeros((NDEV, 12, D), BF16),
        blocks(dwout).astype(BF16),
        dwuq_o.reshape(NDEV, 24, D).astype(BF16), jnp.zeros((NDEV, 8, D), BF16),
        dwukv.reshape(NDEV, 16, D).astype(BF16)], axis=1)

    dx0, da1, db1, g1, do1, h1, dmod012, dnorm1 = _ffn_bwd_dx(
        dx1, xs, a1, b1, o1, modtab[:, 0:3], norm1_w, wall1, 0,
        tm=tm, n_tiles=n_all, tpe=tpe, n_lat=n_lat, name="ffn1_bwd_dx")
    grad_x = dx0.reshape(nb, s, D)
    g_w1, recv_misc = _ffn_bwd_dw_one(da1, h1, tr=tr, name="ffn1_bwd_dw1", part=gmisc)
    g_w3, recv_w1 = _ffn_bwd_dw_one(db1, h1, tr=tr, name="ffn1_bwd_dw3", part=g_w1)
    g_w2, recv_w3 = _ffn_bwd_dw_one(g1, do1, tr=tr, name="ffn1_bwd_dw2", part=g_w3)

    zrow = jnp.zeros((1, D), F32)
    g_lat = jnp.concatenate([dmod012[:nb, 0], dmod012[:nb, 1], dmod012[:nb, 2], dmod34[:nb, 0], dmod34[:nb, 1],
                             dgate5[:, 0], dmod678[:, 0], dmod678[:, 1], dmod678[:, 2]], axis=1)
    g_ctx = jnp.concatenate([dmod012[nb:, 0], dmod012[nb:, 1], dmod012[nb:, 2], dmod34[nb:, 0], dmod34[nb:, 1],
                             zrow, zrow, zrow, zrow], axis=1)
    g_loc = jnp.concatenate([g_lat, g_ctx, jnp.zeros((7 - nb, NMOD * D), F32)], axis=0)

    got_w2, g_all = _scatter_sibling([g_w2], "scatter_sibling_w2", gather=g_loc)
    g_all = g_all.reshape(NDEV * 8, NMOD * D)
    g_cols = lax.dynamic_slice_in_dim(g_all, me * ncol, ncol, axis=1)
    g_w_ada, pc_ctx, g_b_ada = _ada_bwd(a_raw, c_ctx.reshape(D, 1), g_all, g_cols, w_ada[0], nb)
    part_w2 = _add_sibling(g_w2, got_w2, 176, "add_sibling_w2")

    def prow(a):
        a = a.reshape(1, -1)
        return jnp.concatenate([a, jnp.zeros((1, D - a.shape[1]), F32)], axis=1)

    g_qn = dwq.reshape(H, LANE)[:, :DH].sum(0)
    g_kn = dwk.reshape(H, LANE)[:, :DH].sum(0)
    spack = jnp.concatenate([
        dnorm1, dnorm2, dnorm3, prow(dqa), prow(dkva), prow(g_qn), prow(g_kn), prow(dwv),
        prow(dbs[:, :G].T), prow(pc_ctx), prow(lsum[0:1]), jnp.zeros((5, D), F32), dws.reshape(CH, D)],
        axis=0)
    recv_w2, small_all = _scatter_chips([part_w2], "scatter_chips", gather=spack)
    ssum = _sum_slots(small_all, 144, "sum_small")
    loss = ssum[10, 0] * (0.5 / D)
    gsum2 = _sum_direct(g_ffn2, recv_ffn2, 176, "sum_grads_ffn2")
    msum = _sum_direct(gmisc, recv_misc, 368, "sum_grads_misc")

    transposed = ("ffn1_w1", "ffn1_w3", "ffn2_w1", "ffn2_w3", "w_in", "w_uq")
    g_big = {
        "ffn1_w1": _sum_direct(g_w1, recv_w1, 176, "sum_grads_w1"),
        "ffn1_w3": _sum_direct(g_w3, recv_w3, 176, "sum_grads_w3"),
        "ffn1_w2": _sum_chips(part_w2, recv_w2, 176, "sum_grads_w2"),
        "ffn2_w1": gsum2[0:fsh], "ffn2_w3": gsum2[fsh:2 * fsh], "ffn2_w2": gsum2[2 * fsh:3 * fsh],
        "w_in": msum[0:180], "w_out": msum[192:320],
        "w_uq": msum[320:344].reshape(DH, QL), "w_ukv": msum[352:368].reshape(DN + DV, KVL).T,
        "w_ada": g_w_ada,
    }

    big_in = {
        "w_ada": (w_ada, m_w_ada, v_w_ada), "ffn1_w1": (ffn1_w1, m_ffn1_w1, v_ffn1_w1),
        "ffn1_w3": (ffn1_w3, m_ffn1_w3, v_ffn1_w3), "ffn1_w2": (ffn1_w2, m_ffn1_w2, v_ffn1_w2),
        "w_in": (w_in, m_w_in, v_w_in), "w_uq": (w_uq, m_w_uq, v_w_uq), "w_ukv": (w_ukv, m_w_ukv, v_w_ukv),
        "w_out": (w_out, m_w_out, v_w_out), "ffn2_w1": (ffn2_w1, m_ffn2_w1, v_ffn2_w1),
        "ffn2_w3": (ffn2_w3, m_ffn2_w3, v_ffn2_w3), "ffn2_w2": (ffn2_w2, m_ffn2_w2, v_ffn2_w2),
    }
    res = {}
    for nm, (w, m, v_) in big_in.items():
        g = g_big[nm]
        if nm in transposed:
            d_, m_, v2_ = _adamw(w[0].T, g, m[0].T, v_[0].T, "adamw_" + nm)
            res[nm] = tuple(a.T[None] for a in (g, d_, m_, v2_))
        else:
            d_, m_, v2_ = _adamw(w[0], g, m[0], v_[0], "adamw_" + nm)
            res[nm] = tuple(a[None] for a in (g, d_, m_, v2_))

    small_in = [
        ("c_ctx", c_ctx, m_c_ctx, v_c_ctx, ssum[9:10], (1, D)),
        ("b_ada", b_ada, m_b_ada, v_b_ada, g_b_ada, (1, NMOD * D)),
        ("norm1_w", norm1_w, m_norm1_w, v_norm1_w, ssum[0:1], (1, D)),
        ("norm2_w", norm2_w, m_norm2_w, v_norm2_w, ssum[1:2], (1, D)),
        ("norm3_w", norm3_w, m_norm3_w, v_norm3_w, ssum[2:3], (1, D)),
        ("q_a_norm_w", q_a_norm_w, m_q_a_norm_w, v_q_a_norm_w, ssum[3:4, :QL], (1, QL)),
        ("kv_a_norm_w", kv_a_norm_w, m_kv_a_norm_w, v_kv_a_norm_w, ssum[4:5, :KVL], (1, KVL)),
        ("q_norm_w", q_norm_w, m_q_norm_w, v_q_norm_w, ssum[5:6, :DH], (1, DH)),
        ("k_norm_w", k_norm_w, m_k_norm_w, v_k_norm_w, ssum[6:7, :DH], (1, DH)),
        ("v_norm_w", v_norm_w, m_v_norm_w, v_v_norm_w, ssum[7:8, :G * GD], (G, GD)),
        ("b_s", b_s, m_b_s, v_b_s, ssum[8:9], (G, CH)),
        ("w_s", w_s, m_w_s, v_w_s, ssum[16:144], (G * CH, CH)),
    ]
    small_out = _adamw_small(
        [(w.reshape(sh), g.reshape(sh), m.reshape(sh), v_.reshape(sh)) for _, w, m, v_, g, sh in small_in])
    for (nm, w, *_), outs in zip(small_in, small_out):
        res[nm] = tuple(a.reshape(w.shape) for a in outs)

    order = ["c_ctx", "w_ada", "b_ada", "norm1_w", "ffn1_w1", "ffn1_w3", "ffn1_w2", "norm2_w", "w_in", "q_a_norm_w",
             "w_uq", "kv_a_norm_w", "w_ukv", "q_norm_w", "k_norm_w", "v_norm_w", "w_s", "b_s", "w_out", "norm3_w",
             "ffn2_w1", "ffn2_w3", "ffn2_w2"]
    return (loss, grad_x, *[res[n][0] for n in order], *[res[n][1] for n in order],
            *[res[n][2] for n in order], *[res[n][3] for n in order])
```
